```python
import math
import jax, jax.numpy as jnp
from jax import lax
import numpy as np

D_MODEL = 1024
BATCH = 8
SEQ = 4096
DEPTH = 2

N_Q_HEADS = 16
N_KV_HEADS = 2
HEAD_DIM = 64
Q_PER_KV = N_Q_HEADS // N_KV_HEADS
WINDOW = 128
ATTN_BLOCK = 128
ROPE_THETA = 500000.0
ROT_DIM = HEAD_DIM // 4
SGU_WIDTH = 1024
SGU_GROUPS = 8
SGU_GROUP_DIM = SGU_WIDTH // SGU_GROUPS
SGU_CHUNK = 128
FFN_DIM = 2816
CONV_WIDTH = 3
NORM_EPS = 1e-6

Q_END = N_Q_HEADS * HEAD_DIM
K_END = Q_END + N_KV_HEADS * HEAD_DIM
V_END = K_END + N_KV_HEADS * HEAD_DIM
Z_END = V_END + 2 * SGU_WIDTH
IN_COLS = Z_END + 2 * D_MODEL

kernel_name = "hybrid_gmlp_swa_sink_convffn_adaln"


def rms_norm(x, w):
    xf = x.astype(jnp.float32)
    y = xf * lax.rsqrt(jnp.mean(xf * xf, axis=-1, keepdims=True) + NORM_EPS)
    return (y * w.astype(jnp.float32)).astype(x.dtype)


def layer_norm(x, w, b):
    xf = x.astype(jnp.float32)
    mu = jnp.mean(xf, axis=-1, keepdims=True)
    var = jnp.mean(jnp.square(xf - mu), axis=-1, keepdims=True)
    y = (xf - mu) * lax.rsqrt(var + NORM_EPS)
    return (y * w.astype(jnp.float32) + b.astype(jnp.float32)).astype(x.dtype)


def rope_tables(positions, dtype):
    inv_freq = ROPE_THETA ** (-jnp.arange(0, ROT_DIM, 2, dtype=jnp.float32) / ROT_DIM)
    ang = positions.astype(jnp.float32)[..., None] * inv_freq
    return jnp.cos(ang)[:, :, None, :].astype(dtype), jnp.sin(ang)[:, :, None, :].astype(dtype)


def apply_partial_rope(x, cos, sin):
    half = ROT_DIM // 2
    x1, x2, xp = x[..., :half], x[..., half:ROT_DIM], x[..., ROT_DIM:]
    return jnp.concatenate([x1 * cos - x2 * sin, x2 * cos + x1 * sin, xp], axis=-1)


def sliding_window_attention(q, k, v, sinks):
    B, S = q.shape[0], q.shape[1]
    nb = S // ATTN_BLOCK
    qb = q.reshape(B, nb, ATTN_BLOCK, N_KV_HEADS, Q_PER_KV, HEAD_DIM)
    kb = k.reshape(B, nb, ATTN_BLOCK, N_KV_HEADS, HEAD_DIM)
    vb = v.reshape(B, nb, ATTN_BLOCK, N_KV_HEADS, HEAD_DIM)
    pad = ((0, 0), (1, 0), (0, 0), (0, 0), (0, 0))
    k_band = jnp.concatenate([jnp.pad(kb[:, :-1], pad), kb], axis=2)
    v_band = jnp.concatenate([jnp.pad(vb[:, :-1], pad), vb], axis=2)
    scores = jnp.einsum('bnqhgd,bnkhd->bnhgqk', qb, k_band).astype(jnp.float32) * (HEAD_DIM ** -0.5)
    i = jnp.arange(ATTN_BLOCK)[:, None]
    j = jnp.arange(2 * ATTN_BLOCK)[None, :]
    band = (j > i + ATTN_BLOCK - WINDOW) & (j <= i + ATTN_BLOCK)
    exists = (jnp.arange(nb)[:, None, None] > 0) | (j >= ATTN_BLOCK)[None]
    mask = (band[None] & exists)[None, :, None, None]
    scores = jnp.where(mask, scores, -jnp.inf)
    sink = sinks.astype(jnp.float32).reshape(N_KV_HEADS, Q_PER_KV)[None, None, :, :, None, None]
    m = jnp.maximum(jnp.max(scores, axis=-1, keepdims=True), sink)
    p = jnp.exp(scores - m)
    denom = jnp.sum(p, axis=-1, keepdims=True) + jnp.exp(sink - m)
    out = jnp.einsum('bnhgqk,bnkhd->bnqhgd', (p / denom).astype(v.dtype), v_band)
    return out.reshape(B, S, N_Q_HEADS * HEAD_DIM)


def spatial_gating(z, ln_w, ln_b, w_s, b_s):
    B, S = z.shape[0], z.shape[1]
    u, v = jnp.split(z, 2, axis=-1)
    v = layer_norm(v, ln_w, ln_b)
    vb = v.reshape(B, S // SGU_CHUNK, SGU_CHUNK, SGU_GROUPS, SGU_GROUP_DIM)
    causal = jnp.tril(jnp.ones((SGU_CHUNK, SGU_CHUNK), dtype=bool))
    w = jnp.where(causal[None], w_s, jnp.zeros_like(w_s))
    f = jnp.einsum('gts,bnsgc->bntgc', w, vb) + b_s.T[None, None, :, :, None]
    return u * f.reshape(B, S, SGU_WIDTH)


def conv_ffn(h, w_gate, w_up, conv_w, conv_b, w_down):
    a = h @ w_gate
    S = a.shape[1]
    a_pad = jnp.pad(a, ((0, 0), (CONV_WIDTH - 1, 0), (0, 0)))
    a = conv_b + sum(conv_w[k] * a_pad[:, k:k + S] for k in range(CONV_WIDTH))
    return (jax.nn.silu(a) * (h @ w_up)) @ w_down


def _fwd_setup_inputs(seed: int = 0) -> dict:
    key = jax.random.key(seed)
    ks = jax.random.split(key, 24)
    L, D = DEPTH, D_MODEL
    nrm = lambda k, shape, s: jax.random.normal(k, shape, jnp.float32) * s
    offsets = jax.random.randint(ks[2], (BATCH, 1), 0, 2048, dtype=jnp.int32)
    return {
        "x": nrm(ks[0], (BATCH, SEQ, D), 1.0),
        "c": nrm(ks[1], (BATCH, D), 1.0),
        "positions": offsets + jnp.arange(SEQ, dtype=jnp.int32)[None, :],
        "ada_w": nrm(ks[3], (L, D, 6 * D), D ** -0.5),
        "ada_b": nrm(ks[4], (L, 6 * D), 0.02),
        "norm1_w": 1.0 + nrm(ks[5], (L, D), 0.05),
        "w_in": nrm(ks[6], (L, D, IN_COLS), D ** -0.5),
        "attn_sinks": nrm(ks[7], (L, N_Q_HEADS), 0.5),
        "sgu_ln_w": 1.0 + nrm(ks[8], (L, SGU_WIDTH), 0.05),
        "sgu_ln_b": nrm(ks[9], (L, SGU_WIDTH), 0.02),
        "sgu_w": nrm(ks[10], (L, SGU_GROUPS, SGU_CHUNK, SGU_CHUNK), SGU_CHUNK ** -0.5),
        "sgu_b": 1.0 + nrm(ks[11], (L, SGU_GROUPS, SGU_CHUNK), 0.05),
        "proj_a": nrm(ks[12], (L, SGU_WIDTH, D), SGU_WIDTH ** -0.5),
        "proj_b": nrm(ks[13], (L, N_Q_HEADS * HEAD_DIM, D), (N_Q_HEADS * HEAD_DIM) ** -0.5),
        "w_out": nrm(ks[14], (L, D, D), D ** -0.5),
        "norm2_w": 1.0 + nrm(ks[15], (L, D), 0.05),
        "ffn_w_gate": nrm(ks[16], (L, D, FFN_DIM), D ** -0.5),
        "ffn_w_up": nrm(ks[17], (L, D, FFN_DIM), D ** -0.5),
        "ffn_conv_w": nrm(ks[18], (L, CONV_WIDTH, FFN_DIM), CONV_WIDTH ** -0.5),
        "ffn_conv_b": nrm(ks[19], (L, FFN_DIM), 0.01),
        "ffn_w_down": nrm(ks[20], (L, FFN_DIM, D), FFN_DIM ** -0.5),
        "final_norm_w": 1.0 + nrm(ks[21], (D,), 0.05),
    }


def _fwd_reference(x, c, positions, ada_w, ada_b, norm1_w, w_in, attn_sinks, sgu_ln_w, sgu_ln_b, sgu_w, sgu_b,
              proj_a, proj_b, w_out, norm2_w, ffn_w_gate, ffn_w_up, ffn_conv_w, ffn_conv_b, ffn_w_down,
              final_norm_w):
    B, S = x.shape[0], x.shape[1]
    cos, sin = rope_tables(positions, x.dtype)
    c_act = jax.nn.silu(c)
    for l in range(DEPTH):
        mod = (c_act @ ada_w[l] + ada_b[l])[:, None, :]
        sh1, sc1, g1, sh2, sc2, g2 = jnp.split(mod, 6, axis=-1)
        h = rms_norm(x, norm1_w[l]) * (1 + sc1) + sh1
        proj = h @ w_in[l]
        q, k, v, z, gates = jnp.split(proj, [Q_END, K_END, V_END, Z_END], axis=-1)
        q = apply_partial_rope(q.reshape(B, S, N_Q_HEADS, HEAD_DIM), cos, sin)
        k = apply_partial_rope(k.reshape(B, S, N_KV_HEADS, HEAD_DIM), cos, sin)
        v = v.reshape(B, S, N_KV_HEADS, HEAD_DIM)
        y_attn = sliding_window_attention(q, k, v, attn_sinks[l])
        y_sgu = spatial_gating(jax.nn.gelu(z, approximate=False), sgu_ln_w[l], sgu_ln_b[l], sgu_w[l], sgu_b[l])
        gate_a, gate_b = jnp.split(jax.nn.sigmoid(gates), 2, axis=-1)
        merged = gate_a * (y_sgu @ proj_a[l]) + gate_b * (y_attn @ proj_b[l])
        x = x + g1 * (merged @ w_out[l])
        h2 = rms_norm(x, norm2_w[l]) * (1 + sc2) + sh2
        x = x + g2 * conv_ffn(h2, ffn_w_gate[l], ffn_w_up[l], ffn_conv_w[l], ffn_conv_b[l], ffn_w_down[l])
    return rms_norm(x, final_norm_w)


import jax as _jax
import jax.numpy as _jnp

TWIN_FORMAT = 'train_step'
FWD_PARAMS = ['x', 'c', 'positions', 'ada_w', 'ada_b', 'norm1_w', 'w_in', 'attn_sinks', 'sgu_ln_w', 'sgu_ln_b', 'sgu_w', 'sgu_b', 'proj_a', 'proj_b', 'w_out', 'norm2_w', 'ffn_w_gate', 'ffn_w_up', 'ffn_conv_w', 'ffn_conv_b', 'ffn_w_down', 'final_norm_w']
TWIN_WEIGHTS = ['ada_w', 'ada_b', 'norm1_w', 'w_in', 'attn_sinks', 'sgu_ln_w', 'sgu_ln_b', 'sgu_w', 'sgu_b', 'proj_a', 'proj_b', 'w_out', 'norm2_w', 'ffn_w_gate', 'ffn_w_up', 'ffn_conv_w', 'ffn_conv_b', 'ffn_w_down', 'final_norm_w']
TWIN_DIFF_INPUT = 'x'
TWIN_INPUTS = ['x', 'c', 'positions', 'ada_w', 'ada_b', 'norm1_w', 'w_in', 'attn_sinks', 'sgu_ln_w', 'sgu_ln_b', 'sgu_w', 'sgu_b', 'proj_a', 'proj_b', 'w_out', 'norm2_w', 'ffn_w_gate', 'ffn_w_up', 'ffn_conv_w', 'ffn_conv_b', 'ffn_w_down', 'final_norm_w', 'loss_target', 'm_ada_w', 'm_ada_b', 'm_norm1_w', 'm_w_in', 'm_attn_sinks', 'm_sgu_ln_w', 'm_sgu_ln_b', 'm_sgu_w', 'm_sgu_b', 'm_proj_a', 'm_proj_b', 'm_w_out', 'm_norm2_w', 'm_ffn_w_gate', 'm_ffn_w_up', 'm_ffn_conv_w', 'm_ffn_conv_b', 'm_ffn_w_down', 'm_final_norm_w', 'v_ada_w', 'v_ada_b', 'v_norm1_w', 'v_w_in', 'v_attn_sinks', 'v_sgu_ln_w', 'v_sgu_ln_b', 'v_sgu_w', 'v_sgu_b', 'v_proj_a', 'v_proj_b', 'v_w_out', 'v_norm2_w', 'v_ffn_w_gate', 'v_ffn_w_up', 'v_ffn_conv_w', 'v_ffn_conv_b', 'v_ffn_w_down', 'v_final_norm_w']
TWIN_OUTPUTS = ['loss', 'grad_x', 'grad_ada_w', 'grad_ada_b', 'grad_norm1_w', 'grad_w_in', 'grad_attn_sinks', 'grad_sgu_ln_w', 'grad_sgu_ln_b', 'grad_sgu_w', 'grad_sgu_b', 'grad_proj_a', 'grad_proj_b', 'grad_w_out', 'grad_norm2_w', 'grad_ffn_w_gate', 'grad_ffn_w_up', 'grad_ffn_conv_w', 'grad_ffn_conv_b', 'grad_ffn_w_down', 'grad_final_norm_w', 'delta_ada_w', 'delta_ada_b', 'delta_norm1_w', 'delta_w_in', 'delta_attn_sinks', 'delta_sgu_ln_w', 'delta_sgu_ln_b', 'delta_sgu_w', 'delta_sgu_b', 'delta_proj_a', 'delta_proj_b', 'delta_w_out', 'delta_norm2_w', 'delta_ffn_w_gate', 'delta_ffn_w_up', 'delta_ffn_conv_w', 'delta_ffn_conv_b', 'delta_ffn_w_down', 'delta_final_norm_w', 'new_m_ada_w', 'new_m_ada_b', 'new_m_norm1_w', 'new_m_w_in', 'new_m_attn_sinks', 'new_m_sgu_ln_w', 'new_m_sgu_ln_b', 'new_m_sgu_w', 'new_m_sgu_b', 'new_m_proj_a', 'new_m_proj_b', 'new_m_w_out', 'new_m_norm2_w', 'new_m_ffn_w_gate', 'new_m_ffn_w_up', 'new_m_ffn_conv_w', 'new_m_ffn_conv_b', 'new_m_ffn_w_down', 'new_m_final_norm_w', 'new_v_ada_w', 'new_v_ada_b', 'new_v_norm1_w', 'new_v_w_in', 'new_v_attn_sinks', 'new_v_sgu_ln_w', 'new_v_sgu_ln_b', 'new_v_sgu_w', 'new_v_sgu_b', 'new_v_proj_a', 'new_v_proj_b', 'new_v_w_out', 'new_v_norm2_w', 'new_v_ffn_w_gate', 'new_v_ffn_w_up', 'new_v_ffn_conv_w', 'new_v_ffn_conv_b', 'new_v_ffn_w_down', 'new_v_final_norm_w']
TWIN_LEAF_KINDS = {'loss': 'loss', 'grad_x': 'grad_x', 'grad_ada_w': 'grad_w', 'grad_ada_b': 'grad_w', 'grad_norm1_w': 'grad_w', 'grad_w_in': 'grad_w', 'grad_attn_sinks': 'grad_w', 'grad_sgu_ln_w': 'grad_w', 'grad_sgu_ln_b': 'grad_w', 'grad_sgu_w': 'grad_w', 'grad_sgu_b': 'grad_w', 'grad_proj_a': 'grad_w', 'grad_proj_b': 'grad_w', 'grad_w_out': 'grad_w', 'grad_norm2_w': 'grad_w', 'grad_ffn_w_gate': 'grad_w', 'grad_ffn_w_up': 'grad_w', 'grad_ffn_conv_w': 'grad_w', 'grad_ffn_conv_b': 'grad_w', 'grad_ffn_w_down': 'grad_w', 'grad_final_norm_w': 'grad_w', 'delta_ada_w': 'delta_w', 'delta_ada_b': 'delta_w', 'delta_norm1_w': 'delta_w', 'delta_w_in': 'delta_w', 'delta_attn_sinks': 'delta_w', 'delta_sgu_ln_w': 'delta_w', 'delta_sgu_ln_b': 'delta_w', 'delta_sgu_w': 'delta_w', 'delta_sgu_b': 'delta_w', 'delta_proj_a': 'delta_w', 'delta_proj_b': 'delta_w', 'delta_w_out': 'delta_w', 'delta_norm2_w': 'delta_w', 'delta_ffn_w_gate': 'delta_w', 'delta_ffn_w_up': 'delta_w', 'delta_ffn_conv_w': 'delta_w', 'delta_ffn_conv_b': 'delta_w', 'delta_ffn_w_down': 'delta_w', 'delta_final_norm_w': 'delta_w', 'new_m_ada_w': 'new_m', 'new_m_ada_b': 'new_m', 'new_m_norm1_w': 'new_m', 'new_m_w_in': 'new_m', 'new_m_attn_sinks': 'new_m', 'new_m_sgu_ln_w': 'new_m', 'new_m_sgu_ln_b': 'new_m', 'new_m_sgu_w': 'new_m', 'new_m_sgu_b': 'new_m', 'new_m_proj_a': 'new_m', 'new_m_proj_b': 'new_m', 'new_m_w_out': 'new_m', 'new_m_norm2_w': 'new_m', 'new_m_ffn_w_gate': 'new_m', 'new_m_ffn_w_up': 'new_m', 'new_m_ffn_conv_w': 'new_m', 'new_m_ffn_conv_b': 'new_m', 'new_m_ffn_w_down': 'new_m', 'new_m_final_norm_w': 'new_m', 'new_v_ada_w': 'new_v', 'new_v_ada_b': 'new_v', 'new_v_norm1_w': 'new_v', 'new_v_w_in': 'new_v', 'new_v_attn_sinks': 'new_v', 'new_v_sgu_ln_w': 'new_v', 'new_v_sgu_ln_b': 'new_v', 'new_v_sgu_w': 'new_v', 'new_v_sgu_b': 'new_v', 'new_v_proj_a': 'new_v', 'new_v_proj_b': 'new_v', 'new_v_w_out': 'new_v', 'new_v_norm2_w': 'new_v', 'new_v_ffn_w_gate': 'new_v', 'new_v_ffn_w_up': 'new_v', 'new_v_ffn_conv_w': 'new_v', 'new_v_ffn_conv_b': 'new_v', 'new_v_ffn_w_down': 'new_v', 'new_v_final_norm_w': 'new_v'}


def _forward(args):
    return _fwd_reference(*[args[k] for k in FWD_PARAMS])


def _output_shape():
    def fwd():
        inp = _fwd_setup_inputs(0)
        return _fwd_reference(*[inp[k] for k in FWD_PARAMS])
    out = _jax.eval_shape(fwd)
    return out.shape, out.dtype

N_MICROBATCH = 1
ADAM_LR = 0.001
ADAM_B1 = 0.9
ADAM_B2 = 0.999
ADAM_EPS = 1e-08
ADAM_WD = 0.01
ADAM_STEP = 10
PER_EXAMPLE_BATCH_AXIS = {'x': 0, 'c': 0, 'positions': 0, 'loss_target': 0}
SHARED_INPUTS = []
_WEIGHT_DTYPES = {'ada_w': _jnp.float32, 'ada_b': _jnp.float32, 'norm1_w': _jnp.float32, 'w_in': _jnp.float32, 'attn_sinks': _jnp.float32, 'sgu_ln_w': _jnp.float32, 'sgu_ln_b': _jnp.float32, 'sgu_w': _jnp.float32, 'sgu_b': _jnp.float32, 'proj_a': _jnp.float32, 'proj_b': _jnp.float32, 'w_out': _jnp.float32, 'norm2_w': _jnp.float32, 'ffn_w_gate': _jnp.float32, 'ffn_w_up': _jnp.float32, 'ffn_conv_w': _jnp.float32, 'ffn_conv_b': _jnp.float32, 'ffn_w_down': _jnp.float32, 'final_norm_w': _jnp.float32}
MOMENT_SCALE = {'ada_w': 1.097025e-01, 'ada_b': 2.151141e-01, 'norm1_w': 8.213391e-02, 'w_in': 4.783256e-02, 'attn_sinks': 1.592608e-02, 'sgu_ln_w': 3.524244e-02, 'sgu_ln_b': 3.512415e-02, 'sgu_w': 3.520654e-02, 'sgu_b': 4.852616e-02, 'proj_a': 7.351944e-02, 'proj_b': 6.280586e-02, 'w_out': 9.743609e-02, 'norm2_w': 1.171558e-01, 'ffn_w_gate': 6.001911e-02, 'ffn_w_up': 6.129904e-02, 'ffn_conv_w': 6.312563e-02, 'ffn_conv_b': 5.199918e-02, 'ffn_w_down': 1.022041e-01, 'final_norm_w': 3.258680e+01}


def _to_microbatches(a, axis):
    t = _jnp.moveaxis(a, axis, 0)
    t = t.reshape((N_MICROBATCH, t.shape[0] // N_MICROBATCH) + t.shape[1:])
    return _jnp.moveaxis(t, 1, axis + 1)


def setup_inputs(seed: int = 0) -> dict:
    inp = _fwd_setup_inputs(seed)
    key = _jax.random.fold_in(_jax.random.key(seed), 7919)
    shape, _ = _output_shape()
    out = dict(inp)
    out["loss_target"] = _jax.random.normal(_jax.random.fold_in(key, 0), shape, _jnp.float32)
    for i, name in enumerate(TWIN_WEIGHTS):
        w = inp[name].astype(_jnp.float32)
        if MOMENT_SCALE is None:
            s = _jnp.sqrt(_jnp.mean(_jnp.square(w)) + 1e-30)
        else:
            s = MOMENT_SCALE[name]
        km, kv = _jax.random.split(_jax.random.fold_in(key, i + 1))
        out[name] = w
        out["m_" + name] = s * _jax.random.normal(km, w.shape, _jnp.float32)
        out["v_" + name] = (s * s) * _jax.random.uniform(kv, w.shape, _jnp.float32, 0.5, 1.5)
    if N_MICROBATCH > 1:
        for name, axis in PER_EXAMPLE_BATCH_AXIS.items():
            out[name] = _to_microbatches(out[name], axis)
    return {'x': out['x'], 'c': out['c'], 'positions': out['positions'], 'ada_w': out['ada_w'], 'ada_b': out['ada_b'], 'norm1_w': out['norm1_w'], 'w_in': out['w_in'], 'attn_sinks': out['attn_sinks'], 'sgu_ln_w': out['sgu_ln_w'], 'sgu_ln_b': out['sgu_ln_b'], 'sgu_w': out['sgu_w'], 'sgu_b': out['sgu_b'], 'proj_a': out['proj_a'], 'proj_b': out['proj_b'], 'w_out': out['w_out'], 'norm2_w': out['norm2_w'], 'ffn_w_gate': out['ffn_w_gate'], 'ffn_w_up': out['ffn_w_up'], 'ffn_conv_w': out['ffn_conv_w'], 'ffn_conv_b': out['ffn_conv_b'], 'ffn_w_down': out['ffn_w_down'], 'final_norm_w': out['final_norm_w'], 'loss_target': out['loss_target'], 'm_ada_w': out['m_ada_w'], 'm_ada_b': out['m_ada_b'], 'm_norm1_w': out['m_norm1_w'], 'm_w_in': out['m_w_in'], 'm_attn_sinks': out['m_attn_sinks'], 'm_sgu_ln_w': out['m_sgu_ln_w'], 'm_sgu_ln_b': out['m_sgu_ln_b'], 'm_sgu_w': out['m_sgu_w'], 'm_sgu_b': out['m_sgu_b'], 'm_proj_a': out['m_proj_a'], 'm_proj_b': out['m_proj_b'], 'm_w_out': out['m_w_out'], 'm_norm2_w': out['m_norm2_w'], 'm_ffn_w_gate': out['m_ffn_w_gate'], 'm_ffn_w_up': out['m_ffn_w_up'], 'm_ffn_conv_w': out['m_ffn_conv_w'], 'm_ffn_conv_b': out['m_ffn_conv_b'], 'm_ffn_w_down': out['m_ffn_w_down'], 'm_final_norm_w': out['m_final_norm_w'], 'v_ada_w': out['v_ada_w'], 'v_ada_b': out['v_ada_b'], 'v_norm1_w': out['v_norm1_w'], 'v_w_in': out['v_w_in'], 'v_attn_sinks': out['v_attn_sinks'], 'v_sgu_ln_w': out['v_sgu_ln_w'], 'v_sgu_ln_b': out['v_sgu_ln_b'], 'v_sgu_w': out['v_sgu_w'], 'v_sgu_b': out['v_sgu_b'], 'v_proj_a': out['v_proj_a'], 'v_proj_b': out['v_proj_b'], 'v_w_out': out['v_w_out'], 'v_norm2_w': out['v_norm2_w'], 'v_ffn_w_gate': out['v_ffn_w_gate'], 'v_ffn_w_up': out['v_ffn_w_up'], 'v_ffn_conv_w': out['v_ffn_conv_w'], 'v_ffn_conv_b': out['v_ffn_conv_b'], 'v_ffn_w_down': out['v_ffn_w_down'], 'v_final_norm_w': out['v_final_norm_w']}


def _loss(weights, diff, rest, loss_target):
    with _jax.named_scope("forward"):
        args = {**rest, TWIN_DIFF_INPUT: diff, **{k: w.astype(_WEIGHT_DTYPES[k]) for k, w in weights.items()}}
        y = _forward(args)
    with _jax.named_scope("loss_head"):
        err = _jnp.square(y.astype(_jnp.float32) - loss_target)
        return 0.5 * _jnp.sum(_jnp.mean(err, axis=-1)) if err.ndim else 0.5 * err


def _adamw(w, g, m, v):
    m = ADAM_B1 * m + (1.0 - ADAM_B1) * g
    v = ADAM_B2 * v + (1.0 - ADAM_B2) * _jnp.square(g)
    m_hat = m / (1.0 - ADAM_B1 ** ADAM_STEP)
    v_hat = v / (1.0 - ADAM_B2 ** ADAM_STEP)
    delta = -ADAM_LR * (m_hat / (_jnp.sqrt(v_hat) + ADAM_EPS) + ADAM_WD * w)
    return delta, m, v


def reference(x, c, positions, ada_w, ada_b, norm1_w, w_in, attn_sinks, sgu_ln_w, sgu_ln_b, sgu_w, sgu_b, proj_a, proj_b, w_out, norm2_w, ffn_w_gate, ffn_w_up, ffn_conv_w, ffn_conv_b, ffn_w_down, final_norm_w, loss_target, m_ada_w, m_ada_b, m_norm1_w, m_w_in, m_attn_sinks, m_sgu_ln_w, m_sgu_ln_b, m_sgu_w, m_sgu_b, m_proj_a, m_proj_b, m_w_out, m_norm2_w, m_ffn_w_gate, m_ffn_w_up, m_ffn_conv_w, m_ffn_conv_b, m_ffn_w_down, m_final_norm_w, v_ada_w, v_ada_b, v_norm1_w, v_w_in, v_attn_sinks, v_sgu_ln_w, v_sgu_ln_b, v_sgu_w, v_sgu_b, v_proj_a, v_proj_b, v_w_out, v_norm2_w, v_ffn_w_gate, v_ffn_w_up, v_ffn_conv_w, v_ffn_conv_b, v_ffn_w_down, v_final_norm_w):
    given = dict(x=x, c=c, positions=positions, ada_w=ada_w, ada_b=ada_b, norm1_w=norm1_w, w_in=w_in, attn_sinks=attn_sinks, sgu_ln_w=sgu_ln_w, sgu_ln_b=sgu_ln_b, sgu_w=sgu_w, sgu_b=sgu_b, proj_a=proj_a, proj_b=proj_b, w_out=w_out, norm2_w=norm2_w, ffn_w_gate=ffn_w_gate, ffn_w_up=ffn_w_up, ffn_conv_w=ffn_conv_w, ffn_conv_b=ffn_conv_b, ffn_w_down=ffn_w_down, final_norm_w=final_norm_w, loss_target=loss_target, m_ada_w=m_ada_w, m_ada_b=m_ada_b, m_norm1_w=m_norm1_w, m_w_in=m_w_in, m_attn_sinks=m_attn_sinks, m_sgu_ln_w=m_sgu_ln_w, m_sgu_ln_b=m_sgu_ln_b, m_sgu_w=m_sgu_w, m_sgu_b=m_sgu_b, m_proj_a=m_proj_a, m_proj_b=m_proj_b, m_w_out=m_w_out, m_norm2_w=m_norm2_w, m_ffn_w_gate=m_ffn_w_gate, m_ffn_w_up=m_ffn_w_up, m_ffn_conv_w=m_ffn_conv_w, m_ffn_conv_b=m_ffn_conv_b, m_ffn_w_down=m_ffn_w_down, m_final_norm_w=m_final_norm_w, v_ada_w=v_ada_w, v_ada_b=v_ada_b, v_norm1_w=v_norm1_w, v_w_in=v_w_in, v_attn_sinks=v_attn_sinks, v_sgu_ln_w=v_sgu_ln_w, v_sgu_ln_b=v_sgu_ln_b, v_sgu_w=v_sgu_w, v_sgu_b=v_sgu_b, v_proj_a=v_proj_a, v_proj_b=v_proj_b, v_w_out=v_w_out, v_norm2_w=v_norm2_w, v_ffn_w_gate=v_ffn_w_gate, v_ffn_w_up=v_ffn_w_up, v_ffn_conv_w=v_ffn_conv_w, v_ffn_conv_b=v_ffn_conv_b, v_ffn_w_down=v_ffn_w_down, v_final_norm_w=v_final_norm_w)
    weights = {n: given[n] for n in TWIN_WEIGHTS}
    shared = {n: given[n] for n in SHARED_INPUTS}
    per_example = {n: given[n] for n in ['x', 'c', 'positions']}
    grad_fn = _jax.value_and_grad(_loss, argnums=(0, 1))

    def one_microbatch(ex, loss_target):
        ex = dict(ex)
        diff = ex.pop(TWIN_DIFF_INPUT)
        return grad_fn(weights, diff, {**shared, **ex}, loss_target)

    if N_MICROBATCH == 1:
        loss, (grad_w, grad_x) = one_microbatch(per_example, given["loss_target"])
    else:
        def body(carry, xs):
            loss_sum, grad_sum = carry
            l_k, (gw_k, gx_k) = one_microbatch(xs[0], xs[1])
            with _jax.named_scope("update"):
                return (loss_sum + l_k, _jax.tree.map(_jnp.add, grad_sum, gw_k)), gx_k

        init = (_jnp.zeros((), _jnp.float32), _jax.tree.map(_jnp.zeros_like, weights))
        (loss, grad_w), grad_x = _jax.lax.scan(body, init, (per_example, given["loss_target"]))
    with _jax.named_scope("update"):
        delta_w, new_m, new_v = {}, {}, {}
        for n in TWIN_WEIGHTS:
            delta_w[n], new_m[n], new_v[n] = _adamw(weights[n], grad_w[n], given["m_" + n], given["v_" + n])
    return (loss, grad_x, *[grad_w[n] for n in TWIN_WEIGHTS], *[delta_w[n] for n in TWIN_WEIGHTS],
            *[new_m[n] for n in TWIN_WEIGHTS], *[new_v[n] for n in TWIN_WEIGHTS])
```

```python
import functools

import jax
import jax.numpy as jnp
from jax import lax
from jax.experimental import pallas as pl
from jax.experimental.pallas import tpu as pltpu

F32 = jnp.float32
BF = jnp.bfloat16

D_MODEL = 1024
N_Q_HEADS = 16
N_KV_HEADS = 2
HEAD_DIM = 64
ATTN_BLOCK = 128
ROPE_THETA = 500000.0
ROT_DIM = HEAD_DIM // 4
SGU_WIDTH = 1024
SGU_GROUPS = 8
SGU_CHUNK = 128
FFN_DIM = 2816
NORM_EPS = 1e-6
DEPTH = 2
IN_COLS = 5376
N_CHIPS = 4
FFN_SHARD = FFN_DIM // N_CHIPS
FFN_SHARD_PAD = 768
FFN_PAD = N_CHIPS * FFN_SHARD_PAD
LANES = 128
SUBLANES = 8
VMEM_LIMIT = 56 * 1024 * 1024
NEG_BIG = -1e30

ADAM_LR = 0.001
ADAM_B1 = 0.9
ADAM_B2 = 0.999
ADAM_EPS = 1e-08
ADAM_WD = 0.01
ADAM_STEP = 10

MESH = pl.DeviceIdType.MESH

_Q_END = 1024
_K_END = 1152
_V_END = 1280
_U_END = 2304
_Z_END = 3328
_GA_END = 4352
_IN_SEGMENTS = ((0, _Q_END), (_V_END, _U_END), (_U_END, _Z_END), (_Z_END, _GA_END), (_GA_END, IN_COLS),
                (_Q_END, _K_END), (_K_END, _V_END))
COL_Q, COL_U, COL_V, COL_GA, COL_GB = 0, 1, 2, 3, 4
COL_KV = 20


def _sds(shape, dtype):
    return jax.ShapeDtypeStruct(tuple(shape), dtype)


def _call(body, name, out_shape, grid, in_specs, out_specs, semantics, scratch=()):
    return pl.pallas_call(
        body, name=name, out_shape=out_shape, grid=grid, in_specs=in_specs, out_specs=out_specs,
        scratch_shapes=scratch,
        compiler_params=pltpu.CompilerParams(dimension_semantics=semantics, vmem_limit_bytes=VMEM_LIMIT))


def _rows(tm, width, col=0):
    return pl.BlockSpec((tm, width), lambda i: (i, col))


def _vec(width):
    return pl.BlockSpec((1, width), lambda i: (0, 0))


def _sigmoid(x):
    return 1.0 / (1.0 + jnp.exp(-x))


def _gelu(x):
    return 0.5 * x * (1.0 + lax.erf(x * 0.7071067811865476))


def _gelu_grad(x):
    cdf = 0.5 * (1.0 + lax.erf(x * 0.7071067811865476))
    return cdf + x * jnp.exp(-0.5 * x * x) * 0.3989422804014327


def _dot(a, b):
    return jnp.dot(a, b, preferred_element_type=F32)


def _dot_nt(a, b):
    return lax.dot_general(a, b, (((1,), (1,)), ((), ())), preferred_element_type=F32)


def _dot_tn(a, b):
    return lax.dot_general(a, b, (((0,), (0,)), ((), ())), preferred_element_type=F32)


def _matmul(name, a, b, tm=512, tn=1024, out_dtype=F32):
    m, k = a.shape
    n = b.shape[1]
    tn = min(tn, n)
    tm = min(tm, m)

    def body(a_ref, b_ref, o_ref):
        o_ref[...] = _dot(a_ref[...], b_ref[...]).astype(o_ref.dtype)

    return _call(body, name, _sds((m, n), out_dtype), (n // tn, m // tm),
                 [pl.BlockSpec((tm, k), lambda j, i: (i, 0)), pl.BlockSpec((k, tn), lambda j, i: (0, j))],
                 pl.BlockSpec((tm, tn), lambda j, i: (i, j)), ("parallel", "parallel"))(a, b)


def _matmul2(name, a1, b1, a2, b2, tm=512, tn=1024):
    m, k = a1.shape
    n = b1.shape[1]
    tn = min(tn, n)
    tm = min(tm, m)

    def body(a1_ref, b1_ref, a2_ref, b2_ref, o_ref):
        o_ref[...] = _dot(a1_ref[...], b1_ref[...]) + _dot(a2_ref[...], b2_ref[...])

    aspec = pl.BlockSpec((tm, k), lambda j, i: (i, 0))
    bspec = pl.BlockSpec((k, tn), lambda j, i: (0, j))
    return _call(body, name, _sds((m, n), F32), (n // tn, m // tm), [aspec, bspec, aspec, bspec],
                 pl.BlockSpec((tm, tn), lambda j, i: (i, j)), ("parallel", "parallel"))(a1, b1, a2, b2)


def _matmul_tn(name, a, b, tk=512, tn=1024, ts=512):
    s, k = a.shape
    n = b.shape[1]
    tk, tn, ts = min(tk, k), min(tn, n), min(ts, s)
    steps = s // ts

    def body(a_ref, b_ref, o_ref, acc_ref):
        t = pl.program_id(2)

        @pl.when(t == 0)
        def _():
            acc_ref[...] = jnp.zeros_like(acc_ref)

        acc_ref[...] += _dot_tn(a_ref[...], b_ref[...])

        @pl.when(t == steps - 1)
        def _():
            o_ref[...] = acc_ref[...].astype(o_ref.dtype)

    return _call(body, name, _sds((k, n), BF), (k // tk, n // tn, steps),
                 [pl.BlockSpec((ts, tk), lambda i, j, t: (t, i)), pl.BlockSpec((ts, tn), lambda i, j, t: (t, j))],
                 pl.BlockSpec((tk, tn), lambda i, j, t: (i, j)), ("parallel", "parallel", "arbitrary"),
                 scratch=[pltpu.VMEM((tk, tn), F32)])(a, b)


def _norm_mod(name, x, w, sc, sh, tm=512):
    s, d = x.shape
    tm = min(tm, s)

    def body(x_ref, w_ref, sc_ref, sh_ref, h_ref):
        xv = x_ref[...]
        r = lax.rsqrt(jnp.mean(xv * xv, axis=-1, keepdims=True) + NORM_EPS)
        h_ref[...] = ((xv * r) * w_ref[...] * (1.0 + sc_ref[...]) + sh_ref[...]).astype(BF)

    return _call(body, name, _sds((s, d), BF), (s // tm,), [_rows(tm, d), _vec(d), _vec(d), _vec(d)],
                 _rows(tm, d), ("parallel",))(x, w, sc, sh)


def _norm_mod_bwd(name, x, dh, w, sc, dx_in, tm=512):
    s, d = x.shape
    tm = min(tm, s)

    def body(x_ref, dh_ref, w_ref, sc_ref, dxin_ref, dx_ref, da_ref, dsh_ref):
        @pl.when(pl.program_id(0) == 0)
        def _():
            da_ref[...] = jnp.zeros_like(da_ref)
            dsh_ref[...] = jnp.zeros_like(dsh_ref)

        xv = x_ref[...]
        dh = dh_ref[...]
        r = lax.rsqrt(jnp.mean(xv * xv, axis=-1, keepdims=True) + NORM_EPS)
        xn = xv * r
        dxn = dh * (w_ref[...] * (1.0 + sc_ref[...]))
        dx_ref[...] = dxin_ref[...] + r * (dxn - xn * jnp.mean(dxn * xn, axis=-1, keepdims=True))
        da_ref[...] += jnp.sum(dh * xn, axis=0, keepdims=True)
        dsh_ref[...] += jnp.sum(dh, axis=0, keepdims=True)

    return _call(body, name, (_sds((s, d), F32), _sds((1, d), F32), _sds((1, d), F32)), (s // tm,),
                 [_rows(tm, d), _rows(tm, d), _vec(d), _vec(d), _rows(tm, d)],
                 (_rows(tm, d), _vec(d), _vec(d)), ("arbitrary",))(x, dh, w, sc, dx_in)


def _residual(name, x, o, g, tm=512):
    s, d = x.shape
    tm = min(tm, s)

    def body(x_ref, o_ref, g_ref, y_ref):
        y_ref[...] = x_ref[...] + g_ref[...] * o_ref[...]

    return _call(body, name, _sds((s, d), F32), (s // tm,), [_rows(tm, d), _rows(tm, d), _vec(d)],
                 _rows(tm, d), ("parallel",))(x, o, g)


def _residual_bwd(name, dx, o, g, tm=512):
    s, d = dx.shape
    tm = min(tm, s)

    def body(dx_ref, o_ref, g_ref, do_ref, dg_ref):
        @pl.when(pl.program_id(0) == 0)
        def _():
            dg_ref[...] = jnp.zeros_like(dg_ref)

        dxv = dx_ref[...]
        do_ref[...] = (dxv * g_ref[...]).astype(BF)
        dg_ref[...] += jnp.sum(dxv * o_ref[...], axis=0, keepdims=True)

    return _call(body, name, (_sds((s, d), BF), _sds((1, d), F32)), (s // tm,),
                 [_rows(tm, d), _rows(tm, d), _vec(d)], (_rows(tm, d), _vec(d)), ("arbitrary",))(dx, o, g)


def _rope_partner(v):
    lane = lax.broadcasted_iota(jnp.int32, (1, LANES), 1) % HEAD_DIM
    return jnp.where(lane < ROT_DIM // 2, pltpu.roll(v, LANES - ROT_DIM // 2, axis=1), pltpu.roll(v, ROT_DIM // 2, axis=1))


def _dup_half(v, half):
    lane = lax.broadcasted_iota(jnp.int32, (1, LANES), 1)
    keep = jnp.where((lane >= HEAD_DIM) == (half == 1), v, 0.0)
    return keep + pltpu.roll(keep, HEAD_DIM, axis=1)


def _rope_qkv(name, proj, cosf, sinf, tm=512):
    s = proj.shape[0]
    tm = min(tm, s)

    def body(q_ref, kv_ref, cos_ref, sin_ref, qr_ref, kk0_ref, kk1_ref, vv0_ref, vv1_ref):
        cosv, sinv = cos_ref[...], sin_ref[...]
        for j in range(D_MODEL // LANES):
            qv = q_ref[:, j * LANES:(j + 1) * LANES]
            qr_ref[:, j * LANES:(j + 1) * LANES] = (qv * cosv + _rope_partner(qv) * sinv).astype(BF)
        kv = kv_ref[:, :LANES]
        kr = kv * cosv + _rope_partner(kv) * sinv
        vv = kv_ref[:, LANES:]
        kk0_ref[...] = _dup_half(kr, 0).astype(BF)
        kk1_ref[...] = _dup_half(kr, 1).astype(BF)
        vv0_ref[...] = _dup_half(vv, 0).astype(BF)
        vv1_ref[...] = _dup_half(vv, 1).astype(BF)

    kvs = _sds((s, LANES), BF)
    return _call(body, name, (_sds((s, D_MODEL), BF), kvs, kvs, kvs, kvs), (s // tm,),
                 [_rows(tm, D_MODEL, COL_Q), _rows(tm, 2 * LANES, COL_KV), _rows(tm, LANES), _rows(tm, LANES)],
                 (_rows(tm, D_MODEL),) + (_rows(tm, LANES),) * 4, ("parallel",))(proj, proj, cosf, sinf)


def _rope_bwd(name, dqr, dkv_cur, dkv_prev, cosf, sinf, tm=512):
    s = dqr.shape[0]
    tm = min(tm, s)
    steps = s // tm
    per = tm // ATTN_BLOCK
    nb = s // ATTN_BLOCK

    def unrope(v, cosv, sinv):
        return v * cosv - _rope_partner(v) * sinv

    def body(dq_ref, cur_ref, prev_ref, next_ref, cos_ref, sin_ref, dqo_ref, dkvo_ref):
        i = pl.program_id(0)
        cosv, sinv = cos_ref[...], sin_ref[...]
        for j in range(D_MODEL // LANES):
            dqo_ref[:, j * LANES:(j + 1) * LANES] = unrope(dq_ref[:, j * LANES:(j + 1) * LANES], cosv, sinv).astype(BF)
        nxt = jnp.where(i < steps - 1, next_ref[...], 0.0)
        if per > 1:
            shifted = jnp.concatenate([prev_ref[ATTN_BLOCK:, :], nxt], axis=0)
        else:
            shifted = nxt
        tot = cur_ref[...] + shifted
        dkvo_ref[:, :LANES] = unrope(tot[:, :LANES], cosv, sinv).astype(BF)
        dkvo_ref[:, LANES:] = tot[:, LANES:].astype(BF)

    nxt_spec = pl.BlockSpec((ATTN_BLOCK, 2 * LANES), lambda i: (jnp.minimum((i + 1) * per, nb - 1), 0))
    return _call(body, name, (_sds((s, D_MODEL), BF), _sds((s, 2 * LANES), BF)), (steps,),
                 [_rows(tm, D_MODEL), _rows(tm, 2 * LANES), _rows(tm, 2 * LANES), nxt_spec, _rows(tm, LANES),
                  _rows(tm, LANES)],
                 (_rows(tm, D_MODEL), _rows(tm, 2 * LANES)), ("parallel",))(dqr, dkv_cur, dkv_prev, dkv_prev, cosf, sinf)


def _band_mask(n):
    qi = lax.broadcasted_iota(jnp.int32, (ATTN_BLOCK, 2 * ATTN_BLOCK), 0)
    kj = lax.broadcasted_iota(jnp.int32, (ATTN_BLOCK, 2 * ATTN_BLOCK), 1)
    return (kj > qi) & (kj <= qi + ATTN_BLOCK) & ((n > 0) | (kj >= ATTN_BLOCK))


def _attn_probs(qa, kb, allowed, sink):
    sc = jnp.where(allowed, _dot_nt(qa, kb) * (HEAD_DIM ** -0.5), NEG_BIG)
    m = jnp.maximum(jnp.max(sc, axis=-1, keepdims=True), sink)
    p = jnp.exp(sc - m)
    esink = jnp.exp(sink - m)
    inv = 1.0 / (jnp.sum(p, axis=-1, keepdims=True) + esink)
    return p * inv, esink * inv


def _kv_specs():
    cur = pl.BlockSpec((ATTN_BLOCK, LANES), lambda n: (n, 0))
    prev = pl.BlockSpec((ATTN_BLOCK, LANES), lambda n: (jnp.maximum(n - 1, 0), 0))
    return [prev, cur] * 4


def _attention(name, qr, kk0, kk1, vv0, vv1, sinks):
    s = qr.shape[0]
    nb = s // ATTN_BLOCK

    def body(sink_ref, q_ref, k0p, k0c, k1p, k1c, v0p, v0c, v1p, v1c, y_ref):
        allowed = _band_mask(pl.program_id(0))
        upper = lax.broadcasted_iota(jnp.int32, (1, LANES), 1) >= HEAD_DIM
        bands = ((jnp.concatenate([k0p[...], k0c[...]], axis=0), jnp.concatenate([v0p[...], v0c[...]], axis=0)),
                 (jnp.concatenate([k1p[...], k1c[...]], axis=0), jnp.concatenate([v1p[...], v1c[...]], axis=0)))
        for hk in range(N_KV_HEADS):
            kb, vb = bands[hk]
            for j in range(4):
                col = (hk * 4 + j) * LANES
                qp = q_ref[:, col:col + LANES]
                out = jnp.zeros((ATTN_BLOCK, LANES), F32)
                for half in range(2):
                    sel = upper if half else jnp.logical_not(upper)
                    qa = jnp.where(sel, qp, jnp.zeros_like(qp))
                    pn, _ = _attn_probs(qa, kb, allowed, sink_ref[hk * 8 + j * 2 + half])
                    out = jnp.where(sel, _dot(pn.astype(BF), vb), out)
                y_ref[:, col:col + LANES] = out.astype(BF)

    return _call(body, name, _sds((s, D_MODEL), BF), (nb,),
                 [pl.BlockSpec(memory_space=pltpu.SMEM), pl.BlockSpec((ATTN_BLOCK, D_MODEL), lambda n: (n, 0))] + _kv_specs(),
                 pl.BlockSpec((ATTN_BLOCK, D_MODEL), lambda n: (n, 0)), ("parallel",))(
                     sinks, qr, kk0, kk0, kk1, kk1, vv0, vv0, vv1, vv1)


def _attention_bwd(name, qr, kk0, kk1, vv0, vv1, sinks, dy):
    s = qr.shape[0]
    nb = s // ATTN_BLOCK

    def body(sink_ref, q_ref, dy_ref, k0p, k0c, k1p, k1c, v0p, v0c, v1p, v1c, dq_ref, cur_ref, prev_ref, dsink_ref):
        @pl.when(pl.program_id(0) == 0)
        def _():
            dsink_ref[...] = jnp.zeros_like(dsink_ref)

        allowed = _band_mask(pl.program_id(0))
        lane = lax.broadcasted_iota(jnp.int32, (1, LANES), 1)
        upper = lane >= HEAD_DIM
        bands = ((jnp.concatenate([k0p[...], k0c[...]], axis=0), jnp.concatenate([v0p[...], v0c[...]], axis=0)),
                 (jnp.concatenate([k1p[...], k1c[...]], axis=0), jnp.concatenate([v1p[...], v1c[...]], axis=0)))
        dsink = jnp.zeros((1, LANES), F32)
        dk_slab = jnp.zeros((2 * ATTN_BLOCK, LANES), F32)
        dv_slab = jnp.zeros((2 * ATTN_BLOCK, LANES), F32)
        for hk in range(N_KV_HEADS):
            kb, vb = bands[hk]
            dkk = jnp.zeros((2 * ATTN_BLOCK, LANES), F32)
            dvv = jnp.zeros((2 * ATTN_BLOCK, LANES), F32)
            for j in range(4):
                col = (hk * 4 + j) * LANES
                qp = q_ref[:, col:col + LANES]
                dyp = dy_ref[:, col:col + LANES]
                dq_pair = jnp.zeros((ATTN_BLOCK, LANES), F32)
                for half in range(2):
                    h = hk * 8 + j * 2 + half
                    sel = upper if half else jnp.logical_not(upper)
                    qa = jnp.where(sel, qp, jnp.zeros_like(qp))
                    pn, psink = _attn_probs(qa, kb, allowed, sink_ref[h])
                    dya = jnp.where(sel, dyp, 0.0).astype(BF)
                    dp = _dot_nt(dya, vb)
                    delta = jnp.sum(pn * dp, axis=-1, keepdims=True)
                    ds = (pn * (dp - delta) * (HEAD_DIM ** -0.5)).astype(BF)
                    dsink = dsink + jnp.where(lane == h, -jnp.sum(psink * delta), 0.0)
                    dq_pair = jnp.where(sel, _dot(ds, kb), dq_pair)
                    dkk = dkk + _dot_tn(ds, qa)
                    dvv = dvv + _dot_tn(pn.astype(BF), dya)
                dq_ref[:, col:col + LANES] = dq_pair
            mine = upper if hk else jnp.logical_not(upper)
            dk_slab = jnp.where(mine, dkk + pltpu.roll(dkk, HEAD_DIM, axis=1), dk_slab)
            dv_slab = jnp.where(mine, dvv + pltpu.roll(dvv, HEAD_DIM, axis=1), dv_slab)
        prev_ref[:, :LANES] = dk_slab[:ATTN_BLOCK]
        prev_ref[:, LANES:] = dv_slab[:ATTN_BLOCK]
        cur_ref[:, :LANES] = dk_slab[ATTN_BLOCK:]
        cur_ref[:, LANES:] = dv_slab[ATTN_BLOCK:]
        dsink_ref[...] += dsink

    blk = pl.BlockSpec((ATTN_BLOCK, D_MODEL), lambda n: (n, 0))
    kvo = pl.BlockSpec((ATTN_BLOCK, 2 * LANES), lambda n: (n, 0))
    return _call(body, name,
                 (_sds((s, D_MODEL), F32), _sds((s, 2 * LANES), F32), _sds((s, 2 * LANES), F32), _sds((1, LANES), F32)),
                 (nb,), [pl.BlockSpec(memory_space=pltpu.SMEM), blk, blk] + _kv_specs(),
                 (blk, kvo, kvo, pl.BlockSpec((1, LANES), lambda n: (0, 0))), ("arbitrary",))(
                     sinks, qr, dy, kk0, kk0, kk1, kk1, vv0, vv0, vv1, vv1)


def _sgu_weights(wm_ref, g):
    t = lax.broadcasted_iota(jnp.int32, (SGU_CHUNK, SGU_CHUNK), 0)
    sidx = lax.broadcasted_iota(jnp.int32, (SGU_CHUNK, SGU_CHUNK), 1)
    return jnp.where(sidx <= t, wm_ref[g], 0.0).astype(BF)


def _layer_norm_stats(v):
    mu = jnp.mean(v, axis=-1, keepdims=True)
    cen = v - mu
    rstd = lax.rsqrt(jnp.mean(cen * cen, axis=-1, keepdims=True) + NORM_EPS)
    return cen * rstd, rstd


def _sgu(name, proj, ln_w, ln_b, wm, bfull, tm=256):
    s = proj.shape[0]
    tm = min(tm, s)
    w = SGU_WIDTH

    def body(u_ref, v_ref, lw_ref, lb_ref, wm_ref, b_ref, y_ref):
        vhat, _ = _layer_norm_stats(_gelu(v_ref[...]))
        vn = (vhat * lw_ref[...] + lb_ref[...]).astype(BF)
        for g in range(SGU_GROUPS):
            wg = _sgu_weights(wm_ref, g)
            cols = slice(g * SGU_CHUNK, (g + 1) * SGU_CHUNK)
            for ch in range(tm // SGU_CHUNK):
                rows = slice(ch * SGU_CHUNK, (ch + 1) * SGU_CHUNK)
                f = _dot(wg, vn[rows, cols]) + b_ref[g]
                y_ref[rows, cols] = (_gelu(u_ref[rows, cols]) * f).astype(BF)

    full3 = pl.BlockSpec((SGU_GROUPS, SGU_CHUNK, SGU_CHUNK), lambda i: (0, 0, 0))
    return _call(body, name, _sds((s, w), BF), (s // tm,),
                 [_rows(tm, w, COL_U), _rows(tm, w, COL_V), _vec(w), _vec(w), full3, full3],
                 _rows(tm, w), ("parallel",))(proj, proj, ln_w, ln_b, wm, bfull)


def _sgu_bwd(name, proj, ln_w, ln_b, wm, bfull, dy, tm=256):
    s = proj.shape[0]
    tm = min(tm, s)
    w = SGU_WIDTH
    steps = s // tm

    def body(u_ref, v_ref, lw_ref, lb_ref, wm_ref, b_ref, dy_ref, du_ref, dv_ref, dwm_ref, db_ref, dlw_ref, dlb_ref,
             dfsum_ref):
        i = pl.program_id(0)

        @pl.when(i == 0)
        def _():
            dwm_ref[...] = jnp.zeros_like(dwm_ref)
            dlw_ref[...] = jnp.zeros_like(dlw_ref)
            dlb_ref[...] = jnp.zeros_like(dlb_ref)
            dfsum_ref[...] = jnp.zeros_like(dfsum_ref)

        vpre = v_ref[...]
        vhat, rstd = _layer_norm_stats(_gelu(vpre))
        vn = (vhat * lw_ref[...] + lb_ref[...]).astype(BF)
        t = lax.broadcasted_iota(jnp.int32, (SGU_CHUNK, SGU_CHUNK), 0)
        sidx = lax.broadcasted_iota(jnp.int32, (SGU_CHUNK, SGU_CHUNK), 1)
        dvn_cols = []
        for g in range(SGU_GROUPS):
            wg = _sgu_weights(wm_ref, g)
            cols = slice(g * SGU_CHUNK, (g + 1) * SGU_CHUNK)
            dvn_rows = []
            dwg = jnp.zeros((SGU_CHUNK, SGU_CHUNK), F32)
            dfs = jnp.zeros((SGU_CHUNK, SGU_CHUNK), F32)
            for ch in range(tm // SGU_CHUNK):
                rows = slice(ch * SGU_CHUNK, (ch + 1) * SGU_CHUNK)
                upre = u_ref[rows, cols]
                dyv = dy_ref[rows, cols]
                f = _dot(wg, vn[rows, cols]) + b_ref[g]
                du_ref[rows, cols] = (dyv * f * _gelu_grad(upre)).astype(BF)
                df = dyv * _gelu(upre)
                dfb = df.astype(BF)
                dvn_rows.append(_dot_tn(wg, dfb))
                dwg = dwg + _dot_nt(dfb, vn[rows, cols])
                dfs = dfs + df
            dwm_ref[g] += jnp.where(sidx <= t, dwg, 0.0)
            dfsum_ref[g] += dfs
            dvn_cols.append(jnp.concatenate(dvn_rows, axis=0) if len(dvn_rows) > 1 else dvn_rows[0])
        dvn = jnp.concatenate(dvn_cols, axis=1)
        dlw_ref[...] += jnp.sum(dvn * vhat, axis=0, keepdims=True)
        dlb_ref[...] += jnp.sum(dvn, axis=0, keepdims=True)
        dvh = dvn * lw_ref[...]
        dvg = rstd * (dvh - jnp.mean(dvh, axis=-1, keepdims=True) - vhat * jnp.mean(dvh * vhat, axis=-1, keepdims=True))
        dv_ref[...] = (dvg * _gelu_grad(vpre)).astype(BF)

        @pl.when(i == steps - 1)
        def _():
            for g in range(SGU_GROUPS):
                db_ref[g:g + 1, :] = jnp.sum(dfsum_ref[g].T, axis=0, keepdims=True)

    full3 = pl.BlockSpec((SGU_GROUPS, SGU_CHUNK, SGU_CHUNK), lambda i: (0, 0, 0))
    return _call(body, name,
                 (_sds((s, w), BF), _sds((s, w), BF), _sds((SGU_GROUPS, SGU_CHUNK, SGU_CHUNK), F32),
                  _sds((SGU_GROUPS, SGU_CHUNK), F32), _sds((1, w), F32), _sds((1, w), F32)),
                 (steps,),
                 [_rows(tm, w, COL_U), _rows(tm, w, COL_V), _vec(w), _vec(w), full3, full3, _rows(tm, w)],
                 (_rows(tm, w), _rows(tm, w), full3, pl.BlockSpec((SGU_GROUPS, SGU_CHUNK), lambda i: (0, 0)), _vec(w), _vec(w)),
                 ("arbitrary",), scratch=[pltpu.VMEM((SGU_GROUPS, SGU_CHUNK, SGU_CHUNK), F32)])(
                     proj, proj, ln_w, ln_b, wm, bfull, dy)


def _merge(name, y_sgu, y_attn, proj_a, proj_b, proj, tm=512):
    s, d = y_sgu.shape
    tm = min(tm, s)

    def body(ys_ref, ya_ref, wa_ref, wb_ref, ga_ref, gb_ref, m_ref, pa_ref, pb_ref):
        pa = _dot(ys_ref[...], wa_ref[...])
        pb = _dot(ya_ref[...], wb_ref[...])
        pa_ref[...] = pa
        pb_ref[...] = pb
        m_ref[...] = (_sigmoid(ga_ref[...]) * pa + _sigmoid(gb_ref[...]) * pb).astype(BF)

    wspec = pl.BlockSpec((d, d), lambda i: (0, 0))
    return _call(body, name, (_sds((s, d), BF), _sds((s, d), F32), _sds((s, d), F32)), (s // tm,),
                 [_rows(tm, d), _rows(tm, d), wspec, wspec, _rows(tm, d, COL_GA), _rows(tm, d, COL_GB)],
                 (_rows(tm, d),) * 3, ("parallel",))(y_sgu, y_attn, proj_a, proj_b, proj, proj)


def _merge_bwd(name, do, w_out_t, proj, pa, pb, tm=512):
    s, d = do.shape
    tm = min(tm, s)

    def body(do_ref, w_ref, ga_ref, gb_ref, pa_ref, pb_ref, dpa_ref, dpb_ref, dga_ref, dgb_ref):
        dm = _dot(do_ref[...], w_ref[...])
        ga = _sigmoid(ga_ref[...])
        gb = _sigmoid(gb_ref[...])
        dpa_ref[...] = (dm * ga).astype(BF)
        dpb_ref[...] = (dm * gb).astype(BF)
        dga_ref[...] = (dm * pa_ref[...] * ga * (1.0 - ga)).astype(BF)
        dgb_ref[...] = (dm * pb_ref[...] * gb * (1.0 - gb)).astype(BF)

    o = _sds((s, d), BF)
    return _call(body, name, (o, o, o, o), (s // tm,),
                 [_rows(tm, d), pl.BlockSpec((d, d), lambda i: (0, 0)), _rows(tm, d, COL_GA), _rows(tm, d, COL_GB),
                  _rows(tm, d), _rows(tm, d)],
                 (_rows(tm, d),) * 4, ("parallel",))(do, w_out_t, proj, proj, pa, pb)


def _conv_pre(ext, cw_ref, cb_ref):
    return (cb_ref[...] + cw_ref[0:1, :] * pltpu.roll(ext, 2, axis=0) + cw_ref[1:2, :] * pltpu.roll(ext, 1, axis=0)
            + cw_ref[2:3, :] * ext)


def _conv_act(name, a, up, cw, cb, tm=512, tc=FFN_SHARD_PAD):
    s, c = a.shape
    tm = min(tm, s)
    per = tm // SUBLANES

    def body(a_ref, prev_ref, up_ref, cw_ref, cb_ref, h_ref):
        prev = jnp.where(pl.program_id(1) > 0, prev_ref[...], 0.0)
        ac = _conv_pre(jnp.concatenate([prev, a_ref[...]], axis=0), cw_ref, cb_ref)[SUBLANES:]
        h_ref[...] = (ac * _sigmoid(ac) * up_ref[...]).astype(BF)

    tile = pl.BlockSpec((tm, tc), lambda j, i: (i, j))
    prev = pl.BlockSpec((SUBLANES, tc), lambda j, i: (jnp.maximum(i * per - 1, 0), j))
    return _call(body, name, _sds((s, c), BF), (c // tc, s // tm),
                 [tile, prev, tile, pl.BlockSpec((3, tc), lambda j, i: (0, j)), pl.BlockSpec((1, tc), lambda j, i: (0, j))],
                 tile, ("parallel", "parallel"))(a, a, up, cw, cb)


def _conv_act_bwd(name, a, up, dhf, cw, cb, tm=512, tc=FFN_SHARD_PAD):
    s, c = a.shape
    tm = min(tm, s)
    per = tm // SUBLANES
    steps = s // tm
    last8 = s // SUBLANES - 1

    def body(a_ref, aprev_ref, anext_ref, up_ref, upnext_ref, dh_ref, dhnext_ref, cw_ref, cb_ref,
             da_ref, dup_ref, dcw_ref, dcb_ref):
        i = pl.program_id(1)

        @pl.when(i == 0)
        def _():
            dcw_ref[...] = jnp.zeros_like(dcw_ref)
            dcb_ref[...] = jnp.zeros_like(dcb_ref)

        prev = jnp.where(i > 0, aprev_ref[...], 0.0)
        ext = jnp.concatenate([prev, a_ref[...], anext_ref[...]], axis=0)
        a1 = pltpu.roll(ext, 1, axis=0)[SUBLANES:]
        a2 = pltpu.roll(ext, 2, axis=0)[SUBLANES:]
        a0 = ext[SUBLANES:]
        ac = cb_ref[...] + cw_ref[0:1, :] * a2 + cw_ref[1:2, :] * a1 + cw_ref[2:3, :] * a0
        sig = _sigmoid(ac)
        dh = jnp.concatenate([dh_ref[...], jnp.where(i < steps - 1, dhnext_ref[...], 0.0)], axis=0)
        upe = jnp.concatenate([up_ref[...], upnext_ref[...]], axis=0)
        dac = dh * upe * (sig * (1.0 + ac * (1.0 - sig)))
        dup_ref[...] = (dh[:tm] * (ac * sig)[:tm]).astype(BF)
        n = tm + SUBLANES
        da = (cw_ref[2:3, :] * dac + cw_ref[1:2, :] * pltpu.roll(dac, n - 1, axis=0)
              + cw_ref[0:1, :] * pltpu.roll(dac, n - 2, axis=0))
        da_ref[...] = da[:tm].astype(BF)
        dact = dac[:tm]
        dcb_ref[...] += jnp.sum(dact, axis=0, keepdims=True)
        dcw_ref[0:1, :] += jnp.sum(dact * a2[:tm], axis=0, keepdims=True)
        dcw_ref[1:2, :] += jnp.sum(dact * a1[:tm], axis=0, keepdims=True)
        dcw_ref[2:3, :] += jnp.sum(dact * a0[:tm], axis=0, keepdims=True)

    tile = pl.BlockSpec((tm, tc), lambda j, i: (i, j))
    prev = pl.BlockSpec((SUBLANES, tc), lambda j, i: (jnp.maximum(i * per - 1, 0), j))
    nxt = pl.BlockSpec((SUBLANES, tc), lambda j, i: (jnp.minimum((i + 1) * per, last8), j))
    cw_spec = pl.BlockSpec((3, tc), lambda j, i: (0, j))
    cb_spec = pl.BlockSpec((1, tc), lambda j, i: (0, j))
    return _call(body, name, (_sds((s, c), BF), _sds((s, c), BF), _sds((3, c), F32), _sds((1, c), F32)),
                 (c // tc, steps), [tile, prev, nxt, tile, nxt, tile, nxt, cw_spec, cb_spec],
                 (tile, tile, cw_spec, cb_spec), ("parallel", "arbitrary"))(a, a, a, up, up, dhf, dhf, cw, cb)


def _loss_head(name, x, w, target, tm=512):
    s, d = x.shape
    tm = min(tm, s)

    def body(x_ref, w_ref, t_ref, dx_ref, loss_ref, dw_ref):
        @pl.when(pl.program_id(0) == 0)
        def _():
            loss_ref[...] = jnp.zeros_like(loss_ref)
            dw_ref[...] = jnp.zeros_like(dw_ref)

        xv = x_ref[...]
        r = lax.rsqrt(jnp.mean(xv * xv, axis=-1, keepdims=True) + NORM_EPS)
        xn = xv * r
        err = xn * w_ref[...] - t_ref[...]
        loss_ref[...] += 0.5 * jnp.sum(jnp.mean(err * err, axis=-1, keepdims=True))
        dy = err * (1.0 / d)
        dw_ref[...] += jnp.sum(dy * xn, axis=0, keepdims=True)
        dxn = dy * w_ref[...]
        dx_ref[...] = r * (dxn - xn * jnp.mean(dxn * xn, axis=-1, keepdims=True))

    return _call(body, name, (_sds((s, d), F32), _sds((1, LANES), F32), _sds((1, d), F32)), (s // tm,),
                 [_rows(tm, d), _vec(d), _rows(tm, d)], (_rows(tm, d), _vec(LANES), _vec(d)), ("arbitrary",))(x, w, target)


def _layer_fwd(l, x, mod, p, cosf, sinf):
    sh1, sc1, g1, sh2, sc2, g2 = mod
    tag = f"l{l}_"
    h = _norm_mod(tag + "norm1", x, p["norm1_w"], sc1, sh1)
    proj = _matmul(tag + "in_proj", h, p["w_in"], tn=768)
    qr, kk0, kk1, vv0, vv1 = _rope_qkv(tag + "rope", proj, cosf, sinf)
    y_attn = _attention(tag + "attn", qr, kk0, kk1, vv0, vv1, p["sinks"])
    y_sgu = _sgu(tag + "sgu", proj, p["sgu_ln_w"], p["sgu_ln_b"], p["sgu_w"], p["sgu_bfull"])
    merged, pa, pb = _merge(tag + "merge", y_sgu, y_attn, p["proj_a"], p["proj_b"], proj)
    o = _matmul(tag + "out_proj", merged, p["w_out"])
    x1 = _residual(tag + "res1", x, o, g1)
    h2 = _norm_mod(tag + "norm2", x1, p["norm2_w"], sc2, sh2)
    a = _matmul(tag + "ffn_gate", h2, p["w_gate"])
    up = _matmul(tag + "ffn_up", h2, p["w_up"])
    hf = _conv_act(tag + "conv_act", a, up, p["conv_w"], p["conv_b"])
    dn = _matmul(tag + "ffn_down", hf, p["w_down"])
    x2 = _residual(tag + "res2", x1, dn, g2)
    saved = dict(x=x, h=h, proj=proj, qr=qr, kk0=kk0, kk1=kk1, vv0=vv0, vv1=vv1, y_attn=y_attn, y_sgu=y_sgu,
                 merged=merged, pa=pa, pb=pb, o=o, x1=x1, h2=h2, a=a, up=up, hf=hf, dn=dn)
    return x2, saved


def _layer_bwd(l, dx2, mod, p, sv, cosf, sinf):
    sh1, sc1, g1, sh2, sc2, g2 = mod
    tag = f"l{l}_b_"
    g = {}
    ddn, dg2 = _residual_bwd(tag + "res2", dx2, sv["dn"], g2)
    dhf = _matmul(tag + "d_hf", ddn, p["w_down_t"])
    g["w_down"] = _matmul_tn(tag + "dw_down", sv["hf"], ddn, tk=768)
    da, dup, g["conv_w"], g["conv_b"] = _conv_act_bwd(tag + "conv_act", sv["a"], sv["up"], dhf, p["conv_w"], p["conv_b"])
    dh2 = _matmul2(tag + "d_h2", da, p["w_gate_t"], dup, p["w_up_t"])
    g["w_gate"] = _matmul_tn(tag + "dw_gate", sv["h2"], da, tn=768)
    g["w_up"] = _matmul_tn(tag + "dw_up", sv["h2"], dup, tn=768)
    dx1, da2, dsh2 = _norm_mod_bwd(tag + "norm2", sv["x1"], dh2, p["norm2_w"], sc2, dx2)
    g["norm2_w"] = da2 * (1.0 + sc2)
    dsc2 = da2 * p["norm2_w"]
    do, dg1 = _residual_bwd(tag + "res1", dx1, sv["o"], g1)
    g["w_out"] = _matmul_tn(tag + "dw_out", sv["merged"], do)
    dpa, dpb, dga, dgb = _merge_bwd(tag + "merge", do, p["w_out_t"], sv["proj"], sv["pa"], sv["pb"])
    g["proj_a"] = _matmul_tn(tag + "dproj_a", sv["y_sgu"], dpa)
    g["proj_b"] = _matmul_tn(tag + "dproj_b", sv["y_attn"], dpb)
    dy_sgu = _matmul(tag + "d_ysgu", dpa, p["proj_a_t"])
    dy_attn = _matmul(tag + "d_yattn", dpb, p["proj_b_t"])
    du, dv, g["sgu_w"], g["sgu_b"], g["sgu_ln_w"], g["sgu_ln_b"] = _sgu_bwd(
        tag + "sgu", sv["proj"], p["sgu_ln_w"], p["sgu_ln_b"], p["sgu_w"], p["sgu_bfull"], dy_sgu)
    dqr, dkv_cur, dkv_prev, dsink = _attention_bwd(tag + "attn", sv["qr"], sv["kk0"], sv["kk1"], sv["vv0"], sv["vv1"],
                                                   p["sinks"], dy_attn)
    g["sinks"] = dsink[0, :N_Q_HEADS]
    dq, dkv = _rope_bwd(tag + "rope", dqr, dkv_cur, dkv_prev, cosf, sinf)
    dproj = jnp.concatenate([dq, du, dv, dga, dgb, dkv], axis=1)
    dh = _matmul(tag + "d_h", dproj, p["w_in_t"])
    g["w_in"] = _matmul_tn(tag + "dw_in", sv["h"], dproj, tn=768)
    dx, da1, dsh1 = _norm_mod_bwd(tag + "norm1", sv["x"], dh, p["norm1_w"], sc1, dx1)
    g["norm1_w"] = da1 * (1.0 + sc1)
    dsc1 = da1 * p["norm1_w"]
    return dx, (dsh1, dsc1, dg1, dsh2, dsc2, dg2), g


def _pad_to(a, axis, size):
    pad = [(0, 0)] * a.ndim
    pad[axis] = (0, size - a.shape[axis])
    return jnp.pad(a, pad)


def _layer_params(w_in, proj_a, proj_b, w_out, w_gate, w_up, w_down, conv_w, small):
    d = D_MODEL
    nat = w_in.transpose(1, 0, 2).reshape(d, IN_COLS)
    w_in_p = jnp.concatenate([nat[:, a:b] for a, b in _IN_SEGMENTS], axis=1)
    cat = lambda w: w.reshape(d, d)
    hid = lambda w: _pad_to(w, 2, FFN_SHARD_PAD).transpose(1, 0, 2).reshape(w.shape[1], FFN_PAD)
    w_gate_p, w_up_p = hid(w_gate), hid(w_up)
    w_down_p = _pad_to(w_down, 1, FFN_SHARD_PAD).reshape(FFN_PAD, d)
    conv_w_p = _pad_to(conv_w, 2, FFN_SHARD_PAD).transpose(1, 0, 2).reshape(3, FFN_PAD)
    conv_b_p = _pad_to(small["conv_b"].reshape(N_CHIPS, FFN_SHARD), 1, FFN_SHARD_PAD).reshape(1, FFN_PAD)
    p = dict(w_in=w_in_p, w_in_t=w_in_p.T, proj_a=cat(proj_a), proj_a_t=cat(proj_a).T, proj_b=cat(proj_b),
             proj_b_t=cat(proj_b).T, w_out=cat(w_out), w_out_t=cat(w_out).T, w_gate=w_gate_p, w_gate_t=w_gate_p.T,
             w_up=w_up_p, w_up_t=w_up_p.T, w_down=w_down_p, w_down_t=w_down_p.T, conv_w=conv_w_p, conv_b=conv_b_p,
             norm1_w=small["norm1_w"].reshape(1, d), norm2_w=small["norm2_w"].reshape(1, d), sinks=small["sinks"],
             sgu_ln_w=small["sgu_ln_w"].reshape(1, d), sgu_ln_b=small["sgu_ln_b"].reshape(1, d), sgu_w=small["sgu_w"],
             sgu_bfull=jnp.broadcast_to(small["sgu_b"][:, :, None], (SGU_GROUPS, SGU_CHUNK, SGU_CHUNK)))
    return p


def _grad_blocks(g):
    d = D_MODEL
    w_in_p = g["w_in"]
    nat_cols = [None] * 7
    off = 0
    for idx, (a, b) in enumerate(_IN_SEGMENTS):
        nat_cols[idx] = (a, w_in_p[:, off:off + (b - a)])
        off += b - a
    nat = jnp.concatenate([c for _, c in sorted(nat_cols, key=lambda t: t[0])], axis=1)
    unhid = lambda w: w.reshape(d, N_CHIPS, FFN_SHARD_PAD).transpose(1, 0, 2)[:, :, :FFN_SHARD]
    return dict(
        w_in=nat.reshape(d, N_CHIPS, IN_COLS // N_CHIPS).transpose(1, 0, 2),
        proj_a=g["proj_a"].reshape(N_CHIPS, d // N_CHIPS, d), proj_b=g["proj_b"].reshape(N_CHIPS, d // N_CHIPS, d),
        w_out=g["w_out"].reshape(N_CHIPS, d // N_CHIPS, d), w_gate=unhid(g["w_gate"]), w_up=unhid(g["w_up"]),
        w_down=g["w_down"].reshape(N_CHIPS, FFN_SHARD_PAD, d)[:, :FFN_SHARD],
    )


def _conv_grads_natural(g):
    cw = g["conv_w"].reshape(3, N_CHIPS, FFN_SHARD_PAD)[:, :, :FFN_SHARD].reshape(3, FFN_DIM)
    cb = g["conv_b"].reshape(N_CHIPS, FFN_SHARD_PAD)[:, :FFN_SHARD].reshape(FFN_DIM)
    return cw, cb


def _rope_tables(positions):
    inv_freq = ROPE_THETA ** (-jnp.arange(0, ROT_DIM, 2, dtype=F32) / ROT_DIM)
    ang = positions.astype(F32)[:, None] * inv_freq
    cos, sin = jnp.cos(ang), jnp.sin(ang)
    s = positions.shape[0]
    rest = HEAD_DIM - ROT_DIM
    cos_head = jnp.concatenate([cos, cos, jnp.ones((s, rest), F32)], axis=1)
    sin_head = jnp.concatenate([-sin, sin, jnp.zeros((s, rest), F32)], axis=1)
    return jnp.tile(cos_head, (1, LANES // HEAD_DIM)), jnp.tile(sin_head, (1, LANES // HEAD_DIM))


ADA_ROWS = 16


def _ada_fwd(name, c_rows, ada_w, ada_b_cols, tn=512):
    depth, d, n = ada_w.shape

    def body(c_ref, w_ref, b_ref, o_ref):
        cv = c_ref[...]
        act = (cv * _sigmoid(cv)).astype(BF)
        o_ref[0] = _dot(act, w_ref[0].astype(BF)) + b_ref[0]

    return _call(body, name, _sds((depth, ADA_ROWS, n), F32), (depth, n // tn),
                 [pl.BlockSpec((ADA_ROWS, d), lambda l, j: (0, 0)), pl.BlockSpec((1, d, tn), lambda l, j: (l, 0, j)),
                  pl.BlockSpec((1, 1, tn), lambda l, j: (l, 0, j))],
                 pl.BlockSpec((1, ADA_ROWS, tn), lambda l, j: (l, 0, j)), ("parallel", "parallel"))(c_rows, ada_w, ada_b_cols)


def _ada_bwd(name, c_rows, dmod_cols, tn=512):
    depth, _, n = dmod_cols.shape
    d = c_rows.shape[1]

    def body(c_ref, dm_ref, o_ref):
        cv = c_ref[...]
        act = (cv * _sigmoid(cv)).astype(BF)
        o_ref[0] = _dot_tn(act, dm_ref[0].astype(BF))

    return _call(body, name, _sds((depth, d, n), F32), (depth, n // tn),
                 [pl.BlockSpec((ADA_ROWS, d), lambda l, j: (0, 0)), pl.BlockSpec((1, ADA_ROWS, tn), lambda l, j: (l, 0, j))],
                 pl.BlockSpec((1, d, tn), lambda l, j: (l, 0, j)), ("parallel", "parallel"))(c_rows, dmod_cols)


def _colsum(name, a):
    r, n = a.shape

    def body(a_ref, o_ref):
        o_ref[...] = jnp.sum(a_ref[...], axis=0, keepdims=True)

    return _call(body, name, _sds((1, n), F32), (1,), [pl.BlockSpec((r, n), lambda i: (0, 0))],
                 pl.BlockSpec((1, n), lambda i: (0, 0)), ("arbitrary",))(a)


REL_SIBLING = (0, 0, 1)
REL_CHIPS = ((1, 0, 0), (0, 1, 0), (1, 1, 0))
REL_ALL = tuple((fx, fy, fc) for fx in (0, 1) for fy in (0, 1) for fc in (0, 1) if fx or fy or fc)


def _chip_of(dev):
    return 2 * dev[0] + dev[1]


def _dev_of(dev):
    return 4 * dev[0] + 2 * dev[1] + dev[2]


def _exchange(name, srcs, dsts, plan, local):
    ns, nd = len(srcs), len(dsts)

    def at(ref, idx):
        return ref.at[idx] if len(idx) else ref

    def body(*refs):
        src_refs, dst_refs = refs[:ns], refs[ns:ns + nd]
        send_sems, recv_sems, local_sems = refs[ns + nd:]
        me = (lax.axis_index("x"), lax.axis_index("y"), lax.axis_index("c"))

        def flip(rel):
            return tuple(1 - m if f else m for m, f in zip(me, rel))

        def remote(k, sender, receiver):
            rel, si, ssel, di, dsel = plan[k]
            return pltpu.make_async_remote_copy(
                src_ref=at(src_refs[si], ssel(sender, receiver)), dst_ref=at(dst_refs[di], dsel(sender, receiver)),
                send_sem=send_sems.at[k], recv_sem=recv_sems.at[k], device_id=flip(rel), device_id_type=MESH)

        sends = [remote(k, me, flip(plan[k][0])) for k in range(len(plan))]
        for cp in sends:
            cp.start()
        locs = []
        for k, (si, ssel, di, dsel) in enumerate(local):
            cp = pltpu.make_async_copy(at(src_refs[si], ssel(me)), at(dst_refs[di], dsel(me)), local_sems.at[k])
            cp.start()
            locs.append(cp)
        for k in range(len(plan)):
            remote(k, flip(plan[k][0]), me).wait_recv()
        for cp in sends:
            cp.wait_send()
        for cp in locs:
            cp.wait()

    any_spec = pl.BlockSpec(memory_space=pl.ANY)
    return pl.pallas_call(
        body, name=name, out_shape=tuple(dsts), in_specs=[any_spec] * ns, out_specs=tuple([any_spec] * nd),
        scratch_shapes=[pltpu.SemaphoreType.DMA((len(plan),)), pltpu.SemaphoreType.DMA((len(plan),)),
                        pltpu.SemaphoreType.DMA((max(len(local), 1),))])(*srcs)


def _whole(*_):
    return ()


def _gather_chips(name, arrays):
    n = len(arrays)
    dsts = [_sds((N_CHIPS,) + a.shape, a.dtype) for a in arrays]
    by_sender = lambda sender, receiver: (_chip_of(sender),)
    plan = [(rel, t, _whole, t, by_sender) for t in range(n) for rel in REL_CHIPS]
    local = [(t, _whole, t, lambda me: (_chip_of(me),)) for t in range(n)]
    return _exchange(name, arrays, dsts, plan, local)


def _gather_all(name, a):
    by_sender = lambda sender, receiver: (_dev_of(sender),)
    plan = [(rel, 0, _whole, 0, by_sender) for rel in REL_ALL]
    local = [(0, _whole, 0, lambda me: (_dev_of(me),))]
    return _exchange(name, [a], [_sds((2 * N_CHIPS,) + a.shape, a.dtype)], plan, local)[0]


def _swap_layers(name, grads):
    n = len(grads)
    dsts = [_sds(g.shape[1:], g.dtype) for g in grads]
    plan = [(REL_SIBLING, t, lambda sender, receiver: (receiver[2],), t, _whole) for t in range(n)]
    return _exchange(name, grads, dsts, plan, [])


def _scatter_chips(name, sums):
    n = len(sums)
    dsts = [_sds(a.shape, a.dtype) for a in sums]
    plan = [(rel, t, lambda sender, receiver: (_chip_of(receiver),), t, lambda sender, receiver: (_chip_of(sender),))
            for t in range(n) for rel in REL_CHIPS]
    local = [(t, lambda me: (_chip_of(me),), t, lambda me: (_chip_of(me),)) for t in range(n)]
    return _exchange(name, sums, dsts, plan, local)


def _gather_layers(name, totals):
    n = len(totals)
    dsts = [_sds((DEPTH,) + a.shape, a.dtype) for a in totals]
    plan = [(REL_SIBLING, t, _whole, t, lambda sender, receiver: (sender[2],)) for t in range(n)]
    local = [(t, _whole, t, lambda me: (me[2],)) for t in range(n)]
    return _exchange(name, totals, dsts, plan, local)


def _add_layer_halves(name, g, recv, core, tr=256):
    _, nch, r, c = g.shape
    rows = nch * r

    def body(core_ref, g_ref, r_ref, o_ref):
        o_ref[...] = (g_ref[0].astype(F32) + r_ref[...].astype(F32)).astype(o_ref.dtype)

    spec = pltpu.PrefetchScalarGridSpec(
        num_scalar_prefetch=1, grid=(rows // tr,),
        in_specs=[pl.BlockSpec((1, tr, c), lambda i, core_ref: (core_ref[0], i, 0)),
                  pl.BlockSpec((tr, c), lambda i, core_ref: (i, 0))],
        out_specs=pl.BlockSpec((tr, c), lambda i, core_ref: (i, 0)))
    out = pl.pallas_call(body, name=name, out_shape=_sds((rows, c), recv.dtype), grid_spec=spec,
                         compiler_params=pltpu.CompilerParams(dimension_semantics=("parallel",),
                                                              vmem_limit_bytes=VMEM_LIMIT))(
                                                                  core, g.reshape(2, rows, c), recv.reshape(rows, c))
    return out.reshape(nch, r, c)


def _sum_chips(name, a):
    nch, r, c = a.shape
    tr = 256 if r % 256 == 0 else 64

    def body(a_ref, o_ref):
        acc = a_ref[0].astype(F32)
        for k in range(1, nch):
            acc = acc + a_ref[k].astype(F32)
        o_ref[...] = acc

    return _call(body, name, _sds((r, c), F32), (r // tr,), [pl.BlockSpec((nch, tr, c), lambda i: (0, i, 0))],
                 pl.BlockSpec((tr, c), lambda i: (i, 0)), ("parallel",))(a)


def _adamw(name, w, g, m, v):
    shape = w.shape
    c = shape[-1]
    rows = w.size // c
    tr = next(t for t in (256, 128, 64, 32, 16, 8, rows) if rows % t == 0)
    flat = lambda a: a.reshape(rows, c)

    def body(w_ref, g_ref, m_ref, v_ref, d_ref, mo_ref, vo_ref):
        gv = g_ref[...]
        mn = ADAM_B1 * m_ref[...] + (1.0 - ADAM_B1) * gv
        vn = ADAM_B2 * v_ref[...] + (1.0 - ADAM_B2) * (gv * gv)
        m_hat = mn / (1.0 - ADAM_B1 ** ADAM_STEP)
        v_hat = vn / (1.0 - ADAM_B2 ** ADAM_STEP)
        d_ref[...] = -ADAM_LR * (m_hat / (jnp.sqrt(v_hat) + ADAM_EPS) + ADAM_WD * w_ref[...])
        mo_ref[...] = mn
        vo_ref[...] = vn

    spec = pl.BlockSpec((tr, c), lambda i: (i, 0))
    o = _sds((rows, c), F32)
    d, mn, vn = _call(body, name, (o, o, o), (rows // tr,), [spec] * 4, (spec,) * 3, ("parallel",))(
        flat(w), flat(g), flat(m), flat(v))
    return d.reshape(shape), mn.reshape(shape), vn.reshape(shape)


def _pack(arrays, rows):
    flat = jnp.concatenate([a.reshape(-1).astype(F32) for a in arrays])
    return _pad_to(flat, 0, rows * LANES).reshape(rows, LANES)


def _unpack(packed, shapes):
    flat = packed.reshape(-1)
    out, off = [], 0
    for shp in shapes:
        n = 1
        for s_ in shp:
            n *= s_
        out.append(flat[off:off + n].reshape(shp))
        off += n
    return out


_MATRICES = ("w_in", "proj_a", "proj_b", "w_out", "w_gate", "w_up", "w_down")
_SMALL = (("norm1_w", (D_MODEL,)), ("sinks", (N_Q_HEADS,)), ("sgu_ln_w", (SGU_WIDTH,)), ("sgu_ln_b", (SGU_WIDTH,)),
          ("sgu_w", (SGU_GROUPS, SGU_CHUNK, SGU_CHUNK)), ("sgu_b", (SGU_GROUPS, SGU_CHUNK)), ("norm2_w", (D_MODEL,)),
          ("conv_w", (3, FFN_DIM)), ("conv_b", (FFN_DIM,)))
SMALL_ROWS = 320
ADAM_PACK_ROWS = 256


def kernel(x, c, positions, ada_w, ada_b, norm1_w, w_in, attn_sinks, sgu_ln_w, sgu_ln_b, sgu_w, sgu_b, proj_a, proj_b, w_out, norm2_w, ffn_w_gate, ffn_w_up, ffn_conv_w, ffn_conv_b, ffn_w_down, final_norm_w, loss_target, m_ada_w, m_ada_b, m_norm1_w, m_w_in, m_attn_sinks, m_sgu_ln_w, m_sgu_ln_b, m_sgu_w, m_sgu_b, m_proj_a, m_proj_b, m_w_out, m_norm2_w, m_ffn_w_gate, m_ffn_w_up, m_ffn_conv_w, m_ffn_conv_b, m_ffn_w_down, m_final_norm_w, v_ada_w, v_ada_b, v_norm1_w, v_w_in, v_attn_sinks, v_sgu_ln_w, v_sgu_ln_b, v_sgu_w, v_sgu_b, v_proj_a, v_proj_b, v_w_out, v_norm2_w, v_ffn_w_gate, v_ffn_w_up, v_ffn_conv_w, v_ffn_conv_b, v_ffn_w_down, v_final_norm_w):
    d = D_MODEL
    ax, ay, ac = lax.axis_index("x"), lax.axis_index("y"), lax.axis_index("c")
    chip = 2 * ax + ay
    dev = 4 * ax + 2 * ay + ac
    core = ac.astype(jnp.int32).reshape(1)

    shards = [w.astype(BF) for w in (w_in, proj_a, proj_b, w_out, ffn_w_gate, ffn_w_up, ffn_w_down)]
    gathered = _gather_chips("gather_weights", shards + [ffn_conv_w])

    c_all = _gather_all("gather_cond", jnp.broadcast_to(c, (SUBLANES, d)))[:, 0, :]
    c_rows = _pad_to(c_all, 0, ADA_ROWS)
    ada_cols = ada_w.shape[2]
    ada_b_cols = lax.dynamic_slice_in_dim(ada_b, chip * ada_cols, ada_cols, axis=1).reshape(DEPTH, 1, ada_cols)
    mod_cols = _ada_fwd("ada_fwd", c_rows, ada_w, ada_b_cols)
    mod_all = _gather_chips("gather_mod", [mod_cols])[0]
    mod_mine = lax.dynamic_index_in_dim(mod_all, dev, axis=2, keepdims=False)
    mod_mine = mod_mine.transpose(1, 0, 2).reshape(DEPTH, 1, 6 * d)
    mods = [tuple(jnp.split(mod_mine[l], 6, axis=-1)) for l in range(DEPTH)]

    small_in = dict(norm1_w=norm1_w, sinks=attn_sinks, sgu_ln_w=sgu_ln_w, sgu_ln_b=sgu_ln_b, sgu_w=sgu_w, sgu_b=sgu_b,
                    norm2_w=norm2_w, conv_b=ffn_conv_b)
    params = [_layer_params(*[g[:, l] for g in gathered], {k: v[l] for k, v in small_in.items()}) for l in range(DEPTH)]

    cosf, sinf = _rope_tables(positions[0])
    h = x[0]
    saved = []
    for l in range(DEPTH):
        h, sv = _layer_fwd(l, h, mods[l], params[l], cosf, sinf)
        saved.append(sv)
    dx, loss_part, d_final = _loss_head("loss_head", h, final_norm_w.reshape(1, d), loss_target[0])
    grads, dmods = [None] * DEPTH, [None] * DEPTH
    for l in reversed(range(DEPTH)):
        dx, dmods[l], grads[l] = _layer_bwd(l, dx, mods[l], params[l], saved[l], cosf, sinf)
    loss = lax.psum(loss_part[0, 0], ("x", "y", "c"))

    dmod_mine = jnp.concatenate([jnp.concatenate(dmods[l], axis=1) for l in range(DEPTH)], axis=1)
    dmod_all = _gather_all("gather_dmod", jnp.broadcast_to(dmod_mine, (SUBLANES, DEPTH * 6 * d)))[:, 0, :]
    g_ada_b = _colsum("ada_b_grad", dmod_all).reshape(DEPTH, 6 * d)
    dmod_cols = jnp.stack([lax.dynamic_slice_in_dim(dmod_all, l * 6 * d + chip * ada_cols, ada_cols, axis=1)
                           for l in range(DEPTH)])
    g_ada_w = _ada_bwd("ada_w_grad", c_rows, _pad_to(dmod_cols, 1, ADA_ROWS))

    blocks = [_grad_blocks(grads[l]) for l in range(DEPTH)]
    small_list = []
    for l in range(DEPTH):
        cw, cb = _conv_grads_natural(grads[l])
        nat = dict(grads[l], conv_w=cw, conv_b=cb)
        small_list += [nat[k] for k, _ in _SMALL]
    small_list.append(d_final)
    partial = [jnp.stack([blocks[l][k] for l in range(DEPTH)]) for k in _MATRICES]
    partial.append(_pack(small_list, DEPTH * N_CHIPS * SMALL_ROWS).reshape(DEPTH, N_CHIPS, SMALL_ROWS, LANES))
    from_sibling = _swap_layers("reduce_cores", partial)
    chip_sums = [_add_layer_halves(f"reduce_cores_add{t}", partial[t], from_sibling[t], core) for t in range(len(partial))]
    from_chips = _scatter_chips("reduce_chips", chip_sums)
    totals = [_sum_chips(f"reduce_chips_add{t}", from_chips[t]) for t in range(len(partial))]
    reduced = _gather_layers("gather_cores", totals)
    small_all = _gather_chips("gather_small", [reduced[-1]])[0]
    small_shapes = [shp for _ in range(DEPTH) for _, shp in _SMALL] + [(d,)]
    small_g = _unpack(small_all.transpose(1, 0, 2, 3), small_shapes)
    per_layer = len(_SMALL)
    sg = {k: jnp.stack([small_g[l * per_layer + i] for l in range(DEPTH)]) for i, (k, _) in enumerate(_SMALL)}
    g_final = small_g[-1]
    g_conv_w = lax.dynamic_slice_in_dim(sg["conv_w"], chip * FFN_SHARD, FFN_SHARD, axis=2)

    g_big = dict(zip(_MATRICES, reduced[:-1]))
    big = dict(w_in=(w_in, m_w_in, v_w_in), proj_a=(proj_a, m_proj_a, v_proj_a), proj_b=(proj_b, m_proj_b, v_proj_b),
               w_out=(w_out, m_w_out, v_w_out), w_gate=(ffn_w_gate, m_ffn_w_gate, v_ffn_w_gate),
               w_up=(ffn_w_up, m_ffn_w_up, v_ffn_w_up), w_down=(ffn_w_down, m_ffn_w_down, v_ffn_w_down))
    upd = {k: _adamw("adamw_" + k, w, g_big[k], m, v) for k, (w, m, v) in big.items()}
    upd["ada_w"] = _adamw("adamw_ada_w", ada_w, g_ada_w, m_ada_w, v_ada_w)
    g_big["ada_w"] = g_ada_w
    rest = [("ada_b", ada_b, g_ada_b, m_ada_b, v_ada_b), ("norm1_w", norm1_w, sg["norm1_w"], m_norm1_w, v_norm1_w),
            ("attn_sinks", attn_sinks, sg["sinks"], m_attn_sinks, v_attn_sinks),
            ("sgu_ln_w", sgu_ln_w, sg["sgu_ln_w"], m_sgu_ln_w, v_sgu_ln_w),
            ("sgu_ln_b", sgu_ln_b, sg["sgu_ln_b"], m_sgu_ln_b, v_sgu_ln_b), ("sgu_w", sgu_w, sg["sgu_w"], m_sgu_w, v_sgu_w),
            ("sgu_b", sgu_b, sg["sgu_b"], m_sgu_b, v_sgu_b), ("norm2_w", norm2_w, sg["norm2_w"], m_norm2_w, v_norm2_w),
            ("ffn_conv_w", ffn_conv_w, g_conv_w, m_ffn_conv_w, v_ffn_conv_w),
            ("ffn_conv_b", ffn_conv_b, sg["conv_b"], m_ffn_conv_b, v_ffn_conv_b),
            ("final_norm_w", final_norm_w, g_final, m_final_norm_w, v_final_norm_w)]
    rest_shapes = [r[1].shape for r in rest]
    rest_rows = -(-sum(r[1].size for r in rest) // (LANES * ADAM_PACK_ROWS)) * ADAM_PACK_ROWS
    packs = [_pack([r[i] for r in rest], rest_rows) for i in (1, 2, 3, 4)]
    rest_out = [_unpack(a, rest_shapes) for a in _adamw("adamw_rest", *packs)]
    g_rest = {r[0]: r[2] for r in rest}
    u_rest = {r[0]: tuple(o[i] for o in rest_out) for i, r in enumerate(rest)}

    names = ("ada_w", "ada_b", "norm1_w", "w_in", "attn_sinks", "sgu_ln_w", "sgu_ln_b", "sgu_w", "sgu_b", "proj_a", "proj_b",
             "w_out", "norm2_w", "ffn_w_gate", "ffn_w_up", "ffn_conv_w", "ffn_conv_b", "ffn_w_down", "final_norm_w")
    alias = {"ffn_w_gate": "w_gate", "ffn_w_up": "w_up", "ffn_w_down": "w_down"}
    grad_of = lambda n: g_rest[n] if n in g_rest else g_big[alias.get(n, n)]
    upd_of = lambda n: u_rest[n] if n in u_rest else upd[alias.get(n, n)]
    return (loss, dx[None], *[grad_of(n) for n in names], *[upd_of(n)[0] for n in names],
            *[upd_of(n)[1] for n in names], *[upd_of(n)[2] for n in names])
```

```python
import jax
import jax.numpy as jnp
from jax import lax
from jax.experimental import pallas as pl
from jax.experimental.pallas import tpu as pltpu

F32 = jnp.float32
BF = jnp.bfloat16

D_MODEL = 1024
N_Q_HEADS = 16
N_KV_HEADS = 2
HEAD_DIM = 64
ATTN_BLOCK = 128
ROPE_THETA = 500000.0
ROT_DIM = HEAD_DIM // 4
SGU_WIDTH = 1024
SGU_GROUPS = 8
SGU_CHUNK = 128
FFN_DIM = 2816
NORM_EPS = 1e-6
DEPTH = 2
IN_COLS = 5376
N_CHIPS = 4
FFN_SHARD = FFN_DIM // N_CHIPS
FFN_SHARD_PAD = 768
FFN_PAD = N_CHIPS * FFN_SHARD_PAD
LANES = 128
SUBLANES = 8
VMEM_LIMIT = 56 * 1024 * 1024
NEG_BIG = -1e30

ADAM_LR = 0.001
ADAM_B1 = 0.9
ADAM_B2 = 0.999
ADAM_EPS = 1e-08
ADAM_WD = 0.01
ADAM_STEP = 10

MESH = pl.DeviceIdType.MESH

Q_END = 1024
KV_END = 1280
U_END = 2304
Z_END = 3328
GA_END = 4352


def _sds(shape, dtype):
    return jax.ShapeDtypeStruct(tuple(shape), dtype)


def _call(body, name, out_shape, grid, in_specs, out_specs, semantics, scratch=()):
    return pl.pallas_call(
        body, name=name, out_shape=out_shape, grid=grid, in_specs=in_specs, out_specs=out_specs,
        scratch_shapes=scratch,
        compiler_params=pltpu.CompilerParams(dimension_semantics=semantics, vmem_limit_bytes=VMEM_LIMIT))


def _rows(tm, width, col=0):
    return pl.BlockSpec((tm, width), lambda i: (i, col))


def _vec(width):
    return pl.BlockSpec((1, width), lambda i: (0, 0))


def _resident(shape):
    zeros = (0,) * len(shape)
    return pl.BlockSpec(tuple(shape), lambda *_: zeros, pipeline_mode=pl.Buffered(1))


def _sigmoid(x):
    return 1.0 / (1.0 + jnp.exp(-x))


def _gelu(x):
    return 0.5 * x * (1.0 + lax.erf(x * 0.7071067811865476))


def _gelu_grad(x):
    cdf = 0.5 * (1.0 + lax.erf(x * 0.7071067811865476))
    return cdf + x * jnp.exp(-0.5 * x * x) * 0.3989422804014327


def _dot(a, b):
    return jnp.dot(a, b, preferred_element_type=F32)


def _dot_nt(a, b):
    return lax.dot_general(a, b, (((1,), (1,)), ((), ())), preferred_element_type=F32)


def _dot_tn(a, b):
    return lax.dot_general(a, b, (((0,), (0,)), ((), ())), preferred_element_type=F32)


def _rms(xv):
    return lax.rsqrt(jnp.mean(xv * xv, axis=-1, keepdims=True) + NORM_EPS)


def _matmul_tn(name, a, b, tk=512, tn=1024, blocked=False):
    s, k = a.shape
    n = b.shape[1]
    tk, tn = min(tk, k), min(tn, n)

    def body(a_ref, b_ref, o_ref):
        res = _dot_tn(a_ref[...], b_ref[...]).astype(o_ref.dtype)
        if blocked:
            o_ref[0] = res
        else:
            o_ref[...] = res

    if blocked:
        out, ospec = _sds((n // tn, k, tn), BF), pl.BlockSpec((1, tk, tn), lambda i, j: (j, i, 0))
    else:
        out, ospec = _sds((k, n), BF), pl.BlockSpec((tk, tn), lambda i, j: (i, j))
    return _call(body, name, out, (k // tk, n // tn),
                 [pl.BlockSpec((s, tk), lambda i, j: (0, i)), pl.BlockSpec((s, tn), lambda i, j: (0, j))],
                 ospec, ("parallel", "parallel"))(a, b)


def _rope_partner(v):
    lane = lax.broadcasted_iota(jnp.int32, (1, LANES), 1) % HEAD_DIM
    return jnp.where(lane < ROT_DIM // 2, pltpu.roll(v, LANES - ROT_DIM // 2, axis=1), pltpu.roll(v, ROT_DIM // 2, axis=1))


def _dup_half(v, half):
    lane = lax.broadcasted_iota(jnp.int32, (1, LANES), 1)
    keep = jnp.where((lane >= HEAD_DIM) == (half == 1), v, 0.0)
    return keep + pltpu.roll(keep, HEAD_DIM, axis=1)


def _in_proj(name, x, w, sc, sh, w_in, cosf, sinf, tm=256):
    s, d = x.shape
    tm = min(tm, s)

    def body(x_ref, w_ref, sc_ref, sh_ref, win_ref, cos_ref, sin_ref,
             h_ref, qr_ref, kk0_ref, kk1_ref, vv0_ref, vv1_ref, u_ref, v_ref, ga_ref, gb_ref):
        xv = x_ref[...]
        h = ((xv * _rms(xv)) * w_ref[...] * (1.0 + sc_ref[...]) + sh_ref[...]).astype(BF)
        h_ref[...] = h
        cosv, sinv = cos_ref[...], sin_ref[...]
        q = _dot(h, win_ref[:, :Q_END])
        for j in range(D_MODEL // LANES):
            qv = q[:, j * LANES:(j + 1) * LANES]
            qr_ref[:, j * LANES:(j + 1) * LANES] = (qv * cosv + _rope_partner(qv) * sinv).astype(BF)
        kv = _dot(h, win_ref[:, Q_END:KV_END])
        kr = kv[:, :LANES] * cosv + _rope_partner(kv[:, :LANES]) * sinv
        vv = kv[:, LANES:]
        kk0_ref[...] = _dup_half(kr, 0).astype(BF)
        kk1_ref[...] = _dup_half(kr, 1).astype(BF)
        vv0_ref[...] = _dup_half(vv, 0).astype(BF)
        vv1_ref[...] = _dup_half(vv, 1).astype(BF)
        u_ref[...] = _dot(h, win_ref[:, KV_END:U_END])
        v_ref[...] = _dot(h, win_ref[:, U_END:Z_END])
        ga_ref[...] = _dot(h, win_ref[:, Z_END:GA_END])
        gb_ref[...] = _dot(h, win_ref[:, GA_END:])

    wide, kvs, pre = _sds((s, d), BF), _sds((s, LANES), BF), _sds((s, d), F32)
    return _call(body, name, (wide, wide, kvs, kvs, kvs, kvs, pre, pre, pre, pre), (s // tm,),
                 [_rows(tm, d), _vec(d), _vec(d), _vec(d), _resident(w_in.shape), _rows(tm, LANES), _rows(tm, LANES)],
                 (_rows(tm, d), _rows(tm, d)) + (_rows(tm, LANES),) * 4 + (_rows(tm, d),) * 4, ("parallel",))(
                     x, w, sc, sh, w_in, cosf, sinf)


def _in_proj_bwd(name, dq, dkv, du, dv, dga, dgb, w_in, x, w, sc, dx_in, tm=256):
    s, d = x.shape
    tm = min(tm, s)

    def body(dq_ref, dkv_ref, du_ref, dv_ref, dga_ref, dgb_ref, win_ref, x_ref, w_ref, sc_ref, dxin_ref,
             dx_ref, da_ref, dsh_ref):
        @pl.when(pl.program_id(0) == 0)
        def _():
            da_ref[...] = jnp.zeros_like(da_ref)
            dsh_ref[...] = jnp.zeros_like(dsh_ref)

        dh = (_dot_nt(dq_ref[...], win_ref[:, :Q_END]) + _dot_nt(dkv_ref[...], win_ref[:, Q_END:KV_END])
              + _dot_nt(du_ref[...], win_ref[:, KV_END:U_END]) + _dot_nt(dv_ref[...], win_ref[:, U_END:Z_END])
              + _dot_nt(dga_ref[...], win_ref[:, Z_END:GA_END]) + _dot_nt(dgb_ref[...], win_ref[:, GA_END:]))
        xv = x_ref[...]
        r = _rms(xv)
        xn = xv * r
        dxn = dh * (w_ref[...] * (1.0 + sc_ref[...]))
        dx_ref[...] = dxin_ref[...] + r * (dxn - xn * jnp.mean(dxn * xn, axis=-1, keepdims=True))
        da_ref[...] += jnp.sum(dh * xn, axis=0, keepdims=True)
        dsh_ref[...] += jnp.sum(dh, axis=0, keepdims=True)

    return _call(body, name, (_sds((s, d), F32), _sds((1, d), F32), _sds((1, d), F32)), (s // tm,),
                 [_rows(tm, d), _rows(tm, 2 * LANES), _rows(tm, d), _rows(tm, d), _rows(tm, d), _rows(tm, d),
                  _resident(w_in.shape), _rows(tm, d), _vec(d), _vec(d), _rows(tm, d)],
                 (_rows(tm, d), _vec(d), _vec(d)), ("arbitrary",))(dq, dkv, du, dv, dga, dgb, w_in, x, w, sc, dx_in)


def _rope_bwd(name, dqr, dkv_cur, dkv_prev, cosf, sinf, tm=512):
    s = dqr.shape[0]
    tm = min(tm, s)
    steps = s // tm
    per = tm // ATTN_BLOCK
    nb = s // ATTN_BLOCK

    def unrope(v, cosv, sinv):
        return v * cosv - _rope_partner(v) * sinv

    def body(dq_ref, cur_ref, prev_ref, next_ref, cos_ref, sin_ref, dqo_ref, dkvo_ref):
        i = pl.program_id(0)
        cosv, sinv = cos_ref[...], sin_ref[...]
        for j in range(D_MODEL // LANES):
            dqo_ref[:, j * LANES:(j + 1) * LANES] = unrope(dq_ref[:, j * LANES:(j + 1) * LANES], cosv, sinv).astype(BF)
        nxt = jnp.where(i < steps - 1, next_ref[...], 0.0)
        if per > 1:
            shifted = jnp.concatenate([prev_ref[ATTN_BLOCK:, :], nxt], axis=0)
        else:
            shifted = nxt
        tot = cur_ref[...] + shifted
        dkvo_ref[:, :LANES] = unrope(tot[:, :LANES], cosv, sinv).astype(BF)
        dkvo_ref[:, LANES:] = tot[:, LANES:].astype(BF)

    nxt_spec = pl.BlockSpec((ATTN_BLOCK, 2 * LANES), lambda i: (jnp.minimum((i + 1) * per, nb - 1), 0))
    return _call(body, name, (_sds((s, D_MODEL), BF), _sds((s, 2 * LANES), BF)), (steps,),
                 [_rows(tm, D_MODEL), _rows(tm, 2 * LANES), _rows(tm, 2 * LANES), nxt_spec, _rows(tm, LANES),
                  _rows(tm, LANES)],
                 (_rows(tm, D_MODEL), _rows(tm, 2 * LANES)), ("parallel",))(dqr, dkv_cur, dkv_prev, dkv_prev, cosf, sinf)


def _band_mask(n):
    qi = lax.broadcasted_iota(jnp.int32, (ATTN_BLOCK, 2 * ATTN_BLOCK), 0)
    kj = lax.broadcasted_iota(jnp.int32, (ATTN_BLOCK, 2 * ATTN_BLOCK), 1)
    return (kj > qi) & (kj <= qi + ATTN_BLOCK) & ((n > 0) | (kj >= ATTN_BLOCK))


def _attn_probs(qa, kb, allowed, sink):
    sc = jnp.where(allowed, _dot_nt(qa, kb) * (HEAD_DIM ** -0.5), NEG_BIG)
    m = jnp.maximum(jnp.max(sc, axis=-1, keepdims=True), sink)
    p = jnp.exp(sc - m)
    esink = jnp.exp(sink - m)
    inv = 1.0 / (jnp.sum(p, axis=-1, keepdims=True) + esink)
    return p * inv, esink * inv


def _kv_specs():
    cur = pl.BlockSpec((ATTN_BLOCK, LANES), lambda n: (n, 0))
    prev = pl.BlockSpec((ATTN_BLOCK, LANES), lambda n: (jnp.maximum(n - 1, 0), 0))
    return [prev, cur] * 4


def _attention(name, qr, kk0, kk1, vv0, vv1, sinks):
    s = qr.shape[0]
    nb = s // ATTN_BLOCK

    def body(sink_ref, q_ref, k0p, k0c, k1p, k1c, v0p, v0c, v1p, v1c, y_ref):
        allowed = _band_mask(pl.program_id(0))
        upper = lax.broadcasted_iota(jnp.int32, (1, LANES), 1) >= HEAD_DIM
        bands = ((jnp.concatenate([k0p[...], k0c[...]], axis=0), jnp.concatenate([v0p[...], v0c[...]], axis=0)),
                 (jnp.concatenate([k1p[...], k1c[...]], axis=0), jnp.concatenate([v1p[...], v1c[...]], axis=0)))
        for hk in range(N_KV_HEADS):
            kb, vb = bands[hk]
            for j in range(4):
                col = (hk * 4 + j) * LANES
                qp = q_ref[:, col:col + LANES]
                out = jnp.zeros((ATTN_BLOCK, LANES), F32)
                for half in range(2):
                    sel = upper if half else jnp.logical_not(upper)
                    qa = jnp.where(sel, qp, jnp.zeros_like(qp))
                    pn, _ = _attn_probs(qa, kb, allowed, sink_ref[hk * 8 + j * 2 + half])
                    out = jnp.where(sel, _dot(pn.astype(BF), vb), out)
                y_ref[:, col:col + LANES] = out.astype(BF)

    return _call(body, name, _sds((s, D_MODEL), BF), (nb,),
                 [pl.BlockSpec(memory_space=pltpu.SMEM), pl.BlockSpec((ATTN_BLOCK, D_MODEL), lambda n: (n, 0))] + _kv_specs(),
                 pl.BlockSpec((ATTN_BLOCK, D_MODEL), lambda n: (n, 0)), ("parallel",))(
                     sinks, qr, kk0, kk0, kk1, kk1, vv0, vv0, vv1, vv1)


def _attention_bwd(name, qr, kk0, kk1, vv0, vv1, sinks, dy):
    s = qr.shape[0]
    nb = s // ATTN_BLOCK

    def body(sink_ref, q_ref, dy_ref, k0p, k0c, k1p, k1c, v0p, v0c, v1p, v1c, dq_ref, cur_ref, prev_ref, dsink_ref):
        @pl.when(pl.program_id(0) == 0)
        def _():
            dsink_ref[...] = jnp.zeros_like(dsink_ref)

        allowed = _band_mask(pl.program_id(0))
        lane = lax.broadcasted_iota(jnp.int32, (1, LANES), 1)
        upper = lane >= HEAD_DIM
        bands = ((jnp.concatenate([k0p[...], k0c[...]], axis=0), jnp.concatenate([v0p[...], v0c[...]], axis=0)),
                 (jnp.concatenate([k1p[...], k1c[...]], axis=0), jnp.concatenate([v1p[...], v1c[...]], axis=0)))
        dsink = jnp.zeros((1, LANES), F32)
        dk_slab = jnp.zeros((2 * ATTN_BLOCK, LANES), F32)
        dv_slab = jnp.zeros((2 * ATTN_BLOCK, LANES), F32)
        for hk in range(N_KV_HEADS):
            kb, vb = bands[hk]
            dkk = jnp.zeros((2 * ATTN_BLOCK, LANES), F32)
            dvv = jnp.zeros((2 * ATTN_BLOCK, LANES), F32)
            for j in range(4):
                col = (hk * 4 + j) * LANES
                qp = q_ref[:, col:col + LANES]
                dyp = dy_ref[:, col:col + LANES]
                dq_pair = jnp.zeros((ATTN_BLOCK, LANES), F32)
                for half in range(2):
                    h = hk * 8 + j * 2 + half
                    sel = upper if half else jnp.logical_not(upper)
                    qa = jnp.where(sel, qp, jnp.zeros_like(qp))
                    pn, psink = _attn_probs(qa, kb, allowed, sink_ref[h])
                    dya = jnp.where(sel, dyp, 0.0).astype(BF)
                    dp = _dot_nt(dya, vb)
                    delta = jnp.sum(pn * dp, axis=-1, keepdims=True)
                    ds = (pn * (dp - delta) * (HEAD_DIM ** -0.5)).astype(BF)
                    dsink = dsink + jnp.where(lane == h, -jnp.sum(psink * delta), 0.0)
                    dq_pair = jnp.where(sel, _dot(ds, kb), dq_pair)
                    dkk = dkk + _dot_tn(ds, qa)
                    dvv = dvv + _dot_tn(pn.astype(BF), dya)
                dq_ref[:, col:col + LANES] = dq_pair
            mine = upper if hk else jnp.logical_not(upper)
            dk_slab = jnp.where(mine, dkk + pltpu.roll(dkk, HEAD_DIM, axis=1), dk_slab)
            dv_slab = jnp.where(mine, dvv + pltpu.roll(dvv, HEAD_DIM, axis=1), dv_slab)
        prev_ref[:, :LANES] = dk_slab[:ATTN_BLOCK]
        prev_ref[:, LANES:] = dv_slab[:ATTN_BLOCK]
        cur_ref[:, :LANES] = dk_slab[ATTN_BLOCK:]
        cur_ref[:, LANES:] = dv_slab[ATTN_BLOCK:]
        dsink_ref[...] += dsink

    blk = pl.BlockSpec((ATTN_BLOCK, D_MODEL), lambda n: (n, 0))
    kvo = pl.BlockSpec((ATTN_BLOCK, 2 * LANES), lambda n: (n, 0))
    return _call(body, name,
                 (_sds((s, D_MODEL), F32), _sds((s, 2 * LANES), F32), _sds((s, 2 * LANES), F32), _sds((1, LANES), F32)),
                 (nb,), [pl.BlockSpec(memory_space=pltpu.SMEM), blk, blk] + _kv_specs(),
                 (blk, kvo, kvo, pl.BlockSpec((1, LANES), lambda n: (0, 0))), ("arbitrary",))(
                     sinks, qr, dy, kk0, kk0, kk1, kk1, vv0, vv0, vv1, vv1)


def _sgu_weights(wm_ref, g):
    t = lax.broadcasted_iota(jnp.int32, (SGU_CHUNK, SGU_CHUNK), 0)
    sidx = lax.broadcasted_iota(jnp.int32, (SGU_CHUNK, SGU_CHUNK), 1)
    return jnp.where(sidx <= t, wm_ref[g], 0.0).astype(BF)


def _layer_norm_stats(v):
    mu = jnp.mean(v, axis=-1, keepdims=True)
    cen = v - mu
    rstd = lax.rsqrt(jnp.mean(cen * cen, axis=-1, keepdims=True) + NORM_EPS)
    return cen * rstd, rstd


def _sgu(name, u_pre, v_pre, ln_w, ln_b, wm, bfull, tm=256):
    s, w = u_pre.shape
    tm = min(tm, s)

    def body(u_ref, v_ref, lw_ref, lb_ref, wm_ref, b_ref, y_ref):
        vhat, _ = _layer_norm_stats(_gelu(v_ref[...]))
        vn = (vhat * lw_ref[...] + lb_ref[...]).astype(BF)
        for g in range(SGU_GROUPS):
            wg = _sgu_weights(wm_ref, g)
            cols = slice(g * SGU_CHUNK, (g + 1) * SGU_CHUNK)
            for ch in range(tm // SGU_CHUNK):
                rows = slice(ch * SGU_CHUNK, (ch + 1) * SGU_CHUNK)
                f = _dot(wg, vn[rows, cols]) + b_ref[g]
                y_ref[rows, cols] = (_gelu(u_ref[rows, cols]) * f).astype(BF)

    full3 = pl.BlockSpec((SGU_GROUPS, SGU_CHUNK, SGU_CHUNK), lambda i: (0, 0, 0))
    return _call(body, name, _sds((s, w), BF), (s // tm,),
                 [_rows(tm, w), _rows(tm, w), _vec(w), _vec(w), full3, full3],
                 _rows(tm, w), ("parallel",))(u_pre, v_pre, ln_w, ln_b, wm, bfull)


def _sgu_bwd(name, u_pre, v_pre, ln_w, ln_b, wm, bfull, dy, tm=256):
    s, w = u_pre.shape
    tm = min(tm, s)
    steps = s // tm

    def body(u_ref, v_ref, lw_ref, lb_ref, wm_ref, b_ref, dy_ref, du_ref, dv_ref, dwm_ref, db_ref, dlw_ref, dlb_ref,
             dfsum_ref):
        i = pl.program_id(0)

        @pl.when(i == 0)
        def _():
            dwm_ref[...] = jnp.zeros_like(dwm_ref)
            dlw_ref[...] = jnp.zeros_like(dlw_ref)
            dlb_ref[...] = jnp.zeros_like(dlb_ref)
            dfsum_ref[...] = jnp.zeros_like(dfsum_ref)

        vpre = v_ref[...]
        vhat, rstd = _layer_norm_stats(_gelu(vpre))
        vn = (vhat * lw_ref[...] + lb_ref[...]).astype(BF)
        t = lax.broadcasted_iota(jnp.int32, (SGU_CHUNK, SGU_CHUNK), 0)
        sidx = lax.broadcasted_iota(jnp.int32, (SGU_CHUNK, SGU_CHUNK), 1)
        dvn_cols = []
        for g in range(SGU_GROUPS):
            wg = _sgu_weights(wm_ref, g)
            cols = slice(g * SGU_CHUNK, (g + 1) * SGU_CHUNK)
            dvn_rows = []
            dwg = jnp.zeros((SGU_CHUNK, SGU_CHUNK), F32)
            dfs = jnp.zeros((SGU_CHUNK, SGU_CHUNK), F32)
            for ch in range(tm // SGU_CHUNK):
                rows = slice(ch * SGU_CHUNK, (ch + 1) * SGU_CHUNK)
                upre = u_ref[rows, cols]
                dyv = dy_ref[rows, cols]
                f = _dot(wg, vn[rows, cols]) + b_ref[g]
                du_ref[rows, cols] = (dyv * f * _gelu_grad(upre)).astype(BF)
                df = dyv * _gelu(upre)
                dfb = df.astype(BF)
                dvn_rows.append(_dot_tn(wg, dfb))
                dwg = dwg + _dot_nt(dfb, vn[rows, cols])
                dfs = dfs + df
            dwm_ref[g] += jnp.where(sidx <= t, dwg, 0.0)
            dfsum_ref[g] += dfs
            dvn_cols.append(jnp.concatenate(dvn_rows, axis=0) if len(dvn_rows) > 1 else dvn_rows[0])
        dvn = jnp.concatenate(dvn_cols, axis=1)
        dlw_ref[...] += jnp.sum(dvn * vhat, axis=0, keepdims=True)
        dlb_ref[...] += jnp.sum(dvn, axis=0, keepdims=True)
        dvh = dvn * lw_ref[...]
        dvg = rstd * (dvh - jnp.mean(dvh, axis=-1, keepdims=True) - vhat * jnp.mean(dvh * vhat, axis=-1, keepdims=True))
        dv_ref[...] = (dvg * _gelu_grad(vpre)).astype(BF)

        @pl.when(i == steps - 1)
        def _():
            for g in range(SGU_GROUPS):
                db_ref[g:g + 1, :] = jnp.sum(dfsum_ref[g].T, axis=0, keepdims=True)

    full3 = pl.BlockSpec((SGU_GROUPS, SGU_CHUNK, SGU_CHUNK), lambda i: (0, 0, 0))
    return _call(body, name,
                 (_sds((s, w), BF), _sds((s, w), BF), _sds((SGU_GROUPS, SGU_CHUNK, SGU_CHUNK), F32),
                  _sds((SGU_GROUPS, SGU_CHUNK), F32), _sds((1, w), F32), _sds((1, w), F32)),
                 (steps,),
                 [_rows(tm, w), _rows(tm, w), _vec(w), _vec(w), full3, full3, _rows(tm, w)],
                 (_rows(tm, w), _rows(tm, w), full3, pl.BlockSpec((SGU_GROUPS, SGU_CHUNK), lambda i: (0, 0)), _vec(w), _vec(w)),
                 ("arbitrary",), scratch=[pltpu.VMEM((SGU_GROUPS, SGU_CHUNK, SGU_CHUNK), F32)])(
                     u_pre, v_pre, ln_w, ln_b, wm, bfull, dy)


def _mix_out(name, y_sgu, y_attn, ga_pre, gb_pre, x, g1, proj_a, proj_b, w_out, w2, sc2, sh2, tm=256):
    s, d = x.shape
    tm = min(tm, s)

    def body(ys_ref, ya_ref, ga_ref, gb_ref, x_ref, g1_ref, wa_ref, wb_ref, wo_ref, w2_ref, sc2_ref, sh2_ref,
             m_ref, pa_ref, pb_ref, o_ref, x1_ref, h2_ref):
        pa = _dot(ys_ref[...], wa_ref[...].reshape(d, d))
        pb = _dot(ya_ref[...], wb_ref[...].reshape(d, d))
        pa_ref[...] = pa
        pb_ref[...] = pb
        merged = (_sigmoid(ga_ref[...]) * pa + _sigmoid(gb_ref[...]) * pb).astype(BF)
        m_ref[...] = merged
        o = _dot(merged, wo_ref[...].reshape(d, d))
        o_ref[...] = o
        x1 = x_ref[...] + g1_ref[...] * o
        x1_ref[...] = x1
        h2_ref[...] = ((x1 * _rms(x1)) * w2_ref[...] * (1.0 + sc2_ref[...]) + sh2_ref[...]).astype(BF)

    f, b = _sds((s, d), F32), _sds((s, d), BF)
    r = _rows(tm, d)
    wspec = _resident(proj_a.shape)
    return _call(body, name, (b, f, f, f, f, b), (s // tm,),
                 [r, r, r, r, r, _vec(d), wspec, wspec, wspec, _vec(d), _vec(d), _vec(d)], (r,) * 6, ("parallel",))(
                     y_sgu, y_attn, ga_pre, gb_pre, x, g1, proj_a, proj_b, w_out, w2, sc2, sh2)


def _mix_bwd(name, do, w_out, proj_a, proj_b, ga_pre, gb_pre, pa, pb, tm=256):
    s, d = do.shape
    tm = min(tm, s)

    def body(do_ref, wo_ref, wa_ref, wb_ref, ga_ref, gb_ref, pa_ref, pb_ref,
             dpa_ref, dpb_ref, dga_ref, dgb_ref, dys_ref, dya_ref):
        dm = _dot_nt(do_ref[...], wo_ref[...].reshape(d, d))
        ga = _sigmoid(ga_ref[...])
        gb = _sigmoid(gb_ref[...])
        dpa = (dm * ga).astype(BF)
        dpb = (dm * gb).astype(BF)
        dpa_ref[...] = dpa
        dpb_ref[...] = dpb
        dga_ref[...] = (dm * pa_ref[...] * ga * (1.0 - ga)).astype(BF)
        dgb_ref[...] = (dm * pb_ref[...] * gb * (1.0 - gb)).astype(BF)
        dys_ref[...] = _dot_nt(dpa, wa_ref[...].reshape(d, d))
        dya_ref[...] = _dot_nt(dpb, wb_ref[...].reshape(d, d))

    f, b = _sds((s, d), F32), _sds((s, d), BF)
    r = _rows(tm, d)
    wspec = _resident(w_out.shape)
    return _call(body, name, (b, b, b, b, f, f), (s // tm,), [r, wspec, wspec, wspec, r, r, r, r], (r,) * 6,
                 ("parallel",))(do, w_out, proj_a, proj_b, ga_pre, gb_pre, pa, pb)


def _ffn_up(name, h2, w_gate, w_up, tm=1024):
    s, d = h2.shape
    tm = min(tm, s)
    tc = FFN_SHARD_PAD

    def body(h_ref, wg_ref, wu_ref, a_ref, up_ref):
        hv = h_ref[...]
        a_ref[...] = _dot(hv, wg_ref[0])
        up_ref[...] = _dot(hv, wu_ref[0])

    wspec = pl.BlockSpec((1, d, tc), lambda j, i: (j, 0, 0))
    ospec = pl.BlockSpec((tm, tc), lambda j, i: (i, j))
    o = _sds((s, FFN_PAD), F32)
    return _call(body, name, (o, o), (N_CHIPS, s // tm), [pl.BlockSpec((tm, d), lambda j, i: (i, 0)), wspec, wspec],
                 (ospec, ospec), ("parallel", "parallel"))(h2, w_gate, w_up)


def _conv_act(name, a, up, cw, cb, tm=512, tc=FFN_SHARD_PAD):
    s, c = a.shape
    tm = min(tm, s)
    per = tm // SUBLANES

    def body(a_ref, prev_ref, up_ref, cw_ref, cb_ref, h_ref):
        prev = jnp.where(pl.program_id(1) > 0, prev_ref[...], 0.0)
        ext = jnp.concatenate([prev, a_ref[...]], axis=0)
        ac = (cb_ref[...] + cw_ref[0:1, :] * pltpu.roll(ext, 2, axis=0) + cw_ref[1:2, :] * pltpu.roll(ext, 1, axis=0)
              + cw_ref[2:3, :] * ext)[SUBLANES:]
        h_ref[...] = (ac * _sigmoid(ac) * up_ref[...]).astype(BF)

    tile = pl.BlockSpec((tm, tc), lambda j, i: (i, j))
    prev = pl.BlockSpec((SUBLANES, tc), lambda j, i: (jnp.maximum(i * per - 1, 0), j))
    return _call(body, name, _sds((s, c), BF), (c // tc, s // tm),
                 [tile, prev, tile, pl.BlockSpec((3, tc), lambda j, i: (0, j)), pl.BlockSpec((1, tc), lambda j, i: (0, j))],
                 tile, ("parallel", "parallel"))(a, a, up, cw, cb)


def _conv_act_bwd(name, a, up, dhf, cw, cb, tm=512, tc=FFN_SHARD_PAD):
    s, c = a.shape
    tm = min(tm, s)
    per = tm // SUBLANES
    steps = s // tm
    last8 = s // SUBLANES - 1

    def body(a_ref, aprev_ref, anext_ref, up_ref, upnext_ref, dh_ref, dhnext_ref, cw_ref, cb_ref,
             da_ref, dup_ref, dcw_ref, dcb_ref):
        i = pl.program_id(1)

        @pl.when(i == 0)
        def _():
            dcw_ref[...] = jnp.zeros_like(dcw_ref)
            dcb_ref[...] = jnp.zeros_like(dcb_ref)

        prev = jnp.where(i > 0, aprev_ref[...], 0.0)
        ext = jnp.concatenate([prev, a_ref[...], anext_ref[...]], axis=0)
        a1 = pltpu.roll(ext, 1, axis=0)[SUBLANES:]
        a2 = pltpu.roll(ext, 2, axis=0)[SUBLANES:]
        a0 = ext[SUBLANES:]
        ac = cb_ref[...] + cw_ref[0:1, :] * a2 + cw_ref[1:2, :] * a1 + cw_ref[2:3, :] * a0
        sig = _sigmoid(ac)
        dh = jnp.concatenate([dh_ref[...], jnp.where(i < steps - 1, dhnext_ref[...], 0.0)], axis=0)
        upe = jnp.concatenate([up_ref[...], upnext_ref[...]], axis=0)
        dac = dh * upe * (sig * (1.0 + ac * (1.0 - sig)))
        dup_ref[...] = (dh[:tm] * (ac * sig)[:tm]).astype(BF)
        n = tm + SUBLANES
        da = (cw_ref[2:3, :] * dac + cw_ref[1:2, :] * pltpu.roll(dac, n - 1, axis=0)
              + cw_ref[0:1, :] * pltpu.roll(dac, n - 2, axis=0))
        da_ref[...] = da[:tm].astype(BF)
        dact = dac[:tm]
        dcb_ref[...] += jnp.sum(dact, axis=0, keepdims=True)
        dcw_ref[0:1, :] += jnp.sum(dact * a2[:tm], axis=0, keepdims=True)
        dcw_ref[1:2, :] += jnp.sum(dact * a1[:tm], axis=0, keepdims=True)
        dcw_ref[2:3, :] += jnp.sum(dact * a0[:tm], axis=0, keepdims=True)

    tile = pl.BlockSpec((tm, tc), lambda j, i: (i, j))
    prev = pl.BlockSpec((SUBLANES, tc), lambda j, i: (jnp.maximum(i * per - 1, 0), j))
    nxt = pl.BlockSpec((SUBLANES, tc), lambda j, i: (jnp.minimum((i + 1) * per, last8), j))
    cw_spec = pl.BlockSpec((3, tc), lambda j, i: (0, j))
    cb_spec = pl.BlockSpec((1, tc), lambda j, i: (0, j))
    return _call(body, name, (_sds((s, c), BF), _sds((s, c), BF), _sds((3, c), F32), _sds((1, c), F32)),
                 (c // tc, steps), [tile, prev, nxt, tile, nxt, tile, nxt, cw_spec, cb_spec],
                 (tile, tile, cw_spec, cb_spec), ("parallel", "arbitrary"))(a, a, a, up, up, dhf, dhf, cw, cb)


def _ffn_down(name, hf, w_down, x1, g2, tm=512):
    s, d = x1.shape
    tm = min(tm, s)

    def body(hf_ref, wd_ref, x1_ref, g2_ref, dn_ref, x2_ref):
        dn = _dot(hf_ref[...], wd_ref[...].reshape(FFN_PAD, d))
        dn_ref[...] = dn
        x2_ref[...] = x1_ref[...] + g2_ref[...] * dn

    o = _sds((s, d), F32)
    return _call(body, name, (o, o), (s // tm,),
                 [_rows(tm, FFN_PAD), _resident(w_down.shape), _rows(tm, d), _vec(d)],
                 (_rows(tm, d), _rows(tm, d)), ("parallel",))(hf, w_down, x1, g2)


def _ffn_down_bwd(name, dx2, dn, g2, w_down, tm=512):
    s, d = dx2.shape
    tm = min(tm, s)
    tc = FFN_SHARD_PAD

    def body(dx_ref, dn_ref, g2_ref, wd_ref, ddn_ref, dhf_ref, dg_ref):
        @pl.when(pl.program_id(0) == 0)
        def _():
            dg_ref[...] = jnp.zeros_like(dg_ref)

        dxv = dx_ref[...]
        ddn = (dxv * g2_ref[...]).astype(BF)
        ddn_ref[...] = ddn
        dg_ref[...] += jnp.sum(dxv * dn_ref[...], axis=0, keepdims=True)
        for k in range(N_CHIPS):
            dhf_ref[:, k * tc:(k + 1) * tc] = _dot_nt(ddn, wd_ref[k])

    return _call(body, name, (_sds((s, d), BF), _sds((s, FFN_PAD), F32), _sds((1, d), F32)), (s // tm,),
                 [_rows(tm, d), _rows(tm, d), _vec(d), _resident(w_down.shape)],
                 (_rows(tm, d), _rows(tm, FFN_PAD), _vec(d)), ("arbitrary",))(dx2, dn, g2, w_down)


def _ffn_up_bwd(name, da, dup, w_gate, w_up, x1, dx2, w2, sc2, o, g1, tm=256):
    s, d = x1.shape
    tm = min(tm, s)
    tc = FFN_SHARD_PAD

    def body(da_ref, dup_ref, wg_ref, wu_ref, x1_ref, dx2_ref, w2_ref, sc2_ref, o_ref, g1_ref,
             dx1_ref, do_ref, dnw_ref, dsh_ref, dg1_ref):
        @pl.when(pl.program_id(0) == 0)
        def _():
            dnw_ref[...] = jnp.zeros_like(dnw_ref)
            dsh_ref[...] = jnp.zeros_like(dsh_ref)
            dg1_ref[...] = jnp.zeros_like(dg1_ref)

        dh = jnp.zeros((tm, d), F32)
        for k in range(N_CHIPS):
            cols = slice(k * tc, (k + 1) * tc)
            dh = dh + _dot_nt(da_ref[:, cols], wg_ref[k]) + _dot_nt(dup_ref[:, cols], wu_ref[k])
        xv = x1_ref[...]
        r = _rms(xv)
        xn = xv * r
        dxn = dh * (w2_ref[...] * (1.0 + sc2_ref[...]))
        dx1 = dx2_ref[...] + r * (dxn - xn * jnp.mean(dxn * xn, axis=-1, keepdims=True))
        dx1_ref[...] = dx1
        dnw_ref[...] += jnp.sum(dh * xn, axis=0, keepdims=True)
        dsh_ref[...] += jnp.sum(dh, axis=0, keepdims=True)
        do_ref[...] = (dx1 * g1_ref[...]).astype(BF)
        dg1_ref[...] += jnp.sum(dx1 * o_ref[...], axis=0, keepdims=True)

    v = _sds((1, d), F32)
    r = _rows(tm, d)
    wspec = _resident(w_gate.shape)
    return _call(body, name, (_sds((s, d), F32), _sds((s, d), BF), v, v, v), (s // tm,),
                 [_rows(tm, FFN_PAD), _rows(tm, FFN_PAD), wspec, wspec, r, r, _vec(d), _vec(d), r, _vec(d)],
                 (r, r, _vec(d), _vec(d), _vec(d)), ("arbitrary",))(da, dup, w_gate, w_up, x1, dx2, w2, sc2, o, g1)


def _loss_head(name, x, w, target, tm=512):
    s, d = x.shape
    tm = min(tm, s)

    def body(x_ref, w_ref, t_ref, dx_ref, loss_ref, dw_ref):
        @pl.when(pl.program_id(0) == 0)
        def _():
            loss_ref[...] = jnp.zeros_like(loss_ref)
            dw_ref[...] = jnp.zeros_like(dw_ref)

        xv = x_ref[...]
        r = _rms(xv)
        xn = xv * r
        err = xn * w_ref[...] - t_ref[...]
        loss_ref[...] += 0.5 * jnp.sum(jnp.mean(err * err, axis=-1, keepdims=True))
        dy = err * (1.0 / d)
        dw_ref[...] += jnp.sum(dy * xn, axis=0, keepdims=True)
        dxn = dy * w_ref[...]
        dx_ref[...] = r * (dxn - xn * jnp.mean(dxn * xn, axis=-1, keepdims=True))

    return _call(body, name, (_sds((s, d), F32), _sds((1, LANES), F32), _sds((1, d), F32)), (s // tm,),
                 [_rows(tm, d), _vec(d), _rows(tm, d)], (_rows(tm, d), _vec(LANES), _vec(d)), ("arbitrary",))(x, w, target)


def _layer_fwd(l, x, mod, p, cosf, sinf):
    sh1, sc1, g1, sh2, sc2, g2 = mod
    tag = f"l{l}_"
    h, qr, kk0, kk1, vv0, vv1, u_pre, v_pre, ga_pre, gb_pre = _in_proj(
        tag + "in_proj", x, p["norm1_w"], sc1, sh1, p["w_in"], cosf, sinf)
    y_attn = _attention(tag + "attn", qr, kk0, kk1, vv0, vv1, p["sinks"])
    y_sgu = _sgu(tag + "sgu", u_pre, v_pre, p["sgu_ln_w"], p["sgu_ln_b"], p["sgu_w"], p["sgu_bfull"])
    merged, pa, pb, o, x1, h2 = _mix_out(tag + "mix_out", y_sgu, y_attn, ga_pre, gb_pre, x, g1, p["proj_a"], p["proj_b"],
                                         p["w_out"], p["norm2_w"], sc2, sh2)
    a, up = _ffn_up(tag + "ffn_up", h2, p["w_gate"], p["w_up"])
    hf = _conv_act(tag + "conv_act", a, up, p["conv_w"], p["conv_b"])
    dn, x2 = _ffn_down(tag + "ffn_down", hf, p["w_down"], x1, g2)
    saved = dict(x=x, h=h, qr=qr, kk0=kk0, kk1=kk1, vv0=vv0, vv1=vv1, u_pre=u_pre, v_pre=v_pre, ga_pre=ga_pre,
                 gb_pre=gb_pre, y_attn=y_attn, y_sgu=y_sgu, merged=merged, pa=pa, pb=pb, o=o, x1=x1, h2=h2, a=a, up=up,
                 hf=hf, dn=dn)
    return x2, saved


def _layer_bwd(l, dx2, mod, p, sv, cosf, sinf):
    sh1, sc1, g1, sh2, sc2, g2 = mod
    tag = f"l{l}_b_"
    d = D_MODEL
    g = {}
    ddn, dhf, dg2 = _ffn_down_bwd(tag + "ffn_down", dx2, sv["dn"], g2, p["w_down"])
    g["w_down"] = _matmul_tn(tag + "dw_down", sv["hf"], ddn, tk=FFN_SHARD_PAD).reshape(N_CHIPS, FFN_SHARD_PAD, d)
    da, dup, g["conv_w"], g["conv_b"] = _conv_act_bwd(tag + "conv_act", sv["a"], sv["up"], dhf, p["conv_w"], p["conv_b"])
    g["w_gate"] = _matmul_tn(tag + "dw_gate", sv["h2"], da, tn=FFN_SHARD_PAD, blocked=True)
    g["w_up"] = _matmul_tn(tag + "dw_up", sv["h2"], dup, tn=FFN_SHARD_PAD, blocked=True)
    dx1, do, da2, dsh2, dg1 = _ffn_up_bwd(tag + "ffn_up", da, dup, p["w_gate"], p["w_up"], sv["x1"], dx2, p["norm2_w"],
                                          sc2, sv["o"], g1)
    g["norm2_w"] = da2 * (1.0 + sc2)
    dsc2 = da2 * p["norm2_w"]
    g["w_out"] = _matmul_tn(tag + "dw_out", sv["merged"], do).reshape(N_CHIPS, d // N_CHIPS, d)
    dpa, dpb, dga, dgb, dy_sgu, dy_attn = _mix_bwd(tag + "mix", do, p["w_out"], p["proj_a"], p["proj_b"], sv["ga_pre"],
                                                  sv["gb_pre"], sv["pa"], sv["pb"])
    g["proj_a"] = _matmul_tn(tag + "dproj_a", sv["y_sgu"], dpa).reshape(N_CHIPS, d // N_CHIPS, d)
    g["proj_b"] = _matmul_tn(tag + "dproj_b", sv["y_attn"], dpb).reshape(N_CHIPS, d // N_CHIPS, d)
    du, dv, g["sgu_w"], g["sgu_b"], g["sgu_ln_w"], g["sgu_ln_b"] = _sgu_bwd(
        tag + "sgu", sv["u_pre"], sv["v_pre"], p["sgu_ln_w"], p["sgu_ln_b"], p["sgu_w"], p["sgu_bfull"], dy_sgu)
    dqr, dkv_cur, dkv_prev, dsink = _attention_bwd(tag + "attn", sv["qr"], sv["kk0"], sv["kk1"], sv["vv0"], sv["vv1"],
                                                   p["sinks"], dy_attn)
    g["sinks"] = dsink[0, :N_Q_HEADS]
    dq, dkv = _rope_bwd(tag + "rope", dqr, dkv_cur, dkv_prev, cosf, sinf)
    dw_in = [_matmul_tn(tag + "dw_in_" + n, sv["h"], t) for n, t in
             (("q", dq), ("kv", dkv), ("u", du), ("v", dv), ("ga", dga), ("gb", dgb))]
    g["w_in"] = jnp.concatenate(dw_in, axis=1)
    dx, da1, dsh1 = _in_proj_bwd(tag + "in_proj", dq, dkv, du, dv, dga, dgb, p["w_in"], sv["x"], p["norm1_w"], sc1, dx1)
    g["norm1_w"] = da1 * (1.0 + sc1)
    dsc1 = da1 * p["norm1_w"]
    return dx, (dsh1, dsc1, dg1, dsh2, dsc2, dg2), g


def _pad_to(a, axis, size):
    pad = [(0, 0)] * a.ndim
    pad[axis] = (0, size - a.shape[axis])
    return jnp.pad(a, pad)


def _layer_params(w_in, proj_a, proj_b, w_out, w_gate, w_up, w_down, conv_w, small):
    d = D_MODEL
    return dict(
        w_in=w_in.transpose(1, 0, 2).reshape(d, IN_COLS), proj_a=proj_a, proj_b=proj_b, w_out=w_out, w_gate=w_gate, w_up=w_up,
        w_down=w_down, conv_w=conv_w.transpose(1, 0, 2).reshape(3, FFN_PAD),
        conv_b=_pad_to(small["conv_b"].reshape(N_CHIPS, FFN_SHARD), 1, FFN_SHARD_PAD).reshape(1, FFN_PAD),
        norm1_w=small["norm1_w"].reshape(1, d), norm2_w=small["norm2_w"].reshape(1, d), sinks=small["sinks"],
        sgu_ln_w=small["sgu_ln_w"].reshape(1, d), sgu_ln_b=small["sgu_ln_b"].reshape(1, d), sgu_w=small["sgu_w"],
        sgu_bfull=jnp.broadcast_to(small["sgu_b"][:, :, None], (SGU_GROUPS, SGU_CHUNK, SGU_CHUNK)))


def _conv_grads_natural(g):
    cw = g["conv_w"].reshape(3, N_CHIPS, FFN_SHARD_PAD)[:, :, :FFN_SHARD].reshape(3, FFN_DIM)
    cb = g["conv_b"].reshape(N_CHIPS, FFN_SHARD_PAD)[:, :FFN_SHARD].reshape(FFN_DIM)
    return cw, cb


def _rope_tables(positions):
    inv_freq = ROPE_THETA ** (-jnp.arange(0, ROT_DIM, 2, dtype=F32) / ROT_DIM)
    ang = positions.astype(F32)[:, None] * inv_freq
    cos, sin = jnp.cos(ang), jnp.sin(ang)
    s = positions.shape[0]
    rest = HEAD_DIM - ROT_DIM
    cos_head = jnp.concatenate([cos, cos, jnp.ones((s, rest), F32)], axis=1)
    sin_head = jnp.concatenate([-sin, sin, jnp.zeros((s, rest), F32)], axis=1)
    return jnp.tile(cos_head, (1, LANES // HEAD_DIM)), jnp.tile(sin_head, (1, LANES // HEAD_DIM))


ADA_ROWS = 16


def _ada_fwd(name, c_rows, ada_w, ada_b_cols, tn=512):
    depth, d, n = ada_w.shape

    def body(c_ref, w_ref, b_ref, o_ref):
        cv = c_ref[...]
        act = (cv * _sigmoid(cv)).astype(BF)
        o_ref[0] = _dot(act, w_ref[0].astype(BF)) + b_ref[0]

    return _call(body, name, _sds((depth, ADA_ROWS, n), F32), (depth, n // tn),
                 [pl.BlockSpec((ADA_ROWS, d), lambda l, j: (0, 0)), pl.BlockSpec((1, d, tn), lambda l, j: (l, 0, j)),
                  pl.BlockSpec((1, 1, tn), lambda l, j: (l, 0, j))],
                 pl.BlockSpec((1, ADA_ROWS, tn), lambda l, j: (l, 0, j)), ("parallel", "parallel"))(c_rows, ada_w, ada_b_cols)


def _ada_bwd(name, c_rows, dmod_cols, tn=512):
    depth, _, n = dmod_cols.shape
    d = c_rows.shape[1]

    def body(c_ref, dm_ref, o_ref):
        cv = c_ref[...]
        act = (cv * _sigmoid(cv)).astype(BF)
        o_ref[0] = _dot_tn(act, dm_ref[0].astype(BF))

    return _call(body, name, _sds((depth, d, n), F32), (depth, n // tn),
                 [pl.BlockSpec((ADA_ROWS, d), lambda l, j: (0, 0)), pl.BlockSpec((1, ADA_ROWS, tn), lambda l, j: (l, 0, j))],
                 pl.BlockSpec((1, d, tn), lambda l, j: (l, 0, j)), ("parallel", "parallel"))(c_rows, dmod_cols)


def _colsum(name, a):
    r, n = a.shape

    def body(a_ref, o_ref):
        o_ref[...] = jnp.sum(a_ref[...], axis=0, keepdims=True)

    return _call(body, name, _sds((1, n), F32), (1,), [pl.BlockSpec((r, n), lambda i: (0, 0))],
                 pl.BlockSpec((1, n), lambda i: (0, 0)), ("arbitrary",))(a)


REL_SIBLING = (0, 0, 1)
REL_CHIPS = ((1, 0, 0), (0, 1, 0), (1, 1, 0))
REL_ALL = tuple((fx, fy, fc) for fx in (0, 1) for fy in (0, 1) for fc in (0, 1) if fx or fy or fc)


def _chip_of(dev):
    return 2 * dev[0] + dev[1]


def _dev_of(dev):
    return 4 * dev[0] + 2 * dev[1] + dev[2]


def _flip(dev, rel):
    return tuple(1 - m if f else m for m, f in zip(dev, rel))


def _exchange(name, arrays, n_out, stages, aliases=None):
    out_shapes, stages = stages[0], stages[1:]
    n_in = len(arrays)
    aliases = aliases or {}
    n_remote = sum(len(plan) for plan, _ in stages)
    n_local = sum(len(local) for _, local in stages)

    def at(ref, idx):
        return ref.at[idx] if len(idx) else ref

    def body(*refs):
        bufs = list(refs[:n_in + n_out])
        for i_in, i_out in aliases.items():
            bufs[i_in] = bufs[n_in + i_out]
        send_sems, recv_sems, local_sems = refs[n_in + n_out:]
        me = (lax.axis_index("x"), lax.axis_index("y"), lax.axis_index("c"))
        base_r = base_l = 0
        pending = []
        for plan, local in stages:
            def remote(k, entry, sender, receiver):
                rel, si, ssel, di, dsel = entry
                return pltpu.make_async_remote_copy(
                    src_ref=at(bufs[si], ssel(sender, receiver)), dst_ref=at(bufs[di], dsel(sender, receiver)),
                    send_sem=send_sems.at[k], recv_sem=recv_sems.at[k], device_id=_flip(me, rel), device_id_type=MESH)

            sends = [remote(base_r + k, e, me, _flip(me, e[0])) for k, e in enumerate(plan)]
            for cp in sends:
                cp.start()
            for k, (si, ssel, di, dsel) in enumerate(local):
                cp = pltpu.make_async_copy(at(bufs[si], ssel(me)), at(bufs[di], dsel(me)), local_sems.at[base_l + k])
                cp.start()
                pending.append(cp.wait)
            for k, e in enumerate(plan):
                remote(base_r + k, e, _flip(me, e[0]), me).wait_recv()
            pending += [cp.wait_send for cp in sends]
            base_r += len(plan)
            base_l += len(local)
        for wait in pending:
            wait()

    any_spec = pl.BlockSpec(memory_space=pl.ANY)
    return pl.pallas_call(
        body, name=name, out_shape=tuple(out_shapes), in_specs=[any_spec] * n_in, out_specs=tuple([any_spec] * n_out),
        input_output_aliases=dict(aliases),
        scratch_shapes=[pltpu.SemaphoreType.DMA((max(n_remote, 1),)), pltpu.SemaphoreType.DMA((max(n_remote, 1),)),
                        pltpu.SemaphoreType.DMA((max(n_local, 1),))])(*arrays)


def _whole(*_):
    return ()


def _half_rows(rows, core):
    return pl.ds(core * (rows // 2), rows // 2)


def _gather_weights(name, shards):
    n = len(shards)
    dsts = [_sds((N_CHIPS,) + a.shape, a.dtype) for a in shards]
    split = [a.shape[0] % (2 * 16) == 0 for a in shards]
    fetch, forward = [], []
    for t, a in enumerate(shards):
        rows = a.shape[0]
        if split[t]:
            fetch += [(rel, t, (lambda s_, r_, rows=rows: (_half_rows(rows, s_[2]),)), n + t,
                       (lambda s_, r_, rows=rows: (_chip_of(s_), _half_rows(rows, s_[2])))) for rel in REL_CHIPS]
            forward += [(REL_SIBLING, n + t, (lambda s_, r_, rows=rows, rel=rel: (_chip_of(_flip(s_, rel)), _half_rows(rows, s_[2]))),
                         n + t, (lambda s_, r_, rows=rows, rel=rel: (_chip_of(_flip(s_, rel)), _half_rows(rows, s_[2]))))
                        for rel in REL_CHIPS]
        else:
            fetch += [(rel, t, _whole, n + t, lambda s_, r_: (_chip_of(s_),)) for rel in REL_CHIPS]
    local = [(t, _whole, n + t, lambda me: (_chip_of(me),)) for t in range(n)]
    return _exchange(name, shards, n, [dsts, (fetch, local), (forward, [])])


def _gather_chips(name, arrays):
    n = len(arrays)
    dsts = [_sds((N_CHIPS,) + a.shape, a.dtype) for a in arrays]
    plan = [(rel, t, _whole, n + t, lambda s_, r_: (_chip_of(s_),)) for t in range(n) for rel in REL_CHIPS]
    local = [(t, _whole, n + t, lambda me: (_chip_of(me),)) for t in range(n)]
    return _exchange(name, arrays, n, [dsts, (plan, local)])


def _gather_all(name, a):
    plan = [(rel, 0, _whole, 1, lambda s_, r_: (_dev_of(s_),)) for rel in REL_ALL]
    local = [(0, _whole, 1, lambda me: (_dev_of(me),))]
    return _exchange(name, [a], 1, [[_sds((2 * N_CHIPS,) + a.shape, a.dtype)], (plan, local)])[0]


def _swap_halves(name, grads):
    n = len(grads)
    dsts = [_sds((g.shape[0], g.shape[1] // 2, g.shape[2]), g.dtype) for g in grads]
    plan = [(REL_SIBLING, t, (lambda s_, r_, rows=g.shape[1]: (pl.ds(0, N_CHIPS), _half_rows(rows, r_[2]))), n + t, _whole)
            for t, g in enumerate(grads)]
    return _exchange(name, grads, n, [dsts, (plan, [])])


def _scatter_chips(name, sums):
    n = len(sums)
    dsts = [_sds(a.shape, a.dtype) for a in sums]
    plan = [(rel, t, lambda s_, r_: (_chip_of(r_),), n + t, lambda s_, r_: (_chip_of(s_),))
            for t in range(n) for rel in REL_CHIPS]
    local = [(t, lambda me: (_chip_of(me),), n + t, lambda me: (_chip_of(me),)) for t in range(n)]
    return _exchange(name, sums, n, [dsts, (plan, local)])


def _swap_back(name, totals, layer):
    n = len(totals)
    dsts = [_sds(a.shape, a.dtype) for a in totals]
    plan = [(REL_SIBLING, n + t, (lambda s_, r_, rows=a.shape[1]: (layer, _half_rows(rows, s_[2]))),
             n + t, (lambda s_, r_, rows=a.shape[1]: (layer, _half_rows(rows, s_[2])))) for t, a in enumerate(totals)]
    return _exchange(name, totals, n, [dsts, (plan, [])], aliases={t: t for t in range(n)})


def _add_halves(name, g, recv, core):
    nch, half, c = recv.shape

    def body(core_ref, g_ref, r_ref, o_ref):
        o_ref[0] = (g_ref[0, 0].astype(F32) + r_ref[0].astype(F32)).astype(o_ref.dtype)

    spec = pltpu.PrefetchScalarGridSpec(
        num_scalar_prefetch=1, grid=(nch,),
        in_specs=[pl.BlockSpec((1, 1, half, c), lambda k, core_ref: (k, core_ref[0], 0, 0)),
                  pl.BlockSpec((1, half, c), lambda k, core_ref: (k, 0, 0))],
        out_specs=pl.BlockSpec((1, half, c), lambda k, core_ref: (k, 0, 0)))
    return pl.pallas_call(body, name=name, out_shape=_sds(recv.shape, recv.dtype), grid_spec=spec,
                          compiler_params=pltpu.CompilerParams(dimension_semantics=("parallel",),
                                                               vmem_limit_bytes=VMEM_LIMIT))(
                                                                   core, g.reshape(nch, 2, half, c), recv)


def _sum_chips(name, a, core, layer, total):
    nch, half, c = a.shape

    def body(core_ref, a_ref, *rest):
        o_ref = rest[-1]
        acc = a_ref[0].astype(F32)
        for k in range(1, nch):
            acc = acc + a_ref[k].astype(F32)
        o_ref[0, 0] = acc

    in_specs = [pl.BlockSpec((nch, half, c), lambda i, core_ref: (0, 0, 0))]
    args = [core, a]
    if total is not None:
        in_specs.append(pl.BlockSpec(memory_space=pl.ANY))
        args.append(total.reshape(DEPTH, 2, half, c))
    spec = pltpu.PrefetchScalarGridSpec(
        num_scalar_prefetch=1, grid=(1,), in_specs=in_specs,
        out_specs=pl.BlockSpec((1, 1, half, c), lambda i, core_ref: (layer, core_ref[0], 0, 0)))
    out = pl.pallas_call(body, name=name, out_shape=_sds((DEPTH, 2, half, c), F32), grid_spec=spec,
                         input_output_aliases={2: 0} if total is not None else {},
                         compiler_params=pltpu.CompilerParams(dimension_semantics=("arbitrary",),
                                                              vmem_limit_bytes=VMEM_LIMIT))(*args)
    return out.reshape(DEPTH, 2 * half, c)


def _adamw(name, w, g, m, v):
    shape = w.shape
    c = shape[-1]
    rows = w.size // c
    tr = next(t for t in (256, 128, 64, 32, 16, 8, rows) if rows % t == 0)
    flat = lambda a: a.reshape(rows, c)

    def body(w_ref, g_ref, m_ref, v_ref, d_ref, mo_ref, vo_ref):
        gv = g_ref[...]
        mn = ADAM_B1 * m_ref[...] + (1.0 - ADAM_B1) * gv
        vn = ADAM_B2 * v_ref[...] + (1.0 - ADAM_B2) * (gv * gv)
        m_hat = mn / (1.0 - ADAM_B1 ** ADAM_STEP)
        v_hat = vn / (1.0 - ADAM_B2 ** ADAM_STEP)
        d_ref[...] = -ADAM_LR * (m_hat / (jnp.sqrt(v_hat) + ADAM_EPS) + ADAM_WD * w_ref[...])
        mo_ref[...] = mn
        vo_ref[...] = vn

    spec = pl.BlockSpec((tr, c), lambda i: (i, 0))
    o = _sds((rows, c), F32)
    d, mn, vn = _call(body, name, (o, o, o), (rows // tr,), [spec] * 4, (spec,) * 3, ("parallel",))(
        flat(w), flat(g), flat(m), flat(v))
    return d.reshape(shape), mn.reshape(shape), vn.reshape(shape)


def _pack(arrays, rows):
    flat = jnp.concatenate([a.reshape(-1).astype(F32) for a in arrays])
    return _pad_to(flat, 0, rows * LANES).reshape(rows, LANES)


def _unpack(packed, shapes):
    flat = packed.reshape(-1)
    out, off = [], 0
    for shp in shapes:
        n = 1
        for s_ in shp:
            n *= s_
        out.append(flat[off:off + n].reshape(shp))
        off += n
    return out


_MATRICES = ("w_in", "proj_a", "proj_b", "w_out", "w_gate", "w_up", "w_down")
_SMALL = (("norm1_w", (D_MODEL,)), ("sinks", (N_Q_HEADS,)), ("sgu_ln_w", (SGU_WIDTH,)), ("sgu_ln_b", (SGU_WIDTH,)),
          ("sgu_w", (SGU_GROUPS, SGU_CHUNK, SGU_CHUNK)), ("sgu_b", (SGU_GROUPS, SGU_CHUNK)), ("norm2_w", (D_MODEL,)),
          ("conv_w", (3, FFN_DIM)), ("conv_b", (FFN_DIM,)), ("final_norm_w", (D_MODEL,)))
SMALL_ROWS = 320
ADAM_PACK_ROWS = 256


def _reduce_layer(l, grads, small, core, totals):
    tag = f"l{l}_reduce_"
    partial = [grads[k] for k in _MATRICES] + [small]
    from_sibling = _swap_halves(tag + "cores", partial)
    chip_sums = [_add_halves(f"{tag}cores_add{t}", partial[t], from_sibling[t], core) for t in range(len(partial))]
    from_chips = _scatter_chips(tag + "chips", chip_sums)
    totals = [_sum_chips(f"{tag}chips_add{t}", from_chips[t], core, l, None if totals is None else totals[t])
              for t in range(len(partial))]
    return _swap_back(tag + "back", totals, l)


def kernel(x, c, positions, ada_w, ada_b, norm1_w, w_in, attn_sinks, sgu_ln_w, sgu_ln_b, sgu_w, sgu_b, proj_a, proj_b, w_out, norm2_w, ffn_w_gate, ffn_w_up, ffn_conv_w, ffn_conv_b, ffn_w_down, final_norm_w, loss_target, m_ada_w, m_ada_b, m_norm1_w, m_w_in, m_attn_sinks, m_sgu_ln_w, m_sgu_ln_b, m_sgu_w, m_sgu_b, m_proj_a, m_proj_b, m_w_out, m_norm2_w, m_ffn_w_gate, m_ffn_w_up, m_ffn_conv_w, m_ffn_conv_b, m_ffn_w_down, m_final_norm_w, v_ada_w, v_ada_b, v_norm1_w, v_w_in, v_attn_sinks, v_sgu_ln_w, v_sgu_ln_b, v_sgu_w, v_sgu_b, v_proj_a, v_proj_b, v_w_out, v_norm2_w, v_ffn_w_gate, v_ffn_w_up, v_ffn_conv_w, v_ffn_conv_b, v_ffn_w_down, v_final_norm_w):
    d = D_MODEL
    ax, ay, ac = lax.axis_index("x"), lax.axis_index("y"), lax.axis_index("c")
    chip = 2 * ax + ay
    dev = 4 * ax + 2 * ay + ac
    core = ac.astype(jnp.int32).reshape(1)

    shards = [w_in.astype(BF), proj_a.astype(BF), proj_b.astype(BF), w_out.astype(BF),
              _pad_to(ffn_w_gate.astype(BF), 2, FFN_SHARD_PAD), _pad_to(ffn_w_up.astype(BF), 2, FFN_SHARD_PAD),
              _pad_to(ffn_w_down.astype(BF), 1, FFN_SHARD_PAD), _pad_to(ffn_conv_w, 2, FFN_SHARD_PAD)]
    gathered = [_gather_weights(f"l{l}_gather_weights", [a[l] for a in shards]) for l in range(DEPTH)]

    c_all = _gather_all("gather_cond", jnp.broadcast_to(c, (SUBLANES, d)))[:, 0, :]
    c_rows = _pad_to(c_all, 0, ADA_ROWS)
    ada_cols = ada_w.shape[2]
    ada_b_cols = lax.dynamic_slice_in_dim(ada_b, chip * ada_cols, ada_cols, axis=1).reshape(DEPTH, 1, ada_cols)
    mod_cols = _ada_fwd("ada_fwd", c_rows, ada_w, ada_b_cols)
    mod_all = _gather_chips("gather_mod", [mod_cols])[0]
    mod_mine = lax.dynamic_index_in_dim(mod_all, dev, axis=2, keepdims=False)
    mod_mine = mod_mine.transpose(1, 0, 2).reshape(DEPTH, 1, 6 * d)
    mods = [tuple(jnp.split(mod_mine[l], 6, axis=-1)) for l in range(DEPTH)]

    small_in = dict(norm1_w=norm1_w, sinks=attn_sinks, sgu_ln_w=sgu_ln_w, sgu_ln_b=sgu_ln_b, sgu_w=sgu_w, sgu_b=sgu_b,
                    norm2_w=norm2_w, conv_b=ffn_conv_b)
    params = [_layer_params(*gathered[l], {k: v[l] for k, v in small_in.items()}) for l in range(DEPTH)]

    cosf, sinf = _rope_tables(positions[0])
    h = x[0]
    saved = []
    for l in range(DEPTH):
        h, sv = _layer_fwd(l, h, mods[l], params[l], cosf, sinf)
        saved.append(sv)
    dx, loss_part, d_final = _loss_head("loss_head", h, final_norm_w.reshape(1, d), loss_target[0])
    loss = lax.psum(loss_part[0, 0], ("x", "y", "c"))
    dmods = [None] * DEPTH
    totals = None
    for l in reversed(range(DEPTH)):
        dx, dmods[l], grads = _layer_bwd(l, dx, mods[l], params[l], saved[l], cosf, sinf)
        grads["w_in"] = grads["w_in"].reshape(d, N_CHIPS, IN_COLS // N_CHIPS).transpose(1, 0, 2)
        cw, cb = _conv_grads_natural(grads)
        nat = dict(grads, conv_w=cw, conv_b=cb, final_norm_w=d_final if l == DEPTH - 1 else jnp.zeros((d,), F32))
        small = _pack([nat[k] for k, _ in _SMALL], N_CHIPS * SMALL_ROWS).reshape(N_CHIPS, SMALL_ROWS, LANES)
        totals = _reduce_layer(l, grads, small, core, totals)

    dmod_mine = jnp.concatenate([jnp.concatenate(dmods[l], axis=1) for l in range(DEPTH)], axis=1)
    dmod_all = _gather_all("gather_dmod", jnp.broadcast_to(dmod_mine, (SUBLANES, DEPTH * 6 * d)))[:, 0, :]
    g_ada_b = _colsum("ada_b_grad", dmod_all).reshape(DEPTH, 6 * d)
    dmod_cols = jnp.stack([lax.dynamic_slice_in_dim(dmod_all, l * 6 * d + chip * ada_cols, ada_cols, axis=1)
                           for l in range(DEPTH)])
    g_ada_w = _ada_bwd("ada_w_grad", c_rows, _pad_to(dmod_cols, 1, ADA_ROWS))

    small_all = _gather_chips("gather_small", [totals[-1]])[0]
    small_g = _unpack(small_all.transpose(1, 0, 2, 3).reshape(DEPTH, -1), [(DEPTH, N_CHIPS * SMALL_ROWS * LANES)])[0]
    per_layer = [_unpack(small_g[l], [shp for _, shp in _SMALL]) for l in range(DEPTH)]
    sg = {k: jnp.stack([per_layer[l][i] for l in range(DEPTH)]) for i, (k, _) in enumerate(_SMALL)}
    g_final = sg["final_norm_w"][DEPTH - 1]
    g_conv_w = lax.dynamic_slice_in_dim(sg["conv_w"], chip * FFN_SHARD, FFN_SHARD, axis=2)

    g_big = dict(zip(_MATRICES, totals[:-1]))
    g_big["w_gate"] = g_big["w_gate"][:, :, :FFN_SHARD]
    g_big["w_up"] = g_big["w_up"][:, :, :FFN_SHARD]
    g_big["w_down"] = g_big["w_down"][:, :FFN_SHARD, :]
    big = dict(w_in=(w_in, m_w_in, v_w_in), proj_a=(proj_a, m_proj_a, v_proj_a), proj_b=(proj_b, m_proj_b, v_proj_b),
               w_out=(w_out, m_w_out, v_w_out), w_gate=(ffn_w_gate, m_ffn_w_gate, v_ffn_w_gate),
               w_up=(ffn_w_up, m_ffn_w_up, v_ffn_w_up), w_down=(ffn_w_down, m_ffn_w_down, v_ffn_w_down))
    upd = {k: _adamw("adamw_" + k, w, g_big[k], m, v) for k, (w, m, v) in big.items()}
    upd["ada_w"] = _adamw("adamw_ada_w", ada_w, g_ada_w, m_ada_w, v_ada_w)
    g_big["ada_w"] = g_ada_w
    rest = [("ada_b", ada_b, g_ada_b, m_ada_b, v_ada_b), ("norm1_w", norm1_w, sg["norm1_w"], m_norm1_w, v_norm1_w),
            ("attn_sinks", attn_sinks, sg["sinks"], m_attn_sinks, v_attn_sinks),
            ("sgu_ln_w", sgu_ln_w, sg["sgu_ln_w"], m_sgu_ln_w, v_sgu_ln_w),
            ("sgu_ln_b", sgu_ln_b, sg["sgu_ln_b"], m_sgu_ln_b, v_sgu_ln_b), ("sgu_w", sgu_w, sg["sgu_w"], m_sgu_w, v_sgu_w),
            ("sgu_b", sgu_b, sg["sgu_b"], m_sgu_b, v_sgu_b), ("norm2_w", norm2_w, sg["norm2_w"], m_norm2_w, v_norm2_w),
            ("ffn_conv_w", ffn_conv_w, g_conv_w, m_ffn_conv_w, v_ffn_conv_w),
            ("ffn_conv_b", ffn_conv_b, sg["conv_b"], m_ffn_conv_b, v_ffn_conv_b),
            ("final_norm_w", final_norm_w, g_final, m_final_norm_w, v_final_norm_w)]
    rest_shapes = [r[1].shape for r in rest]
    rest_rows = -(-sum(r[1].size for r in rest) // (LANES * ADAM_PACK_ROWS)) * ADAM_PACK_ROWS
    packs = [_pack([r[i] for r in rest], rest_rows) for i in (1, 2, 3, 4)]
    rest_out = [_unpack(a, rest_shapes) for a in _adamw("adamw_rest", *packs)]
    g_rest = {r[0]: r[2] for r in rest}
    u_rest = {r[0]: tuple(o[i] for o in rest_out) for i, r in enumerate(rest)}

    names = ("ada_w", "ada_b", "norm1_w", "w_in", "attn_sinks", "sgu_ln_w", "sgu_ln_b", "sgu_w", "sgu_b", "proj_a", "proj_b",
             "w_out", "norm2_w", "ffn_w_gate", "ffn_w_up", "ffn_conv_w", "ffn_conv_b", "ffn_w_down", "final_norm_w")
    alias = {"ffn_w_gate": "w_gate", "ffn_w_up": "w_up", "ffn_w_down": "w_down"}
    grad_of = lambda n: g_rest[n] if n in g_rest else g_big[alias.get(n, n)]
    upd_of = lambda n: u_rest[n] if n in u_rest else upd[alias.get(n, n)]
    return (loss, dx[None], *[grad_of(n) for n in names], *[upd_of(n)[0] for n in names],
            *[upd_of(n)[1] for n in names], *[upd_of(n)[2] for n in names])
```

```python
import jax
import jax.numpy as jnp
from jax import lax
from jax.experimental import pallas as pl
from jax.experimental.pallas import tpu as pltpu

F32 = jnp.float32
BF = jnp.bfloat16

D_MODEL = 1024
N_Q_HEADS = 16
N_KV_HEADS = 2
HEAD_DIM = 64
ATTN_BLOCK = 128
ROPE_THETA = 500000.0
ROT_DIM = HEAD_DIM // 4
SGU_WIDTH = 1024
SGU_GROUPS = 8
SGU_CHUNK = 128
FFN_DIM = 2816
NORM_EPS = 1e-6
DEPTH = 2
IN_COLS = 5376
N_CHIPS = 4
FFN_SHARD = FFN_DIM // N_CHIPS
FFN_SHARD_PAD = 768
FFN_PAD = N_CHIPS * FFN_SHARD_PAD
LANES = 128
SUBLANES = 8
VMEM_LIMIT = 56 * 1024 * 1024
NEG_BIG = -1e30

ADAM_LR = 0.001
ADAM_B1 = 0.9
ADAM_B2 = 0.999
ADAM_EPS = 1e-08
ADAM_WD = 0.01
ADAM_STEP = 10

MESH = pl.DeviceIdType.MESH

Q_END = 1024
KV_END = 1280
U_END = 2304
Z_END = 3328
GA_END = 4352


def _sds(shape, dtype):
    return jax.ShapeDtypeStruct(tuple(shape), dtype)


def _call(body, name, out_shape, grid, in_specs, out_specs, semantics, scratch=(), after=None):
    n_in = len(in_specs)
    fn = body
    if after is not None:
        def fn(*refs):
            return body(*refs[:n_in], *refs[n_in + 1:])

        in_specs = list(in_specs) + [pl.BlockSpec(memory_space=pl.ANY)]
    call = pl.pallas_call(
        fn, name=name, out_shape=out_shape, grid=grid, in_specs=in_specs, out_specs=out_specs,
        scratch_shapes=scratch,
        compiler_params=pltpu.CompilerParams(dimension_semantics=semantics, vmem_limit_bytes=VMEM_LIMIT))
    if after is None:
        return call
    return lambda *args: call(*args, after)


def _rows(tm, width, col=0):
    return pl.BlockSpec((tm, width), lambda i: (i, col))


def _vec(width):
    return pl.BlockSpec((1, width), lambda i: (0, 0))


def _resident(shape):
    zeros = (0,) * len(shape)
    return pl.BlockSpec(tuple(shape), lambda *_: zeros, pipeline_mode=pl.Buffered(1))


def _sigmoid(x):
    return 1.0 / (1.0 + jnp.exp(-x))


def _gelu(x):
    return 0.5 * x * (1.0 + lax.erf(x * 0.7071067811865476))


def _gelu_grad(x):
    cdf = 0.5 * (1.0 + lax.erf(x * 0.7071067811865476))
    return cdf + x * jnp.exp(-0.5 * x * x) * 0.3989422804014327


def _dot(a, b):
    return jnp.dot(a, b, preferred_element_type=F32)


def _dot_nt(a, b):
    return lax.dot_general(a, b, (((1,), (1,)), ((), ())), preferred_element_type=F32)


def _dot_tn(a, b):
    return lax.dot_general(a, b, (((0,), (0,)), ((), ())), preferred_element_type=F32)


def _rms(xv):
    return lax.rsqrt(jnp.mean(xv * xv, axis=-1, keepdims=True) + NORM_EPS)


def _matmul_tn(name, a, b, tk=512, tn=1024, blocked=False):
    s, k = a.shape
    n = b.shape[1]
    tk, tn = min(tk, k), min(tn, n)

    def body(a_ref, b_ref, o_ref):
        res = _dot_tn(a_ref[...], b_ref[...]).astype(o_ref.dtype)
        if blocked:
            o_ref[0] = res
        else:
            o_ref[...] = res

    if blocked:
        out, ospec = _sds((n // tn, k, tn), BF), pl.BlockSpec((1, tk, tn), lambda i, j: (j, i, 0))
    else:
        out, ospec = _sds((k, n), BF), pl.BlockSpec((tk, tn), lambda i, j: (i, j))
    return _call(body, name, out, (k // tk, n // tn),
                 [pl.BlockSpec((s, tk), lambda i, j: (0, i)), pl.BlockSpec((s, tn), lambda i, j: (0, j))],
                 ospec, ("parallel", "parallel"))(a, b)


def _rope_partner(v):
    lane = lax.broadcasted_iota(jnp.int32, (1, LANES), 1) % HEAD_DIM
    return jnp.where(lane < ROT_DIM // 2, pltpu.roll(v, LANES - ROT_DIM // 2, axis=1), pltpu.roll(v, ROT_DIM // 2, axis=1))


def _dup_half(v, half):
    lane = lax.broadcasted_iota(jnp.int32, (1, LANES), 1)
    keep = jnp.where((lane >= HEAD_DIM) == (half == 1), v, 0.0)
    return keep + pltpu.roll(keep, HEAD_DIM, axis=1)


def _in_proj(name, x, w, sc, sh, w_in, cosf, sinf, tm=256, after=None):
    s, d = x.shape
    tm = min(tm, s)

    def body(x_ref, w_ref, sc_ref, sh_ref, win_ref, cos_ref, sin_ref,
             h_ref, qr_ref, kk0_ref, kk1_ref, vv0_ref, vv1_ref, u_ref, v_ref, ga_ref, gb_ref):
        xv = x_ref[...]
        h = ((xv * _rms(xv)) * w_ref[...] * (1.0 + sc_ref[...]) + sh_ref[...]).astype(BF)
        h_ref[...] = h
        cosv, sinv = cos_ref[...], sin_ref[...]
        q = _dot(h, win_ref[:, :Q_END])
        for j in range(D_MODEL // LANES):
            qv = q[:, j * LANES:(j + 1) * LANES]
            qr_ref[:, j * LANES:(j + 1) * LANES] = (qv * cosv + _rope_partner(qv) * sinv).astype(BF)
        kv = _dot(h, win_ref[:, Q_END:KV_END])
        kr = kv[:, :LANES] * cosv + _rope_partner(kv[:, :LANES]) * sinv
        vv = kv[:, LANES:]
        kk0_ref[...] = _dup_half(kr, 0).astype(BF)
        kk1_ref[...] = _dup_half(kr, 1).astype(BF)
        vv0_ref[...] = _dup_half(vv, 0).astype(BF)
        vv1_ref[...] = _dup_half(vv, 1).astype(BF)
        u_ref[...] = _dot(h, win_ref[:, KV_END:U_END])
        v_ref[...] = _dot(h, win_ref[:, U_END:Z_END])
        ga_ref[...] = _dot(h, win_ref[:, Z_END:GA_END])
        gb_ref[...] = _dot(h, win_ref[:, GA_END:])

    wide, kvs, pre = _sds((s, d), BF), _sds((s, LANES), BF), _sds((s, d), F32)
    return _call(body, name, (wide, wide, kvs, kvs, kvs, kvs, pre, pre, pre, pre), (s // tm,),
                 [_rows(tm, d), _vec(d), _vec(d), _vec(d), _resident(w_in.shape), _rows(tm, LANES), _rows(tm, LANES)],
                 (_rows(tm, d), _rows(tm, d)) + (_rows(tm, LANES),) * 4 + (_rows(tm, d),) * 4, ("parallel",), after=after)(
                     x, w, sc, sh, w_in, cosf, sinf)


def _in_proj_bwd(name, dq, dkv, du, dv, dga, dgb, w_in, x, w, sc, dx_in, tm=256):
    s, d = x.shape
    tm = min(tm, s)

    def body(dq_ref, dkv_ref, du_ref, dv_ref, dga_ref, dgb_ref, win_ref, x_ref, w_ref, sc_ref, dxin_ref,
             dx_ref, da_ref, dsh_ref):
        @pl.when(pl.program_id(0) == 0)
        def _():
            da_ref[...] = jnp.zeros_like(da_ref)
            dsh_ref[...] = jnp.zeros_like(dsh_ref)

        dh = (_dot_nt(dq_ref[...], win_ref[:, :Q_END]) + _dot_nt(dkv_ref[...], win_ref[:, Q_END:KV_END])
              + _dot_nt(du_ref[...], win_ref[:, KV_END:U_END]) + _dot_nt(dv_ref[...], win_ref[:, U_END:Z_END])
              + _dot_nt(dga_ref[...], win_ref[:, Z_END:GA_END]) + _dot_nt(dgb_ref[...], win_ref[:, GA_END:]))
        xv = x_ref[...]
        r = _rms(xv)
        xn = xv * r
        dxn = dh * (w_ref[...] * (1.0 + sc_ref[...]))
        dx_ref[...] = dxin_ref[...] + r * (dxn - xn * jnp.mean(dxn * xn, axis=-1, keepdims=True))
        da_ref[...] += jnp.sum(dh * xn, axis=0, keepdims=True)
        dsh_ref[...] += jnp.sum(dh, axis=0, keepdims=True)

    return _call(body, name, (_sds((s, d), F32), _sds((1, d), F32), _sds((1, d), F32)), (s // tm,),
                 [_rows(tm, d), _rows(tm, 2 * LANES), _rows(tm, d), _rows(tm, d), _rows(tm, d), _rows(tm, d),
                  _resident(w_in.shape), _rows(tm, d), _vec(d), _vec(d), _rows(tm, d)],
                 (_rows(tm, d), _vec(d), _vec(d)), ("arbitrary",))(dq, dkv, du, dv, dga, dgb, w_in, x, w, sc, dx_in)


def _rope_bwd(name, dqr, dkv_cur, dkv_prev, cosf, sinf, tm=512):
    s = dqr.shape[0]
    tm = min(tm, s)
    steps = s // tm
    per = tm // ATTN_BLOCK
    nb = s // ATTN_BLOCK

    def unrope(v, cosv, sinv):
        return v * cosv - _rope_partner(v) * sinv

    def body(dq_ref, cur_ref, prev_ref, next_ref, cos_ref, sin_ref, dqo_ref, dkvo_ref):
        i = pl.program_id(0)
        cosv, sinv = cos_ref[...], sin_ref[...]
        for j in range(D_MODEL // LANES):
            dqo_ref[:, j * LANES:(j + 1) * LANES] = unrope(dq_ref[:, j * LANES:(j + 1) * LANES], cosv, sinv).astype(BF)
        nxt = jnp.where(i < steps - 1, next_ref[...], 0.0)
        if per > 1:
            shifted = jnp.concatenate([prev_ref[ATTN_BLOCK:, :], nxt], axis=0)
        else:
            shifted = nxt
        tot = cur_ref[...] + shifted
        dkvo_ref[:, :LANES] = unrope(tot[:, :LANES], cosv, sinv).astype(BF)
        dkvo_ref[:, LANES:] = tot[:, LANES:].astype(BF)

    nxt_spec = pl.BlockSpec((ATTN_BLOCK, 2 * LANES), lambda i: (jnp.minimum((i + 1) * per, nb - 1), 0))
    return _call(body, name, (_sds((s, D_MODEL), BF), _sds((s, 2 * LANES), BF)), (steps,),
                 [_rows(tm, D_MODEL), _rows(tm, 2 * LANES), _rows(tm, 2 * LANES), nxt_spec, _rows(tm, LANES),
                  _rows(tm, LANES)],
                 (_rows(tm, D_MODEL), _rows(tm, 2 * LANES)), ("parallel",))(dqr, dkv_cur, dkv_prev, dkv_prev, cosf, sinf)


def _band_mask(n):
    qi = lax.broadcasted_iota(jnp.int32, (ATTN_BLOCK, 2 * ATTN_BLOCK), 0)
    kj = lax.broadcasted_iota(jnp.int32, (ATTN_BLOCK, 2 * ATTN_BLOCK), 1)
    return (kj > qi) & (kj <= qi + ATTN_BLOCK) & ((n > 0) | (kj >= ATTN_BLOCK))


def _attn_probs(qa, kb, allowed, sink):
    sc = jnp.where(allowed, _dot_nt(qa, kb) * (HEAD_DIM ** -0.5), NEG_BIG)
    m = jnp.maximum(jnp.max(sc, axis=-1, keepdims=True), sink)
    p = jnp.exp(sc - m)
    esink = jnp.exp(sink - m)
    inv = 1.0 / (jnp.sum(p, axis=-1, keepdims=True) + esink)
    return p * inv, esink * inv


def _kv_specs():
    cur = pl.BlockSpec((ATTN_BLOCK, LANES), lambda n: (n, 0))
    prev = pl.BlockSpec((ATTN_BLOCK, LANES), lambda n: (jnp.maximum(n - 1, 0), 0))
    return [prev, cur] * 4


def _attention(name, qr, kk0, kk1, vv0, vv1, sinks):
    s = qr.shape[0]
    nb = s // ATTN_BLOCK

    def body(sink_ref, q_ref, k0p, k0c, k1p, k1c, v0p, v0c, v1p, v1c, y_ref):
        allowed = _band_mask(pl.program_id(0))
        upper = lax.broadcasted_iota(jnp.int32, (1, LANES), 1) >= HEAD_DIM
        bands = ((jnp.concatenate([k0p[...], k0c[...]], axis=0), jnp.concatenate([v0p[...], v0c[...]], axis=0)),
                 (jnp.concatenate([k1p[...], k1c[...]], axis=0), jnp.concatenate([v1p[...], v1c[...]], axis=0)))
        for hk in range(N_KV_HEADS):
            kb, vb = bands[hk]
            for j in range(4):
                col = (hk * 4 + j) * LANES
                qp = q_ref[:, col:col + LANES]
                out = jnp.zeros((ATTN_BLOCK, LANES), F32)
                for half in range(2):
                    sel = upper if half else jnp.logical_not(upper)
                    qa = jnp.where(sel, qp, jnp.zeros_like(qp))
                    pn, _ = _attn_probs(qa, kb, allowed, sink_ref[hk * 8 + j * 2 + half])
                    out = jnp.where(sel, _dot(pn.astype(BF), vb), out)
                y_ref[:, col:col + LANES] = out.astype(BF)

    return _call(body, name, _sds((s, D_MODEL), BF), (nb,),
                 [pl.BlockSpec(memory_space=pltpu.SMEM), pl.BlockSpec((ATTN_BLOCK, D_MODEL), lambda n: (n, 0))] + _kv_specs(),
                 pl.BlockSpec((ATTN_BLOCK, D_MODEL), lambda n: (n, 0)), ("parallel",))(
                     sinks, qr, kk0, kk0, kk1, kk1, vv0, vv0, vv1, vv1)


def _attention_bwd(name, qr, kk0, kk1, vv0, vv1, sinks, dy):
    s = qr.shape[0]
    nb = s // ATTN_BLOCK

    def body(sink_ref, q_ref, dy_ref, k0p, k0c, k1p, k1c, v0p, v0c, v1p, v1c, dq_ref, cur_ref, prev_ref, dsink_ref):
        @pl.when(pl.program_id(0) == 0)
        def _():
            dsink_ref[...] = jnp.zeros_like(dsink_ref)

        allowed = _band_mask(pl.program_id(0))
        lane = lax.broadcasted_iota(jnp.int32, (1, LANES), 1)
        upper = lane >= HEAD_DIM
        bands = ((jnp.concatenate([k0p[...], k0c[...]], axis=0), jnp.concatenate([v0p[...], v0c[...]], axis=0)),
                 (jnp.concatenate([k1p[...], k1c[...]], axis=0), jnp.concatenate([v1p[...], v1c[...]], axis=0)))
        dsink = jnp.zeros((1, LANES), F32)
        dk_slab = jnp.zeros((2 * ATTN_BLOCK, LANES), F32)
        dv_slab = jnp.zeros((2 * ATTN_BLOCK, LANES), F32)
        for hk in range(N_KV_HEADS):
            kb, vb = bands[hk]
            dkk = jnp.zeros((2 * ATTN_BLOCK, LANES), F32)
            dvv = jnp.zeros((2 * ATTN_BLOCK, LANES), F32)
            for j in range(4):
                col = (hk * 4 + j) * LANES
                qp = q_ref[:, col:col + LANES]
                dyp = dy_ref[:, col:col + LANES]
                dq_pair = jnp.zeros((ATTN_BLOCK, LANES), F32)
                for half in range(2):
                    h = hk * 8 + j * 2 + half
                    sel = upper if half else jnp.logical_not(upper)
                    qa = jnp.where(sel, qp, jnp.zeros_like(qp))
                    pn, psink = _attn_probs(qa, kb, allowed, sink_ref[h])
                    dya = jnp.where(sel, dyp, 0.0).astype(BF)
                    dp = _dot_nt(dya, vb)
                    delta = jnp.sum(pn * dp, axis=-1, keepdims=True)
                    ds = (pn * (dp - delta) * (HEAD_DIM ** -0.5)).astype(BF)
                    dsink = dsink + jnp.where(lane == h, -jnp.sum(psink * delta), 0.0)
                    dq_pair = jnp.where(sel, _dot(ds, kb), dq_pair)
                    dkk = dkk + _dot_tn(ds, qa)
                    dvv = dvv + _dot_tn(pn.astype(BF), dya)
                dq_ref[:, col:col + LANES] = dq_pair
            mine = upper if hk else jnp.logical_not(upper)
            dk_slab = jnp.where(mine, dkk + pltpu.roll(dkk, HEAD_DIM, axis=1), dk_slab)
            dv_slab = jnp.where(mine, dvv + pltpu.roll(dvv, HEAD_DIM, axis=1), dv_slab)
        prev_ref[:, :LANES] = dk_slab[:ATTN_BLOCK]
        prev_ref[:, LANES:] = dv_slab[:ATTN_BLOCK]
        cur_ref[:, :LANES] = dk_slab[ATTN_BLOCK:]
        cur_ref[:, LANES:] = dv_slab[ATTN_BLOCK:]
        dsink_ref[...] += dsink

    blk = pl.BlockSpec((ATTN_BLOCK, D_MODEL), lambda n: (n, 0))
    kvo = pl.BlockSpec((ATTN_BLOCK, 2 * LANES), lambda n: (n, 0))
    return _call(body, name,
                 (_sds((s, D_MODEL), F32), _sds((s, 2 * LANES), F32), _sds((s, 2 * LANES), F32), _sds((1, LANES), F32)),
                 (nb,), [pl.BlockSpec(memory_space=pltpu.SMEM), blk, blk] + _kv_specs(),
                 (blk, kvo, kvo, pl.BlockSpec((1, LANES), lambda n: (0, 0))), ("arbitrary",))(
                     sinks, qr, dy, kk0, kk0, kk1, kk1, vv0, vv0, vv1, vv1)


def _sgu_weights(wm_ref, g):
    t = lax.broadcasted_iota(jnp.int32, (SGU_CHUNK, SGU_CHUNK), 0)
    sidx = lax.broadcasted_iota(jnp.int32, (SGU_CHUNK, SGU_CHUNK), 1)
    return jnp.where(sidx <= t, wm_ref[g], 0.0).astype(BF)


def _layer_norm_stats(v):
    mu = jnp.mean(v, axis=-1, keepdims=True)
    cen = v - mu
    rstd = lax.rsqrt(jnp.mean(cen * cen, axis=-1, keepdims=True) + NORM_EPS)
    return cen * rstd, rstd


def _sgu(name, u_pre, v_pre, ln_w, ln_b, wm, bfull, tm=256):
    s, w = u_pre.shape
    tm = min(tm, s)

    def body(u_ref, v_ref, lw_ref, lb_ref, wm_ref, b_ref, y_ref):
        vhat, _ = _layer_norm_stats(_gelu(v_ref[...]))
        vn = (vhat * lw_ref[...] + lb_ref[...]).astype(BF)
        for g in range(SGU_GROUPS):
            wg = _sgu_weights(wm_ref, g)
            cols = slice(g * SGU_CHUNK, (g + 1) * SGU_CHUNK)
            for ch in range(tm // SGU_CHUNK):
                rows = slice(ch * SGU_CHUNK, (ch + 1) * SGU_CHUNK)
                f = _dot(wg, vn[rows, cols]) + b_ref[g]
                y_ref[rows, cols] = (_gelu(u_ref[rows, cols]) * f).astype(BF)

    full3 = pl.BlockSpec((SGU_GROUPS, SGU_CHUNK, SGU_CHUNK), lambda i: (0, 0, 0))
    return _call(body, name, _sds((s, w), BF), (s // tm,),
                 [_rows(tm, w), _rows(tm, w), _vec(w), _vec(w), full3, full3],
                 _rows(tm, w), ("parallel",))(u_pre, v_pre, ln_w, ln_b, wm, bfull)


def _sgu_bwd(name, u_pre, v_pre, ln_w, ln_b, wm, bfull, dy, tm=256):
    s, w = u_pre.shape
    tm = min(tm, s)
    steps = s // tm

    def body(u_ref, v_ref, lw_ref, lb_ref, wm_ref, b_ref, dy_ref, du_ref, dv_ref, dwm_ref, db_ref, dlw_ref, dlb_ref,
             dfsum_ref):
        i = pl.program_id(0)

        @pl.when(i == 0)
        def _():
            dwm_ref[...] = jnp.zeros_like(dwm_ref)
            dlw_ref[...] = jnp.zeros_like(dlw_ref)
            dlb_ref[...] = jnp.zeros_like(dlb_ref)
            dfsum_ref[...] = jnp.zeros_like(dfsum_ref)

        vpre = v_ref[...]
        vhat, rstd = _layer_norm_stats(_gelu(vpre))
        vn = (vhat * lw_ref[...] + lb_ref[...]).astype(BF)
        t = lax.broadcasted_iota(jnp.int32, (SGU_CHUNK, SGU_CHUNK), 0)
        sidx = lax.broadcasted_iota(jnp.int32, (SGU_CHUNK, SGU_CHUNK), 1)
        dvn_cols = []
        for g in range(SGU_GROUPS):
            wg = _sgu_weights(wm_ref, g)
            cols = slice(g * SGU_CHUNK, (g + 1) * SGU_CHUNK)
            dvn_rows = []
            dwg = jnp.zeros((SGU_CHUNK, SGU_CHUNK), F32)
            dfs = jnp.zeros((SGU_CHUNK, SGU_CHUNK), F32)
            for ch in range(tm // SGU_CHUNK):
                rows = slice(ch * SGU_CHUNK, (ch + 1) * SGU_CHUNK)
                upre = u_ref[rows, cols]
                dyv = dy_ref[rows, cols]
                f = _dot(wg, vn[rows, cols]) + b_ref[g]
                du_ref[rows, cols] = (dyv * f * _gelu_grad(upre)).astype(BF)
                df = dyv * _gelu(upre)
                dfb = df.astype(BF)
                dvn_rows.append(_dot_tn(wg, dfb))
                dwg = dwg + _dot_nt(dfb, vn[rows, cols])
                dfs = dfs + df
            dwm_ref[g] += jnp.where(sidx <= t, dwg, 0.0)
            dfsum_ref[g] += dfs
            dvn_cols.append(jnp.concatenate(dvn_rows, axis=0) if len(dvn_rows) > 1 else dvn_rows[0])
        dvn = jnp.concatenate(dvn_cols, axis=1)
        dlw_ref[...] += jnp.sum(dvn * vhat, axis=0, keepdims=True)
        dlb_ref[...] += jnp.sum(dvn, axis=0, keepdims=True)
        dvh = dvn * lw_ref[...]
        dvg = rstd * (dvh - jnp.mean(dvh, axis=-1, keepdims=True) - vhat * jnp.mean(dvh * vhat, axis=-1, keepdims=True))
        dv_ref[...] = (dvg * _gelu_grad(vpre)).astype(BF)

        @pl.when(i == steps - 1)
        def _():
            for g in range(SGU_GROUPS):
                db_ref[g:g + 1, :] = jnp.sum(dfsum_ref[g].T, axis=0, keepdims=True)

    full3 = pl.BlockSpec((SGU_GROUPS, SGU_CHUNK, SGU_CHUNK), lambda i: (0, 0, 0))
    return _call(body, name,
                 (_sds((s, w), BF), _sds((s, w), BF), _sds((SGU_GROUPS, SGU_CHUNK, SGU_CHUNK), F32),
                  _sds((SGU_GROUPS, SGU_CHUNK), F32), _sds((1, w), F32), _sds((1, w), F32)),
                 (steps,),
                 [_rows(tm, w), _rows(tm, w), _vec(w), _vec(w), full3, full3, _rows(tm, w)],
                 (_rows(tm, w), _rows(tm, w), full3, pl.BlockSpec((SGU_GROUPS, SGU_CHUNK), lambda i: (0, 0)), _vec(w), _vec(w)),
                 ("arbitrary",), scratch=[pltpu.VMEM((SGU_GROUPS, SGU_CHUNK, SGU_CHUNK), F32)])(
                     u_pre, v_pre, ln_w, ln_b, wm, bfull, dy)


def _mix_out(name, y_sgu, y_attn, ga_pre, gb_pre, x, g1, proj_a, proj_b, w_out, w2, sc2, sh2, tm=256):
    s, d = x.shape
    tm = min(tm, s)

    def body(ys_ref, ya_ref, ga_ref, gb_ref, x_ref, g1_ref, wa_ref, wb_ref, wo_ref, w2_ref, sc2_ref, sh2_ref,
             m_ref, pa_ref, pb_ref, o_ref, x1_ref, h2_ref):
        pa = _dot(ys_ref[...], wa_ref[...].reshape(d, d))
        pb = _dot(ya_ref[...], wb_ref[...].reshape(d, d))
        pa_ref[...] = pa
        pb_ref[...] = pb
        merged = (_sigmoid(ga_ref[...]) * pa + _sigmoid(gb_ref[...]) * pb).astype(BF)
        m_ref[...] = merged
        o = _dot(merged, wo_ref[...].reshape(d, d))
        o_ref[...] = o
        x1 = x_ref[...] + g1_ref[...] * o
        x1_ref[...] = x1
        h2_ref[...] = ((x1 * _rms(x1)) * w2_ref[...] * (1.0 + sc2_ref[...]) + sh2_ref[...]).astype(BF)

    f, b = _sds((s, d), F32), _sds((s, d), BF)
    r = _rows(tm, d)
    wspec = _resident(proj_a.shape)
    return _call(body, name, (b, f, f, f, f, b), (s // tm,),
                 [r, r, r, r, r, _vec(d), wspec, wspec, wspec, _vec(d), _vec(d), _vec(d)], (r,) * 6, ("parallel",))(
                     y_sgu, y_attn, ga_pre, gb_pre, x, g1, proj_a, proj_b, w_out, w2, sc2, sh2)


def _mix_bwd(name, do, w_out, proj_a, proj_b, ga_pre, gb_pre, pa, pb, tm=256):
    s, d = do.shape
    tm = min(tm, s)

    def body(do_ref, wo_ref, wa_ref, wb_ref, ga_ref, gb_ref, pa_ref, pb_ref,
             dpa_ref, dpb_ref, dga_ref, dgb_ref, dys_ref, dya_ref):
        dm = _dot_nt(do_ref[...], wo_ref[...].reshape(d, d))
        ga = _sigmoid(ga_ref[...])
        gb = _sigmoid(gb_ref[...])
        dpa = (dm * ga).astype(BF)
        dpb = (dm * gb).astype(BF)
        dpa_ref[...] = dpa
        dpb_ref[...] = dpb
        dga_ref[...] = (dm * pa_ref[...] * ga * (1.0 - ga)).astype(BF)
        dgb_ref[...] = (dm * pb_ref[...] * gb * (1.0 - gb)).astype(BF)
        dys_ref[...] = _dot_nt(dpa, wa_ref[...].reshape(d, d))
        dya_ref[...] = _dot_nt(dpb, wb_ref[...].reshape(d, d))

    f, b = _sds((s, d), F32), _sds((s, d), BF)
    r = _rows(tm, d)
    wspec = _resident(w_out.shape)
    return _call(body, name, (b, b, b, b, f, f), (s // tm,), [r, wspec, wspec, wspec, r, r, r, r], (r,) * 6,
                 ("parallel",))(do, w_out, proj_a, proj_b, ga_pre, gb_pre, pa, pb)


def _ffn_up(name, h2, w_gate, w_up, tm=1024):
    s, d = h2.shape
    tm = min(tm, s)
    tc = FFN_SHARD_PAD

    def body(h_ref, wg_ref, wu_ref, a_ref, up_ref):
        hv = h_ref[...]
        a_ref[...] = _dot(hv, wg_ref[0])
        up_ref[...] = _dot(hv, wu_ref[0])

    wspec = pl.BlockSpec((1, d, tc), lambda j, i: (j, 0, 0))
    ospec = pl.BlockSpec((tm, tc), lambda j, i: (i, j))
    o = _sds((s, FFN_PAD), F32)
    return _call(body, name, (o, o), (N_CHIPS, s // tm), [pl.BlockSpec((tm, d), lambda j, i: (i, 0)), wspec, wspec],
                 (ospec, ospec), ("parallel", "parallel"))(h2, w_gate, w_up)


def _conv_act(name, a, up, cw, cb, tm=512, tc=FFN_SHARD_PAD):
    s, c = a.shape
    tm = min(tm, s)
    per = tm // SUBLANES

    def body(a_ref, prev_ref, up_ref, cw_ref, cb_ref, h_ref):
        prev = jnp.where(pl.program_id(1) > 0, prev_ref[...], 0.0)
        ext = jnp.concatenate([prev, a_ref[...]], axis=0)
        ac = (cb_ref[...] + cw_ref[0:1, :] * pltpu.roll(ext, 2, axis=0) + cw_ref[1:2, :] * pltpu.roll(ext, 1, axis=0)
              + cw_ref[2:3, :] * ext)[SUBLANES:]
        h_ref[...] = (ac * _sigmoid(ac) * up_ref[...]).astype(BF)

    tile = pl.BlockSpec((tm, tc), lambda j, i: (i, j))
    prev = pl.BlockSpec((SUBLANES, tc), lambda j, i: (jnp.maximum(i * per - 1, 0), j))
    return _call(body, name, _sds((s, c), BF), (c // tc, s // tm),
                 [tile, prev, tile, pl.BlockSpec((3, tc), lambda j, i: (0, j)), pl.BlockSpec((1, tc), lambda j, i: (0, j))],
                 tile, ("parallel", "parallel"))(a, a, up, cw, cb)


def _conv_act_bwd(name, a, up, dhf, cw, cb, tm=512, tc=FFN_SHARD_PAD):
    s, c = a.shape
    tm = min(tm, s)
    per = tm // SUBLANES
    steps = s // tm
    last8 = s // SUBLANES - 1

    def body(a_ref, aprev_ref, anext_ref, up_ref, upnext_ref, dh_ref, dhnext_ref, cw_ref, cb_ref,
             da_ref, dup_ref, dcw_ref, dcb_ref):
        i = pl.program_id(1)

        @pl.when(i == 0)
        def _():
            dcw_ref[...] = jnp.zeros_like(dcw_ref)
            dcb_ref[...] = jnp.zeros_like(dcb_ref)

        prev = jnp.where(i > 0, aprev_ref[...], 0.0)
        ext = jnp.concatenate([prev, a_ref[...], anext_ref[...]], axis=0)
        a1 = pltpu.roll(ext, 1, axis=0)[SUBLANES:]
        a2 = pltpu.roll(ext, 2, axis=0)[SUBLANES:]
        a0 = ext[SUBLANES:]
        ac = cb_ref[...] + cw_ref[0:1, :] * a2 + cw_ref[1:2, :] * a1 + cw_ref[2:3, :] * a0
        sig = _sigmoid(ac)
        dh = jnp.concatenate([dh_ref[...], jnp.where(i < steps - 1, dhnext_ref[...], 0.0)], axis=0)
        upe = jnp.concatenate([up_ref[...], upnext_ref[...]], axis=0)
        dac = dh * upe * (sig * (1.0 + ac * (1.0 - sig)))
        dup_ref[...] = (dh[:tm] * (ac * sig)[:tm]).astype(BF)
        n = tm + SUBLANES
        da = (cw_ref[2:3, :] * dac + cw_ref[1:2, :] * pltpu.roll(dac, n - 1, axis=0)
              + cw_ref[0:1, :] * pltpu.roll(dac, n - 2, axis=0))
        da_ref[...] = da[:tm].astype(BF)
        dact = dac[:tm]
        dcb_ref[...] += jnp.sum(dact, axis=0, keepdims=True)
        dcw_ref[0:1, :] += jnp.sum(dact * a2[:tm], axis=0, keepdims=True)
        dcw_ref[1:2, :] += jnp.sum(dact * a1[:tm], axis=0, keepdims=True)
        dcw_ref[2:3, :] += jnp.sum(dact * a0[:tm], axis=0, keepdims=True)

    tile = pl.BlockSpec((tm, tc), lambda j, i: (i, j))
    prev = pl.BlockSpec((SUBLANES, tc), lambda j, i: (jnp.maximum(i * per - 1, 0), j))
    nxt = pl.BlockSpec((SUBLANES, tc), lambda j, i: (jnp.minimum((i + 1) * per, last8), j))
    cw_spec = pl.BlockSpec((3, tc), lambda j, i: (0, j))
    cb_spec = pl.BlockSpec((1, tc), lambda j, i: (0, j))
    return _call(body, name, (_sds((s, c), BF), _sds((s, c), BF), _sds((3, c), F32), _sds((1, c), F32)),
                 (c // tc, steps), [tile, prev, nxt, tile, nxt, tile, nxt, cw_spec, cb_spec],
                 (tile, tile, cw_spec, cb_spec), ("parallel", "arbitrary"))(a, a, a, up, up, dhf, dhf, cw, cb)


def _ffn_down(name, hf, w_down, x1, g2, tm=512):
    s, d = x1.shape
    tm = min(tm, s)

    def body(hf_ref, wd_ref, x1_ref, g2_ref, dn_ref, x2_ref):
        dn = _dot(hf_ref[...], wd_ref[...].reshape(FFN_PAD, d))
        dn_ref[...] = dn
        x2_ref[...] = x1_ref[...] + g2_ref[...] * dn

    o = _sds((s, d), F32)
    return _call(body, name, (o, o), (s // tm,),
                 [_rows(tm, FFN_PAD), _resident(w_down.shape), _rows(tm, d), _vec(d)],
                 (_rows(tm, d), _rows(tm, d)), ("parallel",))(hf, w_down, x1, g2)


def _ffn_down_bwd(name, dx2, dn, g2, w_down, tm=512, after=None):
    s, d = dx2.shape
    tm = min(tm, s)
    tc = FFN_SHARD_PAD

    def body(dx_ref, dn_ref, g2_ref, wd_ref, ddn_ref, dhf_ref, dg_ref):
        @pl.when(pl.program_id(0) == 0)
        def _():
            dg_ref[...] = jnp.zeros_like(dg_ref)

        dxv = dx_ref[...]
        ddn = (dxv * g2_ref[...]).astype(BF)
        ddn_ref[...] = ddn
        dg_ref[...] += jnp.sum(dxv * dn_ref[...], axis=0, keepdims=True)
        for k in range(N_CHIPS):
            dhf_ref[:, k * tc:(k + 1) * tc] = _dot_nt(ddn, wd_ref[k])

    return _call(body, name, (_sds((s, d), BF), _sds((s, FFN_PAD), F32), _sds((1, d), F32)), (s // tm,),
                 [_rows(tm, d), _rows(tm, d), _vec(d), _resident(w_down.shape)],
                 (_rows(tm, d), _rows(tm, FFN_PAD), _vec(d)), ("arbitrary",), after=after)(dx2, dn, g2, w_down)


def _ffn_up_bwd(name, da, dup, w_gate, w_up, x1, dx2, w2, sc2, o, g1, tm=256):
    s, d = x1.shape
    tm = min(tm, s)
    tc = FFN_SHARD_PAD

    def body(da_ref, dup_ref, wg_ref, wu_ref, x1_ref, dx2_ref, w2_ref, sc2_ref, o_ref, g1_ref,
             dx1_ref, do_ref, dnw_ref, dsh_ref, dg1_ref):
        @pl.when(pl.program_id(0) == 0)
        def _():
            dnw_ref[...] = jnp.zeros_like(dnw_ref)
            dsh_ref[...] = jnp.zeros_like(dsh_ref)
            dg1_ref[...] = jnp.zeros_like(dg1_ref)

        dh = jnp.zeros((tm, d), F32)
        for k in range(N_CHIPS):
            cols = slice(k * tc, (k + 1) * tc)
            dh = dh + _dot_nt(da_ref[:, cols], wg_ref[k]) + _dot_nt(dup_ref[:, cols], wu_ref[k])
        xv = x1_ref[...]
        r = _rms(xv)
        xn = xv * r
        dxn = dh * (w2_ref[...] * (1.0 + sc2_ref[...]))
        dx1 = dx2_ref[...] + r * (dxn - xn * jnp.mean(dxn * xn, axis=-1, keepdims=True))
        dx1_ref[...] = dx1
        dnw_ref[...] += jnp.sum(dh * xn, axis=0, keepdims=True)
        dsh_ref[...] += jnp.sum(dh, axis=0, keepdims=True)
        do_ref[...] = (dx1 * g1_ref[...]).astype(BF)
        dg1_ref[...] += jnp.sum(dx1 * o_ref[...], axis=0, keepdims=True)

    v = _sds((1, d), F32)
    r = _rows(tm, d)
    wspec = _resident(w_gate.shape)
    return _call(body, name, (_sds((s, d), F32), _sds((s, d), BF), v, v, v), (s // tm,),
                 [_rows(tm, FFN_PAD), _rows(tm, FFN_PAD), wspec, wspec, r, r, _vec(d), _vec(d), r, _vec(d)],
                 (r, r, _vec(d), _vec(d), _vec(d)), ("arbitrary",))(da, dup, w_gate, w_up, x1, dx2, w2, sc2, o, g1)


def _loss_head(name, x, w, target, tm=512):
    s, d = x.shape
    tm = min(tm, s)

    def body(x_ref, w_ref, t_ref, dx_ref, loss_ref, dw_ref):
        @pl.when(pl.program_id(0) == 0)
        def _():
            loss_ref[...] = jnp.zeros_like(loss_ref)
            dw_ref[...] = jnp.zeros_like(dw_ref)

        xv = x_ref[...]
        r = _rms(xv)
        xn = xv * r
        err = xn * w_ref[...] - t_ref[...]
        loss_ref[...] += 0.5 * jnp.sum(jnp.mean(err * err, axis=-1, keepdims=True))
        dy = err * (1.0 / d)
        dw_ref[...] += jnp.sum(dy * xn, axis=0, keepdims=True)
        dxn = dy * w_ref[...]
        dx_ref[...] = r * (dxn - xn * jnp.mean(dxn * xn, axis=-1, keepdims=True))

    return _call(body, name, (_sds((s, d), F32), _sds((1, LANES), F32), _sds((1, d), F32)), (s // tm,),
                 [_rows(tm, d), _vec(d), _rows(tm, d)], (_rows(tm, d), _vec(LANES), _vec(d)), ("arbitrary",))(x, w, target)


def _layer_fwd(l, x, mod, p, cosf, sinf, after=None):
    sh1, sc1, g1, sh2, sc2, g2 = mod
    tag = f"l{l}_"
    h, qr, kk0, kk1, vv0, vv1, u_pre, v_pre, ga_pre, gb_pre = _in_proj(
        tag + "in_proj", x, p["norm1_w"], sc1, sh1, p["w_in"], cosf, sinf, after=after)
    y_attn = _attention(tag + "attn", qr, kk0, kk1, vv0, vv1, p["sinks"])
    y_sgu = _sgu(tag + "sgu", u_pre, v_pre, p["sgu_ln_w"], p["sgu_ln_b"], p["sgu_w"], p["sgu_bfull"])
    merged, pa, pb, o, x1, h2 = _mix_out(tag + "mix_out", y_sgu, y_attn, ga_pre, gb_pre, x, g1, p["proj_a"], p["proj_b"],
                                         p["w_out"], p["norm2_w"], sc2, sh2)
    a, up = _ffn_up(tag + "ffn_up", h2, p["w_gate"], p["w_up"])
    hf = _conv_act(tag + "conv_act", a, up, p["conv_w"], p["conv_b"])
    dn, x2 = _ffn_down(tag + "ffn_down", hf, p["w_down"], x1, g2)
    saved = dict(x=x, h=h, qr=qr, kk0=kk0, kk1=kk1, vv0=vv0, vv1=vv1, u_pre=u_pre, v_pre=v_pre, ga_pre=ga_pre,
                 gb_pre=gb_pre, y_attn=y_attn, y_sgu=y_sgu, merged=merged, pa=pa, pb=pb, o=o, x1=x1, h2=h2, a=a, up=up,
                 hf=hf, dn=dn)
    return x2, saved


def _layer_bwd(l, dx2, mod, p, sv, cosf, sinf, after=None):
    sh1, sc1, g1, sh2, sc2, g2 = mod
    tag = f"l{l}_b_"
    d = D_MODEL
    g = {}
    ddn, dhf, dg2 = _ffn_down_bwd(tag + "ffn_down", dx2, sv["dn"], g2, p["w_down"], after=after)
    g["w_down"] = _matmul_tn(tag + "dw_down", sv["hf"], ddn, tk=FFN_SHARD_PAD).reshape(N_CHIPS, FFN_SHARD_PAD, d)
    da, dup, g["conv_w"], g["conv_b"] = _conv_act_bwd(tag + "conv_act", sv["a"], sv["up"], dhf, p["conv_w"], p["conv_b"])
    g["w_gate"] = _matmul_tn(tag + "dw_gate", sv["h2"], da, tn=FFN_SHARD_PAD, blocked=True)
    g["w_up"] = _matmul_tn(tag + "dw_up", sv["h2"], dup, tn=FFN_SHARD_PAD, blocked=True)
    dx1, do, da2, dsh2, dg1 = _ffn_up_bwd(tag + "ffn_up", da, dup, p["w_gate"], p["w_up"], sv["x1"], dx2, p["norm2_w"],
                                          sc2, sv["o"], g1)
    g["norm2_w"] = da2 * (1.0 + sc2)
    dsc2 = da2 * p["norm2_w"]
    g["w_out"] = _matmul_tn(tag + "dw_out", sv["merged"], do).reshape(N_CHIPS, d // N_CHIPS, d)
    dpa, dpb, dga, dgb, dy_sgu, dy_attn = _mix_bwd(tag + "mix", do, p["w_out"], p["proj_a"], p["proj_b"], sv["ga_pre"],
                                                  sv["gb_pre"], sv["pa"], sv["pb"])
    g["proj_a"] = _matmul_tn(tag + "dproj_a", sv["y_sgu"], dpa).reshape(N_CHIPS, d // N_CHIPS, d)
    g["proj_b"] = _matmul_tn(tag + "dproj_b", sv["y_attn"], dpb).reshape(N_CHIPS, d // N_CHIPS, d)
    du, dv, g["sgu_w"], g["sgu_b"], g["sgu_ln_w"], g["sgu_ln_b"] = _sgu_bwd(
        tag + "sgu", sv["u_pre"], sv["v_pre"], p["sgu_ln_w"], p["sgu_ln_b"], p["sgu_w"], p["sgu_bfull"], dy_sgu)
    dqr, dkv_cur, dkv_prev, dsink = _attention_bwd(tag + "attn", sv["qr"], sv["kk0"], sv["kk1"], sv["vv0"], sv["vv1"],
                                                   p["sinks"], dy_attn)
    g["sinks"] = dsink[0, :N_Q_HEADS]
    dq, dkv = _rope_bwd(tag + "rope", dqr, dkv_cur, dkv_prev, cosf, sinf)
    dw_in = [_matmul_tn(tag + "dw_in_" + n, sv["h"], t) for n, t in
             (("q", dq), ("kv", dkv), ("u", du), ("v", dv), ("ga", dga), ("gb", dgb))]
    g["w_in"] = jnp.concatenate(dw_in, axis=1)
    dx, da1, dsh1 = _in_proj_bwd(tag + "in_proj", dq, dkv, du, dv, dga, dgb, p["w_in"], sv["x"], p["norm1_w"], sc1, dx1)
    g["norm1_w"] = da1 * (1.0 + sc1)
    dsc1 = da1 * p["norm1_w"]
    return dx, (dsh1, dsc1, dg1, dsh2, dsc2, dg2), g


def _pad_to(a, axis, size):
    pad = [(0, 0)] * a.ndim
    pad[axis] = (0, size - a.shape[axis])
    return jnp.pad(a, pad)


def _layer_params(w_in, proj_a, proj_b, w_out, w_gate, w_up, w_down, conv_w, small):
    d = D_MODEL
    return dict(
        w_in=w_in.transpose(1, 0, 2).reshape(d, IN_COLS), proj_a=proj_a, proj_b=proj_b, w_out=w_out, w_gate=w_gate, w_up=w_up,
        w_down=w_down, conv_w=conv_w.transpose(1, 0, 2).reshape(3, FFN_PAD),
        conv_b=_pad_to(small["conv_b"].reshape(N_CHIPS, FFN_SHARD), 1, FFN_SHARD_PAD).reshape(1, FFN_PAD),
        norm1_w=small["norm1_w"].reshape(1, d), norm2_w=small["norm2_w"].reshape(1, d), sinks=small["sinks"],
        sgu_ln_w=small["sgu_ln_w"].reshape(1, d), sgu_ln_b=small["sgu_ln_b"].reshape(1, d), sgu_w=small["sgu_w"],
        sgu_bfull=jnp.broadcast_to(small["sgu_b"][:, :, None], (SGU_GROUPS, SGU_CHUNK, SGU_CHUNK)))


def _conv_grads_natural(g):
    cw = g["conv_w"].reshape(3, N_CHIPS, FFN_SHARD_PAD)[:, :, :FFN_SHARD].reshape(3, FFN_DIM)
    cb = g["conv_b"].reshape(N_CHIPS, FFN_SHARD_PAD)[:, :FFN_SHARD].reshape(FFN_DIM)
    return cw, cb


def _rope_tables(positions):
    inv_freq = ROPE_THETA ** (-jnp.arange(0, ROT_DIM, 2, dtype=F32) / ROT_DIM)
    ang = positions.astype(F32)[:, None] * inv_freq
    cos, sin = jnp.cos(ang), jnp.sin(ang)
    s = positions.shape[0]
    rest = HEAD_DIM - ROT_DIM
    cos_head = jnp.concatenate([cos, cos, jnp.ones((s, rest), F32)], axis=1)
    sin_head = jnp.concatenate([-sin, sin, jnp.zeros((s, rest), F32)], axis=1)
    return jnp.tile(cos_head, (1, LANES // HEAD_DIM)), jnp.tile(sin_head, (1, LANES // HEAD_DIM))


ADA_ROWS = 16


def _ada_fwd(name, c_rows, ada_w, ada_b_cols, tn=512):
    depth, d, n = ada_w.shape

    def body(c_ref, w_ref, b_ref, o_ref):
        cv = c_ref[...]
        act = (cv * _sigmoid(cv)).astype(BF)
        o_ref[0] = _dot(act, w_ref[0].astype(BF)) + b_ref[0]

    return _call(body, name, _sds((depth, ADA_ROWS, n), F32), (depth, n // tn),
                 [pl.BlockSpec((ADA_ROWS, d), lambda l, j: (0, 0)), pl.BlockSpec((1, d, tn), lambda l, j: (l, 0, j)),
                  pl.BlockSpec((1, 1, tn), lambda l, j: (l, 0, j))],
                 pl.BlockSpec((1, ADA_ROWS, tn), lambda l, j: (l, 0, j)), ("parallel", "parallel"))(c_rows, ada_w, ada_b_cols)


def _ada_bwd(name, c_rows, dmod_cols, tn=512):
    depth, _, n = dmod_cols.shape
    d = c_rows.shape[1]

    def body(c_ref, dm_ref, o_ref):
        cv = c_ref[...]
        act = (cv * _sigmoid(cv)).astype(BF)
        o_ref[0] = _dot_tn(act, dm_ref[0].astype(BF))

    return _call(body, name, _sds((depth, d, n), F32), (depth, n // tn),
                 [pl.BlockSpec((ADA_ROWS, d), lambda l, j: (0, 0)), pl.BlockSpec((1, ADA_ROWS, tn), lambda l, j: (l, 0, j))],
                 pl.BlockSpec((1, d, tn), lambda l, j: (l, 0, j)), ("parallel", "parallel"))(c_rows, dmod_cols)


def _colsum(name, a):
    r, n = a.shape

    def body(a_ref, o_ref):
        o_ref[...] = jnp.sum(a_ref[...], axis=0, keepdims=True)

    return _call(body, name, _sds((1, n), F32), (1,), [pl.BlockSpec((r, n), lambda i: (0, 0))],
                 pl.BlockSpec((1, n), lambda i: (0, 0)), ("arbitrary",))(a)


REL_SIBLING = (0, 0, 1)
REL_CHIPS = ((1, 0, 0), (0, 1, 0), (1, 1, 0))
REL_ALL = tuple((fx, fy, fc) for fx in (0, 1) for fy in (0, 1) for fc in (0, 1) if fx or fy or fc)


def _chip_of(dev):
    return 2 * dev[0] + dev[1]


def _dev_of(dev):
    return 4 * dev[0] + 2 * dev[1] + dev[2]


def _flip(dev, rel):
    return tuple(1 - m if f else m for m, f in zip(dev, rel))


def _exchange(name, arrays, n_out, stages, aliases=None):
    out_shapes, stages = stages[0], stages[1:]
    n_in = len(arrays)
    aliases = aliases or {}
    n_remote = sum(len(plan) for plan, _ in stages)
    n_local = sum(len(local) for _, local in stages)

    def at(ref, idx):
        return ref.at[idx] if len(idx) else ref

    def body(*refs):
        bufs = list(refs[:n_in + n_out])
        for i_in, i_out in aliases.items():
            bufs[i_in] = bufs[n_in + i_out]
        send_sems, recv_sems, local_sems = refs[n_in + n_out:]
        me = (lax.axis_index("x"), lax.axis_index("y"), lax.axis_index("c"))
        base_r = base_l = 0
        pending = []
        for plan, local in stages:
            def remote(k, entry, sender, receiver):
                rel, si, ssel, di, dsel = entry
                return pltpu.make_async_remote_copy(
                    src_ref=at(bufs[si], ssel(sender, receiver)), dst_ref=at(bufs[di], dsel(sender, receiver)),
                    send_sem=send_sems.at[k], recv_sem=recv_sems.at[k], device_id=_flip(me, rel), device_id_type=MESH)

            sends = [remote(base_r + k, e, me, _flip(me, e[0])) for k, e in enumerate(plan)]
            for cp in sends:
                cp.start()
            for k, (si, ssel, di, dsel) in enumerate(local):
                cp = pltpu.make_async_copy(at(bufs[si], ssel(me)), at(bufs[di], dsel(me)), local_sems.at[base_l + k])
                cp.start()
                pending.append(cp.wait)
            for k, e in enumerate(plan):
                remote(base_r + k, e, _flip(me, e[0]), me).wait_recv()
            pending += [cp.wait_send for cp in sends]
            base_r += len(plan)
            base_l += len(local)
        for wait in pending:
            wait()

    any_spec = pl.BlockSpec(memory_space=pl.ANY)
    return pl.pallas_call(
        body, name=name, out_shape=tuple(out_shapes), in_specs=[any_spec] * n_in, out_specs=tuple([any_spec] * n_out),
        input_output_aliases=dict(aliases),
        scratch_shapes=[pltpu.SemaphoreType.DMA((max(n_remote, 1),)), pltpu.SemaphoreType.DMA((max(n_remote, 1),)),
                        pltpu.SemaphoreType.DMA((max(n_local, 1),))])(*arrays)


HBM_SPEC = pl.BlockSpec(memory_space=pltpu.HBM)
SEM_SPEC = pl.BlockSpec(memory_space=pltpu.SEMAPHORE)


def _split_copies(bufs, plan, local, send_sems, recv_sems, local_sems):
    me = (lax.axis_index("x"), lax.axis_index("y"), lax.axis_index("c"))

    def at(ref, idx):
        return ref.at[idx] if len(idx) else ref

    def remote(k, sender, receiver):
        rel, si, ssel, di, dsel = plan[k]
        return pltpu.make_async_remote_copy(
            src_ref=at(bufs[si], ssel(sender, receiver)), dst_ref=at(bufs[di], dsel(sender, receiver)),
            send_sem=send_sems.at[k], recv_sem=recv_sems.at[k], device_id=_flip(me, rel), device_id_type=MESH)

    sends = [remote(k, me, _flip(me, plan[k][0])) for k in range(len(plan))]
    arrivals = [remote(k, _flip(me, plan[k][0]), me) for k in range(len(plan))]
    locs = [pltpu.make_async_copy(at(bufs[si], ssel(me)), at(bufs[di], dsel(me)), local_sems.at[k])
            for k, (si, ssel, di, dsel) in enumerate(local)]
    return sends, arrivals, locs


def _exchange_start(name, arrays, out_shapes, plan, local):
    n_in, n_out = len(arrays), len(out_shapes)

    n_buf = n_in + n_out

    def body(*refs):
        sems = refs[n_buf:n_buf + 3]
        bufs = refs[n_buf + 3:2 * n_buf + 3]
        token = refs[-1]
        sends, _, locs = _split_copies(bufs, plan, local, *sems)
        for cp in sends + locs:
            cp.start()
        token[...] = jnp.zeros_like(token)

    zones = [lax.empty(o.shape, o.dtype) for o in out_shapes]
    operands = [pltpu.with_memory_space_constraint(a, pltpu.HBM) for a in list(arrays) + zones]
    sem = lambda n: pltpu.SemaphoreType.DMA((max(n, 1),))
    out = pl.pallas_call(
        body, name=name,
        out_shape=(sem(len(plan)), sem(len(plan)), sem(len(local)), *[pltpu.HBM(a.shape, a.dtype) for a in operands],
                   _sds((SUBLANES, LANES), F32)),
        in_specs=[HBM_SPEC] * (n_in + n_out),
        out_specs=(SEM_SPEC, SEM_SPEC, SEM_SPEC, *[HBM_SPEC] * (n_in + n_out), pl.BlockSpec(memory_space=pltpu.VMEM)),
        input_output_aliases={i: 3 + i for i in range(n_in + n_out)},
        compiler_params=pltpu.CompilerParams(has_side_effects=pltpu.SideEffectType.DATAFLOW_SIDE_EFFECTING))(*operands)
    return out[:3], out[3:3 + n_in], out[3 + n_in:3 + n_in + n_out], out[-1]


def _exchange_wait(name, sems, thru, zones, plan, local, after):
    n_in, n_out = len(thru), len(zones)

    def body(*refs):
        bufs = refs[:n_in + n_out]
        sends, arrivals, locs = _split_copies(bufs, plan, local, *refs[n_in + n_out:n_in + n_out + 3])
        for cp in arrivals:
            cp.wait_recv()
        for cp in sends:
            cp.wait_send()
        for cp in locs:
            cp.wait()

    out = pl.pallas_call(
        body, name=name, out_shape=tuple(pltpu.HBM(a.shape, a.dtype) for a in list(thru) + list(zones)),
        in_specs=[HBM_SPEC] * (n_in + n_out) + [SEM_SPEC] * 3 + [pl.BlockSpec(memory_space=pl.ANY)],
        out_specs=tuple([HBM_SPEC] * (n_in + n_out)), input_output_aliases={i: i for i in range(n_in + n_out)},
        compiler_params=pltpu.CompilerParams(has_side_effects=pltpu.SideEffectType.DATAFLOW_SIDE_EFFECTING))(
            *thru, *zones, *sems, after)
    return out[n_in:]


def _whole(*_):
    return ()


def _half_rows(rows, core):
    return pl.ds(core * (rows // 2), rows // 2)


def _gather_weights_plan(shards):
    n = len(shards)
    dsts = [_sds((N_CHIPS,) + a.shape, a.dtype) for a in shards]
    fetch, forward = [], []
    for t, a in enumerate(shards):
        rows = a.shape[0]
        if rows % (2 * 16) == 0:
            fetch += [(rel, t, (lambda s_, r_, rows=rows: (_half_rows(rows, s_[2]),)), n + t,
                       (lambda s_, r_, rows=rows: (_chip_of(s_), _half_rows(rows, s_[2])))) for rel in REL_CHIPS]
            forward += [(REL_SIBLING, n + t, (lambda s_, r_, rows=rows, rel=rel: (_chip_of(_flip(s_, rel)), _half_rows(rows, s_[2]))),
                         n + t, (lambda s_, r_, rows=rows, rel=rel: (_chip_of(_flip(s_, rel)), _half_rows(rows, s_[2]))))
                        for rel in REL_CHIPS]
        else:
            fetch += [(rel, t, _whole, n + t, lambda s_, r_: (_chip_of(s_),)) for rel in REL_CHIPS]
    local = [(t, _whole, n + t, lambda me: (_chip_of(me),)) for t in range(n)]
    return dsts, fetch, local, forward


def _gather_weights_start(name, shards):
    dsts, fetch, local, forward = _gather_weights_plan(shards)
    sems, thru, zones, token = _exchange_start(name, shards, dsts, fetch, local)
    return (sems, thru, zones, fetch, local, forward), token


def _gather_weights_finish(name, pending, after):
    sems, thru, zones, fetch, local, forward = pending
    landed = _exchange_wait(name + "_wait", sems, thru, zones, fetch, local, after)
    n = len(landed)
    return _exchange(name + "_forward", landed, n, [[_sds(a.shape, a.dtype) for a in landed], (forward, [])],
                     aliases={t: t for t in range(n)})


def _gather_chips(name, arrays):
    n = len(arrays)
    dsts = [_sds((N_CHIPS,) + a.shape, a.dtype) for a in arrays]
    plan = [(rel, t, _whole, n + t, lambda s_, r_: (_chip_of(s_),)) for t in range(n) for rel in REL_CHIPS]
    local = [(t, _whole, n + t, lambda me: (_chip_of(me),)) for t in range(n)]
    return _exchange(name, arrays, n, [dsts, (plan, local)])


def _gather_all(name, a):
    plan = [(rel, 0, _whole, 1, lambda s_, r_: (_dev_of(s_),)) for rel in REL_ALL]
    local = [(0, _whole, 1, lambda me: (_dev_of(me),))]
    return _exchange(name, [a], 1, [[_sds((2 * N_CHIPS,) + a.shape, a.dtype)], (plan, local)])[0]


def _swap_halves(name, grads):
    n = len(grads)
    dsts = [_sds((g.shape[0], g.shape[1] // 2, g.shape[2]), g.dtype) for g in grads]
    plan = [(REL_SIBLING, t, (lambda s_, r_, rows=g.shape[1]: (pl.ds(0, N_CHIPS), _half_rows(rows, r_[2]))), n + t, _whole)
            for t, g in enumerate(grads)]
    return _exchange(name, grads, n, [dsts, (plan, [])])


def _scatter_chips_plan(sums):
    n = len(sums)
    dsts = [_sds(a.shape, a.dtype) for a in sums]
    plan = [(rel, t, lambda s_, r_: (_chip_of(r_),), n + t, lambda s_, r_: (_chip_of(s_),))
            for t in range(n) for rel in REL_CHIPS]
    local = [(t, lambda me: (_chip_of(me),), n + t, lambda me: (_chip_of(me),)) for t in range(n)]
    return dsts, plan, local


def _scatter_chips(name, sums):
    dsts, plan, local = _scatter_chips_plan(sums)
    return _exchange(name, sums, len(sums), [dsts, (plan, local)])


def _scatter_chips_start(name, sums):
    dsts, plan, local = _scatter_chips_plan(sums)
    sems, thru, zones, token = _exchange_start(name, sums, dsts, plan, local)
    return (sems, thru, zones, plan, local), token


def _scatter_chips_finish(name, pending, after):
    sems, thru, zones, plan, local = pending
    return _exchange_wait(name + "_wait", sems, thru, zones, plan, local, after)


def _swap_back(name, totals, layer):
    n = len(totals)
    dsts = [_sds(a.shape, a.dtype) for a in totals]
    plan = [(REL_SIBLING, n + t, (lambda s_, r_, rows=a.shape[1]: (layer, _half_rows(rows, s_[2]))),
             n + t, (lambda s_, r_, rows=a.shape[1]: (layer, _half_rows(rows, s_[2])))) for t, a in enumerate(totals)]
    return _exchange(name, totals, n, [dsts, (plan, [])], aliases={t: t for t in range(n)})


def _add_halves(name, g, recv, core):
    nch, half, c = recv.shape

    def body(core_ref, g_ref, r_ref, o_ref):
        o_ref[0] = (g_ref[0, 0].astype(F32) + r_ref[0].astype(F32)).astype(o_ref.dtype)

    spec = pltpu.PrefetchScalarGridSpec(
        num_scalar_prefetch=1, grid=(nch,),
        in_specs=[pl.BlockSpec((1, 1, half, c), lambda k, core_ref: (k, core_ref[0], 0, 0)),
                  pl.BlockSpec((1, half, c), lambda k, core_ref: (k, 0, 0))],
        out_specs=pl.BlockSpec((1, half, c), lambda k, core_ref: (k, 0, 0)))
    return pl.pallas_call(body, name=name, out_shape=_sds(recv.shape, recv.dtype), grid_spec=spec,
                          compiler_params=pltpu.CompilerParams(dimension_semantics=("parallel",),
                                                               vmem_limit_bytes=VMEM_LIMIT))(
                                                                   core, g.reshape(nch, 2, half, c), recv)


def _sum_chips(name, a, core, layer, total):
    nch, half, c = a.shape

    def body(core_ref, a_ref, *rest):
        o_ref = rest[-1]
        acc = a_ref[0].astype(F32)
        for k in range(1, nch):
            acc = acc + a_ref[k].astype(F32)
        o_ref[0, 0] = acc

    in_specs = [pl.BlockSpec((nch, half, c), lambda i, core_ref: (0, 0, 0))]
    args = [core, a]
    if total is not None:
        in_specs.append(pl.BlockSpec(memory_space=pl.ANY))
        args.append(total.reshape(DEPTH, 2, half, c))
    spec = pltpu.PrefetchScalarGridSpec(
        num_scalar_prefetch=1, grid=(1,), in_specs=in_specs,
        out_specs=pl.BlockSpec((1, 1, half, c), lambda i, core_ref: (layer, core_ref[0], 0, 0)))
    out = pl.pallas_call(body, name=name, out_shape=_sds((DEPTH, 2, half, c), F32), grid_spec=spec,
                         input_output_aliases={2: 0} if total is not None else {},
                         compiler_params=pltpu.CompilerParams(dimension_semantics=("arbitrary",),
                                                              vmem_limit_bytes=VMEM_LIMIT))(*args)
    return out.reshape(DEPTH, 2 * half, c)


def _adamw(name, w, g, m, v):
    shape = w.shape
    c = shape[-1]
    rows = w.size // c
    tr = next(t for t in (256, 128, 64, 32, 16, 8, rows) if rows % t == 0)
    flat = lambda a: a.reshape(rows, c)

    def body(w_ref, g_ref, m_ref, v_ref, d_ref, mo_ref, vo_ref):
        gv = g_ref[...]
        mn = ADAM_B1 * m_ref[...] + (1.0 - ADAM_B1) * gv
        vn = ADAM_B2 * v_ref[...] + (1.0 - ADAM_B2) * (gv * gv)
        m_hat = mn / (1.0 - ADAM_B1 ** ADAM_STEP)
        v_hat = vn / (1.0 - ADAM_B2 ** ADAM_STEP)
        d_ref[...] = -ADAM_LR * (m_hat / (jnp.sqrt(v_hat) + ADAM_EPS) + ADAM_WD * w_ref[...])
        mo_ref[...] = mn
        vo_ref[...] = vn

    spec = pl.BlockSpec((tr, c), lambda i: (i, 0))
    o = _sds((rows, c), F32)
    d, mn, vn = _call(body, name, (o, o, o), (rows // tr,), [spec] * 4, (spec,) * 3, ("parallel",))(
        flat(w), flat(g), flat(m), flat(v))
    return d.reshape(shape), mn.reshape(shape), vn.reshape(shape)


def _pack(arrays, rows):
    flat = jnp.concatenate([a.reshape(-1).astype(F32) for a in arrays])
    return _pad_to(flat, 0, rows * LANES).reshape(rows, LANES)


def _unpack(packed, shapes):
    flat = packed.reshape(-1)
    out, off = [], 0
    for shp in shapes:
        n = 1
        for s_ in shp:
            n *= s_
        out.append(flat[off:off + n].reshape(shp))
        off += n
    return out


_MATRICES = ("w_in", "proj_a", "proj_b", "w_out", "w_gate", "w_up", "w_down")
_SMALL = (("norm1_w", (D_MODEL,)), ("sinks", (N_Q_HEADS,)), ("sgu_ln_w", (SGU_WIDTH,)), ("sgu_ln_b", (SGU_WIDTH,)),
          ("sgu_w", (SGU_GROUPS, SGU_CHUNK, SGU_CHUNK)), ("sgu_b", (SGU_GROUPS, SGU_CHUNK)), ("norm2_w", (D_MODEL,)),
          ("conv_w", (3, FFN_DIM)), ("conv_b", (FFN_DIM,)), ("final_norm_w", (D_MODEL,)))
SMALL_ROWS = 320
ADAM_PACK_ROWS = 256


def _reduce_cores(l, grads, small, core):
    tag = f"l{l}_reduce_"
    partial = [grads[k] for k in _MATRICES] + [small]
    from_sibling = _swap_halves(tag + "cores", partial)
    return [_add_halves(f"{tag}cores_add{t}", partial[t], from_sibling[t], core) for t in range(len(partial))]


def _reduce_finish(l, from_chips, core, totals):
    tag = f"l{l}_reduce_"
    totals = [_sum_chips(f"{tag}chips_add{t}", from_chips[t], core, l, None if totals is None else totals[t])
              for t in range(len(from_chips))]
    return _swap_back(tag + "back", totals, l)


def kernel(x, c, positions, ada_w, ada_b, norm1_w, w_in, attn_sinks, sgu_ln_w, sgu_ln_b, sgu_w, sgu_b, proj_a, proj_b, w_out, norm2_w, ffn_w_gate, ffn_w_up, ffn_conv_w, ffn_conv_b, ffn_w_down, final_norm_w, loss_target, m_ada_w, m_ada_b, m_norm1_w, m_w_in, m_attn_sinks, m_sgu_ln_w, m_sgu_ln_b, m_sgu_w, m_sgu_b, m_proj_a, m_proj_b, m_w_out, m_norm2_w, m_ffn_w_gate, m_ffn_w_up, m_ffn_conv_w, m_ffn_conv_b, m_ffn_w_down, m_final_norm_w, v_ada_w, v_ada_b, v_norm1_w, v_w_in, v_attn_sinks, v_sgu_ln_w, v_sgu_ln_b, v_sgu_w, v_sgu_b, v_proj_a, v_proj_b, v_w_out, v_norm2_w, v_ffn_w_gate, v_ffn_w_up, v_ffn_conv_w, v_ffn_conv_b, v_ffn_w_down, v_final_norm_w):
    d = D_MODEL
    ax, ay, ac = lax.axis_index("x"), lax.axis_index("y"), lax.axis_index("c")
    chip = 2 * ax + ay
    dev = 4 * ax + 2 * ay + ac
    core = ac.astype(jnp.int32).reshape(1)

    shards = [w_in.astype(BF), proj_a.astype(BF), proj_b.astype(BF), w_out.astype(BF),
              _pad_to(ffn_w_gate.astype(BF), 2, FFN_SHARD_PAD), _pad_to(ffn_w_up.astype(BF), 2, FFN_SHARD_PAD),
              _pad_to(ffn_w_down.astype(BF), 1, FFN_SHARD_PAD), _pad_to(ffn_conv_w, 2, FFN_SHARD_PAD)]
    pending, token = _gather_weights_start("l0_gather", [a[0] for a in shards])

    c_all = _gather_all("gather_cond", jnp.broadcast_to(c + token[0, 0], (SUBLANES, d)))[:, 0, :]
    c_rows = _pad_to(c_all, 0, ADA_ROWS)
    ada_cols = ada_w.shape[2]
    ada_b_cols = lax.dynamic_slice_in_dim(ada_b, chip * ada_cols, ada_cols, axis=1).reshape(DEPTH, 1, ada_cols)
    mod_cols = _ada_fwd("ada_fwd", c_rows, ada_w, ada_b_cols)
    mod_all = _gather_chips("gather_mod", [mod_cols])[0]
    mod_mine = lax.dynamic_index_in_dim(mod_all, dev, axis=2, keepdims=False)
    mod_mine = mod_mine.transpose(1, 0, 2).reshape(DEPTH, 1, 6 * d)
    mods = [tuple(jnp.split(mod_mine[l], 6, axis=-1)) for l in range(DEPTH)]

    small_in = dict(norm1_w=norm1_w, sinks=attn_sinks, sgu_ln_w=sgu_ln_w, sgu_ln_b=sgu_ln_b, sgu_w=sgu_w, sgu_b=sgu_b,
                    norm2_w=norm2_w, conv_b=ffn_conv_b)
    cosf, sinf = _rope_tables(positions[0])
    layer_params = lambda l, g: _layer_params(*g, {k: v[l] for k, v in small_in.items()})

    gathered = _gather_weights_finish("l0_gather", pending, mod_all)
    params = [layer_params(0, gathered)]
    behind = gathered[-1][0, 0, 0] * 0.0
    pending, token = _gather_weights_start("l1_gather", [a[1] for a in shards[:-1]] + [shards[-1][1] + behind])
    h, sv = _layer_fwd(0, x[0], mods[0], params[0], cosf, sinf, after=token)
    saved = [sv]
    params.append(layer_params(1, _gather_weights_finish("l1_gather", pending, h)))
    h, sv = _layer_fwd(1, h, mods[1], params[1], cosf, sinf)
    saved.append(sv)
    dx, loss_part, d_final = _loss_head("loss_head", h, final_norm_w.reshape(1, d), loss_target[0])
    loss = lax.psum(loss_part[0, 0], ("x", "y", "c"))

    def chip_sums_of(l, grads):
        grads["w_in"] = grads["w_in"].reshape(d, N_CHIPS, IN_COLS // N_CHIPS).transpose(1, 0, 2)
        cw, cb = _conv_grads_natural(grads)
        nat = dict(grads, conv_w=cw, conv_b=cb, final_norm_w=d_final if l == DEPTH - 1 else jnp.zeros((d,), F32))
        small = _pack([nat[k] for k, _ in _SMALL], N_CHIPS * SMALL_ROWS).reshape(N_CHIPS, SMALL_ROWS, LANES)
        return _reduce_cores(l, grads, small, core)

    dmods = [None] * DEPTH
    dx, dmods[1], grads = _layer_bwd(1, dx, mods[1], params[1], saved[1], cosf, sinf)
    pending, token = _scatter_chips_start("l1_reduce_chips", chip_sums_of(1, grads))
    dx, dmods[0], grads = _layer_bwd(0, dx, mods[0], params[0], saved[0], cosf, sinf, after=token)
    totals = _reduce_finish(1, _scatter_chips_finish("l1_reduce_chips", pending, dx), core, None)
    totals = _reduce_finish(0, _scatter_chips("l0_reduce_chips", chip_sums_of(0, grads)), core, totals)

    dmod_mine = jnp.concatenate([jnp.concatenate(dmods[l], axis=1) for l in range(DEPTH)], axis=1)
    dmod_all = _gather_all("gather_dmod", jnp.broadcast_to(dmod_mine, (SUBLANES, DEPTH * 6 * d)))[:, 0, :]
    g_ada_b = _colsum("ada_b_grad", dmod_all).reshape(DEPTH, 6 * d)
    dmod_cols = jnp.stack([lax.dynamic_slice_in_dim(dmod_all, l * 6 * d + chip * ada_cols, ada_cols, axis=1)
                           for l in range(DEPTH)])
    g_ada_w = _ada_bwd("ada_w_grad", c_rows, _pad_to(dmod_cols, 1, ADA_ROWS))

    small_all = _gather_chips("gather_small", [totals[-1]])[0]
    small_g = _unpack(small_all.transpose(1, 0, 2, 3).reshape(DEPTH, -1), [(DEPTH, N_CHIPS * SMALL_ROWS * LANES)])[0]
    per_layer = [_unpack(small_g[l], [shp for _, shp in _SMALL]) for l in range(DEPTH)]
    sg = {k: jnp.stack([per_layer[l][i] for l in range(DEPTH)]) for i, (k, _) in enumerate(_SMALL)}
    g_final = sg["final_norm_w"][DEPTH - 1]
    g_conv_w = lax.dynamic_slice_in_dim(sg["conv_w"], chip * FFN_SHARD, FFN_SHARD, axis=2)

    g_big = dict(zip(_MATRICES, totals[:-1]))
    g_big["w_gate"] = g_big["w_gate"][:, :, :FFN_SHARD]
    g_big["w_up"] = g_big["w_up"][:, :, :FFN_SHARD]
    g_big["w_down"] = g_big["w_down"][:, :FFN_SHARD, :]
    big = dict(w_in=(w_in, m_w_in, v_w_in), proj_a=(proj_a, m_proj_a, v_proj_a), proj_b=(proj_b, m_proj_b, v_proj_b),
               w_out=(w_out, m_w_out, v_w_out), w_gate=(ffn_w_gate, m_ffn_w_gate, v_ffn_w_gate),
               w_up=(ffn_w_up, m_ffn_w_up, v_ffn_w_up), w_down=(ffn_w_down, m_ffn_w_down, v_ffn_w_down))
    upd = {k: _adamw("adamw_" + k, w, g_big[k], m, v) for k, (w, m, v) in big.items()}
    upd["ada_w"] = _adamw("adamw_ada_w", ada_w, g_ada_w, m_ada_w, v_ada_w)
    g_big["ada_w"] = g_ada_w
    rest = [("ada_b", ada_b, g_ada_b, m_ada_b, v_ada_b), ("norm1_w", norm1_w, sg["norm1_w"], m_norm1_w, v_norm1_w),
            ("attn_sinks", attn_sinks, sg["sinks"], m_attn_sinks, v_attn_sinks),
            ("sgu_ln_w", sgu_ln_w, sg["sgu_ln_w"], m_sgu_ln_w, v_sgu_ln_w),
            ("sgu_ln_b", sgu_ln_b, sg["sgu_ln_b"], m_sgu_ln_b, v_sgu_ln_b), ("sgu_w", sgu_w, sg["sgu_w"], m_sgu_w, v_sgu_w),
            ("sgu_b", sgu_b, sg["sgu_b"], m_sgu_b, v_sgu_b), ("norm2_w", norm2_w, sg["norm2_w"], m_norm2_w, v_norm2_w),
            ("ffn_conv_w", ffn_conv_w, g_conv_w, m_ffn_conv_w, v_ffn_conv_w),
            ("ffn_conv_b", ffn_conv_b, sg["conv_b"], m_ffn_conv_b, v_ffn_conv_b),
            ("final_norm_w", final_norm_w, g_final, m_final_norm_w, v_final_norm_w)]
    rest_shapes = [r[1].shape for r in rest]
    rest_rows = -(-sum(r[1].size for r in rest) // (LANES * ADAM_PACK_ROWS)) * ADAM_PACK_ROWS
    packs = [_pack([r[i] for r in rest], rest_rows) for i in (1, 2, 3, 4)]
    rest_out = [_unpack(a, rest_shapes) for a in _adamw("adamw_rest", *packs)]
    g_rest = {r[0]: r[2] for r in rest}
    u_rest = {r[0]: tuple(o[i] for o in rest_out) for i, r in enumerate(rest)}

    names = ("ada_w", "ada_b", "norm1_w", "w_in", "attn_sinks", "sgu_ln_w", "sgu_ln_b", "sgu_w", "sgu_b", "proj_a", "proj_b",
             "w_out", "norm2_w", "ffn_w_gate", "ffn_w_up", "ffn_conv_w", "ffn_conv_b", "ffn_w_down", "final_norm_w")
    alias = {"ffn_w_gate": "w_gate", "ffn_w_up": "w_up", "ffn_w_down": "w_down"}
    grad_of = lambda n: g_rest[n] if n in g_rest else g_big[alias.get(n, n)]
    upd_of = lambda n: u_rest[n] if n in u_rest else upd[alias.get(n, n)]
    return (loss, dx[None], *[grad_of(n) for n in names], *[upd_of(n)[0] for n in names],
            *[upd_of(n)[1] for n in names], *[upd_of(n)[2] for n in names])
```

```python
import jax
import jax.numpy as jnp
from jax import lax
from jax.experimental import pallas as pl
from jax.experimental.pallas import tpu as pltpu

F32 = jnp.float32
BF = jnp.bfloat16

D_MODEL = 1024
N_Q_HEADS = 16
N_KV_HEADS = 2
HEAD_DIM = 64
ATTN_BLOCK = 128
ROPE_THETA = 500000.0
ROT_DIM = HEAD_DIM // 4
SGU_WIDTH = 1024
SGU_GROUPS = 8
SGU_CHUNK = 128
FFN_DIM = 2816
NORM_EPS = 1e-6
DEPTH = 2
IN_COLS = 5376
N_CHIPS = 4
FFN_SHARD = FFN_DIM // N_CHIPS
FFN_SHARD_PAD = 768
FFN_PAD = N_CHIPS * FFN_SHARD_PAD
LANES = 128
SUBLANES = 8
VMEM_LIMIT = 56 * 1024 * 1024
NEG_BIG = -1e30

ADAM_LR = 0.001
ADAM_B1 = 0.9
ADAM_B2 = 0.999
ADAM_EPS = 1e-08
ADAM_WD = 0.01
ADAM_STEP = 10

MESH = pl.DeviceIdType.MESH

Q_END = 1024
KV_END = 1280
U_END = 2304
Z_END = 3328
GA_END = 4352


def _sds(shape, dtype):
    return jax.ShapeDtypeStruct(tuple(shape), dtype)


def _call(body, name, out_shape, grid, in_specs, out_specs, semantics, scratch=(), after=None):
    n_in = len(in_specs)
    fn = body
    if after is not None:
        def fn(*refs):
            return body(*refs[:n_in], *refs[n_in + 1:])

        in_specs = list(in_specs) + [pl.BlockSpec(memory_space=pl.ANY)]
    call = pl.pallas_call(
        fn, name=name, out_shape=out_shape, grid=grid, in_specs=in_specs, out_specs=out_specs,
        scratch_shapes=scratch,
        compiler_params=pltpu.CompilerParams(dimension_semantics=semantics, vmem_limit_bytes=VMEM_LIMIT))
    if after is None:
        return call
    return lambda *args: call(*args, after)


def _rows(tm, width, col=0):
    return pl.BlockSpec((tm, width), lambda i: (i, col))


def _vec(width):
    return pl.BlockSpec((1, width), lambda i: (0, 0))


def _resident(shape):
    zeros = (0,) * len(shape)
    return pl.BlockSpec(tuple(shape), lambda *_: zeros, pipeline_mode=pl.Buffered(1))


def _sigmoid(x):
    return 1.0 / (1.0 + jnp.exp(-x))


def _gelu(x):
    return 0.5 * x * (1.0 + lax.erf(x * 0.7071067811865476))


def _gelu_grad(x):
    cdf = 0.5 * (1.0 + lax.erf(x * 0.7071067811865476))
    return cdf + x * jnp.exp(-0.5 * x * x) * 0.3989422804014327


def _dot(a, b):
    return jnp.dot(a, b, preferred_element_type=F32)


def _dot_nt(a, b):
    return lax.dot_general(a, b, (((1,), (1,)), ((), ())), preferred_element_type=F32)


def _dot_tn(a, b):
    return lax.dot_general(a, b, (((0,), (0,)), ((), ())), preferred_element_type=F32)


def _rms(xv):
    return lax.rsqrt(jnp.mean(xv * xv, axis=-1, keepdims=True) + NORM_EPS)


def _matmul_tn(name, a, b, tk=512, tn=1024, blocked=False):
    s, k = a.shape
    n = b.shape[1]
    tk, tn = min(tk, k), min(tn, n)

    def body(a_ref, b_ref, o_ref):
        res = _dot_tn(a_ref[...], b_ref[...]).astype(o_ref.dtype)
        if blocked:
            o_ref[0] = res
        else:
            o_ref[...] = res

    if blocked:
        out, ospec = _sds((n // tn, k, tn), BF), pl.BlockSpec((1, tk, tn), lambda i, j: (j, i, 0))
    else:
        out, ospec = _sds((k, n), BF), pl.BlockSpec((tk, tn), lambda i, j: (i, j))
    return _call(body, name, out, (k // tk, n // tn),
                 [pl.BlockSpec((s, tk), lambda i, j: (0, i)), pl.BlockSpec((s, tn), lambda i, j: (0, j))],
                 ospec, ("parallel", "parallel"))(a, b)


def _matmul_tn_rows(name, a, b, out, row0, rows_total, tk=256):
    s, k = a.shape
    n = b.shape[1]

    def body(a_ref, b_ref, *rest):
        rest[-1][...] = _dot_tn(a_ref[...], b_ref[...]).astype(BF)

    in_specs = [pl.BlockSpec((s, tk), lambda i: (0, i)), _resident(b.shape)]
    args = [a, b]
    if out is not None:
        in_specs.append(pl.BlockSpec(memory_space=pl.ANY))
        args.append(out)
    return pl.pallas_call(
        body, name=name, out_shape=_sds((rows_total, n), BF), grid=(k // tk,), in_specs=in_specs,
        out_specs=pl.BlockSpec((tk, n), lambda i: (row0 // tk + i, 0)),
        input_output_aliases={2: 0} if out is not None else {},
        compiler_params=pltpu.CompilerParams(dimension_semantics=("parallel",), vmem_limit_bytes=VMEM_LIMIT))(*args)


def _rope_partner(v):
    lane = lax.broadcasted_iota(jnp.int32, (1, LANES), 1) % HEAD_DIM
    return jnp.where(lane < ROT_DIM // 2, pltpu.roll(v, LANES - ROT_DIM // 2, axis=1), pltpu.roll(v, ROT_DIM // 2, axis=1))


def _dup_half(v, half):
    lane = lax.broadcasted_iota(jnp.int32, (1, LANES), 1)
    keep = jnp.where((lane >= HEAD_DIM) == (half == 1), v, 0.0)
    return keep + pltpu.roll(keep, HEAD_DIM, axis=1)


def _in_proj(name, x, w, sc, sh, w_in, cosf, sinf, tm=256, after=None):
    s, d = x.shape
    tm = min(tm, s)

    def body(x_ref, w_ref, sc_ref, sh_ref, win_ref, cos_ref, sin_ref,
             h_ref, qr_ref, kk0_ref, kk1_ref, vv0_ref, vv1_ref, u_ref, v_ref, ga_ref, gb_ref):
        xv = x_ref[...]
        h = ((xv * _rms(xv)) * w_ref[...] * (1.0 + sc_ref[...]) + sh_ref[...]).astype(BF)
        h_ref[...] = h
        cosv, sinv = cos_ref[...], sin_ref[...]
        q = _dot_nt(h, win_ref[:Q_END, :])
        for j in range(D_MODEL // LANES):
            qv = q[:, j * LANES:(j + 1) * LANES]
            qr_ref[:, j * LANES:(j + 1) * LANES] = (qv * cosv + _rope_partner(qv) * sinv).astype(BF)
        kv = _dot_nt(h, win_ref[Q_END:KV_END, :])
        kr = kv[:, :LANES] * cosv + _rope_partner(kv[:, :LANES]) * sinv
        vv = kv[:, LANES:]
        kk0_ref[...] = _dup_half(kr, 0).astype(BF)
        kk1_ref[...] = _dup_half(kr, 1).astype(BF)
        vv0_ref[...] = _dup_half(vv, 0).astype(BF)
        vv1_ref[...] = _dup_half(vv, 1).astype(BF)
        u_ref[...] = _dot_nt(h, win_ref[KV_END:U_END, :])
        v_ref[...] = _dot_nt(h, win_ref[U_END:Z_END, :])
        ga_ref[...] = _dot_nt(h, win_ref[Z_END:GA_END, :])
        gb_ref[...] = _dot_nt(h, win_ref[GA_END:, :])

    wide, kvs, pre = _sds((s, d), BF), _sds((s, LANES), BF), _sds((s, d), F32)
    return _call(body, name, (wide, wide, kvs, kvs, kvs, kvs, pre, pre, pre, pre), (s // tm,),
                 [_rows(tm, d), _vec(d), _vec(d), _vec(d), _resident(w_in.shape), _rows(tm, LANES), _rows(tm, LANES)],
                 (_rows(tm, d), _rows(tm, d)) + (_rows(tm, LANES),) * 4 + (_rows(tm, d),) * 4, ("parallel",), after=after)(
                     x, w, sc, sh, w_in, cosf, sinf)


def _in_proj_bwd(name, dq, dkv, du, dv, dga, dgb, w_in, x, w, sc, dx_in, tm=256):
    s, d = x.shape
    tm = min(tm, s)

    def body(dq_ref, dkv_ref, du_ref, dv_ref, dga_ref, dgb_ref, win_ref, x_ref, w_ref, sc_ref, dxin_ref,
             dx_ref, da_ref, dsh_ref):
        @pl.when(pl.program_id(0) == 0)
        def _():
            da_ref[...] = jnp.zeros_like(da_ref)
            dsh_ref[...] = jnp.zeros_like(dsh_ref)

        dh = (_dot(dq_ref[...], win_ref[:Q_END, :]) + _dot(dkv_ref[...], win_ref[Q_END:KV_END, :])
              + _dot(du_ref[...], win_ref[KV_END:U_END, :]) + _dot(dv_ref[...], win_ref[U_END:Z_END, :])
              + _dot(dga_ref[...], win_ref[Z_END:GA_END, :]) + _dot(dgb_ref[...], win_ref[GA_END:, :]))
        xv = x_ref[...]
        r = _rms(xv)
        xn = xv * r
        dxn = dh * (w_ref[...] * (1.0 + sc_ref[...]))
        dx_ref[...] = dxin_ref[...] + r * (dxn - xn * jnp.mean(dxn * xn, axis=-1, keepdims=True))
        da_ref[...] += jnp.sum(dh * xn, axis=0, keepdims=True)
        dsh_ref[...] += jnp.sum(dh, axis=0, keepdims=True)

    return _call(body, name, (_sds((s, d), F32), _sds((1, d), F32), _sds((1, d), F32)), (s // tm,),
                 [_rows(tm, d), _rows(tm, 2 * LANES), _rows(tm, d), _rows(tm, d), _rows(tm, d), _rows(tm, d),
                  _resident(w_in.shape), _rows(tm, d), _vec(d), _vec(d), _rows(tm, d)],
                 (_rows(tm, d), _vec(d), _vec(d)), ("arbitrary",))(dq, dkv, du, dv, dga, dgb, w_in, x, w, sc, dx_in)


def _rope_bwd(name, dqr, dkv_cur, dkv_prev, cosf, sinf, tm=512):
    s = dqr.shape[0]
    tm = min(tm, s)
    steps = s // tm
    per = tm // ATTN_BLOCK
    nb = s // ATTN_BLOCK

    def unrope(v, cosv, sinv):
        return v * cosv - _rope_partner(v) * sinv

    def body(dq_ref, cur_ref, prev_ref, next_ref, cos_ref, sin_ref, dqo_ref, dkvo_ref):
        i = pl.program_id(0)
        cosv, sinv = cos_ref[...], sin_ref[...]
        for j in range(D_MODEL // LANES):
            dqo_ref[:, j * LANES:(j + 1) * LANES] = unrope(dq_ref[:, j * LANES:(j + 1) * LANES], cosv, sinv).astype(BF)
        nxt = jnp.where(i < steps - 1, next_ref[...], 0.0)
        if per > 1:
            shifted = jnp.concatenate([prev_ref[ATTN_BLOCK:, :], nxt], axis=0)
        else:
            shifted = nxt
        tot = cur_ref[...] + shifted
        dkvo_ref[:, :LANES] = unrope(tot[:, :LANES], cosv, sinv).astype(BF)
        dkvo_ref[:, LANES:] = tot[:, LANES:].astype(BF)

    nxt_spec = pl.BlockSpec((ATTN_BLOCK, 2 * LANES), lambda i: (jnp.minimum((i + 1) * per, nb - 1), 0))
    return _call(body, name, (_sds((s, D_MODEL), BF), _sds((s, 2 * LANES), BF)), (steps,),
                 [_rows(tm, D_MODEL), _rows(tm, 2 * LANES), _rows(tm, 2 * LANES), nxt_spec, _rows(tm, LANES),
                  _rows(tm, LANES)],
                 (_rows(tm, D_MODEL), _rows(tm, 2 * LANES)), ("parallel",))(dqr, dkv_cur, dkv_prev, dkv_prev, cosf, sinf)


def _band_mask(n):
    qi = lax.broadcasted_iota(jnp.int32, (ATTN_BLOCK, 2 * ATTN_BLOCK), 0)
    kj = lax.broadcasted_iota(jnp.int32, (ATTN_BLOCK, 2 * ATTN_BLOCK), 1)
    return (kj > qi) & (kj <= qi + ATTN_BLOCK) & ((n > 0) | (kj >= ATTN_BLOCK))


def _attn_probs(qa, kb, allowed, sink):
    sc = jnp.where(allowed, _dot_nt(qa, kb) * (HEAD_DIM ** -0.5), NEG_BIG)
    m = jnp.maximum(jnp.max(sc, axis=-1, keepdims=True), sink)
    p = jnp.exp(sc - m)
    esink = jnp.exp(sink - m)
    inv = 1.0 / (jnp.sum(p, axis=-1, keepdims=True) + esink)
    return p * inv, esink * inv


def _kv_specs():
    cur = pl.BlockSpec((ATTN_BLOCK, LANES), lambda n: (n, 0))
    prev = pl.BlockSpec((ATTN_BLOCK, LANES), lambda n: (jnp.maximum(n - 1, 0), 0))
    return [prev, cur] * 4


def _attention(name, qr, kk0, kk1, vv0, vv1, sinks):
    s = qr.shape[0]
    nb = s // ATTN_BLOCK

    def body(sink_ref, q_ref, k0p, k0c, k1p, k1c, v0p, v0c, v1p, v1c, y_ref):
        allowed = _band_mask(pl.program_id(0))
        upper = lax.broadcasted_iota(jnp.int32, (1, LANES), 1) >= HEAD_DIM
        bands = ((jnp.concatenate([k0p[...], k0c[...]], axis=0), jnp.concatenate([v0p[...], v0c[...]], axis=0)),
                 (jnp.concatenate([k1p[...], k1c[...]], axis=0), jnp.concatenate([v1p[...], v1c[...]], axis=0)))
        for hk in range(N_KV_HEADS):
            kb, vb = bands[hk]
            for j in range(4):
                col = (hk * 4 + j) * LANES
                qp = q_ref[:, col:col + LANES]
                out = jnp.zeros((ATTN_BLOCK, LANES), F32)
                for half in range(2):
                    sel = upper if half else jnp.logical_not(upper)
                    qa = jnp.where(sel, qp, jnp.zeros_like(qp))
                    pn, _ = _attn_probs(qa, kb, allowed, sink_ref[hk * 8 + j * 2 + half])
                    out = jnp.where(sel, _dot(pn.astype(BF), vb), out)
                y_ref[:, col:col + LANES] = out.astype(BF)

    return _call(body, name, _sds((s, D_MODEL), BF), (nb,),
                 [pl.BlockSpec(memory_space=pltpu.SMEM), pl.BlockSpec((ATTN_BLOCK, D_MODEL), lambda n: (n, 0))] + _kv_specs(),
                 pl.BlockSpec((ATTN_BLOCK, D_MODEL), lambda n: (n, 0)), ("parallel",))(
                     sinks, qr, kk0, kk0, kk1, kk1, vv0, vv0, vv1, vv1)


def _attention_bwd(name, qr, kk0, kk1, vv0, vv1, sinks, dy):
    s = qr.shape[0]
    nb = s // ATTN_BLOCK

    def body(sink_ref, q_ref, dy_ref, k0p, k0c, k1p, k1c, v0p, v0c, v1p, v1c, dq_ref, cur_ref, prev_ref, dsink_ref):
        @pl.when(pl.program_id(0) == 0)
        def _():
            dsink_ref[...] = jnp.zeros_like(dsink_ref)

        allowed = _band_mask(pl.program_id(0))
        lane = lax.broadcasted_iota(jnp.int32, (1, LANES), 1)
        upper = lane >= HEAD_DIM
        bands = ((jnp.concatenate([k0p[...], k0c[...]], axis=0), jnp.concatenate([v0p[...], v0c[...]], axis=0)),
                 (jnp.concatenate([k1p[...], k1c[...]], axis=0), jnp.concatenate([v1p[...], v1c[...]], axis=0)))
        dsink = jnp.zeros((1, LANES), F32)
        dk_slab = jnp.zeros((2 * ATTN_BLOCK, LANES), F32)
        dv_slab = jnp.zeros((2 * ATTN_BLOCK, LANES), F32)
        for hk in range(N_KV_HEADS):
            kb, vb = bands[hk]
            dkk = jnp.zeros((2 * ATTN_BLOCK, LANES), F32)
            dvv = jnp.zeros((2 * ATTN_BLOCK, LANES), F32)
            for j in range(4):
                col = (hk * 4 + j) * LANES
                qp = q_ref[:, col:col + LANES]
                dyp = dy_ref[:, col:col + LANES]
                dq_pair = jnp.zeros((ATTN_BLOCK, LANES), F32)
                for half in range(2):
                    h = hk * 8 + j * 2 + half
                    sel = upper if half else jnp.logical_not(upper)
                    qa = jnp.where(sel, qp, jnp.zeros_like(qp))
                    pn, psink = _attn_probs(qa, kb, allowed, sink_ref[h])
                    dya = jnp.where(sel, dyp, 0.0).astype(BF)
                    dp = _dot_nt(dya, vb)
                    delta = jnp.sum(pn * dp, axis=-1, keepdims=True)
                    ds = (pn * (dp - delta) * (HEAD_DIM ** -0.5)).astype(BF)
                    dsink = dsink + jnp.where(lane == h, -jnp.sum(psink * delta), 0.0)
                    dq_pair = jnp.where(sel, _dot(ds, kb), dq_pair)
                    dkk = dkk + _dot_tn(ds, qa)
                    dvv = dvv + _dot_tn(pn.astype(BF), dya)
                dq_ref[:, col:col + LANES] = dq_pair
            mine = upper if hk else jnp.logical_not(upper)
            dk_slab = jnp.where(mine, dkk + pltpu.roll(dkk, HEAD_DIM, axis=1), dk_slab)
            dv_slab = jnp.where(mine, dvv + pltpu.roll(dvv, HEAD_DIM, axis=1), dv_slab)
        prev_ref[:, :LANES] = dk_slab[:ATTN_BLOCK]
        prev_ref[:, LANES:] = dv_slab[:ATTN_BLOCK]
        cur_ref[:, :LANES] = dk_slab[ATTN_BLOCK:]
        cur_ref[:, LANES:] = dv_slab[ATTN_BLOCK:]
        dsink_ref[...] += dsink

    blk = pl.BlockSpec((ATTN_BLOCK, D_MODEL), lambda n: (n, 0))
    kvo = pl.BlockSpec((ATTN_BLOCK, 2 * LANES), lambda n: (n, 0))
    return _call(body, name,
                 (_sds((s, D_MODEL), F32), _sds((s, 2 * LANES), F32), _sds((s, 2 * LANES), F32), _sds((1, LANES), F32)),
                 (nb,), [pl.BlockSpec(memory_space=pltpu.SMEM), blk, blk] + _kv_specs(),
                 (blk, kvo, kvo, pl.BlockSpec((1, LANES), lambda n: (0, 0))), ("arbitrary",))(
                     sinks, qr, dy, kk0, kk0, kk1, kk1, vv0, vv0, vv1, vv1)


def _sgu_weights(wm_ref, g):
    t = lax.broadcasted_iota(jnp.int32, (SGU_CHUNK, SGU_CHUNK), 0)
    sidx = lax.broadcasted_iota(jnp.int32, (SGU_CHUNK, SGU_CHUNK), 1)
    return jnp.where(sidx <= t, wm_ref[g], 0.0).astype(BF)


def _layer_norm_stats(v):
    mu = jnp.mean(v, axis=-1, keepdims=True)
    cen = v - mu
    rstd = lax.rsqrt(jnp.mean(cen * cen, axis=-1, keepdims=True) + NORM_EPS)
    return cen * rstd, rstd


def _sgu(name, u_pre, v_pre, ln_w, ln_b, wm, bfull, tm=256):
    s, w = u_pre.shape
    tm = min(tm, s)

    def body(u_ref, v_ref, lw_ref, lb_ref, wm_ref, b_ref, y_ref):
        vhat, _ = _layer_norm_stats(_gelu(v_ref[...]))
        vn = (vhat * lw_ref[...] + lb_ref[...]).astype(BF)
        for g in range(SGU_GROUPS):
            wg = _sgu_weights(wm_ref, g)
            cols = slice(g * SGU_CHUNK, (g + 1) * SGU_CHUNK)
            for ch in range(tm // SGU_CHUNK):
                rows = slice(ch * SGU_CHUNK, (ch + 1) * SGU_CHUNK)
                f = _dot(wg, vn[rows, cols]) + b_ref[g]
                y_ref[rows, cols] = (_gelu(u_ref[rows, cols]) * f).astype(BF)

    full3 = pl.BlockSpec((SGU_GROUPS, SGU_CHUNK, SGU_CHUNK), lambda i: (0, 0, 0))
    return _call(body, name, _sds((s, w), BF), (s // tm,),
                 [_rows(tm, w), _rows(tm, w), _vec(w), _vec(w), full3, full3],
                 _rows(tm, w), ("parallel",))(u_pre, v_pre, ln_w, ln_b, wm, bfull)


def _sgu_bwd(name, u_pre, v_pre, ln_w, ln_b, wm, bfull, dy, tm=256):
    s, w = u_pre.shape
    tm = min(tm, s)
    steps = s // tm

    def body(u_ref, v_ref, lw_ref, lb_ref, wm_ref, b_ref, dy_ref, du_ref, dv_ref, dwm_ref, db_ref, dlw_ref, dlb_ref,
             dfsum_ref):
        i = pl.program_id(0)

        @pl.when(i == 0)
        def _():
            dwm_ref[...] = jnp.zeros_like(dwm_ref)
            dlw_ref[...] = jnp.zeros_like(dlw_ref)
            dlb_ref[...] = jnp.zeros_like(dlb_ref)
            dfsum_ref[...] = jnp.zeros_like(dfsum_ref)

        vpre = v_ref[...]
        vhat, rstd = _layer_norm_stats(_gelu(vpre))
        vn = (vhat * lw_ref[...] + lb_ref[...]).astype(BF)
        t = lax.broadcasted_iota(jnp.int32, (SGU_CHUNK, SGU_CHUNK), 0)
        sidx = lax.broadcasted_iota(jnp.int32, (SGU_CHUNK, SGU_CHUNK), 1)
        dvn_cols = []
        for g in range(SGU_GROUPS):
            wg = _sgu_weights(wm_ref, g)
            cols = slice(g * SGU_CHUNK, (g + 1) * SGU_CHUNK)
            dvn_rows = []
            dwg = jnp.zeros((SGU_CHUNK, SGU_CHUNK), F32)
            dfs = jnp.zeros((SGU_CHUNK, SGU_CHUNK), F32)
            for ch in range(tm // SGU_CHUNK):
                rows = slice(ch * SGU_CHUNK, (ch + 1) * SGU_CHUNK)
                upre = u_ref[rows, cols]
                dyv = dy_ref[rows, cols]
                f = _dot(wg, vn[rows, cols]) + b_ref[g]
                du_ref[rows, cols] = (dyv * f * _gelu_grad(upre)).astype(BF)
                df = dyv * _gelu(upre)
                dfb = df.astype(BF)
                dvn_rows.append(_dot_tn(wg, dfb))
                dwg = dwg + _dot_nt(dfb, vn[rows, cols])
                dfs = dfs + df
            dwm_ref[g] += jnp.where(sidx <= t, dwg, 0.0)
            dfsum_ref[g] += dfs
            dvn_cols.append(jnp.concatenate(dvn_rows, axis=0) if len(dvn_rows) > 1 else dvn_rows[0])
        dvn = jnp.concatenate(dvn_cols, axis=1)
        dlw_ref[...] += jnp.sum(dvn * vhat, axis=0, keepdims=True)
        dlb_ref[...] += jnp.sum(dvn, axis=0, keepdims=True)
        dvh = dvn * lw_ref[...]
        dvg = rstd * (dvh - jnp.mean(dvh, axis=-1, keepdims=True) - vhat * jnp.mean(dvh * vhat, axis=-1, keepdims=True))
        dv_ref[...] = (dvg * _gelu_grad(vpre)).astype(BF)

        @pl.when(i == steps - 1)
        def _():
            for g in range(SGU_GROUPS):
                db_ref[g:g + 1, :] = jnp.sum(dfsum_ref[g].T, axis=0, keepdims=True)

    full3 = pl.BlockSpec((SGU_GROUPS, SGU_CHUNK, SGU_CHUNK), lambda i: (0, 0, 0))
    return _call(body, name,
                 (_sds((s, w), BF), _sds((s, w), BF), _sds((SGU_GROUPS, SGU_CHUNK, SGU_CHUNK), F32),
                  _sds((SGU_GROUPS, SGU_CHUNK), F32), _sds((1, w), F32), _sds((1, w), F32)),
                 (steps,),
                 [_rows(tm, w), _rows(tm, w), _vec(w), _vec(w), full3, full3, _rows(tm, w)],
                 (_rows(tm, w), _rows(tm, w), full3, pl.BlockSpec((SGU_GROUPS, SGU_CHUNK), lambda i: (0, 0)), _vec(w), _vec(w)),
                 ("arbitrary",), scratch=[pltpu.VMEM((SGU_GROUPS, SGU_CHUNK, SGU_CHUNK), F32)])(
                     u_pre, v_pre, ln_w, ln_b, wm, bfull, dy)


def _mix_out(name, y_sgu, y_attn, ga_pre, gb_pre, x, g1, proj_a, proj_b, w_out, w2, sc2, sh2, tm=256):
    s, d = x.shape
    tm = min(tm, s)

    def body(ys_ref, ya_ref, ga_ref, gb_ref, x_ref, g1_ref, wa_ref, wb_ref, wo_ref, w2_ref, sc2_ref, sh2_ref,
             m_ref, pa_ref, pb_ref, o_ref, x1_ref, h2_ref):
        pa = _dot(ys_ref[...], wa_ref[...].reshape(d, d))
        pb = _dot(ya_ref[...], wb_ref[...].reshape(d, d))
        pa_ref[...] = pa
        pb_ref[...] = pb
        merged = (_sigmoid(ga_ref[...]) * pa + _sigmoid(gb_ref[...]) * pb).astype(BF)
        m_ref[...] = merged
        o = _dot(merged, wo_ref[...].reshape(d, d))
        o_ref[...] = o
        x1 = x_ref[...] + g1_ref[...] * o
        x1_ref[...] = x1
        h2_ref[...] = ((x1 * _rms(x1)) * w2_ref[...] * (1.0 + sc2_ref[...]) + sh2_ref[...]).astype(BF)

    f, b = _sds((s, d), F32), _sds((s, d), BF)
    r = _rows(tm, d)
    wspec = _resident(proj_a.shape)
    return _call(body, name, (b, f, f, f, f, b), (s // tm,),
                 [r, r, r, r, r, _vec(d), wspec, wspec, wspec, _vec(d), _vec(d), _vec(d)], (r,) * 6, ("parallel",))(
                     y_sgu, y_attn, ga_pre, gb_pre, x, g1, proj_a, proj_b, w_out, w2, sc2, sh2)


def _mix_bwd(name, do, w_out, proj_a, proj_b, ga_pre, gb_pre, pa, pb, tm=256):
    s, d = do.shape
    tm = min(tm, s)

    def body(do_ref, wo_ref, wa_ref, wb_ref, ga_ref, gb_ref, pa_ref, pb_ref,
             dpa_ref, dpb_ref, dga_ref, dgb_ref, dys_ref, dya_ref):
        dm = _dot_nt(do_ref[...], wo_ref[...].reshape(d, d))
        ga = _sigmoid(ga_ref[...])
        gb = _sigmoid(gb_ref[...])
        dpa = (dm * ga).astype(BF)
        dpb = (dm * gb).astype(BF)
        dpa_ref[...] = dpa
        dpb_ref[...] = dpb
        dga_ref[...] = (dm * pa_ref[...] * ga * (1.0 - ga)).astype(BF)
        dgb_ref[...] = (dm * pb_ref[...] * gb * (1.0 - gb)).astype(BF)
        dys_ref[...] = _dot_nt(dpa, wa_ref[...].reshape(d, d))
        dya_ref[...] = _dot_nt(dpb, wb_ref[...].reshape(d, d))

    f, b = _sds((s, d), F32), _sds((s, d), BF)
    r = _rows(tm, d)
    wspec = _resident(w_out.shape)
    return _call(body, name, (b, b, b, b, f, f), (s // tm,), [r, wspec, wspec, wspec, r, r, r, r], (r,) * 6,
                 ("parallel",))(do, w_out, proj_a, proj_b, ga_pre, gb_pre, pa, pb)


def _ffn_up(name, h2, w_gate, w_up, tm=1024):
    s, d = h2.shape
    tm = min(tm, s)
    tc = FFN_SHARD_PAD

    def body(h_ref, wg_ref, wu_ref, a_ref, up_ref):
        hv = h_ref[...]
        a_ref[...] = _dot_nt(hv, wg_ref[0])
        up_ref[...] = _dot_nt(hv, wu_ref[0])

    wspec = pl.BlockSpec((1, tc, d), lambda j, i: (j, 0, 0))
    ospec = pl.BlockSpec((tm, tc), lambda j, i: (i, j))
    o = _sds((s, FFN_PAD), F32)
    return _call(body, name, (o, o), (N_CHIPS, s // tm), [pl.BlockSpec((tm, d), lambda j, i: (i, 0)), wspec, wspec],
                 (ospec, ospec), ("parallel", "parallel"))(h2, w_gate, w_up)


def _conv_act(name, a, up, cw, cb, tm=512, tc=FFN_SHARD_PAD):
    s, c = a.shape
    tm = min(tm, s)
    per = tm // SUBLANES

    def body(a_ref, prev_ref, up_ref, cw_ref, cb_ref, h_ref):
        prev = jnp.where(pl.program_id(1) > 0, prev_ref[...], 0.0)
        ext = jnp.concatenate([prev, a_ref[...]], axis=0)
        ac = (cb_ref[...] + cw_ref[0:1, :] * pltpu.roll(ext, 2, axis=0) + cw_ref[1:2, :] * pltpu.roll(ext, 1, axis=0)
              + cw_ref[2:3, :] * ext)[SUBLANES:]
        h_ref[...] = (ac * _sigmoid(ac) * up_ref[...]).astype(BF)

    tile = pl.BlockSpec((tm, tc), lambda j, i: (i, j))
    prev = pl.BlockSpec((SUBLANES, tc), lambda j, i: (jnp.maximum(i * per - 1, 0), j))
    return _call(body, name, _sds((s, c), BF), (c // tc, s // tm),
                 [tile, prev, tile, pl.BlockSpec((3, tc), lambda j, i: (0, j)), pl.BlockSpec((1, tc), lambda j, i: (0, j))],
                 tile, ("parallel", "parallel"))(a, a, up, cw, cb)


def _conv_act_bwd(name, a, up, dhf, cw, cb, tm=512, tc=FFN_SHARD_PAD):
    s, c = a.shape
    tm = min(tm, s)
    per = tm // SUBLANES
    steps = s // tm
    last8 = s // SUBLANES - 1

    def body(a_ref, aprev_ref, anext_ref, up_ref, upnext_ref, dh_ref, dhnext_ref, cw_ref, cb_ref,
             da_ref, dup_ref, dcw_ref, dcb_ref):
        i = pl.program_id(1)

        @pl.when(i == 0)
        def _():
            dcw_ref[...] = jnp.zeros_like(dcw_ref)
            dcb_ref[...] = jnp.zeros_like(dcb_ref)

        prev = jnp.where(i > 0, aprev_ref[...], 0.0)
        ext = jnp.concatenate([prev, a_ref[...], anext_ref[...]], axis=0)
        a1 = pltpu.roll(ext, 1, axis=0)[SUBLANES:]
        a2 = pltpu.roll(ext, 2, axis=0)[SUBLANES:]
        a0 = ext[SUBLANES:]
        ac = cb_ref[...] + cw_ref[0:1, :] * a2 + cw_ref[1:2, :] * a1 + cw_ref[2:3, :] * a0
        sig = _sigmoid(ac)
        dh = jnp.concatenate([dh_ref[...], jnp.where(i < steps - 1, dhnext_ref[...], 0.0)], axis=0)
        upe = jnp.concatenate([up_ref[...], upnext_ref[...]], axis=0)
        dac = dh * upe * (sig * (1.0 + ac * (1.0 - sig)))
        dup_ref[...] = (dh[:tm] * (ac * sig)[:tm]).astype(BF)
        n = tm + SUBLANES
        da = (cw_ref[2:3, :] * dac + cw_ref[1:2, :] * pltpu.roll(dac, n - 1, axis=0)
              + cw_ref[0:1, :] * pltpu.roll(dac, n - 2, axis=0))
        da_ref[...] = da[:tm].astype(BF)
        dact = dac[:tm]
        dcb_ref[...] += jnp.sum(dact, axis=0, keepdims=True)
        dcw_ref[0:1, :] += jnp.sum(dact * a2[:tm], axis=0, keepdims=True)
        dcw_ref[1:2, :] += jnp.sum(dact * a1[:tm], axis=0, keepdims=True)
        dcw_ref[2:3, :] += jnp.sum(dact * a0[:tm], axis=0, keepdims=True)

    tile = pl.BlockSpec((tm, tc), lambda j, i: (i, j))
    prev = pl.BlockSpec((SUBLANES, tc), lambda j, i: (jnp.maximum(i * per - 1, 0), j))
    nxt = pl.BlockSpec((SUBLANES, tc), lambda j, i: (jnp.minimum((i + 1) * per, last8), j))
    cw_spec = pl.BlockSpec((3, tc), lambda j, i: (0, j))
    cb_spec = pl.BlockSpec((1, tc), lambda j, i: (0, j))
    return _call(body, name, (_sds((s, c), BF), _sds((s, c), BF), _sds((3, c), F32), _sds((1, c), F32)),
                 (c // tc, steps), [tile, prev, nxt, tile, nxt, tile, nxt, cw_spec, cb_spec],
                 (tile, tile, cw_spec, cb_spec), ("parallel", "arbitrary"))(a, a, a, up, up, dhf, dhf, cw, cb)


def _ffn_down(name, hf, w_down, x1, g2, tm=512):
    s, d = x1.shape
    tm = min(tm, s)

    def body(hf_ref, wd_ref, x1_ref, g2_ref, dn_ref, x2_ref):
        dn = _dot(hf_ref[...], wd_ref[...].reshape(FFN_PAD, d))
        dn_ref[...] = dn
        x2_ref[...] = x1_ref[...] + g2_ref[...] * dn

    o = _sds((s, d), F32)
    return _call(body, name, (o, o), (s // tm,),
                 [_rows(tm, FFN_PAD), _resident(w_down.shape), _rows(tm, d), _vec(d)],
                 (_rows(tm, d), _rows(tm, d)), ("parallel",))(hf, w_down, x1, g2)


def _ffn_down_bwd(name, dx2, dn, g2, w_down, tm=512, after=None):
    s, d = dx2.shape
    tm = min(tm, s)
    tc = FFN_SHARD_PAD

    def body(dx_ref, dn_ref, g2_ref, wd_ref, ddn_ref, dhf_ref, dg_ref):
        @pl.when(pl.program_id(0) == 0)
        def _():
            dg_ref[...] = jnp.zeros_like(dg_ref)

        dxv = dx_ref[...]
        ddn = (dxv * g2_ref[...]).astype(BF)
        ddn_ref[...] = ddn
        dg_ref[...] += jnp.sum(dxv * dn_ref[...], axis=0, keepdims=True)
        for k in range(N_CHIPS):
            dhf_ref[:, k * tc:(k + 1) * tc] = _dot_nt(ddn, wd_ref[k])

    return _call(body, name, (_sds((s, d), BF), _sds((s, FFN_PAD), F32), _sds((1, d), F32)), (s // tm,),
                 [_rows(tm, d), _rows(tm, d), _vec(d), _resident(w_down.shape)],
                 (_rows(tm, d), _rows(tm, FFN_PAD), _vec(d)), ("arbitrary",), after=after)(dx2, dn, g2, w_down)


def _ffn_up_bwd(name, da, dup, w_gate, w_up, x1, dx2, w2, sc2, o, g1, tm=256):
    s, d = x1.shape
    tm = min(tm, s)
    tc = FFN_SHARD_PAD

    def body(da_ref, dup_ref, wg_ref, wu_ref, x1_ref, dx2_ref, w2_ref, sc2_ref, o_ref, g1_ref,
             dx1_ref, do_ref, dnw_ref, dsh_ref, dg1_ref):
        @pl.when(pl.program_id(0) == 0)
        def _():
            dnw_ref[...] = jnp.zeros_like(dnw_ref)
            dsh_ref[...] = jnp.zeros_like(dsh_ref)
            dg1_ref[...] = jnp.zeros_like(dg1_ref)

        dh = jnp.zeros((tm, d), F32)
        for k in range(N_CHIPS):
            cols = slice(k * tc, (k + 1) * tc)
            dh = dh + _dot(da_ref[:, cols], wg_ref[k]) + _dot(dup_ref[:, cols], wu_ref[k])
        xv = x1_ref[...]
        r = _rms(xv)
        xn = xv * r
        dxn = dh * (w2_ref[...] * (1.0 + sc2_ref[...]))
        dx1 = dx2_ref[...] + r * (dxn - xn * jnp.mean(dxn * xn, axis=-1, keepdims=True))
        dx1_ref[...] = dx1
        dnw_ref[...] += jnp.sum(dh * xn, axis=0, keepdims=True)
        dsh_ref[...] += jnp.sum(dh, axis=0, keepdims=True)
        do_ref[...] = (dx1 * g1_ref[...]).astype(BF)
        dg1_ref[...] += jnp.sum(dx1 * o_ref[...], axis=0, keepdims=True)

    v = _sds((1, d), F32)
    r = _rows(tm, d)
    wspec = _resident(w_gate.shape)
    return _call(body, name, (_sds((s, d), F32), _sds((s, d), BF), v, v, v), (s // tm,),
                 [_rows(tm, FFN_PAD), _rows(tm, FFN_PAD), wspec, wspec, r, r, _vec(d), _vec(d), r, _vec(d)],
                 (r, r, _vec(d), _vec(d), _vec(d)), ("arbitrary",))(da, dup, w_gate, w_up, x1, dx2, w2, sc2, o, g1)


def _loss_head(name, x, w, target, tm=512):
    s, d = x.shape
    tm = min(tm, s)

    def body(x_ref, w_ref, t_ref, dx_ref, loss_ref, dw_ref):
        @pl.when(pl.program_id(0) == 0)
        def _():
            loss_ref[...] = jnp.zeros_like(loss_ref)
            dw_ref[...] = jnp.zeros_like(dw_ref)

        xv = x_ref[...]
        r = _rms(xv)
        xn = xv * r
        err = xn * w_ref[...] - t_ref[...]
        loss_ref[...] += 0.5 * jnp.sum(jnp.mean(err * err, axis=-1, keepdims=True))
        dy = err * (1.0 / d)
        dw_ref[...] += jnp.sum(dy * xn, axis=0, keepdims=True)
        dxn = dy * w_ref[...]
        dx_ref[...] = r * (dxn - xn * jnp.mean(dxn * xn, axis=-1, keepdims=True))

    return _call(body, name, (_sds((s, d), F32), _sds((1, LANES), F32), _sds((1, d), F32)), (s // tm,),
                 [_rows(tm, d), _vec(d), _rows(tm, d)], (_rows(tm, d), _vec(LANES), _vec(d)), ("arbitrary",))(x, w, target)


def _layer_fwd(l, x, mod, p, cosf, sinf, after=None):
    sh1, sc1, g1, sh2, sc2, g2 = mod
    tag = f"l{l}_"
    h, qr, kk0, kk1, vv0, vv1, u_pre, v_pre, ga_pre, gb_pre = _in_proj(
        tag + "in_proj", x, p["norm1_w"], sc1, sh1, p["w_in"], cosf, sinf, after=after)
    y_attn = _attention(tag + "attn", qr, kk0, kk1, vv0, vv1, p["sinks"])
    y_sgu = _sgu(tag + "sgu", u_pre, v_pre, p["sgu_ln_w"], p["sgu_ln_b"], p["sgu_w"], p["sgu_bfull"])
    merged, pa, pb, o, x1, h2 = _mix_out(tag + "mix_out", y_sgu, y_attn, ga_pre, gb_pre, x, g1, p["proj_a"], p["proj_b"],
                                         p["w_out"], p["norm2_w"], sc2, sh2)
    a, up = _ffn_up(tag + "ffn_up", h2, p["w_gate"], p["w_up"])
    hf = _conv_act(tag + "conv_act", a, up, p["conv_w"], p["conv_b"])
    dn, x2 = _ffn_down(tag + "ffn_down", hf, p["w_down"], x1, g2)
    saved = dict(x=x, h=h, qr=qr, kk0=kk0, kk1=kk1, vv0=vv0, vv1=vv1, u_pre=u_pre, v_pre=v_pre, ga_pre=ga_pre,
                 gb_pre=gb_pre, y_attn=y_attn, y_sgu=y_sgu, merged=merged, pa=pa, pb=pb, o=o, x1=x1, h2=h2, a=a, up=up,
                 hf=hf, dn=dn)
    return x2, saved


def _layer_bwd(l, dx2, mod, p, sv, cosf, sinf, after=None):
    sh1, sc1, g1, sh2, sc2, g2 = mod
    tag = f"l{l}_b_"
    d = D_MODEL
    g = {}
    ddn, dhf, dg2 = _ffn_down_bwd(tag + "ffn_down", dx2, sv["dn"], g2, p["w_down"], after=after)
    g["w_down"] = _matmul_tn(tag + "dw_down", sv["hf"], ddn, tk=FFN_SHARD_PAD).reshape(N_CHIPS, FFN_SHARD_PAD, d)
    da, dup, g["conv_w"], g["conv_b"] = _conv_act_bwd(tag + "conv_act", sv["a"], sv["up"], dhf, p["conv_w"], p["conv_b"])
    g["w_gate"] = _matmul_tn(tag + "dw_gate", da, sv["h2"], tk=FFN_SHARD_PAD).reshape(N_CHIPS, FFN_SHARD_PAD, d)
    g["w_up"] = _matmul_tn(tag + "dw_up", dup, sv["h2"], tk=FFN_SHARD_PAD).reshape(N_CHIPS, FFN_SHARD_PAD, d)
    dx1, do, da2, dsh2, dg1 = _ffn_up_bwd(tag + "ffn_up", da, dup, p["w_gate"], p["w_up"], sv["x1"], dx2, p["norm2_w"],
                                          sc2, sv["o"], g1)
    g["norm2_w"] = da2 * (1.0 + sc2)
    dsc2 = da2 * p["norm2_w"]
    g["w_out"] = _matmul_tn(tag + "dw_out", sv["merged"], do).reshape(N_CHIPS, d // N_CHIPS, d)
    dpa, dpb, dga, dgb, dy_sgu, dy_attn = _mix_bwd(tag + "mix", do, p["w_out"], p["proj_a"], p["proj_b"], sv["ga_pre"],
                                                  sv["gb_pre"], sv["pa"], sv["pb"])
    g["proj_a"] = _matmul_tn(tag + "dproj_a", sv["y_sgu"], dpa).reshape(N_CHIPS, d // N_CHIPS, d)
    g["proj_b"] = _matmul_tn(tag + "dproj_b", sv["y_attn"], dpb).reshape(N_CHIPS, d // N_CHIPS, d)
    du, dv, g["sgu_w"], g["sgu_b"], g["sgu_ln_w"], g["sgu_ln_b"] = _sgu_bwd(
        tag + "sgu", sv["u_pre"], sv["v_pre"], p["sgu_ln_w"], p["sgu_ln_b"], p["sgu_w"], p["sgu_bfull"], dy_sgu)
    dqr, dkv_cur, dkv_prev, dsink = _attention_bwd(tag + "attn", sv["qr"], sv["kk0"], sv["kk1"], sv["vv0"], sv["vv1"],
                                                   p["sinks"], dy_attn)
    g["sinks"] = dsink[0, :N_Q_HEADS]
    dq, dkv = _rope_bwd(tag + "rope", dqr, dkv_cur, dkv_prev, cosf, sinf)
    dw_in, row0 = None, 0
    for n, t in (("q", dq), ("kv", dkv), ("u", du), ("v", dv), ("ga", dga), ("gb", dgb)):
        dw_in = _matmul_tn_rows(tag + "dw_in_" + n, t, sv["h"], dw_in, row0, IN_COLS)
        row0 += t.shape[1]
    g["w_in"] = dw_in.reshape(N_CHIPS, IN_COLS // N_CHIPS, d)
    dx, da1, dsh1 = _in_proj_bwd(tag + "in_proj", dq, dkv, du, dv, dga, dgb, p["w_in"], sv["x"], p["norm1_w"], sc1, dx1)
    g["norm1_w"] = da1 * (1.0 + sc1)
    dsc1 = da1 * p["norm1_w"]
    return dx, (dsh1, dsc1, dg1, dsh2, dsc2, dg2), g


def _pad_to(a, axis, size):
    pad = [(0, 0)] * a.ndim
    pad[axis] = (0, size - a.shape[axis])
    return jnp.pad(a, pad)


def _layer_params(w_in, proj_a, proj_b, w_out, w_gate, w_up, w_down, conv_w, small):
    d = D_MODEL
    return dict(
        w_in=w_in.reshape(IN_COLS, d), proj_a=proj_a, proj_b=proj_b, w_out=w_out, w_gate=w_gate, w_up=w_up,
        w_down=w_down, conv_w=conv_w.transpose(1, 0, 2).reshape(3, FFN_PAD),
        conv_b=_pad_to(small["conv_b"].reshape(N_CHIPS, FFN_SHARD), 1, FFN_SHARD_PAD).reshape(1, FFN_PAD),
        norm1_w=small["norm1_w"].reshape(1, d), norm2_w=small["norm2_w"].reshape(1, d), sinks=small["sinks"],
        sgu_ln_w=small["sgu_ln_w"].reshape(1, d), sgu_ln_b=small["sgu_ln_b"].reshape(1, d), sgu_w=small["sgu_w"],
        sgu_bfull=jnp.broadcast_to(small["sgu_b"][:, :, None], (SGU_GROUPS, SGU_CHUNK, SGU_CHUNK)))


def _conv_grads_natural(g):
    cw = g["conv_w"].reshape(3, N_CHIPS, FFN_SHARD_PAD)[:, :, :FFN_SHARD].reshape(3, FFN_DIM)
    cb = g["conv_b"].reshape(N_CHIPS, FFN_SHARD_PAD)[:, :FFN_SHARD].reshape(FFN_DIM)
    return cw, cb


def _rope_tables(positions):
    inv_freq = ROPE_THETA ** (-jnp.arange(0, ROT_DIM, 2, dtype=F32) / ROT_DIM)
    ang = positions.astype(F32)[:, None] * inv_freq
    cos, sin = jnp.cos(ang), jnp.sin(ang)
    s = positions.shape[0]
    rest = HEAD_DIM - ROT_DIM
    cos_head = jnp.concatenate([cos, cos, jnp.ones((s, rest), F32)], axis=1)
    sin_head = jnp.concatenate([-sin, sin, jnp.zeros((s, rest), F32)], axis=1)
    return jnp.tile(cos_head, (1, LANES // HEAD_DIM)), jnp.tile(sin_head, (1, LANES // HEAD_DIM))


ADA_ROWS = 16


def _ada_fwd(name, c_rows, ada_w, ada_b_cols, tn=512):
    depth, d, n = ada_w.shape

    def body(c_ref, w_ref, b_ref, o_ref):
        cv = c_ref[...]
        act = (cv * _sigmoid(cv)).astype(BF)
        o_ref[0] = _dot(act, w_ref[0].astype(BF)) + b_ref[0]

    return _call(body, name, _sds((depth, ADA_ROWS, n), F32), (depth, n // tn),
                 [pl.BlockSpec((ADA_ROWS, d), lambda l, j: (0, 0)), pl.BlockSpec((1, d, tn), lambda l, j: (l, 0, j)),
                  pl.BlockSpec((1, 1, tn), lambda l, j: (l, 0, j))],
                 pl.BlockSpec((1, ADA_ROWS, tn), lambda l, j: (l, 0, j)), ("parallel", "parallel"))(c_rows, ada_w, ada_b_cols)


def _ada_bwd(name, c_rows, dmod_cols, tn=512):
    depth, _, n = dmod_cols.shape
    d = c_rows.shape[1]

    def body(c_ref, dm_ref, o_ref):
        cv = c_ref[...]
        act = (cv * _sigmoid(cv)).astype(BF)
        o_ref[0] = _dot_tn(act, dm_ref[0].astype(BF))

    return _call(body, name, _sds((depth, d, n), F32), (depth, n // tn),
                 [pl.BlockSpec((ADA_ROWS, d), lambda l, j: (0, 0)), pl.BlockSpec((1, ADA_ROWS, tn), lambda l, j: (l, 0, j))],
                 pl.BlockSpec((1, d, tn), lambda l, j: (l, 0, j)), ("parallel", "parallel"))(c_rows, dmod_cols)


def _colsum(name, a):
    r, n = a.shape

    def body(a_ref, o_ref):
        o_ref[...] = jnp.sum(a_ref[...], axis=0, keepdims=True)

    return _call(body, name, _sds((1, n), F32), (1,), [pl.BlockSpec((r, n), lambda i: (0, 0))],
                 pl.BlockSpec((1, n), lambda i: (0, 0)), ("arbitrary",))(a)


REL_SIBLING = (0, 0, 1)
REL_CHIPS = ((1, 0, 0), (0, 1, 0), (1, 1, 0))
REL_ALL = tuple((fx, fy, fc) for fx in (0, 1) for fy in (0, 1) for fc in (0, 1) if fx or fy or fc)


def _chip_of(dev):
    return 2 * dev[0] + dev[1]


def _dev_of(dev):
    return 4 * dev[0] + 2 * dev[1] + dev[2]


def _flip(dev, rel):
    return tuple(1 - m if f else m for m, f in zip(dev, rel))


def _exchange(name, arrays, n_out, stages, aliases=None):
    out_shapes, stages = stages[0], stages[1:]
    n_in = len(arrays)
    aliases = aliases or {}
    n_remote = sum(len(plan) for plan, _ in stages)
    n_local = sum(len(local) for _, local in stages)

    def at(ref, idx):
        return ref.at[idx] if len(idx) else ref

    def body(*refs):
        bufs = list(refs[:n_in + n_out])
        for i_in, i_out in aliases.items():
            bufs[i_in] = bufs[n_in + i_out]
        send_sems, recv_sems, local_sems = refs[n_in + n_out:]
        me = (lax.axis_index("x"), lax.axis_index("y"), lax.axis_index("c"))
        base_r = base_l = 0
        pending = []
        for plan, local in stages:
            def remote(k, entry, sender, receiver):
                rel, si, ssel, di, dsel = entry
                return pltpu.make_async_remote_copy(
                    src_ref=at(bufs[si], ssel(sender, receiver)), dst_ref=at(bufs[di], dsel(sender, receiver)),
                    send_sem=send_sems.at[k], recv_sem=recv_sems.at[k], device_id=_flip(me, rel), device_id_type=MESH)

            sends = [remote(base_r + k, e, me, _flip(me, e[0])) for k, e in enumerate(plan)]
            for cp in sends:
                cp.start()
            for k, (si, ssel, di, dsel) in enumerate(local):
                cp = pltpu.make_async_copy(at(bufs[si], ssel(me)), at(bufs[di], dsel(me)), local_sems.at[base_l + k])
                cp.start()
                pending.append(cp.wait)
            for k, e in enumerate(plan):
                remote(base_r + k, e, _flip(me, e[0]), me).wait_recv()
            pending += [cp.wait_send for cp in sends]
            base_r += len(plan)
            base_l += len(local)
        for wait in pending:
            wait()

    any_spec = pl.BlockSpec(memory_space=pl.ANY)
    return pl.pallas_call(
        body, name=name, out_shape=tuple(out_shapes), in_specs=[any_spec] * n_in, out_specs=tuple([any_spec] * n_out),
        input_output_aliases=dict(aliases),
        scratch_shapes=[pltpu.SemaphoreType.DMA((max(n_remote, 1),)), pltpu.SemaphoreType.DMA((max(n_remote, 1),)),
                        pltpu.SemaphoreType.DMA((max(n_local, 1),))])(*arrays)


HBM_SPEC = pl.BlockSpec(memory_space=pltpu.HBM)
SEM_SPEC = pl.BlockSpec(memory_space=pltpu.SEMAPHORE)


def _split_copies(bufs, plan, local, send_sems, recv_sems, local_sems):
    me = (lax.axis_index("x"), lax.axis_index("y"), lax.axis_index("c"))

    def at(ref, idx):
        return ref.at[idx] if len(idx) else ref

    def remote(k, sender, receiver):
        rel, si, ssel, di, dsel = plan[k]
        return pltpu.make_async_remote_copy(
            src_ref=at(bufs[si], ssel(sender, receiver)), dst_ref=at(bufs[di], dsel(sender, receiver)),
            send_sem=send_sems.at[k], recv_sem=recv_sems.at[k], device_id=_flip(me, rel), device_id_type=MESH)

    sends = [remote(k, me, _flip(me, plan[k][0])) for k in range(len(plan))]
    arrivals = [remote(k, _flip(me, plan[k][0]), me) for k in range(len(plan))]
    locs = [pltpu.make_async_copy(at(bufs[si], ssel(me)), at(bufs[di], dsel(me)), local_sems.at[k])
            for k, (si, ssel, di, dsel) in enumerate(local)]
    return sends, arrivals, locs


def _exchange_start(name, arrays, out_shapes, plan, local):
    n_in, n_out = len(arrays), len(out_shapes)

    n_buf = n_in + n_out

    def body(*refs):
        sems = refs[n_buf:n_buf + 3]
        bufs = refs[n_buf + 3:2 * n_buf + 3]
        token = refs[-1]
        sends, _, locs = _split_copies(bufs, plan, local, *sems)
        for cp in sends + locs:
            cp.start()
        token[...] = jnp.zeros_like(token)

    zones = [lax.empty(o.shape, o.dtype) for o in out_shapes]
    operands = [pltpu.with_memory_space_constraint(a, pltpu.HBM) for a in list(arrays) + zones]
    sem = lambda n: pltpu.SemaphoreType.DMA((max(n, 1),))
    out = pl.pallas_call(
        body, name=name,
        out_shape=(sem(len(plan)), sem(len(plan)), sem(len(local)), *[pltpu.HBM(a.shape, a.dtype) for a in operands],
                   _sds((SUBLANES, LANES), F32)),
        in_specs=[HBM_SPEC] * (n_in + n_out),
        out_specs=(SEM_SPEC, SEM_SPEC, SEM_SPEC, *[HBM_SPEC] * (n_in + n_out), pl.BlockSpec(memory_space=pltpu.VMEM)),
        input_output_aliases={i: 3 + i for i in range(n_in + n_out)},
        compiler_params=pltpu.CompilerParams(has_side_effects=pltpu.SideEffectType.DATAFLOW_SIDE_EFFECTING))(*operands)
    return out[:3], out[3:3 + n_in], out[3 + n_in:3 + n_in + n_out], out[-1]


def _exchange_wait(name, sems, thru, zones, plan, local, after):
    n_in, n_out = len(thru), len(zones)

    def body(*refs):
        bufs = refs[:n_in + n_out]
        sends, arrivals, locs = _split_copies(bufs, plan, local, *refs[n_in + n_out:n_in + n_out + 3])
        for cp in arrivals:
            cp.wait_recv()
        for cp in sends:
            cp.wait_send()
        for cp in locs:
            cp.wait()

    out = pl.pallas_call(
        body, name=name, out_shape=tuple(pltpu.HBM(a.shape, a.dtype) for a in list(thru) + list(zones)),
        in_specs=[HBM_SPEC] * (n_in + n_out) + [SEM_SPEC] * 3 + [pl.BlockSpec(memory_space=pl.ANY)],
        out_specs=tuple([HBM_SPEC] * (n_in + n_out)), input_output_aliases={i: i for i in range(n_in + n_out)},
        compiler_params=pltpu.CompilerParams(has_side_effects=pltpu.SideEffectType.DATAFLOW_SIDE_EFFECTING))(
            *thru, *zones, *sems, after)
    return out[n_in:]


def _whole(*_):
    return ()


def _half_rows(rows, core):
    return pl.ds(core * (rows // 2), rows // 2)


def _gather_weights_plan(shards):
    n = len(shards)
    dsts = [_sds((N_CHIPS,) + a.shape, a.dtype) for a in shards]
    fetch, forward = [], []
    for t, a in enumerate(shards):
        rows = a.shape[0]
        if rows % (2 * 16) == 0:
            fetch += [(rel, t, (lambda s_, r_, rows=rows: (_half_rows(rows, s_[2]),)), n + t,
                       (lambda s_, r_, rows=rows: (_chip_of(s_), _half_rows(rows, s_[2])))) for rel in REL_CHIPS]
            forward += [(REL_SIBLING, n + t, (lambda s_, r_, rows=rows, rel=rel: (_chip_of(_flip(s_, rel)), _half_rows(rows, s_[2]))),
                         n + t, (lambda s_, r_, rows=rows, rel=rel: (_chip_of(_flip(s_, rel)), _half_rows(rows, s_[2]))))
                        for rel in REL_CHIPS]
        else:
            fetch += [(rel, t, _whole, n + t, lambda s_, r_: (_chip_of(s_),)) for rel in REL_CHIPS]
    local = [(t, _whole, n + t, lambda me: (_chip_of(me),)) for t in range(n)]
    return dsts, fetch, local, forward


def _gather_weights_start(name, shards):
    dsts, fetch, local, forward = _gather_weights_plan(shards)
    sems, thru, zones, token = _exchange_start(name, shards, dsts, fetch, local)
    return (sems, thru, zones, fetch, local, forward), token


def _gather_weights_finish(name, pending, after):
    sems, thru, zones, fetch, local, forward = pending
    landed = _exchange_wait(name + "_wait", sems, thru, zones, fetch, local, after)
    n = len(landed)
    return _exchange(name + "_forward", landed, n, [[_sds(a.shape, a.dtype) for a in landed], (forward, [])],
                     aliases={t: t for t in range(n)})


def _gather_chips(name, arrays):
    n = len(arrays)
    dsts = [_sds((N_CHIPS,) + a.shape, a.dtype) for a in arrays]
    plan = [(rel, t, _whole, n + t, lambda s_, r_: (_chip_of(s_),)) for t in range(n) for rel in REL_CHIPS]
    local = [(t, _whole, n + t, lambda me: (_chip_of(me),)) for t in range(n)]
    return _exchange(name, arrays, n, [dsts, (plan, local)])


def _gather_all(name, a):
    plan = [(rel, 0, _whole, 1, lambda s_, r_: (_dev_of(s_),)) for rel in REL_ALL]
    local = [(0, _whole, 1, lambda me: (_dev_of(me),))]
    return _exchange(name, [a], 1, [[_sds((2 * N_CHIPS,) + a.shape, a.dtype)], (plan, local)])[0]


def _swap_halves(name, grads):
    n = len(grads)
    dsts = [_sds((g.shape[0], g.shape[1] // 2, g.shape[2]), g.dtype) for g in grads]
    plan = [(REL_SIBLING, t, (lambda s_, r_, rows=g.shape[1]: (pl.ds(0, N_CHIPS), _half_rows(rows, r_[2]))), n + t, _whole)
            for t, g in enumerate(grads)]
    return _exchange(name, grads, n, [dsts, (plan, [])])


def _scatter_chips_plan(sums):
    n = len(sums)
    dsts = [_sds(a.shape, a.dtype) for a in sums]
    plan = [(rel, t, lambda s_, r_: (_chip_of(r_),), n + t, lambda s_, r_: (_chip_of(s_),))
            for t in range(n) for rel in REL_CHIPS]
    local = [(t, lambda me: (_chip_of(me),), n + t, lambda me: (_chip_of(me),)) for t in range(n)]
    return dsts, plan, local


def _scatter_chips(name, sums):
    dsts, plan, local = _scatter_chips_plan(sums)
    return _exchange(name, sums, len(sums), [dsts, (plan, local)])


def _scatter_chips_start(name, sums):
    dsts, plan, local = _scatter_chips_plan(sums)
    sems, thru, zones, token = _exchange_start(name, sums, dsts, plan, local)
    return (sems, thru, zones, plan, local), token


def _scatter_chips_finish(name, pending, after):
    sems, thru, zones, plan, local = pending
    return _exchange_wait(name + "_wait", sems, thru, zones, plan, local, after)


def _swap_back(name, totals, layer):
    n = len(totals)
    dsts = [_sds(a.shape, a.dtype) for a in totals]
    plan = [(REL_SIBLING, n + t, (lambda s_, r_, rows=a.shape[1]: (layer, _half_rows(rows, s_[2]))),
             n + t, (lambda s_, r_, rows=a.shape[1]: (layer, _half_rows(rows, s_[2])))) for t, a in enumerate(totals)]
    return _exchange(name, totals, n, [dsts, (plan, [])], aliases={t: t for t in range(n)})


def _add_halves(name, g, recv, core):
    nch, half, c = recv.shape

    def body(core_ref, g_ref, r_ref, o_ref):
        o_ref[0] = (g_ref[0, 0].astype(F32) + r_ref[0].astype(F32)).astype(o_ref.dtype)

    spec = pltpu.PrefetchScalarGridSpec(
        num_scalar_prefetch=1, grid=(nch,),
        in_specs=[pl.BlockSpec((1, 1, half, c), lambda k, core_ref: (k, core_ref[0], 0, 0)),
                  pl.BlockSpec((1, half, c), lambda k, core_ref: (k, 0, 0))],
        out_specs=pl.BlockSpec((1, half, c), lambda k, core_ref: (k, 0, 0)))
    return pl.pallas_call(body, name=name, out_shape=_sds(recv.shape, recv.dtype), grid_spec=spec,
                          compiler_params=pltpu.CompilerParams(dimension_semantics=("parallel",),
                                                               vmem_limit_bytes=VMEM_LIMIT))(
                                                                   core, g.reshape(nch, 2, half, c), recv)


def _sum_chips(name, a, core, layer, total):
    nch, half, c = a.shape

    def body(core_ref, a_ref, *rest):
        o_ref = rest[-1]
        acc = a_ref[0].astype(F32)
        for k in range(1, nch):
            acc = acc + a_ref[k].astype(F32)
        o_ref[0, 0] = acc

    in_specs = [pl.BlockSpec((nch, half, c), lambda i, core_ref: (0, 0, 0))]
    args = [core, a]
    if total is not None:
        in_specs.append(pl.BlockSpec(memory_space=pl.ANY))
        args.append(total.reshape(DEPTH, 2, half, c))
    spec = pltpu.PrefetchScalarGridSpec(
        num_scalar_prefetch=1, grid=(1,), in_specs=in_specs,
        out_specs=pl.BlockSpec((1, 1, half, c), lambda i, core_ref: (layer, core_ref[0], 0, 0)))
    out = pl.pallas_call(body, name=name, out_shape=_sds((DEPTH, 2, half, c), F32), grid_spec=spec,
                         input_output_aliases={2: 0} if total is not None else {},
                         compiler_params=pltpu.CompilerParams(dimension_semantics=("arbitrary",),
                                                              vmem_limit_bytes=VMEM_LIMIT))(*args)
    return out.reshape(DEPTH, 2 * half, c)


def _adamw_update(w, g, m, v):
    mn = ADAM_B1 * m + (1.0 - ADAM_B1) * g
    vn = ADAM_B2 * v + (1.0 - ADAM_B2) * (g * g)
    m_hat = mn / (1.0 - ADAM_B1 ** ADAM_STEP)
    v_hat = vn / (1.0 - ADAM_B2 ** ADAM_STEP)
    return -ADAM_LR * (m_hat / (jnp.sqrt(v_hat) + ADAM_EPS) + ADAM_WD * w), mn, vn


def _adamw(name, w, g, m, v):
    depth, r, c = w.shape
    tr = next(t for t in (512, 448, 384, 352, 336, 256, 192, 128, 64, 32, 16, 8) if r % t == 0 and t * c <= ADAM_TILE_ELEMS)

    def body(w_ref, g_ref, m_ref, v_ref, go_ref, d_ref, mo_ref, vo_ref):
        gv = g_ref[...]
        go_ref[...] = gv
        d_ref[...], mo_ref[...], vo_ref[...] = _adamw_update(w_ref[...], gv, m_ref[...], v_ref[...])

    spec = pl.BlockSpec((1, tr, c), lambda l, i: (l, i, 0))
    o = _sds(w.shape, F32)
    return _call(body, name, (o, o, o, o), (depth, r // tr), [spec] * 4, (spec,) * 4, ("parallel", "parallel"))(w, g, m, v)


def _adamw_small(name, ws, gs, ms, vs):
    n = len(ws)

    def body(*refs):
        for t in range(n):
            w_ref, g_ref, m_ref, v_ref = (refs[k * n + t] for k in range(4))
            d_ref, mo_ref, vo_ref = (refs[(4 + k) * n + t] for k in range(3))
            d_ref[...], mo_ref[...], vo_ref[...] = _adamw_update(w_ref[...], g_ref[...], m_ref[...], v_ref[...])

    outs = [_sds(w.shape, F32) for w in ws]
    res = pl.pallas_call(body, name=name, out_shape=tuple(outs * 3))(*ws, *gs, *ms, *vs)
    return res[:n], res[n:2 * n], res[2 * n:]


def _pack(arrays, rows):
    flat = jnp.concatenate([a.reshape(-1).astype(F32) for a in arrays])
    return _pad_to(flat, 0, rows * LANES).reshape(rows, LANES)


def _unpack(packed, shapes):
    flat = packed.reshape(-1)
    out, off = [], 0
    for shp in shapes:
        n = 1
        for s_ in shp:
            n *= s_
        out.append(flat[off:off + n].reshape(shp))
        off += n
    return out


_MATRICES = ("w_in", "proj_a", "proj_b", "w_out", "w_gate", "w_up", "w_down")
_SMALL = (("norm1_w", (D_MODEL,)), ("sinks", (N_Q_HEADS,)), ("sgu_ln_w", (SGU_WIDTH,)), ("sgu_ln_b", (SGU_WIDTH,)),
          ("sgu_w", (SGU_GROUPS, SGU_CHUNK, SGU_CHUNK)), ("sgu_b", (SGU_GROUPS, SGU_CHUNK)), ("norm2_w", (D_MODEL,)),
          ("conv_w", (3, FFN_DIM)), ("conv_b", (FFN_DIM,)), ("final_norm_w", (D_MODEL,)))
SMALL_ROWS = 320
ADAM_TILE_ELEMS = 384 * 1024


def _reduce_cores(l, grads, small, core):
    tag = f"l{l}_reduce_"
    partial = [grads[k] for k in _MATRICES] + [small]
    from_sibling = _swap_halves(tag + "cores", partial)
    return [_add_halves(f"{tag}cores_add{t}", partial[t], from_sibling[t], core) for t in range(len(partial))]


def _reduce_finish(l, from_chips, core, totals):
    tag = f"l{l}_reduce_"
    totals = [_sum_chips(f"{tag}chips_add{t}", from_chips[t], core, l, None if totals is None else totals[t])
              for t in range(len(from_chips))]
    return _swap_back(tag + "back", totals, l)


def kernel(x, c, positions, ada_w, ada_b, norm1_w, w_in, attn_sinks, sgu_ln_w, sgu_ln_b, sgu_w, sgu_b, proj_a, proj_b, w_out, norm2_w, ffn_w_gate, ffn_w_up, ffn_conv_w, ffn_conv_b, ffn_w_down, final_norm_w, loss_target, m_ada_w, m_ada_b, m_norm1_w, m_w_in, m_attn_sinks, m_sgu_ln_w, m_sgu_ln_b, m_sgu_w, m_sgu_b, m_proj_a, m_proj_b, m_w_out, m_norm2_w, m_ffn_w_gate, m_ffn_w_up, m_ffn_conv_w, m_ffn_conv_b, m_ffn_w_down, m_final_norm_w, v_ada_w, v_ada_b, v_norm1_w, v_w_in, v_attn_sinks, v_sgu_ln_w, v_sgu_ln_b, v_sgu_w, v_sgu_b, v_proj_a, v_proj_b, v_w_out, v_norm2_w, v_ffn_w_gate, v_ffn_w_up, v_ffn_conv_w, v_ffn_conv_b, v_ffn_w_down, v_final_norm_w):
    d = D_MODEL
    ax, ay, ac = lax.axis_index("x"), lax.axis_index("y"), lax.axis_index("c")
    chip = 2 * ax + ay
    dev = 4 * ax + 2 * ay + ac
    core = ac.astype(jnp.int32).reshape(1)

    tr = lambda a: jnp.swapaxes(a, 1, 2)
    shards = [tr(w_in).astype(BF), proj_a.astype(BF), proj_b.astype(BF), w_out.astype(BF),
              _pad_to(tr(ffn_w_gate).astype(BF), 1, FFN_SHARD_PAD), _pad_to(tr(ffn_w_up).astype(BF), 1, FFN_SHARD_PAD),
              _pad_to(ffn_w_down.astype(BF), 1, FFN_SHARD_PAD), _pad_to(ffn_conv_w, 2, FFN_SHARD_PAD)]
    pending, token = _gather_weights_start("l0_gather", [a[0] for a in shards])

    c_all = _gather_all("gather_cond", jnp.broadcast_to(c + token[0, 0], (SUBLANES, d)))[:, 0, :]
    c_rows = _pad_to(c_all, 0, ADA_ROWS)
    ada_cols = ada_w.shape[2]
    ada_b_cols = lax.dynamic_slice_in_dim(ada_b, chip * ada_cols, ada_cols, axis=1).reshape(DEPTH, 1, ada_cols)
    mod_cols = _ada_fwd("ada_fwd", c_rows, ada_w, ada_b_cols)
    mod_all = _gather_chips("gather_mod", [mod_cols])[0]
    mod_mine = lax.dynamic_index_in_dim(mod_all, dev, axis=2, keepdims=False)
    mod_mine = mod_mine.transpose(1, 0, 2).reshape(DEPTH, 1, 6 * d)
    mods = [tuple(jnp.split(mod_mine[l], 6, axis=-1)) for l in range(DEPTH)]

    small_in = dict(norm1_w=norm1_w, sinks=attn_sinks, sgu_ln_w=sgu_ln_w, sgu_ln_b=sgu_ln_b, sgu_w=sgu_w, sgu_b=sgu_b,
                    norm2_w=norm2_w, conv_b=ffn_conv_b)
    cosf, sinf = _rope_tables(positions[0])
    layer_params = lambda l, g: _layer_params(*g, {k: v[l] for k, v in small_in.items()})

    gathered = _gather_weights_finish("l0_gather", pending, mod_all)
    params = [layer_params(0, gathered)]
    behind = gathered[-1][0, 0, 0] * 0.0
    pending, token = _gather_weights_start("l1_gather", [a[1] for a in shards[:-1]] + [shards[-1][1] + behind])
    h, sv = _layer_fwd(0, x[0], mods[0], params[0], cosf, sinf, after=token)
    saved = [sv]
    params.append(layer_params(1, _gather_weights_finish("l1_gather", pending, h)))
    h, sv = _layer_fwd(1, h, mods[1], params[1], cosf, sinf)
    saved.append(sv)
    dx, loss_part, d_final = _loss_head("loss_head", h, final_norm_w.reshape(1, d), loss_target[0])
    loss = lax.psum(loss_part[0, 0], ("x", "y", "c"))

    def chip_sums_of(l, grads):
        cw, cb = _conv_grads_natural(grads)
        nat = dict(grads, conv_w=cw, conv_b=cb, final_norm_w=d_final if l == DEPTH - 1 else jnp.zeros((d,), F32))
        small = _pack([nat[k] for k, _ in _SMALL], N_CHIPS * SMALL_ROWS).reshape(N_CHIPS, SMALL_ROWS, LANES)
        return _reduce_cores(l, grads, small, core)

    dmods = [None] * DEPTH
    dx, dmods[1], grads = _layer_bwd(1, dx, mods[1], params[1], saved[1], cosf, sinf)
    pending, token = _scatter_chips_start("l1_reduce_chips", chip_sums_of(1, grads))
    dx, dmods[0], grads = _layer_bwd(0, dx, mods[0], params[0], saved[0], cosf, sinf, after=token)
    totals = _reduce_finish(1, _scatter_chips_finish("l1_reduce_chips", pending, dx), core, None)
    totals = _reduce_finish(0, _scatter_chips("l0_reduce_chips", chip_sums_of(0, grads)), core, totals)

    dmod_mine = jnp.concatenate([jnp.concatenate(dmods[l], axis=1) for l in range(DEPTH)], axis=1)
    dmod_all = _gather_all("gather_dmod", jnp.broadcast_to(dmod_mine, (SUBLANES, DEPTH * 6 * d)))[:, 0, :]
    g_ada_b = _colsum("ada_b_grad", dmod_all).reshape(DEPTH, 6 * d)
    dmod_cols = jnp.stack([lax.dynamic_slice_in_dim(dmod_all, l * 6 * d + chip * ada_cols, ada_cols, axis=1)
                           for l in range(DEPTH)])
    g_ada_w = _ada_bwd("ada_w_grad", c_rows, _pad_to(dmod_cols, 1, ADA_ROWS))

    small_all = _gather_chips("gather_small", [totals[-1]])[0]
    small_g = _unpack(small_all.transpose(1, 0, 2, 3).reshape(DEPTH, -1), [(DEPTH, N_CHIPS * SMALL_ROWS * LANES)])[0]
    per_layer = [_unpack(small_g[l], [shp for _, shp in _SMALL]) for l in range(DEPTH)]
    sg = {k: jnp.stack([per_layer[l][i] for l in range(DEPTH)]) for i, (k, _) in enumerate(_SMALL)}
    g_final = sg["final_norm_w"][DEPTH - 1]
    g_conv_w = lax.dynamic_slice_in_dim(sg["conv_w"], chip * FFN_SHARD, FFN_SHARD, axis=2)

    big = dict(w_in=(tr(w_in), tr(m_w_in), tr(v_w_in)), proj_a=(proj_a, m_proj_a, v_proj_a), proj_b=(proj_b, m_proj_b, v_proj_b),
               w_out=(w_out, m_w_out, v_w_out), w_gate=(tr(ffn_w_gate), tr(m_ffn_w_gate), tr(v_ffn_w_gate)),
               w_up=(tr(ffn_w_up), tr(m_ffn_w_up), tr(v_ffn_w_up)), w_down=(ffn_w_down, m_ffn_w_down, v_ffn_w_down))
    transposed = ("w_in", "w_gate", "w_up")
    upd, g_big = {}, {}
    for (k, (w, m, v)), g in zip(big.items(), totals[:-1]):
        res = _adamw("adamw_" + k, w, g, m, v)
        res = [tr(a) for a in res] if k in transposed else res
        g_big[k], upd[k] = res[0], res[1:]
    g_big["ada_w"], *upd["ada_w"] = _adamw("adamw_ada_w", ada_w, g_ada_w, m_ada_w, v_ada_w)
    rest = [("ada_b", ada_b, g_ada_b, m_ada_b, v_ada_b), ("norm1_w", norm1_w, sg["norm1_w"], m_norm1_w, v_norm1_w),
            ("attn_sinks", attn_sinks, sg["sinks"], m_attn_sinks, v_attn_sinks),
            ("sgu_ln_w", sgu_ln_w, sg["sgu_ln_w"], m_sgu_ln_w, v_sgu_ln_w),
            ("sgu_ln_b", sgu_ln_b, sg["sgu_ln_b"], m_sgu_ln_b, v_sgu_ln_b), ("sgu_w", sgu_w, sg["sgu_w"], m_sgu_w, v_sgu_w),
            ("sgu_b", sgu_b, sg["sgu_b"], m_sgu_b, v_sgu_b), ("norm2_w", norm2_w, sg["norm2_w"], m_norm2_w, v_norm2_w),
            ("ffn_conv_w", ffn_conv_w, g_conv_w, m_ffn_conv_w, v_ffn_conv_w),
            ("ffn_conv_b", ffn_conv_b, sg["conv_b"], m_ffn_conv_b, v_ffn_conv_b),
            ("final_norm_w", final_norm_w.reshape(1, d), g_final.reshape(1, d), m_final_norm_w.reshape(1, d),
             v_final_norm_w.reshape(1, d))]
    rest_out = _adamw_small("adamw_rest", *[[r[i] for r in rest] for i in (1, 2, 3, 4)])
    g_rest = {r[0]: r[2] for r in rest}
    u_rest = {r[0]: tuple(o[i] for o in rest_out) for i, r in enumerate(rest)}
    g_rest["final_norm_w"] = g_final
    u_rest["final_norm_w"] = tuple(a.reshape(d) for a in u_rest["final_norm_w"])

    names = ("ada_w", "ada_b", "norm1_w", "w_in", "attn_sinks", "sgu_ln_w", "sgu_ln_b", "sgu_w", "sgu_b", "proj_a", "proj_b",
             "w_out", "norm2_w", "ffn_w_gate", "ffn_w_up", "ffn_conv_w", "ffn_conv_b", "ffn_w_down", "final_norm_w")
    alias = {"ffn_w_gate": "w_gate", "ffn_w_up": "w_up", "ffn_w_down": "w_down"}
    grad_of = lambda n: g_rest[n] if n in g_rest else g_big[alias.get(n, n)]
    upd_of = lambda n: u_rest[n] if n in u_rest else upd[alias.get(n, n)]
    return (loss, dx[None], *[grad_of(n) for n in names], *[upd_of(n)[0] for n in names],
            *[upd_of(n)[1] for n in names], *[upd_of(n)[2] for n in names])
```

```python
import jax
import jax.numpy as jnp
from jax import lax
from jax.experimental import pallas as pl
from jax.experimental.pallas import tpu as pltpu

F32 = jnp.float32
BF = jnp.bfloat16

D_MODEL = 1024
N_Q_HEADS = 16
N_KV_HEADS = 2
HEAD_DIM = 64
ATTN_BLOCK = 128
ROPE_THETA = 500000.0
ROT_DIM = HEAD_DIM // 4
SGU_WIDTH = 1024
SGU_GROUPS = 8
SGU_CHUNK = 128
FFN_DIM = 2816
NORM_EPS = 1e-6
DEPTH = 2
IN_COLS = 5376
N_CHIPS = 4
FFN_SHARD = FFN_DIM // N_CHIPS
FFN_SHARD_PAD = 768
FFN_PAD = N_CHIPS * FFN_SHARD_PAD
LANES = 128
SUBLANES = 8
VMEM_LIMIT = 56 * 1024 * 1024
NEG_BIG = -1e30

ADAM_LR = 0.001
ADAM_B1 = 0.9
ADAM_B2 = 0.999
ADAM_EPS = 1e-08
ADAM_WD = 0.01
ADAM_STEP = 10

MESH = pl.DeviceIdType.MESH

Q_END = 1024
KV_END = 1280
U_END = 2304
Z_END = 3328
GA_END = 4352


def _sds(shape, dtype):
    return jax.ShapeDtypeStruct(tuple(shape), dtype)


def _call(body, name, out_shape, grid, in_specs, out_specs, semantics, scratch=(), after=None):
    n_in = len(in_specs)
    fn = body
    if after is not None:
        def fn(*refs):
            return body(*refs[:n_in], *refs[n_in + 1:])

        in_specs = list(in_specs) + [pl.BlockSpec(memory_space=pl.ANY)]
    call = pl.pallas_call(
        fn, name=name, out_shape=out_shape, grid=grid, in_specs=in_specs, out_specs=out_specs,
        scratch_shapes=scratch,
        compiler_params=pltpu.CompilerParams(dimension_semantics=semantics, vmem_limit_bytes=VMEM_LIMIT))
    if after is None:
        return call
    return lambda *args: call(*args, after)


def _rows(tm, width, col=0):
    return pl.BlockSpec((tm, width), lambda i: (i, col))


def _vec(width):
    return pl.BlockSpec((1, width), lambda i: (0, 0))


def _resident(shape):
    zeros = (0,) * len(shape)
    return pl.BlockSpec(tuple(shape), lambda *_: zeros, pipeline_mode=pl.Buffered(1))


def _sigmoid(x):
    return 1.0 / (1.0 + jnp.exp(-x))


def _gelu(x):
    return 0.5 * x * (1.0 + lax.erf(x * 0.7071067811865476))


def _gelu_grad(x):
    cdf = 0.5 * (1.0 + lax.erf(x * 0.7071067811865476))
    return cdf + x * jnp.exp(-0.5 * x * x) * 0.3989422804014327


def _dot(a, b):
    return jnp.dot(a, b, preferred_element_type=F32)


def _dot_nt(a, b):
    return lax.dot_general(a, b, (((1,), (1,)), ((), ())), preferred_element_type=F32)


def _dot_tn(a, b):
    return lax.dot_general(a, b, (((0,), (0,)), ((), ())), preferred_element_type=F32)


def _rms(xv):
    return lax.rsqrt(jnp.mean(xv * xv, axis=-1, keepdims=True) + NORM_EPS)


def _matmul_tn(name, a, b, tk=512, tn=1024, blocked=False):
    s, k = a.shape
    n = b.shape[1]
    tk, tn = min(tk, k), min(tn, n)

    def body(a_ref, b_ref, o_ref):
        res = _dot_tn(a_ref[...], b_ref[...]).astype(o_ref.dtype)
        if blocked:
            o_ref[0] = res
        else:
            o_ref[...] = res

    if blocked:
        out, ospec = _sds((n // tn, k, tn), BF), pl.BlockSpec((1, tk, tn), lambda i, j: (j, i, 0))
    else:
        out, ospec = _sds((k, n), BF), pl.BlockSpec((tk, tn), lambda i, j: (i, j))
    return _call(body, name, out, (k // tk, n // tn),
                 [pl.BlockSpec((s, tk), lambda i, j: (0, i)), pl.BlockSpec((s, tn), lambda i, j: (0, j))],
                 ospec, ("parallel", "parallel"))(a, b)


def _matmul_tn_rows(name, a, b, out, row0, rows_total, tk=256):
    s, k = a.shape
    n = b.shape[1]

    def body(a_ref, b_ref, *rest):
        rest[-1][...] = _dot_tn(a_ref[...], b_ref[...]).astype(BF)

    in_specs = [pl.BlockSpec((s, tk), lambda i: (0, i)), _resident(b.shape)]
    args = [a, b]
    if out is not None:
        in_specs.append(pl.BlockSpec(memory_space=pl.ANY))
        args.append(out)
    return pl.pallas_call(
        body, name=name, out_shape=_sds((rows_total, n), BF), grid=(k // tk,), in_specs=in_specs,
        out_specs=pl.BlockSpec((tk, n), lambda i: (row0 // tk + i, 0)),
        input_output_aliases={2: 0} if out is not None else {},
        compiler_params=pltpu.CompilerParams(dimension_semantics=("parallel",), vmem_limit_bytes=VMEM_LIMIT))(*args)


def _rope_partner(v):
    lane = lax.broadcasted_iota(jnp.int32, (1, LANES), 1) % HEAD_DIM
    return jnp.where(lane < ROT_DIM // 2, pltpu.roll(v, LANES - ROT_DIM // 2, axis=1), pltpu.roll(v, ROT_DIM // 2, axis=1))


def _dup_half(v, half):
    lane = lax.broadcasted_iota(jnp.int32, (1, LANES), 1)
    keep = jnp.where((lane >= HEAD_DIM) == (half == 1), v, 0.0)
    return keep + pltpu.roll(keep, HEAD_DIM, axis=1)


def _in_proj(name, x, w, sc, sh, w_in, cosf, sinf, tm=256, after=None):
    s, d = x.shape
    tm = min(tm, s)

    def body(x_ref, w_ref, sc_ref, sh_ref, win_ref, cos_ref, sin_ref,
             h_ref, qr_ref, kk0_ref, kk1_ref, vv0_ref, vv1_ref, u_ref, v_ref, ga_ref, gb_ref):
        xv = x_ref[...]
        h = ((xv * _rms(xv)) * w_ref[...] * (1.0 + sc_ref[...]) + sh_ref[...]).astype(BF)
        h_ref[...] = h
        cosv, sinv = cos_ref[...], sin_ref[...]
        q = _dot_nt(h, win_ref[:Q_END, :])
        for j in range(D_MODEL // LANES):
            qv = q[:, j * LANES:(j + 1) * LANES]
            qr_ref[:, j * LANES:(j + 1) * LANES] = (qv * cosv + _rope_partner(qv) * sinv).astype(BF)
        kv = _dot_nt(h, win_ref[Q_END:KV_END, :])
        kr = kv[:, :LANES] * cosv + _rope_partner(kv[:, :LANES]) * sinv
        vv = kv[:, LANES:]
        kk0_ref[...] = _dup_half(kr, 0).astype(BF)
        kk1_ref[...] = _dup_half(kr, 1).astype(BF)
        vv0_ref[...] = _dup_half(vv, 0).astype(BF)
        vv1_ref[...] = _dup_half(vv, 1).astype(BF)
        u_ref[...] = _dot_nt(h, win_ref[KV_END:U_END, :])
        v_ref[...] = _dot_nt(h, win_ref[U_END:Z_END, :])
        ga_ref[...] = _dot_nt(h, win_ref[Z_END:GA_END, :])
        gb_ref[...] = _dot_nt(h, win_ref[GA_END:, :])

    wide, kvs, pre = _sds((s, d), BF), _sds((s, LANES), BF), _sds((s, d), F32)
    return _call(body, name, (wide, wide, kvs, kvs, kvs, kvs, pre, pre, pre, pre), (s // tm,),
                 [_rows(tm, d), _vec(d), _vec(d), _vec(d), _resident(w_in.shape), _rows(tm, LANES), _rows(tm, LANES)],
                 (_rows(tm, d), _rows(tm, d)) + (_rows(tm, LANES),) * 4 + (_rows(tm, d),) * 4, ("parallel",), after=after)(
                     x, w, sc, sh, w_in, cosf, sinf)


def _in_proj_bwd(name, dq, dkv, du, dv, dga, dgb, w_in, x, w, sc, dx_in, tm=256):
    s, d = x.shape
    tm = min(tm, s)

    def body(dq_ref, dkv_ref, du_ref, dv_ref, dga_ref, dgb_ref, win_ref, x_ref, w_ref, sc_ref, dxin_ref,
             dx_ref, da_ref, dsh_ref):
        @pl.when(pl.program_id(0) == 0)
        def _():
            da_ref[...] = jnp.zeros_like(da_ref)
            dsh_ref[...] = jnp.zeros_like(dsh_ref)

        dh = (_dot(dq_ref[...], win_ref[:Q_END, :]) + _dot(dkv_ref[...], win_ref[Q_END:KV_END, :])
              + _dot(du_ref[...], win_ref[KV_END:U_END, :]) + _dot(dv_ref[...], win_ref[U_END:Z_END, :])
              + _dot(dga_ref[...], win_ref[Z_END:GA_END, :]) + _dot(dgb_ref[...], win_ref[GA_END:, :]))
        xv = x_ref[...]
        r = _rms(xv)
        xn = xv * r
        dxn = dh * (w_ref[...] * (1.0 + sc_ref[...]))
        dx_ref[...] = dxin_ref[...] + r * (dxn - xn * jnp.mean(dxn * xn, axis=-1, keepdims=True))
        da_ref[...] += jnp.sum(dh * xn, axis=0, keepdims=True)
        dsh_ref[...] += jnp.sum(dh, axis=0, keepdims=True)

    return _call(body, name, (_sds((s, d), F32), _sds((1, d), F32), _sds((1, d), F32)), (s // tm,),
                 [_rows(tm, d), _rows(tm, 2 * LANES), _rows(tm, d), _rows(tm, d), _rows(tm, d), _rows(tm, d),
                  _resident(w_in.shape), _rows(tm, d), _vec(d), _vec(d), _rows(tm, d)],
                 (_rows(tm, d), _vec(d), _vec(d)), ("arbitrary",))(dq, dkv, du, dv, dga, dgb, w_in, x, w, sc, dx_in)


def _rope_bwd(name, dqr, dkv_cur, dkv_prev, cosf, sinf, tm=512):
    s = dqr.shape[0]
    tm = min(tm, s)
    steps = s // tm
    per = tm // ATTN_BLOCK
    nb = s // ATTN_BLOCK

    def unrope(v, cosv, sinv):
        return v * cosv - _rope_partner(v) * sinv

    def body(dq_ref, cur_ref, prev_ref, next_ref, cos_ref, sin_ref, dqo_ref, dkvo_ref):
        i = pl.program_id(0)
        cosv, sinv = cos_ref[...], sin_ref[...]
        for j in range(D_MODEL // LANES):
            dqo_ref[:, j * LANES:(j + 1) * LANES] = unrope(dq_ref[:, j * LANES:(j + 1) * LANES], cosv, sinv).astype(BF)
        nxt = jnp.where(i < steps - 1, next_ref[...], 0.0)
        if per > 1:
            shifted = jnp.concatenate([prev_ref[ATTN_BLOCK:, :], nxt], axis=0)
        else:
            shifted = nxt
        tot = cur_ref[...] + shifted
        dkvo_ref[:, :LANES] = unrope(tot[:, :LANES], cosv, sinv).astype(BF)
        dkvo_ref[:, LANES:] = tot[:, LANES:].astype(BF)

    nxt_spec = pl.BlockSpec((ATTN_BLOCK, 2 * LANES), lambda i: (jnp.minimum((i + 1) * per, nb - 1), 0))
    return _call(body, name, (_sds((s, D_MODEL), BF), _sds((s, 2 * LANES), BF)), (steps,),
                 [_rows(tm, D_MODEL), _rows(tm, 2 * LANES), _rows(tm, 2 * LANES), nxt_spec, _rows(tm, LANES),
                  _rows(tm, LANES)],
                 (_rows(tm, D_MODEL), _rows(tm, 2 * LANES)), ("parallel",))(dqr, dkv_cur, dkv_prev, dkv_prev, cosf, sinf)


def _band_mask(n):
    qi = lax.broadcasted_iota(jnp.int32, (ATTN_BLOCK, 2 * ATTN_BLOCK), 0)
    kj = lax.broadcasted_iota(jnp.int32, (ATTN_BLOCK, 2 * ATTN_BLOCK), 1)
    return (kj > qi) & (kj <= qi + ATTN_BLOCK) & ((n > 0) | (kj >= ATTN_BLOCK))


def _attn_probs(qa, kb, allowed, sink):
    sc = jnp.where(allowed, _dot_nt(qa, kb) * (HEAD_DIM ** -0.5), NEG_BIG)
    m = jnp.maximum(jnp.max(sc, axis=-1, keepdims=True), sink)
    p = jnp.exp(sc - m)
    esink = jnp.exp(sink - m)
    inv = 1.0 / (jnp.sum(p, axis=-1, keepdims=True) + esink)
    return p * inv, esink * inv


def _kv_specs():
    cur = pl.BlockSpec((ATTN_BLOCK, LANES), lambda n: (n, 0))
    prev = pl.BlockSpec((ATTN_BLOCK, LANES), lambda n: (jnp.maximum(n - 1, 0), 0))
    return [prev, cur] * 4


def _attention(name, qr, kk0, kk1, vv0, vv1, sinks):
    s = qr.shape[0]
    nb = s // ATTN_BLOCK

    def body(sink_ref, q_ref, k0p, k0c, k1p, k1c, v0p, v0c, v1p, v1c, y_ref):
        allowed = _band_mask(pl.program_id(0))
        upper = lax.broadcasted_iota(jnp.int32, (1, LANES), 1) >= HEAD_DIM
        bands = ((jnp.concatenate([k0p[...], k0c[...]], axis=0), jnp.concatenate([v0p[...], v0c[...]], axis=0)),
                 (jnp.concatenate([k1p[...], k1c[...]], axis=0), jnp.concatenate([v1p[...], v1c[...]], axis=0)))
        for hk in range(N_KV_HEADS):
            kb, vb = bands[hk]
            for j in range(4):
                col = (hk * 4 + j) * LANES
                qp = q_ref[:, col:col + LANES]
                out = jnp.zeros((ATTN_BLOCK, LANES), F32)
                for half in range(2):
                    sel = upper if half else jnp.logical_not(upper)
                    qa = jnp.where(sel, qp, jnp.zeros_like(qp))
                    pn, _ = _attn_probs(qa, kb, allowed, sink_ref[hk * 8 + j * 2 + half])
                    out = jnp.where(sel, _dot(pn.astype(BF), vb), out)
                y_ref[:, col:col + LANES] = out.astype(BF)

    return _call(body, name, _sds((s, D_MODEL), BF), (nb,),
                 [pl.BlockSpec(memory_space=pltpu.SMEM), pl.BlockSpec((ATTN_BLOCK, D_MODEL), lambda n: (n, 0))] + _kv_specs(),
                 pl.BlockSpec((ATTN_BLOCK, D_MODEL), lambda n: (n, 0)), ("parallel",))(
                     sinks, qr, kk0, kk0, kk1, kk1, vv0, vv0, vv1, vv1)


def _attention_bwd(name, qr, kk0, kk1, vv0, vv1, sinks, dy):
    s = qr.shape[0]
    nb = s // ATTN_BLOCK

    def body(sink_ref, q_ref, dy_ref, k0p, k0c, k1p, k1c, v0p, v0c, v1p, v1c, dq_ref, cur_ref, prev_ref, dsink_ref):
        @pl.when(pl.program_id(0) == 0)
        def _():
            dsink_ref[...] = jnp.zeros_like(dsink_ref)

        allowed = _band_mask(pl.program_id(0))
        lane = lax.broadcasted_iota(jnp.int32, (1, LANES), 1)
        upper = lane >= HEAD_DIM
        bands = ((jnp.concatenate([k0p[...], k0c[...]], axis=0), jnp.concatenate([v0p[...], v0c[...]], axis=0)),
                 (jnp.concatenate([k1p[...], k1c[...]], axis=0), jnp.concatenate([v1p[...], v1c[...]], axis=0)))
        dsink = jnp.zeros((1, LANES), F32)
        dk_slab = jnp.zeros((2 * ATTN_BLOCK, LANES), F32)
        dv_slab = jnp.zeros((2 * ATTN_BLOCK, LANES), F32)
        for hk in range(N_KV_HEADS):
            kb, vb = bands[hk]
            dkk = jnp.zeros((2 * ATTN_BLOCK, LANES), F32)
            dvv = jnp.zeros((2 * ATTN_BLOCK, LANES), F32)
            for j in range(4):
                col = (hk * 4 + j) * LANES
                qp = q_ref[:, col:col + LANES]
                dyp = dy_ref[:, col:col + LANES]
                dq_pair = jnp.zeros((ATTN_BLOCK, LANES), F32)
                for half in range(2):
                    h = hk * 8 + j * 2 + half
                    sel = upper if half else jnp.logical_not(upper)
                    qa = jnp.where(sel, qp, jnp.zeros_like(qp))
                    pn, psink = _attn_probs(qa, kb, allowed, sink_ref[h])
                    dya = jnp.where(sel, dyp, 0.0).astype(BF)
                    dp = _dot_nt(dya, vb)
                    delta = jnp.sum(pn * dp, axis=-1, keepdims=True)
                    ds = (pn * (dp - delta) * (HEAD_DIM ** -0.5)).astype(BF)
                    dsink = dsink + jnp.where(lane == h, -jnp.sum(psink * delta), 0.0)
                    dq_pair = jnp.where(sel, _dot(ds, kb), dq_pair)
                    dkk = dkk + _dot_tn(ds, qa)
                    dvv = dvv + _dot_tn(pn.astype(BF), dya)
                dq_ref[:, col:col + LANES] = dq_pair
            mine = upper if hk else jnp.logical_not(upper)
            dk_slab = jnp.where(mine, dkk + pltpu.roll(dkk, HEAD_DIM, axis=1), dk_slab)
            dv_slab = jnp.where(mine, dvv + pltpu.roll(dvv, HEAD_DIM, axis=1), dv_slab)
        prev_ref[:, :LANES] = dk_slab[:ATTN_BLOCK]
        prev_ref[:, LANES:] = dv_slab[:ATTN_BLOCK]
        cur_ref[:, :LANES] = dk_slab[ATTN_BLOCK:]
        cur_ref[:, LANES:] = dv_slab[ATTN_BLOCK:]
        dsink_ref[...] += dsink

    blk = pl.BlockSpec((ATTN_BLOCK, D_MODEL), lambda n: (n, 0))
    kvo = pl.BlockSpec((ATTN_BLOCK, 2 * LANES), lambda n: (n, 0))
    return _call(body, name,
                 (_sds((s, D_MODEL), F32), _sds((s, 2 * LANES), F32), _sds((s, 2 * LANES), F32), _sds((1, LANES), F32)),
                 (nb,), [pl.BlockSpec(memory_space=pltpu.SMEM), blk, blk] + _kv_specs(),
                 (blk, kvo, kvo, pl.BlockSpec((1, LANES), lambda n: (0, 0))), ("arbitrary",))(
                     sinks, qr, dy, kk0, kk0, kk1, kk1, vv0, vv0, vv1, vv1)


def _sgu_weights(wm_ref, g):
    t = lax.broadcasted_iota(jnp.int32, (SGU_CHUNK, SGU_CHUNK), 0)
    sidx = lax.broadcasted_iota(jnp.int32, (SGU_CHUNK, SGU_CHUNK), 1)
    return jnp.where(sidx <= t, wm_ref[g], 0.0).astype(BF)


def _layer_norm_stats(v):
    mu = jnp.mean(v, axis=-1, keepdims=True)
    cen = v - mu
    rstd = lax.rsqrt(jnp.mean(cen * cen, axis=-1, keepdims=True) + NORM_EPS)
    return cen * rstd, rstd


def _sgu(name, u_pre, v_pre, ln_w, ln_b, wm, bfull, tm=256):
    s, w = u_pre.shape
    tm = min(tm, s)

    def body(u_ref, v_ref, lw_ref, lb_ref, wm_ref, b_ref, y_ref):
        vhat, _ = _layer_norm_stats(_gelu(v_ref[...]))
        vn = (vhat * lw_ref[...] + lb_ref[...]).astype(BF)
        for g in range(SGU_GROUPS):
            wg = _sgu_weights(wm_ref, g)
            cols = slice(g * SGU_CHUNK, (g + 1) * SGU_CHUNK)
            for ch in range(tm // SGU_CHUNK):
                rows = slice(ch * SGU_CHUNK, (ch + 1) * SGU_CHUNK)
                f = _dot(wg, vn[rows, cols]) + b_ref[g]
                y_ref[rows, cols] = (_gelu(u_ref[rows, cols]) * f).astype(BF)

    full3 = pl.BlockSpec((SGU_GROUPS, SGU_CHUNK, SGU_CHUNK), lambda i: (0, 0, 0))
    return _call(body, name, _sds((s, w), BF), (s // tm,),
                 [_rows(tm, w), _rows(tm, w), _vec(w), _vec(w), full3, full3],
                 _rows(tm, w), ("parallel",))(u_pre, v_pre, ln_w, ln_b, wm, bfull)


def _sgu_bwd(name, u_pre, v_pre, ln_w, ln_b, wm, bfull, dy, tm=256, after=None):
    s, w = u_pre.shape
    tm = min(tm, s)
    steps = s // tm

    def body(u_ref, v_ref, lw_ref, lb_ref, wm_ref, b_ref, dy_ref, du_ref, dv_ref, dwm_ref, db_ref, dlw_ref, dlb_ref,
             dfsum_ref):
        i = pl.program_id(0)

        @pl.when(i == 0)
        def _():
            dwm_ref[...] = jnp.zeros_like(dwm_ref)
            dlw_ref[...] = jnp.zeros_like(dlw_ref)
            dlb_ref[...] = jnp.zeros_like(dlb_ref)
            dfsum_ref[...] = jnp.zeros_like(dfsum_ref)

        vpre = v_ref[...]
        vhat, rstd = _layer_norm_stats(_gelu(vpre))
        vn = (vhat * lw_ref[...] + lb_ref[...]).astype(BF)
        t = lax.broadcasted_iota(jnp.int32, (SGU_CHUNK, SGU_CHUNK), 0)
        sidx = lax.broadcasted_iota(jnp.int32, (SGU_CHUNK, SGU_CHUNK), 1)
        dvn_cols = []
        for g in range(SGU_GROUPS):
            wg = _sgu_weights(wm_ref, g)
            cols = slice(g * SGU_CHUNK, (g + 1) * SGU_CHUNK)
            dvn_rows = []
            dwg = jnp.zeros((SGU_CHUNK, SGU_CHUNK), F32)
            dfs = jnp.zeros((SGU_CHUNK, SGU_CHUNK), F32)
            for ch in range(tm // SGU_CHUNK):
                rows = slice(ch * SGU_CHUNK, (ch + 1) * SGU_CHUNK)
                upre = u_ref[rows, cols]
                dyv = dy_ref[rows, cols]
                f = _dot(wg, vn[rows, cols]) + b_ref[g]
                du_ref[rows, cols] = (dyv * f * _gelu_grad(upre)).astype(BF)
                df = dyv * _gelu(upre)
                dfb = df.astype(BF)
                dvn_rows.append(_dot_tn(wg, dfb))
                dwg = dwg + _dot_nt(dfb, vn[rows, cols])
                dfs = dfs + df
            dwm_ref[g] += jnp.where(sidx <= t, dwg, 0.0)
            dfsum_ref[g] += dfs
            dvn_cols.append(jnp.concatenate(dvn_rows, axis=0) if len(dvn_rows) > 1 else dvn_rows[0])
        dvn = jnp.concatenate(dvn_cols, axis=1)
        dlw_ref[...] += jnp.sum(dvn * vhat, axis=0, keepdims=True)
        dlb_ref[...] += jnp.sum(dvn, axis=0, keepdims=True)
        dvh = dvn * lw_ref[...]
        dvg = rstd * (dvh - jnp.mean(dvh, axis=-1, keepdims=True) - vhat * jnp.mean(dvh * vhat, axis=-1, keepdims=True))
        dv_ref[...] = (dvg * _gelu_grad(vpre)).astype(BF)

        @pl.when(i == steps - 1)
        def _():
            for g in range(SGU_GROUPS):
                db_ref[g:g + 1, :] = jnp.sum(dfsum_ref[g].T, axis=0, keepdims=True)

    full3 = pl.BlockSpec((SGU_GROUPS, SGU_CHUNK, SGU_CHUNK), lambda i: (0, 0, 0))
    return _call(body, name,
                 (_sds((s, w), BF), _sds((s, w), BF), _sds((SGU_GROUPS, SGU_CHUNK, SGU_CHUNK), F32),
                  _sds((SGU_GROUPS, SGU_CHUNK), F32), _sds((1, w), F32), _sds((1, w), F32)),
                 (steps,),
                 [_rows(tm, w), _rows(tm, w), _vec(w), _vec(w), full3, full3, _rows(tm, w)],
                 (_rows(tm, w), _rows(tm, w), full3, pl.BlockSpec((SGU_GROUPS, SGU_CHUNK), lambda i: (0, 0)), _vec(w), _vec(w)),
                 ("arbitrary",), scratch=[pltpu.VMEM((SGU_GROUPS, SGU_CHUNK, SGU_CHUNK), F32)], after=after)(
                     u_pre, v_pre, ln_w, ln_b, wm, bfull, dy)


def _mix_out(name, y_sgu, y_attn, ga_pre, gb_pre, x, g1, proj_a, proj_b, w_out, w2, sc2, sh2, tm=256):
    s, d = x.shape
    tm = min(tm, s)

    def body(ys_ref, ya_ref, ga_ref, gb_ref, x_ref, g1_ref, wa_ref, wb_ref, wo_ref, w2_ref, sc2_ref, sh2_ref,
             m_ref, pa_ref, pb_ref, o_ref, x1_ref, h2_ref):
        pa = _dot(ys_ref[...], wa_ref[...].reshape(d, d))
        pb = _dot(ya_ref[...], wb_ref[...].reshape(d, d))
        pa_ref[...] = pa
        pb_ref[...] = pb
        merged = (_sigmoid(ga_ref[...]) * pa + _sigmoid(gb_ref[...]) * pb).astype(BF)
        m_ref[...] = merged
        o = _dot(merged, wo_ref[...].reshape(d, d))
        o_ref[...] = o
        x1 = x_ref[...] + g1_ref[...] * o
        x1_ref[...] = x1
        h2_ref[...] = ((x1 * _rms(x1)) * w2_ref[...] * (1.0 + sc2_ref[...]) + sh2_ref[...]).astype(BF)

    f, b = _sds((s, d), F32), _sds((s, d), BF)
    r = _rows(tm, d)
    wspec = _resident(proj_a.shape)
    return _call(body, name, (b, f, f, f, f, b), (s // tm,),
                 [r, r, r, r, r, _vec(d), wspec, wspec, wspec, _vec(d), _vec(d), _vec(d)], (r,) * 6, ("parallel",))(
                     y_sgu, y_attn, ga_pre, gb_pre, x, g1, proj_a, proj_b, w_out, w2, sc2, sh2)


def _mix_bwd(name, do, w_out, proj_a, proj_b, ga_pre, gb_pre, pa, pb, tm=256):
    s, d = do.shape
    tm = min(tm, s)

    def body(do_ref, wo_ref, wa_ref, wb_ref, ga_ref, gb_ref, pa_ref, pb_ref,
             dpa_ref, dpb_ref, dga_ref, dgb_ref, dys_ref, dya_ref):
        dm = _dot_nt(do_ref[...], wo_ref[...].reshape(d, d))
        ga = _sigmoid(ga_ref[...])
        gb = _sigmoid(gb_ref[...])
        dpa = (dm * ga).astype(BF)
        dpb = (dm * gb).astype(BF)
        dpa_ref[...] = dpa
        dpb_ref[...] = dpb
        dga_ref[...] = (dm * pa_ref[...] * ga * (1.0 - ga)).astype(BF)
        dgb_ref[...] = (dm * pb_ref[...] * gb * (1.0 - gb)).astype(BF)
        dys_ref[...] = _dot_nt(dpa, wa_ref[...].reshape(d, d))
        dya_ref[...] = _dot_nt(dpb, wb_ref[...].reshape(d, d))

    f, b = _sds((s, d), F32), _sds((s, d), BF)
    r = _rows(tm, d)
    wspec = _resident(w_out.shape)
    return _call(body, name, (b, b, b, b, f, f), (s // tm,), [r, wspec, wspec, wspec, r, r, r, r], (r,) * 6,
                 ("parallel",))(do, w_out, proj_a, proj_b, ga_pre, gb_pre, pa, pb)


def _ffn_up(name, h2, w_gate, w_up, tm=1024):
    s, d = h2.shape
    tm = min(tm, s)
    tc = FFN_SHARD_PAD

    def body(h_ref, wg_ref, wu_ref, a_ref, up_ref):
        hv = h_ref[...]
        a_ref[...] = _dot_nt(hv, wg_ref[0])
        up_ref[...] = _dot_nt(hv, wu_ref[0])

    wspec = pl.BlockSpec((1, tc, d), lambda j, i: (j, 0, 0))
    ospec = pl.BlockSpec((tm, tc), lambda j, i: (i, j))
    o = _sds((s, FFN_PAD), F32)
    return _call(body, name, (o, o), (N_CHIPS, s // tm), [pl.BlockSpec((tm, d), lambda j, i: (i, 0)), wspec, wspec],
                 (ospec, ospec), ("parallel", "parallel"))(h2, w_gate, w_up)


def _conv_act(name, a, up, cw, cb, tm=512, tc=FFN_SHARD_PAD):
    s, c = a.shape
    tm = min(tm, s)
    per = tm // SUBLANES

    def body(a_ref, prev_ref, up_ref, cw_ref, cb_ref, h_ref):
        prev = jnp.where(pl.program_id(1) > 0, prev_ref[...], 0.0)
        ext = jnp.concatenate([prev, a_ref[...]], axis=0)
        ac = (cb_ref[...] + cw_ref[0:1, :] * pltpu.roll(ext, 2, axis=0) + cw_ref[1:2, :] * pltpu.roll(ext, 1, axis=0)
              + cw_ref[2:3, :] * ext)[SUBLANES:]
        h_ref[...] = (ac * _sigmoid(ac) * up_ref[...]).astype(BF)

    tile = pl.BlockSpec((tm, tc), lambda j, i: (i, j))
    prev = pl.BlockSpec((SUBLANES, tc), lambda j, i: (jnp.maximum(i * per - 1, 0), j))
    return _call(body, name, _sds((s, c), BF), (c // tc, s // tm),
                 [tile, prev, tile, pl.BlockSpec((3, tc), lambda j, i: (0, j)), pl.BlockSpec((1, tc), lambda j, i: (0, j))],
                 tile, ("parallel", "parallel"))(a, a, up, cw, cb)


def _conv_act_bwd(name, a, up, dhf, cw, cb, tm=512, tc=FFN_SHARD_PAD):
    s, c = a.shape
    tm = min(tm, s)
    per = tm // SUBLANES
    steps = s // tm
    last8 = s // SUBLANES - 1

    def body(a_ref, aprev_ref, anext_ref, up_ref, upnext_ref, dh_ref, dhnext_ref, cw_ref, cb_ref,
             da_ref, dup_ref, dcw_ref, dcb_ref):
        i = pl.program_id(1)

        @pl.when(i == 0)
        def _():
            dcw_ref[...] = jnp.zeros_like(dcw_ref)
            dcb_ref[...] = jnp.zeros_like(dcb_ref)

        prev = jnp.where(i > 0, aprev_ref[...], 0.0)
        ext = jnp.concatenate([prev, a_ref[...], anext_ref[...]], axis=0)
        a1 = pltpu.roll(ext, 1, axis=0)[SUBLANES:]
        a2 = pltpu.roll(ext, 2, axis=0)[SUBLANES:]
        a0 = ext[SUBLANES:]
        ac = cb_ref[...] + cw_ref[0:1, :] * a2 + cw_ref[1:2, :] * a1 + cw_ref[2:3, :] * a0
        sig = _sigmoid(ac)
        dh = jnp.concatenate([dh_ref[...], jnp.where(i < steps - 1, dhnext_ref[...], 0.0)], axis=0)
        upe = jnp.concatenate([up_ref[...], upnext_ref[...]], axis=0)
        dac = dh * upe * (sig * (1.0 + ac * (1.0 - sig)))
        dup_ref[...] = (dh[:tm] * (ac * sig)[:tm]).astype(BF)
        n = tm + SUBLANES
        da = (cw_ref[2:3, :] * dac + cw_ref[1:2, :] * pltpu.roll(dac, n - 1, axis=0)
              + cw_ref[0:1, :] * pltpu.roll(dac, n - 2, axis=0))
        da_ref[...] = da[:tm].astype(BF)
        dact = dac[:tm]
        dcb_ref[...] += jnp.sum(dact, axis=0, keepdims=True)
        dcw_ref[0:1, :] += jnp.sum(dact * a2[:tm], axis=0, keepdims=True)
        dcw_ref[1:2, :] += jnp.sum(dact * a1[:tm], axis=0, keepdims=True)
        dcw_ref[2:3, :] += jnp.sum(dact * a0[:tm], axis=0, keepdims=True)

    tile = pl.BlockSpec((tm, tc), lambda j, i: (i, j))
    prev = pl.BlockSpec((SUBLANES, tc), lambda j, i: (jnp.maximum(i * per - 1, 0), j))
    nxt = pl.BlockSpec((SUBLANES, tc), lambda j, i: (jnp.minimum((i + 1) * per, last8), j))
    cw_spec = pl.BlockSpec((3, tc), lambda j, i: (0, j))
    cb_spec = pl.BlockSpec((1, tc), lambda j, i: (0, j))
    return _call(body, name, (_sds((s, c), BF), _sds((s, c), BF), _sds((3, c), F32), _sds((1, c), F32)),
                 (c // tc, steps), [tile, prev, nxt, tile, nxt, tile, nxt, cw_spec, cb_spec],
                 (tile, tile, cw_spec, cb_spec), ("parallel", "arbitrary"))(a, a, a, up, up, dhf, dhf, cw, cb)


def _ffn_down(name, hf, w_down, x1, g2, tm=512):
    s, d = x1.shape
    tm = min(tm, s)

    def body(hf_ref, wd_ref, x1_ref, g2_ref, dn_ref, x2_ref):
        dn = _dot(hf_ref[...], wd_ref[...].reshape(FFN_PAD, d))
        dn_ref[...] = dn
        x2_ref[...] = x1_ref[...] + g2_ref[...] * dn

    o = _sds((s, d), F32)
    return _call(body, name, (o, o), (s // tm,),
                 [_rows(tm, FFN_PAD), _resident(w_down.shape), _rows(tm, d), _vec(d)],
                 (_rows(tm, d), _rows(tm, d)), ("parallel",))(hf, w_down, x1, g2)


def _ffn_down_bwd(name, dx2, dn, g2, w_down, tm=512, after=None):
    s, d = dx2.shape
    tm = min(tm, s)
    tc = FFN_SHARD_PAD

    def body(dx_ref, dn_ref, g2_ref, wd_ref, ddn_ref, dhf_ref, dg_ref):
        @pl.when(pl.program_id(0) == 0)
        def _():
            dg_ref[...] = jnp.zeros_like(dg_ref)

        dxv = dx_ref[...]
        ddn = (dxv * g2_ref[...]).astype(BF)
        ddn_ref[...] = ddn
        dg_ref[...] += jnp.sum(dxv * dn_ref[...], axis=0, keepdims=True)
        for k in range(N_CHIPS):
            dhf_ref[:, k * tc:(k + 1) * tc] = _dot_nt(ddn, wd_ref[k])

    return _call(body, name, (_sds((s, d), BF), _sds((s, FFN_PAD), F32), _sds((1, d), F32)), (s // tm,),
                 [_rows(tm, d), _rows(tm, d), _vec(d), _resident(w_down.shape)],
                 (_rows(tm, d), _rows(tm, FFN_PAD), _vec(d)), ("arbitrary",), after=after)(dx2, dn, g2, w_down)


def _ffn_up_bwd(name, da, dup, w_gate, w_up, x1, dx2, w2, sc2, o, g1, tm=256, after=None):
    s, d = x1.shape
    tm = min(tm, s)
    tc = FFN_SHARD_PAD

    def body(da_ref, dup_ref, wg_ref, wu_ref, x1_ref, dx2_ref, w2_ref, sc2_ref, o_ref, g1_ref,
             dx1_ref, do_ref, dnw_ref, dsh_ref, dg1_ref):
        @pl.when(pl.program_id(0) == 0)
        def _():
            dnw_ref[...] = jnp.zeros_like(dnw_ref)
            dsh_ref[...] = jnp.zeros_like(dsh_ref)
            dg1_ref[...] = jnp.zeros_like(dg1_ref)

        dh = jnp.zeros((tm, d), F32)
        for k in range(N_CHIPS):
            cols = slice(k * tc, (k + 1) * tc)
            dh = dh + _dot(da_ref[:, cols], wg_ref[k]) + _dot(dup_ref[:, cols], wu_ref[k])
        xv = x1_ref[...]
        r = _rms(xv)
        xn = xv * r
        dxn = dh * (w2_ref[...] * (1.0 + sc2_ref[...]))
        dx1 = dx2_ref[...] + r * (dxn - xn * jnp.mean(dxn * xn, axis=-1, keepdims=True))
        dx1_ref[...] = dx1
        dnw_ref[...] += jnp.sum(dh * xn, axis=0, keepdims=True)
        dsh_ref[...] += jnp.sum(dh, axis=0, keepdims=True)
        do_ref[...] = (dx1 * g1_ref[...]).astype(BF)
        dg1_ref[...] += jnp.sum(dx1 * o_ref[...], axis=0, keepdims=True)

    v = _sds((1, d), F32)
    r = _rows(tm, d)
    wspec = _resident(w_gate.shape)
    return _call(body, name, (_sds((s, d), F32), _sds((s, d), BF), v, v, v), (s // tm,),
                 [_rows(tm, FFN_PAD), _rows(tm, FFN_PAD), wspec, wspec, r, r, _vec(d), _vec(d), r, _vec(d)],
                 (r, r, _vec(d), _vec(d), _vec(d)), ("arbitrary",), after=after)(da, dup, w_gate, w_up, x1, dx2, w2, sc2, o, g1)


def _loss_head(name, x, w, target, tm=512):
    s, d = x.shape
    tm = min(tm, s)

    def body(x_ref, w_ref, t_ref, dx_ref, loss_ref, dw_ref):
        @pl.when(pl.program_id(0) == 0)
        def _():
            loss_ref[...] = jnp.zeros_like(loss_ref)
            dw_ref[...] = jnp.zeros_like(dw_ref)

        xv = x_ref[...]
        r = _rms(xv)
        xn = xv * r
        err = xn * w_ref[...] - t_ref[...]
        loss_ref[...] += 0.5 * jnp.sum(jnp.mean(err * err, axis=-1, keepdims=True))
        dy = err * (1.0 / d)
        dw_ref[...] += jnp.sum(dy * xn, axis=0, keepdims=True)
        dxn = dy * w_ref[...]
        dx_ref[...] = r * (dxn - xn * jnp.mean(dxn * xn, axis=-1, keepdims=True))

    return _call(body, name, (_sds((s, d), F32), _sds((1, LANES), F32), _sds((1, d), F32)), (s // tm,),
                 [_rows(tm, d), _vec(d), _rows(tm, d)], (_rows(tm, d), _vec(LANES), _vec(d)), ("arbitrary",))(x, w, target)


def _layer_fwd(l, x, mod, p, cosf, sinf, after=None, late=None):
    sh1, sc1, g1, sh2, sc2, g2 = mod
    tag = f"l{l}_"
    h, qr, kk0, kk1, vv0, vv1, u_pre, v_pre, ga_pre, gb_pre = _in_proj(
        tag + "in_proj", x, p["norm1_w"], sc1, sh1, p["w_in"], cosf, sinf, after=after)
    y_attn = _attention(tag + "attn", qr, kk0, kk1, vv0, vv1, p["sinks"])
    y_sgu = _sgu(tag + "sgu", u_pre, v_pre, p["sgu_ln_w"], p["sgu_ln_b"], p["sgu_w"], p["sgu_bfull"])
    if late is not None:
        p = dict(p, **late(y_sgu))
    merged, pa, pb, o, x1, h2 = _mix_out(tag + "mix_out", y_sgu, y_attn, ga_pre, gb_pre, x, g1, p["proj_a"], p["proj_b"],
                                         p["w_out"], p["norm2_w"], sc2, sh2)
    a, up = _ffn_up(tag + "ffn_up", h2, p["w_gate"], p["w_up"])
    hf = _conv_act(tag + "conv_act", a, up, p["conv_w"], p["conv_b"])
    dn, x2 = _ffn_down(tag + "ffn_down", hf, p["w_down"], x1, g2)
    saved = dict(x=x, h=h, qr=qr, kk0=kk0, kk1=kk1, vv0=vv0, vv1=vv1, u_pre=u_pre, v_pre=v_pre, ga_pre=ga_pre,
                 gb_pre=gb_pre, y_attn=y_attn, y_sgu=y_sgu, merged=merged, pa=pa, pb=pb, o=o, x1=x1, h2=h2, a=a, up=up,
                 hf=hf, dn=dn)
    return x2, saved, p


def _layer_bwd(l, dx2, mod, p, sv, cosf, sinf, after=None, emit=None):
    sh1, sc1, g1, sh2, sc2, g2 = mod
    tag = f"l{l}_b_"
    d = D_MODEL
    g = {}
    ready = (lambda names: emit({k: g.pop(k) for k in names})) if emit else (lambda names: None)
    ddn, dhf, dg2 = _ffn_down_bwd(tag + "ffn_down", dx2, sv["dn"], g2, p["w_down"], after=after)
    g["w_down"] = _matmul_tn(tag + "dw_down", sv["hf"], ddn, tk=FFN_SHARD_PAD).reshape(N_CHIPS, FFN_SHARD_PAD, d)
    da, dup, g["conv_w"], g["conv_b"] = _conv_act_bwd(tag + "conv_act", sv["a"], sv["up"], dhf, p["conv_w"], p["conv_b"])
    g["w_gate"] = _matmul_tn(tag + "dw_gate", da, sv["h2"], tk=FFN_SHARD_PAD).reshape(N_CHIPS, FFN_SHARD_PAD, d)
    g["w_up"] = _matmul_tn(tag + "dw_up", dup, sv["h2"], tk=FFN_SHARD_PAD).reshape(N_CHIPS, FFN_SHARD_PAD, d)
    dx1, do, da2, dsh2, dg1 = _ffn_up_bwd(tag + "ffn_up", da, dup, p["w_gate"], p["w_up"], sv["x1"], dx2, p["norm2_w"],
                                          sc2, sv["o"], g1, after=ready(("w_down", "w_gate", "w_up")))
    g["norm2_w"] = da2 * (1.0 + sc2)
    dsc2 = da2 * p["norm2_w"]
    g["w_out"] = _matmul_tn(tag + "dw_out", sv["merged"], do).reshape(N_CHIPS, d // N_CHIPS, d)
    dpa, dpb, dga, dgb, dy_sgu, dy_attn = _mix_bwd(tag + "mix", do, p["w_out"], p["proj_a"], p["proj_b"], sv["ga_pre"],
                                                  sv["gb_pre"], sv["pa"], sv["pb"])
    g["proj_a"] = _matmul_tn(tag + "dproj_a", sv["y_sgu"], dpa).reshape(N_CHIPS, d // N_CHIPS, d)
    g["proj_b"] = _matmul_tn(tag + "dproj_b", sv["y_attn"], dpb).reshape(N_CHIPS, d // N_CHIPS, d)
    du, dv, g["sgu_w"], g["sgu_b"], g["sgu_ln_w"], g["sgu_ln_b"] = _sgu_bwd(
        tag + "sgu", sv["u_pre"], sv["v_pre"], p["sgu_ln_w"], p["sgu_ln_b"], p["sgu_w"], p["sgu_bfull"], dy_sgu,
        after=ready(("w_out", "proj_a", "proj_b")))
    dqr, dkv_cur, dkv_prev, dsink = _attention_bwd(tag + "attn", sv["qr"], sv["kk0"], sv["kk1"], sv["vv0"], sv["vv1"],
                                                   p["sinks"], dy_attn)
    g["sinks"] = dsink[0, :N_Q_HEADS]
    dq, dkv = _rope_bwd(tag + "rope", dqr, dkv_cur, dkv_prev, cosf, sinf)
    dw_in, row0 = None, 0
    for n, t in (("q", dq), ("kv", dkv), ("u", du), ("v", dv), ("ga", dga), ("gb", dgb)):
        dw_in = _matmul_tn_rows(tag + "dw_in_" + n, t, sv["h"], dw_in, row0, IN_COLS)
        row0 += t.shape[1]
    g["w_in"] = dw_in.reshape(N_CHIPS, IN_COLS // N_CHIPS, d)
    dx, da1, dsh1 = _in_proj_bwd(tag + "in_proj", dq, dkv, du, dv, dga, dgb, p["w_in"], sv["x"], p["norm1_w"], sc1, dx1)
    g["norm1_w"] = da1 * (1.0 + sc1)
    dsc1 = da1 * p["norm1_w"]
    return dx, (dsh1, dsc1, dg1, dsh2, dsc2, dg2), g


def _pad_to(a, axis, size):
    pad = [(0, 0)] * a.ndim
    pad[axis] = (0, size - a.shape[axis])
    return jnp.pad(a, pad)


def _early_params(w_in, small):
    d = D_MODEL
    return dict(
        w_in=w_in.reshape(IN_COLS, d), norm1_w=small["norm1_w"].reshape(1, d), sinks=small["sinks"],
        sgu_ln_w=small["sgu_ln_w"].reshape(1, d), sgu_ln_b=small["sgu_ln_b"].reshape(1, d), sgu_w=small["sgu_w"],
        sgu_bfull=jnp.broadcast_to(small["sgu_b"][:, :, None], (SGU_GROUPS, SGU_CHUNK, SGU_CHUNK)))


def _late_params(proj_a, proj_b, w_out, w_gate, w_up, w_down, conv_w, small):
    d = D_MODEL
    return dict(
        proj_a=proj_a, proj_b=proj_b, w_out=w_out, w_gate=w_gate, w_up=w_up, w_down=w_down,
        conv_w=conv_w.transpose(1, 0, 2).reshape(3, FFN_PAD),
        conv_b=_pad_to(small["conv_b"].reshape(N_CHIPS, FFN_SHARD), 1, FFN_SHARD_PAD).reshape(1, FFN_PAD),
        norm2_w=small["norm2_w"].reshape(1, d))


def _layer_params(w_in, proj_a, proj_b, w_out, w_gate, w_up, w_down, conv_w, small):
    return dict(_early_params(w_in, small), **_late_params(proj_a, proj_b, w_out, w_gate, w_up, w_down, conv_w, small))


def _conv_grads_natural(g):
    cw = g["conv_w"].reshape(3, N_CHIPS, FFN_SHARD_PAD)[:, :, :FFN_SHARD].reshape(3, FFN_DIM)
    cb = g["conv_b"].reshape(N_CHIPS, FFN_SHARD_PAD)[:, :FFN_SHARD].reshape(FFN_DIM)
    return cw, cb


def _rope_tables(positions):
    inv_freq = ROPE_THETA ** (-jnp.arange(0, ROT_DIM, 2, dtype=F32) / ROT_DIM)
    ang = positions.astype(F32)[:, None] * inv_freq
    cos, sin = jnp.cos(ang), jnp.sin(ang)
    s = positions.shape[0]
    rest = HEAD_DIM - ROT_DIM
    cos_head = jnp.concatenate([cos, cos, jnp.ones((s, rest), F32)], axis=1)
    sin_head = jnp.concatenate([-sin, sin, jnp.zeros((s, rest), F32)], axis=1)
    return jnp.tile(cos_head, (1, LANES // HEAD_DIM)), jnp.tile(sin_head, (1, LANES // HEAD_DIM))


ADA_ROWS = 16


def _ada_fwd(name, c_rows, ada_w, ada_b_cols, tn=512):
    depth, d, n = ada_w.shape

    def body(c_ref, w_ref, b_ref, o_ref):
        cv = c_ref[...]
        act = (cv * _sigmoid(cv)).astype(BF)
        o_ref[0] = _dot(act, w_ref[0].astype(BF)) + b_ref[0]

    return _call(body, name, _sds((depth, ADA_ROWS, n), F32), (depth, n // tn),
                 [pl.BlockSpec((ADA_ROWS, d), lambda l, j: (0, 0)), pl.BlockSpec((1, d, tn), lambda l, j: (l, 0, j)),
                  pl.BlockSpec((1, 1, tn), lambda l, j: (l, 0, j))],
                 pl.BlockSpec((1, ADA_ROWS, tn), lambda l, j: (l, 0, j)), ("parallel", "parallel"))(c_rows, ada_w, ada_b_cols)


def _ada_bwd(name, c_rows, dmod_cols, tn=512):
    depth, _, n = dmod_cols.shape
    d = c_rows.shape[1]

    def body(c_ref, dm_ref, o_ref):
        cv = c_ref[...]
        act = (cv * _sigmoid(cv)).astype(BF)
        o_ref[0] = _dot_tn(act, dm_ref[0].astype(BF))

    return _call(body, name, _sds((depth, d, n), F32), (depth, n // tn),
                 [pl.BlockSpec((ADA_ROWS, d), lambda l, j: (0, 0)), pl.BlockSpec((1, ADA_ROWS, tn), lambda l, j: (l, 0, j))],
                 pl.BlockSpec((1, d, tn), lambda l, j: (l, 0, j)), ("parallel", "parallel"))(c_rows, dmod_cols)


def _colsum(name, a):
    r, n = a.shape

    def body(a_ref, o_ref):
        o_ref[...] = jnp.sum(a_ref[...], axis=0, keepdims=True)

    return _call(body, name, _sds((1, n), F32), (1,), [pl.BlockSpec((r, n), lambda i: (0, 0))],
                 pl.BlockSpec((1, n), lambda i: (0, 0)), ("arbitrary",))(a)


REL_SIBLING = (0, 0, 1)
REL_CHIPS = ((1, 0, 0), (0, 1, 0), (1, 1, 0))
REL_ALL = tuple((fx, fy, fc) for fx in (0, 1) for fy in (0, 1) for fc in (0, 1) if fx or fy or fc)


def _chip_of(dev):
    return 2 * dev[0] + dev[1]


def _dev_of(dev):
    return 4 * dev[0] + 2 * dev[1] + dev[2]


def _flip(dev, rel):
    return tuple(1 - m if f else m for m, f in zip(dev, rel))


def _exchange(name, arrays, n_out, stages, aliases=None):
    out_shapes, stages = stages[0], stages[1:]
    n_in = len(arrays)
    aliases = aliases or {}
    n_remote = sum(len(plan) for plan, _ in stages)
    n_local = sum(len(local) for _, local in stages)

    def at(ref, idx):
        return ref.at[idx] if len(idx) else ref

    def body(*refs):
        bufs = list(refs[:n_in + n_out])
        for i_in, i_out in aliases.items():
            bufs[i_in] = bufs[n_in + i_out]
        send_sems, recv_sems, local_sems = refs[n_in + n_out:]
        me = (lax.axis_index("x"), lax.axis_index("y"), lax.axis_index("c"))
        base_r = base_l = 0
        pending = []
        for plan, local in stages:
            def remote(k, entry, sender, receiver):
                rel, si, ssel, di, dsel = entry
                return pltpu.make_async_remote_copy(
                    src_ref=at(bufs[si], ssel(sender, receiver)), dst_ref=at(bufs[di], dsel(sender, receiver)),
                    send_sem=send_sems.at[k], recv_sem=recv_sems.at[k], device_id=_flip(me, rel), device_id_type=MESH)

            sends = [remote(base_r + k, e, me, _flip(me, e[0])) for k, e in enumerate(plan)]
            for cp in sends:
                cp.start()
            for k, (si, ssel, di, dsel) in enumerate(local):
                cp = pltpu.make_async_copy(at(bufs[si], ssel(me)), at(bufs[di], dsel(me)), local_sems.at[base_l + k])
                cp.start()
                pending.append(cp.wait)
            for k, e in enumerate(plan):
                remote(base_r + k, e, _flip(me, e[0]), me).wait_recv()
            pending += [cp.wait_send for cp in sends]
            base_r += len(plan)
            base_l += len(local)
        for wait in pending:
            wait()

    any_spec = pl.BlockSpec(memory_space=pl.ANY)
    return pl.pallas_call(
        body, name=name, out_shape=tuple(out_shapes), in_specs=[any_spec] * n_in, out_specs=tuple([any_spec] * n_out),
        input_output_aliases=dict(aliases),
        scratch_shapes=[pltpu.SemaphoreType.DMA((max(n_remote, 1),)), pltpu.SemaphoreType.DMA((max(n_remote, 1),)),
                        pltpu.SemaphoreType.DMA((max(n_local, 1),))])(*arrays)


HBM_SPEC = pl.BlockSpec(memory_space=pltpu.HBM)
SEM_SPEC = pl.BlockSpec(memory_space=pltpu.SEMAPHORE)


def _split_copies(bufs, plan, local, send_sems, recv_sems, local_sems):
    me = (lax.axis_index("x"), lax.axis_index("y"), lax.axis_index("c"))

    def at(ref, idx):
        return ref.at[idx] if len(idx) else ref

    def remote(k, sender, receiver):
        rel, si, ssel, di, dsel = plan[k]
        return pltpu.make_async_remote_copy(
            src_ref=at(bufs[si], ssel(sender, receiver)), dst_ref=at(bufs[di], dsel(sender, receiver)),
            send_sem=send_sems.at[k], recv_sem=recv_sems.at[k], device_id=_flip(me, rel), device_id_type=MESH)

    sends = [remote(k, me, _flip(me, plan[k][0])) for k in range(len(plan))]
    arrivals = [remote(k, _flip(me, plan[k][0]), me) for k in range(len(plan))]
    locs = [pltpu.make_async_copy(at(bufs[si], ssel(me)), at(bufs[di], dsel(me)), local_sems.at[k])
            for k, (si, ssel, di, dsel) in enumerate(local)]
    return sends, arrivals, locs


def _exchange_start(name, arrays, out_shapes, plan, local):
    n_in, n_out = len(arrays), len(out_shapes)

    n_buf = n_in + n_out

    def body(*refs):
        sems = refs[n_buf:n_buf + 3]
        bufs = refs[n_buf + 3:2 * n_buf + 3]
        token = refs[-1]
        sends, _, locs = _split_copies(bufs, plan, local, *sems)
        for cp in sends + locs:
            cp.start()
        token[...] = jnp.zeros_like(token)

    zones = [lax.empty(o.shape, o.dtype) for o in out_shapes]
    operands = [pltpu.with_memory_space_constraint(a, pltpu.HBM) for a in list(arrays) + zones]
    sem = lambda n: pltpu.SemaphoreType.DMA((max(n, 1),))
    out = pl.pallas_call(
        body, name=name,
        out_shape=(sem(len(plan)), sem(len(plan)), sem(len(local)), *[pltpu.HBM(a.shape, a.dtype) for a in operands],
                   _sds((SUBLANES, LANES), F32)),
        in_specs=[HBM_SPEC] * (n_in + n_out),
        out_specs=(SEM_SPEC, SEM_SPEC, SEM_SPEC, *[HBM_SPEC] * (n_in + n_out), pl.BlockSpec(memory_space=pltpu.VMEM)),
        input_output_aliases={i: 3 + i for i in range(n_in + n_out)},
        compiler_params=pltpu.CompilerParams(has_side_effects=pltpu.SideEffectType.DATAFLOW_SIDE_EFFECTING))(*operands)
    return out[:3], out[3:3 + n_in], out[3 + n_in:3 + n_in + n_out], out[-1]


def _exchange_wait(name, sems, thru, zones, plan, local, after):
    n_in, n_out = len(thru), len(zones)

    def body(*refs):
        bufs = refs[:n_in + n_out]
        sends, arrivals, locs = _split_copies(bufs, plan, local, *refs[n_in + n_out:n_in + n_out + 3])
        for cp in arrivals:
            cp.wait_recv()
        for cp in sends:
            cp.wait_send()
        for cp in locs:
            cp.wait()

    out = pl.pallas_call(
        body, name=name, out_shape=tuple(pltpu.HBM(a.shape, a.dtype) for a in list(thru) + list(zones)),
        in_specs=[HBM_SPEC] * (n_in + n_out) + [SEM_SPEC] * 3 + [pl.BlockSpec(memory_space=pl.ANY)],
        out_specs=tuple([HBM_SPEC] * (n_in + n_out)), input_output_aliases={i: i for i in range(n_in + n_out)},
        compiler_params=pltpu.CompilerParams(has_side_effects=pltpu.SideEffectType.DATAFLOW_SIDE_EFFECTING))(
            *thru, *zones, *sems, after)
    return out[n_in:]


def _whole(*_):
    return ()


def _half_rows(rows, core):
    return pl.ds(core * (rows // 2), rows // 2)


def _gather_weights_plan(shards):
    n = len(shards)
    dsts = [_sds((N_CHIPS,) + a.shape, a.dtype) for a in shards]
    fetch, forward = [], []
    for t, a in enumerate(shards):
        rows = a.shape[0]
        if rows % (2 * 16) == 0:
            fetch += [(rel, t, (lambda s_, r_, rows=rows: (_half_rows(rows, s_[2]),)), n + t,
                       (lambda s_, r_, rows=rows: (_chip_of(s_), _half_rows(rows, s_[2])))) for rel in REL_CHIPS]
            forward += [(REL_SIBLING, n + t, (lambda s_, r_, rows=rows, rel=rel: (_chip_of(_flip(s_, rel)), _half_rows(rows, s_[2]))),
                         n + t, (lambda s_, r_, rows=rows, rel=rel: (_chip_of(_flip(s_, rel)), _half_rows(rows, s_[2]))))
                        for rel in REL_CHIPS]
        else:
            fetch += [(rel, t, _whole, n + t, lambda s_, r_: (_chip_of(s_),)) for rel in REL_CHIPS]
    local = [(t, _whole, n + t, lambda me: (_chip_of(me),)) for t in range(n)]
    return dsts, fetch, local, forward


def _gather_weights_start(name, shards):
    dsts, fetch, local, forward = _gather_weights_plan(shards)
    sems, thru, zones, token = _exchange_start(name, shards, dsts, fetch, local)
    return (sems, thru, zones, fetch, local, forward), token


def _gather_weights_finish(name, pending, after):
    sems, thru, zones, fetch, local, forward = pending
    landed = _exchange_wait(name + "_wait", sems, thru, zones, fetch, local, after)
    n = len(landed)
    return _exchange(name + "_forward", landed, n, [[_sds(a.shape, a.dtype) for a in landed], (forward, [])],
                     aliases={t: t for t in range(n)})


def _gather_chips(name, arrays):
    n = len(arrays)
    dsts = [_sds((N_CHIPS,) + a.shape, a.dtype) for a in arrays]
    plan = [(rel, t, _whole, n + t, lambda s_, r_: (_chip_of(s_),)) for t in range(n) for rel in REL_CHIPS]
    local = [(t, _whole, n + t, lambda me: (_chip_of(me),)) for t in range(n)]
    return _exchange(name, arrays, n, [dsts, (plan, local)])


def _gather_all(name, a):
    plan = [(rel, 0, _whole, 1, lambda s_, r_: (_dev_of(s_),)) for rel in REL_ALL]
    local = [(0, _whole, 1, lambda me: (_dev_of(me),))]
    return _exchange(name, [a], 1, [[_sds((2 * N_CHIPS,) + a.shape, a.dtype)], (plan, local)])[0]


def _swap_halves(name, grads):
    n = len(grads)
    dsts = [_sds((g.shape[0], g.shape[1] // 2, g.shape[2]), g.dtype) for g in grads]
    plan = [(REL_SIBLING, t, (lambda s_, r_, rows=g.shape[1]: (pl.ds(0, N_CHIPS), _half_rows(rows, r_[2]))), n + t, _whole)
            for t, g in enumerate(grads)]
    return _exchange(name, grads, n, [dsts, (plan, [])])


def _scatter_chips_plan(sums):
    n = len(sums)
    dsts = [_sds(a.shape, a.dtype) for a in sums]
    plan = [(rel, t, lambda s_, r_: (_chip_of(r_),), n + t, lambda s_, r_: (_chip_of(s_),))
            for t in range(n) for rel in REL_CHIPS]
    local = [(t, lambda me: (_chip_of(me),), n + t, lambda me: (_chip_of(me),)) for t in range(n)]
    return dsts, plan, local


def _scatter_chips(name, sums):
    dsts, plan, local = _scatter_chips_plan(sums)
    return _exchange(name, sums, len(sums), [dsts, (plan, local)])


def _scatter_chips_start(name, sums):
    dsts, plan, local = _scatter_chips_plan(sums)
    sems, thru, zones, token = _exchange_start(name, sums, dsts, plan, local)
    return (sems, thru, zones, plan, local), token


def _scatter_chips_finish(name, pending, after):
    sems, thru, zones, plan, local = pending
    return _exchange_wait(name + "_wait", sems, thru, zones, plan, local, after)


def _swap_back(name, totals, layer):
    n = len(totals)
    dsts = [_sds(a.shape, a.dtype) for a in totals]
    plan = [(REL_SIBLING, n + t, (lambda s_, r_, rows=a.shape[1]: (layer, _half_rows(rows, s_[2]))),
             n + t, (lambda s_, r_, rows=a.shape[1]: (layer, _half_rows(rows, s_[2])))) for t, a in enumerate(totals)]
    return _exchange(name, totals, n, [dsts, (plan, [])], aliases={t: t for t in range(n)})


def _add_halves(name, g, recv, core):
    nch, half, c = recv.shape

    def body(core_ref, g_ref, r_ref, o_ref):
        o_ref[0] = (g_ref[0, 0].astype(F32) + r_ref[0].astype(F32)).astype(o_ref.dtype)

    spec = pltpu.PrefetchScalarGridSpec(
        num_scalar_prefetch=1, grid=(nch,),
        in_specs=[pl.BlockSpec((1, 1, half, c), lambda k, core_ref: (k, core_ref[0], 0, 0)),
                  pl.BlockSpec((1, half, c), lambda k, core_ref: (k, 0, 0))],
        out_specs=pl.BlockSpec((1, half, c), lambda k, core_ref: (k, 0, 0)))
    return pl.pallas_call(body, name=name, out_shape=_sds(recv.shape, recv.dtype), grid_spec=spec,
                          compiler_params=pltpu.CompilerParams(dimension_semantics=("parallel",),
                                                               vmem_limit_bytes=VMEM_LIMIT))(
                                                                   core, g.reshape(nch, 2, half, c), recv)


def _sum_chips(name, a, core, layer, total):
    nch, half, c = a.shape

    def body(core_ref, a_ref, *rest):
        o_ref = rest[-1]
        acc = a_ref[0].astype(F32)
        for k in range(1, nch):
            acc = acc + a_ref[k].astype(F32)
        o_ref[0, 0] = acc

    in_specs = [pl.BlockSpec((nch, half, c), lambda i, core_ref: (0, 0, 0))]
    args = [core, a]
    if total is not None:
        in_specs.append(pl.BlockSpec(memory_space=pl.ANY))
        args.append(total.reshape(DEPTH, 2, half, c))
    spec = pltpu.PrefetchScalarGridSpec(
        num_scalar_prefetch=1, grid=(1,), in_specs=in_specs,
        out_specs=pl.BlockSpec((1, 1, half, c), lambda i, core_ref: (layer, core_ref[0], 0, 0)))
    out = pl.pallas_call(body, name=name, out_shape=_sds((DEPTH, 2, half, c), F32), grid_spec=spec,
                         input_output_aliases={2: 0} if total is not None else {},
                         compiler_params=pltpu.CompilerParams(dimension_semantics=("arbitrary",),
                                                              vmem_limit_bytes=VMEM_LIMIT))(*args)
    return out.reshape(DEPTH, 2 * half, c)


def _adamw_update(w, g, m, v):
    mn = ADAM_B1 * m + (1.0 - ADAM_B1) * g
    vn = ADAM_B2 * v + (1.0 - ADAM_B2) * (g * g)
    m_hat = mn / (1.0 - ADAM_B1 ** ADAM_STEP)
    v_hat = vn / (1.0 - ADAM_B2 ** ADAM_STEP)
    return -ADAM_LR * (m_hat / (jnp.sqrt(v_hat) + ADAM_EPS) + ADAM_WD * w), mn, vn


def _adamw(name, w, g, m, v):
    depth, r, c = w.shape
    tr = next(t for t in (512, 448, 384, 352, 336, 256, 192, 128, 64, 32, 16, 8) if r % t == 0 and t * c <= ADAM_TILE_ELEMS)

    def body(w_ref, g_ref, m_ref, v_ref, go_ref, d_ref, mo_ref, vo_ref):
        gv = g_ref[...]
        go_ref[...] = gv
        d_ref[...], mo_ref[...], vo_ref[...] = _adamw_update(w_ref[...], gv, m_ref[...], v_ref[...])

    spec = pl.BlockSpec((1, tr, c), lambda l, i: (l, i, 0))
    o = _sds(w.shape, F32)
    return _call(body, name, (o, o, o, o), (depth, r // tr), [spec] * 4, (spec,) * 4, ("parallel", "parallel"))(w, g, m, v)


def _adamw_small(name, ws, gs, ms, vs):
    n = len(ws)

    def body(*refs):
        for t in range(n):
            w_ref, g_ref, m_ref, v_ref = (refs[k * n + t] for k in range(4))
            d_ref, mo_ref, vo_ref = (refs[(4 + k) * n + t] for k in range(3))
            d_ref[...], mo_ref[...], vo_ref[...] = _adamw_update(w_ref[...], g_ref[...], m_ref[...], v_ref[...])

    outs = [_sds(w.shape, F32) for w in ws]
    res = pl.pallas_call(body, name=name, out_shape=tuple(outs * 3))(*ws, *gs, *ms, *vs)
    return res[:n], res[n:2 * n], res[2 * n:]


def _pack(arrays, rows):
    flat = jnp.concatenate([a.reshape(-1).astype(F32) for a in arrays])
    return _pad_to(flat, 0, rows * LANES).reshape(rows, LANES)


def _unpack(packed, shapes):
    flat = packed.reshape(-1)
    out, off = [], 0
    for shp in shapes:
        n = 1
        for s_ in shp:
            n *= s_
        out.append(flat[off:off + n].reshape(shp))
        off += n
    return out


_MATRICES = ("w_in", "proj_a", "proj_b", "w_out", "w_gate", "w_up", "w_down")
_SMALL = (("norm1_w", (D_MODEL,)), ("sinks", (N_Q_HEADS,)), ("sgu_ln_w", (SGU_WIDTH,)), ("sgu_ln_b", (SGU_WIDTH,)),
          ("sgu_w", (SGU_GROUPS, SGU_CHUNK, SGU_CHUNK)), ("sgu_b", (SGU_GROUPS, SGU_CHUNK)), ("norm2_w", (D_MODEL,)),
          ("conv_w", (3, FFN_DIM)), ("conv_b", (FFN_DIM,)), ("final_norm_w", (D_MODEL,)))
SMALL_ROWS = 320
ADAM_TILE_ELEMS = 384 * 1024


def _reduce_cores(tag, partial, core):
    names = list(partial)
    from_sibling = _swap_halves(tag + "_cores", [partial[k] for k in names])
    return names, [_add_halves(f"{tag}_cores_add_{k}", partial[k], r, core) for k, r in zip(names, from_sibling)]


def _reduce_finish(tag, l, names, from_chips, core, totals):
    sums = [_sum_chips(f"{tag}_chips_add_{k}", a, core, l, totals.get(k)) for k, a in zip(names, from_chips)]
    return dict(totals, **dict(zip(names, _swap_back(tag + "_back", sums, l))))


def kernel(x, c, positions, ada_w, ada_b, norm1_w, w_in, attn_sinks, sgu_ln_w, sgu_ln_b, sgu_w, sgu_b, proj_a, proj_b, w_out, norm2_w, ffn_w_gate, ffn_w_up, ffn_conv_w, ffn_conv_b, ffn_w_down, final_norm_w, loss_target, m_ada_w, m_ada_b, m_norm1_w, m_w_in, m_attn_sinks, m_sgu_ln_w, m_sgu_ln_b, m_sgu_w, m_sgu_b, m_proj_a, m_proj_b, m_w_out, m_norm2_w, m_ffn_w_gate, m_ffn_w_up, m_ffn_conv_w, m_ffn_conv_b, m_ffn_w_down, m_final_norm_w, v_ada_w, v_ada_b, v_norm1_w, v_w_in, v_attn_sinks, v_sgu_ln_w, v_sgu_ln_b, v_sgu_w, v_sgu_b, v_proj_a, v_proj_b, v_w_out, v_norm2_w, v_ffn_w_gate, v_ffn_w_up, v_ffn_conv_w, v_ffn_conv_b, v_ffn_w_down, v_final_norm_w):
    d = D_MODEL
    ax, ay, ac = lax.axis_index("x"), lax.axis_index("y"), lax.axis_index("c")
    chip = 2 * ax + ay
    dev = 4 * ax + 2 * ay + ac
    core = ac.astype(jnp.int32).reshape(1)

    c_all = _gather_all("gather_cond", c.reshape(SUBLANES, d // SUBLANES)).reshape(2 * N_CHIPS, d)
    c_rows = _pad_to(c_all, 0, ADA_ROWS)
    ada_cols = ada_w.shape[2]
    ada_b_cols = lax.dynamic_slice_in_dim(ada_b, chip * ada_cols, ada_cols, axis=1).reshape(DEPTH, 1, ada_cols)
    mod_cols = _ada_fwd("ada_fwd", c_rows, ada_w, ada_b_cols)
    mod_all = _gather_chips("gather_mod", [mod_cols])[0]
    mod_mine = lax.dynamic_index_in_dim(mod_all, dev, axis=2, keepdims=False)
    mod_mine = mod_mine.transpose(1, 0, 2).reshape(DEPTH, 1, 6 * d)
    mods = [tuple(jnp.split(mod_mine[l], 6, axis=-1)) for l in range(DEPTH)]

    tr = lambda a: jnp.swapaxes(a, 1, 2)
    behind = mod_all[0, 0, 0, 0] * 0.0
    shards = [tr(w_in).astype(BF) + behind.astype(BF), proj_a.astype(BF), proj_b.astype(BF), w_out.astype(BF),
              _pad_to(tr(ffn_w_gate).astype(BF), 1, FFN_SHARD_PAD), _pad_to(tr(ffn_w_up).astype(BF), 1, FFN_SHARD_PAD),
              _pad_to(ffn_w_down.astype(BF), 1, FFN_SHARD_PAD), _pad_to(ffn_conv_w, 2, FFN_SHARD_PAD)]
    first, token = _gather_weights_start("l0_gather_in", [shards[0][0]])
    rest0, token = _gather_weights_start("l0_gather_rest", [a[0] for a in shards[1:-1]] + [shards[-1][0] + token[0, 0]])
    layer1, token = _gather_weights_start("l1_gather", [a[1] for a in shards[:-1]] + [shards[-1][1] + token[0, 0]])

    small_in = dict(norm1_w=norm1_w, sinks=attn_sinks, sgu_ln_w=sgu_ln_w, sgu_ln_b=sgu_ln_b, sgu_w=sgu_w, sgu_b=sgu_b,
                    norm2_w=norm2_w, conv_b=ffn_conv_b)
    cosf, sinf = _rope_tables(positions[0])
    small_of = lambda l: {k: v[l] for k, v in small_in.items()}

    w_in0 = _gather_weights_finish("l0_gather_in", first, token)
    late = lambda y: _late_params(*_gather_weights_finish("l0_gather_rest", rest0, y), small_of(0))
    h, sv, p0 = _layer_fwd(0, x[0], mods[0], _early_params(w_in0[0], small_of(0)), cosf, sinf, late=late)
    saved, params = [sv], [p0]
    p1 = _layer_params(*_gather_weights_finish("l1_gather", layer1, h), small_of(1))
    h, sv, p1 = _layer_fwd(1, h, mods[1], p1, cosf, sinf)
    saved.append(sv)
    params.append(p1)
    dx, loss_part, d_final = _loss_head("loss_head", h, final_norm_w.reshape(1, d), loss_target[0])
    loss = lax.psum(loss_part[0, 0], ("x", "y", "c"))

    def small_pack(l, grads):
        cw, cb = _conv_grads_natural(grads)
        nat = dict(grads, conv_w=cw, conv_b=cb, final_norm_w=d_final if l == DEPTH - 1 else jnp.zeros((d,), F32))
        return _pack([nat[k] for k, _ in _SMALL], N_CHIPS * SMALL_ROWS).reshape(N_CHIPS, SMALL_ROWS, LANES)

    inflight = []

    def send(tag, l, partial):
        names, sums = _reduce_cores(tag, partial, core)
        pending, token = _scatter_chips_start(tag + "_chips", sums)
        inflight.append((tag, l, names, pending))
        return token

    dmods = [None] * DEPTH
    dx, dmods[1], grads = _layer_bwd(1, dx, mods[1], params[1], saved[1], cosf, sinf)
    token = send("l1_reduce", 1, dict({k: grads[k] for k in _MATRICES}, small=small_pack(1, grads)))
    dx, dmods[0], grads = _layer_bwd(0, dx, mods[0], params[0], saved[0], cosf, sinf, after=token,
                                     emit=lambda part: send("l0_reduce_" + "_".join(part), 0, part))
    send("l0_reduce_in", 0, dict(w_in=grads["w_in"], small=small_pack(0, grads)))
    totals = {}
    for tag, l, names, pending in inflight:
        totals = _reduce_finish(tag, l, names, _scatter_chips_finish(tag + "_chips", pending, dx), core, totals)

    dmod_mine = jnp.concatenate([jnp.concatenate(dmods[l], axis=1) for l in range(DEPTH)], axis=1)
    dmod_all = _gather_all("gather_dmod", dmod_mine.reshape(SUBLANES, -1)).reshape(2 * N_CHIPS, DEPTH * 6 * d)
    g_ada_b = _colsum("ada_b_grad", dmod_all).reshape(DEPTH, 6 * d)
    dmod_cols = jnp.stack([lax.dynamic_slice_in_dim(dmod_all, l * 6 * d + chip * ada_cols, ada_cols, axis=1)
                           for l in range(DEPTH)])
    g_ada_w = _ada_bwd("ada_w_grad", c_rows, _pad_to(dmod_cols, 1, ADA_ROWS))

    small_all = _gather_chips("gather_small", [totals["small"]])[0]
    small_g = small_all.transpose(1, 0, 2, 3).reshape(DEPTH, -1)
    per_layer = [_unpack(small_g[l], [shp for _, shp in _SMALL]) for l in range(DEPTH)]
    sg = {k: jnp.stack([per_layer[l][i] for l in range(DEPTH)]) for i, (k, _) in enumerate(_SMALL)}
    g_final = sg["final_norm_w"][DEPTH - 1]
    g_conv_w = lax.dynamic_slice_in_dim(sg["conv_w"], chip * FFN_SHARD, FFN_SHARD, axis=2)

    big = dict(w_in=(tr(w_in), tr(m_w_in), tr(v_w_in)), proj_a=(proj_a, m_proj_a, v_proj_a), proj_b=(proj_b, m_proj_b, v_proj_b),
               w_out=(w_out, m_w_out, v_w_out), w_gate=(tr(ffn_w_gate), tr(m_ffn_w_gate), tr(v_ffn_w_gate)),
               w_up=(tr(ffn_w_up), tr(m_ffn_w_up), tr(v_ffn_w_up)), w_down=(ffn_w_down, m_ffn_w_down, v_ffn_w_down))
    transposed = ("w_in", "w_gate", "w_up")
    upd, g_big = {}, {}
    for k, (w, m, v) in big.items():
        res = _adamw("adamw_" + k, w, totals[k], m, v)
        res = [tr(a) for a in res] if k in transposed else res
        g_big[k], upd[k] = res[0], res[1:]
    g_big["ada_w"], *upd["ada_w"] = _adamw("adamw_ada_w", ada_w, g_ada_w, m_ada_w, v_ada_w)
    rest = [("ada_b", ada_b, g_ada_b, m_ada_b, v_ada_b), ("norm1_w", norm1_w, sg["norm1_w"], m_norm1_w, v_norm1_w),
            ("attn_sinks", attn_sinks, sg["sinks"], m_attn_sinks, v_attn_sinks),
            ("sgu_ln_w", sgu_ln_w, sg["sgu_ln_w"], m_sgu_ln_w, v_sgu_ln_w),
            ("sgu_ln_b", sgu_ln_b, sg["sgu_ln_b"], m_sgu_ln_b, v_sgu_ln_b), ("sgu_w", sgu_w, sg["sgu_w"], m_sgu_w, v_sgu_w),
            ("sgu_b", sgu_b, sg["sgu_b"], m_sgu_b, v_sgu_b), ("norm2_w", norm2_w, sg["norm2_w"], m_norm2_w, v_norm2_w),
            ("ffn_conv_w", ffn_conv_w, g_conv_w, m_ffn_conv_w, v_ffn_conv_w),
            ("ffn_conv_b", ffn_conv_b, sg["conv_b"], m_ffn_conv_b, v_ffn_conv_b),
            ("final_norm_w", final_norm_w.reshape(1, d), g_final.reshape(1, d), m_final_norm_w.reshape(1, d),
             v_final_norm_w.reshape(1, d))]
    rest_out = _adamw_small("adamw_rest", *[[r[i] for r in rest] for i in (1, 2, 3, 4)])
    g_rest = {r[0]: r[2] for r in rest}
    u_rest = {r[0]: tuple(o[i] for o in rest_out) for i, r in enumerate(rest)}
    g_rest["final_norm_w"] = g_final
    u_rest["final_norm_w"] = tuple(a.reshape(d) for a in u_rest["final_norm_w"])

    names = ("ada_w", "ada_b", "norm1_w", "w_in", "attn_sinks", "sgu_ln_w", "sgu_ln_b", "sgu_w", "sgu_b", "proj_a", "proj_b",
             "w_out", "norm2_w", "ffn_w_gate", "ffn_w_up", "ffn_conv_w", "ffn_conv_b", "ffn_w_down", "final_norm_w")
    alias = {"ffn_w_gate": "w_gate", "ffn_w_up": "w_up", "ffn_w_down": "w_down"}
    grad_of = lambda n: g_rest[n] if n in g_rest else g_big[alias.get(n, n)]
    upd_of = lambda n: u_rest[n] if n in u_rest else upd[alias.get(n, n)]
    return (loss, dx[None], *[grad_of(n) for n in names], *[upd_of(n)[0] for n in names],
            *[upd_of(n)[1] for n in names], *[upd_of(n)[2] for n in names])
```

```python
import jax
import jax.numpy as jnp
from jax import lax
from jax.experimental import pallas as pl
from jax.experimental.pallas import tpu as pltpu

F32 = jnp.float32
BF = jnp.bfloat16

D_MODEL = 1024
N_Q_HEADS = 16
N_KV_HEADS = 2
HEAD_DIM = 64
ATTN_BLOCK = 128
ROPE_THETA = 500000.0
ROT_DIM = HEAD_DIM // 4
SGU_WIDTH = 1024
SGU_GROUPS = 8
SGU_CHUNK = 128
FFN_DIM = 2816
NORM_EPS = 1e-6
DEPTH = 2
IN_COLS = 5376
N_CHIPS = 4
FFN_SHARD = FFN_DIM // N_CHIPS
FFN_SHARD_PAD = 768
FFN_PAD = N_CHIPS * FFN_SHARD_PAD
LANES = 128
SUBLANES = 8
VMEM_LIMIT = 56 * 1024 * 1024
NEG_BIG = -1e30

ADAM_LR = 0.001
ADAM_B1 = 0.9
ADAM_B2 = 0.999
ADAM_EPS = 1e-08
ADAM_WD = 0.01
ADAM_STEP = 10

MESH = pl.DeviceIdType.MESH

Q_END = 1024
KV_END = 1280
U_END = 2304
Z_END = 3328
GA_END = 4352


def _sds(shape, dtype):
    return jax.ShapeDtypeStruct(tuple(shape), dtype)


def _call(body, name, out_shape, grid, in_specs, out_specs, semantics, scratch=(), after=None):
    n_in = len(in_specs)
    fn = body
    if after is not None:
        def fn(*refs):
            return body(*refs[:n_in], *refs[n_in + 1:])

        in_specs = list(in_specs) + [pl.BlockSpec(memory_space=pl.ANY)]
    call = pl.pallas_call(
        fn, name=name, out_shape=out_shape, grid=grid, in_specs=in_specs, out_specs=out_specs,
        scratch_shapes=scratch,
        compiler_params=pltpu.CompilerParams(dimension_semantics=semantics, vmem_limit_bytes=VMEM_LIMIT))
    if after is None:
        return call
    return lambda *args: call(*args, after)


def _rows(tm, width, col=0):
    return pl.BlockSpec((tm, width), lambda i: (i, col))


def _vec(width):
    return pl.BlockSpec((1, width), lambda i: (0, 0))


def _resident(shape):
    zeros = (0,) * len(shape)
    return pl.BlockSpec(tuple(shape), lambda *_: zeros, pipeline_mode=pl.Buffered(1))


def _sigmoid(x):
    return 1.0 / (1.0 + jnp.exp(-x))


def _gelu(x):
    return 0.5 * x * (1.0 + lax.erf(x * 0.7071067811865476))


def _gelu_grad(x):
    cdf = 0.5 * (1.0 + lax.erf(x * 0.7071067811865476))
    return cdf + x * jnp.exp(-0.5 * x * x) * 0.3989422804014327


def _dot(a, b):
    return jnp.dot(a, b, preferred_element_type=F32)


def _dot_nt(a, b):
    return lax.dot_general(a, b, (((1,), (1,)), ((), ())), preferred_element_type=F32)


def _dot_tn(a, b):
    return lax.dot_general(a, b, (((0,), (0,)), ((), ())), preferred_element_type=F32)


def _rms(xv):
    return lax.rsqrt(jnp.mean(xv * xv, axis=-1, keepdims=True) + NORM_EPS)


def _matmul_tn(name, a, b, tk=512, tn=1024, blocked=False):
    s, k = a.shape
    n = b.shape[1]
    tk, tn = min(tk, k), min(tn, n)

    def body(a_ref, b_ref, o_ref):
        res = _dot_tn(a_ref[...], b_ref[...]).astype(o_ref.dtype)
        if blocked:
            o_ref[0] = res
        else:
            o_ref[...] = res

    if blocked:
        out, ospec = _sds((n // tn, k, tn), BF), pl.BlockSpec((1, tk, tn), lambda i, j: (j, i, 0))
    else:
        out, ospec = _sds((k, n), BF), pl.BlockSpec((tk, tn), lambda i, j: (i, j))
    return _call(body, name, out, (k // tk, n // tn),
                 [pl.BlockSpec((s, tk), lambda i, j: (0, i)), pl.BlockSpec((s, tn), lambda i, j: (0, j))],
                 ospec, ("parallel", "parallel"))(a, b)


def _matmul_tn_rows(name, a, b, out, row0, rows_total, tk=256):
    s, k = a.shape
    n = b.shape[1]

    def body(a_ref, b_ref, *rest):
        rest[-1][...] = _dot_tn(a_ref[...], b_ref[...]).astype(BF)

    in_specs = [pl.BlockSpec((s, tk), lambda i: (0, i)), _resident(b.shape)]
    args = [a, b]
    if out is not None:
        in_specs.append(pl.BlockSpec(memory_space=pl.ANY))
        args.append(out)
    return pl.pallas_call(
        body, name=name, out_shape=_sds((rows_total, n), BF), grid=(k // tk,), in_specs=in_specs,
        out_specs=pl.BlockSpec((tk, n), lambda i: (row0 // tk + i, 0)),
        input_output_aliases={2: 0} if out is not None else {},
        compiler_params=pltpu.CompilerParams(dimension_semantics=("parallel",), vmem_limit_bytes=VMEM_LIMIT))(*args)


def _rope_partner(v):
    lane = lax.broadcasted_iota(jnp.int32, (1, LANES), 1) % HEAD_DIM
    return jnp.where(lane < ROT_DIM // 2, pltpu.roll(v, LANES - ROT_DIM // 2, axis=1), pltpu.roll(v, ROT_DIM // 2, axis=1))


def _dup_half(v, half):
    lane = lax.broadcasted_iota(jnp.int32, (1, LANES), 1)
    keep = jnp.where((lane >= HEAD_DIM) == (half == 1), v, 0.0)
    return keep + pltpu.roll(keep, HEAD_DIM, axis=1)


def _in_proj(name, x, w, sc, sh, w_in, cosf, sinf, tm=256, after=None):
    s, d = x.shape
    tm = min(tm, s)

    def body(x_ref, w_ref, sc_ref, sh_ref, win_ref, cos_ref, sin_ref,
             h_ref, qr_ref, kk0_ref, kk1_ref, vv0_ref, vv1_ref, u_ref, v_ref, ga_ref, gb_ref):
        xv = x_ref[...]
        h = ((xv * _rms(xv)) * w_ref[...] * (1.0 + sc_ref[...]) + sh_ref[...]).astype(BF)
        h_ref[...] = h
        cosv, sinv = cos_ref[...], sin_ref[...]
        q = _dot_nt(h, win_ref[:Q_END, :])
        for j in range(D_MODEL // LANES):
            qv = q[:, j * LANES:(j + 1) * LANES]
            qr_ref[:, j * LANES:(j + 1) * LANES] = (qv * cosv + _rope_partner(qv) * sinv).astype(BF)
        kv = _dot_nt(h, win_ref[Q_END:KV_END, :])
        kr = kv[:, :LANES] * cosv + _rope_partner(kv[:, :LANES]) * sinv
        vv = kv[:, LANES:]
        kk0_ref[...] = _dup_half(kr, 0).astype(BF)
        kk1_ref[...] = _dup_half(kr, 1).astype(BF)
        vv0_ref[...] = _dup_half(vv, 0).astype(BF)
        vv1_ref[...] = _dup_half(vv, 1).astype(BF)
        u_ref[...] = _dot_nt(h, win_ref[KV_END:U_END, :])
        v_ref[...] = _dot_nt(h, win_ref[U_END:Z_END, :])
        ga_ref[...] = _dot_nt(h, win_ref[Z_END:GA_END, :])
        gb_ref[...] = _dot_nt(h, win_ref[GA_END:, :])

    wide, kvs, pre = _sds((s, d), BF), _sds((s, LANES), BF), _sds((s, d), F32)
    return _call(body, name, (wide, wide, kvs, kvs, kvs, kvs, pre, pre, pre, pre), (s // tm,),
                 [_rows(tm, d), _vec(d), _vec(d), _vec(d), _resident(w_in.shape), _rows(tm, LANES), _rows(tm, LANES)],
                 (_rows(tm, d), _rows(tm, d)) + (_rows(tm, LANES),) * 4 + (_rows(tm, d),) * 4, ("parallel",), after=after)(
                     x, w, sc, sh, w_in, cosf, sinf)


def _in_proj_bwd(name, dq, dkv, du, dv, dga, dgb, w_in, x, w, sc, dx_in, tm=256):
    s, d = x.shape
    tm = min(tm, s)

    def body(dq_ref, dkv_ref, du_ref, dv_ref, dga_ref, dgb_ref, win_ref, x_ref, w_ref, sc_ref, dxin_ref,
             dx_ref, da_ref, dsh_ref):
        @pl.when(pl.program_id(0) == 0)
        def _():
            da_ref[...] = jnp.zeros_like(da_ref)
            dsh_ref[...] = jnp.zeros_like(dsh_ref)

        dh = (_dot(dq_ref[...], win_ref[:Q_END, :]) + _dot(dkv_ref[...], win_ref[Q_END:KV_END, :])
              + _dot(du_ref[...], win_ref[KV_END:U_END, :]) + _dot(dv_ref[...], win_ref[U_END:Z_END, :])
              + _dot(dga_ref[...], win_ref[Z_END:GA_END, :]) + _dot(dgb_ref[...], win_ref[GA_END:, :]))
        xv = x_ref[...]
        r = _rms(xv)
        xn = xv * r
        dxn = dh * (w_ref[...] * (1.0 + sc_ref[...]))
        dx_ref[...] = dxin_ref[...] + r * (dxn - xn * jnp.mean(dxn * xn, axis=-1, keepdims=True))
        da_ref[...] += jnp.sum(dh * xn, axis=0, keepdims=True)
        dsh_ref[...] += jnp.sum(dh, axis=0, keepdims=True)

    return _call(body, name, (_sds((s, d), F32), _sds((1, d), F32), _sds((1, d), F32)), (s // tm,),
                 [_rows(tm, d), _rows(tm, 2 * LANES), _rows(tm, d), _rows(tm, d), _rows(tm, d), _rows(tm, d),
                  _resident(w_in.shape), _rows(tm, d), _vec(d), _vec(d), _rows(tm, d)],
                 (_rows(tm, d), _vec(d), _vec(d)), ("arbitrary",))(dq, dkv, du, dv, dga, dgb, w_in, x, w, sc, dx_in)


def _rope_bwd(name, dqr, dkv_cur, dkv_prev, cosf, sinf, tm=512):
    s = dqr.shape[0]
    tm = min(tm, s)
    steps = s // tm
    per = tm // ATTN_BLOCK
    nb = s // ATTN_BLOCK

    def unrope(v, cosv, sinv):
        return v * cosv - _rope_partner(v) * sinv

    def body(dq_ref, cur_ref, prev_ref, next_ref, cos_ref, sin_ref, dqo_ref, dkvo_ref):
        i = pl.program_id(0)
        cosv, sinv = cos_ref[...], sin_ref[...]
        for j in range(D_MODEL // LANES):
            dqo_ref[:, j * LANES:(j + 1) * LANES] = unrope(dq_ref[:, j * LANES:(j + 1) * LANES], cosv, sinv).astype(BF)
        nxt = jnp.where(i < steps - 1, next_ref[...], 0.0)
        if per > 1:
            shifted = jnp.concatenate([prev_ref[ATTN_BLOCK:, :], nxt], axis=0)
        else:
            shifted = nxt
        tot = cur_ref[...] + shifted
        dkvo_ref[:, :LANES] = unrope(tot[:, :LANES], cosv, sinv).astype(BF)
        dkvo_ref[:, LANES:] = tot[:, LANES:].astype(BF)

    nxt_spec = pl.BlockSpec((ATTN_BLOCK, 2 * LANES), lambda i: (jnp.minimum((i + 1) * per, nb - 1), 0))
    return _call(body, name, (_sds((s, D_MODEL), BF), _sds((s, 2 * LANES), BF)), (steps,),
                 [_rows(tm, D_MODEL), _rows(tm, 2 * LANES), _rows(tm, 2 * LANES), nxt_spec, _rows(tm, LANES),
                  _rows(tm, LANES)],
                 (_rows(tm, D_MODEL), _rows(tm, 2 * LANES)), ("parallel",))(dqr, dkv_cur, dkv_prev, dkv_prev, cosf, sinf)


Q_PER_KV = N_Q_HEADS // N_KV_HEADS


def _band_mask_t(n):
    kj = lax.broadcasted_iota(jnp.int32, (2 * ATTN_BLOCK, ATTN_BLOCK), 0)
    qi = lax.broadcasted_iota(jnp.int32, (2 * ATTN_BLOCK, ATTN_BLOCK), 1)
    return (kj > qi) & (kj <= qi + ATTN_BLOCK) & ((n > 0) | (kj >= ATTN_BLOCK))


def _softmax_t(raw, allowed, sink):
    sc = jnp.where(allowed, raw * (HEAD_DIM ** -0.5), NEG_BIG)
    m = jnp.maximum(jnp.max(sc, axis=0, keepdims=True), sink)
    p = jnp.exp(sc - m)
    esink = jnp.exp(sink - m)
    inv = 1.0 / (jnp.sum(p, axis=0, keepdims=True) + esink)
    return p * inv, esink * inv


def _kv_specs():
    cur = pl.BlockSpec((ATTN_BLOCK, LANES), lambda n: (n, 0))
    prev = pl.BlockSpec((ATTN_BLOCK, LANES), lambda n: (jnp.maximum(n - 1, 0), 0))
    return [prev, cur] * 4


def _attention(name, qr, kk0, kk1, vv0, vv1, sinks):
    s = qr.shape[0]
    nb = s // ATTN_BLOCK

    def body(sink_ref, q_ref, k0p, k0c, k1p, k1c, v0p, v0c, v1p, v1c, y_ref):
        allowed = _band_mask_t(pl.program_id(0))
        upper = lax.broadcasted_iota(jnp.int32, (1, LANES), 1) >= HEAD_DIM
        upper_rows = lax.broadcasted_iota(jnp.int32, (LANES, 1), 0) >= HEAD_DIM
        bands = ((jnp.concatenate([k0p[...], k0c[...]], axis=0), jnp.concatenate([v0p[...], v0c[...]], axis=0)),
                 (jnp.concatenate([k1p[...], k1c[...]], axis=0), jnp.concatenate([v1p[...], v1c[...]], axis=0)))
        vbts = (bands[0][1].T, bands[1][1].T)

        def scores(h):
            hk, j, half = h // Q_PER_KV, (h % Q_PER_KV) // 2, h % 2
            col = (hk * 4 + j) * LANES
            qp = q_ref[:, col:col + LANES]
            return _dot_nt(bands[hk][0], jnp.where(upper if half else jnp.logical_not(upper), qp, jnp.zeros_like(qp)))

        out_t = None
        ahead = scores(0)
        for h in range(N_Q_HEADS):
            hk, j, half = h // Q_PER_KV, (h % Q_PER_KV) // 2, h % 2
            raw = ahead
            if h + 1 < N_Q_HEADS:
                ahead = scores(h + 1)
            pn, _ = _softmax_t(raw, allowed, sink_ref[h])
            o_h = _dot(vbts[hk], pn.astype(BF))
            out_t = jnp.where(upper_rows, o_h, out_t) if half else o_h
            if half:
                col = (hk * 4 + j) * LANES
                y_ref[:, col:col + LANES] = out_t.T.astype(BF)

    return _call(body, name, _sds((s, D_MODEL), BF), (nb,),
                 [pl.BlockSpec(memory_space=pltpu.SMEM), pl.BlockSpec((ATTN_BLOCK, D_MODEL), lambda n: (n, 0))] + _kv_specs(),
                 pl.BlockSpec((ATTN_BLOCK, D_MODEL), lambda n: (n, 0)), ("parallel",))(
                     sinks, qr, kk0, kk0, kk1, kk1, vv0, vv0, vv1, vv1)


def _attention_bwd(name, qr, kk0, kk1, vv0, vv1, sinks, dy):
    s = qr.shape[0]
    nb = s // ATTN_BLOCK

    def body(sink_ref, q_ref, dy_ref, k0p, k0c, k1p, k1c, v0p, v0c, v1p, v1c, dq_ref, cur_ref, prev_ref, dsink_ref):
        @pl.when(pl.program_id(0) == 0)
        def _():
            dsink_ref[...] = jnp.zeros_like(dsink_ref)

        allowed = _band_mask_t(pl.program_id(0))
        lane = lax.broadcasted_iota(jnp.int32, (1, LANES), 1)
        upper = lane >= HEAD_DIM
        upper_rows = lax.broadcasted_iota(jnp.int32, (LANES, 1), 0) >= HEAD_DIM
        bands = ((jnp.concatenate([k0p[...], k0c[...]], axis=0), jnp.concatenate([v0p[...], v0c[...]], axis=0)),
                 (jnp.concatenate([k1p[...], k1c[...]], axis=0), jnp.concatenate([v1p[...], v1c[...]], axis=0)))
        kbts = (bands[0][0].T, bands[1][0].T)

        def scores(h):
            hk, j, half = h // Q_PER_KV, (h % Q_PER_KV) // 2, h % 2
            kb, vb = bands[hk]
            col = (hk * 4 + j) * LANES
            sel = upper if half else jnp.logical_not(upper)
            qp = q_ref[:, col:col + LANES]
            qa = jnp.where(sel, qp, jnp.zeros_like(qp))
            dya = jnp.where(sel, dy_ref[:, col:col + LANES], 0.0).astype(BF)
            return qa, dya, _dot_nt(kb, qa), _dot_nt(vb, dya)

        dsink = jnp.zeros((1, LANES), F32)
        dk_slab = jnp.zeros((2 * ATTN_BLOCK, LANES), F32)
        dv_slab = jnp.zeros((2 * ATTN_BLOCK, LANES), F32)
        dkk = dvv = dq_t = None
        ahead = scores(0)
        for h in range(N_Q_HEADS):
            hk, j, half = h // Q_PER_KV, (h % Q_PER_KV) // 2, h % 2
            qa, dya, raw, dp = ahead
            if h + 1 < N_Q_HEADS:
                ahead = scores(h + 1)
            pn, psink = _softmax_t(raw, allowed, sink_ref[h])
            delta = jnp.sum(pn * dp, axis=0, keepdims=True)
            ds = (pn * (dp - delta) * (HEAD_DIM ** -0.5)).astype(BF)
            dsink = dsink + jnp.where(lane == h, -jnp.sum(psink * delta), 0.0)
            dq_h = _dot(kbts[hk], ds)
            dq_t = jnp.where(upper_rows, dq_h, dq_t) if half else dq_h
            dk_h, dv_h = _dot(ds, qa), _dot(pn.astype(BF), dya)
            dkk, dvv = (dk_h, dv_h) if h % Q_PER_KV == 0 else (dkk + dk_h, dvv + dv_h)
            if half:
                col = (hk * 4 + j) * LANES
                dq_ref[:, col:col + LANES] = dq_t.T
            if h % Q_PER_KV == Q_PER_KV - 1:
                mine = upper if hk else jnp.logical_not(upper)
                dk_slab = jnp.where(mine, dkk + pltpu.roll(dkk, HEAD_DIM, axis=1), dk_slab)
                dv_slab = jnp.where(mine, dvv + pltpu.roll(dvv, HEAD_DIM, axis=1), dv_slab)
        prev_ref[:, :LANES] = dk_slab[:ATTN_BLOCK]
        prev_ref[:, LANES:] = dv_slab[:ATTN_BLOCK]
        cur_ref[:, :LANES] = dk_slab[ATTN_BLOCK:]
        cur_ref[:, LANES:] = dv_slab[ATTN_BLOCK:]
        dsink_ref[...] += dsink

    blk = pl.BlockSpec((ATTN_BLOCK, D_MODEL), lambda n: (n, 0))
    kvo = pl.BlockSpec((ATTN_BLOCK, 2 * LANES), lambda n: (n, 0))
    return _call(body, name,
                 (_sds((s, D_MODEL), F32), _sds((s, 2 * LANES), F32), _sds((s, 2 * LANES), F32), _sds((1, LANES), F32)),
                 (nb,), [pl.BlockSpec(memory_space=pltpu.SMEM), blk, blk] + _kv_specs(),
                 (blk, kvo, kvo, pl.BlockSpec((1, LANES), lambda n: (0, 0))), ("arbitrary",))(
                     sinks, qr, dy, kk0, kk0, kk1, kk1, vv0, vv0, vv1, vv1)


def _sgu_weights(wm_ref, g):
    t = lax.broadcasted_iota(jnp.int32, (SGU_CHUNK, SGU_CHUNK), 0)
    sidx = lax.broadcasted_iota(jnp.int32, (SGU_CHUNK, SGU_CHUNK), 1)
    return jnp.where(sidx <= t, wm_ref[g], 0.0).astype(BF)


def _layer_norm_stats(v):
    mu = jnp.mean(v, axis=-1, keepdims=True)
    cen = v - mu
    rstd = lax.rsqrt(jnp.mean(cen * cen, axis=-1, keepdims=True) + NORM_EPS)
    return cen * rstd, rstd


def _sgu(name, u_pre, v_pre, ln_w, ln_b, wm, bfull, tm=256):
    s, w = u_pre.shape
    tm = min(tm, s)

    def body(u_ref, v_ref, lw_ref, lb_ref, wm_ref, b_ref, y_ref):
        vhat, _ = _layer_norm_stats(_gelu(v_ref[...]))
        vn = (vhat * lw_ref[...] + lb_ref[...]).astype(BF)
        for g in range(SGU_GROUPS):
            wg = _sgu_weights(wm_ref, g)
            cols = slice(g * SGU_CHUNK, (g + 1) * SGU_CHUNK)
            for ch in range(tm // SGU_CHUNK):
                rows = slice(ch * SGU_CHUNK, (ch + 1) * SGU_CHUNK)
                f = _dot(wg, vn[rows, cols]) + b_ref[g]
                y_ref[rows, cols] = (_gelu(u_ref[rows, cols]) * f).astype(BF)

    full3 = pl.BlockSpec((SGU_GROUPS, SGU_CHUNK, SGU_CHUNK), lambda i: (0, 0, 0))
    return _call(body, name, _sds((s, w), BF), (s // tm,),
                 [_rows(tm, w), _rows(tm, w), _vec(w), _vec(w), full3, full3],
                 _rows(tm, w), ("parallel",))(u_pre, v_pre, ln_w, ln_b, wm, bfull)


def _sgu_bwd(name, u_pre, v_pre, ln_w, ln_b, wm, bfull, dy, tm=256, after=None):
    s, w = u_pre.shape
    tm = min(tm, s)
    steps = s // tm

    def body(u_ref, v_ref, lw_ref, lb_ref, wm_ref, b_ref, dy_ref, du_ref, dv_ref, dwm_ref, db_ref, dlw_ref, dlb_ref,
             dfsum_ref):
        i = pl.program_id(0)

        @pl.when(i == 0)
        def _():
            dwm_ref[...] = jnp.zeros_like(dwm_ref)
            dlw_ref[...] = jnp.zeros_like(dlw_ref)
            dlb_ref[...] = jnp.zeros_like(dlb_ref)
            dfsum_ref[...] = jnp.zeros_like(dfsum_ref)

        vpre = v_ref[...]
        vhat, rstd = _layer_norm_stats(_gelu(vpre))
        vn = (vhat * lw_ref[...] + lb_ref[...]).astype(BF)
        t = lax.broadcasted_iota(jnp.int32, (SGU_CHUNK, SGU_CHUNK), 0)
        sidx = lax.broadcasted_iota(jnp.int32, (SGU_CHUNK, SGU_CHUNK), 1)
        dvn_cols = []
        for g in range(SGU_GROUPS):
            wg = _sgu_weights(wm_ref, g)
            cols = slice(g * SGU_CHUNK, (g + 1) * SGU_CHUNK)
            dvn_rows = []
            dwg = jnp.zeros((SGU_CHUNK, SGU_CHUNK), F32)
            dfs = jnp.zeros((SGU_CHUNK, SGU_CHUNK), F32)
            for ch in range(tm // SGU_CHUNK):
                rows = slice(ch * SGU_CHUNK, (ch + 1) * SGU_CHUNK)
                upre = u_ref[rows, cols]
                dyv = dy_ref[rows, cols]
                f = _dot(wg, vn[rows, cols]) + b_ref[g]
                du_ref[rows, cols] = (dyv * f * _gelu_grad(upre)).astype(BF)
                df = dyv * _gelu(upre)
                dfb = df.astype(BF)
                dvn_rows.append(_dot_tn(wg, dfb))
                dwg = dwg + _dot_nt(dfb, vn[rows, cols])
                dfs = dfs + df
            dwm_ref[g] += jnp.where(sidx <= t, dwg, 0.0)
            dfsum_ref[g] += dfs
            dvn_cols.append(jnp.concatenate(dvn_rows, axis=0) if len(dvn_rows) > 1 else dvn_rows[0])
        dvn = jnp.concatenate(dvn_cols, axis=1)
        dlw_ref[...] += jnp.sum(dvn * vhat, axis=0, keepdims=True)
        dlb_ref[...] += jnp.sum(dvn, axis=0, keepdims=True)
        dvh = dvn * lw_ref[...]
        dvg = rstd * (dvh - jnp.mean(dvh, axis=-1, keepdims=True) - vhat * jnp.mean(dvh * vhat, axis=-1, keepdims=True))
        dv_ref[...] = (dvg * _gelu_grad(vpre)).astype(BF)

        @pl.when(i == steps - 1)
        def _():
            for g in range(SGU_GROUPS):
                db_ref[g:g + 1, :] = jnp.sum(dfsum_ref[g].T, axis=0, keepdims=True)

    full3 = pl.BlockSpec((SGU_GROUPS, SGU_CHUNK, SGU_CHUNK), lambda i: (0, 0, 0))
    return _call(body, name,
                 (_sds((s, w), BF), _sds((s, w), BF), _sds((SGU_GROUPS, SGU_CHUNK, SGU_CHUNK), F32),
                  _sds((SGU_GROUPS, SGU_CHUNK), F32), _sds((1, w), F32), _sds((1, w), F32)),
                 (steps,),
                 [_rows(tm, w), _rows(tm, w), _vec(w), _vec(w), full3, full3, _rows(tm, w)],
                 (_rows(tm, w), _rows(tm, w), full3, pl.BlockSpec((SGU_GROUPS, SGU_CHUNK), lambda i: (0, 0)), _vec(w), _vec(w)),
                 ("arbitrary",), scratch=[pltpu.VMEM((SGU_GROUPS, SGU_CHUNK, SGU_CHUNK), F32)], after=after)(
                     u_pre, v_pre, ln_w, ln_b, wm, bfull, dy)


def _mix_out(name, y_sgu, y_attn, ga_pre, gb_pre, x, g1, proj_a, proj_b, w_out, w2, sc2, sh2, tm=256):
    s, d = x.shape
    tm = min(tm, s)

    def body(ys_ref, ya_ref, ga_ref, gb_ref, x_ref, g1_ref, wa_ref, wb_ref, wo_ref, w2_ref, sc2_ref, sh2_ref,
             m_ref, pa_ref, pb_ref, o_ref, x1_ref, h2_ref):
        pa = _dot(ys_ref[...], wa_ref[...].reshape(d, d))
        pb = _dot(ya_ref[...], wb_ref[...].reshape(d, d))
        pa_ref[...] = pa
        pb_ref[...] = pb
        merged = (_sigmoid(ga_ref[...]) * pa + _sigmoid(gb_ref[...]) * pb).astype(BF)
        m_ref[...] = merged
        o = _dot(merged, wo_ref[...].reshape(d, d))
        o_ref[...] = o
        x1 = x_ref[...] + g1_ref[...] * o
        x1_ref[...] = x1
        h2_ref[...] = ((x1 * _rms(x1)) * w2_ref[...] * (1.0 + sc2_ref[...]) + sh2_ref[...]).astype(BF)

    f, b = _sds((s, d), F32), _sds((s, d), BF)
    r = _rows(tm, d)
    wspec = _resident(proj_a.shape)
    return _call(body, name, (b, f, f, f, f, b), (s // tm,),
                 [r, r, r, r, r, _vec(d), wspec, wspec, wspec, _vec(d), _vec(d), _vec(d)], (r,) * 6, ("parallel",))(
                     y_sgu, y_attn, ga_pre, gb_pre, x, g1, proj_a, proj_b, w_out, w2, sc2, sh2)


def _mix_bwd(name, do, w_out, proj_a, proj_b, ga_pre, gb_pre, pa, pb, tm=256):
    s, d = do.shape
    tm = min(tm, s)

    def body(do_ref, wo_ref, wa_ref, wb_ref, ga_ref, gb_ref, pa_ref, pb_ref,
             dpa_ref, dpb_ref, dga_ref, dgb_ref, dys_ref, dya_ref):
        dm = _dot_nt(do_ref[...], wo_ref[...].reshape(d, d))
        ga = _sigmoid(ga_ref[...])
        gb = _sigmoid(gb_ref[...])
        dpa = (dm * ga).astype(BF)
        dpb = (dm * gb).astype(BF)
        dpa_ref[...] = dpa
        dpb_ref[...] = dpb
        dga_ref[...] = (dm * pa_ref[...] * ga * (1.0 - ga)).astype(BF)
        dgb_ref[...] = (dm * pb_ref[...] * gb * (1.0 - gb)).astype(BF)
        dys_ref[...] = _dot_nt(dpa, wa_ref[...].reshape(d, d))
        dya_ref[...] = _dot_nt(dpb, wb_ref[...].reshape(d, d))

    f, b = _sds((s, d), F32), _sds((s, d), BF)
    r = _rows(tm, d)
    wspec = _resident(w_out.shape)
    return _call(body, name, (b, b, b, b, f, f), (s // tm,), [r, wspec, wspec, wspec, r, r, r, r], (r,) * 6,
                 ("parallel",))(do, w_out, proj_a, proj_b, ga_pre, gb_pre, pa, pb)


def _ffn_up(name, h2, w_gate, w_up, tm=1024):
    s, d = h2.shape
    tm = min(tm, s)
    tc = FFN_SHARD_PAD

    def body(h_ref, wg_ref, wu_ref, a_ref, up_ref):
        hv = h_ref[...]
        a_ref[...] = _dot_nt(hv, wg_ref[0])
        up_ref[...] = _dot_nt(hv, wu_ref[0])

    wspec = pl.BlockSpec((1, tc, d), lambda j, i: (j, 0, 0))
    ospec = pl.BlockSpec((tm, tc), lambda j, i: (i, j))
    o = _sds((s, FFN_PAD), F32)
    return _call(body, name, (o, o), (N_CHIPS, s // tm), [pl.BlockSpec((tm, d), lambda j, i: (i, 0)), wspec, wspec],
                 (ospec, ospec), ("parallel", "parallel"))(h2, w_gate, w_up)


def _conv_act(name, a, up, cw, cb, tm=512, tc=FFN_SHARD_PAD):
    s, c = a.shape
    tm = min(tm, s)
    per = tm // SUBLANES

    def body(a_ref, prev_ref, up_ref, cw_ref, cb_ref, h_ref):
        prev = jnp.where(pl.program_id(1) > 0, prev_ref[...], 0.0)
        ext = jnp.concatenate([prev, a_ref[...]], axis=0)
        ac = (cb_ref[...] + cw_ref[0:1, :] * pltpu.roll(ext, 2, axis=0) + cw_ref[1:2, :] * pltpu.roll(ext, 1, axis=0)
              + cw_ref[2:3, :] * ext)[SUBLANES:]
        h_ref[...] = (ac * _sigmoid(ac) * up_ref[...]).astype(BF)

    tile = pl.BlockSpec((tm, tc), lambda j, i: (i, j))
    prev = pl.BlockSpec((SUBLANES, tc), lambda j, i: (jnp.maximum(i * per - 1, 0), j))
    return _call(body, name, _sds((s, c), BF), (c // tc, s // tm),
                 [tile, prev, tile, pl.BlockSpec((3, tc), lambda j, i: (0, j)), pl.BlockSpec((1, tc), lambda j, i: (0, j))],
                 tile, ("parallel", "parallel"))(a, a, up, cw, cb)


def _conv_act_bwd(name, a, up, dhf, cw, cb, tm=512, tc=FFN_SHARD_PAD):
    s, c = a.shape
    tm = min(tm, s)
    per = tm // SUBLANES
    steps = s // tm
    last8 = s // SUBLANES - 1

    def body(a_ref, aprev_ref, anext_ref, up_ref, upnext_ref, dh_ref, dhnext_ref, cw_ref, cb_ref,
             da_ref, dup_ref, dcw_ref, dcb_ref):
        i = pl.program_id(1)

        @pl.when(i == 0)
        def _():
            dcw_ref[...] = jnp.zeros_like(dcw_ref)
            dcb_ref[...] = jnp.zeros_like(dcb_ref)

        prev = jnp.where(i > 0, aprev_ref[...], 0.0)
        ext = jnp.concatenate([prev, a_ref[...], anext_ref[...]], axis=0)
        a1 = pltpu.roll(ext, 1, axis=0)[SUBLANES:]
        a2 = pltpu.roll(ext, 2, axis=0)[SUBLANES:]
        a0 = ext[SUBLANES:]
        ac = cb_ref[...] + cw_ref[0:1, :] * a2 + cw_ref[1:2, :] * a1 + cw_ref[2:3, :] * a0
        sig = _sigmoid(ac)
        dh = jnp.concatenate([dh_ref[...], jnp.where(i < steps - 1, dhnext_ref[...], 0.0)], axis=0)
        upe = jnp.concatenate([up_ref[...], upnext_ref[...]], axis=0)
        dac = dh * upe * (sig * (1.0 + ac * (1.0 - sig)))
        dup_ref[...] = (dh[:tm] * (ac * sig)[:tm]).astype(BF)
        n = tm + SUBLANES
        da = (cw_ref[2:3, :] * dac + cw_ref[1:2, :] * pltpu.roll(dac, n - 1, axis=0)
              + cw_ref[0:1, :] * pltpu.roll(dac, n - 2, axis=0))
        da_ref[...] = da[:tm].astype(BF)
        dact = dac[:tm]
        dcb_ref[...] += jnp.sum(dact, axis=0, keepdims=True)
        dcw_ref[0:1, :] += jnp.sum(dact * a2[:tm], axis=0, keepdims=True)
        dcw_ref[1:2, :] += jnp.sum(dact * a1[:tm], axis=0, keepdims=True)
        dcw_ref[2:3, :] += jnp.sum(dact * a0[:tm], axis=0, keepdims=True)

    tile = pl.BlockSpec((tm, tc), lambda j, i: (i, j))
    prev = pl.BlockSpec((SUBLANES, tc), lambda j, i: (jnp.maximum(i * per - 1, 0), j))
    nxt = pl.BlockSpec((SUBLANES, tc), lambda j, i: (jnp.minimum((i + 1) * per, last8), j))
    cw_spec = pl.BlockSpec((3, tc), lambda j, i: (0, j))
    cb_spec = pl.BlockSpec((1, tc), lambda j, i: (0, j))
    return _call(body, name, (_sds((s, c), BF), _sds((s, c), BF), _sds((3, c), F32), _sds((1, c), F32)),
                 (c // tc, steps), [tile, prev, nxt, tile, nxt, tile, nxt, cw_spec, cb_spec],
                 (tile, tile, cw_spec, cb_spec), ("parallel", "arbitrary"))(a, a, a, up, up, dhf, dhf, cw, cb)


def _ffn_down(name, hf, w_down, x1, g2, tm=512):
    s, d = x1.shape
    tm = min(tm, s)

    def body(hf_ref, wd_ref, x1_ref, g2_ref, dn_ref, x2_ref):
        dn = _dot(hf_ref[...], wd_ref[...].reshape(FFN_PAD, d))
        dn_ref[...] = dn
        x2_ref[...] = x1_ref[...] + g2_ref[...] * dn

    o = _sds((s, d), F32)
    return _call(body, name, (o, o), (s // tm,),
                 [_rows(tm, FFN_PAD), _resident(w_down.shape), _rows(tm, d), _vec(d)],
                 (_rows(tm, d), _rows(tm, d)), ("parallel",))(hf, w_down, x1, g2)


def _ffn_down_bwd(name, dx2, dn, g2, w_down, tm=512, after=None):
    s, d = dx2.shape
    tm = min(tm, s)
    tc = FFN_SHARD_PAD

    def body(dx_ref, dn_ref, g2_ref, wd_ref, ddn_ref, dhf_ref, dg_ref):
        @pl.when(pl.program_id(0) == 0)
        def _():
            dg_ref[...] = jnp.zeros_like(dg_ref)

        dxv = dx_ref[...]
        ddn = (dxv * g2_ref[...]).astype(BF)
        ddn_ref[...] = ddn
        dg_ref[...] += jnp.sum(dxv * dn_ref[...], axis=0, keepdims=True)
        for k in range(N_CHIPS):
            dhf_ref[:, k * tc:(k + 1) * tc] = _dot_nt(ddn, wd_ref[k])

    return _call(body, name, (_sds((s, d), BF), _sds((s, FFN_PAD), F32), _sds((1, d), F32)), (s // tm,),
                 [_rows(tm, d), _rows(tm, d), _vec(d), _resident(w_down.shape)],
                 (_rows(tm, d), _rows(tm, FFN_PAD), _vec(d)), ("arbitrary",), after=after)(dx2, dn, g2, w_down)


def _ffn_up_bwd(name, da, dup, w_gate, w_up, x1, dx2, w2, sc2, o, g1, tm=256, after=None):
    s, d = x1.shape
    tm = min(tm, s)
    tc = FFN_SHARD_PAD

    def body(da_ref, dup_ref, wg_ref, wu_ref, x1_ref, dx2_ref, w2_ref, sc2_ref, o_ref, g1_ref,
             dx1_ref, do_ref, dnw_ref, dsh_ref, dg1_ref):
        @pl.when(pl.program_id(0) == 0)
        def _():
            dnw_ref[...] = jnp.zeros_like(dnw_ref)
            dsh_ref[...] = jnp.zeros_like(dsh_ref)
            dg1_ref[...] = jnp.zeros_like(dg1_ref)

        dh = jnp.zeros((tm, d), F32)
        for k in range(N_CHIPS):
            cols = slice(k * tc, (k + 1) * tc)
            dh = dh + _dot(da_ref[:, cols], wg_ref[k]) + _dot(dup_ref[:, cols], wu_ref[k])
        xv = x1_ref[...]
        r = _rms(xv)
        xn = xv * r
        dxn = dh * (w2_ref[...] * (1.0 + sc2_ref[...]))
        dx1 = dx2_ref[...] + r * (dxn - xn * jnp.mean(dxn * xn, axis=-1, keepdims=True))
        dx1_ref[...] = dx1
        dnw_ref[...] += jnp.sum(dh * xn, axis=0, keepdims=True)
        dsh_ref[...] += jnp.sum(dh, axis=0, keepdims=True)
        do_ref[...] = (dx1 * g1_ref[...]).astype(BF)
        dg1_ref[...] += jnp.sum(dx1 * o_ref[...], axis=0, keepdims=True)

    v = _sds((1, d), F32)
    r = _rows(tm, d)
    wspec = _resident(w_gate.shape)
    return _call(body, name, (_sds((s, d), F32), _sds((s, d), BF), v, v, v), (s // tm,),
                 [_rows(tm, FFN_PAD), _rows(tm, FFN_PAD), wspec, wspec, r, r, _vec(d), _vec(d), r, _vec(d)],
                 (r, r, _vec(d), _vec(d), _vec(d)), ("arbitrary",), after=after)(da, dup, w_gate, w_up, x1, dx2, w2, sc2, o, g1)


def _loss_head(name, x, w, target, tm=512):
    s, d = x.shape
    tm = min(tm, s)

    def body(x_ref, w_ref, t_ref, dx_ref, loss_ref, dw_ref):
        @pl.when(pl.program_id(0) == 0)
        def _():
            loss_ref[...] = jnp.zeros_like(loss_ref)
            dw_ref[...] = jnp.zeros_like(dw_ref)

        xv = x_ref[...]
        r = _rms(xv)
        xn = xv * r
        err = xn * w_ref[...] - t_ref[...]
        loss_ref[...] += 0.5 * jnp.sum(jnp.mean(err * err, axis=-1, keepdims=True))
        dy = err * (1.0 / d)
        dw_ref[...] += jnp.sum(dy * xn, axis=0, keepdims=True)
        dxn = dy * w_ref[...]
        dx_ref[...] = r * (dxn - xn * jnp.mean(dxn * xn, axis=-1, keepdims=True))

    return _call(body, name, (_sds((s, d), F32), _sds((1, LANES), F32), _sds((1, d), F32)), (s // tm,),
                 [_rows(tm, d), _vec(d), _rows(tm, d)], (_rows(tm, d), _vec(LANES), _vec(d)), ("arbitrary",))(x, w, target)


def _layer_fwd(l, x, mod, p, cosf, sinf, after=None, late=None):
    sh1, sc1, g1, sh2, sc2, g2 = mod
    tag = f"l{l}_"
    h, qr, kk0, kk1, vv0, vv1, u_pre, v_pre, ga_pre, gb_pre = _in_proj(
        tag + "in_proj", x, p["norm1_w"], sc1, sh1, p["w_in"], cosf, sinf, after=after)
    y_attn = _attention(tag + "attn", qr, kk0, kk1, vv0, vv1, p["sinks"])
    y_sgu = _sgu(tag + "sgu", u_pre, v_pre, p["sgu_ln_w"], p["sgu_ln_b"], p["sgu_w"], p["sgu_bfull"])
    if late is not None:
        p = dict(p, **late(y_sgu))
    merged, pa, pb, o, x1, h2 = _mix_out(tag + "mix_out", y_sgu, y_attn, ga_pre, gb_pre, x, g1, p["proj_a"], p["proj_b"],
                                         p["w_out"], p["norm2_w"], sc2, sh2)
    a, up = _ffn_up(tag + "ffn_up", h2, p["w_gate"], p["w_up"])
    hf = _conv_act(tag + "conv_act", a, up, p["conv_w"], p["conv_b"])
    dn, x2 = _ffn_down(tag + "ffn_down", hf, p["w_down"], x1, g2)
    saved = dict(x=x, h=h, qr=qr, kk0=kk0, kk1=kk1, vv0=vv0, vv1=vv1, u_pre=u_pre, v_pre=v_pre, ga_pre=ga_pre,
                 gb_pre=gb_pre, y_attn=y_attn, y_sgu=y_sgu, merged=merged, pa=pa, pb=pb, o=o, x1=x1, h2=h2, a=a, up=up,
                 hf=hf, dn=dn)
    return x2, saved, p


def _layer_bwd(l, dx2, mod, p, sv, cosf, sinf, after=None, emit=None):
    sh1, sc1, g1, sh2, sc2, g2 = mod
    tag = f"l{l}_b_"
    d = D_MODEL
    g = {}
    ready = (lambda names: emit({k: g.pop(k) for k in names})) if emit else (lambda names: None)
    ddn, dhf, dg2 = _ffn_down_bwd(tag + "ffn_down", dx2, sv["dn"], g2, p["w_down"], after=after)
    g["w_down"] = _matmul_tn(tag + "dw_down", sv["hf"], ddn, tk=FFN_SHARD_PAD).reshape(N_CHIPS, FFN_SHARD_PAD, d)
    da, dup, g["conv_w"], g["conv_b"] = _conv_act_bwd(tag + "conv_act", sv["a"], sv["up"], dhf, p["conv_w"], p["conv_b"])
    g["w_gate"] = _matmul_tn(tag + "dw_gate", da, sv["h2"], tk=FFN_SHARD_PAD).reshape(N_CHIPS, FFN_SHARD_PAD, d)
    g["w_up"] = _matmul_tn(tag + "dw_up", dup, sv["h2"], tk=FFN_SHARD_PAD).reshape(N_CHIPS, FFN_SHARD_PAD, d)
    dx1, do, da2, dsh2, dg1 = _ffn_up_bwd(tag + "ffn_up", da, dup, p["w_gate"], p["w_up"], sv["x1"], dx2, p["norm2_w"],
                                          sc2, sv["o"], g1, after=ready(("w_down", "w_gate", "w_up")))
    g["norm2_w"] = da2 * (1.0 + sc2)
    dsc2 = da2 * p["norm2_w"]
    g["w_out"] = _matmul_tn(tag + "dw_out", sv["merged"], do).reshape(N_CHIPS, d // N_CHIPS, d)
    dpa, dpb, dga, dgb, dy_sgu, dy_attn = _mix_bwd(tag + "mix", do, p["w_out"], p["proj_a"], p["proj_b"], sv["ga_pre"],
                                                  sv["gb_pre"], sv["pa"], sv["pb"])
    g["proj_a"] = _matmul_tn(tag + "dproj_a", sv["y_sgu"], dpa).reshape(N_CHIPS, d // N_CHIPS, d)
    g["proj_b"] = _matmul_tn(tag + "dproj_b", sv["y_attn"], dpb).reshape(N_CHIPS, d // N_CHIPS, d)
    du, dv, g["sgu_w"], g["sgu_b"], g["sgu_ln_w"], g["sgu_ln_b"] = _sgu_bwd(
        tag + "sgu", sv["u_pre"], sv["v_pre"], p["sgu_ln_w"], p["sgu_ln_b"], p["sgu_w"], p["sgu_bfull"], dy_sgu,
        after=ready(("w_out", "proj_a", "proj_b")))
    dqr, dkv_cur, dkv_prev, dsink = _attention_bwd(tag + "attn", sv["qr"], sv["kk0"], sv["kk1"], sv["vv0"], sv["vv1"],
                                                   p["sinks"], dy_attn)
    g["sinks"] = dsink[0, :N_Q_HEADS]
    dq, dkv = _rope_bwd(tag + "rope", dqr, dkv_cur, dkv_prev, cosf, sinf)
    dw_in, row0 = None, 0
    for n, t in (("q", dq), ("kv", dkv), ("u", du), ("v", dv), ("ga", dga), ("gb", dgb)):
        dw_in = _matmul_tn_rows(tag + "dw_in_" + n, t, sv["h"], dw_in, row0, IN_COLS)
        row0 += t.shape[1]
    g["w_in"] = dw_in.reshape(N_CHIPS, IN_COLS // N_CHIPS, d)
    dx, da1, dsh1 = _in_proj_bwd(tag + "in_proj", dq, dkv, du, dv, dga, dgb, p["w_in"], sv["x"], p["norm1_w"], sc1, dx1)
    g["norm1_w"] = da1 * (1.0 + sc1)
    dsc1 = da1 * p["norm1_w"]
    return dx, (dsh1, dsc1, dg1, dsh2, dsc2, dg2), g


def _pad_to(a, axis, size):
    pad = [(0, 0)] * a.ndim
    pad[axis] = (0, size - a.shape[axis])
    return jnp.pad(a, pad)


def _early_params(w_in, small):
    d = D_MODEL
    return dict(
        w_in=w_in.reshape(IN_COLS, d), norm1_w=small["norm1_w"].reshape(1, d), sinks=small["sinks"],
        sgu_ln_w=small["sgu_ln_w"].reshape(1, d), sgu_ln_b=small["sgu_ln_b"].reshape(1, d), sgu_w=small["sgu_w"],
        sgu_bfull=jnp.broadcast_to(small["sgu_b"][:, :, None], (SGU_GROUPS, SGU_CHUNK, SGU_CHUNK)))


def _late_params(proj_a, proj_b, w_out, w_gate, w_up, w_down, conv_w, small):
    d = D_MODEL
    return dict(
        proj_a=proj_a, proj_b=proj_b, w_out=w_out, w_gate=w_gate, w_up=w_up, w_down=w_down,
        conv_w=conv_w.transpose(1, 0, 2).reshape(3, FFN_PAD),
        conv_b=_pad_to(small["conv_b"].reshape(N_CHIPS, FFN_SHARD), 1, FFN_SHARD_PAD).reshape(1, FFN_PAD),
        norm2_w=small["norm2_w"].reshape(1, d))


def _layer_params(w_in, proj_a, proj_b, w_out, w_gate, w_up, w_down, conv_w, small):
    return dict(_early_params(w_in, small), **_late_params(proj_a, proj_b, w_out, w_gate, w_up, w_down, conv_w, small))


def _conv_grads_natural(g):
    cw = g["conv_w"].reshape(3, N_CHIPS, FFN_SHARD_PAD)[:, :, :FFN_SHARD].reshape(3, FFN_DIM)
    cb = g["conv_b"].reshape(N_CHIPS, FFN_SHARD_PAD)[:, :FFN_SHARD].reshape(FFN_DIM)
    return cw, cb


def _rope_tables(positions):
    inv_freq = ROPE_THETA ** (-jnp.arange(0, ROT_DIM, 2, dtype=F32) / ROT_DIM)
    ang = positions.astype(F32)[:, None] * inv_freq
    cos, sin = jnp.cos(ang), jnp.sin(ang)
    s = positions.shape[0]
    rest = HEAD_DIM - ROT_DIM
    cos_head = jnp.concatenate([cos, cos, jnp.ones((s, rest), F32)], axis=1)
    sin_head = jnp.concatenate([-sin, sin, jnp.zeros((s, rest), F32)], axis=1)
    return jnp.tile(cos_head, (1, LANES // HEAD_DIM)), jnp.tile(sin_head, (1, LANES // HEAD_DIM))


ADA_ROWS = 16


def _ada_fwd(name, c_rows, ada_w, ada_b_cols, tn=512):
    depth, d, n = ada_w.shape

    def body(c_ref, w_ref, b_ref, o_ref):
        cv = c_ref[...]
        act = (cv * _sigmoid(cv)).astype(BF)
        o_ref[0] = _dot(act, w_ref[0].astype(BF)) + b_ref[0]

    return _call(body, name, _sds((depth, ADA_ROWS, n), F32), (depth, n // tn),
                 [pl.BlockSpec((ADA_ROWS, d), lambda l, j: (0, 0)), pl.BlockSpec((1, d, tn), lambda l, j: (l, 0, j)),
                  pl.BlockSpec((1, 1, tn), lambda l, j: (l, 0, j))],
                 pl.BlockSpec((1, ADA_ROWS, tn), lambda l, j: (l, 0, j)), ("parallel", "parallel"))(c_rows, ada_w, ada_b_cols)


def _ada_bwd(name, c_rows, dmod_cols, tn=512):
    depth, _, n = dmod_cols.shape
    d = c_rows.shape[1]

    def body(c_ref, dm_ref, o_ref):
        cv = c_ref[...]
        act = (cv * _sigmoid(cv)).astype(BF)
        o_ref[0] = _dot_tn(act, dm_ref[0].astype(BF))

    return _call(body, name, _sds((depth, d, n), F32), (depth, n // tn),
                 [pl.BlockSpec((ADA_ROWS, d), lambda l, j: (0, 0)), pl.BlockSpec((1, ADA_ROWS, tn), lambda l, j: (l, 0, j))],
                 pl.BlockSpec((1, d, tn), lambda l, j: (l, 0, j)), ("parallel", "parallel"))(c_rows, dmod_cols)


def _colsum(name, a):
    r, n = a.shape

    def body(a_ref, o_ref):
        o_ref[...] = jnp.sum(a_ref[...], axis=0, keepdims=True)

    return _call(body, name, _sds((1, n), F32), (1,), [pl.BlockSpec((r, n), lambda i: (0, 0))],
                 pl.BlockSpec((1, n), lambda i: (0, 0)), ("arbitrary",))(a)


REL_SIBLING = (0, 0, 1)
REL_CHIPS = ((1, 0, 0), (0, 1, 0), (1, 1, 0))
REL_ALL = tuple((fx, fy, fc) for fx in (0, 1) for fy in (0, 1) for fc in (0, 1) if fx or fy or fc)


def _chip_of(dev):
    return 2 * dev[0] + dev[1]


def _dev_of(dev):
    return 4 * dev[0] + 2 * dev[1] + dev[2]


def _flip(dev, rel):
    return tuple(1 - m if f else m for m, f in zip(dev, rel))


def _exchange(name, arrays, n_out, stages, aliases=None):
    out_shapes, stages = stages[0], stages[1:]
    n_in = len(arrays)
    aliases = aliases or {}
    n_remote = sum(len(plan) for plan, _ in stages)
    n_local = sum(len(local) for _, local in stages)

    def at(ref, idx):
        return ref.at[idx] if len(idx) else ref

    def body(*refs):
        bufs = list(refs[:n_in + n_out])
        for i_in, i_out in aliases.items():
            bufs[i_in] = bufs[n_in + i_out]
        send_sems, recv_sems, local_sems = refs[n_in + n_out:]
        me = (lax.axis_index("x"), lax.axis_index("y"), lax.axis_index("c"))
        base_r = base_l = 0
        pending = []
        for plan, local in stages:
            def remote(k, entry, sender, receiver):
                rel, si, ssel, di, dsel = entry
                return pltpu.make_async_remote_copy(
                    src_ref=at(bufs[si], ssel(sender, receiver)), dst_ref=at(bufs[di], dsel(sender, receiver)),
                    send_sem=send_sems.at[k], recv_sem=recv_sems.at[k], device_id=_flip(me, rel), device_id_type=MESH)

            sends = [remote(base_r + k, e, me, _flip(me, e[0])) for k, e in enumerate(plan)]
            for cp in sends:
                cp.start()
            for k, (si, ssel, di, dsel) in enumerate(local):
                cp = pltpu.make_async_copy(at(bufs[si], ssel(me)), at(bufs[di], dsel(me)), local_sems.at[base_l + k])
                cp.start()
                pending.append(cp.wait)
            for k, e in enumerate(plan):
                remote(base_r + k, e, _flip(me, e[0]), me).wait_recv()
            pending += [cp.wait_send for cp in sends]
            base_r += len(plan)
            base_l += len(local)
        for wait in pending:
            wait()

    any_spec = pl.BlockSpec(memory_space=pl.ANY)
    return pl.pallas_call(
        body, name=name, out_shape=tuple(out_shapes), in_specs=[any_spec] * n_in, out_specs=tuple([any_spec] * n_out),
        input_output_aliases=dict(aliases),
        scratch_shapes=[pltpu.SemaphoreType.DMA((max(n_remote, 1),)), pltpu.SemaphoreType.DMA((max(n_remote, 1),)),
                        pltpu.SemaphoreType.DMA((max(n_local, 1),))])(*arrays)


HBM_SPEC = pl.BlockSpec(memory_space=pltpu.HBM)
SEM_SPEC = pl.BlockSpec(memory_space=pltpu.SEMAPHORE)


def _split_copies(bufs, plan, local, send_sems, recv_sems, local_sems):
    me = (lax.axis_index("x"), lax.axis_index("y"), lax.axis_index("c"))

    def at(ref, idx):
        return ref.at[idx] if len(idx) else ref

    def remote(k, sender, receiver):
        rel, si, ssel, di, dsel = plan[k]
        return pltpu.make_async_remote_copy(
            src_ref=at(bufs[si], ssel(sender, receiver)), dst_ref=at(bufs[di], dsel(sender, receiver)),
            send_sem=send_sems.at[k], recv_sem=recv_sems.at[k], device_id=_flip(me, rel), device_id_type=MESH)

    sends = [remote(k, me, _flip(me, plan[k][0])) for k in range(len(plan))]
    arrivals = [remote(k, _flip(me, plan[k][0]), me) for k in range(len(plan))]
    locs = [pltpu.make_async_copy(at(bufs[si], ssel(me)), at(bufs[di], dsel(me)), local_sems.at[k])
            for k, (si, ssel, di, dsel) in enumerate(local)]
    return sends, arrivals, locs


def _exchange_start(name, arrays, out_shapes, plan, local):
    n_in, n_out = len(arrays), len(out_shapes)

    n_buf = n_in + n_out

    def body(*refs):
        sems = refs[n_buf:n_buf + 3]
        bufs = refs[n_buf + 3:2 * n_buf + 3]
        token = refs[-1]
        sends, _, locs = _split_copies(bufs, plan, local, *sems)
        for cp in sends + locs:
            cp.start()
        token[...] = jnp.zeros_like(token)

    zones = [lax.empty(o.shape, o.dtype) for o in out_shapes]
    operands = [pltpu.with_memory_space_constraint(a, pltpu.HBM) for a in list(arrays) + zones]
    sem = lambda n: pltpu.SemaphoreType.DMA((max(n, 1),))
    out = pl.pallas_call(
        body, name=name,
        out_shape=(sem(len(plan)), sem(len(plan)), sem(len(local)), *[pltpu.HBM(a.shape, a.dtype) for a in operands],
                   _sds((SUBLANES, LANES), F32)),
        in_specs=[HBM_SPEC] * (n_in + n_out),
        out_specs=(SEM_SPEC, SEM_SPEC, SEM_SPEC, *[HBM_SPEC] * (n_in + n_out), pl.BlockSpec(memory_space=pltpu.VMEM)),
        input_output_aliases={i: 3 + i for i in range(n_in + n_out)},
        compiler_params=pltpu.CompilerParams(has_side_effects=pltpu.SideEffectType.DATAFLOW_SIDE_EFFECTING))(*operands)
    return out[:3], out[3:3 + n_in], out[3 + n_in:3 + n_in + n_out], out[-1]


def _exchange_wait(name, sems, thru, zones, plan, local, after):
    n_in, n_out = len(thru), len(zones)

    def body(*refs):
        bufs = refs[:n_in + n_out]
        sends, arrivals, locs = _split_copies(bufs, plan, local, *refs[n_in + n_out:n_in + n_out + 3])
        for cp in arrivals:
            cp.wait_recv()
        for cp in sends:
            cp.wait_send()
        for cp in locs:
            cp.wait()

    out = pl.pallas_call(
        body, name=name, out_shape=tuple(pltpu.HBM(a.shape, a.dtype) for a in list(thru) + list(zones)),
        in_specs=[HBM_SPEC] * (n_in + n_out) + [SEM_SPEC] * 3 + [pl.BlockSpec(memory_space=pl.ANY)],
        out_specs=tuple([HBM_SPEC] * (n_in + n_out)), input_output_aliases={i: i for i in range(n_in + n_out)},
        compiler_params=pltpu.CompilerParams(has_side_effects=pltpu.SideEffectType.DATAFLOW_SIDE_EFFECTING))(
            *thru, *zones, *sems, after)
    return out[n_in:]


def _whole(*_):
    return ()


def _half_rows(rows, core):
    return pl.ds(core * (rows // 2), rows // 2)


def _gather_weights_plan(shards):
    n = len(shards)
    dsts = [_sds((N_CHIPS,) + a.shape, a.dtype) for a in shards]
    fetch, forward = [], []
    for t, a in enumerate(shards):
        rows = a.shape[0]
        if rows % (2 * 16) == 0:
            fetch += [(rel, t, (lambda s_, r_, rows=rows: (_half_rows(rows, s_[2]),)), n + t,
                       (lambda s_, r_, rows=rows: (_chip_of(s_), _half_rows(rows, s_[2])))) for rel in REL_CHIPS]
            forward += [(REL_SIBLING, n + t, (lambda s_, r_, rows=rows, rel=rel: (_chip_of(_flip(s_, rel)), _half_rows(rows, s_[2]))),
                         n + t, (lambda s_, r_, rows=rows, rel=rel: (_chip_of(_flip(s_, rel)), _half_rows(rows, s_[2]))))
                        for rel in REL_CHIPS]
        else:
            fetch += [(rel, t, _whole, n + t, lambda s_, r_: (_chip_of(s_),)) for rel in REL_CHIPS]
    local = [(t, _whole, n + t, lambda me: (_chip_of(me),)) for t in range(n)]
    return dsts, fetch, local, forward


def _gather_weights_start(name, shards):
    dsts, fetch, local, forward = _gather_weights_plan(shards)
    sems, thru, zones, token = _exchange_start(name, shards, dsts, fetch, local)
    return (sems, thru, zones, fetch, local, forward), token


def _gather_weights_finish(name, pending, after):
    sems, thru, zones, fetch, local, forward = pending
    landed = _exchange_wait(name + "_wait", sems, thru, zones, fetch, local, after)
    n = len(landed)
    return _exchange(name + "_forward", landed, n, [[_sds(a.shape, a.dtype) for a in landed], (forward, [])],
                     aliases={t: t for t in range(n)})


def _gather_chips(name, arrays):
    n = len(arrays)
    dsts = [_sds((N_CHIPS,) + a.shape, a.dtype) for a in arrays]
    plan = [(rel, t, _whole, n + t, lambda s_, r_: (_chip_of(s_),)) for t in range(n) for rel in REL_CHIPS]
    local = [(t, _whole, n + t, lambda me: (_chip_of(me),)) for t in range(n)]
    return _exchange(name, arrays, n, [dsts, (plan, local)])


def _gather_all(name, a):
    plan = [(rel, 0, _whole, 1, lambda s_, r_: (_dev_of(s_),)) for rel in REL_ALL]
    local = [(0, _whole, 1, lambda me: (_dev_of(me),))]
    return _exchange(name, [a], 1, [[_sds((2 * N_CHIPS,) + a.shape, a.dtype)], (plan, local)])[0]


def _swap_halves(name, grads):
    n = len(grads)
    dsts = [_sds((g.shape[0], g.shape[1] // 2, g.shape[2]), g.dtype) for g in grads]
    plan = [(REL_SIBLING, t, (lambda s_, r_, rows=g.shape[1]: (pl.ds(0, N_CHIPS), _half_rows(rows, r_[2]))), n + t, _whole)
            for t, g in enumerate(grads)]
    return _exchange(name, grads, n, [dsts, (plan, [])])


def _scatter_chips_plan(sums):
    n = len(sums)
    dsts = [_sds(a.shape, a.dtype) for a in sums]
    plan = [(rel, t, lambda s_, r_: (_chip_of(r_),), n + t, lambda s_, r_: (_chip_of(s_),))
            for t in range(n) for rel in REL_CHIPS]
    local = [(t, lambda me: (_chip_of(me),), n + t, lambda me: (_chip_of(me),)) for t in range(n)]
    return dsts, plan, local


def _scatter_chips(name, sums):
    dsts, plan, local = _scatter_chips_plan(sums)
    return _exchange(name, sums, len(sums), [dsts, (plan, local)])


def _scatter_chips_start(name, sums):
    dsts, plan, local = _scatter_chips_plan(sums)
    sems, thru, zones, token = _exchange_start(name, sums, dsts, plan, local)
    return (sems, thru, zones, plan, local), token


def _scatter_chips_finish(name, pending, after):
    sems, thru, zones, plan, local = pending
    return _exchange_wait(name + "_wait", sems, thru, zones, plan, local, after)


def _swap_back(name, totals, layer):
    n = len(totals)
    dsts = [_sds(a.shape, a.dtype) for a in totals]
    plan = [(REL_SIBLING, n + t, (lambda s_, r_, rows=a.shape[1]: (layer, _half_rows(rows, s_[2]))),
             n + t, (lambda s_, r_, rows=a.shape[1]: (layer, _half_rows(rows, s_[2])))) for t, a in enumerate(totals)]
    return _exchange(name, totals, n, [dsts, (plan, [])], aliases={t: t for t in range(n)})


def _add_halves(name, g, recv, core):
    nch, half, c = recv.shape

    def body(core_ref, g_ref, r_ref, o_ref):
        o_ref[0] = (g_ref[0, 0].astype(F32) + r_ref[0].astype(F32)).astype(o_ref.dtype)

    spec = pltpu.PrefetchScalarGridSpec(
        num_scalar_prefetch=1, grid=(nch,),
        in_specs=[pl.BlockSpec((1, 1, half, c), lambda k, core_ref: (k, core_ref[0], 0, 0)),
                  pl.BlockSpec((1, half, c), lambda k, core_ref: (k, 0, 0))],
        out_specs=pl.BlockSpec((1, half, c), lambda k, core_ref: (k, 0, 0)))
    return pl.pallas_call(body, name=name, out_shape=_sds(recv.shape, recv.dtype), grid_spec=spec,
                          compiler_params=pltpu.CompilerParams(dimension_semantics=("parallel",),
                                                               vmem_limit_bytes=VMEM_LIMIT))(
                                                                   core, g.reshape(nch, 2, half, c), recv)


def _sum_chips(name, a, core, layer, total):
    nch, half, c = a.shape

    def body(core_ref, a_ref, *rest):
        o_ref = rest[-1]
        acc = a_ref[0].astype(F32)
        for k in range(1, nch):
            acc = acc + a_ref[k].astype(F32)
        o_ref[0, 0] = acc

    in_specs = [pl.BlockSpec((nch, half, c), lambda i, core_ref: (0, 0, 0))]
    args = [core, a]
    if total is not None:
        in_specs.append(pl.BlockSpec(memory_space=pl.ANY))
        args.append(total.reshape(DEPTH, 2, half, c))
    spec = pltpu.PrefetchScalarGridSpec(
        num_scalar_prefetch=1, grid=(1,), in_specs=in_specs,
        out_specs=pl.BlockSpec((1, 1, half, c), lambda i, core_ref: (layer, core_ref[0], 0, 0)))
    out = pl.pallas_call(body, name=name, out_shape=_sds((DEPTH, 2, half, c), F32), grid_spec=spec,
                         input_output_aliases={2: 0} if total is not None else {},
                         compiler_params=pltpu.CompilerParams(dimension_semantics=("arbitrary",),
                                                              vmem_limit_bytes=VMEM_LIMIT))(*args)
    return out.reshape(DEPTH, 2 * half, c)


def _adamw_update(w, g, m, v):
    mn = ADAM_B1 * m + (1.0 - ADAM_B1) * g
    vn = ADAM_B2 * v + (1.0 - ADAM_B2) * (g * g)
    m_hat = mn / (1.0 - ADAM_B1 ** ADAM_STEP)
    v_hat = vn / (1.0 - ADAM_B2 ** ADAM_STEP)
    return -ADAM_LR * (m_hat / (jnp.sqrt(v_hat) + ADAM_EPS) + ADAM_WD * w), mn, vn


def _adamw(name, w, g, m, v):
    depth, r, c = w.shape
    tr = next(t for t in (512, 448, 384, 352, 336, 256, 192, 128, 64, 32, 16, 8) if r % t == 0 and t * c <= ADAM_TILE_ELEMS)

    def body(w_ref, g_ref, m_ref, v_ref, go_ref, d_ref, mo_ref, vo_ref):
        gv = g_ref[...]
        go_ref[...] = gv
        d_ref[...], mo_ref[...], vo_ref[...] = _adamw_update(w_ref[...], gv, m_ref[...], v_ref[...])

    spec = pl.BlockSpec((1, tr, c), lambda l, i: (l, i, 0))
    o = _sds(w.shape, F32)
    return _call(body, name, (o, o, o, o), (depth, r // tr), [spec] * 4, (spec,) * 4, ("parallel", "parallel"))(w, g, m, v)


def _adamw_small(name, ws, gs, ms, vs):
    n = len(ws)

    def body(*refs):
        for t in range(n):
            w_ref, g_ref, m_ref, v_ref = (refs[k * n + t] for k in range(4))
            d_ref, mo_ref, vo_ref = (refs[(4 + k) * n + t] for k in range(3))
            d_ref[...], mo_ref[...], vo_ref[...] = _adamw_update(w_ref[...], g_ref[...], m_ref[...], v_ref[...])

    outs = [_sds(w.shape, F32) for w in ws]
    res = pl.pallas_call(body, name=name, out_shape=tuple(outs * 3))(*ws, *gs, *ms, *vs)
    return res[:n], res[n:2 * n], res[2 * n:]


def _pack(arrays, rows):
    flat = jnp.concatenate([a.reshape(-1).astype(F32) for a in arrays])
    return _pad_to(flat, 0, rows * LANES).reshape(rows, LANES)


def _unpack(packed, shapes):
    flat = packed.reshape(-1)
    out, off = [], 0
    for shp in shapes:
        n = 1
        for s_ in shp:
            n *= s_
        out.append(flat[off:off + n].reshape(shp))
        off += n
    return out


_MATRICES = ("w_in", "proj_a", "proj_b", "w_out", "w_gate", "w_up", "w_down")
_SMALL = (("norm1_w", (D_MODEL,)), ("sinks", (N_Q_HEADS,)), ("sgu_ln_w", (SGU_WIDTH,)), ("sgu_ln_b", (SGU_WIDTH,)),
          ("sgu_w", (SGU_GROUPS, SGU_CHUNK, SGU_CHUNK)), ("sgu_b", (SGU_GROUPS, SGU_CHUNK)), ("norm2_w", (D_MODEL,)),
          ("conv_w", (3, FFN_DIM)), ("conv_b", (FFN_DIM,)), ("final_norm_w", (D_MODEL,)))
SMALL_ROWS = 320
ADAM_TILE_ELEMS = 384 * 1024


def _reduce_cores(tag, partial, core):
    names = list(partial)
    from_sibling = _swap_halves(tag + "_cores", [partial[k] for k in names])
    return names, [_add_halves(f"{tag}_cores_add_{k}", partial[k], r, core) for k, r in zip(names, from_sibling)]


def _reduce_finish(tag, l, names, from_chips, core, totals):
    sums = [_sum_chips(f"{tag}_chips_add_{k}", a, core, l, totals.get(k)) for k, a in zip(names, from_chips)]
    return dict(totals, **dict(zip(names, _swap_back(tag + "_back", sums, l))))


def kernel(x, c, positions, ada_w, ada_b, norm1_w, w_in, attn_sinks, sgu_ln_w, sgu_ln_b, sgu_w, sgu_b, proj_a, proj_b, w_out, norm2_w, ffn_w_gate, ffn_w_up, ffn_conv_w, ffn_conv_b, ffn_w_down, final_norm_w, loss_target, m_ada_w, m_ada_b, m_norm1_w, m_w_in, m_attn_sinks, m_sgu_ln_w, m_sgu_ln_b, m_sgu_w, m_sgu_b, m_proj_a, m_proj_b, m_w_out, m_norm2_w, m_ffn_w_gate, m_ffn_w_up, m_ffn_conv_w, m_ffn_conv_b, m_ffn_w_down, m_final_norm_w, v_ada_w, v_ada_b, v_norm1_w, v_w_in, v_attn_sinks, v_sgu_ln_w, v_sgu_ln_b, v_sgu_w, v_sgu_b, v_proj_a, v_proj_b, v_w_out, v_norm2_w, v_ffn_w_gate, v_ffn_w_up, v_ffn_conv_w, v_ffn_conv_b, v_ffn_w_down, v_final_norm_w):
    d = D_MODEL
    ax, ay, ac = lax.axis_index("x"), lax.axis_index("y"), lax.axis_index("c")
    chip = 2 * ax + ay
    dev = 4 * ax + 2 * ay + ac
    core = ac.astype(jnp.int32).reshape(1)

    c_all = _gather_all("gather_cond", c.reshape(SUBLANES, d // SUBLANES)).reshape(2 * N_CHIPS, d)
    c_rows = _pad_to(c_all, 0, ADA_ROWS)
    ada_cols = ada_w.shape[2]
    ada_b_cols = lax.dynamic_slice_in_dim(ada_b, chip * ada_cols, ada_cols, axis=1).reshape(DEPTH, 1, ada_cols)
    mod_cols = _ada_fwd("ada_fwd", c_rows, ada_w, ada_b_cols)
    mod_all = _gather_chips("gather_mod", [mod_cols])[0]
    mod_mine = lax.dynamic_index_in_dim(mod_all, dev, axis=2, keepdims=False)
    mod_mine = mod_mine.transpose(1, 0, 2).reshape(DEPTH, 1, 6 * d)
    mods = [tuple(jnp.split(mod_mine[l], 6, axis=-1)) for l in range(DEPTH)]

    tr = lambda a: jnp.swapaxes(a, 1, 2)
    behind = mod_all[0, 0, 0, 0] * 0.0
    shards = [tr(w_in).astype(BF) + behind.astype(BF), proj_a.astype(BF), proj_b.astype(BF), w_out.astype(BF),
              _pad_to(tr(ffn_w_gate).astype(BF), 1, FFN_SHARD_PAD), _pad_to(tr(ffn_w_up).astype(BF), 1, FFN_SHARD_PAD),
              _pad_to(ffn_w_down.astype(BF), 1, FFN_SHARD_PAD), _pad_to(ffn_conv_w, 2, FFN_SHARD_PAD)]
    first, token = _gather_weights_start("l0_gather_in", [shards[0][0]])
    rest0, token = _gather_weights_start("l0_gather_rest", [a[0] for a in shards[1:-1]] + [shards[-1][0] + token[0, 0]])
    layer1, token = _gather_weights_start("l1_gather", [a[1] for a in shards[:-1]] + [shards[-1][1] + token[0, 0]])

    small_in = dict(norm1_w=norm1_w, sinks=attn_sinks, sgu_ln_w=sgu_ln_w, sgu_ln_b=sgu_ln_b, sgu_w=sgu_w, sgu_b=sgu_b,
                    norm2_w=norm2_w, conv_b=ffn_conv_b)
    cosf, sinf = _rope_tables(positions[0])
    small_of = lambda l: {k: v[l] for k, v in small_in.items()}

    w_in0 = _gather_weights_finish("l0_gather_in", first, token)
    late = lambda y: _late_params(*_gather_weights_finish("l0_gather_rest", rest0, y), small_of(0))
    h, sv, p0 = _layer_fwd(0, x[0], mods[0], _early_params(w_in0[0], small_of(0)), cosf, sinf, late=late)
    saved, params = [sv], [p0]
    p1 = _layer_params(*_gather_weights_finish("l1_gather", layer1, h), small_of(1))
    h, sv, p1 = _layer_fwd(1, h, mods[1], p1, cosf, sinf)
    saved.append(sv)
    params.append(p1)
    dx, loss_part, d_final = _loss_head("loss_head", h, final_norm_w.reshape(1, d), loss_target[0])
    loss = lax.psum(loss_part[0, 0], ("x", "y", "c"))

    def small_pack(l, grads):
        cw, cb = _conv_grads_natural(grads)
        nat = dict(grads, conv_w=cw, conv_b=cb, final_norm_w=d_final if l == DEPTH - 1 else jnp.zeros((d,), F32))
        return _pack([nat[k] for k, _ in _SMALL], N_CHIPS * SMALL_ROWS).reshape(N_CHIPS, SMALL_ROWS, LANES)

    inflight = []

    def send(tag, l, partial):
        names, sums = _reduce_cores(tag, partial, core)
        pending, token = _scatter_chips_start(tag + "_chips", sums)
        inflight.append((tag, l, names, pending))
        return token

    dmods = [None] * DEPTH
    dx, dmods[1], grads = _layer_bwd(1, dx, mods[1], params[1], saved[1], cosf, sinf)
    token = send("l1_reduce", 1, dict({k: grads[k] for k in _MATRICES}, small=small_pack(1, grads)))
    dx, dmods[0], grads = _layer_bwd(0, dx, mods[0], params[0], saved[0], cosf, sinf, after=token,
                                     emit=lambda part: send("l0_reduce_" + "_".join(part), 0, part))
    send("l0_reduce_in", 0, dict(w_in=grads["w_in"], small=small_pack(0, grads)))
    totals = {}
    for tag, l, names, pending in inflight[:-1]:
        totals = _reduce_finish(tag, l, names, _scatter_chips_finish(tag + "_chips", pending, dx), core, totals)

    dmod_mine = jnp.concatenate([jnp.concatenate(dmods[l], axis=1) for l in range(DEPTH)], axis=1)
    dmod_all = _gather_all("gather_dmod", dmod_mine.reshape(SUBLANES, -1)).reshape(2 * N_CHIPS, DEPTH * 6 * d)
    g_ada_b = _colsum("ada_b_grad", dmod_all).reshape(DEPTH, 6 * d)
    dmod_cols = jnp.stack([lax.dynamic_slice_in_dim(dmod_all, l * 6 * d + chip * ada_cols, ada_cols, axis=1)
                           for l in range(DEPTH)])
    g_ada_w = _ada_bwd("ada_w_grad", c_rows, _pad_to(dmod_cols, 1, ADA_ROWS))
    big = dict(w_in=(tr(w_in), tr(m_w_in), tr(v_w_in)), proj_a=(proj_a, m_proj_a, v_proj_a), proj_b=(proj_b, m_proj_b, v_proj_b),
               w_out=(w_out, m_w_out, v_w_out), w_gate=(tr(ffn_w_gate), tr(m_ffn_w_gate), tr(v_ffn_w_gate)),
               w_up=(tr(ffn_w_up), tr(m_ffn_w_up), tr(v_ffn_w_up)), w_down=(ffn_w_down, m_ffn_w_down, v_ffn_w_down))
    upd, g_big = {}, {}

    def update(k):
        res = _adamw("adamw_" + k, big[k][0], totals[k], *big[k][1:])
        res = [tr(a) for a in res] if k in ("w_in", "w_gate", "w_up") else res
        g_big[k], upd[k] = res[0], res[1:]

    for k in ("w_down", "w_gate", "w_up", "w_out", "proj_a", "proj_b"):
        update(k)
    g_big["ada_w"], *upd["ada_w"] = _adamw("adamw_ada_w", ada_w, g_ada_w, m_ada_w, v_ada_w)
    tag, l, names, pending = inflight[-1]
    totals = _reduce_finish(tag, l, names, _scatter_chips_finish(tag + "_chips", pending, upd["ada_w"][0]), core, totals)
    update("w_in")

    small_all = _gather_chips("gather_small", [totals["small"]])[0]
    small_g = small_all.transpose(1, 0, 2, 3).reshape(DEPTH, -1)
    per_layer = [_unpack(small_g[l], [shp for _, shp in _SMALL]) for l in range(DEPTH)]
    sg = {k: jnp.stack([per_layer[l][i] for l in range(DEPTH)]) for i, (k, _) in enumerate(_SMALL)}
    g_final = sg["final_norm_w"][DEPTH - 1]
    g_conv_w = lax.dynamic_slice_in_dim(sg["conv_w"], chip * FFN_SHARD, FFN_SHARD, axis=2)

    rest = [("ada_b", ada_b, g_ada_b, m_ada_b, v_ada_b), ("norm1_w", norm1_w, sg["norm1_w"], m_norm1_w, v_norm1_w),
            ("attn_sinks", attn_sinks, sg["sinks"], m_attn_sinks, v_attn_sinks),
            ("sgu_ln_w", sgu_ln_w, sg["sgu_ln_w"], m_sgu_ln_w, v_sgu_ln_w),
            ("sgu_ln_b", sgu_ln_b, sg["sgu_ln_b"], m_sgu_ln_b, v_sgu_ln_b), ("sgu_w", sgu_w, sg["sgu_w"], m_sgu_w, v_sgu_w),
            ("sgu_b", sgu_b, sg["sgu_b"], m_sgu_b, v_sgu_b), ("norm2_w", norm2_w, sg["norm2_w"], m_norm2_w, v_norm2_w),
            ("ffn_conv_w", ffn_conv_w, g_conv_w, m_ffn_conv_w, v_ffn_conv_w),
            ("ffn_conv_b", ffn_conv_b, sg["conv_b"], m_ffn_conv_b, v_ffn_conv_b),
            ("final_norm_w", final_norm_w.reshape(1, d), g_final.reshape(1, d), m_final_norm_w.reshape(1, d),
             v_final_norm_w.reshape(1, d))]
    rest_out = _adamw_small("adamw_rest", *[[r[i] for r in rest] for i in (1, 2, 3, 4)])
    g_rest = {r[0]: r[2] for r in rest}
    u_rest = {r[0]: tuple(o[i] for o in rest_out) for i, r in enumerate(rest)}
    g_rest["final_norm_w"] = g_final
    u_rest["final_norm_w"] = tuple(a.reshape(d) for a in u_rest["final_norm_w"])

    names = ("ada_w", "ada_b", "norm1_w", "w_in", "attn_sinks", "sgu_ln_w", "sgu_ln_b", "sgu_w", "sgu_b", "proj_a", "proj_b",
             "w_out", "norm2_w", "ffn_w_gate", "ffn_w_up", "ffn_conv_w", "ffn_conv_b", "ffn_w_down", "final_norm_w")
    alias = {"ffn_w_gate": "w_gate", "ffn_w_up": "w_up", "ffn_w_down": "w_down"}
    grad_of = lambda n: g_rest[n] if n in g_rest else g_big[alias.get(n, n)]
    upd_of = lambda n: u_rest[n] if n in u_rest else upd[alias.get(n, n)]
    return (loss, dx[None], *[grad_of(n) for n in names], *[upd_of(n)[0] for n in names],
            *[upd_of(n)[1] for n in names], *[upd_of(n)[2] for n in names])
```

```python
import jax
import jax.numpy as jnp
from jax import lax
from jax.experimental import pallas as pl
from jax.experimental.pallas import tpu as pltpu

F32 = jnp.float32
BF = jnp.bfloat16

D_MODEL = 1024
N_Q_HEADS = 16
N_KV_HEADS = 2
HEAD_DIM = 64
ATTN_BLOCK = 128
ROPE_THETA = 500000.0
ROT_DIM = HEAD_DIM // 4
SGU_WIDTH = 1024
SGU_GROUPS = 8
SGU_CHUNK = 128
FFN_DIM = 2816
NORM_EPS = 1e-6
DEPTH = 2
IN_COLS = 5376
N_CHIPS = 4
FFN_SHARD = FFN_DIM // N_CHIPS
FFN_SHARD_PAD = 768
FFN_PAD = N_CHIPS * FFN_SHARD_PAD
LANES = 128
SUBLANES = 8
HALO = 16
VMEM_LIMIT = 56 * 1024 * 1024
NEG_BIG = -1e30

ADAM_LR = 0.001
ADAM_B1 = 0.9
ADAM_B2 = 0.999
ADAM_EPS = 1e-08
ADAM_WD = 0.01
ADAM_STEP = 10

MESH = pl.DeviceIdType.MESH

Q_END = 1024
KV_END = 1280
U_END = 2304
Z_END = 3328
GA_END = 4352


def _sds(shape, dtype):
    return jax.ShapeDtypeStruct(tuple(shape), dtype)


def _call(body, name, out_shape, grid, in_specs, out_specs, semantics, scratch=(), after=None):
    n_in = len(in_specs)
    fn = body
    if after is not None:
        def fn(*refs):
            return body(*refs[:n_in], *refs[n_in + 1:])

        in_specs = list(in_specs) + [pl.BlockSpec(memory_space=pl.ANY)]
    call = pl.pallas_call(
        fn, name=name, out_shape=out_shape, grid=grid, in_specs=in_specs, out_specs=out_specs,
        scratch_shapes=scratch,
        compiler_params=pltpu.CompilerParams(dimension_semantics=semantics, vmem_limit_bytes=VMEM_LIMIT))
    if after is None:
        return call
    return lambda *args: call(*args, after)


def _rows(tm, width, col=0):
    return pl.BlockSpec((tm, width), lambda i: (i, col))


def _vec(width):
    return pl.BlockSpec((1, width), lambda i: (0, 0))


def _resident(shape):
    zeros = (0,) * len(shape)
    return pl.BlockSpec(tuple(shape), lambda *_: zeros, pipeline_mode=pl.Buffered(1))


def _sigmoid(x):
    return 1.0 / (1.0 + jnp.exp(-x))


def _gelu(x):
    return 0.5 * x * (1.0 + lax.erf(x * 0.7071067811865476))


def _gelu_grad(x):
    cdf = 0.5 * (1.0 + lax.erf(x * 0.7071067811865476))
    return cdf + x * jnp.exp(-0.5 * x * x) * 0.3989422804014327


def _dot(a, b):
    return jnp.dot(a, b, preferred_element_type=F32)


def _dot_nt(a, b):
    return lax.dot_general(a, b, (((1,), (1,)), ((), ())), preferred_element_type=F32)


def _dot_tn(a, b):
    return lax.dot_general(a, b, (((0,), (0,)), ((), ())), preferred_element_type=F32)


def _rms(xv):
    return lax.rsqrt(jnp.mean(xv * xv, axis=-1, keepdims=True) + NORM_EPS)


def _matmul_tn(name, a, b, tk=512, tn=1024, blocked=False):
    s, k = a.shape
    n = b.shape[1]
    tk, tn = min(tk, k), min(tn, n)

    def body(a_ref, b_ref, o_ref):
        res = _dot_tn(a_ref[...], b_ref[...]).astype(o_ref.dtype)
        if blocked:
            o_ref[0] = res
        else:
            o_ref[...] = res

    if blocked:
        out, ospec = _sds((n // tn, k, tn), BF), pl.BlockSpec((1, tk, tn), lambda i, j: (j, i, 0))
    else:
        out, ospec = _sds((k, n), BF), pl.BlockSpec((tk, tn), lambda i, j: (i, j))
    return _call(body, name, out, (k // tk, n // tn),
                 [pl.BlockSpec((s, tk), lambda i, j: (0, i)), pl.BlockSpec((s, tn), lambda i, j: (0, j))],
                 ospec, ("parallel", "parallel"))(a, b)


def _matmul_tn_rows(name, a, b, out, row0, rows_total, tk=256):
    s, k = a.shape
    n = b.shape[1]

    def body(a_ref, b_ref, *rest):
        rest[-1][...] = _dot_tn(a_ref[...], b_ref[...]).astype(BF)

    in_specs = [pl.BlockSpec((s, tk), lambda i: (0, i)), _resident(b.shape)]
    args = [a, b]
    if out is not None:
        in_specs.append(pl.BlockSpec(memory_space=pl.ANY))
        args.append(out)
    return pl.pallas_call(
        body, name=name, out_shape=_sds((rows_total, n), BF), grid=(k // tk,), in_specs=in_specs,
        out_specs=pl.BlockSpec((tk, n), lambda i: (row0 // tk + i, 0)),
        input_output_aliases={2: 0} if out is not None else {},
        compiler_params=pltpu.CompilerParams(dimension_semantics=("parallel",), vmem_limit_bytes=VMEM_LIMIT))(*args)


def _rope_partner(v):
    lane = lax.broadcasted_iota(jnp.int32, (1, LANES), 1) % HEAD_DIM
    return jnp.where(lane < ROT_DIM // 2, pltpu.roll(v, LANES - ROT_DIM // 2, axis=1), pltpu.roll(v, ROT_DIM // 2, axis=1))


def _dup_half(v, half):
    lane = lax.broadcasted_iota(jnp.int32, (1, LANES), 1)
    keep = jnp.where((lane >= HEAD_DIM) == (half == 1), v, 0.0)
    return keep + pltpu.roll(keep, HEAD_DIM, axis=1)


def _in_proj(name, x, w, sc, sh, w_in, cosf, sinf, tm=256, after=None):
    s, d = x.shape
    tm = min(tm, s)

    def body(x_ref, w_ref, sc_ref, sh_ref, win_ref, cos_ref, sin_ref,
             h_ref, qr_ref, kk0_ref, kk1_ref, vv0_ref, vv1_ref, u_ref, v_ref, ga_ref, gb_ref):
        xv = x_ref[...]
        h = ((xv * _rms(xv)) * w_ref[...] * (1.0 + sc_ref[...]) + sh_ref[...]).astype(BF)
        h_ref[...] = h
        cosv, sinv = cos_ref[...], sin_ref[...]
        q = _dot_nt(h, win_ref[:Q_END, :])
        for j in range(D_MODEL // LANES):
            qv = q[:, j * LANES:(j + 1) * LANES]
            qr_ref[:, j * LANES:(j + 1) * LANES] = (qv * cosv + _rope_partner(qv) * sinv).astype(BF)
        kv = _dot_nt(h, win_ref[Q_END:KV_END, :])
        kr = kv[:, :LANES] * cosv + _rope_partner(kv[:, :LANES]) * sinv
        vv = kv[:, LANES:]
        kk0_ref[...] = _dup_half(kr, 0).astype(BF)
        kk1_ref[...] = _dup_half(kr, 1).astype(BF)
        vv0_ref[...] = _dup_half(vv, 0).astype(BF)
        vv1_ref[...] = _dup_half(vv, 1).astype(BF)
        u_ref[...] = _dot_nt(h, win_ref[KV_END:U_END, :])
        v_ref[...] = _dot_nt(h, win_ref[U_END:Z_END, :])
        ga_ref[...] = _dot_nt(h, win_ref[Z_END:GA_END, :]).astype(BF)
        gb_ref[...] = _dot_nt(h, win_ref[GA_END:, :]).astype(BF)

    wide, kvs, pre = _sds((s, d), BF), _sds((s, LANES), BF), _sds((s, d), F32)
    return _call(body, name, (wide, wide, kvs, kvs, kvs, kvs, pre, pre, wide, wide), (s // tm,),
                 [_rows(tm, d), _vec(d), _vec(d), _vec(d), _resident(w_in.shape), _rows(tm, LANES), _rows(tm, LANES)],
                 (_rows(tm, d), _rows(tm, d)) + (_rows(tm, LANES),) * 4 + (_rows(tm, d),) * 4, ("parallel",), after=after)(
                     x, w, sc, sh, w_in, cosf, sinf)


def _in_proj_bwd(name, dq, dkv, du, dv, dga, dgb, w_in, x, w, sc, dx_in, tm=256):
    s, d = x.shape
    tm = min(tm, s)

    def body(dq_ref, dkv_ref, du_ref, dv_ref, dga_ref, dgb_ref, win_ref, x_ref, w_ref, sc_ref, dxin_ref,
             dx_ref, da_ref, dsh_ref):
        @pl.when(pl.program_id(0) == 0)
        def _():
            da_ref[...] = jnp.zeros_like(da_ref)
            dsh_ref[...] = jnp.zeros_like(dsh_ref)

        dh = (_dot(dq_ref[...], win_ref[:Q_END, :]) + _dot(dkv_ref[...], win_ref[Q_END:KV_END, :])
              + _dot(du_ref[...], win_ref[KV_END:U_END, :]) + _dot(dv_ref[...], win_ref[U_END:Z_END, :])
              + _dot(dga_ref[...], win_ref[Z_END:GA_END, :]) + _dot(dgb_ref[...], win_ref[GA_END:, :]))
        xv = x_ref[...]
        r = _rms(xv)
        xn = xv * r
        dxn = dh * (w_ref[...] * (1.0 + sc_ref[...]))
        dx_ref[...] = dxin_ref[...] + r * (dxn - xn * jnp.mean(dxn * xn, axis=-1, keepdims=True))
        da_ref[...] += jnp.sum(dh * xn, axis=0, keepdims=True)
        dsh_ref[...] += jnp.sum(dh, axis=0, keepdims=True)

    return _call(body, name, (_sds((s, d), F32), _sds((1, d), F32), _sds((1, d), F32)), (s // tm,),
                 [_rows(tm, d), _rows(tm, 2 * LANES), _rows(tm, d), _rows(tm, d), _rows(tm, d), _rows(tm, d),
                  _resident(w_in.shape), _rows(tm, d), _vec(d), _vec(d), _rows(tm, d)],
                 (_rows(tm, d), _vec(d), _vec(d)), ("arbitrary",))(dq, dkv, du, dv, dga, dgb, w_in, x, w, sc, dx_in)


def _rope_bwd(name, dqr, dkv_cur, dkv_prev, cosf, sinf, tm=512):
    s = dqr.shape[0]
    tm = min(tm, s)
    steps = s // tm
    per = tm // ATTN_BLOCK
    nb = s // ATTN_BLOCK

    def unrope(v, cosv, sinv):
        return v * cosv - _rope_partner(v) * sinv

    def body(dq_ref, cur_ref, prev_ref, next_ref, cos_ref, sin_ref, dqo_ref, dkvo_ref):
        i = pl.program_id(0)
        cosv, sinv = cos_ref[...], sin_ref[...]
        for j in range(D_MODEL // LANES):
            dqo_ref[:, j * LANES:(j + 1) * LANES] = unrope(dq_ref[:, j * LANES:(j + 1) * LANES], cosv, sinv).astype(BF)
        nxt = jnp.where(i < steps - 1, next_ref[...], 0.0)
        if per > 1:
            shifted = jnp.concatenate([prev_ref[ATTN_BLOCK:, :], nxt], axis=0)
        else:
            shifted = nxt
        tot = cur_ref[...] + shifted
        dkvo_ref[:, :LANES] = unrope(tot[:, :LANES], cosv, sinv).astype(BF)
        dkvo_ref[:, LANES:] = tot[:, LANES:].astype(BF)

    nxt_spec = pl.BlockSpec((ATTN_BLOCK, 2 * LANES), lambda i: (jnp.minimum((i + 1) * per, nb - 1), 0))
    return _call(body, name, (_sds((s, D_MODEL), BF), _sds((s, 2 * LANES), BF)), (steps,),
                 [_rows(tm, D_MODEL), _rows(tm, 2 * LANES), _rows(tm, 2 * LANES), nxt_spec, _rows(tm, LANES),
                  _rows(tm, LANES)],
                 (_rows(tm, D_MODEL), _rows(tm, 2 * LANES)), ("parallel",))(dqr, dkv_cur, dkv_prev, dkv_prev, cosf, sinf)


Q_PER_KV = N_Q_HEADS // N_KV_HEADS


def _band_mask_t(n):
    kj = lax.broadcasted_iota(jnp.int32, (2 * ATTN_BLOCK, ATTN_BLOCK), 0)
    qi = lax.broadcasted_iota(jnp.int32, (2 * ATTN_BLOCK, ATTN_BLOCK), 1)
    return (kj > qi) & (kj <= qi + ATTN_BLOCK) & ((n > 0) | (kj >= ATTN_BLOCK))


def _softmax_t(raw, allowed, sink):
    sc = jnp.where(allowed, raw * (HEAD_DIM ** -0.5), NEG_BIG)
    m = jnp.maximum(jnp.max(sc, axis=0, keepdims=True), sink)
    p = jnp.exp(sc - m)
    esink = jnp.exp(sink - m)
    inv = 1.0 / (jnp.sum(p, axis=0, keepdims=True) + esink)
    return p * inv, esink * inv


def _kv_specs():
    cur = pl.BlockSpec((ATTN_BLOCK, LANES), lambda n: (n, 0))
    prev = pl.BlockSpec((ATTN_BLOCK, LANES), lambda n: (jnp.maximum(n - 1, 0), 0))
    return [prev, cur] * 4


def _attention(name, qr, kk0, kk1, vv0, vv1, sinks):
    s = qr.shape[0]
    nb = s // ATTN_BLOCK

    def body(sink_ref, q_ref, k0p, k0c, k1p, k1c, v0p, v0c, v1p, v1c, y_ref):
        allowed = _band_mask_t(pl.program_id(0))
        upper = lax.broadcasted_iota(jnp.int32, (1, LANES), 1) >= HEAD_DIM
        upper_rows = lax.broadcasted_iota(jnp.int32, (LANES, 1), 0) >= HEAD_DIM
        bands = ((jnp.concatenate([k0p[...], k0c[...]], axis=0), jnp.concatenate([v0p[...], v0c[...]], axis=0)),
                 (jnp.concatenate([k1p[...], k1c[...]], axis=0), jnp.concatenate([v1p[...], v1c[...]], axis=0)))
        vbts = (bands[0][1].T, bands[1][1].T)

        def scores(h):
            hk, j, half = h // Q_PER_KV, (h % Q_PER_KV) // 2, h % 2
            col = (hk * 4 + j) * LANES
            qp = q_ref[:, col:col + LANES]
            return _dot_nt(bands[hk][0], jnp.where(upper if half else jnp.logical_not(upper), qp, jnp.zeros_like(qp)))

        out_t = None
        ahead = scores(0)
        for h in range(N_Q_HEADS):
            hk, j, half = h // Q_PER_KV, (h % Q_PER_KV) // 2, h % 2
            raw = ahead
            if h + 1 < N_Q_HEADS:
                ahead = scores(h + 1)
            pn, _ = _softmax_t(raw, allowed, sink_ref[h])
            o_h = _dot(vbts[hk], pn.astype(BF))
            out_t = jnp.where(upper_rows, o_h, out_t) if half else o_h
            if half:
                col = (hk * 4 + j) * LANES
                y_ref[:, col:col + LANES] = out_t.T.astype(BF)

    return _call(body, name, _sds((s, D_MODEL), BF), (nb,),
                 [pl.BlockSpec(memory_space=pltpu.SMEM), pl.BlockSpec((ATTN_BLOCK, D_MODEL), lambda n: (n, 0))] + _kv_specs(),
                 pl.BlockSpec((ATTN_BLOCK, D_MODEL), lambda n: (n, 0)), ("parallel",))(
                     sinks, qr, kk0, kk0, kk1, kk1, vv0, vv0, vv1, vv1)


def _attention_bwd(name, qr, kk0, kk1, vv0, vv1, sinks, dy):
    s = qr.shape[0]
    nb = s // ATTN_BLOCK

    def body(sink_ref, q_ref, dy_ref, k0p, k0c, k1p, k1c, v0p, v0c, v1p, v1c, dq_ref, cur_ref, prev_ref, dsink_ref):
        @pl.when(pl.program_id(0) == 0)
        def _():
            dsink_ref[...] = jnp.zeros_like(dsink_ref)

        allowed = _band_mask_t(pl.program_id(0))
        lane = lax.broadcasted_iota(jnp.int32, (1, LANES), 1)
        upper = lane >= HEAD_DIM
        upper_rows = lax.broadcasted_iota(jnp.int32, (LANES, 1), 0) >= HEAD_DIM
        bands = ((jnp.concatenate([k0p[...], k0c[...]], axis=0), jnp.concatenate([v0p[...], v0c[...]], axis=0)),
                 (jnp.concatenate([k1p[...], k1c[...]], axis=0), jnp.concatenate([v1p[...], v1c[...]], axis=0)))
        kbts = (bands[0][0].T, bands[1][0].T)

        def scores(h):
            hk, j, half = h // Q_PER_KV, (h % Q_PER_KV) // 2, h % 2
            kb, vb = bands[hk]
            col = (hk * 4 + j) * LANES
            sel = upper if half else jnp.logical_not(upper)
            qp = q_ref[:, col:col + LANES]
            qa = jnp.where(sel, qp, jnp.zeros_like(qp))
            dya = jnp.where(sel, dy_ref[:, col:col + LANES], 0.0).astype(BF)
            return qa, dya, _dot_nt(kb, qa), _dot_nt(vb, dya)

        dsink = jnp.zeros((1, LANES), F32)
        dk_slab = jnp.zeros((2 * ATTN_BLOCK, LANES), F32)
        dv_slab = jnp.zeros((2 * ATTN_BLOCK, LANES), F32)
        dkk = dvv = dq_t = None
        ahead = scores(0)
        for h in range(N_Q_HEADS):
            hk, j, half = h // Q_PER_KV, (h % Q_PER_KV) // 2, h % 2
            qa, dya, raw, dp = ahead
            if h + 1 < N_Q_HEADS:
                ahead = scores(h + 1)
            pn, psink = _softmax_t(raw, allowed, sink_ref[h])
            delta = jnp.sum(pn * dp, axis=0, keepdims=True)
            ds = (pn * (dp - delta) * (HEAD_DIM ** -0.5)).astype(BF)
            dsink = dsink + jnp.where(lane == h, -jnp.sum(psink * delta), 0.0)
            dq_h = _dot(kbts[hk], ds)
            dq_t = jnp.where(upper_rows, dq_h, dq_t) if half else dq_h
            dk_h, dv_h = _dot(ds, qa), _dot(pn.astype(BF), dya)
            dkk, dvv = (dk_h, dv_h) if h % Q_PER_KV == 0 else (dkk + dk_h, dvv + dv_h)
            if half:
                col = (hk * 4 + j) * LANES
                dq_ref[:, col:col + LANES] = dq_t.T
            if h % Q_PER_KV == Q_PER_KV - 1:
                mine = upper if hk else jnp.logical_not(upper)
                dk_slab = jnp.where(mine, dkk + pltpu.roll(dkk, HEAD_DIM, axis=1), dk_slab)
                dv_slab = jnp.where(mine, dvv + pltpu.roll(dvv, HEAD_DIM, axis=1), dv_slab)
        prev_ref[:, :LANES] = dk_slab[:ATTN_BLOCK]
        prev_ref[:, LANES:] = dv_slab[:ATTN_BLOCK]
        cur_ref[:, :LANES] = dk_slab[ATTN_BLOCK:]
        cur_ref[:, LANES:] = dv_slab[ATTN_BLOCK:]
        dsink_ref[...] += dsink

    blk = pl.BlockSpec((ATTN_BLOCK, D_MODEL), lambda n: (n, 0))
    kvo = pl.BlockSpec((ATTN_BLOCK, 2 * LANES), lambda n: (n, 0))
    return _call(body, name,
                 (_sds((s, D_MODEL), F32), _sds((s, 2 * LANES), F32), _sds((s, 2 * LANES), F32), _sds((1, LANES), F32)),
                 (nb,), [pl.BlockSpec(memory_space=pltpu.SMEM), blk, blk] + _kv_specs(),
                 (blk, kvo, kvo, pl.BlockSpec((1, LANES), lambda n: (0, 0))), ("arbitrary",))(
                     sinks, qr, dy, kk0, kk0, kk1, kk1, vv0, vv0, vv1, vv1)


def _sgu_weights(wm_ref, g):
    t = lax.broadcasted_iota(jnp.int32, (SGU_CHUNK, SGU_CHUNK), 0)
    sidx = lax.broadcasted_iota(jnp.int32, (SGU_CHUNK, SGU_CHUNK), 1)
    return jnp.where(sidx <= t, wm_ref[g], 0.0).astype(BF)


def _layer_norm_stats(v):
    mu = jnp.mean(v, axis=-1, keepdims=True)
    cen = v - mu
    rstd = lax.rsqrt(jnp.mean(cen * cen, axis=-1, keepdims=True) + NORM_EPS)
    return cen * rstd, rstd


def _sgu(name, u_pre, v_pre, ln_w, ln_b, wm, bfull, tm=256):
    s, w = u_pre.shape
    tm = min(tm, s)

    def body(u_ref, v_ref, lw_ref, lb_ref, wm_ref, b_ref, y_ref):
        vhat, _ = _layer_norm_stats(_gelu(v_ref[...]))
        vn = (vhat * lw_ref[...] + lb_ref[...]).astype(BF)
        for g in range(SGU_GROUPS):
            wg = _sgu_weights(wm_ref, g)
            cols = slice(g * SGU_CHUNK, (g + 1) * SGU_CHUNK)
            for ch in range(tm // SGU_CHUNK):
                rows = slice(ch * SGU_CHUNK, (ch + 1) * SGU_CHUNK)
                f = _dot(wg, vn[rows, cols]) + b_ref[g]
                y_ref[rows, cols] = (_gelu(u_ref[rows, cols]) * f).astype(BF)

    full3 = pl.BlockSpec((SGU_GROUPS, SGU_CHUNK, SGU_CHUNK), lambda i: (0, 0, 0))
    return _call(body, name, _sds((s, w), BF), (s // tm,),
                 [_rows(tm, w), _rows(tm, w), _vec(w), _vec(w), full3, full3],
                 _rows(tm, w), ("parallel",))(u_pre, v_pre, ln_w, ln_b, wm, bfull)


def _sgu_bwd(name, u_pre, v_pre, ln_w, ln_b, wm, bfull, dy, tm=256, after=None):
    s, w = u_pre.shape
    tm = min(tm, s)
    steps = s // tm

    def body(u_ref, v_ref, lw_ref, lb_ref, wm_ref, b_ref, dy_ref, du_ref, dv_ref, dwm_ref, db_ref, dlw_ref, dlb_ref,
             dfsum_ref):
        i = pl.program_id(0)

        @pl.when(i == 0)
        def _():
            dwm_ref[...] = jnp.zeros_like(dwm_ref)
            dlw_ref[...] = jnp.zeros_like(dlw_ref)
            dlb_ref[...] = jnp.zeros_like(dlb_ref)
            dfsum_ref[...] = jnp.zeros_like(dfsum_ref)

        vpre = v_ref[...]
        vhat, rstd = _layer_norm_stats(_gelu(vpre))
        vn = (vhat * lw_ref[...] + lb_ref[...]).astype(BF)
        t = lax.broadcasted_iota(jnp.int32, (SGU_CHUNK, SGU_CHUNK), 0)
        sidx = lax.broadcasted_iota(jnp.int32, (SGU_CHUNK, SGU_CHUNK), 1)
        dvn_cols = []
        for g in range(SGU_GROUPS):
            wg = _sgu_weights(wm_ref, g)
            cols = slice(g * SGU_CHUNK, (g + 1) * SGU_CHUNK)
            dvn_rows = []
            dwg = jnp.zeros((SGU_CHUNK, SGU_CHUNK), F32)
            dfs = jnp.zeros((SGU_CHUNK, SGU_CHUNK), F32)
            for ch in range(tm // SGU_CHUNK):
                rows = slice(ch * SGU_CHUNK, (ch + 1) * SGU_CHUNK)
                upre = u_ref[rows, cols]
                dyv = dy_ref[rows, cols].astype(F32)
                f = _dot(wg, vn[rows, cols]) + b_ref[g]
                du_ref[rows, cols] = (dyv * f * _gelu_grad(upre)).astype(BF)
                df = dyv * _gelu(upre)
                dfb = df.astype(BF)
                dvn_rows.append(_dot_tn(wg, dfb))
                dwg = dwg + _dot_nt(dfb, vn[rows, cols])
                dfs = dfs + df
            dwm_ref[g] += jnp.where(sidx <= t, dwg, 0.0)
            dfsum_ref[g] += dfs
            dvn_cols.append(jnp.concatenate(dvn_rows, axis=0) if len(dvn_rows) > 1 else dvn_rows[0])
        dvn = jnp.concatenate(dvn_cols, axis=1)
        dlw_ref[...] += jnp.sum(dvn * vhat, axis=0, keepdims=True)
        dlb_ref[...] += jnp.sum(dvn, axis=0, keepdims=True)
        dvh = dvn * lw_ref[...]
        dvg = rstd * (dvh - jnp.mean(dvh, axis=-1, keepdims=True) - vhat * jnp.mean(dvh * vhat, axis=-1, keepdims=True))
        dv_ref[...] = (dvg * _gelu_grad(vpre)).astype(BF)

        @pl.when(i == steps - 1)
        def _():
            for g in range(SGU_GROUPS):
                db_ref[g:g + 1, :] = jnp.sum(dfsum_ref[g].T, axis=0, keepdims=True)

    full3 = pl.BlockSpec((SGU_GROUPS, SGU_CHUNK, SGU_CHUNK), lambda i: (0, 0, 0))
    return _call(body, name,
                 (_sds((s, w), BF), _sds((s, w), BF), _sds((SGU_GROUPS, SGU_CHUNK, SGU_CHUNK), F32),
                  _sds((SGU_GROUPS, SGU_CHUNK), F32), _sds((1, w), F32), _sds((1, w), F32)),
                 (steps,),
                 [_rows(tm, w), _rows(tm, w), _vec(w), _vec(w), full3, full3, _rows(tm, w)],
                 (_rows(tm, w), _rows(tm, w), full3, pl.BlockSpec((SGU_GROUPS, SGU_CHUNK), lambda i: (0, 0)), _vec(w), _vec(w)),
                 ("arbitrary",), scratch=[pltpu.VMEM((SGU_GROUPS, SGU_CHUNK, SGU_CHUNK), F32)], after=after)(
                     u_pre, v_pre, ln_w, ln_b, wm, bfull, dy)


def _mix_out(name, y_sgu, y_attn, ga_pre, gb_pre, x, g1, proj_a, proj_b, w_out, w2, sc2, sh2, tm=256):
    s, d = x.shape
    tm = min(tm, s)

    def body(ys_ref, ya_ref, ga_ref, gb_ref, x_ref, g1_ref, wa_ref, wb_ref, wo_ref, w2_ref, sc2_ref, sh2_ref,
             m_ref, pa_ref, pb_ref, o_ref, x1_ref, h2_ref):
        pa = _dot(ys_ref[...], wa_ref[...].reshape(d, d))
        pb = _dot(ya_ref[...], wb_ref[...].reshape(d, d))
        pa_ref[...] = pa.astype(BF)
        pb_ref[...] = pb.astype(BF)
        merged = (_sigmoid(ga_ref[...].astype(F32)) * pa + _sigmoid(gb_ref[...].astype(F32)) * pb).astype(BF)
        m_ref[...] = merged
        o = _dot(merged, wo_ref[...].reshape(d, d))
        o_ref[...] = o.astype(BF)
        x1 = x_ref[...] + g1_ref[...] * o
        x1_ref[...] = x1
        h2_ref[...] = ((x1 * _rms(x1)) * w2_ref[...] * (1.0 + sc2_ref[...]) + sh2_ref[...]).astype(BF)

    f, b = _sds((s, d), F32), _sds((s, d), BF)
    r = _rows(tm, d)
    wspec = _resident(proj_a.shape)
    return _call(body, name, (b, b, b, b, f, b), (s // tm,),
                 [r, r, r, r, r, _vec(d), wspec, wspec, wspec, _vec(d), _vec(d), _vec(d)], (r,) * 6, ("parallel",))(
                     y_sgu, y_attn, ga_pre, gb_pre, x, g1, proj_a, proj_b, w_out, w2, sc2, sh2)


def _mix_bwd(name, do, w_out, proj_a, proj_b, ga_pre, gb_pre, pa, pb, tm=256):
    s, d = do.shape
    tm = min(tm, s)

    def body(do_ref, wo_ref, wa_ref, wb_ref, ga_ref, gb_ref, pa_ref, pb_ref,
             dpa_ref, dpb_ref, dga_ref, dgb_ref, dys_ref, dya_ref):
        dm = _dot_nt(do_ref[...], wo_ref[...].reshape(d, d))
        ga = _sigmoid(ga_ref[...].astype(F32))
        gb = _sigmoid(gb_ref[...].astype(F32))
        dpa = (dm * ga).astype(BF)
        dpb = (dm * gb).astype(BF)
        dpa_ref[...] = dpa
        dpb_ref[...] = dpb
        dga_ref[...] = (dm * pa_ref[...].astype(F32) * ga * (1.0 - ga)).astype(BF)
        dgb_ref[...] = (dm * pb_ref[...].astype(F32) * gb * (1.0 - gb)).astype(BF)
        dys_ref[...] = _dot_nt(dpa, wa_ref[...].reshape(d, d)).astype(BF)
        dya_ref[...] = _dot_nt(dpb, wb_ref[...].reshape(d, d)).astype(BF)

    f, b = _sds((s, d), F32), _sds((s, d), BF)
    r = _rows(tm, d)
    wspec = _resident(w_out.shape)
    return _call(body, name, (b, b, b, b, b, b), (s // tm,), [r, wspec, wspec, wspec, r, r, r, r], (r,) * 6,
                 ("parallel",))(do, w_out, proj_a, proj_b, ga_pre, gb_pre, pa, pb)


def _ffn_up(name, h2, w_gate, w_up, tm=1024):
    s, d = h2.shape
    tm = min(tm, s)
    tc = FFN_SHARD_PAD

    def body(h_ref, wg_ref, wu_ref, a_ref, up_ref):
        hv = h_ref[...]
        a_ref[...] = _dot_nt(hv, wg_ref[0]).astype(BF)
        up_ref[...] = _dot_nt(hv, wu_ref[0]).astype(BF)

    wspec = pl.BlockSpec((1, tc, d), lambda j, i: (j, 0, 0))
    ospec = pl.BlockSpec((tm, tc), lambda j, i: (i, j))
    o = _sds((s, FFN_PAD), BF)
    return _call(body, name, (o, o), (N_CHIPS, s // tm), [pl.BlockSpec((tm, d), lambda j, i: (i, 0)), wspec, wspec],
                 (ospec, ospec), ("parallel", "parallel"))(h2, w_gate, w_up)


def _conv_act(name, a, up, cw, cb, tm=512, tc=FFN_SHARD_PAD):
    s, c = a.shape
    tm = min(tm, s)
    per = tm // HALO

    def body(a_ref, prev_ref, up_ref, cw_ref, cb_ref, h_ref):
        prev = jnp.where(pl.program_id(1) > 0, prev_ref[...].astype(F32), 0.0)
        ext = jnp.concatenate([prev, a_ref[...].astype(F32)], axis=0)
        ac = (cb_ref[...] + cw_ref[0:1, :] * pltpu.roll(ext, 2, axis=0) + cw_ref[1:2, :] * pltpu.roll(ext, 1, axis=0)
              + cw_ref[2:3, :] * ext)[HALO:]
        h_ref[...] = (ac * _sigmoid(ac) * up_ref[...].astype(F32)).astype(BF)

    tile = pl.BlockSpec((tm, tc), lambda j, i: (i, j))
    prev = pl.BlockSpec((HALO, tc), lambda j, i: (jnp.maximum(i * per - 1, 0), j))
    return _call(body, name, _sds((s, c), BF), (c // tc, s // tm),
                 [tile, prev, tile, pl.BlockSpec((3, tc), lambda j, i: (0, j)), pl.BlockSpec((1, tc), lambda j, i: (0, j))],
                 tile, ("parallel", "parallel"))(a, a, up, cw, cb)


def _conv_act_bwd(name, a, up, dhf, cw, cb, tm=512, tc=FFN_SHARD_PAD):
    s, c = a.shape
    tm = min(tm, s)
    per = tm // HALO
    steps = s // tm
    last = s // HALO - 1

    def body(a_ref, aprev_ref, anext_ref, up_ref, upnext_ref, dh_ref, dhnext_ref, cw_ref, cb_ref,
             da_ref, dup_ref, dcw_ref, dcb_ref):
        i = pl.program_id(1)

        @pl.when(i == 0)
        def _():
            dcw_ref[...] = jnp.zeros_like(dcw_ref)
            dcb_ref[...] = jnp.zeros_like(dcb_ref)

        prev = jnp.where(i > 0, aprev_ref[...].astype(F32), 0.0)
        ext = jnp.concatenate([prev, a_ref[...].astype(F32), anext_ref[...].astype(F32)], axis=0)
        a1 = pltpu.roll(ext, 1, axis=0)[HALO:]
        a2 = pltpu.roll(ext, 2, axis=0)[HALO:]
        a0 = ext[HALO:]
        ac = cb_ref[...] + cw_ref[0:1, :] * a2 + cw_ref[1:2, :] * a1 + cw_ref[2:3, :] * a0
        sig = _sigmoid(ac)
        dh = jnp.concatenate([dh_ref[...].astype(F32), jnp.where(i < steps - 1, dhnext_ref[...].astype(F32), 0.0)], axis=0)
        upe = jnp.concatenate([up_ref[...].astype(F32), upnext_ref[...].astype(F32)], axis=0)
        dac = dh * upe * (sig * (1.0 + ac * (1.0 - sig)))
        dup_ref[...] = (dh[:tm] * (ac * sig)[:tm]).astype(BF)
        n = tm + HALO
        da = (cw_ref[2:3, :] * dac + cw_ref[1:2, :] * pltpu.roll(dac, n - 1, axis=0)
              + cw_ref[0:1, :] * pltpu.roll(dac, n - 2, axis=0))
        da_ref[...] = da[:tm].astype(BF)
        dact = dac[:tm]
        dcb_ref[...] += jnp.sum(dact, axis=0, keepdims=True)
        dcw_ref[0:1, :] += jnp.sum(dact * a2[:tm], axis=0, keepdims=True)
        dcw_ref[1:2, :] += jnp.sum(dact * a1[:tm], axis=0, keepdims=True)
        dcw_ref[2:3, :] += jnp.sum(dact * a0[:tm], axis=0, keepdims=True)

    tile = pl.BlockSpec((tm, tc), lambda j, i: (i, j))
    prev = pl.BlockSpec((HALO, tc), lambda j, i: (jnp.maximum(i * per - 1, 0), j))
    nxt = pl.BlockSpec((HALO, tc), lambda j, i: (jnp.minimum((i + 1) * per, last), j))
    cw_spec = pl.BlockSpec((3, tc), lambda j, i: (0, j))
    cb_spec = pl.BlockSpec((1, tc), lambda j, i: (0, j))
    return _call(body, name, (_sds((s, c), BF), _sds((s, c), BF), _sds((3, c), F32), _sds((1, c), F32)),
                 (c // tc, steps), [tile, prev, nxt, tile, nxt, tile, nxt, cw_spec, cb_spec],
                 (tile, tile, cw_spec, cb_spec), ("parallel", "arbitrary"))(a, a, a, up, up, dhf, dhf, cw, cb)


def _ffn_down(name, hf, w_down, x1, g2, tm=512):
    s, d = x1.shape
    tm = min(tm, s)

    def body(hf_ref, wd_ref, x1_ref, g2_ref, dn_ref, x2_ref):
        dn = _dot(hf_ref[...], wd_ref[...].reshape(FFN_PAD, d))
        dn_ref[...] = dn.astype(BF)
        x2_ref[...] = x1_ref[...] + g2_ref[...] * dn

    return _call(body, name, (_sds((s, d), BF), _sds((s, d), F32)), (s // tm,),
                 [_rows(tm, FFN_PAD), _resident(w_down.shape), _rows(tm, d), _vec(d)],
                 (_rows(tm, d), _rows(tm, d)), ("parallel",))(hf, w_down, x1, g2)


def _ffn_down_bwd(name, dx2, dn, g2, w_down, tm=512, after=None):
    s, d = dx2.shape
    tm = min(tm, s)
    tc = FFN_SHARD_PAD

    def body(dx_ref, dn_ref, g2_ref, wd_ref, ddn_ref, dhf_ref, dg_ref):
        @pl.when(pl.program_id(0) == 0)
        def _():
            dg_ref[...] = jnp.zeros_like(dg_ref)

        dxv = dx_ref[...]
        ddn = (dxv * g2_ref[...]).astype(BF)
        ddn_ref[...] = ddn
        dg_ref[...] += jnp.sum(dxv * dn_ref[...].astype(F32), axis=0, keepdims=True)
        for k in range(N_CHIPS):
            dhf_ref[:, k * tc:(k + 1) * tc] = _dot_nt(ddn, wd_ref[k]).astype(BF)

    return _call(body, name, (_sds((s, d), BF), _sds((s, FFN_PAD), BF), _sds((1, d), F32)), (s // tm,),
                 [_rows(tm, d), _rows(tm, d), _vec(d), _resident(w_down.shape)],
                 (_rows(tm, d), _rows(tm, FFN_PAD), _vec(d)), ("arbitrary",), after=after)(dx2, dn, g2, w_down)


def _ffn_up_bwd(name, da, dup, w_gate, w_up, x1, dx2, w2, sc2, o, g1, tm=256, after=None):
    s, d = x1.shape
    tm = min(tm, s)
    tc = FFN_SHARD_PAD

    def body(da_ref, dup_ref, wg_ref, wu_ref, x1_ref, dx2_ref, w2_ref, sc2_ref, o_ref, g1_ref,
             dx1_ref, do_ref, dnw_ref, dsh_ref, dg1_ref):
        @pl.when(pl.program_id(0) == 0)
        def _():
            dnw_ref[...] = jnp.zeros_like(dnw_ref)
            dsh_ref[...] = jnp.zeros_like(dsh_ref)
            dg1_ref[...] = jnp.zeros_like(dg1_ref)

        dh = jnp.zeros((tm, d), F32)
        for k in range(N_CHIPS):
            cols = slice(k * tc, (k + 1) * tc)
            dh = dh + _dot(da_ref[:, cols], wg_ref[k]) + _dot(dup_ref[:, cols], wu_ref[k])
        xv = x1_ref[...]
        r = _rms(xv)
        xn = xv * r
        dxn = dh * (w2_ref[...] * (1.0 + sc2_ref[...]))
        dx1 = dx2_ref[...] + r * (dxn - xn * jnp.mean(dxn * xn, axis=-1, keepdims=True))
        dx1_ref[...] = dx1
        dnw_ref[...] += jnp.sum(dh * xn, axis=0, keepdims=True)
        dsh_ref[...] += jnp.sum(dh, axis=0, keepdims=True)
        do_ref[...] = (dx1 * g1_ref[...]).astype(BF)
        dg1_ref[...] += jnp.sum(dx1 * o_ref[...].astype(F32), axis=0, keepdims=True)

    v = _sds((1, d), F32)
    r = _rows(tm, d)
    wspec = _resident(w_gate.shape)
    return _call(body, name, (_sds((s, d), F32), _sds((s, d), BF), v, v, v), (s // tm,),
                 [_rows(tm, FFN_PAD), _rows(tm, FFN_PAD), wspec, wspec, r, r, _vec(d), _vec(d), r, _vec(d)],
                 (r, r, _vec(d), _vec(d), _vec(d)), ("arbitrary",), after=after)(da, dup, w_gate, w_up, x1, dx2, w2, sc2, o, g1)


def _loss_head(name, x, w, target, tm=512):
    s, d = x.shape
    tm = min(tm, s)

    def body(x_ref, w_ref, t_ref, dx_ref, loss_ref, dw_ref):
        @pl.when(pl.program_id(0) == 0)
        def _():
            loss_ref[...] = jnp.zeros_like(loss_ref)
            dw_ref[...] = jnp.zeros_like(dw_ref)

        xv = x_ref[...]
        r = _rms(xv)
        xn = xv * r
        err = xn * w_ref[...] - t_ref[...]
        loss_ref[...] += 0.5 * jnp.sum(jnp.mean(err * err, axis=-1, keepdims=True))
        dy = err * (1.0 / d)
        dw_ref[...] += jnp.sum(dy * xn, axis=0, keepdims=True)
        dxn = dy * w_ref[...]
        dx_ref[...] = r * (dxn - xn * jnp.mean(dxn * xn, axis=-1, keepdims=True))

    return _call(body, name, (_sds((s, d), F32), _sds((1, LANES), F32), _sds((1, d), F32)), (s // tm,),
                 [_rows(tm, d), _vec(d), _rows(tm, d)], (_rows(tm, d), _vec(LANES), _vec(d)), ("arbitrary",))(x, w, target)


def _layer_fwd(l, x, mod, p, cosf, sinf, after=None, late=None, later=None):
    sh1, sc1, g1, sh2, sc2, g2 = mod
    tag = f"l{l}_"
    h, qr, kk0, kk1, vv0, vv1, u_pre, v_pre, ga_pre, gb_pre = _in_proj(
        tag + "in_proj", x, p["norm1_w"], sc1, sh1, p["w_in"], cosf, sinf, after=after)
    y_attn = _attention(tag + "attn", qr, kk0, kk1, vv0, vv1, p["sinks"])
    y_sgu = _sgu(tag + "sgu", u_pre, v_pre, p["sgu_ln_w"], p["sgu_ln_b"], p["sgu_w"], p["sgu_bfull"])
    if late is not None:
        p = dict(p, **late(y_sgu))
    merged, pa, pb, o, x1, h2 = _mix_out(tag + "mix_out", y_sgu, y_attn, ga_pre, gb_pre, x, g1, p["proj_a"], p["proj_b"],
                                         p["w_out"], p["norm2_w"], sc2, sh2)
    if later is not None:
        p = dict(p, **later(h2))
    a, up = _ffn_up(tag + "ffn_up", h2, p["w_gate"], p["w_up"])
    hf = _conv_act(tag + "conv_act", a, up, p["conv_w"], p["conv_b"])
    dn, x2 = _ffn_down(tag + "ffn_down", hf, p["w_down"], x1, g2)
    saved = dict(x=x, h=h, qr=qr, kk0=kk0, kk1=kk1, vv0=vv0, vv1=vv1, u_pre=u_pre, v_pre=v_pre, ga_pre=ga_pre,
                 gb_pre=gb_pre, y_attn=y_attn, y_sgu=y_sgu, merged=merged, pa=pa, pb=pb, o=o, x1=x1, h2=h2, a=a, up=up,
                 hf=hf, dn=dn)
    return x2, saved, p


def _layer_bwd(l, dx2, mod, p, sv, cosf, sinf, after=None, emit=None):
    sh1, sc1, g1, sh2, sc2, g2 = mod
    tag = f"l{l}_b_"
    d = D_MODEL
    g = {}
    ready = (lambda names: emit({k: g.pop(k) for k in names})) if emit else (lambda names: None)
    ddn, dhf, dg2 = _ffn_down_bwd(tag + "ffn_down", dx2, sv["dn"], g2, p["w_down"], after=after)
    g["w_down"] = _matmul_tn(tag + "dw_down", sv["hf"], ddn, tk=FFN_SHARD_PAD).reshape(N_CHIPS, FFN_SHARD_PAD, d)
    da, dup, g["conv_w"], g["conv_b"] = _conv_act_bwd(tag + "conv_act", sv["a"], sv["up"], dhf, p["conv_w"], p["conv_b"])
    g["w_gate"] = _matmul_tn(tag + "dw_gate", da, sv["h2"], tk=FFN_SHARD_PAD).reshape(N_CHIPS, FFN_SHARD_PAD, d)
    g["w_up"] = _matmul_tn(tag + "dw_up", dup, sv["h2"], tk=FFN_SHARD_PAD).reshape(N_CHIPS, FFN_SHARD_PAD, d)
    dx1, do, da2, dsh2, dg1 = _ffn_up_bwd(tag + "ffn_up", da, dup, p["w_gate"], p["w_up"], sv["x1"], dx2, p["norm2_w"],
                                          sc2, sv["o"], g1, after=ready(("w_down", "w_gate", "w_up")))
    g["norm2_w"] = da2 * (1.0 + sc2)
    dsc2 = da2 * p["norm2_w"]
    g["w_out"] = _matmul_tn(tag + "dw_out", sv["merged"], do).reshape(N_CHIPS, d // N_CHIPS, d)
    dpa, dpb, dga, dgb, dy_sgu, dy_attn = _mix_bwd(tag + "mix", do, p["w_out"], p["proj_a"], p["proj_b"], sv["ga_pre"],
                                                  sv["gb_pre"], sv["pa"], sv["pb"])
    g["proj_a"] = _matmul_tn(tag + "dproj_a", sv["y_sgu"], dpa).reshape(N_CHIPS, d // N_CHIPS, d)
    g["proj_b"] = _matmul_tn(tag + "dproj_b", sv["y_attn"], dpb).reshape(N_CHIPS, d // N_CHIPS, d)
    du, dv, g["sgu_w"], g["sgu_b"], g["sgu_ln_w"], g["sgu_ln_b"] = _sgu_bwd(
        tag + "sgu", sv["u_pre"], sv["v_pre"], p["sgu_ln_w"], p["sgu_ln_b"], p["sgu_w"], p["sgu_bfull"], dy_sgu,
        after=ready(("w_out", "proj_a", "proj_b")))
    dqr, dkv_cur, dkv_prev, dsink = _attention_bwd(tag + "attn", sv["qr"], sv["kk0"], sv["kk1"], sv["vv0"], sv["vv1"],
                                                   p["sinks"], dy_attn)
    g["sinks"] = dsink[0, :N_Q_HEADS]
    dq, dkv = _rope_bwd(tag + "rope", dqr, dkv_cur, dkv_prev, cosf, sinf)
    dw_in, row0 = None, 0
    for n, t in (("q", dq), ("kv", dkv), ("u", du), ("v", dv), ("ga", dga), ("gb", dgb)):
        dw_in = _matmul_tn_rows(tag + "dw_in_" + n, t, sv["h"], dw_in, row0, IN_COLS)
        row0 += t.shape[1]
    g["w_in"] = dw_in.reshape(N_CHIPS, IN_COLS // N_CHIPS, d)
    dx, da1, dsh1 = _in_proj_bwd(tag + "in_proj", dq, dkv, du, dv, dga, dgb, p["w_in"], sv["x"], p["norm1_w"], sc1, dx1)
    g["norm1_w"] = da1 * (1.0 + sc1)
    dsc1 = da1 * p["norm1_w"]
    return dx, (dsh1, dsc1, dg1, dsh2, dsc2, dg2), g


def _pad_to(a, axis, size):
    pad = [(0, 0)] * a.ndim
    pad[axis] = (0, size - a.shape[axis])
    return jnp.pad(a, pad)


def _early_params(w_in, small):
    d = D_MODEL
    return dict(
        w_in=w_in.reshape(IN_COLS, d), norm1_w=small["norm1_w"].reshape(1, d), sinks=small["sinks"],
        sgu_ln_w=small["sgu_ln_w"].reshape(1, d), sgu_ln_b=small["sgu_ln_b"].reshape(1, d), sgu_w=small["sgu_w"],
        sgu_bfull=jnp.broadcast_to(small["sgu_b"][:, :, None], (SGU_GROUPS, SGU_CHUNK, SGU_CHUNK)))


def _mix_params(proj_a, proj_b, w_out, small):
    return dict(proj_a=proj_a, proj_b=proj_b, w_out=w_out, norm2_w=small["norm2_w"].reshape(1, D_MODEL))


def _ffn_params(w_gate, w_up, w_down, conv_w, small):
    return dict(
        w_gate=w_gate, w_up=w_up, w_down=w_down, conv_w=conv_w.transpose(1, 0, 2).reshape(3, FFN_PAD),
        conv_b=_pad_to(small["conv_b"].reshape(N_CHIPS, FFN_SHARD), 1, FFN_SHARD_PAD).reshape(1, FFN_PAD))


def _layer_params(w_in, proj_a, proj_b, w_out, w_gate, w_up, w_down, conv_w, small):
    return dict(_early_params(w_in, small), **_mix_params(proj_a, proj_b, w_out, small),
                **_ffn_params(w_gate, w_up, w_down, conv_w, small))


def _conv_grads_natural(g):
    cw = g["conv_w"].reshape(3, N_CHIPS, FFN_SHARD_PAD)[:, :, :FFN_SHARD].reshape(3, FFN_DIM)
    cb = g["conv_b"].reshape(N_CHIPS, FFN_SHARD_PAD)[:, :FFN_SHARD].reshape(FFN_DIM)
    return cw, cb


def _rope_tables(positions):
    inv_freq = ROPE_THETA ** (-jnp.arange(0, ROT_DIM, 2, dtype=F32) / ROT_DIM)
    ang = positions.astype(F32)[:, None] * inv_freq
    cos, sin = jnp.cos(ang), jnp.sin(ang)
    s = positions.shape[0]
    rest = HEAD_DIM - ROT_DIM
    cos_head = jnp.concatenate([cos, cos, jnp.ones((s, rest), F32)], axis=1)
    sin_head = jnp.concatenate([-sin, sin, jnp.zeros((s, rest), F32)], axis=1)
    return jnp.tile(cos_head, (1, LANES // HEAD_DIM)), jnp.tile(sin_head, (1, LANES // HEAD_DIM))


ADA_ROWS = 16


def _ada_fwd(name, c_rows, ada_w, ada_b_cols, tn=512):
    depth, d, n = ada_w.shape

    def body(c_ref, w_ref, b_ref, o_ref):
        cv = c_ref[...]
        act = (cv * _sigmoid(cv)).astype(BF)
        o_ref[0] = _dot(act, w_ref[0].astype(BF)) + b_ref[0]

    return _call(body, name, _sds((depth, ADA_ROWS, n), F32), (depth, n // tn),
                 [pl.BlockSpec((ADA_ROWS, d), lambda l, j: (0, 0)), pl.BlockSpec((1, d, tn), lambda l, j: (l, 0, j)),
                  pl.BlockSpec((1, 1, tn), lambda l, j: (l, 0, j))],
                 pl.BlockSpec((1, ADA_ROWS, tn), lambda l, j: (l, 0, j)), ("parallel", "parallel"))(c_rows, ada_w, ada_b_cols)


def _ada_bwd(name, c_rows, dmod_cols, tn=512):
    depth, _, n = dmod_cols.shape
    d = c_rows.shape[1]

    def body(c_ref, dm_ref, o_ref):
        cv = c_ref[...]
        act = (cv * _sigmoid(cv)).astype(BF)
        o_ref[0] = _dot_tn(act, dm_ref[0].astype(BF))

    return _call(body, name, _sds((depth, d, n), F32), (depth, n // tn),
                 [pl.BlockSpec((ADA_ROWS, d), lambda l, j: (0, 0)), pl.BlockSpec((1, ADA_ROWS, tn), lambda l, j: (l, 0, j))],
                 pl.BlockSpec((1, d, tn), lambda l, j: (l, 0, j)), ("parallel", "parallel"))(c_rows, dmod_cols)


def _colsum(name, a):
    r, n = a.shape

    def body(a_ref, o_ref):
        o_ref[...] = jnp.sum(a_ref[...], axis=0, keepdims=True)

    return _call(body, name, _sds((1, n), F32), (1,), [pl.BlockSpec((r, n), lambda i: (0, 0))],
                 pl.BlockSpec((1, n), lambda i: (0, 0)), ("arbitrary",))(a)


REL_SIBLING = (0, 0, 1)
REL_CHIPS = ((1, 0, 0), (0, 1, 0), (1, 1, 0))
REL_ALL = tuple((fx, fy, fc) for fx in (0, 1) for fy in (0, 1) for fc in (0, 1) if fx or fy or fc)


def _chip_of(dev):
    return 2 * dev[0] + dev[1]


def _dev_of(dev):
    return 4 * dev[0] + 2 * dev[1] + dev[2]


def _flip(dev, rel):
    return tuple(1 - m if f else m for m, f in zip(dev, rel))


def _exchange(name, arrays, n_out, stages, aliases=None):
    out_shapes, stages = stages[0], stages[1:]
    n_in = len(arrays)
    aliases = aliases or {}
    n_remote = sum(len(plan) for plan, _ in stages)
    n_local = sum(len(local) for _, local in stages)

    def at(ref, idx):
        return ref.at[idx] if len(idx) else ref

    def body(*refs):
        bufs = list(refs[:n_in + n_out])
        for i_in, i_out in aliases.items():
            bufs[i_in] = bufs[n_in + i_out]
        send_sems, recv_sems, local_sems = refs[n_in + n_out:]
        me = (lax.axis_index("x"), lax.axis_index("y"), lax.axis_index("c"))
        base_r = base_l = 0
        pending = []
        for plan, local in stages:
            def remote(k, entry, sender, receiver):
                rel, si, ssel, di, dsel = entry
                return pltpu.make_async_remote_copy(
                    src_ref=at(bufs[si], ssel(sender, receiver)), dst_ref=at(bufs[di], dsel(sender, receiver)),
                    send_sem=send_sems.at[k], recv_sem=recv_sems.at[k], device_id=_flip(me, rel), device_id_type=MESH)

            sends = [remote(base_r + k, e, me, _flip(me, e[0])) for k, e in enumerate(plan)]
            for cp in sends:
                cp.start()
            for k, (si, ssel, di, dsel) in enumerate(local):
                cp = pltpu.make_async_copy(at(bufs[si], ssel(me)), at(bufs[di], dsel(me)), local_sems.at[base_l + k])
                cp.start()
                pending.append(cp.wait)
            for k, e in enumerate(plan):
                remote(base_r + k, e, _flip(me, e[0]), me).wait_recv()
            pending += [cp.wait_send for cp in sends]
            base_r += len(plan)
            base_l += len(local)
        for wait in pending:
            wait()

    any_spec = pl.BlockSpec(memory_space=pl.ANY)
    return pl.pallas_call(
        body, name=name, out_shape=tuple(out_shapes), in_specs=[any_spec] * n_in, out_specs=tuple([any_spec] * n_out),
        input_output_aliases=dict(aliases),
        scratch_shapes=[pltpu.SemaphoreType.DMA((max(n_remote, 1),)), pltpu.SemaphoreType.DMA((max(n_remote, 1),)),
                        pltpu.SemaphoreType.DMA((max(n_local, 1),))])(*arrays)


HBM_SPEC = pl.BlockSpec(memory_space=pltpu.HBM)
SEM_SPEC = pl.BlockSpec(memory_space=pltpu.SEMAPHORE)


def _split_copies(bufs, plan, local, send_sems, recv_sems, local_sems):
    me = (lax.axis_index("x"), lax.axis_index("y"), lax.axis_index("c"))

    def at(ref, idx):
        return ref.at[idx] if len(idx) else ref

    def remote(k, sender, receiver):
        rel, si, ssel, di, dsel = plan[k]
        return pltpu.make_async_remote_copy(
            src_ref=at(bufs[si], ssel(sender, receiver)), dst_ref=at(bufs[di], dsel(sender, receiver)),
            send_sem=send_sems.at[k], recv_sem=recv_sems.at[k], device_id=_flip(me, rel), device_id_type=MESH)

    sends = [remote(k, me, _flip(me, plan[k][0])) for k in range(len(plan))]
    arrivals = [remote(k, _flip(me, plan[k][0]), me) for k in range(len(plan))]
    locs = [pltpu.make_async_copy(at(bufs[si], ssel(me)), at(bufs[di], dsel(me)), local_sems.at[k])
            for k, (si, ssel, di, dsel) in enumerate(local)]
    return sends, arrivals, locs


def _exchange_start(name, arrays, out_shapes, plan, local):
    n_in, n_out = len(arrays), len(out_shapes)

    n_buf = n_in + n_out

    def body(*refs):
        sems = refs[n_buf:n_buf + 3]
        bufs = refs[n_buf + 3:2 * n_buf + 3]
        token = refs[-1]
        sends, _, locs = _split_copies(bufs, plan, local, *sems)
        for cp in sends + locs:
            cp.start()
        token[...] = jnp.zeros_like(token)

    zones = [lax.empty(o.shape, o.dtype) for o in out_shapes]
    operands = [pltpu.with_memory_space_constraint(a, pltpu.HBM) for a in list(arrays) + zones]
    sem = lambda n: pltpu.SemaphoreType.DMA((max(n, 1),))
    out = pl.pallas_call(
        body, name=name,
        out_shape=(sem(len(plan)), sem(len(plan)), sem(len(local)), *[pltpu.HBM(a.shape, a.dtype) for a in operands],
                   _sds((SUBLANES, LANES), F32)),
        in_specs=[HBM_SPEC] * (n_in + n_out),
        out_specs=(SEM_SPEC, SEM_SPEC, SEM_SPEC, *[HBM_SPEC] * (n_in + n_out), pl.BlockSpec(memory_space=pltpu.VMEM)),
        input_output_aliases={i: 3 + i for i in range(n_in + n_out)},
        compiler_params=pltpu.CompilerParams(has_side_effects=pltpu.SideEffectType.DATAFLOW_SIDE_EFFECTING))(*operands)
    return out[:3], out[3:3 + n_in], out[3 + n_in:3 + n_in + n_out], out[-1]


def _exchange_wait(name, sems, thru, zones, plan, local, after):
    n_in, n_out = len(thru), len(zones)

    def body(*refs):
        bufs = refs[:n_in + n_out]
        sends, arrivals, locs = _split_copies(bufs, plan, local, *refs[n_in + n_out:n_in + n_out + 3])
        for cp in arrivals:
            cp.wait_recv()
        for cp in sends:
            cp.wait_send()
        for cp in locs:
            cp.wait()

    out = pl.pallas_call(
        body, name=name, out_shape=tuple(pltpu.HBM(a.shape, a.dtype) for a in list(thru) + list(zones)),
        in_specs=[HBM_SPEC] * (n_in + n_out) + [SEM_SPEC] * 3 + [pl.BlockSpec(memory_space=pl.ANY)],
        out_specs=tuple([HBM_SPEC] * (n_in + n_out)), input_output_aliases={i: i for i in range(n_in + n_out)},
        compiler_params=pltpu.CompilerParams(has_side_effects=pltpu.SideEffectType.DATAFLOW_SIDE_EFFECTING))(
            *thru, *zones, *sems, after)
    return out[n_in:]


def _whole(*_):
    return ()


def _half_rows(rows, core):
    return pl.ds(core * (rows // 2), rows // 2)


def _gather_weights_plan(shards):
    n = len(shards)
    dsts = [_sds((N_CHIPS,) + a.shape, a.dtype) for a in shards]
    fetch, forward = [], []
    for t, a in enumerate(shards):
        rows = a.shape[0]
        if rows % (2 * 16) == 0:
            fetch += [(rel, t, (lambda s_, r_, rows=rows: (_half_rows(rows, s_[2]),)), n + t,
                       (lambda s_, r_, rows=rows: (_chip_of(s_), _half_rows(rows, s_[2])))) for rel in REL_CHIPS]
            forward += [(REL_SIBLING, n + t, (lambda s_, r_, rows=rows, rel=rel: (_chip_of(_flip(s_, rel)), _half_rows(rows, s_[2]))),
                         n + t, (lambda s_, r_, rows=rows, rel=rel: (_chip_of(_flip(s_, rel)), _half_rows(rows, s_[2]))))
                        for rel in REL_CHIPS]
        else:
            fetch += [(rel, t, _whole, n + t, lambda s_, r_: (_chip_of(s_),)) for rel in REL_CHIPS]
    local = [(t, _whole, n + t, lambda me: (_chip_of(me),)) for t in range(n)]
    return dsts, fetch, local, forward


def _gather_weights_start(name, shards):
    dsts, fetch, local, forward = _gather_weights_plan(shards)
    sems, thru, zones, token = _exchange_start(name, shards, dsts, fetch, local)
    return (sems, thru, zones, fetch, local, forward), token


def _gather_weights_finish(name, pending, after):
    sems, thru, zones, fetch, local, forward = pending
    landed = _exchange_wait(name + "_wait", sems, thru, zones, fetch, local, after)
    n = len(landed)
    return _exchange(name + "_forward", landed, n, [[_sds(a.shape, a.dtype) for a in landed], (forward, [])],
                     aliases={t: t for t in range(n)})


def _gather_chips(name, arrays):
    n = len(arrays)
    dsts = [_sds((N_CHIPS,) + a.shape, a.dtype) for a in arrays]
    plan = [(rel, t, _whole, n + t, lambda s_, r_: (_chip_of(s_),)) for t in range(n) for rel in REL_CHIPS]
    local = [(t, _whole, n + t, lambda me: (_chip_of(me),)) for t in range(n)]
    return _exchange(name, arrays, n, [dsts, (plan, local)])


def _gather_all(name, a):
    plan = [(rel, 0, _whole, 1, lambda s_, r_: (_dev_of(s_),)) for rel in REL_ALL]
    local = [(0, _whole, 1, lambda me: (_dev_of(me),))]
    return _exchange(name, [a], 1, [[_sds((2 * N_CHIPS,) + a.shape, a.dtype)], (plan, local)])[0]


def _swap_halves(name, grads):
    n = len(grads)
    dsts = [_sds((g.shape[0], g.shape[1] // 2, g.shape[2]), g.dtype) for g in grads]
    plan = [(REL_SIBLING, t, (lambda s_, r_, rows=g.shape[1]: (pl.ds(0, N_CHIPS), _half_rows(rows, r_[2]))), n + t, _whole)
            for t, g in enumerate(grads)]
    return _exchange(name, grads, n, [dsts, (plan, [])])


def _scatter_chips_plan(sums):
    n = len(sums)
    dsts = [_sds(a.shape, a.dtype) for a in sums]
    plan = [(rel, t, lambda s_, r_: (_chip_of(r_),), n + t, lambda s_, r_: (_chip_of(s_),))
            for t in range(n) for rel in REL_CHIPS]
    local = [(t, lambda me: (_chip_of(me),), n + t, lambda me: (_chip_of(me),)) for t in range(n)]
    return dsts, plan, local


def _scatter_chips(name, sums):
    dsts, plan, local = _scatter_chips_plan(sums)
    return _exchange(name, sums, len(sums), [dsts, (plan, local)])


def _scatter_chips_start(name, sums):
    dsts, plan, local = _scatter_chips_plan(sums)
    sems, thru, zones, token = _exchange_start(name, sums, dsts, plan, local)
    return (sems, thru, zones, plan, local), token


def _scatter_chips_finish(name, pending, after):
    sems, thru, zones, plan, local = pending
    return _exchange_wait(name + "_wait", sems, thru, zones, plan, local, after)


def _swap_back(name, totals, layer):
    n = len(totals)
    dsts = [_sds(a.shape, a.dtype) for a in totals]
    plan = [(REL_SIBLING, n + t, (lambda s_, r_, rows=a.shape[1]: (layer, _half_rows(rows, s_[2]))),
             n + t, (lambda s_, r_, rows=a.shape[1]: (layer, _half_rows(rows, s_[2])))) for t, a in enumerate(totals)]
    return _exchange(name, totals, n, [dsts, (plan, [])], aliases={t: t for t in range(n)})


def _add_halves(name, g, recv, core):
    nch, half, c = recv.shape

    def body(core_ref, g_ref, r_ref, o_ref):
        o_ref[0] = (g_ref[0, 0].astype(F32) + r_ref[0].astype(F32)).astype(o_ref.dtype)

    spec = pltpu.PrefetchScalarGridSpec(
        num_scalar_prefetch=1, grid=(nch,),
        in_specs=[pl.BlockSpec((1, 1, half, c), lambda k, core_ref: (k, core_ref[0], 0, 0)),
                  pl.BlockSpec((1, half, c), lambda k, core_ref: (k, 0, 0))],
        out_specs=pl.BlockSpec((1, half, c), lambda k, core_ref: (k, 0, 0)))
    return pl.pallas_call(body, name=name, out_shape=_sds(recv.shape, recv.dtype), grid_spec=spec,
                          compiler_params=pltpu.CompilerParams(dimension_semantics=("parallel",),
                                                               vmem_limit_bytes=VMEM_LIMIT))(
                                                                   core, g.reshape(nch, 2, half, c), recv)


def _sum_chips(name, a, core, layer, total):
    nch, half, c = a.shape

    def body(core_ref, a_ref, *rest):
        o_ref = rest[-1]
        acc = a_ref[0].astype(F32)
        for k in range(1, nch):
            acc = acc + a_ref[k].astype(F32)
        o_ref[0, 0] = acc

    in_specs = [pl.BlockSpec((nch, half, c), lambda i, core_ref: (0, 0, 0))]
    args = [core, a]
    if total is not None:
        in_specs.append(pl.BlockSpec(memory_space=pl.ANY))
        args.append(total.reshape(DEPTH, 2, half, c))
    spec = pltpu.PrefetchScalarGridSpec(
        num_scalar_prefetch=1, grid=(1,), in_specs=in_specs,
        out_specs=pl.BlockSpec((1, 1, half, c), lambda i, core_ref: (layer, core_ref[0], 0, 0)))
    out = pl.pallas_call(body, name=name, out_shape=_sds((DEPTH, 2, half, c), F32), grid_spec=spec,
                         input_output_aliases={2: 0} if total is not None else {},
                         compiler_params=pltpu.CompilerParams(dimension_semantics=("arbitrary",),
                                                              vmem_limit_bytes=VMEM_LIMIT))(*args)
    return out.reshape(DEPTH, 2 * half, c)


def _adamw_update(w, g, m, v):
    mn = ADAM_B1 * m + (1.0 - ADAM_B1) * g
    vn = ADAM_B2 * v + (1.0 - ADAM_B2) * (g * g)
    m_hat = mn / (1.0 - ADAM_B1 ** ADAM_STEP)
    v_hat = vn / (1.0 - ADAM_B2 ** ADAM_STEP)
    return -ADAM_LR * (m_hat / (jnp.sqrt(v_hat) + ADAM_EPS) + ADAM_WD * w), mn, vn


def _adamw(name, w, g, m, v):
    depth, r, c = w.shape
    tr = next(t for t in (512, 448, 384, 352, 336, 256, 192, 128, 64, 32, 16, 8) if r % t == 0 and t * c <= ADAM_TILE_ELEMS)

    def body(w_ref, g_ref, m_ref, v_ref, go_ref, d_ref, mo_ref, vo_ref):
        gv = g_ref[...]
        go_ref[...] = gv
        d_ref[...], mo_ref[...], vo_ref[...] = _adamw_update(w_ref[...], gv, m_ref[...], v_ref[...])

    spec = pl.BlockSpec((1, tr, c), lambda l, i: (l, i, 0))
    o = _sds(w.shape, F32)
    return _call(body, name, (o, o, o, o), (depth, r // tr), [spec] * 4, (spec,) * 4, ("parallel", "parallel"))(w, g, m, v)


def _adamw_small(name, ws, gs, ms, vs):
    n = len(ws)

    def body(*refs):
        for t in range(n):
            w_ref, g_ref, m_ref, v_ref = (refs[k * n + t] for k in range(4))
            d_ref, mo_ref, vo_ref = (refs[(4 + k) * n + t] for k in range(3))
            d_ref[...], mo_ref[...], vo_ref[...] = _adamw_update(w_ref[...], g_ref[...], m_ref[...], v_ref[...])

    outs = [_sds(w.shape, F32) for w in ws]
    res = pl.pallas_call(body, name=name, out_shape=tuple(outs * 3))(*ws, *gs, *ms, *vs)
    return res[:n], res[n:2 * n], res[2 * n:]


def _pack(arrays, rows):
    flat = jnp.concatenate([a.reshape(-1).astype(F32) for a in arrays])
    return _pad_to(flat, 0, rows * LANES).reshape(rows, LANES)


def _unpack(packed, shapes):
    flat = packed.reshape(-1)
    out, off = [], 0
    for shp in shapes:
        n = 1
        for s_ in shp:
            n *= s_
        out.append(flat[off:off + n].reshape(shp))
        off += n
    return out


_MATRICES = ("w_in", "proj_a", "proj_b", "w_out", "w_gate", "w_up", "w_down")
_SMALL = (("norm1_w", (D_MODEL,)), ("sinks", (N_Q_HEADS,)), ("sgu_ln_w", (SGU_WIDTH,)), ("sgu_ln_b", (SGU_WIDTH,)),
          ("sgu_w", (SGU_GROUPS, SGU_CHUNK, SGU_CHUNK)), ("sgu_b", (SGU_GROUPS, SGU_CHUNK)), ("norm2_w", (D_MODEL,)),
          ("conv_w", (3, FFN_DIM)), ("conv_b", (FFN_DIM,)), ("final_norm_w", (D_MODEL,)))
SMALL_ROWS = 320
ADAM_TILE_ELEMS = 384 * 1024


def _reduce_cores(tag, partial, core):
    names = list(partial)
    from_sibling = _swap_halves(tag + "_cores", [partial[k] for k in names])
    return names, [_add_halves(f"{tag}_cores_add_{k}", partial[k], r, core) for k, r in zip(names, from_sibling)]


def _reduce_finish(tag, l, names, from_chips, core, totals):
    sums = [_sum_chips(f"{tag}_chips_add_{k}", a, core, l, totals.get(k)) for k, a in zip(names, from_chips)]
    return dict(totals, **dict(zip(names, _swap_back(tag + "_back", sums, l))))


def kernel(x, c, positions, ada_w, ada_b, norm1_w, w_in, attn_sinks, sgu_ln_w, sgu_ln_b, sgu_w, sgu_b, proj_a, proj_b, w_out, norm2_w, ffn_w_gate, ffn_w_up, ffn_conv_w, ffn_conv_b, ffn_w_down, final_norm_w, loss_target, m_ada_w, m_ada_b, m_norm1_w, m_w_in, m_attn_sinks, m_sgu_ln_w, m_sgu_ln_b, m_sgu_w, m_sgu_b, m_proj_a, m_proj_b, m_w_out, m_norm2_w, m_ffn_w_gate, m_ffn_w_up, m_ffn_conv_w, m_ffn_conv_b, m_ffn_w_down, m_final_norm_w, v_ada_w, v_ada_b, v_norm1_w, v_w_in, v_attn_sinks, v_sgu_ln_w, v_sgu_ln_b, v_sgu_w, v_sgu_b, v_proj_a, v_proj_b, v_w_out, v_norm2_w, v_ffn_w_gate, v_ffn_w_up, v_ffn_conv_w, v_ffn_conv_b, v_ffn_w_down, v_final_norm_w):
    d = D_MODEL
    ax, ay, ac = lax.axis_index("x"), lax.axis_index("y"), lax.axis_index("c")
    chip = 2 * ax + ay
    dev = 4 * ax + 2 * ay + ac
    core = ac.astype(jnp.int32).reshape(1)

    c_all = _gather_all("gather_cond", c.reshape(SUBLANES, d // SUBLANES)).reshape(2 * N_CHIPS, d)
    c_rows = _pad_to(c_all, 0, ADA_ROWS)
    ada_cols = ada_w.shape[2]
    ada_b_cols = lax.dynamic_slice_in_dim(ada_b, chip * ada_cols, ada_cols, axis=1).reshape(DEPTH, 1, ada_cols)
    mod_cols = _ada_fwd("ada_fwd", c_rows, ada_w, ada_b_cols)
    mod_all = _gather_chips("gather_mod", [mod_cols])[0]
    mod_mine = lax.dynamic_index_in_dim(mod_all, dev, axis=2, keepdims=False)
    mod_mine = mod_mine.transpose(1, 0, 2).reshape(DEPTH, 1, 6 * d)
    mods = [tuple(jnp.split(mod_mine[l], 6, axis=-1)) for l in range(DEPTH)]

    tr = lambda a: jnp.swapaxes(a, 1, 2)
    behind = mod_all[0, 0, 0, 0] * 0.0
    shards = [tr(w_in).astype(BF) + behind.astype(BF), proj_a.astype(BF), proj_b.astype(BF), w_out.astype(BF),
              _pad_to(tr(ffn_w_gate).astype(BF), 1, FFN_SHARD_PAD), _pad_to(tr(ffn_w_up).astype(BF), 1, FFN_SHARD_PAD),
              _pad_to(ffn_w_down.astype(BF), 1, FFN_SHARD_PAD), _pad_to(ffn_conv_w, 2, FFN_SHARD_PAD)]
    first, token = _gather_weights_start("l0_gather_in", [shards[0][0]])
    mix0, token = _gather_weights_start("l0_gather_mix", [shards[1][0] + token[0, 0].astype(BF), shards[2][0], shards[3][0]])
    ffn0, token = _gather_weights_start("l0_gather_ffn", [a[0] for a in shards[4:-1]] + [shards[-1][0] + token[0, 0]])
    layer1, token = _gather_weights_start("l1_gather", [a[1] for a in shards[:-1]] + [shards[-1][1] + token[0, 0]])

    small_in = dict(norm1_w=norm1_w, sinks=attn_sinks, sgu_ln_w=sgu_ln_w, sgu_ln_b=sgu_ln_b, sgu_w=sgu_w, sgu_b=sgu_b,
                    norm2_w=norm2_w, conv_b=ffn_conv_b)
    cosf, sinf = _rope_tables(positions[0])
    small_of = lambda l: {k: v[l] for k, v in small_in.items()}

    w_in0 = _gather_weights_finish("l0_gather_in", first, token)
    late = lambda y: _mix_params(*_gather_weights_finish("l0_gather_mix", mix0, y), small_of(0))
    later = lambda y: _ffn_params(*_gather_weights_finish("l0_gather_ffn", ffn0, y), small_of(0))
    h, sv, p0 = _layer_fwd(0, x[0], mods[0], _early_params(w_in0[0], small_of(0)), cosf, sinf, late=late, later=later)
    saved, params = [sv], [p0]
    p1 = _layer_params(*_gather_weights_finish("l1_gather", layer1, h), small_of(1))
    h, sv, p1 = _layer_fwd(1, h, mods[1], p1, cosf, sinf)
    saved.append(sv)
    params.append(p1)
    dx, loss_part, d_final = _loss_head("loss_head", h, final_norm_w.reshape(1, d), loss_target[0])
    loss = lax.psum(loss_part[0, 0], ("x", "y", "c"))

    def small_pack(l, grads):
        cw, cb = _conv_grads_natural(grads)
        nat = dict(grads, conv_w=cw, conv_b=cb, final_norm_w=d_final if l == DEPTH - 1 else jnp.zeros((d,), F32))
        return _pack([nat[k] for k, _ in _SMALL], N_CHIPS * SMALL_ROWS).reshape(N_CHIPS, SMALL_ROWS, LANES)

    inflight = []

    def send(tag, l, partial):
        names, sums = _reduce_cores(tag, partial, core)
        pending, token = _scatter_chips_start(tag + "_chips", sums)
        inflight.append((tag, l, names, pending))
        return token

    dmods = [None] * DEPTH
    dx, dmods[1], grads = _layer_bwd(1, dx, mods[1], params[1], saved[1], cosf, sinf)
    token = send("l1_reduce", 1, dict({k: grads[k] for k in _MATRICES}, small=small_pack(1, grads)))
    dx, dmods[0], grads = _layer_bwd(0, dx, mods[0], params[0], saved[0], cosf, sinf, after=token,
                                     emit=lambda part: send("l0_reduce_" + "_".join(part), 0, part))
    dmod_mine = jnp.concatenate([jnp.concatenate(dmods[l], axis=1) for l in range(DEPTH)], axis=1)
    dmod_all = _gather_all("gather_dmod", dmod_mine.reshape(SUBLANES, -1)).reshape(2 * N_CHIPS, DEPTH * 6 * d)
    send("l0_reduce_in", 0, dict(w_in=grads["w_in"], small=small_pack(0, grads) + dmod_all[0, 0] * 0.0))
    totals = {}
    for tag, l, names, pending in inflight[:-1]:
        totals = _reduce_finish(tag, l, names, _scatter_chips_finish(tag + "_chips", pending, dx), core, totals)

    g_ada_b = _colsum("ada_b_grad", dmod_all).reshape(DEPTH, 6 * d)
    dmod_cols = jnp.stack([lax.dynamic_slice_in_dim(dmod_all, l * 6 * d + chip * ada_cols, ada_cols, axis=1)
                           for l in range(DEPTH)])
    g_ada_w = _ada_bwd("ada_w_grad", c_rows, _pad_to(dmod_cols, 1, ADA_ROWS))
    big = dict(w_in=(tr(w_in), tr(m_w_in), tr(v_w_in)), proj_a=(proj_a, m_proj_a, v_proj_a), proj_b=(proj_b, m_proj_b, v_proj_b),
               w_out=(w_out, m_w_out, v_w_out), w_gate=(tr(ffn_w_gate), tr(m_ffn_w_gate), tr(v_ffn_w_gate)),
               w_up=(tr(ffn_w_up), tr(m_ffn_w_up), tr(v_ffn_w_up)), w_down=(ffn_w_down, m_ffn_w_down, v_ffn_w_down))
    upd, g_big = {}, {}

    def update(k):
        res = _adamw("adamw_" + k, big[k][0], totals[k], *big[k][1:])
        res = [tr(a) for a in res] if k in ("w_in", "w_gate", "w_up") else res
        g_big[k], upd[k] = res[0], res[1:]

    for k in ("w_down", "w_gate", "w_up", "w_out", "proj_a", "proj_b"):
        update(k)
    g_big["ada_w"], *upd["ada_w"] = _adamw("adamw_ada_w", ada_w, g_ada_w, m_ada_w, v_ada_w)
    tag, l, names, pending = inflight[-1]
    totals = _reduce_finish(tag, l, names, _scatter_chips_finish(tag + "_chips", pending, upd["ada_w"][0]), core, totals)
    update("w_in")

    small_all = _gather_chips("gather_small", [totals["small"]])[0]
    small_g = small_all.transpose(1, 0, 2, 3).reshape(DEPTH, -1)
    per_layer = [_unpack(small_g[l], [shp for _, shp in _SMALL]) for l in range(DEPTH)]
    sg = {k: jnp.stack([per_layer[l][i] for l in range(DEPTH)]) for i, (k, _) in enumerate(_SMALL)}
    g_final = sg["final_norm_w"][DEPTH - 1]
    g_conv_w = lax.dynamic_slice_in_dim(sg["conv_w"], chip * FFN_SHARD, FFN_SHARD, axis=2)

    rest = [("ada_b", ada_b, g_ada_b, m_ada_b, v_ada_b), ("norm1_w", norm1_w, sg["norm1_w"], m_norm1_w, v_norm1_w),
            ("attn_sinks", attn_sinks, sg["sinks"], m_attn_sinks, v_attn_sinks),
            ("sgu_ln_w", sgu_ln_w, sg["sgu_ln_w"], m_sgu_ln_w, v_sgu_ln_w),
            ("sgu_ln_b", sgu_ln_b, sg["sgu_ln_b"], m_sgu_ln_b, v_sgu_ln_b), ("sgu_w", sgu_w, sg["sgu_w"], m_sgu_w, v_sgu_w),
            ("sgu_b", sgu_b, sg["sgu_b"], m_sgu_b, v_sgu_b), ("norm2_w", norm2_w, sg["norm2_w"], m_norm2_w, v_norm2_w),
            ("ffn_conv_w", ffn_conv_w, g_conv_w, m_ffn_conv_w, v_ffn_conv_w),
            ("ffn_conv_b", ffn_conv_b, sg["conv_b"], m_ffn_conv_b, v_ffn_conv_b),
            ("final_norm_w", final_norm_w.reshape(1, d), g_final.reshape(1, d), m_final_norm_w.reshape(1, d),
             v_final_norm_w.reshape(1, d))]
    rest_out = _adamw_small("adamw_rest", *[[r[i] for r in rest] for i in (1, 2, 3, 4)])
    g_rest = {r[0]: r[2] for r in rest}
    u_rest = {r[0]: tuple(o[i] for o in rest_out) for i, r in enumerate(rest)}
    g_rest["final_norm_w"] = g_final
    u_rest["final_norm_w"] = tuple(a.reshape(d) for a in u_rest["final_norm_w"])

    names = ("ada_w", "ada_b", "norm1_w", "w_in", "attn_sinks", "sgu_ln_w", "sgu_ln_b", "sgu_w", "sgu_b", "proj_a", "proj_b",
             "w_out", "norm2_w", "ffn_w_gate", "ffn_w_up", "ffn_conv_w", "ffn_conv_b", "ffn_w_down", "final_norm_w")
    alias = {"ffn_w_gate": "w_gate", "ffn_w_up": "w_up", "ffn_w_down": "w_down"}
    grad_of = lambda n: g_rest[n] if n in g_rest else g_big[alias.get(n, n)]
    upd_of = lambda n: u_rest[n] if n in u_rest else upd[alias.get(n, n)]
    return (loss, dx[None], *[grad_of(n) for n in names], *[upd_of(n)[0] for n in names],
            *[upd_of(n)[1] for n in names], *[upd_of(n)[2] for n in names])
```

```python
import jax
import jax.numpy as jnp
from jax import lax
from jax.experimental import pallas as pl
from jax.experimental.pallas import tpu as pltpu

F32 = jnp.float32
BF = jnp.bfloat16

D_MODEL = 1024
N_Q_HEADS = 16
N_KV_HEADS = 2
HEAD_DIM = 64
ATTN_BLOCK = 128
ROPE_THETA = 500000.0
ROT_DIM = HEAD_DIM // 4
SGU_WIDTH = 1024
SGU_GROUPS = 8
SGU_CHUNK = 128
FFN_DIM = 2816
NORM_EPS = 1e-6
DEPTH = 2
IN_COLS = 5376
N_CHIPS = 4
FFN_SHARD = FFN_DIM // N_CHIPS
FFN_SHARD_PAD = 768
FFN_PAD = N_CHIPS * FFN_SHARD_PAD
LANES = 128
SUBLANES = 8
HALO = 16
VMEM_LIMIT = 56 * 1024 * 1024
NEG_BIG = -1e30

ADAM_LR = 0.001
ADAM_B1 = 0.9
ADAM_B2 = 0.999
ADAM_EPS = 1e-08
ADAM_WD = 0.01
ADAM_STEP = 10

MESH = pl.DeviceIdType.MESH

Q_END = 1024
KV_END = 1280
U_END = 2304
Z_END = 3328
GA_END = 4352


def _sds(shape, dtype):
    return jax.ShapeDtypeStruct(tuple(shape), dtype)


def _call(body, name, out_shape, grid, in_specs, out_specs, semantics, scratch=(), after=None):
    n_in = len(in_specs)
    fn = body
    if after is not None:
        def fn(*refs):
            return body(*refs[:n_in], *refs[n_in + 1:])

        in_specs = list(in_specs) + [pl.BlockSpec(memory_space=pl.ANY)]
    call = pl.pallas_call(
        fn, name=name, out_shape=out_shape, grid=grid, in_specs=in_specs, out_specs=out_specs,
        scratch_shapes=scratch,
        compiler_params=pltpu.CompilerParams(dimension_semantics=semantics, vmem_limit_bytes=VMEM_LIMIT))
    if after is None:
        return call
    return lambda *args: call(*args, after)


def _rows(tm, width, col=0):
    return pl.BlockSpec((tm, width), lambda i: (i, col))


def _vec(width):
    return pl.BlockSpec((1, width), lambda i: (0, 0))


def _resident(shape):
    zeros = (0,) * len(shape)
    return pl.BlockSpec(tuple(shape), lambda *_: zeros, pipeline_mode=pl.Buffered(1))


def _sigmoid(x):
    return 1.0 / (1.0 + jnp.exp(-x))


def _gelu(x):
    return 0.5 * x * (1.0 + lax.erf(x * 0.7071067811865476))


def _gelu_grad(x):
    cdf = 0.5 * (1.0 + lax.erf(x * 0.7071067811865476))
    return cdf + x * jnp.exp(-0.5 * x * x) * 0.3989422804014327


def _dot(a, b):
    return jnp.dot(a, b, preferred_element_type=F32)


def _dot_nt(a, b):
    return lax.dot_general(a, b, (((1,), (1,)), ((), ())), preferred_element_type=F32)


def _dot_tn(a, b):
    return lax.dot_general(a, b, (((0,), (0,)), ((), ())), preferred_element_type=F32)


def _rms(xv):
    return lax.rsqrt(jnp.mean(xv * xv, axis=-1, keepdims=True) + NORM_EPS)


def _matmul_tn(name, a, b, tk=512, tn=1024, blocked=False):
    s, k = a.shape
    n = b.shape[1]
    tk, tn = min(tk, k), min(tn, n)

    def body(a_ref, b_ref, o_ref):
        res = _dot_tn(a_ref[...], b_ref[...]).astype(o_ref.dtype)
        if blocked:
            o_ref[0] = res
        else:
            o_ref[...] = res

    if blocked:
        out, ospec = _sds((n // tn, k, tn), BF), pl.BlockSpec((1, tk, tn), lambda i, j: (j, i, 0))
    else:
        out, ospec = _sds((k, n), BF), pl.BlockSpec((tk, tn), lambda i, j: (i, j))
    return _call(body, name, out, (k // tk, n // tn),
                 [pl.BlockSpec((s, tk), lambda i, j: (0, i)), pl.BlockSpec((s, tn), lambda i, j: (0, j))],
                 ospec, ("parallel", "parallel"))(a, b)


def _matmul_tn_rows(name, a, b, out, row0, rows_total, tk=256):
    s, k = a.shape
    n = b.shape[1]

    def body(a_ref, b_ref, *rest):
        rest[-1][...] = _dot_tn(a_ref[...], b_ref[...]).astype(BF)

    in_specs = [pl.BlockSpec((s, tk), lambda i: (0, i)), _resident(b.shape)]
    args = [a, b]
    if out is not None:
        in_specs.append(pl.BlockSpec(memory_space=pl.ANY))
        args.append(out)
    return pl.pallas_call(
        body, name=name, out_shape=_sds((rows_total, n), BF), grid=(k // tk,), in_specs=in_specs,
        out_specs=pl.BlockSpec((tk, n), lambda i: (row0 // tk + i, 0)),
        input_output_aliases={2: 0} if out is not None else {},
        compiler_params=pltpu.CompilerParams(dimension_semantics=("parallel",), vmem_limit_bytes=VMEM_LIMIT))(*args)


def _rope_partner(v):
    lane = lax.broadcasted_iota(jnp.int32, (1, LANES), 1) % HEAD_DIM
    return jnp.where(lane < ROT_DIM // 2, pltpu.roll(v, LANES - ROT_DIM // 2, axis=1), pltpu.roll(v, ROT_DIM // 2, axis=1))


def _dup_half(v, half):
    lane = lax.broadcasted_iota(jnp.int32, (1, LANES), 1)
    keep = jnp.where((lane >= HEAD_DIM) == (half == 1), v, 0.0)
    return keep + pltpu.roll(keep, HEAD_DIM, axis=1)


def _in_proj(name, x, w, sc, sh, w_in, cosf, sinf, tm=256, after=None):
    s, d = x.shape
    tm = min(tm, s)

    def body(x_ref, w_ref, sc_ref, sh_ref, win_ref, cos_ref, sin_ref,
             h_ref, qr_ref, kk0_ref, kk1_ref, vv0_ref, vv1_ref, u_ref, v_ref, ga_ref, gb_ref):
        xv = x_ref[...]
        h = ((xv * _rms(xv)) * w_ref[...] * (1.0 + sc_ref[...]) + sh_ref[...]).astype(BF)
        h_ref[...] = h
        cosv, sinv = cos_ref[...], sin_ref[...]
        q = _dot_nt(h, win_ref[:Q_END, :])
        for j in range(D_MODEL // LANES):
            qv = q[:, j * LANES:(j + 1) * LANES]
            qr_ref[:, j * LANES:(j + 1) * LANES] = (qv * cosv + _rope_partner(qv) * sinv).astype(BF)
        kv = _dot_nt(h, win_ref[Q_END:KV_END, :])
        kr = kv[:, :LANES] * cosv + _rope_partner(kv[:, :LANES]) * sinv
        vv = kv[:, LANES:]
        kk0_ref[...] = _dup_half(kr, 0).astype(BF)
        kk1_ref[...] = _dup_half(kr, 1).astype(BF)
        vv0_ref[...] = _dup_half(vv, 0).astype(BF)
        vv1_ref[...] = _dup_half(vv, 1).astype(BF)
        u_ref[...] = _dot_nt(h, win_ref[KV_END:U_END, :])
        v_ref[...] = _dot_nt(h, win_ref[U_END:Z_END, :])
        ga_ref[...] = _dot_nt(h, win_ref[Z_END:GA_END, :]).astype(BF)
        gb_ref[...] = _dot_nt(h, win_ref[GA_END:, :]).astype(BF)

    wide, kvs, pre = _sds((s, d), BF), _sds((s, LANES), BF), _sds((s, d), F32)
    return _call(body, name, (wide, wide, kvs, kvs, kvs, kvs, pre, pre, wide, wide), (s // tm,),
                 [_rows(tm, d), _vec(d), _vec(d), _vec(d), _resident(w_in.shape), _rows(tm, LANES), _rows(tm, LANES)],
                 (_rows(tm, d), _rows(tm, d)) + (_rows(tm, LANES),) * 4 + (_rows(tm, d),) * 4, ("parallel",), after=after)(
                     x, w, sc, sh, w_in, cosf, sinf)


def _in_proj_bwd(name, dq, dkv, du, dv, dga, dgb, w_in, x, w, sc, dx_in, tm=256):
    s, d = x.shape
    tm = min(tm, s)

    def body(dq_ref, dkv_ref, du_ref, dv_ref, dga_ref, dgb_ref, win_ref, x_ref, w_ref, sc_ref, dxin_ref,
             dx_ref, da_ref, dsh_ref):
        @pl.when(pl.program_id(0) == 0)
        def _():
            da_ref[...] = jnp.zeros_like(da_ref)
            dsh_ref[...] = jnp.zeros_like(dsh_ref)

        dh = (_dot(dq_ref[...], win_ref[:Q_END, :]) + _dot(dkv_ref[...], win_ref[Q_END:KV_END, :])
              + _dot(du_ref[...], win_ref[KV_END:U_END, :]) + _dot(dv_ref[...], win_ref[U_END:Z_END, :])
              + _dot(dga_ref[...], win_ref[Z_END:GA_END, :]) + _dot(dgb_ref[...], win_ref[GA_END:, :]))
        xv = x_ref[...]
        r = _rms(xv)
        xn = xv * r
        dxn = dh * (w_ref[...] * (1.0 + sc_ref[...]))
        dx_ref[...] = dxin_ref[...] + r * (dxn - xn * jnp.mean(dxn * xn, axis=-1, keepdims=True))
        da_ref[...] += jnp.sum(dh * xn, axis=0, keepdims=True)
        dsh_ref[...] += jnp.sum(dh, axis=0, keepdims=True)

    return _call(body, name, (_sds((s, d), F32), _sds((1, d), F32), _sds((1, d), F32)), (s // tm,),
                 [_rows(tm, d), _rows(tm, 2 * LANES), _rows(tm, d), _rows(tm, d), _rows(tm, d), _rows(tm, d),
                  _resident(w_in.shape), _rows(tm, d), _vec(d), _vec(d), _rows(tm, d)],
                 (_rows(tm, d), _vec(d), _vec(d)), ("arbitrary",))(dq, dkv, du, dv, dga, dgb, w_in, x, w, sc, dx_in)


def _rope_bwd(name, dqr, dkv_cur, dkv_prev, cosf, sinf, tm=512):
    s = dqr.shape[0]
    tm = min(tm, s)
    steps = s // tm
    per = tm // ATTN_BLOCK
    nb = s // ATTN_BLOCK

    def unrope(v, cosv, sinv):
        return v * cosv - _rope_partner(v) * sinv

    def body(dq_ref, cur_ref, prev_ref, next_ref, cos_ref, sin_ref, dqo_ref, dkvo_ref):
        i = pl.program_id(0)
        cosv, sinv = cos_ref[...], sin_ref[...]
        for j in range(D_MODEL // LANES):
            dqo_ref[:, j * LANES:(j + 1) * LANES] = unrope(dq_ref[:, j * LANES:(j + 1) * LANES], cosv, sinv).astype(BF)
        nxt = jnp.where(i < steps - 1, next_ref[...], 0.0)
        if per > 1:
            shifted = jnp.concatenate([prev_ref[ATTN_BLOCK:, :], nxt], axis=0)
        else:
            shifted = nxt
        tot = cur_ref[...] + shifted
        dkvo_ref[:, :LANES] = unrope(tot[:, :LANES], cosv, sinv).astype(BF)
        dkvo_ref[:, LANES:] = tot[:, LANES:].astype(BF)

    nxt_spec = pl.BlockSpec((ATTN_BLOCK, 2 * LANES), lambda i: (jnp.minimum((i + 1) * per, nb - 1), 0))
    return _call(body, name, (_sds((s, D_MODEL), BF), _sds((s, 2 * LANES), BF)), (steps,),
                 [_rows(tm, D_MODEL), _rows(tm, 2 * LANES), _rows(tm, 2 * LANES), nxt_spec, _rows(tm, LANES),
                  _rows(tm, LANES)],
                 (_rows(tm, D_MODEL), _rows(tm, 2 * LANES)), ("parallel",))(dqr, dkv_cur, dkv_prev, dkv_prev, cosf, sinf)


Q_PER_KV = N_Q_HEADS // N_KV_HEADS


def _band_mask_t(n):
    kj = lax.broadcasted_iota(jnp.int32, (2 * ATTN_BLOCK, ATTN_BLOCK), 0)
    qi = lax.broadcasted_iota(jnp.int32, (2 * ATTN_BLOCK, ATTN_BLOCK), 1)
    return (kj > qi) & (kj <= qi + ATTN_BLOCK) & ((n > 0) | (kj >= ATTN_BLOCK))


def _softmax_t(raw, allowed, sink):
    sc = jnp.where(allowed, raw * (HEAD_DIM ** -0.5), NEG_BIG)
    m = jnp.maximum(jnp.max(sc, axis=0, keepdims=True), sink)
    p = jnp.exp(sc - m)
    esink = jnp.exp(sink - m)
    inv = 1.0 / (jnp.sum(p, axis=0, keepdims=True) + esink)
    return p * inv, esink * inv


def _kv_specs():
    cur = pl.BlockSpec((ATTN_BLOCK, LANES), lambda n: (n, 0))
    prev = pl.BlockSpec((ATTN_BLOCK, LANES), lambda n: (jnp.maximum(n - 1, 0), 0))
    return [prev, cur] * 4


def _attention(name, qr, kk0, kk1, vv0, vv1, sinks):
    s = qr.shape[0]
    nb = s // ATTN_BLOCK

    def body(sink_ref, q_ref, k0p, k0c, k1p, k1c, v0p, v0c, v1p, v1c, y_ref):
        allowed = _band_mask_t(pl.program_id(0))
        upper = lax.broadcasted_iota(jnp.int32, (1, LANES), 1) >= HEAD_DIM
        upper_rows = lax.broadcasted_iota(jnp.int32, (LANES, 1), 0) >= HEAD_DIM
        bands = ((jnp.concatenate([k0p[...], k0c[...]], axis=0), jnp.concatenate([v0p[...], v0c[...]], axis=0)),
                 (jnp.concatenate([k1p[...], k1c[...]], axis=0), jnp.concatenate([v1p[...], v1c[...]], axis=0)))
        vbts = (bands[0][1].T, bands[1][1].T)

        def scores(h):
            hk, j, half = h // Q_PER_KV, (h % Q_PER_KV) // 2, h % 2
            col = (hk * 4 + j) * LANES
            qp = q_ref[:, col:col + LANES]
            return _dot_nt(bands[hk][0], jnp.where(upper if half else jnp.logical_not(upper), qp, jnp.zeros_like(qp)))

        out_t = None
        ahead = scores(0)
        for h in range(N_Q_HEADS):
            hk, j, half = h // Q_PER_KV, (h % Q_PER_KV) // 2, h % 2
            raw = ahead
            if h + 1 < N_Q_HEADS:
                ahead = scores(h + 1)
            pn, _ = _softmax_t(raw, allowed, sink_ref[h])
            o_h = _dot(vbts[hk], pn.astype(BF))
            out_t = jnp.where(upper_rows, o_h, out_t) if half else o_h
            if half:
                col = (hk * 4 + j) * LANES
                y_ref[:, col:col + LANES] = out_t.T.astype(BF)

    return _call(body, name, _sds((s, D_MODEL), BF), (nb,),
                 [pl.BlockSpec(memory_space=pltpu.SMEM), pl.BlockSpec((ATTN_BLOCK, D_MODEL), lambda n: (n, 0))] + _kv_specs(),
                 pl.BlockSpec((ATTN_BLOCK, D_MODEL), lambda n: (n, 0)), ("parallel",))(
                     sinks, qr, kk0, kk0, kk1, kk1, vv0, vv0, vv1, vv1)


def _attention_bwd(name, qr, kk0, kk1, vv0, vv1, sinks, dy):
    s = qr.shape[0]
    nb = s // ATTN_BLOCK

    def body(sink_ref, q_ref, dy_ref, k0p, k0c, k1p, k1c, v0p, v0c, v1p, v1c, dq_ref, cur_ref, prev_ref, dsink_ref):
        @pl.when(pl.program_id(0) == 0)
        def _():
            dsink_ref[...] = jnp.zeros_like(dsink_ref)

        allowed = _band_mask_t(pl.program_id(0))
        lane = lax.broadcasted_iota(jnp.int32, (1, LANES), 1)
        upper = lane >= HEAD_DIM
        upper_rows = lax.broadcasted_iota(jnp.int32, (LANES, 1), 0) >= HEAD_DIM
        bands = ((jnp.concatenate([k0p[...], k0c[...]], axis=0), jnp.concatenate([v0p[...], v0c[...]], axis=0)),
                 (jnp.concatenate([k1p[...], k1c[...]], axis=0), jnp.concatenate([v1p[...], v1c[...]], axis=0)))
        kbts = (bands[0][0].T, bands[1][0].T)

        def scores(h):
            hk, j, half = h // Q_PER_KV, (h % Q_PER_KV) // 2, h % 2
            kb, vb = bands[hk]
            col = (hk * 4 + j) * LANES
            sel = upper if half else jnp.logical_not(upper)
            qp = q_ref[:, col:col + LANES]
            qa = jnp.where(sel, qp, jnp.zeros_like(qp))
            dya = jnp.where(sel, dy_ref[:, col:col + LANES], 0.0).astype(BF)
            return qa, dya, _dot_nt(kb, qa), _dot_nt(vb, dya)

        dsink = jnp.zeros((1, LANES), F32)
        dk_slab = jnp.zeros((2 * ATTN_BLOCK, LANES), F32)
        dv_slab = jnp.zeros((2 * ATTN_BLOCK, LANES), F32)
        dkk = dvv = dq_t = None
        ahead = scores(0)
        for h in range(N_Q_HEADS):
            hk, j, half = h // Q_PER_KV, (h % Q_PER_KV) // 2, h % 2
            qa, dya, raw, dp = ahead
            if h + 1 < N_Q_HEADS:
                ahead = scores(h + 1)
            pn, psink = _softmax_t(raw, allowed, sink_ref[h])
            delta = jnp.sum(pn * dp, axis=0, keepdims=True)
            ds = (pn * (dp - delta) * (HEAD_DIM ** -0.5)).astype(BF)
            dsink = dsink + jnp.where(lane == h, -jnp.sum(psink * delta), 0.0)
            dq_h = _dot(kbts[hk], ds)
            dq_t = jnp.where(upper_rows, dq_h, dq_t) if half else dq_h
            dk_h, dv_h = _dot(ds, qa), _dot(pn.astype(BF), dya)
            dkk, dvv = (dk_h, dv_h) if h % Q_PER_KV == 0 else (dkk + dk_h, dvv + dv_h)
            if half:
                col = (hk * 4 + j) * LANES
                dq_ref[:, col:col + LANES] = dq_t.T
            if h % Q_PER_KV == Q_PER_KV - 1:
                mine = upper if hk else jnp.logical_not(upper)
                dk_slab = jnp.where(mine, dkk + pltpu.roll(dkk, HEAD_DIM, axis=1), dk_slab)
                dv_slab = jnp.where(mine, dvv + pltpu.roll(dvv, HEAD_DIM, axis=1), dv_slab)
        prev_ref[:, :LANES] = dk_slab[:ATTN_BLOCK]
        prev_ref[:, LANES:] = dv_slab[:ATTN_BLOCK]
        cur_ref[:, :LANES] = dk_slab[ATTN_BLOCK:]
        cur_ref[:, LANES:] = dv_slab[ATTN_BLOCK:]
        dsink_ref[...] += dsink

    blk = pl.BlockSpec((ATTN_BLOCK, D_MODEL), lambda n: (n, 0))
    kvo = pl.BlockSpec((ATTN_BLOCK, 2 * LANES), lambda n: (n, 0))
    return _call(body, name,
                 (_sds((s, D_MODEL), F32), _sds((s, 2 * LANES), F32), _sds((s, 2 * LANES), F32), _sds((1, LANES), F32)),
                 (nb,), [pl.BlockSpec(memory_space=pltpu.SMEM), blk, blk] + _kv_specs(),
                 (blk, kvo, kvo, pl.BlockSpec((1, LANES), lambda n: (0, 0))), ("arbitrary",))(
                     sinks, qr, dy, kk0, kk0, kk1, kk1, vv0, vv0, vv1, vv1)


def _sgu_weights(wm_ref, g):
    t = lax.broadcasted_iota(jnp.int32, (SGU_CHUNK, SGU_CHUNK), 0)
    sidx = lax.broadcasted_iota(jnp.int32, (SGU_CHUNK, SGU_CHUNK), 1)
    return jnp.where(sidx <= t, wm_ref[g], 0.0).astype(BF)


def _layer_norm_stats(v):
    mu = jnp.mean(v, axis=-1, keepdims=True)
    cen = v - mu
    rstd = lax.rsqrt(jnp.mean(cen * cen, axis=-1, keepdims=True) + NORM_EPS)
    return cen * rstd, rstd


def _sgu(name, u_pre, v_pre, ln_w, ln_b, wm, bfull, tm=256):
    s, w = u_pre.shape
    tm = min(tm, s)

    def body(u_ref, v_ref, lw_ref, lb_ref, wm_ref, b_ref, y_ref):
        vhat, _ = _layer_norm_stats(_gelu(v_ref[...]))
        vn = (vhat * lw_ref[...] + lb_ref[...]).astype(BF)
        for g in range(SGU_GROUPS):
            wg = _sgu_weights(wm_ref, g)
            cols = slice(g * SGU_CHUNK, (g + 1) * SGU_CHUNK)
            for ch in range(tm // SGU_CHUNK):
                rows = slice(ch * SGU_CHUNK, (ch + 1) * SGU_CHUNK)
                f = _dot(wg, vn[rows, cols]) + b_ref[g]
                y_ref[rows, cols] = (_gelu(u_ref[rows, cols]) * f).astype(BF)

    full3 = pl.BlockSpec((SGU_GROUPS, SGU_CHUNK, SGU_CHUNK), lambda i: (0, 0, 0))
    return _call(body, name, _sds((s, w), BF), (s // tm,),
                 [_rows(tm, w), _rows(tm, w), _vec(w), _vec(w), full3, full3],
                 _rows(tm, w), ("parallel",))(u_pre, v_pre, ln_w, ln_b, wm, bfull)


def _sgu_bwd(name, u_pre, v_pre, ln_w, ln_b, wm, bfull, dy, tm=256, after=None):
    s, w = u_pre.shape
    tm = min(tm, s)
    steps = s // tm

    def body(u_ref, v_ref, lw_ref, lb_ref, wm_ref, b_ref, dy_ref, du_ref, dv_ref, dwm_ref, db_ref, dlw_ref, dlb_ref,
             dfsum_ref):
        i = pl.program_id(0)

        @pl.when(i == 0)
        def _():
            dwm_ref[...] = jnp.zeros_like(dwm_ref)
            dlw_ref[...] = jnp.zeros_like(dlw_ref)
            dlb_ref[...] = jnp.zeros_like(dlb_ref)
            dfsum_ref[...] = jnp.zeros_like(dfsum_ref)

        vpre = v_ref[...]
        vhat, rstd = _layer_norm_stats(_gelu(vpre))
        vn = (vhat * lw_ref[...] + lb_ref[...]).astype(BF)
        t = lax.broadcasted_iota(jnp.int32, (SGU_CHUNK, SGU_CHUNK), 0)
        sidx = lax.broadcasted_iota(jnp.int32, (SGU_CHUNK, SGU_CHUNK), 1)
        dvn_cols = []
        for g in range(SGU_GROUPS):
            wg = _sgu_weights(wm_ref, g)
            cols = slice(g * SGU_CHUNK, (g + 1) * SGU_CHUNK)
            dvn_rows = []
            dwg = jnp.zeros((SGU_CHUNK, SGU_CHUNK), F32)
            dfs = jnp.zeros((SGU_CHUNK, SGU_CHUNK), F32)
            for ch in range(tm // SGU_CHUNK):
                rows = slice(ch * SGU_CHUNK, (ch + 1) * SGU_CHUNK)
                upre = u_ref[rows, cols]
                dyv = dy_ref[rows, cols].astype(F32)
                f = _dot(wg, vn[rows, cols]) + b_ref[g]
                du_ref[rows, cols] = (dyv * f * _gelu_grad(upre)).astype(BF)
                df = dyv * _gelu(upre)
                dfb = df.astype(BF)
                dvn_rows.append(_dot_tn(wg, dfb))
                dwg = dwg + _dot_nt(dfb, vn[rows, cols])
                dfs = dfs + df
            dwm_ref[g] += jnp.where(sidx <= t, dwg, 0.0)
            dfsum_ref[g] += dfs
            dvn_cols.append(jnp.concatenate(dvn_rows, axis=0) if len(dvn_rows) > 1 else dvn_rows[0])
        dvn = jnp.concatenate(dvn_cols, axis=1)
        dlw_ref[...] += jnp.sum(dvn * vhat, axis=0, keepdims=True)
        dlb_ref[...] += jnp.sum(dvn, axis=0, keepdims=True)
        dvh = dvn * lw_ref[...]
        dvg = rstd * (dvh - jnp.mean(dvh, axis=-1, keepdims=True) - vhat * jnp.mean(dvh * vhat, axis=-1, keepdims=True))
        dv_ref[...] = (dvg * _gelu_grad(vpre)).astype(BF)

        @pl.when(i == steps - 1)
        def _():
            for g in range(SGU_GROUPS):
                db_ref[g:g + 1, :] = jnp.sum(dfsum_ref[g].T, axis=0, keepdims=True)

    full3 = pl.BlockSpec((SGU_GROUPS, SGU_CHUNK, SGU_CHUNK), lambda i: (0, 0, 0))
    return _call(body, name,
                 (_sds((s, w), BF), _sds((s, w), BF), _sds((SGU_GROUPS, SGU_CHUNK, SGU_CHUNK), F32),
                  _sds((SGU_GROUPS, SGU_CHUNK), F32), _sds((1, w), F32), _sds((1, w), F32)),
                 (steps,),
                 [_rows(tm, w), _rows(tm, w), _vec(w), _vec(w), full3, full3, _rows(tm, w)],
                 (_rows(tm, w), _rows(tm, w), full3, pl.BlockSpec((SGU_GROUPS, SGU_CHUNK), lambda i: (0, 0)), _vec(w), _vec(w)),
                 ("arbitrary",), scratch=[pltpu.VMEM((SGU_GROUPS, SGU_CHUNK, SGU_CHUNK), F32)], after=after)(
                     u_pre, v_pre, ln_w, ln_b, wm, bfull, dy)


def _mix_out(name, y_sgu, y_attn, ga_pre, gb_pre, x, g1, proj_a, proj_b, w_out, w2, sc2, sh2, tm=256):
    s, d = x.shape
    tm = min(tm, s)

    def body(ys_ref, ya_ref, ga_ref, gb_ref, x_ref, g1_ref, wa_ref, wb_ref, wo_ref, w2_ref, sc2_ref, sh2_ref,
             m_ref, pa_ref, pb_ref, o_ref, x1_ref, h2_ref):
        pa = _dot(ys_ref[...], wa_ref[...].reshape(d, d))
        pb = _dot(ya_ref[...], wb_ref[...].reshape(d, d))
        pa_ref[...] = pa.astype(BF)
        pb_ref[...] = pb.astype(BF)
        merged = (_sigmoid(ga_ref[...].astype(F32)) * pa + _sigmoid(gb_ref[...].astype(F32)) * pb).astype(BF)
        m_ref[...] = merged
        o = _dot(merged, wo_ref[...].reshape(d, d))
        o_ref[...] = o.astype(BF)
        x1 = x_ref[...] + g1_ref[...] * o
        x1_ref[...] = x1
        h2_ref[...] = ((x1 * _rms(x1)) * w2_ref[...] * (1.0 + sc2_ref[...]) + sh2_ref[...]).astype(BF)

    f, b = _sds((s, d), F32), _sds((s, d), BF)
    r = _rows(tm, d)
    wspec = _resident(proj_a.shape)
    return _call(body, name, (b, b, b, b, f, b), (s // tm,),
                 [r, r, r, r, r, _vec(d), wspec, wspec, wspec, _vec(d), _vec(d), _vec(d)], (r,) * 6, ("parallel",))(
                     y_sgu, y_attn, ga_pre, gb_pre, x, g1, proj_a, proj_b, w_out, w2, sc2, sh2)


def _mix_bwd(name, do, w_out, proj_a, proj_b, ga_pre, gb_pre, pa, pb, tm=256):
    s, d = do.shape
    tm = min(tm, s)

    def body(do_ref, wo_ref, wa_ref, wb_ref, ga_ref, gb_ref, pa_ref, pb_ref,
             dpa_ref, dpb_ref, dga_ref, dgb_ref, dys_ref, dya_ref):
        dm = _dot_nt(do_ref[...], wo_ref[...].reshape(d, d))
        ga = _sigmoid(ga_ref[...].astype(F32))
        gb = _sigmoid(gb_ref[...].astype(F32))
        dpa = (dm * ga).astype(BF)
        dpb = (dm * gb).astype(BF)
        dpa_ref[...] = dpa
        dpb_ref[...] = dpb
        dga_ref[...] = (dm * pa_ref[...].astype(F32) * ga * (1.0 - ga)).astype(BF)
        dgb_ref[...] = (dm * pb_ref[...].astype(F32) * gb * (1.0 - gb)).astype(BF)
        dys_ref[...] = _dot_nt(dpa, wa_ref[...].reshape(d, d)).astype(BF)
        dya_ref[...] = _dot_nt(dpb, wb_ref[...].reshape(d, d)).astype(BF)

    f, b = _sds((s, d), F32), _sds((s, d), BF)
    r = _rows(tm, d)
    wspec = _resident(w_out.shape)
    return _call(body, name, (b, b, b, b, b, b), (s // tm,), [r, wspec, wspec, wspec, r, r, r, r], (r,) * 6,
                 ("parallel",))(do, w_out, proj_a, proj_b, ga_pre, gb_pre, pa, pb)


def _ffn_up(name, h2, w_gate, w_up, tm=1024):
    s, d = h2.shape
    tm = min(tm, s)
    tc = FFN_SHARD_PAD

    def body(h_ref, wg_ref, wu_ref, a_ref, up_ref):
        hv = h_ref[...]
        a_ref[...] = _dot_nt(hv, wg_ref[0]).astype(BF)
        up_ref[...] = _dot_nt(hv, wu_ref[0]).astype(BF)

    wspec = pl.BlockSpec((1, tc, d), lambda j, i: (j, 0, 0))
    ospec = pl.BlockSpec((tm, tc), lambda j, i: (i, j))
    o = _sds((s, FFN_PAD), BF)
    return _call(body, name, (o, o), (N_CHIPS, s // tm), [pl.BlockSpec((tm, d), lambda j, i: (i, 0)), wspec, wspec],
                 (ospec, ospec), ("parallel", "parallel"))(h2, w_gate, w_up)


def _conv_act(name, a, up, cw, cb, tm=512, tc=FFN_SHARD_PAD):
    s, c = a.shape
    tm = min(tm, s)
    per = tm // HALO

    def body(a_ref, prev_ref, up_ref, cw_ref, cb_ref, h_ref):
        prev = jnp.where(pl.program_id(1) > 0, prev_ref[...].astype(F32), 0.0)
        ext = jnp.concatenate([prev, a_ref[...].astype(F32)], axis=0)
        ac = (cb_ref[...] + cw_ref[0:1, :] * pltpu.roll(ext, 2, axis=0) + cw_ref[1:2, :] * pltpu.roll(ext, 1, axis=0)
              + cw_ref[2:3, :] * ext)[HALO:]
        h_ref[...] = (ac * _sigmoid(ac) * up_ref[...].astype(F32)).astype(BF)

    tile = pl.BlockSpec((tm, tc), lambda j, i: (i, j))
    prev = pl.BlockSpec((HALO, tc), lambda j, i: (jnp.maximum(i * per - 1, 0), j))
    return _call(body, name, _sds((s, c), BF), (c // tc, s // tm),
                 [tile, prev, tile, pl.BlockSpec((3, tc), lambda j, i: (0, j)), pl.BlockSpec((1, tc), lambda j, i: (0, j))],
                 tile, ("parallel", "parallel"))(a, a, up, cw, cb)


def _conv_act_bwd(name, a, up, dhf, cw, cb, tm=512, tc=FFN_SHARD_PAD):
    s, c = a.shape
    tm = min(tm, s)
    per = tm // HALO
    steps = s // tm
    last = s // HALO - 1

    def body(a_ref, aprev_ref, anext_ref, up_ref, upnext_ref, dh_ref, dhnext_ref, cw_ref, cb_ref,
             da_ref, dup_ref, dcw_ref, dcb_ref):
        i = pl.program_id(1)

        @pl.when(i == 0)
        def _():
            dcw_ref[...] = jnp.zeros_like(dcw_ref)
            dcb_ref[...] = jnp.zeros_like(dcb_ref)

        prev = jnp.where(i > 0, aprev_ref[...].astype(F32), 0.0)
        ext = jnp.concatenate([prev, a_ref[...].astype(F32), anext_ref[...].astype(F32)], axis=0)
        a1 = pltpu.roll(ext, 1, axis=0)[HALO:]
        a2 = pltpu.roll(ext, 2, axis=0)[HALO:]
        a0 = ext[HALO:]
        ac = cb_ref[...] + cw_ref[0:1, :] * a2 + cw_ref[1:2, :] * a1 + cw_ref[2:3, :] * a0
        sig = _sigmoid(ac)
        dh = jnp.concatenate([dh_ref[...].astype(F32), jnp.where(i < steps - 1, dhnext_ref[...].astype(F32), 0.0)], axis=0)
        upe = jnp.concatenate([up_ref[...].astype(F32), upnext_ref[...].astype(F32)], axis=0)
        dac = dh * upe * (sig * (1.0 + ac * (1.0 - sig)))
        dup_ref[...] = (dh[:tm] * (ac * sig)[:tm]).astype(BF)
        n = tm + HALO
        da = (cw_ref[2:3, :] * dac + cw_ref[1:2, :] * pltpu.roll(dac, n - 1, axis=0)
              + cw_ref[0:1, :] * pltpu.roll(dac, n - 2, axis=0))
        da_ref[...] = da[:tm].astype(BF)
        dact = dac[:tm]
        dcb_ref[...] += jnp.sum(dact, axis=0, keepdims=True)
        dcw_ref[0:1, :] += jnp.sum(dact * a2[:tm], axis=0, keepdims=True)
        dcw_ref[1:2, :] += jnp.sum(dact * a1[:tm], axis=0, keepdims=True)
        dcw_ref[2:3, :] += jnp.sum(dact * a0[:tm], axis=0, keepdims=True)

    tile = pl.BlockSpec((tm, tc), lambda j, i: (i, j))
    prev = pl.BlockSpec((HALO, tc), lambda j, i: (jnp.maximum(i * per - 1, 0), j))
    nxt = pl.BlockSpec((HALO, tc), lambda j, i: (jnp.minimum((i + 1) * per, last), j))
    cw_spec = pl.BlockSpec((3, tc), lambda j, i: (0, j))
    cb_spec = pl.BlockSpec((1, tc), lambda j, i: (0, j))
    return _call(body, name, (_sds((s, c), BF), _sds((s, c), BF), _sds((3, c), F32), _sds((1, c), F32)),
                 (c // tc, steps), [tile, prev, nxt, tile, nxt, tile, nxt, cw_spec, cb_spec],
                 (tile, tile, cw_spec, cb_spec), ("parallel", "arbitrary"))(a, a, a, up, up, dhf, dhf, cw, cb)


def _ffn_down(name, hf, w_down, x1, g2, tm=512):
    s, d = x1.shape
    tm = min(tm, s)

    def body(hf_ref, wd_ref, x1_ref, g2_ref, dn_ref, x2_ref):
        dn = _dot(hf_ref[...], wd_ref[...].reshape(FFN_PAD, d))
        dn_ref[...] = dn.astype(BF)
        x2_ref[...] = x1_ref[...] + g2_ref[...] * dn

    return _call(body, name, (_sds((s, d), BF), _sds((s, d), F32)), (s // tm,),
                 [_rows(tm, FFN_PAD), _resident(w_down.shape), _rows(tm, d), _vec(d)],
                 (_rows(tm, d), _rows(tm, d)), ("parallel",))(hf, w_down, x1, g2)


def _ffn_down_bwd(name, dx2, dn, g2, w_down, tm=512, after=None):
    s, d = dx2.shape
    tm = min(tm, s)
    tc = FFN_SHARD_PAD

    def body(dx_ref, dn_ref, g2_ref, wd_ref, ddn_ref, dhf_ref, dg_ref):
        @pl.when(pl.program_id(0) == 0)
        def _():
            dg_ref[...] = jnp.zeros_like(dg_ref)

        dxv = dx_ref[...]
        ddn = (dxv * g2_ref[...]).astype(BF)
        ddn_ref[...] = ddn
        dg_ref[...] += jnp.sum(dxv * dn_ref[...].astype(F32), axis=0, keepdims=True)
        for k in range(N_CHIPS):
            dhf_ref[:, k * tc:(k + 1) * tc] = _dot_nt(ddn, wd_ref[k]).astype(BF)

    return _call(body, name, (_sds((s, d), BF), _sds((s, FFN_PAD), BF), _sds((1, d), F32)), (s // tm,),
                 [_rows(tm, d), _rows(tm, d), _vec(d), _resident(w_down.shape)],
                 (_rows(tm, d), _rows(tm, FFN_PAD), _vec(d)), ("arbitrary",), after=after)(dx2, dn, g2, w_down)


def _ffn_up_bwd(name, da, dup, w_gate, w_up, x1, dx2, w2, sc2, o, g1, tm=256, after=None):
    s, d = x1.shape
    tm = min(tm, s)
    tc = FFN_SHARD_PAD

    def body(da_ref, dup_ref, wg_ref, wu_ref, x1_ref, dx2_ref, w2_ref, sc2_ref, o_ref, g1_ref,
             dx1_ref, do_ref, dnw_ref, dsh_ref, dg1_ref):
        @pl.when(pl.program_id(0) == 0)
        def _():
            dnw_ref[...] = jnp.zeros_like(dnw_ref)
            dsh_ref[...] = jnp.zeros_like(dsh_ref)
            dg1_ref[...] = jnp.zeros_like(dg1_ref)

        dh = jnp.zeros((tm, d), F32)
        for k in range(N_CHIPS):
            cols = slice(k * tc, (k + 1) * tc)
            dh = dh + _dot(da_ref[:, cols], wg_ref[k]) + _dot(dup_ref[:, cols], wu_ref[k])
        xv = x1_ref[...]
        r = _rms(xv)
        xn = xv * r
        dxn = dh * (w2_ref[...] * (1.0 + sc2_ref[...]))
        dx1 = dx2_ref[...] + r * (dxn - xn * jnp.mean(dxn * xn, axis=-1, keepdims=True))
        dx1_ref[...] = dx1
        dnw_ref[...] += jnp.sum(dh * xn, axis=0, keepdims=True)
        dsh_ref[...] += jnp.sum(dh, axis=0, keepdims=True)
        do_ref[...] = (dx1 * g1_ref[...]).astype(BF)
        dg1_ref[...] += jnp.sum(dx1 * o_ref[...].astype(F32), axis=0, keepdims=True)

    v = _sds((1, d), F32)
    r = _rows(tm, d)
    wspec = _resident(w_gate.shape)
    return _call(body, name, (_sds((s, d), F32), _sds((s, d), BF), v, v, v), (s // tm,),
                 [_rows(tm, FFN_PAD), _rows(tm, FFN_PAD), wspec, wspec, r, r, _vec(d), _vec(d), r, _vec(d)],
                 (r, r, _vec(d), _vec(d), _vec(d)), ("arbitrary",), after=after)(da, dup, w_gate, w_up, x1, dx2, w2, sc2, o, g1)


def _loss_head(name, x, w, target, tm=512):
    s, d = x.shape
    tm = min(tm, s)

    def body(x_ref, w_ref, t_ref, dx_ref, loss_ref, dw_ref):
        @pl.when(pl.program_id(0) == 0)
        def _():
            loss_ref[...] = jnp.zeros_like(loss_ref)
            dw_ref[...] = jnp.zeros_like(dw_ref)

        xv = x_ref[...]
        r = _rms(xv)
        xn = xv * r
        err = xn * w_ref[...] - t_ref[...]
        loss_ref[...] += 0.5 * jnp.sum(jnp.mean(err * err, axis=-1, keepdims=True))
        dy = err * (1.0 / d)
        dw_ref[...] += jnp.sum(dy * xn, axis=0, keepdims=True)
        dxn = dy * w_ref[...]
        dx_ref[...] = r * (dxn - xn * jnp.mean(dxn * xn, axis=-1, keepdims=True))

    return _call(body, name, (_sds((s, d), F32), _sds((1, LANES), F32), _sds((1, d), F32)), (s // tm,),
                 [_rows(tm, d), _vec(d), _rows(tm, d)], (_rows(tm, d), _vec(LANES), _vec(d)), ("arbitrary",))(x, w, target)


def _layer_fwd(l, x, mod, p, cosf, sinf, after=None, late=None, later=None):
    sh1, sc1, g1, sh2, sc2, g2 = mod
    tag = f"l{l}_"
    h, qr, kk0, kk1, vv0, vv1, u_pre, v_pre, ga_pre, gb_pre = _in_proj(
        tag + "in_proj", x, p["norm1_w"], sc1, sh1, p["w_in"], cosf, sinf, after=after)
    y_attn = _attention(tag + "attn", qr, kk0, kk1, vv0, vv1, p["sinks"])
    y_sgu = _sgu(tag + "sgu", u_pre, v_pre, p["sgu_ln_w"], p["sgu_ln_b"], p["sgu_w"], p["sgu_bfull"])
    if late is not None:
        p = dict(p, **late(y_sgu))
    merged, pa, pb, o, x1, h2 = _mix_out(tag + "mix_out", y_sgu, y_attn, ga_pre, gb_pre, x, g1, p["proj_a"], p["proj_b"],
                                         p["w_out"], p["norm2_w"], sc2, sh2)
    if later is not None:
        p = dict(p, **later(h2))
    a, up = _ffn_up(tag + "ffn_up", h2, p["w_gate"], p["w_up"])
    hf = _conv_act(tag + "conv_act", a, up, p["conv_w"], p["conv_b"])
    dn, x2 = _ffn_down(tag + "ffn_down", hf, p["w_down"], x1, g2)
    saved = dict(x=x, h=h, qr=qr, kk0=kk0, kk1=kk1, vv0=vv0, vv1=vv1, u_pre=u_pre, v_pre=v_pre, ga_pre=ga_pre,
                 gb_pre=gb_pre, y_attn=y_attn, y_sgu=y_sgu, merged=merged, pa=pa, pb=pb, o=o, x1=x1, h2=h2, a=a, up=up,
                 hf=hf, dn=dn)
    return x2, saved, p


def _layer_bwd(l, dx2, mod, p, sv, cosf, sinf, after=None, emit=None):
    sh1, sc1, g1, sh2, sc2, g2 = mod
    tag = f"l{l}_b_"
    d = D_MODEL
    g = {}
    ready = (lambda names: emit({k: g.pop(k) for k in names})) if emit else (lambda names: None)
    ddn, dhf, dg2 = _ffn_down_bwd(tag + "ffn_down", dx2, sv["dn"], g2, p["w_down"], after=after)
    g["w_down"] = _matmul_tn(tag + "dw_down", sv["hf"], ddn, tk=FFN_SHARD_PAD).reshape(N_CHIPS, FFN_SHARD_PAD, d)
    da, dup, g["conv_w"], g["conv_b"] = _conv_act_bwd(tag + "conv_act", sv["a"], sv["up"], dhf, p["conv_w"], p["conv_b"])
    g["w_gate"] = _matmul_tn(tag + "dw_gate", da, sv["h2"], tk=FFN_SHARD_PAD).reshape(N_CHIPS, FFN_SHARD_PAD, d)
    g["w_up"] = _matmul_tn(tag + "dw_up", dup, sv["h2"], tk=FFN_SHARD_PAD).reshape(N_CHIPS, FFN_SHARD_PAD, d)
    dx1, do, da2, dsh2, dg1 = _ffn_up_bwd(tag + "ffn_up", da, dup, p["w_gate"], p["w_up"], sv["x1"], dx2, p["norm2_w"],
                                          sc2, sv["o"], g1, after=ready(("w_down", "w_gate", "w_up")))
    g["norm2_w"] = da2 * (1.0 + sc2)
    dsc2 = da2 * p["norm2_w"]
    g["w_out"] = _matmul_tn(tag + "dw_out", sv["merged"], do).reshape(N_CHIPS, d // N_CHIPS, d)
    dpa, dpb, dga, dgb, dy_sgu, dy_attn = _mix_bwd(tag + "mix", do, p["w_out"], p["proj_a"], p["proj_b"], sv["ga_pre"],
                                                  sv["gb_pre"], sv["pa"], sv["pb"])
    g["proj_a"] = _matmul_tn(tag + "dproj_a", sv["y_sgu"], dpa).reshape(N_CHIPS, d // N_CHIPS, d)
    g["proj_b"] = _matmul_tn(tag + "dproj_b", sv["y_attn"], dpb).reshape(N_CHIPS, d // N_CHIPS, d)
    du, dv, g["sgu_w"], g["sgu_b"], g["sgu_ln_w"], g["sgu_ln_b"] = _sgu_bwd(
        tag + "sgu", sv["u_pre"], sv["v_pre"], p["sgu_ln_w"], p["sgu_ln_b"], p["sgu_w"], p["sgu_bfull"], dy_sgu,
        after=ready(("w_out", "proj_a", "proj_b")))
    dqr, dkv_cur, dkv_prev, dsink = _attention_bwd(tag + "attn", sv["qr"], sv["kk0"], sv["kk1"], sv["vv0"], sv["vv1"],
                                                   p["sinks"], dy_attn)
    g["sinks"] = dsink[0, :N_Q_HEADS]
    dq, dkv = _rope_bwd(tag + "rope", dqr, dkv_cur, dkv_prev, cosf, sinf)
    dw_in, row0 = None, 0
    for n, t in (("q", dq), ("kv", dkv), ("u", du), ("v", dv), ("ga", dga), ("gb", dgb)):
        dw_in = _matmul_tn_rows(tag + "dw_in_" + n, t, sv["h"], dw_in, row0, IN_COLS)
        row0 += t.shape[1]
    g["w_in"] = dw_in.reshape(N_CHIPS, IN_COLS // N_CHIPS, d)
    dx, da1, dsh1 = _in_proj_bwd(tag + "in_proj", dq, dkv, du, dv, dga, dgb, p["w_in"], sv["x"], p["norm1_w"], sc1, dx1)
    g["norm1_w"] = da1 * (1.0 + sc1)
    dsc1 = da1 * p["norm1_w"]
    return dx, (dsh1, dsc1, dg1, dsh2, dsc2, dg2), g


def _pad_to(a, axis, size):
    pad = [(0, 0)] * a.ndim
    pad[axis] = (0, size - a.shape[axis])
    return jnp.pad(a, pad)


def _early_params(w_in, small):
    d = D_MODEL
    return dict(
        w_in=w_in.reshape(IN_COLS, d), norm1_w=small["norm1_w"].reshape(1, d), sinks=small["sinks"],
        sgu_ln_w=small["sgu_ln_w"].reshape(1, d), sgu_ln_b=small["sgu_ln_b"].reshape(1, d), sgu_w=small["sgu_w"],
        sgu_bfull=jnp.broadcast_to(small["sgu_b"][:, :, None], (SGU_GROUPS, SGU_CHUNK, SGU_CHUNK)))


def _mix_params(proj_a, proj_b, w_out, small):
    return dict(proj_a=proj_a, proj_b=proj_b, w_out=w_out, norm2_w=small["norm2_w"].reshape(1, D_MODEL))


def _ffn_params(w_gate, w_up, w_down, conv_w, small):
    return dict(
        w_gate=w_gate, w_up=w_up, w_down=w_down, conv_w=conv_w.transpose(1, 0, 2).reshape(3, FFN_PAD),
        conv_b=_pad_to(small["conv_b"].reshape(N_CHIPS, FFN_SHARD), 1, FFN_SHARD_PAD).reshape(1, FFN_PAD))


def _layer_params(w_in, proj_a, proj_b, w_out, w_gate, w_up, w_down, conv_w, small):
    return dict(_early_params(w_in, small), **_mix_params(proj_a, proj_b, w_out, small),
                **_ffn_params(w_gate, w_up, w_down, conv_w, small))


def _conv_grads_natural(g):
    cw = g["conv_w"].reshape(3, N_CHIPS, FFN_SHARD_PAD)[:, :, :FFN_SHARD].reshape(3, FFN_DIM)
    cb = g["conv_b"].reshape(N_CHIPS, FFN_SHARD_PAD)[:, :FFN_SHARD].reshape(FFN_DIM)
    return cw, cb


def _rope_tables(positions):
    inv_freq = ROPE_THETA ** (-jnp.arange(0, ROT_DIM, 2, dtype=F32) / ROT_DIM)
    ang = positions.astype(F32)[:, None] * inv_freq
    cos, sin = jnp.cos(ang), jnp.sin(ang)
    s = positions.shape[0]
    rest = HEAD_DIM - ROT_DIM
    cos_head = jnp.concatenate([cos, cos, jnp.ones((s, rest), F32)], axis=1)
    sin_head = jnp.concatenate([-sin, sin, jnp.zeros((s, rest), F32)], axis=1)
    return jnp.tile(cos_head, (1, LANES // HEAD_DIM)), jnp.tile(sin_head, (1, LANES // HEAD_DIM))


ADA_ROWS = 16


def _ada_fwd(name, c_rows, ada_w, ada_b_cols, tn=512):
    depth, d, n = ada_w.shape

    def body(c_ref, w_ref, b_ref, o_ref):
        cv = c_ref[...]
        act = (cv * _sigmoid(cv)).astype(BF)
        o_ref[0] = _dot(act, w_ref[0].astype(BF)) + b_ref[0]

    return _call(body, name, _sds((depth, ADA_ROWS, n), F32), (depth, n // tn),
                 [pl.BlockSpec((ADA_ROWS, d), lambda l, j: (0, 0)), pl.BlockSpec((1, d, tn), lambda l, j: (l, 0, j)),
                  pl.BlockSpec((1, 1, tn), lambda l, j: (l, 0, j))],
                 pl.BlockSpec((1, ADA_ROWS, tn), lambda l, j: (l, 0, j)), ("parallel", "parallel"))(c_rows, ada_w, ada_b_cols)


def _ada_bwd(name, c_rows, dmod_cols, tn=512):
    depth, _, n = dmod_cols.shape
    d = c_rows.shape[1]

    def body(c_ref, dm_ref, o_ref):
        cv = c_ref[...]
        act = (cv * _sigmoid(cv)).astype(BF)
        o_ref[0] = _dot_tn(act, dm_ref[0].astype(BF))

    return _call(body, name, _sds((depth, d, n), F32), (depth, n // tn),
                 [pl.BlockSpec((ADA_ROWS, d), lambda l, j: (0, 0)), pl.BlockSpec((1, ADA_ROWS, tn), lambda l, j: (l, 0, j))],
                 pl.BlockSpec((1, d, tn), lambda l, j: (l, 0, j)), ("parallel", "parallel"))(c_rows, dmod_cols)


def _colsum(name, a):
    r, n = a.shape

    def body(a_ref, o_ref):
        o_ref[...] = jnp.sum(a_ref[...], axis=0, keepdims=True)

    return _call(body, name, _sds((1, n), F32), (1,), [pl.BlockSpec((r, n), lambda i: (0, 0))],
                 pl.BlockSpec((1, n), lambda i: (0, 0)), ("arbitrary",))(a)


REL_SIBLING = (0, 0, 1)
REL_CHIPS = ((1, 0, 0), (0, 1, 0), (1, 1, 0))
REL_ALL = tuple((fx, fy, fc) for fx in (0, 1) for fy in (0, 1) for fc in (0, 1) if fx or fy or fc)


def _chip_of(dev):
    return 2 * dev[0] + dev[1]


def _dev_of(dev):
    return 4 * dev[0] + 2 * dev[1] + dev[2]


def _flip(dev, rel):
    return tuple(1 - m if f else m for m, f in zip(dev, rel))


def _exchange(name, arrays, n_out, stages, aliases=None):
    out_shapes, stages = stages[0], stages[1:]
    n_in = len(arrays)
    aliases = aliases or {}
    n_remote = sum(len(plan) for plan, _ in stages)
    n_local = sum(len(local) for _, local in stages)

    def at(ref, idx):
        return ref.at[idx] if len(idx) else ref

    def body(*refs):
        bufs = list(refs[:n_in + n_out])
        for i_in, i_out in aliases.items():
            bufs[i_in] = bufs[n_in + i_out]
        send_sems, recv_sems, local_sems = refs[n_in + n_out:]
        me = (lax.axis_index("x"), lax.axis_index("y"), lax.axis_index("c"))
        base_r = base_l = 0
        pending = []
        for plan, local in stages:
            def remote(k, entry, sender, receiver):
                rel, si, ssel, di, dsel = entry
                return pltpu.make_async_remote_copy(
                    src_ref=at(bufs[si], ssel(sender, receiver)), dst_ref=at(bufs[di], dsel(sender, receiver)),
                    send_sem=send_sems.at[k], recv_sem=recv_sems.at[k], device_id=_flip(me, rel), device_id_type=MESH)

            sends = [remote(base_r + k, e, me, _flip(me, e[0])) for k, e in enumerate(plan)]
            for cp in sends:
                cp.start()
            for k, (si, ssel, di, dsel) in enumerate(local):
                cp = pltpu.make_async_copy(at(bufs[si], ssel(me)), at(bufs[di], dsel(me)), local_sems.at[base_l + k])
                cp.start()
                pending.append(cp.wait)
            for k, e in enumerate(plan):
                remote(base_r + k, e, _flip(me, e[0]), me).wait_recv()
            pending += [cp.wait_send for cp in sends]
            base_r += len(plan)
            base_l += len(local)
        for wait in pending:
            wait()

    any_spec = pl.BlockSpec(memory_space=pl.ANY)
    return pl.pallas_call(
        body, name=name, out_shape=tuple(out_shapes), in_specs=[any_spec] * n_in, out_specs=tuple([any_spec] * n_out),
        input_output_aliases=dict(aliases),
        scratch_shapes=[pltpu.SemaphoreType.DMA((max(n_remote, 1),)), pltpu.SemaphoreType.DMA((max(n_remote, 1),)),
                        pltpu.SemaphoreType.DMA((max(n_local, 1),))])(*arrays)


HBM_SPEC = pl.BlockSpec(memory_space=pltpu.HBM)
SEM_SPEC = pl.BlockSpec(memory_space=pltpu.SEMAPHORE)


def _split_copies(bufs, plan, local, send_sems, recv_sems, local_sems):
    me = (lax.axis_index("x"), lax.axis_index("y"), lax.axis_index("c"))

    def at(ref, idx):
        return ref.at[idx] if len(idx) else ref

    def remote(k, sender, receiver):
        rel, si, ssel, di, dsel = plan[k]
        return pltpu.make_async_remote_copy(
            src_ref=at(bufs[si], ssel(sender, receiver)), dst_ref=at(bufs[di], dsel(sender, receiver)),
            send_sem=send_sems.at[k], recv_sem=recv_sems.at[k], device_id=_flip(me, rel), device_id_type=MESH)

    sends = [remote(k, me, _flip(me, plan[k][0])) for k in range(len(plan))]
    arrivals = [remote(k, _flip(me, plan[k][0]), me) for k in range(len(plan))]
    locs = [pltpu.make_async_copy(at(bufs[si], ssel(me)), at(bufs[di], dsel(me)), local_sems.at[k])
            for k, (si, ssel, di, dsel) in enumerate(local)]
    return sends, arrivals, locs


def _exchange_start(name, arrays, out_shapes, plan, local):
    n_in, n_out = len(arrays), len(out_shapes)

    n_buf = n_in + n_out

    def body(*refs):
        sems = refs[n_buf:n_buf + 3]
        bufs = refs[n_buf + 3:2 * n_buf + 3]
        token = refs[-1]
        sends, _, locs = _split_copies(bufs, plan, local, *sems)
        for cp in sends + locs:
            cp.start()
        token[...] = jnp.zeros_like(token)

    zones = [lax.empty(o.shape, o.dtype) for o in out_shapes]
    operands = [pltpu.with_memory_space_constraint(a, pltpu.HBM) for a in list(arrays) + zones]
    sem = lambda n: pltpu.SemaphoreType.DMA((max(n, 1),))
    out = pl.pallas_call(
        body, name=name,
        out_shape=(sem(len(plan)), sem(len(plan)), sem(len(local)), *[pltpu.HBM(a.shape, a.dtype) for a in operands],
                   _sds((SUBLANES, LANES), F32)),
        in_specs=[HBM_SPEC] * (n_in + n_out),
        out_specs=(SEM_SPEC, SEM_SPEC, SEM_SPEC, *[HBM_SPEC] * (n_in + n_out), pl.BlockSpec(memory_space=pltpu.VMEM)),
        input_output_aliases={i: 3 + i for i in range(n_in + n_out)},
        compiler_params=pltpu.CompilerParams(has_side_effects=pltpu.SideEffectType.DATAFLOW_SIDE_EFFECTING))(*operands)
    return out[:3], out[3:3 + n_in], out[3 + n_in:3 + n_in + n_out], out[-1]


def _exchange_wait(name, sems, thru, zones, plan, local, after):
    n_in, n_out = len(thru), len(zones)

    def body(*refs):
        bufs = refs[:n_in + n_out]
        sends, arrivals, locs = _split_copies(bufs, plan, local, *refs[n_in + n_out:n_in + n_out + 3])
        for cp in arrivals:
            cp.wait_recv()
        for cp in sends:
            cp.wait_send()
        for cp in locs:
            cp.wait()

    out = pl.pallas_call(
        body, name=name, out_shape=tuple(pltpu.HBM(a.shape, a.dtype) for a in list(thru) + list(zones)),
        in_specs=[HBM_SPEC] * (n_in + n_out) + [SEM_SPEC] * 3 + [pl.BlockSpec(memory_space=pl.ANY)],
        out_specs=tuple([HBM_SPEC] * (n_in + n_out)), input_output_aliases={i: i for i in range(n_in + n_out)},
        compiler_params=pltpu.CompilerParams(has_side_effects=pltpu.SideEffectType.DATAFLOW_SIDE_EFFECTING))(
            *thru, *zones, *sems, after)
    return out[n_in:]


def _whole(*_):
    return ()


def _half_rows(rows, core):
    return pl.ds(core * (rows // 2), rows // 2)


def _gather_weights_plan(shards):
    n = len(shards)
    dsts = [_sds((N_CHIPS,) + a.shape, a.dtype) for a in shards]
    fetch, forward = [], []
    for t, a in enumerate(shards):
        rows = a.shape[0]
        if rows % (2 * 16) == 0:
            fetch += [(rel, t, (lambda s_, r_, rows=rows: (_half_rows(rows, s_[2]),)), n + t,
                       (lambda s_, r_, rows=rows: (_chip_of(s_), _half_rows(rows, s_[2])))) for rel in REL_CHIPS]
            forward += [(REL_SIBLING, n + t, (lambda s_, r_, rows=rows, rel=rel: (_chip_of(_flip(s_, rel)), _half_rows(rows, s_[2]))),
                         n + t, (lambda s_, r_, rows=rows, rel=rel: (_chip_of(_flip(s_, rel)), _half_rows(rows, s_[2]))))
                        for rel in REL_CHIPS]
        else:
            fetch += [(rel, t, _whole, n + t, lambda s_, r_: (_chip_of(s_),)) for rel in REL_CHIPS]
    local = [(t, _whole, n + t, lambda me: (_chip_of(me),)) for t in range(n)]
    return dsts, fetch, local, forward


def _gather_weights_start(name, shards):
    dsts, fetch, local, forward = _gather_weights_plan(shards)
    sems, thru, zones, token = _exchange_start(name, shards, dsts, fetch, local)
    return (sems, thru, zones, fetch, local, forward), token


def _gather_weights_finish(name, pending, after):
    sems, thru, zones, fetch, local, forward = pending
    landed = _exchange_wait(name + "_wait", sems, thru, zones, fetch, local, after)
    n = len(landed)
    return _exchange(name + "_forward", landed, n, [[_sds(a.shape, a.dtype) for a in landed], (forward, [])],
                     aliases={t: t for t in range(n)})


def _gather_chips(name, arrays):
    n = len(arrays)
    dsts = [_sds((N_CHIPS,) + a.shape, a.dtype) for a in arrays]
    plan = [(rel, t, _whole, n + t, lambda s_, r_: (_chip_of(s_),)) for t in range(n) for rel in REL_CHIPS]
    local = [(t, _whole, n + t, lambda me: (_chip_of(me),)) for t in range(n)]
    return _exchange(name, arrays, n, [dsts, (plan, local)])


def _gather_all(name, a):
    plan = [(rel, 0, _whole, 1, lambda s_, r_: (_dev_of(s_),)) for rel in REL_ALL]
    local = [(0, _whole, 1, lambda me: (_dev_of(me),))]
    return _exchange(name, [a], 1, [[_sds((2 * N_CHIPS,) + a.shape, a.dtype)], (plan, local)])[0]


def _swap_halves(name, grads):
    n = len(grads)
    dsts = [_sds((g.shape[0], g.shape[1] // 2, g.shape[2]), g.dtype) for g in grads]
    plan = [(REL_SIBLING, t, (lambda s_, r_, rows=g.shape[1]: (pl.ds(0, N_CHIPS), _half_rows(rows, r_[2]))), n + t, _whole)
            for t, g in enumerate(grads)]
    return _exchange(name, grads, n, [dsts, (plan, [])])


def _scatter_chips_plan(sums):
    n = len(sums)
    dsts = [_sds(a.shape, a.dtype) for a in sums]
    plan = [(rel, t, lambda s_, r_: (_chip_of(r_),), n + t, lambda s_, r_: (_chip_of(s_),))
            for t in range(n) for rel in REL_CHIPS]
    local = [(t, lambda me: (_chip_of(me),), n + t, lambda me: (_chip_of(me),)) for t in range(n)]
    return dsts, plan, local


def _scatter_chips(name, sums):
    dsts, plan, local = _scatter_chips_plan(sums)
    return _exchange(name, sums, len(sums), [dsts, (plan, local)])


def _scatter_chips_start(name, sums):
    dsts, plan, local = _scatter_chips_plan(sums)
    sems, thru, zones, token = _exchange_start(name, sums, dsts, plan, local)
    return (sems, thru, zones, plan, local), token


def _scatter_chips_finish(name, pending, after):
    sems, thru, zones, plan, local = pending
    return _exchange_wait(name + "_wait", sems, thru, zones, plan, local, after)


def _swap_back(name, totals, layer):
    n = len(totals)
    dsts = [_sds(a.shape, a.dtype) for a in totals]
    plan = [(REL_SIBLING, n + t, (lambda s_, r_, rows=a.shape[1]: (layer, _half_rows(rows, s_[2]))),
             n + t, (lambda s_, r_, rows=a.shape[1]: (layer, _half_rows(rows, s_[2])))) for t, a in enumerate(totals)]
    return _exchange(name, totals, n, [dsts, (plan, [])], aliases={t: t for t in range(n)})


def _add_halves(name, g, recv, core):
    nch, half, c = recv.shape

    def body(core_ref, g_ref, r_ref, o_ref):
        o_ref[0] = (g_ref[0, 0].astype(F32) + r_ref[0].astype(F32)).astype(o_ref.dtype)

    spec = pltpu.PrefetchScalarGridSpec(
        num_scalar_prefetch=1, grid=(nch,),
        in_specs=[pl.BlockSpec((1, 1, half, c), lambda k, core_ref: (k, core_ref[0], 0, 0)),
                  pl.BlockSpec((1, half, c), lambda k, core_ref: (k, 0, 0))],
        out_specs=pl.BlockSpec((1, half, c), lambda k, core_ref: (k, 0, 0)))
    return pl.pallas_call(body, name=name, out_shape=_sds(recv.shape, recv.dtype), grid_spec=spec,
                          compiler_params=pltpu.CompilerParams(dimension_semantics=("parallel",),
                                                               vmem_limit_bytes=VMEM_LIMIT))(
                                                                   core, g.reshape(nch, 2, half, c), recv)


def _sum_chips(name, a, core, layer, total):
    nch, half, c = a.shape

    def body(core_ref, a_ref, *rest):
        o_ref = rest[-1]
        acc = a_ref[0].astype(F32)
        for k in range(1, nch):
            acc = acc + a_ref[k].astype(F32)
        o_ref[0, 0] = acc

    in_specs = [pl.BlockSpec((nch, half, c), lambda i, core_ref: (0, 0, 0))]
    args = [core, a]
    if total is not None:
        in_specs.append(pl.BlockSpec(memory_space=pl.ANY))
        args.append(total.reshape(DEPTH, 2, half, c))
    spec = pltpu.PrefetchScalarGridSpec(
        num_scalar_prefetch=1, grid=(1,), in_specs=in_specs,
        out_specs=pl.BlockSpec((1, 1, half, c), lambda i, core_ref: (layer, core_ref[0], 0, 0)))
    out = pl.pallas_call(body, name=name, out_shape=_sds((DEPTH, 2, half, c), F32), grid_spec=spec,
                         input_output_aliases={2: 0} if total is not None else {},
                         compiler_params=pltpu.CompilerParams(dimension_semantics=("arbitrary",),
                                                              vmem_limit_bytes=VMEM_LIMIT))(*args)
    return out.reshape(DEPTH, 2 * half, c)


def _adamw_update(w, g, m, v):
    mn = ADAM_B1 * m + (1.0 - ADAM_B1) * g
    vn = ADAM_B2 * v + (1.0 - ADAM_B2) * (g * g)
    m_hat = mn / (1.0 - ADAM_B1 ** ADAM_STEP)
    v_hat = vn / (1.0 - ADAM_B2 ** ADAM_STEP)
    return -ADAM_LR * (m_hat / (jnp.sqrt(v_hat) + ADAM_EPS) + ADAM_WD * w), mn, vn


def _adamw(name, w, g, m, v):
    depth, r, c = w.shape
    tr = next(t for t in (512, 448, 384, 352, 336, 256, 192, 128, 64, 32, 16, 8) if r % t == 0 and t * c <= ADAM_TILE_ELEMS)

    def body(w_ref, g_ref, m_ref, v_ref, go_ref, d_ref, mo_ref, vo_ref):
        gv = g_ref[...]
        go_ref[...] = gv
        d_ref[...], mo_ref[...], vo_ref[...] = _adamw_update(w_ref[...], gv, m_ref[...], v_ref[...])

    spec = pl.BlockSpec((1, tr, c), lambda l, i: (l, i, 0))
    o = _sds(w.shape, F32)
    return _call(body, name, (o, o, o, o), (depth, r // tr), [spec] * 4, (spec,) * 4, ("parallel", "parallel"))(w, g, m, v)


def _adamw_small(name, ws, gs, ms, vs):
    n = len(ws)

    def body(*refs):
        for t in range(n):
            w_ref, g_ref, m_ref, v_ref = (refs[k * n + t] for k in range(4))
            d_ref, mo_ref, vo_ref = (refs[(4 + k) * n + t] for k in range(3))
            d_ref[...], mo_ref[...], vo_ref[...] = _adamw_update(w_ref[...], g_ref[...], m_ref[...], v_ref[...])

    outs = [_sds(w.shape, F32) for w in ws]
    res = pl.pallas_call(body, name=name, out_shape=tuple(outs * 3))(*ws, *gs, *ms, *vs)
    return res[:n], res[n:2 * n], res[2 * n:]


def _pack(arrays, rows):
    flat = jnp.concatenate([a.reshape(-1).astype(F32) for a in arrays])
    return _pad_to(flat, 0, rows * LANES).reshape(rows, LANES)


def _unpack(packed, shapes):
    flat = packed.reshape(-1)
    out, off = [], 0
    for shp in shapes:
        n = 1
        for s_ in shp:
            n *= s_
        out.append(flat[off:off + n].reshape(shp))
        off += n
    return out


_MATRICES = ("w_in", "proj_a", "proj_b", "w_out", "w_gate", "w_up", "w_down")
_SMALL = (("norm1_w", (D_MODEL,)), ("sinks", (N_Q_HEADS,)), ("sgu_ln_w", (SGU_WIDTH,)), ("sgu_ln_b", (SGU_WIDTH,)),
          ("sgu_w", (SGU_GROUPS, SGU_CHUNK, SGU_CHUNK)), ("sgu_b", (SGU_GROUPS, SGU_CHUNK)), ("norm2_w", (D_MODEL,)),
          ("conv_w", (3, FFN_DIM)), ("conv_b", (FFN_DIM,)), ("final_norm_w", (D_MODEL,)))
SMALL_ROWS = 320
ADAM_TILE_ELEMS = 384 * 1024


def _reduce_cores(tag, partial, core):
    names = list(partial)
    from_sibling = _swap_halves(tag + "_cores", [partial[k] for k in names])
    return names, [_add_halves(f"{tag}_cores_add_{k}", partial[k], r, core) for k, r in zip(names, from_sibling)]


def _reduce_finish(tag, l, names, from_chips, core, totals):
    sums = [_sum_chips(f"{tag}_chips_add_{k}", a, core, l, totals.get(k)) for k, a in zip(names, from_chips)]
    return dict(totals, **dict(zip(names, _swap_back(tag + "_back", sums, l))))


def kernel(x, c, positions, ada_w, ada_b, norm1_w, w_in, attn_sinks, sgu_ln_w, sgu_ln_b, sgu_w, sgu_b, proj_a, proj_b, w_out, norm2_w, ffn_w_gate, ffn_w_up, ffn_conv_w, ffn_conv_b, ffn_w_down, final_norm_w, loss_target, m_ada_w, m_ada_b, m_norm1_w, m_w_in, m_attn_sinks, m_sgu_ln_w, m_sgu_ln_b, m_sgu_w, m_sgu_b, m_proj_a, m_proj_b, m_w_out, m_norm2_w, m_ffn_w_gate, m_ffn_w_up, m_ffn_conv_w, m_ffn_conv_b, m_ffn_w_down, m_final_norm_w, v_ada_w, v_ada_b, v_norm1_w, v_w_in, v_attn_sinks, v_sgu_ln_w, v_sgu_ln_b, v_sgu_w, v_sgu_b, v_proj_a, v_proj_b, v_w_out, v_norm2_w, v_ffn_w_gate, v_ffn_w_up, v_ffn_conv_w, v_ffn_conv_b, v_ffn_w_down, v_final_norm_w):
    d = D_MODEL
    ax, ay, ac = lax.axis_index("x"), lax.axis_index("y"), lax.axis_index("c")
    chip = 2 * ax + ay
    dev = 4 * ax + 2 * ay + ac
    core = ac.astype(jnp.int32).reshape(1)

    c_all = _gather_all("gather_cond", c.reshape(SUBLANES, d // SUBLANES)).reshape(2 * N_CHIPS, d)
    c_rows = _pad_to(c_all, 0, ADA_ROWS)
    ada_cols = ada_w.shape[2]
    ada_b_cols = lax.dynamic_slice_in_dim(ada_b, chip * ada_cols, ada_cols, axis=1).reshape(DEPTH, 1, ada_cols)
    mod_cols = _ada_fwd("ada_fwd", c_rows, ada_w, ada_b_cols)
    mod_all = _gather_chips("gather_mod", [mod_cols])[0]
    mod_mine = lax.dynamic_index_in_dim(mod_all, dev, axis=2, keepdims=False)
    mod_mine = mod_mine.transpose(1, 0, 2).reshape(DEPTH, 1, 6 * d)
    mods = [tuple(jnp.split(mod_mine[l], 6, axis=-1)) for l in range(DEPTH)]

    tr = lambda a: jnp.swapaxes(a, 1, 2)
    shards = [tr(w_in).astype(BF), proj_a.astype(BF), proj_b.astype(BF), w_out.astype(BF),
              _pad_to(tr(ffn_w_gate).astype(BF), 1, FFN_SHARD_PAD), _pad_to(tr(ffn_w_up).astype(BF), 1, FFN_SHARD_PAD),
              _pad_to(ffn_w_down.astype(BF), 1, FFN_SHARD_PAD), _pad_to(ffn_conv_w, 2, FFN_SHARD_PAD)]
    token = mod_all[0, 0, :SUBLANES, :LANES]
    fetches = []
    for l in range(DEPTH):
        groups = []
        for tag, members in (("in", shards[:1]), ("mix", shards[1:4]), ("ffn", shards[4:])):
            behind = (token[0, 0] * 0.0).astype(members[0].dtype)
            pending, token = _gather_weights_start(f"l{l}_gather_{tag}", [members[0][l] + behind] + [a[l] for a in members[1:]])
            groups.append(pending)
        fetches.append(groups)

    small_in = dict(norm1_w=norm1_w, sinks=attn_sinks, sgu_ln_w=sgu_ln_w, sgu_ln_b=sgu_ln_b, sgu_w=sgu_w, sgu_b=sgu_b,
                    norm2_w=norm2_w, conv_b=ffn_conv_b)
    cosf, sinf = _rope_tables(positions[0])
    small_of = lambda l: {k: v[l] for k, v in small_in.items()}

    h = x[0]
    saved, params = [], []
    for l in range(DEPTH):
        first, mix, ffn = fetches[l]
        w_in_l = _gather_weights_finish(f"l{l}_gather_in", first, token if l == 0 else h)
        late = lambda y, l=l, mix=mix: _mix_params(*_gather_weights_finish(f"l{l}_gather_mix", mix, y), small_of(l))
        later = lambda y, l=l, ffn=ffn: _ffn_params(*_gather_weights_finish(f"l{l}_gather_ffn", ffn, y), small_of(l))
        h, sv, p = _layer_fwd(l, h, mods[l], _early_params(w_in_l[0], small_of(l)), cosf, sinf, late=late, later=later)
        saved.append(sv)
        params.append(p)
    dx, loss_part, d_final = _loss_head("loss_head", h, final_norm_w.reshape(1, d), loss_target[0])
    loss = lax.psum(loss_part[0, 0], ("x", "y", "c"))

    def small_pack(l, grads):
        cw, cb = _conv_grads_natural(grads)
        nat = dict(grads, conv_w=cw, conv_b=cb, final_norm_w=d_final if l == DEPTH - 1 else jnp.zeros((d,), F32))
        return _pack([nat[k] for k, _ in _SMALL], N_CHIPS * SMALL_ROWS).reshape(N_CHIPS, SMALL_ROWS, LANES)

    inflight = []

    def send(tag, l, partial):
        names, sums = _reduce_cores(tag, partial, core)
        pending, token = _scatter_chips_start(tag + "_chips", sums)
        inflight.append((tag, l, names, pending))
        return token

    dmods = [None] * DEPTH
    dx, dmods[1], grads = _layer_bwd(1, dx, mods[1], params[1], saved[1], cosf, sinf)
    token = send("l1_reduce", 1, dict({k: grads[k] for k in _MATRICES}, small=small_pack(1, grads)))
    dx, dmods[0], grads = _layer_bwd(0, dx, mods[0], params[0], saved[0], cosf, sinf, after=token,
                                     emit=lambda part: send("l0_reduce_" + "_".join(part), 0, part))
    dmod_mine = jnp.concatenate([jnp.concatenate(dmods[l], axis=1) for l in range(DEPTH)], axis=1)
    dmod_all = _gather_all("gather_dmod", dmod_mine.reshape(SUBLANES, -1)).reshape(2 * N_CHIPS, DEPTH * 6 * d)
    send("l0_reduce_in", 0, dict(w_in=grads["w_in"], small=small_pack(0, grads) + dmod_all[0, 0] * 0.0))
    totals = {}
    for tag, l, names, pending in inflight[:-1]:
        totals = _reduce_finish(tag, l, names, _scatter_chips_finish(tag + "_chips", pending, dx), core, totals)

    g_ada_b = _colsum("ada_b_grad", dmod_all).reshape(DEPTH, 6 * d)
    dmod_cols = jnp.stack([lax.dynamic_slice_in_dim(dmod_all, l * 6 * d + chip * ada_cols, ada_cols, axis=1)
                           for l in range(DEPTH)])
    g_ada_w = _ada_bwd("ada_w_grad", c_rows, _pad_to(dmod_cols, 1, ADA_ROWS))
    big = dict(w_in=(tr(w_in), tr(m_w_in), tr(v_w_in)), proj_a=(proj_a, m_proj_a, v_proj_a), proj_b=(proj_b, m_proj_b, v_proj_b),
               w_out=(w_out, m_w_out, v_w_out), w_gate=(tr(ffn_w_gate), tr(m_ffn_w_gate), tr(v_ffn_w_gate)),
               w_up=(tr(ffn_w_up), tr(m_ffn_w_up), tr(v_ffn_w_up)), w_down=(ffn_w_down, m_ffn_w_down, v_ffn_w_down))
    upd, g_big = {}, {}

    def update(k):
        res = _adamw("adamw_" + k, big[k][0], totals[k], *big[k][1:])
        res = [tr(a) for a in res] if k in ("w_in", "w_gate", "w_up") else res
        g_big[k], upd[k] = res[0], res[1:]

    for k in ("w_down", "w_gate", "w_up", "w_out", "proj_a", "proj_b"):
        update(k)
    g_big["ada_w"], *upd["ada_w"] = _adamw("adamw_ada_w", ada_w, g_ada_w, m_ada_w, v_ada_w)
    tag, l, names, pending = inflight[-1]
    totals = _reduce_finish(tag, l, names, _scatter_chips_finish(tag + "_chips", pending, upd["ada_w"][0]), core, totals)
    update("w_in")

    small_all = _gather_chips("gather_small", [totals["small"]])[0]
    small_g = small_all.transpose(1, 0, 2, 3).reshape(DEPTH, -1)
    per_layer = [_unpack(small_g[l], [shp for _, shp in _SMALL]) for l in range(DEPTH)]
    sg = {k: jnp.stack([per_layer[l][i] for l in range(DEPTH)]) for i, (k, _) in enumerate(_SMALL)}
    g_final = sg["final_norm_w"][DEPTH - 1]
    g_conv_w = lax.dynamic_slice_in_dim(sg["conv_w"], chip * FFN_SHARD, FFN_SHARD, axis=2)

    rest = [("ada_b", ada_b, g_ada_b, m_ada_b, v_ada_b), ("norm1_w", norm1_w, sg["norm1_w"], m_norm1_w, v_norm1_w),
            ("attn_sinks", attn_sinks, sg["sinks"], m_attn_sinks, v_attn_sinks),
            ("sgu_ln_w", sgu_ln_w, sg["sgu_ln_w"], m_sgu_ln_w, v_sgu_ln_w),
            ("sgu_ln_b", sgu_ln_b, sg["sgu_ln_b"], m_sgu_ln_b, v_sgu_ln_b), ("sgu_w", sgu_w, sg["sgu_w"], m_sgu_w, v_sgu_w),
            ("sgu_b", sgu_b, sg["sgu_b"], m_sgu_b, v_sgu_b), ("norm2_w", norm2_w, sg["norm2_w"], m_norm2_w, v_norm2_w),
            ("ffn_conv_w", ffn_conv_w, g_conv_w, m_ffn_conv_w, v_ffn_conv_w),
            ("ffn_conv_b", ffn_conv_b, sg["conv_b"], m_ffn_conv_b, v_ffn_conv_b),
            ("final_norm_w", final_norm_w.reshape(1, d), g_final.reshape(1, d), m_final_norm_w.reshape(1, d),
             v_final_norm_w.reshape(1, d))]
    rest_out = _adamw_small("adamw_rest", *[[r[i] for r in rest] for i in (1, 2, 3, 4)])
    g_rest = {r[0]: r[2] for r in rest}
    u_rest = {r[0]: tuple(o[i] for o in rest_out) for i, r in enumerate(rest)}
    g_rest["final_norm_w"] = g_final
    u_rest["final_norm_w"] = tuple(a.reshape(d) for a in u_rest["final_norm_w"])

    names = ("ada_w", "ada_b", "norm1_w", "w_in", "attn_sinks", "sgu_ln_w", "sgu_ln_b", "sgu_w", "sgu_b", "proj_a", "proj_b",
             "w_out", "norm2_w", "ffn_w_gate", "ffn_w_up", "ffn_conv_w", "ffn_conv_b", "ffn_w_down", "final_norm_w")
    alias = {"ffn_w_gate": "w_gate", "ffn_w_up": "w_up", "ffn_w_down": "w_down"}
    grad_of = lambda n: g_rest[n] if n in g_rest else g_big[alias.get(n, n)]
    upd_of = lambda n: u_rest[n] if n in u_rest else upd[alias.get(n, n)]
    return (loss, dx[None], *[grad_of(n) for n in names], *[upd_of(n)[0] for n in names],
            *[upd_of(n)[1] for n in names], *[upd_of(n)[2] for n in names])
```

```python
import jax
import jax.numpy as jnp
from jax import lax
from jax.experimental import pallas as pl
from jax.experimental.pallas import tpu as pltpu

F32 = jnp.float32
BF = jnp.bfloat16

D_MODEL = 1024
N_Q_HEADS = 16
N_KV_HEADS = 2
HEAD_DIM = 64
ATTN_BLOCK = 128
ROPE_THETA = 500000.0
ROT_DIM = HEAD_DIM // 4
SGU_WIDTH = 1024
SGU_GROUPS = 8
SGU_CHUNK = 128
FFN_DIM = 2816
NORM_EPS = 1e-6
DEPTH = 2
IN_COLS = 5376
N_CHIPS = 4
FFN_SHARD = FFN_DIM // N_CHIPS
FFN_SHARD_PAD = 768
FFN_PAD = N_CHIPS * FFN_SHARD_PAD
LANES = 128
SUBLANES = 8
HALO = 16
VMEM_LIMIT = 56 * 1024 * 1024
NEG_BIG = -1e30

ADAM_LR = 0.001
ADAM_B1 = 0.9
ADAM_B2 = 0.999
ADAM_EPS = 1e-08
ADAM_WD = 0.01
ADAM_STEP = 10

MESH = pl.DeviceIdType.MESH

Q_END = 1024
KV_END = 1280
U_END = 2304
Z_END = 3328
GA_END = 4352


def _sds(shape, dtype):
    return jax.ShapeDtypeStruct(tuple(shape), dtype)


def _call(body, name, out_shape, grid, in_specs, out_specs, semantics, scratch=(), after=None):
    n_in = len(in_specs)
    fn = body
    if after is not None:
        def fn(*refs):
            return body(*refs[:n_in], *refs[n_in + 1:])

        in_specs = list(in_specs) + [pl.BlockSpec(memory_space=pl.ANY)]
    call = pl.pallas_call(
        fn, name=name, out_shape=out_shape, grid=grid, in_specs=in_specs, out_specs=out_specs,
        scratch_shapes=scratch,
        compiler_params=pltpu.CompilerParams(dimension_semantics=semantics, vmem_limit_bytes=VMEM_LIMIT))
    if after is None:
        return call
    return lambda *args: call(*args, after)


def _rows(tm, width, col=0):
    return pl.BlockSpec((tm, width), lambda i: (i, col))


def _vec(width):
    return pl.BlockSpec((1, width), lambda i: (0, 0))


def _resident(shape):
    zeros = (0,) * len(shape)
    return pl.BlockSpec(tuple(shape), lambda *_: zeros, pipeline_mode=pl.Buffered(1))


def _sigmoid(x):
    return 1.0 / (1.0 + jnp.exp(-x))


def _gelu(x):
    return 0.5 * x * (1.0 + lax.erf(x * 0.7071067811865476))


def _gelu_grad(x):
    cdf = 0.5 * (1.0 + lax.erf(x * 0.7071067811865476))
    return cdf + x * jnp.exp(-0.5 * x * x) * 0.3989422804014327


def _dot(a, b):
    return jnp.dot(a, b, preferred_element_type=F32)


def _dot_nt(a, b):
    return lax.dot_general(a, b, (((1,), (1,)), ((), ())), preferred_element_type=F32)


def _dot_tn(a, b):
    return lax.dot_general(a, b, (((0,), (0,)), ((), ())), preferred_element_type=F32)


def _rms(xv):
    return lax.rsqrt(jnp.mean(xv * xv, axis=-1, keepdims=True) + NORM_EPS)


def _matmul_tn(name, a, b, tk=512, tn=1024, blocked=False):
    s, k = a.shape
    n = b.shape[1]
    tk, tn = min(tk, k), min(tn, n)

    def body(a_ref, b_ref, o_ref):
        res = _dot_tn(a_ref[...], b_ref[...]).astype(o_ref.dtype)
        if blocked:
            o_ref[0] = res
        else:
            o_ref[...] = res

    if blocked:
        out, ospec = _sds((n // tn, k, tn), BF), pl.BlockSpec((1, tk, tn), lambda i, j: (j, i, 0))
    else:
        out, ospec = _sds((k, n), BF), pl.BlockSpec((tk, tn), lambda i, j: (i, j))
    return _call(body, name, out, (k // tk, n // tn),
                 [pl.BlockSpec((s, tk), lambda i, j: (0, i)), pl.BlockSpec((s, tn), lambda i, j: (0, j))],
                 ospec, ("parallel", "parallel"))(a, b)


def _matmul_tn_rows(name, a, b, out, row0, rows_total, tk=256):
    s, k = a.shape
    n = b.shape[1]

    def body(a_ref, b_ref, *rest):
        rest[-1][...] = _dot_tn(a_ref[...], b_ref[...]).astype(BF)

    in_specs = [pl.BlockSpec((s, tk), lambda i: (0, i)), _resident(b.shape)]
    args = [a, b]
    if out is not None:
        in_specs.append(pl.BlockSpec(memory_space=pl.ANY))
        args.append(out)
    return pl.pallas_call(
        body, name=name, out_shape=_sds((rows_total, n), BF), grid=(k // tk,), in_specs=in_specs,
        out_specs=pl.BlockSpec((tk, n), lambda i: (row0 // tk + i, 0)),
        input_output_aliases={2: 0} if out is not None else {},
        compiler_params=pltpu.CompilerParams(dimension_semantics=("parallel",), vmem_limit_bytes=VMEM_LIMIT))(*args)


def _rope_partner(v):
    lane = lax.broadcasted_iota(jnp.int32, (1, LANES), 1) % HEAD_DIM
    return jnp.where(lane < ROT_DIM // 2, pltpu.roll(v, LANES - ROT_DIM // 2, axis=1), pltpu.roll(v, ROT_DIM // 2, axis=1))


def _dup_half(v, half):
    lane = lax.broadcasted_iota(jnp.int32, (1, LANES), 1)
    keep = jnp.where((lane >= HEAD_DIM) == (half == 1), v, 0.0)
    return keep + pltpu.roll(keep, HEAD_DIM, axis=1)


def _in_proj(name, x, w, sc, sh, w_in, cosf, sinf, tm=256, after=None):
    s, d = x.shape
    tm = min(tm, s)

    def body(x_ref, w_ref, sc_ref, sh_ref, win_ref, cos_ref, sin_ref,
             h_ref, qr_ref, kk0_ref, kk1_ref, vv0_ref, vv1_ref, u_ref, v_ref, ga_ref, gb_ref):
        xv = x_ref[...]
        h = ((xv * _rms(xv)) * w_ref[...] * (1.0 + sc_ref[...]) + sh_ref[...]).astype(BF)
        h_ref[...] = h
        cosv, sinv = cos_ref[...], sin_ref[...]
        q = _dot_nt(h, win_ref[:Q_END, :])
        for j in range(D_MODEL // LANES):
            qv = q[:, j * LANES:(j + 1) * LANES]
            qr_ref[:, j * LANES:(j + 1) * LANES] = (qv * cosv + _rope_partner(qv) * sinv).astype(BF)
        kv = _dot_nt(h, win_ref[Q_END:KV_END, :])
        kr = kv[:, :LANES] * cosv + _rope_partner(kv[:, :LANES]) * sinv
        vv = kv[:, LANES:]
        kk0_ref[...] = _dup_half(kr, 0).astype(BF)
        kk1_ref[...] = _dup_half(kr, 1).astype(BF)
        vv0_ref[...] = _dup_half(vv, 0).astype(BF)
        vv1_ref[...] = _dup_half(vv, 1).astype(BF)
        u_ref[...] = _dot_nt(h, win_ref[KV_END:U_END, :])
        v_ref[...] = _dot_nt(h, win_ref[U_END:Z_END, :])
        ga_ref[...] = _dot_nt(h, win_ref[Z_END:GA_END, :]).astype(BF)
        gb_ref[...] = _dot_nt(h, win_ref[GA_END:, :]).astype(BF)

    wide, kvs, pre = _sds((s, d), BF), _sds((s, LANES), BF), _sds((s, d), F32)
    return _call(body, name, (wide, wide, kvs, kvs, kvs, kvs, pre, pre, wide, wide), (s // tm,),
                 [_rows(tm, d), _vec(d), _vec(d), _vec(d), _resident(w_in.shape), _rows(tm, LANES), _rows(tm, LANES)],
                 (_rows(tm, d), _rows(tm, d)) + (_rows(tm, LANES),) * 4 + (_rows(tm, d),) * 4, ("parallel",), after=after)(
                     x, w, sc, sh, w_in, cosf, sinf)


def _in_proj_bwd(name, dq, dkv, du, dv, dga, dgb, w_in, x, w, sc, dx_in, tm=256):
    s, d = x.shape
    tm = min(tm, s)

    def body(dq_ref, dkv_ref, du_ref, dv_ref, dga_ref, dgb_ref, win_ref, x_ref, w_ref, sc_ref, dxin_ref,
             dx_ref, da_ref, dsh_ref):
        @pl.when(pl.program_id(0) == 0)
        def _():
            da_ref[...] = jnp.zeros_like(da_ref)
            dsh_ref[...] = jnp.zeros_like(dsh_ref)

        dh = (_dot(dq_ref[...], win_ref[:Q_END, :]) + _dot(dkv_ref[...], win_ref[Q_END:KV_END, :])
              + _dot(du_ref[...], win_ref[KV_END:U_END, :]) + _dot(dv_ref[...], win_ref[U_END:Z_END, :])
              + _dot(dga_ref[...], win_ref[Z_END:GA_END, :]) + _dot(dgb_ref[...], win_ref[GA_END:, :]))
        xv = x_ref[...]
        r = _rms(xv)
        xn = xv * r
        dxn = dh * (w_ref[...] * (1.0 + sc_ref[...]))
        dx_ref[...] = dxin_ref[...] + r * (dxn - xn * jnp.mean(dxn * xn, axis=-1, keepdims=True))
        da_ref[...] += jnp.sum(dh * xn, axis=0, keepdims=True)
        dsh_ref[...] += jnp.sum(dh, axis=0, keepdims=True)

    return _call(body, name, (_sds((s, d), F32), _sds((1, d), F32), _sds((1, d), F32)), (s // tm,),
                 [_rows(tm, d), _rows(tm, 2 * LANES), _rows(tm, d), _rows(tm, d), _rows(tm, d), _rows(tm, d),
                  _resident(w_in.shape), _rows(tm, d), _vec(d), _vec(d), _rows(tm, d)],
                 (_rows(tm, d), _vec(d), _vec(d)), ("arbitrary",))(dq, dkv, du, dv, dga, dgb, w_in, x, w, sc, dx_in)


def _rope_bwd(name, dqr, dkv_cur, dkv_prev, cosf, sinf, tm=512):
    s = dqr.shape[0]
    tm = min(tm, s)
    steps = s // tm
    per = tm // ATTN_BLOCK
    nb = s // ATTN_BLOCK

    def unrope(v, cosv, sinv):
        return v * cosv - _rope_partner(v) * sinv

    def body(dq_ref, cur_ref, prev_ref, next_ref, cos_ref, sin_ref, dqo_ref, dkvo_ref):
        i = pl.program_id(0)
        cosv, sinv = cos_ref[...], sin_ref[...]
        for j in range(D_MODEL // LANES):
            dqo_ref[:, j * LANES:(j + 1) * LANES] = unrope(dq_ref[:, j * LANES:(j + 1) * LANES], cosv, sinv).astype(BF)
        nxt = jnp.where(i < steps - 1, next_ref[...], 0.0)
        if per > 1:
            shifted = jnp.concatenate([prev_ref[ATTN_BLOCK:, :], nxt], axis=0)
        else:
            shifted = nxt
        tot = cur_ref[...] + shifted
        dkvo_ref[:, :LANES] = unrope(tot[:, :LANES], cosv, sinv).astype(BF)
        dkvo_ref[:, LANES:] = tot[:, LANES:].astype(BF)

    nxt_spec = pl.BlockSpec((ATTN_BLOCK, 2 * LANES), lambda i: (jnp.minimum((i + 1) * per, nb - 1), 0))
    return _call(body, name, (_sds((s, D_MODEL), BF), _sds((s, 2 * LANES), BF)), (steps,),
                 [_rows(tm, D_MODEL), _rows(tm, 2 * LANES), _rows(tm, 2 * LANES), nxt_spec, _rows(tm, LANES),
                  _rows(tm, LANES)],
                 (_rows(tm, D_MODEL), _rows(tm, 2 * LANES)), ("parallel",))(dqr, dkv_cur, dkv_prev, dkv_prev, cosf, sinf)


Q_PER_KV = N_Q_HEADS // N_KV_HEADS


def _band_mask_t(n):
    kj = lax.broadcasted_iota(jnp.int32, (2 * ATTN_BLOCK, ATTN_BLOCK), 0)
    qi = lax.broadcasted_iota(jnp.int32, (2 * ATTN_BLOCK, ATTN_BLOCK), 1)
    return (kj > qi) & (kj <= qi + ATTN_BLOCK) & ((n > 0) | (kj >= ATTN_BLOCK))


def _softmax_t(raw, allowed, sink):
    sc = jnp.where(allowed, raw * (HEAD_DIM ** -0.5), NEG_BIG)
    m = jnp.maximum(jnp.max(sc, axis=0, keepdims=True), sink)
    p = jnp.exp(sc - m)
    esink = jnp.exp(sink - m)
    inv = 1.0 / (jnp.sum(p, axis=0, keepdims=True) + esink)
    return p * inv, esink * inv


def _kv_specs():
    cur = pl.BlockSpec((ATTN_BLOCK, LANES), lambda n: (n, 0))
    prev = pl.BlockSpec((ATTN_BLOCK, LANES), lambda n: (jnp.maximum(n - 1, 0), 0))
    return [prev, cur] * 4


def _attention(name, qr, kk0, kk1, vv0, vv1, sinks):
    s = qr.shape[0]
    nb = s // ATTN_BLOCK

    def body(sink_ref, q_ref, k0p, k0c, k1p, k1c, v0p, v0c, v1p, v1c, y_ref):
        allowed = _band_mask_t(pl.program_id(0))
        upper = lax.broadcasted_iota(jnp.int32, (1, LANES), 1) >= HEAD_DIM
        upper_rows = lax.broadcasted_iota(jnp.int32, (LANES, 1), 0) >= HEAD_DIM
        bands = ((jnp.concatenate([k0p[...], k0c[...]], axis=0), jnp.concatenate([v0p[...], v0c[...]], axis=0)),
                 (jnp.concatenate([k1p[...], k1c[...]], axis=0), jnp.concatenate([v1p[...], v1c[...]], axis=0)))
        vbts = (bands[0][1].T, bands[1][1].T)

        def scores(h):
            hk, j, half = h // Q_PER_KV, (h % Q_PER_KV) // 2, h % 2
            col = (hk * 4 + j) * LANES
            qp = q_ref[:, col:col + LANES]
            return _dot_nt(bands[hk][0], jnp.where(upper if half else jnp.logical_not(upper), qp, jnp.zeros_like(qp)))

        out_t = None
        ahead = scores(0)
        for h in range(N_Q_HEADS):
            hk, j, half = h // Q_PER_KV, (h % Q_PER_KV) // 2, h % 2
            raw = ahead
            if h + 1 < N_Q_HEADS:
                ahead = scores(h + 1)
            pn, _ = _softmax_t(raw, allowed, sink_ref[h])
            o_h = _dot(vbts[hk], pn.astype(BF))
            out_t = jnp.where(upper_rows, o_h, out_t) if half else o_h
            if half:
                col = (hk * 4 + j) * LANES
                y_ref[:, col:col + LANES] = out_t.T.astype(BF)

    return _call(body, name, _sds((s, D_MODEL), BF), (nb,),
                 [pl.BlockSpec(memory_space=pltpu.SMEM), pl.BlockSpec((ATTN_BLOCK, D_MODEL), lambda n: (n, 0))] + _kv_specs(),
                 pl.BlockSpec((ATTN_BLOCK, D_MODEL), lambda n: (n, 0)), ("parallel",))(
                     sinks, qr, kk0, kk0, kk1, kk1, vv0, vv0, vv1, vv1)


def _attention_bwd(name, qr, kk0, kk1, vv0, vv1, sinks, dy):
    s = qr.shape[0]
    nb = s // ATTN_BLOCK

    def body(sink_ref, q_ref, dy_ref, k0p, k0c, k1p, k1c, v0p, v0c, v1p, v1c, dq_ref, cur_ref, prev_ref, dsink_ref):
        @pl.when(pl.program_id(0) == 0)
        def _():
            dsink_ref[...] = jnp.zeros_like(dsink_ref)

        allowed = _band_mask_t(pl.program_id(0))
        lane = lax.broadcasted_iota(jnp.int32, (1, LANES), 1)
        upper = lane >= HEAD_DIM
        upper_rows = lax.broadcasted_iota(jnp.int32, (LANES, 1), 0) >= HEAD_DIM
        bands = ((jnp.concatenate([k0p[...], k0c[...]], axis=0), jnp.concatenate([v0p[...], v0c[...]], axis=0)),
                 (jnp.concatenate([k1p[...], k1c[...]], axis=0), jnp.concatenate([v1p[...], v1c[...]], axis=0)))
        kbts = (bands[0][0].T, bands[1][0].T)

        def scores(h):
            hk, j, half = h // Q_PER_KV, (h % Q_PER_KV) // 2, h % 2
            kb, vb = bands[hk]
            col = (hk * 4 + j) * LANES
            sel = upper if half else jnp.logical_not(upper)
            qp = q_ref[:, col:col + LANES]
            qa = jnp.where(sel, qp, jnp.zeros_like(qp))
            dya = jnp.where(sel, dy_ref[:, col:col + LANES], 0.0).astype(BF)
            return qa, dya, _dot_nt(kb, qa), _dot_nt(vb, dya)

        dsink = jnp.zeros((1, LANES), F32)
        dk_slab = jnp.zeros((2 * ATTN_BLOCK, LANES), F32)
        dv_slab = jnp.zeros((2 * ATTN_BLOCK, LANES), F32)
        dkk = dvv = dq_t = None
        ahead = scores(0)
        for h in range(N_Q_HEADS):
            hk, j, half = h // Q_PER_KV, (h % Q_PER_KV) // 2, h % 2
            qa, dya, raw, dp = ahead
            if h + 1 < N_Q_HEADS:
                ahead = scores(h + 1)
            pn, psink = _softmax_t(raw, allowed, sink_ref[h])
            delta = jnp.sum(pn * dp, axis=0, keepdims=True)
            ds = (pn * (dp - delta) * (HEAD_DIM ** -0.5)).astype(BF)
            dsink = dsink + jnp.where(lane == h, -jnp.sum(psink * delta), 0.0)
            dq_h = _dot(kbts[hk], ds)
            dq_t = jnp.where(upper_rows, dq_h, dq_t) if half else dq_h
            dk_h, dv_h = _dot(ds, qa), _dot(pn.astype(BF), dya)
            dkk, dvv = (dk_h, dv_h) if h % Q_PER_KV == 0 else (dkk + dk_h, dvv + dv_h)
            if half:
                col = (hk * 4 + j) * LANES
                dq_ref[:, col:col + LANES] = dq_t.T
            if h % Q_PER_KV == Q_PER_KV - 1:
                mine = upper if hk else jnp.logical_not(upper)
                dk_slab = jnp.where(mine, dkk + pltpu.roll(dkk, HEAD_DIM, axis=1), dk_slab)
                dv_slab = jnp.where(mine, dvv + pltpu.roll(dvv, HEAD_DIM, axis=1), dv_slab)
        prev_ref[:, :LANES] = dk_slab[:ATTN_BLOCK]
        prev_ref[:, LANES:] = dv_slab[:ATTN_BLOCK]
        cur_ref[:, :LANES] = dk_slab[ATTN_BLOCK:]
        cur_ref[:, LANES:] = dv_slab[ATTN_BLOCK:]
        dsink_ref[...] += dsink

    blk = pl.BlockSpec((ATTN_BLOCK, D_MODEL), lambda n: (n, 0))
    kvo = pl.BlockSpec((ATTN_BLOCK, 2 * LANES), lambda n: (n, 0))
    return _call(body, name,
                 (_sds((s, D_MODEL), F32), _sds((s, 2 * LANES), F32), _sds((s, 2 * LANES), F32), _sds((1, LANES), F32)),
                 (nb,), [pl.BlockSpec(memory_space=pltpu.SMEM), blk, blk] + _kv_specs(),
                 (blk, kvo, kvo, pl.BlockSpec((1, LANES), lambda n: (0, 0))), ("arbitrary",))(
                     sinks, qr, dy, kk0, kk0, kk1, kk1, vv0, vv0, vv1, vv1)


def _sgu_weights(wm_ref, g):
    t = lax.broadcasted_iota(jnp.int32, (SGU_CHUNK, SGU_CHUNK), 0)
    sidx = lax.broadcasted_iota(jnp.int32, (SGU_CHUNK, SGU_CHUNK), 1)
    return jnp.where(sidx <= t, wm_ref[g], 0.0).astype(BF)


def _layer_norm_stats(v):
    mu = jnp.mean(v, axis=-1, keepdims=True)
    cen = v - mu
    rstd = lax.rsqrt(jnp.mean(cen * cen, axis=-1, keepdims=True) + NORM_EPS)
    return cen * rstd, rstd


def _sgu(name, u_pre, v_pre, ln_w, ln_b, wm, bfull, tm=256):
    s, w = u_pre.shape
    tm = min(tm, s)

    def body(u_ref, v_ref, lw_ref, lb_ref, wm_ref, b_ref, y_ref):
        vhat, _ = _layer_norm_stats(_gelu(v_ref[...]))
        vn = (vhat * lw_ref[...] + lb_ref[...]).astype(BF)
        for g in range(SGU_GROUPS):
            wg = _sgu_weights(wm_ref, g)
            cols = slice(g * SGU_CHUNK, (g + 1) * SGU_CHUNK)
            for ch in range(tm // SGU_CHUNK):
                rows = slice(ch * SGU_CHUNK, (ch + 1) * SGU_CHUNK)
                f = _dot(wg, vn[rows, cols]) + b_ref[g]
                y_ref[rows, cols] = (_gelu(u_ref[rows, cols]) * f).astype(BF)

    full3 = pl.BlockSpec((SGU_GROUPS, SGU_CHUNK, SGU_CHUNK), lambda i: (0, 0, 0))
    return _call(body, name, _sds((s, w), BF), (s // tm,),
                 [_rows(tm, w), _rows(tm, w), _vec(w), _vec(w), full3, full3],
                 _rows(tm, w), ("parallel",))(u_pre, v_pre, ln_w, ln_b, wm, bfull)


def _sgu_bwd(name, u_pre, v_pre, ln_w, ln_b, wm, bfull, dy, tm=256, after=None):
    s, w = u_pre.shape
    tm = min(tm, s)
    steps = s // tm

    def body(u_ref, v_ref, lw_ref, lb_ref, wm_ref, b_ref, dy_ref, du_ref, dv_ref, dwm_ref, db_ref, dlw_ref, dlb_ref,
             dfsum_ref):
        i = pl.program_id(0)

        @pl.when(i == 0)
        def _():
            dwm_ref[...] = jnp.zeros_like(dwm_ref)
            dlw_ref[...] = jnp.zeros_like(dlw_ref)
            dlb_ref[...] = jnp.zeros_like(dlb_ref)
            dfsum_ref[...] = jnp.zeros_like(dfsum_ref)

        vpre = v_ref[...]
        vhat, rstd = _layer_norm_stats(_gelu(vpre))
        vn = (vhat * lw_ref[...] + lb_ref[...]).astype(BF)
        t = lax.broadcasted_iota(jnp.int32, (SGU_CHUNK, SGU_CHUNK), 0)
        sidx = lax.broadcasted_iota(jnp.int32, (SGU_CHUNK, SGU_CHUNK), 1)
        dvn_cols = []
        for g in range(SGU_GROUPS):
            wg = _sgu_weights(wm_ref, g)
            cols = slice(g * SGU_CHUNK, (g + 1) * SGU_CHUNK)
            dvn_rows = []
            dwg = jnp.zeros((SGU_CHUNK, SGU_CHUNK), F32)
            dfs = jnp.zeros((SGU_CHUNK, SGU_CHUNK), F32)
            for ch in range(tm // SGU_CHUNK):
                rows = slice(ch * SGU_CHUNK, (ch + 1) * SGU_CHUNK)
                upre = u_ref[rows, cols]
                dyv = dy_ref[rows, cols].astype(F32)
                f = _dot(wg, vn[rows, cols]) + b_ref[g]
                du_ref[rows, cols] = (dyv * f * _gelu_grad(upre)).astype(BF)
                df = dyv * _gelu(upre)
                dfb = df.astype(BF)
                dvn_rows.append(_dot_tn(wg, dfb))
                dwg = dwg + _dot_nt(dfb, vn[rows, cols])
                dfs = dfs + df
            dwm_ref[g] += jnp.where(sidx <= t, dwg, 0.0)
            dfsum_ref[g] += dfs
            dvn_cols.append(jnp.concatenate(dvn_rows, axis=0) if len(dvn_rows) > 1 else dvn_rows[0])
        dvn = jnp.concatenate(dvn_cols, axis=1)
        dlw_ref[...] += jnp.sum(dvn * vhat, axis=0, keepdims=True)
        dlb_ref[...] += jnp.sum(dvn, axis=0, keepdims=True)
        dvh = dvn * lw_ref[...]
        dvg = rstd * (dvh - jnp.mean(dvh, axis=-1, keepdims=True) - vhat * jnp.mean(dvh * vhat, axis=-1, keepdims=True))
        dv_ref[...] = (dvg * _gelu_grad(vpre)).astype(BF)

        @pl.when(i == steps - 1)
        def _():
            for g in range(SGU_GROUPS):
                db_ref[g:g + 1, :] = jnp.sum(dfsum_ref[g].T, axis=0, keepdims=True)

    full3 = pl.BlockSpec((SGU_GROUPS, SGU_CHUNK, SGU_CHUNK), lambda i: (0, 0, 0))
    return _call(body, name,
                 (_sds((s, w), BF), _sds((s, w), BF), _sds((SGU_GROUPS, SGU_CHUNK, SGU_CHUNK), F32),
                  _sds((SGU_GROUPS, SGU_CHUNK), F32), _sds((1, w), F32), _sds((1, w), F32)),
                 (steps,),
                 [_rows(tm, w), _rows(tm, w), _vec(w), _vec(w), full3, full3, _rows(tm, w)],
                 (_rows(tm, w), _rows(tm, w), full3, pl.BlockSpec((SGU_GROUPS, SGU_CHUNK), lambda i: (0, 0)), _vec(w), _vec(w)),
                 ("arbitrary",), scratch=[pltpu.VMEM((SGU_GROUPS, SGU_CHUNK, SGU_CHUNK), F32)], after=after)(
                     u_pre, v_pre, ln_w, ln_b, wm, bfull, dy)


def _mix_out(name, y_sgu, y_attn, ga_pre, gb_pre, x, g1, proj_a, proj_b, w_out, w2, sc2, sh2, tm=256):
    s, d = x.shape
    tm = min(tm, s)

    def body(ys_ref, ya_ref, ga_ref, gb_ref, x_ref, g1_ref, wa_ref, wb_ref, wo_ref, w2_ref, sc2_ref, sh2_ref,
             m_ref, pa_ref, pb_ref, o_ref, x1_ref, h2_ref):
        pa = _dot(ys_ref[...], wa_ref[...].reshape(d, d))
        pb = _dot(ya_ref[...], wb_ref[...].reshape(d, d))
        pa_ref[...] = pa.astype(BF)
        pb_ref[...] = pb.astype(BF)
        merged = (_sigmoid(ga_ref[...].astype(F32)) * pa + _sigmoid(gb_ref[...].astype(F32)) * pb).astype(BF)
        m_ref[...] = merged
        o = _dot(merged, wo_ref[...].reshape(d, d))
        o_ref[...] = o.astype(BF)
        x1 = x_ref[...] + g1_ref[...] * o
        x1_ref[...] = x1
        h2_ref[...] = ((x1 * _rms(x1)) * w2_ref[...] * (1.0 + sc2_ref[...]) + sh2_ref[...]).astype(BF)

    f, b = _sds((s, d), F32), _sds((s, d), BF)
    r = _rows(tm, d)
    wspec = _resident(proj_a.shape)
    return _call(body, name, (b, b, b, b, f, b), (s // tm,),
                 [r, r, r, r, r, _vec(d), wspec, wspec, wspec, _vec(d), _vec(d), _vec(d)], (r,) * 6, ("parallel",))(
                     y_sgu, y_attn, ga_pre, gb_pre, x, g1, proj_a, proj_b, w_out, w2, sc2, sh2)


def _mix_bwd(name, do, w_out, proj_a, proj_b, ga_pre, gb_pre, pa, pb, tm=256):
    s, d = do.shape
    tm = min(tm, s)

    def body(do_ref, wo_ref, wa_ref, wb_ref, ga_ref, gb_ref, pa_ref, pb_ref,
             dpa_ref, dpb_ref, dga_ref, dgb_ref, dys_ref, dya_ref):
        dm = _dot_nt(do_ref[...], wo_ref[...].reshape(d, d))
        ga = _sigmoid(ga_ref[...].astype(F32))
        gb = _sigmoid(gb_ref[...].astype(F32))
        dpa = (dm * ga).astype(BF)
        dpb = (dm * gb).astype(BF)
        dpa_ref[...] = dpa
        dpb_ref[...] = dpb
        dga_ref[...] = (dm * pa_ref[...].astype(F32) * ga * (1.0 - ga)).astype(BF)
        dgb_ref[...] = (dm * pb_ref[...].astype(F32) * gb * (1.0 - gb)).astype(BF)
        dys_ref[...] = _dot_nt(dpa, wa_ref[...].reshape(d, d)).astype(BF)
        dya_ref[...] = _dot_nt(dpb, wb_ref[...].reshape(d, d)).astype(BF)

    f, b = _sds((s, d), F32), _sds((s, d), BF)
    r = _rows(tm, d)
    wspec = _resident(w_out.shape)
    return _call(body, name, (b, b, b, b, b, b), (s // tm,), [r, wspec, wspec, wspec, r, r, r, r], (r,) * 6,
                 ("parallel",))(do, w_out, proj_a, proj_b, ga_pre, gb_pre, pa, pb)


def _ffn_up_act(name, h2, w_gate, w_up, cw, cb, tm=1024):
    s, d = h2.shape
    tm = min(tm, s)
    tc = FFN_SHARD_PAD
    per = tm // HALO

    def body(h_ref, hprev_ref, wg_ref, wu_ref, cw_ref, cb_ref, a_ref, up_ref, hf_ref):
        hv = h_ref[...]
        a = _dot_nt(hv, wg_ref[0]).astype(BF)
        up = _dot_nt(hv, wu_ref[0]).astype(BF)
        a_ref[...] = a
        up_ref[...] = up
        prev = jnp.where(pl.program_id(1) > 0, _dot_nt(hprev_ref[...], wg_ref[0]).astype(BF).astype(F32), 0.0)
        ext = jnp.concatenate([prev, a.astype(F32)], axis=0)
        ac = (cb_ref[...] + cw_ref[0:1, :] * pltpu.roll(ext, 2, axis=0) + cw_ref[1:2, :] * pltpu.roll(ext, 1, axis=0)
              + cw_ref[2:3, :] * ext)[HALO:]
        hf_ref[...] = (ac * _sigmoid(ac) * up.astype(F32)).astype(BF)

    wspec = pl.BlockSpec((1, tc, d), lambda j, i: (j, 0, 0))
    ospec = pl.BlockSpec((tm, tc), lambda j, i: (i, j))
    o = _sds((s, FFN_PAD), BF)
    return _call(body, name, (o, o, o), (N_CHIPS, s // tm),
                 [pl.BlockSpec((tm, d), lambda j, i: (i, 0)), pl.BlockSpec((HALO, d), lambda j, i: (jnp.maximum(i * per - 1, 0), 0)),
                  wspec, wspec, pl.BlockSpec((3, tc), lambda j, i: (0, j)), pl.BlockSpec((1, tc), lambda j, i: (0, j))],
                 (ospec, ospec, ospec), ("parallel", "parallel"))(h2, h2, w_gate, w_up, cw, cb)


def _ffn_down(name, hf, w_down, x1, g2, tm=512):
    s, d = x1.shape
    tm = min(tm, s)

    def body(hf_ref, wd_ref, x1_ref, g2_ref, dn_ref, x2_ref):
        dn = _dot(hf_ref[...], wd_ref[...].reshape(FFN_PAD, d))
        dn_ref[...] = dn.astype(BF)
        x2_ref[...] = x1_ref[...] + g2_ref[...] * dn

    return _call(body, name, (_sds((s, d), BF), _sds((s, d), F32)), (s // tm,),
                 [_rows(tm, FFN_PAD), _resident(w_down.shape), _rows(tm, d), _vec(d)],
                 (_rows(tm, d), _rows(tm, d)), ("parallel",))(hf, w_down, x1, g2)


def _ffn_down_bwd_act(name, dx2, dn, g2, w_down, a, up, cw, cb, tm=256, after=None):
    s, d = dx2.shape
    c = a.shape[1]
    tm = min(tm, s)
    tc = FFN_SHARD_PAD
    per = tm // HALO
    steps = s // tm
    last = s // HALO - 1
    n = tm + HALO

    def body(dx_ref, dxnext_ref, dn_ref, g2_ref, wd_ref, a_ref, aprev_ref, anext_ref, up_ref, upnext_ref, cw_ref, cb_ref,
             ddn_ref, da_ref, dup_ref, dg_ref, dcw_ref, dcb_ref):
        i = pl.program_id(0)

        @pl.when(i == 0)
        def _():
            dg_ref[...] = jnp.zeros_like(dg_ref)
            dcw_ref[...] = jnp.zeros_like(dcw_ref)
            dcb_ref[...] = jnp.zeros_like(dcb_ref)

        dxv = dx_ref[...]
        ddn = (dxv * g2_ref[...]).astype(BF)
        ddn_ref[...] = ddn
        dg_ref[...] += jnp.sum(dxv * dn_ref[...].astype(F32), axis=0, keepdims=True)
        ddn_next = jnp.where(i < steps - 1, dxnext_ref[...] * g2_ref[...], 0.0).astype(BF)
        ddn_ext = jnp.concatenate([ddn, ddn_next], axis=0)
        for k in range(N_CHIPS):
            cols = slice(k * tc, (k + 1) * tc)
            dh = _dot_nt(ddn_ext, wd_ref[k])
            prev = jnp.where(i > 0, aprev_ref[:, cols].astype(F32), 0.0)
            ext = jnp.concatenate([prev, a_ref[:, cols].astype(F32), anext_ref[:, cols].astype(F32)], axis=0)
            a1 = pltpu.roll(ext, 1, axis=0)[HALO:]
            a2 = pltpu.roll(ext, 2, axis=0)[HALO:]
            a0 = ext[HALO:]
            ac = cb_ref[:, cols] + cw_ref[0:1, cols] * a2 + cw_ref[1:2, cols] * a1 + cw_ref[2:3, cols] * a0
            sig = _sigmoid(ac)
            upe = jnp.concatenate([up_ref[:, cols].astype(F32), upnext_ref[:, cols].astype(F32)], axis=0)
            dac = dh * upe * (sig * (1.0 + ac * (1.0 - sig)))
            dup_ref[:, cols] = (dh[:tm] * (ac * sig)[:tm]).astype(BF)
            da = (cw_ref[2:3, cols] * dac + cw_ref[1:2, cols] * pltpu.roll(dac, n - 1, axis=0)
                  + cw_ref[0:1, cols] * pltpu.roll(dac, n - 2, axis=0))
            da_ref[:, cols] = da[:tm].astype(BF)
            dact = dac[:tm]
            dcb_ref[:, cols] += jnp.sum(dact, axis=0, keepdims=True)
            dcw_ref[0:1, cols] += jnp.sum(dact * a2[:tm], axis=0, keepdims=True)
            dcw_ref[1:2, cols] += jnp.sum(dact * a1[:tm], axis=0, keepdims=True)
            dcw_ref[2:3, cols] += jnp.sum(dact * a0[:tm], axis=0, keepdims=True)

    def halo(width, shift):
        if shift < 0:
            return pl.BlockSpec((HALO, width), lambda i: (jnp.maximum(i * per - 1, 0), 0))
        return pl.BlockSpec((HALO, width), lambda i: (jnp.minimum((i + 1) * per, last), 0))

    wide = _sds((s, c), BF)
    return _call(body, name, (_sds((s, d), BF), wide, wide, _sds((1, d), F32), _sds((3, c), F32), _sds((1, c), F32)), (steps,),
                 [_rows(tm, d), halo(d, 1), _rows(tm, d), _vec(d), _resident(w_down.shape), _rows(tm, c), halo(c, -1), halo(c, 1),
                  _rows(tm, c), halo(c, 1), pl.BlockSpec((3, c), lambda i: (0, 0)), _vec(c)],
                 (_rows(tm, d), _rows(tm, c), _rows(tm, c), _vec(d), pl.BlockSpec((3, c), lambda i: (0, 0)), _vec(c)),
                 ("arbitrary",), after=after)(dx2, dx2, dn, g2, w_down, a, a, a, up, up, cw, cb)


def _ffn_up_bwd(name, da, dup, w_gate, w_up, x1, dx2, w2, sc2, o, g1, tm=256, after=None):
    s, d = x1.shape
    tm = min(tm, s)
    tc = FFN_SHARD_PAD

    def body(da_ref, dup_ref, wg_ref, wu_ref, x1_ref, dx2_ref, w2_ref, sc2_ref, o_ref, g1_ref,
             dx1_ref, do_ref, dnw_ref, dsh_ref, dg1_ref):
        @pl.when(pl.program_id(0) == 0)
        def _():
            dnw_ref[...] = jnp.zeros_like(dnw_ref)
            dsh_ref[...] = jnp.zeros_like(dsh_ref)
            dg1_ref[...] = jnp.zeros_like(dg1_ref)

        dh = jnp.zeros((tm, d), F32)
        for k in range(N_CHIPS):
            cols = slice(k * tc, (k + 1) * tc)
            dh = dh + _dot(da_ref[:, cols], wg_ref[k]) + _dot(dup_ref[:, cols], wu_ref[k])
        xv = x1_ref[...]
        r = _rms(xv)
        xn = xv * r
        dxn = dh * (w2_ref[...] * (1.0 + sc2_ref[...]))
        dx1 = dx2_ref[...] + r * (dxn - xn * jnp.mean(dxn * xn, axis=-1, keepdims=True))
        dx1_ref[...] = dx1
        dnw_ref[...] += jnp.sum(dh * xn, axis=0, keepdims=True)
        dsh_ref[...] += jnp.sum(dh, axis=0, keepdims=True)
        do_ref[...] = (dx1 * g1_ref[...]).astype(BF)
        dg1_ref[...] += jnp.sum(dx1 * o_ref[...].astype(F32), axis=0, keepdims=True)

    v = _sds((1, d), F32)
    r = _rows(tm, d)
    wspec = _resident(w_gate.shape)
    return _call(body, name, (_sds((s, d), F32), _sds((s, d), BF), v, v, v), (s // tm,),
                 [_rows(tm, FFN_PAD), _rows(tm, FFN_PAD), wspec, wspec, r, r, _vec(d), _vec(d), r, _vec(d)],
                 (r, r, _vec(d), _vec(d), _vec(d)), ("arbitrary",), after=after)(da, dup, w_gate, w_up, x1, dx2, w2, sc2, o, g1)


def _loss_head(name, x, w, target, tm=512):
    s, d = x.shape
    tm = min(tm, s)

    def body(x_ref, w_ref, t_ref, dx_ref, loss_ref, dw_ref):
        @pl.when(pl.program_id(0) == 0)
        def _():
            loss_ref[...] = jnp.zeros_like(loss_ref)
            dw_ref[...] = jnp.zeros_like(dw_ref)

        xv = x_ref[...]
        r = _rms(xv)
        xn = xv * r
        err = xn * w_ref[...] - t_ref[...]
        loss_ref[...] += 0.5 * jnp.sum(jnp.mean(err * err, axis=-1, keepdims=True))
        dy = err * (1.0 / d)
        dw_ref[...] += jnp.sum(dy * xn, axis=0, keepdims=True)
        dxn = dy * w_ref[...]
        dx_ref[...] = r * (dxn - xn * jnp.mean(dxn * xn, axis=-1, keepdims=True))

    return _call(body, name, (_sds((s, d), F32), _sds((1, LANES), F32), _sds((1, d), F32)), (s // tm,),
                 [_rows(tm, d), _vec(d), _rows(tm, d)], (_rows(tm, d), _vec(LANES), _vec(d)), ("arbitrary",))(x, w, target)


def _layer_fwd(l, x, mod, p, cosf, sinf, after=None, late=None, later=None):
    sh1, sc1, g1, sh2, sc2, g2 = mod
    tag = f"l{l}_"
    h, qr, kk0, kk1, vv0, vv1, u_pre, v_pre, ga_pre, gb_pre = _in_proj(
        tag + "in_proj", x, p["norm1_w"], sc1, sh1, p["w_in"], cosf, sinf, after=after)
    y_attn = _attention(tag + "attn", qr, kk0, kk1, vv0, vv1, p["sinks"])
    y_sgu = _sgu(tag + "sgu", u_pre, v_pre, p["sgu_ln_w"], p["sgu_ln_b"], p["sgu_w"], p["sgu_bfull"])
    if late is not None:
        p = dict(p, **late(y_sgu))
    merged, pa, pb, o, x1, h2 = _mix_out(tag + "mix_out", y_sgu, y_attn, ga_pre, gb_pre, x, g1, p["proj_a"], p["proj_b"],
                                         p["w_out"], p["norm2_w"], sc2, sh2)
    if later is not None:
        p = dict(p, **later(h2))
    a, up, hf = _ffn_up_act(tag + "ffn_up", h2, p["w_gate"], p["w_up"], p["conv_w"], p["conv_b"])
    dn, x2 = _ffn_down(tag + "ffn_down", hf, p["w_down"], x1, g2)
    saved = dict(x=x, h=h, qr=qr, kk0=kk0, kk1=kk1, vv0=vv0, vv1=vv1, u_pre=u_pre, v_pre=v_pre, ga_pre=ga_pre,
                 gb_pre=gb_pre, y_attn=y_attn, y_sgu=y_sgu, merged=merged, pa=pa, pb=pb, o=o, x1=x1, h2=h2, a=a, up=up,
                 hf=hf, dn=dn)
    return x2, saved, p


def _layer_bwd(l, dx2, mod, p, sv, cosf, sinf, after=None, emit=None):
    sh1, sc1, g1, sh2, sc2, g2 = mod
    tag = f"l{l}_b_"
    d = D_MODEL
    g = {}
    ready = (lambda names: emit({k: g.pop(k) for k in names})) if emit else (lambda names: None)
    ddn, da, dup, dg2, g["conv_w"], g["conv_b"] = _ffn_down_bwd_act(
        tag + "ffn_down", dx2, sv["dn"], g2, p["w_down"], sv["a"], sv["up"], p["conv_w"], p["conv_b"], after=after)
    g["w_down"] = _matmul_tn(tag + "dw_down", sv["hf"], ddn, tk=FFN_SHARD_PAD).reshape(N_CHIPS, FFN_SHARD_PAD, d)
    g["w_gate"] = _matmul_tn(tag + "dw_gate", da, sv["h2"], tk=FFN_SHARD_PAD).reshape(N_CHIPS, FFN_SHARD_PAD, d)
    g["w_up"] = _matmul_tn(tag + "dw_up", dup, sv["h2"], tk=FFN_SHARD_PAD).reshape(N_CHIPS, FFN_SHARD_PAD, d)
    dx1, do, da2, dsh2, dg1 = _ffn_up_bwd(tag + "ffn_up", da, dup, p["w_gate"], p["w_up"], sv["x1"], dx2, p["norm2_w"],
                                          sc2, sv["o"], g1, after=ready(("w_down", "w_gate", "w_up")))
    g["norm2_w"] = da2 * (1.0 + sc2)
    dsc2 = da2 * p["norm2_w"]
    g["w_out"] = _matmul_tn(tag + "dw_out", sv["merged"], do).reshape(N_CHIPS, d // N_CHIPS, d)
    dpa, dpb, dga, dgb, dy_sgu, dy_attn = _mix_bwd(tag + "mix", do, p["w_out"], p["proj_a"], p["proj_b"], sv["ga_pre"],
                                                  sv["gb_pre"], sv["pa"], sv["pb"])
    g["proj_a"] = _matmul_tn(tag + "dproj_a", sv["y_sgu"], dpa).reshape(N_CHIPS, d // N_CHIPS, d)
    g["proj_b"] = _matmul_tn(tag + "dproj_b", sv["y_attn"], dpb).reshape(N_CHIPS, d // N_CHIPS, d)
    du, dv, g["sgu_w"], g["sgu_b"], g["sgu_ln_w"], g["sgu_ln_b"] = _sgu_bwd(
        tag + "sgu", sv["u_pre"], sv["v_pre"], p["sgu_ln_w"], p["sgu_ln_b"], p["sgu_w"], p["sgu_bfull"], dy_sgu,
        after=ready(("w_out", "proj_a", "proj_b")))
    dqr, dkv_cur, dkv_prev, dsink = _attention_bwd(tag + "attn", sv["qr"], sv["kk0"], sv["kk1"], sv["vv0"], sv["vv1"],
                                                   p["sinks"], dy_attn)
    g["sinks"] = dsink[0, :N_Q_HEADS]
    dq, dkv = _rope_bwd(tag + "rope", dqr, dkv_cur, dkv_prev, cosf, sinf)
    dw_in, row0 = None, 0
    for n, t in (("q", dq), ("kv", dkv), ("u", du), ("v", dv), ("ga", dga), ("gb", dgb)):
        dw_in = _matmul_tn_rows(tag + "dw_in_" + n, t, sv["h"], dw_in, row0, IN_COLS)
        row0 += t.shape[1]
    g["w_in"] = dw_in.reshape(N_CHIPS, IN_COLS // N_CHIPS, d)
    dx, da1, dsh1 = _in_proj_bwd(tag + "in_proj", dq, dkv, du, dv, dga, dgb, p["w_in"], sv["x"], p["norm1_w"], sc1, dx1)
    g["norm1_w"] = da1 * (1.0 + sc1)
    dsc1 = da1 * p["norm1_w"]
    return dx, (dsh1, dsc1, dg1, dsh2, dsc2, dg2), g


def _pad_to(a, axis, size):
    pad = [(0, 0)] * a.ndim
    pad[axis] = (0, size - a.shape[axis])
    return jnp.pad(a, pad)


def _early_params(w_in, small):
    d = D_MODEL
    return dict(
        w_in=w_in.reshape(IN_COLS, d), norm1_w=small["norm1_w"].reshape(1, d), sinks=small["sinks"],
        sgu_ln_w=small["sgu_ln_w"].reshape(1, d), sgu_ln_b=small["sgu_ln_b"].reshape(1, d), sgu_w=small["sgu_w"],
        sgu_bfull=jnp.broadcast_to(small["sgu_b"][:, :, None], (SGU_GROUPS, SGU_CHUNK, SGU_CHUNK)))


def _mix_params(proj_a, proj_b, w_out, small):
    return dict(proj_a=proj_a, proj_b=proj_b, w_out=w_out, norm2_w=small["norm2_w"].reshape(1, D_MODEL))


def _ffn_params(w_gate, w_up, w_down, conv_w, small):
    return dict(
        w_gate=w_gate, w_up=w_up, w_down=w_down, conv_w=conv_w.transpose(1, 0, 2).reshape(3, FFN_PAD),
        conv_b=_pad_to(small["conv_b"].reshape(N_CHIPS, FFN_SHARD), 1, FFN_SHARD_PAD).reshape(1, FFN_PAD))


def _layer_params(w_in, proj_a, proj_b, w_out, w_gate, w_up, w_down, conv_w, small):
    return dict(_early_params(w_in, small), **_mix_params(proj_a, proj_b, w_out, small),
                **_ffn_params(w_gate, w_up, w_down, conv_w, small))


def _conv_grads_natural(g):
    cw = g["conv_w"].reshape(3, N_CHIPS, FFN_SHARD_PAD)[:, :, :FFN_SHARD].reshape(3, FFN_DIM)
    cb = g["conv_b"].reshape(N_CHIPS, FFN_SHARD_PAD)[:, :FFN_SHARD].reshape(FFN_DIM)
    return cw, cb


def _rope_tables(positions):
    inv_freq = ROPE_THETA ** (-jnp.arange(0, ROT_DIM, 2, dtype=F32) / ROT_DIM)
    ang = positions.astype(F32)[:, None] * inv_freq
    cos, sin = jnp.cos(ang), jnp.sin(ang)
    s = positions.shape[0]
    rest = HEAD_DIM - ROT_DIM
    cos_head = jnp.concatenate([cos, cos, jnp.ones((s, rest), F32)], axis=1)
    sin_head = jnp.concatenate([-sin, sin, jnp.zeros((s, rest), F32)], axis=1)
    return jnp.tile(cos_head, (1, LANES // HEAD_DIM)), jnp.tile(sin_head, (1, LANES // HEAD_DIM))


ADA_ROWS = 16


def _ada_fwd(name, c_rows, ada_w, ada_b_cols, tn=512):
    depth, d, n = ada_w.shape

    def body(c_ref, w_ref, b_ref, o_ref):
        cv = c_ref[...]
        act = (cv * _sigmoid(cv)).astype(BF)
        o_ref[0] = _dot(act, w_ref[0].astype(BF)) + b_ref[0]

    return _call(body, name, _sds((depth, ADA_ROWS, n), F32), (depth, n // tn),
                 [pl.BlockSpec((ADA_ROWS, d), lambda l, j: (0, 0)), pl.BlockSpec((1, d, tn), lambda l, j: (l, 0, j)),
                  pl.BlockSpec((1, 1, tn), lambda l, j: (l, 0, j))],
                 pl.BlockSpec((1, ADA_ROWS, tn), lambda l, j: (l, 0, j)), ("parallel", "parallel"))(c_rows, ada_w, ada_b_cols)


def _ada_bwd(name, c_rows, dmod_cols, tn=512):
    depth, _, n = dmod_cols.shape
    d = c_rows.shape[1]

    def body(c_ref, dm_ref, o_ref):
        cv = c_ref[...]
        act = (cv * _sigmoid(cv)).astype(BF)
        o_ref[0] = _dot_tn(act, dm_ref[0].astype(BF))

    return _call(body, name, _sds((depth, d, n), F32), (depth, n // tn),
                 [pl.BlockSpec((ADA_ROWS, d), lambda l, j: (0, 0)), pl.BlockSpec((1, ADA_ROWS, tn), lambda l, j: (l, 0, j))],
                 pl.BlockSpec((1, d, tn), lambda l, j: (l, 0, j)), ("parallel", "parallel"))(c_rows, dmod_cols)


def _colsum(name, a):
    r, n = a.shape

    def body(a_ref, o_ref):
        o_ref[...] = jnp.sum(a_ref[...], axis=0, keepdims=True)

    return _call(body, name, _sds((1, n), F32), (1,), [pl.BlockSpec((r, n), lambda i: (0, 0))],
                 pl.BlockSpec((1, n), lambda i: (0, 0)), ("arbitrary",))(a)


REL_SIBLING = (0, 0, 1)
REL_CHIPS = ((1, 0, 0), (0, 1, 0), (1, 1, 0))
REL_ALL = tuple((fx, fy, fc) for fx in (0, 1) for fy in (0, 1) for fc in (0, 1) if fx or fy or fc)


def _chip_of(dev):
    return 2 * dev[0] + dev[1]


def _dev_of(dev):
    return 4 * dev[0] + 2 * dev[1] + dev[2]


def _flip(dev, rel):
    return tuple(1 - m if f else m for m, f in zip(dev, rel))


def _exchange(name, arrays, n_out, stages, aliases=None):
    out_shapes, stages = stages[0], stages[1:]
    n_in = len(arrays)
    aliases = aliases or {}
    n_remote = sum(len(plan) for plan, _ in stages)
    n_local = sum(len(local) for _, local in stages)

    def at(ref, idx):
        return ref.at[idx] if len(idx) else ref

    def body(*refs):
        bufs = list(refs[:n_in + n_out])
        for i_in, i_out in aliases.items():
            bufs[i_in] = bufs[n_in + i_out]
        send_sems, recv_sems, local_sems = refs[n_in + n_out:]
        me = (lax.axis_index("x"), lax.axis_index("y"), lax.axis_index("c"))
        base_r = base_l = 0
        pending = []
        for plan, local in stages:
            def remote(k, entry, sender, receiver):
                rel, si, ssel, di, dsel = entry
                return pltpu.make_async_remote_copy(
                    src_ref=at(bufs[si], ssel(sender, receiver)), dst_ref=at(bufs[di], dsel(sender, receiver)),
                    send_sem=send_sems.at[k], recv_sem=recv_sems.at[k], device_id=_flip(me, rel), device_id_type=MESH)

            sends = [remote(base_r + k, e, me, _flip(me, e[0])) for k, e in enumerate(plan)]
            for cp in sends:
                cp.start()
            for k, (si, ssel, di, dsel) in enumerate(local):
                cp = pltpu.make_async_copy(at(bufs[si], ssel(me)), at(bufs[di], dsel(me)), local_sems.at[base_l + k])
                cp.start()
                pending.append(cp.wait)
            for k, e in enumerate(plan):
                remote(base_r + k, e, _flip(me, e[0]), me).wait_recv()
            pending += [cp.wait_send for cp in sends]
            base_r += len(plan)
            base_l += len(local)
        for wait in pending:
            wait()

    any_spec = pl.BlockSpec(memory_space=pl.ANY)
    return pl.pallas_call(
        body, name=name, out_shape=tuple(out_shapes), in_specs=[any_spec] * n_in, out_specs=tuple([any_spec] * n_out),
        input_output_aliases=dict(aliases),
        scratch_shapes=[pltpu.SemaphoreType.DMA((max(n_remote, 1),)), pltpu.SemaphoreType.DMA((max(n_remote, 1),)),
                        pltpu.SemaphoreType.DMA((max(n_local, 1),))])(*arrays)


HBM_SPEC = pl.BlockSpec(memory_space=pltpu.HBM)
SEM_SPEC = pl.BlockSpec(memory_space=pltpu.SEMAPHORE)


def _split_copies(bufs, plan, local, send_sems, recv_sems, local_sems):
    me = (lax.axis_index("x"), lax.axis_index("y"), lax.axis_index("c"))

    def at(ref, idx):
        return ref.at[idx] if len(idx) else ref

    def remote(k, sender, receiver):
        rel, si, ssel, di, dsel = plan[k]
        return pltpu.make_async_remote_copy(
            src_ref=at(bufs[si], ssel(sender, receiver)), dst_ref=at(bufs[di], dsel(sender, receiver)),
            send_sem=send_sems.at[k], recv_sem=recv_sems.at[k], device_id=_flip(me, rel), device_id_type=MESH)

    sends = [remote(k, me, _flip(me, plan[k][0])) for k in range(len(plan))]
    arrivals = [remote(k, _flip(me, plan[k][0]), me) for k in range(len(plan))]
    locs = [pltpu.make_async_copy(at(bufs[si], ssel(me)), at(bufs[di], dsel(me)), local_sems.at[k])
            for k, (si, ssel, di, dsel) in enumerate(local)]
    return sends, arrivals, locs


def _exchange_start(name, arrays, out_shapes, plan, local):
    n_in, n_out = len(arrays), len(out_shapes)

    n_buf = n_in + n_out

    def body(*refs):
        sems = refs[n_buf:n_buf + 3]
        bufs = refs[n_buf + 3:2 * n_buf + 3]
        token = refs[-1]
        sends, _, locs = _split_copies(bufs, plan, local, *sems)
        for cp in sends + locs:
            cp.start()
        token[...] = jnp.zeros_like(token)

    zones = [lax.empty(o.shape, o.dtype) for o in out_shapes]
    operands = [pltpu.with_memory_space_constraint(a, pltpu.HBM) for a in list(arrays) + zones]
    sem = lambda n: pltpu.SemaphoreType.DMA((max(n, 1),))
    out = pl.pallas_call(
        body, name=name,
        out_shape=(sem(len(plan)), sem(len(plan)), sem(len(local)), *[pltpu.HBM(a.shape, a.dtype) for a in operands],
                   _sds((SUBLANES, LANES), F32)),
        in_specs=[HBM_SPEC] * (n_in + n_out),
        out_specs=(SEM_SPEC, SEM_SPEC, SEM_SPEC, *[HBM_SPEC] * (n_in + n_out), pl.BlockSpec(memory_space=pltpu.VMEM)),
        input_output_aliases={i: 3 + i for i in range(n_in + n_out)},
        compiler_params=pltpu.CompilerParams(has_side_effects=pltpu.SideEffectType.DATAFLOW_SIDE_EFFECTING))(*operands)
    return out[:3], out[3:3 + n_in], out[3 + n_in:3 + n_in + n_out], out[-1]


def _exchange_wait(name, sems, thru, zones, plan, local, after):
    n_in, n_out = len(thru), len(zones)

    def body(*refs):
        bufs = refs[:n_in + n_out]
        sends, arrivals, locs = _split_copies(bufs, plan, local, *refs[n_in + n_out:n_in + n_out + 3])
        for cp in arrivals:
            cp.wait_recv()
        for cp in sends:
            cp.wait_send()
        for cp in locs:
            cp.wait()

    out = pl.pallas_call(
        body, name=name, out_shape=tuple(pltpu.HBM(a.shape, a.dtype) for a in list(thru) + list(zones)),
        in_specs=[HBM_SPEC] * (n_in + n_out) + [SEM_SPEC] * 3 + [pl.BlockSpec(memory_space=pl.ANY)],
        out_specs=tuple([HBM_SPEC] * (n_in + n_out)), input_output_aliases={i: i for i in range(n_in + n_out)},
        compiler_params=pltpu.CompilerParams(has_side_effects=pltpu.SideEffectType.DATAFLOW_SIDE_EFFECTING))(
            *thru, *zones, *sems, after)
    return out[n_in:]


def _whole(*_):
    return ()


def _half_rows(rows, core):
    return pl.ds(core * (rows // 2), rows // 2)


def _gather_weights_plan(shards):
    n = len(shards)
    dsts = [_sds((N_CHIPS,) + a.shape, a.dtype) for a in shards]
    fetch, forward = [], []
    for t, a in enumerate(shards):
        rows = a.shape[0]
        if rows % (2 * 16) == 0:
            fetch += [(rel, t, (lambda s_, r_, rows=rows: (_half_rows(rows, s_[2]),)), n + t,
                       (lambda s_, r_, rows=rows: (_chip_of(s_), _half_rows(rows, s_[2])))) for rel in REL_CHIPS]
            forward += [(REL_SIBLING, n + t, (lambda s_, r_, rows=rows, rel=rel: (_chip_of(_flip(s_, rel)), _half_rows(rows, s_[2]))),
                         n + t, (lambda s_, r_, rows=rows, rel=rel: (_chip_of(_flip(s_, rel)), _half_rows(rows, s_[2]))))
                        for rel in REL_CHIPS]
        else:
            fetch += [(rel, t, _whole, n + t, lambda s_, r_: (_chip_of(s_),)) for rel in REL_CHIPS]
    local = [(t, _whole, n + t, lambda me: (_chip_of(me),)) for t in range(n)]
    return dsts, fetch, local, forward


def _gather_weights_start(name, shards):
    dsts, fetch, local, forward = _gather_weights_plan(shards)
    sems, thru, zones, token = _exchange_start(name, shards, dsts, fetch, local)
    return (sems, thru, zones, fetch, local, forward), token


def _gather_weights_finish(name, pending, after):
    sems, thru, zones, fetch, local, forward = pending
    landed = _exchange_wait(name + "_wait", sems, thru, zones, fetch, local, after)
    n = len(landed)
    return _exchange(name + "_forward", landed, n, [[_sds(a.shape, a.dtype) for a in landed], (forward, [])],
                     aliases={t: t for t in range(n)})


def _gather_chips(name, arrays):
    n = len(arrays)
    dsts = [_sds((N_CHIPS,) + a.shape, a.dtype) for a in arrays]
    plan = [(rel, t, _whole, n + t, lambda s_, r_: (_chip_of(s_),)) for t in range(n) for rel in REL_CHIPS]
    local = [(t, _whole, n + t, lambda me: (_chip_of(me),)) for t in range(n)]
    return _exchange(name, arrays, n, [dsts, (plan, local)])


def _gather_all(name, a):
    plan = [(rel, 0, _whole, 1, lambda s_, r_: (_dev_of(s_),)) for rel in REL_ALL]
    local = [(0, _whole, 1, lambda me: (_dev_of(me),))]
    return _exchange(name, [a], 1, [[_sds((2 * N_CHIPS,) + a.shape, a.dtype)], (plan, local)])[0]


def _swap_halves(name, grads):
    n = len(grads)
    dsts = [_sds((g.shape[0], g.shape[1] // 2, g.shape[2]), g.dtype) for g in grads]
    plan = [(REL_SIBLING, t, (lambda s_, r_, rows=g.shape[1]: (pl.ds(0, N_CHIPS), _half_rows(rows, r_[2]))), n + t, _whole)
            for t, g in enumerate(grads)]
    return _exchange(name, grads, n, [dsts, (plan, [])])


def _scatter_chips_plan(sums):
    n = len(sums)
    dsts = [_sds(a.shape, a.dtype) for a in sums]
    plan = [(rel, t, lambda s_, r_: (_chip_of(r_),), n + t, lambda s_, r_: (_chip_of(s_),))
            for t in range(n) for rel in REL_CHIPS]
    local = [(t, lambda me: (_chip_of(me),), n + t, lambda me: (_chip_of(me),)) for t in range(n)]
    return dsts, plan, local


def _scatter_chips(name, sums):
    dsts, plan, local = _scatter_chips_plan(sums)
    return _exchange(name, sums, len(sums), [dsts, (plan, local)])


def _scatter_chips_start(name, sums):
    dsts, plan, local = _scatter_chips_plan(sums)
    sems, thru, zones, token = _exchange_start(name, sums, dsts, plan, local)
    return (sems, thru, zones, plan, local), token


def _scatter_chips_finish(name, pending, after):
    sems, thru, zones, plan, local = pending
    return _exchange_wait(name + "_wait", sems, thru, zones, plan, local, after)


def _swap_back(name, totals, layer):
    n = len(totals)
    dsts = [_sds(a.shape, a.dtype) for a in totals]
    plan = [(REL_SIBLING, n + t, (lambda s_, r_, rows=a.shape[1]: (layer, _half_rows(rows, s_[2]))),
             n + t, (lambda s_, r_, rows=a.shape[1]: (layer, _half_rows(rows, s_[2])))) for t, a in enumerate(totals)]
    return _exchange(name, totals, n, [dsts, (plan, [])], aliases={t: t for t in range(n)})


def _add_halves(name, g, recv, core):
    nch, half, c = recv.shape

    def body(core_ref, g_ref, r_ref, o_ref):
        o_ref[0] = (g_ref[0, 0].astype(F32) + r_ref[0].astype(F32)).astype(o_ref.dtype)

    spec = pltpu.PrefetchScalarGridSpec(
        num_scalar_prefetch=1, grid=(nch,),
        in_specs=[pl.BlockSpec((1, 1, half, c), lambda k, core_ref: (k, core_ref[0], 0, 0)),
                  pl.BlockSpec((1, half, c), lambda k, core_ref: (k, 0, 0))],
        out_specs=pl.BlockSpec((1, half, c), lambda k, core_ref: (k, 0, 0)))
    return pl.pallas_call(body, name=name, out_shape=_sds(recv.shape, recv.dtype), grid_spec=spec,
                          compiler_params=pltpu.CompilerParams(dimension_semantics=("parallel",),
                                                               vmem_limit_bytes=VMEM_LIMIT))(
                                                                   core, g.reshape(nch, 2, half, c), recv)


def _sum_chips(name, a, core, layer, total):
    nch, half, c = a.shape

    def body(core_ref, a_ref, *rest):
        o_ref = rest[-1]
        acc = a_ref[0].astype(F32)
        for k in range(1, nch):
            acc = acc + a_ref[k].astype(F32)
        o_ref[0, 0] = acc

    in_specs = [pl.BlockSpec((nch, half, c), lambda i, core_ref: (0, 0, 0))]
    args = [core, a]
    if total is not None:
        in_specs.append(pl.BlockSpec(memory_space=pl.ANY))
        args.append(total.reshape(DEPTH, 2, half, c))
    spec = pltpu.PrefetchScalarGridSpec(
        num_scalar_prefetch=1, grid=(1,), in_specs=in_specs,
        out_specs=pl.BlockSpec((1, 1, half, c), lambda i, core_ref: (layer, core_ref[0], 0, 0)))
    out = pl.pallas_call(body, name=name, out_shape=_sds((DEPTH, 2, half, c), F32), grid_spec=spec,
                         input_output_aliases={2: 0} if total is not None else {},
                         compiler_params=pltpu.CompilerParams(dimension_semantics=("arbitrary",),
                                                              vmem_limit_bytes=VMEM_LIMIT))(*args)
    return out.reshape(DEPTH, 2 * half, c)


def _adamw_update(w, g, m, v):
    mn = ADAM_B1 * m + (1.0 - ADAM_B1) * g
    vn = ADAM_B2 * v + (1.0 - ADAM_B2) * (g * g)
    m_hat = mn / (1.0 - ADAM_B1 ** ADAM_STEP)
    v_hat = vn / (1.0 - ADAM_B2 ** ADAM_STEP)
    return -ADAM_LR * (m_hat / (jnp.sqrt(v_hat) + ADAM_EPS) + ADAM_WD * w), mn, vn


def _adamw(name, w, g, m, v):
    depth, r, c = w.shape
    tr = next(t for t in (512, 448, 384, 352, 336, 256, 192, 128, 64, 32, 16, 8) if r % t == 0 and t * c <= ADAM_TILE_ELEMS)

    def body(w_ref, g_ref, m_ref, v_ref, go_ref, d_ref, mo_ref, vo_ref):
        gv = g_ref[...]
        go_ref[...] = gv
        d_ref[...], mo_ref[...], vo_ref[...] = _adamw_update(w_ref[...], gv, m_ref[...], v_ref[...])

    spec = pl.BlockSpec((1, tr, c), lambda l, i: (l, i, 0))
    o = _sds(w.shape, F32)
    return _call(body, name, (o, o, o, o), (depth, r // tr), [spec] * 4, (spec,) * 4, ("parallel", "parallel"))(w, g, m, v)


def _adamw_small(name, ws, gs, ms, vs):
    n = len(ws)

    def body(*refs):
        for t in range(n):
            w_ref, g_ref, m_ref, v_ref = (refs[k * n + t] for k in range(4))
            d_ref, mo_ref, vo_ref = (refs[(4 + k) * n + t] for k in range(3))
            d_ref[...], mo_ref[...], vo_ref[...] = _adamw_update(w_ref[...], g_ref[...], m_ref[...], v_ref[...])

    outs = [_sds(w.shape, F32) for w in ws]
    res = pl.pallas_call(body, name=name, out_shape=tuple(outs * 3))(*ws, *gs, *ms, *vs)
    return res[:n], res[n:2 * n], res[2 * n:]


def _pack(arrays, rows):
    flat = jnp.concatenate([a.reshape(-1).astype(F32) for a in arrays])
    return _pad_to(flat, 0, rows * LANES).reshape(rows, LANES)


def _unpack(packed, shapes):
    flat = packed.reshape(-1)
    out, off = [], 0
    for shp in shapes:
        n = 1
        for s_ in shp:
            n *= s_
        out.append(flat[off:off + n].reshape(shp))
        off += n
    return out


_MATRICES = ("w_in", "proj_a", "proj_b", "w_out", "w_gate", "w_up", "w_down")
_SMALL = (("norm1_w", (D_MODEL,)), ("sinks", (N_Q_HEADS,)), ("sgu_ln_w", (SGU_WIDTH,)), ("sgu_ln_b", (SGU_WIDTH,)),
          ("sgu_w", (SGU_GROUPS, SGU_CHUNK, SGU_CHUNK)), ("sgu_b", (SGU_GROUPS, SGU_CHUNK)), ("norm2_w", (D_MODEL,)),
          ("conv_w", (3, FFN_DIM)), ("conv_b", (FFN_DIM,)), ("final_norm_w", (D_MODEL,)))
SMALL_ROWS = 320
ADAM_TILE_ELEMS = 384 * 1024


def _reduce_cores(tag, partial, core):
    names = list(partial)
    from_sibling = _swap_halves(tag + "_cores", [partial[k] for k in names])
    return names, [_add_halves(f"{tag}_cores_add_{k}", partial[k], r, core) for k, r in zip(names, from_sibling)]


def _reduce_finish(tag, l, names, from_chips, core, totals):
    sums = [_sum_chips(f"{tag}_chips_add_{k}", a, core, l, totals.get(k)) for k, a in zip(names, from_chips)]
    return dict(totals, **dict(zip(names, _swap_back(tag + "_back", sums, l))))


def kernel(x, c, positions, ada_w, ada_b, norm1_w, w_in, attn_sinks, sgu_ln_w, sgu_ln_b, sgu_w, sgu_b, proj_a, proj_b, w_out, norm2_w, ffn_w_gate, ffn_w_up, ffn_conv_w, ffn_conv_b, ffn_w_down, final_norm_w, loss_target, m_ada_w, m_ada_b, m_norm1_w, m_w_in, m_attn_sinks, m_sgu_ln_w, m_sgu_ln_b, m_sgu_w, m_sgu_b, m_proj_a, m_proj_b, m_w_out, m_norm2_w, m_ffn_w_gate, m_ffn_w_up, m_ffn_conv_w, m_ffn_conv_b, m_ffn_w_down, m_final_norm_w, v_ada_w, v_ada_b, v_norm1_w, v_w_in, v_attn_sinks, v_sgu_ln_w, v_sgu_ln_b, v_sgu_w, v_sgu_b, v_proj_a, v_proj_b, v_w_out, v_norm2_w, v_ffn_w_gate, v_ffn_w_up, v_ffn_conv_w, v_ffn_conv_b, v_ffn_w_down, v_final_norm_w):
    d = D_MODEL
    ax, ay, ac = lax.axis_index("x"), lax.axis_index("y"), lax.axis_index("c")
    chip = 2 * ax + ay
    dev = 4 * ax + 2 * ay + ac
    core = ac.astype(jnp.int32).reshape(1)

    c_all = _gather_all("gather_cond", c.reshape(SUBLANES, d // SUBLANES)).reshape(2 * N_CHIPS, d)
    c_rows = _pad_to(c_all, 0, ADA_ROWS)
    ada_cols = ada_w.shape[2]
    ada_b_cols = lax.dynamic_slice_in_dim(ada_b, chip * ada_cols, ada_cols, axis=1).reshape(DEPTH, 1, ada_cols)
    mod_cols = _ada_fwd("ada_fwd", c_rows, ada_w, ada_b_cols)
    mod_all = _gather_chips("gather_mod", [mod_cols])[0]
    mod_mine = lax.dynamic_index_in_dim(mod_all, dev, axis=2, keepdims=False)
    mod_mine = mod_mine.transpose(1, 0, 2).reshape(DEPTH, 1, 6 * d)
    mods = [tuple(jnp.split(mod_mine[l], 6, axis=-1)) for l in range(DEPTH)]

    tr = lambda a: jnp.swapaxes(a, 1, 2)
    shards = [tr(w_in).astype(BF), proj_a.astype(BF), proj_b.astype(BF), w_out.astype(BF),
              _pad_to(tr(ffn_w_gate).astype(BF), 1, FFN_SHARD_PAD), _pad_to(tr(ffn_w_up).astype(BF), 1, FFN_SHARD_PAD),
              _pad_to(ffn_w_down.astype(BF), 1, FFN_SHARD_PAD), _pad_to(ffn_conv_w, 2, FFN_SHARD_PAD)]
    token = mod_all[0, 0, :SUBLANES, :LANES]
    fetches = []
    for l in range(DEPTH):
        groups = []
        for tag, members in (("in", shards[:1]), ("mix", shards[1:4]), ("ffn", shards[4:])):
            behind = (token[0, 0] * 0.0).astype(members[0].dtype)
            pending, token = _gather_weights_start(f"l{l}_gather_{tag}", [members[0][l] + behind] + [a[l] for a in members[1:]])
            groups.append(pending)
        fetches.append(groups)

    small_in = dict(norm1_w=norm1_w, sinks=attn_sinks, sgu_ln_w=sgu_ln_w, sgu_ln_b=sgu_ln_b, sgu_w=sgu_w, sgu_b=sgu_b,
                    norm2_w=norm2_w, conv_b=ffn_conv_b)
    cosf, sinf = _rope_tables(positions[0])
    small_of = lambda l: {k: v[l] for k, v in small_in.items()}

    h = x[0]
    saved, params = [], []
    for l in range(DEPTH):
        first, mix, ffn = fetches[l]
        w_in_l = _gather_weights_finish(f"l{l}_gather_in", first, token if l == 0 else h)
        late = lambda y, l=l, mix=mix: _mix_params(*_gather_weights_finish(f"l{l}_gather_mix", mix, y), small_of(l))
        later = lambda y, l=l, ffn=ffn: _ffn_params(*_gather_weights_finish(f"l{l}_gather_ffn", ffn, y), small_of(l))
        h, sv, p = _layer_fwd(l, h, mods[l], _early_params(w_in_l[0], small_of(l)), cosf, sinf, late=late, later=later)
        saved.append(sv)
        params.append(p)
    dx, loss_part, d_final = _loss_head("loss_head", h, final_norm_w.reshape(1, d), loss_target[0])
    loss = lax.psum(loss_part[0, 0], ("x", "y", "c"))

    def small_pack(l, grads):
        cw, cb = _conv_grads_natural(grads)
        nat = dict(grads, conv_w=cw, conv_b=cb, final_norm_w=d_final if l == DEPTH - 1 else jnp.zeros((d,), F32))
        return _pack([nat[k] for k, _ in _SMALL], N_CHIPS * SMALL_ROWS).reshape(N_CHIPS, SMALL_ROWS, LANES)

    inflight = []

    def send(tag, l, partial):
        names, sums = _reduce_cores(tag, partial, core)
        pending, token = _scatter_chips_start(tag + "_chips", sums)
        inflight.append((tag, l, names, pending))
        return token

    dmods = [None] * DEPTH
    dx, dmods[1], grads = _layer_bwd(1, dx, mods[1], params[1], saved[1], cosf, sinf)
    token = send("l1_reduce", 1, dict({k: grads[k] for k in _MATRICES}, small=small_pack(1, grads)))
    dx, dmods[0], grads = _layer_bwd(0, dx, mods[0], params[0], saved[0], cosf, sinf, after=token,
                                     emit=lambda part: send("l0_reduce_" + "_".join(part), 0, part))
    dmod_mine = jnp.concatenate([jnp.concatenate(dmods[l], axis=1) for l in range(DEPTH)], axis=1)
    dmod_all = _gather_all("gather_dmod", dmod_mine.reshape(SUBLANES, -1)).reshape(2 * N_CHIPS, DEPTH * 6 * d)
    send("l0_reduce_in", 0, dict(w_in=grads["w_in"], small=small_pack(0, grads) + dmod_all[0, 0] * 0.0))
    totals = {}
    for tag, l, names, pending in inflight[:-1]:
        totals = _reduce_finish(tag, l, names, _scatter_chips_finish(tag + "_chips", pending, dx), core, totals)

    g_ada_b = _colsum("ada_b_grad", dmod_all).reshape(DEPTH, 6 * d)
    dmod_cols = jnp.stack([lax.dynamic_slice_in_dim(dmod_all, l * 6 * d + chip * ada_cols, ada_cols, axis=1)
                           for l in range(DEPTH)])
    g_ada_w = _ada_bwd("ada_w_grad", c_rows, _pad_to(dmod_cols, 1, ADA_ROWS))
    big = dict(w_in=(tr(w_in), tr(m_w_in), tr(v_w_in)), proj_a=(proj_a, m_proj_a, v_proj_a), proj_b=(proj_b, m_proj_b, v_proj_b),
               w_out=(w_out, m_w_out, v_w_out), w_gate=(tr(ffn_w_gate), tr(m_ffn_w_gate), tr(v_ffn_w_gate)),
               w_up=(tr(ffn_w_up), tr(m_ffn_w_up), tr(v_ffn_w_up)), w_down=(ffn_w_down, m_ffn_w_down, v_ffn_w_down))
    upd, g_big = {}, {}

    def update(k):
        res = _adamw("adamw_" + k, big[k][0], totals[k], *big[k][1:])
        res = [tr(a) for a in res] if k in ("w_in", "w_gate", "w_up") else res
        g_big[k], upd[k] = res[0], res[1:]

    for k in ("w_down", "w_gate", "w_up", "w_out", "proj_a", "proj_b"):
        update(k)
    g_big["ada_w"], *upd["ada_w"] = _adamw("adamw_ada_w", ada_w, g_ada_w, m_ada_w, v_ada_w)
    tag, l, names, pending = inflight[-1]
    totals = _reduce_finish(tag, l, names, _scatter_chips_finish(tag + "_chips", pending, upd["ada_w"][0]), core, totals)
    update("w_in")

    small_all = _gather_chips("gather_small", [totals["small"]])[0]
    small_g = small_all.transpose(1, 0, 2, 3).reshape(DEPTH, -1)
    per_layer = [_unpack(small_g[l], [shp for _, shp in _SMALL]) for l in range(DEPTH)]
    sg = {k: jnp.stack([per_layer[l][i] for l in range(DEPTH)]) for i, (k, _) in enumerate(_SMALL)}
    g_final = sg["final_norm_w"][DEPTH - 1]
    g_conv_w = lax.dynamic_slice_in_dim(sg["conv_w"], chip * FFN_SHARD, FFN_SHARD, axis=2)

    rest = [("ada_b", ada_b, g_ada_b, m_ada_b, v_ada_b), ("norm1_w", norm1_w, sg["norm1_w"], m_norm1_w, v_norm1_w),
            ("attn_sinks", attn_sinks, sg["sinks"], m_attn_sinks, v_attn_sinks),
            ("sgu_ln_w", sgu_ln_w, sg["sgu_ln_w"], m_sgu_ln_w, v_sgu_ln_w),
            ("sgu_ln_b", sgu_ln_b, sg["sgu_ln_b"], m_sgu_ln_b, v_sgu_ln_b), ("sgu_w", sgu_w, sg["sgu_w"], m_sgu_w, v_sgu_w),
            ("sgu_b", sgu_b, sg["sgu_b"], m_sgu_b, v_sgu_b), ("norm2_w", norm2_w, sg["norm2_w"], m_norm2_w, v_norm2_w),
            ("ffn_conv_w", ffn_conv_w, g_conv_w, m_ffn_conv_w, v_ffn_conv_w),
            ("ffn_conv_b", ffn_conv_b, sg["conv_b"], m_ffn_conv_b, v_ffn_conv_b),
            ("final_norm_w", final_norm_w.reshape(1, d), g_final.reshape(1, d), m_final_norm_w.reshape(1, d),
             v_final_norm_w.reshape(1, d))]
    rest_out = _adamw_small("adamw_rest", *[[r[i] for r in rest] for i in (1, 2, 3, 4)])
    g_rest = {r[0]: r[2] for r in rest}
    u_rest = {r[0]: tuple(o[i] for o in rest_out) for i, r in enumerate(rest)}
    g_rest["final_norm_w"] = g_final
    u_rest["final_norm_w"] = tuple(a.reshape(d) for a in u_rest["final_norm_w"])

    names = ("ada_w", "ada_b", "norm1_w", "w_in", "attn_sinks", "sgu_ln_w", "sgu_ln_b", "sgu_w", "sgu_b", "proj_a", "proj_b",
             "w_out", "norm2_w", "ffn_w_gate", "ffn_w_up", "ffn_conv_w", "ffn_conv_b", "ffn_w_down", "final_norm_w")
    alias = {"ffn_w_gate": "w_gate", "ffn_w_up": "w_up", "ffn_w_down": "w_down"}
    grad_of = lambda n: g_rest[n] if n in g_rest else g_big[alias.get(n, n)]
    upd_of = lambda n: u_rest[n] if n in u_rest else upd[alias.get(n, n)]
    return (loss, dx[None], *[grad_of(n) for n in names], *[upd_of(n)[0] for n in names],
            *[upd_of(n)[1] for n in names], *[upd_of(n)[2] for n in names])
```

```python
import jax
import jax.numpy as jnp
from jax import lax
from jax.experimental import pallas as pl
from jax.experimental.pallas import tpu as pltpu

F32 = jnp.float32
BF = jnp.bfloat16

D_MODEL = 1024
N_Q_HEADS = 16
N_KV_HEADS = 2
HEAD_DIM = 64
ATTN_BLOCK = 128
ROPE_THETA = 500000.0
ROT_DIM = HEAD_DIM // 4
SGU_WIDTH = 1024
SGU_GROUPS = 8
SGU_CHUNK = 128
FFN_DIM = 2816
NORM_EPS = 1e-6
DEPTH = 2
IN_COLS = 5376
N_CHIPS = 4
FFN_SHARD = FFN_DIM // N_CHIPS
FFN_SHARD_PAD = 768
FFN_PAD = N_CHIPS * FFN_SHARD_PAD
LANES = 128
SUBLANES = 8
HALO = 16
VMEM_LIMIT = 56 * 1024 * 1024
NEG_BIG = -1e30

ADAM_LR = 0.001
ADAM_B1 = 0.9
ADAM_B2 = 0.999
ADAM_EPS = 1e-08
ADAM_WD = 0.01
ADAM_STEP = 10

MESH = pl.DeviceIdType.MESH

Q_END = 1024
KV_END = 1280
U_END = 2304
Z_END = 3328
GA_END = 4352


def _sds(shape, dtype):
    return jax.ShapeDtypeStruct(tuple(shape), dtype)


def _call(body, name, out_shape, grid, in_specs, out_specs, semantics, scratch=(), after=None):
    n_in = len(in_specs)
    fn = body
    if after is not None:
        def fn(*refs):
            return body(*refs[:n_in], *refs[n_in + 1:])

        in_specs = list(in_specs) + [pl.BlockSpec(memory_space=pl.ANY)]
    call = pl.pallas_call(
        fn, name=name, out_shape=out_shape, grid=grid, in_specs=in_specs, out_specs=out_specs,
        scratch_shapes=scratch,
        compiler_params=pltpu.CompilerParams(dimension_semantics=semantics, vmem_limit_bytes=VMEM_LIMIT))
    if after is None:
        return call
    return lambda *args: call(*args, after)


def _rows(tm, width, col=0):
    return pl.BlockSpec((tm, width), lambda i: (i, col))


def _vec(width):
    return pl.BlockSpec((1, width), lambda i: (0, 0))


def _resident(shape):
    zeros = (0,) * len(shape)
    return pl.BlockSpec(tuple(shape), lambda *_: zeros, pipeline_mode=pl.Buffered(1))


def _sigmoid(x):
    return 0.5 + 0.5 * jnp.tanh(0.5 * x)


def _gelu(x):
    return 0.5 * x * (1.0 + lax.erf(x * 0.7071067811865476))


def _gelu_grad(x):
    cdf = 0.5 * (1.0 + lax.erf(x * 0.7071067811865476))
    return cdf + x * jnp.exp(-0.5 * x * x) * 0.3989422804014327


def _dot(a, b):
    return jnp.dot(a, b, preferred_element_type=F32)


def _dot_nt(a, b):
    return lax.dot_general(a, b, (((1,), (1,)), ((), ())), preferred_element_type=F32)


def _dot_tn(a, b):
    return lax.dot_general(a, b, (((0,), (0,)), ((), ())), preferred_element_type=F32)


def _rms(xv):
    return lax.rsqrt(jnp.mean(xv * xv, axis=-1, keepdims=True) + NORM_EPS)


def _matmul_tn(name, a, b, tk=512, tn=1024, blocked=False):
    s, k = a.shape
    n = b.shape[1]
    tk, tn = min(tk, k), min(tn, n)

    def body(a_ref, b_ref, o_ref):
        res = _dot_tn(a_ref[...], b_ref[...]).astype(o_ref.dtype)
        if blocked:
            o_ref[0] = res
        else:
            o_ref[...] = res

    if blocked:
        out, ospec = _sds((n // tn, k, tn), BF), pl.BlockSpec((1, tk, tn), lambda i, j: (j, i, 0))
    else:
        out, ospec = _sds((k, n), BF), pl.BlockSpec((tk, tn), lambda i, j: (i, j))
    return _call(body, name, out, (k // tk, n // tn),
                 [pl.BlockSpec((s, tk), lambda i, j: (0, i)), pl.BlockSpec((s, tn), lambda i, j: (0, j))],
                 ospec, ("parallel", "parallel"))(a, b)


def _matmul_tn_rows(name, a, b, out, row0, rows_total, tk=256):
    s, k = a.shape
    n = b.shape[1]

    def body(a_ref, b_ref, *rest):
        rest[-1][...] = _dot_tn(a_ref[...], b_ref[...]).astype(BF)

    in_specs = [pl.BlockSpec((s, tk), lambda i: (0, i)), _resident(b.shape)]
    args = [a, b]
    if out is not None:
        in_specs.append(pl.BlockSpec(memory_space=pl.ANY))
        args.append(out)
    return pl.pallas_call(
        body, name=name, out_shape=_sds((rows_total, n), BF), grid=(k // tk,), in_specs=in_specs,
        out_specs=pl.BlockSpec((tk, n), lambda i: (row0 // tk + i, 0)),
        input_output_aliases={2: 0} if out is not None else {},
        compiler_params=pltpu.CompilerParams(dimension_semantics=("parallel",), vmem_limit_bytes=VMEM_LIMIT))(*args)


def _rope_partner(v):
    lane = lax.broadcasted_iota(jnp.int32, (1, LANES), 1) % HEAD_DIM
    return jnp.where(lane < ROT_DIM // 2, pltpu.roll(v, LANES - ROT_DIM // 2, axis=1), pltpu.roll(v, ROT_DIM // 2, axis=1))


def _dup_half(v, half):
    lane = lax.broadcasted_iota(jnp.int32, (1, LANES), 1)
    keep = jnp.where((lane >= HEAD_DIM) == (half == 1), v, 0.0)
    return keep + pltpu.roll(keep, HEAD_DIM, axis=1)


def _in_proj(name, x, w, sc, sh, w_in, cosf, sinf, tm=256, after=None):
    s, d = x.shape
    tm = min(tm, s)

    def body(x_ref, w_ref, sc_ref, sh_ref, win_ref, cos_ref, sin_ref,
             h_ref, qr_ref, kk0_ref, kk1_ref, vv0_ref, vv1_ref, u_ref, v_ref, ga_ref, gb_ref):
        xv = x_ref[...]
        h = ((xv * _rms(xv)) * w_ref[...] * (1.0 + sc_ref[...]) + sh_ref[...]).astype(BF)
        h_ref[...] = h
        cosv, sinv = cos_ref[...], sin_ref[...]
        q = _dot_nt(h, win_ref[:Q_END, :])
        for j in range(D_MODEL // LANES):
            qv = q[:, j * LANES:(j + 1) * LANES]
            qr_ref[:, j * LANES:(j + 1) * LANES] = (qv * cosv + _rope_partner(qv) * sinv).astype(BF)
        kv = _dot_nt(h, win_ref[Q_END:KV_END, :])
        kr = kv[:, :LANES] * cosv + _rope_partner(kv[:, :LANES]) * sinv
        vv = kv[:, LANES:]
        kk0_ref[...] = _dup_half(kr, 0).astype(BF)
        kk1_ref[...] = _dup_half(kr, 1).astype(BF)
        vv0_ref[...] = _dup_half(vv, 0).astype(BF)
        vv1_ref[...] = _dup_half(vv, 1).astype(BF)
        u_ref[...] = _dot_nt(h, win_ref[KV_END:U_END, :])
        v_ref[...] = _dot_nt(h, win_ref[U_END:Z_END, :])
        ga_ref[...] = _dot_nt(h, win_ref[Z_END:GA_END, :]).astype(BF)
        gb_ref[...] = _dot_nt(h, win_ref[GA_END:, :]).astype(BF)

    wide, kvs, pre = _sds((s, d), BF), _sds((s, LANES), BF), _sds((s, d), F32)
    return _call(body, name, (wide, wide, kvs, kvs, kvs, kvs, pre, pre, wide, wide), (s // tm,),
                 [_rows(tm, d), _vec(d), _vec(d), _vec(d), _resident(w_in.shape), _rows(tm, LANES), _rows(tm, LANES)],
                 (_rows(tm, d), _rows(tm, d)) + (_rows(tm, LANES),) * 4 + (_rows(tm, d),) * 4, ("parallel",), after=after)(
                     x, w, sc, sh, w_in, cosf, sinf)


def _in_proj_bwd(name, dq, dkv, du, dv, dga, dgb, w_in, x, w, sc, dx_in, tm=256):
    s, d = x.shape
    tm = min(tm, s)

    def body(dq_ref, dkv_ref, du_ref, dv_ref, dga_ref, dgb_ref, win_ref, x_ref, w_ref, sc_ref, dxin_ref,
             dx_ref, da_ref, dsh_ref):
        @pl.when(pl.program_id(0) == 0)
        def _():
            da_ref[...] = jnp.zeros_like(da_ref)
            dsh_ref[...] = jnp.zeros_like(dsh_ref)

        dh = (_dot(dq_ref[...], win_ref[:Q_END, :]) + _dot(dkv_ref[...], win_ref[Q_END:KV_END, :])
              + _dot(du_ref[...], win_ref[KV_END:U_END, :]) + _dot(dv_ref[...], win_ref[U_END:Z_END, :])
              + _dot(dga_ref[...], win_ref[Z_END:GA_END, :]) + _dot(dgb_ref[...], win_ref[GA_END:, :]))
        xv = x_ref[...]
        r = _rms(xv)
        xn = xv * r
        dxn = dh * (w_ref[...] * (1.0 + sc_ref[...]))
        dx_ref[...] = dxin_ref[...] + r * (dxn - xn * jnp.mean(dxn * xn, axis=-1, keepdims=True))
        da_ref[...] += jnp.sum(dh * xn, axis=0, keepdims=True)
        dsh_ref[...] += jnp.sum(dh, axis=0, keepdims=True)

    return _call(body, name, (_sds((s, d), F32), _sds((1, d), F32), _sds((1, d), F32)), (s // tm,),
                 [_rows(tm, d), _rows(tm, 2 * LANES), _rows(tm, d), _rows(tm, d), _rows(tm, d), _rows(tm, d),
                  _resident(w_in.shape), _rows(tm, d), _vec(d), _vec(d), _rows(tm, d)],
                 (_rows(tm, d), _vec(d), _vec(d)), ("arbitrary",))(dq, dkv, du, dv, dga, dgb, w_in, x, w, sc, dx_in)


def _rope_bwd(name, dqr, dkv_cur, dkv_prev, cosf, sinf, tm=512):
    s = dqr.shape[0]
    tm = min(tm, s)
    steps = s // tm
    per = tm // ATTN_BLOCK
    nb = s // ATTN_BLOCK

    def unrope(v, cosv, sinv):
        return v * cosv - _rope_partner(v) * sinv

    def body(dq_ref, cur_ref, prev_ref, next_ref, cos_ref, sin_ref, dqo_ref, dkvo_ref):
        i = pl.program_id(0)
        cosv, sinv = cos_ref[...], sin_ref[...]
        for j in range(D_MODEL // LANES):
            dqo_ref[:, j * LANES:(j + 1) * LANES] = unrope(dq_ref[:, j * LANES:(j + 1) * LANES], cosv, sinv).astype(BF)
        nxt = jnp.where(i < steps - 1, next_ref[...], 0.0)
        if per > 1:
            shifted = jnp.concatenate([prev_ref[ATTN_BLOCK:, :], nxt], axis=0)
        else:
            shifted = nxt
        tot = cur_ref[...] + shifted
        dkvo_ref[:, :LANES] = unrope(tot[:, :LANES], cosv, sinv).astype(BF)
        dkvo_ref[:, LANES:] = tot[:, LANES:].astype(BF)

    nxt_spec = pl.BlockSpec((ATTN_BLOCK, 2 * LANES), lambda i: (jnp.minimum((i + 1) * per, nb - 1), 0))
    return _call(body, name, (_sds((s, D_MODEL), BF), _sds((s, 2 * LANES), BF)), (steps,),
                 [_rows(tm, D_MODEL), _rows(tm, 2 * LANES), _rows(tm, 2 * LANES), nxt_spec, _rows(tm, LANES),
                  _rows(tm, LANES)],
                 (_rows(tm, D_MODEL), _rows(tm, 2 * LANES)), ("parallel",))(dqr, dkv_cur, dkv_prev, dkv_prev, cosf, sinf)


Q_PER_KV = N_Q_HEADS // N_KV_HEADS


def _band_mask_t(n):
    kj = lax.broadcasted_iota(jnp.int32, (2 * ATTN_BLOCK, ATTN_BLOCK), 0)
    qi = lax.broadcasted_iota(jnp.int32, (2 * ATTN_BLOCK, ATTN_BLOCK), 1)
    return (kj > qi) & (kj <= qi + ATTN_BLOCK) & ((n > 0) | (kj >= ATTN_BLOCK))


def _softmax_t(raw, allowed, sink):
    sc = jnp.where(allowed, raw * (HEAD_DIM ** -0.5), NEG_BIG)
    m = jnp.maximum(jnp.max(sc, axis=0, keepdims=True), sink)
    p = jnp.exp(sc - m)
    esink = jnp.exp(sink - m)
    inv = 1.0 / (jnp.sum(p, axis=0, keepdims=True) + esink)
    return p * inv, esink * inv


def _kv_specs():
    cur = pl.BlockSpec((ATTN_BLOCK, LANES), lambda n: (n, 0))
    prev = pl.BlockSpec((ATTN_BLOCK, LANES), lambda n: (jnp.maximum(n - 1, 0), 0))
    return [prev, cur] * 4


def _attention(name, qr, kk0, kk1, vv0, vv1, sinks):
    s = qr.shape[0]
    nb = s // ATTN_BLOCK

    def body(sink_ref, q_ref, k0p, k0c, k1p, k1c, v0p, v0c, v1p, v1c, y_ref):
        allowed = _band_mask_t(pl.program_id(0))
        upper = lax.broadcasted_iota(jnp.int32, (1, LANES), 1) >= HEAD_DIM
        upper_rows = lax.broadcasted_iota(jnp.int32, (LANES, 1), 0) >= HEAD_DIM
        bands = ((jnp.concatenate([k0p[...], k0c[...]], axis=0), jnp.concatenate([v0p[...], v0c[...]], axis=0)),
                 (jnp.concatenate([k1p[...], k1c[...]], axis=0), jnp.concatenate([v1p[...], v1c[...]], axis=0)))
        vbts = (bands[0][1].T, bands[1][1].T)

        def scores(h):
            hk, j, half = h // Q_PER_KV, (h % Q_PER_KV) // 2, h % 2
            col = (hk * 4 + j) * LANES
            qp = q_ref[:, col:col + LANES]
            return _dot_nt(bands[hk][0], jnp.where(upper if half else jnp.logical_not(upper), qp, jnp.zeros_like(qp)))

        out_t = None
        ahead = scores(0)
        for h in range(N_Q_HEADS):
            hk, j, half = h // Q_PER_KV, (h % Q_PER_KV) // 2, h % 2
            raw = ahead
            if h + 1 < N_Q_HEADS:
                ahead = scores(h + 1)
            pn, _ = _softmax_t(raw, allowed, sink_ref[h])
            o_h = _dot(vbts[hk], pn.astype(BF))
            out_t = jnp.where(upper_rows, o_h, out_t) if half else o_h
            if half:
                col = (hk * 4 + j) * LANES
                y_ref[:, col:col + LANES] = out_t.T.astype(BF)

    return _call(body, name, _sds((s, D_MODEL), BF), (nb,),
                 [pl.BlockSpec(memory_space=pltpu.SMEM), pl.BlockSpec((ATTN_BLOCK, D_MODEL), lambda n: (n, 0))] + _kv_specs(),
                 pl.BlockSpec((ATTN_BLOCK, D_MODEL), lambda n: (n, 0)), ("parallel",))(
                     sinks, qr, kk0, kk0, kk1, kk1, vv0, vv0, vv1, vv1)


def _attention_bwd(name, qr, kk0, kk1, vv0, vv1, sinks, dy):
    s = qr.shape[0]
    nb = s // ATTN_BLOCK

    def body(sink_ref, q_ref, dy_ref, k0p, k0c, k1p, k1c, v0p, v0c, v1p, v1c, dq_ref, cur_ref, prev_ref, dsink_ref):
        @pl.when(pl.program_id(0) == 0)
        def _():
            dsink_ref[...] = jnp.zeros_like(dsink_ref)

        allowed = _band_mask_t(pl.program_id(0))
        lane = lax.broadcasted_iota(jnp.int32, (1, LANES), 1)
        upper = lane >= HEAD_DIM
        upper_rows = lax.broadcasted_iota(jnp.int32, (LANES, 1), 0) >= HEAD_DIM
        bands = ((jnp.concatenate([k0p[...], k0c[...]], axis=0), jnp.concatenate([v0p[...], v0c[...]], axis=0)),
                 (jnp.concatenate([k1p[...], k1c[...]], axis=0), jnp.concatenate([v1p[...], v1c[...]], axis=0)))
        kbts = (bands[0][0].T, bands[1][0].T)

        def scores(h):
            hk, j, half = h // Q_PER_KV, (h % Q_PER_KV) // 2, h % 2
            kb, vb = bands[hk]
            col = (hk * 4 + j) * LANES
            sel = upper if half else jnp.logical_not(upper)
            qp = q_ref[:, col:col + LANES]
            qa = jnp.where(sel, qp, jnp.zeros_like(qp))
            dya = jnp.where(sel, dy_ref[:, col:col + LANES], 0.0).astype(BF)
            return qa, dya, _dot_nt(kb, qa), _dot_nt(vb, dya)

        dsink = jnp.zeros((1, LANES), F32)
        dk_slab = jnp.zeros((2 * ATTN_BLOCK, LANES), F32)
        dv_slab = jnp.zeros((2 * ATTN_BLOCK, LANES), F32)
        dkk = dvv = dq_t = None
        ahead = scores(0)
        for h in range(N_Q_HEADS):
            hk, j, half = h // Q_PER_KV, (h % Q_PER_KV) // 2, h % 2
            qa, dya, raw, dp = ahead
            if h + 1 < N_Q_HEADS:
                ahead = scores(h + 1)
            pn, psink = _softmax_t(raw, allowed, sink_ref[h])
            delta = jnp.sum(pn * dp, axis=0, keepdims=True)
            ds = (pn * (dp - delta) * (HEAD_DIM ** -0.5)).astype(BF)
            dsink = dsink + jnp.where(lane == h, -jnp.sum(psink * delta), 0.0)
            dq_h = _dot(kbts[hk], ds)
            dq_t = jnp.where(upper_rows, dq_h, dq_t) if half else dq_h
            dk_h, dv_h = _dot(ds, qa), _dot(pn.astype(BF), dya)
            dkk, dvv = (dk_h, dv_h) if h % Q_PER_KV == 0 else (dkk + dk_h, dvv + dv_h)
            if half:
                col = (hk * 4 + j) * LANES
                dq_ref[:, col:col + LANES] = dq_t.T
            if h % Q_PER_KV == Q_PER_KV - 1:
                mine = upper if hk else jnp.logical_not(upper)
                dk_slab = jnp.where(mine, dkk + pltpu.roll(dkk, HEAD_DIM, axis=1), dk_slab)
                dv_slab = jnp.where(mine, dvv + pltpu.roll(dvv, HEAD_DIM, axis=1), dv_slab)
        prev_ref[:, :LANES] = dk_slab[:ATTN_BLOCK]
        prev_ref[:, LANES:] = dv_slab[:ATTN_BLOCK]
        cur_ref[:, :LANES] = dk_slab[ATTN_BLOCK:]
        cur_ref[:, LANES:] = dv_slab[ATTN_BLOCK:]
        dsink_ref[...] += dsink

    blk = pl.BlockSpec((ATTN_BLOCK, D_MODEL), lambda n: (n, 0))
    kvo = pl.BlockSpec((ATTN_BLOCK, 2 * LANES), lambda n: (n, 0))
    return _call(body, name,
                 (_sds((s, D_MODEL), F32), _sds((s, 2 * LANES), F32), _sds((s, 2 * LANES), F32), _sds((1, LANES), F32)),
                 (nb,), [pl.BlockSpec(memory_space=pltpu.SMEM), blk, blk] + _kv_specs(),
                 (blk, kvo, kvo, pl.BlockSpec((1, LANES), lambda n: (0, 0))), ("arbitrary",))(
                     sinks, qr, dy, kk0, kk0, kk1, kk1, vv0, vv0, vv1, vv1)


def _sgu_weights(wm_ref, g):
    t = lax.broadcasted_iota(jnp.int32, (SGU_CHUNK, SGU_CHUNK), 0)
    sidx = lax.broadcasted_iota(jnp.int32, (SGU_CHUNK, SGU_CHUNK), 1)
    return jnp.where(sidx <= t, wm_ref[g], 0.0).astype(BF)


def _layer_norm_stats(v):
    mu = jnp.mean(v, axis=-1, keepdims=True)
    cen = v - mu
    rstd = lax.rsqrt(jnp.mean(cen * cen, axis=-1, keepdims=True) + NORM_EPS)
    return cen * rstd, rstd


def _sgu(name, u_pre, v_pre, ln_w, ln_b, wm, bfull, tm=256):
    s, w = u_pre.shape
    tm = min(tm, s)

    def body(u_ref, v_ref, lw_ref, lb_ref, wm_ref, b_ref, y_ref):
        vhat, _ = _layer_norm_stats(_gelu(v_ref[...]))
        vn = (vhat * lw_ref[...] + lb_ref[...]).astype(BF)
        for g in range(SGU_GROUPS):
            wg = _sgu_weights(wm_ref, g)
            cols = slice(g * SGU_CHUNK, (g + 1) * SGU_CHUNK)
            for ch in range(tm // SGU_CHUNK):
                rows = slice(ch * SGU_CHUNK, (ch + 1) * SGU_CHUNK)
                f = _dot(wg, vn[rows, cols]) + b_ref[g]
                y_ref[rows, cols] = (_gelu(u_ref[rows, cols]) * f).astype(BF)

    full3 = pl.BlockSpec((SGU_GROUPS, SGU_CHUNK, SGU_CHUNK), lambda i: (0, 0, 0))
    return _call(body, name, _sds((s, w), BF), (s // tm,),
                 [_rows(tm, w), _rows(tm, w), _vec(w), _vec(w), full3, full3],
                 _rows(tm, w), ("parallel",))(u_pre, v_pre, ln_w, ln_b, wm, bfull)


def _sgu_bwd(name, u_pre, v_pre, ln_w, ln_b, wm, bfull, dy, tm=256, after=None):
    s, w = u_pre.shape
    tm = min(tm, s)
    steps = s // tm

    def body(u_ref, v_ref, lw_ref, lb_ref, wm_ref, b_ref, dy_ref, du_ref, dv_ref, dwm_ref, db_ref, dlw_ref, dlb_ref,
             dfsum_ref):
        i = pl.program_id(0)

        @pl.when(i == 0)
        def _():
            dwm_ref[...] = jnp.zeros_like(dwm_ref)
            dlw_ref[...] = jnp.zeros_like(dlw_ref)
            dlb_ref[...] = jnp.zeros_like(dlb_ref)
            dfsum_ref[...] = jnp.zeros_like(dfsum_ref)

        vpre = v_ref[...]
        vhat, rstd = _layer_norm_stats(_gelu(vpre))
        vn = (vhat * lw_ref[...] + lb_ref[...]).astype(BF)
        t = lax.broadcasted_iota(jnp.int32, (SGU_CHUNK, SGU_CHUNK), 0)
        sidx = lax.broadcasted_iota(jnp.int32, (SGU_CHUNK, SGU_CHUNK), 1)
        dvn_cols = []
        for g in range(SGU_GROUPS):
            wg = _sgu_weights(wm_ref, g)
            cols = slice(g * SGU_CHUNK, (g + 1) * SGU_CHUNK)
            dvn_rows = []
            dwg = jnp.zeros((SGU_CHUNK, SGU_CHUNK), F32)
            dfs = jnp.zeros((SGU_CHUNK, SGU_CHUNK), F32)
            for ch in range(tm // SGU_CHUNK):
                rows = slice(ch * SGU_CHUNK, (ch + 1) * SGU_CHUNK)
                upre = u_ref[rows, cols]
                dyv = dy_ref[rows, cols].astype(F32)
                f = _dot(wg, vn[rows, cols]) + b_ref[g]
                du_ref[rows, cols] = (dyv * f * _gelu_grad(upre)).astype(BF)
                df = dyv * _gelu(upre)
                dfb = df.astype(BF)
                dvn_rows.append(_dot_tn(wg, dfb))
                dwg = dwg + _dot_nt(dfb, vn[rows, cols])
                dfs = dfs + df
            dwm_ref[g] += jnp.where(sidx <= t, dwg, 0.0)
            dfsum_ref[g] += dfs
            dvn_cols.append(jnp.concatenate(dvn_rows, axis=0) if len(dvn_rows) > 1 else dvn_rows[0])
        dvn = jnp.concatenate(dvn_cols, axis=1)
        dlw_ref[...] += jnp.sum(dvn * vhat, axis=0, keepdims=True)
        dlb_ref[...] += jnp.sum(dvn, axis=0, keepdims=True)
        dvh = dvn * lw_ref[...]
        dvg = rstd * (dvh - jnp.mean(dvh, axis=-1, keepdims=True) - vhat * jnp.mean(dvh * vhat, axis=-1, keepdims=True))
        dv_ref[...] = (dvg * _gelu_grad(vpre)).astype(BF)

        @pl.when(i == steps - 1)
        def _():
            for g in range(SGU_GROUPS):
                db_ref[g:g + 1, :] = jnp.sum(dfsum_ref[g].T, axis=0, keepdims=True)

    full3 = pl.BlockSpec((SGU_GROUPS, SGU_CHUNK, SGU_CHUNK), lambda i: (0, 0, 0))
    return _call(body, name,
                 (_sds((s, w), BF), _sds((s, w), BF), _sds((SGU_GROUPS, SGU_CHUNK, SGU_CHUNK), F32),
                  _sds((SGU_GROUPS, SGU_CHUNK), F32), _sds((1, w), F32), _sds((1, w), F32)),
                 (steps,),
                 [_rows(tm, w), _rows(tm, w), _vec(w), _vec(w), full3, full3, _rows(tm, w)],
                 (_rows(tm, w), _rows(tm, w), full3, pl.BlockSpec((SGU_GROUPS, SGU_CHUNK), lambda i: (0, 0)), _vec(w), _vec(w)),
                 ("arbitrary",), scratch=[pltpu.VMEM((SGU_GROUPS, SGU_CHUNK, SGU_CHUNK), F32)], after=after)(
                     u_pre, v_pre, ln_w, ln_b, wm, bfull, dy)


def _mix_out(name, y_sgu, y_attn, ga_pre, gb_pre, x, g1, proj_a, proj_b, w_out, w2, sc2, sh2, tm=256):
    s, d = x.shape
    tm = min(tm, s)

    def body(ys_ref, ya_ref, ga_ref, gb_ref, x_ref, g1_ref, wa_ref, wb_ref, wo_ref, w2_ref, sc2_ref, sh2_ref,
             m_ref, pa_ref, pb_ref, o_ref, x1_ref, h2_ref):
        pa = _dot(ys_ref[...], wa_ref[...].reshape(d, d))
        pb = _dot(ya_ref[...], wb_ref[...].reshape(d, d))
        pa_ref[...] = pa.astype(BF)
        pb_ref[...] = pb.astype(BF)
        merged = (_sigmoid(ga_ref[...].astype(F32)) * pa + _sigmoid(gb_ref[...].astype(F32)) * pb).astype(BF)
        m_ref[...] = merged
        o = _dot(merged, wo_ref[...].reshape(d, d))
        o_ref[...] = o.astype(BF)
        x1 = x_ref[...] + g1_ref[...] * o
        x1_ref[...] = x1
        h2_ref[...] = ((x1 * _rms(x1)) * w2_ref[...] * (1.0 + sc2_ref[...]) + sh2_ref[...]).astype(BF)

    f, b = _sds((s, d), F32), _sds((s, d), BF)
    r = _rows(tm, d)
    wspec = _resident(proj_a.shape)
    return _call(body, name, (b, b, b, b, f, b), (s // tm,),
                 [r, r, r, r, r, _vec(d), wspec, wspec, wspec, _vec(d), _vec(d), _vec(d)], (r,) * 6, ("parallel",))(
                     y_sgu, y_attn, ga_pre, gb_pre, x, g1, proj_a, proj_b, w_out, w2, sc2, sh2)


def _mix_bwd(name, do, w_out, proj_a, proj_b, ga_pre, gb_pre, pa, pb, tm=256):
    s, d = do.shape
    tm = min(tm, s)

    def body(do_ref, wo_ref, wa_ref, wb_ref, ga_ref, gb_ref, pa_ref, pb_ref,
             dpa_ref, dpb_ref, dga_ref, dgb_ref, dys_ref, dya_ref):
        dm = _dot_nt(do_ref[...], wo_ref[...].reshape(d, d))
        ga = _sigmoid(ga_ref[...].astype(F32))
        gb = _sigmoid(gb_ref[...].astype(F32))
        dpa = (dm * ga).astype(BF)
        dpb = (dm * gb).astype(BF)
        dpa_ref[...] = dpa
        dpb_ref[...] = dpb
        dga_ref[...] = (dm * pa_ref[...].astype(F32) * ga * (1.0 - ga)).astype(BF)
        dgb_ref[...] = (dm * pb_ref[...].astype(F32) * gb * (1.0 - gb)).astype(BF)
        dys_ref[...] = _dot_nt(dpa, wa_ref[...].reshape(d, d)).astype(BF)
        dya_ref[...] = _dot_nt(dpb, wb_ref[...].reshape(d, d)).astype(BF)

    f, b = _sds((s, d), F32), _sds((s, d), BF)
    r = _rows(tm, d)
    wspec = _resident(w_out.shape)
    return _call(body, name, (b, b, b, b, b, b), (s // tm,), [r, wspec, wspec, wspec, r, r, r, r], (r,) * 6,
                 ("parallel",))(do, w_out, proj_a, proj_b, ga_pre, gb_pre, pa, pb)


def _ffn_up_act(name, h2, w_gate, w_up, cw, cb, tm=1024):
    s, d = h2.shape
    tm = min(tm, s)
    tc = FFN_SHARD_PAD
    per = tm // HALO

    def body(h_ref, hprev_ref, wg_ref, wu_ref, cw_ref, cb_ref, a_ref, ac_ref, up_ref, hf_ref):
        hv = h_ref[...]
        a = _dot_nt(hv, wg_ref[0]).astype(BF)
        up = _dot_nt(hv, wu_ref[0]).astype(BF)
        a_ref[...] = a
        up_ref[...] = up
        prev = jnp.where(pl.program_id(1) > 0, _dot_nt(hprev_ref[...], wg_ref[0]).astype(BF).astype(F32), 0.0)
        ext = jnp.concatenate([prev, a.astype(F32)], axis=0)
        ac = (cb_ref[...] + cw_ref[0:1, :] * pltpu.roll(ext, 2, axis=0) + cw_ref[1:2, :] * pltpu.roll(ext, 1, axis=0)
              + cw_ref[2:3, :] * ext)[HALO:]
        ac_ref[...] = ac.astype(BF)
        hf_ref[...] = (ac * _sigmoid(ac) * up.astype(F32)).astype(BF)

    wspec = pl.BlockSpec((1, tc, d), lambda j, i: (j, 0, 0))
    ospec = pl.BlockSpec((tm, tc), lambda j, i: (i, j))
    o = _sds((s, FFN_PAD), BF)
    return _call(body, name, (o, o, o, o), (N_CHIPS, s // tm),
                 [pl.BlockSpec((tm, d), lambda j, i: (i, 0)), pl.BlockSpec((HALO, d), lambda j, i: (jnp.maximum(i * per - 1, 0), 0)),
                  wspec, wspec, pl.BlockSpec((3, tc), lambda j, i: (0, j)), pl.BlockSpec((1, tc), lambda j, i: (0, j))],
                 (ospec, ospec, ospec, ospec), ("parallel", "parallel"))(h2, h2, w_gate, w_up, cw, cb)


def _ffn_down(name, hf, w_down, x1, g2, tm=512):
    s, d = x1.shape
    tm = min(tm, s)

    def body(hf_ref, wd_ref, x1_ref, g2_ref, dn_ref, x2_ref):
        dn = _dot(hf_ref[...], wd_ref[...].reshape(FFN_PAD, d))
        dn_ref[...] = dn.astype(BF)
        x2_ref[...] = x1_ref[...] + g2_ref[...] * dn

    return _call(body, name, (_sds((s, d), BF), _sds((s, d), F32)), (s // tm,),
                 [_rows(tm, FFN_PAD), _resident(w_down.shape), _rows(tm, d), _vec(d)],
                 (_rows(tm, d), _rows(tm, d)), ("parallel",))(hf, w_down, x1, g2)


def _ffn_down_bwd_act(name, dx2, dn, g2, w_down, a, ac, up, cw, tm=256, after=None):
    s, d = dx2.shape
    c = a.shape[1]
    tm = min(tm, s)
    tc = FFN_SHARD_PAD
    per = tm // HALO
    steps = s // tm
    last = s // HALO - 1
    n = tm + HALO

    def body(dx_ref, dxnext_ref, dn_ref, g2_ref, wd_ref, a_ref, ac_ref, acnext_ref, up_ref, upnext_ref, cw_ref,
             ddn_ref, da_ref, dup_ref, dg_ref, dcw_ref, dcb_ref):
        i = pl.program_id(0)

        @pl.when(i == 0)
        def _():
            dg_ref[...] = jnp.zeros_like(dg_ref)
            dcw_ref[...] = jnp.zeros_like(dcw_ref)
            dcb_ref[...] = jnp.zeros_like(dcb_ref)

        dxv = dx_ref[...]
        ddn = (dxv * g2_ref[...]).astype(BF)
        ddn_ref[...] = ddn
        dg_ref[...] += jnp.sum(dxv * dn_ref[...].astype(F32), axis=0, keepdims=True)
        ddn_next = jnp.where(i < steps - 1, dxnext_ref[...] * g2_ref[...], 0.0).astype(BF)
        ddn_ext = jnp.concatenate([ddn, ddn_next], axis=0)
        for k in range(N_CHIPS):
            cols = slice(k * tc, (k + 1) * tc)
            dh = _dot_nt(ddn_ext, wd_ref[k])
            ace = jnp.concatenate([ac_ref[:, cols].astype(F32), acnext_ref[:, cols].astype(F32)], axis=0)
            upe = jnp.concatenate([up_ref[:, cols].astype(F32), upnext_ref[:, cols].astype(F32)], axis=0)
            sig = _sigmoid(ace)
            silu = ace * sig
            dac = dh * upe * (sig + silu * (1.0 - sig))
            dup_ref[:, cols] = (dh[:tm] * silu[:tm]).astype(BF)
            d1 = pltpu.roll(dac, n - 1, axis=0)[:tm]
            d2 = pltpu.roll(dac, n - 2, axis=0)[:tm]
            d0 = dac[:tm]
            da_ref[:, cols] = (cw_ref[2:3, cols] * d0 + cw_ref[1:2, cols] * d1 + cw_ref[0:1, cols] * d2).astype(BF)
            a0 = a_ref[:, cols].astype(F32)
            dcb_ref[:, cols] += jnp.sum(d0, axis=0, keepdims=True)
            dcw_ref[0:1, cols] += jnp.sum(d2 * a0, axis=0, keepdims=True)
            dcw_ref[1:2, cols] += jnp.sum(d1 * a0, axis=0, keepdims=True)
            dcw_ref[2:3, cols] += jnp.sum(d0 * a0, axis=0, keepdims=True)

    nxt = lambda width: pl.BlockSpec((HALO, width), lambda i: (jnp.minimum((i + 1) * per, last), 0))
    wide = _sds((s, c), BF)
    return _call(body, name, (_sds((s, d), BF), wide, wide, _sds((1, d), F32), _sds((3, c), F32), _sds((1, c), F32)), (steps,),
                 [_rows(tm, d), nxt(d), _rows(tm, d), _vec(d), _resident(w_down.shape), _rows(tm, c), _rows(tm, c), nxt(c),
                  _rows(tm, c), nxt(c), pl.BlockSpec((3, c), lambda i: (0, 0))],
                 (_rows(tm, d), _rows(tm, c), _rows(tm, c), _vec(d), pl.BlockSpec((3, c), lambda i: (0, 0)), _vec(c)),
                 ("arbitrary",), after=after)(dx2, dx2, dn, g2, w_down, a, ac, ac, up, up, cw)


def _ffn_up_bwd(name, da, dup, w_gate, w_up, x1, dx2, w2, sc2, o, g1, tm=256, after=None):
    s, d = x1.shape
    tm = min(tm, s)
    tc = FFN_SHARD_PAD

    def body(da_ref, dup_ref, wg_ref, wu_ref, x1_ref, dx2_ref, w2_ref, sc2_ref, o_ref, g1_ref,
             dx1_ref, do_ref, dnw_ref, dsh_ref, dg1_ref):
        @pl.when(pl.program_id(0) == 0)
        def _():
            dnw_ref[...] = jnp.zeros_like(dnw_ref)
            dsh_ref[...] = jnp.zeros_like(dsh_ref)
            dg1_ref[...] = jnp.zeros_like(dg1_ref)

        dh = jnp.zeros((tm, d), F32)
        for k in range(N_CHIPS):
            cols = slice(k * tc, (k + 1) * tc)
            dh = dh + _dot(da_ref[:, cols], wg_ref[k]) + _dot(dup_ref[:, cols], wu_ref[k])
        xv = x1_ref[...]
        r = _rms(xv)
        xn = xv * r
        dxn = dh * (w2_ref[...] * (1.0 + sc2_ref[...]))
        dx1 = dx2_ref[...] + r * (dxn - xn * jnp.mean(dxn * xn, axis=-1, keepdims=True))
        dx1_ref[...] = dx1
        dnw_ref[...] += jnp.sum(dh * xn, axis=0, keepdims=True)
        dsh_ref[...] += jnp.sum(dh, axis=0, keepdims=True)
        do_ref[...] = (dx1 * g1_ref[...]).astype(BF)
        dg1_ref[...] += jnp.sum(dx1 * o_ref[...].astype(F32), axis=0, keepdims=True)

    v = _sds((1, d), F32)
    r = _rows(tm, d)
    wspec = _resident(w_gate.shape)
    return _call(body, name, (_sds((s, d), F32), _sds((s, d), BF), v, v, v), (s // tm,),
                 [_rows(tm, FFN_PAD), _rows(tm, FFN_PAD), wspec, wspec, r, r, _vec(d), _vec(d), r, _vec(d)],
                 (r, r, _vec(d), _vec(d), _vec(d)), ("arbitrary",), after=after)(da, dup, w_gate, w_up, x1, dx2, w2, sc2, o, g1)


def _loss_head(name, x, w, target, tm=512):
    s, d = x.shape
    tm = min(tm, s)

    def body(x_ref, w_ref, t_ref, dx_ref, loss_ref, dw_ref):
        @pl.when(pl.program_id(0) == 0)
        def _():
            loss_ref[...] = jnp.zeros_like(loss_ref)
            dw_ref[...] = jnp.zeros_like(dw_ref)

        xv = x_ref[...]
        r = _rms(xv)
        xn = xv * r
        err = xn * w_ref[...] - t_ref[...]
        loss_ref[...] += 0.5 * jnp.sum(jnp.mean(err * err, axis=-1, keepdims=True))
        dy = err * (1.0 / d)
        dw_ref[...] += jnp.sum(dy * xn, axis=0, keepdims=True)
        dxn = dy * w_ref[...]
        dx_ref[...] = r * (dxn - xn * jnp.mean(dxn * xn, axis=-1, keepdims=True))

    return _call(body, name, (_sds((s, d), F32), _sds((1, LANES), F32), _sds((1, d), F32)), (s // tm,),
                 [_rows(tm, d), _vec(d), _rows(tm, d)], (_rows(tm, d), _vec(LANES), _vec(d)), ("arbitrary",))(x, w, target)


def _layer_fwd(l, x, mod, p, cosf, sinf, after=None, late=None, later=None):
    sh1, sc1, g1, sh2, sc2, g2 = mod
    tag = f"l{l}_"
    h, qr, kk0, kk1, vv0, vv1, u_pre, v_pre, ga_pre, gb_pre = _in_proj(
        tag + "in_proj", x, p["norm1_w"], sc1, sh1, p["w_in"], cosf, sinf, after=after)
    y_attn = _attention(tag + "attn", qr, kk0, kk1, vv0, vv1, p["sinks"])
    y_sgu = _sgu(tag + "sgu", u_pre, v_pre, p["sgu_ln_w"], p["sgu_ln_b"], p["sgu_w"], p["sgu_bfull"])
    if late is not None:
        p = dict(p, **late(y_sgu))
    merged, pa, pb, o, x1, h2 = _mix_out(tag + "mix_out", y_sgu, y_attn, ga_pre, gb_pre, x, g1, p["proj_a"], p["proj_b"],
                                         p["w_out"], p["norm2_w"], sc2, sh2)
    if later is not None:
        p = dict(p, **later(h2))
    a, ac, up, hf = _ffn_up_act(tag + "ffn_up", h2, p["w_gate"], p["w_up"], p["conv_w"], p["conv_b"])
    dn, x2 = _ffn_down(tag + "ffn_down", hf, p["w_down"], x1, g2)
    saved = dict(x=x, h=h, qr=qr, kk0=kk0, kk1=kk1, vv0=vv0, vv1=vv1, u_pre=u_pre, v_pre=v_pre, ga_pre=ga_pre,
                 gb_pre=gb_pre, y_attn=y_attn, y_sgu=y_sgu, merged=merged, pa=pa, pb=pb, o=o, x1=x1, h2=h2, a=a, ac=ac, up=up,
                 hf=hf, dn=dn)
    return x2, saved, p


def _layer_bwd(l, dx2, mod, p, sv, cosf, sinf, after=None, emit=None):
    sh1, sc1, g1, sh2, sc2, g2 = mod
    tag = f"l{l}_b_"
    d = D_MODEL
    g = {}
    ready = (lambda names: emit({k: g.pop(k) for k in names})) if emit else (lambda names: None)
    ddn, da, dup, dg2, g["conv_w"], g["conv_b"] = _ffn_down_bwd_act(
        tag + "ffn_down", dx2, sv["dn"], g2, p["w_down"], sv["a"], sv["ac"], sv["up"], p["conv_w"], after=after)
    g["w_down"] = _matmul_tn(tag + "dw_down", sv["hf"], ddn, tk=FFN_SHARD_PAD).reshape(N_CHIPS, FFN_SHARD_PAD, d)
    g["w_gate"] = _matmul_tn(tag + "dw_gate", da, sv["h2"], tk=FFN_SHARD_PAD).reshape(N_CHIPS, FFN_SHARD_PAD, d)
    g["w_up"] = _matmul_tn(tag + "dw_up", dup, sv["h2"], tk=FFN_SHARD_PAD).reshape(N_CHIPS, FFN_SHARD_PAD, d)
    dx1, do, da2, dsh2, dg1 = _ffn_up_bwd(tag + "ffn_up", da, dup, p["w_gate"], p["w_up"], sv["x1"], dx2, p["norm2_w"],
                                          sc2, sv["o"], g1, after=ready(("w_down", "w_gate", "w_up")))
    g["norm2_w"] = da2 * (1.0 + sc2)
    dsc2 = da2 * p["norm2_w"]
    g["w_out"] = _matmul_tn(tag + "dw_out", sv["merged"], do).reshape(N_CHIPS, d // N_CHIPS, d)
    dpa, dpb, dga, dgb, dy_sgu, dy_attn = _mix_bwd(tag + "mix", do, p["w_out"], p["proj_a"], p["proj_b"], sv["ga_pre"],
                                                  sv["gb_pre"], sv["pa"], sv["pb"])
    g["proj_a"] = _matmul_tn(tag + "dproj_a", sv["y_sgu"], dpa).reshape(N_CHIPS, d // N_CHIPS, d)
    g["proj_b"] = _matmul_tn(tag + "dproj_b", sv["y_attn"], dpb).reshape(N_CHIPS, d // N_CHIPS, d)
    du, dv, g["sgu_w"], g["sgu_b"], g["sgu_ln_w"], g["sgu_ln_b"] = _sgu_bwd(
        tag + "sgu", sv["u_pre"], sv["v_pre"], p["sgu_ln_w"], p["sgu_ln_b"], p["sgu_w"], p["sgu_bfull"], dy_sgu,
        after=ready(("w_out", "proj_a", "proj_b")))
    dqr, dkv_cur, dkv_prev, dsink = _attention_bwd(tag + "attn", sv["qr"], sv["kk0"], sv["kk1"], sv["vv0"], sv["vv1"],
                                                   p["sinks"], dy_attn)
    g["sinks"] = dsink[0, :N_Q_HEADS]
    dq, dkv = _rope_bwd(tag + "rope", dqr, dkv_cur, dkv_prev, cosf, sinf)
    dw_in, row0 = None, 0
    for n, t in (("q", dq), ("kv", dkv), ("u", du), ("v", dv), ("ga", dga), ("gb", dgb)):
        dw_in = _matmul_tn_rows(tag + "dw_in_" + n, t, sv["h"], dw_in, row0, IN_COLS)
        row0 += t.shape[1]
    g["w_in"] = dw_in.reshape(N_CHIPS, IN_COLS // N_CHIPS, d)
    dx, da1, dsh1 = _in_proj_bwd(tag + "in_proj", dq, dkv, du, dv, dga, dgb, p["w_in"], sv["x"], p["norm1_w"], sc1, dx1)
    g["norm1_w"] = da1 * (1.0 + sc1)
    dsc1 = da1 * p["norm1_w"]
    return dx, (dsh1, dsc1, dg1, dsh2, dsc2, dg2), g


def _pad_to(a, axis, size):
    pad = [(0, 0)] * a.ndim
    pad[axis] = (0, size - a.shape[axis])
    return jnp.pad(a, pad)


def _early_params(w_in, small):
    d = D_MODEL
    return dict(
        w_in=w_in.reshape(IN_COLS, d), norm1_w=small["norm1_w"].reshape(1, d), sinks=small["sinks"],
        sgu_ln_w=small["sgu_ln_w"].reshape(1, d), sgu_ln_b=small["sgu_ln_b"].reshape(1, d), sgu_w=small["sgu_w"],
        sgu_bfull=jnp.broadcast_to(small["sgu_b"][:, :, None], (SGU_GROUPS, SGU_CHUNK, SGU_CHUNK)))


def _mix_params(proj_a, proj_b, w_out, small):
    return dict(proj_a=proj_a, proj_b=proj_b, w_out=w_out, norm2_w=small["norm2_w"].reshape(1, D_MODEL))


def _ffn_params(w_gate, w_up, w_down, conv_w, small):
    return dict(
        w_gate=w_gate, w_up=w_up, w_down=w_down, conv_w=conv_w.transpose(1, 0, 2).reshape(3, FFN_PAD),
        conv_b=_pad_to(small["conv_b"].reshape(N_CHIPS, FFN_SHARD), 1, FFN_SHARD_PAD).reshape(1, FFN_PAD))


def _layer_params(w_in, proj_a, proj_b, w_out, w_gate, w_up, w_down, conv_w, small):
    return dict(_early_params(w_in, small), **_mix_params(proj_a, proj_b, w_out, small),
                **_ffn_params(w_gate, w_up, w_down, conv_w, small))


def _conv_grads_natural(g):
    cw = g["conv_w"].reshape(3, N_CHIPS, FFN_SHARD_PAD)[:, :, :FFN_SHARD].reshape(3, FFN_DIM)
    cb = g["conv_b"].reshape(N_CHIPS, FFN_SHARD_PAD)[:, :FFN_SHARD].reshape(FFN_DIM)
    return cw, cb


def _rope_tables(positions):
    inv_freq = ROPE_THETA ** (-jnp.arange(0, ROT_DIM, 2, dtype=F32) / ROT_DIM)
    ang = positions.astype(F32)[:, None] * inv_freq
    cos, sin = jnp.cos(ang), jnp.sin(ang)
    s = positions.shape[0]
    rest = HEAD_DIM - ROT_DIM
    cos_head = jnp.concatenate([cos, cos, jnp.ones((s, rest), F32)], axis=1)
    sin_head = jnp.concatenate([-sin, sin, jnp.zeros((s, rest), F32)], axis=1)
    return jnp.tile(cos_head, (1, LANES // HEAD_DIM)), jnp.tile(sin_head, (1, LANES // HEAD_DIM))


ADA_ROWS = 16


def _ada_fwd(name, c_rows, ada_w, ada_b_cols, tn=512):
    depth, d, n = ada_w.shape

    def body(c_ref, w_ref, b_ref, o_ref):
        cv = c_ref[...]
        act = (cv * _sigmoid(cv)).astype(BF)
        o_ref[0] = _dot(act, w_ref[0].astype(BF)) + b_ref[0]

    return _call(body, name, _sds((depth, ADA_ROWS, n), F32), (depth, n // tn),
                 [pl.BlockSpec((ADA_ROWS, d), lambda l, j: (0, 0)), pl.BlockSpec((1, d, tn), lambda l, j: (l, 0, j)),
                  pl.BlockSpec((1, 1, tn), lambda l, j: (l, 0, j))],
                 pl.BlockSpec((1, ADA_ROWS, tn), lambda l, j: (l, 0, j)), ("parallel", "parallel"))(c_rows, ada_w, ada_b_cols)


def _ada_bwd(name, c_rows, dmod_cols, tn=512):
    depth, _, n = dmod_cols.shape
    d = c_rows.shape[1]

    def body(c_ref, dm_ref, o_ref):
        cv = c_ref[...]
        act = (cv * _sigmoid(cv)).astype(BF)
        o_ref[0] = _dot_tn(act, dm_ref[0].astype(BF))

    return _call(body, name, _sds((depth, d, n), F32), (depth, n // tn),
                 [pl.BlockSpec((ADA_ROWS, d), lambda l, j: (0, 0)), pl.BlockSpec((1, ADA_ROWS, tn), lambda l, j: (l, 0, j))],
                 pl.BlockSpec((1, d, tn), lambda l, j: (l, 0, j)), ("parallel", "parallel"))(c_rows, dmod_cols)


def _colsum(name, a):
    r, n = a.shape

    def body(a_ref, o_ref):
        o_ref[...] = jnp.sum(a_ref[...], axis=0, keepdims=True)

    return _call(body, name, _sds((1, n), F32), (1,), [pl.BlockSpec((r, n), lambda i: (0, 0))],
                 pl.BlockSpec((1, n), lambda i: (0, 0)), ("arbitrary",))(a)


REL_SIBLING = (0, 0, 1)
REL_CHIPS = ((1, 0, 0), (0, 1, 0), (1, 1, 0))
REL_ALL = tuple((fx, fy, fc) for fx in (0, 1) for fy in (0, 1) for fc in (0, 1) if fx or fy or fc)


def _chip_of(dev):
    return 2 * dev[0] + dev[1]


def _dev_of(dev):
    return 4 * dev[0] + 2 * dev[1] + dev[2]


def _flip(dev, rel):
    return tuple(1 - m if f else m for m, f in zip(dev, rel))


def _exchange(name, arrays, n_out, stages, aliases=None):
    out_shapes, stages = stages[0], stages[1:]
    n_in = len(arrays)
    aliases = aliases or {}
    n_remote = sum(len(plan) for plan, _ in stages)
    n_local = sum(len(local) for _, local in stages)

    def at(ref, idx):
        return ref.at[idx] if len(idx) else ref

    def body(*refs):
        bufs = list(refs[:n_in + n_out])
        for i_in, i_out in aliases.items():
            bufs[i_in] = bufs[n_in + i_out]
        send_sems, recv_sems, local_sems = refs[n_in + n_out:]
        me = (lax.axis_index("x"), lax.axis_index("y"), lax.axis_index("c"))
        base_r = base_l = 0
        pending = []
        for plan, local in stages:
            def remote(k, entry, sender, receiver):
                rel, si, ssel, di, dsel = entry
                return pltpu.make_async_remote_copy(
                    src_ref=at(bufs[si], ssel(sender, receiver)), dst_ref=at(bufs[di], dsel(sender, receiver)),
                    send_sem=send_sems.at[k], recv_sem=recv_sems.at[k], device_id=_flip(me, rel), device_id_type=MESH)

            sends = [remote(base_r + k, e, me, _flip(me, e[0])) for k, e in enumerate(plan)]
            for cp in sends:
                cp.start()
            for k, (si, ssel, di, dsel) in enumerate(local):
                cp = pltpu.make_async_copy(at(bufs[si], ssel(me)), at(bufs[di], dsel(me)), local_sems.at[base_l + k])
                cp.start()
                pending.append(cp.wait)
            for k, e in enumerate(plan):
                remote(base_r + k, e, _flip(me, e[0]), me).wait_recv()
            pending += [cp.wait_send for cp in sends]
            base_r += len(plan)
            base_l += len(local)
        for wait in pending:
            wait()

    any_spec = pl.BlockSpec(memory_space=pl.ANY)
    return pl.pallas_call(
        body, name=name, out_shape=tuple(out_shapes), in_specs=[any_spec] * n_in, out_specs=tuple([any_spec] * n_out),
        input_output_aliases=dict(aliases),
        scratch_shapes=[pltpu.SemaphoreType.DMA((max(n_remote, 1),)), pltpu.SemaphoreType.DMA((max(n_remote, 1),)),
                        pltpu.SemaphoreType.DMA((max(n_local, 1),))])(*arrays)


HBM_SPEC = pl.BlockSpec(memory_space=pltpu.HBM)
SEM_SPEC = pl.BlockSpec(memory_space=pltpu.SEMAPHORE)


def _split_copies(bufs, plan, local, send_sems, recv_sems, local_sems):
    me = (lax.axis_index("x"), lax.axis_index("y"), lax.axis_index("c"))

    def at(ref, idx):
        return ref.at[idx] if len(idx) else ref

    def remote(k, sender, receiver):
        rel, si, ssel, di, dsel = plan[k]
        return pltpu.make_async_remote_copy(
            src_ref=at(bufs[si], ssel(sender, receiver)), dst_ref=at(bufs[di], dsel(sender, receiver)),
            send_sem=send_sems.at[k], recv_sem=recv_sems.at[k], device_id=_flip(me, rel), device_id_type=MESH)

    sends = [remote(k, me, _flip(me, plan[k][0])) for k in range(len(plan))]
    arrivals = [remote(k, _flip(me, plan[k][0]), me) for k in range(len(plan))]
    locs = [pltpu.make_async_copy(at(bufs[si], ssel(me)), at(bufs[di], dsel(me)), local_sems.at[k])
            for k, (si, ssel, di, dsel) in enumerate(local)]
    return sends, arrivals, locs


def _exchange_start(name, arrays, out_shapes, plan, local):
    n_in, n_out = len(arrays), len(out_shapes)

    n_buf = n_in + n_out

    def body(*refs):
        sems = refs[n_buf:n_buf + 3]
        bufs = refs[n_buf + 3:2 * n_buf + 3]
        token = refs[-1]
        sends, _, locs = _split_copies(bufs, plan, local, *sems)
        for cp in sends + locs:
            cp.start()
        token[...] = jnp.zeros_like(token)

    zones = [lax.empty(o.shape, o.dtype) for o in out_shapes]
    operands = [pltpu.with_memory_space_constraint(a, pltpu.HBM) for a in list(arrays) + zones]
    sem = lambda n: pltpu.SemaphoreType.DMA((max(n, 1),))
    out = pl.pallas_call(
        body, name=name,
        out_shape=(sem(len(plan)), sem(len(plan)), sem(len(local)), *[pltpu.HBM(a.shape, a.dtype) for a in operands],
                   _sds((SUBLANES, LANES), F32)),
        in_specs=[HBM_SPEC] * (n_in + n_out),
        out_specs=(SEM_SPEC, SEM_SPEC, SEM_SPEC, *[HBM_SPEC] * (n_in + n_out), pl.BlockSpec(memory_space=pltpu.VMEM)),
        input_output_aliases={i: 3 + i for i in range(n_in + n_out)},
        compiler_params=pltpu.CompilerParams(has_side_effects=pltpu.SideEffectType.DATAFLOW_SIDE_EFFECTING))(*operands)
    return out[:3], out[3:3 + n_in], out[3 + n_in:3 + n_in + n_out], out[-1]


def _exchange_wait(name, sems, thru, zones, plan, local, after):
    n_in, n_out = len(thru), len(zones)

    def body(*refs):
        bufs = refs[:n_in + n_out]
        sends, arrivals, locs = _split_copies(bufs, plan, local, *refs[n_in + n_out:n_in + n_out + 3])
        for cp in arrivals:
            cp.wait_recv()
        for cp in sends:
            cp.wait_send()
        for cp in locs:
            cp.wait()

    out = pl.pallas_call(
        body, name=name, out_shape=tuple(pltpu.HBM(a.shape, a.dtype) for a in list(thru) + list(zones)),
        in_specs=[HBM_SPEC] * (n_in + n_out) + [SEM_SPEC] * 3 + [pl.BlockSpec(memory_space=pl.ANY)],
        out_specs=tuple([HBM_SPEC] * (n_in + n_out)), input_output_aliases={i: i for i in range(n_in + n_out)},
        compiler_params=pltpu.CompilerParams(has_side_effects=pltpu.SideEffectType.DATAFLOW_SIDE_EFFECTING))(
            *thru, *zones, *sems, after)
    return out[n_in:]


def _whole(*_):
    return ()


def _half_rows(rows, core):
    return pl.ds(core * (rows // 2), rows // 2)


def _gather_weights_plan(shards):
    n = len(shards)
    dsts = [_sds((N_CHIPS,) + a.shape, a.dtype) for a in shards]
    fetch, forward = [], []
    for t, a in enumerate(shards):
        rows = a.shape[0]
        if rows % (2 * 16) == 0:
            fetch += [(rel, t, (lambda s_, r_, rows=rows: (_half_rows(rows, s_[2]),)), n + t,
                       (lambda s_, r_, rows=rows: (_chip_of(s_), _half_rows(rows, s_[2])))) for rel in REL_CHIPS]
            forward += [(REL_SIBLING, n + t, (lambda s_, r_, rows=rows, rel=rel: (_chip_of(_flip(s_, rel)), _half_rows(rows, s_[2]))),
                         n + t, (lambda s_, r_, rows=rows, rel=rel: (_chip_of(_flip(s_, rel)), _half_rows(rows, s_[2]))))
                        for rel in REL_CHIPS]
        else:
            fetch += [(rel, t, _whole, n + t, lambda s_, r_: (_chip_of(s_),)) for rel in REL_CHIPS]
    local = [(t, _whole, n + t, lambda me: (_chip_of(me),)) for t in range(n)]
    return dsts, fetch, local, forward


def _gather_weights_start(name, shards):
    dsts, fetch, local, forward = _gather_weights_plan(shards)
    sems, thru, zones, token = _exchange_start(name, shards, dsts, fetch, local)
    return (sems, thru, zones, fetch, local, forward), token


def _gather_weights_finish(name, pending, after):
    sems, thru, zones, fetch, local, forward = pending
    landed = _exchange_wait(name + "_wait", sems, thru, zones, fetch, local, after)
    n = len(landed)
    return _exchange(name + "_forward", landed, n, [[_sds(a.shape, a.dtype) for a in landed], (forward, [])],
                     aliases={t: t for t in range(n)})


def _gather_chips(name, arrays):
    n = len(arrays)
    dsts = [_sds((N_CHIPS,) + a.shape, a.dtype) for a in arrays]
    plan = [(rel, t, _whole, n + t, lambda s_, r_: (_chip_of(s_),)) for t in range(n) for rel in REL_CHIPS]
    local = [(t, _whole, n + t, lambda me: (_chip_of(me),)) for t in range(n)]
    return _exchange(name, arrays, n, [dsts, (plan, local)])


def _gather_all(name, a):
    plan = [(rel, 0, _whole, 1, lambda s_, r_: (_dev_of(s_),)) for rel in REL_ALL]
    local = [(0, _whole, 1, lambda me: (_dev_of(me),))]
    return _exchange(name, [a], 1, [[_sds((2 * N_CHIPS,) + a.shape, a.dtype)], (plan, local)])[0]


def _swap_halves(name, grads):
    n = len(grads)
    dsts = [_sds((g.shape[0], g.shape[1] // 2, g.shape[2]), g.dtype) for g in grads]
    plan = [(REL_SIBLING, t, (lambda s_, r_, rows=g.shape[1]: (pl.ds(0, N_CHIPS), _half_rows(rows, r_[2]))), n + t, _whole)
            for t, g in enumerate(grads)]
    return _exchange(name, grads, n, [dsts, (plan, [])])


def _scatter_chips_plan(sums):
    n = len(sums)
    dsts = [_sds(a.shape, a.dtype) for a in sums]
    plan = [(rel, t, lambda s_, r_: (_chip_of(r_),), n + t, lambda s_, r_: (_chip_of(s_),))
            for t in range(n) for rel in REL_CHIPS]
    local = [(t, lambda me: (_chip_of(me),), n + t, lambda me: (_chip_of(me),)) for t in range(n)]
    return dsts, plan, local


def _scatter_chips(name, sums):
    dsts, plan, local = _scatter_chips_plan(sums)
    return _exchange(name, sums, len(sums), [dsts, (plan, local)])


def _scatter_chips_start(name, sums):
    dsts, plan, local = _scatter_chips_plan(sums)
    sems, thru, zones, token = _exchange_start(name, sums, dsts, plan, local)
    return (sems, thru, zones, plan, local), token


def _scatter_chips_finish(name, pending, after):
    sems, thru, zones, plan, local = pending
    return _exchange_wait(name + "_wait", sems, thru, zones, plan, local, after)


def _swap_back(name, totals, layer):
    n = len(totals)
    dsts = [_sds(a.shape, a.dtype) for a in totals]
    plan = [(REL_SIBLING, n + t, (lambda s_, r_, rows=a.shape[1]: (layer, _half_rows(rows, s_[2]))),
             n + t, (lambda s_, r_, rows=a.shape[1]: (layer, _half_rows(rows, s_[2])))) for t, a in enumerate(totals)]
    return _exchange(name, totals, n, [dsts, (plan, [])], aliases={t: t for t in range(n)})


def _add_halves(name, g, recv, core):
    nch, half, c = recv.shape

    def body(core_ref, g_ref, r_ref, o_ref):
        o_ref[0] = (g_ref[0, 0].astype(F32) + r_ref[0].astype(F32)).astype(o_ref.dtype)

    spec = pltpu.PrefetchScalarGridSpec(
        num_scalar_prefetch=1, grid=(nch,),
        in_specs=[pl.BlockSpec((1, 1, half, c), lambda k, core_ref: (k, core_ref[0], 0, 0)),
                  pl.BlockSpec((1, half, c), lambda k, core_ref: (k, 0, 0))],
        out_specs=pl.BlockSpec((1, half, c), lambda k, core_ref: (k, 0, 0)))
    return pl.pallas_call(body, name=name, out_shape=_sds(recv.shape, recv.dtype), grid_spec=spec,
                          compiler_params=pltpu.CompilerParams(dimension_semantics=("parallel",),
                                                               vmem_limit_bytes=VMEM_LIMIT))(
                                                                   core, g.reshape(nch, 2, half, c), recv)


def _sum_chips(name, a, core, layer, total):
    nch, half, c = a.shape

    def body(core_ref, a_ref, *rest):
        o_ref = rest[-1]
        acc = a_ref[0].astype(F32)
        for k in range(1, nch):
            acc = acc + a_ref[k].astype(F32)
        o_ref[0, 0] = acc

    in_specs = [pl.BlockSpec((nch, half, c), lambda i, core_ref: (0, 0, 0))]
    args = [core, a]
    if total is not None:
        in_specs.append(pl.BlockSpec(memory_space=pl.ANY))
        args.append(total.reshape(DEPTH, 2, half, c))
    spec = pltpu.PrefetchScalarGridSpec(
        num_scalar_prefetch=1, grid=(1,), in_specs=in_specs,
        out_specs=pl.BlockSpec((1, 1, half, c), lambda i, core_ref: (layer, core_ref[0], 0, 0)))
    out = pl.pallas_call(body, name=name, out_shape=_sds((DEPTH, 2, half, c), F32), grid_spec=spec,
                         input_output_aliases={2: 0} if total is not None else {},
                         compiler_params=pltpu.CompilerParams(dimension_semantics=("arbitrary",),
                                                              vmem_limit_bytes=VMEM_LIMIT))(*args)
    return out.reshape(DEPTH, 2 * half, c)


def _adamw_update(w, g, m, v):
    mn = ADAM_B1 * m + (1.0 - ADAM_B1) * g
    vn = ADAM_B2 * v + (1.0 - ADAM_B2) * (g * g)
    m_hat = mn / (1.0 - ADAM_B1 ** ADAM_STEP)
    v_hat = vn / (1.0 - ADAM_B2 ** ADAM_STEP)
    return -ADAM_LR * (m_hat / (jnp.sqrt(v_hat) + ADAM_EPS) + ADAM_WD * w), mn, vn


def _adamw(name, w, g, m, v):
    depth, r, c = w.shape
    tr = next(t for t in (512, 448, 384, 352, 336, 256, 192, 128, 64, 32, 16, 8) if r % t == 0 and t * c <= ADAM_TILE_ELEMS)

    def body(w_ref, g_ref, m_ref, v_ref, go_ref, d_ref, mo_ref, vo_ref):
        gv = g_ref[...]
        go_ref[...] = gv
        d_ref[...], mo_ref[...], vo_ref[...] = _adamw_update(w_ref[...], gv, m_ref[...], v_ref[...])

    spec = pl.BlockSpec((1, tr, c), lambda l, i: (l, i, 0))
    o = _sds(w.shape, F32)
    return _call(body, name, (o, o, o, o), (depth, r // tr), [spec] * 4, (spec,) * 4, ("parallel", "parallel"))(w, g, m, v)


def _adamw_small(name, ws, gs, ms, vs):
    n = len(ws)

    def body(*refs):
        for t in range(n):
            w_ref, g_ref, m_ref, v_ref = (refs[k * n + t] for k in range(4))
            d_ref, mo_ref, vo_ref = (refs[(4 + k) * n + t] for k in range(3))
            d_ref[...], mo_ref[...], vo_ref[...] = _adamw_update(w_ref[...], g_ref[...], m_ref[...], v_ref[...])

    outs = [_sds(w.shape, F32) for w in ws]
    res = pl.pallas_call(body, name=name, out_shape=tuple(outs * 3))(*ws, *gs, *ms, *vs)
    return res[:n], res[n:2 * n], res[2 * n:]


def _pack(arrays, rows):
    flat = jnp.concatenate([a.reshape(-1).astype(F32) for a in arrays])
    return _pad_to(flat, 0, rows * LANES).reshape(rows, LANES)


def _unpack(packed, shapes):
    flat = packed.reshape(-1)
    out, off = [], 0
    for shp in shapes:
        n = 1
        for s_ in shp:
            n *= s_
        out.append(flat[off:off + n].reshape(shp))
        off += n
    return out


_MATRICES = ("w_in", "proj_a", "proj_b", "w_out", "w_gate", "w_up", "w_down")
_SMALL = (("norm1_w", (D_MODEL,)), ("sinks", (N_Q_HEADS,)), ("sgu_ln_w", (SGU_WIDTH,)), ("sgu_ln_b", (SGU_WIDTH,)),
          ("sgu_w", (SGU_GROUPS, SGU_CHUNK, SGU_CHUNK)), ("sgu_b", (SGU_GROUPS, SGU_CHUNK)), ("norm2_w", (D_MODEL,)),
          ("conv_w", (3, FFN_DIM)), ("conv_b", (FFN_DIM,)), ("final_norm_w", (D_MODEL,)))
SMALL_ROWS = 320
ADAM_TILE_ELEMS = 384 * 1024


def _reduce_cores(tag, partial, core):
    names = list(partial)
    from_sibling = _swap_halves(tag + "_cores", [partial[k] for k in names])
    return names, [_add_halves(f"{tag}_cores_add_{k}", partial[k], r, core) for k, r in zip(names, from_sibling)]


def _reduce_finish(tag, l, names, from_chips, core, totals):
    sums = [_sum_chips(f"{tag}_chips_add_{k}", a, core, l, totals.get(k)) for k, a in zip(names, from_chips)]
    return dict(totals, **dict(zip(names, _swap_back(tag + "_back", sums, l))))


def kernel(x, c, positions, ada_w, ada_b, norm1_w, w_in, attn_sinks, sgu_ln_w, sgu_ln_b, sgu_w, sgu_b, proj_a, proj_b, w_out, norm2_w, ffn_w_gate, ffn_w_up, ffn_conv_w, ffn_conv_b, ffn_w_down, final_norm_w, loss_target, m_ada_w, m_ada_b, m_norm1_w, m_w_in, m_attn_sinks, m_sgu_ln_w, m_sgu_ln_b, m_sgu_w, m_sgu_b, m_proj_a, m_proj_b, m_w_out, m_norm2_w, m_ffn_w_gate, m_ffn_w_up, m_ffn_conv_w, m_ffn_conv_b, m_ffn_w_down, m_final_norm_w, v_ada_w, v_ada_b, v_norm1_w, v_w_in, v_attn_sinks, v_sgu_ln_w, v_sgu_ln_b, v_sgu_w, v_sgu_b, v_proj_a, v_proj_b, v_w_out, v_norm2_w, v_ffn_w_gate, v_ffn_w_up, v_ffn_conv_w, v_ffn_conv_b, v_ffn_w_down, v_final_norm_w):
    d = D_MODEL
    ax, ay, ac = lax.axis_index("x"), lax.axis_index("y"), lax.axis_index("c")
    chip = 2 * ax + ay
    dev = 4 * ax + 2 * ay + ac
    core = ac.astype(jnp.int32).reshape(1)

    c_all = _gather_all("gather_cond", c.reshape(SUBLANES, d // SUBLANES)).reshape(2 * N_CHIPS, d)
    c_rows = _pad_to(c_all, 0, ADA_ROWS)
    ada_cols = ada_w.shape[2]
    ada_b_cols = lax.dynamic_slice_in_dim(ada_b, chip * ada_cols, ada_cols, axis=1).reshape(DEPTH, 1, ada_cols)
    mod_cols = _ada_fwd("ada_fwd", c_rows, ada_w, ada_b_cols)
    mod_all = _gather_chips("gather_mod", [mod_cols])[0]
    mod_mine = lax.dynamic_index_in_dim(mod_all, dev, axis=2, keepdims=False)
    mod_mine = mod_mine.transpose(1, 0, 2).reshape(DEPTH, 1, 6 * d)
    mods = [tuple(jnp.split(mod_mine[l], 6, axis=-1)) for l in range(DEPTH)]

    tr = lambda a: jnp.swapaxes(a, 1, 2)
    shards = [tr(w_in).astype(BF), proj_a.astype(BF), proj_b.astype(BF), w_out.astype(BF),
              _pad_to(tr(ffn_w_gate).astype(BF), 1, FFN_SHARD_PAD), _pad_to(tr(ffn_w_up).astype(BF), 1, FFN_SHARD_PAD),
              _pad_to(ffn_w_down.astype(BF), 1, FFN_SHARD_PAD), _pad_to(ffn_conv_w, 2, FFN_SHARD_PAD)]
    token = mod_all[0, 0, :SUBLANES, :LANES]
    fetches = []
    for l in range(DEPTH):
        groups = []
        for tag, members in (("in", shards[:1]), ("mix", shards[1:4]), ("ffn", shards[4:])):
            behind = (token[0, 0] * 0.0).astype(members[0].dtype)
            pending, token = _gather_weights_start(f"l{l}_gather_{tag}", [members[0][l] + behind] + [a[l] for a in members[1:]])
            groups.append(pending)
        fetches.append(groups)

    small_in = dict(norm1_w=norm1_w, sinks=attn_sinks, sgu_ln_w=sgu_ln_w, sgu_ln_b=sgu_ln_b, sgu_w=sgu_w, sgu_b=sgu_b,
                    norm2_w=norm2_w, conv_b=ffn_conv_b)
    cosf, sinf = _rope_tables(positions[0])
    small_of = lambda l: {k: v[l] for k, v in small_in.items()}

    h = x[0]
    saved, params = [], []
    for l in range(DEPTH):
        first, mix, ffn = fetches[l]
        w_in_l = _gather_weights_finish(f"l{l}_gather_in", first, token if l == 0 else h)
        late = lambda y, l=l, mix=mix: _mix_params(*_gather_weights_finish(f"l{l}_gather_mix", mix, y), small_of(l))
        later = lambda y, l=l, ffn=ffn: _ffn_params(*_gather_weights_finish(f"l{l}_gather_ffn", ffn, y), small_of(l))
        h, sv, p = _layer_fwd(l, h, mods[l], _early_params(w_in_l[0], small_of(l)), cosf, sinf, late=late, later=later)
        saved.append(sv)
        params.append(p)
    dx, loss_part, d_final = _loss_head("loss_head", h, final_norm_w.reshape(1, d), loss_target[0])
    loss = lax.psum(loss_part[0, 0], ("x", "y", "c"))

    def small_pack(l, grads):
        cw, cb = _conv_grads_natural(grads)
        nat = dict(grads, conv_w=cw, conv_b=cb, final_norm_w=d_final if l == DEPTH - 1 else jnp.zeros((d,), F32))
        return _pack([nat[k] for k, _ in _SMALL], N_CHIPS * SMALL_ROWS).reshape(N_CHIPS, SMALL_ROWS, LANES)

    inflight = []

    def send(tag, l, partial):
        names, sums = _reduce_cores(tag, partial, core)
        pending, token = _scatter_chips_start(tag + "_chips", sums)
        inflight.append((tag, l, names, pending))
        return token

    dmods = [None] * DEPTH
    dx, dmods[1], grads = _layer_bwd(1, dx, mods[1], params[1], saved[1], cosf, sinf)
    token = send("l1_reduce", 1, dict({k: grads[k] for k in _MATRICES}, small=small_pack(1, grads)))
    dx, dmods[0], grads = _layer_bwd(0, dx, mods[0], params[0], saved[0], cosf, sinf, after=token,
                                     emit=lambda part: send("l0_reduce_" + "_".join(part), 0, part))
    dmod_mine = jnp.concatenate([jnp.concatenate(dmods[l], axis=1) for l in range(DEPTH)], axis=1)
    dmod_all = _gather_all("gather_dmod", dmod_mine.reshape(SUBLANES, -1)).reshape(2 * N_CHIPS, DEPTH * 6 * d)
    send("l0_reduce_in", 0, dict(w_in=grads["w_in"], small=small_pack(0, grads) + dmod_all[0, 0] * 0.0))
    totals = {}
    for tag, l, names, pending in inflight[:-1]:
        totals = _reduce_finish(tag, l, names, _scatter_chips_finish(tag + "_chips", pending, dx), core, totals)

    g_ada_b = _colsum("ada_b_grad", dmod_all).reshape(DEPTH, 6 * d)
    dmod_cols = jnp.stack([lax.dynamic_slice_in_dim(dmod_all, l * 6 * d + chip * ada_cols, ada_cols, axis=1)
                           for l in range(DEPTH)])
    g_ada_w = _ada_bwd("ada_w_grad", c_rows, _pad_to(dmod_cols, 1, ADA_ROWS))
    big = dict(w_in=(tr(w_in), tr(m_w_in), tr(v_w_in)), proj_a=(proj_a, m_proj_a, v_proj_a), proj_b=(proj_b, m_proj_b, v_proj_b),
               w_out=(w_out, m_w_out, v_w_out), w_gate=(tr(ffn_w_gate), tr(m_ffn_w_gate), tr(v_ffn_w_gate)),
               w_up=(tr(ffn_w_up), tr(m_ffn_w_up), tr(v_ffn_w_up)), w_down=(ffn_w_down, m_ffn_w_down, v_ffn_w_down))
    upd, g_big = {}, {}

    def update(k):
        res = _adamw("adamw_" + k, big[k][0], totals[k], *big[k][1:])
        res = [tr(a) for a in res] if k in ("w_in", "w_gate", "w_up") else res
        g_big[k], upd[k] = res[0], res[1:]

    for k in ("w_down", "w_gate", "w_up", "w_out", "proj_a", "proj_b"):
        update(k)
    g_big["ada_w"], *upd["ada_w"] = _adamw("adamw_ada_w", ada_w, g_ada_w, m_ada_w, v_ada_w)
    tag, l, names, pending = inflight[-1]
    totals = _reduce_finish(tag, l, names, _scatter_chips_finish(tag + "_chips", pending, upd["ada_w"][0]), core, totals)
    update("w_in")

    small_all = _gather_chips("gather_small", [totals["small"]])[0]
    small_g = small_all.transpose(1, 0, 2, 3).reshape(DEPTH, -1)
    per_layer = [_unpack(small_g[l], [shp for _, shp in _SMALL]) for l in range(DEPTH)]
    sg = {k: jnp.stack([per_layer[l][i] for l in range(DEPTH)]) for i, (k, _) in enumerate(_SMALL)}
    g_final = sg["final_norm_w"][DEPTH - 1]
    g_conv_w = lax.dynamic_slice_in_dim(sg["conv_w"], chip * FFN_SHARD, FFN_SHARD, axis=2)

    rest = [("ada_b", ada_b, g_ada_b, m_ada_b, v_ada_b), ("norm1_w", norm1_w, sg["norm1_w"], m_norm1_w, v_norm1_w),
            ("attn_sinks", attn_sinks, sg["sinks"], m_attn_sinks, v_attn_sinks),
            ("sgu_ln_w", sgu_ln_w, sg["sgu_ln_w"], m_sgu_ln_w, v_sgu_ln_w),
            ("sgu_ln_b", sgu_ln_b, sg["sgu_ln_b"], m_sgu_ln_b, v_sgu_ln_b), ("sgu_w", sgu_w, sg["sgu_w"], m_sgu_w, v_sgu_w),
            ("sgu_b", sgu_b, sg["sgu_b"], m_sgu_b, v_sgu_b), ("norm2_w", norm2_w, sg["norm2_w"], m_norm2_w, v_norm2_w),
            ("ffn_conv_w", ffn_conv_w, g_conv_w, m_ffn_conv_w, v_ffn_conv_w),
            ("ffn_conv_b", ffn_conv_b, sg["conv_b"], m_ffn_conv_b, v_ffn_conv_b),
            ("final_norm_w", final_norm_w.reshape(1, d), g_final.reshape(1, d), m_final_norm_w.reshape(1, d),
             v_final_norm_w.reshape(1, d))]
    rest_out = _adamw_small("adamw_rest", *[[r[i] for r in rest] for i in (1, 2, 3, 4)])
    g_rest = {r[0]: r[2] for r in rest}
    u_rest = {r[0]: tuple(o[i] for o in rest_out) for i, r in enumerate(rest)}
    g_rest["final_norm_w"] = g_final
    u_rest["final_norm_w"] = tuple(a.reshape(d) for a in u_rest["final_norm_w"])

    names = ("ada_w", "ada_b", "norm1_w", "w_in", "attn_sinks", "sgu_ln_w", "sgu_ln_b", "sgu_w", "sgu_b", "proj_a", "proj_b",
             "w_out", "norm2_w", "ffn_w_gate", "ffn_w_up", "ffn_conv_w", "ffn_conv_b", "ffn_w_down", "final_norm_w")
    alias = {"ffn_w_gate": "w_gate", "ffn_w_up": "w_up", "ffn_w_down": "w_down"}
    grad_of = lambda n: g_rest[n] if n in g_rest else g_big[alias.get(n, n)]
    upd_of = lambda n: u_rest[n] if n in u_rest else upd[alias.get(n, n)]
    return (loss, dx[None], *[grad_of(n) for n in names], *[upd_of(n)[0] for n in names],
            *[upd_of(n)[1] for n in names], *[upd_of(n)[2] for n in names])
```

```python
import jax
import jax.numpy as jnp
from jax import lax
from jax.experimental import pallas as pl
from jax.experimental.pallas import tpu as pltpu

F32 = jnp.float32
BF = jnp.bfloat16

D_MODEL = 1024
N_Q_HEADS = 16
N_KV_HEADS = 2
HEAD_DIM = 64
ATTN_BLOCK = 128
ROPE_THETA = 500000.0
ROT_DIM = HEAD_DIM // 4
SGU_WIDTH = 1024
SGU_GROUPS = 8
SGU_CHUNK = 128
FFN_DIM = 2816
NORM_EPS = 1e-6
DEPTH = 2
IN_COLS = 5376
N_CHIPS = 4
FFN_SHARD = FFN_DIM // N_CHIPS
FFN_SHARD_PAD = 768
FFN_PAD = N_CHIPS * FFN_SHARD_PAD
LANES = 128
SUBLANES = 8
HALO = 16
VMEM_LIMIT = 56 * 1024 * 1024
NEG_BIG = -1e30

ADAM_LR = 0.001
ADAM_B1 = 0.9
ADAM_B2 = 0.999
ADAM_EPS = 1e-08
ADAM_WD = 0.01
ADAM_STEP = 10

MESH = pl.DeviceIdType.MESH

Q_END = 1024
KV_END = 1280
U_END = 2304
Z_END = 3328
GA_END = 4352


def _sds(shape, dtype):
    return jax.ShapeDtypeStruct(tuple(shape), dtype)


def _call(body, name, out_shape, grid, in_specs, out_specs, semantics, scratch=(), after=None):
    n_in = len(in_specs)
    fn = body
    if after is not None:
        def fn(*refs):
            return body(*refs[:n_in], *refs[n_in + 1:])

        in_specs = list(in_specs) + [pl.BlockSpec(memory_space=pl.ANY)]
    call = pl.pallas_call(
        fn, name=name, out_shape=out_shape, grid=grid, in_specs=in_specs, out_specs=out_specs,
        scratch_shapes=scratch,
        compiler_params=pltpu.CompilerParams(dimension_semantics=semantics, vmem_limit_bytes=VMEM_LIMIT))
    if after is None:
        return call
    return lambda *args: call(*args, after)


def _rows(tm, width, col=0):
    return pl.BlockSpec((tm, width), lambda i: (i, col))


def _vec(width):
    return pl.BlockSpec((1, width), lambda i: (0, 0))


def _resident(shape):
    zeros = (0,) * len(shape)
    return pl.BlockSpec(tuple(shape), lambda *_: zeros, pipeline_mode=pl.Buffered(1))


def _sigmoid(x):
    return 0.5 + 0.5 * jnp.tanh(0.5 * x)


def _gelu(x):
    return 0.5 * x * (1.0 + lax.erf(x * 0.7071067811865476))


def _gelu_grad(x):
    cdf = 0.5 * (1.0 + lax.erf(x * 0.7071067811865476))
    return cdf + x * jnp.exp(-0.5 * x * x) * 0.3989422804014327


def _dot(a, b):
    return jnp.dot(a, b, preferred_element_type=F32)


def _dot_nt(a, b):
    return lax.dot_general(a, b, (((1,), (1,)), ((), ())), preferred_element_type=F32)


def _dot_tn(a, b):
    return lax.dot_general(a, b, (((0,), (0,)), ((), ())), preferred_element_type=F32)


def _rms(xv):
    return lax.rsqrt(jnp.mean(xv * xv, axis=-1, keepdims=True) + NORM_EPS)


def _matmul_tn(name, a, b, tk=512, tn=1024, blocked=False, after=None):
    s, k = a.shape
    n = b.shape[1]
    tk, tn = min(tk, k), min(tn, n)

    def body(a_ref, b_ref, o_ref):
        res = _dot_tn(a_ref[...], b_ref[...]).astype(o_ref.dtype)
        if blocked:
            o_ref[0] = res
        else:
            o_ref[...] = res

    if blocked:
        out, ospec = _sds((n // tn, k, tn), BF), pl.BlockSpec((1, tk, tn), lambda i, j: (j, i, 0))
    else:
        out, ospec = _sds((k, n), BF), pl.BlockSpec((tk, tn), lambda i, j: (i, j))
    return _call(body, name, out, (k // tk, n // tn),
                 [pl.BlockSpec((s, tk), lambda i, j: (0, i)), pl.BlockSpec((s, tn), lambda i, j: (0, j))],
                 ospec, ("parallel", "parallel"), after=after)(a, b)


def _matmul_tn_rows(name, a, b, out, row0, rows_total, tk=256):
    s, k = a.shape
    n = b.shape[1]

    def body(a_ref, b_ref, *rest):
        rest[-1][...] = _dot_tn(a_ref[...], b_ref[...]).astype(BF)

    in_specs = [pl.BlockSpec((s, tk), lambda i: (0, i)), _resident(b.shape)]
    args = [a, b]
    if out is not None:
        in_specs.append(pl.BlockSpec(memory_space=pl.ANY))
        args.append(out)
    return pl.pallas_call(
        body, name=name, out_shape=_sds((rows_total, n), BF), grid=(k // tk,), in_specs=in_specs,
        out_specs=pl.BlockSpec((tk, n), lambda i: (row0 // tk + i, 0)),
        input_output_aliases={2: 0} if out is not None else {},
        compiler_params=pltpu.CompilerParams(dimension_semantics=("parallel",), vmem_limit_bytes=VMEM_LIMIT))(*args)


def _rope_partner(v):
    lane = lax.broadcasted_iota(jnp.int32, (1, LANES), 1) % HEAD_DIM
    return jnp.where(lane < ROT_DIM // 2, pltpu.roll(v, LANES - ROT_DIM // 2, axis=1), pltpu.roll(v, ROT_DIM // 2, axis=1))


def _dup_half(v, half):
    lane = lax.broadcasted_iota(jnp.int32, (1, LANES), 1)
    keep = jnp.where((lane >= HEAD_DIM) == (half == 1), v, 0.0)
    return keep + pltpu.roll(keep, HEAD_DIM, axis=1)


def _in_proj(name, x, w, sc, sh, w_in, cosf, sinf, tm=256, after=None):
    s, d = x.shape
    tm = min(tm, s)

    def body(x_ref, w_ref, sc_ref, sh_ref, win_ref, cos_ref, sin_ref,
             h_ref, qr_ref, kk0_ref, kk1_ref, vv0_ref, vv1_ref, u_ref, v_ref, ga_ref, gb_ref):
        xv = x_ref[...]
        h = ((xv * _rms(xv)) * w_ref[...] * (1.0 + sc_ref[...]) + sh_ref[...]).astype(BF)
        h_ref[...] = h
        cosv, sinv = cos_ref[...], sin_ref[...]
        q = _dot_nt(h, win_ref[:Q_END, :])
        for j in range(D_MODEL // LANES):
            qv = q[:, j * LANES:(j + 1) * LANES]
            qr_ref[:, j * LANES:(j + 1) * LANES] = (qv * cosv + _rope_partner(qv) * sinv).astype(BF)
        kv = _dot_nt(h, win_ref[Q_END:KV_END, :])
        kr = kv[:, :LANES] * cosv + _rope_partner(kv[:, :LANES]) * sinv
        vv = kv[:, LANES:]
        kk0_ref[...] = _dup_half(kr, 0).astype(BF)
        kk1_ref[...] = _dup_half(kr, 1).astype(BF)
        vv0_ref[...] = _dup_half(vv, 0).astype(BF)
        vv1_ref[...] = _dup_half(vv, 1).astype(BF)
        u_ref[...] = _dot_nt(h, win_ref[KV_END:U_END, :])
        v_ref[...] = _dot_nt(h, win_ref[U_END:Z_END, :])
        ga_ref[...] = _dot_nt(h, win_ref[Z_END:GA_END, :]).astype(BF)
        gb_ref[...] = _dot_nt(h, win_ref[GA_END:, :]).astype(BF)

    wide, kvs, pre = _sds((s, d), BF), _sds((s, LANES), BF), _sds((s, d), F32)
    return _call(body, name, (wide, wide, kvs, kvs, kvs, kvs, pre, pre, wide, wide), (s // tm,),
                 [_rows(tm, d), _vec(d), _vec(d), _vec(d), _resident(w_in.shape), _rows(tm, LANES), _rows(tm, LANES)],
                 (_rows(tm, d), _rows(tm, d)) + (_rows(tm, LANES),) * 4 + (_rows(tm, d),) * 4, ("parallel",), after=after)(
                     x, w, sc, sh, w_in, cosf, sinf)


def _in_proj_bwd(name, dq, dkv, du, dv, dga, dgb, w_in, x, w, sc, dx_in, tm=256):
    s, d = x.shape
    tm = min(tm, s)

    def body(dq_ref, dkv_ref, du_ref, dv_ref, dga_ref, dgb_ref, win_ref, x_ref, w_ref, sc_ref, dxin_ref,
             dx_ref, da_ref, dsh_ref):
        @pl.when(pl.program_id(0) == 0)
        def _():
            da_ref[...] = jnp.zeros_like(da_ref)
            dsh_ref[...] = jnp.zeros_like(dsh_ref)

        dh = (_dot(dq_ref[...], win_ref[:Q_END, :]) + _dot(dkv_ref[...], win_ref[Q_END:KV_END, :])
              + _dot(du_ref[...], win_ref[KV_END:U_END, :]) + _dot(dv_ref[...], win_ref[U_END:Z_END, :])
              + _dot(dga_ref[...], win_ref[Z_END:GA_END, :]) + _dot(dgb_ref[...], win_ref[GA_END:, :]))
        xv = x_ref[...]
        r = _rms(xv)
        xn = xv * r
        dxn = dh * (w_ref[...] * (1.0 + sc_ref[...]))
        dx_ref[...] = dxin_ref[...] + r * (dxn - xn * jnp.mean(dxn * xn, axis=-1, keepdims=True))
        da_ref[...] += jnp.sum(dh * xn, axis=0, keepdims=True)
        dsh_ref[...] += jnp.sum(dh, axis=0, keepdims=True)

    return _call(body, name, (_sds((s, d), F32), _sds((1, d), F32), _sds((1, d), F32)), (s // tm,),
                 [_rows(tm, d), _rows(tm, 2 * LANES), _rows(tm, d), _rows(tm, d), _rows(tm, d), _rows(tm, d),
                  _resident(w_in.shape), _rows(tm, d), _vec(d), _vec(d), _rows(tm, d)],
                 (_rows(tm, d), _vec(d), _vec(d)), ("arbitrary",))(dq, dkv, du, dv, dga, dgb, w_in, x, w, sc, dx_in)


def _rope_bwd(name, dqr, dkv_cur, dkv_prev, cosf, sinf, tm=512):
    s = dqr.shape[0]
    tm = min(tm, s)
    steps = s // tm
    per = tm // ATTN_BLOCK
    nb = s // ATTN_BLOCK

    def unrope(v, cosv, sinv):
        return v * cosv - _rope_partner(v) * sinv

    def body(dq_ref, cur_ref, prev_ref, next_ref, cos_ref, sin_ref, dqo_ref, dkvo_ref):
        i = pl.program_id(0)
        cosv, sinv = cos_ref[...], sin_ref[...]
        for j in range(D_MODEL // LANES):
            dqo_ref[:, j * LANES:(j + 1) * LANES] = unrope(dq_ref[:, j * LANES:(j + 1) * LANES], cosv, sinv).astype(BF)
        nxt = jnp.where(i < steps - 1, next_ref[...], 0.0)
        if per > 1:
            shifted = jnp.concatenate([prev_ref[ATTN_BLOCK:, :], nxt], axis=0)
        else:
            shifted = nxt
        tot = cur_ref[...] + shifted
        dkvo_ref[:, :LANES] = unrope(tot[:, :LANES], cosv, sinv).astype(BF)
        dkvo_ref[:, LANES:] = tot[:, LANES:].astype(BF)

    nxt_spec = pl.BlockSpec((ATTN_BLOCK, 2 * LANES), lambda i: (jnp.minimum((i + 1) * per, nb - 1), 0))
    return _call(body, name, (_sds((s, D_MODEL), BF), _sds((s, 2 * LANES), BF)), (steps,),
                 [_rows(tm, D_MODEL), _rows(tm, 2 * LANES), _rows(tm, 2 * LANES), nxt_spec, _rows(tm, LANES),
                  _rows(tm, LANES)],
                 (_rows(tm, D_MODEL), _rows(tm, 2 * LANES)), ("parallel",))(dqr, dkv_cur, dkv_prev, dkv_prev, cosf, sinf)


Q_PER_KV = N_Q_HEADS // N_KV_HEADS


def _band_mask_t(n):
    kj = lax.broadcasted_iota(jnp.int32, (2 * ATTN_BLOCK, ATTN_BLOCK), 0)
    qi = lax.broadcasted_iota(jnp.int32, (2 * ATTN_BLOCK, ATTN_BLOCK), 1)
    return (kj > qi) & (kj <= qi + ATTN_BLOCK) & ((n > 0) | (kj >= ATTN_BLOCK))


def _softmax_t(raw, allowed, sink):
    sc = jnp.where(allowed, raw * (HEAD_DIM ** -0.5), NEG_BIG)
    m = jnp.maximum(jnp.max(sc, axis=0, keepdims=True), sink)
    p = jnp.exp(sc - m)
    esink = jnp.exp(sink - m)
    inv = 1.0 / (jnp.sum(p, axis=0, keepdims=True) + esink)
    return p * inv, esink * inv


def _kv_specs():
    cur = pl.BlockSpec((ATTN_BLOCK, LANES), lambda n: (n, 0))
    prev = pl.BlockSpec((ATTN_BLOCK, LANES), lambda n: (jnp.maximum(n - 1, 0), 0))
    return [prev, cur] * 4


def _attention(name, qr, kk0, kk1, vv0, vv1, sinks):
    s = qr.shape[0]
    nb = s // ATTN_BLOCK

    def body(sink_ref, q_ref, k0p, k0c, k1p, k1c, v0p, v0c, v1p, v1c, y_ref):
        allowed = _band_mask_t(pl.program_id(0))
        upper = lax.broadcasted_iota(jnp.int32, (1, LANES), 1) >= HEAD_DIM
        upper_rows = lax.broadcasted_iota(jnp.int32, (LANES, 1), 0) >= HEAD_DIM
        bands = ((jnp.concatenate([k0p[...], k0c[...]], axis=0), jnp.concatenate([v0p[...], v0c[...]], axis=0)),
                 (jnp.concatenate([k1p[...], k1c[...]], axis=0), jnp.concatenate([v1p[...], v1c[...]], axis=0)))
        vbts = (bands[0][1].T, bands[1][1].T)

        def scores(h):
            hk, j, half = h // Q_PER_KV, (h % Q_PER_KV) // 2, h % 2
            col = (hk * 4 + j) * LANES
            qp = q_ref[:, col:col + LANES]
            return _dot_nt(bands[hk][0], jnp.where(upper if half else jnp.logical_not(upper), qp, jnp.zeros_like(qp)))

        out_t = None
        ahead = scores(0)
        for h in range(N_Q_HEADS):
            hk, j, half = h // Q_PER_KV, (h % Q_PER_KV) // 2, h % 2
            raw = ahead
            if h + 1 < N_Q_HEADS:
                ahead = scores(h + 1)
            pn, _ = _softmax_t(raw, allowed, sink_ref[h])
            o_h = _dot(vbts[hk], pn.astype(BF))
            out_t = jnp.where(upper_rows, o_h, out_t) if half else o_h
            if half:
                col = (hk * 4 + j) * LANES
                y_ref[:, col:col + LANES] = out_t.T.astype(BF)

    return _call(body, name, _sds((s, D_MODEL), BF), (nb,),
                 [pl.BlockSpec(memory_space=pltpu.SMEM), pl.BlockSpec((ATTN_BLOCK, D_MODEL), lambda n: (n, 0))] + _kv_specs(),
                 pl.BlockSpec((ATTN_BLOCK, D_MODEL), lambda n: (n, 0)), ("parallel",))(
                     sinks, qr, kk0, kk0, kk1, kk1, vv0, vv0, vv1, vv1)


def _attention_bwd(name, qr, kk0, kk1, vv0, vv1, sinks, dy, after=None):
    s = qr.shape[0]
    nb = s // ATTN_BLOCK

    def body(sink_ref, q_ref, dy_ref, k0p, k0c, k1p, k1c, v0p, v0c, v1p, v1c, dq_ref, cur_ref, prev_ref, dsink_ref):
        @pl.when(pl.program_id(0) == 0)
        def _():
            dsink_ref[...] = jnp.zeros_like(dsink_ref)

        allowed = _band_mask_t(pl.program_id(0))
        lane = lax.broadcasted_iota(jnp.int32, (1, LANES), 1)
        upper = lane >= HEAD_DIM
        upper_rows = lax.broadcasted_iota(jnp.int32, (LANES, 1), 0) >= HEAD_DIM
        bands = ((jnp.concatenate([k0p[...], k0c[...]], axis=0), jnp.concatenate([v0p[...], v0c[...]], axis=0)),
                 (jnp.concatenate([k1p[...], k1c[...]], axis=0), jnp.concatenate([v1p[...], v1c[...]], axis=0)))
        kbts = (bands[0][0].T, bands[1][0].T)

        def scores(h):
            hk, j, half = h // Q_PER_KV, (h % Q_PER_KV) // 2, h % 2
            kb, vb = bands[hk]
            col = (hk * 4 + j) * LANES
            sel = upper if half else jnp.logical_not(upper)
            qp = q_ref[:, col:col + LANES]
            qa = jnp.where(sel, qp, jnp.zeros_like(qp))
            dya = jnp.where(sel, dy_ref[:, col:col + LANES], 0.0).astype(BF)
            return qa, dya, _dot_nt(kb, qa), _dot_nt(vb, dya)

        dsink = jnp.zeros((1, LANES), F32)
        dk_slab = jnp.zeros((2 * ATTN_BLOCK, LANES), F32)
        dv_slab = jnp.zeros((2 * ATTN_BLOCK, LANES), F32)
        dkk = dvv = dq_t = None
        ahead = scores(0)
        for h in range(N_Q_HEADS):
            hk, j, half = h // Q_PER_KV, (h % Q_PER_KV) // 2, h % 2
            qa, dya, raw, dp = ahead
            if h + 1 < N_Q_HEADS:
                ahead = scores(h + 1)
            pn, psink = _softmax_t(raw, allowed, sink_ref[h])
            delta = jnp.sum(pn * dp, axis=0, keepdims=True)
            ds = (pn * (dp - delta) * (HEAD_DIM ** -0.5)).astype(BF)
            dsink = dsink + jnp.where(lane == h, -jnp.sum(psink * delta), 0.0)
            dq_h = _dot(kbts[hk], ds)
            dq_t = jnp.where(upper_rows, dq_h, dq_t) if half else dq_h
            dk_h, dv_h = _dot(ds, qa), _dot(pn.astype(BF), dya)
            dkk, dvv = (dk_h, dv_h) if h % Q_PER_KV == 0 else (dkk + dk_h, dvv + dv_h)
            if half:
                col = (hk * 4 + j) * LANES
                dq_ref[:, col:col + LANES] = dq_t.T
            if h % Q_PER_KV == Q_PER_KV - 1:
                mine = upper if hk else jnp.logical_not(upper)
                dk_slab = jnp.where(mine, dkk + pltpu.roll(dkk, HEAD_DIM, axis=1), dk_slab)
                dv_slab = jnp.where(mine, dvv + pltpu.roll(dvv, HEAD_DIM, axis=1), dv_slab)
        prev_ref[:, :LANES] = dk_slab[:ATTN_BLOCK]
        prev_ref[:, LANES:] = dv_slab[:ATTN_BLOCK]
        cur_ref[:, :LANES] = dk_slab[ATTN_BLOCK:]
        cur_ref[:, LANES:] = dv_slab[ATTN_BLOCK:]
        dsink_ref[...] += dsink

    blk = pl.BlockSpec((ATTN_BLOCK, D_MODEL), lambda n: (n, 0))
    kvo = pl.BlockSpec((ATTN_BLOCK, 2 * LANES), lambda n: (n, 0))
    return _call(body, name,
                 (_sds((s, D_MODEL), F32), _sds((s, 2 * LANES), F32), _sds((s, 2 * LANES), F32), _sds((1, LANES), F32)),
                 (nb,), [pl.BlockSpec(memory_space=pltpu.SMEM), blk, blk] + _kv_specs(),
                 (blk, kvo, kvo, pl.BlockSpec((1, LANES), lambda n: (0, 0))), ("arbitrary",), after=after)(
                     sinks, qr, dy, kk0, kk0, kk1, kk1, vv0, vv0, vv1, vv1)


def _sgu_weights(wm_ref, g):
    t = lax.broadcasted_iota(jnp.int32, (SGU_CHUNK, SGU_CHUNK), 0)
    sidx = lax.broadcasted_iota(jnp.int32, (SGU_CHUNK, SGU_CHUNK), 1)
    return jnp.where(sidx <= t, wm_ref[g], 0.0).astype(BF)


def _layer_norm_stats(v):
    mu = jnp.mean(v, axis=-1, keepdims=True)
    cen = v - mu
    rstd = lax.rsqrt(jnp.mean(cen * cen, axis=-1, keepdims=True) + NORM_EPS)
    return cen * rstd, rstd


def _sgu(name, u_pre, v_pre, ln_w, ln_b, wm, bfull, tm=256):
    s, w = u_pre.shape
    tm = min(tm, s)

    def body(u_ref, v_ref, lw_ref, lb_ref, wm_ref, b_ref, y_ref):
        vhat, _ = _layer_norm_stats(_gelu(v_ref[...]))
        vn = (vhat * lw_ref[...] + lb_ref[...]).astype(BF)
        for g in range(SGU_GROUPS):
            wg = _sgu_weights(wm_ref, g)
            cols = slice(g * SGU_CHUNK, (g + 1) * SGU_CHUNK)
            for ch in range(tm // SGU_CHUNK):
                rows = slice(ch * SGU_CHUNK, (ch + 1) * SGU_CHUNK)
                f = _dot(wg, vn[rows, cols]) + b_ref[g]
                y_ref[rows, cols] = (_gelu(u_ref[rows, cols]) * f).astype(BF)

    full3 = pl.BlockSpec((SGU_GROUPS, SGU_CHUNK, SGU_CHUNK), lambda i: (0, 0, 0))
    return _call(body, name, _sds((s, w), BF), (s // tm,),
                 [_rows(tm, w), _rows(tm, w), _vec(w), _vec(w), full3, full3],
                 _rows(tm, w), ("parallel",))(u_pre, v_pre, ln_w, ln_b, wm, bfull)


def _sgu_bwd(name, u_pre, v_pre, ln_w, ln_b, wm, bfull, dy, tm=256, after=None):
    s, w = u_pre.shape
    tm = min(tm, s)
    steps = s // tm

    def body(u_ref, v_ref, lw_ref, lb_ref, wm_ref, b_ref, dy_ref, du_ref, dv_ref, dwm_ref, db_ref, dlw_ref, dlb_ref,
             dfsum_ref):
        i = pl.program_id(0)

        @pl.when(i == 0)
        def _():
            dwm_ref[...] = jnp.zeros_like(dwm_ref)
            dlw_ref[...] = jnp.zeros_like(dlw_ref)
            dlb_ref[...] = jnp.zeros_like(dlb_ref)
            dfsum_ref[...] = jnp.zeros_like(dfsum_ref)

        vpre = v_ref[...]
        vhat, rstd = _layer_norm_stats(_gelu(vpre))
        vn = (vhat * lw_ref[...] + lb_ref[...]).astype(BF)
        t = lax.broadcasted_iota(jnp.int32, (SGU_CHUNK, SGU_CHUNK), 0)
        sidx = lax.broadcasted_iota(jnp.int32, (SGU_CHUNK, SGU_CHUNK), 1)
        dvn_cols = []
        for g in range(SGU_GROUPS):
            wg = _sgu_weights(wm_ref, g)
            cols = slice(g * SGU_CHUNK, (g + 1) * SGU_CHUNK)
            dvn_rows = []
            dwg = jnp.zeros((SGU_CHUNK, SGU_CHUNK), F32)
            dfs = jnp.zeros((SGU_CHUNK, SGU_CHUNK), F32)
            for ch in range(tm // SGU_CHUNK):
                rows = slice(ch * SGU_CHUNK, (ch + 1) * SGU_CHUNK)
                upre = u_ref[rows, cols]
                dyv = dy_ref[rows, cols].astype(F32)
                f = _dot(wg, vn[rows, cols]) + b_ref[g]
                du_ref[rows, cols] = (dyv * f * _gelu_grad(upre)).astype(BF)
                df = dyv * _gelu(upre)
                dfb = df.astype(BF)
                dvn_rows.append(_dot_tn(wg, dfb))
                dwg = dwg + _dot_nt(dfb, vn[rows, cols])
                dfs = dfs + df
            dwm_ref[g] += jnp.where(sidx <= t, dwg, 0.0)
            dfsum_ref[g] += dfs
            dvn_cols.append(jnp.concatenate(dvn_rows, axis=0) if len(dvn_rows) > 1 else dvn_rows[0])
        dvn = jnp.concatenate(dvn_cols, axis=1)
        dlw_ref[...] += jnp.sum(dvn * vhat, axis=0, keepdims=True)
        dlb_ref[...] += jnp.sum(dvn, axis=0, keepdims=True)
        dvh = dvn * lw_ref[...]
        dvg = rstd * (dvh - jnp.mean(dvh, axis=-1, keepdims=True) - vhat * jnp.mean(dvh * vhat, axis=-1, keepdims=True))
        dv_ref[...] = (dvg * _gelu_grad(vpre)).astype(BF)

        @pl.when(i == steps - 1)
        def _():
            for g in range(SGU_GROUPS):
                db_ref[g:g + 1, :] = jnp.sum(dfsum_ref[g].T, axis=0, keepdims=True)

    full3 = pl.BlockSpec((SGU_GROUPS, SGU_CHUNK, SGU_CHUNK), lambda i: (0, 0, 0))
    return _call(body, name,
                 (_sds((s, w), BF), _sds((s, w), BF), _sds((SGU_GROUPS, SGU_CHUNK, SGU_CHUNK), F32),
                  _sds((SGU_GROUPS, SGU_CHUNK), F32), _sds((1, w), F32), _sds((1, w), F32)),
                 (steps,),
                 [_rows(tm, w), _rows(tm, w), _vec(w), _vec(w), full3, full3, _rows(tm, w)],
                 (_rows(tm, w), _rows(tm, w), full3, pl.BlockSpec((SGU_GROUPS, SGU_CHUNK), lambda i: (0, 0)), _vec(w), _vec(w)),
                 ("arbitrary",), scratch=[pltpu.VMEM((SGU_GROUPS, SGU_CHUNK, SGU_CHUNK), F32)], after=after)(
                     u_pre, v_pre, ln_w, ln_b, wm, bfull, dy)


def _mix_out(name, y_sgu, y_attn, ga_pre, gb_pre, x, g1, proj_a, proj_b, w_out, w2, sc2, sh2, tm=256):
    s, d = x.shape
    tm = min(tm, s)

    def body(ys_ref, ya_ref, ga_ref, gb_ref, x_ref, g1_ref, wa_ref, wb_ref, wo_ref, w2_ref, sc2_ref, sh2_ref,
             m_ref, pa_ref, pb_ref, o_ref, x1_ref, h2_ref):
        pa = _dot(ys_ref[...], wa_ref[...].reshape(d, d))
        pb = _dot(ya_ref[...], wb_ref[...].reshape(d, d))
        pa_ref[...] = pa.astype(BF)
        pb_ref[...] = pb.astype(BF)
        merged = (_sigmoid(ga_ref[...].astype(F32)) * pa + _sigmoid(gb_ref[...].astype(F32)) * pb).astype(BF)
        m_ref[...] = merged
        o = _dot(merged, wo_ref[...].reshape(d, d))
        o_ref[...] = o.astype(BF)
        x1 = x_ref[...] + g1_ref[...] * o
        x1_ref[...] = x1
        h2_ref[...] = ((x1 * _rms(x1)) * w2_ref[...] * (1.0 + sc2_ref[...]) + sh2_ref[...]).astype(BF)

    f, b = _sds((s, d), F32), _sds((s, d), BF)
    r = _rows(tm, d)
    wspec = _resident(proj_a.shape)
    return _call(body, name, (b, b, b, b, f, b), (s // tm,),
                 [r, r, r, r, r, _vec(d), wspec, wspec, wspec, _vec(d), _vec(d), _vec(d)], (r,) * 6, ("parallel",))(
                     y_sgu, y_attn, ga_pre, gb_pre, x, g1, proj_a, proj_b, w_out, w2, sc2, sh2)


def _mix_bwd(name, do, w_out, proj_a, proj_b, ga_pre, gb_pre, pa, pb, tm=256):
    s, d = do.shape
    tm = min(tm, s)

    def body(do_ref, wo_ref, wa_ref, wb_ref, ga_ref, gb_ref, pa_ref, pb_ref,
             dpa_ref, dpb_ref, dga_ref, dgb_ref, dys_ref, dya_ref):
        dm = _dot_nt(do_ref[...], wo_ref[...].reshape(d, d))
        ga = _sigmoid(ga_ref[...].astype(F32))
        gb = _sigmoid(gb_ref[...].astype(F32))
        dpa = (dm * ga).astype(BF)
        dpb = (dm * gb).astype(BF)
        dpa_ref[...] = dpa
        dpb_ref[...] = dpb
        dga_ref[...] = (dm * pa_ref[...].astype(F32) * ga * (1.0 - ga)).astype(BF)
        dgb_ref[...] = (dm * pb_ref[...].astype(F32) * gb * (1.0 - gb)).astype(BF)
        dys_ref[...] = _dot_nt(dpa, wa_ref[...].reshape(d, d)).astype(BF)
        dya_ref[...] = _dot_nt(dpb, wb_ref[...].reshape(d, d)).astype(BF)

    f, b = _sds((s, d), F32), _sds((s, d), BF)
    r = _rows(tm, d)
    wspec = _resident(w_out.shape)
    return _call(body, name, (b, b, b, b, b, b), (s // tm,), [r, wspec, wspec, wspec, r, r, r, r], (r,) * 6,
                 ("parallel",))(do, w_out, proj_a, proj_b, ga_pre, gb_pre, pa, pb)


def _ffn_up_act(name, h2, w_gate, w_up, cw, cb, tm=1024):
    s, d = h2.shape
    tm = min(tm, s)
    tc = FFN_SHARD_PAD
    per = tm // HALO

    def body(h_ref, hprev_ref, wg_ref, wu_ref, cw_ref, cb_ref, a_ref, ac_ref, up_ref, hf_ref):
        hv = h_ref[...]
        a = _dot_nt(hv, wg_ref[0]).astype(BF)
        up = _dot_nt(hv, wu_ref[0]).astype(BF)
        a_ref[...] = a
        up_ref[...] = up
        prev = jnp.where(pl.program_id(1) > 0, _dot_nt(hprev_ref[...], wg_ref[0]).astype(BF).astype(F32), 0.0)
        ext = jnp.concatenate([prev, a.astype(F32)], axis=0)
        ac = (cb_ref[...] + cw_ref[0:1, :] * pltpu.roll(ext, 2, axis=0) + cw_ref[1:2, :] * pltpu.roll(ext, 1, axis=0)
              + cw_ref[2:3, :] * ext)[HALO:]
        ac_ref[...] = ac.astype(BF)
        hf_ref[...] = (ac * _sigmoid(ac) * up.astype(F32)).astype(BF)

    wspec = pl.BlockSpec((1, tc, d), lambda j, i: (j, 0, 0))
    ospec = pl.BlockSpec((tm, tc), lambda j, i: (i, j))
    o = _sds((s, FFN_PAD), BF)
    return _call(body, name, (o, o, o, o), (N_CHIPS, s // tm),
                 [pl.BlockSpec((tm, d), lambda j, i: (i, 0)), pl.BlockSpec((HALO, d), lambda j, i: (jnp.maximum(i * per - 1, 0), 0)),
                  wspec, wspec, pl.BlockSpec((3, tc), lambda j, i: (0, j)), pl.BlockSpec((1, tc), lambda j, i: (0, j))],
                 (ospec, ospec, ospec, ospec), ("parallel", "parallel"))(h2, h2, w_gate, w_up, cw, cb)


def _ffn_down(name, hf, w_down, x1, g2, tm=512):
    s, d = x1.shape
    tm = min(tm, s)

    def body(hf_ref, wd_ref, x1_ref, g2_ref, dn_ref, x2_ref):
        dn = _dot(hf_ref[...], wd_ref[...].reshape(FFN_PAD, d))
        dn_ref[...] = dn.astype(BF)
        x2_ref[...] = x1_ref[...] + g2_ref[...] * dn

    return _call(body, name, (_sds((s, d), BF), _sds((s, d), F32)), (s // tm,),
                 [_rows(tm, FFN_PAD), _resident(w_down.shape), _rows(tm, d), _vec(d)],
                 (_rows(tm, d), _rows(tm, d)), ("parallel",))(hf, w_down, x1, g2)


def _ffn_down_bwd_act(name, dx2, dn, g2, w_down, a, ac, up, cw, tm=256, after=None):
    s, d = dx2.shape
    c = a.shape[1]
    tm = min(tm, s)
    tc = FFN_SHARD_PAD
    per = tm // HALO
    steps = s // tm
    last = s // HALO - 1
    n = tm + HALO

    def body(dx_ref, dxnext_ref, dn_ref, g2_ref, wd_ref, a_ref, ac_ref, acnext_ref, up_ref, upnext_ref, cw_ref,
             ddn_ref, da_ref, dup_ref, dg_ref, dcw_ref, dcb_ref):
        i = pl.program_id(0)

        @pl.when(i == 0)
        def _():
            dg_ref[...] = jnp.zeros_like(dg_ref)
            dcw_ref[...] = jnp.zeros_like(dcw_ref)
            dcb_ref[...] = jnp.zeros_like(dcb_ref)

        dxv = dx_ref[...]
        ddn = (dxv * g2_ref[...]).astype(BF)
        ddn_ref[...] = ddn
        dg_ref[...] += jnp.sum(dxv * dn_ref[...].astype(F32), axis=0, keepdims=True)
        ddn_next = jnp.where(i < steps - 1, dxnext_ref[...] * g2_ref[...], 0.0).astype(BF)
        ddn_ext = jnp.concatenate([ddn, ddn_next], axis=0)
        for k in range(N_CHIPS):
            cols = slice(k * tc, (k + 1) * tc)
            dh = _dot_nt(ddn_ext, wd_ref[k])
            ace = jnp.concatenate([ac_ref[:, cols].astype(F32), acnext_ref[:, cols].astype(F32)], axis=0)
            upe = jnp.concatenate([up_ref[:, cols].astype(F32), upnext_ref[:, cols].astype(F32)], axis=0)
            sig = _sigmoid(ace)
            silu = ace * sig
            dac = dh * upe * (sig + silu * (1.0 - sig))
            dup_ref[:, cols] = (dh[:tm] * silu[:tm]).astype(BF)
            d1 = pltpu.roll(dac, n - 1, axis=0)[:tm]
            d2 = pltpu.roll(dac, n - 2, axis=0)[:tm]
            d0 = dac[:tm]
            da_ref[:, cols] = (cw_ref[2:3, cols] * d0 + cw_ref[1:2, cols] * d1 + cw_ref[0:1, cols] * d2).astype(BF)
            a0 = a_ref[:, cols].astype(F32)
            dcb_ref[:, cols] += jnp.sum(d0, axis=0, keepdims=True)
            dcw_ref[0:1, cols] += jnp.sum(d2 * a0, axis=0, keepdims=True)
            dcw_ref[1:2, cols] += jnp.sum(d1 * a0, axis=0, keepdims=True)
            dcw_ref[2:3, cols] += jnp.sum(d0 * a0, axis=0, keepdims=True)

    nxt = lambda width: pl.BlockSpec((HALO, width), lambda i: (jnp.minimum((i + 1) * per, last), 0))
    wide = _sds((s, c), BF)
    return _call(body, name, (_sds((s, d), BF), wide, wide, _sds((1, d), F32), _sds((3, c), F32), _sds((1, c), F32)), (steps,),
                 [_rows(tm, d), nxt(d), _rows(tm, d), _vec(d), _resident(w_down.shape), _rows(tm, c), _rows(tm, c), nxt(c),
                  _rows(tm, c), nxt(c), pl.BlockSpec((3, c), lambda i: (0, 0))],
                 (_rows(tm, d), _rows(tm, c), _rows(tm, c), _vec(d), pl.BlockSpec((3, c), lambda i: (0, 0)), _vec(c)),
                 ("arbitrary",), after=after)(dx2, dx2, dn, g2, w_down, a, ac, ac, up, up, cw)


def _ffn_up_bwd(name, da, dup, w_gate, w_up, x1, dx2, w2, sc2, o, g1, tm=256, after=None):
    s, d = x1.shape
    tm = min(tm, s)
    tc = FFN_SHARD_PAD

    def body(da_ref, dup_ref, wg_ref, wu_ref, x1_ref, dx2_ref, w2_ref, sc2_ref, o_ref, g1_ref,
             dx1_ref, do_ref, dnw_ref, dsh_ref, dg1_ref):
        @pl.when(pl.program_id(0) == 0)
        def _():
            dnw_ref[...] = jnp.zeros_like(dnw_ref)
            dsh_ref[...] = jnp.zeros_like(dsh_ref)
            dg1_ref[...] = jnp.zeros_like(dg1_ref)

        dh = jnp.zeros((tm, d), F32)
        for k in range(N_CHIPS):
            cols = slice(k * tc, (k + 1) * tc)
            dh = dh + _dot(da_ref[:, cols], wg_ref[k]) + _dot(dup_ref[:, cols], wu_ref[k])
        xv = x1_ref[...]
        r = _rms(xv)
        xn = xv * r
        dxn = dh * (w2_ref[...] * (1.0 + sc2_ref[...]))
        dx1 = dx2_ref[...] + r * (dxn - xn * jnp.mean(dxn * xn, axis=-1, keepdims=True))
        dx1_ref[...] = dx1
        dnw_ref[...] += jnp.sum(dh * xn, axis=0, keepdims=True)
        dsh_ref[...] += jnp.sum(dh, axis=0, keepdims=True)
        do_ref[...] = (dx1 * g1_ref[...]).astype(BF)
        dg1_ref[...] += jnp.sum(dx1 * o_ref[...].astype(F32), axis=0, keepdims=True)

    v = _sds((1, d), F32)
    r = _rows(tm, d)
    wspec = _resident(w_gate.shape)
    return _call(body, name, (_sds((s, d), F32), _sds((s, d), BF), v, v, v), (s // tm,),
                 [_rows(tm, FFN_PAD), _rows(tm, FFN_PAD), wspec, wspec, r, r, _vec(d), _vec(d), r, _vec(d)],
                 (r, r, _vec(d), _vec(d), _vec(d)), ("arbitrary",), after=after)(da, dup, w_gate, w_up, x1, dx2, w2, sc2, o, g1)


def _loss_head(name, x, w, target, tm=512):
    s, d = x.shape
    tm = min(tm, s)

    def body(x_ref, w_ref, t_ref, dx_ref, loss_ref, dw_ref):
        @pl.when(pl.program_id(0) == 0)
        def _():
            loss_ref[...] = jnp.zeros_like(loss_ref)
            dw_ref[...] = jnp.zeros_like(dw_ref)

        xv = x_ref[...]
        r = _rms(xv)
        xn = xv * r
        err = xn * w_ref[...] - t_ref[...]
        loss_ref[...] += 0.5 * jnp.sum(jnp.mean(err * err, axis=-1, keepdims=True))
        dy = err * (1.0 / d)
        dw_ref[...] += jnp.sum(dy * xn, axis=0, keepdims=True)
        dxn = dy * w_ref[...]
        dx_ref[...] = r * (dxn - xn * jnp.mean(dxn * xn, axis=-1, keepdims=True))

    return _call(body, name, (_sds((s, d), F32), _sds((1, LANES), F32), _sds((1, d), F32)), (s // tm,),
                 [_rows(tm, d), _vec(d), _rows(tm, d)], (_rows(tm, d), _vec(LANES), _vec(d)), ("arbitrary",))(x, w, target)


def _layer_fwd(l, x, mod, p, cosf, sinf, after=None, late=None, later=None):
    sh1, sc1, g1, sh2, sc2, g2 = mod
    tag = f"l{l}_"
    h, qr, kk0, kk1, vv0, vv1, u_pre, v_pre, ga_pre, gb_pre = _in_proj(
        tag + "in_proj", x, p["norm1_w"], sc1, sh1, p["w_in"], cosf, sinf, after=after)
    y_attn = _attention(tag + "attn", qr, kk0, kk1, vv0, vv1, p["sinks"])
    y_sgu = _sgu(tag + "sgu", u_pre, v_pre, p["sgu_ln_w"], p["sgu_ln_b"], p["sgu_w"], p["sgu_bfull"])
    if late is not None:
        p = dict(p, **late(y_sgu))
    merged, pa, pb, o, x1, h2 = _mix_out(tag + "mix_out", y_sgu, y_attn, ga_pre, gb_pre, x, g1, p["proj_a"], p["proj_b"],
                                         p["w_out"], p["norm2_w"], sc2, sh2)
    if later is not None:
        p = dict(p, **later(h2))
    a, ac, up, hf = _ffn_up_act(tag + "ffn_up", h2, p["w_gate"], p["w_up"], p["conv_w"], p["conv_b"])
    dn, x2 = _ffn_down(tag + "ffn_down", hf, p["w_down"], x1, g2)
    saved = dict(x=x, h=h, qr=qr, kk0=kk0, kk1=kk1, vv0=vv0, vv1=vv1, u_pre=u_pre, v_pre=v_pre, ga_pre=ga_pre,
                 gb_pre=gb_pre, y_attn=y_attn, y_sgu=y_sgu, merged=merged, pa=pa, pb=pb, o=o, x1=x1, h2=h2, a=a, ac=ac, up=up,
                 hf=hf, dn=dn)
    return x2, saved, p


def _layer_bwd(l, dx2, mod, p, sv, cosf, sinf, after=None, emit=None, tick=None):
    sh1, sc1, g1, sh2, sc2, g2 = mod
    tag = f"l{l}_b_"
    d = D_MODEL
    g = {}
    ready = (lambda names: emit({k: g.pop(k) for k in names})) if emit else (lambda names: None)
    tick = tick or (lambda y: None)
    ddn, da, dup, dg2, g["conv_w"], g["conv_b"] = _ffn_down_bwd_act(
        tag + "ffn_down", dx2, sv["dn"], g2, p["w_down"], sv["a"], sv["ac"], sv["up"], p["conv_w"], after=after)
    g["w_down"] = _matmul_tn(tag + "dw_down", sv["hf"], ddn, tk=FFN_SHARD_PAD, after=tick(ddn)).reshape(N_CHIPS, FFN_SHARD_PAD, d)
    g["w_gate"] = _matmul_tn(tag + "dw_gate", da, sv["h2"], tk=FFN_SHARD_PAD).reshape(N_CHIPS, FFN_SHARD_PAD, d)
    g["w_up"] = _matmul_tn(tag + "dw_up", dup, sv["h2"], tk=FFN_SHARD_PAD).reshape(N_CHIPS, FFN_SHARD_PAD, d)
    dx1, do, da2, dsh2, dg1 = _ffn_up_bwd(tag + "ffn_up", da, dup, p["w_gate"], p["w_up"], sv["x1"], dx2, p["norm2_w"],
                                          sc2, sv["o"], g1, after=ready(("w_down", "w_gate", "w_up")))
    g["norm2_w"] = da2 * (1.0 + sc2)
    dsc2 = da2 * p["norm2_w"]
    g["w_out"] = _matmul_tn(tag + "dw_out", sv["merged"], do, after=tick(do)).reshape(N_CHIPS, d // N_CHIPS, d)
    dpa, dpb, dga, dgb, dy_sgu, dy_attn = _mix_bwd(tag + "mix", do, p["w_out"], p["proj_a"], p["proj_b"], sv["ga_pre"],
                                                  sv["gb_pre"], sv["pa"], sv["pb"])
    g["proj_a"] = _matmul_tn(tag + "dproj_a", sv["y_sgu"], dpa).reshape(N_CHIPS, d // N_CHIPS, d)
    g["proj_b"] = _matmul_tn(tag + "dproj_b", sv["y_attn"], dpb).reshape(N_CHIPS, d // N_CHIPS, d)
    du, dv, g["sgu_w"], g["sgu_b"], g["sgu_ln_w"], g["sgu_ln_b"] = _sgu_bwd(
        tag + "sgu", sv["u_pre"], sv["v_pre"], p["sgu_ln_w"], p["sgu_ln_b"], p["sgu_w"], p["sgu_bfull"], dy_sgu,
        after=ready(("w_out", "proj_a", "proj_b")))
    dqr, dkv_cur, dkv_prev, dsink = _attention_bwd(tag + "attn", sv["qr"], sv["kk0"], sv["kk1"], sv["vv0"], sv["vv1"],
                                                   p["sinks"], dy_attn, after=tick(du))
    g["sinks"] = dsink[0, :N_Q_HEADS]
    dq, dkv = _rope_bwd(tag + "rope", dqr, dkv_cur, dkv_prev, cosf, sinf)
    dw_in, row0 = None, 0
    for n, t in (("q", dq), ("kv", dkv), ("u", du), ("v", dv), ("ga", dga), ("gb", dgb)):
        dw_in = _matmul_tn_rows(tag + "dw_in_" + n, t, sv["h"], dw_in, row0, IN_COLS)
        row0 += t.shape[1]
    g["w_in"] = dw_in.reshape(N_CHIPS, IN_COLS // N_CHIPS, d)
    dx, da1, dsh1 = _in_proj_bwd(tag + "in_proj", dq, dkv, du, dv, dga, dgb, p["w_in"], sv["x"], p["norm1_w"], sc1, dx1)
    g["norm1_w"] = da1 * (1.0 + sc1)
    dsc1 = da1 * p["norm1_w"]
    return dx, (dsh1, dsc1, dg1, dsh2, dsc2, dg2), g


def _pad_to(a, axis, size):
    pad = [(0, 0)] * a.ndim
    pad[axis] = (0, size - a.shape[axis])
    return jnp.pad(a, pad)


def _early_params(w_in, small):
    d = D_MODEL
    return dict(
        w_in=w_in.reshape(IN_COLS, d), norm1_w=small["norm1_w"].reshape(1, d), sinks=small["sinks"],
        sgu_ln_w=small["sgu_ln_w"].reshape(1, d), sgu_ln_b=small["sgu_ln_b"].reshape(1, d), sgu_w=small["sgu_w"],
        sgu_bfull=jnp.broadcast_to(small["sgu_b"][:, :, None], (SGU_GROUPS, SGU_CHUNK, SGU_CHUNK)))


def _mix_params(proj_a, proj_b, w_out, small):
    return dict(proj_a=proj_a, proj_b=proj_b, w_out=w_out, norm2_w=small["norm2_w"].reshape(1, D_MODEL))


def _ffn_params(w_gate, w_up, w_down, conv_w, small):
    return dict(
        w_gate=w_gate, w_up=w_up, w_down=w_down, conv_w=conv_w.transpose(1, 0, 2).reshape(3, FFN_PAD),
        conv_b=_pad_to(small["conv_b"].reshape(N_CHIPS, FFN_SHARD), 1, FFN_SHARD_PAD).reshape(1, FFN_PAD))


def _layer_params(w_in, proj_a, proj_b, w_out, w_gate, w_up, w_down, conv_w, small):
    return dict(_early_params(w_in, small), **_mix_params(proj_a, proj_b, w_out, small),
                **_ffn_params(w_gate, w_up, w_down, conv_w, small))


def _conv_grads_natural(g):
    cw = g["conv_w"].reshape(3, N_CHIPS, FFN_SHARD_PAD)[:, :, :FFN_SHARD].reshape(3, FFN_DIM)
    cb = g["conv_b"].reshape(N_CHIPS, FFN_SHARD_PAD)[:, :FFN_SHARD].reshape(FFN_DIM)
    return cw, cb


def _rope_tables(positions):
    inv_freq = ROPE_THETA ** (-jnp.arange(0, ROT_DIM, 2, dtype=F32) / ROT_DIM)
    ang = positions.astype(F32)[:, None] * inv_freq
    cos, sin = jnp.cos(ang), jnp.sin(ang)
    s = positions.shape[0]
    rest = HEAD_DIM - ROT_DIM
    cos_head = jnp.concatenate([cos, cos, jnp.ones((s, rest), F32)], axis=1)
    sin_head = jnp.concatenate([-sin, sin, jnp.zeros((s, rest), F32)], axis=1)
    return jnp.tile(cos_head, (1, LANES // HEAD_DIM)), jnp.tile(sin_head, (1, LANES // HEAD_DIM))


ADA_ROWS = 16


def _ada_fwd(name, c_rows, ada_w, ada_b_cols, tn=512):
    depth, d, n = ada_w.shape

    def body(c_ref, w_ref, b_ref, o_ref):
        cv = c_ref[...]
        act = (cv * _sigmoid(cv)).astype(BF)
        o_ref[0] = _dot(act, w_ref[0].astype(BF)) + b_ref[0]

    return _call(body, name, _sds((depth, ADA_ROWS, n), F32), (depth, n // tn),
                 [pl.BlockSpec((ADA_ROWS, d), lambda l, j: (0, 0)), pl.BlockSpec((1, d, tn), lambda l, j: (l, 0, j)),
                  pl.BlockSpec((1, 1, tn), lambda l, j: (l, 0, j))],
                 pl.BlockSpec((1, ADA_ROWS, tn), lambda l, j: (l, 0, j)), ("parallel", "parallel"))(c_rows, ada_w, ada_b_cols)


def _ada_bwd(name, c_rows, dmod_cols, tn=512):
    depth, _, n = dmod_cols.shape
    d = c_rows.shape[1]

    def body(c_ref, dm_ref, o_ref):
        cv = c_ref[...]
        act = (cv * _sigmoid(cv)).astype(BF)
        o_ref[0] = _dot_tn(act, dm_ref[0].astype(BF))

    return _call(body, name, _sds((depth, d, n), F32), (depth, n // tn),
                 [pl.BlockSpec((ADA_ROWS, d), lambda l, j: (0, 0)), pl.BlockSpec((1, ADA_ROWS, tn), lambda l, j: (l, 0, j))],
                 pl.BlockSpec((1, d, tn), lambda l, j: (l, 0, j)), ("parallel", "parallel"))(c_rows, dmod_cols)


def _colsum(name, a):
    r, n = a.shape

    def body(a_ref, o_ref):
        o_ref[...] = jnp.sum(a_ref[...], axis=0, keepdims=True)

    return _call(body, name, _sds((1, n), F32), (1,), [pl.BlockSpec((r, n), lambda i: (0, 0))],
                 pl.BlockSpec((1, n), lambda i: (0, 0)), ("arbitrary",))(a)


REL_SIBLING = (0, 0, 1)
REL_CHIPS = ((1, 0, 0), (0, 1, 0), (1, 1, 0))
REL_ALL = tuple((fx, fy, fc) for fx in (0, 1) for fy in (0, 1) for fc in (0, 1) if fx or fy or fc)


def _chip_of(dev):
    return 2 * dev[0] + dev[1]


def _dev_of(dev):
    return 4 * dev[0] + 2 * dev[1] + dev[2]


def _flip(dev, rel):
    return tuple(1 - m if f else m for m, f in zip(dev, rel))


def _exchange(name, arrays, n_out, stages, aliases=None):
    out_shapes, stages = stages[0], stages[1:]
    n_in = len(arrays)
    aliases = aliases or {}
    n_remote = sum(len(plan) for plan, _ in stages)
    n_local = sum(len(local) for _, local in stages)

    def at(ref, idx):
        return ref.at[idx] if len(idx) else ref

    def body(*refs):
        bufs = list(refs[:n_in + n_out])
        for i_in, i_out in aliases.items():
            bufs[i_in] = bufs[n_in + i_out]
        send_sems, recv_sems, local_sems = refs[n_in + n_out:]
        me = (lax.axis_index("x"), lax.axis_index("y"), lax.axis_index("c"))
        base_r = base_l = 0
        pending = []
        for plan, local in stages:
            def remote(k, entry, sender, receiver):
                rel, si, ssel, di, dsel = entry
                return pltpu.make_async_remote_copy(
                    src_ref=at(bufs[si], ssel(sender, receiver)), dst_ref=at(bufs[di], dsel(sender, receiver)),
                    send_sem=send_sems.at[k], recv_sem=recv_sems.at[k], device_id=_flip(me, rel), device_id_type=MESH)

            sends = [remote(base_r + k, e, me, _flip(me, e[0])) for k, e in enumerate(plan)]
            for cp in sends:
                cp.start()
            for k, (si, ssel, di, dsel) in enumerate(local):
                cp = pltpu.make_async_copy(at(bufs[si], ssel(me)), at(bufs[di], dsel(me)), local_sems.at[base_l + k])
                cp.start()
                pending.append(cp.wait)
            for k, e in enumerate(plan):
                remote(base_r + k, e, _flip(me, e[0]), me).wait_recv()
            pending += [cp.wait_send for cp in sends]
            base_r += len(plan)
            base_l += len(local)
        for wait in pending:
            wait()

    any_spec = pl.BlockSpec(memory_space=pl.ANY)
    return pl.pallas_call(
        body, name=name, out_shape=tuple(out_shapes), in_specs=[any_spec] * n_in, out_specs=tuple([any_spec] * n_out),
        input_output_aliases=dict(aliases),
        scratch_shapes=[pltpu.SemaphoreType.DMA((max(n_remote, 1),)), pltpu.SemaphoreType.DMA((max(n_remote, 1),)),
                        pltpu.SemaphoreType.DMA((max(n_local, 1),))])(*arrays)


HBM_SPEC = pl.BlockSpec(memory_space=pltpu.HBM)
SEM_SPEC = pl.BlockSpec(memory_space=pltpu.SEMAPHORE)


def _split_copies(bufs, plan, local, send_sems, recv_sems, local_sems):
    me = (lax.axis_index("x"), lax.axis_index("y"), lax.axis_index("c"))

    def at(ref, idx):
        return ref.at[idx] if len(idx) else ref

    def remote(k, sender, receiver):
        rel, si, ssel, di, dsel = plan[k]
        return pltpu.make_async_remote_copy(
            src_ref=at(bufs[si], ssel(sender, receiver)), dst_ref=at(bufs[di], dsel(sender, receiver)),
            send_sem=send_sems.at[k], recv_sem=recv_sems.at[k], device_id=_flip(me, rel), device_id_type=MESH)

    sends = [remote(k, me, _flip(me, plan[k][0])) for k in range(len(plan))]
    arrivals = [remote(k, _flip(me, plan[k][0]), me) for k in range(len(plan))]
    locs = [pltpu.make_async_copy(at(bufs[si], ssel(me)), at(bufs[di], dsel(me)), local_sems.at[k])
            for k, (si, ssel, di, dsel) in enumerate(local)]
    return sends, arrivals, locs


def _exchange_start(name, arrays, out_shapes, plan, local, inplace=False):
    n_in, n_out = len(arrays), len(out_shapes)
    n_buf = n_in + n_out

    def body(*refs):
        sems = refs[n_buf:n_buf + 3]
        bufs = refs[n_buf + 3:2 * n_buf + 3]
        sends, _, locs = _split_copies(bufs * 2 if inplace else bufs, plan, local, *sems)
        for cp in sends + locs:
            cp.start()
        refs[-1][...] = jnp.zeros_like(refs[-1])

    zones = [lax.empty(o.shape, o.dtype) for o in out_shapes]
    operands = [pltpu.with_memory_space_constraint(a, pltpu.HBM) for a in list(arrays) + zones]
    sem = lambda n: pltpu.SemaphoreType.DMA((max(n, 1),))
    out = pl.pallas_call(
        body, name=name,
        out_shape=(sem(len(plan)), sem(len(plan)), sem(len(local)), *[pltpu.HBM(a.shape, a.dtype) for a in operands],
                   _sds((SUBLANES, LANES), F32)),
        in_specs=[HBM_SPEC] * n_buf,
        out_specs=(SEM_SPEC, SEM_SPEC, SEM_SPEC, *[HBM_SPEC] * n_buf, pl.BlockSpec(memory_space=pltpu.VMEM)),
        input_output_aliases={i: 3 + i for i in range(n_buf)},
        compiler_params=pltpu.CompilerParams(has_side_effects=pltpu.SideEffectType.DATAFLOW_SIDE_EFFECTING))(*operands)
    pending = dict(name=name, sems=out[:3], thru=out[3:3 + n_in], zones=out[3 + n_in:3 + n_buf], plan=plan, local=local,
                   inplace=inplace)
    return pending, out[-1]


def _exchange_wait(pending, after):
    thru, zones, plan, local, inplace = (pending[k] for k in ("thru", "zones", "plan", "local", "inplace"))
    n_in, n_buf = len(thru), len(thru) + len(zones)

    def body(*refs):
        bufs = refs[:n_buf]
        sends, arrivals, locs = _split_copies(bufs * 2 if inplace else bufs, plan, local, *refs[n_buf:n_buf + 3])
        for cp in arrivals:
            cp.wait_recv()
        for cp in sends:
            cp.wait_send()
        for cp in locs:
            cp.wait()

    out = pl.pallas_call(
        body, name=pending["name"] + "_wait", out_shape=tuple(pltpu.HBM(a.shape, a.dtype) for a in list(thru) + list(zones)),
        in_specs=[HBM_SPEC] * n_buf + [SEM_SPEC] * 3 + [pl.BlockSpec(memory_space=pl.ANY)],
        out_specs=tuple([HBM_SPEC] * n_buf), input_output_aliases={i: i for i in range(n_buf)},
        compiler_params=pltpu.CompilerParams(has_side_effects=pltpu.SideEffectType.DATAFLOW_SIDE_EFFECTING))(
            *thru, *zones, *pending["sems"], after)
    return out[:n_in] if inplace else out[n_in:]


def _whole(*_):
    return ()


def _half_rows(rows, core):
    return pl.ds(core * (rows // 2), rows // 2)


def _gather_weights_plan(shards):
    n = len(shards)
    dsts = [_sds((N_CHIPS,) + a.shape, a.dtype) for a in shards]
    fetch, forward = [], []
    for t, a in enumerate(shards):
        rows = a.shape[0]
        if rows % (2 * 16) == 0:
            fetch += [(rel, t, (lambda s_, r_, rows=rows: (_half_rows(rows, s_[2]),)), n + t,
                       (lambda s_, r_, rows=rows: (_chip_of(s_), _half_rows(rows, s_[2])))) for rel in REL_CHIPS]
            forward += [(REL_SIBLING, n + t, (lambda s_, r_, rows=rows, rel=rel: (_chip_of(_flip(s_, rel)), _half_rows(rows, s_[2]))),
                         n + t, (lambda s_, r_, rows=rows, rel=rel: (_chip_of(_flip(s_, rel)), _half_rows(rows, s_[2]))))
                        for rel in REL_CHIPS]
        else:
            fetch += [(rel, t, _whole, n + t, lambda s_, r_: (_chip_of(s_),)) for rel in REL_CHIPS]
    local = [(t, _whole, n + t, lambda me: (_chip_of(me),)) for t in range(n)]
    return dsts, fetch, local, forward


def _gather_weights_start(name, shards):
    dsts, fetch, local, forward = _gather_weights_plan(shards)
    pending, token = _exchange_start(name, shards, dsts, fetch, local)
    return dict(pending, forward=forward), token


def _gather_weights_finish(pending, after):
    landed = _exchange_wait(pending, after)
    n = len(landed)
    return _exchange(pending["name"] + "_forward", landed, n, [[_sds(a.shape, a.dtype) for a in landed], (pending["forward"], [])],
                     aliases={t: t for t in range(n)})


def _gather_chips(name, arrays):
    n = len(arrays)
    dsts = [_sds((N_CHIPS,) + a.shape, a.dtype) for a in arrays]
    plan = [(rel, t, _whole, n + t, lambda s_, r_: (_chip_of(s_),)) for t in range(n) for rel in REL_CHIPS]
    local = [(t, _whole, n + t, lambda me: (_chip_of(me),)) for t in range(n)]
    return _exchange(name, arrays, n, [dsts, (plan, local)])


def _gather_all(name, a):
    plan = [(rel, 0, _whole, 1, lambda s_, r_: (_dev_of(s_),)) for rel in REL_ALL]
    local = [(0, _whole, 1, lambda me: (_dev_of(me),))]
    return _exchange(name, [a], 1, [[_sds((2 * N_CHIPS,) + a.shape, a.dtype)], (plan, local)])[0]


def _swap_halves_start(name, grads):
    n = len(grads)
    dsts = [_sds((g.shape[0], g.shape[1] // 2, g.shape[2]), g.dtype) for g in grads]
    plan = [(REL_SIBLING, t, (lambda s_, r_, rows=g.shape[1]: (pl.ds(0, N_CHIPS), _half_rows(rows, r_[2]))), n + t, _whole)
            for t, g in enumerate(grads)]
    return _exchange_start(name, grads, dsts, plan, [])


def _scatter_chips_plan(sums):
    n = len(sums)
    dsts = [_sds(a.shape, a.dtype) for a in sums]
    plan = [(rel, t, lambda s_, r_: (_chip_of(r_),), n + t, lambda s_, r_: (_chip_of(s_),))
            for t in range(n) for rel in REL_CHIPS]
    local = [(t, lambda me: (_chip_of(me),), n + t, lambda me: (_chip_of(me),)) for t in range(n)]
    return dsts, plan, local


def _scatter_chips(name, sums):
    dsts, plan, local = _scatter_chips_plan(sums)
    return _exchange(name, sums, len(sums), [dsts, (plan, local)])


def _scatter_chips_start(name, sums):
    dsts, plan, local = _scatter_chips_plan(sums)
    return _exchange_start(name, sums, dsts, plan, local)


def _swap_back_start(name, totals, layer):
    n = len(totals)
    plan = [(REL_SIBLING, n + t, (lambda s_, r_, rows=a.shape[1]: (layer, _half_rows(rows, s_[2]))),
             n + t, (lambda s_, r_, rows=a.shape[1]: (layer, _half_rows(rows, s_[2])))) for t, a in enumerate(totals)]
    return _exchange_start(name, totals, [], plan, [], inplace=True)


def _add_halves(name, g, recv, core):
    nch, half, c = recv.shape

    def body(core_ref, g_ref, r_ref, o_ref):
        o_ref[0] = (g_ref[0, 0].astype(F32) + r_ref[0].astype(F32)).astype(o_ref.dtype)

    spec = pltpu.PrefetchScalarGridSpec(
        num_scalar_prefetch=1, grid=(nch,),
        in_specs=[pl.BlockSpec((1, 1, half, c), lambda k, core_ref: (k, core_ref[0], 0, 0)),
                  pl.BlockSpec((1, half, c), lambda k, core_ref: (k, 0, 0))],
        out_specs=pl.BlockSpec((1, half, c), lambda k, core_ref: (k, 0, 0)))
    return pl.pallas_call(body, name=name, out_shape=_sds(recv.shape, recv.dtype), grid_spec=spec,
                          compiler_params=pltpu.CompilerParams(dimension_semantics=("parallel",),
                                                               vmem_limit_bytes=VMEM_LIMIT))(
                                                                   core, g.reshape(nch, 2, half, c), recv)


def _sum_chips(name, a, core, layer, total):
    nch, half, c = a.shape

    def body(core_ref, a_ref, *rest):
        o_ref = rest[-1]
        acc = a_ref[0].astype(F32)
        for k in range(1, nch):
            acc = acc + a_ref[k].astype(F32)
        o_ref[0, 0] = acc

    in_specs = [pl.BlockSpec((nch, half, c), lambda i, core_ref: (0, 0, 0))]
    args = [core, a]
    if total is not None:
        in_specs.append(pl.BlockSpec(memory_space=pl.ANY))
        args.append(total.reshape(DEPTH, 2, half, c))
    spec = pltpu.PrefetchScalarGridSpec(
        num_scalar_prefetch=1, grid=(1,), in_specs=in_specs,
        out_specs=pl.BlockSpec((1, 1, half, c), lambda i, core_ref: (layer, core_ref[0], 0, 0)))
    out = pl.pallas_call(body, name=name, out_shape=_sds((DEPTH, 2, half, c), F32), grid_spec=spec,
                         input_output_aliases={2: 0} if total is not None else {},
                         compiler_params=pltpu.CompilerParams(dimension_semantics=("arbitrary",),
                                                              vmem_limit_bytes=VMEM_LIMIT))(*args)
    return out.reshape(DEPTH, 2 * half, c)


def _adamw_update(w, g, m, v):
    mn = ADAM_B1 * m + (1.0 - ADAM_B1) * g
    vn = ADAM_B2 * v + (1.0 - ADAM_B2) * (g * g)
    m_hat = mn / (1.0 - ADAM_B1 ** ADAM_STEP)
    v_hat = vn / (1.0 - ADAM_B2 ** ADAM_STEP)
    return -ADAM_LR * (m_hat / (jnp.sqrt(v_hat) + ADAM_EPS) + ADAM_WD * w), mn, vn


def _adamw(name, w, g, m, v):
    depth, r, c = w.shape
    tr = next(t for t in (512, 448, 384, 352, 336, 256, 192, 128, 64, 32, 16, 8) if r % t == 0 and t * c <= ADAM_TILE_ELEMS)

    def body(w_ref, g_ref, m_ref, v_ref, go_ref, d_ref, mo_ref, vo_ref):
        gv = g_ref[...]
        go_ref[...] = gv
        d_ref[...], mo_ref[...], vo_ref[...] = _adamw_update(w_ref[...], gv, m_ref[...], v_ref[...])

    spec = pl.BlockSpec((1, tr, c), lambda l, i: (l, i, 0))
    o = _sds(w.shape, F32)
    return _call(body, name, (o, o, o, o), (depth, r // tr), [spec] * 4, (spec,) * 4, ("parallel", "parallel"))(w, g, m, v)


def _adamw_small(name, ws, gs, ms, vs):
    n = len(ws)

    def body(*refs):
        for t in range(n):
            w_ref, g_ref, m_ref, v_ref = (refs[k * n + t] for k in range(4))
            d_ref, mo_ref, vo_ref = (refs[(4 + k) * n + t] for k in range(3))
            d_ref[...], mo_ref[...], vo_ref[...] = _adamw_update(w_ref[...], g_ref[...], m_ref[...], v_ref[...])

    outs = [_sds(w.shape, F32) for w in ws]
    res = pl.pallas_call(body, name=name, out_shape=tuple(outs * 3))(*ws, *gs, *ms, *vs)
    return res[:n], res[n:2 * n], res[2 * n:]


def _pack(arrays, rows):
    flat = jnp.concatenate([a.reshape(-1).astype(F32) for a in arrays])
    return _pad_to(flat, 0, rows * LANES).reshape(rows, LANES)


def _unpack(packed, shapes):
    flat = packed.reshape(-1)
    out, off = [], 0
    for shp in shapes:
        n = 1
        for s_ in shp:
            n *= s_
        out.append(flat[off:off + n].reshape(shp))
        off += n
    return out


_MATRICES = ("w_in", "proj_a", "proj_b", "w_out", "w_gate", "w_up", "w_down")
_SMALL = (("norm1_w", (D_MODEL,)), ("sinks", (N_Q_HEADS,)), ("sgu_ln_w", (SGU_WIDTH,)), ("sgu_ln_b", (SGU_WIDTH,)),
          ("sgu_w", (SGU_GROUPS, SGU_CHUNK, SGU_CHUNK)), ("sgu_b", (SGU_GROUPS, SGU_CHUNK)), ("norm2_w", (D_MODEL,)),
          ("conv_w", (3, FFN_DIM)), ("conv_b", (FFN_DIM,)), ("final_norm_w", (D_MODEL,)))
SMALL_ROWS = 320
ADAM_TILE_ELEMS = 384 * 1024


def _reduce_cores_start(tag, partial):
    names = list(partial)
    pending, token = _swap_halves_start(tag + "_cores", [partial[k] for k in names])
    return dict(pending, tag=tag, names=names, partial=partial), token


def _reduce_chips_start(pending, core, after):
    tag, names, partial = pending["tag"], pending["names"], pending["partial"]
    sums = [_add_halves(f"{tag}_cores_add_{k}", partial[k], r, core) for k, r in zip(names, _exchange_wait(pending, after))]
    scatter, token = _scatter_chips_start(tag + "_chips", sums)
    return dict(scatter, tag=tag, names=names), token


def _reduce_back_start(pending, l, core, totals, after):
    tag, names = pending["tag"], pending["names"]
    sums = [_sum_chips(f"{tag}_chips_add_{k}", a, core, l, totals.get(k)) for k, a in zip(names, _exchange_wait(pending, after))]
    back, _ = _swap_back_start(tag + "_back", sums, l)
    return dict(back, names=names)


def kernel(x, c, positions, ada_w, ada_b, norm1_w, w_in, attn_sinks, sgu_ln_w, sgu_ln_b, sgu_w, sgu_b, proj_a, proj_b, w_out, norm2_w, ffn_w_gate, ffn_w_up, ffn_conv_w, ffn_conv_b, ffn_w_down, final_norm_w, loss_target, m_ada_w, m_ada_b, m_norm1_w, m_w_in, m_attn_sinks, m_sgu_ln_w, m_sgu_ln_b, m_sgu_w, m_sgu_b, m_proj_a, m_proj_b, m_w_out, m_norm2_w, m_ffn_w_gate, m_ffn_w_up, m_ffn_conv_w, m_ffn_conv_b, m_ffn_w_down, m_final_norm_w, v_ada_w, v_ada_b, v_norm1_w, v_w_in, v_attn_sinks, v_sgu_ln_w, v_sgu_ln_b, v_sgu_w, v_sgu_b, v_proj_a, v_proj_b, v_w_out, v_norm2_w, v_ffn_w_gate, v_ffn_w_up, v_ffn_conv_w, v_ffn_conv_b, v_ffn_w_down, v_final_norm_w):
    d = D_MODEL
    ax, ay, ac = lax.axis_index("x"), lax.axis_index("y"), lax.axis_index("c")
    chip = 2 * ax + ay
    dev = 4 * ax + 2 * ay + ac
    core = ac.astype(jnp.int32).reshape(1)

    c_all = _gather_all("gather_cond", c.reshape(SUBLANES, d // SUBLANES)).reshape(2 * N_CHIPS, d)
    c_rows = _pad_to(c_all, 0, ADA_ROWS)
    ada_cols = ada_w.shape[2]
    ada_b_cols = lax.dynamic_slice_in_dim(ada_b, chip * ada_cols, ada_cols, axis=1).reshape(DEPTH, 1, ada_cols)
    mod_cols = _ada_fwd("ada_fwd", c_rows, ada_w, ada_b_cols)
    mod_all = _gather_chips("gather_mod", [mod_cols])[0]
    mod_mine = lax.dynamic_index_in_dim(mod_all, dev, axis=2, keepdims=False)
    mod_mine = mod_mine.transpose(1, 0, 2).reshape(DEPTH, 1, 6 * d)
    mods = [tuple(jnp.split(mod_mine[l], 6, axis=-1)) for l in range(DEPTH)]

    tr = lambda a: jnp.swapaxes(a, 1, 2)
    shards = [tr(w_in).astype(BF), proj_a.astype(BF), proj_b.astype(BF), w_out.astype(BF),
              _pad_to(tr(ffn_w_gate).astype(BF), 1, FFN_SHARD_PAD), _pad_to(tr(ffn_w_up).astype(BF), 1, FFN_SHARD_PAD),
              _pad_to(ffn_w_down.astype(BF), 1, FFN_SHARD_PAD), _pad_to(ffn_conv_w, 2, FFN_SHARD_PAD)]
    token = mod_all[0, 0, :SUBLANES, :LANES]
    fetches = []
    for l in range(DEPTH):
        groups = []
        for tag, members in (("in", shards[:1]), ("mix", shards[1:4]), ("ffn", shards[4:])):
            behind = (token[0, 0] * 0.0).astype(members[0].dtype)
            pending, token = _gather_weights_start(f"l{l}_gather_{tag}", [members[0][l] + behind] + [a[l] for a in members[1:]])
            groups.append(pending)
        fetches.append(groups)

    small_in = dict(norm1_w=norm1_w, sinks=attn_sinks, sgu_ln_w=sgu_ln_w, sgu_ln_b=sgu_ln_b, sgu_w=sgu_w, sgu_b=sgu_b,
                    norm2_w=norm2_w, conv_b=ffn_conv_b)
    cosf, sinf = _rope_tables(positions[0])
    small_of = lambda l: {k: v[l] for k, v in small_in.items()}

    h = x[0]
    saved, params = [], []
    for l in range(DEPTH):
        first, mix, ffn = fetches[l]
        w_in_l = _gather_weights_finish(first, token if l == 0 else h)
        late = lambda y, l=l, mix=mix: _mix_params(*_gather_weights_finish(mix, y), small_of(l))
        later = lambda y, l=l, ffn=ffn: _ffn_params(*_gather_weights_finish(ffn, y), small_of(l))
        h, sv, p = _layer_fwd(l, h, mods[l], _early_params(w_in_l[0], small_of(l)), cosf, sinf, late=late, later=later)
        saved.append(sv)
        params.append(p)
    dx, loss_part, d_final = _loss_head("loss_head", h, final_norm_w.reshape(1, d), loss_target[0])
    loss = lax.psum(loss_part[0, 0], ("x", "y", "c"))

    def small_pack(l, grads):
        cw, cb = _conv_grads_natural(grads)
        nat = dict(grads, conv_w=cw, conv_b=cb, final_norm_w=d_final if l == DEPTH - 1 else jnp.zeros((d,), F32))
        return _pack([nat[k] for k, _ in _SMALL], N_CHIPS * SMALL_ROWS).reshape(N_CHIPS, SMALL_ROWS, LANES)

    waiting, inflight = [], []

    def send(tag, partial):
        pending, token = _reduce_cores_start(tag, partial)
        waiting.append(pending)
        return token

    def tick(y):
        token = None
        while waiting:
            pending, token = _reduce_chips_start(waiting.pop(0), core, y)
            inflight.append(pending)
        return token

    dmods = [None] * DEPTH
    dx, dmods[1], grads = _layer_bwd(1, dx, mods[1], params[1], saved[1], cosf, sinf)
    token = send("l1_reduce", dict({k: grads[k] for k in _MATRICES}, small=small_pack(1, grads)))
    dx, dmods[0], grads = _layer_bwd(0, dx, mods[0], params[0], saved[0], cosf, sinf, after=token,
                                     emit=lambda part: send("l0_reduce_" + "_".join(part), part), tick=tick)
    dmod_mine = jnp.concatenate([jnp.concatenate(dmods[l], axis=1) for l in range(DEPTH)], axis=1)
    dmod_all = _gather_all("gather_dmod", dmod_mine.reshape(SUBLANES, -1)).reshape(2 * N_CHIPS, DEPTH * 6 * d)
    send("l0_reduce_in", dict(w_in=grads["w_in"], small=small_pack(0, grads) + dmod_all[0, 0] * 0.0))
    tick(dmod_all)

    totals, flying = {}, None

    def land(after):
        if flying is not None:
            totals.update(zip(flying["names"], _exchange_wait(flying, after)))

    for pending in inflight[:-1]:
        land(dx)
        flying = _reduce_back_start(pending, 1 if pending["tag"].startswith("l1") else 0, core, totals, dx)
    land(dx)
    flying = None

    g_ada_b = _colsum("ada_b_grad", dmod_all).reshape(DEPTH, 6 * d)
    dmod_cols = jnp.stack([lax.dynamic_slice_in_dim(dmod_all, l * 6 * d + chip * ada_cols, ada_cols, axis=1)
                           for l in range(DEPTH)])
    g_ada_w = _ada_bwd("ada_w_grad", c_rows, _pad_to(dmod_cols, 1, ADA_ROWS))
    big = dict(w_in=(tr(w_in), tr(m_w_in), tr(v_w_in)), proj_a=(proj_a, m_proj_a, v_proj_a), proj_b=(proj_b, m_proj_b, v_proj_b),
               w_out=(w_out, m_w_out, v_w_out), w_gate=(tr(ffn_w_gate), tr(m_ffn_w_gate), tr(v_ffn_w_gate)),
               w_up=(tr(ffn_w_up), tr(m_ffn_w_up), tr(v_ffn_w_up)), w_down=(ffn_w_down, m_ffn_w_down, v_ffn_w_down))
    upd, g_big = {}, {}

    def update(k):
        res = _adamw("adamw_" + k, big[k][0], totals[k], *big[k][1:])
        res = [tr(a) for a in res] if k in ("w_in", "w_gate", "w_up") else res
        g_big[k], upd[k] = res[0], res[1:]

    for k in ("w_down", "w_gate", "w_up", "w_out", "proj_a", "proj_b"):
        update(k)
    g_big["ada_w"], *upd["ada_w"] = _adamw("adamw_ada_w", ada_w, g_ada_w, m_ada_w, v_ada_w)
    flying = _reduce_back_start(inflight[-1], 0, core, totals, upd["ada_w"][0])
    land(upd["ada_w"][1])
    update("w_in")

    small_all = _gather_chips("gather_small", [totals["small"]])[0]
    small_g = small_all.transpose(1, 0, 2, 3).reshape(DEPTH, -1)
    per_layer = [_unpack(small_g[l], [shp for _, shp in _SMALL]) for l in range(DEPTH)]
    sg = {k: jnp.stack([per_layer[l][i] for l in range(DEPTH)]) for i, (k, _) in enumerate(_SMALL)}
    g_final = sg["final_norm_w"][DEPTH - 1]
    g_conv_w = lax.dynamic_slice_in_dim(sg["conv_w"], chip * FFN_SHARD, FFN_SHARD, axis=2)

    rest = [("ada_b", ada_b, g_ada_b, m_ada_b, v_ada_b), ("norm1_w", norm1_w, sg["norm1_w"], m_norm1_w, v_norm1_w),
            ("attn_sinks", attn_sinks, sg["sinks"], m_attn_sinks, v_attn_sinks),
            ("sgu_ln_w", sgu_ln_w, sg["sgu_ln_w"], m_sgu_ln_w, v_sgu_ln_w),
            ("sgu_ln_b", sgu_ln_b, sg["sgu_ln_b"], m_sgu_ln_b, v_sgu_ln_b), ("sgu_w", sgu_w, sg["sgu_w"], m_sgu_w, v_sgu_w),
            ("sgu_b", sgu_b, sg["sgu_b"], m_sgu_b, v_sgu_b), ("norm2_w", norm2_w, sg["norm2_w"], m_norm2_w, v_norm2_w),
            ("ffn_conv_w", ffn_conv_w, g_conv_w, m_ffn_conv_w, v_ffn_conv_w),
            ("ffn_conv_b", ffn_conv_b, sg["conv_b"], m_ffn_conv_b, v_ffn_conv_b),
            ("final_norm_w", final_norm_w.reshape(1, d), g_final.reshape(1, d), m_final_norm_w.reshape(1, d),
             v_final_norm_w.reshape(1, d))]
    rest_out = _adamw_small("adamw_rest", *[[r[i] for r in rest] for i in (1, 2, 3, 4)])
    g_rest = {r[0]: r[2] for r in rest}
    u_rest = {r[0]: tuple(o[i] for o in rest_out) for i, r in enumerate(rest)}
    g_rest["final_norm_w"] = g_final
    u_rest["final_norm_w"] = tuple(a.reshape(d) for a in u_rest["final_norm_w"])

    names = ("ada_w", "ada_b", "norm1_w", "w_in", "attn_sinks", "sgu_ln_w", "sgu_ln_b", "sgu_w", "sgu_b", "proj_a", "proj_b",
             "w_out", "norm2_w", "ffn_w_gate", "ffn_w_up", "ffn_conv_w", "ffn_conv_b", "ffn_w_down", "final_norm_w")
    alias = {"ffn_w_gate": "w_gate", "ffn_w_up": "w_up", "ffn_w_down": "w_down"}
    grad_of = lambda n: g_rest[n] if n in g_rest else g_big[alias.get(n, n)]
    upd_of = lambda n: u_rest[n] if n in u_rest else upd[alias.get(n, n)]
    return (loss, dx[None], *[grad_of(n) for n in names], *[upd_of(n)[0] for n in names],
            *[upd_of(n)[1] for n in names], *[upd_of(n)[2] for n in names])
```

```python
import jax
import jax.numpy as jnp
from jax import lax
from jax.experimental import pallas as pl
from jax.experimental.pallas import tpu as pltpu

F32 = jnp.float32
BF = jnp.bfloat16

D_MODEL = 1024
N_Q_HEADS = 16
N_KV_HEADS = 2
HEAD_DIM = 64
ATTN_BLOCK = 128
ROPE_THETA = 500000.0
ROT_DIM = HEAD_DIM // 4
SGU_WIDTH = 1024
SGU_GROUPS = 8
SGU_CHUNK = 128
FFN_DIM = 2816
NORM_EPS = 1e-6
DEPTH = 2
IN_COLS = 5376
N_CHIPS = 4
FFN_SHARD = FFN_DIM // N_CHIPS
FFN_SHARD_PAD = 768
FFN_PAD = N_CHIPS * FFN_SHARD_PAD
LANES = 128
SUBLANES = 8
HALO = 16
VMEM_LIMIT = 56 * 1024 * 1024
NEG_BIG = -1e30

ADAM_LR = 0.001
ADAM_B1 = 0.9
ADAM_B2 = 0.999
ADAM_EPS = 1e-08
ADAM_WD = 0.01
ADAM_STEP = 10

MESH = pl.DeviceIdType.MESH

Q_END = 1024
KV_END = 1280
U_END = 2304
Z_END = 3328
GA_END = 4352


def _sds(shape, dtype):
    return jax.ShapeDtypeStruct(tuple(shape), dtype)


def _call(body, name, out_shape, grid, in_specs, out_specs, semantics, scratch=(), after=None):
    n_in = len(in_specs)
    fn = body
    if after is not None:
        def fn(*refs):
            return body(*refs[:n_in], *refs[n_in + 1:])

        in_specs = list(in_specs) + [pl.BlockSpec(memory_space=pl.ANY)]
    call = pl.pallas_call(
        fn, name=name, out_shape=out_shape, grid=grid, in_specs=in_specs, out_specs=out_specs,
        scratch_shapes=scratch,
        compiler_params=pltpu.CompilerParams(dimension_semantics=semantics, vmem_limit_bytes=VMEM_LIMIT))
    if after is None:
        return call
    return lambda *args: call(*args, after)


def _rows(tm, width, col=0):
    return pl.BlockSpec((tm, width), lambda i: (i, col))


def _vec(width):
    return pl.BlockSpec((1, width), lambda i: (0, 0))


def _resident(shape):
    zeros = (0,) * len(shape)
    return pl.BlockSpec(tuple(shape), lambda *_: zeros, pipeline_mode=pl.Buffered(1))


def _sigmoid(x):
    return 0.5 + 0.5 * jnp.tanh(0.5 * x)


def _gelu(x):
    return 0.5 * x * (1.0 + lax.erf(x * 0.7071067811865476))


def _gelu_grad(x):
    cdf = 0.5 * (1.0 + lax.erf(x * 0.7071067811865476))
    return cdf + x * jnp.exp(-0.5 * x * x) * 0.3989422804014327


def _dot(a, b):
    return jnp.dot(a, b, preferred_element_type=F32)


def _dot_nt(a, b):
    return lax.dot_general(a, b, (((1,), (1,)), ((), ())), preferred_element_type=F32)


def _dot_tn(a, b):
    return lax.dot_general(a, b, (((0,), (0,)), ((), ())), preferred_element_type=F32)


def _rms(xv):
    return lax.rsqrt(jnp.mean(xv * xv, axis=-1, keepdims=True) + NORM_EPS)


def _matmul_tn(name, a, b, tk=512, tn=1024, blocked=False, after=None):
    s, k = a.shape
    n = b.shape[1]
    tk, tn = min(tk, k), min(tn, n)

    def body(a_ref, b_ref, o_ref):
        res = _dot_tn(a_ref[...], b_ref[...]).astype(o_ref.dtype)
        if blocked:
            o_ref[0] = res
        else:
            o_ref[...] = res

    if blocked:
        out, ospec = _sds((n // tn, k, tn), BF), pl.BlockSpec((1, tk, tn), lambda i, j: (j, i, 0))
    else:
        out, ospec = _sds((k, n), BF), pl.BlockSpec((tk, tn), lambda i, j: (i, j))
    return _call(body, name, out, (k // tk, n // tn),
                 [pl.BlockSpec((s, tk), lambda i, j: (0, i)), pl.BlockSpec((s, tn), lambda i, j: (0, j))],
                 ospec, ("parallel", "parallel"), after=after)(a, b)


def _matmul_tn_rows(name, a, b, out, row0, rows_total, tk=256):
    s, k = a.shape
    n = b.shape[1]

    def body(a_ref, b_ref, *rest):
        rest[-1][...] = _dot_tn(a_ref[...], b_ref[...]).astype(BF)

    in_specs = [pl.BlockSpec((s, tk), lambda i: (0, i)), _resident(b.shape)]
    args = [a, b]
    if out is not None:
        in_specs.append(pl.BlockSpec(memory_space=pl.ANY))
        args.append(out)
    return pl.pallas_call(
        body, name=name, out_shape=_sds((rows_total, n), BF), grid=(k // tk,), in_specs=in_specs,
        out_specs=pl.BlockSpec((tk, n), lambda i: (row0 // tk + i, 0)),
        input_output_aliases={2: 0} if out is not None else {},
        compiler_params=pltpu.CompilerParams(dimension_semantics=("parallel",), vmem_limit_bytes=VMEM_LIMIT))(*args)


def _rope_partner(v):
    lane = lax.broadcasted_iota(jnp.int32, (1, LANES), 1) % HEAD_DIM
    return jnp.where(lane < ROT_DIM // 2, pltpu.roll(v, LANES - ROT_DIM // 2, axis=1), pltpu.roll(v, ROT_DIM // 2, axis=1))


def _dup_half(v, half):
    lane = lax.broadcasted_iota(jnp.int32, (1, LANES), 1)
    keep = jnp.where((lane >= HEAD_DIM) == (half == 1), v, 0.0)
    return keep + pltpu.roll(keep, HEAD_DIM, axis=1)


def _in_proj(name, x, w, sc, sh, w_in, cosf, sinf, tm=256, after=None):
    s, d = x.shape
    tm = min(tm, s)

    def body(x_ref, w_ref, sc_ref, sh_ref, win_ref, cos_ref, sin_ref,
             h_ref, qr_ref, kk0_ref, kk1_ref, vv0_ref, vv1_ref, u_ref, v_ref, ga_ref, gb_ref):
        xv = x_ref[...]
        h = ((xv * _rms(xv)) * w_ref[...] * (1.0 + sc_ref[...]) + sh_ref[...]).astype(BF)
        h_ref[...] = h
        cosv, sinv = cos_ref[...], sin_ref[...]
        q = _dot_nt(h, win_ref[:Q_END, :])
        for j in range(D_MODEL // LANES):
            qv = q[:, j * LANES:(j + 1) * LANES]
            qr_ref[:, j * LANES:(j + 1) * LANES] = (qv * cosv + _rope_partner(qv) * sinv).astype(BF)
        kv = _dot_nt(h, win_ref[Q_END:KV_END, :])
        kr = kv[:, :LANES] * cosv + _rope_partner(kv[:, :LANES]) * sinv
        vv = kv[:, LANES:]
        kk0_ref[...] = _dup_half(kr, 0).astype(BF)
        kk1_ref[...] = _dup_half(kr, 1).astype(BF)
        vv0_ref[...] = _dup_half(vv, 0).astype(BF)
        vv1_ref[...] = _dup_half(vv, 1).astype(BF)
        u_ref[...] = _dot_nt(h, win_ref[KV_END:U_END, :])
        v_ref[...] = _dot_nt(h, win_ref[U_END:Z_END, :])
        ga_ref[...] = _dot_nt(h, win_ref[Z_END:GA_END, :]).astype(BF)
        gb_ref[...] = _dot_nt(h, win_ref[GA_END:, :]).astype(BF)

    wide, kvs, pre = _sds((s, d), BF), _sds((s, LANES), BF), _sds((s, d), F32)
    return _call(body, name, (wide, wide, kvs, kvs, kvs, kvs, pre, pre, wide, wide), (s // tm,),
                 [_rows(tm, d), _vec(d), _vec(d), _vec(d), _resident(w_in.shape), _rows(tm, LANES), _rows(tm, LANES)],
                 (_rows(tm, d), _rows(tm, d)) + (_rows(tm, LANES),) * 4 + (_rows(tm, d),) * 4, ("parallel",), after=after)(
                     x, w, sc, sh, w_in, cosf, sinf)


def _in_proj_bwd(name, dq, dkv, du, dv, dga, dgb, w_in, x, w, sc, dx_in, tm=256):
    s, d = x.shape
    tm = min(tm, s)

    def body(dq_ref, dkv_ref, du_ref, dv_ref, dga_ref, dgb_ref, win_ref, x_ref, w_ref, sc_ref, dxin_ref,
             dx_ref, da_ref, dsh_ref):
        @pl.when(pl.program_id(0) == 0)
        def _():
            da_ref[...] = jnp.zeros_like(da_ref)
            dsh_ref[...] = jnp.zeros_like(dsh_ref)

        dh = (_dot(dq_ref[...], win_ref[:Q_END, :]) + _dot(dkv_ref[...], win_ref[Q_END:KV_END, :])
              + _dot(du_ref[...], win_ref[KV_END:U_END, :]) + _dot(dv_ref[...], win_ref[U_END:Z_END, :])
              + _dot(dga_ref[...], win_ref[Z_END:GA_END, :]) + _dot(dgb_ref[...], win_ref[GA_END:, :]))
        xv = x_ref[...]
        r = _rms(xv)
        xn = xv * r
        dxn = dh * (w_ref[...] * (1.0 + sc_ref[...]))
        dx_ref[...] = dxin_ref[...] + r * (dxn - xn * jnp.mean(dxn * xn, axis=-1, keepdims=True))
        da_ref[...] += jnp.sum(dh * xn, axis=0, keepdims=True)
        dsh_ref[...] += jnp.sum(dh, axis=0, keepdims=True)

    return _call(body, name, (_sds((s, d), F32), _sds((1, d), F32), _sds((1, d), F32)), (s // tm,),
                 [_rows(tm, d), _rows(tm, 2 * LANES), _rows(tm, d), _rows(tm, d), _rows(tm, d), _rows(tm, d),
                  _resident(w_in.shape), _rows(tm, d), _vec(d), _vec(d), _rows(tm, d)],
                 (_rows(tm, d), _vec(d), _vec(d)), ("arbitrary",))(dq, dkv, du, dv, dga, dgb, w_in, x, w, sc, dx_in)


def _rope_bwd(name, dqr, dkv_cur, dkv_prev, cosf, sinf, tm=512):
    s = dqr.shape[0]
    tm = min(tm, s)
    steps = s // tm
    per = tm // ATTN_BLOCK
    nb = s // ATTN_BLOCK

    def unrope(v, cosv, sinv):
        return v * cosv - _rope_partner(v) * sinv

    def body(dq_ref, cur_ref, prev_ref, next_ref, cos_ref, sin_ref, dqo_ref, dkvo_ref):
        i = pl.program_id(0)
        cosv, sinv = cos_ref[...], sin_ref[...]
        for j in range(D_MODEL // LANES):
            dqo_ref[:, j * LANES:(j + 1) * LANES] = unrope(dq_ref[:, j * LANES:(j + 1) * LANES], cosv, sinv).astype(BF)
        nxt = jnp.where(i < steps - 1, next_ref[...], 0.0)
        if per > 1:
            shifted = jnp.concatenate([prev_ref[ATTN_BLOCK:, :], nxt], axis=0)
        else:
            shifted = nxt
        tot = cur_ref[...] + shifted
        dkvo_ref[:, :LANES] = unrope(tot[:, :LANES], cosv, sinv).astype(BF)
        dkvo_ref[:, LANES:] = tot[:, LANES:].astype(BF)

    nxt_spec = pl.BlockSpec((ATTN_BLOCK, 2 * LANES), lambda i: (jnp.minimum((i + 1) * per, nb - 1), 0))
    return _call(body, name, (_sds((s, D_MODEL), BF), _sds((s, 2 * LANES), BF)), (steps,),
                 [_rows(tm, D_MODEL), _rows(tm, 2 * LANES), _rows(tm, 2 * LANES), nxt_spec, _rows(tm, LANES),
                  _rows(tm, LANES)],
                 (_rows(tm, D_MODEL), _rows(tm, 2 * LANES)), ("parallel",))(dqr, dkv_cur, dkv_prev, dkv_prev, cosf, sinf)


Q_PER_KV = N_Q_HEADS // N_KV_HEADS


def _band_mask_t(n):
    kj = lax.broadcasted_iota(jnp.int32, (2 * ATTN_BLOCK, ATTN_BLOCK), 0)
    qi = lax.broadcasted_iota(jnp.int32, (2 * ATTN_BLOCK, ATTN_BLOCK), 1)
    return (kj > qi) & (kj <= qi + ATTN_BLOCK) & ((n > 0) | (kj >= ATTN_BLOCK))


def _softmax_t(raw, allowed, sink):
    sc = jnp.where(allowed, raw * (HEAD_DIM ** -0.5), NEG_BIG)
    m = jnp.maximum(jnp.max(sc, axis=0, keepdims=True), sink)
    p = jnp.exp(sc - m)
    esink = jnp.exp(sink - m)
    inv = 1.0 / (jnp.sum(p, axis=0, keepdims=True) + esink)
    return p * inv, esink * inv


def _kv_specs():
    cur = pl.BlockSpec((ATTN_BLOCK, LANES), lambda n: (n, 0))
    prev = pl.BlockSpec((ATTN_BLOCK, LANES), lambda n: (jnp.maximum(n - 1, 0), 0))
    return [prev, cur] * 4


def _attention(name, qr, kk0, kk1, vv0, vv1, sinks):
    s = qr.shape[0]
    nb = s // ATTN_BLOCK

    def body(sink_ref, q_ref, k0p, k0c, k1p, k1c, v0p, v0c, v1p, v1c, y_ref):
        allowed = _band_mask_t(pl.program_id(0))
        upper = lax.broadcasted_iota(jnp.int32, (1, LANES), 1) >= HEAD_DIM
        upper_rows = lax.broadcasted_iota(jnp.int32, (LANES, 1), 0) >= HEAD_DIM
        bands = ((jnp.concatenate([k0p[...], k0c[...]], axis=0), jnp.concatenate([v0p[...], v0c[...]], axis=0)),
                 (jnp.concatenate([k1p[...], k1c[...]], axis=0), jnp.concatenate([v1p[...], v1c[...]], axis=0)))
        vbts = (bands[0][1].T, bands[1][1].T)

        def scores(h):
            hk, j, half = h // Q_PER_KV, (h % Q_PER_KV) // 2, h % 2
            col = (hk * 4 + j) * LANES
            qp = q_ref[:, col:col + LANES]
            return _dot_nt(bands[hk][0], jnp.where(upper if half else jnp.logical_not(upper), qp, jnp.zeros_like(qp)))

        out_t = None
        ahead = scores(0)
        for h in range(N_Q_HEADS):
            hk, j, half = h // Q_PER_KV, (h % Q_PER_KV) // 2, h % 2
            raw = ahead
            if h + 1 < N_Q_HEADS:
                ahead = scores(h + 1)
            pn, _ = _softmax_t(raw, allowed, sink_ref[h])
            o_h = _dot(vbts[hk], pn.astype(BF))
            out_t = jnp.where(upper_rows, o_h, out_t) if half else o_h
            if half:
                col = (hk * 4 + j) * LANES
                y_ref[:, col:col + LANES] = out_t.T.astype(BF)

    return _call(body, name, _sds((s, D_MODEL), BF), (nb,),
                 [pl.BlockSpec(memory_space=pltpu.SMEM), pl.BlockSpec((ATTN_BLOCK, D_MODEL), lambda n: (n, 0))] + _kv_specs(),
                 pl.BlockSpec((ATTN_BLOCK, D_MODEL), lambda n: (n, 0)), ("parallel",))(
                     sinks, qr, kk0, kk0, kk1, kk1, vv0, vv0, vv1, vv1)


def _attention_bwd(name, qr, kk0, kk1, vv0, vv1, sinks, dy, after=None):
    s = qr.shape[0]
    nb = s // ATTN_BLOCK

    def body(sink_ref, q_ref, dy_ref, k0p, k0c, k1p, k1c, v0p, v0c, v1p, v1c, dq_ref, cur_ref, prev_ref, dsink_ref):
        @pl.when(pl.program_id(0) == 0)
        def _():
            dsink_ref[...] = jnp.zeros_like(dsink_ref)

        allowed = _band_mask_t(pl.program_id(0))
        lane = lax.broadcasted_iota(jnp.int32, (1, LANES), 1)
        upper = lane >= HEAD_DIM
        upper_rows = lax.broadcasted_iota(jnp.int32, (LANES, 1), 0) >= HEAD_DIM
        bands = ((jnp.concatenate([k0p[...], k0c[...]], axis=0), jnp.concatenate([v0p[...], v0c[...]], axis=0)),
                 (jnp.concatenate([k1p[...], k1c[...]], axis=0), jnp.concatenate([v1p[...], v1c[...]], axis=0)))
        kbts = (bands[0][0].T, bands[1][0].T)

        def scores(h):
            hk, j, half = h // Q_PER_KV, (h % Q_PER_KV) // 2, h % 2
            kb, vb = bands[hk]
            col = (hk * 4 + j) * LANES
            sel = upper if half else jnp.logical_not(upper)
            qp = q_ref[:, col:col + LANES]
            qa = jnp.where(sel, qp, jnp.zeros_like(qp))
            dya = jnp.where(sel, dy_ref[:, col:col + LANES], 0.0).astype(BF)
            return qa, dya, _dot_nt(kb, qa), _dot_nt(vb, dya)

        dsink = jnp.zeros((1, LANES), F32)
        dk_slab = jnp.zeros((2 * ATTN_BLOCK, LANES), F32)
        dv_slab = jnp.zeros((2 * ATTN_BLOCK, LANES), F32)
        dkk = dvv = dq_t = None
        ahead = scores(0)
        for h in range(N_Q_HEADS):
            hk, j, half = h // Q_PER_KV, (h % Q_PER_KV) // 2, h % 2
            qa, dya, raw, dp = ahead
            if h + 1 < N_Q_HEADS:
                ahead = scores(h + 1)
            pn, psink = _softmax_t(raw, allowed, sink_ref[h])
            delta = jnp.sum(pn * dp, axis=0, keepdims=True)
            ds = (pn * (dp - delta) * (HEAD_DIM ** -0.5)).astype(BF)
            dsink = dsink + jnp.where(lane == h, -jnp.sum(psink * delta), 0.0)
            dq_h = _dot(kbts[hk], ds)
            dq_t = jnp.where(upper_rows, dq_h, dq_t) if half else dq_h
            dk_h, dv_h = _dot(ds, qa), _dot(pn.astype(BF), dya)
            dkk, dvv = (dk_h, dv_h) if h % Q_PER_KV == 0 else (dkk + dk_h, dvv + dv_h)
            if half:
                col = (hk * 4 + j) * LANES
                dq_ref[:, col:col + LANES] = dq_t.T
            if h % Q_PER_KV == Q_PER_KV - 1:
                mine = upper if hk else jnp.logical_not(upper)
                dk_slab = jnp.where(mine, dkk + pltpu.roll(dkk, HEAD_DIM, axis=1), dk_slab)
                dv_slab = jnp.where(mine, dvv + pltpu.roll(dvv, HEAD_DIM, axis=1), dv_slab)
        prev_ref[:, :LANES] = dk_slab[:ATTN_BLOCK]
        prev_ref[:, LANES:] = dv_slab[:ATTN_BLOCK]
        cur_ref[:, :LANES] = dk_slab[ATTN_BLOCK:]
        cur_ref[:, LANES:] = dv_slab[ATTN_BLOCK:]
        dsink_ref[...] += dsink

    blk = pl.BlockSpec((ATTN_BLOCK, D_MODEL), lambda n: (n, 0))
    kvo = pl.BlockSpec((ATTN_BLOCK, 2 * LANES), lambda n: (n, 0))
    return _call(body, name,
                 (_sds((s, D_MODEL), F32), _sds((s, 2 * LANES), F32), _sds((s, 2 * LANES), F32), _sds((1, LANES), F32)),
                 (nb,), [pl.BlockSpec(memory_space=pltpu.SMEM), blk, blk] + _kv_specs(),
                 (blk, kvo, kvo, pl.BlockSpec((1, LANES), lambda n: (0, 0))), ("arbitrary",), after=after)(
                     sinks, qr, dy, kk0, kk0, kk1, kk1, vv0, vv0, vv1, vv1)


def _sgu_weights(wm_ref, g):
    t = lax.broadcasted_iota(jnp.int32, (SGU_CHUNK, SGU_CHUNK), 0)
    sidx = lax.broadcasted_iota(jnp.int32, (SGU_CHUNK, SGU_CHUNK), 1)
    return jnp.where(sidx <= t, wm_ref[g], 0.0).astype(BF)


def _layer_norm_stats(v):
    mu = jnp.mean(v, axis=-1, keepdims=True)
    cen = v - mu
    rstd = lax.rsqrt(jnp.mean(cen * cen, axis=-1, keepdims=True) + NORM_EPS)
    return cen * rstd, rstd


def _sgu(name, u_pre, v_pre, ln_w, ln_b, wm, bfull, tm=256):
    s, w = u_pre.shape
    tm = min(tm, s)

    def body(u_ref, v_ref, lw_ref, lb_ref, wm_ref, b_ref, y_ref):
        vhat, _ = _layer_norm_stats(_gelu(v_ref[...]))
        vn = (vhat * lw_ref[...] + lb_ref[...]).astype(BF)
        for g in range(SGU_GROUPS):
            wg = _sgu_weights(wm_ref, g)
            cols = slice(g * SGU_CHUNK, (g + 1) * SGU_CHUNK)
            for ch in range(tm // SGU_CHUNK):
                rows = slice(ch * SGU_CHUNK, (ch + 1) * SGU_CHUNK)
                f = _dot(wg, vn[rows, cols]) + b_ref[g]
                y_ref[rows, cols] = (_gelu(u_ref[rows, cols]) * f).astype(BF)

    full3 = pl.BlockSpec((SGU_GROUPS, SGU_CHUNK, SGU_CHUNK), lambda i: (0, 0, 0))
    return _call(body, name, _sds((s, w), BF), (s // tm,),
                 [_rows(tm, w), _rows(tm, w), _vec(w), _vec(w), full3, full3],
                 _rows(tm, w), ("parallel",))(u_pre, v_pre, ln_w, ln_b, wm, bfull)


def _sgu_bwd(name, u_pre, v_pre, ln_w, ln_b, wm, bfull, dy, tm=256, after=None):
    s, w = u_pre.shape
    tm = min(tm, s)
    steps = s // tm

    def body(u_ref, v_ref, lw_ref, lb_ref, wm_ref, b_ref, dy_ref, du_ref, dv_ref, dwm_ref, db_ref, dlw_ref, dlb_ref,
             dfsum_ref):
        i = pl.program_id(0)

        @pl.when(i == 0)
        def _():
            dwm_ref[...] = jnp.zeros_like(dwm_ref)
            dlw_ref[...] = jnp.zeros_like(dlw_ref)
            dlb_ref[...] = jnp.zeros_like(dlb_ref)
            dfsum_ref[...] = jnp.zeros_like(dfsum_ref)

        vpre = v_ref[...]
        vhat, rstd = _layer_norm_stats(_gelu(vpre))
        vn = (vhat * lw_ref[...] + lb_ref[...]).astype(BF)
        t = lax.broadcasted_iota(jnp.int32, (SGU_CHUNK, SGU_CHUNK), 0)
        sidx = lax.broadcasted_iota(jnp.int32, (SGU_CHUNK, SGU_CHUNK), 1)
        dvn_cols = []
        for g in range(SGU_GROUPS):
            wg = _sgu_weights(wm_ref, g)
            cols = slice(g * SGU_CHUNK, (g + 1) * SGU_CHUNK)
            dvn_rows = []
            dwg = jnp.zeros((SGU_CHUNK, SGU_CHUNK), F32)
            dfs = jnp.zeros((SGU_CHUNK, SGU_CHUNK), F32)
            for ch in range(tm // SGU_CHUNK):
                rows = slice(ch * SGU_CHUNK, (ch + 1) * SGU_CHUNK)
                upre = u_ref[rows, cols]
                dyv = dy_ref[rows, cols].astype(F32)
                f = _dot(wg, vn[rows, cols]) + b_ref[g]
                du_ref[rows, cols] = (dyv * f * _gelu_grad(upre)).astype(BF)
                df = dyv * _gelu(upre)
                dfb = df.astype(BF)
                dvn_rows.append(_dot_tn(wg, dfb))
                dwg = dwg + _dot_nt(dfb, vn[rows, cols])
                dfs = dfs + df
            dwm_ref[g] += jnp.where(sidx <= t, dwg, 0.0)
            dfsum_ref[g] += dfs
            dvn_cols.append(jnp.concatenate(dvn_rows, axis=0) if len(dvn_rows) > 1 else dvn_rows[0])
        dvn = jnp.concatenate(dvn_cols, axis=1)
        dlw_ref[...] += jnp.sum(dvn * vhat, axis=0, keepdims=True)
        dlb_ref[...] += jnp.sum(dvn, axis=0, keepdims=True)
        dvh = dvn * lw_ref[...]
        dvg = rstd * (dvh - jnp.mean(dvh, axis=-1, keepdims=True) - vhat * jnp.mean(dvh * vhat, axis=-1, keepdims=True))
        dv_ref[...] = (dvg * _gelu_grad(vpre)).astype(BF)

        @pl.when(i == steps - 1)
        def _():
            for g in range(SGU_GROUPS):
                db_ref[g:g + 1, :] = jnp.sum(dfsum_ref[g].T, axis=0, keepdims=True)

    full3 = pl.BlockSpec((SGU_GROUPS, SGU_CHUNK, SGU_CHUNK), lambda i: (0, 0, 0))
    return _call(body, name,
                 (_sds((s, w), BF), _sds((s, w), BF), _sds((SGU_GROUPS, SGU_CHUNK, SGU_CHUNK), F32),
                  _sds((SGU_GROUPS, SGU_CHUNK), F32), _sds((1, w), F32), _sds((1, w), F32)),
                 (steps,),
                 [_rows(tm, w), _rows(tm, w), _vec(w), _vec(w), full3, full3, _rows(tm, w)],
                 (_rows(tm, w), _rows(tm, w), full3, pl.BlockSpec((SGU_GROUPS, SGU_CHUNK), lambda i: (0, 0)), _vec(w), _vec(w)),
                 ("arbitrary",), scratch=[pltpu.VMEM((SGU_GROUPS, SGU_CHUNK, SGU_CHUNK), F32)], after=after)(
                     u_pre, v_pre, ln_w, ln_b, wm, bfull, dy)


def _mix_out(name, y_sgu, y_attn, ga_pre, gb_pre, x, g1, proj_a, proj_b, w_out, w2, sc2, sh2, tm=256):
    s, d = x.shape
    tm = min(tm, s)

    def body(ys_ref, ya_ref, ga_ref, gb_ref, x_ref, g1_ref, wa_ref, wb_ref, wo_ref, w2_ref, sc2_ref, sh2_ref,
             m_ref, pa_ref, pb_ref, o_ref, x1_ref, h2_ref):
        pa = _dot(ys_ref[...], wa_ref[...].reshape(d, d))
        pb = _dot(ya_ref[...], wb_ref[...].reshape(d, d))
        pa_ref[...] = pa.astype(BF)
        pb_ref[...] = pb.astype(BF)
        merged = (_sigmoid(ga_ref[...].astype(F32)) * pa + _sigmoid(gb_ref[...].astype(F32)) * pb).astype(BF)
        m_ref[...] = merged
        o = _dot(merged, wo_ref[...].reshape(d, d))
        o_ref[...] = o.astype(BF)
        x1 = x_ref[...] + g1_ref[...] * o
        x1_ref[...] = x1
        h2_ref[...] = ((x1 * _rms(x1)) * w2_ref[...] * (1.0 + sc2_ref[...]) + sh2_ref[...]).astype(BF)

    f, b = _sds((s, d), F32), _sds((s, d), BF)
    r = _rows(tm, d)
    wspec = _resident(proj_a.shape)
    return _call(body, name, (b, b, b, b, f, b), (s // tm,),
                 [r, r, r, r, r, _vec(d), wspec, wspec, wspec, _vec(d), _vec(d), _vec(d)], (r,) * 6, ("parallel",))(
                     y_sgu, y_attn, ga_pre, gb_pre, x, g1, proj_a, proj_b, w_out, w2, sc2, sh2)


def _mix_bwd(name, do, w_out, proj_a, proj_b, ga_pre, gb_pre, pa, pb, tm=256):
    s, d = do.shape
    tm = min(tm, s)

    def body(do_ref, wo_ref, wa_ref, wb_ref, ga_ref, gb_ref, pa_ref, pb_ref,
             dpa_ref, dpb_ref, dga_ref, dgb_ref, dys_ref, dya_ref):
        dm = _dot_nt(do_ref[...], wo_ref[...].reshape(d, d))
        ga = _sigmoid(ga_ref[...].astype(F32))
        gb = _sigmoid(gb_ref[...].astype(F32))
        dpa = (dm * ga).astype(BF)
        dpb = (dm * gb).astype(BF)
        dpa_ref[...] = dpa
        dpb_ref[...] = dpb
        dga_ref[...] = (dm * pa_ref[...].astype(F32) * ga * (1.0 - ga)).astype(BF)
        dgb_ref[...] = (dm * pb_ref[...].astype(F32) * gb * (1.0 - gb)).astype(BF)
        dys_ref[...] = _dot_nt(dpa, wa_ref[...].reshape(d, d)).astype(BF)
        dya_ref[...] = _dot_nt(dpb, wb_ref[...].reshape(d, d)).astype(BF)

    f, b = _sds((s, d), F32), _sds((s, d), BF)
    r = _rows(tm, d)
    wspec = _resident(w_out.shape)
    return _call(body, name, (b, b, b, b, b, b), (s // tm,), [r, wspec, wspec, wspec, r, r, r, r], (r,) * 6,
                 ("parallel",))(do, w_out, proj_a, proj_b, ga_pre, gb_pre, pa, pb)


def _ffn_up_act(name, h2, w_gate, w_up, cw, cb, tm=1024):
    s, d = h2.shape
    tm = min(tm, s)
    tc = FFN_SHARD_PAD
    per = tm // HALO

    def body(h_ref, hprev_ref, wg_ref, wu_ref, cw_ref, cb_ref, a_ref, ac_ref, up_ref, hf_ref):
        hv = h_ref[...]
        a = _dot_nt(hv, wg_ref[0]).astype(BF)
        up = _dot_nt(hv, wu_ref[0]).astype(BF)
        a_ref[...] = a
        up_ref[...] = up
        prev = jnp.where(pl.program_id(1) > 0, _dot_nt(hprev_ref[...], wg_ref[0]).astype(BF).astype(F32), 0.0)
        ext = jnp.concatenate([prev, a.astype(F32)], axis=0)
        ac = (cb_ref[...] + cw_ref[0:1, :] * pltpu.roll(ext, 2, axis=0) + cw_ref[1:2, :] * pltpu.roll(ext, 1, axis=0)
              + cw_ref[2:3, :] * ext)[HALO:]
        ac_ref[...] = ac.astype(BF)
        hf_ref[...] = (ac * _sigmoid(ac) * up.astype(F32)).astype(BF)

    wspec = pl.BlockSpec((1, tc, d), lambda j, i: (j, 0, 0))
    ospec = pl.BlockSpec((tm, tc), lambda j, i: (i, j))
    o = _sds((s, FFN_PAD), BF)
    return _call(body, name, (o, o, o, o), (N_CHIPS, s // tm),
                 [pl.BlockSpec((tm, d), lambda j, i: (i, 0)), pl.BlockSpec((HALO, d), lambda j, i: (jnp.maximum(i * per - 1, 0), 0)),
                  wspec, wspec, pl.BlockSpec((3, tc), lambda j, i: (0, j)), pl.BlockSpec((1, tc), lambda j, i: (0, j))],
                 (ospec, ospec, ospec, ospec), ("parallel", "parallel"))(h2, h2, w_gate, w_up, cw, cb)


def _ffn_down(name, hf, w_down, x1, g2, tm=512):
    s, d = x1.shape
    tm = min(tm, s)

    def body(hf_ref, wd_ref, x1_ref, g2_ref, dn_ref, x2_ref):
        dn = _dot(hf_ref[...], wd_ref[...].reshape(FFN_PAD, d))
        dn_ref[...] = dn.astype(BF)
        x2_ref[...] = x1_ref[...] + g2_ref[...] * dn

    return _call(body, name, (_sds((s, d), BF), _sds((s, d), F32)), (s // tm,),
                 [_rows(tm, FFN_PAD), _resident(w_down.shape), _rows(tm, d), _vec(d)],
                 (_rows(tm, d), _rows(tm, d)), ("parallel",))(hf, w_down, x1, g2)


def _ffn_down_bwd_act(name, dx2, dn, g2, w_down, a, ac, up, cw, tm=256, after=None):
    s, d = dx2.shape
    c = a.shape[1]
    tm = min(tm, s)
    tc = FFN_SHARD_PAD
    per = tm // HALO
    steps = s // tm
    last = s // HALO - 1
    n = tm + HALO

    def body(dx_ref, dxnext_ref, dn_ref, g2_ref, wd_ref, a_ref, ac_ref, acnext_ref, up_ref, upnext_ref, cw_ref,
             ddn_ref, da_ref, dup_ref, dg_ref, dcw_ref, dcb_ref):
        i = pl.program_id(0)

        @pl.when(i == 0)
        def _():
            dg_ref[...] = jnp.zeros_like(dg_ref)
            dcw_ref[...] = jnp.zeros_like(dcw_ref)
            dcb_ref[...] = jnp.zeros_like(dcb_ref)

        dxv = dx_ref[...]
        ddn = (dxv * g2_ref[...]).astype(BF)
        ddn_ref[...] = ddn
        dg_ref[...] += jnp.sum(dxv * dn_ref[...].astype(F32), axis=0, keepdims=True)
        ddn_next = jnp.where(i < steps - 1, dxnext_ref[...] * g2_ref[...], 0.0).astype(BF)
        ddn_ext = jnp.concatenate([ddn, ddn_next], axis=0)
        for k in range(N_CHIPS):
            cols = slice(k * tc, (k + 1) * tc)
            dh = _dot_nt(ddn_ext, wd_ref[k])
            ace = jnp.concatenate([ac_ref[:, cols].astype(F32), acnext_ref[:, cols].astype(F32)], axis=0)
            upe = jnp.concatenate([up_ref[:, cols].astype(F32), upnext_ref[:, cols].astype(F32)], axis=0)
            sig = _sigmoid(ace)
            silu = ace * sig
            dac = dh * upe * (sig + silu * (1.0 - sig))
            dup_ref[:, cols] = (dh[:tm] * silu[:tm]).astype(BF)
            d1 = pltpu.roll(dac, n - 1, axis=0)[:tm]
            d2 = pltpu.roll(dac, n - 2, axis=0)[:tm]
            d0 = dac[:tm]
            da_ref[:, cols] = (cw_ref[2:3, cols] * d0 + cw_ref[1:2, cols] * d1 + cw_ref[0:1, cols] * d2).astype(BF)
            a0 = a_ref[:, cols].astype(F32)
            dcb_ref[:, cols] += jnp.sum(d0, axis=0, keepdims=True)
            dcw_ref[0:1, cols] += jnp.sum(d2 * a0, axis=0, keepdims=True)
            dcw_ref[1:2, cols] += jnp.sum(d1 * a0, axis=0, keepdims=True)
            dcw_ref[2:3, cols] += jnp.sum(d0 * a0, axis=0, keepdims=True)

    nxt = lambda width: pl.BlockSpec((HALO, width), lambda i: (jnp.minimum((i + 1) * per, last), 0))
    wide = _sds((s, c), BF)
    return _call(body, name, (_sds((s, d), BF), wide, wide, _sds((1, d), F32), _sds((3, c), F32), _sds((1, c), F32)), (steps,),
                 [_rows(tm, d), nxt(d), _rows(tm, d), _vec(d), _resident(w_down.shape), _rows(tm, c), _rows(tm, c), nxt(c),
                  _rows(tm, c), nxt(c), pl.BlockSpec((3, c), lambda i: (0, 0))],
                 (_rows(tm, d), _rows(tm, c), _rows(tm, c), _vec(d), pl.BlockSpec((3, c), lambda i: (0, 0)), _vec(c)),
                 ("arbitrary",), after=after)(dx2, dx2, dn, g2, w_down, a, ac, ac, up, up, cw)


def _ffn_up_bwd(name, da, dup, w_gate, w_up, x1, dx2, w2, sc2, o, g1, tm=256, after=None):
    s, d = x1.shape
    tm = min(tm, s)
    tc = FFN_SHARD_PAD

    def body(da_ref, dup_ref, wg_ref, wu_ref, x1_ref, dx2_ref, w2_ref, sc2_ref, o_ref, g1_ref,
             dx1_ref, do_ref, dnw_ref, dsh_ref, dg1_ref):
        @pl.when(pl.program_id(0) == 0)
        def _():
            dnw_ref[...] = jnp.zeros_like(dnw_ref)
            dsh_ref[...] = jnp.zeros_like(dsh_ref)
            dg1_ref[...] = jnp.zeros_like(dg1_ref)

        dh = jnp.zeros((tm, d), F32)
        for k in range(N_CHIPS):
            cols = slice(k * tc, (k + 1) * tc)
            dh = dh + _dot(da_ref[:, cols], wg_ref[k]) + _dot(dup_ref[:, cols], wu_ref[k])
        xv = x1_ref[...]
        r = _rms(xv)
        xn = xv * r
        dxn = dh * (w2_ref[...] * (1.0 + sc2_ref[...]))
        dx1 = dx2_ref[...] + r * (dxn - xn * jnp.mean(dxn * xn, axis=-1, keepdims=True))
        dx1_ref[...] = dx1
        dnw_ref[...] += jnp.sum(dh * xn, axis=0, keepdims=True)
        dsh_ref[...] += jnp.sum(dh, axis=0, keepdims=True)
        do_ref[...] = (dx1 * g1_ref[...]).astype(BF)
        dg1_ref[...] += jnp.sum(dx1 * o_ref[...].astype(F32), axis=0, keepdims=True)

    v = _sds((1, d), F32)
    r = _rows(tm, d)
    wspec = _resident(w_gate.shape)
    return _call(body, name, (_sds((s, d), F32), _sds((s, d), BF), v, v, v), (s // tm,),
                 [_rows(tm, FFN_PAD), _rows(tm, FFN_PAD), wspec, wspec, r, r, _vec(d), _vec(d), r, _vec(d)],
                 (r, r, _vec(d), _vec(d), _vec(d)), ("arbitrary",), after=after)(da, dup, w_gate, w_up, x1, dx2, w2, sc2, o, g1)


def _loss_head(name, x, w, target, tm=512):
    s, d = x.shape
    tm = min(tm, s)

    def body(x_ref, w_ref, t_ref, dx_ref, loss_ref, dw_ref):
        @pl.when(pl.program_id(0) == 0)
        def _():
            loss_ref[...] = jnp.zeros_like(loss_ref)
            dw_ref[...] = jnp.zeros_like(dw_ref)

        xv = x_ref[...]
        r = _rms(xv)
        xn = xv * r
        err = xn * w_ref[...] - t_ref[...]
        loss_ref[...] += 0.5 * jnp.sum(jnp.mean(err * err, axis=-1, keepdims=True))
        dy = err * (1.0 / d)
        dw_ref[...] += jnp.sum(dy * xn, axis=0, keepdims=True)
        dxn = dy * w_ref[...]
        dx_ref[...] = r * (dxn - xn * jnp.mean(dxn * xn, axis=-1, keepdims=True))

    return _call(body, name, (_sds((s, d), F32), _sds((1, LANES), F32), _sds((1, d), F32)), (s // tm,),
                 [_rows(tm, d), _vec(d), _rows(tm, d)], (_rows(tm, d), _vec(LANES), _vec(d)), ("arbitrary",))(x, w, target)


def _layer_fwd(l, x, mod, p, cosf, sinf, after=None, late=None, later=None):
    sh1, sc1, g1, sh2, sc2, g2 = mod
    tag = f"l{l}_"
    h, qr, kk0, kk1, vv0, vv1, u_pre, v_pre, ga_pre, gb_pre = _in_proj(
        tag + "in_proj", x, p["norm1_w"], sc1, sh1, p["w_in"], cosf, sinf, after=after)
    y_attn = _attention(tag + "attn", qr, kk0, kk1, vv0, vv1, p["sinks"])
    y_sgu = _sgu(tag + "sgu", u_pre, v_pre, p["sgu_ln_w"], p["sgu_ln_b"], p["sgu_w"], p["sgu_bfull"])
    if late is not None:
        p = dict(p, **late(y_sgu))
    merged, pa, pb, o, x1, h2 = _mix_out(tag + "mix_out", y_sgu, y_attn, ga_pre, gb_pre, x, g1, p["proj_a"], p["proj_b"],
                                         p["w_out"], p["norm2_w"], sc2, sh2)
    if later is not None:
        p = dict(p, **later(h2))
    a, ac, up, hf = _ffn_up_act(tag + "ffn_up", h2, p["w_gate"], p["w_up"], p["conv_w"], p["conv_b"])
    dn, x2 = _ffn_down(tag + "ffn_down", hf, p["w_down"], x1, g2)
    saved = dict(x=x, h=h, qr=qr, kk0=kk0, kk1=kk1, vv0=vv0, vv1=vv1, u_pre=u_pre, v_pre=v_pre, ga_pre=ga_pre,
                 gb_pre=gb_pre, y_attn=y_attn, y_sgu=y_sgu, merged=merged, pa=pa, pb=pb, o=o, x1=x1, h2=h2, a=a, ac=ac, up=up,
                 hf=hf, dn=dn)
    return x2, saved, p


def _layer_bwd(l, dx2, mod, p, sv, cosf, sinf, after=None, emit=None, tick=None):
    sh1, sc1, g1, sh2, sc2, g2 = mod
    tag = f"l{l}_b_"
    d = D_MODEL
    g = {}
    ready = (lambda names: emit({k: g.pop(k) for k in names})) if emit else (lambda names: None)
    tick = tick or (lambda y: None)
    ddn, da, dup, dg2, g["conv_w"], g["conv_b"] = _ffn_down_bwd_act(
        tag + "ffn_down", dx2, sv["dn"], g2, p["w_down"], sv["a"], sv["ac"], sv["up"], p["conv_w"], after=after)
    g["w_down"] = _matmul_tn(tag + "dw_down", sv["hf"], ddn, tk=FFN_SHARD_PAD, after=tick(ddn)).reshape(N_CHIPS, FFN_SHARD_PAD, d)
    g["w_gate"] = _matmul_tn(tag + "dw_gate", da, sv["h2"], tk=FFN_SHARD_PAD).reshape(N_CHIPS, FFN_SHARD_PAD, d)
    g["w_up"] = _matmul_tn(tag + "dw_up", dup, sv["h2"], tk=FFN_SHARD_PAD).reshape(N_CHIPS, FFN_SHARD_PAD, d)
    dx1, do, da2, dsh2, dg1 = _ffn_up_bwd(tag + "ffn_up", da, dup, p["w_gate"], p["w_up"], sv["x1"], dx2, p["norm2_w"],
                                          sc2, sv["o"], g1, after=ready(("w_down", "w_gate", "w_up")))
    g["norm2_w"] = da2 * (1.0 + sc2)
    dsc2 = da2 * p["norm2_w"]
    g["w_out"] = _matmul_tn(tag + "dw_out", sv["merged"], do, after=tick(do)).reshape(N_CHIPS, d // N_CHIPS, d)
    dpa, dpb, dga, dgb, dy_sgu, dy_attn = _mix_bwd(tag + "mix", do, p["w_out"], p["proj_a"], p["proj_b"], sv["ga_pre"],
                                                  sv["gb_pre"], sv["pa"], sv["pb"])
    g["proj_a"] = _matmul_tn(tag + "dproj_a", sv["y_sgu"], dpa).reshape(N_CHIPS, d // N_CHIPS, d)
    g["proj_b"] = _matmul_tn(tag + "dproj_b", sv["y_attn"], dpb).reshape(N_CHIPS, d // N_CHIPS, d)
    du, dv, g["sgu_w"], g["sgu_b"], g["sgu_ln_w"], g["sgu_ln_b"] = _sgu_bwd(
        tag + "sgu", sv["u_pre"], sv["v_pre"], p["sgu_ln_w"], p["sgu_ln_b"], p["sgu_w"], p["sgu_bfull"], dy_sgu,
        after=ready(("w_out", "proj_a", "proj_b")))
    dqr, dkv_cur, dkv_prev, dsink = _attention_bwd(tag + "attn", sv["qr"], sv["kk0"], sv["kk1"], sv["vv0"], sv["vv1"],
                                                   p["sinks"], dy_attn, after=tick(du))
    g["sinks"] = dsink[0, :N_Q_HEADS]
    dq, dkv = _rope_bwd(tag + "rope", dqr, dkv_cur, dkv_prev, cosf, sinf)
    dw_in, row0 = None, 0
    for n, t in (("q", dq), ("kv", dkv), ("u", du), ("v", dv), ("ga", dga), ("gb", dgb)):
        dw_in = _matmul_tn_rows(tag + "dw_in_" + n, t, sv["h"], dw_in, row0, IN_COLS)
        row0 += t.shape[1]
    g["w_in"] = dw_in.reshape(N_CHIPS, IN_COLS // N_CHIPS, d)
    dx, da1, dsh1 = _in_proj_bwd(tag + "in_proj", dq, dkv, du, dv, dga, dgb, p["w_in"], sv["x"], p["norm1_w"], sc1, dx1)
    g["norm1_w"] = da1 * (1.0 + sc1)
    dsc1 = da1 * p["norm1_w"]
    return dx, (dsh1, dsc1, dg1, dsh2, dsc2, dg2), g


def _pad_to(a, axis, size):
    pad = [(0, 0)] * a.ndim
    pad[axis] = (0, size - a.shape[axis])
    return jnp.pad(a, pad)


def _early_params(w_in, small):
    d = D_MODEL
    return dict(
        w_in=w_in.reshape(IN_COLS, d), norm1_w=small["norm1_w"].reshape(1, d), sinks=small["sinks"],
        sgu_ln_w=small["sgu_ln_w"].reshape(1, d), sgu_ln_b=small["sgu_ln_b"].reshape(1, d), sgu_w=small["sgu_w"],
        sgu_bfull=jnp.broadcast_to(small["sgu_b"][:, :, None], (SGU_GROUPS, SGU_CHUNK, SGU_CHUNK)))


def _mix_params(proj_a, proj_b, w_out, small):
    return dict(proj_a=proj_a, proj_b=proj_b, w_out=w_out, norm2_w=small["norm2_w"].reshape(1, D_MODEL))


def _ffn_params(w_gate, w_up, w_down, conv_w, small):
    return dict(
        w_gate=w_gate, w_up=w_up, w_down=w_down, conv_w=conv_w.transpose(1, 0, 2).reshape(3, FFN_PAD),
        conv_b=_pad_to(small["conv_b"].reshape(N_CHIPS, FFN_SHARD), 1, FFN_SHARD_PAD).reshape(1, FFN_PAD))


def _layer_params(w_in, proj_a, proj_b, w_out, w_gate, w_up, w_down, conv_w, small):
    return dict(_early_params(w_in, small), **_mix_params(proj_a, proj_b, w_out, small),
                **_ffn_params(w_gate, w_up, w_down, conv_w, small))


def _conv_grads_natural(g):
    cw = g["conv_w"].reshape(3, N_CHIPS, FFN_SHARD_PAD)[:, :, :FFN_SHARD].reshape(3, FFN_DIM)
    cb = g["conv_b"].reshape(N_CHIPS, FFN_SHARD_PAD)[:, :FFN_SHARD].reshape(FFN_DIM)
    return cw, cb


def _rope_tables(positions):
    inv_freq = ROPE_THETA ** (-jnp.arange(0, ROT_DIM, 2, dtype=F32) / ROT_DIM)
    ang = positions.astype(F32)[:, None] * inv_freq
    cos, sin = jnp.cos(ang), jnp.sin(ang)
    s = positions.shape[0]
    rest = HEAD_DIM - ROT_DIM
    cos_head = jnp.concatenate([cos, cos, jnp.ones((s, rest), F32)], axis=1)
    sin_head = jnp.concatenate([-sin, sin, jnp.zeros((s, rest), F32)], axis=1)
    return jnp.tile(cos_head, (1, LANES // HEAD_DIM)), jnp.tile(sin_head, (1, LANES // HEAD_DIM))


ADA_ROWS = 16


def _ada_fwd(name, c_rows, ada_w, ada_b_cols, tn=512):
    depth, d, n = ada_w.shape

    def body(c_ref, w_ref, b_ref, o_ref):
        cv = c_ref[...]
        act = (cv * _sigmoid(cv)).astype(BF)
        o_ref[0] = _dot(act, w_ref[0].astype(BF)) + b_ref[0]

    return _call(body, name, _sds((depth, ADA_ROWS, n), F32), (depth, n // tn),
                 [pl.BlockSpec((ADA_ROWS, d), lambda l, j: (0, 0)), pl.BlockSpec((1, d, tn), lambda l, j: (l, 0, j)),
                  pl.BlockSpec((1, 1, tn), lambda l, j: (l, 0, j))],
                 pl.BlockSpec((1, ADA_ROWS, tn), lambda l, j: (l, 0, j)), ("parallel", "parallel"))(c_rows, ada_w, ada_b_cols)


def _ada_bwd(name, c_rows, dmod_cols, tn=512):
    depth, _, n = dmod_cols.shape
    d = c_rows.shape[1]

    def body(c_ref, dm_ref, o_ref):
        cv = c_ref[...]
        act = (cv * _sigmoid(cv)).astype(BF)
        o_ref[0] = _dot_tn(act, dm_ref[0].astype(BF))

    return _call(body, name, _sds((depth, d, n), F32), (depth, n // tn),
                 [pl.BlockSpec((ADA_ROWS, d), lambda l, j: (0, 0)), pl.BlockSpec((1, ADA_ROWS, tn), lambda l, j: (l, 0, j))],
                 pl.BlockSpec((1, d, tn), lambda l, j: (l, 0, j)), ("parallel", "parallel"))(c_rows, dmod_cols)


def _colsum(name, a):
    r, n = a.shape

    def body(a_ref, o_ref):
        o_ref[...] = jnp.sum(a_ref[...], axis=0, keepdims=True)

    return _call(body, name, _sds((1, n), F32), (1,), [pl.BlockSpec((r, n), lambda i: (0, 0))],
                 pl.BlockSpec((1, n), lambda i: (0, 0)), ("arbitrary",))(a)


REL_SIBLING = (0, 0, 1)
REL_CHIPS = ((1, 0, 0), (0, 1, 0), (1, 1, 0))
REL_ALL = tuple((fx, fy, fc) for fx in (0, 1) for fy in (0, 1) for fc in (0, 1) if fx or fy or fc)


def _chip_of(dev):
    return 2 * dev[0] + dev[1]


def _dev_of(dev):
    return 4 * dev[0] + 2 * dev[1] + dev[2]


def _flip(dev, rel):
    return tuple(1 - m if f else m for m, f in zip(dev, rel))


def _exchange(name, arrays, n_out, stages, aliases=None):
    out_shapes, stages = stages[0], stages[1:]
    n_in = len(arrays)
    aliases = aliases or {}
    n_remote = sum(len(plan) for plan, _ in stages)
    n_local = sum(len(local) for _, local in stages)

    def at(ref, idx):
        return ref.at[idx] if len(idx) else ref

    def body(*refs):
        bufs = list(refs[:n_in + n_out])
        for i_in, i_out in aliases.items():
            bufs[i_in] = bufs[n_in + i_out]
        send_sems, recv_sems, local_sems = refs[n_in + n_out:]
        me = (lax.axis_index("x"), lax.axis_index("y"), lax.axis_index("c"))
        base_r = base_l = 0
        pending = []
        for plan, local in stages:
            def remote(k, entry, sender, receiver):
                rel, si, ssel, di, dsel = entry
                return pltpu.make_async_remote_copy(
                    src_ref=at(bufs[si], ssel(sender, receiver)), dst_ref=at(bufs[di], dsel(sender, receiver)),
                    send_sem=send_sems.at[k], recv_sem=recv_sems.at[k], device_id=_flip(me, rel), device_id_type=MESH)

            sends = [remote(base_r + k, e, me, _flip(me, e[0])) for k, e in enumerate(plan)]
            for cp in sends:
                cp.start()
            for k, (si, ssel, di, dsel) in enumerate(local):
                cp = pltpu.make_async_copy(at(bufs[si], ssel(me)), at(bufs[di], dsel(me)), local_sems.at[base_l + k])
                cp.start()
                pending.append(cp.wait)
            for k, e in enumerate(plan):
                remote(base_r + k, e, _flip(me, e[0]), me).wait_recv()
            pending += [cp.wait_send for cp in sends]
            base_r += len(plan)
            base_l += len(local)
        for wait in pending:
            wait()

    any_spec = pl.BlockSpec(memory_space=pl.ANY)
    return pl.pallas_call(
        body, name=name, out_shape=tuple(out_shapes), in_specs=[any_spec] * n_in, out_specs=tuple([any_spec] * n_out),
        input_output_aliases=dict(aliases),
        scratch_shapes=[pltpu.SemaphoreType.DMA((max(n_remote, 1),)), pltpu.SemaphoreType.DMA((max(n_remote, 1),)),
                        pltpu.SemaphoreType.DMA((max(n_local, 1),))])(*arrays)


HBM_SPEC = pl.BlockSpec(memory_space=pltpu.HBM)
SEM_SPEC = pl.BlockSpec(memory_space=pltpu.SEMAPHORE)


def _split_copies(bufs, plan, local, send_sems, recv_sems, local_sems):
    me = (lax.axis_index("x"), lax.axis_index("y"), lax.axis_index("c"))

    def at(ref, idx):
        return ref.at[idx] if len(idx) else ref

    def remote(k, sender, receiver):
        rel, si, ssel, di, dsel = plan[k]
        return pltpu.make_async_remote_copy(
            src_ref=at(bufs[si], ssel(sender, receiver)), dst_ref=at(bufs[di], dsel(sender, receiver)),
            send_sem=send_sems.at[k], recv_sem=recv_sems.at[k], device_id=_flip(me, rel), device_id_type=MESH)

    sends = [remote(k, me, _flip(me, plan[k][0])) for k in range(len(plan))]
    arrivals = [remote(k, _flip(me, plan[k][0]), me) for k in range(len(plan))]
    locs = [pltpu.make_async_copy(at(bufs[si], ssel(me)), at(bufs[di], dsel(me)), local_sems.at[k])
            for k, (si, ssel, di, dsel) in enumerate(local)]
    return sends, arrivals, locs


def _exchange_start(name, arrays, out_shapes, plan, local, inplace=False):
    n_in, n_out = len(arrays), len(out_shapes)
    n_buf = n_in + n_out

    def body(*refs):
        sems = refs[n_buf:n_buf + 3]
        bufs = refs[n_buf + 3:2 * n_buf + 3]
        sends, _, locs = _split_copies(bufs * 2 if inplace else bufs, plan, local, *sems)
        for cp in sends + locs:
            cp.start()
        refs[-1][...] = jnp.zeros_like(refs[-1])

    zones = [lax.empty(o.shape, o.dtype) for o in out_shapes]
    operands = [pltpu.with_memory_space_constraint(a, pltpu.HBM) for a in list(arrays) + zones]
    sem = lambda n: pltpu.SemaphoreType.DMA((max(n, 1),))
    out = pl.pallas_call(
        body, name=name,
        out_shape=(sem(len(plan)), sem(len(plan)), sem(len(local)), *[pltpu.HBM(a.shape, a.dtype) for a in operands],
                   _sds((SUBLANES, LANES), F32)),
        in_specs=[HBM_SPEC] * n_buf,
        out_specs=(SEM_SPEC, SEM_SPEC, SEM_SPEC, *[HBM_SPEC] * n_buf, pl.BlockSpec(memory_space=pltpu.VMEM)),
        input_output_aliases={i: 3 + i for i in range(n_buf)},
        compiler_params=pltpu.CompilerParams(has_side_effects=pltpu.SideEffectType.DATAFLOW_SIDE_EFFECTING))(*operands)
    pending = dict(name=name, sems=out[:3], thru=out[3:3 + n_in], zones=out[3 + n_in:3 + n_buf], plan=plan, local=local,
                   inplace=inplace)
    return pending, out[-1]


def _exchange_wait(pending, after, both=False):
    thru, zones, plan, local, inplace = (pending[k] for k in ("thru", "zones", "plan", "local", "inplace"))
    n_in, n_buf = len(thru), len(thru) + len(zones)

    def body(*refs):
        bufs = refs[:n_buf]
        sends, arrivals, locs = _split_copies(bufs * 2 if inplace else bufs, plan, local, *refs[n_buf:n_buf + 3])
        for cp in arrivals:
            cp.wait_recv()
        for cp in sends:
            cp.wait_send()
        for cp in locs:
            cp.wait()

    out = pl.pallas_call(
        body, name=pending["name"] + "_wait", out_shape=tuple(pltpu.HBM(a.shape, a.dtype) for a in list(thru) + list(zones)),
        in_specs=[HBM_SPEC] * n_buf + [SEM_SPEC] * 3 + [pl.BlockSpec(memory_space=pl.ANY)],
        out_specs=tuple([HBM_SPEC] * n_buf), input_output_aliases={i: i for i in range(n_buf)},
        compiler_params=pltpu.CompilerParams(has_side_effects=pltpu.SideEffectType.DATAFLOW_SIDE_EFFECTING))(
            *thru, *zones, *pending["sems"], after)
    if both:
        return out[:n_in], out[n_in:]
    return out[:n_in] if inplace else out[n_in:]


def _whole(*_):
    return ()


def _half_rows(rows, core):
    return pl.ds(core * (rows // 2), rows // 2)


def _gather_weights_plan(shards):
    n = len(shards)
    dsts = [_sds((N_CHIPS,) + a.shape, a.dtype) for a in shards]
    fetch, forward = [], []
    for t, a in enumerate(shards):
        rows = a.shape[0]
        if rows % (2 * 16) == 0:
            fetch += [(rel, t, (lambda s_, r_, rows=rows: (_half_rows(rows, s_[2]),)), n + t,
                       (lambda s_, r_, rows=rows: (_chip_of(s_), _half_rows(rows, s_[2])))) for rel in REL_CHIPS]
            forward += [(REL_SIBLING, n + t, (lambda s_, r_, rows=rows, rel=rel: (_chip_of(_flip(s_, rel)), _half_rows(rows, s_[2]))),
                         n + t, (lambda s_, r_, rows=rows, rel=rel: (_chip_of(_flip(s_, rel)), _half_rows(rows, s_[2]))))
                        for rel in REL_CHIPS]
        else:
            fetch += [(rel, t, _whole, n + t, lambda s_, r_: (_chip_of(s_),)) for rel in REL_CHIPS]
    local = [(t, _whole, n + t, lambda me: (_chip_of(me),)) for t in range(n)]
    return dsts, fetch, local, forward


def _gather_weights_start(name, shards):
    dsts, fetch, local, forward = _gather_weights_plan(shards)
    pending, token = _exchange_start(name, shards, dsts, fetch, local)
    return dict(pending, forward=forward), token


def _gather_weights_finish(pending, after):
    landed = _exchange_wait(pending, after)
    n = len(landed)
    return _exchange(pending["name"] + "_forward", landed, n, [[_sds(a.shape, a.dtype) for a in landed], (pending["forward"], [])],
                     aliases={t: t for t in range(n)})


def _gather_chips(name, arrays):
    n = len(arrays)
    dsts = [_sds((N_CHIPS,) + a.shape, a.dtype) for a in arrays]
    plan = [(rel, t, _whole, n + t, lambda s_, r_: (_chip_of(s_),)) for t in range(n) for rel in REL_CHIPS]
    local = [(t, _whole, n + t, lambda me: (_chip_of(me),)) for t in range(n)]
    return _exchange(name, arrays, n, [dsts, (plan, local)])


def _gather_all(name, a):
    plan = [(rel, 0, _whole, 1, lambda s_, r_: (_dev_of(s_),)) for rel in REL_ALL]
    local = [(0, _whole, 1, lambda me: (_dev_of(me),))]
    return _exchange(name, [a], 1, [[_sds((2 * N_CHIPS,) + a.shape, a.dtype)], (plan, local)])[0]


def _swap_halves_start(name, grads):
    n = len(grads)
    dsts = [_sds((g.shape[0], g.shape[1] // 2, g.shape[2]), g.dtype) for g in grads]
    plan = [(REL_SIBLING, t, (lambda s_, r_, rows=g.shape[1]: (pl.ds(0, N_CHIPS), _half_rows(rows, r_[2]))), n + t, _whole)
            for t, g in enumerate(grads)]
    return _exchange_start(name, grads, dsts, plan, [])


def _scatter_chips_plan(sums):
    n = len(sums)
    dsts = [_sds(a.shape, a.dtype) for a in sums]
    plan = [(rel, t, lambda s_, r_: (_chip_of(r_),), n + t, lambda s_, r_: (_chip_of(s_),))
            for t in range(n) for rel in REL_CHIPS]
    local = [(t, lambda me: (_chip_of(me),), n + t, lambda me: (_chip_of(me),)) for t in range(n)]
    return dsts, plan, local


def _scatter_chips(name, sums):
    dsts, plan, local = _scatter_chips_plan(sums)
    return _exchange(name, sums, len(sums), [dsts, (plan, local)])


def _scatter_chips_start(name, sums):
    dsts, plan, local = _scatter_chips_plan(sums)
    return _exchange_start(name, sums, dsts, plan, local)


def _swap_back_start(name, totals, layer):
    n = len(totals)
    plan = [(REL_SIBLING, n + t, (lambda s_, r_, rows=a.shape[1]: (layer, _half_rows(rows, s_[2]))),
             n + t, (lambda s_, r_, rows=a.shape[1]: (layer, _half_rows(rows, s_[2])))) for t, a in enumerate(totals)]
    return _exchange_start(name, totals, [], plan, [], inplace=True)


def _add_halves(name, g, recv, core):
    nch, half, c = recv.shape

    def body(core_ref, g_ref, r_ref, o_ref):
        o_ref[0] = (g_ref[0, 0].astype(F32) + r_ref[0].astype(F32)).astype(o_ref.dtype)

    spec = pltpu.PrefetchScalarGridSpec(
        num_scalar_prefetch=1, grid=(nch,),
        in_specs=[pl.BlockSpec((1, 1, half, c), lambda k, core_ref: (k, core_ref[0], 0, 0)),
                  pl.BlockSpec((1, half, c), lambda k, core_ref: (k, 0, 0))],
        out_specs=pl.BlockSpec((1, half, c), lambda k, core_ref: (k, 0, 0)))
    return pl.pallas_call(body, name=name, out_shape=_sds(recv.shape, recv.dtype), grid_spec=spec,
                          compiler_params=pltpu.CompilerParams(dimension_semantics=("parallel",),
                                                               vmem_limit_bytes=VMEM_LIMIT))(
                                                                   core, g.reshape(nch, 2, half, c), recv)


def _sum_chips(name, a, core, layer, total):
    nch, half, c = a.shape

    def body(core_ref, a_ref, *rest):
        o_ref = rest[-1]
        acc = a_ref[0].astype(F32)
        for k in range(1, nch):
            acc = acc + a_ref[k].astype(F32)
        o_ref[0, 0] = acc

    in_specs = [pl.BlockSpec((nch, half, c), lambda i, core_ref: (0, 0, 0))]
    args = [core, a]
    if total is not None:
        in_specs.append(pl.BlockSpec(memory_space=pl.ANY))
        args.append(total.reshape(DEPTH, 2, half, c))
    spec = pltpu.PrefetchScalarGridSpec(
        num_scalar_prefetch=1, grid=(1,), in_specs=in_specs,
        out_specs=pl.BlockSpec((1, 1, half, c), lambda i, core_ref: (layer, core_ref[0], 0, 0)))
    out = pl.pallas_call(body, name=name, out_shape=_sds((DEPTH, 2, half, c), F32), grid_spec=spec,
                         input_output_aliases={2: 0} if total is not None else {},
                         compiler_params=pltpu.CompilerParams(dimension_semantics=("arbitrary",),
                                                              vmem_limit_bytes=VMEM_LIMIT))(*args)
    return out.reshape(DEPTH, 2 * half, c)


def _adamw_update(w, g, m, v):
    mn = ADAM_B1 * m + (1.0 - ADAM_B1) * g
    vn = ADAM_B2 * v + (1.0 - ADAM_B2) * (g * g)
    m_hat = mn / (1.0 - ADAM_B1 ** ADAM_STEP)
    v_hat = vn / (1.0 - ADAM_B2 ** ADAM_STEP)
    return -ADAM_LR * (m_hat / (jnp.sqrt(v_hat) + ADAM_EPS) + ADAM_WD * w), mn, vn


def _adamw(name, w, g, m, v):
    depth, r, c = w.shape
    tr = next(t for t in (512, 448, 384, 352, 336, 256, 192, 128, 64, 32, 16, 8) if r % t == 0 and t * c <= ADAM_TILE_ELEMS)

    def body(w_ref, g_ref, m_ref, v_ref, go_ref, d_ref, mo_ref, vo_ref):
        gv = g_ref[...]
        go_ref[...] = gv
        d_ref[...], mo_ref[...], vo_ref[...] = _adamw_update(w_ref[...], gv, m_ref[...], v_ref[...])

    spec = pl.BlockSpec((1, tr, c), lambda l, i: (l, i, 0))
    o = _sds(w.shape, F32)
    return _call(body, name, (o, o, o, o), (depth, r // tr), [spec] * 4, (spec,) * 4, ("parallel", "parallel"))(w, g, m, v)


def _adamw_small(name, ws, gs, ms, vs):
    n = len(ws)

    def body(*refs):
        for t in range(n):
            w_ref, g_ref, m_ref, v_ref = (refs[k * n + t] for k in range(4))
            d_ref, mo_ref, vo_ref = (refs[(4 + k) * n + t] for k in range(3))
            d_ref[...], mo_ref[...], vo_ref[...] = _adamw_update(w_ref[...], g_ref[...], m_ref[...], v_ref[...])

    outs = [_sds(w.shape, F32) for w in ws]
    res = pl.pallas_call(body, name=name, out_shape=tuple(outs * 3))(*ws, *gs, *ms, *vs)
    return res[:n], res[n:2 * n], res[2 * n:]


def _pack(arrays, rows):
    flat = jnp.concatenate([a.reshape(-1).astype(F32) for a in arrays])
    return _pad_to(flat, 0, rows * LANES).reshape(rows, LANES)


def _unpack(packed, shapes):
    flat = packed.reshape(-1)
    out, off = [], 0
    for shp in shapes:
        n = 1
        for s_ in shp:
            n *= s_
        out.append(flat[off:off + n].reshape(shp))
        off += n
    return out


_MATRICES = ("w_in", "proj_a", "proj_b", "w_out", "w_gate", "w_up", "w_down")
_SMALL = (("norm1_w", (D_MODEL,)), ("sinks", (N_Q_HEADS,)), ("sgu_ln_w", (SGU_WIDTH,)), ("sgu_ln_b", (SGU_WIDTH,)),
          ("sgu_w", (SGU_GROUPS, SGU_CHUNK, SGU_CHUNK)), ("sgu_b", (SGU_GROUPS, SGU_CHUNK)), ("norm2_w", (D_MODEL,)),
          ("conv_w", (3, FFN_DIM)), ("conv_b", (FFN_DIM,)), ("final_norm_w", (D_MODEL,)))
SMALL_ROWS = 320
ADAM_TILE_ELEMS = 384 * 1024


def _reduce_cores_start(tag, partial):
    names = list(partial)
    pending, token = _swap_halves_start(tag + "_cores", [partial[k] for k in names])
    return dict(pending, tag=tag, names=names), token


def _reduce_chips_start(pending, core, after):
    tag, names = pending["tag"], pending["names"]
    mine, theirs = _exchange_wait(pending, after, both=True)
    sums = [_add_halves(f"{tag}_cores_add_{k}", g, r, core) for k, g, r in zip(names, mine, theirs)]
    scatter, token = _scatter_chips_start(tag + "_chips", sums)
    return dict(scatter, tag=tag, names=names), token


def _reduce_back_start(pending, l, core, totals, after):
    tag, names = pending["tag"], pending["names"]
    sums = [_sum_chips(f"{tag}_chips_add_{k}", a, core, l, totals.get(k)) for k, a in zip(names, _exchange_wait(pending, after))]
    back, _ = _swap_back_start(tag + "_back", sums, l)
    return dict(back, names=names)


def kernel(x, c, positions, ada_w, ada_b, norm1_w, w_in, attn_sinks, sgu_ln_w, sgu_ln_b, sgu_w, sgu_b, proj_a, proj_b, w_out, norm2_w, ffn_w_gate, ffn_w_up, ffn_conv_w, ffn_conv_b, ffn_w_down, final_norm_w, loss_target, m_ada_w, m_ada_b, m_norm1_w, m_w_in, m_attn_sinks, m_sgu_ln_w, m_sgu_ln_b, m_sgu_w, m_sgu_b, m_proj_a, m_proj_b, m_w_out, m_norm2_w, m_ffn_w_gate, m_ffn_w_up, m_ffn_conv_w, m_ffn_conv_b, m_ffn_w_down, m_final_norm_w, v_ada_w, v_ada_b, v_norm1_w, v_w_in, v_attn_sinks, v_sgu_ln_w, v_sgu_ln_b, v_sgu_w, v_sgu_b, v_proj_a, v_proj_b, v_w_out, v_norm2_w, v_ffn_w_gate, v_ffn_w_up, v_ffn_conv_w, v_ffn_conv_b, v_ffn_w_down, v_final_norm_w):
    d = D_MODEL
    ax, ay, ac = lax.axis_index("x"), lax.axis_index("y"), lax.axis_index("c")
    chip = 2 * ax + ay
    dev = 4 * ax + 2 * ay + ac
    core = ac.astype(jnp.int32).reshape(1)

    c_all = _gather_all("gather_cond", c.reshape(SUBLANES, d // SUBLANES)).reshape(2 * N_CHIPS, d)
    c_rows = _pad_to(c_all, 0, ADA_ROWS)
    ada_cols = ada_w.shape[2]
    ada_b_cols = lax.dynamic_slice_in_dim(ada_b, chip * ada_cols, ada_cols, axis=1).reshape(DEPTH, 1, ada_cols)
    mod_cols = _ada_fwd("ada_fwd", c_rows, ada_w, ada_b_cols)
    mod_all = _gather_chips("gather_mod", [mod_cols])[0]
    mod_mine = lax.dynamic_index_in_dim(mod_all, dev, axis=2, keepdims=False)
    mod_mine = mod_mine.transpose(1, 0, 2).reshape(DEPTH, 1, 6 * d)
    mods = [tuple(jnp.split(mod_mine[l], 6, axis=-1)) for l in range(DEPTH)]

    tr = lambda a: jnp.swapaxes(a, 1, 2)
    shards = [tr(w_in).astype(BF), proj_a.astype(BF), proj_b.astype(BF), w_out.astype(BF),
              _pad_to(tr(ffn_w_gate).astype(BF), 1, FFN_SHARD_PAD), _pad_to(tr(ffn_w_up).astype(BF), 1, FFN_SHARD_PAD),
              _pad_to(ffn_w_down.astype(BF), 1, FFN_SHARD_PAD), _pad_to(ffn_conv_w, 2, FFN_SHARD_PAD)]
    token = mod_all[0, 0, :SUBLANES, :LANES]
    fetches = []
    for l in range(DEPTH):
        groups = []
        for tag, members in (("in", shards[:1]), ("mix", shards[1:4]), ("ffn", shards[4:])):
            behind = (token[0, 0] * 0.0).astype(members[0].dtype)
            pending, token = _gather_weights_start(f"l{l}_gather_{tag}", [members[0][l] + behind] + [a[l] for a in members[1:]])
            groups.append(pending)
        fetches.append(groups)

    small_in = dict(norm1_w=norm1_w, sinks=attn_sinks, sgu_ln_w=sgu_ln_w, sgu_ln_b=sgu_ln_b, sgu_w=sgu_w, sgu_b=sgu_b,
                    norm2_w=norm2_w, conv_b=ffn_conv_b)
    cosf, sinf = _rope_tables(positions[0])
    small_of = lambda l: {k: v[l] for k, v in small_in.items()}

    h = x[0]
    saved, params = [], []
    for l in range(DEPTH):
        first, mix, ffn = fetches[l]
        w_in_l = _gather_weights_finish(first, token if l == 0 else h)
        late = lambda y, l=l, mix=mix: _mix_params(*_gather_weights_finish(mix, y), small_of(l))
        later = lambda y, l=l, ffn=ffn: _ffn_params(*_gather_weights_finish(ffn, y), small_of(l))
        h, sv, p = _layer_fwd(l, h, mods[l], _early_params(w_in_l[0], small_of(l)), cosf, sinf, late=late, later=later)
        saved.append(sv)
        params.append(p)
    dx, loss_part, d_final = _loss_head("loss_head", h, final_norm_w.reshape(1, d), loss_target[0])
    loss = lax.psum(loss_part[0, 0], ("x", "y", "c"))

    def small_pack(l, grads):
        cw, cb = _conv_grads_natural(grads)
        nat = dict(grads, conv_w=cw, conv_b=cb, final_norm_w=d_final if l == DEPTH - 1 else jnp.zeros((d,), F32))
        return _pack([nat[k] for k, _ in _SMALL], N_CHIPS * SMALL_ROWS).reshape(N_CHIPS, SMALL_ROWS, LANES)

    waiting, inflight = [], []

    def send(tag, partial):
        pending, token = _reduce_cores_start(tag, partial)
        waiting.append(pending)
        return token

    def tick(y):
        token = None
        while waiting:
            pending, token = _reduce_chips_start(waiting.pop(0), core, y)
            inflight.append(pending)
        return token

    dmods = [None] * DEPTH
    dx, dmods[1], grads = _layer_bwd(1, dx, mods[1], params[1], saved[1], cosf, sinf)
    token = send("l1_reduce", dict({k: grads[k] for k in _MATRICES}, small=small_pack(1, grads)))
    dx, dmods[0], grads = _layer_bwd(0, dx, mods[0], params[0], saved[0], cosf, sinf, after=token,
                                     emit=lambda part: send("l0_reduce_" + "_".join(part), part), tick=tick)
    dmod_mine = jnp.concatenate([jnp.concatenate(dmods[l], axis=1) for l in range(DEPTH)], axis=1)
    dmod_all = _gather_all("gather_dmod", dmod_mine.reshape(SUBLANES, -1)).reshape(2 * N_CHIPS, DEPTH * 6 * d)
    send("l0_reduce_in", dict(w_in=grads["w_in"], small=small_pack(0, grads) + dmod_all[0, 0] * 0.0))
    tick(dmod_all)

    totals, flying = {}, None

    def land(after):
        if flying is not None:
            totals.update(zip(flying["names"], _exchange_wait(flying, after)))

    for pending in inflight[:-1]:
        land(dx)
        flying = _reduce_back_start(pending, 1 if pending["tag"].startswith("l1") else 0, core, totals, dx)
    land(dx)
    flying = None

    g_ada_b = _colsum("ada_b_grad", dmod_all).reshape(DEPTH, 6 * d)
    dmod_cols = jnp.stack([lax.dynamic_slice_in_dim(dmod_all, l * 6 * d + chip * ada_cols, ada_cols, axis=1)
                           for l in range(DEPTH)])
    g_ada_w = _ada_bwd("ada_w_grad", c_rows, _pad_to(dmod_cols, 1, ADA_ROWS))
    big = dict(w_in=(tr(w_in), tr(m_w_in), tr(v_w_in)), proj_a=(proj_a, m_proj_a, v_proj_a), proj_b=(proj_b, m_proj_b, v_proj_b),
               w_out=(w_out, m_w_out, v_w_out), w_gate=(tr(ffn_w_gate), tr(m_ffn_w_gate), tr(v_ffn_w_gate)),
               w_up=(tr(ffn_w_up), tr(m_ffn_w_up), tr(v_ffn_w_up)), w_down=(ffn_w_down, m_ffn_w_down, v_ffn_w_down))
    upd, g_big = {}, {}

    def update(k):
        res = _adamw("adamw_" + k, big[k][0], totals[k], *big[k][1:])
        res = [tr(a) for a in res] if k in ("w_in", "w_gate", "w_up") else res
        g_big[k], upd[k] = res[0], res[1:]

    for k in ("w_down", "w_gate", "w_up", "w_out", "proj_a", "proj_b"):
        update(k)
    g_big["ada_w"], *upd["ada_w"] = _adamw("adamw_ada_w", ada_w, g_ada_w, m_ada_w, v_ada_w)
    flying = _reduce_back_start(inflight[-1], 0, core, totals, upd["ada_w"][0])
    land(upd["ada_w"][1])
    update("w_in")

    small_all = _gather_chips("gather_small", [totals["small"]])[0]
    small_g = small_all.transpose(1, 0, 2, 3).reshape(DEPTH, -1)
    per_layer = [_unpack(small_g[l], [shp for _, shp in _SMALL]) for l in range(DEPTH)]
    sg = {k: jnp.stack([per_layer[l][i] for l in range(DEPTH)]) for i, (k, _) in enumerate(_SMALL)}
    g_final = sg["final_norm_w"][DEPTH - 1]
    g_conv_w = lax.dynamic_slice_in_dim(sg["conv_w"], chip * FFN_SHARD, FFN_SHARD, axis=2)

    rest = [("ada_b", ada_b, g_ada_b, m_ada_b, v_ada_b), ("norm1_w", norm1_w, sg["norm1_w"], m_norm1_w, v_norm1_w),
            ("attn_sinks", attn_sinks, sg["sinks"], m_attn_sinks, v_attn_sinks),
            ("sgu_ln_w", sgu_ln_w, sg["sgu_ln_w"], m_sgu_ln_w, v_sgu_ln_w),
            ("sgu_ln_b", sgu_ln_b, sg["sgu_ln_b"], m_sgu_ln_b, v_sgu_ln_b), ("sgu_w", sgu_w, sg["sgu_w"], m_sgu_w, v_sgu_w),
            ("sgu_b", sgu_b, sg["sgu_b"], m_sgu_b, v_sgu_b), ("norm2_w", norm2_w, sg["norm2_w"], m_norm2_w, v_norm2_w),
            ("ffn_conv_w", ffn_conv_w, g_conv_w, m_ffn_conv_w, v_ffn_conv_w),
            ("ffn_conv_b", ffn_conv_b, sg["conv_b"], m_ffn_conv_b, v_ffn_conv_b),
            ("final_norm_w", final_norm_w.reshape(1, d), g_final.reshape(1, d), m_final_norm_w.reshape(1, d),
             v_final_norm_w.reshape(1, d))]
    rest_out = _adamw_small("adamw_rest", *[[r[i] for r in rest] for i in (1, 2, 3, 4)])
    g_rest = {r[0]: r[2] for r in rest}
    u_rest = {r[0]: tuple(o[i] for o in rest_out) for i, r in enumerate(rest)}
    g_rest["final_norm_w"] = g_final
    u_rest["final_norm_w"] = tuple(a.reshape(d) for a in u_rest["final_norm_w"])

    names = ("ada_w", "ada_b", "norm1_w", "w_in", "attn_sinks", "sgu_ln_w", "sgu_ln_b", "sgu_w", "sgu_b", "proj_a", "proj_b",
             "w_out", "norm2_w", "ffn_w_gate", "ffn_w_up", "ffn_conv_w", "ffn_conv_b", "ffn_w_down", "final_norm_w")
    alias = {"ffn_w_gate": "w_gate", "ffn_w_up": "w_up", "ffn_w_down": "w_down"}
    grad_of = lambda n: g_rest[n] if n in g_rest else g_big[alias.get(n, n)]
    upd_of = lambda n: u_rest[n] if n in u_rest else upd[alias.get(n, n)]
    return (loss, dx[None], *[grad_of(n) for n in names], *[upd_of(n)[0] for n in names],
            *[upd_of(n)[1] for n in names], *[upd_of(n)[2] for n in names])
```

```python
import jax
import jax.numpy as jnp
from jax import lax
from jax.experimental import pallas as pl
from jax.experimental.pallas import tpu as pltpu

F32 = jnp.float32
BF = jnp.bfloat16

D_MODEL = 1024
N_Q_HEADS = 16
N_KV_HEADS = 2
HEAD_DIM = 64
ATTN_BLOCK = 128
ROPE_THETA = 500000.0
ROT_DIM = HEAD_DIM // 4
SGU_WIDTH = 1024
SGU_GROUPS = 8
SGU_CHUNK = 128
FFN_DIM = 2816
NORM_EPS = 1e-6
DEPTH = 2
IN_COLS = 5376
N_CHIPS = 4
FFN_SHARD = FFN_DIM // N_CHIPS
FFN_SHARD_PAD = 768
FFN_PAD = N_CHIPS * FFN_SHARD_PAD
LANES = 128
SUBLANES = 8
HALO = 16
VMEM_LIMIT = 56 * 1024 * 1024
NEG_BIG = -1e30

ADAM_LR = 0.001
ADAM_B1 = 0.9
ADAM_B2 = 0.999
ADAM_EPS = 1e-08
ADAM_WD = 0.01
ADAM_STEP = 10

MESH = pl.DeviceIdType.MESH

Q_END = 1024
KV_END = 1280
U_END = 2304
Z_END = 3328
GA_END = 4352


def _sds(shape, dtype):
    return jax.ShapeDtypeStruct(tuple(shape), dtype)


def _call(body, name, out_shape, grid, in_specs, out_specs, semantics, scratch=(), after=None):
    n_in = len(in_specs)
    fn = body
    if after is not None:
        def fn(*refs):
            return body(*refs[:n_in], *refs[n_in + 1:])

        in_specs = list(in_specs) + [pl.BlockSpec(memory_space=pl.ANY)]
    call = pl.pallas_call(
        fn, name=name, out_shape=out_shape, grid=grid, in_specs=in_specs, out_specs=out_specs,
        scratch_shapes=scratch,
        compiler_params=pltpu.CompilerParams(dimension_semantics=semantics, vmem_limit_bytes=VMEM_LIMIT))
    if after is None:
        return call
    return lambda *args: call(*args, after)


def _rows(tm, width, col=0):
    return pl.BlockSpec((tm, width), lambda i: (i, col))


def _vec(width):
    return pl.BlockSpec((1, width), lambda i: (0, 0))


def _resident(shape):
    zeros = (0,) * len(shape)
    return pl.BlockSpec(tuple(shape), lambda *_: zeros, pipeline_mode=pl.Buffered(1))


def _sigmoid(x):
    return 0.5 + 0.5 * jnp.tanh(0.5 * x)


def _gelu(x):
    return 0.5 * x * (1.0 + lax.erf(x * 0.7071067811865476))


def _gelu_grad(x):
    cdf = 0.5 * (1.0 + lax.erf(x * 0.7071067811865476))
    return cdf + x * jnp.exp(-0.5 * x * x) * 0.3989422804014327


def _dot(a, b):
    return jnp.dot(a, b, preferred_element_type=F32)


def _dot_nt(a, b):
    return lax.dot_general(a, b, (((1,), (1,)), ((), ())), preferred_element_type=F32)


def _dot_tn(a, b):
    return lax.dot_general(a, b, (((0,), (0,)), ((), ())), preferred_element_type=F32)


def _rms(xv):
    return lax.rsqrt(jnp.mean(xv * xv, axis=-1, keepdims=True) + NORM_EPS)


def _matmul_tn(name, a, b, tk=512, tn=1024, blocked=False, after=None):
    s, k = a.shape
    n = b.shape[1]
    tk, tn = min(tk, k), min(tn, n)

    def body(a_ref, b_ref, o_ref):
        res = _dot_tn(a_ref[...], b_ref[...]).astype(o_ref.dtype)
        if blocked:
            o_ref[0] = res
        else:
            o_ref[...] = res

    if blocked:
        out, ospec = _sds((n // tn, k, tn), BF), pl.BlockSpec((1, tk, tn), lambda i, j: (j, i, 0))
    else:
        out, ospec = _sds((k, n), BF), pl.BlockSpec((tk, tn), lambda i, j: (i, j))
    return _call(body, name, out, (k // tk, n // tn),
                 [pl.BlockSpec((s, tk), lambda i, j: (0, i)), pl.BlockSpec((s, tn), lambda i, j: (0, j))],
                 ospec, ("parallel", "parallel"), after=after)(a, b)


def _matmul_tn_rows(name, a, b, out, row0, rows_total, tk=256):
    s, k = a.shape
    n = b.shape[1]

    def body(a_ref, b_ref, *rest):
        rest[-1][...] = _dot_tn(a_ref[...], b_ref[...]).astype(BF)

    in_specs = [pl.BlockSpec((s, tk), lambda i: (0, i)), _resident(b.shape)]
    args = [a, b]
    if out is not None:
        in_specs.append(pl.BlockSpec(memory_space=pl.ANY))
        args.append(out)
    return pl.pallas_call(
        body, name=name, out_shape=_sds((rows_total, n), BF), grid=(k // tk,), in_specs=in_specs,
        out_specs=pl.BlockSpec((tk, n), lambda i: (row0 // tk + i, 0)),
        input_output_aliases={2: 0} if out is not None else {},
        compiler_params=pltpu.CompilerParams(dimension_semantics=("parallel",), vmem_limit_bytes=VMEM_LIMIT))(*args)


def _rope_partner(v):
    lane = lax.broadcasted_iota(jnp.int32, (1, LANES), 1) % HEAD_DIM
    return jnp.where(lane < ROT_DIM // 2, pltpu.roll(v, LANES - ROT_DIM // 2, axis=1), pltpu.roll(v, ROT_DIM // 2, axis=1))


def _dup_half(v, half):
    lane = lax.broadcasted_iota(jnp.int32, (1, LANES), 1)
    keep = jnp.where((lane >= HEAD_DIM) == (half == 1), v, 0.0)
    return keep + pltpu.roll(keep, HEAD_DIM, axis=1)


def _in_proj(name, x, w, sc, sh, w_in, cosf, sinf, tm=256, after=None):
    s, d = x.shape
    tm = min(tm, s)

    def body(x_ref, w_ref, sc_ref, sh_ref, win_ref, cos_ref, sin_ref,
             h_ref, qr_ref, kk0_ref, kk1_ref, vv0_ref, vv1_ref, u_ref, v_ref, ga_ref, gb_ref):
        xv = x_ref[...]
        h = ((xv * _rms(xv)) * w_ref[...] * (1.0 + sc_ref[...]) + sh_ref[...]).astype(BF)
        h_ref[...] = h
        cosv, sinv = cos_ref[...], sin_ref[...]
        q = _dot_nt(h, win_ref[:Q_END, :])
        for j in range(D_MODEL // LANES):
            qv = q[:, j * LANES:(j + 1) * LANES]
            qr_ref[:, j * LANES:(j + 1) * LANES] = (qv * cosv + _rope_partner(qv) * sinv).astype(BF)
        kv = _dot_nt(h, win_ref[Q_END:KV_END, :])
        kr = kv[:, :LANES] * cosv + _rope_partner(kv[:, :LANES]) * sinv
        vv = kv[:, LANES:]
        kk0_ref[...] = _dup_half(kr, 0).astype(BF)
        kk1_ref[...] = _dup_half(kr, 1).astype(BF)
        vv0_ref[...] = _dup_half(vv, 0).astype(BF)
        vv1_ref[...] = _dup_half(vv, 1).astype(BF)
        u_ref[...] = _dot_nt(h, win_ref[KV_END:U_END, :])
        v_ref[...] = _dot_nt(h, win_ref[U_END:Z_END, :])
        ga_ref[...] = _dot_nt(h, win_ref[Z_END:GA_END, :]).astype(BF)
        gb_ref[...] = _dot_nt(h, win_ref[GA_END:, :]).astype(BF)

    wide, kvs, pre = _sds((s, d), BF), _sds((s, LANES), BF), _sds((s, d), F32)
    return _call(body, name, (wide, wide, kvs, kvs, kvs, kvs, pre, pre, wide, wide), (s // tm,),
                 [_rows(tm, d), _vec(d), _vec(d), _vec(d), _resident(w_in.shape), _rows(tm, LANES), _rows(tm, LANES)],
                 (_rows(tm, d), _rows(tm, d)) + (_rows(tm, LANES),) * 4 + (_rows(tm, d),) * 4, ("parallel",), after=after)(
                     x, w, sc, sh, w_in, cosf, sinf)


def _in_proj_bwd(name, dq, dkv, du, dv, dga, dgb, w_in, x, w, sc, dx_in, tm=256):
    s, d = x.shape
    tm = min(tm, s)

    def body(dq_ref, dkv_ref, du_ref, dv_ref, dga_ref, dgb_ref, win_ref, x_ref, w_ref, sc_ref, dxin_ref,
             dx_ref, da_ref, dsh_ref):
        @pl.when(pl.program_id(0) == 0)
        def _():
            da_ref[...] = jnp.zeros_like(da_ref)
            dsh_ref[...] = jnp.zeros_like(dsh_ref)

        dh = (_dot(dq_ref[...], win_ref[:Q_END, :]) + _dot(dkv_ref[...], win_ref[Q_END:KV_END, :])
              + _dot(du_ref[...], win_ref[KV_END:U_END, :]) + _dot(dv_ref[...], win_ref[U_END:Z_END, :])
              + _dot(dga_ref[...], win_ref[Z_END:GA_END, :]) + _dot(dgb_ref[...], win_ref[GA_END:, :]))
        xv = x_ref[...]
        r = _rms(xv)
        xn = xv * r
        dxn = dh * (w_ref[...] * (1.0 + sc_ref[...]))
        dx_ref[...] = dxin_ref[...] + r * (dxn - xn * jnp.mean(dxn * xn, axis=-1, keepdims=True))
        da_ref[...] += jnp.sum(dh * xn, axis=0, keepdims=True)
        dsh_ref[...] += jnp.sum(dh, axis=0, keepdims=True)

    return _call(body, name, (_sds((s, d), F32), _sds((1, d), F32), _sds((1, d), F32)), (s // tm,),
                 [_rows(tm, d), _rows(tm, 2 * LANES), _rows(tm, d), _rows(tm, d), _rows(tm, d), _rows(tm, d),
                  _resident(w_in.shape), _rows(tm, d), _vec(d), _vec(d), _rows(tm, d)],
                 (_rows(tm, d), _vec(d), _vec(d)), ("arbitrary",))(dq, dkv, du, dv, dga, dgb, w_in, x, w, sc, dx_in)


def _rope_bwd(name, dqr, dkv_cur, dkv_prev, cosf, sinf, tm=512):
    s = dqr.shape[0]
    tm = min(tm, s)
    steps = s // tm
    per = tm // ATTN_BLOCK
    nb = s // ATTN_BLOCK

    def unrope(v, cosv, sinv):
        return v * cosv - _rope_partner(v) * sinv

    def body(dq_ref, cur_ref, prev_ref, next_ref, cos_ref, sin_ref, dqo_ref, dkvo_ref):
        i = pl.program_id(0)
        cosv, sinv = cos_ref[...], sin_ref[...]
        for j in range(D_MODEL // LANES):
            dqo_ref[:, j * LANES:(j + 1) * LANES] = unrope(dq_ref[:, j * LANES:(j + 1) * LANES], cosv, sinv).astype(BF)
        nxt = jnp.where(i < steps - 1, next_ref[...], 0.0)
        if per > 1:
            shifted = jnp.concatenate([prev_ref[ATTN_BLOCK:, :], nxt], axis=0)
        else:
            shifted = nxt
        tot = cur_ref[...] + shifted
        dkvo_ref[:, :LANES] = unrope(tot[:, :LANES], cosv, sinv).astype(BF)
        dkvo_ref[:, LANES:] = tot[:, LANES:].astype(BF)

    nxt_spec = pl.BlockSpec((ATTN_BLOCK, 2 * LANES), lambda i: (jnp.minimum((i + 1) * per, nb - 1), 0))
    return _call(body, name, (_sds((s, D_MODEL), BF), _sds((s, 2 * LANES), BF)), (steps,),
                 [_rows(tm, D_MODEL), _rows(tm, 2 * LANES), _rows(tm, 2 * LANES), nxt_spec, _rows(tm, LANES),
                  _rows(tm, LANES)],
                 (_rows(tm, D_MODEL), _rows(tm, 2 * LANES)), ("parallel",))(dqr, dkv_cur, dkv_prev, dkv_prev, cosf, sinf)


Q_PER_KV = N_Q_HEADS // N_KV_HEADS


def _band_mask_t(n):
    kj = lax.broadcasted_iota(jnp.int32, (2 * ATTN_BLOCK, ATTN_BLOCK), 0)
    qi = lax.broadcasted_iota(jnp.int32, (2 * ATTN_BLOCK, ATTN_BLOCK), 1)
    return (kj > qi) & (kj <= qi + ATTN_BLOCK) & ((n > 0) | (kj >= ATTN_BLOCK))


def _softmax_t(raw, allowed, sink):
    sc = jnp.where(allowed, raw * (HEAD_DIM ** -0.5), NEG_BIG)
    m = jnp.maximum(jnp.max(sc, axis=0, keepdims=True), sink)
    p = jnp.exp(sc - m)
    esink = jnp.exp(sink - m)
    inv = 1.0 / (jnp.sum(p, axis=0, keepdims=True) + esink)
    return p * inv, esink * inv


def _kv_specs():
    cur = pl.BlockSpec((ATTN_BLOCK, LANES), lambda n: (n, 0))
    prev = pl.BlockSpec((ATTN_BLOCK, LANES), lambda n: (jnp.maximum(n - 1, 0), 0))
    return [prev, cur] * 4


def _attention(name, qr, kk0, kk1, vv0, vv1, sinks):
    s = qr.shape[0]
    nb = s // ATTN_BLOCK

    def body(sink_ref, q_ref, k0p, k0c, k1p, k1c, v0p, v0c, v1p, v1c, y_ref):
        allowed = _band_mask_t(pl.program_id(0))
        upper = lax.broadcasted_iota(jnp.int32, (1, LANES), 1) >= HEAD_DIM
        upper_rows = lax.broadcasted_iota(jnp.int32, (LANES, 1), 0) >= HEAD_DIM
        bands = ((jnp.concatenate([k0p[...], k0c[...]], axis=0), jnp.concatenate([v0p[...], v0c[...]], axis=0)),
                 (jnp.concatenate([k1p[...], k1c[...]], axis=0), jnp.concatenate([v1p[...], v1c[...]], axis=0)))
        vbts = (bands[0][1].T, bands[1][1].T)

        def scores(h):
            hk, j, half = h // Q_PER_KV, (h % Q_PER_KV) // 2, h % 2
            col = (hk * 4 + j) * LANES
            qp = q_ref[:, col:col + LANES]
            return _dot_nt(bands[hk][0], jnp.where(upper if half else jnp.logical_not(upper), qp, jnp.zeros_like(qp)))

        out_t = None
        ahead = scores(0)
        for h in range(N_Q_HEADS):
            hk, j, half = h // Q_PER_KV, (h % Q_PER_KV) // 2, h % 2
            raw = ahead
            if h + 1 < N_Q_HEADS:
                ahead = scores(h + 1)
            pn, _ = _softmax_t(raw, allowed, sink_ref[h])
            o_h = _dot(vbts[hk], pn.astype(BF))
            out_t = jnp.where(upper_rows, o_h, out_t) if half else o_h
            if half:
                col = (hk * 4 + j) * LANES
                y_ref[:, col:col + LANES] = out_t.T.astype(BF)

    return _call(body, name, _sds((s, D_MODEL), BF), (nb,),
                 [pl.BlockSpec(memory_space=pltpu.SMEM), pl.BlockSpec((ATTN_BLOCK, D_MODEL), lambda n: (n, 0))] + _kv_specs(),
                 pl.BlockSpec((ATTN_BLOCK, D_MODEL), lambda n: (n, 0)), ("parallel",))(
                     sinks, qr, kk0, kk0, kk1, kk1, vv0, vv0, vv1, vv1)


def _attention_bwd(name, qr, kk0, kk1, vv0, vv1, sinks, dy, after=None):
    s = qr.shape[0]
    nb = s // ATTN_BLOCK

    def body(sink_ref, q_ref, dy_ref, k0p, k0c, k1p, k1c, v0p, v0c, v1p, v1c, dq_ref, cur_ref, prev_ref, dsink_ref):
        @pl.when(pl.program_id(0) == 0)
        def _():
            dsink_ref[...] = jnp.zeros_like(dsink_ref)

        allowed = _band_mask_t(pl.program_id(0))
        lane = lax.broadcasted_iota(jnp.int32, (1, LANES), 1)
        upper = lane >= HEAD_DIM
        upper_rows = lax.broadcasted_iota(jnp.int32, (LANES, 1), 0) >= HEAD_DIM
        bands = ((jnp.concatenate([k0p[...], k0c[...]], axis=0), jnp.concatenate([v0p[...], v0c[...]], axis=0)),
                 (jnp.concatenate([k1p[...], k1c[...]], axis=0), jnp.concatenate([v1p[...], v1c[...]], axis=0)))
        kbts = (bands[0][0].T, bands[1][0].T)

        def scores(h):
            hk, j, half = h // Q_PER_KV, (h % Q_PER_KV) // 2, h % 2
            kb, vb = bands[hk]
            col = (hk * 4 + j) * LANES
            sel = upper if half else jnp.logical_not(upper)
            qp = q_ref[:, col:col + LANES]
            qa = jnp.where(sel, qp, jnp.zeros_like(qp))
            dya = jnp.where(sel, dy_ref[:, col:col + LANES], 0.0).astype(BF)
            return qa, dya, _dot_nt(kb, qa), _dot_nt(vb, dya)

        dsink = jnp.zeros((1, LANES), F32)
        dk_slab = jnp.zeros((2 * ATTN_BLOCK, LANES), F32)
        dv_slab = jnp.zeros((2 * ATTN_BLOCK, LANES), F32)
        dkk = dvv = dq_t = None
        ahead = scores(0)
        for h in range(N_Q_HEADS):
            hk, j, half = h // Q_PER_KV, (h % Q_PER_KV) // 2, h % 2
            qa, dya, raw, dp = ahead
            if h + 1 < N_Q_HEADS:
                ahead = scores(h + 1)
            pn, psink = _softmax_t(raw, allowed, sink_ref[h])
            delta = jnp.sum(pn * dp, axis=0, keepdims=True)
            ds = (pn * (dp - delta) * (HEAD_DIM ** -0.5)).astype(BF)
            dsink = dsink + jnp.where(lane == h, -jnp.sum(psink * delta), 0.0)
            dq_h = _dot(kbts[hk], ds)
            dq_t = jnp.where(upper_rows, dq_h, dq_t) if half else dq_h
            dk_h, dv_h = _dot(ds, qa), _dot(pn.astype(BF), dya)
            dkk, dvv = (dk_h, dv_h) if h % Q_PER_KV == 0 else (dkk + dk_h, dvv + dv_h)
            if half:
                col = (hk * 4 + j) * LANES
                dq_ref[:, col:col + LANES] = dq_t.T
            if h % Q_PER_KV == Q_PER_KV - 1:
                mine = upper if hk else jnp.logical_not(upper)
                dk_slab = jnp.where(mine, dkk + pltpu.roll(dkk, HEAD_DIM, axis=1), dk_slab)
                dv_slab = jnp.where(mine, dvv + pltpu.roll(dvv, HEAD_DIM, axis=1), dv_slab)
        prev_ref[:, :LANES] = dk_slab[:ATTN_BLOCK]
        prev_ref[:, LANES:] = dv_slab[:ATTN_BLOCK]
        cur_ref[:, :LANES] = dk_slab[ATTN_BLOCK:]
        cur_ref[:, LANES:] = dv_slab[ATTN_BLOCK:]
        dsink_ref[...] += dsink

    blk = pl.BlockSpec((ATTN_BLOCK, D_MODEL), lambda n: (n, 0))
    kvo = pl.BlockSpec((ATTN_BLOCK, 2 * LANES), lambda n: (n, 0))
    return _call(body, name,
                 (_sds((s, D_MODEL), F32), _sds((s, 2 * LANES), F32), _sds((s, 2 * LANES), F32), _sds((1, LANES), F32)),
                 (nb,), [pl.BlockSpec(memory_space=pltpu.SMEM), blk, blk] + _kv_specs(),
                 (blk, kvo, kvo, pl.BlockSpec((1, LANES), lambda n: (0, 0))), ("arbitrary",), after=after)(
                     sinks, qr, dy, kk0, kk0, kk1, kk1, vv0, vv0, vv1, vv1)


def _sgu_weights(wm_ref, g):
    t = lax.broadcasted_iota(jnp.int32, (SGU_CHUNK, SGU_CHUNK), 0)
    sidx = lax.broadcasted_iota(jnp.int32, (SGU_CHUNK, SGU_CHUNK), 1)
    return jnp.where(sidx <= t, wm_ref[g], 0.0).astype(BF)


def _layer_norm_stats(v):
    mu = jnp.mean(v, axis=-1, keepdims=True)
    cen = v - mu
    rstd = lax.rsqrt(jnp.mean(cen * cen, axis=-1, keepdims=True) + NORM_EPS)
    return cen * rstd, rstd


def _sgu(name, u_pre, v_pre, ln_w, ln_b, wm, bfull, tm=256):
    s, w = u_pre.shape
    tm = min(tm, s)

    def body(u_ref, v_ref, lw_ref, lb_ref, wm_ref, b_ref, y_ref):
        vhat, _ = _layer_norm_stats(_gelu(v_ref[...]))
        vn = (vhat * lw_ref[...] + lb_ref[...]).astype(BF)
        for g in range(SGU_GROUPS):
            wg = _sgu_weights(wm_ref, g)
            cols = slice(g * SGU_CHUNK, (g + 1) * SGU_CHUNK)
            for ch in range(tm // SGU_CHUNK):
                rows = slice(ch * SGU_CHUNK, (ch + 1) * SGU_CHUNK)
                f = _dot(wg, vn[rows, cols]) + b_ref[g]
                y_ref[rows, cols] = (_gelu(u_ref[rows, cols]) * f).astype(BF)

    full3 = pl.BlockSpec((SGU_GROUPS, SGU_CHUNK, SGU_CHUNK), lambda i: (0, 0, 0))
    return _call(body, name, _sds((s, w), BF), (s // tm,),
                 [_rows(tm, w), _rows(tm, w), _vec(w), _vec(w), full3, full3],
                 _rows(tm, w), ("parallel",))(u_pre, v_pre, ln_w, ln_b, wm, bfull)


def _sgu_bwd(name, u_pre, v_pre, ln_w, ln_b, wm, bfull, dy, tm=256, after=None):
    s, w = u_pre.shape
    tm = min(tm, s)
    steps = s // tm

    def body(u_ref, v_ref, lw_ref, lb_ref, wm_ref, b_ref, dy_ref, du_ref, dv_ref, dwm_ref, db_ref, dlw_ref, dlb_ref,
             dfsum_ref):
        i = pl.program_id(0)

        @pl.when(i == 0)
        def _():
            dwm_ref[...] = jnp.zeros_like(dwm_ref)
            dlw_ref[...] = jnp.zeros_like(dlw_ref)
            dlb_ref[...] = jnp.zeros_like(dlb_ref)
            dfsum_ref[...] = jnp.zeros_like(dfsum_ref)

        vpre = v_ref[...]
        vhat, rstd = _layer_norm_stats(_gelu(vpre))
        vn = (vhat * lw_ref[...] + lb_ref[...]).astype(BF)
        t = lax.broadcasted_iota(jnp.int32, (SGU_CHUNK, SGU_CHUNK), 0)
        sidx = lax.broadcasted_iota(jnp.int32, (SGU_CHUNK, SGU_CHUNK), 1)
        dvn_cols = []
        for g in range(SGU_GROUPS):
            wg = _sgu_weights(wm_ref, g)
            cols = slice(g * SGU_CHUNK, (g + 1) * SGU_CHUNK)
            dvn_rows = []
            dwg = jnp.zeros((SGU_CHUNK, SGU_CHUNK), F32)
            dfs = jnp.zeros((SGU_CHUNK, SGU_CHUNK), F32)
            for ch in range(tm // SGU_CHUNK):
                rows = slice(ch * SGU_CHUNK, (ch + 1) * SGU_CHUNK)
                upre = u_ref[rows, cols]
                dyv = dy_ref[rows, cols].astype(F32)
                f = _dot(wg, vn[rows, cols]) + b_ref[g]
                du_ref[rows, cols] = (dyv * f * _gelu_grad(upre)).astype(BF)
                df = dyv * _gelu(upre)
                dfb = df.astype(BF)
                dvn_rows.append(_dot_tn(wg, dfb))
                dwg = dwg + _dot_nt(dfb, vn[rows, cols])
                dfs = dfs + df
            dwm_ref[g] += jnp.where(sidx <= t, dwg, 0.0)
            dfsum_ref[g] += dfs
            dvn_cols.append(jnp.concatenate(dvn_rows, axis=0) if len(dvn_rows) > 1 else dvn_rows[0])
        dvn = jnp.concatenate(dvn_cols, axis=1)
        dlw_ref[...] += jnp.sum(dvn * vhat, axis=0, keepdims=True)
        dlb_ref[...] += jnp.sum(dvn, axis=0, keepdims=True)
        dvh = dvn * lw_ref[...]
        dvg = rstd * (dvh - jnp.mean(dvh, axis=-1, keepdims=True) - vhat * jnp.mean(dvh * vhat, axis=-1, keepdims=True))
        dv_ref[...] = (dvg * _gelu_grad(vpre)).astype(BF)

        @pl.when(i == steps - 1)
        def _():
            for g in range(SGU_GROUPS):
                db_ref[g:g + 1, :] = jnp.sum(dfsum_ref[g].T, axis=0, keepdims=True)

    full3 = pl.BlockSpec((SGU_GROUPS, SGU_CHUNK, SGU_CHUNK), lambda i: (0, 0, 0))
    return _call(body, name,
                 (_sds((s, w), BF), _sds((s, w), BF), _sds((SGU_GROUPS, SGU_CHUNK, SGU_CHUNK), F32),
                  _sds((SGU_GROUPS, SGU_CHUNK), F32), _sds((1, w), F32), _sds((1, w), F32)),
                 (steps,),
                 [_rows(tm, w), _rows(tm, w), _vec(w), _vec(w), full3, full3, _rows(tm, w)],
                 (_rows(tm, w), _rows(tm, w), full3, pl.BlockSpec((SGU_GROUPS, SGU_CHUNK), lambda i: (0, 0)), _vec(w), _vec(w)),
                 ("arbitrary",), scratch=[pltpu.VMEM((SGU_GROUPS, SGU_CHUNK, SGU_CHUNK), F32)], after=after)(
                     u_pre, v_pre, ln_w, ln_b, wm, bfull, dy)


def _mix_out(name, y_sgu, y_attn, ga_pre, gb_pre, x, g1, proj_a, proj_b, w_out, w2, sc2, sh2, tm=256):
    s, d = x.shape
    tm = min(tm, s)

    def body(ys_ref, ya_ref, ga_ref, gb_ref, x_ref, g1_ref, wa_ref, wb_ref, wo_ref, w2_ref, sc2_ref, sh2_ref,
             m_ref, pa_ref, pb_ref, o_ref, x1_ref, h2_ref):
        pa = _dot(ys_ref[...], wa_ref[...].reshape(d, d))
        pb = _dot(ya_ref[...], wb_ref[...].reshape(d, d))
        pa_ref[...] = pa.astype(BF)
        pb_ref[...] = pb.astype(BF)
        merged = (_sigmoid(ga_ref[...].astype(F32)) * pa + _sigmoid(gb_ref[...].astype(F32)) * pb).astype(BF)
        m_ref[...] = merged
        o = _dot(merged, wo_ref[...].reshape(d, d))
        o_ref[...] = o.astype(BF)
        x1 = x_ref[...] + g1_ref[...] * o
        x1_ref[...] = x1
        h2_ref[...] = ((x1 * _rms(x1)) * w2_ref[...] * (1.0 + sc2_ref[...]) + sh2_ref[...]).astype(BF)

    f, b = _sds((s, d), F32), _sds((s, d), BF)
    r = _rows(tm, d)
    wspec = _resident(proj_a.shape)
    return _call(body, name, (b, b, b, b, f, b), (s // tm,),
                 [r, r, r, r, r, _vec(d), wspec, wspec, wspec, _vec(d), _vec(d), _vec(d)], (r,) * 6, ("parallel",))(
                     y_sgu, y_attn, ga_pre, gb_pre, x, g1, proj_a, proj_b, w_out, w2, sc2, sh2)


def _mix_bwd(name, do, w_out, proj_a, proj_b, ga_pre, gb_pre, pa, pb, tm=256):
    s, d = do.shape
    tm = min(tm, s)

    def body(do_ref, wo_ref, wa_ref, wb_ref, ga_ref, gb_ref, pa_ref, pb_ref,
             dpa_ref, dpb_ref, dga_ref, dgb_ref, dys_ref, dya_ref):
        dm = _dot_nt(do_ref[...], wo_ref[...].reshape(d, d))
        ga = _sigmoid(ga_ref[...].astype(F32))
        gb = _sigmoid(gb_ref[...].astype(F32))
        dpa = (dm * ga).astype(BF)
        dpb = (dm * gb).astype(BF)
        dpa_ref[...] = dpa
        dpb_ref[...] = dpb
        dga_ref[...] = (dm * pa_ref[...].astype(F32) * ga * (1.0 - ga)).astype(BF)
        dgb_ref[...] = (dm * pb_ref[...].astype(F32) * gb * (1.0 - gb)).astype(BF)
        dys_ref[...] = _dot_nt(dpa, wa_ref[...].reshape(d, d)).astype(BF)
        dya_ref[...] = _dot_nt(dpb, wb_ref[...].reshape(d, d)).astype(BF)

    f, b = _sds((s, d), F32), _sds((s, d), BF)
    r = _rows(tm, d)
    wspec = _resident(w_out.shape)
    return _call(body, name, (b, b, b, b, b, b), (s // tm,), [r, wspec, wspec, wspec, r, r, r, r], (r,) * 6,
                 ("parallel",))(do, w_out, proj_a, proj_b, ga_pre, gb_pre, pa, pb)


def _ffn_up_act(name, h2, w_gate, w_up, cw, cb, tm=1024):
    s, d = h2.shape
    tm = min(tm, s)
    tc = FFN_SHARD_PAD
    per = tm // HALO

    def body(h_ref, hprev_ref, wg_ref, wu_ref, cw_ref, cb_ref, a_ref, ac_ref, up_ref, hf_ref):
        hv = h_ref[...]
        a = _dot_nt(hv, wg_ref[0]).astype(BF)
        up = _dot_nt(hv, wu_ref[0]).astype(BF)
        a_ref[...] = a
        up_ref[...] = up
        prev = jnp.where(pl.program_id(1) > 0, _dot_nt(hprev_ref[...], wg_ref[0]).astype(BF).astype(F32), 0.0)
        ext = jnp.concatenate([prev, a.astype(F32)], axis=0)
        ac = (cb_ref[...] + cw_ref[0:1, :] * pltpu.roll(ext, 2, axis=0) + cw_ref[1:2, :] * pltpu.roll(ext, 1, axis=0)
              + cw_ref[2:3, :] * ext)[HALO:]
        ac_ref[...] = ac.astype(BF)
        hf_ref[...] = (ac * _sigmoid(ac) * up.astype(F32)).astype(BF)

    wspec = pl.BlockSpec((1, tc, d), lambda j, i: (j, 0, 0))
    ospec = pl.BlockSpec((tm, tc), lambda j, i: (i, j))
    o = _sds((s, FFN_PAD), BF)
    return _call(body, name, (o, o, o, o), (N_CHIPS, s // tm),
                 [pl.BlockSpec((tm, d), lambda j, i: (i, 0)), pl.BlockSpec((HALO, d), lambda j, i: (jnp.maximum(i * per - 1, 0), 0)),
                  wspec, wspec, pl.BlockSpec((3, tc), lambda j, i: (0, j)), pl.BlockSpec((1, tc), lambda j, i: (0, j))],
                 (ospec, ospec, ospec, ospec), ("parallel", "parallel"))(h2, h2, w_gate, w_up, cw, cb)


def _ffn_down(name, hf, w_down, x1, g2, tm=512):
    s, d = x1.shape
    tm = min(tm, s)

    def body(hf_ref, wd_ref, x1_ref, g2_ref, dn_ref, x2_ref):
        dn = _dot(hf_ref[...], wd_ref[...].reshape(FFN_PAD, d))
        dn_ref[...] = dn.astype(BF)
        x2_ref[...] = x1_ref[...] + g2_ref[...] * dn

    return _call(body, name, (_sds((s, d), BF), _sds((s, d), F32)), (s // tm,),
                 [_rows(tm, FFN_PAD), _resident(w_down.shape), _rows(tm, d), _vec(d)],
                 (_rows(tm, d), _rows(tm, d)), ("parallel",))(hf, w_down, x1, g2)


def _ffn_down_bwd_act(name, dx2, dn, g2, w_down, a, ac, up, cw, tm=256, after=None):
    s, d = dx2.shape
    c = a.shape[1]
    tm = min(tm, s)
    tc = FFN_SHARD_PAD
    per = tm // HALO
    steps = s // tm
    last = s // HALO - 1
    n = tm + HALO

    def body(dx_ref, dxnext_ref, dn_ref, g2_ref, wd_ref, a_ref, ac_ref, acnext_ref, up_ref, upnext_ref, cw_ref,
             ddn_ref, da_ref, dup_ref, dg_ref, dcw_ref, dcb_ref):
        i = pl.program_id(0)

        @pl.when(i == 0)
        def _():
            dg_ref[...] = jnp.zeros_like(dg_ref)
            dcw_ref[...] = jnp.zeros_like(dcw_ref)
            dcb_ref[...] = jnp.zeros_like(dcb_ref)

        dxv = dx_ref[...]
        ddn = (dxv * g2_ref[...]).astype(BF)
        ddn_ref[...] = ddn
        dg_ref[...] += jnp.sum(dxv * dn_ref[...].astype(F32), axis=0, keepdims=True)
        ddn_next = jnp.where(i < steps - 1, dxnext_ref[...] * g2_ref[...], 0.0).astype(BF)
        ddn_ext = jnp.concatenate([ddn, ddn_next], axis=0)
        for k in range(N_CHIPS):
            cols = slice(k * tc, (k + 1) * tc)
            dh = _dot_nt(ddn_ext, wd_ref[k])
            ace = jnp.concatenate([ac_ref[:, cols].astype(F32), acnext_ref[:, cols].astype(F32)], axis=0)
            upe = jnp.concatenate([up_ref[:, cols].astype(F32), upnext_ref[:, cols].astype(F32)], axis=0)
            sig = _sigmoid(ace)
            silu = ace * sig
            dac = dh * upe * (sig + silu * (1.0 - sig))
            dup_ref[:, cols] = (dh[:tm] * silu[:tm]).astype(BF)
            d1 = pltpu.roll(dac, n - 1, axis=0)[:tm]
            d2 = pltpu.roll(dac, n - 2, axis=0)[:tm]
            d0 = dac[:tm]
            da_ref[:, cols] = (cw_ref[2:3, cols] * d0 + cw_ref[1:2, cols] * d1 + cw_ref[0:1, cols] * d2).astype(BF)
            a0 = a_ref[:, cols].astype(F32)
            dcb_ref[:, cols] += jnp.sum(d0, axis=0, keepdims=True)
            dcw_ref[0:1, cols] += jnp.sum(d2 * a0, axis=0, keepdims=True)
            dcw_ref[1:2, cols] += jnp.sum(d1 * a0, axis=0, keepdims=True)
            dcw_ref[2:3, cols] += jnp.sum(d0 * a0, axis=0, keepdims=True)

    nxt = lambda width: pl.BlockSpec((HALO, width), lambda i: (jnp.minimum((i + 1) * per, last), 0))
    wide = _sds((s, c), BF)
    return _call(body, name, (_sds((s, d), BF), wide, wide, _sds((1, d), F32), _sds((3, c), F32), _sds((1, c), F32)), (steps,),
                 [_rows(tm, d), nxt(d), _rows(tm, d), _vec(d), _resident(w_down.shape), _rows(tm, c), _rows(tm, c), nxt(c),
                  _rows(tm, c), nxt(c), pl.BlockSpec((3, c), lambda i: (0, 0))],
                 (_rows(tm, d), _rows(tm, c), _rows(tm, c), _vec(d), pl.BlockSpec((3, c), lambda i: (0, 0)), _vec(c)),
                 ("arbitrary",), after=after)(dx2, dx2, dn, g2, w_down, a, ac, ac, up, up, cw)


def _ffn_up_bwd(name, da, dup, w_gate, w_up, x1, dx2, w2, sc2, o, g1, tm=256, after=None):
    s, d = x1.shape
    tm = min(tm, s)
    tc = FFN_SHARD_PAD

    def body(da_ref, dup_ref, wg_ref, wu_ref, x1_ref, dx2_ref, w2_ref, sc2_ref, o_ref, g1_ref,
             dx1_ref, do_ref, dnw_ref, dsh_ref, dg1_ref):
        @pl.when(pl.program_id(0) == 0)
        def _():
            dnw_ref[...] = jnp.zeros_like(dnw_ref)
            dsh_ref[...] = jnp.zeros_like(dsh_ref)
            dg1_ref[...] = jnp.zeros_like(dg1_ref)

        dh = jnp.zeros((tm, d), F32)
        for k in range(N_CHIPS):
            cols = slice(k * tc, (k + 1) * tc)
            dh = dh + _dot(da_ref[:, cols], wg_ref[k]) + _dot(dup_ref[:, cols], wu_ref[k])
        xv = x1_ref[...]
        r = _rms(xv)
        xn = xv * r
        dxn = dh * (w2_ref[...] * (1.0 + sc2_ref[...]))
        dx1 = dx2_ref[...] + r * (dxn - xn * jnp.mean(dxn * xn, axis=-1, keepdims=True))
        dx1_ref[...] = dx1
        dnw_ref[...] += jnp.sum(dh * xn, axis=0, keepdims=True)
        dsh_ref[...] += jnp.sum(dh, axis=0, keepdims=True)
        do_ref[...] = (dx1 * g1_ref[...]).astype(BF)
        dg1_ref[...] += jnp.sum(dx1 * o_ref[...].astype(F32), axis=0, keepdims=True)

    v = _sds((1, d), F32)
    r = _rows(tm, d)
    wspec = _resident(w_gate.shape)
    return _call(body, name, (_sds((s, d), F32), _sds((s, d), BF), v, v, v), (s // tm,),
                 [_rows(tm, FFN_PAD), _rows(tm, FFN_PAD), wspec, wspec, r, r, _vec(d), _vec(d), r, _vec(d)],
                 (r, r, _vec(d), _vec(d), _vec(d)), ("arbitrary",), after=after)(da, dup, w_gate, w_up, x1, dx2, w2, sc2, o, g1)


def _loss_head(name, x, w, target, tm=512):
    s, d = x.shape
    tm = min(tm, s)

    def body(x_ref, w_ref, t_ref, dx_ref, loss_ref, dw_ref):
        @pl.when(pl.program_id(0) == 0)
        def _():
            loss_ref[...] = jnp.zeros_like(loss_ref)
            dw_ref[...] = jnp.zeros_like(dw_ref)

        xv = x_ref[...]
        r = _rms(xv)
        xn = xv * r
        err = xn * w_ref[...] - t_ref[...]
        loss_ref[...] += 0.5 * jnp.sum(jnp.mean(err * err, axis=-1, keepdims=True))
        dy = err * (1.0 / d)
        dw_ref[...] += jnp.sum(dy * xn, axis=0, keepdims=True)
        dxn = dy * w_ref[...]
        dx_ref[...] = r * (dxn - xn * jnp.mean(dxn * xn, axis=-1, keepdims=True))

    return _call(body, name, (_sds((s, d), F32), _sds((1, LANES), F32), _sds((1, d), F32)), (s // tm,),
                 [_rows(tm, d), _vec(d), _rows(tm, d)], (_rows(tm, d), _vec(LANES), _vec(d)), ("arbitrary",))(x, w, target)


def _layer_fwd(l, x, mod, p, cosf, sinf, after=None, early=None, late=None, later=None):
    sh1, sc1, g1, sh2, sc2, g2 = mod
    tag = f"l{l}_"
    h, qr, kk0, kk1, vv0, vv1, u_pre, v_pre, ga_pre, gb_pre = _in_proj(
        tag + "in_proj", x, p["norm1_w"], sc1, sh1, p["w_in"], cosf, sinf, after=after)
    if early is not None:
        early(h)
    y_attn = _attention(tag + "attn", qr, kk0, kk1, vv0, vv1, p["sinks"])
    y_sgu = _sgu(tag + "sgu", u_pre, v_pre, p["sgu_ln_w"], p["sgu_ln_b"], p["sgu_w"], p["sgu_bfull"])
    if late is not None:
        p = dict(p, **late(y_sgu))
    merged, pa, pb, o, x1, h2 = _mix_out(tag + "mix_out", y_sgu, y_attn, ga_pre, gb_pre, x, g1, p["proj_a"], p["proj_b"],
                                         p["w_out"], p["norm2_w"], sc2, sh2)
    if later is not None:
        p = dict(p, **later(h2))
    a, ac, up, hf = _ffn_up_act(tag + "ffn_up", h2, p["w_gate"], p["w_up"], p["conv_w"], p["conv_b"])
    dn, x2 = _ffn_down(tag + "ffn_down", hf, p["w_down"], x1, g2)
    saved = dict(x=x, h=h, qr=qr, kk0=kk0, kk1=kk1, vv0=vv0, vv1=vv1, u_pre=u_pre, v_pre=v_pre, ga_pre=ga_pre,
                 gb_pre=gb_pre, y_attn=y_attn, y_sgu=y_sgu, merged=merged, pa=pa, pb=pb, o=o, x1=x1, h2=h2, a=a, ac=ac, up=up,
                 hf=hf, dn=dn)
    return x2, saved, p


def _layer_bwd(l, dx2, mod, p, sv, cosf, sinf, after=None, emit=None, tick=None):
    sh1, sc1, g1, sh2, sc2, g2 = mod
    tag = f"l{l}_b_"
    d = D_MODEL
    g = {}
    ready = (lambda names: emit({k: g.pop(k) for k in names})) if emit else (lambda names: None)
    tick = tick or (lambda y: None)
    ddn, da, dup, dg2, g["conv_w"], g["conv_b"] = _ffn_down_bwd_act(
        tag + "ffn_down", dx2, sv["dn"], g2, p["w_down"], sv["a"], sv["ac"], sv["up"], p["conv_w"], after=after)
    g["w_down"] = _matmul_tn(tag + "dw_down", sv["hf"], ddn, tk=FFN_SHARD_PAD, after=tick(ddn)).reshape(N_CHIPS, FFN_SHARD_PAD, d)
    g["w_gate"] = _matmul_tn(tag + "dw_gate", da, sv["h2"], tk=FFN_SHARD_PAD).reshape(N_CHIPS, FFN_SHARD_PAD, d)
    g["w_up"] = _matmul_tn(tag + "dw_up", dup, sv["h2"], tk=FFN_SHARD_PAD).reshape(N_CHIPS, FFN_SHARD_PAD, d)
    dx1, do, da2, dsh2, dg1 = _ffn_up_bwd(tag + "ffn_up", da, dup, p["w_gate"], p["w_up"], sv["x1"], dx2, p["norm2_w"],
                                          sc2, sv["o"], g1, after=ready(("w_down", "w_gate", "w_up")))
    g["norm2_w"] = da2 * (1.0 + sc2)
    dsc2 = da2 * p["norm2_w"]
    g["w_out"] = _matmul_tn(tag + "dw_out", sv["merged"], do, after=tick(do)).reshape(N_CHIPS, d // N_CHIPS, d)
    dpa, dpb, dga, dgb, dy_sgu, dy_attn = _mix_bwd(tag + "mix", do, p["w_out"], p["proj_a"], p["proj_b"], sv["ga_pre"],
                                                  sv["gb_pre"], sv["pa"], sv["pb"])
    g["proj_a"] = _matmul_tn(tag + "dproj_a", sv["y_sgu"], dpa).reshape(N_CHIPS, d // N_CHIPS, d)
    g["proj_b"] = _matmul_tn(tag + "dproj_b", sv["y_attn"], dpb).reshape(N_CHIPS, d // N_CHIPS, d)
    du, dv, g["sgu_w"], g["sgu_b"], g["sgu_ln_w"], g["sgu_ln_b"] = _sgu_bwd(
        tag + "sgu", sv["u_pre"], sv["v_pre"], p["sgu_ln_w"], p["sgu_ln_b"], p["sgu_w"], p["sgu_bfull"], dy_sgu,
        after=ready(("w_out", "proj_a", "proj_b")))
    dqr, dkv_cur, dkv_prev, dsink = _attention_bwd(tag + "attn", sv["qr"], sv["kk0"], sv["kk1"], sv["vv0"], sv["vv1"],
                                                   p["sinks"], dy_attn, after=tick(du))
    g["sinks"] = dsink[0, :N_Q_HEADS]
    dq, dkv = _rope_bwd(tag + "rope", dqr, dkv_cur, dkv_prev, cosf, sinf)
    dw_in, row0 = None, 0
    for n, t in (("q", dq), ("kv", dkv), ("u", du), ("v", dv), ("ga", dga), ("gb", dgb)):
        dw_in = _matmul_tn_rows(tag + "dw_in_" + n, t, sv["h"], dw_in, row0, IN_COLS)
        row0 += t.shape[1]
    g["w_in"] = dw_in.reshape(N_CHIPS, IN_COLS // N_CHIPS, d)
    dx, da1, dsh1 = _in_proj_bwd(tag + "in_proj", dq, dkv, du, dv, dga, dgb, p["w_in"], sv["x"], p["norm1_w"], sc1, dx1)
    g["norm1_w"] = da1 * (1.0 + sc1)
    dsc1 = da1 * p["norm1_w"]
    return dx, (dsh1, dsc1, dg1, dsh2, dsc2, dg2), g


def _pad_to(a, axis, size):
    pad = [(0, 0)] * a.ndim
    pad[axis] = (0, size - a.shape[axis])
    return jnp.pad(a, pad)


def _early_params(w_in, small):
    d = D_MODEL
    return dict(
        w_in=w_in.reshape(IN_COLS, d), norm1_w=small["norm1_w"].reshape(1, d), sinks=small["sinks"],
        sgu_ln_w=small["sgu_ln_w"].reshape(1, d), sgu_ln_b=small["sgu_ln_b"].reshape(1, d), sgu_w=small["sgu_w"],
        sgu_bfull=jnp.broadcast_to(small["sgu_b"][:, :, None], (SGU_GROUPS, SGU_CHUNK, SGU_CHUNK)))


def _mix_params(proj_a, proj_b, w_out, small):
    return dict(proj_a=proj_a, proj_b=proj_b, w_out=w_out, norm2_w=small["norm2_w"].reshape(1, D_MODEL))


def _ffn_params(w_gate, w_up, w_down, conv_w, small):
    return dict(
        w_gate=w_gate, w_up=w_up, w_down=w_down, conv_w=conv_w.transpose(1, 0, 2).reshape(3, FFN_PAD),
        conv_b=_pad_to(small["conv_b"].reshape(N_CHIPS, FFN_SHARD), 1, FFN_SHARD_PAD).reshape(1, FFN_PAD))


def _layer_params(w_in, proj_a, proj_b, w_out, w_gate, w_up, w_down, conv_w, small):
    return dict(_early_params(w_in, small), **_mix_params(proj_a, proj_b, w_out, small),
                **_ffn_params(w_gate, w_up, w_down, conv_w, small))


def _conv_grads_natural(g):
    cw = g["conv_w"].reshape(3, N_CHIPS, FFN_SHARD_PAD)[:, :, :FFN_SHARD].reshape(3, FFN_DIM)
    cb = g["conv_b"].reshape(N_CHIPS, FFN_SHARD_PAD)[:, :FFN_SHARD].reshape(FFN_DIM)
    return cw, cb


def _rope_tables(positions):
    inv_freq = ROPE_THETA ** (-jnp.arange(0, ROT_DIM, 2, dtype=F32) / ROT_DIM)
    ang = positions.astype(F32)[:, None] * inv_freq
    cos, sin = jnp.cos(ang), jnp.sin(ang)
    s = positions.shape[0]
    rest = HEAD_DIM - ROT_DIM
    cos_head = jnp.concatenate([cos, cos, jnp.ones((s, rest), F32)], axis=1)
    sin_head = jnp.concatenate([-sin, sin, jnp.zeros((s, rest), F32)], axis=1)
    return jnp.tile(cos_head, (1, LANES // HEAD_DIM)), jnp.tile(sin_head, (1, LANES // HEAD_DIM))


ADA_ROWS = 16


def _ada_fwd(name, c_rows, ada_w, ada_b_cols, tn=512):
    depth, d, n = ada_w.shape

    def body(c_ref, w_ref, b_ref, o_ref):
        cv = c_ref[...]
        act = (cv * _sigmoid(cv)).astype(BF)
        o_ref[0] = _dot(act, w_ref[0].astype(BF)) + b_ref[0]

    return _call(body, name, _sds((depth, ADA_ROWS, n), F32), (depth, n // tn),
                 [pl.BlockSpec((ADA_ROWS, d), lambda l, j: (0, 0)), pl.BlockSpec((1, d, tn), lambda l, j: (l, 0, j)),
                  pl.BlockSpec((1, 1, tn), lambda l, j: (l, 0, j))],
                 pl.BlockSpec((1, ADA_ROWS, tn), lambda l, j: (l, 0, j)), ("parallel", "parallel"))(c_rows, ada_w, ada_b_cols)


def _ada_bwd(name, c_rows, dmod_cols, tn=512):
    depth, _, n = dmod_cols.shape
    d = c_rows.shape[1]

    def body(c_ref, dm_ref, o_ref):
        cv = c_ref[...]
        act = (cv * _sigmoid(cv)).astype(BF)
        o_ref[0] = _dot_tn(act, dm_ref[0].astype(BF))

    return _call(body, name, _sds((depth, d, n), F32), (depth, n // tn),
                 [pl.BlockSpec((ADA_ROWS, d), lambda l, j: (0, 0)), pl.BlockSpec((1, ADA_ROWS, tn), lambda l, j: (l, 0, j))],
                 pl.BlockSpec((1, d, tn), lambda l, j: (l, 0, j)), ("parallel", "parallel"))(c_rows, dmod_cols)


def _colsum(name, a):
    r, n = a.shape

    def body(a_ref, o_ref):
        o_ref[...] = jnp.sum(a_ref[...], axis=0, keepdims=True)

    return _call(body, name, _sds((1, n), F32), (1,), [pl.BlockSpec((r, n), lambda i: (0, 0))],
                 pl.BlockSpec((1, n), lambda i: (0, 0)), ("arbitrary",))(a)


REL_SIBLING = (0, 0, 1)
REL_CHIPS = ((1, 0, 0), (0, 1, 0), (1, 1, 0))
REL_ALL = tuple((fx, fy, fc) for fx in (0, 1) for fy in (0, 1) for fc in (0, 1) if fx or fy or fc)


def _chip_of(dev):
    return 2 * dev[0] + dev[1]


def _dev_of(dev):
    return 4 * dev[0] + 2 * dev[1] + dev[2]


def _flip(dev, rel):
    return tuple(1 - m if f else m for m, f in zip(dev, rel))


def _exchange(name, arrays, n_out, stages, aliases=None):
    out_shapes, stages = stages[0], stages[1:]
    n_in = len(arrays)
    aliases = aliases or {}
    n_remote = sum(len(plan) for plan, _ in stages)
    n_local = sum(len(local) for _, local in stages)

    def at(ref, idx):
        return ref.at[idx] if len(idx) else ref

    def body(*refs):
        bufs = list(refs[:n_in + n_out])
        for i_in, i_out in aliases.items():
            bufs[i_in] = bufs[n_in + i_out]
        send_sems, recv_sems, local_sems = refs[n_in + n_out:]
        me = (lax.axis_index("x"), lax.axis_index("y"), lax.axis_index("c"))
        base_r = base_l = 0
        pending = []
        for plan, local in stages:
            def remote(k, entry, sender, receiver):
                rel, si, ssel, di, dsel = entry
                return pltpu.make_async_remote_copy(
                    src_ref=at(bufs[si], ssel(sender, receiver)), dst_ref=at(bufs[di], dsel(sender, receiver)),
                    send_sem=send_sems.at[k], recv_sem=recv_sems.at[k], device_id=_flip(me, rel), device_id_type=MESH)

            sends = [remote(base_r + k, e, me, _flip(me, e[0])) for k, e in enumerate(plan)]
            for cp in sends:
                cp.start()
            for k, (si, ssel, di, dsel) in enumerate(local):
                cp = pltpu.make_async_copy(at(bufs[si], ssel(me)), at(bufs[di], dsel(me)), local_sems.at[base_l + k])
                cp.start()
                pending.append(cp.wait)
            for k, e in enumerate(plan):
                remote(base_r + k, e, _flip(me, e[0]), me).wait_recv()
            pending += [cp.wait_send for cp in sends]
            base_r += len(plan)
            base_l += len(local)
        for wait in pending:
            wait()

    any_spec = pl.BlockSpec(memory_space=pl.ANY)
    return pl.pallas_call(
        body, name=name, out_shape=tuple(out_shapes), in_specs=[any_spec] * n_in, out_specs=tuple([any_spec] * n_out),
        input_output_aliases=dict(aliases),
        scratch_shapes=[pltpu.SemaphoreType.DMA((max(n_remote, 1),)), pltpu.SemaphoreType.DMA((max(n_remote, 1),)),
                        pltpu.SemaphoreType.DMA((max(n_local, 1),))])(*arrays)


HBM_SPEC = pl.BlockSpec(memory_space=pltpu.HBM)
SEM_SPEC = pl.BlockSpec(memory_space=pltpu.SEMAPHORE)


def _split_copies(bufs, plan, local, send_sems, recv_sems, local_sems):
    me = (lax.axis_index("x"), lax.axis_index("y"), lax.axis_index("c"))

    def at(ref, idx):
        return ref.at[idx] if len(idx) else ref

    def remote(k, sender, receiver):
        rel, si, ssel, di, dsel = plan[k]
        return pltpu.make_async_remote_copy(
            src_ref=at(bufs[si], ssel(sender, receiver)), dst_ref=at(bufs[di], dsel(sender, receiver)),
            send_sem=send_sems.at[k], recv_sem=recv_sems.at[k], device_id=_flip(me, rel), device_id_type=MESH)

    sends = [remote(k, me, _flip(me, plan[k][0])) for k in range(len(plan))]
    arrivals = [remote(k, _flip(me, plan[k][0]), me) for k in range(len(plan))]
    locs = [pltpu.make_async_copy(at(bufs[si], ssel(me)), at(bufs[di], dsel(me)), local_sems.at[k])
            for k, (si, ssel, di, dsel) in enumerate(local)]
    return sends, arrivals, locs


def _exchange_start(name, arrays, out_shapes, plan, local, inplace=False):
    n_in, n_out = len(arrays), len(out_shapes)
    n_buf = n_in + n_out

    def body(*refs):
        sems = refs[n_buf:n_buf + 3]
        bufs = refs[n_buf + 3:2 * n_buf + 3]
        sends, _, locs = _split_copies(bufs * 2 if inplace else bufs, plan, local, *sems)
        for cp in sends + locs:
            cp.start()
        refs[-1][...] = jnp.zeros_like(refs[-1])

    zones = [lax.empty(o.shape, o.dtype) for o in out_shapes]
    operands = [pltpu.with_memory_space_constraint(a, pltpu.HBM) for a in list(arrays) + zones]
    sem = lambda n: pltpu.SemaphoreType.DMA((max(n, 1),))
    out = pl.pallas_call(
        body, name=name,
        out_shape=(sem(len(plan)), sem(len(plan)), sem(len(local)), *[pltpu.HBM(a.shape, a.dtype) for a in operands],
                   _sds((SUBLANES, LANES), F32)),
        in_specs=[HBM_SPEC] * n_buf,
        out_specs=(SEM_SPEC, SEM_SPEC, SEM_SPEC, *[HBM_SPEC] * n_buf, pl.BlockSpec(memory_space=pltpu.VMEM)),
        input_output_aliases={i: 3 + i for i in range(n_buf)},
        compiler_params=pltpu.CompilerParams(has_side_effects=pltpu.SideEffectType.DATAFLOW_SIDE_EFFECTING))(*operands)
    pending = dict(name=name, sems=out[:3], thru=out[3:3 + n_in], zones=out[3 + n_in:3 + n_buf], plan=plan, local=local,
                   inplace=inplace)
    return pending, out[-1]


def _exchange_wait(pending, after, both=False):
    thru, zones, plan, local, inplace = (pending[k] for k in ("thru", "zones", "plan", "local", "inplace"))
    n_in, n_buf = len(thru), len(thru) + len(zones)

    def body(*refs):
        bufs = refs[:n_buf]
        sends, arrivals, locs = _split_copies(bufs * 2 if inplace else bufs, plan, local, *refs[n_buf:n_buf + 3])
        for cp in arrivals:
            cp.wait_recv()
        for cp in sends:
            cp.wait_send()
        for cp in locs:
            cp.wait()

    out = pl.pallas_call(
        body, name=pending["name"] + "_wait", out_shape=tuple(pltpu.HBM(a.shape, a.dtype) for a in list(thru) + list(zones)),
        in_specs=[HBM_SPEC] * n_buf + [SEM_SPEC] * 3 + [pl.BlockSpec(memory_space=pl.ANY)],
        out_specs=tuple([HBM_SPEC] * n_buf), input_output_aliases={i: i for i in range(n_buf)},
        compiler_params=pltpu.CompilerParams(has_side_effects=pltpu.SideEffectType.DATAFLOW_SIDE_EFFECTING))(
            *thru, *zones, *pending["sems"], after)
    if both:
        return out[:n_in], out[n_in:]
    return out[:n_in] if inplace else out[n_in:]


def _whole(*_):
    return ()


def _half_rows(rows, core):
    return pl.ds(core * (rows // 2), rows // 2)


def _gather_weights_plan(shards):
    n = len(shards)
    dsts = [_sds((N_CHIPS,) + a.shape, a.dtype) for a in shards]
    fetch, forward = [], []
    for t, a in enumerate(shards):
        rows = a.shape[0]
        if rows % (2 * 16) == 0:
            fetch += [(rel, t, (lambda s_, r_, rows=rows: (_half_rows(rows, s_[2]),)), n + t,
                       (lambda s_, r_, rows=rows: (_chip_of(s_), _half_rows(rows, s_[2])))) for rel in REL_CHIPS]
            forward += [(REL_SIBLING, n + t, (lambda s_, r_, rows=rows, rel=rel: (_chip_of(_flip(s_, rel)), _half_rows(rows, s_[2]))),
                         n + t, (lambda s_, r_, rows=rows, rel=rel: (_chip_of(_flip(s_, rel)), _half_rows(rows, s_[2]))))
                        for rel in REL_CHIPS]
        else:
            fetch += [(rel, t, _whole, n + t, lambda s_, r_: (_chip_of(s_),)) for rel in REL_CHIPS]
    local = [(t, _whole, n + t, lambda me: (_chip_of(me),)) for t in range(n)]
    return dsts, fetch, local, forward


def _gather_weights_start(name, shards):
    dsts, fetch, local, forward = _gather_weights_plan(shards)
    pending, token = _exchange_start(name, shards, dsts, fetch, local)
    return dict(pending, forward=forward), token


def _gather_weights_landed(pending, after):
    return _exchange_start(pending["name"] + "_forward", _exchange_wait(pending, after), [], pending["forward"], [], inplace=True)


def _gather_weights_done(forwarding, after=None):
    pending, token = forwarding
    return _exchange_wait(pending, token if after is None else after)


def _gather_chips(name, arrays):
    n = len(arrays)
    dsts = [_sds((N_CHIPS,) + a.shape, a.dtype) for a in arrays]
    plan = [(rel, t, _whole, n + t, lambda s_, r_: (_chip_of(s_),)) for t in range(n) for rel in REL_CHIPS]
    local = [(t, _whole, n + t, lambda me: (_chip_of(me),)) for t in range(n)]
    return _exchange(name, arrays, n, [dsts, (plan, local)])


def _gather_all(name, a):
    plan = [(rel, 0, _whole, 1, lambda s_, r_: (_dev_of(s_),)) for rel in REL_ALL]
    local = [(0, _whole, 1, lambda me: (_dev_of(me),))]
    return _exchange(name, [a], 1, [[_sds((2 * N_CHIPS,) + a.shape, a.dtype)], (plan, local)])[0]


def _swap_halves_start(name, grads):
    n = len(grads)
    dsts = [_sds((g.shape[0], g.shape[1] // 2, g.shape[2]), g.dtype) for g in grads]
    plan = [(REL_SIBLING, t, (lambda s_, r_, rows=g.shape[1]: (pl.ds(0, N_CHIPS), _half_rows(rows, r_[2]))), n + t, _whole)
            for t, g in enumerate(grads)]
    return _exchange_start(name, grads, dsts, plan, [])


def _scatter_chips_plan(sums):
    n = len(sums)
    dsts = [_sds(a.shape, a.dtype) for a in sums]
    plan = [(rel, t, lambda s_, r_: (_chip_of(r_),), n + t, lambda s_, r_: (_chip_of(s_),))
            for t in range(n) for rel in REL_CHIPS]
    local = [(t, lambda me: (_chip_of(me),), n + t, lambda me: (_chip_of(me),)) for t in range(n)]
    return dsts, plan, local


def _scatter_chips(name, sums):
    dsts, plan, local = _scatter_chips_plan(sums)
    return _exchange(name, sums, len(sums), [dsts, (plan, local)])


def _scatter_chips_start(name, sums):
    dsts, plan, local = _scatter_chips_plan(sums)
    return _exchange_start(name, sums, dsts, plan, local)


def _swap_back_start(name, totals, layer):
    n = len(totals)
    plan = [(REL_SIBLING, n + t, (lambda s_, r_, rows=a.shape[1]: (layer, _half_rows(rows, s_[2]))),
             n + t, (lambda s_, r_, rows=a.shape[1]: (layer, _half_rows(rows, s_[2])))) for t, a in enumerate(totals)]
    return _exchange_start(name, totals, [], plan, [], inplace=True)


def _add_halves(name, g, recv, core):
    nch, half, c = recv.shape

    def body(core_ref, g_ref, r_ref, o_ref):
        o_ref[0] = (g_ref[0, 0].astype(F32) + r_ref[0].astype(F32)).astype(o_ref.dtype)

    spec = pltpu.PrefetchScalarGridSpec(
        num_scalar_prefetch=1, grid=(nch,),
        in_specs=[pl.BlockSpec((1, 1, half, c), lambda k, core_ref: (k, core_ref[0], 0, 0)),
                  pl.BlockSpec((1, half, c), lambda k, core_ref: (k, 0, 0))],
        out_specs=pl.BlockSpec((1, half, c), lambda k, core_ref: (k, 0, 0)))
    return pl.pallas_call(body, name=name, out_shape=_sds(recv.shape, recv.dtype), grid_spec=spec,
                          compiler_params=pltpu.CompilerParams(dimension_semantics=("parallel",),
                                                               vmem_limit_bytes=VMEM_LIMIT))(
                                                                   core, g.reshape(nch, 2, half, c), recv)


def _sum_chips(name, a, core, layer, total):
    nch, half, c = a.shape

    def body(core_ref, a_ref, *rest):
        o_ref = rest[-1]
        acc = a_ref[0].astype(F32)
        for k in range(1, nch):
            acc = acc + a_ref[k].astype(F32)
        o_ref[0, 0] = acc

    in_specs = [pl.BlockSpec((nch, half, c), lambda i, core_ref: (0, 0, 0))]
    args = [core, a]
    if total is not None:
        in_specs.append(pl.BlockSpec(memory_space=pl.ANY))
        args.append(total.reshape(DEPTH, 2, half, c))
    spec = pltpu.PrefetchScalarGridSpec(
        num_scalar_prefetch=1, grid=(1,), in_specs=in_specs,
        out_specs=pl.BlockSpec((1, 1, half, c), lambda i, core_ref: (layer, core_ref[0], 0, 0)))
    out = pl.pallas_call(body, name=name, out_shape=_sds((DEPTH, 2, half, c), F32), grid_spec=spec,
                         input_output_aliases={2: 0} if total is not None else {},
                         compiler_params=pltpu.CompilerParams(dimension_semantics=("arbitrary",),
                                                              vmem_limit_bytes=VMEM_LIMIT))(*args)
    return out.reshape(DEPTH, 2 * half, c)


def _adamw_update(w, g, m, v):
    mn = ADAM_B1 * m + (1.0 - ADAM_B1) * g
    vn = ADAM_B2 * v + (1.0 - ADAM_B2) * (g * g)
    m_hat = mn / (1.0 - ADAM_B1 ** ADAM_STEP)
    v_hat = vn / (1.0 - ADAM_B2 ** ADAM_STEP)
    return -ADAM_LR * (m_hat / (jnp.sqrt(v_hat) + ADAM_EPS) + ADAM_WD * w), mn, vn


def _adamw(name, w, g, m, v):
    depth, r, c = w.shape
    tr = next(t for t in (512, 448, 384, 352, 336, 256, 192, 128, 64, 32, 16, 8) if r % t == 0 and t * c <= ADAM_TILE_ELEMS)

    def body(w_ref, g_ref, m_ref, v_ref, go_ref, d_ref, mo_ref, vo_ref):
        gv = g_ref[...]
        go_ref[...] = gv
        d_ref[...], mo_ref[...], vo_ref[...] = _adamw_update(w_ref[...], gv, m_ref[...], v_ref[...])

    spec = pl.BlockSpec((1, tr, c), lambda l, i: (l, i, 0))
    o = _sds(w.shape, F32)
    return _call(body, name, (o, o, o, o), (depth, r // tr), [spec] * 4, (spec,) * 4, ("parallel", "parallel"))(w, g, m, v)


def _adamw_small(name, ws, gs, ms, vs):
    n = len(ws)

    def body(*refs):
        for t in range(n):
            w_ref, g_ref, m_ref, v_ref = (refs[k * n + t] for k in range(4))
            d_ref, mo_ref, vo_ref = (refs[(4 + k) * n + t] for k in range(3))
            d_ref[...], mo_ref[...], vo_ref[...] = _adamw_update(w_ref[...], g_ref[...], m_ref[...], v_ref[...])

    outs = [_sds(w.shape, F32) for w in ws]
    res = pl.pallas_call(body, name=name, out_shape=tuple(outs * 3))(*ws, *gs, *ms, *vs)
    return res[:n], res[n:2 * n], res[2 * n:]


def _pack(arrays, rows):
    flat = jnp.concatenate([a.reshape(-1).astype(F32) for a in arrays])
    return _pad_to(flat, 0, rows * LANES).reshape(rows, LANES)


def _unpack(packed, shapes):
    flat = packed.reshape(-1)
    out, off = [], 0
    for shp in shapes:
        n = 1
        for s_ in shp:
            n *= s_
        out.append(flat[off:off + n].reshape(shp))
        off += n
    return out


_MATRICES = ("w_in", "proj_a", "proj_b", "w_out", "w_gate", "w_up", "w_down")
_SMALL = (("norm1_w", (D_MODEL,)), ("sinks", (N_Q_HEADS,)), ("sgu_ln_w", (SGU_WIDTH,)), ("sgu_ln_b", (SGU_WIDTH,)),
          ("sgu_w", (SGU_GROUPS, SGU_CHUNK, SGU_CHUNK)), ("sgu_b", (SGU_GROUPS, SGU_CHUNK)), ("norm2_w", (D_MODEL,)),
          ("conv_w", (3, FFN_DIM)), ("conv_b", (FFN_DIM,)), ("final_norm_w", (D_MODEL,)))
SMALL_ROWS = 320
ADAM_TILE_ELEMS = 384 * 1024


def _reduce_cores_start(tag, partial):
    names = list(partial)
    pending, token = _swap_halves_start(tag + "_cores", [partial[k] for k in names])
    return dict(pending, tag=tag, names=names), token


def _reduce_chips_start(pending, core, after):
    tag, names = pending["tag"], pending["names"]
    mine, theirs = _exchange_wait(pending, after, both=True)
    sums = [_add_halves(f"{tag}_cores_add_{k}", g, r, core) for k, g, r in zip(names, mine, theirs)]
    scatter, token = _scatter_chips_start(tag + "_chips", sums)
    return dict(scatter, tag=tag, names=names), token


def _reduce_back_start(pending, l, core, totals, after):
    tag, names = pending["tag"], pending["names"]
    sums = [_sum_chips(f"{tag}_chips_add_{k}", a, core, l, totals.get(k)) for k, a in zip(names, _exchange_wait(pending, after))]
    back, _ = _swap_back_start(tag + "_back", sums, l)
    return dict(back, names=names)


def kernel(x, c, positions, ada_w, ada_b, norm1_w, w_in, attn_sinks, sgu_ln_w, sgu_ln_b, sgu_w, sgu_b, proj_a, proj_b, w_out, norm2_w, ffn_w_gate, ffn_w_up, ffn_conv_w, ffn_conv_b, ffn_w_down, final_norm_w, loss_target, m_ada_w, m_ada_b, m_norm1_w, m_w_in, m_attn_sinks, m_sgu_ln_w, m_sgu_ln_b, m_sgu_w, m_sgu_b, m_proj_a, m_proj_b, m_w_out, m_norm2_w, m_ffn_w_gate, m_ffn_w_up, m_ffn_conv_w, m_ffn_conv_b, m_ffn_w_down, m_final_norm_w, v_ada_w, v_ada_b, v_norm1_w, v_w_in, v_attn_sinks, v_sgu_ln_w, v_sgu_ln_b, v_sgu_w, v_sgu_b, v_proj_a, v_proj_b, v_w_out, v_norm2_w, v_ffn_w_gate, v_ffn_w_up, v_ffn_conv_w, v_ffn_conv_b, v_ffn_w_down, v_final_norm_w):
    d = D_MODEL
    ax, ay, ac = lax.axis_index("x"), lax.axis_index("y"), lax.axis_index("c")
    chip = 2 * ax + ay
    dev = 4 * ax + 2 * ay + ac
    core = ac.astype(jnp.int32).reshape(1)

    c_all = _gather_all("gather_cond", c.reshape(SUBLANES, d // SUBLANES)).reshape(2 * N_CHIPS, d)
    c_rows = _pad_to(c_all, 0, ADA_ROWS)
    ada_cols = ada_w.shape[2]
    ada_b_cols = lax.dynamic_slice_in_dim(ada_b, chip * ada_cols, ada_cols, axis=1).reshape(DEPTH, 1, ada_cols)
    mod_cols = _ada_fwd("ada_fwd", c_rows, ada_w, ada_b_cols)
    mod_all = _gather_chips("gather_mod", [mod_cols])[0]
    mod_mine = lax.dynamic_index_in_dim(mod_all, dev, axis=2, keepdims=False)
    mod_mine = mod_mine.transpose(1, 0, 2).reshape(DEPTH, 1, 6 * d)
    mods = [tuple(jnp.split(mod_mine[l], 6, axis=-1)) for l in range(DEPTH)]

    tr = lambda a: jnp.swapaxes(a, 1, 2)
    shards = [tr(w_in).astype(BF), proj_a.astype(BF), proj_b.astype(BF), w_out.astype(BF),
              _pad_to(tr(ffn_w_gate).astype(BF), 1, FFN_SHARD_PAD), _pad_to(tr(ffn_w_up).astype(BF), 1, FFN_SHARD_PAD),
              _pad_to(ffn_w_down.astype(BF), 1, FFN_SHARD_PAD), _pad_to(ffn_conv_w, 2, FFN_SHARD_PAD)]
    token = mod_all[0, 0, :SUBLANES, :LANES]
    fetches = []
    for l in range(DEPTH):
        groups = []
        for tag, members in (("in", shards[:1]), ("mix", shards[1:4]), ("ffn", shards[4:])):
            behind = (token[0, 0] * 0.0).astype(members[0].dtype)
            pending, token = _gather_weights_start(f"l{l}_gather_{tag}", [members[0][l] + behind] + [a[l] for a in members[1:]])
            groups.append(pending)
        fetches.append(groups)

    small_in = dict(norm1_w=norm1_w, sinks=attn_sinks, sgu_ln_w=sgu_ln_w, sgu_ln_b=sgu_ln_b, sgu_w=sgu_w, sgu_b=sgu_b,
                    norm2_w=norm2_w, conv_b=ffn_conv_b)
    cosf, sinf = _rope_tables(positions[0])
    small_of = lambda l: {k: v[l] for k, v in small_in.items()}

    h = x[0]
    saved, params = [], []
    for l in range(DEPTH):
        first, mix, ffn = fetches[l]
        w_in_l = _gather_weights_done(_gather_weights_landed(first, token if l == 0 else h))
        stage = {}

        def early(y, stage=stage, mix=mix):
            stage["mix"] = _gather_weights_landed(mix, y)

        def late(y, l=l, stage=stage, ffn=ffn):
            ready = _mix_params(*_gather_weights_done(stage["mix"], y), small_of(l))
            stage["ffn"] = _gather_weights_landed(ffn, y)
            return ready

        def later(y, l=l, stage=stage):
            return _ffn_params(*_gather_weights_done(stage["ffn"], y), small_of(l))

        h, sv, p = _layer_fwd(l, h, mods[l], _early_params(w_in_l[0], small_of(l)), cosf, sinf, early=early, late=late, later=later)
        saved.append(sv)
        params.append(p)
    dx, loss_part, d_final = _loss_head("loss_head", h, final_norm_w.reshape(1, d), loss_target[0])
    loss = lax.psum(loss_part[0, 0], ("x", "y", "c"))

    def small_pack(l, grads):
        cw, cb = _conv_grads_natural(grads)
        nat = dict(grads, conv_w=cw, conv_b=cb, final_norm_w=d_final if l == DEPTH - 1 else jnp.zeros((d,), F32))
        return _pack([nat[k] for k, _ in _SMALL], N_CHIPS * SMALL_ROWS).reshape(N_CHIPS, SMALL_ROWS, LANES)

    waiting, inflight = [], []

    def send(tag, partial):
        pending, token = _reduce_cores_start(tag, partial)
        waiting.append(pending)
        return token

    def tick(y):
        token = None
        while waiting:
            pending, token = _reduce_chips_start(waiting.pop(0), core, y)
            inflight.append(pending)
        return token

    dmods = [None] * DEPTH
    dx, dmods[1], grads = _layer_bwd(1, dx, mods[1], params[1], saved[1], cosf, sinf)
    token = send("l1_reduce", dict({k: grads[k] for k in _MATRICES}, small=small_pack(1, grads)))
    dx, dmods[0], grads = _layer_bwd(0, dx, mods[0], params[0], saved[0], cosf, sinf, after=token,
                                     emit=lambda part: send("l0_reduce_" + "_".join(part), part), tick=tick)
    dmod_mine = jnp.concatenate([jnp.concatenate(dmods[l], axis=1) for l in range(DEPTH)], axis=1)
    dmod_all = _gather_all("gather_dmod", dmod_mine.reshape(SUBLANES, -1)).reshape(2 * N_CHIPS, DEPTH * 6 * d)
    send("l0_reduce_in", dict(w_in=grads["w_in"], small=small_pack(0, grads) + dmod_all[0, 0] * 0.0))
    tick(dmod_all)

    totals, flying = {}, None

    def land(after):
        if flying is not None:
            totals.update(zip(flying["names"], _exchange_wait(flying, after)))

    for pending in inflight[:-1]:
        land(dx)
        flying = _reduce_back_start(pending, 1 if pending["tag"].startswith("l1") else 0, core, totals, dx)
    land(dx)
    flying = None

    g_ada_b = _colsum("ada_b_grad", dmod_all).reshape(DEPTH, 6 * d)
    dmod_cols = jnp.stack([lax.dynamic_slice_in_dim(dmod_all, l * 6 * d + chip * ada_cols, ada_cols, axis=1)
                           for l in range(DEPTH)])
    g_ada_w = _ada_bwd("ada_w_grad", c_rows, _pad_to(dmod_cols, 1, ADA_ROWS))
    big = dict(w_in=(tr(w_in), tr(m_w_in), tr(v_w_in)), proj_a=(proj_a, m_proj_a, v_proj_a), proj_b=(proj_b, m_proj_b, v_proj_b),
               w_out=(w_out, m_w_out, v_w_out), w_gate=(tr(ffn_w_gate), tr(m_ffn_w_gate), tr(v_ffn_w_gate)),
               w_up=(tr(ffn_w_up), tr(m_ffn_w_up), tr(v_ffn_w_up)), w_down=(ffn_w_down, m_ffn_w_down, v_ffn_w_down))
    upd, g_big = {}, {}

    def update(k):
        res = _adamw("adamw_" + k, big[k][0], totals[k], *big[k][1:])
        res = [tr(a) for a in res] if k in ("w_in", "w_gate", "w_up") else res
        g_big[k], upd[k] = res[0], res[1:]

    for k in ("w_down", "w_gate", "w_up", "w_out", "proj_a", "proj_b"):
        update(k)
    g_big["ada_w"], *upd["ada_w"] = _adamw("adamw_ada_w", ada_w, g_ada_w, m_ada_w, v_ada_w)
    flying = _reduce_back_start(inflight[-1], 0, core, totals, upd["ada_w"][0])
    land(upd["ada_w"][1])
    update("w_in")

    small_all = _gather_chips("gather_small", [totals["small"]])[0]
    small_g = small_all.transpose(1, 0, 2, 3).reshape(DEPTH, -1)
    per_layer = [_unpack(small_g[l], [shp for _, shp in _SMALL]) for l in range(DEPTH)]
    sg = {k: jnp.stack([per_layer[l][i] for l in range(DEPTH)]) for i, (k, _) in enumerate(_SMALL)}
    g_final = sg["final_norm_w"][DEPTH - 1]
    g_conv_w = lax.dynamic_slice_in_dim(sg["conv_w"], chip * FFN_SHARD, FFN_SHARD, axis=2)

    rest = [("ada_b", ada_b, g_ada_b, m_ada_b, v_ada_b), ("norm1_w", norm1_w, sg["norm1_w"], m_norm1_w, v_norm1_w),
            ("attn_sinks", attn_sinks, sg["sinks"], m_attn_sinks, v_attn_sinks),
            ("sgu_ln_w", sgu_ln_w, sg["sgu_ln_w"], m_sgu_ln_w, v_sgu_ln_w),
            ("sgu_ln_b", sgu_ln_b, sg["sgu_ln_b"], m_sgu_ln_b, v_sgu_ln_b), ("sgu_w", sgu_w, sg["sgu_w"], m_sgu_w, v_sgu_w),
            ("sgu_b", sgu_b, sg["sgu_b"], m_sgu_b, v_sgu_b), ("norm2_w", norm2_w, sg["norm2_w"], m_norm2_w, v_norm2_w),
            ("ffn_conv_w", ffn_conv_w, g_conv_w, m_ffn_conv_w, v_ffn_conv_w),
            ("ffn_conv_b", ffn_conv_b, sg["conv_b"], m_ffn_conv_b, v_ffn_conv_b),
            ("final_norm_w", final_norm_w.reshape(1, d), g_final.reshape(1, d), m_final_norm_w.reshape(1, d),
             v_final_norm_w.reshape(1, d))]
    rest_out = _adamw_small("adamw_rest", *[[r[i] for r in rest] for i in (1, 2, 3, 4)])
    g_rest = {r[0]: r[2] for r in rest}
    u_rest = {r[0]: tuple(o[i] for o in rest_out) for i, r in enumerate(rest)}
    g_rest["final_norm_w"] = g_final
    u_rest["final_norm_w"] = tuple(a.reshape(d) for a in u_rest["final_norm_w"])

    names = ("ada_w", "ada_b", "norm1_w", "w_in", "attn_sinks", "sgu_ln_w", "sgu_ln_b", "sgu_w", "sgu_b", "proj_a", "proj_b",
             "w_out", "norm2_w", "ffn_w_gate", "ffn_w_up", "ffn_conv_w", "ffn_conv_b", "ffn_w_down", "final_norm_w")
    alias = {"ffn_w_gate": "w_gate", "ffn_w_up": "w_up", "ffn_w_down": "w_down"}
    grad_of = lambda n: g_rest[n] if n in g_rest else g_big[alias.get(n, n)]
    upd_of = lambda n: u_rest[n] if n in u_rest else upd[alias.get(n, n)]
    return (loss, dx[None], *[grad_of(n) for n in names], *[upd_of(n)[0] for n in names],
            *[upd_of(n)[1] for n in names], *[upd_of(n)[2] for n in names])
```

```python
import jax
import jax.numpy as jnp
from jax import lax
from jax.experimental import pallas as pl
from jax.experimental.pallas import tpu as pltpu

F32 = jnp.float32
BF = jnp.bfloat16

D_MODEL = 1024
N_Q_HEADS = 16
N_KV_HEADS = 2
HEAD_DIM = 64
ATTN_BLOCK = 128
ROPE_THETA = 500000.0
ROT_DIM = HEAD_DIM // 4
SGU_WIDTH = 1024
SGU_GROUPS = 8
SGU_CHUNK = 128
FFN_DIM = 2816
NORM_EPS = 1e-6
DEPTH = 2
IN_COLS = 5376
N_CHIPS = 4
FFN_SHARD = FFN_DIM // N_CHIPS
FFN_SHARD_PAD = 768
FFN_PAD = N_CHIPS * FFN_SHARD_PAD
LANES = 128
SUBLANES = 8
HALO = 16
VMEM_LIMIT = 56 * 1024 * 1024
NEG_BIG = -1e30

ADAM_LR = 0.001
ADAM_B1 = 0.9
ADAM_B2 = 0.999
ADAM_EPS = 1e-08
ADAM_WD = 0.01
ADAM_STEP = 10

MESH = pl.DeviceIdType.MESH

Q_END = 1024
KV_END = 1280
U_END = 2304
Z_END = 3328
GA_END = 4352


def _sds(shape, dtype):
    return jax.ShapeDtypeStruct(tuple(shape), dtype)


def _call(body, name, out_shape, grid, in_specs, out_specs, semantics, scratch=(), after=None):
    n_in = len(in_specs)
    fn = body
    if after is not None:
        def fn(*refs):
            return body(*refs[:n_in], *refs[n_in + 1:])

        in_specs = list(in_specs) + [pl.BlockSpec(memory_space=pl.ANY)]
    call = pl.pallas_call(
        fn, name=name, out_shape=out_shape, grid=grid, in_specs=in_specs, out_specs=out_specs,
        scratch_shapes=scratch,
        compiler_params=pltpu.CompilerParams(dimension_semantics=semantics, vmem_limit_bytes=VMEM_LIMIT))
    if after is None:
        return call
    return lambda *args: call(*args, after)


def _rows(tm, width, col=0):
    return pl.BlockSpec((tm, width), lambda i: (i, col))


def _vec(width):
    return pl.BlockSpec((1, width), lambda i: (0, 0))


def _resident(shape):
    zeros = (0,) * len(shape)
    return pl.BlockSpec(tuple(shape), lambda *_: zeros, pipeline_mode=pl.Buffered(1))


def _sigmoid(x):
    return 0.5 + 0.5 * jnp.tanh(0.5 * x)


def _gelu(x):
    return 0.5 * x * (1.0 + lax.erf(x * 0.7071067811865476))


def _gelu_and_grad(x):
    cdf = 0.5 * (1.0 + lax.erf(x * 0.7071067811865476))
    return x * cdf, cdf + x * jnp.exp(-0.5 * x * x) * 0.3989422804014327


def _dot(a, b):
    return jnp.dot(a, b, preferred_element_type=F32)


def _dot_nt(a, b):
    return lax.dot_general(a, b, (((1,), (1,)), ((), ())), preferred_element_type=F32)


def _dot_tn(a, b):
    return lax.dot_general(a, b, (((0,), (0,)), ((), ())), preferred_element_type=F32)


def _rms(xv):
    return lax.rsqrt(jnp.mean(xv * xv, axis=-1, keepdims=True) + NORM_EPS)


def _matmul_tn(name, a, b, tk=512, tn=1024, blocked=False, after=None):
    s, k = a.shape
    n = b.shape[1]
    tk, tn = min(tk, k), min(tn, n)

    def body(a_ref, b_ref, o_ref):
        res = _dot_tn(a_ref[...], b_ref[...]).astype(o_ref.dtype)
        if blocked:
            o_ref[0] = res
        else:
            o_ref[...] = res

    if blocked:
        out, ospec = _sds((n // tn, k, tn), BF), pl.BlockSpec((1, tk, tn), lambda i, j: (j, i, 0))
    else:
        out, ospec = _sds((k, n), BF), pl.BlockSpec((tk, tn), lambda i, j: (i, j))
    return _call(body, name, out, (k // tk, n // tn),
                 [pl.BlockSpec((s, tk), lambda i, j: (0, i)), pl.BlockSpec((s, tn), lambda i, j: (0, j))],
                 ospec, ("parallel", "parallel"), after=after)(a, b)


def _matmul_tn_rows(name, a, b, out, row0, rows_total, tk=256):
    s, k = a.shape
    n = b.shape[1]

    def body(a_ref, b_ref, *rest):
        rest[-1][...] = _dot_tn(a_ref[...], b_ref[...]).astype(BF)

    in_specs = [pl.BlockSpec((s, tk), lambda i: (0, i)), _resident(b.shape)]
    args = [a, b]
    if out is not None:
        in_specs.append(pl.BlockSpec(memory_space=pl.ANY))
        args.append(out)
    return pl.pallas_call(
        body, name=name, out_shape=_sds((rows_total, n), BF), grid=(k // tk,), in_specs=in_specs,
        out_specs=pl.BlockSpec((tk, n), lambda i: (row0 // tk + i, 0)),
        input_output_aliases={2: 0} if out is not None else {},
        compiler_params=pltpu.CompilerParams(dimension_semantics=("parallel",), vmem_limit_bytes=VMEM_LIMIT))(*args)


def _rope_partner(v):
    lane = lax.broadcasted_iota(jnp.int32, (1, LANES), 1) % HEAD_DIM
    return jnp.where(lane < ROT_DIM // 2, pltpu.roll(v, LANES - ROT_DIM // 2, axis=1), pltpu.roll(v, ROT_DIM // 2, axis=1))


def _dup_half(v, half):
    lane = lax.broadcasted_iota(jnp.int32, (1, LANES), 1)
    keep = jnp.where((lane >= HEAD_DIM) == (half == 1), v, 0.0)
    return keep + pltpu.roll(keep, HEAD_DIM, axis=1)


def _in_proj(name, x, w, sc, sh, w_in, cosf, sinf, tm=256, after=None):
    s, d = x.shape
    tm = min(tm, s)

    def body(x_ref, w_ref, sc_ref, sh_ref, win_ref, cos_ref, sin_ref,
             h_ref, qr_ref, kk0_ref, kk1_ref, vv0_ref, vv1_ref, u_ref, v_ref, ga_ref, gb_ref):
        xv = x_ref[...]
        h = ((xv * _rms(xv)) * w_ref[...] * (1.0 + sc_ref[...]) + sh_ref[...]).astype(BF)
        h_ref[...] = h
        cosv, sinv = cos_ref[...], sin_ref[...]
        q = _dot_nt(h, win_ref[:Q_END, :])
        for j in range(D_MODEL // LANES):
            qv = q[:, j * LANES:(j + 1) * LANES]
            qr_ref[:, j * LANES:(j + 1) * LANES] = ((qv * cosv + _rope_partner(qv) * sinv) * ATTN_SCALE).astype(BF)
        kv = _dot_nt(h, win_ref[Q_END:KV_END, :])
        kr = kv[:, :LANES] * cosv + _rope_partner(kv[:, :LANES]) * sinv
        vv = kv[:, LANES:]
        kk0_ref[...] = _dup_half(kr, 0).astype(BF)
        kk1_ref[...] = _dup_half(kr, 1).astype(BF)
        vv0_ref[...] = _dup_half(vv, 0).astype(BF)
        vv1_ref[...] = _dup_half(vv, 1).astype(BF)
        u_ref[...] = _dot_nt(h, win_ref[KV_END:U_END, :])
        v_ref[...] = _dot_nt(h, win_ref[U_END:Z_END, :])
        ga_ref[...] = _dot_nt(h, win_ref[Z_END:GA_END, :]).astype(BF)
        gb_ref[...] = _dot_nt(h, win_ref[GA_END:, :]).astype(BF)

    wide, kvs, pre = _sds((s, d), BF), _sds((s, LANES), BF), _sds((s, d), F32)
    return _call(body, name, (wide, wide, kvs, kvs, kvs, kvs, pre, pre, wide, wide), (s // tm,),
                 [_rows(tm, d), _vec(d), _vec(d), _vec(d), _resident(w_in.shape), _rows(tm, LANES), _rows(tm, LANES)],
                 (_rows(tm, d), _rows(tm, d)) + (_rows(tm, LANES),) * 4 + (_rows(tm, d),) * 4, ("parallel",), after=after)(
                     x, w, sc, sh, w_in, cosf, sinf)


def _in_proj_bwd(name, dq, dkv, du, dv, dga, dgb, w_in, x, w, sc, dx_in, tm=256):
    s, d = x.shape
    tm = min(tm, s)

    def body(dq_ref, dkv_ref, du_ref, dv_ref, dga_ref, dgb_ref, win_ref, x_ref, w_ref, sc_ref, dxin_ref,
             dx_ref, da_ref, dsh_ref):
        @pl.when(pl.program_id(0) == 0)
        def _():
            da_ref[...] = jnp.zeros_like(da_ref)
            dsh_ref[...] = jnp.zeros_like(dsh_ref)

        dh = (_dot(dq_ref[...], win_ref[:Q_END, :]) + _dot(dkv_ref[...], win_ref[Q_END:KV_END, :])
              + _dot(du_ref[...], win_ref[KV_END:U_END, :]) + _dot(dv_ref[...], win_ref[U_END:Z_END, :])
              + _dot(dga_ref[...], win_ref[Z_END:GA_END, :]) + _dot(dgb_ref[...], win_ref[GA_END:, :]))
        xv = x_ref[...]
        r = _rms(xv)
        xn = xv * r
        dxn = dh * (w_ref[...] * (1.0 + sc_ref[...]))
        dx_ref[...] = dxin_ref[...] + r * (dxn - xn * jnp.mean(dxn * xn, axis=-1, keepdims=True))
        da_ref[...] += jnp.sum(dh * xn, axis=0, keepdims=True)
        dsh_ref[...] += jnp.sum(dh, axis=0, keepdims=True)

    return _call(body, name, (_sds((s, d), F32), _sds((1, d), F32), _sds((1, d), F32)), (s // tm,),
                 [_rows(tm, d), _rows(tm, 2 * LANES), _rows(tm, d), _rows(tm, d), _rows(tm, d), _rows(tm, d),
                  _resident(w_in.shape), _rows(tm, d), _vec(d), _vec(d), _rows(tm, d)],
                 (_rows(tm, d), _vec(d), _vec(d)), ("arbitrary",))(dq, dkv, du, dv, dga, dgb, w_in, x, w, sc, dx_in)


def _rope_bwd(name, dqr, dkv_cur, dkv_prev, cosf, sinf, tm=512):
    s = dqr.shape[0]
    tm = min(tm, s)
    steps = s // tm
    per = tm // ATTN_BLOCK
    nb = s // ATTN_BLOCK

    def unrope(v, cosv, sinv):
        return v * cosv - _rope_partner(v) * sinv

    def body(dq_ref, cur_ref, prev_ref, next_ref, cos_ref, sin_ref, dqo_ref, dkvo_ref):
        i = pl.program_id(0)
        cosv, sinv = cos_ref[...], sin_ref[...]
        for j in range(D_MODEL // LANES):
            dqo_ref[:, j * LANES:(j + 1) * LANES] = unrope(dq_ref[:, j * LANES:(j + 1) * LANES], cosv, sinv).astype(BF)
        nxt = jnp.where(i < steps - 1, next_ref[...], 0.0)
        if per > 1:
            shifted = jnp.concatenate([prev_ref[ATTN_BLOCK:, :], nxt], axis=0)
        else:
            shifted = nxt
        tot = cur_ref[...] + shifted
        dkvo_ref[:, :LANES] = unrope(tot[:, :LANES], cosv, sinv).astype(BF)
        dkvo_ref[:, LANES:] = tot[:, LANES:].astype(BF)

    nxt_spec = pl.BlockSpec((ATTN_BLOCK, 2 * LANES), lambda i: (jnp.minimum((i + 1) * per, nb - 1), 0))
    return _call(body, name, (_sds((s, D_MODEL), BF), _sds((s, 2 * LANES), BF)), (steps,),
                 [_rows(tm, D_MODEL), _rows(tm, 2 * LANES), _rows(tm, 2 * LANES), nxt_spec, _rows(tm, LANES),
                  _rows(tm, LANES)],
                 (_rows(tm, D_MODEL), _rows(tm, 2 * LANES)), ("parallel",))(dqr, dkv_cur, dkv_prev, dkv_prev, cosf, sinf)


Q_PER_KV = N_Q_HEADS // N_KV_HEADS
ATTN_SCALE = HEAD_DIM ** -0.5


def _band_mask_t(n):
    kj = lax.broadcasted_iota(jnp.int32, (2 * ATTN_BLOCK, ATTN_BLOCK), 0)
    qi = lax.broadcasted_iota(jnp.int32, (2 * ATTN_BLOCK, ATTN_BLOCK), 1)
    return (kj > qi) & (kj <= qi + ATTN_BLOCK) & ((n > 0) | (kj >= ATTN_BLOCK))


def _softmax_t(raw, allowed, sink):
    sc = jnp.where(allowed, raw, NEG_BIG)
    m = jnp.maximum(jnp.max(sc, axis=0, keepdims=True), sink)
    p = jnp.exp(sc - m)
    esink = jnp.exp(sink - m)
    inv = 1.0 / (jnp.sum(p, axis=0, keepdims=True) + esink)
    return p * inv, esink * inv


def _kv_specs():
    cur = pl.BlockSpec((ATTN_BLOCK, LANES), lambda n: (n, 0))
    prev = pl.BlockSpec((ATTN_BLOCK, LANES), lambda n: (jnp.maximum(n - 1, 0), 0))
    return [prev, cur] * 4


def _attention(name, qr, kk0, kk1, vv0, vv1, sinks):
    s = qr.shape[0]
    nb = s // ATTN_BLOCK

    def body(sink_ref, q_ref, k0p, k0c, k1p, k1c, v0p, v0c, v1p, v1c, y_ref):
        allowed = _band_mask_t(pl.program_id(0))
        upper = lax.broadcasted_iota(jnp.int32, (1, LANES), 1) >= HEAD_DIM
        upper_rows = lax.broadcasted_iota(jnp.int32, (LANES, 1), 0) >= HEAD_DIM
        bands = ((jnp.concatenate([k0p[...], k0c[...]], axis=0), jnp.concatenate([v0p[...], v0c[...]], axis=0)),
                 (jnp.concatenate([k1p[...], k1c[...]], axis=0), jnp.concatenate([v1p[...], v1c[...]], axis=0)))
        vbts = (bands[0][1].T, bands[1][1].T)

        def scores(h):
            hk, j, half = h // Q_PER_KV, (h % Q_PER_KV) // 2, h % 2
            col = (hk * 4 + j) * LANES
            qp = q_ref[:, col:col + LANES]
            return _dot_nt(bands[hk][0], jnp.where(upper if half else jnp.logical_not(upper), qp, jnp.zeros_like(qp)))

        out_t = None
        ahead = scores(0)
        for h in range(N_Q_HEADS):
            hk, j, half = h // Q_PER_KV, (h % Q_PER_KV) // 2, h % 2
            raw = ahead
            if h + 1 < N_Q_HEADS:
                ahead = scores(h + 1)
            pn, _ = _softmax_t(raw, allowed, sink_ref[h])
            o_h = _dot(vbts[hk], pn.astype(BF))
            out_t = jnp.where(upper_rows, o_h, out_t) if half else o_h
            if half:
                col = (hk * 4 + j) * LANES
                y_ref[:, col:col + LANES] = out_t.T.astype(BF)

    return _call(body, name, _sds((s, D_MODEL), BF), (nb,),
                 [pl.BlockSpec(memory_space=pltpu.SMEM), pl.BlockSpec((ATTN_BLOCK, D_MODEL), lambda n: (n, 0))] + _kv_specs(),
                 pl.BlockSpec((ATTN_BLOCK, D_MODEL), lambda n: (n, 0)), ("parallel",))(
                     sinks, qr, kk0, kk0, kk1, kk1, vv0, vv0, vv1, vv1)


def _attention_bwd(name, qr, kk0, kk1, vv0, vv1, sinks, dy, after=None):
    s = qr.shape[0]
    nb = s // ATTN_BLOCK

    def body(sink_ref, q_ref, dy_ref, k0p, k0c, k1p, k1c, v0p, v0c, v1p, v1c, dq_ref, cur_ref, prev_ref, dsink_ref):
        @pl.when(pl.program_id(0) == 0)
        def _():
            dsink_ref[...] = jnp.zeros_like(dsink_ref)

        allowed = _band_mask_t(pl.program_id(0))
        lane = lax.broadcasted_iota(jnp.int32, (1, LANES), 1)
        upper = lane >= HEAD_DIM
        upper_rows = lax.broadcasted_iota(jnp.int32, (LANES, 1), 0) >= HEAD_DIM
        bands = ((jnp.concatenate([k0p[...], k0c[...]], axis=0), jnp.concatenate([v0p[...], v0c[...]], axis=0)),
                 (jnp.concatenate([k1p[...], k1c[...]], axis=0), jnp.concatenate([v1p[...], v1c[...]], axis=0)))
        kbts = (bands[0][0].T, bands[1][0].T)

        def scores(h):
            hk, j, half = h // Q_PER_KV, (h % Q_PER_KV) // 2, h % 2
            kb, vb = bands[hk]
            col = (hk * 4 + j) * LANES
            sel = upper if half else jnp.logical_not(upper)
            qp = q_ref[:, col:col + LANES]
            qa = jnp.where(sel, qp, jnp.zeros_like(qp))
            dya = jnp.where(sel, dy_ref[:, col:col + LANES], 0.0).astype(BF)
            return qa, dya, _dot_nt(kb, qa), _dot_nt(vb, dya)

        dsink = jnp.zeros((1, LANES), F32)
        dk_slab = jnp.zeros((2 * ATTN_BLOCK, LANES), F32)
        dv_slab = jnp.zeros((2 * ATTN_BLOCK, LANES), F32)
        dkk = dvv = dq_t = None
        ahead = scores(0)
        for h in range(N_Q_HEADS):
            hk, j, half = h // Q_PER_KV, (h % Q_PER_KV) // 2, h % 2
            qa, dya, raw, dp = ahead
            if h + 1 < N_Q_HEADS:
                ahead = scores(h + 1)
            pn, psink = _softmax_t(raw, allowed, sink_ref[h])
            delta = jnp.sum(pn * dp, axis=0, keepdims=True)
            ds = (pn * (dp - delta)).astype(BF)
            dsink = dsink + jnp.where(lane == h, -jnp.sum(psink * delta), 0.0)
            dq_h = _dot(kbts[hk], ds) * ATTN_SCALE
            dq_t = jnp.where(upper_rows, dq_h, dq_t) if half else dq_h
            dk_h, dv_h = _dot(ds, qa), _dot(pn.astype(BF), dya)
            dkk, dvv = (dk_h, dv_h) if h % Q_PER_KV == 0 else (dkk + dk_h, dvv + dv_h)
            if half:
                col = (hk * 4 + j) * LANES
                dq_ref[:, col:col + LANES] = dq_t.T
            if h % Q_PER_KV == Q_PER_KV - 1:
                mine = upper if hk else jnp.logical_not(upper)
                dk_slab = jnp.where(mine, dkk + pltpu.roll(dkk, HEAD_DIM, axis=1), dk_slab)
                dv_slab = jnp.where(mine, dvv + pltpu.roll(dvv, HEAD_DIM, axis=1), dv_slab)
        prev_ref[:, :LANES] = dk_slab[:ATTN_BLOCK]
        prev_ref[:, LANES:] = dv_slab[:ATTN_BLOCK]
        cur_ref[:, :LANES] = dk_slab[ATTN_BLOCK:]
        cur_ref[:, LANES:] = dv_slab[ATTN_BLOCK:]
        dsink_ref[...] += dsink

    blk = pl.BlockSpec((ATTN_BLOCK, D_MODEL), lambda n: (n, 0))
    kvo = pl.BlockSpec((ATTN_BLOCK, 2 * LANES), lambda n: (n, 0))
    return _call(body, name,
                 (_sds((s, D_MODEL), F32), _sds((s, 2 * LANES), F32), _sds((s, 2 * LANES), F32), _sds((1, LANES), F32)),
                 (nb,), [pl.BlockSpec(memory_space=pltpu.SMEM), blk, blk] + _kv_specs(),
                 (blk, kvo, kvo, pl.BlockSpec((1, LANES), lambda n: (0, 0))), ("arbitrary",), after=after)(
                     sinks, qr, dy, kk0, kk0, kk1, kk1, vv0, vv0, vv1, vv1)


def _sgu_weights(wm_ref, g):
    t = lax.broadcasted_iota(jnp.int32, (SGU_CHUNK, SGU_CHUNK), 0)
    sidx = lax.broadcasted_iota(jnp.int32, (SGU_CHUNK, SGU_CHUNK), 1)
    return jnp.where(sidx <= t, wm_ref[g], 0.0).astype(BF)


def _layer_norm_stats(v):
    mu = jnp.mean(v, axis=-1, keepdims=True)
    cen = v - mu
    rstd = lax.rsqrt(jnp.mean(cen * cen, axis=-1, keepdims=True) + NORM_EPS)
    return cen * rstd, rstd


def _sgu(name, u_pre, v_pre, ln_w, ln_b, wm, bfull, tm=256):
    s, w = u_pre.shape
    tm = min(tm, s)

    def body(u_ref, v_ref, lw_ref, lb_ref, wm_ref, b_ref, y_ref):
        vhat, _ = _layer_norm_stats(_gelu(v_ref[...]))
        vn = (vhat * lw_ref[...] + lb_ref[...]).astype(BF)
        for g in range(SGU_GROUPS):
            wg = _sgu_weights(wm_ref, g)
            cols = slice(g * SGU_CHUNK, (g + 1) * SGU_CHUNK)
            for ch in range(tm // SGU_CHUNK):
                rows = slice(ch * SGU_CHUNK, (ch + 1) * SGU_CHUNK)
                f = _dot(wg, vn[rows, cols]) + b_ref[g]
                y_ref[rows, cols] = (_gelu(u_ref[rows, cols]) * f).astype(BF)

    full3 = pl.BlockSpec((SGU_GROUPS, SGU_CHUNK, SGU_CHUNK), lambda i: (0, 0, 0))
    return _call(body, name, _sds((s, w), BF), (s // tm,),
                 [_rows(tm, w), _rows(tm, w), _vec(w), _vec(w), full3, full3],
                 _rows(tm, w), ("parallel",))(u_pre, v_pre, ln_w, ln_b, wm, bfull)


def _sgu_bwd(name, u_pre, v_pre, ln_w, ln_b, wm, bfull, dy, tm=256, after=None):
    s, w = u_pre.shape
    tm = min(tm, s)
    steps = s // tm

    def body(u_ref, v_ref, lw_ref, lb_ref, wm_ref, b_ref, dy_ref, du_ref, dv_ref, dwm_ref, db_ref, dlw_ref, dlb_ref,
             dfsum_ref):
        i = pl.program_id(0)

        @pl.when(i == 0)
        def _():
            dwm_ref[...] = jnp.zeros_like(dwm_ref)
            dlw_ref[...] = jnp.zeros_like(dlw_ref)
            dlb_ref[...] = jnp.zeros_like(dlb_ref)
            dfsum_ref[...] = jnp.zeros_like(dfsum_ref)

        vpre = v_ref[...]
        vg, dvg_dv = _gelu_and_grad(vpre)
        vhat, rstd = _layer_norm_stats(vg)
        vn = (vhat * lw_ref[...] + lb_ref[...]).astype(BF)
        t = lax.broadcasted_iota(jnp.int32, (SGU_CHUNK, SGU_CHUNK), 0)
        sidx = lax.broadcasted_iota(jnp.int32, (SGU_CHUNK, SGU_CHUNK), 1)
        dvn_cols = []
        for g in range(SGU_GROUPS):
            wg = _sgu_weights(wm_ref, g)
            cols = slice(g * SGU_CHUNK, (g + 1) * SGU_CHUNK)
            dvn_rows = []
            dwg = jnp.zeros((SGU_CHUNK, SGU_CHUNK), F32)
            dfs = jnp.zeros((SGU_CHUNK, SGU_CHUNK), F32)
            for ch in range(tm // SGU_CHUNK):
                rows = slice(ch * SGU_CHUNK, (ch + 1) * SGU_CHUNK)
                upre = u_ref[rows, cols]
                dyv = dy_ref[rows, cols].astype(F32)
                f = _dot(wg, vn[rows, cols]) + b_ref[g]
                ug, dug_du = _gelu_and_grad(upre)
                du_ref[rows, cols] = (dyv * f * dug_du).astype(BF)
                df = dyv * ug
                dfb = df.astype(BF)
                dvn_rows.append(_dot_tn(wg, dfb))
                dwg = dwg + _dot_nt(dfb, vn[rows, cols])
                dfs = dfs + df
            dwm_ref[g] += jnp.where(sidx <= t, dwg, 0.0)
            dfsum_ref[g] += dfs
            dvn_cols.append(jnp.concatenate(dvn_rows, axis=0) if len(dvn_rows) > 1 else dvn_rows[0])
        dvn = jnp.concatenate(dvn_cols, axis=1)
        dlw_ref[...] += jnp.sum(dvn * vhat, axis=0, keepdims=True)
        dlb_ref[...] += jnp.sum(dvn, axis=0, keepdims=True)
        dvh = dvn * lw_ref[...]
        dvg = rstd * (dvh - jnp.mean(dvh, axis=-1, keepdims=True) - vhat * jnp.mean(dvh * vhat, axis=-1, keepdims=True))
        dv_ref[...] = (dvg * dvg_dv).astype(BF)

        @pl.when(i == steps - 1)
        def _():
            for g in range(SGU_GROUPS):
                db_ref[g:g + 1, :] = jnp.sum(dfsum_ref[g].T, axis=0, keepdims=True)

    full3 = pl.BlockSpec((SGU_GROUPS, SGU_CHUNK, SGU_CHUNK), lambda i: (0, 0, 0))
    return _call(body, name,
                 (_sds((s, w), BF), _sds((s, w), BF), _sds((SGU_GROUPS, SGU_CHUNK, SGU_CHUNK), F32),
                  _sds((SGU_GROUPS, SGU_CHUNK), F32), _sds((1, w), F32), _sds((1, w), F32)),
                 (steps,),
                 [_rows(tm, w), _rows(tm, w), _vec(w), _vec(w), full3, full3, _rows(tm, w)],
                 (_rows(tm, w), _rows(tm, w), full3, pl.BlockSpec((SGU_GROUPS, SGU_CHUNK), lambda i: (0, 0)), _vec(w), _vec(w)),
                 ("arbitrary",), scratch=[pltpu.VMEM((SGU_GROUPS, SGU_CHUNK, SGU_CHUNK), F32)], after=after)(
                     u_pre, v_pre, ln_w, ln_b, wm, bfull, dy)


def _mix_out(name, y_sgu, y_attn, ga_pre, gb_pre, x, g1, proj_a, proj_b, w_out, w2, sc2, sh2, tm=256):
    s, d = x.shape
    tm = min(tm, s)

    def body(ys_ref, ya_ref, ga_ref, gb_ref, x_ref, g1_ref, wa_ref, wb_ref, wo_ref, w2_ref, sc2_ref, sh2_ref,
             m_ref, pa_ref, pb_ref, o_ref, x1_ref, h2_ref):
        pa = _dot(ys_ref[...], wa_ref[...].reshape(d, d))
        pb = _dot(ya_ref[...], wb_ref[...].reshape(d, d))
        pa_ref[...] = pa.astype(BF)
        pb_ref[...] = pb.astype(BF)
        merged = (_sigmoid(ga_ref[...].astype(F32)) * pa + _sigmoid(gb_ref[...].astype(F32)) * pb).astype(BF)
        m_ref[...] = merged
        o = _dot(merged, wo_ref[...].reshape(d, d))
        o_ref[...] = o.astype(BF)
        x1 = x_ref[...] + g1_ref[...] * o
        x1_ref[...] = x1
        h2_ref[...] = ((x1 * _rms(x1)) * w2_ref[...] * (1.0 + sc2_ref[...]) + sh2_ref[...]).astype(BF)

    f, b = _sds((s, d), F32), _sds((s, d), BF)
    r = _rows(tm, d)
    wspec = _resident(proj_a.shape)
    return _call(body, name, (b, b, b, b, f, b), (s // tm,),
                 [r, r, r, r, r, _vec(d), wspec, wspec, wspec, _vec(d), _vec(d), _vec(d)], (r,) * 6, ("parallel",))(
                     y_sgu, y_attn, ga_pre, gb_pre, x, g1, proj_a, proj_b, w_out, w2, sc2, sh2)


def _mix_bwd(name, do, w_out, proj_a, proj_b, ga_pre, gb_pre, pa, pb, tm=256):
    s, d = do.shape
    tm = min(tm, s)

    def body(do_ref, wo_ref, wa_ref, wb_ref, ga_ref, gb_ref, pa_ref, pb_ref,
             dpa_ref, dpb_ref, dga_ref, dgb_ref, dys_ref, dya_ref):
        dm = _dot_nt(do_ref[...], wo_ref[...].reshape(d, d))
        ga = _sigmoid(ga_ref[...].astype(F32))
        gb = _sigmoid(gb_ref[...].astype(F32))
        dpa = (dm * ga).astype(BF)
        dpb = (dm * gb).astype(BF)
        dpa_ref[...] = dpa
        dpb_ref[...] = dpb
        dga_ref[...] = (dm * pa_ref[...].astype(F32) * ga * (1.0 - ga)).astype(BF)
        dgb_ref[...] = (dm * pb_ref[...].astype(F32) * gb * (1.0 - gb)).astype(BF)
        dys_ref[...] = _dot_nt(dpa, wa_ref[...].reshape(d, d)).astype(BF)
        dya_ref[...] = _dot_nt(dpb, wb_ref[...].reshape(d, d)).astype(BF)

    f, b = _sds((s, d), F32), _sds((s, d), BF)
    r = _rows(tm, d)
    wspec = _resident(w_out.shape)
    return _call(body, name, (b, b, b, b, b, b), (s // tm,), [r, wspec, wspec, wspec, r, r, r, r], (r,) * 6,
                 ("parallel",))(do, w_out, proj_a, proj_b, ga_pre, gb_pre, pa, pb)


def _ffn_up_act(name, h2, w_gate, w_up, cw, cb, tm=1024):
    s, d = h2.shape
    tm = min(tm, s)
    tc = FFN_SHARD_PAD
    per = tm // HALO

    def body(h_ref, hprev_ref, wg_ref, wu_ref, cw_ref, cb_ref, a_ref, ac_ref, up_ref, hf_ref):
        hv = h_ref[...]
        a = _dot_nt(hv, wg_ref[0]).astype(BF)
        up = _dot_nt(hv, wu_ref[0]).astype(BF)
        a_ref[...] = a
        up_ref[...] = up
        prev = jnp.where(pl.program_id(1) > 0, _dot_nt(hprev_ref[...], wg_ref[0]).astype(BF).astype(F32), 0.0)
        ext = jnp.concatenate([prev, a.astype(F32)], axis=0)
        ac = (cb_ref[...] + cw_ref[0:1, :] * pltpu.roll(ext, 2, axis=0) + cw_ref[1:2, :] * pltpu.roll(ext, 1, axis=0)
              + cw_ref[2:3, :] * ext)[HALO:]
        ac_ref[...] = ac.astype(BF)
        hf_ref[...] = (ac * _sigmoid(ac) * up.astype(F32)).astype(BF)

    wspec = pl.BlockSpec((1, tc, d), lambda j, i: (j, 0, 0))
    ospec = pl.BlockSpec((tm, tc), lambda j, i: (i, j))
    o = _sds((s, FFN_PAD), BF)
    return _call(body, name, (o, o, o, o), (N_CHIPS, s // tm),
                 [pl.BlockSpec((tm, d), lambda j, i: (i, 0)), pl.BlockSpec((HALO, d), lambda j, i: (jnp.maximum(i * per - 1, 0), 0)),
                  wspec, wspec, pl.BlockSpec((3, tc), lambda j, i: (0, j)), pl.BlockSpec((1, tc), lambda j, i: (0, j))],
                 (ospec, ospec, ospec, ospec), ("parallel", "parallel"))(h2, h2, w_gate, w_up, cw, cb)


def _ffn_down(name, hf, w_down, x1, g2, tm=512):
    s, d = x1.shape
    tm = min(tm, s)

    def body(hf_ref, wd_ref, x1_ref, g2_ref, dn_ref, x2_ref):
        dn = _dot(hf_ref[...], wd_ref[...].reshape(FFN_PAD, d))
        dn_ref[...] = dn.astype(BF)
        x2_ref[...] = x1_ref[...] + g2_ref[...] * dn

    return _call(body, name, (_sds((s, d), BF), _sds((s, d), F32)), (s // tm,),
                 [_rows(tm, FFN_PAD), _resident(w_down.shape), _rows(tm, d), _vec(d)],
                 (_rows(tm, d), _rows(tm, d)), ("parallel",))(hf, w_down, x1, g2)


def _ffn_down_bwd_act(name, dx2, dn, g2, w_down, a, ac, up, cw, tm=256, after=None):
    s, d = dx2.shape
    c = a.shape[1]
    tm = min(tm, s)
    tc = FFN_SHARD_PAD
    per = tm // HALO
    steps = s // tm
    last = s // HALO - 1
    n = tm + HALO

    def body(dx_ref, dxnext_ref, dn_ref, g2_ref, wd_ref, a_ref, ac_ref, acnext_ref, up_ref, upnext_ref, cw_ref,
             ddn_ref, da_ref, dup_ref, dg_ref, dcw_ref, dcb_ref):
        i = pl.program_id(0)

        @pl.when(i == 0)
        def _():
            dg_ref[...] = jnp.zeros_like(dg_ref)
            dcw_ref[...] = jnp.zeros_like(dcw_ref)
            dcb_ref[...] = jnp.zeros_like(dcb_ref)

        dxv = dx_ref[...]
        ddn = (dxv * g2_ref[...]).astype(BF)
        ddn_ref[...] = ddn
        dg_ref[...] += jnp.sum(dxv * dn_ref[...].astype(F32), axis=0, keepdims=True)
        ddn_next = jnp.where(i < steps - 1, dxnext_ref[...] * g2_ref[...], 0.0).astype(BF)
        ddn_ext = jnp.concatenate([ddn, ddn_next], axis=0)
        for k in range(N_CHIPS):
            cols = slice(k * tc, (k + 1) * tc)
            dh = _dot_nt(ddn_ext, wd_ref[k])
            ace = jnp.concatenate([ac_ref[:, cols].astype(F32), acnext_ref[:, cols].astype(F32)], axis=0)
            upe = jnp.concatenate([up_ref[:, cols].astype(F32), upnext_ref[:, cols].astype(F32)], axis=0)
            sig = _sigmoid(ace)
            silu = ace * sig
            dac = dh * upe * (sig + silu * (1.0 - sig))
            dup_ref[:, cols] = (dh[:tm] * silu[:tm]).astype(BF)
            d1 = pltpu.roll(dac, n - 1, axis=0)[:tm]
            d2 = pltpu.roll(dac, n - 2, axis=0)[:tm]
            d0 = dac[:tm]
            da_ref[:, cols] = (cw_ref[2:3, cols] * d0 + cw_ref[1:2, cols] * d1 + cw_ref[0:1, cols] * d2).astype(BF)
            a0 = a_ref[:, cols].astype(F32)
            dcb_ref[:, cols] += jnp.sum(d0, axis=0, keepdims=True)
            dcw_ref[0:1, cols] += jnp.sum(d2 * a0, axis=0, keepdims=True)
            dcw_ref[1:2, cols] += jnp.sum(d1 * a0, axis=0, keepdims=True)
            dcw_ref[2:3, cols] += jnp.sum(d0 * a0, axis=0, keepdims=True)

    nxt = lambda width: pl.BlockSpec((HALO, width), lambda i: (jnp.minimum((i + 1) * per, last), 0))
    wide = _sds((s, c), BF)
    return _call(body, name, (_sds((s, d), BF), wide, wide, _sds((1, d), F32), _sds((3, c), F32), _sds((1, c), F32)), (steps,),
                 [_rows(tm, d), nxt(d), _rows(tm, d), _vec(d), _resident(w_down.shape), _rows(tm, c), _rows(tm, c), nxt(c),
                  _rows(tm, c), nxt(c), pl.BlockSpec((3, c), lambda i: (0, 0))],
                 (_rows(tm, d), _rows(tm, c), _rows(tm, c), _vec(d), pl.BlockSpec((3, c), lambda i: (0, 0)), _vec(c)),
                 ("arbitrary",), after=after)(dx2, dx2, dn, g2, w_down, a, ac, ac, up, up, cw)


def _ffn_up_bwd(name, da, dup, w_gate, w_up, x1, dx2, w2, sc2, o, g1, tm=256, after=None):
    s, d = x1.shape
    tm = min(tm, s)
    tc = FFN_SHARD_PAD

    def body(da_ref, dup_ref, wg_ref, wu_ref, x1_ref, dx2_ref, w2_ref, sc2_ref, o_ref, g1_ref,
             dx1_ref, do_ref, dnw_ref, dsh_ref, dg1_ref):
        @pl.when(pl.program_id(0) == 0)
        def _():
            dnw_ref[...] = jnp.zeros_like(dnw_ref)
            dsh_ref[...] = jnp.zeros_like(dsh_ref)
            dg1_ref[...] = jnp.zeros_like(dg1_ref)

        dh = jnp.zeros((tm, d), F32)
        for k in range(N_CHIPS):
            cols = slice(k * tc, (k + 1) * tc)
            dh = dh + _dot(da_ref[:, cols], wg_ref[k]) + _dot(dup_ref[:, cols], wu_ref[k])
        xv = x1_ref[...]
        r = _rms(xv)
        xn = xv * r
        dxn = dh * (w2_ref[...] * (1.0 + sc2_ref[...]))
        dx1 = dx2_ref[...] + r * (dxn - xn * jnp.mean(dxn * xn, axis=-1, keepdims=True))
        dx1_ref[...] = dx1
        dnw_ref[...] += jnp.sum(dh * xn, axis=0, keepdims=True)
        dsh_ref[...] += jnp.sum(dh, axis=0, keepdims=True)
        do_ref[...] = (dx1 * g1_ref[...]).astype(BF)
        dg1_ref[...] += jnp.sum(dx1 * o_ref[...].astype(F32), axis=0, keepdims=True)

    v = _sds((1, d), F32)
    r = _rows(tm, d)
    wspec = _resident(w_gate.shape)
    return _call(body, name, (_sds((s, d), F32), _sds((s, d), BF), v, v, v), (s // tm,),
                 [_rows(tm, FFN_PAD), _rows(tm, FFN_PAD), wspec, wspec, r, r, _vec(d), _vec(d), r, _vec(d)],
                 (r, r, _vec(d), _vec(d), _vec(d)), ("arbitrary",), after=after)(da, dup, w_gate, w_up, x1, dx2, w2, sc2, o, g1)


def _loss_head(name, x, w, target, tm=512):
    s, d = x.shape
    tm = min(tm, s)

    def body(x_ref, w_ref, t_ref, dx_ref, loss_ref, dw_ref):
        @pl.when(pl.program_id(0) == 0)
        def _():
            loss_ref[...] = jnp.zeros_like(loss_ref)
            dw_ref[...] = jnp.zeros_like(dw_ref)

        xv = x_ref[...]
        r = _rms(xv)
        xn = xv * r
        err = xn * w_ref[...] - t_ref[...]
        loss_ref[...] += 0.5 * jnp.sum(jnp.mean(err * err, axis=-1, keepdims=True))
        dy = err * (1.0 / d)
        dw_ref[...] += jnp.sum(dy * xn, axis=0, keepdims=True)
        dxn = dy * w_ref[...]
        dx_ref[...] = r * (dxn - xn * jnp.mean(dxn * xn, axis=-1, keepdims=True))

    return _call(body, name, (_sds((s, d), F32), _sds((1, LANES), F32), _sds((1, d), F32)), (s // tm,),
                 [_rows(tm, d), _vec(d), _rows(tm, d)], (_rows(tm, d), _vec(LANES), _vec(d)), ("arbitrary",))(x, w, target)


def _layer_fwd(l, x, mod, p, cosf, sinf, after=None, late=None, later=None):
    sh1, sc1, g1, sh2, sc2, g2 = mod
    tag = f"l{l}_"
    h, qr, kk0, kk1, vv0, vv1, u_pre, v_pre, ga_pre, gb_pre = _in_proj(
        tag + "in_proj", x, p["norm1_w"], sc1, sh1, p["w_in"], cosf, sinf, after=after)
    y_attn = _attention(tag + "attn", qr, kk0, kk1, vv0, vv1, p["sinks"])
    y_sgu = _sgu(tag + "sgu", u_pre, v_pre, p["sgu_ln_w"], p["sgu_ln_b"], p["sgu_w"], p["sgu_bfull"])
    if late is not None:
        p = dict(p, **late(y_sgu))
    merged, pa, pb, o, x1, h2 = _mix_out(tag + "mix_out", y_sgu, y_attn, ga_pre, gb_pre, x, g1, p["proj_a"], p["proj_b"],
                                         p["w_out"], p["norm2_w"], sc2, sh2)
    if later is not None:
        p = dict(p, **later(h2))
    a, ac, up, hf = _ffn_up_act(tag + "ffn_up", h2, p["w_gate"], p["w_up"], p["conv_w"], p["conv_b"])
    dn, x2 = _ffn_down(tag + "ffn_down", hf, p["w_down"], x1, g2)
    saved = dict(x=x, h=h, qr=qr, kk0=kk0, kk1=kk1, vv0=vv0, vv1=vv1, u_pre=u_pre, v_pre=v_pre, ga_pre=ga_pre,
                 gb_pre=gb_pre, y_attn=y_attn, y_sgu=y_sgu, merged=merged, pa=pa, pb=pb, o=o, x1=x1, h2=h2, a=a, ac=ac, up=up,
                 hf=hf, dn=dn)
    return x2, saved, p


def _layer_bwd(l, dx2, mod, p, sv, cosf, sinf, after=None, emit=None, tick=None):
    sh1, sc1, g1, sh2, sc2, g2 = mod
    tag = f"l{l}_b_"
    d = D_MODEL
    g = {}
    ready = (lambda names: emit({k: g.pop(k) for k in names})) if emit else (lambda names: None)
    tick = tick or (lambda y: None)
    ddn, da, dup, dg2, g["conv_w"], g["conv_b"] = _ffn_down_bwd_act(
        tag + "ffn_down", dx2, sv["dn"], g2, p["w_down"], sv["a"], sv["ac"], sv["up"], p["conv_w"], after=after)
    g["w_down"] = _matmul_tn(tag + "dw_down", sv["hf"], ddn, tk=FFN_SHARD_PAD, after=tick(ddn)).reshape(N_CHIPS, FFN_SHARD_PAD, d)
    g["w_gate"] = _matmul_tn(tag + "dw_gate", da, sv["h2"], tk=FFN_SHARD_PAD).reshape(N_CHIPS, FFN_SHARD_PAD, d)
    g["w_up"] = _matmul_tn(tag + "dw_up", dup, sv["h2"], tk=FFN_SHARD_PAD).reshape(N_CHIPS, FFN_SHARD_PAD, d)
    dx1, do, da2, dsh2, dg1 = _ffn_up_bwd(tag + "ffn_up", da, dup, p["w_gate"], p["w_up"], sv["x1"], dx2, p["norm2_w"],
                                          sc2, sv["o"], g1, after=ready(("w_down", "w_gate", "w_up")))
    g["norm2_w"] = da2 * (1.0 + sc2)
    dsc2 = da2 * p["norm2_w"]
    g["w_out"] = _matmul_tn(tag + "dw_out", sv["merged"], do, after=tick(do)).reshape(N_CHIPS, d // N_CHIPS, d)
    dpa, dpb, dga, dgb, dy_sgu, dy_attn = _mix_bwd(tag + "mix", do, p["w_out"], p["proj_a"], p["proj_b"], sv["ga_pre"],
                                                  sv["gb_pre"], sv["pa"], sv["pb"])
    g["proj_a"] = _matmul_tn(tag + "dproj_a", sv["y_sgu"], dpa).reshape(N_CHIPS, d // N_CHIPS, d)
    g["proj_b"] = _matmul_tn(tag + "dproj_b", sv["y_attn"], dpb).reshape(N_CHIPS, d // N_CHIPS, d)
    du, dv, g["sgu_w"], g["sgu_b"], g["sgu_ln_w"], g["sgu_ln_b"] = _sgu_bwd(
        tag + "sgu", sv["u_pre"], sv["v_pre"], p["sgu_ln_w"], p["sgu_ln_b"], p["sgu_w"], p["sgu_bfull"], dy_sgu,
        after=ready(("w_out", "proj_a", "proj_b")))
    dqr, dkv_cur, dkv_prev, dsink = _attention_bwd(tag + "attn", sv["qr"], sv["kk0"], sv["kk1"], sv["vv0"], sv["vv1"],
                                                   p["sinks"], dy_attn, after=tick(du))
    g["sinks"] = dsink[0, :N_Q_HEADS]
    dq, dkv = _rope_bwd(tag + "rope", dqr, dkv_cur, dkv_prev, cosf, sinf)
    dw_in, row0 = None, 0
    for n, t in (("q", dq), ("kv", dkv), ("u", du), ("v", dv), ("ga", dga), ("gb", dgb)):
        dw_in = _matmul_tn_rows(tag + "dw_in_" + n, t, sv["h"], dw_in, row0, IN_COLS)
        row0 += t.shape[1]
    g["w_in"] = dw_in.reshape(N_CHIPS, IN_COLS // N_CHIPS, d)
    dx, da1, dsh1 = _in_proj_bwd(tag + "in_proj", dq, dkv, du, dv, dga, dgb, p["w_in"], sv["x"], p["norm1_w"], sc1, dx1)
    g["norm1_w"] = da1 * (1.0 + sc1)
    dsc1 = da1 * p["norm1_w"]
    return dx, (dsh1, dsc1, dg1, dsh2, dsc2, dg2), g


def _pad_to(a, axis, size):
    pad = [(0, 0)] * a.ndim
    pad[axis] = (0, size - a.shape[axis])
    return jnp.pad(a, pad)


def _early_params(w_in, small):
    d = D_MODEL
    return dict(
        w_in=w_in.reshape(IN_COLS, d), norm1_w=small["norm1_w"].reshape(1, d), sinks=small["sinks"],
        sgu_ln_w=small["sgu_ln_w"].reshape(1, d), sgu_ln_b=small["sgu_ln_b"].reshape(1, d), sgu_w=small["sgu_w"],
        sgu_bfull=jnp.broadcast_to(small["sgu_b"][:, :, None], (SGU_GROUPS, SGU_CHUNK, SGU_CHUNK)))


def _mix_params(proj_a, proj_b, w_out, small):
    return dict(proj_a=proj_a, proj_b=proj_b, w_out=w_out, norm2_w=small["norm2_w"].reshape(1, D_MODEL))


def _ffn_params(w_gate, w_up, w_down, conv_w, small):
    return dict(
        w_gate=w_gate, w_up=w_up, w_down=w_down, conv_w=conv_w.transpose(1, 0, 2).reshape(3, FFN_PAD),
        conv_b=_pad_to(small["conv_b"].reshape(N_CHIPS, FFN_SHARD), 1, FFN_SHARD_PAD).reshape(1, FFN_PAD))


def _layer_params(w_in, proj_a, proj_b, w_out, w_gate, w_up, w_down, conv_w, small):
    return dict(_early_params(w_in, small), **_mix_params(proj_a, proj_b, w_out, small),
                **_ffn_params(w_gate, w_up, w_down, conv_w, small))


def _conv_grads_natural(g):
    cw = g["conv_w"].reshape(3, N_CHIPS, FFN_SHARD_PAD)[:, :, :FFN_SHARD].reshape(3, FFN_DIM)
    cb = g["conv_b"].reshape(N_CHIPS, FFN_SHARD_PAD)[:, :FFN_SHARD].reshape(FFN_DIM)
    return cw, cb


def _rope_tables(positions):
    inv_freq = ROPE_THETA ** (-jnp.arange(0, ROT_DIM, 2, dtype=F32) / ROT_DIM)
    ang = positions.astype(F32)[:, None] * inv_freq
    cos, sin = jnp.cos(ang), jnp.sin(ang)
    s = positions.shape[0]
    rest = HEAD_DIM - ROT_DIM
    cos_head = jnp.concatenate([cos, cos, jnp.ones((s, rest), F32)], axis=1)
    sin_head = jnp.concatenate([-sin, sin, jnp.zeros((s, rest), F32)], axis=1)
    return jnp.tile(cos_head, (1, LANES // HEAD_DIM)), jnp.tile(sin_head, (1, LANES // HEAD_DIM))


ADA_ROWS = 16


def _ada_fwd(name, c_rows, ada_w, ada_b_cols, tn=512):
    depth, d, n = ada_w.shape

    def body(c_ref, w_ref, b_ref, o_ref):
        cv = c_ref[...]
        act = (cv * _sigmoid(cv)).astype(BF)
        o_ref[0] = _dot(act, w_ref[0].astype(BF)) + b_ref[0]

    return _call(body, name, _sds((depth, ADA_ROWS, n), F32), (depth, n // tn),
                 [pl.BlockSpec((ADA_ROWS, d), lambda l, j: (0, 0)), pl.BlockSpec((1, d, tn), lambda l, j: (l, 0, j)),
                  pl.BlockSpec((1, 1, tn), lambda l, j: (l, 0, j))],
                 pl.BlockSpec((1, ADA_ROWS, tn), lambda l, j: (l, 0, j)), ("parallel", "parallel"))(c_rows, ada_w, ada_b_cols)


def _ada_bwd(name, c_rows, dmod_cols, tn=512):
    depth, _, n = dmod_cols.shape
    d = c_rows.shape[1]

    def body(c_ref, dm_ref, o_ref):
        cv = c_ref[...]
        act = (cv * _sigmoid(cv)).astype(BF)
        o_ref[0] = _dot_tn(act, dm_ref[0].astype(BF))

    return _call(body, name, _sds((depth, d, n), F32), (depth, n // tn),
                 [pl.BlockSpec((ADA_ROWS, d), lambda l, j: (0, 0)), pl.BlockSpec((1, ADA_ROWS, tn), lambda l, j: (l, 0, j))],
                 pl.BlockSpec((1, d, tn), lambda l, j: (l, 0, j)), ("parallel", "parallel"))(c_rows, dmod_cols)


def _colsum(name, a):
    r, n = a.shape

    def body(a_ref, o_ref):
        o_ref[...] = jnp.sum(a_ref[...], axis=0, keepdims=True)

    return _call(body, name, _sds((1, n), F32), (1,), [pl.BlockSpec((r, n), lambda i: (0, 0))],
                 pl.BlockSpec((1, n), lambda i: (0, 0)), ("arbitrary",))(a)


REL_SIBLING = (0, 0, 1)
REL_CHIPS = ((1, 0, 0), (0, 1, 0), (1, 1, 0))
REL_ALL = tuple((fx, fy, fc) for fx in (0, 1) for fy in (0, 1) for fc in (0, 1) if fx or fy or fc)


def _chip_of(dev):
    return 2 * dev[0] + dev[1]


def _dev_of(dev):
    return 4 * dev[0] + 2 * dev[1] + dev[2]


def _flip(dev, rel):
    return tuple(1 - m if f else m for m, f in zip(dev, rel))


def _exchange(name, arrays, n_out, stages, aliases=None):
    out_shapes, stages = stages[0], stages[1:]
    n_in = len(arrays)
    aliases = aliases or {}
    n_remote = sum(len(plan) for plan, _ in stages)
    n_local = sum(len(local) for _, local in stages)

    def at(ref, idx):
        return ref.at[idx] if len(idx) else ref

    def body(*refs):
        bufs = list(refs[:n_in + n_out])
        for i_in, i_out in aliases.items():
            bufs[i_in] = bufs[n_in + i_out]
        send_sems, recv_sems, local_sems = refs[n_in + n_out:]
        me = (lax.axis_index("x"), lax.axis_index("y"), lax.axis_index("c"))
        base_r = base_l = 0
        pending = []
        for plan, local in stages:
            def remote(k, entry, sender, receiver):
                rel, si, ssel, di, dsel = entry
                return pltpu.make_async_remote_copy(
                    src_ref=at(bufs[si], ssel(sender, receiver)), dst_ref=at(bufs[di], dsel(sender, receiver)),
                    send_sem=send_sems.at[k], recv_sem=recv_sems.at[k], device_id=_flip(me, rel), device_id_type=MESH)

            sends = [remote(base_r + k, e, me, _flip(me, e[0])) for k, e in enumerate(plan)]
            for cp in sends:
                cp.start()
            for k, (si, ssel, di, dsel) in enumerate(local):
                cp = pltpu.make_async_copy(at(bufs[si], ssel(me)), at(bufs[di], dsel(me)), local_sems.at[base_l + k])
                cp.start()
                pending.append(cp.wait)
            for k, e in enumerate(plan):
                remote(base_r + k, e, _flip(me, e[0]), me).wait_recv()
            pending += [cp.wait_send for cp in sends]
            base_r += len(plan)
            base_l += len(local)
        for wait in pending:
            wait()

    any_spec = pl.BlockSpec(memory_space=pl.ANY)
    return pl.pallas_call(
        body, name=name, out_shape=tuple(out_shapes), in_specs=[any_spec] * n_in, out_specs=tuple([any_spec] * n_out),
        input_output_aliases=dict(aliases),
        scratch_shapes=[pltpu.SemaphoreType.DMA((max(n_remote, 1),)), pltpu.SemaphoreType.DMA((max(n_remote, 1),)),
                        pltpu.SemaphoreType.DMA((max(n_local, 1),))])(*arrays)


HBM_SPEC = pl.BlockSpec(memory_space=pltpu.HBM)
SEM_SPEC = pl.BlockSpec(memory_space=pltpu.SEMAPHORE)


def _split_copies(bufs, plan, local, send_sems, recv_sems, local_sems):
    me = (lax.axis_index("x"), lax.axis_index("y"), lax.axis_index("c"))

    def at(ref, idx):
        return ref.at[idx] if len(idx) else ref

    def remote(k, sender, receiver):
        rel, si, ssel, di, dsel = plan[k]
        return pltpu.make_async_remote_copy(
            src_ref=at(bufs[si], ssel(sender, receiver)), dst_ref=at(bufs[di], dsel(sender, receiver)),
            send_sem=send_sems.at[k], recv_sem=recv_sems.at[k], device_id=_flip(me, rel), device_id_type=MESH)

    sends = [remote(k, me, _flip(me, plan[k][0])) for k in range(len(plan))]
    arrivals = [remote(k, _flip(me, plan[k][0]), me) for k in range(len(plan))]
    locs = [pltpu.make_async_copy(at(bufs[si], ssel(me)), at(bufs[di], dsel(me)), local_sems.at[k])
            for k, (si, ssel, di, dsel) in enumerate(local)]
    return sends, arrivals, locs


def _exchange_start(name, arrays, out_shapes, plan, local, inplace=False):
    n_in, n_out = len(arrays), len(out_shapes)
    n_buf = n_in + n_out

    def body(*refs):
        sems = refs[n_buf:n_buf + 3]
        bufs = refs[n_buf + 3:2 * n_buf + 3]
        sends, _, locs = _split_copies(bufs * 2 if inplace else bufs, plan, local, *sems)
        for cp in sends + locs:
            cp.start()
        refs[-1][...] = jnp.zeros_like(refs[-1])

    zones = [lax.empty(o.shape, o.dtype) for o in out_shapes]
    operands = [pltpu.with_memory_space_constraint(a, pltpu.HBM) for a in list(arrays) + zones]
    sem = lambda n: pltpu.SemaphoreType.DMA((max(n, 1),))
    out = pl.pallas_call(
        body, name=name,
        out_shape=(sem(len(plan)), sem(len(plan)), sem(len(local)), *[pltpu.HBM(a.shape, a.dtype) for a in operands],
                   _sds((SUBLANES, LANES), F32)),
        in_specs=[HBM_SPEC] * n_buf,
        out_specs=(SEM_SPEC, SEM_SPEC, SEM_SPEC, *[HBM_SPEC] * n_buf, pl.BlockSpec(memory_space=pltpu.VMEM)),
        input_output_aliases={i: 3 + i for i in range(n_buf)},
        compiler_params=pltpu.CompilerParams(has_side_effects=pltpu.SideEffectType.DATAFLOW_SIDE_EFFECTING))(*operands)
    pending = dict(name=name, sems=out[:3], thru=out[3:3 + n_in], zones=out[3 + n_in:3 + n_buf], plan=plan, local=local,
                   inplace=inplace)
    return pending, out[-1]


def _exchange_wait(pending, after, both=False):
    thru, zones, plan, local, inplace = (pending[k] for k in ("thru", "zones", "plan", "local", "inplace"))
    n_in, n_buf = len(thru), len(thru) + len(zones)

    def body(*refs):
        bufs = refs[:n_buf]
        sends, arrivals, locs = _split_copies(bufs * 2 if inplace else bufs, plan, local, *refs[n_buf:n_buf + 3])
        for cp in arrivals:
            cp.wait_recv()
        for cp in sends:
            cp.wait_send()
        for cp in locs:
            cp.wait()

    out = pl.pallas_call(
        body, name=pending["name"] + "_wait", out_shape=tuple(pltpu.HBM(a.shape, a.dtype) for a in list(thru) + list(zones)),
        in_specs=[HBM_SPEC] * n_buf + [SEM_SPEC] * 3 + [pl.BlockSpec(memory_space=pl.ANY)],
        out_specs=tuple([HBM_SPEC] * n_buf), input_output_aliases={i: i for i in range(n_buf)},
        compiler_params=pltpu.CompilerParams(has_side_effects=pltpu.SideEffectType.DATAFLOW_SIDE_EFFECTING))(
            *thru, *zones, *pending["sems"], after)
    if both:
        return out[:n_in], out[n_in:]
    return out[:n_in] if inplace else out[n_in:]


def _whole(*_):
    return ()


def _half_rows(rows, core):
    return pl.ds(core * (rows // 2), rows // 2)


def _gather_weights_plan(shards):
    n = len(shards)
    dsts = [_sds((N_CHIPS,) + a.shape, a.dtype) for a in shards]
    fetch, forward = [], []
    for t, a in enumerate(shards):
        rows = a.shape[0]
        if rows % (2 * 16) == 0:
            fetch += [(rel, t, (lambda s_, r_, rows=rows: (_half_rows(rows, s_[2]),)), n + t,
                       (lambda s_, r_, rows=rows: (_chip_of(s_), _half_rows(rows, s_[2])))) for rel in REL_CHIPS]
            forward += [(REL_SIBLING, n + t, (lambda s_, r_, rows=rows, rel=rel: (_chip_of(_flip(s_, rel)), _half_rows(rows, s_[2]))),
                         n + t, (lambda s_, r_, rows=rows, rel=rel: (_chip_of(_flip(s_, rel)), _half_rows(rows, s_[2]))))
                        for rel in REL_CHIPS]
        else:
            fetch += [(rel, t, _whole, n + t, lambda s_, r_: (_chip_of(s_),)) for rel in REL_CHIPS]
    local = [(t, _whole, n + t, lambda me: (_chip_of(me),)) for t in range(n)]
    return dsts, fetch, local, forward


def _gather_weights_start(name, shards):
    dsts, fetch, local, forward = _gather_weights_plan(shards)
    pending, token = _exchange_start(name, shards, dsts, fetch, local)
    return dict(pending, forward=forward), token


def _gather_weights_finish(pending, after):
    landed = _exchange_wait(pending, after)
    n = len(landed)
    return _exchange(pending["name"] + "_forward", landed, n, [[_sds(a.shape, a.dtype) for a in landed], (pending["forward"], [])],
                     aliases={t: t for t in range(n)})


def _gather_chips(name, arrays):
    n = len(arrays)
    dsts = [_sds((N_CHIPS,) + a.shape, a.dtype) for a in arrays]
    plan = [(rel, t, _whole, n + t, lambda s_, r_: (_chip_of(s_),)) for t in range(n) for rel in REL_CHIPS]
    local = [(t, _whole, n + t, lambda me: (_chip_of(me),)) for t in range(n)]
    return _exchange(name, arrays, n, [dsts, (plan, local)])


def _gather_all(name, a):
    plan = [(rel, 0, _whole, 1, lambda s_, r_: (_dev_of(s_),)) for rel in REL_ALL]
    local = [(0, _whole, 1, lambda me: (_dev_of(me),))]
    return _exchange(name, [a], 1, [[_sds((2 * N_CHIPS,) + a.shape, a.dtype)], (plan, local)])[0]


def _swap_halves_start(name, grads):
    n = len(grads)
    dsts = [_sds((g.shape[0], g.shape[1] // 2, g.shape[2]), g.dtype) for g in grads]
    plan = [(REL_SIBLING, t, (lambda s_, r_, rows=g.shape[1]: (pl.ds(0, N_CHIPS), _half_rows(rows, r_[2]))), n + t, _whole)
            for t, g in enumerate(grads)]
    return _exchange_start(name, grads, dsts, plan, [])


def _scatter_chips_plan(sums):
    n = len(sums)
    dsts = [_sds(a.shape, a.dtype) for a in sums]
    plan = [(rel, t, lambda s_, r_: (_chip_of(r_),), n + t, lambda s_, r_: (_chip_of(s_),))
            for t in range(n) for rel in REL_CHIPS]
    local = [(t, lambda me: (_chip_of(me),), n + t, lambda me: (_chip_of(me),)) for t in range(n)]
    return dsts, plan, local


def _scatter_chips(name, sums):
    dsts, plan, local = _scatter_chips_plan(sums)
    return _exchange(name, sums, len(sums), [dsts, (plan, local)])


def _scatter_chips_start(name, sums):
    dsts, plan, local = _scatter_chips_plan(sums)
    return _exchange_start(name, sums, dsts, plan, local)


def _swap_back_start(name, totals, layer):
    n = len(totals)
    plan = [(REL_SIBLING, n + t, (lambda s_, r_, rows=a.shape[1]: (layer, _half_rows(rows, s_[2]))),
             n + t, (lambda s_, r_, rows=a.shape[1]: (layer, _half_rows(rows, s_[2])))) for t, a in enumerate(totals)]
    return _exchange_start(name, totals, [], plan, [], inplace=True)


def _add_halves(name, g, recv, core):
    nch, half, c = recv.shape

    def body(core_ref, g_ref, r_ref, o_ref):
        o_ref[0] = (g_ref[0, 0].astype(F32) + r_ref[0].astype(F32)).astype(o_ref.dtype)

    spec = pltpu.PrefetchScalarGridSpec(
        num_scalar_prefetch=1, grid=(nch,),
        in_specs=[pl.BlockSpec((1, 1, half, c), lambda k, core_ref: (k, core_ref[0], 0, 0)),
                  pl.BlockSpec((1, half, c), lambda k, core_ref: (k, 0, 0))],
        out_specs=pl.BlockSpec((1, half, c), lambda k, core_ref: (k, 0, 0)))
    return pl.pallas_call(body, name=name, out_shape=_sds(recv.shape, recv.dtype), grid_spec=spec,
                          compiler_params=pltpu.CompilerParams(dimension_semantics=("parallel",),
                                                               vmem_limit_bytes=VMEM_LIMIT))(
                                                                   core, g.reshape(nch, 2, half, c), recv)


def _sum_chips(name, a, core, layer, total):
    nch, half, c = a.shape

    def body(core_ref, a_ref, *rest):
        o_ref = rest[-1]
        acc = a_ref[0].astype(F32)
        for k in range(1, nch):
            acc = acc + a_ref[k].astype(F32)
        o_ref[0, 0] = acc

    in_specs = [pl.BlockSpec((nch, half, c), lambda i, core_ref: (0, 0, 0))]
    args = [core, a]
    if total is not None:
        in_specs.append(pl.BlockSpec(memory_space=pl.ANY))
        args.append(total.reshape(DEPTH, 2, half, c))
    spec = pltpu.PrefetchScalarGridSpec(
        num_scalar_prefetch=1, grid=(1,), in_specs=in_specs,
        out_specs=pl.BlockSpec((1, 1, half, c), lambda i, core_ref: (layer, core_ref[0], 0, 0)))
    out = pl.pallas_call(body, name=name, out_shape=_sds((DEPTH, 2, half, c), F32), grid_spec=spec,
                         input_output_aliases={2: 0} if total is not None else {},
                         compiler_params=pltpu.CompilerParams(dimension_semantics=("arbitrary",),
                                                              vmem_limit_bytes=VMEM_LIMIT))(*args)
    return out.reshape(DEPTH, 2 * half, c)


def _adamw_update(w, g, m, v):
    mn = ADAM_B1 * m + (1.0 - ADAM_B1) * g
    vn = ADAM_B2 * v + (1.0 - ADAM_B2) * (g * g)
    m_hat = mn / (1.0 - ADAM_B1 ** ADAM_STEP)
    v_hat = vn / (1.0 - ADAM_B2 ** ADAM_STEP)
    return -ADAM_LR * (m_hat / (jnp.sqrt(v_hat) + ADAM_EPS) + ADAM_WD * w), mn, vn


def _adamw(name, w, g, m, v):
    depth, r, c = w.shape
    tr = next(t for t in (512, 448, 384, 352, 336, 256, 192, 128, 64, 32, 16, 8) if r % t == 0 and t * c <= ADAM_TILE_ELEMS)

    def body(w_ref, g_ref, m_ref, v_ref, go_ref, d_ref, mo_ref, vo_ref):
        gv = g_ref[...]
        go_ref[...] = gv
        d_ref[...], mo_ref[...], vo_ref[...] = _adamw_update(w_ref[...], gv, m_ref[...], v_ref[...])

    spec = pl.BlockSpec((1, tr, c), lambda l, i: (l, i, 0))
    o = _sds(w.shape, F32)
    return _call(body, name, (o, o, o, o), (depth, r // tr), [spec] * 4, (spec,) * 4, ("parallel", "parallel"))(w, g, m, v)


def _adamw_small(name, ws, gs, ms, vs):
    n = len(ws)

    def body(*refs):
        for t in range(n):
            w_ref, g_ref, m_ref, v_ref = (refs[k * n + t] for k in range(4))
            d_ref, mo_ref, vo_ref = (refs[(4 + k) * n + t] for k in range(3))
            d_ref[...], mo_ref[...], vo_ref[...] = _adamw_update(w_ref[...], g_ref[...], m_ref[...], v_ref[...])

    outs = [_sds(w.shape, F32) for w in ws]
    res = pl.pallas_call(body, name=name, out_shape=tuple(outs * 3))(*ws, *gs, *ms, *vs)
    return res[:n], res[n:2 * n], res[2 * n:]


def _pack(arrays, rows):
    flat = jnp.concatenate([a.reshape(-1).astype(F32) for a in arrays])
    return _pad_to(flat, 0, rows * LANES).reshape(rows, LANES)


def _unpack(packed, shapes):
    flat = packed.reshape(-1)
    out, off = [], 0
    for shp in shapes:
        n = 1
        for s_ in shp:
            n *= s_
        out.append(flat[off:off + n].reshape(shp))
        off += n
    return out


_MATRICES = ("w_in", "proj_a", "proj_b", "w_out", "w_gate", "w_up", "w_down")
_SMALL = (("norm1_w", (D_MODEL,)), ("sinks", (N_Q_HEADS,)), ("sgu_ln_w", (SGU_WIDTH,)), ("sgu_ln_b", (SGU_WIDTH,)),
          ("sgu_w", (SGU_GROUPS, SGU_CHUNK, SGU_CHUNK)), ("sgu_b", (SGU_GROUPS, SGU_CHUNK)), ("norm2_w", (D_MODEL,)),
          ("conv_w", (3, FFN_DIM)), ("conv_b", (FFN_DIM,)), ("final_norm_w", (D_MODEL,)))
SMALL_ROWS = 320
ADAM_TILE_ELEMS = 384 * 1024


def _reduce_cores_start(tag, partial):
    names = list(partial)
    pending, token = _swap_halves_start(tag + "_cores", [partial[k] for k in names])
    return dict(pending, tag=tag, names=names), token


def _reduce_chips_start(pending, core, after):
    tag, names = pending["tag"], pending["names"]
    mine, theirs = _exchange_wait(pending, after, both=True)
    sums = [_add_halves(f"{tag}_cores_add_{k}", g, r, core) for k, g, r in zip(names, mine, theirs)]
    scatter, token = _scatter_chips_start(tag + "_chips", sums)
    return dict(scatter, tag=tag, names=names), token


def _reduce_back_start(pending, l, core, totals, after):
    tag, names = pending["tag"], pending["names"]
    sums = [_sum_chips(f"{tag}_chips_add_{k}", a, core, l, totals.get(k)) for k, a in zip(names, _exchange_wait(pending, after))]
    back, _ = _swap_back_start(tag + "_back", sums, l)
    return dict(back, names=names)


def kernel(x, c, positions, ada_w, ada_b, norm1_w, w_in, attn_sinks, sgu_ln_w, sgu_ln_b, sgu_w, sgu_b, proj_a, proj_b, w_out, norm2_w, ffn_w_gate, ffn_w_up, ffn_conv_w, ffn_conv_b, ffn_w_down, final_norm_w, loss_target, m_ada_w, m_ada_b, m_norm1_w, m_w_in, m_attn_sinks, m_sgu_ln_w, m_sgu_ln_b, m_sgu_w, m_sgu_b, m_proj_a, m_proj_b, m_w_out, m_norm2_w, m_ffn_w_gate, m_ffn_w_up, m_ffn_conv_w, m_ffn_conv_b, m_ffn_w_down, m_final_norm_w, v_ada_w, v_ada_b, v_norm1_w, v_w_in, v_attn_sinks, v_sgu_ln_w, v_sgu_ln_b, v_sgu_w, v_sgu_b, v_proj_a, v_proj_b, v_w_out, v_norm2_w, v_ffn_w_gate, v_ffn_w_up, v_ffn_conv_w, v_ffn_conv_b, v_ffn_w_down, v_final_norm_w):
    d = D_MODEL
    ax, ay, ac = lax.axis_index("x"), lax.axis_index("y"), lax.axis_index("c")
    chip = 2 * ax + ay
    dev = 4 * ax + 2 * ay + ac
    core = ac.astype(jnp.int32).reshape(1)

    c_all = _gather_all("gather_cond", c.reshape(SUBLANES, d // SUBLANES)).reshape(2 * N_CHIPS, d)
    c_rows = _pad_to(c_all, 0, ADA_ROWS)
    ada_cols = ada_w.shape[2]
    ada_b_cols = lax.dynamic_slice_in_dim(ada_b, chip * ada_cols, ada_cols, axis=1).reshape(DEPTH, 1, ada_cols)
    mod_cols = _ada_fwd("ada_fwd", c_rows, ada_w, ada_b_cols)
    mod_all = _gather_chips("gather_mod", [mod_cols])[0]
    mod_mine = lax.dynamic_index_in_dim(mod_all, dev, axis=2, keepdims=False)
    mod_mine = mod_mine.transpose(1, 0, 2).reshape(DEPTH, 1, 6 * d)
    mods = [tuple(jnp.split(mod_mine[l], 6, axis=-1)) for l in range(DEPTH)]

    tr = lambda a: jnp.swapaxes(a, 1, 2)
    shards = [tr(w_in).astype(BF), proj_a.astype(BF), proj_b.astype(BF), w_out.astype(BF),
              _pad_to(tr(ffn_w_gate).astype(BF), 1, FFN_SHARD_PAD), _pad_to(tr(ffn_w_up).astype(BF), 1, FFN_SHARD_PAD),
              _pad_to(ffn_w_down.astype(BF), 1, FFN_SHARD_PAD), _pad_to(ffn_conv_w, 2, FFN_SHARD_PAD)]
    token = mod_all[0, 0, :SUBLANES, :LANES]
    fetches = []
    for l in range(DEPTH):
        groups = []
        for tag, members in (("in", shards[:1]), ("mix", shards[1:4]), ("ffn", shards[4:])):
            behind = (token[0, 0] * 0.0).astype(members[0].dtype)
            pending, token = _gather_weights_start(f"l{l}_gather_{tag}", [members[0][l] + behind] + [a[l] for a in members[1:]])
            groups.append(pending)
        fetches.append(groups)

    small_in = dict(norm1_w=norm1_w, sinks=attn_sinks, sgu_ln_w=sgu_ln_w, sgu_ln_b=sgu_ln_b, sgu_w=sgu_w, sgu_b=sgu_b,
                    norm2_w=norm2_w, conv_b=ffn_conv_b)
    cosf, sinf = _rope_tables(positions[0])
    small_of = lambda l: {k: v[l] for k, v in small_in.items()}

    h = x[0]
    saved, params = [], []
    for l in range(DEPTH):
        first, mix, ffn = fetches[l]
        w_in_l = _gather_weights_finish(first, token if l == 0 else h)
        late = lambda y, l=l, mix=mix: _mix_params(*_gather_weights_finish(mix, y), small_of(l))
        later = lambda y, l=l, ffn=ffn: _ffn_params(*_gather_weights_finish(ffn, y), small_of(l))
        h, sv, p = _layer_fwd(l, h, mods[l], _early_params(w_in_l[0], small_of(l)), cosf, sinf, late=late, later=later)
        saved.append(sv)
        params.append(p)
    dx, loss_part, d_final = _loss_head("loss_head", h, final_norm_w.reshape(1, d), loss_target[0])
    loss = lax.psum(loss_part[0, 0], ("x", "y", "c"))

    def small_pack(l, grads):
        cw, cb = _conv_grads_natural(grads)
        nat = dict(grads, conv_w=cw, conv_b=cb, final_norm_w=d_final if l == DEPTH - 1 else jnp.zeros((d,), F32))
        return _pack([nat[k] for k, _ in _SMALL], N_CHIPS * SMALL_ROWS).reshape(N_CHIPS, SMALL_ROWS, LANES)

    waiting, inflight = [], []

    def send(tag, partial):
        pending, token = _reduce_cores_start(tag, partial)
        waiting.append(pending)
        return token

    def tick(y):
        token = None
        while waiting:
            pending, token = _reduce_chips_start(waiting.pop(0), core, y)
            inflight.append(pending)
        return token

    dmods = [None] * DEPTH
    dx, dmods[1], grads = _layer_bwd(1, dx, mods[1], params[1], saved[1], cosf, sinf)
    token = send("l1_reduce", dict({k: grads[k] for k in _MATRICES}, small=small_pack(1, grads)))
    dx, dmods[0], grads = _layer_bwd(0, dx, mods[0], params[0], saved[0], cosf, sinf, after=token,
                                     emit=lambda part: send("l0_reduce_" + "_".join(part), part), tick=tick)
    dmod_mine = jnp.concatenate([jnp.concatenate(dmods[l], axis=1) for l in range(DEPTH)], axis=1)
    dmod_all = _gather_all("gather_dmod", dmod_mine.reshape(SUBLANES, -1)).reshape(2 * N_CHIPS, DEPTH * 6 * d)
    send("l0_reduce_in", dict(w_in=grads["w_in"], small=small_pack(0, grads) + dmod_all[0, 0] * 0.0))
    tick(dmod_all)

    totals, flying = {}, None

    def land(after):
        if flying is not None:
            totals.update(zip(flying["names"], _exchange_wait(flying, after)))

    for pending in inflight[:-1]:
        land(dx)
        flying = _reduce_back_start(pending, 1 if pending["tag"].startswith("l1") else 0, core, totals, dx)
    land(dx)
    flying = None

    g_ada_b = _colsum("ada_b_grad", dmod_all).reshape(DEPTH, 6 * d)
    dmod_cols = jnp.stack([lax.dynamic_slice_in_dim(dmod_all, l * 6 * d + chip * ada_cols, ada_cols, axis=1)
                           for l in range(DEPTH)])
    g_ada_w = _ada_bwd("ada_w_grad", c_rows, _pad_to(dmod_cols, 1, ADA_ROWS))
    big = dict(w_in=(tr(w_in), tr(m_w_in), tr(v_w_in)), proj_a=(proj_a, m_proj_a, v_proj_a), proj_b=(proj_b, m_proj_b, v_proj_b),
               w_out=(w_out, m_w_out, v_w_out), w_gate=(tr(ffn_w_gate), tr(m_ffn_w_gate), tr(v_ffn_w_gate)),
               w_up=(tr(ffn_w_up), tr(m_ffn_w_up), tr(v_ffn_w_up)), w_down=(ffn_w_down, m_ffn_w_down, v_ffn_w_down))
    upd, g_big = {}, {}

    def update(k):
        res = _adamw("adamw_" + k, big[k][0], totals[k], *big[k][1:])
        res = [tr(a) for a in res] if k in ("w_in", "w_gate", "w_up") else res
        g_big[k], upd[k] = res[0], res[1:]

    for k in ("w_down", "w_gate", "w_up", "w_out", "proj_a", "proj_b"):
        update(k)
    g_big["ada_w"], *upd["ada_w"] = _adamw("adamw_ada_w", ada_w, g_ada_w, m_ada_w, v_ada_w)
    flying = _reduce_back_start(inflight[-1], 0, core, totals, upd["ada_w"][0])
    land(upd["ada_w"][1])
    update("w_in")

    small_all = _gather_chips("gather_small", [totals["small"]])[0]
    small_g = small_all.transpose(1, 0, 2, 3).reshape(DEPTH, -1)
    per_layer = [_unpack(small_g[l], [shp for _, shp in _SMALL]) for l in range(DEPTH)]
    sg = {k: jnp.stack([per_layer[l][i] for l in range(DEPTH)]) for i, (k, _) in enumerate(_SMALL)}
    g_final = sg["final_norm_w"][DEPTH - 1]
    g_conv_w = lax.dynamic_slice_in_dim(sg["conv_w"], chip * FFN_SHARD, FFN_SHARD, axis=2)

    rest = [("ada_b", ada_b, g_ada_b, m_ada_b, v_ada_b), ("norm1_w", norm1_w, sg["norm1_w"], m_norm1_w, v_norm1_w),
            ("attn_sinks", attn_sinks, sg["sinks"], m_attn_sinks, v_attn_sinks),
            ("sgu_ln_w", sgu_ln_w, sg["sgu_ln_w"], m_sgu_ln_w, v_sgu_ln_w),
            ("sgu_ln_b", sgu_ln_b, sg["sgu_ln_b"], m_sgu_ln_b, v_sgu_ln_b), ("sgu_w", sgu_w, sg["sgu_w"], m_sgu_w, v_sgu_w),
            ("sgu_b", sgu_b, sg["sgu_b"], m_sgu_b, v_sgu_b), ("norm2_w", norm2_w, sg["norm2_w"], m_norm2_w, v_norm2_w),
            ("ffn_conv_w", ffn_conv_w, g_conv_w, m_ffn_conv_w, v_ffn_conv_w),
            ("ffn_conv_b", ffn_conv_b, sg["conv_b"], m_ffn_conv_b, v_ffn_conv_b),
            ("final_norm_w", final_norm_w.reshape(1, d), g_final.reshape(1, d), m_final_norm_w.reshape(1, d),
             v_final_norm_w.reshape(1, d))]
    rest_out = _adamw_small("adamw_rest", *[[r[i] for r in rest] for i in (1, 2, 3, 4)])
    g_rest = {r[0]: r[2] for r in rest}
    u_rest = {r[0]: tuple(o[i] for o in rest_out) for i, r in enumerate(rest)}
    g_rest["final_norm_w"] = g_final
    u_rest["final_norm_w"] = tuple(a.reshape(d) for a in u_rest["final_norm_w"])

    names = ("ada_w", "ada_b", "norm1_w", "w_in", "attn_sinks", "sgu_ln_w", "sgu_ln_b", "sgu_w", "sgu_b", "proj_a", "proj_b",
             "w_out", "norm2_w", "ffn_w_gate", "ffn_w_up", "ffn_conv_w", "ffn_conv_b", "ffn_w_down", "final_norm_w")
    alias = {"ffn_w_gate": "w_gate", "ffn_w_up": "w_up", "ffn_w_down": "w_down"}
    grad_of = lambda n: g_rest[n] if n in g_rest else g_big[alias.get(n, n)]
    upd_of = lambda n: u_rest[n] if n in u_rest else upd[alias.get(n, n)]
    return (loss, dx[None], *[grad_of(n) for n in names], *[upd_of(n)[0] for n in names],
            *[upd_of(n)[1] for n in names], *[upd_of(n)[2] for n in names])
```

```python
import jax
import jax.numpy as jnp
from jax import lax
from jax.experimental import pallas as pl
from jax.experimental.pallas import tpu as pltpu

F32 = jnp.float32
BF = jnp.bfloat16

D_MODEL = 1024
N_Q_HEADS = 16
N_KV_HEADS = 2
HEAD_DIM = 64
ATTN_BLOCK = 128
ROPE_THETA = 500000.0
ROT_DIM = HEAD_DIM // 4
SGU_WIDTH = 1024
SGU_GROUPS = 8
SGU_CHUNK = 128
FFN_DIM = 2816
NORM_EPS = 1e-6
DEPTH = 2
IN_COLS = 5376
N_CHIPS = 4
FFN_SHARD = FFN_DIM // N_CHIPS
FFN_SHARD_PAD = 768
FFN_PAD = N_CHIPS * FFN_SHARD_PAD
LANES = 128
SUBLANES = 8
HALO = 16
VMEM_LIMIT = 56 * 1024 * 1024
NEG_BIG = -1e30

ADAM_LR = 0.001
ADAM_B1 = 0.9
ADAM_B2 = 0.999
ADAM_EPS = 1e-08
ADAM_WD = 0.01
ADAM_STEP = 10

MESH = pl.DeviceIdType.MESH

Q_END = 1024
KV_END = 1280
U_END = 2304
Z_END = 3328
GA_END = 4352


def _sds(shape, dtype):
    return jax.ShapeDtypeStruct(tuple(shape), dtype)


def _call(body, name, out_shape, grid, in_specs, out_specs, semantics, scratch=(), after=None):
    n_in = len(in_specs)
    fn = body
    if after is not None:
        def fn(*refs):
            return body(*refs[:n_in], *refs[n_in + 1:])

        in_specs = list(in_specs) + [pl.BlockSpec(memory_space=pl.ANY)]
    call = pl.pallas_call(
        fn, name=name, out_shape=out_shape, grid=grid, in_specs=in_specs, out_specs=out_specs,
        scratch_shapes=scratch,
        compiler_params=pltpu.CompilerParams(dimension_semantics=semantics, vmem_limit_bytes=VMEM_LIMIT))
    if after is None:
        return call
    return lambda *args: call(*args, after)


def _rows(tm, width, col=0):
    return pl.BlockSpec((tm, width), lambda i: (i, col))


def _vec(width):
    return pl.BlockSpec((1, width), lambda i: (0, 0))


def _resident(shape):
    zeros = (0,) * len(shape)
    return pl.BlockSpec(tuple(shape), lambda *_: zeros, pipeline_mode=pl.Buffered(1))


def _sigmoid(x):
    return 0.5 + 0.5 * jnp.tanh(0.5 * x)


def _gelu(x):
    return 0.5 * x * (1.0 + lax.erf(x * 0.7071067811865476))


def _gelu_and_grad(x):
    cdf = 0.5 * (1.0 + lax.erf(x * 0.7071067811865476))
    return x * cdf, cdf + x * jnp.exp(-0.5 * x * x) * 0.3989422804014327


def _dot(a, b):
    return jnp.dot(a, b, preferred_element_type=F32)


def _dot_nt(a, b):
    return lax.dot_general(a, b, (((1,), (1,)), ((), ())), preferred_element_type=F32)


def _dot_tn(a, b):
    return lax.dot_general(a, b, (((0,), (0,)), ((), ())), preferred_element_type=F32)


def _rms(xv):
    return lax.rsqrt(jnp.mean(xv * xv, axis=-1, keepdims=True) + NORM_EPS)


def _matmul_tn(name, a, b, tk=512, tn=1024, blocked=False, after=None):
    s, k = a.shape
    n = b.shape[1]
    tk, tn = min(tk, k), min(tn, n)

    def body(a_ref, b_ref, o_ref):
        res = _dot_tn(a_ref[...], b_ref[...]).astype(o_ref.dtype)
        if blocked:
            o_ref[0] = res
        else:
            o_ref[...] = res

    if blocked:
        out, ospec = _sds((n // tn, k, tn), BF), pl.BlockSpec((1, tk, tn), lambda i, j: (j, i, 0))
    else:
        out, ospec = _sds((k, n), BF), pl.BlockSpec((tk, tn), lambda i, j: (i, j))
    return _call(body, name, out, (k // tk, n // tn),
                 [pl.BlockSpec((s, tk), lambda i, j: (0, i)), pl.BlockSpec((s, tn), lambda i, j: (0, j))],
                 ospec, ("parallel", "parallel"), after=after)(a, b)


def _matmul_tn_rows(name, a, b, out, row0, rows_total, tk=256):
    s, k = a.shape
    n = b.shape[1]

    def body(a_ref, b_ref, *rest):
        rest[-1][...] = _dot_tn(a_ref[...], b_ref[...]).astype(BF)

    in_specs = [pl.BlockSpec((s, tk), lambda i: (0, i)), _resident(b.shape)]
    args = [a, b]
    if out is not None:
        in_specs.append(pl.BlockSpec(memory_space=pl.ANY))
        args.append(out)
    return pl.pallas_call(
        body, name=name, out_shape=_sds((rows_total, n), BF), grid=(k // tk,), in_specs=in_specs,
        out_specs=pl.BlockSpec((tk, n), lambda i: (row0 // tk + i, 0)),
        input_output_aliases={2: 0} if out is not None else {},
        compiler_params=pltpu.CompilerParams(dimension_semantics=("parallel",), vmem_limit_bytes=VMEM_LIMIT))(*args)


def _rope_partner(v):
    lane = lax.broadcasted_iota(jnp.int32, (1, LANES), 1) % HEAD_DIM
    return jnp.where(lane < ROT_DIM // 2, pltpu.roll(v, LANES - ROT_DIM // 2, axis=1), pltpu.roll(v, ROT_DIM // 2, axis=1))


def _dup_half(v, half):
    lane = lax.broadcasted_iota(jnp.int32, (1, LANES), 1)
    keep = jnp.where((lane >= HEAD_DIM) == (half == 1), v, 0.0)
    return keep + pltpu.roll(keep, HEAD_DIM, axis=1)


def _in_proj(name, x, w, sc, sh, w_in, cosf, sinf, tm=256, after=None):
    s, d = x.shape
    tm = min(tm, s)

    def body(x_ref, w_ref, sc_ref, sh_ref, win_ref, cos_ref, sin_ref,
             h_ref, qr_ref, kk0_ref, kk1_ref, vv0_ref, vv1_ref, u_ref, v_ref, ga_ref, gb_ref):
        xv = x_ref[...]
        h = ((xv * _rms(xv)) * w_ref[...] * (1.0 + sc_ref[...]) + sh_ref[...]).astype(BF)
        h_ref[...] = h
        cosv, sinv = cos_ref[...], sin_ref[...]
        q = _dot_nt(h, win_ref[:Q_END, :])
        for j in range(D_MODEL // LANES):
            qv = q[:, j * LANES:(j + 1) * LANES]
            qr_ref[:, j * LANES:(j + 1) * LANES] = ((qv * cosv + _rope_partner(qv) * sinv) * ATTN_SCALE).astype(BF)
        kv = _dot_nt(h, win_ref[Q_END:KV_END, :])
        kr = kv[:, :LANES] * cosv + _rope_partner(kv[:, :LANES]) * sinv
        vv = kv[:, LANES:]
        kk0_ref[...] = _dup_half(kr, 0).astype(BF)
        kk1_ref[...] = _dup_half(kr, 1).astype(BF)
        vv0_ref[...] = _dup_half(vv, 0).astype(BF)
        vv1_ref[...] = _dup_half(vv, 1).astype(BF)
        u_ref[...] = _dot_nt(h, win_ref[KV_END:U_END, :])
        v_ref[...] = _dot_nt(h, win_ref[U_END:Z_END, :])
        ga_ref[...] = _dot_nt(h, win_ref[Z_END:GA_END, :]).astype(BF)
        gb_ref[...] = _dot_nt(h, win_ref[GA_END:, :]).astype(BF)

    wide, kvs, pre = _sds((s, d), BF), _sds((s, LANES), BF), _sds((s, d), F32)
    return _call(body, name, (wide, wide, kvs, kvs, kvs, kvs, pre, pre, wide, wide), (s // tm,),
                 [_rows(tm, d), _vec(d), _vec(d), _vec(d), _resident(w_in.shape), _rows(tm, LANES), _rows(tm, LANES)],
                 (_rows(tm, d), _rows(tm, d)) + (_rows(tm, LANES),) * 4 + (_rows(tm, d),) * 4, ("parallel",), after=after)(
                     x, w, sc, sh, w_in, cosf, sinf)


def _in_proj_bwd(name, dq, dkv, du, dv, dga, dgb, w_in, x, w, sc, dx_in, tm=256):
    s, d = x.shape
    tm = min(tm, s)

    def body(dq_ref, dkv_ref, du_ref, dv_ref, dga_ref, dgb_ref, win_ref, x_ref, w_ref, sc_ref, dxin_ref,
             dx_ref, da_ref, dsh_ref):
        @pl.when(pl.program_id(0) == 0)
        def _():
            da_ref[...] = jnp.zeros_like(da_ref)
            dsh_ref[...] = jnp.zeros_like(dsh_ref)

        dh = (_dot(dq_ref[...], win_ref[:Q_END, :]) + _dot(dkv_ref[...], win_ref[Q_END:KV_END, :])
              + _dot(du_ref[...], win_ref[KV_END:U_END, :]) + _dot(dv_ref[...], win_ref[U_END:Z_END, :])
              + _dot(dga_ref[...], win_ref[Z_END:GA_END, :]) + _dot(dgb_ref[...], win_ref[GA_END:, :]))
        xv = x_ref[...]
        r = _rms(xv)
        xn = xv * r
        dxn = dh * (w_ref[...] * (1.0 + sc_ref[...]))
        dx_ref[...] = dxin_ref[...] + r * (dxn - xn * jnp.mean(dxn * xn, axis=-1, keepdims=True))
        da_ref[...] += jnp.sum(dh * xn, axis=0, keepdims=True)
        dsh_ref[...] += jnp.sum(dh, axis=0, keepdims=True)

    return _call(body, name, (_sds((s, d), F32), _sds((1, d), F32), _sds((1, d), F32)), (s // tm,),
                 [_rows(tm, d), _rows(tm, 2 * LANES), _rows(tm, d), _rows(tm, d), _rows(tm, d), _rows(tm, d),
                  _resident(w_in.shape), _rows(tm, d), _vec(d), _vec(d), _rows(tm, d)],
                 (_rows(tm, d), _vec(d), _vec(d)), ("arbitrary",))(dq, dkv, du, dv, dga, dgb, w_in, x, w, sc, dx_in)


def _rope_bwd(name, dqr, dkv_cur, dkv_prev, cosf, sinf, tm=512):
    s = dqr.shape[0]
    tm = min(tm, s)
    steps = s // tm
    per = tm // ATTN_BLOCK
    nb = s // ATTN_BLOCK

    def unrope(v, cosv, sinv):
        return v * cosv - _rope_partner(v) * sinv

    def body(dq_ref, cur_ref, prev_ref, next_ref, cos_ref, sin_ref, dqo_ref, dkvo_ref):
        i = pl.program_id(0)
        cosv, sinv = cos_ref[...], sin_ref[...]
        for j in range(D_MODEL // LANES):
            dqo_ref[:, j * LANES:(j + 1) * LANES] = unrope(dq_ref[:, j * LANES:(j + 1) * LANES], cosv, sinv).astype(BF)
        nxt = jnp.where(i < steps - 1, next_ref[...], 0.0)
        if per > 1:
            shifted = jnp.concatenate([prev_ref[ATTN_BLOCK:, :], nxt], axis=0)
        else:
            shifted = nxt
        tot = cur_ref[...] + shifted
        dkvo_ref[:, :LANES] = unrope(tot[:, :LANES], cosv, sinv).astype(BF)
        dkvo_ref[:, LANES:] = tot[:, LANES:].astype(BF)

    nxt_spec = pl.BlockSpec((ATTN_BLOCK, 2 * LANES), lambda i: (jnp.minimum((i + 1) * per, nb - 1), 0))
    return _call(body, name, (_sds((s, D_MODEL), BF), _sds((s, 2 * LANES), BF)), (steps,),
                 [_rows(tm, D_MODEL), _rows(tm, 2 * LANES), _rows(tm, 2 * LANES), nxt_spec, _rows(tm, LANES),
                  _rows(tm, LANES)],
                 (_rows(tm, D_MODEL), _rows(tm, 2 * LANES)), ("parallel",))(dqr, dkv_cur, dkv_prev, dkv_prev, cosf, sinf)


Q_PER_KV = N_Q_HEADS // N_KV_HEADS
ATTN_SCALE = HEAD_DIM ** -0.5
HEADS_AHEAD_FWD = 2
HEADS_AHEAD_BWD = 3


def _band_mask_t(n):
    kj = lax.broadcasted_iota(jnp.int32, (2 * ATTN_BLOCK, ATTN_BLOCK), 0)
    qi = lax.broadcasted_iota(jnp.int32, (2 * ATTN_BLOCK, ATTN_BLOCK), 1)
    return (kj > qi) & (kj <= qi + ATTN_BLOCK) & ((n > 0) | (kj >= ATTN_BLOCK))


def _softmax_t(raw, allowed, sink):
    sc = jnp.where(allowed, raw, NEG_BIG)
    m = jnp.maximum(jnp.max(sc, axis=0, keepdims=True), sink)
    p = jnp.exp(sc - m)
    esink = jnp.exp(sink - m)
    inv = 1.0 / (jnp.sum(p, axis=0, keepdims=True) + esink)
    return p * inv, esink * inv


def _kv_specs():
    cur = pl.BlockSpec((ATTN_BLOCK, LANES), lambda n: (n, 0))
    prev = pl.BlockSpec((ATTN_BLOCK, LANES), lambda n: (jnp.maximum(n - 1, 0), 0))
    return [prev, cur] * 4


def _attention(name, qr, kk0, kk1, vv0, vv1, sinks):
    s = qr.shape[0]
    nb = s // ATTN_BLOCK

    def body(sink_ref, q_ref, k0p, k0c, k1p, k1c, v0p, v0c, v1p, v1c, y_ref):
        allowed = _band_mask_t(pl.program_id(0))
        upper = lax.broadcasted_iota(jnp.int32, (1, LANES), 1) >= HEAD_DIM
        upper_rows = lax.broadcasted_iota(jnp.int32, (LANES, 1), 0) >= HEAD_DIM
        bands = ((jnp.concatenate([k0p[...], k0c[...]], axis=0), jnp.concatenate([v0p[...], v0c[...]], axis=0)),
                 (jnp.concatenate([k1p[...], k1c[...]], axis=0), jnp.concatenate([v1p[...], v1c[...]], axis=0)))
        vbts = (bands[0][1].T, bands[1][1].T)

        def scores(h):
            hk, j, half = h // Q_PER_KV, (h % Q_PER_KV) // 2, h % 2
            col = (hk * 4 + j) * LANES
            qp = q_ref[:, col:col + LANES]
            return _dot_nt(bands[hk][0], jnp.where(upper if half else jnp.logical_not(upper), qp, jnp.zeros_like(qp)))

        out_t = None
        ahead = [scores(h) for h in range(HEADS_AHEAD_FWD)]
        for h in range(N_Q_HEADS):
            hk, j, half = h // Q_PER_KV, (h % Q_PER_KV) // 2, h % 2
            raw = ahead.pop(0)
            if h + HEADS_AHEAD_FWD < N_Q_HEADS:
                ahead.append(scores(h + HEADS_AHEAD_FWD))
            pn, _ = _softmax_t(raw, allowed, sink_ref[h])
            o_h = _dot(vbts[hk], pn.astype(BF))
            out_t = jnp.where(upper_rows, o_h, out_t) if half else o_h
            if half:
                col = (hk * 4 + j) * LANES
                y_ref[:, col:col + LANES] = out_t.T.astype(BF)

    return _call(body, name, _sds((s, D_MODEL), BF), (nb,),
                 [pl.BlockSpec(memory_space=pltpu.SMEM), pl.BlockSpec((ATTN_BLOCK, D_MODEL), lambda n: (n, 0))] + _kv_specs(),
                 pl.BlockSpec((ATTN_BLOCK, D_MODEL), lambda n: (n, 0)), ("parallel",))(
                     sinks, qr, kk0, kk0, kk1, kk1, vv0, vv0, vv1, vv1)


def _attention_bwd(name, qr, kk0, kk1, vv0, vv1, sinks, dy, after=None):
    s = qr.shape[0]
    nb = s // ATTN_BLOCK

    def body(sink_ref, q_ref, dy_ref, k0p, k0c, k1p, k1c, v0p, v0c, v1p, v1c, dq_ref, cur_ref, prev_ref, dsink_ref):
        @pl.when(pl.program_id(0) == 0)
        def _():
            dsink_ref[...] = jnp.zeros_like(dsink_ref)

        allowed = _band_mask_t(pl.program_id(0))
        lane = lax.broadcasted_iota(jnp.int32, (1, LANES), 1)
        upper = lane >= HEAD_DIM
        upper_rows = lax.broadcasted_iota(jnp.int32, (LANES, 1), 0) >= HEAD_DIM
        bands = ((jnp.concatenate([k0p[...], k0c[...]], axis=0), jnp.concatenate([v0p[...], v0c[...]], axis=0)),
                 (jnp.concatenate([k1p[...], k1c[...]], axis=0), jnp.concatenate([v1p[...], v1c[...]], axis=0)))
        kbts = (bands[0][0].T, bands[1][0].T)

        def scores(h):
            hk, j, half = h // Q_PER_KV, (h % Q_PER_KV) // 2, h % 2
            kb, vb = bands[hk]
            col = (hk * 4 + j) * LANES
            sel = upper if half else jnp.logical_not(upper)
            qp = q_ref[:, col:col + LANES]
            qa = jnp.where(sel, qp, jnp.zeros_like(qp))
            dya = jnp.where(sel, dy_ref[:, col:col + LANES], 0.0).astype(BF)
            return qa, dya, _dot_nt(kb, qa), _dot_nt(vb, dya)

        dsink = jnp.zeros((1, LANES), F32)
        dk_slab = jnp.zeros((2 * ATTN_BLOCK, LANES), F32)
        dv_slab = jnp.zeros((2 * ATTN_BLOCK, LANES), F32)
        dkk = dvv = dq_t = None
        ahead = [scores(h) for h in range(HEADS_AHEAD_BWD)]
        for h in range(N_Q_HEADS):
            hk, j, half = h // Q_PER_KV, (h % Q_PER_KV) // 2, h % 2
            qa, dya, raw, dp = ahead.pop(0)
            if h + HEADS_AHEAD_BWD < N_Q_HEADS:
                ahead.append(scores(h + HEADS_AHEAD_BWD))
            pn, psink = _softmax_t(raw, allowed, sink_ref[h])
            delta = jnp.sum(pn * dp, axis=0, keepdims=True)
            ds = (pn * (dp - delta)).astype(BF)
            dsink = dsink + jnp.where(lane == h, -jnp.sum(psink * delta), 0.0)
            dq_h = _dot(kbts[hk], ds) * ATTN_SCALE
            dq_t = jnp.where(upper_rows, dq_h, dq_t) if half else dq_h
            dk_h, dv_h = _dot(ds, qa), _dot(pn.astype(BF), dya)
            dkk, dvv = (dk_h, dv_h) if h % Q_PER_KV == 0 else (dkk + dk_h, dvv + dv_h)
            if half:
                col = (hk * 4 + j) * LANES
                dq_ref[:, col:col + LANES] = dq_t.T
            if h % Q_PER_KV == Q_PER_KV - 1:
                mine = upper if hk else jnp.logical_not(upper)
                dk_slab = jnp.where(mine, dkk + pltpu.roll(dkk, HEAD_DIM, axis=1), dk_slab)
                dv_slab = jnp.where(mine, dvv + pltpu.roll(dvv, HEAD_DIM, axis=1), dv_slab)
        prev_ref[:, :LANES] = dk_slab[:ATTN_BLOCK]
        prev_ref[:, LANES:] = dv_slab[:ATTN_BLOCK]
        cur_ref[:, :LANES] = dk_slab[ATTN_BLOCK:]
        cur_ref[:, LANES:] = dv_slab[ATTN_BLOCK:]
        dsink_ref[...] += dsink

    blk = pl.BlockSpec((ATTN_BLOCK, D_MODEL), lambda n: (n, 0))
    kvo = pl.BlockSpec((ATTN_BLOCK, 2 * LANES), lambda n: (n, 0))
    return _call(body, name,
                 (_sds((s, D_MODEL), F32), _sds((s, 2 * LANES), F32), _sds((s, 2 * LANES), F32), _sds((1, LANES), F32)),
                 (nb,), [pl.BlockSpec(memory_space=pltpu.SMEM), blk, blk] + _kv_specs(),
                 (blk, kvo, kvo, pl.BlockSpec((1, LANES), lambda n: (0, 0))), ("arbitrary",), after=after)(
                     sinks, qr, dy, kk0, kk0, kk1, kk1, vv0, vv0, vv1, vv1)


def _sgu_weights(wm_ref, g):
    t = lax.broadcasted_iota(jnp.int32, (SGU_CHUNK, SGU_CHUNK), 0)
    sidx = lax.broadcasted_iota(jnp.int32, (SGU_CHUNK, SGU_CHUNK), 1)
    return jnp.where(sidx <= t, wm_ref[g], 0.0).astype(BF)


def _layer_norm_stats(v):
    mu = jnp.mean(v, axis=-1, keepdims=True)
    cen = v - mu
    rstd = lax.rsqrt(jnp.mean(cen * cen, axis=-1, keepdims=True) + NORM_EPS)
    return cen * rstd, rstd


def _sgu(name, u_pre, v_pre, ln_w, ln_b, wm, bfull, tm=256):
    s, w = u_pre.shape
    tm = min(tm, s)

    def body(u_ref, v_ref, lw_ref, lb_ref, wm_ref, b_ref, y_ref):
        vhat, _ = _layer_norm_stats(_gelu(v_ref[...]))
        vn = (vhat * lw_ref[...] + lb_ref[...]).astype(BF)
        for g in range(SGU_GROUPS):
            wg = _sgu_weights(wm_ref, g)
            cols = slice(g * SGU_CHUNK, (g + 1) * SGU_CHUNK)
            for ch in range(tm // SGU_CHUNK):
                rows = slice(ch * SGU_CHUNK, (ch + 1) * SGU_CHUNK)
                f = _dot(wg, vn[rows, cols]) + b_ref[g]
                y_ref[rows, cols] = (_gelu(u_ref[rows, cols]) * f).astype(BF)

    full3 = pl.BlockSpec((SGU_GROUPS, SGU_CHUNK, SGU_CHUNK), lambda i: (0, 0, 0))
    return _call(body, name, _sds((s, w), BF), (s // tm,),
                 [_rows(tm, w), _rows(tm, w), _vec(w), _vec(w), full3, full3],
                 _rows(tm, w), ("parallel",))(u_pre, v_pre, ln_w, ln_b, wm, bfull)


def _sgu_bwd(name, u_pre, v_pre, ln_w, ln_b, wm, bfull, dy, tm=256, after=None):
    s, w = u_pre.shape
    tm = min(tm, s)
    steps = s // tm

    def body(u_ref, v_ref, lw_ref, lb_ref, wm_ref, b_ref, dy_ref, du_ref, dv_ref, dwm_ref, db_ref, dlw_ref, dlb_ref,
             dfsum_ref):
        i = pl.program_id(0)

        @pl.when(i == 0)
        def _():
            dwm_ref[...] = jnp.zeros_like(dwm_ref)
            dlw_ref[...] = jnp.zeros_like(dlw_ref)
            dlb_ref[...] = jnp.zeros_like(dlb_ref)
            dfsum_ref[...] = jnp.zeros_like(dfsum_ref)

        vpre = v_ref[...]
        vg, dvg_dv = _gelu_and_grad(vpre)
        vhat, rstd = _layer_norm_stats(vg)
        vn = (vhat * lw_ref[...] + lb_ref[...]).astype(BF)
        t = lax.broadcasted_iota(jnp.int32, (SGU_CHUNK, SGU_CHUNK), 0)
        sidx = lax.broadcasted_iota(jnp.int32, (SGU_CHUNK, SGU_CHUNK), 1)
        dvn_cols = []
        for g in range(SGU_GROUPS):
            wg = _sgu_weights(wm_ref, g)
            cols = slice(g * SGU_CHUNK, (g + 1) * SGU_CHUNK)
            dvn_rows = []
            dwg = jnp.zeros((SGU_CHUNK, SGU_CHUNK), F32)
            dfs = jnp.zeros((SGU_CHUNK, SGU_CHUNK), F32)
            for ch in range(tm // SGU_CHUNK):
                rows = slice(ch * SGU_CHUNK, (ch + 1) * SGU_CHUNK)
                upre = u_ref[rows, cols]
                dyv = dy_ref[rows, cols].astype(F32)
                f = _dot(wg, vn[rows, cols]) + b_ref[g]
                ug, dug_du = _gelu_and_grad(upre)
                du_ref[rows, cols] = (dyv * f * dug_du).astype(BF)
                df = dyv * ug
                dfb = df.astype(BF)
                dvn_rows.append(_dot_tn(wg, dfb))
                dwg = dwg + _dot_nt(dfb, vn[rows, cols])
                dfs = dfs + df
            dwm_ref[g] += jnp.where(sidx <= t, dwg, 0.0)
            dfsum_ref[g] += dfs
            dvn_cols.append(jnp.concatenate(dvn_rows, axis=0) if len(dvn_rows) > 1 else dvn_rows[0])
        dvn = jnp.concatenate(dvn_cols, axis=1)
        dlw_ref[...] += jnp.sum(dvn * vhat, axis=0, keepdims=True)
        dlb_ref[...] += jnp.sum(dvn, axis=0, keepdims=True)
        dvh = dvn * lw_ref[...]
        dvg = rstd * (dvh - jnp.mean(dvh, axis=-1, keepdims=True) - vhat * jnp.mean(dvh * vhat, axis=-1, keepdims=True))
        dv_ref[...] = (dvg * dvg_dv).astype(BF)

        @pl.when(i == steps - 1)
        def _():
            for g in range(SGU_GROUPS):
                db_ref[g:g + 1, :] = jnp.sum(dfsum_ref[g].T, axis=0, keepdims=True)

    full3 = pl.BlockSpec((SGU_GROUPS, SGU_CHUNK, SGU_CHUNK), lambda i: (0, 0, 0))
    return _call(body, name,
                 (_sds((s, w), BF), _sds((s, w), BF), _sds((SGU_GROUPS, SGU_CHUNK, SGU_CHUNK), F32),
                  _sds((SGU_GROUPS, SGU_CHUNK), F32), _sds((1, w), F32), _sds((1, w), F32)),
                 (steps,),
                 [_rows(tm, w), _rows(tm, w), _vec(w), _vec(w), full3, full3, _rows(tm, w)],
                 (_rows(tm, w), _rows(tm, w), full3, pl.BlockSpec((SGU_GROUPS, SGU_CHUNK), lambda i: (0, 0)), _vec(w), _vec(w)),
                 ("arbitrary",), scratch=[pltpu.VMEM((SGU_GROUPS, SGU_CHUNK, SGU_CHUNK), F32)], after=after)(
                     u_pre, v_pre, ln_w, ln_b, wm, bfull, dy)


def _mix_out(name, y_sgu, y_attn, ga_pre, gb_pre, x, g1, proj_a, proj_b, w_out, w2, sc2, sh2, tm=256):
    s, d = x.shape
    tm = min(tm, s)

    def body(ys_ref, ya_ref, ga_ref, gb_ref, x_ref, g1_ref, wa_ref, wb_ref, wo_ref, w2_ref, sc2_ref, sh2_ref,
             m_ref, pa_ref, pb_ref, o_ref, x1_ref, h2_ref):
        pa = _dot(ys_ref[...], wa_ref[...].reshape(d, d))
        pb = _dot(ya_ref[...], wb_ref[...].reshape(d, d))
        pa_ref[...] = pa.astype(BF)
        pb_ref[...] = pb.astype(BF)
        merged = (_sigmoid(ga_ref[...].astype(F32)) * pa + _sigmoid(gb_ref[...].astype(F32)) * pb).astype(BF)
        m_ref[...] = merged
        o = _dot(merged, wo_ref[...].reshape(d, d))
        o_ref[...] = o.astype(BF)
        x1 = x_ref[...] + g1_ref[...] * o
        x1_ref[...] = x1
        h2_ref[...] = ((x1 * _rms(x1)) * w2_ref[...] * (1.0 + sc2_ref[...]) + sh2_ref[...]).astype(BF)

    f, b = _sds((s, d), F32), _sds((s, d), BF)
    r = _rows(tm, d)
    wspec = _resident(proj_a.shape)
    return _call(body, name, (b, b, b, b, f, b), (s // tm,),
                 [r, r, r, r, r, _vec(d), wspec, wspec, wspec, _vec(d), _vec(d), _vec(d)], (r,) * 6, ("parallel",))(
                     y_sgu, y_attn, ga_pre, gb_pre, x, g1, proj_a, proj_b, w_out, w2, sc2, sh2)


def _mix_bwd(name, do, w_out, proj_a, proj_b, ga_pre, gb_pre, pa, pb, tm=256):
    s, d = do.shape
    tm = min(tm, s)

    def body(do_ref, wo_ref, wa_ref, wb_ref, ga_ref, gb_ref, pa_ref, pb_ref,
             dpa_ref, dpb_ref, dga_ref, dgb_ref, dys_ref, dya_ref):
        dm = _dot_nt(do_ref[...], wo_ref[...].reshape(d, d))
        ga = _sigmoid(ga_ref[...].astype(F32))
        gb = _sigmoid(gb_ref[...].astype(F32))
        dpa = (dm * ga).astype(BF)
        dpb = (dm * gb).astype(BF)
        dpa_ref[...] = dpa
        dpb_ref[...] = dpb
        dga_ref[...] = (dm * pa_ref[...].astype(F32) * ga * (1.0 - ga)).astype(BF)
        dgb_ref[...] = (dm * pb_ref[...].astype(F32) * gb * (1.0 - gb)).astype(BF)
        dys_ref[...] = _dot_nt(dpa, wa_ref[...].reshape(d, d)).astype(BF)
        dya_ref[...] = _dot_nt(dpb, wb_ref[...].reshape(d, d)).astype(BF)

    f, b = _sds((s, d), F32), _sds((s, d), BF)
    r = _rows(tm, d)
    wspec = _resident(w_out.shape)
    return _call(body, name, (b, b, b, b, b, b), (s // tm,), [r, wspec, wspec, wspec, r, r, r, r], (r,) * 6,
                 ("parallel",))(do, w_out, proj_a, proj_b, ga_pre, gb_pre, pa, pb)


def _ffn_up_act(name, h2, w_gate, w_up, cw, cb, tm=1024):
    s, d = h2.shape
    tm = min(tm, s)
    tc = FFN_SHARD_PAD
    per = tm // HALO

    def body(h_ref, hprev_ref, wg_ref, wu_ref, cw_ref, cb_ref, a_ref, ac_ref, up_ref, hf_ref):
        hv = h_ref[...]
        a = _dot_nt(hv, wg_ref[0]).astype(BF)
        up = _dot_nt(hv, wu_ref[0]).astype(BF)
        a_ref[...] = a
        up_ref[...] = up
        prev = jnp.where(pl.program_id(1) > 0, _dot_nt(hprev_ref[...], wg_ref[0]).astype(BF).astype(F32), 0.0)
        ext = jnp.concatenate([prev, a.astype(F32)], axis=0)
        ac = (cb_ref[...] + cw_ref[0:1, :] * pltpu.roll(ext, 2, axis=0) + cw_ref[1:2, :] * pltpu.roll(ext, 1, axis=0)
              + cw_ref[2:3, :] * ext)[HALO:]
        ac_ref[...] = ac.astype(BF)
        hf_ref[...] = (ac * _sigmoid(ac) * up.astype(F32)).astype(BF)

    wspec = pl.BlockSpec((1, tc, d), lambda j, i: (j, 0, 0))
    ospec = pl.BlockSpec((tm, tc), lambda j, i: (i, j))
    o = _sds((s, FFN_PAD), BF)
    return _call(body, name, (o, o, o, o), (N_CHIPS, s // tm),
                 [pl.BlockSpec((tm, d), lambda j, i: (i, 0)), pl.BlockSpec((HALO, d), lambda j, i: (jnp.maximum(i * per - 1, 0), 0)),
                  wspec, wspec, pl.BlockSpec((3, tc), lambda j, i: (0, j)), pl.BlockSpec((1, tc), lambda j, i: (0, j))],
                 (ospec, ospec, ospec, ospec), ("parallel", "parallel"))(h2, h2, w_gate, w_up, cw, cb)


def _ffn_down(name, hf, w_down, x1, g2, tm=512):
    s, d = x1.shape
    tm = min(tm, s)

    def body(hf_ref, wd_ref, x1_ref, g2_ref, dn_ref, x2_ref):
        dn = _dot(hf_ref[...], wd_ref[...].reshape(FFN_PAD, d))
        dn_ref[...] = dn.astype(BF)
        x2_ref[...] = x1_ref[...] + g2_ref[...] * dn

    return _call(body, name, (_sds((s, d), BF), _sds((s, d), F32)), (s // tm,),
                 [_rows(tm, FFN_PAD), _resident(w_down.shape), _rows(tm, d), _vec(d)],
                 (_rows(tm, d), _rows(tm, d)), ("parallel",))(hf, w_down, x1, g2)


def _ffn_down_bwd_act(name, dx2, dn, g2, w_down, a, ac, up, cw, tm=256, after=None):
    s, d = dx2.shape
    c = a.shape[1]
    tm = min(tm, s)
    tc = FFN_SHARD_PAD
    per = tm // HALO
    steps = s // tm
    last = s // HALO - 1
    n = tm + HALO

    def body(dx_ref, dxnext_ref, dn_ref, g2_ref, wd_ref, a_ref, ac_ref, acnext_ref, up_ref, upnext_ref, cw_ref,
             ddn_ref, da_ref, dup_ref, dg_ref, dcw_ref, dcb_ref):
        i = pl.program_id(0)

        @pl.when(i == 0)
        def _():
            dg_ref[...] = jnp.zeros_like(dg_ref)
            dcw_ref[...] = jnp.zeros_like(dcw_ref)
            dcb_ref[...] = jnp.zeros_like(dcb_ref)

        dxv = dx_ref[...]
        ddn = (dxv * g2_ref[...]).astype(BF)
        ddn_ref[...] = ddn
        dg_ref[...] += jnp.sum(dxv * dn_ref[...].astype(F32), axis=0, keepdims=True)
        ddn_next = jnp.where(i < steps - 1, dxnext_ref[...] * g2_ref[...], 0.0).astype(BF)
        ddn_ext = jnp.concatenate([ddn, ddn_next], axis=0)
        for k in range(N_CHIPS):
            cols = slice(k * tc, (k + 1) * tc)
            dh = _dot_nt(ddn_ext, wd_ref[k])
            ace = jnp.concatenate([ac_ref[:, cols].astype(F32), acnext_ref[:, cols].astype(F32)], axis=0)
            upe = jnp.concatenate([up_ref[:, cols].astype(F32), upnext_ref[:, cols].astype(F32)], axis=0)
            sig = _sigmoid(ace)
            silu = ace * sig
            dac = dh * upe * (sig + silu * (1.0 - sig))
            dup_ref[:, cols] = (dh[:tm] * silu[:tm]).astype(BF)
            d1 = pltpu.roll(dac, n - 1, axis=0)[:tm]
            d2 = pltpu.roll(dac, n - 2, axis=0)[:tm]
            d0 = dac[:tm]
            da_ref[:, cols] = (cw_ref[2:3, cols] * d0 + cw_ref[1:2, cols] * d1 + cw_ref[0:1, cols] * d2).astype(BF)
            a0 = a_ref[:, cols].astype(F32)
            dcb_ref[:, cols] += jnp.sum(d0, axis=0, keepdims=True)
            dcw_ref[0:1, cols] += jnp.sum(d2 * a0, axis=0, keepdims=True)
            dcw_ref[1:2, cols] += jnp.sum(d1 * a0, axis=0, keepdims=True)
            dcw_ref[2:3, cols] += jnp.sum(d0 * a0, axis=0, keepdims=True)

    nxt = lambda width: pl.BlockSpec((HALO, width), lambda i: (jnp.minimum((i + 1) * per, last), 0))
    wide = _sds((s, c), BF)
    return _call(body, name, (_sds((s, d), BF), wide, wide, _sds((1, d), F32), _sds((3, c), F32), _sds((1, c), F32)), (steps,),
                 [_rows(tm, d), nxt(d), _rows(tm, d), _vec(d), _resident(w_down.shape), _rows(tm, c), _rows(tm, c), nxt(c),
                  _rows(tm, c), nxt(c), pl.BlockSpec((3, c), lambda i: (0, 0))],
                 (_rows(tm, d), _rows(tm, c), _rows(tm, c), _vec(d), pl.BlockSpec((3, c), lambda i: (0, 0)), _vec(c)),
                 ("arbitrary",), after=after)(dx2, dx2, dn, g2, w_down, a, ac, ac, up, up, cw)


def _ffn_up_bwd(name, da, dup, w_gate, w_up, x1, dx2, w2, sc2, o, g1, tm=256, after=None):
    s, d = x1.shape
    tm = min(tm, s)
    tc = FFN_SHARD_PAD

    def body(da_ref, dup_ref, wg_ref, wu_ref, x1_ref, dx2_ref, w2_ref, sc2_ref, o_ref, g1_ref,
             dx1_ref, do_ref, dnw_ref, dsh_ref, dg1_ref):
        @pl.when(pl.program_id(0) == 0)
        def _():
            dnw_ref[...] = jnp.zeros_like(dnw_ref)
            dsh_ref[...] = jnp.zeros_like(dsh_ref)
            dg1_ref[...] = jnp.zeros_like(dg1_ref)

        dh = jnp.zeros((tm, d), F32)
        for k in range(N_CHIPS):
            cols = slice(k * tc, (k + 1) * tc)
            dh = dh + _dot(da_ref[:, cols], wg_ref[k]) + _dot(dup_ref[:, cols], wu_ref[k])
        xv = x1_ref[...]
        r = _rms(xv)
        xn = xv * r
        dxn = dh * (w2_ref[...] * (1.0 + sc2_ref[...]))
        dx1 = dx2_ref[...] + r * (dxn - xn * jnp.mean(dxn * xn, axis=-1, keepdims=True))
        dx1_ref[...] = dx1
        dnw_ref[...] += jnp.sum(dh * xn, axis=0, keepdims=True)
        dsh_ref[...] += jnp.sum(dh, axis=0, keepdims=True)
        do_ref[...] = (dx1 * g1_ref[...]).astype(BF)
        dg1_ref[...] += jnp.sum(dx1 * o_ref[...].astype(F32), axis=0, keepdims=True)

    v = _sds((1, d), F32)
    r = _rows(tm, d)
    wspec = _resident(w_gate.shape)
    return _call(body, name, (_sds((s, d), F32), _sds((s, d), BF), v, v, v), (s // tm,),
                 [_rows(tm, FFN_PAD), _rows(tm, FFN_PAD), wspec, wspec, r, r, _vec(d), _vec(d), r, _vec(d)],
                 (r, r, _vec(d), _vec(d), _vec(d)), ("arbitrary",), after=after)(da, dup, w_gate, w_up, x1, dx2, w2, sc2, o, g1)


def _loss_head(name, x, w, target, tm=512):
    s, d = x.shape
    tm = min(tm, s)

    def body(x_ref, w_ref, t_ref, dx_ref, loss_ref, dw_ref):
        @pl.when(pl.program_id(0) == 0)
        def _():
            loss_ref[...] = jnp.zeros_like(loss_ref)
            dw_ref[...] = jnp.zeros_like(dw_ref)

        xv = x_ref[...]
        r = _rms(xv)
        xn = xv * r
        err = xn * w_ref[...] - t_ref[...]
        loss_ref[...] += 0.5 * jnp.sum(jnp.mean(err * err, axis=-1, keepdims=True))
        dy = err * (1.0 / d)
        dw_ref[...] += jnp.sum(dy * xn, axis=0, keepdims=True)
        dxn = dy * w_ref[...]
        dx_ref[...] = r * (dxn - xn * jnp.mean(dxn * xn, axis=-1, keepdims=True))

    return _call(body, name, (_sds((s, d), F32), _sds((1, LANES), F32), _sds((1, d), F32)), (s // tm,),
                 [_rows(tm, d), _vec(d), _rows(tm, d)], (_rows(tm, d), _vec(LANES), _vec(d)), ("arbitrary",))(x, w, target)


def _layer_fwd(l, x, mod, p, cosf, sinf, after=None, late=None, later=None):
    sh1, sc1, g1, sh2, sc2, g2 = mod
    tag = f"l{l}_"
    h, qr, kk0, kk1, vv0, vv1, u_pre, v_pre, ga_pre, gb_pre = _in_proj(
        tag + "in_proj", x, p["norm1_w"], sc1, sh1, p["w_in"], cosf, sinf, after=after)
    y_attn = _attention(tag + "attn", qr, kk0, kk1, vv0, vv1, p["sinks"])
    y_sgu = _sgu(tag + "sgu", u_pre, v_pre, p["sgu_ln_w"], p["sgu_ln_b"], p["sgu_w"], p["sgu_bfull"])
    if late is not None:
        p = dict(p, **late(y_sgu))
    merged, pa, pb, o, x1, h2 = _mix_out(tag + "mix_out", y_sgu, y_attn, ga_pre, gb_pre, x, g1, p["proj_a"], p["proj_b"],
                                         p["w_out"], p["norm2_w"], sc2, sh2)
    if later is not None:
        p = dict(p, **later(h2))
    a, ac, up, hf = _ffn_up_act(tag + "ffn_up", h2, p["w_gate"], p["w_up"], p["conv_w"], p["conv_b"])
    dn, x2 = _ffn_down(tag + "ffn_down", hf, p["w_down"], x1, g2)
    saved = dict(x=x, h=h, qr=qr, kk0=kk0, kk1=kk1, vv0=vv0, vv1=vv1, u_pre=u_pre, v_pre=v_pre, ga_pre=ga_pre,
                 gb_pre=gb_pre, y_attn=y_attn, y_sgu=y_sgu, merged=merged, pa=pa, pb=pb, o=o, x1=x1, h2=h2, a=a, ac=ac, up=up,
                 hf=hf, dn=dn)
    return x2, saved, p


def _layer_bwd(l, dx2, mod, p, sv, cosf, sinf, after=None, emit=None, tick=None):
    sh1, sc1, g1, sh2, sc2, g2 = mod
    tag = f"l{l}_b_"
    d = D_MODEL
    g = {}
    ready = (lambda names: emit({k: g.pop(k) for k in names})) if emit else (lambda names: None)
    tick = tick or (lambda y: None)
    ddn, da, dup, dg2, g["conv_w"], g["conv_b"] = _ffn_down_bwd_act(
        tag + "ffn_down", dx2, sv["dn"], g2, p["w_down"], sv["a"], sv["ac"], sv["up"], p["conv_w"], after=after)
    g["w_down"] = _matmul_tn(tag + "dw_down", sv["hf"], ddn, tk=FFN_SHARD_PAD, after=tick(ddn)).reshape(N_CHIPS, FFN_SHARD_PAD, d)
    g["w_gate"] = _matmul_tn(tag + "dw_gate", da, sv["h2"], tk=FFN_SHARD_PAD).reshape(N_CHIPS, FFN_SHARD_PAD, d)
    g["w_up"] = _matmul_tn(tag + "dw_up", dup, sv["h2"], tk=FFN_SHARD_PAD).reshape(N_CHIPS, FFN_SHARD_PAD, d)
    dx1, do, da2, dsh2, dg1 = _ffn_up_bwd(tag + "ffn_up", da, dup, p["w_gate"], p["w_up"], sv["x1"], dx2, p["norm2_w"],
                                          sc2, sv["o"], g1, after=ready(("w_down", "w_gate", "w_up")))
    g["norm2_w"] = da2 * (1.0 + sc2)
    dsc2 = da2 * p["norm2_w"]
    g["w_out"] = _matmul_tn(tag + "dw_out", sv["merged"], do, after=tick(do)).reshape(N_CHIPS, d // N_CHIPS, d)
    dpa, dpb, dga, dgb, dy_sgu, dy_attn = _mix_bwd(tag + "mix", do, p["w_out"], p["proj_a"], p["proj_b"], sv["ga_pre"],
                                                  sv["gb_pre"], sv["pa"], sv["pb"])
    g["proj_a"] = _matmul_tn(tag + "dproj_a", sv["y_sgu"], dpa).reshape(N_CHIPS, d // N_CHIPS, d)
    g["proj_b"] = _matmul_tn(tag + "dproj_b", sv["y_attn"], dpb).reshape(N_CHIPS, d // N_CHIPS, d)
    du, dv, g["sgu_w"], g["sgu_b"], g["sgu_ln_w"], g["sgu_ln_b"] = _sgu_bwd(
        tag + "sgu", sv["u_pre"], sv["v_pre"], p["sgu_ln_w"], p["sgu_ln_b"], p["sgu_w"], p["sgu_bfull"], dy_sgu,
        after=ready(("w_out", "proj_a", "proj_b")))
    dqr, dkv_cur, dkv_prev, dsink = _attention_bwd(tag + "attn", sv["qr"], sv["kk0"], sv["kk1"], sv["vv0"], sv["vv1"],
                                                   p["sinks"], dy_attn, after=tick(du))
    g["sinks"] = dsink[0, :N_Q_HEADS]
    dq, dkv = _rope_bwd(tag + "rope", dqr, dkv_cur, dkv_prev, cosf, sinf)
    dw_in, row0 = None, 0
    for n, t in (("q", dq), ("kv", dkv), ("u", du), ("v", dv), ("ga", dga), ("gb", dgb)):
        dw_in = _matmul_tn_rows(tag + "dw_in_" + n, t, sv["h"], dw_in, row0, IN_COLS)
        row0 += t.shape[1]
    g["w_in"] = dw_in.reshape(N_CHIPS, IN_COLS // N_CHIPS, d)
    dx, da1, dsh1 = _in_proj_bwd(tag + "in_proj", dq, dkv, du, dv, dga, dgb, p["w_in"], sv["x"], p["norm1_w"], sc1, dx1)
    g["norm1_w"] = da1 * (1.0 + sc1)
    dsc1 = da1 * p["norm1_w"]
    return dx, (dsh1, dsc1, dg1, dsh2, dsc2, dg2), g


def _pad_to(a, axis, size):
    pad = [(0, 0)] * a.ndim
    pad[axis] = (0, size - a.shape[axis])
    return jnp.pad(a, pad)


def _early_params(w_in, small):
    d = D_MODEL
    return dict(
        w_in=w_in.reshape(IN_COLS, d), norm1_w=small["norm1_w"].reshape(1, d), sinks=small["sinks"],
        sgu_ln_w=small["sgu_ln_w"].reshape(1, d), sgu_ln_b=small["sgu_ln_b"].reshape(1, d), sgu_w=small["sgu_w"],
        sgu_bfull=jnp.broadcast_to(small["sgu_b"][:, :, None], (SGU_GROUPS, SGU_CHUNK, SGU_CHUNK)))


def _mix_params(proj_a, proj_b, w_out, small):
    return dict(proj_a=proj_a, proj_b=proj_b, w_out=w_out, norm2_w=small["norm2_w"].reshape(1, D_MODEL))


def _ffn_params(w_gate, w_up, w_down, conv_w, small):
    return dict(
        w_gate=w_gate, w_up=w_up, w_down=w_down, conv_w=conv_w.transpose(1, 0, 2).reshape(3, FFN_PAD),
        conv_b=_pad_to(small["conv_b"].reshape(N_CHIPS, FFN_SHARD), 1, FFN_SHARD_PAD).reshape(1, FFN_PAD))


def _layer_params(w_in, proj_a, proj_b, w_out, w_gate, w_up, w_down, conv_w, small):
    return dict(_early_params(w_in, small), **_mix_params(proj_a, proj_b, w_out, small),
                **_ffn_params(w_gate, w_up, w_down, conv_w, small))


def _conv_grads_natural(g):
    cw = g["conv_w"].reshape(3, N_CHIPS, FFN_SHARD_PAD)[:, :, :FFN_SHARD].reshape(3, FFN_DIM)
    cb = g["conv_b"].reshape(N_CHIPS, FFN_SHARD_PAD)[:, :FFN_SHARD].reshape(FFN_DIM)
    return cw, cb


def _rope_tables(positions):
    inv_freq = ROPE_THETA ** (-jnp.arange(0, ROT_DIM, 2, dtype=F32) / ROT_DIM)
    ang = positions.astype(F32)[:, None] * inv_freq
    cos, sin = jnp.cos(ang), jnp.sin(ang)
    s = positions.shape[0]
    rest = HEAD_DIM - ROT_DIM
    cos_head = jnp.concatenate([cos, cos, jnp.ones((s, rest), F32)], axis=1)
    sin_head = jnp.concatenate([-sin, sin, jnp.zeros((s, rest), F32)], axis=1)
    return jnp.tile(cos_head, (1, LANES // HEAD_DIM)), jnp.tile(sin_head, (1, LANES // HEAD_DIM))


ADA_ROWS = 16


def _ada_fwd(name, c_rows, ada_w, ada_b_cols, tn=512):
    depth, d, n = ada_w.shape

    def body(c_ref, w_ref, b_ref, o_ref):
        cv = c_ref[...]
        act = (cv * _sigmoid(cv)).astype(BF)
        o_ref[0] = _dot(act, w_ref[0].astype(BF)) + b_ref[0]

    return _call(body, name, _sds((depth, ADA_ROWS, n), F32), (depth, n // tn),
                 [pl.BlockSpec((ADA_ROWS, d), lambda l, j: (0, 0)), pl.BlockSpec((1, d, tn), lambda l, j: (l, 0, j)),
                  pl.BlockSpec((1, 1, tn), lambda l, j: (l, 0, j))],
                 pl.BlockSpec((1, ADA_ROWS, tn), lambda l, j: (l, 0, j)), ("parallel", "parallel"))(c_rows, ada_w, ada_b_cols)


def _ada_bwd(name, c_rows, dmod_cols, tn=512):
    depth, _, n = dmod_cols.shape
    d = c_rows.shape[1]

    def body(c_ref, dm_ref, o_ref):
        cv = c_ref[...]
        act = (cv * _sigmoid(cv)).astype(BF)
        o_ref[0] = _dot_tn(act, dm_ref[0].astype(BF))

    return _call(body, name, _sds((depth, d, n), F32), (depth, n // tn),
                 [pl.BlockSpec((ADA_ROWS, d), lambda l, j: (0, 0)), pl.BlockSpec((1, ADA_ROWS, tn), lambda l, j: (l, 0, j))],
                 pl.BlockSpec((1, d, tn), lambda l, j: (l, 0, j)), ("parallel", "parallel"))(c_rows, dmod_cols)


def _colsum(name, a):
    r, n = a.shape

    def body(a_ref, o_ref):
        o_ref[...] = jnp.sum(a_ref[...], axis=0, keepdims=True)

    return _call(body, name, _sds((1, n), F32), (1,), [pl.BlockSpec((r, n), lambda i: (0, 0))],
                 pl.BlockSpec((1, n), lambda i: (0, 0)), ("arbitrary",))(a)


REL_SIBLING = (0, 0, 1)
REL_CHIPS = ((1, 0, 0), (0, 1, 0), (1, 1, 0))
REL_ALL = tuple((fx, fy, fc) for fx in (0, 1) for fy in (0, 1) for fc in (0, 1) if fx or fy or fc)


def _chip_of(dev):
    return 2 * dev[0] + dev[1]


def _dev_of(dev):
    return 4 * dev[0] + 2 * dev[1] + dev[2]


def _flip(dev, rel):
    return tuple(1 - m if f else m for m, f in zip(dev, rel))


def _exchange(name, arrays, n_out, stages, aliases=None):
    out_shapes, stages = stages[0], stages[1:]
    n_in = len(arrays)
    aliases = aliases or {}
    n_remote = sum(len(plan) for plan, _ in stages)
    n_local = sum(len(local) for _, local in stages)

    def at(ref, idx):
        return ref.at[idx] if len(idx) else ref

    def body(*refs):
        bufs = list(refs[:n_in + n_out])
        for i_in, i_out in aliases.items():
            bufs[i_in] = bufs[n_in + i_out]
        send_sems, recv_sems, local_sems = refs[n_in + n_out:]
        me = (lax.axis_index("x"), lax.axis_index("y"), lax.axis_index("c"))
        base_r = base_l = 0
        pending = []
        for plan, local in stages:
            def remote(k, entry, sender, receiver):
                rel, si, ssel, di, dsel = entry
                return pltpu.make_async_remote_copy(
                    src_ref=at(bufs[si], ssel(sender, receiver)), dst_ref=at(bufs[di], dsel(sender, receiver)),
                    send_sem=send_sems.at[k], recv_sem=recv_sems.at[k], device_id=_flip(me, rel), device_id_type=MESH)

            sends = [remote(base_r + k, e, me, _flip(me, e[0])) for k, e in enumerate(plan)]
            for cp in sends:
                cp.start()
            for k, (si, ssel, di, dsel) in enumerate(local):
                cp = pltpu.make_async_copy(at(bufs[si], ssel(me)), at(bufs[di], dsel(me)), local_sems.at[base_l + k])
                cp.start()
                pending.append(cp.wait)
            for k, e in enumerate(plan):
                remote(base_r + k, e, _flip(me, e[0]), me).wait_recv()
            pending += [cp.wait_send for cp in sends]
            base_r += len(plan)
            base_l += len(local)
        for wait in pending:
            wait()

    any_spec = pl.BlockSpec(memory_space=pl.ANY)
    return pl.pallas_call(
        body, name=name, out_shape=tuple(out_shapes), in_specs=[any_spec] * n_in, out_specs=tuple([any_spec] * n_out),
        input_output_aliases=dict(aliases),
        scratch_shapes=[pltpu.SemaphoreType.DMA((max(n_remote, 1),)), pltpu.SemaphoreType.DMA((max(n_remote, 1),)),
                        pltpu.SemaphoreType.DMA((max(n_local, 1),))])(*arrays)


HBM_SPEC = pl.BlockSpec(memory_space=pltpu.HBM)
SEM_SPEC = pl.BlockSpec(memory_space=pltpu.SEMAPHORE)


def _split_copies(bufs, plan, local, send_sems, recv_sems, local_sems):
    me = (lax.axis_index("x"), lax.axis_index("y"), lax.axis_index("c"))

    def at(ref, idx):
        return ref.at[idx] if len(idx) else ref

    def remote(k, sender, receiver):
        rel, si, ssel, di, dsel = plan[k]
        return pltpu.make_async_remote_copy(
            src_ref=at(bufs[si], ssel(sender, receiver)), dst_ref=at(bufs[di], dsel(sender, receiver)),
            send_sem=send_sems.at[k], recv_sem=recv_sems.at[k], device_id=_flip(me, rel), device_id_type=MESH)

    sends = [remote(k, me, _flip(me, plan[k][0])) for k in range(len(plan))]
    arrivals = [remote(k, _flip(me, plan[k][0]), me) for k in range(len(plan))]
    locs = [pltpu.make_async_copy(at(bufs[si], ssel(me)), at(bufs[di], dsel(me)), local_sems.at[k])
            for k, (si, ssel, di, dsel) in enumerate(local)]
    return sends, arrivals, locs


def _exchange_start(name, arrays, out_shapes, plan, local, inplace=False):
    n_in, n_out = len(arrays), len(out_shapes)
    n_buf = n_in + n_out

    def body(*refs):
        sems = refs[n_buf:n_buf + 3]
        bufs = refs[n_buf + 3:2 * n_buf + 3]
        sends, _, locs = _split_copies(bufs * 2 if inplace else bufs, plan, local, *sems)
        for cp in sends + locs:
            cp.start()
        refs[-1][...] = jnp.zeros_like(refs[-1])

    zones = [lax.empty(o.shape, o.dtype) for o in out_shapes]
    operands = [pltpu.with_memory_space_constraint(a, pltpu.HBM) for a in list(arrays) + zones]
    sem = lambda n: pltpu.SemaphoreType.DMA((max(n, 1),))
    out = pl.pallas_call(
        body, name=name,
        out_shape=(sem(len(plan)), sem(len(plan)), sem(len(local)), *[pltpu.HBM(a.shape, a.dtype) for a in operands],
                   _sds((SUBLANES, LANES), F32)),
        in_specs=[HBM_SPEC] * n_buf,
        out_specs=(SEM_SPEC, SEM_SPEC, SEM_SPEC, *[HBM_SPEC] * n_buf, pl.BlockSpec(memory_space=pltpu.VMEM)),
        input_output_aliases={i: 3 + i for i in range(n_buf)},
        compiler_params=pltpu.CompilerParams(has_side_effects=pltpu.SideEffectType.DATAFLOW_SIDE_EFFECTING))(*operands)
    pending = dict(name=name, sems=out[:3], thru=out[3:3 + n_in], zones=out[3 + n_in:3 + n_buf], plan=plan, local=local,
                   inplace=inplace)
    return pending, out[-1]


def _exchange_wait(pending, after, both=False):
    thru, zones, plan, local, inplace = (pending[k] for k in ("thru", "zones", "plan", "local", "inplace"))
    n_in, n_buf = len(thru), len(thru) + len(zones)

    def body(*refs):
        bufs = refs[:n_buf]
        sends, arrivals, locs = _split_copies(bufs * 2 if inplace else bufs, plan, local, *refs[n_buf:n_buf + 3])
        for cp in arrivals:
            cp.wait_recv()
        for cp in sends:
            cp.wait_send()
        for cp in locs:
            cp.wait()

    out = pl.pallas_call(
        body, name=pending["name"] + "_wait", out_shape=tuple(pltpu.HBM(a.shape, a.dtype) for a in list(thru) + list(zones)),
        in_specs=[HBM_SPEC] * n_buf + [SEM_SPEC] * 3 + [pl.BlockSpec(memory_space=pl.ANY)],
        out_specs=tuple([HBM_SPEC] * n_buf), input_output_aliases={i: i for i in range(n_buf)},
        compiler_params=pltpu.CompilerParams(has_side_effects=pltpu.SideEffectType.DATAFLOW_SIDE_EFFECTING))(
            *thru, *zones, *pending["sems"], after)
    if both:
        return out[:n_in], out[n_in:]
    return out[:n_in] if inplace else out[n_in:]


def _whole(*_):
    return ()


def _half_rows(rows, core):
    return pl.ds(core * (rows // 2), rows // 2)


def _gather_weights_plan(shards):
    n = len(shards)
    dsts = [_sds((N_CHIPS,) + a.shape, a.dtype) for a in shards]
    fetch, forward = [], []
    for t, a in enumerate(shards):
        rows = a.shape[0]
        if rows % (2 * 16) == 0:
            fetch += [(rel, t, (lambda s_, r_, rows=rows: (_half_rows(rows, s_[2]),)), n + t,
                       (lambda s_, r_, rows=rows: (_chip_of(s_), _half_rows(rows, s_[2])))) for rel in REL_CHIPS]
            forward += [(REL_SIBLING, n + t, (lambda s_, r_, rows=rows, rel=rel: (_chip_of(_flip(s_, rel)), _half_rows(rows, s_[2]))),
                         n + t, (lambda s_, r_, rows=rows, rel=rel: (_chip_of(_flip(s_, rel)), _half_rows(rows, s_[2]))))
                        for rel in REL_CHIPS]
        else:
            fetch += [(rel, t, _whole, n + t, lambda s_, r_: (_chip_of(s_),)) for rel in REL_CHIPS]
    local = [(t, _whole, n + t, lambda me: (_chip_of(me),)) for t in range(n)]
    return dsts, fetch, local, forward


def _gather_weights_start(name, shards):
    dsts, fetch, local, forward = _gather_weights_plan(shards)
    pending, token = _exchange_start(name, shards, dsts, fetch, local)
    return dict(pending, forward=forward), token


def _gather_weights_finish(pending, after):
    landed = _exchange_wait(pending, after)
    n = len(landed)
    return _exchange(pending["name"] + "_forward", landed, n, [[_sds(a.shape, a.dtype) for a in landed], (pending["forward"], [])],
                     aliases={t: t for t in range(n)})


def _gather_chips(name, arrays):
    n = len(arrays)
    dsts = [_sds((N_CHIPS,) + a.shape, a.dtype) for a in arrays]
    plan = [(rel, t, _whole, n + t, lambda s_, r_: (_chip_of(s_),)) for t in range(n) for rel in REL_CHIPS]
    local = [(t, _whole, n + t, lambda me: (_chip_of(me),)) for t in range(n)]
    return _exchange(name, arrays, n, [dsts, (plan, local)])


def _gather_all(name, a):
    plan = [(rel, 0, _whole, 1, lambda s_, r_: (_dev_of(s_),)) for rel in REL_ALL]
    local = [(0, _whole, 1, lambda me: (_dev_of(me),))]
    return _exchange(name, [a], 1, [[_sds((2 * N_CHIPS,) + a.shape, a.dtype)], (plan, local)])[0]


def _swap_halves_start(name, grads):
    n = len(grads)
    dsts = [_sds((g.shape[0], g.shape[1] // 2, g.shape[2]), g.dtype) for g in grads]
    plan = [(REL_SIBLING, t, (lambda s_, r_, rows=g.shape[1]: (pl.ds(0, N_CHIPS), _half_rows(rows, r_[2]))), n + t, _whole)
            for t, g in enumerate(grads)]
    return _exchange_start(name, grads, dsts, plan, [])


def _scatter_chips_plan(sums):
    n = len(sums)
    dsts = [_sds(a.shape, a.dtype) for a in sums]
    plan = [(rel, t, lambda s_, r_: (_chip_of(r_),), n + t, lambda s_, r_: (_chip_of(s_),))
            for t in range(n) for rel in REL_CHIPS]
    local = [(t, lambda me: (_chip_of(me),), n + t, lambda me: (_chip_of(me),)) for t in range(n)]
    return dsts, plan, local


def _scatter_chips(name, sums):
    dsts, plan, local = _scatter_chips_plan(sums)
    return _exchange(name, sums, len(sums), [dsts, (plan, local)])


def _scatter_chips_start(name, sums):
    dsts, plan, local = _scatter_chips_plan(sums)
    return _exchange_start(name, sums, dsts, plan, local)


def _swap_back_start(name, totals, layer):
    n = len(totals)
    plan = [(REL_SIBLING, n + t, (lambda s_, r_, rows=a.shape[1]: (layer, _half_rows(rows, s_[2]))),
             n + t, (lambda s_, r_, rows=a.shape[1]: (layer, _half_rows(rows, s_[2])))) for t, a in enumerate(totals)]
    return _exchange_start(name, totals, [], plan, [], inplace=True)


def _add_halves(name, g, recv, core):
    nch, half, c = recv.shape

    def body(core_ref, g_ref, r_ref, o_ref):
        o_ref[0] = (g_ref[0, 0].astype(F32) + r_ref[0].astype(F32)).astype(o_ref.dtype)

    spec = pltpu.PrefetchScalarGridSpec(
        num_scalar_prefetch=1, grid=(nch,),
        in_specs=[pl.BlockSpec((1, 1, half, c), lambda k, core_ref: (k, core_ref[0], 0, 0)),
                  pl.BlockSpec((1, half, c), lambda k, core_ref: (k, 0, 0))],
        out_specs=pl.BlockSpec((1, half, c), lambda k, core_ref: (k, 0, 0)))
    return pl.pallas_call(body, name=name, out_shape=_sds(recv.shape, recv.dtype), grid_spec=spec,
                          compiler_params=pltpu.CompilerParams(dimension_semantics=("parallel",),
                                                               vmem_limit_bytes=VMEM_LIMIT))(
                                                                   core, g.reshape(nch, 2, half, c), recv)


def _sum_chips(name, a, core, layer, total):
    nch, half, c = a.shape

    def body(core_ref, a_ref, *rest):
        o_ref = rest[-1]
        acc = a_ref[0].astype(F32)
        for k in range(1, nch):
            acc = acc + a_ref[k].astype(F32)
        o_ref[0, 0] = acc

    in_specs = [pl.BlockSpec((nch, half, c), lambda i, core_ref: (0, 0, 0))]
    args = [core, a]
    if total is not None:
        in_specs.append(pl.BlockSpec(memory_space=pl.ANY))
        args.append(total.reshape(DEPTH, 2, half, c))
    spec = pltpu.PrefetchScalarGridSpec(
        num_scalar_prefetch=1, grid=(1,), in_specs=in_specs,
        out_specs=pl.BlockSpec((1, 1, half, c), lambda i, core_ref: (layer, core_ref[0], 0, 0)))
    out = pl.pallas_call(body, name=name, out_shape=_sds((DEPTH, 2, half, c), F32), grid_spec=spec,
                         input_output_aliases={2: 0} if total is not None else {},
                         compiler_params=pltpu.CompilerParams(dimension_semantics=("arbitrary",),
                                                              vmem_limit_bytes=VMEM_LIMIT))(*args)
    return out.reshape(DEPTH, 2 * half, c)


def _adamw_update(w, g, m, v):
    mn = ADAM_B1 * m + (1.0 - ADAM_B1) * g
    vn = ADAM_B2 * v + (1.0 - ADAM_B2) * (g * g)
    m_hat = mn / (1.0 - ADAM_B1 ** ADAM_STEP)
    v_hat = vn / (1.0 - ADAM_B2 ** ADAM_STEP)
    return -ADAM_LR * (m_hat / (jnp.sqrt(v_hat) + ADAM_EPS) + ADAM_WD * w), mn, vn


def _adamw(name, w, g, m, v):
    depth, r, c = w.shape
    tr = next(t for t in (512, 448, 384, 352, 336, 256, 192, 128, 64, 32, 16, 8) if r % t == 0 and t * c <= ADAM_TILE_ELEMS)

    def body(w_ref, g_ref, m_ref, v_ref, go_ref, d_ref, mo_ref, vo_ref):
        gv = g_ref[...]
        go_ref[...] = gv
        d_ref[...], mo_ref[...], vo_ref[...] = _adamw_update(w_ref[...], gv, m_ref[...], v_ref[...])

    spec = pl.BlockSpec((1, tr, c), lambda l, i: (l, i, 0))
    o = _sds(w.shape, F32)
    return _call(body, name, (o, o, o, o), (depth, r // tr), [spec] * 4, (spec,) * 4, ("parallel", "parallel"))(w, g, m, v)


def _adamw_small(name, ws, gs, ms, vs):
    n = len(ws)

    def body(*refs):
        for t in range(n):
            w_ref, g_ref, m_ref, v_ref = (refs[k * n + t] for k in range(4))
            d_ref, mo_ref, vo_ref = (refs[(4 + k) * n + t] for k in range(3))
            d_ref[...], mo_ref[...], vo_ref[...] = _adamw_update(w_ref[...], g_ref[...], m_ref[...], v_ref[...])

    outs = [_sds(w.shape, F32) for w in ws]
    res = pl.pallas_call(body, name=name, out_shape=tuple(outs * 3))(*ws, *gs, *ms, *vs)
    return res[:n], res[n:2 * n], res[2 * n:]


def _pack(arrays, rows):
    flat = jnp.concatenate([a.reshape(-1).astype(F32) for a in arrays])
    return _pad_to(flat, 0, rows * LANES).reshape(rows, LANES)


def _unpack(packed, shapes):
    flat = packed.reshape(-1)
    out, off = [], 0
    for shp in shapes:
        n = 1
        for s_ in shp:
            n *= s_
        out.append(flat[off:off + n].reshape(shp))
        off += n
    return out


_MATRICES = ("w_in", "proj_a", "proj_b", "w_out", "w_gate", "w_up", "w_down")
_SMALL = (("norm1_w", (D_MODEL,)), ("sinks", (N_Q_HEADS,)), ("sgu_ln_w", (SGU_WIDTH,)), ("sgu_ln_b", (SGU_WIDTH,)),
          ("sgu_w", (SGU_GROUPS, SGU_CHUNK, SGU_CHUNK)), ("sgu_b", (SGU_GROUPS, SGU_CHUNK)), ("norm2_w", (D_MODEL,)),
          ("conv_w", (3, FFN_DIM)), ("conv_b", (FFN_DIM,)), ("final_norm_w", (D_MODEL,)))
SMALL_ROWS = 320
ADAM_TILE_ELEMS = 384 * 1024


def _reduce_cores_start(tag, partial):
    names = list(partial)
    pending, token = _swap_halves_start(tag + "_cores", [partial[k] for k in names])
    return dict(pending, tag=tag, names=names), token


def _reduce_chips_start(pending, core, after):
    tag, names = pending["tag"], pending["names"]
    mine, theirs = _exchange_wait(pending, after, both=True)
    sums = [_add_halves(f"{tag}_cores_add_{k}", g, r, core) for k, g, r in zip(names, mine, theirs)]
    scatter, token = _scatter_chips_start(tag + "_chips", sums)
    return dict(scatter, tag=tag, names=names), token


def _reduce_back_start(pending, l, core, totals, after):
    tag, names = pending["tag"], pending["names"]
    sums = [_sum_chips(f"{tag}_chips_add_{k}", a, core, l, totals.get(k)) for k, a in zip(names, _exchange_wait(pending, after))]
    back, _ = _swap_back_start(tag + "_back", sums, l)
    return dict(back, names=names)


def kernel(x, c, positions, ada_w, ada_b, norm1_w, w_in, attn_sinks, sgu_ln_w, sgu_ln_b, sgu_w, sgu_b, proj_a, proj_b, w_out, norm2_w, ffn_w_gate, ffn_w_up, ffn_conv_w, ffn_conv_b, ffn_w_down, final_norm_w, loss_target, m_ada_w, m_ada_b, m_norm1_w, m_w_in, m_attn_sinks, m_sgu_ln_w, m_sgu_ln_b, m_sgu_w, m_sgu_b, m_proj_a, m_proj_b, m_w_out, m_norm2_w, m_ffn_w_gate, m_ffn_w_up, m_ffn_conv_w, m_ffn_conv_b, m_ffn_w_down, m_final_norm_w, v_ada_w, v_ada_b, v_norm1_w, v_w_in, v_attn_sinks, v_sgu_ln_w, v_sgu_ln_b, v_sgu_w, v_sgu_b, v_proj_a, v_proj_b, v_w_out, v_norm2_w, v_ffn_w_gate, v_ffn_w_up, v_ffn_conv_w, v_ffn_conv_b, v_ffn_w_down, v_final_norm_w):
    d = D_MODEL
    ax, ay, ac = lax.axis_index("x"), lax.axis_index("y"), lax.axis_index("c")
    chip = 2 * ax + ay
    dev = 4 * ax + 2 * ay + ac
    core = ac.astype(jnp.int32).reshape(1)

    c_all = _gather_all("gather_cond", c.reshape(SUBLANES, d // SUBLANES)).reshape(2 * N_CHIPS, d)
    c_rows = _pad_to(c_all, 0, ADA_ROWS)
    ada_cols = ada_w.shape[2]
    ada_b_cols = lax.dynamic_slice_in_dim(ada_b, chip * ada_cols, ada_cols, axis=1).reshape(DEPTH, 1, ada_cols)
    mod_cols = _ada_fwd("ada_fwd", c_rows, ada_w, ada_b_cols)
    mod_all = _gather_chips("gather_mod", [mod_cols])[0]
    mod_mine = lax.dynamic_index_in_dim(mod_all, dev, axis=2, keepdims=False)
    mod_mine = mod_mine.transpose(1, 0, 2).reshape(DEPTH, 1, 6 * d)
    mods = [tuple(jnp.split(mod_mine[l], 6, axis=-1)) for l in range(DEPTH)]

    tr = lambda a: jnp.swapaxes(a, 1, 2)
    shards = [tr(w_in).astype(BF), proj_a.astype(BF), proj_b.astype(BF), w_out.astype(BF),
              _pad_to(tr(ffn_w_gate).astype(BF), 1, FFN_SHARD_PAD), _pad_to(tr(ffn_w_up).astype(BF), 1, FFN_SHARD_PAD),
              _pad_to(ffn_w_down.astype(BF), 1, FFN_SHARD_PAD), _pad_to(ffn_conv_w, 2, FFN_SHARD_PAD)]
    token = mod_all[0, 0, :SUBLANES, :LANES]
    fetches = []
    for l in range(DEPTH):
        groups = []
        for tag, members in (("in", shards[:1]), ("mix", shards[1:4]), ("ffn", shards[4:])):
            behind = (token[0, 0] * 0.0).astype(members[0].dtype)
            pending, token = _gather_weights_start(f"l{l}_gather_{tag}", [members[0][l] + behind] + [a[l] for a in members[1:]])
            groups.append(pending)
        fetches.append(groups)

    small_in = dict(norm1_w=norm1_w, sinks=attn_sinks, sgu_ln_w=sgu_ln_w, sgu_ln_b=sgu_ln_b, sgu_w=sgu_w, sgu_b=sgu_b,
                    norm2_w=norm2_w, conv_b=ffn_conv_b)
    cosf, sinf = _rope_tables(positions[0])
    small_of = lambda l: {k: v[l] for k, v in small_in.items()}

    h = x[0]
    saved, params = [], []
    for l in range(DEPTH):
        first, mix, ffn = fetches[l]
        w_in_l = _gather_weights_finish(first, token if l == 0 else h)
        late = lambda y, l=l, mix=mix: _mix_params(*_gather_weights_finish(mix, y), small_of(l))
        later = lambda y, l=l, ffn=ffn: _ffn_params(*_gather_weights_finish(ffn, y), small_of(l))
        h, sv, p = _layer_fwd(l, h, mods[l], _early_params(w_in_l[0], small_of(l)), cosf, sinf, late=late, later=later)
        saved.append(sv)
        params.append(p)
    dx, loss_part, d_final = _loss_head("loss_head", h, final_norm_w.reshape(1, d), loss_target[0])
    loss = lax.psum(loss_part[0, 0], ("x", "y", "c"))

    def small_pack(l, grads):
        cw, cb = _conv_grads_natural(grads)
        nat = dict(grads, conv_w=cw, conv_b=cb, final_norm_w=d_final if l == DEPTH - 1 else jnp.zeros((d,), F32))
        return _pack([nat[k] for k, _ in _SMALL], N_CHIPS * SMALL_ROWS).reshape(N_CHIPS, SMALL_ROWS, LANES)

    waiting, inflight = [], []

    def send(tag, partial):
        pending, token = _reduce_cores_start(tag, partial)
        waiting.append(pending)
        return token

    def tick(y):
        token = None
        while waiting:
            pending, token = _reduce_chips_start(waiting.pop(0), core, y)
            inflight.append(pending)
        return token

    dmods = [None] * DEPTH
    dx, dmods[1], grads = _layer_bwd(1, dx, mods[1], params[1], saved[1], cosf, sinf)
    token = send("l1_reduce", dict({k: grads[k] for k in _MATRICES}, small=small_pack(1, grads)))
    dx, dmods[0], grads = _layer_bwd(0, dx, mods[0], params[0], saved[0], cosf, sinf, after=token,
                                     emit=lambda part: send("l0_reduce_" + "_".join(part), part), tick=tick)
    dmod_mine = jnp.concatenate([jnp.concatenate(dmods[l], axis=1) for l in range(DEPTH)], axis=1)
    dmod_all = _gather_all("gather_dmod", dmod_mine.reshape(SUBLANES, -1)).reshape(2 * N_CHIPS, DEPTH * 6 * d)
    send("l0_reduce_in", dict(w_in=grads["w_in"], small=small_pack(0, grads) + dmod_all[0, 0] * 0.0))
    tick(dmod_all)

    totals, flying = {}, None

    def land(after):
        if flying is not None:
            totals.update(zip(flying["names"], _exchange_wait(flying, after)))

    for pending in inflight[:-1]:
        land(dx)
        flying = _reduce_back_start(pending, 1 if pending["tag"].startswith("l1") else 0, core, totals, dx)
    land(dx)
    flying = None

    g_ada_b = _colsum("ada_b_grad", dmod_all).reshape(DEPTH, 6 * d)
    dmod_cols = jnp.stack([lax.dynamic_slice_in_dim(dmod_all, l * 6 * d + chip * ada_cols, ada_cols, axis=1)
                           for l in range(DEPTH)])
    g_ada_w = _ada_bwd("ada_w_grad", c_rows, _pad_to(dmod_cols, 1, ADA_ROWS))
    big = dict(w_in=(tr(w_in), tr(m_w_in), tr(v_w_in)), proj_a=(proj_a, m_proj_a, v_proj_a), proj_b=(proj_b, m_proj_b, v_proj_b),
               w_out=(w_out, m_w_out, v_w_out), w_gate=(tr(ffn_w_gate), tr(m_ffn_w_gate), tr(v_ffn_w_gate)),
               w_up=(tr(ffn_w_up), tr(m_ffn_w_up), tr(v_ffn_w_up)), w_down=(ffn_w_down, m_ffn_w_down, v_ffn_w_down))
    upd, g_big = {}, {}

    def update(k):
        res = _adamw("adamw_" + k, big[k][0], totals[k], *big[k][1:])
        res = [tr(a) for a in res] if k in ("w_in", "w_gate", "w_up") else res
        g_big[k], upd[k] = res[0], res[1:]

    for k in ("w_down", "w_gate", "w_up", "w_out", "proj_a", "proj_b"):
        update(k)
    g_big["ada_w"], *upd["ada_w"] = _adamw("adamw_ada_w", ada_w, g_ada_w, m_ada_w, v_ada_w)
    flying = _reduce_back_start(inflight[-1], 0, core, totals, upd["ada_w"][0])
    land(upd["ada_w"][1])
    update("w_in")

    small_all = _gather_chips("gather_small", [totals["small"]])[0]
    small_g = small_all.transpose(1, 0, 2, 3).reshape(DEPTH, -1)
    per_layer = [_unpack(small_g[l], [shp for _, shp in _SMALL]) for l in range(DEPTH)]
    sg = {k: jnp.stack([per_layer[l][i] for l in range(DEPTH)]) for i, (k, _) in enumerate(_SMALL)}
    g_final = sg["final_norm_w"][DEPTH - 1]
    g_conv_w = lax.dynamic_slice_in_dim(sg["conv_w"], chip * FFN_SHARD, FFN_SHARD, axis=2)

    rest = [("ada_b", ada_b, g_ada_b, m_ada_b, v_ada_b), ("norm1_w", norm1_w, sg["norm1_w"], m_norm1_w, v_norm1_w),
            ("attn_sinks", attn_sinks, sg["sinks"], m_attn_sinks, v_attn_sinks),
            ("sgu_ln_w", sgu_ln_w, sg["sgu_ln_w"], m_sgu_ln_w, v_sgu_ln_w),
            ("sgu_ln_b", sgu_ln_b, sg["sgu_ln_b"], m_sgu_ln_b, v_sgu_ln_b), ("sgu_w", sgu_w, sg["sgu_w"], m_sgu_w, v_sgu_w),
            ("sgu_b", sgu_b, sg["sgu_b"], m_sgu_b, v_sgu_b), ("norm2_w", norm2_w, sg["norm2_w"], m_norm2_w, v_norm2_w),
            ("ffn_conv_w", ffn_conv_w, g_conv_w, m_ffn_conv_w, v_ffn_conv_w),
            ("ffn_conv_b", ffn_conv_b, sg["conv_b"], m_ffn_conv_b, v_ffn_conv_b),
            ("final_norm_w", final_norm_w.reshape(1, d), g_final.reshape(1, d), m_final_norm_w.reshape(1, d),
             v_final_norm_w.reshape(1, d))]
    rest_out = _adamw_small("adamw_rest", *[[r[i] for r in rest] for i in (1, 2, 3, 4)])
    g_rest = {r[0]: r[2] for r in rest}
    u_rest = {r[0]: tuple(o[i] for o in rest_out) for i, r in enumerate(rest)}
    g_rest["final_norm_w"] = g_final
    u_rest["final_norm_w"] = tuple(a.reshape(d) for a in u_rest["final_norm_w"])

    names = ("ada_w", "ada_b", "norm1_w", "w_in", "attn_sinks", "sgu_ln_w", "sgu_ln_b", "sgu_w", "sgu_b", "proj_a", "proj_b",
             "w_out", "norm2_w", "ffn_w_gate", "ffn_w_up", "ffn_conv_w", "ffn_conv_b", "ffn_w_down", "final_norm_w")
    alias = {"ffn_w_gate": "w_gate", "ffn_w_up": "w_up", "ffn_w_down": "w_down"}
    grad_of = lambda n: g_rest[n] if n in g_rest else g_big[alias.get(n, n)]
    upd_of = lambda n: u_rest[n] if n in u_rest else upd[alias.get(n, n)]
    return (loss, dx[None], *[grad_of(n) for n in names], *[upd_of(n)[0] for n in names],
            *[upd_of(n)[1] for n in names], *[upd_of(n)[2] for n in names])
```

```python
import jax
import jax.numpy as jnp
from jax import lax
from jax.experimental import pallas as pl
from jax.experimental.pallas import tpu as pltpu

F32 = jnp.float32
BF = jnp.bfloat16

D_MODEL = 1024
N_Q_HEADS = 16
N_KV_HEADS = 2
HEAD_DIM = 64
ATTN_BLOCK = 128
ROPE_THETA = 500000.0
ROT_DIM = HEAD_DIM // 4
SGU_WIDTH = 1024
SGU_GROUPS = 8
SGU_CHUNK = 128
FFN_DIM = 2816
NORM_EPS = 1e-6
DEPTH = 2
IN_COLS = 5376
N_CHIPS = 4
FFN_SHARD = FFN_DIM // N_CHIPS
FFN_SHARD_PAD = 768
FFN_PAD = N_CHIPS * FFN_SHARD_PAD
LANES = 128
SUBLANES = 8
HALO = 16
VMEM_LIMIT = 56 * 1024 * 1024
NEG_BIG = -1e30

ADAM_LR = 0.001
ADAM_B1 = 0.9
ADAM_B2 = 0.999
ADAM_EPS = 1e-08
ADAM_WD = 0.01
ADAM_STEP = 10

MESH = pl.DeviceIdType.MESH

Q_END = 1024
KV_END = 1280
U_END = 2304
Z_END = 3328
GA_END = 4352


def _sds(shape, dtype):
    return jax.ShapeDtypeStruct(tuple(shape), dtype)


def _call(body, name, out_shape, grid, in_specs, out_specs, semantics, scratch=(), after=None):
    n_in = len(in_specs)
    fn = body
    if after is not None:
        def fn(*refs):
            return body(*refs[:n_in], *refs[n_in + 1:])

        in_specs = list(in_specs) + [pl.BlockSpec(memory_space=pl.ANY)]
    call = pl.pallas_call(
        fn, name=name, out_shape=out_shape, grid=grid, in_specs=in_specs, out_specs=out_specs,
        scratch_shapes=scratch,
        compiler_params=pltpu.CompilerParams(dimension_semantics=semantics, vmem_limit_bytes=VMEM_LIMIT))
    if after is None:
        return call
    return lambda *args: call(*args, after)


def _rows(tm, width, col=0):
    return pl.BlockSpec((tm, width), lambda i: (i, col))


def _vec(width):
    return pl.BlockSpec((1, width), lambda i: (0, 0))


def _resident(shape):
    zeros = (0,) * len(shape)
    return pl.BlockSpec(tuple(shape), lambda *_: zeros, pipeline_mode=pl.Buffered(1))


def _sigmoid(x):
    return 0.5 + 0.5 * jnp.tanh(0.5 * x)


def _gelu(x):
    return 0.5 * x * (1.0 + lax.erf(x * 0.7071067811865476))


def _gelu_and_grad(x):
    cdf = 0.5 * (1.0 + lax.erf(x * 0.7071067811865476))
    return x * cdf, cdf + x * jnp.exp(-0.5 * x * x) * 0.3989422804014327


def _dot(a, b):
    return jnp.dot(a, b, preferred_element_type=F32)


def _dot_nt(a, b):
    return lax.dot_general(a, b, (((1,), (1,)), ((), ())), preferred_element_type=F32)


def _dot_tn(a, b):
    return lax.dot_general(a, b, (((0,), (0,)), ((), ())), preferred_element_type=F32)


def _rms(xv):
    return lax.rsqrt(jnp.mean(xv * xv, axis=-1, keepdims=True) + NORM_EPS)


def _matmul_tn(name, a, b, tk=512, tn=1024, blocked=False, after=None):
    s, k = a.shape
    n = b.shape[1]
    tk, tn = min(tk, k), min(tn, n)

    def body(a_ref, b_ref, o_ref):
        res = _dot_tn(a_ref[...], b_ref[...]).astype(o_ref.dtype)
        if blocked:
            o_ref[0] = res
        else:
            o_ref[...] = res

    if blocked:
        out, ospec = _sds((n // tn, k, tn), BF), pl.BlockSpec((1, tk, tn), lambda i, j: (j, i, 0))
    else:
        out, ospec = _sds((k, n), BF), pl.BlockSpec((tk, tn), lambda i, j: (i, j))
    return _call(body, name, out, (k // tk, n // tn),
                 [pl.BlockSpec((s, tk), lambda i, j: (0, i)), pl.BlockSpec((s, tn), lambda i, j: (0, j))],
                 ospec, ("parallel", "parallel"), after=after)(a, b)


def _matmul_tn_rows(name, a, b, out, row0, rows_total, tk=256):
    s, k = a.shape
    n = b.shape[1]

    def body(a_ref, b_ref, *rest):
        rest[-1][...] = _dot_tn(a_ref[...], b_ref[...]).astype(BF)

    in_specs = [pl.BlockSpec((s, tk), lambda i: (0, i)), _resident(b.shape)]
    args = [a, b]
    if out is not None:
        in_specs.append(pl.BlockSpec(memory_space=pl.ANY))
        args.append(out)
    return pl.pallas_call(
        body, name=name, out_shape=_sds((rows_total, n), BF), grid=(k // tk,), in_specs=in_specs,
        out_specs=pl.BlockSpec((tk, n), lambda i: (row0 // tk + i, 0)),
        input_output_aliases={2: 0} if out is not None else {},
        compiler_params=pltpu.CompilerParams(dimension_semantics=("parallel",), vmem_limit_bytes=VMEM_LIMIT))(*args)


def _rope_partner(v):
    lane = lax.broadcasted_iota(jnp.int32, (1, LANES), 1) % HEAD_DIM
    return jnp.where(lane < ROT_DIM // 2, pltpu.roll(v, LANES - ROT_DIM // 2, axis=1), pltpu.roll(v, ROT_DIM // 2, axis=1))


def _dup_half(v, half):
    lane = lax.broadcasted_iota(jnp.int32, (1, LANES), 1)
    keep = jnp.where((lane >= HEAD_DIM) == (half == 1), v, 0.0)
    return keep + pltpu.roll(keep, HEAD_DIM, axis=1)


def _in_proj(name, x, w, sc, sh, w_in, cosf, sinf, tm=512, after=None):
    s, d = x.shape
    tm = min(tm, s)

    def body(x_ref, w_ref, sc_ref, sh_ref, win_ref, cos_ref, sin_ref,
             h_ref, qr_ref, kk0_ref, kk1_ref, vv0_ref, vv1_ref, u_ref, v_ref, ga_ref, gb_ref):
        xv = x_ref[...]
        h = ((xv * _rms(xv)) * w_ref[...] * (1.0 + sc_ref[...]) + sh_ref[...]).astype(BF)
        h_ref[...] = h
        cosv, sinv = cos_ref[...], sin_ref[...]
        q = _dot_nt(h, win_ref[:Q_END, :])
        for j in range(D_MODEL // LANES):
            qv = q[:, j * LANES:(j + 1) * LANES]
            qr_ref[:, j * LANES:(j + 1) * LANES] = ((qv * cosv + _rope_partner(qv) * sinv) * ATTN_SCALE).astype(BF)
        kv = _dot_nt(h, win_ref[Q_END:KV_END, :])
        kr = kv[:, :LANES] * cosv + _rope_partner(kv[:, :LANES]) * sinv
        vv = kv[:, LANES:]
        kk0_ref[...] = _dup_half(kr, 0).astype(BF)
        kk1_ref[...] = _dup_half(kr, 1).astype(BF)
        vv0_ref[...] = _dup_half(vv, 0).astype(BF)
        vv1_ref[...] = _dup_half(vv, 1).astype(BF)
        u_ref[...] = _dot_nt(h, win_ref[KV_END:U_END, :])
        v_ref[...] = _dot_nt(h, win_ref[U_END:Z_END, :])
        ga_ref[...] = _dot_nt(h, win_ref[Z_END:GA_END, :]).astype(BF)
        gb_ref[...] = _dot_nt(h, win_ref[GA_END:, :]).astype(BF)

    wide, kvs, pre = _sds((s, d), BF), _sds((s, LANES), BF), _sds((s, d), F32)
    return _call(body, name, (wide, wide, kvs, kvs, kvs, kvs, pre, pre, wide, wide), (s // tm,),
                 [_rows(tm, d), _vec(d), _vec(d), _vec(d), _resident(w_in.shape), _rows(tm, LANES), _rows(tm, LANES)],
                 (_rows(tm, d), _rows(tm, d)) + (_rows(tm, LANES),) * 4 + (_rows(tm, d),) * 4, ("parallel",), after=after)(
                     x, w, sc, sh, w_in, cosf, sinf)


def _in_proj_bwd(name, dq, dkv, du, dv, dga, dgb, w_in, x, w, sc, dx_in, tm=512):
    s, d = x.shape
    tm = min(tm, s)

    def body(dq_ref, dkv_ref, du_ref, dv_ref, dga_ref, dgb_ref, win_ref, x_ref, w_ref, sc_ref, dxin_ref,
             dx_ref, da_ref, dsh_ref):
        @pl.when(pl.program_id(0) == 0)
        def _():
            da_ref[...] = jnp.zeros_like(da_ref)
            dsh_ref[...] = jnp.zeros_like(dsh_ref)

        dh = (_dot(dq_ref[...], win_ref[:Q_END, :]) + _dot(dkv_ref[...], win_ref[Q_END:KV_END, :])
              + _dot(du_ref[...], win_ref[KV_END:U_END, :]) + _dot(dv_ref[...], win_ref[U_END:Z_END, :])
              + _dot(dga_ref[...], win_ref[Z_END:GA_END, :]) + _dot(dgb_ref[...], win_ref[GA_END:, :]))
        xv = x_ref[...]
        r = _rms(xv)
        xn = xv * r
        dxn = dh * (w_ref[...] * (1.0 + sc_ref[...]))
        dx_ref[...] = dxin_ref[...] + r * (dxn - xn * jnp.mean(dxn * xn, axis=-1, keepdims=True))
        da_ref[...] += jnp.sum(dh * xn, axis=0, keepdims=True)
        dsh_ref[...] += jnp.sum(dh, axis=0, keepdims=True)

    return _call(body, name, (_sds((s, d), F32), _sds((1, d), F32), _sds((1, d), F32)), (s // tm,),
                 [_rows(tm, d), _rows(tm, 2 * LANES), _rows(tm, d), _rows(tm, d), _rows(tm, d), _rows(tm, d),
                  _resident(w_in.shape), _rows(tm, d), _vec(d), _vec(d), _rows(tm, d)],
                 (_rows(tm, d), _vec(d), _vec(d)), ("arbitrary",))(dq, dkv, du, dv, dga, dgb, w_in, x, w, sc, dx_in)


def _rope_bwd(name, dqr, dkv_cur, dkv_prev, cosf, sinf, tm=512):
    s = dqr.shape[0]
    tm = min(tm, s)
    steps = s // tm
    per = tm // ATTN_BLOCK
    nb = s // ATTN_BLOCK

    def unrope(v, cosv, sinv):
        return v * cosv - _rope_partner(v) * sinv

    def body(dq_ref, cur_ref, prev_ref, next_ref, cos_ref, sin_ref, dqo_ref, dkvo_ref):
        i = pl.program_id(0)
        cosv, sinv = cos_ref[...], sin_ref[...]
        for j in range(D_MODEL // LANES):
            dqo_ref[:, j * LANES:(j + 1) * LANES] = unrope(dq_ref[:, j * LANES:(j + 1) * LANES], cosv, sinv).astype(BF)
        nxt = jnp.where(i < steps - 1, next_ref[...], 0.0)
        if per > 1:
            shifted = jnp.concatenate([prev_ref[ATTN_BLOCK:, :], nxt], axis=0)
        else:
            shifted = nxt
        tot = cur_ref[...] + shifted
        dkvo_ref[:, :LANES] = unrope(tot[:, :LANES], cosv, sinv).astype(BF)
        dkvo_ref[:, LANES:] = tot[:, LANES:].astype(BF)

    nxt_spec = pl.BlockSpec((ATTN_BLOCK, 2 * LANES), lambda i: (jnp.minimum((i + 1) * per, nb - 1), 0))
    return _call(body, name, (_sds((s, D_MODEL), BF), _sds((s, 2 * LANES), BF)), (steps,),
                 [_rows(tm, D_MODEL), _rows(tm, 2 * LANES), _rows(tm, 2 * LANES), nxt_spec, _rows(tm, LANES),
                  _rows(tm, LANES)],
                 (_rows(tm, D_MODEL), _rows(tm, 2 * LANES)), ("parallel",))(dqr, dkv_cur, dkv_prev, dkv_prev, cosf, sinf)


Q_PER_KV = N_Q_HEADS // N_KV_HEADS
ATTN_SCALE = HEAD_DIM ** -0.5
HEADS_AHEAD_FWD = 2
HEADS_AHEAD_BWD = 3


def _band_mask_t(n):
    kj = lax.broadcasted_iota(jnp.int32, (2 * ATTN_BLOCK, ATTN_BLOCK), 0)
    qi = lax.broadcasted_iota(jnp.int32, (2 * ATTN_BLOCK, ATTN_BLOCK), 1)
    return (kj > qi) & (kj <= qi + ATTN_BLOCK) & ((n > 0) | (kj >= ATTN_BLOCK))


def _softmax_t(raw, allowed, sink):
    sc = jnp.where(allowed, raw, NEG_BIG)
    m = jnp.maximum(jnp.max(sc, axis=0, keepdims=True), sink)
    p = jnp.exp(sc - m)
    esink = jnp.exp(sink - m)
    inv = 1.0 / (jnp.sum(p, axis=0, keepdims=True) + esink)
    return p * inv, esink * inv


def _kv_specs():
    cur = pl.BlockSpec((ATTN_BLOCK, LANES), lambda n: (n, 0))
    prev = pl.BlockSpec((ATTN_BLOCK, LANES), lambda n: (jnp.maximum(n - 1, 0), 0))
    return [prev, cur] * 4


def _attention(name, qr, kk0, kk1, vv0, vv1, sinks):
    s = qr.shape[0]
    nb = s // ATTN_BLOCK

    def body(sink_ref, q_ref, k0p, k0c, k1p, k1c, v0p, v0c, v1p, v1c, y_ref):
        allowed = _band_mask_t(pl.program_id(0))
        upper = lax.broadcasted_iota(jnp.int32, (1, LANES), 1) >= HEAD_DIM
        upper_rows = lax.broadcasted_iota(jnp.int32, (LANES, 1), 0) >= HEAD_DIM
        bands = ((jnp.concatenate([k0p[...], k0c[...]], axis=0), jnp.concatenate([v0p[...], v0c[...]], axis=0)),
                 (jnp.concatenate([k1p[...], k1c[...]], axis=0), jnp.concatenate([v1p[...], v1c[...]], axis=0)))
        vbts = (bands[0][1].T, bands[1][1].T)

        def scores(h):
            hk, j, half = h // Q_PER_KV, (h % Q_PER_KV) // 2, h % 2
            col = (hk * 4 + j) * LANES
            qp = q_ref[:, col:col + LANES]
            return _dot_nt(bands[hk][0], jnp.where(upper if half else jnp.logical_not(upper), qp, jnp.zeros_like(qp)))

        out_t = None
        ahead = [scores(h) for h in range(HEADS_AHEAD_FWD)]
        for h in range(N_Q_HEADS):
            hk, j, half = h // Q_PER_KV, (h % Q_PER_KV) // 2, h % 2
            raw = ahead.pop(0)
            if h + HEADS_AHEAD_FWD < N_Q_HEADS:
                ahead.append(scores(h + HEADS_AHEAD_FWD))
            pn, _ = _softmax_t(raw, allowed, sink_ref[h])
            o_h = _dot(vbts[hk], pn.astype(BF))
            out_t = jnp.where(upper_rows, o_h, out_t) if half else o_h
            if half:
                col = (hk * 4 + j) * LANES
                y_ref[:, col:col + LANES] = out_t.T.astype(BF)

    return _call(body, name, _sds((s, D_MODEL), BF), (nb,),
                 [pl.BlockSpec(memory_space=pltpu.SMEM), pl.BlockSpec((ATTN_BLOCK, D_MODEL), lambda n: (n, 0))] + _kv_specs(),
                 pl.BlockSpec((ATTN_BLOCK, D_MODEL), lambda n: (n, 0)), ("parallel",))(
                     sinks, qr, kk0, kk0, kk1, kk1, vv0, vv0, vv1, vv1)


def _attention_bwd(name, qr, kk0, kk1, vv0, vv1, sinks, dy, after=None):
    s = qr.shape[0]
    nb = s // ATTN_BLOCK

    def body(sink_ref, q_ref, dy_ref, k0p, k0c, k1p, k1c, v0p, v0c, v1p, v1c, dq_ref, cur_ref, prev_ref, dsink_ref):
        @pl.when(pl.program_id(0) == 0)
        def _():
            dsink_ref[...] = jnp.zeros_like(dsink_ref)

        allowed = _band_mask_t(pl.program_id(0))
        lane = lax.broadcasted_iota(jnp.int32, (1, LANES), 1)
        upper = lane >= HEAD_DIM
        upper_rows = lax.broadcasted_iota(jnp.int32, (LANES, 1), 0) >= HEAD_DIM
        bands = ((jnp.concatenate([k0p[...], k0c[...]], axis=0), jnp.concatenate([v0p[...], v0c[...]], axis=0)),
                 (jnp.concatenate([k1p[...], k1c[...]], axis=0), jnp.concatenate([v1p[...], v1c[...]], axis=0)))
        kbts = (bands[0][0].T, bands[1][0].T)

        def scores(h):
            hk, j, half = h // Q_PER_KV, (h % Q_PER_KV) // 2, h % 2
            kb, vb = bands[hk]
            col = (hk * 4 + j) * LANES
            sel = upper if half else jnp.logical_not(upper)
            qp = q_ref[:, col:col + LANES]
            qa = jnp.where(sel, qp, jnp.zeros_like(qp))
            dya = jnp.where(sel, dy_ref[:, col:col + LANES], 0.0).astype(BF)
            return qa, dya, _dot_nt(kb, qa), _dot_nt(vb, dya)

        dsink = jnp.zeros((1, LANES), F32)
        dk_slab = jnp.zeros((2 * ATTN_BLOCK, LANES), F32)
        dv_slab = jnp.zeros((2 * ATTN_BLOCK, LANES), F32)
        dkk = dvv = dq_t = None
        ahead = [scores(h) for h in range(HEADS_AHEAD_BWD)]
        for h in range(N_Q_HEADS):
            hk, j, half = h // Q_PER_KV, (h % Q_PER_KV) // 2, h % 2
            qa, dya, raw, dp = ahead.pop(0)
            if h + HEADS_AHEAD_BWD < N_Q_HEADS:
                ahead.append(scores(h + HEADS_AHEAD_BWD))
            pn, psink = _softmax_t(raw, allowed, sink_ref[h])
            delta = jnp.sum(pn * dp, axis=0, keepdims=True)
            ds = (pn * (dp - delta)).astype(BF)
            dsink = dsink + jnp.where(lane == h, -jnp.sum(psink * delta), 0.0)
            dq_h = _dot(kbts[hk], ds) * ATTN_SCALE
            dq_t = jnp.where(upper_rows, dq_h, dq_t) if half else dq_h
            dk_h, dv_h = _dot(ds, qa), _dot(pn.astype(BF), dya)
            dkk, dvv = (dk_h, dv_h) if h % Q_PER_KV == 0 else (dkk + dk_h, dvv + dv_h)
            if half:
                col = (hk * 4 + j) * LANES
                dq_ref[:, col:col + LANES] = dq_t.T
            if h % Q_PER_KV == Q_PER_KV - 1:
                mine = upper if hk else jnp.logical_not(upper)
                dk_slab = jnp.where(mine, dkk + pltpu.roll(dkk, HEAD_DIM, axis=1), dk_slab)
                dv_slab = jnp.where(mine, dvv + pltpu.roll(dvv, HEAD_DIM, axis=1), dv_slab)
        prev_ref[:, :LANES] = dk_slab[:ATTN_BLOCK]
        prev_ref[:, LANES:] = dv_slab[:ATTN_BLOCK]
        cur_ref[:, :LANES] = dk_slab[ATTN_BLOCK:]
        cur_ref[:, LANES:] = dv_slab[ATTN_BLOCK:]
        dsink_ref[...] += dsink

    blk = pl.BlockSpec((ATTN_BLOCK, D_MODEL), lambda n: (n, 0))
    kvo = pl.BlockSpec((ATTN_BLOCK, 2 * LANES), lambda n: (n, 0))
    return _call(body, name,
                 (_sds((s, D_MODEL), F32), _sds((s, 2 * LANES), F32), _sds((s, 2 * LANES), F32), _sds((1, LANES), F32)),
                 (nb,), [pl.BlockSpec(memory_space=pltpu.SMEM), blk, blk] + _kv_specs(),
                 (blk, kvo, kvo, pl.BlockSpec((1, LANES), lambda n: (0, 0))), ("arbitrary",), after=after)(
                     sinks, qr, dy, kk0, kk0, kk1, kk1, vv0, vv0, vv1, vv1)


def _sgu_weights(wm_ref, g):
    t = lax.broadcasted_iota(jnp.int32, (SGU_CHUNK, SGU_CHUNK), 0)
    sidx = lax.broadcasted_iota(jnp.int32, (SGU_CHUNK, SGU_CHUNK), 1)
    return jnp.where(sidx <= t, wm_ref[g], 0.0).astype(BF)


def _layer_norm_stats(v):
    mu = jnp.mean(v, axis=-1, keepdims=True)
    cen = v - mu
    rstd = lax.rsqrt(jnp.mean(cen * cen, axis=-1, keepdims=True) + NORM_EPS)
    return cen * rstd, rstd


def _sgu(name, u_pre, v_pre, ln_w, ln_b, wm, bfull, tm=256):
    s, w = u_pre.shape
    tm = min(tm, s)

    def body(u_ref, v_ref, lw_ref, lb_ref, wm_ref, b_ref, y_ref):
        vhat, _ = _layer_norm_stats(_gelu(v_ref[...]))
        vn = (vhat * lw_ref[...] + lb_ref[...]).astype(BF)
        for g in range(SGU_GROUPS):
            wg = _sgu_weights(wm_ref, g)
            cols = slice(g * SGU_CHUNK, (g + 1) * SGU_CHUNK)
            for ch in range(tm // SGU_CHUNK):
                rows = slice(ch * SGU_CHUNK, (ch + 1) * SGU_CHUNK)
                f = _dot(wg, vn[rows, cols]) + b_ref[g]
                y_ref[rows, cols] = (_gelu(u_ref[rows, cols]) * f).astype(BF)

    full3 = pl.BlockSpec((SGU_GROUPS, SGU_CHUNK, SGU_CHUNK), lambda i: (0, 0, 0))
    return _call(body, name, _sds((s, w), BF), (s // tm,),
                 [_rows(tm, w), _rows(tm, w), _vec(w), _vec(w), full3, full3],
                 _rows(tm, w), ("parallel",))(u_pre, v_pre, ln_w, ln_b, wm, bfull)


def _sgu_bwd(name, u_pre, v_pre, ln_w, ln_b, wm, bfull, dy, tm=256, after=None):
    s, w = u_pre.shape
    tm = min(tm, s)
    steps = s // tm

    def body(u_ref, v_ref, lw_ref, lb_ref, wm_ref, b_ref, dy_ref, du_ref, dv_ref, dwm_ref, db_ref, dlw_ref, dlb_ref,
             dfsum_ref):
        i = pl.program_id(0)

        @pl.when(i == 0)
        def _():
            dwm_ref[...] = jnp.zeros_like(dwm_ref)
            dlw_ref[...] = jnp.zeros_like(dlw_ref)
            dlb_ref[...] = jnp.zeros_like(dlb_ref)
            dfsum_ref[...] = jnp.zeros_like(dfsum_ref)

        vpre = v_ref[...]
        vg, dvg_dv = _gelu_and_grad(vpre)
        vhat, rstd = _layer_norm_stats(vg)
        vn = (vhat * lw_ref[...] + lb_ref[...]).astype(BF)
        t = lax.broadcasted_iota(jnp.int32, (SGU_CHUNK, SGU_CHUNK), 0)
        sidx = lax.broadcasted_iota(jnp.int32, (SGU_CHUNK, SGU_CHUNK), 1)
        dvn_cols = []
        for g in range(SGU_GROUPS):
            wg = _sgu_weights(wm_ref, g)
            cols = slice(g * SGU_CHUNK, (g + 1) * SGU_CHUNK)
            dvn_rows = []
            dwg = jnp.zeros((SGU_CHUNK, SGU_CHUNK), F32)
            dfs = jnp.zeros((SGU_CHUNK, SGU_CHUNK), F32)
            for ch in range(tm // SGU_CHUNK):
                rows = slice(ch * SGU_CHUNK, (ch + 1) * SGU_CHUNK)
                upre = u_ref[rows, cols]
                dyv = dy_ref[rows, cols].astype(F32)
                f = _dot(wg, vn[rows, cols]) + b_ref[g]
                ug, dug_du = _gelu_and_grad(upre)
                du_ref[rows, cols] = (dyv * f * dug_du).astype(BF)
                df = dyv * ug
                dfb = df.astype(BF)
                dvn_rows.append(_dot_tn(wg, dfb))
                dwg = dwg + _dot_nt(dfb, vn[rows, cols])
                dfs = dfs + df
            dwm_ref[g] += jnp.where(sidx <= t, dwg, 0.0)
            dfsum_ref[g] += dfs
            dvn_cols.append(jnp.concatenate(dvn_rows, axis=0) if len(dvn_rows) > 1 else dvn_rows[0])
        dvn = jnp.concatenate(dvn_cols, axis=1)
        dlw_ref[...] += jnp.sum(dvn * vhat, axis=0, keepdims=True)
        dlb_ref[...] += jnp.sum(dvn, axis=0, keepdims=True)
        dvh = dvn * lw_ref[...]
        dvg = rstd * (dvh - jnp.mean(dvh, axis=-1, keepdims=True) - vhat * jnp.mean(dvh * vhat, axis=-1, keepdims=True))
        dv_ref[...] = (dvg * dvg_dv).astype(BF)

        @pl.when(i == steps - 1)
        def _():
            for g in range(SGU_GROUPS):
                db_ref[g:g + 1, :] = jnp.sum(dfsum_ref[g].T, axis=0, keepdims=True)

    full3 = pl.BlockSpec((SGU_GROUPS, SGU_CHUNK, SGU_CHUNK), lambda i: (0, 0, 0))
    return _call(body, name,
                 (_sds((s, w), BF), _sds((s, w), BF), _sds((SGU_GROUPS, SGU_CHUNK, SGU_CHUNK), F32),
                  _sds((SGU_GROUPS, SGU_CHUNK), F32), _sds((1, w), F32), _sds((1, w), F32)),
                 (steps,),
                 [_rows(tm, w), _rows(tm, w), _vec(w), _vec(w), full3, full3, _rows(tm, w)],
                 (_rows(tm, w), _rows(tm, w), full3, pl.BlockSpec((SGU_GROUPS, SGU_CHUNK), lambda i: (0, 0)), _vec(w), _vec(w)),
                 ("arbitrary",), scratch=[pltpu.VMEM((SGU_GROUPS, SGU_CHUNK, SGU_CHUNK), F32)], after=after)(
                     u_pre, v_pre, ln_w, ln_b, wm, bfull, dy)


def _mix_out(name, y_sgu, y_attn, ga_pre, gb_pre, x, g1, proj_a, proj_b, w_out, w2, sc2, sh2, tm=512):
    s, d = x.shape
    tm = min(tm, s)

    def body(ys_ref, ya_ref, ga_ref, gb_ref, x_ref, g1_ref, wa_ref, wb_ref, wo_ref, w2_ref, sc2_ref, sh2_ref,
             m_ref, pa_ref, pb_ref, o_ref, x1_ref, h2_ref):
        pa = _dot(ys_ref[...], wa_ref[...].reshape(d, d))
        pb = _dot(ya_ref[...], wb_ref[...].reshape(d, d))
        pa_ref[...] = pa.astype(BF)
        pb_ref[...] = pb.astype(BF)
        merged = (_sigmoid(ga_ref[...].astype(F32)) * pa + _sigmoid(gb_ref[...].astype(F32)) * pb).astype(BF)
        m_ref[...] = merged
        o = _dot(merged, wo_ref[...].reshape(d, d))
        o_ref[...] = o.astype(BF)
        x1 = x_ref[...] + g1_ref[...] * o
        x1_ref[...] = x1
        h2_ref[...] = ((x1 * _rms(x1)) * w2_ref[...] * (1.0 + sc2_ref[...]) + sh2_ref[...]).astype(BF)

    f, b = _sds((s, d), F32), _sds((s, d), BF)
    r = _rows(tm, d)
    wspec = _resident(proj_a.shape)
    return _call(body, name, (b, b, b, b, f, b), (s // tm,),
                 [r, r, r, r, r, _vec(d), wspec, wspec, wspec, _vec(d), _vec(d), _vec(d)], (r,) * 6, ("parallel",))(
                     y_sgu, y_attn, ga_pre, gb_pre, x, g1, proj_a, proj_b, w_out, w2, sc2, sh2)


def _mix_bwd(name, do, w_out, proj_a, proj_b, ga_pre, gb_pre, pa, pb, tm=512):
    s, d = do.shape
    tm = min(tm, s)

    def body(do_ref, wo_ref, wa_ref, wb_ref, ga_ref, gb_ref, pa_ref, pb_ref,
             dpa_ref, dpb_ref, dga_ref, dgb_ref, dys_ref, dya_ref):
        dm = _dot_nt(do_ref[...], wo_ref[...].reshape(d, d))
        ga = _sigmoid(ga_ref[...].astype(F32))
        gb = _sigmoid(gb_ref[...].astype(F32))
        dpa = (dm * ga).astype(BF)
        dpb = (dm * gb).astype(BF)
        dpa_ref[...] = dpa
        dpb_ref[...] = dpb
        dga_ref[...] = (dm * pa_ref[...].astype(F32) * ga * (1.0 - ga)).astype(BF)
        dgb_ref[...] = (dm * pb_ref[...].astype(F32) * gb * (1.0 - gb)).astype(BF)
        dys_ref[...] = _dot_nt(dpa, wa_ref[...].reshape(d, d)).astype(BF)
        dya_ref[...] = _dot_nt(dpb, wb_ref[...].reshape(d, d)).astype(BF)

    f, b = _sds((s, d), F32), _sds((s, d), BF)
    r = _rows(tm, d)
    wspec = _resident(w_out.shape)
    return _call(body, name, (b, b, b, b, b, b), (s // tm,), [r, wspec, wspec, wspec, r, r, r, r], (r,) * 6,
                 ("parallel",))(do, w_out, proj_a, proj_b, ga_pre, gb_pre, pa, pb)


def _ffn_up_act(name, h2, w_gate, w_up, cw, cb, tm=1024):
    s, d = h2.shape
    tm = min(tm, s)
    tc = FFN_SHARD_PAD
    per = tm // HALO

    def body(h_ref, hprev_ref, wg_ref, wu_ref, cw_ref, cb_ref, a_ref, ac_ref, up_ref, hf_ref):
        hv = h_ref[...]
        a = _dot_nt(hv, wg_ref[0]).astype(BF)
        up = _dot_nt(hv, wu_ref[0]).astype(BF)
        a_ref[...] = a
        up_ref[...] = up
        prev = jnp.where(pl.program_id(1) > 0, _dot_nt(hprev_ref[...], wg_ref[0]).astype(BF).astype(F32), 0.0)
        ext = jnp.concatenate([prev, a.astype(F32)], axis=0)
        ac = (cb_ref[...] + cw_ref[0:1, :] * pltpu.roll(ext, 2, axis=0) + cw_ref[1:2, :] * pltpu.roll(ext, 1, axis=0)
              + cw_ref[2:3, :] * ext)[HALO:]
        ac_ref[...] = ac.astype(BF)
        hf_ref[...] = (ac * _sigmoid(ac) * up.astype(F32)).astype(BF)

    wspec = pl.BlockSpec((1, tc, d), lambda j, i: (j, 0, 0))
    ospec = pl.BlockSpec((tm, tc), lambda j, i: (i, j))
    o = _sds((s, FFN_PAD), BF)
    return _call(body, name, (o, o, o, o), (N_CHIPS, s // tm),
                 [pl.BlockSpec((tm, d), lambda j, i: (i, 0)), pl.BlockSpec((HALO, d), lambda j, i: (jnp.maximum(i * per - 1, 0), 0)),
                  wspec, wspec, pl.BlockSpec((3, tc), lambda j, i: (0, j)), pl.BlockSpec((1, tc), lambda j, i: (0, j))],
                 (ospec, ospec, ospec, ospec), ("parallel", "parallel"))(h2, h2, w_gate, w_up, cw, cb)


def _ffn_down(name, hf, w_down, x1, g2, tm=512):
    s, d = x1.shape
    tm = min(tm, s)

    def body(hf_ref, wd_ref, x1_ref, g2_ref, dn_ref, x2_ref):
        dn = _dot(hf_ref[...], wd_ref[...].reshape(FFN_PAD, d))
        dn_ref[...] = dn.astype(BF)
        x2_ref[...] = x1_ref[...] + g2_ref[...] * dn

    return _call(body, name, (_sds((s, d), BF), _sds((s, d), F32)), (s // tm,),
                 [_rows(tm, FFN_PAD), _resident(w_down.shape), _rows(tm, d), _vec(d)],
                 (_rows(tm, d), _rows(tm, d)), ("parallel",))(hf, w_down, x1, g2)


def _ffn_down_bwd_act(name, dx2, dn, g2, w_down, a, ac, up, cw, tm=256, after=None):
    s, d = dx2.shape
    c = a.shape[1]
    tm = min(tm, s)
    tc = FFN_SHARD_PAD
    per = tm // HALO
    steps = s // tm
    last = s // HALO - 1
    n = tm + HALO

    def body(dx_ref, dxnext_ref, dn_ref, g2_ref, wd_ref, a_ref, ac_ref, acnext_ref, up_ref, upnext_ref, cw_ref,
             ddn_ref, da_ref, dup_ref, dg_ref, dcw_ref, dcb_ref):
        i = pl.program_id(0)

        @pl.when(i == 0)
        def _():
            dg_ref[...] = jnp.zeros_like(dg_ref)
            dcw_ref[...] = jnp.zeros_like(dcw_ref)
            dcb_ref[...] = jnp.zeros_like(dcb_ref)

        dxv = dx_ref[...]
        ddn = (dxv * g2_ref[...]).astype(BF)
        ddn_ref[...] = ddn
        dg_ref[...] += jnp.sum(dxv * dn_ref[...].astype(F32), axis=0, keepdims=True)
        ddn_next = jnp.where(i < steps - 1, dxnext_ref[...] * g2_ref[...], 0.0).astype(BF)
        ddn_ext = jnp.concatenate([ddn, ddn_next], axis=0)
        for k in range(N_CHIPS):
            cols = slice(k * tc, (k + 1) * tc)
            dh = _dot_nt(ddn_ext, wd_ref[k])
            ace = jnp.concatenate([ac_ref[:, cols].astype(F32), acnext_ref[:, cols].astype(F32)], axis=0)
            upe = jnp.concatenate([up_ref[:, cols].astype(F32), upnext_ref[:, cols].astype(F32)], axis=0)
            sig = _sigmoid(ace)
            silu = ace * sig
            dac = dh * upe * (sig + silu * (1.0 - sig))
            dup_ref[:, cols] = (dh[:tm] * silu[:tm]).astype(BF)
            d1 = pltpu.roll(dac, n - 1, axis=0)[:tm]
            d2 = pltpu.roll(dac, n - 2, axis=0)[:tm]
            d0 = dac[:tm]
            da_ref[:, cols] = (cw_ref[2:3, cols] * d0 + cw_ref[1:2, cols] * d1 + cw_ref[0:1, cols] * d2).astype(BF)
            a0 = a_ref[:, cols].astype(F32)
            dcb_ref[:, cols] += jnp.sum(d0, axis=0, keepdims=True)
            dcw_ref[0:1, cols] += jnp.sum(d2 * a0, axis=0, keepdims=True)
            dcw_ref[1:2, cols] += jnp.sum(d1 * a0, axis=0, keepdims=True)
            dcw_ref[2:3, cols] += jnp.sum(d0 * a0, axis=0, keepdims=True)

    nxt = lambda width: pl.BlockSpec((HALO, width), lambda i: (jnp.minimum((i + 1) * per, last), 0))
    wide = _sds((s, c), BF)
    return _call(body, name, (_sds((s, d), BF), wide, wide, _sds((1, d), F32), _sds((3, c), F32), _sds((1, c), F32)), (steps,),
                 [_rows(tm, d), nxt(d), _rows(tm, d), _vec(d), _resident(w_down.shape), _rows(tm, c), _rows(tm, c), nxt(c),
                  _rows(tm, c), nxt(c), pl.BlockSpec((3, c), lambda i: (0, 0))],
                 (_rows(tm, d), _rows(tm, c), _rows(tm, c), _vec(d), pl.BlockSpec((3, c), lambda i: (0, 0)), _vec(c)),
                 ("arbitrary",), after=after)(dx2, dx2, dn, g2, w_down, a, ac, ac, up, up, cw)


def _ffn_up_bwd(name, da, dup, w_gate, w_up, x1, dx2, w2, sc2, o, g1, tm=512, after=None):
    s, d = x1.shape
    tm = min(tm, s)
    tc = FFN_SHARD_PAD

    def body(da_ref, dup_ref, wg_ref, wu_ref, x1_ref, dx2_ref, w2_ref, sc2_ref, o_ref, g1_ref,
             dx1_ref, do_ref, dnw_ref, dsh_ref, dg1_ref):
        @pl.when(pl.program_id(0) == 0)
        def _():
            dnw_ref[...] = jnp.zeros_like(dnw_ref)
            dsh_ref[...] = jnp.zeros_like(dsh_ref)
            dg1_ref[...] = jnp.zeros_like(dg1_ref)

        dh = jnp.zeros((tm, d), F32)
        for k in range(N_CHIPS):
            cols = slice(k * tc, (k + 1) * tc)
            dh = dh + _dot(da_ref[:, cols], wg_ref[k]) + _dot(dup_ref[:, cols], wu_ref[k])
        xv = x1_ref[...]
        r = _rms(xv)
        xn = xv * r
        dxn = dh * (w2_ref[...] * (1.0 + sc2_ref[...]))
        dx1 = dx2_ref[...] + r * (dxn - xn * jnp.mean(dxn * xn, axis=-1, keepdims=True))
        dx1_ref[...] = dx1
        dnw_ref[...] += jnp.sum(dh * xn, axis=0, keepdims=True)
        dsh_ref[...] += jnp.sum(dh, axis=0, keepdims=True)
        do_ref[...] = (dx1 * g1_ref[...]).astype(BF)
        dg1_ref[...] += jnp.sum(dx1 * o_ref[...].astype(F32), axis=0, keepdims=True)

    v = _sds((1, d), F32)
    r = _rows(tm, d)
    wspec = _resident(w_gate.shape)
    return _call(body, name, (_sds((s, d), F32), _sds((s, d), BF), v, v, v), (s // tm,),
                 [_rows(tm, FFN_PAD), _rows(tm, FFN_PAD), wspec, wspec, r, r, _vec(d), _vec(d), r, _vec(d)],
                 (r, r, _vec(d), _vec(d), _vec(d)), ("arbitrary",), after=after)(da, dup, w_gate, w_up, x1, dx2, w2, sc2, o, g1)


def _loss_head(name, x, w, target, tm=512):
    s, d = x.shape
    tm = min(tm, s)

    def body(x_ref, w_ref, t_ref, dx_ref, loss_ref, dw_ref):
        @pl.when(pl.program_id(0) == 0)
        def _():
            loss_ref[...] = jnp.zeros_like(loss_ref)
            dw_ref[...] = jnp.zeros_like(dw_ref)

        xv = x_ref[...]
        r = _rms(xv)
        xn = xv * r
        err = xn * w_ref[...] - t_ref[...]
        loss_ref[...] += 0.5 * jnp.sum(jnp.mean(err * err, axis=-1, keepdims=True))
        dy = err * (1.0 / d)
        dw_ref[...] += jnp.sum(dy * xn, axis=0, keepdims=True)
        dxn = dy * w_ref[...]
        dx_ref[...] = r * (dxn - xn * jnp.mean(dxn * xn, axis=-1, keepdims=True))

    return _call(body, name, (_sds((s, d), F32), _sds((1, LANES), F32), _sds((1, d), F32)), (s // tm,),
                 [_rows(tm, d), _vec(d), _rows(tm, d)], (_rows(tm, d), _vec(LANES), _vec(d)), ("arbitrary",))(x, w, target)


def _layer_fwd(l, x, mod, p, cosf, sinf, after=None, late=None, later=None):
    sh1, sc1, g1, sh2, sc2, g2 = mod
    tag = f"l{l}_"
    h, qr, kk0, kk1, vv0, vv1, u_pre, v_pre, ga_pre, gb_pre = _in_proj(
        tag + "in_proj", x, p["norm1_w"], sc1, sh1, p["w_in"], cosf, sinf, after=after)
    y_attn = _attention(tag + "attn", qr, kk0, kk1, vv0, vv1, p["sinks"])
    y_sgu = _sgu(tag + "sgu", u_pre, v_pre, p["sgu_ln_w"], p["sgu_ln_b"], p["sgu_w"], p["sgu_bfull"])
    if late is not None:
        p = dict(p, **late(y_sgu))
    merged, pa, pb, o, x1, h2 = _mix_out(tag + "mix_out", y_sgu, y_attn, ga_pre, gb_pre, x, g1, p["proj_a"], p["proj_b"],
                                         p["w_out"], p["norm2_w"], sc2, sh2)
    if later is not None:
        p = dict(p, **later(h2))
    a, ac, up, hf = _ffn_up_act(tag + "ffn_up", h2, p["w_gate"], p["w_up"], p["conv_w"], p["conv_b"])
    dn, x2 = _ffn_down(tag + "ffn_down", hf, p["w_down"], x1, g2)
    saved = dict(x=x, h=h, qr=qr, kk0=kk0, kk1=kk1, vv0=vv0, vv1=vv1, u_pre=u_pre, v_pre=v_pre, ga_pre=ga_pre,
                 gb_pre=gb_pre, y_attn=y_attn, y_sgu=y_sgu, merged=merged, pa=pa, pb=pb, o=o, x1=x1, h2=h2, a=a, ac=ac, up=up,
                 hf=hf, dn=dn)
    return x2, saved, p


def _layer_bwd(l, dx2, mod, p, sv, cosf, sinf, after=None, emit=None, tick=None):
    sh1, sc1, g1, sh2, sc2, g2 = mod
    tag = f"l{l}_b_"
    d = D_MODEL
    g = {}
    ready = (lambda names: emit({k: g.pop(k) for k in names})) if emit else (lambda names: None)
    tick = tick or (lambda y: None)
    ddn, da, dup, dg2, g["conv_w"], g["conv_b"] = _ffn_down_bwd_act(
        tag + "ffn_down", dx2, sv["dn"], g2, p["w_down"], sv["a"], sv["ac"], sv["up"], p["conv_w"], after=after)
    g["w_down"] = _matmul_tn(tag + "dw_down", sv["hf"], ddn, tk=FFN_SHARD_PAD, after=tick(ddn)).reshape(N_CHIPS, FFN_SHARD_PAD, d)
    g["w_gate"] = _matmul_tn(tag + "dw_gate", da, sv["h2"], tk=FFN_SHARD_PAD).reshape(N_CHIPS, FFN_SHARD_PAD, d)
    g["w_up"] = _matmul_tn(tag + "dw_up", dup, sv["h2"], tk=FFN_SHARD_PAD).reshape(N_CHIPS, FFN_SHARD_PAD, d)
    dx1, do, da2, dsh2, dg1 = _ffn_up_bwd(tag + "ffn_up", da, dup, p["w_gate"], p["w_up"], sv["x1"], dx2, p["norm2_w"],
                                          sc2, sv["o"], g1, after=ready(("w_down", "w_gate", "w_up")))
    g["norm2_w"] = da2 * (1.0 + sc2)
    dsc2 = da2 * p["norm2_w"]
    g["w_out"] = _matmul_tn(tag + "dw_out", sv["merged"], do, after=tick(do)).reshape(N_CHIPS, d // N_CHIPS, d)
    dpa, dpb, dga, dgb, dy_sgu, dy_attn = _mix_bwd(tag + "mix", do, p["w_out"], p["proj_a"], p["proj_b"], sv["ga_pre"],
                                                  sv["gb_pre"], sv["pa"], sv["pb"])
    g["proj_a"] = _matmul_tn(tag + "dproj_a", sv["y_sgu"], dpa).reshape(N_CHIPS, d // N_CHIPS, d)
    g["proj_b"] = _matmul_tn(tag + "dproj_b", sv["y_attn"], dpb).reshape(N_CHIPS, d // N_CHIPS, d)
    du, dv, g["sgu_w"], g["sgu_b"], g["sgu_ln_w"], g["sgu_ln_b"] = _sgu_bwd(
        tag + "sgu", sv["u_pre"], sv["v_pre"], p["sgu_ln_w"], p["sgu_ln_b"], p["sgu_w"], p["sgu_bfull"], dy_sgu,
        after=ready(("w_out", "proj_a", "proj_b")))
    dqr, dkv_cur, dkv_prev, dsink = _attention_bwd(tag + "attn", sv["qr"], sv["kk0"], sv["kk1"], sv["vv0"], sv["vv1"],
                                                   p["sinks"], dy_attn, after=tick(du))
    g["sinks"] = dsink[0, :N_Q_HEADS]
    dq, dkv = _rope_bwd(tag + "rope", dqr, dkv_cur, dkv_prev, cosf, sinf)
    dw_in, row0 = None, 0
    for n, t in (("q", dq), ("kv", dkv), ("u", du), ("v", dv), ("ga", dga), ("gb", dgb)):
        dw_in = _matmul_tn_rows(tag + "dw_in_" + n, t, sv["h"], dw_in, row0, IN_COLS)
        row0 += t.shape[1]
    g["w_in"] = dw_in.reshape(N_CHIPS, IN_COLS // N_CHIPS, d)
    dx, da1, dsh1 = _in_proj_bwd(tag + "in_proj", dq, dkv, du, dv, dga, dgb, p["w_in"], sv["x"], p["norm1_w"], sc1, dx1)
    g["norm1_w"] = da1 * (1.0 + sc1)
    dsc1 = da1 * p["norm1_w"]
    return dx, (dsh1, dsc1, dg1, dsh2, dsc2, dg2), g


def _pad_to(a, axis, size):
    pad = [(0, 0)] * a.ndim
    pad[axis] = (0, size - a.shape[axis])
    return jnp.pad(a, pad)


def _early_params(w_in, small):
    d = D_MODEL
    return dict(
        w_in=w_in.reshape(IN_COLS, d), norm1_w=small["norm1_w"].reshape(1, d), sinks=small["sinks"],
        sgu_ln_w=small["sgu_ln_w"].reshape(1, d), sgu_ln_b=small["sgu_ln_b"].reshape(1, d), sgu_w=small["sgu_w"],
        sgu_bfull=jnp.broadcast_to(small["sgu_b"][:, :, None], (SGU_GROUPS, SGU_CHUNK, SGU_CHUNK)))


def _mix_params(proj_a, proj_b, w_out, small):
    return dict(proj_a=proj_a, proj_b=proj_b, w_out=w_out, norm2_w=small["norm2_w"].reshape(1, D_MODEL))


def _ffn_params(w_gate, w_up, w_down, conv_w, small):
    return dict(
        w_gate=w_gate, w_up=w_up, w_down=w_down, conv_w=conv_w.transpose(1, 0, 2).reshape(3, FFN_PAD),
        conv_b=_pad_to(small["conv_b"].reshape(N_CHIPS, FFN_SHARD), 1, FFN_SHARD_PAD).reshape(1, FFN_PAD))


def _layer_params(w_in, proj_a, proj_b, w_out, w_gate, w_up, w_down, conv_w, small):
    return dict(_early_params(w_in, small), **_mix_params(proj_a, proj_b, w_out, small),
                **_ffn_params(w_gate, w_up, w_down, conv_w, small))


def _conv_grads_natural(g):
    cw = g["conv_w"].reshape(3, N_CHIPS, FFN_SHARD_PAD)[:, :, :FFN_SHARD].reshape(3, FFN_DIM)
    cb = g["conv_b"].reshape(N_CHIPS, FFN_SHARD_PAD)[:, :FFN_SHARD].reshape(FFN_DIM)
    return cw, cb


def _rope_tables(positions):
    inv_freq = ROPE_THETA ** (-jnp.arange(0, ROT_DIM, 2, dtype=F32) / ROT_DIM)
    ang = positions.astype(F32)[:, None] * inv_freq
    cos, sin = jnp.cos(ang), jnp.sin(ang)
    s = positions.shape[0]
    rest = HEAD_DIM - ROT_DIM
    cos_head = jnp.concatenate([cos, cos, jnp.ones((s, rest), F32)], axis=1)
    sin_head = jnp.concatenate([-sin, sin, jnp.zeros((s, rest), F32)], axis=1)
    return jnp.tile(cos_head, (1, LANES // HEAD_DIM)), jnp.tile(sin_head, (1, LANES // HEAD_DIM))


ADA_ROWS = 16


def _ada_fwd(name, c_rows, ada_w, ada_b_cols, tn=512):
    depth, d, n = ada_w.shape

    def body(c_ref, w_ref, b_ref, o_ref):
        cv = c_ref[...]
        act = (cv * _sigmoid(cv)).astype(BF)
        o_ref[0] = _dot(act, w_ref[0].astype(BF)) + b_ref[0]

    return _call(body, name, _sds((depth, ADA_ROWS, n), F32), (depth, n // tn),
                 [pl.BlockSpec((ADA_ROWS, d), lambda l, j: (0, 0)), pl.BlockSpec((1, d, tn), lambda l, j: (l, 0, j)),
                  pl.BlockSpec((1, 1, tn), lambda l, j: (l, 0, j))],
                 pl.BlockSpec((1, ADA_ROWS, tn), lambda l, j: (l, 0, j)), ("parallel", "parallel"))(c_rows, ada_w, ada_b_cols)


def _ada_bwd(name, c_rows, dmod_cols, tn=512):
    depth, _, n = dmod_cols.shape
    d = c_rows.shape[1]

    def body(c_ref, dm_ref, o_ref):
        cv = c_ref[...]
        act = (cv * _sigmoid(cv)).astype(BF)
        o_ref[0] = _dot_tn(act, dm_ref[0].astype(BF))

    return _call(body, name, _sds((depth, d, n), F32), (depth, n // tn),
                 [pl.BlockSpec((ADA_ROWS, d), lambda l, j: (0, 0)), pl.BlockSpec((1, ADA_ROWS, tn), lambda l, j: (l, 0, j))],
                 pl.BlockSpec((1, d, tn), lambda l, j: (l, 0, j)), ("parallel", "parallel"))(c_rows, dmod_cols)


def _colsum(name, a):
    r, n = a.shape

    def body(a_ref, o_ref):
        o_ref[...] = jnp.sum(a_ref[...], axis=0, keepdims=True)

    return _call(body, name, _sds((1, n), F32), (1,), [pl.BlockSpec((r, n), lambda i: (0, 0))],
                 pl.BlockSpec((1, n), lambda i: (0, 0)), ("arbitrary",))(a)


REL_SIBLING = (0, 0, 1)
REL_CHIPS = ((1, 0, 0), (0, 1, 0), (1, 1, 0))
REL_ALL = tuple((fx, fy, fc) for fx in (0, 1) for fy in (0, 1) for fc in (0, 1) if fx or fy or fc)


def _chip_of(dev):
    return 2 * dev[0] + dev[1]


def _dev_of(dev):
    return 4 * dev[0] + 2 * dev[1] + dev[2]


def _flip(dev, rel):
    return tuple(1 - m if f else m for m, f in zip(dev, rel))


def _exchange(name, arrays, n_out, stages, aliases=None):
    out_shapes, stages = stages[0], stages[1:]
    n_in = len(arrays)
    aliases = aliases or {}
    n_remote = sum(len(plan) for plan, _ in stages)
    n_local = sum(len(local) for _, local in stages)

    def at(ref, idx):
        return ref.at[idx] if len(idx) else ref

    def body(*refs):
        bufs = list(refs[:n_in + n_out])
        for i_in, i_out in aliases.items():
            bufs[i_in] = bufs[n_in + i_out]
        send_sems, recv_sems, local_sems = refs[n_in + n_out:]
        me = (lax.axis_index("x"), lax.axis_index("y"), lax.axis_index("c"))
        base_r = base_l = 0
        pending = []
        for plan, local in stages:
            def remote(k, entry, sender, receiver):
                rel, si, ssel, di, dsel = entry
                return pltpu.make_async_remote_copy(
                    src_ref=at(bufs[si], ssel(sender, receiver)), dst_ref=at(bufs[di], dsel(sender, receiver)),
                    send_sem=send_sems.at[k], recv_sem=recv_sems.at[k], device_id=_flip(me, rel), device_id_type=MESH)

            sends = [remote(base_r + k, e, me, _flip(me, e[0])) for k, e in enumerate(plan)]
            for cp in sends:
                cp.start()
            for k, (si, ssel, di, dsel) in enumerate(local):
                cp = pltpu.make_async_copy(at(bufs[si], ssel(me)), at(bufs[di], dsel(me)), local_sems.at[base_l + k])
                cp.start()
                pending.append(cp.wait)
            for k, e in enumerate(plan):
                remote(base_r + k, e, _flip(me, e[0]), me).wait_recv()
            pending += [cp.wait_send for cp in sends]
            base_r += len(plan)
            base_l += len(local)
        for wait in pending:
            wait()

    any_spec = pl.BlockSpec(memory_space=pl.ANY)
    return pl.pallas_call(
        body, name=name, out_shape=tuple(out_shapes), in_specs=[any_spec] * n_in, out_specs=tuple([any_spec] * n_out),
        input_output_aliases=dict(aliases),
        scratch_shapes=[pltpu.SemaphoreType.DMA((max(n_remote, 1),)), pltpu.SemaphoreType.DMA((max(n_remote, 1),)),
                        pltpu.SemaphoreType.DMA((max(n_local, 1),))])(*arrays)


HBM_SPEC = pl.BlockSpec(memory_space=pltpu.HBM)
SEM_SPEC = pl.BlockSpec(memory_space=pltpu.SEMAPHORE)


def _split_copies(bufs, plan, local, send_sems, recv_sems, local_sems):
    me = (lax.axis_index("x"), lax.axis_index("y"), lax.axis_index("c"))

    def at(ref, idx):
        return ref.at[idx] if len(idx) else ref

    def remote(k, sender, receiver):
        rel, si, ssel, di, dsel = plan[k]
        return pltpu.make_async_remote_copy(
            src_ref=at(bufs[si], ssel(sender, receiver)), dst_ref=at(bufs[di], dsel(sender, receiver)),
            send_sem=send_sems.at[k], recv_sem=recv_sems.at[k], device_id=_flip(me, rel), device_id_type=MESH)

    sends = [remote(k, me, _flip(me, plan[k][0])) for k in range(len(plan))]
    arrivals = [remote(k, _flip(me, plan[k][0]), me) for k in range(len(plan))]
    locs = [pltpu.make_async_copy(at(bufs[si], ssel(me)), at(bufs[di], dsel(me)), local_sems.at[k])
            for k, (si, ssel, di, dsel) in enumerate(local)]
    return sends, arrivals, locs


def _exchange_start(name, arrays, out_shapes, plan, local, inplace=False):
    n_in, n_out = len(arrays), len(out_shapes)
    n_buf = n_in + n_out

    def body(*refs):
        sems = refs[n_buf:n_buf + 3]
        bufs = refs[n_buf + 3:2 * n_buf + 3]
        sends, _, locs = _split_copies(bufs * 2 if inplace else bufs, plan, local, *sems)
        for cp in sends + locs:
            cp.start()
        refs[-1][...] = jnp.zeros_like(refs[-1])

    zones = [lax.empty(o.shape, o.dtype) for o in out_shapes]
    operands = [pltpu.with_memory_space_constraint(a, pltpu.HBM) for a in list(arrays) + zones]
    sem = lambda n: pltpu.SemaphoreType.DMA((max(n, 1),))
    out = pl.pallas_call(
        body, name=name,
        out_shape=(sem(len(plan)), sem(len(plan)), sem(len(local)), *[pltpu.HBM(a.shape, a.dtype) for a in operands],
                   _sds((SUBLANES, LANES), F32)),
        in_specs=[HBM_SPEC] * n_buf,
        out_specs=(SEM_SPEC, SEM_SPEC, SEM_SPEC, *[HBM_SPEC] * n_buf, pl.BlockSpec(memory_space=pltpu.VMEM)),
        input_output_aliases={i: 3 + i for i in range(n_buf)},
        compiler_params=pltpu.CompilerParams(has_side_effects=pltpu.SideEffectType.DATAFLOW_SIDE_EFFECTING))(*operands)
    pending = dict(name=name, sems=out[:3], thru=out[3:3 + n_in], zones=out[3 + n_in:3 + n_buf], plan=plan, local=local,
                   inplace=inplace)
    return pending, out[-1]


def _exchange_wait(pending, after, both=False):
    thru, zones, plan, local, inplace = (pending[k] for k in ("thru", "zones", "plan", "local", "inplace"))
    n_in, n_buf = len(thru), len(thru) + len(zones)

    def body(*refs):
        bufs = refs[:n_buf]
        sends, arrivals, locs = _split_copies(bufs * 2 if inplace else bufs, plan, local, *refs[n_buf:n_buf + 3])
        for cp in arrivals:
            cp.wait_recv()
        for cp in sends:
            cp.wait_send()
        for cp in locs:
            cp.wait()

    out = pl.pallas_call(
        body, name=pending["name"] + "_wait", out_shape=tuple(pltpu.HBM(a.shape, a.dtype) for a in list(thru) + list(zones)),
        in_specs=[HBM_SPEC] * n_buf + [SEM_SPEC] * 3 + [pl.BlockSpec(memory_space=pl.ANY)],
        out_specs=tuple([HBM_SPEC] * n_buf), input_output_aliases={i: i for i in range(n_buf)},
        compiler_params=pltpu.CompilerParams(has_side_effects=pltpu.SideEffectType.DATAFLOW_SIDE_EFFECTING))(
            *thru, *zones, *pending["sems"], after)
    if both:
        return out[:n_in], out[n_in:]
    return out[:n_in] if inplace else out[n_in:]


def _whole(*_):
    return ()


def _half_rows(rows, core):
    return pl.ds(core * (rows // 2), rows // 2)


def _gather_weights_plan(shards):
    n = len(shards)
    dsts = [_sds((N_CHIPS,) + a.shape, a.dtype) for a in shards]
    fetch, forward = [], []
    for t, a in enumerate(shards):
        rows = a.shape[0]
        if rows % (2 * 16) == 0:
            fetch += [(rel, t, (lambda s_, r_, rows=rows: (_half_rows(rows, s_[2]),)), n + t,
                       (lambda s_, r_, rows=rows: (_chip_of(s_), _half_rows(rows, s_[2])))) for rel in REL_CHIPS]
            forward += [(REL_SIBLING, n + t, (lambda s_, r_, rows=rows, rel=rel: (_chip_of(_flip(s_, rel)), _half_rows(rows, s_[2]))),
                         n + t, (lambda s_, r_, rows=rows, rel=rel: (_chip_of(_flip(s_, rel)), _half_rows(rows, s_[2]))))
                        for rel in REL_CHIPS]
        else:
            fetch += [(rel, t, _whole, n + t, lambda s_, r_: (_chip_of(s_),)) for rel in REL_CHIPS]
    local = [(t, _whole, n + t, lambda me: (_chip_of(me),)) for t in range(n)]
    return dsts, fetch, local, forward


def _gather_weights_start(name, shards):
    dsts, fetch, local, forward = _gather_weights_plan(shards)
    pending, token = _exchange_start(name, shards, dsts, fetch, local)
    return dict(pending, forward=forward), token


def _gather_weights_finish(pending, after):
    landed = _exchange_wait(pending, after)
    n = len(landed)
    return _exchange(pending["name"] + "_forward", landed, n, [[_sds(a.shape, a.dtype) for a in landed], (pending["forward"], [])],
                     aliases={t: t for t in range(n)})


def _gather_chips(name, arrays):
    n = len(arrays)
    dsts = [_sds((N_CHIPS,) + a.shape, a.dtype) for a in arrays]
    plan = [(rel, t, _whole, n + t, lambda s_, r_: (_chip_of(s_),)) for t in range(n) for rel in REL_CHIPS]
    local = [(t, _whole, n + t, lambda me: (_chip_of(me),)) for t in range(n)]
    return _exchange(name, arrays, n, [dsts, (plan, local)])


def _gather_all(name, a):
    plan = [(rel, 0, _whole, 1, lambda s_, r_: (_dev_of(s_),)) for rel in REL_ALL]
    local = [(0, _whole, 1, lambda me: (_dev_of(me),))]
    return _exchange(name, [a], 1, [[_sds((2 * N_CHIPS,) + a.shape, a.dtype)], (plan, local)])[0]


def _swap_halves_start(name, grads):
    n = len(grads)
    dsts = [_sds((g.shape[0], g.shape[1] // 2, g.shape[2]), g.dtype) for g in grads]
    plan = [(REL_SIBLING, t, (lambda s_, r_, rows=g.shape[1]: (pl.ds(0, N_CHIPS), _half_rows(rows, r_[2]))), n + t, _whole)
            for t, g in enumerate(grads)]
    return _exchange_start(name, grads, dsts, plan, [])


def _scatter_chips_plan(sums):
    n = len(sums)
    dsts = [_sds(a.shape, a.dtype) for a in sums]
    plan = [(rel, t, lambda s_, r_: (_chip_of(r_),), n + t, lambda s_, r_: (_chip_of(s_),))
            for t in range(n) for rel in REL_CHIPS]
    local = [(t, lambda me: (_chip_of(me),), n + t, lambda me: (_chip_of(me),)) for t in range(n)]
    return dsts, plan, local


def _scatter_chips(name, sums):
    dsts, plan, local = _scatter_chips_plan(sums)
    return _exchange(name, sums, len(sums), [dsts, (plan, local)])


def _scatter_chips_start(name, sums):
    dsts, plan, local = _scatter_chips_plan(sums)
    return _exchange_start(name, sums, dsts, plan, local)


def _swap_back_start(name, totals, layer):
    n = len(totals)
    plan = [(REL_SIBLING, n + t, (lambda s_, r_, rows=a.shape[1]: (layer, _half_rows(rows, s_[2]))),
             n + t, (lambda s_, r_, rows=a.shape[1]: (layer, _half_rows(rows, s_[2])))) for t, a in enumerate(totals)]
    return _exchange_start(name, totals, [], plan, [], inplace=True)


def _add_halves(name, g, recv, core):
    nch, half, c = recv.shape

    def body(core_ref, g_ref, r_ref, o_ref):
        o_ref[0] = (g_ref[0, 0].astype(F32) + r_ref[0].astype(F32)).astype(o_ref.dtype)

    spec = pltpu.PrefetchScalarGridSpec(
        num_scalar_prefetch=1, grid=(nch,),
        in_specs=[pl.BlockSpec((1, 1, half, c), lambda k, core_ref: (k, core_ref[0], 0, 0)),
                  pl.BlockSpec((1, half, c), lambda k, core_ref: (k, 0, 0))],
        out_specs=pl.BlockSpec((1, half, c), lambda k, core_ref: (k, 0, 0)))
    return pl.pallas_call(body, name=name, out_shape=_sds(recv.shape, recv.dtype), grid_spec=spec,
                          compiler_params=pltpu.CompilerParams(dimension_semantics=("parallel",),
                                                               vmem_limit_bytes=VMEM_LIMIT))(
                                                                   core, g.reshape(nch, 2, half, c), recv)


def _sum_chips(name, a, core, layer, total):
    nch, half, c = a.shape

    def body(core_ref, a_ref, *rest):
        o_ref = rest[-1]
        acc = a_ref[0].astype(F32)
        for k in range(1, nch):
            acc = acc + a_ref[k].astype(F32)
        o_ref[0, 0] = acc

    in_specs = [pl.BlockSpec((nch, half, c), lambda i, core_ref: (0, 0, 0))]
    args = [core, a]
    if total is not None:
        in_specs.append(pl.BlockSpec(memory_space=pl.ANY))
        args.append(total.reshape(DEPTH, 2, half, c))
    spec = pltpu.PrefetchScalarGridSpec(
        num_scalar_prefetch=1, grid=(1,), in_specs=in_specs,
        out_specs=pl.BlockSpec((1, 1, half, c), lambda i, core_ref: (layer, core_ref[0], 0, 0)))
    out = pl.pallas_call(body, name=name, out_shape=_sds((DEPTH, 2, half, c), F32), grid_spec=spec,
                         input_output_aliases={2: 0} if total is not None else {},
                         compiler_params=pltpu.CompilerParams(dimension_semantics=("arbitrary",),
                                                              vmem_limit_bytes=VMEM_LIMIT))(*args)
    return out.reshape(DEPTH, 2 * half, c)


def _adamw_update(w, g, m, v):
    mn = ADAM_B1 * m + (1.0 - ADAM_B1) * g
    vn = ADAM_B2 * v + (1.0 - ADAM_B2) * (g * g)
    m_hat = mn / (1.0 - ADAM_B1 ** ADAM_STEP)
    v_hat = vn / (1.0 - ADAM_B2 ** ADAM_STEP)
    return -ADAM_LR * (m_hat / (jnp.sqrt(v_hat) + ADAM_EPS) + ADAM_WD * w), mn, vn


def _adamw(name, w, g, m, v):
    depth, r, c = w.shape
    tr = next(t for t in (512, 448, 384, 352, 336, 256, 192, 128, 64, 32, 16, 8) if r % t == 0 and t * c <= ADAM_TILE_ELEMS)

    def body(w_ref, g_ref, m_ref, v_ref, go_ref, d_ref, mo_ref, vo_ref):
        gv = g_ref[...]
        go_ref[...] = gv
        d_ref[...], mo_ref[...], vo_ref[...] = _adamw_update(w_ref[...], gv, m_ref[...], v_ref[...])

    spec = pl.BlockSpec((1, tr, c), lambda l, i: (l, i, 0))
    o = _sds(w.shape, F32)
    return _call(body, name, (o, o, o, o), (depth, r // tr), [spec] * 4, (spec,) * 4, ("parallel", "parallel"))(w, g, m, v)


def _adamw_small(name, ws, gs, ms, vs):
    n = len(ws)

    def body(*refs):
        for t in range(n):
            w_ref, g_ref, m_ref, v_ref = (refs[k * n + t] for k in range(4))
            d_ref, mo_ref, vo_ref = (refs[(4 + k) * n + t] for k in range(3))
            d_ref[...], mo_ref[...], vo_ref[...] = _adamw_update(w_ref[...], g_ref[...], m_ref[...], v_ref[...])

    outs = [_sds(w.shape, F32) for w in ws]
    res = pl.pallas_call(body, name=name, out_shape=tuple(outs * 3))(*ws, *gs, *ms, *vs)
    return res[:n], res[n:2 * n], res[2 * n:]


def _pack(arrays, rows):
    flat = jnp.concatenate([a.reshape(-1).astype(F32) for a in arrays])
    return _pad_to(flat, 0, rows * LANES).reshape(rows, LANES)


def _unpack(packed, shapes):
    flat = packed.reshape(-1)
    out, off = [], 0
    for shp in shapes:
        n = 1
        for s_ in shp:
            n *= s_
        out.append(flat[off:off + n].reshape(shp))
        off += n
    return out


_MATRICES = ("w_in", "proj_a", "proj_b", "w_out", "w_gate", "w_up", "w_down")
_SMALL = (("norm1_w", (D_MODEL,)), ("sinks", (N_Q_HEADS,)), ("sgu_ln_w", (SGU_WIDTH,)), ("sgu_ln_b", (SGU_WIDTH,)),
          ("sgu_w", (SGU_GROUPS, SGU_CHUNK, SGU_CHUNK)), ("sgu_b", (SGU_GROUPS, SGU_CHUNK)), ("norm2_w", (D_MODEL,)),
          ("conv_w", (3, FFN_DIM)), ("conv_b", (FFN_DIM,)), ("final_norm_w", (D_MODEL,)))
SMALL_ROWS = 320
ADAM_TILE_ELEMS = 384 * 1024


def _reduce_cores_start(tag, partial):
    names = list(partial)
    pending, token = _swap_halves_start(tag + "_cores", [partial[k] for k in names])
    return dict(pending, tag=tag, names=names), token


def _reduce_chips_start(pending, core, after):
    tag, names = pending["tag"], pending["names"]
    mine, theirs = _exchange_wait(pending, after, both=True)
    sums = [_add_halves(f"{tag}_cores_add_{k}", g, r, core) for k, g, r in zip(names, mine, theirs)]
    scatter, token = _scatter_chips_start(tag + "_chips", sums)
    return dict(scatter, tag=tag, names=names), token


def _reduce_back_start(pending, l, core, totals, after):
    tag, names = pending["tag"], pending["names"]
    sums = [_sum_chips(f"{tag}_chips_add_{k}", a, core, l, totals.get(k)) for k, a in zip(names, _exchange_wait(pending, after))]
    back, _ = _swap_back_start(tag + "_back", sums, l)
    return dict(back, names=names)


def kernel(x, c, positions, ada_w, ada_b, norm1_w, w_in, attn_sinks, sgu_ln_w, sgu_ln_b, sgu_w, sgu_b, proj_a, proj_b, w_out, norm2_w, ffn_w_gate, ffn_w_up, ffn_conv_w, ffn_conv_b, ffn_w_down, final_norm_w, loss_target, m_ada_w, m_ada_b, m_norm1_w, m_w_in, m_attn_sinks, m_sgu_ln_w, m_sgu_ln_b, m_sgu_w, m_sgu_b, m_proj_a, m_proj_b, m_w_out, m_norm2_w, m_ffn_w_gate, m_ffn_w_up, m_ffn_conv_w, m_ffn_conv_b, m_ffn_w_down, m_final_norm_w, v_ada_w, v_ada_b, v_norm1_w, v_w_in, v_attn_sinks, v_sgu_ln_w, v_sgu_ln_b, v_sgu_w, v_sgu_b, v_proj_a, v_proj_b, v_w_out, v_norm2_w, v_ffn_w_gate, v_ffn_w_up, v_ffn_conv_w, v_ffn_conv_b, v_ffn_w_down, v_final_norm_w):
    d = D_MODEL
    ax, ay, ac = lax.axis_index("x"), lax.axis_index("y"), lax.axis_index("c")
    chip = 2 * ax + ay
    dev = 4 * ax + 2 * ay + ac
    core = ac.astype(jnp.int32).reshape(1)

    c_all = _gather_all("gather_cond", c.reshape(SUBLANES, d // SUBLANES)).reshape(2 * N_CHIPS, d)
    c_rows = _pad_to(c_all, 0, ADA_ROWS)
    ada_cols = ada_w.shape[2]
    ada_b_cols = lax.dynamic_slice_in_dim(ada_b, chip * ada_cols, ada_cols, axis=1).reshape(DEPTH, 1, ada_cols)
    mod_cols = _ada_fwd("ada_fwd", c_rows, ada_w, ada_b_cols)
    mod_all = _gather_chips("gather_mod", [mod_cols])[0]
    mod_mine = lax.dynamic_index_in_dim(mod_all, dev, axis=2, keepdims=False)
    mod_mine = mod_mine.transpose(1, 0, 2).reshape(DEPTH, 1, 6 * d)
    mods = [tuple(jnp.split(mod_mine[l], 6, axis=-1)) for l in range(DEPTH)]

    tr = lambda a: jnp.swapaxes(a, 1, 2)
    shards = [tr(w_in).astype(BF), proj_a.astype(BF), proj_b.astype(BF), w_out.astype(BF),
              _pad_to(tr(ffn_w_gate).astype(BF), 1, FFN_SHARD_PAD), _pad_to(tr(ffn_w_up).astype(BF), 1, FFN_SHARD_PAD),
              _pad_to(ffn_w_down.astype(BF), 1, FFN_SHARD_PAD), _pad_to(ffn_conv_w, 2, FFN_SHARD_PAD)]
    token = mod_all[0, 0, :SUBLANES, :LANES]
    fetches = []
    for l in range(DEPTH):
        groups = []
        for tag, members in (("in", shards[:1]), ("mix", shards[1:4]), ("ffn", shards[4:])):
            behind = (token[0, 0] * 0.0).astype(members[0].dtype)
            pending, token = _gather_weights_start(f"l{l}_gather_{tag}", [members[0][l] + behind] + [a[l] for a in members[1:]])
            groups.append(pending)
        fetches.append(groups)

    small_in = dict(norm1_w=norm1_w, sinks=attn_sinks, sgu_ln_w=sgu_ln_w, sgu_ln_b=sgu_ln_b, sgu_w=sgu_w, sgu_b=sgu_b,
                    norm2_w=norm2_w, conv_b=ffn_conv_b)
    cosf, sinf = _rope_tables(positions[0])
    small_of = lambda l: {k: v[l] for k, v in small_in.items()}

    h = x[0]
    saved, params = [], []
    for l in range(DEPTH):
        first, mix, ffn = fetches[l]
        w_in_l = _gather_weights_finish(first, token if l == 0 else h)
        late = lambda y, l=l, mix=mix: _mix_params(*_gather_weights_finish(mix, y), small_of(l))
        later = lambda y, l=l, ffn=ffn: _ffn_params(*_gather_weights_finish(ffn, y), small_of(l))
        h, sv, p = _layer_fwd(l, h, mods[l], _early_params(w_in_l[0], small_of(l)), cosf, sinf, late=late, later=later)
        saved.append(sv)
        params.append(p)
    dx, loss_part, d_final = _loss_head("loss_head", h, final_norm_w.reshape(1, d), loss_target[0])
    loss = lax.psum(loss_part[0, 0], ("x", "y", "c"))

    def small_pack(l, grads):
        cw, cb = _conv_grads_natural(grads)
        nat = dict(grads, conv_w=cw, conv_b=cb, final_norm_w=d_final if l == DEPTH - 1 else jnp.zeros((d,), F32))
        return _pack([nat[k] for k, _ in _SMALL], N_CHIPS * SMALL_ROWS).reshape(N_CHIPS, SMALL_ROWS, LANES)

    waiting, inflight = [], []

    def send(tag, partial):
        pending, token = _reduce_cores_start(tag, partial)
        waiting.append(pending)
        return token

    def tick(y):
        token = None
        while waiting:
            pending, token = _reduce_chips_start(waiting.pop(0), core, y)
            inflight.append(pending)
        return token

    dmods = [None] * DEPTH
    dx, dmods[1], grads = _layer_bwd(1, dx, mods[1], params[1], saved[1], cosf, sinf)
    token = send("l1_reduce", dict({k: grads[k] for k in _MATRICES}, small=small_pack(1, grads)))
    dx, dmods[0], grads = _layer_bwd(0, dx, mods[0], params[0], saved[0], cosf, sinf, after=token,
                                     emit=lambda part: send("l0_reduce_" + "_".join(part), part), tick=tick)
    dmod_mine = jnp.concatenate([jnp.concatenate(dmods[l], axis=1) for l in range(DEPTH)], axis=1)
    dmod_all = _gather_all("gather_dmod", dmod_mine.reshape(SUBLANES, -1)).reshape(2 * N_CHIPS, DEPTH * 6 * d)
    send("l0_reduce_in", dict(w_in=grads["w_in"], small=small_pack(0, grads) + dmod_all[0, 0] * 0.0))
    tick(dmod_all)

    totals, flying = {}, None

    def land(after):
        if flying is not None:
            totals.update(zip(flying["names"], _exchange_wait(flying, after)))

    for pending in inflight[:-1]:
        land(dx)
        flying = _reduce_back_start(pending, 1 if pending["tag"].startswith("l1") else 0, core, totals, dx)
    land(dx)
    flying = None

    g_ada_b = _colsum("ada_b_grad", dmod_all).reshape(DEPTH, 6 * d)
    dmod_cols = jnp.stack([lax.dynamic_slice_in_dim(dmod_all, l * 6 * d + chip * ada_cols, ada_cols, axis=1)
                           for l in range(DEPTH)])
    g_ada_w = _ada_bwd("ada_w_grad", c_rows, _pad_to(dmod_cols, 1, ADA_ROWS))
    big = dict(w_in=(tr(w_in), tr(m_w_in), tr(v_w_in)), proj_a=(proj_a, m_proj_a, v_proj_a), proj_b=(proj_b, m_proj_b, v_proj_b),
               w_out=(w_out, m_w_out, v_w_out), w_gate=(tr(ffn_w_gate), tr(m_ffn_w_gate), tr(v_ffn_w_gate)),
               w_up=(tr(ffn_w_up), tr(m_ffn_w_up), tr(v_ffn_w_up)), w_down=(ffn_w_down, m_ffn_w_down, v_ffn_w_down))
    upd, g_big = {}, {}

    def update(k):
        res = _adamw("adamw_" + k, big[k][0], totals[k], *big[k][1:])
        res = [tr(a) for a in res] if k in ("w_in", "w_gate", "w_up") else res
        g_big[k], upd[k] = res[0], res[1:]

    for k in ("w_down", "w_gate", "w_up", "w_out", "proj_a", "proj_b"):
        update(k)
    g_big["ada_w"], *upd["ada_w"] = _adamw("adamw_ada_w", ada_w, g_ada_w, m_ada_w, v_ada_w)
    flying = _reduce_back_start(inflight[-1], 0, core, totals, upd["ada_w"][0])
    land(upd["ada_w"][1])
    update("w_in")

    small_all = _gather_chips("gather_small", [totals["small"]])[0]
    small_g = small_all.transpose(1, 0, 2, 3).reshape(DEPTH, -1)
    per_layer = [_unpack(small_g[l], [shp for _, shp in _SMALL]) for l in range(DEPTH)]
    sg = {k: jnp.stack([per_layer[l][i] for l in range(DEPTH)]) for i, (k, _) in enumerate(_SMALL)}
    g_final = sg["final_norm_w"][DEPTH - 1]
    g_conv_w = lax.dynamic_slice_in_dim(sg["conv_w"], chip * FFN_SHARD, FFN_SHARD, axis=2)

    rest = [("ada_b", ada_b, g_ada_b, m_ada_b, v_ada_b), ("norm1_w", norm1_w, sg["norm1_w"], m_norm1_w, v_norm1_w),
            ("attn_sinks", attn_sinks, sg["sinks"], m_attn_sinks, v_attn_sinks),
            ("sgu_ln_w", sgu_ln_w, sg["sgu_ln_w"], m_sgu_ln_w, v_sgu_ln_w),
            ("sgu_ln_b", sgu_ln_b, sg["sgu_ln_b"], m_sgu_ln_b, v_sgu_ln_b), ("sgu_w", sgu_w, sg["sgu_w"], m_sgu_w, v_sgu_w),
            ("sgu_b", sgu_b, sg["sgu_b"], m_sgu_b, v_sgu_b), ("norm2_w", norm2_w, sg["norm2_w"], m_norm2_w, v_norm2_w),
            ("ffn_conv_w", ffn_conv_w, g_conv_w, m_ffn_conv_w, v_ffn_conv_w),
            ("ffn_conv_b", ffn_conv_b, sg["conv_b"], m_ffn_conv_b, v_ffn_conv_b),
            ("final_norm_w", final_norm_w.reshape(1, d), g_final.reshape(1, d), m_final_norm_w.reshape(1, d),
             v_final_norm_w.reshape(1, d))]
    rest_out = _adamw_small("adamw_rest", *[[r[i] for r in rest] for i in (1, 2, 3, 4)])
    g_rest = {r[0]: r[2] for r in rest}
    u_rest = {r[0]: tuple(o[i] for o in rest_out) for i, r in enumerate(rest)}
    g_rest["final_norm_w"] = g_final
    u_rest["final_norm_w"] = tuple(a.reshape(d) for a in u_rest["final_norm_w"])

    names = ("ada_w", "ada_b", "norm1_w", "w_in", "attn_sinks", "sgu_ln_w", "sgu_ln_b", "sgu_w", "sgu_b", "proj_a", "proj_b",
             "w_out", "norm2_w", "ffn_w_gate", "ffn_w_up", "ffn_conv_w", "ffn_conv_b", "ffn_w_down", "final_norm_w")
    alias = {"ffn_w_gate": "w_gate", "ffn_w_up": "w_up", "ffn_w_down": "w_down"}
    grad_of = lambda n: g_rest[n] if n in g_rest else g_big[alias.get(n, n)]
    upd_of = lambda n: u_rest[n] if n in u_rest else upd[alias.get(n, n)]
    return (loss, dx[None], *[grad_of(n) for n in names], *[upd_of(n)[0] for n in names],
            *[upd_of(n)[1] for n in names], *[upd_of(n)[2] for n in names])
```

```python
import jax
import jax.numpy as jnp
from jax import lax
from jax.experimental import pallas as pl
from jax.experimental.pallas import tpu as pltpu

F32 = jnp.float32
BF = jnp.bfloat16

D_MODEL = 1024
N_Q_HEADS = 16
N_KV_HEADS = 2
HEAD_DIM = 64
ATTN_BLOCK = 128
ROPE_THETA = 500000.0
ROT_DIM = HEAD_DIM // 4
SGU_WIDTH = 1024
SGU_GROUPS = 8
SGU_CHUNK = 128
FFN_DIM = 2816
NORM_EPS = 1e-6
DEPTH = 2
IN_COLS = 5376
N_CHIPS = 4
FFN_SHARD = FFN_DIM // N_CHIPS
CHIPS_PER_CHUNK = 2
FFN_CHUNK = CHIPS_PER_CHUNK * FFN_SHARD
FFN_CHUNKS = FFN_DIM // FFN_CHUNK
LANES = 128
SUBLANES = 8
HALO = 16
VMEM_LIMIT = 56 * 1024 * 1024
NEG_BIG = -1e30

ADAM_LR = 0.001
ADAM_B1 = 0.9
ADAM_B2 = 0.999
ADAM_EPS = 1e-08
ADAM_WD = 0.01
ADAM_STEP = 10

MESH = pl.DeviceIdType.MESH

Q_END = 1024
KV_END = 1280
U_END = 2304
Z_END = 3328
GA_END = 4352


def _sds(shape, dtype):
    return jax.ShapeDtypeStruct(tuple(shape), dtype)


def _call(body, name, out_shape, grid, in_specs, out_specs, semantics, scratch=(), after=None):
    n_in = len(in_specs)
    fn = body
    if after is not None:
        def fn(*refs):
            return body(*refs[:n_in], *refs[n_in + 1:])

        in_specs = list(in_specs) + [pl.BlockSpec(memory_space=pl.ANY)]
    call = pl.pallas_call(
        fn, name=name, out_shape=out_shape, grid=grid, in_specs=in_specs, out_specs=out_specs,
        scratch_shapes=scratch,
        compiler_params=pltpu.CompilerParams(dimension_semantics=semantics, vmem_limit_bytes=VMEM_LIMIT))
    if after is None:
        return call
    return lambda *args: call(*args, after)


def _rows(tm, width, col=0):
    return pl.BlockSpec((tm, width), lambda i: (i, col))


def _vec(width):
    return pl.BlockSpec((1, width), lambda i: (0, 0))


def _resident(shape):
    zeros = (0,) * len(shape)
    return pl.BlockSpec(tuple(shape), lambda *_: zeros, pipeline_mode=pl.Buffered(1))


def _sigmoid(x):
    return 0.5 + 0.5 * jnp.tanh(0.5 * x)


def _gelu(x):
    return 0.5 * x * (1.0 + lax.erf(x * 0.7071067811865476))


def _gelu_and_grad(x):
    cdf = 0.5 * (1.0 + lax.erf(x * 0.7071067811865476))
    return x * cdf, cdf + x * jnp.exp(-0.5 * x * x) * 0.3989422804014327


def _dot(a, b):
    return jnp.dot(a, b, preferred_element_type=F32)


def _dot_nt(a, b):
    return lax.dot_general(a, b, (((1,), (1,)), ((), ())), preferred_element_type=F32)


def _dot_tn(a, b):
    return lax.dot_general(a, b, (((0,), (0,)), ((), ())), preferred_element_type=F32)


def _rms(xv):
    return lax.rsqrt(jnp.mean(xv * xv, axis=-1, keepdims=True) + NORM_EPS)


def _matmul_tn(name, a, b, tk=512, tn=1024, blocked=False, after=None):
    s, k = a.shape
    n = b.shape[1]
    tk, tn = min(tk, k), min(tn, n)

    def body(a_ref, b_ref, o_ref):
        res = _dot_tn(a_ref[...], b_ref[...]).astype(o_ref.dtype)
        if blocked:
            o_ref[0] = res
        else:
            o_ref[...] = res

    if blocked:
        out, ospec = _sds((n // tn, k, tn), BF), pl.BlockSpec((1, tk, tn), lambda i, j: (j, i, 0))
    else:
        out, ospec = _sds((k, n), BF), pl.BlockSpec((tk, tn), lambda i, j: (i, j))
    return _call(body, name, out, (k // tk, n // tn),
                 [pl.BlockSpec((s, tk), lambda i, j: (0, i)), pl.BlockSpec((s, tn), lambda i, j: (0, j))],
                 ospec, ("parallel", "parallel"), after=after)(a, b)


def _matmul_tn_rows(name, a, b, out, row0, rows_total, tk=256):
    s, k = a.shape
    n = b.shape[1]

    def body(a_ref, b_ref, *rest):
        rest[-1][...] = _dot_tn(a_ref[...], b_ref[...]).astype(BF)

    in_specs = [pl.BlockSpec((s, tk), lambda i: (0, i)), _resident(b.shape)]
    args = [a, b]
    if out is not None:
        in_specs.append(pl.BlockSpec(memory_space=pl.ANY))
        args.append(out)
    return pl.pallas_call(
        body, name=name, out_shape=_sds((rows_total, n), BF), grid=(k // tk,), in_specs=in_specs,
        out_specs=pl.BlockSpec((tk, n), lambda i: (row0 // tk + i, 0)),
        input_output_aliases={2: 0} if out is not None else {},
        compiler_params=pltpu.CompilerParams(dimension_semantics=("parallel",), vmem_limit_bytes=VMEM_LIMIT))(*args)


def _rope_partner(v):
    lane = lax.broadcasted_iota(jnp.int32, (1, LANES), 1) % HEAD_DIM
    return jnp.where(lane < ROT_DIM // 2, pltpu.roll(v, LANES - ROT_DIM // 2, axis=1), pltpu.roll(v, ROT_DIM // 2, axis=1))


def _dup_half(v, half):
    lane = lax.broadcasted_iota(jnp.int32, (1, LANES), 1)
    keep = jnp.where((lane >= HEAD_DIM) == (half == 1), v, 0.0)
    return keep + pltpu.roll(keep, HEAD_DIM, axis=1)


def _in_proj(name, x, w, sc, sh, w_in, cosf, sinf, tm=512, after=None):
    s, d = x.shape
    tm = min(tm, s)

    def body(x_ref, w_ref, sc_ref, sh_ref, win_ref, cos_ref, sin_ref,
             h_ref, qr_ref, kk0_ref, kk1_ref, vv0_ref, vv1_ref, u_ref, v_ref, ga_ref, gb_ref):
        xv = x_ref[...]
        h = ((xv * _rms(xv)) * w_ref[...] * (1.0 + sc_ref[...]) + sh_ref[...]).astype(BF)
        h_ref[...] = h
        cosv, sinv = cos_ref[...], sin_ref[...]
        q = _dot_nt(h, win_ref[:Q_END, :])
        for j in range(D_MODEL // LANES):
            qv = q[:, j * LANES:(j + 1) * LANES]
            qr_ref[:, j * LANES:(j + 1) * LANES] = ((qv * cosv + _rope_partner(qv) * sinv) * ATTN_SCALE).astype(BF)
        kv = _dot_nt(h, win_ref[Q_END:KV_END, :])
        kr = kv[:, :LANES] * cosv + _rope_partner(kv[:, :LANES]) * sinv
        vv = kv[:, LANES:]
        kk0_ref[...] = _dup_half(kr, 0).astype(BF)
        kk1_ref[...] = _dup_half(kr, 1).astype(BF)
        vv0_ref[...] = _dup_half(vv, 0).astype(BF)
        vv1_ref[...] = _dup_half(vv, 1).astype(BF)
        u_ref[...] = _dot_nt(h, win_ref[KV_END:U_END, :])
        v_ref[...] = _dot_nt(h, win_ref[U_END:Z_END, :])
        ga_ref[...] = _dot_nt(h, win_ref[Z_END:GA_END, :]).astype(BF)
        gb_ref[...] = _dot_nt(h, win_ref[GA_END:, :]).astype(BF)

    wide, kvs, pre = _sds((s, d), BF), _sds((s, LANES), BF), _sds((s, d), F32)
    return _call(body, name, (wide, wide, kvs, kvs, kvs, kvs, pre, pre, wide, wide), (s // tm,),
                 [_rows(tm, d), _vec(d), _vec(d), _vec(d), _resident(w_in.shape), _rows(tm, LANES), _rows(tm, LANES)],
                 (_rows(tm, d), _rows(tm, d)) + (_rows(tm, LANES),) * 4 + (_rows(tm, d),) * 4, ("parallel",), after=after)(
                     x, w, sc, sh, w_in, cosf, sinf)


def _in_proj_bwd(name, dq, dkv, du, dv, dga, dgb, w_in, x, w, sc, dx_in, tm=512):
    s, d = x.shape
    tm = min(tm, s)

    def body(dq_ref, dkv_ref, du_ref, dv_ref, dga_ref, dgb_ref, win_ref, x_ref, w_ref, sc_ref, dxin_ref,
             dx_ref, da_ref, dsh_ref):
        @pl.when(pl.program_id(0) == 0)
        def _():
            da_ref[...] = jnp.zeros_like(da_ref)
            dsh_ref[...] = jnp.zeros_like(dsh_ref)

        dh = (_dot(dq_ref[...], win_ref[:Q_END, :]) + _dot(dkv_ref[...], win_ref[Q_END:KV_END, :])
              + _dot(du_ref[...], win_ref[KV_END:U_END, :]) + _dot(dv_ref[...], win_ref[U_END:Z_END, :])
              + _dot(dga_ref[...], win_ref[Z_END:GA_END, :]) + _dot(dgb_ref[...], win_ref[GA_END:, :]))
        xv = x_ref[...]
        r = _rms(xv)
        xn = xv * r
        dxn = dh * (w_ref[...] * (1.0 + sc_ref[...]))
        dx_ref[...] = dxin_ref[...] + r * (dxn - xn * jnp.mean(dxn * xn, axis=-1, keepdims=True))
        da_ref[...] += jnp.sum(dh * xn, axis=0, keepdims=True)
        dsh_ref[...] += jnp.sum(dh, axis=0, keepdims=True)

    return _call(body, name, (_sds((s, d), F32), _sds((1, d), F32), _sds((1, d), F32)), (s // tm,),
                 [_rows(tm, d), _rows(tm, 2 * LANES), _rows(tm, d), _rows(tm, d), _rows(tm, d), _rows(tm, d),
                  _resident(w_in.shape), _rows(tm, d), _vec(d), _vec(d), _rows(tm, d)],
                 (_rows(tm, d), _vec(d), _vec(d)), ("arbitrary",))(dq, dkv, du, dv, dga, dgb, w_in, x, w, sc, dx_in)


def _rope_bwd(name, dqr, dkv_cur, dkv_prev, cosf, sinf, tm=512):
    s = dqr.shape[0]
    tm = min(tm, s)
    steps = s // tm
    per = tm // ATTN_BLOCK
    nb = s // ATTN_BLOCK

    def unrope(v, cosv, sinv):
        return v * cosv - _rope_partner(v) * sinv

    def body(dq_ref, cur_ref, prev_ref, next_ref, cos_ref, sin_ref, dqo_ref, dkvo_ref):
        i = pl.program_id(0)
        cosv, sinv = cos_ref[...], sin_ref[...]
        for j in range(D_MODEL // LANES):
            dqo_ref[:, j * LANES:(j + 1) * LANES] = unrope(dq_ref[:, j * LANES:(j + 1) * LANES], cosv, sinv).astype(BF)
        nxt = jnp.where(i < steps - 1, next_ref[...], 0.0)
        if per > 1:
            shifted = jnp.concatenate([prev_ref[ATTN_BLOCK:, :], nxt], axis=0)
        else:
            shifted = nxt
        tot = cur_ref[...] + shifted
        dkvo_ref[:, :LANES] = unrope(tot[:, :LANES], cosv, sinv).astype(BF)
        dkvo_ref[:, LANES:] = tot[:, LANES:].astype(BF)

    nxt_spec = pl.BlockSpec((ATTN_BLOCK, 2 * LANES), lambda i: (jnp.minimum((i + 1) * per, nb - 1), 0))
    return _call(body, name, (_sds((s, D_MODEL), BF), _sds((s, 2 * LANES), BF)), (steps,),
                 [_rows(tm, D_MODEL), _rows(tm, 2 * LANES), _rows(tm, 2 * LANES), nxt_spec, _rows(tm, LANES),
                  _rows(tm, LANES)],
                 (_rows(tm, D_MODEL), _rows(tm, 2 * LANES)), ("parallel",))(dqr, dkv_cur, dkv_prev, dkv_prev, cosf, sinf)


Q_PER_KV = N_Q_HEADS // N_KV_HEADS
ATTN_SCALE = HEAD_DIM ** -0.5
HEADS_AHEAD_FWD = 2
HEADS_AHEAD_BWD = 3


def _band_mask_t(n):
    kj = lax.broadcasted_iota(jnp.int32, (2 * ATTN_BLOCK, ATTN_BLOCK), 0)
    qi = lax.broadcasted_iota(jnp.int32, (2 * ATTN_BLOCK, ATTN_BLOCK), 1)
    return (kj > qi) & (kj <= qi + ATTN_BLOCK) & ((n > 0) | (kj >= ATTN_BLOCK))


def _softmax_t(raw, allowed, sink):
    sc = jnp.where(allowed, raw, NEG_BIG)
    m = jnp.maximum(jnp.max(sc, axis=0, keepdims=True), sink)
    p = jnp.exp(sc - m)
    esink = jnp.exp(sink - m)
    inv = 1.0 / (jnp.sum(p, axis=0, keepdims=True) + esink)
    return p * inv, esink * inv


def _kv_specs():
    cur = pl.BlockSpec((ATTN_BLOCK, LANES), lambda n: (n, 0))
    prev = pl.BlockSpec((ATTN_BLOCK, LANES), lambda n: (jnp.maximum(n - 1, 0), 0))
    return [prev, cur] * 4


def _attention(name, qr, kk0, kk1, vv0, vv1, sinks):
    s = qr.shape[0]
    nb = s // ATTN_BLOCK

    def body(sink_ref, q_ref, k0p, k0c, k1p, k1c, v0p, v0c, v1p, v1c, y_ref):
        allowed = _band_mask_t(pl.program_id(0))
        upper = lax.broadcasted_iota(jnp.int32, (1, LANES), 1) >= HEAD_DIM
        upper_rows = lax.broadcasted_iota(jnp.int32, (LANES, 1), 0) >= HEAD_DIM
        bands = ((jnp.concatenate([k0p[...], k0c[...]], axis=0), jnp.concatenate([v0p[...], v0c[...]], axis=0)),
                 (jnp.concatenate([k1p[...], k1c[...]], axis=0), jnp.concatenate([v1p[...], v1c[...]], axis=0)))
        vbts = (bands[0][1].T, bands[1][1].T)

        def scores(h):
            hk, j, half = h // Q_PER_KV, (h % Q_PER_KV) // 2, h % 2
            col = (hk * 4 + j) * LANES
            qp = q_ref[:, col:col + LANES]
            return _dot_nt(bands[hk][0], jnp.where(upper if half else jnp.logical_not(upper), qp, jnp.zeros_like(qp)))

        out_t = None
        ahead = [scores(h) for h in range(HEADS_AHEAD_FWD)]
        for h in range(N_Q_HEADS):
            hk, j, half = h // Q_PER_KV, (h % Q_PER_KV) // 2, h % 2
            raw = ahead.pop(0)
            if h + HEADS_AHEAD_FWD < N_Q_HEADS:
                ahead.append(scores(h + HEADS_AHEAD_FWD))
            pn, _ = _softmax_t(raw, allowed, sink_ref[h])
            o_h = _dot(vbts[hk], pn.astype(BF))
            out_t = jnp.where(upper_rows, o_h, out_t) if half else o_h
            if half:
                col = (hk * 4 + j) * LANES
                y_ref[:, col:col + LANES] = out_t.T.astype(BF)

    return _call(body, name, _sds((s, D_MODEL), BF), (nb,),
                 [pl.BlockSpec(memory_space=pltpu.SMEM), pl.BlockSpec((ATTN_BLOCK, D_MODEL), lambda n: (n, 0))] + _kv_specs(),
                 pl.BlockSpec((ATTN_BLOCK, D_MODEL), lambda n: (n, 0)), ("parallel",))(
                     sinks, qr, kk0, kk0, kk1, kk1, vv0, vv0, vv1, vv1)


def _attention_bwd(name, qr, kk0, kk1, vv0, vv1, sinks, dy, after=None):
    s = qr.shape[0]
    nb = s // ATTN_BLOCK

    def body(sink_ref, q_ref, dy_ref, k0p, k0c, k1p, k1c, v0p, v0c, v1p, v1c, dq_ref, cur_ref, prev_ref, dsink_ref):
        @pl.when(pl.program_id(0) == 0)
        def _():
            dsink_ref[...] = jnp.zeros_like(dsink_ref)

        allowed = _band_mask_t(pl.program_id(0))
        lane = lax.broadcasted_iota(jnp.int32, (1, LANES), 1)
        upper = lane >= HEAD_DIM
        upper_rows = lax.broadcasted_iota(jnp.int32, (LANES, 1), 0) >= HEAD_DIM
        bands = ((jnp.concatenate([k0p[...], k0c[...]], axis=0), jnp.concatenate([v0p[...], v0c[...]], axis=0)),
                 (jnp.concatenate([k1p[...], k1c[...]], axis=0), jnp.concatenate([v1p[...], v1c[...]], axis=0)))
        kbts = (bands[0][0].T, bands[1][0].T)

        def scores(h):
            hk, j, half = h // Q_PER_KV, (h % Q_PER_KV) // 2, h % 2
            kb, vb = bands[hk]
            col = (hk * 4 + j) * LANES
            sel = upper if half else jnp.logical_not(upper)
            qp = q_ref[:, col:col + LANES]
            qa = jnp.where(sel, qp, jnp.zeros_like(qp))
            dya = jnp.where(sel, dy_ref[:, col:col + LANES], 0.0).astype(BF)
            return qa, dya, _dot_nt(kb, qa), _dot_nt(vb, dya)

        dsink = jnp.zeros((1, LANES), F32)
        dk_slab = jnp.zeros((2 * ATTN_BLOCK, LANES), F32)
        dv_slab = jnp.zeros((2 * ATTN_BLOCK, LANES), F32)
        dkk = dvv = dq_t = None
        ahead = [scores(h) for h in range(HEADS_AHEAD_BWD)]
        for h in range(N_Q_HEADS):
            hk, j, half = h // Q_PER_KV, (h % Q_PER_KV) // 2, h % 2
            qa, dya, raw, dp = ahead.pop(0)
            if h + HEADS_AHEAD_BWD < N_Q_HEADS:
                ahead.append(scores(h + HEADS_AHEAD_BWD))
            pn, psink = _softmax_t(raw, allowed, sink_ref[h])
            delta = jnp.sum(pn * dp, axis=0, keepdims=True)
            ds = (pn * (dp - delta)).astype(BF)
            dsink = dsink + jnp.where(lane == h, -jnp.sum(psink * delta), 0.0)
            dq_h = _dot(kbts[hk], ds) * ATTN_SCALE
            dq_t = jnp.where(upper_rows, dq_h, dq_t) if half else dq_h
            dk_h, dv_h = _dot(ds, qa), _dot(pn.astype(BF), dya)
            dkk, dvv = (dk_h, dv_h) if h % Q_PER_KV == 0 else (dkk + dk_h, dvv + dv_h)
            if half:
                col = (hk * 4 + j) * LANES
                dq_ref[:, col:col + LANES] = dq_t.T
            if h % Q_PER_KV == Q_PER_KV - 1:
                mine = upper if hk else jnp.logical_not(upper)
                dk_slab = jnp.where(mine, dkk + pltpu.roll(dkk, HEAD_DIM, axis=1), dk_slab)
                dv_slab = jnp.where(mine, dvv + pltpu.roll(dvv, HEAD_DIM, axis=1), dv_slab)
        prev_ref[:, :LANES] = dk_slab[:ATTN_BLOCK]
        prev_ref[:, LANES:] = dv_slab[:ATTN_BLOCK]
        cur_ref[:, :LANES] = dk_slab[ATTN_BLOCK:]
        cur_ref[:, LANES:] = dv_slab[ATTN_BLOCK:]
        dsink_ref[...] += dsink

    blk = pl.BlockSpec((ATTN_BLOCK, D_MODEL), lambda n: (n, 0))
    kvo = pl.BlockSpec((ATTN_BLOCK, 2 * LANES), lambda n: (n, 0))
    return _call(body, name,
                 (_sds((s, D_MODEL), F32), _sds((s, 2 * LANES), F32), _sds((s, 2 * LANES), F32), _sds((1, LANES), F32)),
                 (nb,), [pl.BlockSpec(memory_space=pltpu.SMEM), blk, blk] + _kv_specs(),
                 (blk, kvo, kvo, pl.BlockSpec((1, LANES), lambda n: (0, 0))), ("arbitrary",), after=after)(
                     sinks, qr, dy, kk0, kk0, kk1, kk1, vv0, vv0, vv1, vv1)


def _sgu_weights(wm_ref, g):
    t = lax.broadcasted_iota(jnp.int32, (SGU_CHUNK, SGU_CHUNK), 0)
    sidx = lax.broadcasted_iota(jnp.int32, (SGU_CHUNK, SGU_CHUNK), 1)
    return jnp.where(sidx <= t, wm_ref[g], 0.0).astype(BF)


def _layer_norm_stats(v):
    mu = jnp.mean(v, axis=-1, keepdims=True)
    cen = v - mu
    rstd = lax.rsqrt(jnp.mean(cen * cen, axis=-1, keepdims=True) + NORM_EPS)
    return cen * rstd, rstd


def _sgu(name, u_pre, v_pre, ln_w, ln_b, wm, bfull, tm=256):
    s, w = u_pre.shape
    tm = min(tm, s)

    def body(u_ref, v_ref, lw_ref, lb_ref, wm_ref, b_ref, y_ref):
        vhat, _ = _layer_norm_stats(_gelu(v_ref[...]))
        vn = (vhat * lw_ref[...] + lb_ref[...]).astype(BF)
        for g in range(SGU_GROUPS):
            wg = _sgu_weights(wm_ref, g)
            cols = slice(g * SGU_CHUNK, (g + 1) * SGU_CHUNK)
            for ch in range(tm // SGU_CHUNK):
                rows = slice(ch * SGU_CHUNK, (ch + 1) * SGU_CHUNK)
                f = _dot(wg, vn[rows, cols]) + b_ref[g]
                y_ref[rows, cols] = (_gelu(u_ref[rows, cols]) * f).astype(BF)

    full3 = pl.BlockSpec((SGU_GROUPS, SGU_CHUNK, SGU_CHUNK), lambda i: (0, 0, 0))
    return _call(body, name, _sds((s, w), BF), (s // tm,),
                 [_rows(tm, w), _rows(tm, w), _vec(w), _vec(w), full3, full3],
                 _rows(tm, w), ("parallel",))(u_pre, v_pre, ln_w, ln_b, wm, bfull)


def _sgu_bwd(name, u_pre, v_pre, ln_w, ln_b, wm, bfull, dy, tm=256, after=None):
    s, w = u_pre.shape
    tm = min(tm, s)
    steps = s // tm

    def body(u_ref, v_ref, lw_ref, lb_ref, wm_ref, b_ref, dy_ref, du_ref, dv_ref, dwm_ref, db_ref, dlw_ref, dlb_ref,
             dfsum_ref):
        i = pl.program_id(0)

        @pl.when(i == 0)
        def _():
            dwm_ref[...] = jnp.zeros_like(dwm_ref)
            dlw_ref[...] = jnp.zeros_like(dlw_ref)
            dlb_ref[...] = jnp.zeros_like(dlb_ref)
            dfsum_ref[...] = jnp.zeros_like(dfsum_ref)

        vpre = v_ref[...]
        vg, dvg_dv = _gelu_and_grad(vpre)
        vhat, rstd = _layer_norm_stats(vg)
        vn = (vhat * lw_ref[...] + lb_ref[...]).astype(BF)
        t = lax.broadcasted_iota(jnp.int32, (SGU_CHUNK, SGU_CHUNK), 0)
        sidx = lax.broadcasted_iota(jnp.int32, (SGU_CHUNK, SGU_CHUNK), 1)
        dvn_cols = []
        for g in range(SGU_GROUPS):
            wg = _sgu_weights(wm_ref, g)
            cols = slice(g * SGU_CHUNK, (g + 1) * SGU_CHUNK)
            dvn_rows = []
            dwg = jnp.zeros((SGU_CHUNK, SGU_CHUNK), F32)
            dfs = jnp.zeros((SGU_CHUNK, SGU_CHUNK), F32)
            for ch in range(tm // SGU_CHUNK):
                rows = slice(ch * SGU_CHUNK, (ch + 1) * SGU_CHUNK)
                upre = u_ref[rows, cols]
                dyv = dy_ref[rows, cols].astype(F32)
                f = _dot(wg, vn[rows, cols]) + b_ref[g]
                ug, dug_du = _gelu_and_grad(upre)
                du_ref[rows, cols] = (dyv * f * dug_du).astype(BF)
                df = dyv * ug
                dfb = df.astype(BF)
                dvn_rows.append(_dot_tn(wg, dfb))
                dwg = dwg + _dot_nt(dfb, vn[rows, cols])
                dfs = dfs + df
            dwm_ref[g] += jnp.where(sidx <= t, dwg, 0.0)
            dfsum_ref[g] += dfs
            dvn_cols.append(jnp.concatenate(dvn_rows, axis=0) if len(dvn_rows) > 1 else dvn_rows[0])
        dvn = jnp.concatenate(dvn_cols, axis=1)
        dlw_ref[...] += jnp.sum(dvn * vhat, axis=0, keepdims=True)
        dlb_ref[...] += jnp.sum(dvn, axis=0, keepdims=True)
        dvh = dvn * lw_ref[...]
        dvg = rstd * (dvh - jnp.mean(dvh, axis=-1, keepdims=True) - vhat * jnp.mean(dvh * vhat, axis=-1, keepdims=True))
        dv_ref[...] = (dvg * dvg_dv).astype(BF)

        @pl.when(i == steps - 1)
        def _():
            for g in range(SGU_GROUPS):
                db_ref[g:g + 1, :] = jnp.sum(dfsum_ref[g].T, axis=0, keepdims=True)

    full3 = pl.BlockSpec((SGU_GROUPS, SGU_CHUNK, SGU_CHUNK), lambda i: (0, 0, 0))
    return _call(body, name,
                 (_sds((s, w), BF), _sds((s, w), BF), _sds((SGU_GROUPS, SGU_CHUNK, SGU_CHUNK), F32),
                  _sds((SGU_GROUPS, SGU_CHUNK), F32), _sds((1, w), F32), _sds((1, w), F32)),
                 (steps,),
                 [_rows(tm, w), _rows(tm, w), _vec(w), _vec(w), full3, full3, _rows(tm, w)],
                 (_rows(tm, w), _rows(tm, w), full3, pl.BlockSpec((SGU_GROUPS, SGU_CHUNK), lambda i: (0, 0)), _vec(w), _vec(w)),
                 ("arbitrary",), scratch=[pltpu.VMEM((SGU_GROUPS, SGU_CHUNK, SGU_CHUNK), F32)], after=after)(
                     u_pre, v_pre, ln_w, ln_b, wm, bfull, dy)


def _mix_out(name, y_sgu, y_attn, ga_pre, gb_pre, x, g1, proj_a, proj_b, w_out, w2, sc2, sh2, tm=512):
    s, d = x.shape
    tm = min(tm, s)

    def body(ys_ref, ya_ref, ga_ref, gb_ref, x_ref, g1_ref, wa_ref, wb_ref, wo_ref, w2_ref, sc2_ref, sh2_ref,
             m_ref, pa_ref, pb_ref, o_ref, x1_ref, h2_ref):
        pa = _dot(ys_ref[...], wa_ref[...].reshape(d, d))
        pb = _dot(ya_ref[...], wb_ref[...].reshape(d, d))
        pa_ref[...] = pa.astype(BF)
        pb_ref[...] = pb.astype(BF)
        merged = (_sigmoid(ga_ref[...].astype(F32)) * pa + _sigmoid(gb_ref[...].astype(F32)) * pb).astype(BF)
        m_ref[...] = merged
        o = _dot(merged, wo_ref[...].reshape(d, d))
        o_ref[...] = o.astype(BF)
        x1 = x_ref[...] + g1_ref[...] * o
        x1_ref[...] = x1
        h2_ref[...] = ((x1 * _rms(x1)) * w2_ref[...] * (1.0 + sc2_ref[...]) + sh2_ref[...]).astype(BF)

    f, b = _sds((s, d), F32), _sds((s, d), BF)
    r = _rows(tm, d)
    wspec = _resident(proj_a.shape)
    return _call(body, name, (b, b, b, b, f, b), (s // tm,),
                 [r, r, r, r, r, _vec(d), wspec, wspec, wspec, _vec(d), _vec(d), _vec(d)], (r,) * 6, ("parallel",))(
                     y_sgu, y_attn, ga_pre, gb_pre, x, g1, proj_a, proj_b, w_out, w2, sc2, sh2)


def _mix_bwd(name, do, w_out, proj_a, proj_b, ga_pre, gb_pre, pa, pb, tm=512):
    s, d = do.shape
    tm = min(tm, s)

    def body(do_ref, wo_ref, wa_ref, wb_ref, ga_ref, gb_ref, pa_ref, pb_ref,
             dpa_ref, dpb_ref, dga_ref, dgb_ref, dys_ref, dya_ref):
        dm = _dot_nt(do_ref[...], wo_ref[...].reshape(d, d))
        ga = _sigmoid(ga_ref[...].astype(F32))
        gb = _sigmoid(gb_ref[...].astype(F32))
        dpa = (dm * ga).astype(BF)
        dpb = (dm * gb).astype(BF)
        dpa_ref[...] = dpa
        dpb_ref[...] = dpb
        dga_ref[...] = (dm * pa_ref[...].astype(F32) * ga * (1.0 - ga)).astype(BF)
        dgb_ref[...] = (dm * pb_ref[...].astype(F32) * gb * (1.0 - gb)).astype(BF)
        dys_ref[...] = _dot_nt(dpa, wa_ref[...].reshape(d, d)).astype(BF)
        dya_ref[...] = _dot_nt(dpb, wb_ref[...].reshape(d, d)).astype(BF)

    f, b = _sds((s, d), F32), _sds((s, d), BF)
    r = _rows(tm, d)
    wspec = _resident(w_out.shape)
    return _call(body, name, (b, b, b, b, b, b), (s // tm,), [r, wspec, wspec, wspec, r, r, r, r], (r,) * 6,
                 ("parallel",))(do, w_out, proj_a, proj_b, ga_pre, gb_pre, pa, pb)


def _ffn_up_act(name, h2, w_gate, w_up, cw, cb, tm=1024):
    s, d = h2.shape
    tm = min(tm, s)
    tc = FFN_CHUNK
    per = tm // HALO

    def body(h_ref, hprev_ref, wg_ref, wu_ref, cw_ref, cb_ref, a_ref, ac_ref, up_ref, hf_ref):
        hv = h_ref[...]
        wg = wg_ref[...].reshape(tc, d)
        a = _dot_nt(hv, wg).astype(BF)
        up = _dot_nt(hv, wu_ref[...].reshape(tc, d)).astype(BF)
        a_ref[...] = a
        up_ref[...] = up
        prev = jnp.where(pl.program_id(1) > 0, _dot_nt(hprev_ref[...], wg).astype(BF).astype(F32), 0.0)
        ext = jnp.concatenate([prev, a.astype(F32)], axis=0)
        ac = (cb_ref[...] + cw_ref[0:1, :] * pltpu.roll(ext, 2, axis=0) + cw_ref[1:2, :] * pltpu.roll(ext, 1, axis=0)
              + cw_ref[2:3, :] * ext)[HALO:]
        ac_ref[...] = ac.astype(BF)
        hf_ref[...] = (ac * _sigmoid(ac) * up.astype(F32)).astype(BF)

    wspec = pl.BlockSpec((CHIPS_PER_CHUNK, FFN_SHARD, d), lambda j, i: (j, 0, 0))
    ospec = pl.BlockSpec((tm, tc), lambda j, i: (i, j))
    o = _sds((s, FFN_DIM), BF)
    return _call(body, name, (o, o, o, o), (FFN_CHUNKS, s // tm),
                 [pl.BlockSpec((tm, d), lambda j, i: (i, 0)), pl.BlockSpec((HALO, d), lambda j, i: (jnp.maximum(i * per - 1, 0), 0)),
                  wspec, wspec, pl.BlockSpec((3, tc), lambda j, i: (0, j)), pl.BlockSpec((1, tc), lambda j, i: (0, j))],
                 (ospec, ospec, ospec, ospec), ("parallel", "parallel"))(h2, h2, w_gate, w_up, cw, cb)


def _ffn_down(name, hf, w_down, x1, g2, tm=512):
    s, d = x1.shape
    tm = min(tm, s)

    def body(hf_ref, wd_ref, x1_ref, g2_ref, dn_ref, x2_ref):
        dn = _dot(hf_ref[...], wd_ref[...].reshape(FFN_DIM, d))
        dn_ref[...] = dn.astype(BF)
        x2_ref[...] = x1_ref[...] + g2_ref[...] * dn

    return _call(body, name, (_sds((s, d), BF), _sds((s, d), F32)), (s // tm,),
                 [_rows(tm, FFN_DIM), _resident(w_down.shape), _rows(tm, d), _vec(d)],
                 (_rows(tm, d), _rows(tm, d)), ("parallel",))(hf, w_down, x1, g2)


def _ffn_down_bwd_act(name, dx2, dn, g2, w_down, a, ac, up, cw, tm=256, after=None):
    s, d = dx2.shape
    c = a.shape[1]
    tm = min(tm, s)
    tc = FFN_CHUNK
    per = tm // HALO
    steps = s // tm
    last = s // HALO - 1
    n = tm + HALO

    def body(dx_ref, dxnext_ref, dn_ref, g2_ref, wd_ref, a_ref, ac_ref, acnext_ref, up_ref, upnext_ref, cw_ref,
             ddn_ref, da_ref, dup_ref, dg_ref, dcw_ref, dcb_ref):
        i = pl.program_id(0)

        @pl.when(i == 0)
        def _():
            dg_ref[...] = jnp.zeros_like(dg_ref)
            dcw_ref[...] = jnp.zeros_like(dcw_ref)
            dcb_ref[...] = jnp.zeros_like(dcb_ref)

        dxv = dx_ref[...]
        ddn = (dxv * g2_ref[...]).astype(BF)
        ddn_ref[...] = ddn
        dg_ref[...] += jnp.sum(dxv * dn_ref[...].astype(F32), axis=0, keepdims=True)
        ddn_next = jnp.where(i < steps - 1, dxnext_ref[...] * g2_ref[...], 0.0).astype(BF)
        ddn_ext = jnp.concatenate([ddn, ddn_next], axis=0)
        for k in range(FFN_CHUNKS):
            cols = slice(k * tc, (k + 1) * tc)
            dh = _dot_nt(ddn_ext, wd_ref[k * CHIPS_PER_CHUNK:(k + 1) * CHIPS_PER_CHUNK].reshape(tc, d))
            ace = jnp.concatenate([ac_ref[:, cols].astype(F32), acnext_ref[:, cols].astype(F32)], axis=0)
            upe = jnp.concatenate([up_ref[:, cols].astype(F32), upnext_ref[:, cols].astype(F32)], axis=0)
            sig = _sigmoid(ace)
            silu = ace * sig
            dac = dh * upe * (sig + silu * (1.0 - sig))
            dup_ref[:, cols] = (dh[:tm] * silu[:tm]).astype(BF)
            d1 = pltpu.roll(dac, n - 1, axis=0)[:tm]
            d2 = pltpu.roll(dac, n - 2, axis=0)[:tm]
            d0 = dac[:tm]
            da_ref[:, cols] = (cw_ref[2:3, cols] * d0 + cw_ref[1:2, cols] * d1 + cw_ref[0:1, cols] * d2).astype(BF)
            a0 = a_ref[:, cols].astype(F32)
            dcb_ref[:, cols] += jnp.sum(d0, axis=0, keepdims=True)
            dcw_ref[0:1, cols] += jnp.sum(d2 * a0, axis=0, keepdims=True)
            dcw_ref[1:2, cols] += jnp.sum(d1 * a0, axis=0, keepdims=True)
            dcw_ref[2:3, cols] += jnp.sum(d0 * a0, axis=0, keepdims=True)

    nxt = lambda width: pl.BlockSpec((HALO, width), lambda i: (jnp.minimum((i + 1) * per, last), 0))
    wide = _sds((s, c), BF)
    return _call(body, name, (_sds((s, d), BF), wide, wide, _sds((1, d), F32), _sds((3, c), F32), _sds((1, c), F32)), (steps,),
                 [_rows(tm, d), nxt(d), _rows(tm, d), _vec(d), _resident(w_down.shape), _rows(tm, c), _rows(tm, c), nxt(c),
                  _rows(tm, c), nxt(c), pl.BlockSpec((3, c), lambda i: (0, 0))],
                 (_rows(tm, d), _rows(tm, c), _rows(tm, c), _vec(d), pl.BlockSpec((3, c), lambda i: (0, 0)), _vec(c)),
                 ("arbitrary",), after=after)(dx2, dx2, dn, g2, w_down, a, ac, ac, up, up, cw)


def _ffn_up_bwd(name, da, dup, w_gate, w_up, x1, dx2, w2, sc2, o, g1, tm=512, after=None):
    s, d = x1.shape
    tm = min(tm, s)

    def body(da_ref, dup_ref, wg_ref, wu_ref, x1_ref, dx2_ref, w2_ref, sc2_ref, o_ref, g1_ref,
             dx1_ref, do_ref, dnw_ref, dsh_ref, dg1_ref):
        @pl.when(pl.program_id(0) == 0)
        def _():
            dnw_ref[...] = jnp.zeros_like(dnw_ref)
            dsh_ref[...] = jnp.zeros_like(dsh_ref)
            dg1_ref[...] = jnp.zeros_like(dg1_ref)

        dh = _dot(da_ref[...], wg_ref[...].reshape(FFN_DIM, d)) + _dot(dup_ref[...], wu_ref[...].reshape(FFN_DIM, d))
        xv = x1_ref[...]
        r = _rms(xv)
        xn = xv * r
        dxn = dh * (w2_ref[...] * (1.0 + sc2_ref[...]))
        dx1 = dx2_ref[...] + r * (dxn - xn * jnp.mean(dxn * xn, axis=-1, keepdims=True))
        dx1_ref[...] = dx1
        dnw_ref[...] += jnp.sum(dh * xn, axis=0, keepdims=True)
        dsh_ref[...] += jnp.sum(dh, axis=0, keepdims=True)
        do_ref[...] = (dx1 * g1_ref[...]).astype(BF)
        dg1_ref[...] += jnp.sum(dx1 * o_ref[...].astype(F32), axis=0, keepdims=True)

    v = _sds((1, d), F32)
    r = _rows(tm, d)
    wspec = _resident(w_gate.shape)
    return _call(body, name, (_sds((s, d), F32), _sds((s, d), BF), v, v, v), (s // tm,),
                 [_rows(tm, FFN_DIM), _rows(tm, FFN_DIM), wspec, wspec, r, r, _vec(d), _vec(d), r, _vec(d)],
                 (r, r, _vec(d), _vec(d), _vec(d)), ("arbitrary",), after=after)(da, dup, w_gate, w_up, x1, dx2, w2, sc2, o, g1)


def _loss_head(name, x, w, target, tm=512):
    s, d = x.shape
    tm = min(tm, s)

    def body(x_ref, w_ref, t_ref, dx_ref, loss_ref, dw_ref):
        @pl.when(pl.program_id(0) == 0)
        def _():
            loss_ref[...] = jnp.zeros_like(loss_ref)
            dw_ref[...] = jnp.zeros_like(dw_ref)

        xv = x_ref[...]
        r = _rms(xv)
        xn = xv * r
        err = xn * w_ref[...] - t_ref[...]
        loss_ref[...] += 0.5 * jnp.sum(jnp.mean(err * err, axis=-1, keepdims=True))
        dy = err * (1.0 / d)
        dw_ref[...] += jnp.sum(dy * xn, axis=0, keepdims=True)
        dxn = dy * w_ref[...]
        dx_ref[...] = r * (dxn - xn * jnp.mean(dxn * xn, axis=-1, keepdims=True))

    return _call(body, name, (_sds((s, d), F32), _sds((1, LANES), F32), _sds((1, d), F32)), (s // tm,),
                 [_rows(tm, d), _vec(d), _rows(tm, d)], (_rows(tm, d), _vec(LANES), _vec(d)), ("arbitrary",))(x, w, target)


def _layer_fwd(l, x, mod, p, cosf, sinf, after=None, late=None, later=None):
    sh1, sc1, g1, sh2, sc2, g2 = mod
    tag = f"l{l}_"
    h, qr, kk0, kk1, vv0, vv1, u_pre, v_pre, ga_pre, gb_pre = _in_proj(
        tag + "in_proj", x, p["norm1_w"], sc1, sh1, p["w_in"], cosf, sinf, after=after)
    y_attn = _attention(tag + "attn", qr, kk0, kk1, vv0, vv1, p["sinks"])
    y_sgu = _sgu(tag + "sgu", u_pre, v_pre, p["sgu_ln_w"], p["sgu_ln_b"], p["sgu_w"], p["sgu_bfull"])
    if late is not None:
        p = dict(p, **late(y_sgu))
    merged, pa, pb, o, x1, h2 = _mix_out(tag + "mix_out", y_sgu, y_attn, ga_pre, gb_pre, x, g1, p["proj_a"], p["proj_b"],
                                         p["w_out"], p["norm2_w"], sc2, sh2)
    if later is not None:
        p = dict(p, **later(h2))
    a, ac, up, hf = _ffn_up_act(tag + "ffn_up", h2, p["w_gate"], p["w_up"], p["conv_w"], p["conv_b"])
    dn, x2 = _ffn_down(tag + "ffn_down", hf, p["w_down"], x1, g2)
    saved = dict(x=x, h=h, qr=qr, kk0=kk0, kk1=kk1, vv0=vv0, vv1=vv1, u_pre=u_pre, v_pre=v_pre, ga_pre=ga_pre,
                 gb_pre=gb_pre, y_attn=y_attn, y_sgu=y_sgu, merged=merged, pa=pa, pb=pb, o=o, x1=x1, h2=h2, a=a, ac=ac, up=up,
                 hf=hf, dn=dn)
    return x2, saved, p


def _layer_bwd(l, dx2, mod, p, sv, cosf, sinf, after=None, emit=None, tick=None):
    sh1, sc1, g1, sh2, sc2, g2 = mod
    tag = f"l{l}_b_"
    d = D_MODEL
    g = {}
    ready = (lambda names: emit({k: g.pop(k) for k in names})) if emit else (lambda names: None)
    tick = tick or (lambda y: None)
    ddn, da, dup, dg2, g["conv_w"], g["conv_b"] = _ffn_down_bwd_act(
        tag + "ffn_down", dx2, sv["dn"], g2, p["w_down"], sv["a"], sv["ac"], sv["up"], p["conv_w"], after=after)
    g["w_down"] = _matmul_tn(tag + "dw_down", sv["hf"], ddn, tk=FFN_CHUNK, after=tick(ddn)).reshape(N_CHIPS, FFN_SHARD, d)
    g["w_gate"] = _matmul_tn(tag + "dw_gate", da, sv["h2"], tk=FFN_CHUNK).reshape(N_CHIPS, FFN_SHARD, d)
    g["w_up"] = _matmul_tn(tag + "dw_up", dup, sv["h2"], tk=FFN_CHUNK).reshape(N_CHIPS, FFN_SHARD, d)
    dx1, do, da2, dsh2, dg1 = _ffn_up_bwd(tag + "ffn_up", da, dup, p["w_gate"], p["w_up"], sv["x1"], dx2, p["norm2_w"],
                                          sc2, sv["o"], g1, after=ready(("w_down", "w_gate", "w_up")))
    g["norm2_w"] = da2 * (1.0 + sc2)
    dsc2 = da2 * p["norm2_w"]
    g["w_out"] = _matmul_tn(tag + "dw_out", sv["merged"], do, after=tick(do)).reshape(N_CHIPS, d // N_CHIPS, d)
    dpa, dpb, dga, dgb, dy_sgu, dy_attn = _mix_bwd(tag + "mix", do, p["w_out"], p["proj_a"], p["proj_b"], sv["ga_pre"],
                                                  sv["gb_pre"], sv["pa"], sv["pb"])
    g["proj_a"] = _matmul_tn(tag + "dproj_a", sv["y_sgu"], dpa).reshape(N_CHIPS, d // N_CHIPS, d)
    g["proj_b"] = _matmul_tn(tag + "dproj_b", sv["y_attn"], dpb).reshape(N_CHIPS, d // N_CHIPS, d)
    du, dv, g["sgu_w"], g["sgu_b"], g["sgu_ln_w"], g["sgu_ln_b"] = _sgu_bwd(
        tag + "sgu", sv["u_pre"], sv["v_pre"], p["sgu_ln_w"], p["sgu_ln_b"], p["sgu_w"], p["sgu_bfull"], dy_sgu,
        after=ready(("w_out", "proj_a", "proj_b")))
    dqr, dkv_cur, dkv_prev, dsink = _attention_bwd(tag + "attn", sv["qr"], sv["kk0"], sv["kk1"], sv["vv0"], sv["vv1"],
                                                   p["sinks"], dy_attn, after=tick(du))
    g["sinks"] = dsink[0, :N_Q_HEADS]
    dq, dkv = _rope_bwd(tag + "rope", dqr, dkv_cur, dkv_prev, cosf, sinf)
    dw_in, row0 = None, 0
    for n, t in (("q", dq), ("kv", dkv), ("u", du), ("v", dv), ("ga", dga), ("gb", dgb)):
        dw_in = _matmul_tn_rows(tag + "dw_in_" + n, t, sv["h"], dw_in, row0, IN_COLS)
        row0 += t.shape[1]
    g["w_in"] = dw_in.reshape(N_CHIPS, IN_COLS // N_CHIPS, d)
    dx, da1, dsh1 = _in_proj_bwd(tag + "in_proj", dq, dkv, du, dv, dga, dgb, p["w_in"], sv["x"], p["norm1_w"], sc1, dx1)
    g["norm1_w"] = da1 * (1.0 + sc1)
    dsc1 = da1 * p["norm1_w"]
    return dx, (dsh1, dsc1, dg1, dsh2, dsc2, dg2), g


def _pad_to(a, axis, size):
    pad = [(0, 0)] * a.ndim
    pad[axis] = (0, size - a.shape[axis])
    return jnp.pad(a, pad)


def _early_params(w_in, small):
    d = D_MODEL
    return dict(
        w_in=w_in.reshape(IN_COLS, d), norm1_w=small["norm1_w"].reshape(1, d), sinks=small["sinks"],
        sgu_ln_w=small["sgu_ln_w"].reshape(1, d), sgu_ln_b=small["sgu_ln_b"].reshape(1, d), sgu_w=small["sgu_w"],
        sgu_bfull=jnp.broadcast_to(small["sgu_b"][:, :, None], (SGU_GROUPS, SGU_CHUNK, SGU_CHUNK)))


def _mix_params(proj_a, proj_b, w_out, small):
    return dict(proj_a=proj_a, proj_b=proj_b, w_out=w_out, norm2_w=small["norm2_w"].reshape(1, D_MODEL))


def _ffn_params(w_gate, w_up, w_down, conv_w, small):
    return dict(
        w_gate=w_gate, w_up=w_up, w_down=w_down, conv_w=conv_w.transpose(1, 0, 2).reshape(3, FFN_DIM),
        conv_b=small["conv_b"].reshape(1, FFN_DIM))


def _layer_params(w_in, proj_a, proj_b, w_out, w_gate, w_up, w_down, conv_w, small):
    return dict(_early_params(w_in, small), **_mix_params(proj_a, proj_b, w_out, small),
                **_ffn_params(w_gate, w_up, w_down, conv_w, small))


def _conv_grads_natural(g):
    cw = g["conv_w"]
    cb = g["conv_b"].reshape(FFN_DIM)
    return cw, cb


def _rope_tables(positions):
    inv_freq = ROPE_THETA ** (-jnp.arange(0, ROT_DIM, 2, dtype=F32) / ROT_DIM)
    ang = positions.astype(F32)[:, None] * inv_freq
    cos, sin = jnp.cos(ang), jnp.sin(ang)
    s = positions.shape[0]
    rest = HEAD_DIM - ROT_DIM
    cos_head = jnp.concatenate([cos, cos, jnp.ones((s, rest), F32)], axis=1)
    sin_head = jnp.concatenate([-sin, sin, jnp.zeros((s, rest), F32)], axis=1)
    return jnp.tile(cos_head, (1, LANES // HEAD_DIM)), jnp.tile(sin_head, (1, LANES // HEAD_DIM))


ADA_ROWS = 16


def _ada_fwd(name, c_rows, ada_w, ada_b_cols, tn=512):
    depth, d, n = ada_w.shape

    def body(c_ref, w_ref, b_ref, o_ref):
        cv = c_ref[...]
        act = (cv * _sigmoid(cv)).astype(BF)
        o_ref[0] = _dot(act, w_ref[0].astype(BF)) + b_ref[0]

    return _call(body, name, _sds((depth, ADA_ROWS, n), F32), (depth, n // tn),
                 [pl.BlockSpec((ADA_ROWS, d), lambda l, j: (0, 0)), pl.BlockSpec((1, d, tn), lambda l, j: (l, 0, j)),
                  pl.BlockSpec((1, 1, tn), lambda l, j: (l, 0, j))],
                 pl.BlockSpec((1, ADA_ROWS, tn), lambda l, j: (l, 0, j)), ("parallel", "parallel"))(c_rows, ada_w, ada_b_cols)


def _ada_bwd(name, c_rows, dmod_cols, tn=512):
    depth, _, n = dmod_cols.shape
    d = c_rows.shape[1]

    def body(c_ref, dm_ref, o_ref):
        cv = c_ref[...]
        act = (cv * _sigmoid(cv)).astype(BF)
        o_ref[0] = _dot_tn(act, dm_ref[0].astype(BF))

    return _call(body, name, _sds((depth, d, n), F32), (depth, n // tn),
                 [pl.BlockSpec((ADA_ROWS, d), lambda l, j: (0, 0)), pl.BlockSpec((1, ADA_ROWS, tn), lambda l, j: (l, 0, j))],
                 pl.BlockSpec((1, d, tn), lambda l, j: (l, 0, j)), ("parallel", "parallel"))(c_rows, dmod_cols)


def _colsum(name, a):
    r, n = a.shape

    def body(a_ref, o_ref):
        o_ref[...] = jnp.sum(a_ref[...], axis=0, keepdims=True)

    return _call(body, name, _sds((1, n), F32), (1,), [pl.BlockSpec((r, n), lambda i: (0, 0))],
                 pl.BlockSpec((1, n), lambda i: (0, 0)), ("arbitrary",))(a)


REL_SIBLING = (0, 0, 1)
REL_CHIPS = ((1, 0, 0), (0, 1, 0), (1, 1, 0))
REL_ALL = tuple((fx, fy, fc) for fx in (0, 1) for fy in (0, 1) for fc in (0, 1) if fx or fy or fc)


def _chip_of(dev):
    return 2 * dev[0] + dev[1]


def _dev_of(dev):
    return 4 * dev[0] + 2 * dev[1] + dev[2]


def _flip(dev, rel):
    return tuple(1 - m if f else m for m, f in zip(dev, rel))


def _exchange(name, arrays, n_out, stages, aliases=None):
    out_shapes, stages = stages[0], stages[1:]
    n_in = len(arrays)
    aliases = aliases or {}
    n_remote = sum(len(plan) for plan, _ in stages)
    n_local = sum(len(local) for _, local in stages)

    def at(ref, idx):
        return ref.at[idx] if len(idx) else ref

    def body(*refs):
        bufs = list(refs[:n_in + n_out])
        for i_in, i_out in aliases.items():
            bufs[i_in] = bufs[n_in + i_out]
        send_sems, recv_sems, local_sems = refs[n_in + n_out:]
        me = (lax.axis_index("x"), lax.axis_index("y"), lax.axis_index("c"))
        base_r = base_l = 0
        pending = []
        for plan, local in stages:
            def remote(k, entry, sender, receiver):
                rel, si, ssel, di, dsel = entry
                return pltpu.make_async_remote_copy(
                    src_ref=at(bufs[si], ssel(sender, receiver)), dst_ref=at(bufs[di], dsel(sender, receiver)),
                    send_sem=send_sems.at[k], recv_sem=recv_sems.at[k], device_id=_flip(me, rel), device_id_type=MESH)

            sends = [remote(base_r + k, e, me, _flip(me, e[0])) for k, e in enumerate(plan)]
            for cp in sends:
                cp.start()
            for k, (si, ssel, di, dsel) in enumerate(local):
                cp = pltpu.make_async_copy(at(bufs[si], ssel(me)), at(bufs[di], dsel(me)), local_sems.at[base_l + k])
                cp.start()
                pending.append(cp.wait)
            for k, e in enumerate(plan):
                remote(base_r + k, e, _flip(me, e[0]), me).wait_recv()
            pending += [cp.wait_send for cp in sends]
            base_r += len(plan)
            base_l += len(local)
        for wait in pending:
            wait()

    any_spec = pl.BlockSpec(memory_space=pl.ANY)
    return pl.pallas_call(
        body, name=name, out_shape=tuple(out_shapes), in_specs=[any_spec] * n_in, out_specs=tuple([any_spec] * n_out),
        input_output_aliases=dict(aliases),
        scratch_shapes=[pltpu.SemaphoreType.DMA((max(n_remote, 1),)), pltpu.SemaphoreType.DMA((max(n_remote, 1),)),
                        pltpu.SemaphoreType.DMA((max(n_local, 1),))])(*arrays)


HBM_SPEC = pl.BlockSpec(memory_space=pltpu.HBM)
SEM_SPEC = pl.BlockSpec(memory_space=pltpu.SEMAPHORE)


def _split_copies(bufs, plan, local, send_sems, recv_sems, local_sems):
    me = (lax.axis_index("x"), lax.axis_index("y"), lax.axis_index("c"))

    def at(ref, idx):
        return ref.at[idx] if len(idx) else ref

    def remote(k, sender, receiver):
        rel, si, ssel, di, dsel = plan[k]
        return pltpu.make_async_remote_copy(
            src_ref=at(bufs[si], ssel(sender, receiver)), dst_ref=at(bufs[di], dsel(sender, receiver)),
            send_sem=send_sems.at[k], recv_sem=recv_sems.at[k], device_id=_flip(me, rel), device_id_type=MESH)

    sends = [remote(k, me, _flip(me, plan[k][0])) for k in range(len(plan))]
    arrivals = [remote(k, _flip(me, plan[k][0]), me) for k in range(len(plan))]
    locs = [pltpu.make_async_copy(at(bufs[si], ssel(me)), at(bufs[di], dsel(me)), local_sems.at[k])
            for k, (si, ssel, di, dsel) in enumerate(local)]
    return sends, arrivals, locs


def _exchange_start(name, arrays, out_shapes, plan, local, inplace=False):
    n_in, n_out = len(arrays), len(out_shapes)
    n_buf = n_in + n_out

    def body(*refs):
        sems = refs[n_buf:n_buf + 3]
        bufs = refs[n_buf + 3:2 * n_buf + 3]
        sends, _, locs = _split_copies(bufs * 2 if inplace else bufs, plan, local, *sems)
        for cp in sends + locs:
            cp.start()
        refs[-1][...] = jnp.zeros_like(refs[-1])

    zones = [lax.empty(o.shape, o.dtype) for o in out_shapes]
    operands = [pltpu.with_memory_space_constraint(a, pltpu.HBM) for a in list(arrays) + zones]
    sem = lambda n: pltpu.SemaphoreType.DMA((max(n, 1),))
    out = pl.pallas_call(
        body, name=name,
        out_shape=(sem(len(plan)), sem(len(plan)), sem(len(local)), *[pltpu.HBM(a.shape, a.dtype) for a in operands],
                   _sds((SUBLANES, LANES), F32)),
        in_specs=[HBM_SPEC] * n_buf,
        out_specs=(SEM_SPEC, SEM_SPEC, SEM_SPEC, *[HBM_SPEC] * n_buf, pl.BlockSpec(memory_space=pltpu.VMEM)),
        input_output_aliases={i: 3 + i for i in range(n_buf)},
        compiler_params=pltpu.CompilerParams(has_side_effects=pltpu.SideEffectType.DATAFLOW_SIDE_EFFECTING))(*operands)
    pending = dict(name=name, sems=out[:3], thru=out[3:3 + n_in], zones=out[3 + n_in:3 + n_buf], plan=plan, local=local,
                   inplace=inplace)
    return pending, out[-1]


def _exchange_wait(pending, after, both=False):
    thru, zones, plan, local, inplace = (pending[k] for k in ("thru", "zones", "plan", "local", "inplace"))
    n_in, n_buf = len(thru), len(thru) + len(zones)

    def body(*refs):
        bufs = refs[:n_buf]
        sends, arrivals, locs = _split_copies(bufs * 2 if inplace else bufs, plan, local, *refs[n_buf:n_buf + 3])
        for cp in arrivals:
            cp.wait_recv()
        for cp in sends:
            cp.wait_send()
        for cp in locs:
            cp.wait()

    out = pl.pallas_call(
        body, name=pending["name"] + "_wait", out_shape=tuple(pltpu.HBM(a.shape, a.dtype) for a in list(thru) + list(zones)),
        in_specs=[HBM_SPEC] * n_buf + [SEM_SPEC] * 3 + [pl.BlockSpec(memory_space=pl.ANY)],
        out_specs=tuple([HBM_SPEC] * n_buf), input_output_aliases={i: i for i in range(n_buf)},
        compiler_params=pltpu.CompilerParams(has_side_effects=pltpu.SideEffectType.DATAFLOW_SIDE_EFFECTING))(
            *thru, *zones, *pending["sems"], after)
    if both:
        return out[:n_in], out[n_in:]
    return out[:n_in] if inplace else out[n_in:]


def _whole(*_):
    return ()


def _half_rows(rows, core):
    return pl.ds(core * (rows // 2), rows // 2)


def _gather_weights_plan(shards):
    n = len(shards)
    dsts = [_sds((N_CHIPS,) + a.shape, a.dtype) for a in shards]
    fetch, forward = [], []
    for t, a in enumerate(shards):
        rows = a.shape[0]
        if rows % (2 * 16) == 0:
            fetch += [(rel, t, (lambda s_, r_, rows=rows: (_half_rows(rows, s_[2]),)), n + t,
                       (lambda s_, r_, rows=rows: (_chip_of(s_), _half_rows(rows, s_[2])))) for rel in REL_CHIPS]
            forward += [(REL_SIBLING, n + t, (lambda s_, r_, rows=rows, rel=rel: (_chip_of(_flip(s_, rel)), _half_rows(rows, s_[2]))),
                         n + t, (lambda s_, r_, rows=rows, rel=rel: (_chip_of(_flip(s_, rel)), _half_rows(rows, s_[2]))))
                        for rel in REL_CHIPS]
        else:
            fetch += [(rel, t, _whole, n + t, lambda s_, r_: (_chip_of(s_),)) for rel in REL_CHIPS]
    local = [(t, _whole, n + t, lambda me: (_chip_of(me),)) for t in range(n)]
    return dsts, fetch, local, forward


def _gather_weights_start(name, shards):
    dsts, fetch, local, forward = _gather_weights_plan(shards)
    pending, token = _exchange_start(name, shards, dsts, fetch, local)
    return dict(pending, forward=forward), token


def _gather_weights_finish(pending, after):
    landed = _exchange_wait(pending, after)
    n = len(landed)
    return _exchange(pending["name"] + "_forward", landed, n, [[_sds(a.shape, a.dtype) for a in landed], (pending["forward"], [])],
                     aliases={t: t for t in range(n)})


def _gather_chips(name, arrays):
    n = len(arrays)
    dsts = [_sds((N_CHIPS,) + a.shape, a.dtype) for a in arrays]
    plan = [(rel, t, _whole, n + t, lambda s_, r_: (_chip_of(s_),)) for t in range(n) for rel in REL_CHIPS]
    local = [(t, _whole, n + t, lambda me: (_chip_of(me),)) for t in range(n)]
    return _exchange(name, arrays, n, [dsts, (plan, local)])


def _gather_all(name, a):
    plan = [(rel, 0, _whole, 1, lambda s_, r_: (_dev_of(s_),)) for rel in REL_ALL]
    local = [(0, _whole, 1, lambda me: (_dev_of(me),))]
    return _exchange(name, [a], 1, [[_sds((2 * N_CHIPS,) + a.shape, a.dtype)], (plan, local)])[0]


def _swap_halves_start(name, grads):
    n = len(grads)
    dsts = [_sds((g.shape[0], g.shape[1] // 2, g.shape[2]), g.dtype) for g in grads]
    plan = [(REL_SIBLING, t, (lambda s_, r_, rows=g.shape[1]: (pl.ds(0, N_CHIPS), _half_rows(rows, r_[2]))), n + t, _whole)
            for t, g in enumerate(grads)]
    return _exchange_start(name, grads, dsts, plan, [])


def _scatter_chips_plan(sums):
    n = len(sums)
    dsts = [_sds(a.shape, a.dtype) for a in sums]
    plan = [(rel, t, lambda s_, r_: (_chip_of(r_),), n + t, lambda s_, r_: (_chip_of(s_),))
            for t in range(n) for rel in REL_CHIPS]
    local = [(t, lambda me: (_chip_of(me),), n + t, lambda me: (_chip_of(me),)) for t in range(n)]
    return dsts, plan, local


def _scatter_chips(name, sums):
    dsts, plan, local = _scatter_chips_plan(sums)
    return _exchange(name, sums, len(sums), [dsts, (plan, local)])


def _scatter_chips_start(name, sums):
    dsts, plan, local = _scatter_chips_plan(sums)
    return _exchange_start(name, sums, dsts, plan, local)


def _swap_back_start(name, totals, layer):
    n = len(totals)
    plan = [(REL_SIBLING, n + t, (lambda s_, r_, rows=a.shape[1]: (layer, _half_rows(rows, s_[2]))),
             n + t, (lambda s_, r_, rows=a.shape[1]: (layer, _half_rows(rows, s_[2])))) for t, a in enumerate(totals)]
    return _exchange_start(name, totals, [], plan, [], inplace=True)


def _add_halves(name, g, recv, core):
    nch, half, c = recv.shape

    def body(core_ref, g_ref, r_ref, o_ref):
        o_ref[0] = (g_ref[0, 0].astype(F32) + r_ref[0].astype(F32)).astype(o_ref.dtype)

    spec = pltpu.PrefetchScalarGridSpec(
        num_scalar_prefetch=1, grid=(nch,),
        in_specs=[pl.BlockSpec((1, 1, half, c), lambda k, core_ref: (k, core_ref[0], 0, 0)),
                  pl.BlockSpec((1, half, c), lambda k, core_ref: (k, 0, 0))],
        out_specs=pl.BlockSpec((1, half, c), lambda k, core_ref: (k, 0, 0)))
    return pl.pallas_call(body, name=name, out_shape=_sds(recv.shape, recv.dtype), grid_spec=spec,
                          compiler_params=pltpu.CompilerParams(dimension_semantics=("parallel",),
                                                               vmem_limit_bytes=VMEM_LIMIT))(
                                                                   core, g.reshape(nch, 2, half, c), recv)


def _sum_chips(name, a, core, layer, total):
    nch, half, c = a.shape

    def body(core_ref, a_ref, *rest):
        o_ref = rest[-1]
        acc = a_ref[0].astype(F32)
        for k in range(1, nch):
            acc = acc + a_ref[k].astype(F32)
        o_ref[0, 0] = acc

    in_specs = [pl.BlockSpec((nch, half, c), lambda i, core_ref: (0, 0, 0))]
    args = [core, a]
    if total is not None:
        in_specs.append(pl.BlockSpec(memory_space=pl.ANY))
        args.append(total.reshape(DEPTH, 2, half, c))
    spec = pltpu.PrefetchScalarGridSpec(
        num_scalar_prefetch=1, grid=(1,), in_specs=in_specs,
        out_specs=pl.BlockSpec((1, 1, half, c), lambda i, core_ref: (layer, core_ref[0], 0, 0)))
    out = pl.pallas_call(body, name=name, out_shape=_sds((DEPTH, 2, half, c), F32), grid_spec=spec,
                         input_output_aliases={2: 0} if total is not None else {},
                         compiler_params=pltpu.CompilerParams(dimension_semantics=("arbitrary",),
                                                              vmem_limit_bytes=VMEM_LIMIT))(*args)
    return out.reshape(DEPTH, 2 * half, c)


def _adamw_update(w, g, m, v):
    mn = ADAM_B1 * m + (1.0 - ADAM_B1) * g
    vn = ADAM_B2 * v + (1.0 - ADAM_B2) * (g * g)
    m_hat = mn / (1.0 - ADAM_B1 ** ADAM_STEP)
    v_hat = vn / (1.0 - ADAM_B2 ** ADAM_STEP)
    return -ADAM_LR * (m_hat / (jnp.sqrt(v_hat) + ADAM_EPS) + ADAM_WD * w), mn, vn


def _adamw(name, w, g, m, v):
    depth, r, c = w.shape
    tr = next(t for t in (512, 448, 384, 352, 336, 256, 192, 128, 64, 32, 16, 8) if r % t == 0 and t * c <= ADAM_TILE_ELEMS)

    def body(w_ref, g_ref, m_ref, v_ref, go_ref, d_ref, mo_ref, vo_ref):
        gv = g_ref[...]
        go_ref[...] = gv
        d_ref[...], mo_ref[...], vo_ref[...] = _adamw_update(w_ref[...], gv, m_ref[...], v_ref[...])

    spec = pl.BlockSpec((1, tr, c), lambda l, i: (l, i, 0))
    o = _sds(w.shape, F32)
    return _call(body, name, (o, o, o, o), (depth, r // tr), [spec] * 4, (spec,) * 4, ("parallel", "parallel"))(w, g, m, v)


def _adamw_small(name, ws, gs, ms, vs):
    n = len(ws)

    def body(*refs):
        for t in range(n):
            w_ref, g_ref, m_ref, v_ref = (refs[k * n + t] for k in range(4))
            d_ref, mo_ref, vo_ref = (refs[(4 + k) * n + t] for k in range(3))
            d_ref[...], mo_ref[...], vo_ref[...] = _adamw_update(w_ref[...], g_ref[...], m_ref[...], v_ref[...])

    outs = [_sds(w.shape, F32) for w in ws]
    res = pl.pallas_call(body, name=name, out_shape=tuple(outs * 3))(*ws, *gs, *ms, *vs)
    return res[:n], res[n:2 * n], res[2 * n:]


def _pack(arrays, rows):
    flat = jnp.concatenate([a.reshape(-1).astype(F32) for a in arrays])
    return _pad_to(flat, 0, rows * LANES).reshape(rows, LANES)


def _unpack(packed, shapes):
    flat = packed.reshape(-1)
    out, off = [], 0
    for shp in shapes:
        n = 1
        for s_ in shp:
            n *= s_
        out.append(flat[off:off + n].reshape(shp))
        off += n
    return out


_MATRICES = ("w_in", "proj_a", "proj_b", "w_out", "w_gate", "w_up", "w_down")
_SMALL = (("norm1_w", (D_MODEL,)), ("sinks", (N_Q_HEADS,)), ("sgu_ln_w", (SGU_WIDTH,)), ("sgu_ln_b", (SGU_WIDTH,)),
          ("sgu_w", (SGU_GROUPS, SGU_CHUNK, SGU_CHUNK)), ("sgu_b", (SGU_GROUPS, SGU_CHUNK)), ("norm2_w", (D_MODEL,)),
          ("conv_w", (3, FFN_DIM)), ("conv_b", (FFN_DIM,)), ("final_norm_w", (D_MODEL,)))
SMALL_ROWS = 320
ADAM_TILE_ELEMS = 384 * 1024


def _reduce_cores_start(tag, partial):
    names = list(partial)
    pending, token = _swap_halves_start(tag + "_cores", [partial[k] for k in names])
    return dict(pending, tag=tag, names=names), token


def _reduce_chips_start(pending, core, after):
    tag, names = pending["tag"], pending["names"]
    mine, theirs = _exchange_wait(pending, after, both=True)
    sums = [_add_halves(f"{tag}_cores_add_{k}", g, r, core) for k, g, r in zip(names, mine, theirs)]
    scatter, token = _scatter_chips_start(tag + "_chips", sums)
    return dict(scatter, tag=tag, names=names), token


def _reduce_back_start(pending, l, core, totals, after):
    tag, names = pending["tag"], pending["names"]
    sums = [_sum_chips(f"{tag}_chips_add_{k}", a, core, l, totals.get(k)) for k, a in zip(names, _exchange_wait(pending, after))]
    back, _ = _swap_back_start(tag + "_back", sums, l)
    return dict(back, names=names)


def kernel(x, c, positions, ada_w, ada_b, norm1_w, w_in, attn_sinks, sgu_ln_w, sgu_ln_b, sgu_w, sgu_b, proj_a, proj_b, w_out, norm2_w, ffn_w_gate, ffn_w_up, ffn_conv_w, ffn_conv_b, ffn_w_down, final_norm_w, loss_target, m_ada_w, m_ada_b, m_norm1_w, m_w_in, m_attn_sinks, m_sgu_ln_w, m_sgu_ln_b, m_sgu_w, m_sgu_b, m_proj_a, m_proj_b, m_w_out, m_norm2_w, m_ffn_w_gate, m_ffn_w_up, m_ffn_conv_w, m_ffn_conv_b, m_ffn_w_down, m_final_norm_w, v_ada_w, v_ada_b, v_norm1_w, v_w_in, v_attn_sinks, v_sgu_ln_w, v_sgu_ln_b, v_sgu_w, v_sgu_b, v_proj_a, v_proj_b, v_w_out, v_norm2_w, v_ffn_w_gate, v_ffn_w_up, v_ffn_conv_w, v_ffn_conv_b, v_ffn_w_down, v_final_norm_w):
    d = D_MODEL
    ax, ay, ac = lax.axis_index("x"), lax.axis_index("y"), lax.axis_index("c")
    chip = 2 * ax + ay
    dev = 4 * ax + 2 * ay + ac
    core = ac.astype(jnp.int32).reshape(1)

    c_all = _gather_all("gather_cond", c.reshape(SUBLANES, d // SUBLANES)).reshape(2 * N_CHIPS, d)
    c_rows = _pad_to(c_all, 0, ADA_ROWS)
    ada_cols = ada_w.shape[2]
    ada_b_cols = lax.dynamic_slice_in_dim(ada_b, chip * ada_cols, ada_cols, axis=1).reshape(DEPTH, 1, ada_cols)
    mod_cols = _ada_fwd("ada_fwd", c_rows, ada_w, ada_b_cols)
    mod_all = _gather_chips("gather_mod", [mod_cols])[0]
    mod_mine = lax.dynamic_index_in_dim(mod_all, dev, axis=2, keepdims=False)
    mod_mine = mod_mine.transpose(1, 0, 2).reshape(DEPTH, 1, 6 * d)
    mods = [tuple(jnp.split(mod_mine[l], 6, axis=-1)) for l in range(DEPTH)]

    tr = lambda a: jnp.swapaxes(a, 1, 2)
    shards = [tr(w_in).astype(BF), proj_a.astype(BF), proj_b.astype(BF), w_out.astype(BF),
              tr(ffn_w_gate).astype(BF), tr(ffn_w_up).astype(BF), ffn_w_down.astype(BF), ffn_conv_w]
    token = mod_all[0, 0, :SUBLANES, :LANES]
    fetches = []
    for l in range(DEPTH):
        groups = []
        for tag, members in (("in", shards[:1]), ("mix", shards[1:4]), ("ffn", shards[4:])):
            behind = (token[0, 0] * 0.0).astype(members[0].dtype)
            pending, token = _gather_weights_start(f"l{l}_gather_{tag}", [members[0][l] + behind] + [a[l] for a in members[1:]])
            groups.append(pending)
        fetches.append(groups)

    small_in = dict(norm1_w=norm1_w, sinks=attn_sinks, sgu_ln_w=sgu_ln_w, sgu_ln_b=sgu_ln_b, sgu_w=sgu_w, sgu_b=sgu_b,
                    norm2_w=norm2_w, conv_b=ffn_conv_b)
    cosf, sinf = _rope_tables(positions[0])
    small_of = lambda l: {k: v[l] for k, v in small_in.items()}

    h = x[0]
    saved, params = [], []
    for l in range(DEPTH):
        first, mix, ffn = fetches[l]
        w_in_l = _gather_weights_finish(first, token if l == 0 else h)
        late = lambda y, l=l, mix=mix: _mix_params(*_gather_weights_finish(mix, y), small_of(l))
        later = lambda y, l=l, ffn=ffn: _ffn_params(*_gather_weights_finish(ffn, y), small_of(l))
        h, sv, p = _layer_fwd(l, h, mods[l], _early_params(w_in_l[0], small_of(l)), cosf, sinf, late=late, later=later)
        saved.append(sv)
        params.append(p)
    dx, loss_part, d_final = _loss_head("loss_head", h, final_norm_w.reshape(1, d), loss_target[0])
    loss = lax.psum(loss_part[0, 0], ("x", "y", "c"))

    def small_pack(l, grads):
        cw, cb = _conv_grads_natural(grads)
        nat = dict(grads, conv_w=cw, conv_b=cb, final_norm_w=d_final if l == DEPTH - 1 else jnp.zeros((d,), F32))
        return _pack([nat[k] for k, _ in _SMALL], N_CHIPS * SMALL_ROWS).reshape(N_CHIPS, SMALL_ROWS, LANES)

    waiting, inflight = [], []

    def send(tag, partial):
        pending, token = _reduce_cores_start(tag, partial)
        waiting.append(pending)
        return token

    def tick(y):
        token = None
        while waiting:
            pending, token = _reduce_chips_start(waiting.pop(0), core, y)
            inflight.append(pending)
        return token

    dmods = [None] * DEPTH
    dx, dmods[1], grads = _layer_bwd(1, dx, mods[1], params[1], saved[1], cosf, sinf)
    token = send("l1_reduce", dict({k: grads[k] for k in _MATRICES}, small=small_pack(1, grads)))
    dx, dmods[0], grads = _layer_bwd(0, dx, mods[0], params[0], saved[0], cosf, sinf, after=token,
                                     emit=lambda part: send("l0_reduce_" + "_".join(part), part), tick=tick)
    dmod_mine = jnp.concatenate([jnp.concatenate(dmods[l], axis=1) for l in range(DEPTH)], axis=1)
    dmod_all = _gather_all("gather_dmod", dmod_mine.reshape(SUBLANES, -1)).reshape(2 * N_CHIPS, DEPTH * 6 * d)
    send("l0_reduce_in", dict(w_in=grads["w_in"], small=small_pack(0, grads) + dmod_all[0, 0] * 0.0))
    tick(dmod_all)

    totals, flying = {}, None

    def land(after):
        if flying is not None:
            totals.update(zip(flying["names"], _exchange_wait(flying, after)))

    for pending in inflight[:-1]:
        land(dx)
        flying = _reduce_back_start(pending, 1 if pending["tag"].startswith("l1") else 0, core, totals, dx)
    land(dx)
    flying = None

    g_ada_b = _colsum("ada_b_grad", dmod_all).reshape(DEPTH, 6 * d)
    dmod_cols = jnp.stack([lax.dynamic_slice_in_dim(dmod_all, l * 6 * d + chip * ada_cols, ada_cols, axis=1)
                           for l in range(DEPTH)])
    g_ada_w = _ada_bwd("ada_w_grad", c_rows, _pad_to(dmod_cols, 1, ADA_ROWS))
    big = dict(w_in=(tr(w_in), tr(m_w_in), tr(v_w_in)), proj_a=(proj_a, m_proj_a, v_proj_a), proj_b=(proj_b, m_proj_b, v_proj_b),
               w_out=(w_out, m_w_out, v_w_out), w_gate=(tr(ffn_w_gate), tr(m_ffn_w_gate), tr(v_ffn_w_gate)),
               w_up=(tr(ffn_w_up), tr(m_ffn_w_up), tr(v_ffn_w_up)), w_down=(ffn_w_down, m_ffn_w_down, v_ffn_w_down))
    upd, g_big = {}, {}

    def update(k):
        res = _adamw("adamw_" + k, big[k][0], totals[k], *big[k][1:])
        res = [tr(a) for a in res] if k in ("w_in", "w_gate", "w_up") else res
        g_big[k], upd[k] = res[0], res[1:]

    for k in ("w_down", "w_gate", "w_up", "w_out", "proj_a", "proj_b"):
        update(k)
    g_big["ada_w"], *upd["ada_w"] = _adamw("adamw_ada_w", ada_w, g_ada_w, m_ada_w, v_ada_w)
    flying = _reduce_back_start(inflight[-1], 0, core, totals, upd["ada_w"][0])
    land(upd["ada_w"][1])
    update("w_in")

    small_all = _gather_chips("gather_small", [totals["small"]])[0]
    small_g = small_all.transpose(1, 0, 2, 3).reshape(DEPTH, -1)
    per_layer = [_unpack(small_g[l], [shp for _, shp in _SMALL]) for l in range(DEPTH)]
    sg = {k: jnp.stack([per_layer[l][i] for l in range(DEPTH)]) for i, (k, _) in enumerate(_SMALL)}
    g_final = sg["final_norm_w"][DEPTH - 1]
    g_conv_w = lax.dynamic_slice_in_dim(sg["conv_w"], chip * FFN_SHARD, FFN_SHARD, axis=2)

    rest = [("ada_b", ada_b, g_ada_b, m_ada_b, v_ada_b), ("norm1_w", norm1_w, sg["norm1_w"], m_norm1_w, v_norm1_w),
            ("attn_sinks", attn_sinks, sg["sinks"], m_attn_sinks, v_attn_sinks),
            ("sgu_ln_w", sgu_ln_w, sg["sgu_ln_w"], m_sgu_ln_w, v_sgu_ln_w),
            ("sgu_ln_b", sgu_ln_b, sg["sgu_ln_b"], m_sgu_ln_b, v_sgu_ln_b), ("sgu_w", sgu_w, sg["sgu_w"], m_sgu_w, v_sgu_w),
            ("sgu_b", sgu_b, sg["sgu_b"], m_sgu_b, v_sgu_b), ("norm2_w", norm2_w, sg["norm2_w"], m_norm2_w, v_norm2_w),
            ("ffn_conv_w", ffn_conv_w, g_conv_w, m_ffn_conv_w, v_ffn_conv_w),
            ("ffn_conv_b", ffn_conv_b, sg["conv_b"], m_ffn_conv_b, v_ffn_conv_b),
            ("final_norm_w", final_norm_w.reshape(1, d), g_final.reshape(1, d), m_final_norm_w.reshape(1, d),
             v_final_norm_w.reshape(1, d))]
    rest_out = _adamw_small("adamw_rest", *[[r[i] for r in rest] for i in (1, 2, 3, 4)])
    g_rest = {r[0]: r[2] for r in rest}
    u_rest = {r[0]: tuple(o[i] for o in rest_out) for i, r in enumerate(rest)}
    g_rest["final_norm_w"] = g_final
    u_rest["final_norm_w"] = tuple(a.reshape(d) for a in u_rest["final_norm_w"])

    names = ("ada_w", "ada_b", "norm1_w", "w_in", "attn_sinks", "sgu_ln_w", "sgu_ln_b", "sgu_w", "sgu_b", "proj_a", "proj_b",
             "w_out", "norm2_w", "ffn_w_gate", "ffn_w_up", "ffn_conv_w", "ffn_conv_b", "ffn_w_down", "final_norm_w")
    alias = {"ffn_w_gate": "w_gate", "ffn_w_up": "w_up", "ffn_w_down": "w_down"}
    grad_of = lambda n: g_rest[n] if n in g_rest else g_big[alias.get(n, n)]
    upd_of = lambda n: u_rest[n] if n in u_rest else upd[alias.get(n, n)]
    return (loss, dx[None], *[grad_of(n) for n in names], *[upd_of(n)[0] for n in names],
            *[upd_of(n)[1] for n in names], *[upd_of(n)[2] for n in names])
```

```python
import jax
import jax.numpy as jnp
from jax import lax
from jax.experimental import pallas as pl
from jax.experimental.pallas import tpu as pltpu

F32 = jnp.float32
BF = jnp.bfloat16

D_MODEL = 1024
N_Q_HEADS = 16
N_KV_HEADS = 2
HEAD_DIM = 64
ATTN_BLOCK = 128
ROPE_THETA = 500000.0
ROT_DIM = HEAD_DIM // 4
SGU_WIDTH = 1024
SGU_GROUPS = 8
SGU_CHUNK = 128
FFN_DIM = 2816
NORM_EPS = 1e-6
DEPTH = 2
IN_COLS = 5376
N_CHIPS = 4
FFN_SHARD = FFN_DIM // N_CHIPS
CHIPS_PER_CHUNK = 2
FFN_CHUNK = CHIPS_PER_CHUNK * FFN_SHARD
FFN_CHUNKS = FFN_DIM // FFN_CHUNK
LANES = 128
SUBLANES = 8
HALO = 16
VMEM_LIMIT = 56 * 1024 * 1024
NEG_BIG = -1e30

ADAM_LR = 0.001
ADAM_B1 = 0.9
ADAM_B2 = 0.999
ADAM_EPS = 1e-08
ADAM_WD = 0.01
ADAM_STEP = 10

MESH = pl.DeviceIdType.MESH

Q_END = 1024
KV_END = 1280
U_END = 2304
Z_END = 3328
GA_END = 4352


def _sds(shape, dtype):
    return jax.ShapeDtypeStruct(tuple(shape), dtype)


def _call(body, name, out_shape, grid, in_specs, out_specs, semantics, scratch=(), after=None):
    n_in = len(in_specs)
    fn = body
    if after is not None:
        def fn(*refs):
            return body(*refs[:n_in], *refs[n_in + 1:])

        in_specs = list(in_specs) + [pl.BlockSpec(memory_space=pl.ANY)]
    call = pl.pallas_call(
        fn, name=name, out_shape=out_shape, grid=grid, in_specs=in_specs, out_specs=out_specs,
        scratch_shapes=scratch,
        compiler_params=pltpu.CompilerParams(dimension_semantics=semantics, vmem_limit_bytes=VMEM_LIMIT))
    if after is None:
        return call
    return lambda *args: call(*args, after)


def _rows(tm, width, col=0):
    return pl.BlockSpec((tm, width), lambda i: (i, col))


def _vec(width):
    return pl.BlockSpec((1, width), lambda i: (0, 0))


def _resident(shape):
    zeros = (0,) * len(shape)
    return pl.BlockSpec(tuple(shape), lambda *_: zeros, pipeline_mode=pl.Buffered(1))


def _sigmoid(x):
    return 0.5 + 0.5 * jnp.tanh(0.5 * x)


def _gelu(x):
    return 0.5 * x * (1.0 + lax.erf(x * 0.7071067811865476))


def _gelu_and_grad(x):
    cdf = 0.5 * (1.0 + lax.erf(x * 0.7071067811865476))
    return x * cdf, cdf + x * jnp.exp(-0.5 * x * x) * 0.3989422804014327


def _dot(a, b):
    return jnp.dot(a, b, preferred_element_type=F32)


def _dot_nt(a, b):
    return lax.dot_general(a, b, (((1,), (1,)), ((), ())), preferred_element_type=F32)


def _dot_tn(a, b):
    return lax.dot_general(a, b, (((0,), (0,)), ((), ())), preferred_element_type=F32)


def _rms(xv):
    return lax.rsqrt(jnp.mean(xv * xv, axis=-1, keepdims=True) + NORM_EPS)


def _matmul_tn(name, a, b, tk=512, tn=1024, after=None):
    s, k = a.shape
    n = b.shape[1]
    tk, tn = min(tk, k), min(tn, n)

    def body(a_ref, b_ref, o_ref):
        o_ref[...] = _dot_tn(a_ref[...], b_ref[...]).astype(o_ref.dtype)

    return _call(body, name, _sds((k, n), BF), (k // tk, n // tn),
                 [pl.BlockSpec((s, tk), lambda i, j: (0, i)), pl.BlockSpec((s, tn), lambda i, j: (0, j))],
                 pl.BlockSpec((tk, tn), lambda i, j: (i, j)), ("parallel", "parallel"), after=after)(a, b)


def _matmul_tn_rows(name, a, b, out, row0, rows_total, tk=256):
    s, k = a.shape
    n = b.shape[1]

    def body(a_ref, b_ref, *rest):
        rest[-1][...] = _dot_tn(a_ref[...], b_ref[...]).astype(BF)

    in_specs = [pl.BlockSpec((s, tk), lambda i: (0, i)), _resident(b.shape)]
    args = [a, b]
    if out is not None:
        in_specs.append(pl.BlockSpec(memory_space=pl.ANY))
        args.append(out)
    return pl.pallas_call(
        body, name=name, out_shape=_sds((rows_total, n), BF), grid=(k // tk,), in_specs=in_specs,
        out_specs=pl.BlockSpec((tk, n), lambda i: (row0 // tk + i, 0)),
        input_output_aliases={2: 0} if out is not None else {},
        compiler_params=pltpu.CompilerParams(dimension_semantics=("parallel",), vmem_limit_bytes=VMEM_LIMIT))(*args)


def _rope_partner(v):
    lane = lax.broadcasted_iota(jnp.int32, (1, LANES), 1) % HEAD_DIM
    return jnp.where(lane < ROT_DIM // 2, pltpu.roll(v, LANES - ROT_DIM // 2, axis=1), pltpu.roll(v, ROT_DIM // 2, axis=1))


def _dup_half(v, half):
    lane = lax.broadcasted_iota(jnp.int32, (1, LANES), 1)
    keep = jnp.where((lane >= HEAD_DIM) == (half == 1), v, 0.0)
    return keep + pltpu.roll(keep, HEAD_DIM, axis=1)


def _in_proj(name, x, w, sc, sh, w_in, cosf, sinf, tm=512, after=None):
    s, d = x.shape
    tm = min(tm, s)

    def body(x_ref, w_ref, sc_ref, sh_ref, win_ref, cos_ref, sin_ref,
             h_ref, qr_ref, kk0_ref, kk1_ref, vv0_ref, vv1_ref, u_ref, v_ref, ga_ref, gb_ref):
        xv = x_ref[...]
        h = ((xv * _rms(xv)) * w_ref[...] * (1.0 + sc_ref[...]) + sh_ref[...]).astype(BF)
        h_ref[...] = h
        cosv, sinv = cos_ref[...], sin_ref[...]
        q = _dot_nt(h, win_ref[:Q_END, :])
        for j in range(D_MODEL // LANES):
            qv = q[:, j * LANES:(j + 1) * LANES]
            qr_ref[:, j * LANES:(j + 1) * LANES] = ((qv * cosv + _rope_partner(qv) * sinv) * ATTN_SCALE).astype(BF)
        kv = _dot_nt(h, win_ref[Q_END:KV_END, :])
        kr = kv[:, :LANES] * cosv + _rope_partner(kv[:, :LANES]) * sinv
        vv = kv[:, LANES:]
        kk0_ref[...] = _dup_half(kr, 0).astype(BF)
        kk1_ref[...] = _dup_half(kr, 1).astype(BF)
        vv0_ref[...] = _dup_half(vv, 0).astype(BF)
        vv1_ref[...] = _dup_half(vv, 1).astype(BF)
        u_ref[...] = _dot_nt(h, win_ref[KV_END:U_END, :])
        v_ref[...] = _dot_nt(h, win_ref[U_END:Z_END, :])
        ga_ref[...] = _dot_nt(h, win_ref[Z_END:GA_END, :]).astype(BF)
        gb_ref[...] = _dot_nt(h, win_ref[GA_END:, :]).astype(BF)

    wide, kvs, pre = _sds((s, d), BF), _sds((s, LANES), BF), _sds((s, d), F32)
    return _call(body, name, (wide, wide, kvs, kvs, kvs, kvs, pre, pre, wide, wide), (s // tm,),
                 [_rows(tm, d), _vec(d), _vec(d), _vec(d), _resident(w_in.shape), _rows(tm, LANES), _rows(tm, LANES)],
                 (_rows(tm, d), _rows(tm, d)) + (_rows(tm, LANES),) * 4 + (_rows(tm, d),) * 4, ("parallel",), after=after)(
                     x, w, sc, sh, w_in, cosf, sinf)


def _in_proj_bwd(name, dq, dkv, du, dv, dga, dgb, w_in, x, w, sc, dx_in, tm=512):
    s, d = x.shape
    tm = min(tm, s)

    def body(dq_ref, dkv_ref, du_ref, dv_ref, dga_ref, dgb_ref, win_ref, x_ref, w_ref, sc_ref, dxin_ref,
             dx_ref, da_ref, dsh_ref):
        @pl.when(pl.program_id(0) == 0)
        def _():
            da_ref[...] = jnp.zeros_like(da_ref)
            dsh_ref[...] = jnp.zeros_like(dsh_ref)

        dh = (_dot(dq_ref[...], win_ref[:Q_END, :]) + _dot(dkv_ref[...], win_ref[Q_END:KV_END, :])
              + _dot(du_ref[...], win_ref[KV_END:U_END, :]) + _dot(dv_ref[...], win_ref[U_END:Z_END, :])
              + _dot(dga_ref[...], win_ref[Z_END:GA_END, :]) + _dot(dgb_ref[...], win_ref[GA_END:, :]))
        xv = x_ref[...]
        r = _rms(xv)
        xn = xv * r
        dxn = dh * (w_ref[...] * (1.0 + sc_ref[...]))
        dx_ref[...] = dxin_ref[...] + r * (dxn - xn * jnp.mean(dxn * xn, axis=-1, keepdims=True))
        da_ref[...] += jnp.sum(dh * xn, axis=0, keepdims=True)
        dsh_ref[...] += jnp.sum(dh, axis=0, keepdims=True)

    return _call(body, name, (_sds((s, d), F32), _sds((1, d), F32), _sds((1, d), F32)), (s // tm,),
                 [_rows(tm, d), _rows(tm, 2 * LANES), _rows(tm, d), _rows(tm, d), _rows(tm, d), _rows(tm, d),
                  _resident(w_in.shape), _rows(tm, d), _vec(d), _vec(d), _rows(tm, d)],
                 (_rows(tm, d), _vec(d), _vec(d)), ("arbitrary",))(dq, dkv, du, dv, dga, dgb, w_in, x, w, sc, dx_in)


def _rope_bwd(name, dqr, dkv_cur, dkv_prev, cosf, sinf, tm=512):
    s = dqr.shape[0]
    tm = min(tm, s)
    steps = s // tm
    per = tm // ATTN_BLOCK
    nb = s // ATTN_BLOCK

    def unrope(v, cosv, sinv):
        return v * cosv - _rope_partner(v) * sinv

    def body(dq_ref, cur_ref, prev_ref, next_ref, cos_ref, sin_ref, dqo_ref, dkvo_ref):
        i = pl.program_id(0)
        cosv, sinv = cos_ref[...], sin_ref[...]
        for j in range(D_MODEL // LANES):
            dqo_ref[:, j * LANES:(j + 1) * LANES] = unrope(dq_ref[:, j * LANES:(j + 1) * LANES], cosv, sinv).astype(BF)
        nxt = jnp.where(i < steps - 1, next_ref[...], 0.0)
        if per > 1:
            shifted = jnp.concatenate([prev_ref[ATTN_BLOCK:, :], nxt], axis=0)
        else:
            shifted = nxt
        tot = cur_ref[...] + shifted
        dkvo_ref[:, :LANES] = unrope(tot[:, :LANES], cosv, sinv).astype(BF)
        dkvo_ref[:, LANES:] = tot[:, LANES:].astype(BF)

    nxt_spec = pl.BlockSpec((ATTN_BLOCK, 2 * LANES), lambda i: (jnp.minimum((i + 1) * per, nb - 1), 0))
    return _call(body, name, (_sds((s, D_MODEL), BF), _sds((s, 2 * LANES), BF)), (steps,),
                 [_rows(tm, D_MODEL), _rows(tm, 2 * LANES), _rows(tm, 2 * LANES), nxt_spec, _rows(tm, LANES),
                  _rows(tm, LANES)],
                 (_rows(tm, D_MODEL), _rows(tm, 2 * LANES)), ("parallel",))(dqr, dkv_cur, dkv_prev, dkv_prev, cosf, sinf)


Q_PER_KV = N_Q_HEADS // N_KV_HEADS
ATTN_SCALE = HEAD_DIM ** -0.5
HEADS_AHEAD_FWD = 2
HEADS_AHEAD_BWD = 3


def _band_mask_t(n):
    kj = lax.broadcasted_iota(jnp.int32, (2 * ATTN_BLOCK, ATTN_BLOCK), 0)
    qi = lax.broadcasted_iota(jnp.int32, (2 * ATTN_BLOCK, ATTN_BLOCK), 1)
    return (kj > qi) & (kj <= qi + ATTN_BLOCK) & ((n > 0) | (kj >= ATTN_BLOCK))


def _softmax_t(raw, allowed, sink):
    sc = jnp.where(allowed, raw, NEG_BIG)
    m = jnp.maximum(jnp.max(sc, axis=0, keepdims=True), sink)
    p = jnp.exp(sc - m)
    esink = jnp.exp(sink - m)
    inv = 1.0 / (jnp.sum(p, axis=0, keepdims=True) + esink)
    return p * inv, esink * inv


def _kv_specs():
    cur = pl.BlockSpec((ATTN_BLOCK, LANES), lambda n: (n, 0))
    prev = pl.BlockSpec((ATTN_BLOCK, LANES), lambda n: (jnp.maximum(n - 1, 0), 0))
    return [prev, cur] * 4


def _attention(name, qr, kk0, kk1, vv0, vv1, sinks):
    s = qr.shape[0]
    nb = s // ATTN_BLOCK

    def body(sink_ref, q_ref, k0p, k0c, k1p, k1c, v0p, v0c, v1p, v1c, y_ref):
        allowed = _band_mask_t(pl.program_id(0))
        upper = lax.broadcasted_iota(jnp.int32, (1, LANES), 1) >= HEAD_DIM
        upper_rows = lax.broadcasted_iota(jnp.int32, (LANES, 1), 0) >= HEAD_DIM
        bands = ((jnp.concatenate([k0p[...], k0c[...]], axis=0), jnp.concatenate([v0p[...], v0c[...]], axis=0)),
                 (jnp.concatenate([k1p[...], k1c[...]], axis=0), jnp.concatenate([v1p[...], v1c[...]], axis=0)))
        vbts = (bands[0][1].T, bands[1][1].T)

        def scores(h):
            hk, j, half = h // Q_PER_KV, (h % Q_PER_KV) // 2, h % 2
            col = (hk * 4 + j) * LANES
            qp = q_ref[:, col:col + LANES]
            return _dot_nt(bands[hk][0], jnp.where(upper if half else jnp.logical_not(upper), qp, jnp.zeros_like(qp)))

        out_t = None
        ahead = [scores(h) for h in range(HEADS_AHEAD_FWD)]
        for h in range(N_Q_HEADS):
            hk, j, half = h // Q_PER_KV, (h % Q_PER_KV) // 2, h % 2
            raw = ahead.pop(0)
            if h + HEADS_AHEAD_FWD < N_Q_HEADS:
                ahead.append(scores(h + HEADS_AHEAD_FWD))
            pn, _ = _softmax_t(raw, allowed, sink_ref[h])
            o_h = _dot(vbts[hk], pn.astype(BF))
            out_t = jnp.where(upper_rows, o_h, out_t) if half else o_h
            if half:
                col = (hk * 4 + j) * LANES
                y_ref[:, col:col + LANES] = out_t.T.astype(BF)

    return _call(body, name, _sds((s, D_MODEL), BF), (nb,),
                 [pl.BlockSpec(memory_space=pltpu.SMEM), pl.BlockSpec((ATTN_BLOCK, D_MODEL), lambda n: (n, 0))] + _kv_specs(),
                 pl.BlockSpec((ATTN_BLOCK, D_MODEL), lambda n: (n, 0)), ("parallel",))(
                     sinks, qr, kk0, kk0, kk1, kk1, vv0, vv0, vv1, vv1)


def _attention_bwd(name, qr, kk0, kk1, vv0, vv1, sinks, dy, after=None):
    s = qr.shape[0]
    nb = s // ATTN_BLOCK

    def body(sink_ref, q_ref, dy_ref, k0p, k0c, k1p, k1c, v0p, v0c, v1p, v1c, dq_ref, cur_ref, prev_ref, dsink_ref):
        @pl.when(pl.program_id(0) == 0)
        def _():
            dsink_ref[...] = jnp.zeros_like(dsink_ref)

        allowed = _band_mask_t(pl.program_id(0))
        lane = lax.broadcasted_iota(jnp.int32, (1, LANES), 1)
        upper = lane >= HEAD_DIM
        upper_rows = lax.broadcasted_iota(jnp.int32, (LANES, 1), 0) >= HEAD_DIM
        bands = ((jnp.concatenate([k0p[...], k0c[...]], axis=0), jnp.concatenate([v0p[...], v0c[...]], axis=0)),
                 (jnp.concatenate([k1p[...], k1c[...]], axis=0), jnp.concatenate([v1p[...], v1c[...]], axis=0)))
        kbts = (bands[0][0].T, bands[1][0].T)

        def scores(h):
            hk, j, half = h // Q_PER_KV, (h % Q_PER_KV) // 2, h % 2
            kb, vb = bands[hk]
            col = (hk * 4 + j) * LANES
            sel = upper if half else jnp.logical_not(upper)
            qp = q_ref[:, col:col + LANES]
            qa = jnp.where(sel, qp, jnp.zeros_like(qp))
            dya = jnp.where(sel, dy_ref[:, col:col + LANES], 0.0).astype(BF)
            return qa, dya, _dot_nt(kb, qa), _dot_nt(vb, dya)

        dsink = jnp.zeros((1, LANES), F32)
        dk_slab = jnp.zeros((2 * ATTN_BLOCK, LANES), F32)
        dv_slab = jnp.zeros((2 * ATTN_BLOCK, LANES), F32)
        dkk = dvv = dq_t = None
        ahead = [scores(h) for h in range(HEADS_AHEAD_BWD)]
        for h in range(N_Q_HEADS):
            hk, j, half = h // Q_PER_KV, (h % Q_PER_KV) // 2, h % 2
            qa, dya, raw, dp = ahead.pop(0)
            if h + HEADS_AHEAD_BWD < N_Q_HEADS:
                ahead.append(scores(h + HEADS_AHEAD_BWD))
            pn, psink = _softmax_t(raw, allowed, sink_ref[h])
            delta = jnp.sum(pn * dp, axis=0, keepdims=True)
            ds = (pn * (dp - delta)).astype(BF)
            dsink = dsink + jnp.where(lane == h, -jnp.sum(psink * delta), 0.0)
            dq_h = _dot(kbts[hk], ds) * ATTN_SCALE
            dq_t = jnp.where(upper_rows, dq_h, dq_t) if half else dq_h
            dk_h, dv_h = _dot(ds, qa), _dot(pn.astype(BF), dya)
            dkk, dvv = (dk_h, dv_h) if h % Q_PER_KV == 0 else (dkk + dk_h, dvv + dv_h)
            if half:
                col = (hk * 4 + j) * LANES
                dq_ref[:, col:col + LANES] = dq_t.T
            if h % Q_PER_KV == Q_PER_KV - 1:
                mine = upper if hk else jnp.logical_not(upper)
                dk_slab = jnp.where(mine, dkk + pltpu.roll(dkk, HEAD_DIM, axis=1), dk_slab)
                dv_slab = jnp.where(mine, dvv + pltpu.roll(dvv, HEAD_DIM, axis=1), dv_slab)
        prev_ref[:, :LANES] = dk_slab[:ATTN_BLOCK]
        prev_ref[:, LANES:] = dv_slab[:ATTN_BLOCK]
        cur_ref[:, :LANES] = dk_slab[ATTN_BLOCK:]
        cur_ref[:, LANES:] = dv_slab[ATTN_BLOCK:]
        dsink_ref[...] += dsink

    blk = pl.BlockSpec((ATTN_BLOCK, D_MODEL), lambda n: (n, 0))
    kvo = pl.BlockSpec((ATTN_BLOCK, 2 * LANES), lambda n: (n, 0))
    return _call(body, name,
                 (_sds((s, D_MODEL), F32), _sds((s, 2 * LANES), F32), _sds((s, 2 * LANES), F32), _sds((1, LANES), F32)),
                 (nb,), [pl.BlockSpec(memory_space=pltpu.SMEM), blk, blk] + _kv_specs(),
                 (blk, kvo, kvo, pl.BlockSpec((1, LANES), lambda n: (0, 0))), ("arbitrary",), after=after)(
                     sinks, qr, dy, kk0, kk0, kk1, kk1, vv0, vv0, vv1, vv1)


def _sgu_weights(wm_ref, g):
    t = lax.broadcasted_iota(jnp.int32, (SGU_CHUNK, SGU_CHUNK), 0)
    sidx = lax.broadcasted_iota(jnp.int32, (SGU_CHUNK, SGU_CHUNK), 1)
    return jnp.where(sidx <= t, wm_ref[g], 0.0).astype(BF)


def _layer_norm_stats(v):
    mu = jnp.mean(v, axis=-1, keepdims=True)
    cen = v - mu
    rstd = lax.rsqrt(jnp.mean(cen * cen, axis=-1, keepdims=True) + NORM_EPS)
    return cen * rstd, rstd


def _sgu(name, u_pre, v_pre, ln_w, ln_b, wm, bfull, tm=512):
    s, w = u_pre.shape
    tm = min(tm, s)

    def body(u_ref, v_ref, lw_ref, lb_ref, wm_ref, b_ref, y_ref):
        vhat, _ = _layer_norm_stats(_gelu(v_ref[...]))
        vn = (vhat * lw_ref[...] + lb_ref[...]).astype(BF)
        for g in range(SGU_GROUPS):
            wg = _sgu_weights(wm_ref, g)
            cols = slice(g * SGU_CHUNK, (g + 1) * SGU_CHUNK)
            for ch in range(tm // SGU_CHUNK):
                rows = slice(ch * SGU_CHUNK, (ch + 1) * SGU_CHUNK)
                f = _dot(wg, vn[rows, cols]) + b_ref[g]
                y_ref[rows, cols] = (_gelu(u_ref[rows, cols]) * f).astype(BF)

    full3 = pl.BlockSpec((SGU_GROUPS, SGU_CHUNK, SGU_CHUNK), lambda i: (0, 0, 0))
    return _call(body, name, _sds((s, w), BF), (s // tm,),
                 [_rows(tm, w), _rows(tm, w), _vec(w), _vec(w), full3, full3],
                 _rows(tm, w), ("parallel",))(u_pre, v_pre, ln_w, ln_b, wm, bfull)


def _sgu_bwd(name, u_pre, v_pre, ln_w, ln_b, wm, bfull, dy, tm=512, after=None):
    s, w = u_pre.shape
    tm = min(tm, s)
    steps = s // tm

    def body(u_ref, v_ref, lw_ref, lb_ref, wm_ref, b_ref, dy_ref, du_ref, dv_ref, dwm_ref, db_ref, dlw_ref, dlb_ref,
             dfsum_ref):
        i = pl.program_id(0)

        @pl.when(i == 0)
        def _():
            dwm_ref[...] = jnp.zeros_like(dwm_ref)
            dlw_ref[...] = jnp.zeros_like(dlw_ref)
            dlb_ref[...] = jnp.zeros_like(dlb_ref)
            dfsum_ref[...] = jnp.zeros_like(dfsum_ref)

        vpre = v_ref[...]
        vg, dvg_dv = _gelu_and_grad(vpre)
        vhat, rstd = _layer_norm_stats(vg)
        vn = (vhat * lw_ref[...] + lb_ref[...]).astype(BF)
        t = lax.broadcasted_iota(jnp.int32, (SGU_CHUNK, SGU_CHUNK), 0)
        sidx = lax.broadcasted_iota(jnp.int32, (SGU_CHUNK, SGU_CHUNK), 1)
        dvn_cols = []
        for g in range(SGU_GROUPS):
            wg = _sgu_weights(wm_ref, g)
            cols = slice(g * SGU_CHUNK, (g + 1) * SGU_CHUNK)
            dvn_rows = []
            dwg = jnp.zeros((SGU_CHUNK, SGU_CHUNK), F32)
            dfs = jnp.zeros((SGU_CHUNK, SGU_CHUNK), F32)
            for ch in range(tm // SGU_CHUNK):
                rows = slice(ch * SGU_CHUNK, (ch + 1) * SGU_CHUNK)
                upre = u_ref[rows, cols]
                dyv = dy_ref[rows, cols].astype(F32)
                f = _dot(wg, vn[rows, cols]) + b_ref[g]
                ug, dug_du = _gelu_and_grad(upre)
                du_ref[rows, cols] = (dyv * f * dug_du).astype(BF)
                df = dyv * ug
                dfb = df.astype(BF)
                dvn_rows.append(_dot_tn(wg, dfb))
                dwg = dwg + _dot_nt(dfb, vn[rows, cols])
                dfs = dfs + df
            dwm_ref[g] += jnp.where(sidx <= t, dwg, 0.0)
            dfsum_ref[g] += dfs
            dvn_cols.append(jnp.concatenate(dvn_rows, axis=0) if len(dvn_rows) > 1 else dvn_rows[0])
        dvn = jnp.concatenate(dvn_cols, axis=1)
        dlw_ref[...] += jnp.sum(dvn * vhat, axis=0, keepdims=True)
        dlb_ref[...] += jnp.sum(dvn, axis=0, keepdims=True)
        dvh = dvn * lw_ref[...]
        dvg = rstd * (dvh - jnp.mean(dvh, axis=-1, keepdims=True) - vhat * jnp.mean(dvh * vhat, axis=-1, keepdims=True))
        dv_ref[...] = (dvg * dvg_dv).astype(BF)

        @pl.when(i == steps - 1)
        def _():
            for g in range(SGU_GROUPS):
                db_ref[g:g + 1, :] = jnp.sum(dfsum_ref[g].T, axis=0, keepdims=True)

    full3 = pl.BlockSpec((SGU_GROUPS, SGU_CHUNK, SGU_CHUNK), lambda i: (0, 0, 0))
    return _call(body, name,
                 (_sds((s, w), BF), _sds((s, w), BF), _sds((SGU_GROUPS, SGU_CHUNK, SGU_CHUNK), F32),
                  _sds((SGU_GROUPS, SGU_CHUNK), F32), _sds((1, w), F32), _sds((1, w), F32)),
                 (steps,),
                 [_rows(tm, w), _rows(tm, w), _vec(w), _vec(w), full3, full3, _rows(tm, w)],
                 (_rows(tm, w), _rows(tm, w), full3, pl.BlockSpec((SGU_GROUPS, SGU_CHUNK), lambda i: (0, 0)), _vec(w), _vec(w)),
                 ("arbitrary",), scratch=[pltpu.VMEM((SGU_GROUPS, SGU_CHUNK, SGU_CHUNK), F32)], after=after)(
                     u_pre, v_pre, ln_w, ln_b, wm, bfull, dy)


def _mix_out(name, y_sgu, y_attn, ga_pre, gb_pre, x, g1, proj_a, proj_b, w_out, w2, sc2, sh2, tm=512):
    s, d = x.shape
    tm = min(tm, s)

    def body(ys_ref, ya_ref, ga_ref, gb_ref, x_ref, g1_ref, wa_ref, wb_ref, wo_ref, w2_ref, sc2_ref, sh2_ref,
             m_ref, pa_ref, pb_ref, o_ref, x1_ref, h2_ref):
        pa = _dot(ys_ref[...], wa_ref[...].reshape(d, d))
        pb = _dot(ya_ref[...], wb_ref[...].reshape(d, d))
        pa_ref[...] = pa.astype(BF)
        pb_ref[...] = pb.astype(BF)
        merged = (_sigmoid(ga_ref[...].astype(F32)) * pa + _sigmoid(gb_ref[...].astype(F32)) * pb).astype(BF)
        m_ref[...] = merged
        o = _dot(merged, wo_ref[...].reshape(d, d))
        o_ref[...] = o.astype(BF)
        x1 = x_ref[...] + g1_ref[...] * o
        x1_ref[...] = x1
        h2_ref[...] = ((x1 * _rms(x1)) * w2_ref[...] * (1.0 + sc2_ref[...]) + sh2_ref[...]).astype(BF)

    f, b = _sds((s, d), F32), _sds((s, d), BF)
    r = _rows(tm, d)
    wspec = _resident(proj_a.shape)
    return _call(body, name, (b, b, b, b, f, b), (s // tm,),
                 [r, r, r, r, r, _vec(d), wspec, wspec, wspec, _vec(d), _vec(d), _vec(d)], (r,) * 6, ("parallel",))(
                     y_sgu, y_attn, ga_pre, gb_pre, x, g1, proj_a, proj_b, w_out, w2, sc2, sh2)


def _mix_bwd(name, do, w_out, proj_a, proj_b, ga_pre, gb_pre, pa, pb, tm=512):
    s, d = do.shape
    tm = min(tm, s)

    def body(do_ref, wo_ref, wa_ref, wb_ref, ga_ref, gb_ref, pa_ref, pb_ref,
             dpa_ref, dpb_ref, dga_ref, dgb_ref, dys_ref, dya_ref):
        dm = _dot_nt(do_ref[...], wo_ref[...].reshape(d, d))
        ga = _sigmoid(ga_ref[...].astype(F32))
        gb = _sigmoid(gb_ref[...].astype(F32))
        dpa = (dm * ga).astype(BF)
        dpb = (dm * gb).astype(BF)
        dpa_ref[...] = dpa
        dpb_ref[...] = dpb
        dga_ref[...] = (dm * pa_ref[...].astype(F32) * ga * (1.0 - ga)).astype(BF)
        dgb_ref[...] = (dm * pb_ref[...].astype(F32) * gb * (1.0 - gb)).astype(BF)
        dys_ref[...] = _dot_nt(dpa, wa_ref[...].reshape(d, d)).astype(BF)
        dya_ref[...] = _dot_nt(dpb, wb_ref[...].reshape(d, d)).astype(BF)

    f, b = _sds((s, d), F32), _sds((s, d), BF)
    r = _rows(tm, d)
    wspec = _resident(w_out.shape)
    return _call(body, name, (b, b, b, b, b, b), (s // tm,), [r, wspec, wspec, wspec, r, r, r, r], (r,) * 6,
                 ("parallel",))(do, w_out, proj_a, proj_b, ga_pre, gb_pre, pa, pb)


def _ffn_up_act(name, h2, w_gate, w_up, cw, cb, tm=1024):
    s, d = h2.shape
    tm = min(tm, s)
    tc = FFN_CHUNK
    per = tm // HALO

    def body(h_ref, hprev_ref, wg_ref, wu_ref, cw_ref, cb_ref, a_ref, ac_ref, up_ref, hf_ref):
        hv = h_ref[...]
        wg = wg_ref[...].reshape(tc, d)
        a = _dot_nt(hv, wg).astype(BF)
        up = _dot_nt(hv, wu_ref[...].reshape(tc, d)).astype(BF)
        a_ref[...] = a
        up_ref[...] = up
        prev = jnp.where(pl.program_id(1) > 0, _dot_nt(hprev_ref[...], wg).astype(BF).astype(F32), 0.0)
        ext = jnp.concatenate([prev, a.astype(F32)], axis=0)
        ac = (cb_ref[...] + cw_ref[0:1, :] * pltpu.roll(ext, 2, axis=0) + cw_ref[1:2, :] * pltpu.roll(ext, 1, axis=0)
              + cw_ref[2:3, :] * ext)[HALO:]
        ac_ref[...] = ac.astype(BF)
        hf_ref[...] = (ac * _sigmoid(ac) * up.astype(F32)).astype(BF)

    wspec = pl.BlockSpec((CHIPS_PER_CHUNK, FFN_SHARD, d), lambda j, i: (j, 0, 0))
    ospec = pl.BlockSpec((tm, tc), lambda j, i: (i, j))
    o = _sds((s, FFN_DIM), BF)
    return _call(body, name, (o, o, o, o), (FFN_CHUNKS, s // tm),
                 [pl.BlockSpec((tm, d), lambda j, i: (i, 0)), pl.BlockSpec((HALO, d), lambda j, i: (jnp.maximum(i * per - 1, 0), 0)),
                  wspec, wspec, pl.BlockSpec((3, tc), lambda j, i: (0, j)), pl.BlockSpec((1, tc), lambda j, i: (0, j))],
                 (ospec, ospec, ospec, ospec), ("parallel", "parallel"))(h2, h2, w_gate, w_up, cw, cb)


def _ffn_down(name, hf, w_down, x1, g2, tm=512):
    s, d = x1.shape
    tm = min(tm, s)

    def body(hf_ref, wd_ref, x1_ref, g2_ref, dn_ref, x2_ref):
        dn = _dot(hf_ref[...], wd_ref[...].reshape(FFN_DIM, d))
        dn_ref[...] = dn.astype(BF)
        x2_ref[...] = x1_ref[...] + g2_ref[...] * dn

    return _call(body, name, (_sds((s, d), BF), _sds((s, d), F32)), (s // tm,),
                 [_rows(tm, FFN_DIM), _resident(w_down.shape), _rows(tm, d), _vec(d)],
                 (_rows(tm, d), _rows(tm, d)), ("parallel",))(hf, w_down, x1, g2)


def _ffn_down_bwd_act(name, dx2, dn, g2, w_down, a, ac, up, cw, tm=256, after=None):
    s, d = dx2.shape
    c = a.shape[1]
    tm = min(tm, s)
    tc = FFN_CHUNK
    per = tm // HALO
    steps = s // tm
    last = s // HALO - 1
    n = tm + HALO

    def body(dx_ref, dxnext_ref, dn_ref, g2_ref, wd_ref, a_ref, ac_ref, acnext_ref, up_ref, upnext_ref, cw_ref,
             ddn_ref, da_ref, dup_ref, dg_ref, dcw_ref, dcb_ref):
        i = pl.program_id(0)

        @pl.when(i == 0)
        def _():
            dg_ref[...] = jnp.zeros_like(dg_ref)
            dcw_ref[...] = jnp.zeros_like(dcw_ref)
            dcb_ref[...] = jnp.zeros_like(dcb_ref)

        dxv = dx_ref[...]
        ddn = (dxv * g2_ref[...]).astype(BF)
        ddn_ref[...] = ddn
        dg_ref[...] += jnp.sum(dxv * dn_ref[...].astype(F32), axis=0, keepdims=True)
        ddn_next = jnp.where(i < steps - 1, dxnext_ref[...] * g2_ref[...], 0.0).astype(BF)
        ddn_ext = jnp.concatenate([ddn, ddn_next], axis=0)
        for k in range(FFN_CHUNKS):
            cols = slice(k * tc, (k + 1) * tc)
            dh = _dot_nt(ddn_ext, wd_ref[k * CHIPS_PER_CHUNK:(k + 1) * CHIPS_PER_CHUNK].reshape(tc, d))
            ace = jnp.concatenate([ac_ref[:, cols].astype(F32), acnext_ref[:, cols].astype(F32)], axis=0)
            upe = jnp.concatenate([up_ref[:, cols].astype(F32), upnext_ref[:, cols].astype(F32)], axis=0)
            sig = _sigmoid(ace)
            silu = ace * sig
            dac = dh * upe * (sig + silu * (1.0 - sig))
            dup_ref[:, cols] = (dh[:tm] * silu[:tm]).astype(BF)
            d1 = pltpu.roll(dac, n - 1, axis=0)[:tm]
            d2 = pltpu.roll(dac, n - 2, axis=0)[:tm]
            d0 = dac[:tm]
            da_ref[:, cols] = (cw_ref[2:3, cols] * d0 + cw_ref[1:2, cols] * d1 + cw_ref[0:1, cols] * d2).astype(BF)
            a0 = a_ref[:, cols].astype(F32)
            dcb_ref[:, cols] += jnp.sum(d0, axis=0, keepdims=True)
            dcw_ref[0:1, cols] += jnp.sum(d2 * a0, axis=0, keepdims=True)
            dcw_ref[1:2, cols] += jnp.sum(d1 * a0, axis=0, keepdims=True)
            dcw_ref[2:3, cols] += jnp.sum(d0 * a0, axis=0, keepdims=True)

    nxt = lambda width: pl.BlockSpec((HALO, width), lambda i: (jnp.minimum((i + 1) * per, last), 0))
    wide = _sds((s, c), BF)
    return _call(body, name, (_sds((s, d), BF), wide, wide, _sds((1, d), F32), _sds((3, c), F32), _sds((1, c), F32)), (steps,),
                 [_rows(tm, d), nxt(d), _rows(tm, d), _vec(d), _resident(w_down.shape), _rows(tm, c), _rows(tm, c), nxt(c),
                  _rows(tm, c), nxt(c), pl.BlockSpec((3, c), lambda i: (0, 0))],
                 (_rows(tm, d), _rows(tm, c), _rows(tm, c), _vec(d), pl.BlockSpec((3, c), lambda i: (0, 0)), _vec(c)),
                 ("arbitrary",), after=after)(dx2, dx2, dn, g2, w_down, a, ac, ac, up, up, cw)


def _ffn_up_bwd(name, da, dup, w_gate, w_up, x1, dx2, w2, sc2, o, g1, tm=512, after=None):
    s, d = x1.shape
    tm = min(tm, s)

    def body(da_ref, dup_ref, wg_ref, wu_ref, x1_ref, dx2_ref, w2_ref, sc2_ref, o_ref, g1_ref,
             dx1_ref, do_ref, dnw_ref, dsh_ref, dg1_ref):
        @pl.when(pl.program_id(0) == 0)
        def _():
            dnw_ref[...] = jnp.zeros_like(dnw_ref)
            dsh_ref[...] = jnp.zeros_like(dsh_ref)
            dg1_ref[...] = jnp.zeros_like(dg1_ref)

        dh = _dot(da_ref[...], wg_ref[...].reshape(FFN_DIM, d)) + _dot(dup_ref[...], wu_ref[...].reshape(FFN_DIM, d))
        xv = x1_ref[...]
        r = _rms(xv)
        xn = xv * r
        dxn = dh * (w2_ref[...] * (1.0 + sc2_ref[...]))
        dx1 = dx2_ref[...] + r * (dxn - xn * jnp.mean(dxn * xn, axis=-1, keepdims=True))
        dx1_ref[...] = dx1
        dnw_ref[...] += jnp.sum(dh * xn, axis=0, keepdims=True)
        dsh_ref[...] += jnp.sum(dh, axis=0, keepdims=True)
        do_ref[...] = (dx1 * g1_ref[...]).astype(BF)
        dg1_ref[...] += jnp.sum(dx1 * o_ref[...].astype(F32), axis=0, keepdims=True)

    v = _sds((1, d), F32)
    r = _rows(tm, d)
    wspec = _resident(w_gate.shape)
    return _call(body, name, (_sds((s, d), F32), _sds((s, d), BF), v, v, v), (s // tm,),
                 [_rows(tm, FFN_DIM), _rows(tm, FFN_DIM), wspec, wspec, r, r, _vec(d), _vec(d), r, _vec(d)],
                 (r, r, _vec(d), _vec(d), _vec(d)), ("arbitrary",), after=after)(da, dup, w_gate, w_up, x1, dx2, w2, sc2, o, g1)


def _loss_head(name, x, w, target, tm=512):
    s, d = x.shape
    tm = min(tm, s)

    def body(x_ref, w_ref, t_ref, dx_ref, loss_ref, dw_ref):
        @pl.when(pl.program_id(0) == 0)
        def _():
            loss_ref[...] = jnp.zeros_like(loss_ref)
            dw_ref[...] = jnp.zeros_like(dw_ref)

        xv = x_ref[...]
        r = _rms(xv)
        xn = xv * r
        err = xn * w_ref[...] - t_ref[...]
        loss_ref[...] += 0.5 * jnp.sum(jnp.mean(err * err, axis=-1, keepdims=True))
        dy = err * (1.0 / d)
        dw_ref[...] += jnp.sum(dy * xn, axis=0, keepdims=True)
        dxn = dy * w_ref[...]
        dx_ref[...] = r * (dxn - xn * jnp.mean(dxn * xn, axis=-1, keepdims=True))

    return _call(body, name, (_sds((s, d), F32), _sds((1, LANES), F32), _sds((1, d), F32)), (s // tm,),
                 [_rows(tm, d), _vec(d), _rows(tm, d)], (_rows(tm, d), _vec(LANES), _vec(d)), ("arbitrary",))(x, w, target)


def _layer_fwd(l, x, mod, p, cosf, sinf, after=None, late=None, later=None):
    sh1, sc1, g1, sh2, sc2, g2 = mod
    tag = f"l{l}_"
    h, qr, kk0, kk1, vv0, vv1, u_pre, v_pre, ga_pre, gb_pre = _in_proj(
        tag + "in_proj", x, p["norm1_w"], sc1, sh1, p["w_in"], cosf, sinf, after=after)
    y_attn = _attention(tag + "attn", qr, kk0, kk1, vv0, vv1, p["sinks"])
    y_sgu = _sgu(tag + "sgu", u_pre, v_pre, p["sgu_ln_w"], p["sgu_ln_b"], p["sgu_w"], p["sgu_bfull"])
    if late is not None:
        p = dict(p, **late(y_sgu))
    merged, pa, pb, o, x1, h2 = _mix_out(tag + "mix_out", y_sgu, y_attn, ga_pre, gb_pre, x, g1, p["proj_a"], p["proj_b"],
                                         p["w_out"], p["norm2_w"], sc2, sh2)
    if later is not None:
        p = dict(p, **later(h2))
    a, ac, up, hf = _ffn_up_act(tag + "ffn_up", h2, p["w_gate"], p["w_up"], p["conv_w"], p["conv_b"])
    dn, x2 = _ffn_down(tag + "ffn_down", hf, p["w_down"], x1, g2)
    saved = dict(x=x, h=h, qr=qr, kk0=kk0, kk1=kk1, vv0=vv0, vv1=vv1, u_pre=u_pre, v_pre=v_pre, ga_pre=ga_pre,
                 gb_pre=gb_pre, y_attn=y_attn, y_sgu=y_sgu, merged=merged, pa=pa, pb=pb, o=o, x1=x1, h2=h2, a=a, ac=ac, up=up,
                 hf=hf, dn=dn)
    return x2, saved, p


def _layer_bwd(l, dx2, mod, p, sv, cosf, sinf, after=None, emit=None, tick=None):
    sh1, sc1, g1, sh2, sc2, g2 = mod
    tag = f"l{l}_b_"
    d = D_MODEL
    g = {}
    ready = (lambda names: emit({k: g.pop(k) for k in names})) if emit else (lambda names: None)
    tick = tick or (lambda y: None)
    ddn, da, dup, dg2, g["conv_w"], g["conv_b"] = _ffn_down_bwd_act(
        tag + "ffn_down", dx2, sv["dn"], g2, p["w_down"], sv["a"], sv["ac"], sv["up"], p["conv_w"], after=after)
    g["w_down"] = _matmul_tn(tag + "dw_down", sv["hf"], ddn, tk=FFN_CHUNK, after=tick(ddn)).reshape(N_CHIPS, FFN_SHARD, d)
    g["w_gate"] = _matmul_tn(tag + "dw_gate", da, sv["h2"], tk=FFN_CHUNK).reshape(N_CHIPS, FFN_SHARD, d)
    g["w_up"] = _matmul_tn(tag + "dw_up", dup, sv["h2"], tk=FFN_CHUNK).reshape(N_CHIPS, FFN_SHARD, d)
    dx1, do, da2, dsh2, dg1 = _ffn_up_bwd(tag + "ffn_up", da, dup, p["w_gate"], p["w_up"], sv["x1"], dx2, p["norm2_w"],
                                          sc2, sv["o"], g1, after=ready(("w_down", "w_gate", "w_up")))
    g["norm2_w"] = da2 * (1.0 + sc2)
    dsc2 = da2 * p["norm2_w"]
    g["w_out"] = _matmul_tn(tag + "dw_out", sv["merged"], do, after=tick(do)).reshape(N_CHIPS, d // N_CHIPS, d)
    dpa, dpb, dga, dgb, dy_sgu, dy_attn = _mix_bwd(tag + "mix", do, p["w_out"], p["proj_a"], p["proj_b"], sv["ga_pre"],
                                                  sv["gb_pre"], sv["pa"], sv["pb"])
    g["proj_a"] = _matmul_tn(tag + "dproj_a", sv["y_sgu"], dpa).reshape(N_CHIPS, d // N_CHIPS, d)
    g["proj_b"] = _matmul_tn(tag + "dproj_b", sv["y_attn"], dpb).reshape(N_CHIPS, d // N_CHIPS, d)
    du, dv, g["sgu_w"], g["sgu_b"], g["sgu_ln_w"], g["sgu_ln_b"] = _sgu_bwd(
        tag + "sgu", sv["u_pre"], sv["v_pre"], p["sgu_ln_w"], p["sgu_ln_b"], p["sgu_w"], p["sgu_bfull"], dy_sgu,
        after=ready(("w_out", "proj_a", "proj_b")))
    dqr, dkv_cur, dkv_prev, dsink = _attention_bwd(tag + "attn", sv["qr"], sv["kk0"], sv["kk1"], sv["vv0"], sv["vv1"],
                                                   p["sinks"], dy_attn, after=tick(du))
    g["sinks"] = dsink[0, :N_Q_HEADS]
    dq, dkv = _rope_bwd(tag + "rope", dqr, dkv_cur, dkv_prev, cosf, sinf)
    dw_in, row0 = None, 0
    for n, t in (("q", dq), ("kv", dkv), ("u", du), ("v", dv), ("ga", dga), ("gb", dgb)):
        dw_in = _matmul_tn_rows(tag + "dw_in_" + n, t, sv["h"], dw_in, row0, IN_COLS)
        row0 += t.shape[1]
    g["w_in"] = dw_in.reshape(N_CHIPS, IN_COLS // N_CHIPS, d)
    dx, da1, dsh1 = _in_proj_bwd(tag + "in_proj", dq, dkv, du, dv, dga, dgb, p["w_in"], sv["x"], p["norm1_w"], sc1, dx1)
    g["norm1_w"] = da1 * (1.0 + sc1)
    dsc1 = da1 * p["norm1_w"]
    return dx, (dsh1, dsc1, dg1, dsh2, dsc2, dg2), g


def _pad_to(a, axis, size):
    pad = [(0, 0)] * a.ndim
    pad[axis] = (0, size - a.shape[axis])
    return jnp.pad(a, pad)


def _early_params(w_in, small):
    d = D_MODEL
    return dict(
        w_in=w_in.reshape(IN_COLS, d), norm1_w=small["norm1_w"].reshape(1, d), sinks=small["sinks"],
        sgu_ln_w=small["sgu_ln_w"].reshape(1, d), sgu_ln_b=small["sgu_ln_b"].reshape(1, d), sgu_w=small["sgu_w"],
        sgu_bfull=jnp.broadcast_to(small["sgu_b"][:, :, None], (SGU_GROUPS, SGU_CHUNK, SGU_CHUNK)))


def _mix_params(proj_a, proj_b, w_out, small):
    return dict(proj_a=proj_a, proj_b=proj_b, w_out=w_out, norm2_w=small["norm2_w"].reshape(1, D_MODEL))


def _ffn_params(w_gate, w_up, w_down, conv_w, small):
    return dict(
        w_gate=w_gate, w_up=w_up, w_down=w_down, conv_w=conv_w.transpose(1, 0, 2).reshape(3, FFN_DIM),
        conv_b=small["conv_b"].reshape(1, FFN_DIM))


def _layer_params(w_in, proj_a, proj_b, w_out, w_gate, w_up, w_down, conv_w, small):
    return dict(_early_params(w_in, small), **_mix_params(proj_a, proj_b, w_out, small),
                **_ffn_params(w_gate, w_up, w_down, conv_w, small))


def _conv_grads_natural(g):
    cw = g["conv_w"]
    cb = g["conv_b"].reshape(FFN_DIM)
    return cw, cb


def _rope_tables(positions):
    inv_freq = ROPE_THETA ** (-jnp.arange(0, ROT_DIM, 2, dtype=F32) / ROT_DIM)
    ang = positions.astype(F32)[:, None] * inv_freq
    cos, sin = jnp.cos(ang), jnp.sin(ang)
    s = positions.shape[0]
    rest = HEAD_DIM - ROT_DIM
    cos_head = jnp.concatenate([cos, cos, jnp.ones((s, rest), F32)], axis=1)
    sin_head = jnp.concatenate([-sin, sin, jnp.zeros((s, rest), F32)], axis=1)
    return jnp.tile(cos_head, (1, LANES // HEAD_DIM)), jnp.tile(sin_head, (1, LANES // HEAD_DIM))


ADA_ROWS = 16


def _ada_fwd(name, c_rows, ada_w, ada_b_cols, tn=512):
    depth, d, n = ada_w.shape

    def body(c_ref, w_ref, b_ref, o_ref):
        cv = c_ref[...]
        act = (cv * _sigmoid(cv)).astype(BF)
        o_ref[0] = _dot(act, w_ref[0].astype(BF)) + b_ref[0]

    return _call(body, name, _sds((depth, ADA_ROWS, n), F32), (depth, n // tn),
                 [pl.BlockSpec((ADA_ROWS, d), lambda l, j: (0, 0)), pl.BlockSpec((1, d, tn), lambda l, j: (l, 0, j)),
                  pl.BlockSpec((1, 1, tn), lambda l, j: (l, 0, j))],
                 pl.BlockSpec((1, ADA_ROWS, tn), lambda l, j: (l, 0, j)), ("parallel", "parallel"))(c_rows, ada_w, ada_b_cols)


def _ada_bwd(name, c_rows, dmod_cols, tn=512):
    depth, _, n = dmod_cols.shape
    d = c_rows.shape[1]

    def body(c_ref, dm_ref, o_ref):
        cv = c_ref[...]
        act = (cv * _sigmoid(cv)).astype(BF)
        o_ref[0] = _dot_tn(act, dm_ref[0].astype(BF))

    return _call(body, name, _sds((depth, d, n), F32), (depth, n // tn),
                 [pl.BlockSpec((ADA_ROWS, d), lambda l, j: (0, 0)), pl.BlockSpec((1, ADA_ROWS, tn), lambda l, j: (l, 0, j))],
                 pl.BlockSpec((1, d, tn), lambda l, j: (l, 0, j)), ("parallel", "parallel"))(c_rows, dmod_cols)


def _colsum(name, a):
    r, n = a.shape

    def body(a_ref, o_ref):
        o_ref[...] = jnp.sum(a_ref[...], axis=0, keepdims=True)

    return _call(body, name, _sds((1, n), F32), (1,), [pl.BlockSpec((r, n), lambda i: (0, 0))],
                 pl.BlockSpec((1, n), lambda i: (0, 0)), ("arbitrary",))(a)


REL_SIBLING = (0, 0, 1)
REL_CHIPS = ((1, 0, 0), (0, 1, 0), (1, 1, 0))
REL_ALL = tuple((fx, fy, fc) for fx in (0, 1) for fy in (0, 1) for fc in (0, 1) if fx or fy or fc)


def _chip_of(dev):
    return 2 * dev[0] + dev[1]


def _dev_of(dev):
    return 4 * dev[0] + 2 * dev[1] + dev[2]


def _flip(dev, rel):
    return tuple(1 - m if f else m for m, f in zip(dev, rel))


def _exchange(name, arrays, n_out, stages, aliases=None):
    out_shapes, stages = stages[0], stages[1:]
    n_in = len(arrays)
    aliases = aliases or {}
    n_remote = sum(len(plan) for plan, _ in stages)
    n_local = sum(len(local) for _, local in stages)

    def at(ref, idx):
        return ref.at[idx] if len(idx) else ref

    def body(*refs):
        bufs = list(refs[:n_in + n_out])
        for i_in, i_out in aliases.items():
            bufs[i_in] = bufs[n_in + i_out]
        send_sems, recv_sems, local_sems = refs[n_in + n_out:]
        me = (lax.axis_index("x"), lax.axis_index("y"), lax.axis_index("c"))
        base_r = base_l = 0
        pending = []
        for plan, local in stages:
            def remote(k, entry, sender, receiver):
                rel, si, ssel, di, dsel = entry
                return pltpu.make_async_remote_copy(
                    src_ref=at(bufs[si], ssel(sender, receiver)), dst_ref=at(bufs[di], dsel(sender, receiver)),
                    send_sem=send_sems.at[k], recv_sem=recv_sems.at[k], device_id=_flip(me, rel), device_id_type=MESH)

            sends = [remote(base_r + k, e, me, _flip(me, e[0])) for k, e in enumerate(plan)]
            for cp in sends:
                cp.start()
            for k, (si, ssel, di, dsel) in enumerate(local):
                cp = pltpu.make_async_copy(at(bufs[si], ssel(me)), at(bufs[di], dsel(me)), local_sems.at[base_l + k])
                cp.start()
                pending.append(cp.wait)
            for k, e in enumerate(plan):
                remote(base_r + k, e, _flip(me, e[0]), me).wait_recv()
            pending += [cp.wait_send for cp in sends]
            base_r += len(plan)
            base_l += len(local)
        for wait in pending:
            wait()

    any_spec = pl.BlockSpec(memory_space=pl.ANY)
    return pl.pallas_call(
        body, name=name, out_shape=tuple(out_shapes), in_specs=[any_spec] * n_in, out_specs=tuple([any_spec] * n_out),
        input_output_aliases=dict(aliases),
        scratch_shapes=[pltpu.SemaphoreType.DMA((max(n_remote, 1),)), pltpu.SemaphoreType.DMA((max(n_remote, 1),)),
                        pltpu.SemaphoreType.DMA((max(n_local, 1),))])(*arrays)


HBM_SPEC = pl.BlockSpec(memory_space=pltpu.HBM)
SEM_SPEC = pl.BlockSpec(memory_space=pltpu.SEMAPHORE)


def _split_copies(bufs, plan, local, send_sems, recv_sems, local_sems):
    me = (lax.axis_index("x"), lax.axis_index("y"), lax.axis_index("c"))

    def at(ref, idx):
        return ref.at[idx] if len(idx) else ref

    def remote(k, sender, receiver):
        rel, si, ssel, di, dsel = plan[k]
        return pltpu.make_async_remote_copy(
            src_ref=at(bufs[si], ssel(sender, receiver)), dst_ref=at(bufs[di], dsel(sender, receiver)),
            send_sem=send_sems.at[k], recv_sem=recv_sems.at[k], device_id=_flip(me, rel), device_id_type=MESH)

    sends = [remote(k, me, _flip(me, plan[k][0])) for k in range(len(plan))]
    arrivals = [remote(k, _flip(me, plan[k][0]), me) for k in range(len(plan))]
    locs = [pltpu.make_async_copy(at(bufs[si], ssel(me)), at(bufs[di], dsel(me)), local_sems.at[k])
            for k, (si, ssel, di, dsel) in enumerate(local)]
    return sends, arrivals, locs


def _exchange_start(name, arrays, out_shapes, plan, local, inplace=False):
    n_in, n_out = len(arrays), len(out_shapes)
    n_buf = n_in + n_out

    def body(*refs):
        sems = refs[n_buf:n_buf + 3]
        bufs = refs[n_buf + 3:2 * n_buf + 3]
        sends, _, locs = _split_copies(bufs * 2 if inplace else bufs, plan, local, *sems)
        for cp in sends + locs:
            cp.start()
        refs[-1][...] = jnp.zeros_like(refs[-1])

    zones = [lax.empty(o.shape, o.dtype) for o in out_shapes]
    operands = [pltpu.with_memory_space_constraint(a, pltpu.HBM) for a in list(arrays) + zones]
    sem = lambda n: pltpu.SemaphoreType.DMA((max(n, 1),))
    out = pl.pallas_call(
        body, name=name,
        out_shape=(sem(len(plan)), sem(len(plan)), sem(len(local)), *[pltpu.HBM(a.shape, a.dtype) for a in operands],
                   _sds((SUBLANES, LANES), F32)),
        in_specs=[HBM_SPEC] * n_buf,
        out_specs=(SEM_SPEC, SEM_SPEC, SEM_SPEC, *[HBM_SPEC] * n_buf, pl.BlockSpec(memory_space=pltpu.VMEM)),
        input_output_aliases={i: 3 + i for i in range(n_buf)},
        compiler_params=pltpu.CompilerParams(has_side_effects=pltpu.SideEffectType.DATAFLOW_SIDE_EFFECTING))(*operands)
    pending = dict(name=name, sems=out[:3], thru=out[3:3 + n_in], zones=out[3 + n_in:3 + n_buf], plan=plan, local=local,
                   inplace=inplace)
    return pending, out[-1]


def _exchange_wait(pending, after, both=False):
    thru, zones, plan, local, inplace = (pending[k] for k in ("thru", "zones", "plan", "local", "inplace"))
    n_in, n_buf = len(thru), len(thru) + len(zones)

    def body(*refs):
        bufs = refs[:n_buf]
        sends, arrivals, locs = _split_copies(bufs * 2 if inplace else bufs, plan, local, *refs[n_buf:n_buf + 3])
        for cp in arrivals:
            cp.wait_recv()
        for cp in sends:
            cp.wait_send()
        for cp in locs:
            cp.wait()

    out = pl.pallas_call(
        body, name=pending["name"] + "_wait", out_shape=tuple(pltpu.HBM(a.shape, a.dtype) for a in list(thru) + list(zones)),
        in_specs=[HBM_SPEC] * n_buf + [SEM_SPEC] * 3 + [pl.BlockSpec(memory_space=pl.ANY)],
        out_specs=tuple([HBM_SPEC] * n_buf), input_output_aliases={i: i for i in range(n_buf)},
        compiler_params=pltpu.CompilerParams(has_side_effects=pltpu.SideEffectType.DATAFLOW_SIDE_EFFECTING))(
            *thru, *zones, *pending["sems"], after)
    if both:
        return out[:n_in], out[n_in:]
    return out[:n_in] if inplace else out[n_in:]


def _whole(*_):
    return ()


def _half_rows(rows, core):
    return pl.ds(core * (rows // 2), rows // 2)


def _gather_weights_plan(shards):
    n = len(shards)
    dsts = [_sds((N_CHIPS,) + a.shape, a.dtype) for a in shards]
    fetch, forward = [], []
    for t, a in enumerate(shards):
        rows = a.shape[0]
        if rows % (2 * 16) == 0:
            fetch += [(rel, t, (lambda s_, r_, rows=rows: (_half_rows(rows, s_[2]),)), n + t,
                       (lambda s_, r_, rows=rows: (_chip_of(s_), _half_rows(rows, s_[2])))) for rel in REL_CHIPS]
            forward += [(REL_SIBLING, n + t, (lambda s_, r_, rows=rows, rel=rel: (_chip_of(_flip(s_, rel)), _half_rows(rows, s_[2]))),
                         n + t, (lambda s_, r_, rows=rows, rel=rel: (_chip_of(_flip(s_, rel)), _half_rows(rows, s_[2]))))
                        for rel in REL_CHIPS]
        else:
            fetch += [(rel, t, _whole, n + t, lambda s_, r_: (_chip_of(s_),)) for rel in REL_CHIPS]
    local = [(t, _whole, n + t, lambda me: (_chip_of(me),)) for t in range(n)]
    return dsts, fetch, local, forward


def _gather_weights_start(name, shards):
    dsts, fetch, local, forward = _gather_weights_plan(shards)
    pending, token = _exchange_start(name, shards, dsts, fetch, local)
    return dict(pending, forward=forward), token


def _gather_weights_finish(pending, after):
    landed = _exchange_wait(pending, after)
    n = len(landed)
    return _exchange(pending["name"] + "_forward", landed, n, [[_sds(a.shape, a.dtype) for a in landed], (pending["forward"], [])],
                     aliases={t: t for t in range(n)})


def _gather_chips_plan(arrays):
    n = len(arrays)
    dsts = [_sds((N_CHIPS,) + a.shape, a.dtype) for a in arrays]
    plan = [(rel, t, _whole, n + t, lambda s_, r_: (_chip_of(s_),)) for t in range(n) for rel in REL_CHIPS]
    local = [(t, _whole, n + t, lambda me: (_chip_of(me),)) for t in range(n)]
    return dsts, plan, local


def _gather_chips(name, arrays):
    dsts, plan, local = _gather_chips_plan(arrays)
    return _exchange(name, arrays, len(arrays), [dsts, (plan, local)])


def _gather_chips_start(name, arrays):
    dsts, plan, local = _gather_chips_plan(arrays)
    return _exchange_start(name, arrays, dsts, plan, local)


def _gather_all(name, a):
    plan = [(rel, 0, _whole, 1, lambda s_, r_: (_dev_of(s_),)) for rel in REL_ALL]
    local = [(0, _whole, 1, lambda me: (_dev_of(me),))]
    return _exchange(name, [a], 1, [[_sds((2 * N_CHIPS,) + a.shape, a.dtype)], (plan, local)])[0]


def _swap_halves_start(name, grads):
    n = len(grads)
    dsts = [_sds((g.shape[0], g.shape[1] // 2, g.shape[2]), g.dtype) for g in grads]
    plan = [(REL_SIBLING, t, (lambda s_, r_, rows=g.shape[1]: (pl.ds(0, N_CHIPS), _half_rows(rows, r_[2]))), n + t, _whole)
            for t, g in enumerate(grads)]
    return _exchange_start(name, grads, dsts, plan, [])


def _scatter_chips_plan(sums):
    n = len(sums)
    dsts = [_sds(a.shape, a.dtype) for a in sums]
    plan = [(rel, t, lambda s_, r_: (_chip_of(r_),), n + t, lambda s_, r_: (_chip_of(s_),))
            for t in range(n) for rel in REL_CHIPS]
    local = [(t, lambda me: (_chip_of(me),), n + t, lambda me: (_chip_of(me),)) for t in range(n)]
    return dsts, plan, local


def _scatter_chips_start(name, sums):
    dsts, plan, local = _scatter_chips_plan(sums)
    return _exchange_start(name, sums, dsts, plan, local)


def _swap_back_start(name, totals, layer):
    n = len(totals)
    plan = [(REL_SIBLING, n + t, (lambda s_, r_, rows=a.shape[1]: (layer, _half_rows(rows, s_[2]))),
             n + t, (lambda s_, r_, rows=a.shape[1]: (layer, _half_rows(rows, s_[2])))) for t, a in enumerate(totals)]
    return _exchange_start(name, totals, [], plan, [], inplace=True)


def _add_halves(name, g, recv, core):
    nch, half, c = recv.shape

    def body(core_ref, g_ref, r_ref, o_ref):
        o_ref[0] = (g_ref[0, 0].astype(F32) + r_ref[0].astype(F32)).astype(o_ref.dtype)

    spec = pltpu.PrefetchScalarGridSpec(
        num_scalar_prefetch=1, grid=(nch,),
        in_specs=[pl.BlockSpec((1, 1, half, c), lambda k, core_ref: (k, core_ref[0], 0, 0)),
                  pl.BlockSpec((1, half, c), lambda k, core_ref: (k, 0, 0))],
        out_specs=pl.BlockSpec((1, half, c), lambda k, core_ref: (k, 0, 0)))
    return pl.pallas_call(body, name=name, out_shape=_sds(recv.shape, recv.dtype), grid_spec=spec,
                          compiler_params=pltpu.CompilerParams(dimension_semantics=("parallel",),
                                                               vmem_limit_bytes=VMEM_LIMIT))(
                                                                   core, g.reshape(nch, 2, half, c), recv)


def _sum_chips(name, a, core, layer, total):
    nch, half, c = a.shape

    def body(core_ref, a_ref, *rest):
        o_ref = rest[-1]
        acc = a_ref[0].astype(F32)
        for k in range(1, nch):
            acc = acc + a_ref[k].astype(F32)
        o_ref[0, 0] = acc

    in_specs = [pl.BlockSpec((nch, half, c), lambda i, core_ref: (0, 0, 0))]
    args = [core, a]
    if total is not None:
        in_specs.append(pl.BlockSpec(memory_space=pl.ANY))
        args.append(total.reshape(DEPTH, 2, half, c))
    spec = pltpu.PrefetchScalarGridSpec(
        num_scalar_prefetch=1, grid=(1,), in_specs=in_specs,
        out_specs=pl.BlockSpec((1, 1, half, c), lambda i, core_ref: (layer, core_ref[0], 0, 0)))
    out = pl.pallas_call(body, name=name, out_shape=_sds((DEPTH, 2, half, c), F32), grid_spec=spec,
                         input_output_aliases={2: 0} if total is not None else {},
                         compiler_params=pltpu.CompilerParams(dimension_semantics=("arbitrary",),
                                                              vmem_limit_bytes=VMEM_LIMIT))(*args)
    return out.reshape(DEPTH, 2 * half, c)


def _adamw_update(w, g, m, v):
    mn = ADAM_B1 * m + (1.0 - ADAM_B1) * g
    vn = ADAM_B2 * v + (1.0 - ADAM_B2) * (g * g)
    m_hat = mn / (1.0 - ADAM_B1 ** ADAM_STEP)
    v_hat = vn / (1.0 - ADAM_B2 ** ADAM_STEP)
    return -ADAM_LR * (m_hat / (jnp.sqrt(v_hat) + ADAM_EPS) + ADAM_WD * w), mn, vn


def _adamw(name, w, g, m, v, after=None):
    depth, r, c = w.shape
    tr = next(t for t in (512, 448, 384, 352, 336, 256, 192, 128, 64, 32, 16, 8) if r % t == 0 and t * c <= ADAM_TILE_ELEMS)

    def body(w_ref, g_ref, m_ref, v_ref, go_ref, d_ref, mo_ref, vo_ref):
        gv = g_ref[...]
        go_ref[...] = gv
        d_ref[...], mo_ref[...], vo_ref[...] = _adamw_update(w_ref[...], gv, m_ref[...], v_ref[...])

    spec = pl.BlockSpec((1, tr, c), lambda l, i: (l, i, 0))
    o = _sds(w.shape, F32)
    return _call(body, name, (o, o, o, o), (depth, r // tr), [spec] * 4, (spec,) * 4, ("parallel", "parallel"),
                 after=after)(w, g, m, v)


def _adamw_small(name, ws, gs, ms, vs):
    n = len(ws)

    def body(*refs):
        for t in range(n):
            w_ref, g_ref, m_ref, v_ref = (refs[k * n + t] for k in range(4))
            d_ref, mo_ref, vo_ref = (refs[(4 + k) * n + t] for k in range(3))
            d_ref[...], mo_ref[...], vo_ref[...] = _adamw_update(w_ref[...], g_ref[...], m_ref[...], v_ref[...])

    outs = [_sds(w.shape, F32) for w in ws]
    res = pl.pallas_call(body, name=name, out_shape=tuple(outs * 3))(*ws, *gs, *ms, *vs)
    return res[:n], res[n:2 * n], res[2 * n:]


def _pack(arrays, rows):
    flat = jnp.concatenate([a.reshape(-1).astype(F32) for a in arrays])
    return _pad_to(flat, 0, rows * LANES).reshape(rows, LANES)


def _unpack(packed, shapes):
    flat = packed.reshape(-1)
    out, off = [], 0
    for shp in shapes:
        n = 1
        for s_ in shp:
            n *= s_
        out.append(flat[off:off + n].reshape(shp))
        off += n
    return out


_MATRICES = ("w_in", "proj_a", "proj_b", "w_out", "w_gate", "w_up", "w_down")
_SMALL = (("norm1_w", (D_MODEL,)), ("sinks", (N_Q_HEADS,)), ("sgu_ln_w", (SGU_WIDTH,)), ("sgu_ln_b", (SGU_WIDTH,)),
          ("sgu_w", (SGU_GROUPS, SGU_CHUNK, SGU_CHUNK)), ("sgu_b", (SGU_GROUPS, SGU_CHUNK)), ("norm2_w", (D_MODEL,)),
          ("conv_w", (3, FFN_DIM)), ("conv_b", (FFN_DIM,)), ("final_norm_w", (D_MODEL,)))
SMALL_ROWS = 320
ADAM_TILE_ELEMS = 384 * 1024


def _reduce_cores_start(tag, partial):
    names = list(partial)
    pending, token = _swap_halves_start(tag + "_cores", [partial[k] for k in names])
    return dict(pending, tag=tag, names=names), token


def _reduce_chips_start(pending, core, after):
    tag, names = pending["tag"], pending["names"]
    mine, theirs = _exchange_wait(pending, after, both=True)
    sums = [_add_halves(f"{tag}_cores_add_{k}", g, r, core) for k, g, r in zip(names, mine, theirs)]
    scatter, token = _scatter_chips_start(tag + "_chips", sums)
    return dict(scatter, tag=tag, names=names), token


def _reduce_back_start(pending, l, core, totals, after):
    tag, names = pending["tag"], pending["names"]
    sums = [_sum_chips(f"{tag}_chips_add_{k}", a, core, l, totals.get(k)) for k, a in zip(names, _exchange_wait(pending, after))]
    back, _ = _swap_back_start(tag + "_back", sums, l)
    return dict(back, names=names)


def kernel(x, c, positions, ada_w, ada_b, norm1_w, w_in, attn_sinks, sgu_ln_w, sgu_ln_b, sgu_w, sgu_b, proj_a, proj_b, w_out, norm2_w, ffn_w_gate, ffn_w_up, ffn_conv_w, ffn_conv_b, ffn_w_down, final_norm_w, loss_target, m_ada_w, m_ada_b, m_norm1_w, m_w_in, m_attn_sinks, m_sgu_ln_w, m_sgu_ln_b, m_sgu_w, m_sgu_b, m_proj_a, m_proj_b, m_w_out, m_norm2_w, m_ffn_w_gate, m_ffn_w_up, m_ffn_conv_w, m_ffn_conv_b, m_ffn_w_down, m_final_norm_w, v_ada_w, v_ada_b, v_norm1_w, v_w_in, v_attn_sinks, v_sgu_ln_w, v_sgu_ln_b, v_sgu_w, v_sgu_b, v_proj_a, v_proj_b, v_w_out, v_norm2_w, v_ffn_w_gate, v_ffn_w_up, v_ffn_conv_w, v_ffn_conv_b, v_ffn_w_down, v_final_norm_w):
    d = D_MODEL
    ax, ay, ac = lax.axis_index("x"), lax.axis_index("y"), lax.axis_index("c")
    chip = 2 * ax + ay
    dev = 4 * ax + 2 * ay + ac
    core = ac.astype(jnp.int32).reshape(1)

    c_all = _gather_all("gather_cond", c.reshape(SUBLANES, d // SUBLANES)).reshape(2 * N_CHIPS, d)
    c_rows = _pad_to(c_all, 0, ADA_ROWS)
    ada_cols = ada_w.shape[2]
    ada_b_cols = lax.dynamic_slice_in_dim(ada_b, chip * ada_cols, ada_cols, axis=1).reshape(DEPTH, 1, ada_cols)
    mod_cols = _ada_fwd("ada_fwd", c_rows, ada_w, ada_b_cols)
    mod_all = _gather_chips("gather_mod", [mod_cols])[0]
    mod_mine = lax.dynamic_index_in_dim(mod_all, dev, axis=2, keepdims=False)
    mod_mine = mod_mine.transpose(1, 0, 2).reshape(DEPTH, 1, 6 * d)
    mods = [tuple(jnp.split(mod_mine[l], 6, axis=-1)) for l in range(DEPTH)]

    tr = lambda a: jnp.swapaxes(a, 1, 2)
    shards = [tr(w_in).astype(BF), proj_a.astype(BF), proj_b.astype(BF), w_out.astype(BF),
              tr(ffn_w_gate).astype(BF), tr(ffn_w_up).astype(BF), ffn_w_down.astype(BF), ffn_conv_w]
    token = mod_all[0, 0, :SUBLANES, :LANES]
    fetches = []
    for l in range(DEPTH):
        groups = []
        for tag, members in (("in", shards[:1]), ("mix", shards[1:4]), ("ffn", shards[4:])):
            behind = (token[0, 0] * 0.0).astype(members[0].dtype)
            pending, token = _gather_weights_start(f"l{l}_gather_{tag}", [members[0][l] + behind] + [a[l] for a in members[1:]])
            groups.append(pending)
        fetches.append(groups)

    small_in = dict(norm1_w=norm1_w, sinks=attn_sinks, sgu_ln_w=sgu_ln_w, sgu_ln_b=sgu_ln_b, sgu_w=sgu_w, sgu_b=sgu_b,
                    norm2_w=norm2_w, conv_b=ffn_conv_b)
    cosf, sinf = _rope_tables(positions[0])
    small_of = lambda l: {k: v[l] for k, v in small_in.items()}

    h = x[0]
    saved, params = [], []
    for l in range(DEPTH):
        first, mix, ffn = fetches[l]
        w_in_l = _gather_weights_finish(first, token if l == 0 else h)
        late = lambda y, l=l, mix=mix: _mix_params(*_gather_weights_finish(mix, y), small_of(l))
        later = lambda y, l=l, ffn=ffn: _ffn_params(*_gather_weights_finish(ffn, y), small_of(l))
        h, sv, p = _layer_fwd(l, h, mods[l], _early_params(w_in_l[0], small_of(l)), cosf, sinf, late=late, later=later)
        saved.append(sv)
        params.append(p)
    dx, loss_part, d_final = _loss_head("loss_head", h, final_norm_w.reshape(1, d), loss_target[0])
    loss = lax.psum(loss_part[0, 0], ("x", "y", "c"))

    def small_pack(l, grads):
        cw, cb = _conv_grads_natural(grads)
        nat = dict(grads, conv_w=cw, conv_b=cb, final_norm_w=d_final if l == DEPTH - 1 else jnp.zeros((d,), F32))
        return _pack([nat[k] for k, _ in _SMALL], N_CHIPS * SMALL_ROWS).reshape(N_CHIPS, SMALL_ROWS, LANES)

    waiting, inflight = [], []

    def send(tag, partial):
        pending, token = _reduce_cores_start(tag, partial)
        waiting.append(pending)
        return token

    def tick(y):
        token = None
        while waiting:
            pending, token = _reduce_chips_start(waiting.pop(0), core, y)
            inflight.append(pending)
        return token

    dmods = [None] * DEPTH
    dx, dmods[1], grads = _layer_bwd(1, dx, mods[1], params[1], saved[1], cosf, sinf)
    token = send("l1_reduce", dict({k: grads[k] for k in _MATRICES}, small=small_pack(1, grads)))
    dx, dmods[0], grads = _layer_bwd(0, dx, mods[0], params[0], saved[0], cosf, sinf, after=token,
                                     emit=lambda part: send("l0_reduce_" + "_".join(part), part), tick=tick)
    dmod_mine = jnp.concatenate([jnp.concatenate(dmods[l], axis=1) for l in range(DEPTH)], axis=1)
    dmod_all = _gather_all("gather_dmod", dmod_mine.reshape(SUBLANES, -1)).reshape(2 * N_CHIPS, DEPTH * 6 * d)
    send("l0_reduce_in", dict(w_in=grads["w_in"], small=small_pack(0, grads) + dmod_all[0, 0] * 0.0))
    tick(dmod_all)

    totals, flying = {}, None

    def land(after):
        if flying is not None:
            totals.update(zip(flying["names"], _exchange_wait(flying, after)))

    for pending in inflight[:-1]:
        land(dx)
        flying = _reduce_back_start(pending, 1 if pending["tag"].startswith("l1") else 0, core, totals, dx)
    land(dx)
    flying = None

    g_ada_b = _colsum("ada_b_grad", dmod_all).reshape(DEPTH, 6 * d)
    dmod_cols = jnp.stack([lax.dynamic_slice_in_dim(dmod_all, l * 6 * d + chip * ada_cols, ada_cols, axis=1)
                           for l in range(DEPTH)])
    g_ada_w = _ada_bwd("ada_w_grad", c_rows, _pad_to(dmod_cols, 1, ADA_ROWS))
    big = dict(w_in=(tr(w_in), tr(m_w_in), tr(v_w_in)), proj_a=(proj_a, m_proj_a, v_proj_a), proj_b=(proj_b, m_proj_b, v_proj_b),
               w_out=(w_out, m_w_out, v_w_out), w_gate=(tr(ffn_w_gate), tr(m_ffn_w_gate), tr(v_ffn_w_gate)),
               w_up=(tr(ffn_w_up), tr(m_ffn_w_up), tr(v_ffn_w_up)), w_down=(ffn_w_down, m_ffn_w_down, v_ffn_w_down))
    upd, g_big = {}, {}

    def update(k, after=None):
        res = _adamw("adamw_" + k, big[k][0], totals[k], *big[k][1:], after=after)
        res = [tr(a) for a in res] if k in ("w_in", "w_gate", "w_up") else res
        g_big[k], upd[k] = res[0], res[1:]

    for k in ("w_down", "w_gate", "w_up", "w_out", "proj_a", "proj_b"):
        update(k)
    g_big["ada_w"], *upd["ada_w"] = _adamw("adamw_ada_w", ada_w, g_ada_w, m_ada_w, v_ada_w)
    flying = _reduce_back_start(inflight[-1], 0, core, totals, upd["ada_w"][0])
    land(upd["ada_w"][1])
    gathering, token = _gather_chips_start("gather_small", [totals["small"]])
    update("w_in", after=token)
    small_all = _exchange_wait(gathering, upd["w_in"][0])[0]
    small_g = small_all.transpose(1, 0, 2, 3).reshape(DEPTH, -1)
    per_layer = [_unpack(small_g[l], [shp for _, shp in _SMALL]) for l in range(DEPTH)]
    sg = {k: jnp.stack([per_layer[l][i] for l in range(DEPTH)]) for i, (k, _) in enumerate(_SMALL)}
    g_final = sg["final_norm_w"][DEPTH - 1]
    g_conv_w = lax.dynamic_slice_in_dim(sg["conv_w"], chip * FFN_SHARD, FFN_SHARD, axis=2)

    rest = [("ada_b", ada_b, g_ada_b, m_ada_b, v_ada_b), ("norm1_w", norm1_w, sg["norm1_w"], m_norm1_w, v_norm1_w),
            ("attn_sinks", attn_sinks, sg["sinks"], m_attn_sinks, v_attn_sinks),
            ("sgu_ln_w", sgu_ln_w, sg["sgu_ln_w"], m_sgu_ln_w, v_sgu_ln_w),
            ("sgu_ln_b", sgu_ln_b, sg["sgu_ln_b"], m_sgu_ln_b, v_sgu_ln_b), ("sgu_w", sgu_w, sg["sgu_w"], m_sgu_w, v_sgu_w),
            ("sgu_b", sgu_b, sg["sgu_b"], m_sgu_b, v_sgu_b), ("norm2_w", norm2_w, sg["norm2_w"], m_norm2_w, v_norm2_w),
            ("ffn_conv_w", ffn_conv_w, g_conv_w, m_ffn_conv_w, v_ffn_conv_w),
            ("ffn_conv_b", ffn_conv_b, sg["conv_b"], m_ffn_conv_b, v_ffn_conv_b),
            ("final_norm_w", final_norm_w.reshape(1, d), g_final.reshape(1, d), m_final_norm_w.reshape(1, d),
             v_final_norm_w.reshape(1, d))]
    rest_out = _adamw_small("adamw_rest", *[[r[i] for r in rest] for i in (1, 2, 3, 4)])
    g_rest = {r[0]: r[2] for r in rest}
    u_rest = {r[0]: tuple(o[i] for o in rest_out) for i, r in enumerate(rest)}
    g_rest["final_norm_w"] = g_final
    u_rest["final_norm_w"] = tuple(a.reshape(d) for a in u_rest["final_norm_w"])

    names = ("ada_w", "ada_b", "norm1_w", "w_in", "attn_sinks", "sgu_ln_w", "sgu_ln_b", "sgu_w", "sgu_b", "proj_a", "proj_b",
             "w_out", "norm2_w", "ffn_w_gate", "ffn_w_up", "ffn_conv_w", "ffn_conv_b", "ffn_w_down", "final_norm_w")
    alias = {"ffn_w_gate": "w_gate", "ffn_w_up": "w_up", "ffn_w_down": "w_down"}
    grad_of = lambda n: g_rest[n] if n in g_rest else g_big[alias.get(n, n)]
    upd_of = lambda n: u_rest[n] if n in u_rest else upd[alias.get(n, n)]
    return (loss, dx[None], *[grad_of(n) for n in names], *[upd_of(n)[0] for n in names],
            *[upd_of(n)[1] for n in names], *[upd_of(n)[2] for n in names])
```

```python
import jax
import jax.numpy as jnp
from jax import lax
from jax.experimental import pallas as pl
from jax.experimental.pallas import tpu as pltpu

F32 = jnp.float32
BF = jnp.bfloat16

D_MODEL = 1024
N_Q_HEADS = 16
N_KV_HEADS = 2
HEAD_DIM = 64
ATTN_BLOCK = 128
ROPE_THETA = 500000.0
ROT_DIM = HEAD_DIM // 4
SGU_WIDTH = 1024
SGU_GROUPS = 8
SGU_CHUNK = 128
FFN_DIM = 2816
NORM_EPS = 1e-6
DEPTH = 2
IN_COLS = 5376
N_CHIPS = 4
FFN_SHARD = FFN_DIM // N_CHIPS
CHIPS_PER_CHUNK = 2
FFN_CHUNK = CHIPS_PER_CHUNK * FFN_SHARD
FFN_CHUNKS = FFN_DIM // FFN_CHUNK
LANES = 128
SUBLANES = 8
HALO = 16
VMEM_LIMIT = 56 * 1024 * 1024
NEG_BIG = -1e30

ADAM_LR = 0.001
ADAM_B1 = 0.9
ADAM_B2 = 0.999
ADAM_EPS = 1e-08
ADAM_WD = 0.01
ADAM_STEP = 10

MESH = pl.DeviceIdType.MESH

Q_END = 1024
KV_END = 1280
U_END = 2304
Z_END = 3328
GA_END = 4352


def _sds(shape, dtype):
    return jax.ShapeDtypeStruct(tuple(shape), dtype)


def _call(body, name, out_shape, grid, in_specs, out_specs, semantics, scratch=(), after=None):
    n_in = len(in_specs)
    fn = body
    if after is not None:
        def fn(*refs):
            return body(*refs[:n_in], *refs[n_in + 1:])

        in_specs = list(in_specs) + [pl.BlockSpec(memory_space=pl.ANY)]
    call = pl.pallas_call(
        fn, name=name, out_shape=out_shape, grid=grid, in_specs=in_specs, out_specs=out_specs,
        scratch_shapes=scratch,
        compiler_params=pltpu.CompilerParams(dimension_semantics=semantics, vmem_limit_bytes=VMEM_LIMIT))
    if after is None:
        return call
    return lambda *args: call(*args, after)


def _rows(tm, width, col=0):
    return pl.BlockSpec((tm, width), lambda i: (i, col))


def _vec(width):
    return pl.BlockSpec((1, width), lambda i: (0, 0))


def _resident(shape):
    zeros = (0,) * len(shape)
    return pl.BlockSpec(tuple(shape), lambda *_: zeros, pipeline_mode=pl.Buffered(1))


def _sigmoid(x):
    return 0.5 + 0.5 * jnp.tanh(0.5 * x)


def _gelu(x):
    return 0.5 * x * (1.0 + lax.erf(x * 0.7071067811865476))


def _gelu_and_grad(x):
    cdf = 0.5 * (1.0 + lax.erf(x * 0.7071067811865476))
    return x * cdf, cdf + x * jnp.exp(-0.5 * x * x) * 0.3989422804014327


def _dot(a, b):
    return jnp.dot(a, b, preferred_element_type=F32)


def _dot_nt(a, b):
    return lax.dot_general(a, b, (((1,), (1,)), ((), ())), preferred_element_type=F32)


def _dot_tn(a, b):
    return lax.dot_general(a, b, (((0,), (0,)), ((), ())), preferred_element_type=F32)


def _rms(xv):
    return lax.rsqrt(jnp.mean(xv * xv, axis=-1, keepdims=True) + NORM_EPS)


def _matmul_tn(name, a, b, tk=512, tn=1024, after=None):
    s, k = a.shape
    n = b.shape[1]
    tk, tn = min(tk, k), min(tn, n)

    def body(a_ref, b_ref, o_ref):
        o_ref[...] = _dot_tn(a_ref[...], b_ref[...]).astype(o_ref.dtype)

    return _call(body, name, _sds((k, n), BF), (k // tk, n // tn),
                 [pl.BlockSpec((s, tk), lambda i, j: (0, i)), pl.BlockSpec((s, tn), lambda i, j: (0, j))],
                 pl.BlockSpec((tk, tn), lambda i, j: (i, j)), ("parallel", "parallel"), after=after)(a, b)


def _matmul_tn_segments(name, segments, b, tk=256):
    s, n = b.shape
    counts = [a.shape[1] // tk for a in segments]
    firsts = [sum(counts[:t]) for t in range(len(segments))]

    def body(*refs):
        a_refs, b_ref, o_ref = refs[:len(segments)], refs[len(segments)], refs[-1]
        i = pl.program_id(0)
        for a_ref, first, count in zip(a_refs, firsts, counts):
            @pl.when((i >= first) & (i < first + count))
            def _(a_ref=a_ref):
                o_ref[...] = _dot_tn(a_ref[...], b_ref[...]).astype(BF)

    specs = [pl.BlockSpec((s, tk), lambda i, first=first, count=count: (0, jnp.clip(i - first, 0, count - 1)))
             for first, count in zip(firsts, counts)]
    return _call(body, name, _sds((sum(counts) * tk, n), BF), (sum(counts),), specs + [_resident(b.shape)],
                 pl.BlockSpec((tk, n), lambda i: (i, 0)), ("arbitrary",))(*segments, b)


def _rope_partner(v):
    lane = lax.broadcasted_iota(jnp.int32, (1, LANES), 1) % HEAD_DIM
    return jnp.where(lane < ROT_DIM // 2, pltpu.roll(v, LANES - ROT_DIM // 2, axis=1), pltpu.roll(v, ROT_DIM // 2, axis=1))


def _dup_half(v, half):
    lane = lax.broadcasted_iota(jnp.int32, (1, LANES), 1)
    keep = jnp.where((lane >= HEAD_DIM) == (half == 1), v, 0.0)
    return keep + pltpu.roll(keep, HEAD_DIM, axis=1)


def _in_proj(name, x, w, sc, sh, w_in, cosf, sinf, tm=512, after=None):
    s, d = x.shape
    tm = min(tm, s)

    def body(x_ref, w_ref, sc_ref, sh_ref, win_ref, cos_ref, sin_ref,
             h_ref, qr_ref, kk0_ref, kk1_ref, vv0_ref, vv1_ref, u_ref, v_ref, ga_ref, gb_ref):
        xv = x_ref[...]
        h = ((xv * _rms(xv)) * w_ref[...] * (1.0 + sc_ref[...]) + sh_ref[...]).astype(BF)
        h_ref[...] = h
        cosv, sinv = cos_ref[...], sin_ref[...]
        q = _dot_nt(h, win_ref[:Q_END, :])
        for j in range(D_MODEL // LANES):
            qv = q[:, j * LANES:(j + 1) * LANES]
            qr_ref[:, j * LANES:(j + 1) * LANES] = ((qv * cosv + _rope_partner(qv) * sinv) * ATTN_SCALE).astype(BF)
        kv = _dot_nt(h, win_ref[Q_END:KV_END, :])
        kr = kv[:, :LANES] * cosv + _rope_partner(kv[:, :LANES]) * sinv
        vv = kv[:, LANES:]
        kk0_ref[...] = _dup_half(kr, 0).astype(BF)
        kk1_ref[...] = _dup_half(kr, 1).astype(BF)
        vv0_ref[...] = _dup_half(vv, 0).astype(BF)
        vv1_ref[...] = _dup_half(vv, 1).astype(BF)
        u_ref[...] = _dot_nt(h, win_ref[KV_END:U_END, :])
        v_ref[...] = _dot_nt(h, win_ref[U_END:Z_END, :])
        ga_ref[...] = _dot_nt(h, win_ref[Z_END:GA_END, :]).astype(BF)
        gb_ref[...] = _dot_nt(h, win_ref[GA_END:, :]).astype(BF)

    wide, kvs, pre = _sds((s, d), BF), _sds((s, LANES), BF), _sds((s, d), F32)
    return _call(body, name, (wide, wide, kvs, kvs, kvs, kvs, pre, pre, wide, wide), (s // tm,),
                 [_rows(tm, d), _vec(d), _vec(d), _vec(d), _resident(w_in.shape), _rows(tm, LANES), _rows(tm, LANES)],
                 (_rows(tm, d), _rows(tm, d)) + (_rows(tm, LANES),) * 4 + (_rows(tm, d),) * 4, ("parallel",), after=after)(
                     x, w, sc, sh, w_in, cosf, sinf)


def _in_proj_bwd(name, dq, dkv, du, dv, dga, dgb, w_in, x, w, sc, dx_in, tm=512):
    s, d = x.shape
    tm = min(tm, s)

    def body(dq_ref, dkv_ref, du_ref, dv_ref, dga_ref, dgb_ref, win_ref, x_ref, w_ref, sc_ref, dxin_ref,
             dx_ref, da_ref, dsh_ref):
        @pl.when(pl.program_id(0) == 0)
        def _():
            da_ref[...] = jnp.zeros_like(da_ref)
            dsh_ref[...] = jnp.zeros_like(dsh_ref)

        dh = (_dot(dq_ref[...], win_ref[:Q_END, :]) + _dot(dkv_ref[...], win_ref[Q_END:KV_END, :])
              + _dot(du_ref[...], win_ref[KV_END:U_END, :]) + _dot(dv_ref[...], win_ref[U_END:Z_END, :])
              + _dot(dga_ref[...], win_ref[Z_END:GA_END, :]) + _dot(dgb_ref[...], win_ref[GA_END:, :]))
        xv = x_ref[...]
        r = _rms(xv)
        xn = xv * r
        dxn = dh * (w_ref[...] * (1.0 + sc_ref[...]))
        dx_ref[...] = dxin_ref[...] + r * (dxn - xn * jnp.mean(dxn * xn, axis=-1, keepdims=True))
        da_ref[...] += jnp.sum(dh * xn, axis=0, keepdims=True)
        dsh_ref[...] += jnp.sum(dh, axis=0, keepdims=True)

    return _call(body, name, (_sds((s, d), F32), _sds((1, d), F32), _sds((1, d), F32)), (s // tm,),
                 [_rows(tm, d), _rows(tm, 2 * LANES), _rows(tm, d), _rows(tm, d), _rows(tm, d), _rows(tm, d),
                  _resident(w_in.shape), _rows(tm, d), _vec(d), _vec(d), _rows(tm, d)],
                 (_rows(tm, d), _vec(d), _vec(d)), ("arbitrary",))(dq, dkv, du, dv, dga, dgb, w_in, x, w, sc, dx_in)


def _rope_bwd(name, dqr, dkv_cur, dkv_prev, cosf, sinf, tm=512):
    s = dqr.shape[0]
    tm = min(tm, s)
    steps = s // tm
    per = tm // ATTN_BLOCK
    nb = s // ATTN_BLOCK

    def unrope(v, cosv, sinv):
        return v * cosv - _rope_partner(v) * sinv

    def body(dq_ref, cur_ref, prev_ref, next_ref, cos_ref, sin_ref, dqo_ref, dkvo_ref):
        i = pl.program_id(0)
        cosv, sinv = cos_ref[...], sin_ref[...]
        for j in range(D_MODEL // LANES):
            dqo_ref[:, j * LANES:(j + 1) * LANES] = unrope(dq_ref[:, j * LANES:(j + 1) * LANES], cosv, sinv).astype(BF)
        nxt = jnp.where(i < steps - 1, next_ref[...], 0.0)
        if per > 1:
            shifted = jnp.concatenate([prev_ref[ATTN_BLOCK:, :], nxt], axis=0)
        else:
            shifted = nxt
        tot = cur_ref[...] + shifted
        dkvo_ref[:, :LANES] = unrope(tot[:, :LANES], cosv, sinv).astype(BF)
        dkvo_ref[:, LANES:] = tot[:, LANES:].astype(BF)

    nxt_spec = pl.BlockSpec((ATTN_BLOCK, 2 * LANES), lambda i: (jnp.minimum((i + 1) * per, nb - 1), 0))
    return _call(body, name, (_sds((s, D_MODEL), BF), _sds((s, 2 * LANES), BF)), (steps,),
                 [_rows(tm, D_MODEL), _rows(tm, 2 * LANES), _rows(tm, 2 * LANES), nxt_spec, _rows(tm, LANES),
                  _rows(tm, LANES)],
                 (_rows(tm, D_MODEL), _rows(tm, 2 * LANES)), ("parallel",))(dqr, dkv_cur, dkv_prev, dkv_prev, cosf, sinf)


Q_PER_KV = N_Q_HEADS // N_KV_HEADS
ATTN_SCALE = HEAD_DIM ** -0.5
HEADS_AHEAD_FWD = 2
HEADS_AHEAD_BWD = 3


def _band_mask_t(n):
    kj = lax.broadcasted_iota(jnp.int32, (2 * ATTN_BLOCK, ATTN_BLOCK), 0)
    qi = lax.broadcasted_iota(jnp.int32, (2 * ATTN_BLOCK, ATTN_BLOCK), 1)
    return (kj > qi) & (kj <= qi + ATTN_BLOCK) & ((n > 0) | (kj >= ATTN_BLOCK))


def _softmax_t(raw, allowed, sink):
    sc = jnp.where(allowed, raw, NEG_BIG)
    m = jnp.maximum(jnp.max(sc, axis=0, keepdims=True), sink)
    p = jnp.exp(sc - m)
    esink = jnp.exp(sink - m)
    inv = 1.0 / (jnp.sum(p, axis=0, keepdims=True) + esink)
    return p * inv, esink * inv


def _kv_specs():
    cur = pl.BlockSpec((ATTN_BLOCK, LANES), lambda n: (n, 0))
    prev = pl.BlockSpec((ATTN_BLOCK, LANES), lambda n: (jnp.maximum(n - 1, 0), 0))
    return [prev, cur] * 4


def _attention(name, qr, kk0, kk1, vv0, vv1, sinks):
    s = qr.shape[0]
    nb = s // ATTN_BLOCK

    def body(sink_ref, q_ref, k0p, k0c, k1p, k1c, v0p, v0c, v1p, v1c, y_ref):
        allowed = _band_mask_t(pl.program_id(0))
        upper = lax.broadcasted_iota(jnp.int32, (1, LANES), 1) >= HEAD_DIM
        upper_rows = lax.broadcasted_iota(jnp.int32, (LANES, 1), 0) >= HEAD_DIM
        bands = ((jnp.concatenate([k0p[...], k0c[...]], axis=0), jnp.concatenate([v0p[...], v0c[...]], axis=0)),
                 (jnp.concatenate([k1p[...], k1c[...]], axis=0), jnp.concatenate([v1p[...], v1c[...]], axis=0)))
        vbts = (bands[0][1].T, bands[1][1].T)

        def scores(h):
            hk, j, half = h // Q_PER_KV, (h % Q_PER_KV) // 2, h % 2
            col = (hk * 4 + j) * LANES
            qp = q_ref[:, col:col + LANES]
            return _dot_nt(bands[hk][0], jnp.where(upper if half else jnp.logical_not(upper), qp, jnp.zeros_like(qp)))

        out_t = None
        ahead = [scores(h) for h in range(HEADS_AHEAD_FWD)]
        for h in range(N_Q_HEADS):
            hk, j, half = h // Q_PER_KV, (h % Q_PER_KV) // 2, h % 2
            raw = ahead.pop(0)
            if h + HEADS_AHEAD_FWD < N_Q_HEADS:
                ahead.append(scores(h + HEADS_AHEAD_FWD))
            pn, _ = _softmax_t(raw, allowed, sink_ref[h])
            o_h = _dot(vbts[hk], pn.astype(BF))
            out_t = jnp.where(upper_rows, o_h, out_t) if half else o_h
            if half:
                col = (hk * 4 + j) * LANES
                y_ref[:, col:col + LANES] = out_t.T.astype(BF)

    return _call(body, name, _sds((s, D_MODEL), BF), (nb,),
                 [pl.BlockSpec(memory_space=pltpu.SMEM), pl.BlockSpec((ATTN_BLOCK, D_MODEL), lambda n: (n, 0))] + _kv_specs(),
                 pl.BlockSpec((ATTN_BLOCK, D_MODEL), lambda n: (n, 0)), ("parallel",))(
                     sinks, qr, kk0, kk0, kk1, kk1, vv0, vv0, vv1, vv1)


def _attention_bwd(name, qr, kk0, kk1, vv0, vv1, sinks, dy, after=None):
    s = qr.shape[0]
    nb = s // ATTN_BLOCK

    def body(sink_ref, q_ref, dy_ref, k0p, k0c, k1p, k1c, v0p, v0c, v1p, v1c, dq_ref, cur_ref, prev_ref, dsink_ref):
        @pl.when(pl.program_id(0) == 0)
        def _():
            dsink_ref[...] = jnp.zeros_like(dsink_ref)

        allowed = _band_mask_t(pl.program_id(0))
        lane = lax.broadcasted_iota(jnp.int32, (1, LANES), 1)
        upper = lane >= HEAD_DIM
        upper_rows = lax.broadcasted_iota(jnp.int32, (LANES, 1), 0) >= HEAD_DIM
        bands = ((jnp.concatenate([k0p[...], k0c[...]], axis=0), jnp.concatenate([v0p[...], v0c[...]], axis=0)),
                 (jnp.concatenate([k1p[...], k1c[...]], axis=0), jnp.concatenate([v1p[...], v1c[...]], axis=0)))
        kbts = (bands[0][0].T, bands[1][0].T)

        def scores(h):
            hk, j, half = h // Q_PER_KV, (h % Q_PER_KV) // 2, h % 2
            kb, vb = bands[hk]
            col = (hk * 4 + j) * LANES
            sel = upper if half else jnp.logical_not(upper)
            qp = q_ref[:, col:col + LANES]
            qa = jnp.where(sel, qp, jnp.zeros_like(qp))
            dya = jnp.where(sel, dy_ref[:, col:col + LANES], 0.0).astype(BF)
            return qa, dya, _dot_nt(kb, qa), _dot_nt(vb, dya)

        dsink = jnp.zeros((1, LANES), F32)
        dk_slab = jnp.zeros((2 * ATTN_BLOCK, LANES), F32)
        dv_slab = jnp.zeros((2 * ATTN_BLOCK, LANES), F32)
        dkk = dvv = dq_t = None
        ahead = [scores(h) for h in range(HEADS_AHEAD_BWD)]
        for h in range(N_Q_HEADS):
            hk, j, half = h // Q_PER_KV, (h % Q_PER_KV) // 2, h % 2
            qa, dya, raw, dp = ahead.pop(0)
            if h + HEADS_AHEAD_BWD < N_Q_HEADS:
                ahead.append(scores(h + HEADS_AHEAD_BWD))
            pn, psink = _softmax_t(raw, allowed, sink_ref[h])
            delta = jnp.sum(pn * dp, axis=0, keepdims=True)
            ds = (pn * (dp - delta)).astype(BF)
            dsink = dsink + jnp.where(lane == h, -jnp.sum(psink * delta), 0.0)
            dq_h = _dot(kbts[hk], ds) * ATTN_SCALE
            dq_t = jnp.where(upper_rows, dq_h, dq_t) if half else dq_h
            dk_h, dv_h = _dot(ds, qa), _dot(pn.astype(BF), dya)
            dkk, dvv = (dk_h, dv_h) if h % Q_PER_KV == 0 else (dkk + dk_h, dvv + dv_h)
            if half:
                col = (hk * 4 + j) * LANES
                dq_ref[:, col:col + LANES] = dq_t.T
            if h % Q_PER_KV == Q_PER_KV - 1:
                mine = upper if hk else jnp.logical_not(upper)
                dk_slab = jnp.where(mine, dkk + pltpu.roll(dkk, HEAD_DIM, axis=1), dk_slab)
                dv_slab = jnp.where(mine, dvv + pltpu.roll(dvv, HEAD_DIM, axis=1), dv_slab)
        prev_ref[:, :LANES] = dk_slab[:ATTN_BLOCK]
        prev_ref[:, LANES:] = dv_slab[:ATTN_BLOCK]
        cur_ref[:, :LANES] = dk_slab[ATTN_BLOCK:]
        cur_ref[:, LANES:] = dv_slab[ATTN_BLOCK:]
        dsink_ref[...] += dsink

    blk = pl.BlockSpec((ATTN_BLOCK, D_MODEL), lambda n: (n, 0))
    kvo = pl.BlockSpec((ATTN_BLOCK, 2 * LANES), lambda n: (n, 0))
    return _call(body, name,
                 (_sds((s, D_MODEL), F32), _sds((s, 2 * LANES), F32), _sds((s, 2 * LANES), F32), _sds((1, LANES), F32)),
                 (nb,), [pl.BlockSpec(memory_space=pltpu.SMEM), blk, blk] + _kv_specs(),
                 (blk, kvo, kvo, pl.BlockSpec((1, LANES), lambda n: (0, 0))), ("arbitrary",), after=after)(
                     sinks, qr, dy, kk0, kk0, kk1, kk1, vv0, vv0, vv1, vv1)


def _sgu_weights(wm_ref, g):
    t = lax.broadcasted_iota(jnp.int32, (SGU_CHUNK, SGU_CHUNK), 0)
    sidx = lax.broadcasted_iota(jnp.int32, (SGU_CHUNK, SGU_CHUNK), 1)
    return jnp.where(sidx <= t, wm_ref[g], 0.0).astype(BF)


def _layer_norm_stats(v):
    mu = jnp.mean(v, axis=-1, keepdims=True)
    cen = v - mu
    rstd = lax.rsqrt(jnp.mean(cen * cen, axis=-1, keepdims=True) + NORM_EPS)
    return cen * rstd, rstd


def _sgu(name, u_pre, v_pre, ln_w, ln_b, wm, bfull, tm=512):
    s, w = u_pre.shape
    tm = min(tm, s)

    def body(u_ref, v_ref, lw_ref, lb_ref, wm_ref, b_ref, y_ref):
        vhat, _ = _layer_norm_stats(_gelu(v_ref[...]))
        vn = (vhat * lw_ref[...] + lb_ref[...]).astype(BF)
        for g in range(SGU_GROUPS):
            wg = _sgu_weights(wm_ref, g)
            cols = slice(g * SGU_CHUNK, (g + 1) * SGU_CHUNK)
            for ch in range(tm // SGU_CHUNK):
                rows = slice(ch * SGU_CHUNK, (ch + 1) * SGU_CHUNK)
                f = _dot(wg, vn[rows, cols]) + b_ref[g]
                y_ref[rows, cols] = (_gelu(u_ref[rows, cols]) * f).astype(BF)

    full3 = pl.BlockSpec((SGU_GROUPS, SGU_CHUNK, SGU_CHUNK), lambda i: (0, 0, 0))
    return _call(body, name, _sds((s, w), BF), (s // tm,),
                 [_rows(tm, w), _rows(tm, w), _vec(w), _vec(w), full3, full3],
                 _rows(tm, w), ("parallel",))(u_pre, v_pre, ln_w, ln_b, wm, bfull)


def _sgu_bwd(name, u_pre, v_pre, ln_w, ln_b, wm, bfull, dy, tm=512, after=None):
    s, w = u_pre.shape
    tm = min(tm, s)
    steps = s // tm

    def body(u_ref, v_ref, lw_ref, lb_ref, wm_ref, b_ref, dy_ref, du_ref, dv_ref, dwm_ref, db_ref, dlw_ref, dlb_ref,
             dfsum_ref):
        i = pl.program_id(0)

        @pl.when(i == 0)
        def _():
            dwm_ref[...] = jnp.zeros_like(dwm_ref)
            dlw_ref[...] = jnp.zeros_like(dlw_ref)
            dlb_ref[...] = jnp.zeros_like(dlb_ref)
            dfsum_ref[...] = jnp.zeros_like(dfsum_ref)

        vpre = v_ref[...]
        vg, dvg_dv = _gelu_and_grad(vpre)
        vhat, rstd = _layer_norm_stats(vg)
        vn = (vhat * lw_ref[...] + lb_ref[...]).astype(BF)
        t = lax.broadcasted_iota(jnp.int32, (SGU_CHUNK, SGU_CHUNK), 0)
        sidx = lax.broadcasted_iota(jnp.int32, (SGU_CHUNK, SGU_CHUNK), 1)
        dvn_cols = []
        for g in range(SGU_GROUPS):
            wg = _sgu_weights(wm_ref, g)
            cols = slice(g * SGU_CHUNK, (g + 1) * SGU_CHUNK)
            dvn_rows = []
            dwg = jnp.zeros((SGU_CHUNK, SGU_CHUNK), F32)
            dfs = jnp.zeros((SGU_CHUNK, SGU_CHUNK), F32)
            for ch in range(tm // SGU_CHUNK):
                rows = slice(ch * SGU_CHUNK, (ch + 1) * SGU_CHUNK)
                upre = u_ref[rows, cols]
                dyv = dy_ref[rows, cols].astype(F32)
                f = _dot(wg, vn[rows, cols]) + b_ref[g]
                ug, dug_du = _gelu_and_grad(upre)
                du_ref[rows, cols] = (dyv * f * dug_du).astype(BF)
                df = dyv * ug
                dfb = df.astype(BF)
                dvn_rows.append(_dot_tn(wg, dfb))
                dwg = dwg + _dot_nt(dfb, vn[rows, cols])
                dfs = dfs + df
            dwm_ref[g] += jnp.where(sidx <= t, dwg, 0.0)
            dfsum_ref[g] += dfs
            dvn_cols.append(jnp.concatenate(dvn_rows, axis=0) if len(dvn_rows) > 1 else dvn_rows[0])
        dvn = jnp.concatenate(dvn_cols, axis=1)
        dlw_ref[...] += jnp.sum(dvn * vhat, axis=0, keepdims=True)
        dlb_ref[...] += jnp.sum(dvn, axis=0, keepdims=True)
        dvh = dvn * lw_ref[...]
        dvg = rstd * (dvh - jnp.mean(dvh, axis=-1, keepdims=True) - vhat * jnp.mean(dvh * vhat, axis=-1, keepdims=True))
        dv_ref[...] = (dvg * dvg_dv).astype(BF)

        @pl.when(i == steps - 1)
        def _():
            for g in range(SGU_GROUPS):
                db_ref[g:g + 1, :] = jnp.sum(dfsum_ref[g].T, axis=0, keepdims=True)

    full3 = pl.BlockSpec((SGU_GROUPS, SGU_CHUNK, SGU_CHUNK), lambda i: (0, 0, 0))
    return _call(body, name,
                 (_sds((s, w), BF), _sds((s, w), BF), _sds((SGU_GROUPS, SGU_CHUNK, SGU_CHUNK), F32),
                  _sds((SGU_GROUPS, SGU_CHUNK), F32), _sds((1, w), F32), _sds((1, w), F32)),
                 (steps,),
                 [_rows(tm, w), _rows(tm, w), _vec(w), _vec(w), full3, full3, _rows(tm, w)],
                 (_rows(tm, w), _rows(tm, w), full3, pl.BlockSpec((SGU_GROUPS, SGU_CHUNK), lambda i: (0, 0)), _vec(w), _vec(w)),
                 ("arbitrary",), scratch=[pltpu.VMEM((SGU_GROUPS, SGU_CHUNK, SGU_CHUNK), F32)], after=after)(
                     u_pre, v_pre, ln_w, ln_b, wm, bfull, dy)


def _mix_out(name, y_sgu, y_attn, ga_pre, gb_pre, x, g1, proj_a, proj_b, w_out, w2, sc2, sh2, tm=512):
    s, d = x.shape
    tm = min(tm, s)

    def body(ys_ref, ya_ref, ga_ref, gb_ref, x_ref, g1_ref, wa_ref, wb_ref, wo_ref, w2_ref, sc2_ref, sh2_ref,
             m_ref, pa_ref, pb_ref, o_ref, x1_ref, h2_ref):
        pa = _dot(ys_ref[...], wa_ref[...].reshape(d, d))
        pb = _dot(ya_ref[...], wb_ref[...].reshape(d, d))
        pa_ref[...] = pa.astype(BF)
        pb_ref[...] = pb.astype(BF)
        merged = (_sigmoid(ga_ref[...].astype(F32)) * pa + _sigmoid(gb_ref[...].astype(F32)) * pb).astype(BF)
        m_ref[...] = merged
        o = _dot(merged, wo_ref[...].reshape(d, d))
        o_ref[...] = o.astype(BF)
        x1 = x_ref[...] + g1_ref[...] * o
        x1_ref[...] = x1
        h2_ref[...] = ((x1 * _rms(x1)) * w2_ref[...] * (1.0 + sc2_ref[...]) + sh2_ref[...]).astype(BF)

    f, b = _sds((s, d), F32), _sds((s, d), BF)
    r = _rows(tm, d)
    wspec = _resident(proj_a.shape)
    return _call(body, name, (b, b, b, b, f, b), (s // tm,),
                 [r, r, r, r, r, _vec(d), wspec, wspec, wspec, _vec(d), _vec(d), _vec(d)], (r,) * 6, ("parallel",))(
                     y_sgu, y_attn, ga_pre, gb_pre, x, g1, proj_a, proj_b, w_out, w2, sc2, sh2)


def _mix_bwd(name, do, w_out, proj_a, proj_b, ga_pre, gb_pre, pa, pb, tm=512):
    s, d = do.shape
    tm = min(tm, s)

    def body(do_ref, wo_ref, wa_ref, wb_ref, ga_ref, gb_ref, pa_ref, pb_ref,
             dpa_ref, dpb_ref, dga_ref, dgb_ref, dys_ref, dya_ref):
        dm = _dot_nt(do_ref[...], wo_ref[...].reshape(d, d))
        ga = _sigmoid(ga_ref[...].astype(F32))
        gb = _sigmoid(gb_ref[...].astype(F32))
        dpa = (dm * ga).astype(BF)
        dpb = (dm * gb).astype(BF)
        dpa_ref[...] = dpa
        dpb_ref[...] = dpb
        dga_ref[...] = (dm * pa_ref[...].astype(F32) * ga * (1.0 - ga)).astype(BF)
        dgb_ref[...] = (dm * pb_ref[...].astype(F32) * gb * (1.0 - gb)).astype(BF)
        dys_ref[...] = _dot_nt(dpa, wa_ref[...].reshape(d, d)).astype(BF)
        dya_ref[...] = _dot_nt(dpb, wb_ref[...].reshape(d, d)).astype(BF)

    f, b = _sds((s, d), F32), _sds((s, d), BF)
    r = _rows(tm, d)
    wspec = _resident(w_out.shape)
    return _call(body, name, (b, b, b, b, b, b), (s // tm,), [r, wspec, wspec, wspec, r, r, r, r], (r,) * 6,
                 ("parallel",))(do, w_out, proj_a, proj_b, ga_pre, gb_pre, pa, pb)


def _ffn_up_act(name, h2, w_gate, w_up, cw, cb, tm=1024):
    s, d = h2.shape
    tm = min(tm, s)
    tc = FFN_CHUNK
    per = tm // HALO

    def body(h_ref, hprev_ref, wg_ref, wu_ref, cw_ref, cb_ref, a_ref, ac_ref, up_ref, hf_ref):
        hv = h_ref[...]
        wg = wg_ref[...].reshape(tc, d)
        a = _dot_nt(hv, wg).astype(BF)
        up = _dot_nt(hv, wu_ref[...].reshape(tc, d)).astype(BF)
        a_ref[...] = a
        up_ref[...] = up
        prev = jnp.where(pl.program_id(1) > 0, _dot_nt(hprev_ref[...], wg).astype(BF).astype(F32), 0.0)
        ext = jnp.concatenate([prev, a.astype(F32)], axis=0)
        ac = (cb_ref[...] + cw_ref[0:1, :] * pltpu.roll(ext, 2, axis=0) + cw_ref[1:2, :] * pltpu.roll(ext, 1, axis=0)
              + cw_ref[2:3, :] * ext)[HALO:]
        ac_ref[...] = ac.astype(BF)
        hf_ref[...] = (ac * _sigmoid(ac) * up.astype(F32)).astype(BF)

    wspec = pl.BlockSpec((CHIPS_PER_CHUNK, FFN_SHARD, d), lambda j, i: (j, 0, 0))
    ospec = pl.BlockSpec((tm, tc), lambda j, i: (i, j))
    o = _sds((s, FFN_DIM), BF)
    return _call(body, name, (o, o, o, o), (FFN_CHUNKS, s // tm),
                 [pl.BlockSpec((tm, d), lambda j, i: (i, 0)), pl.BlockSpec((HALO, d), lambda j, i: (jnp.maximum(i * per - 1, 0), 0)),
                  wspec, wspec, pl.BlockSpec((3, tc), lambda j, i: (0, j)), pl.BlockSpec((1, tc), lambda j, i: (0, j))],
                 (ospec, ospec, ospec, ospec), ("parallel", "parallel"))(h2, h2, w_gate, w_up, cw, cb)


def _ffn_down(name, hf, w_down, x1, g2, tm=512):
    s, d = x1.shape
    tm = min(tm, s)

    def body(hf_ref, wd_ref, x1_ref, g2_ref, dn_ref, x2_ref):
        dn = _dot(hf_ref[...], wd_ref[...].reshape(FFN_DIM, d))
        dn_ref[...] = dn.astype(BF)
        x2_ref[...] = x1_ref[...] + g2_ref[...] * dn

    return _call(body, name, (_sds((s, d), BF), _sds((s, d), F32)), (s // tm,),
                 [_rows(tm, FFN_DIM), _resident(w_down.shape), _rows(tm, d), _vec(d)],
                 (_rows(tm, d), _rows(tm, d)), ("parallel",))(hf, w_down, x1, g2)


def _ffn_down_bwd_act(name, dx2, dn, g2, w_down, a, ac, up, cw, tm=256, after=None):
    s, d = dx2.shape
    c = a.shape[1]
    tm = min(tm, s)
    tc = FFN_CHUNK
    per = tm // HALO
    steps = s // tm
    last = s // HALO - 1
    n = tm + HALO

    def body(dx_ref, dxnext_ref, dn_ref, g2_ref, wd_ref, a_ref, ac_ref, acnext_ref, up_ref, upnext_ref, cw_ref,
             ddn_ref, da_ref, dup_ref, dg_ref, dcw_ref, dcb_ref):
        i = pl.program_id(0)

        @pl.when(i == 0)
        def _():
            dg_ref[...] = jnp.zeros_like(dg_ref)
            dcw_ref[...] = jnp.zeros_like(dcw_ref)
            dcb_ref[...] = jnp.zeros_like(dcb_ref)

        dxv = dx_ref[...]
        ddn = (dxv * g2_ref[...]).astype(BF)
        ddn_ref[...] = ddn
        dg_ref[...] += jnp.sum(dxv * dn_ref[...].astype(F32), axis=0, keepdims=True)
        ddn_next = jnp.where(i < steps - 1, dxnext_ref[...] * g2_ref[...], 0.0).astype(BF)
        ddn_ext = jnp.concatenate([ddn, ddn_next], axis=0)
        for k in range(FFN_CHUNKS):
            cols = slice(k * tc, (k + 1) * tc)
            dh = _dot_nt(ddn_ext, wd_ref[k * CHIPS_PER_CHUNK:(k + 1) * CHIPS_PER_CHUNK].reshape(tc, d))
            ace = jnp.concatenate([ac_ref[:, cols].astype(F32), acnext_ref[:, cols].astype(F32)], axis=0)
            upe = jnp.concatenate([up_ref[:, cols].astype(F32), upnext_ref[:, cols].astype(F32)], axis=0)
            sig = _sigmoid(ace)
            silu = ace * sig
            dac = dh * upe * (sig + silu * (1.0 - sig))
            dup_ref[:, cols] = (dh[:tm] * silu[:tm]).astype(BF)
            d1 = pltpu.roll(dac, n - 1, axis=0)[:tm]
            d2 = pltpu.roll(dac, n - 2, axis=0)[:tm]
            d0 = dac[:tm]
            da_ref[:, cols] = (cw_ref[2:3, cols] * d0 + cw_ref[1:2, cols] * d1 + cw_ref[0:1, cols] * d2).astype(BF)
            a0 = a_ref[:, cols].astype(F32)
            dcb_ref[:, cols] += jnp.sum(d0, axis=0, keepdims=True)
            dcw_ref[0:1, cols] += jnp.sum(d2 * a0, axis=0, keepdims=True)
            dcw_ref[1:2, cols] += jnp.sum(d1 * a0, axis=0, keepdims=True)
            dcw_ref[2:3, cols] += jnp.sum(d0 * a0, axis=0, keepdims=True)

    nxt = lambda width: pl.BlockSpec((HALO, width), lambda i: (jnp.minimum((i + 1) * per, last), 0))
    wide = _sds((s, c), BF)
    return _call(body, name, (_sds((s, d), BF), wide, wide, _sds((1, d), F32), _sds((3, c), F32), _sds((1, c), F32)), (steps,),
                 [_rows(tm, d), nxt(d), _rows(tm, d), _vec(d), _resident(w_down.shape), _rows(tm, c), _rows(tm, c), nxt(c),
                  _rows(tm, c), nxt(c), pl.BlockSpec((3, c), lambda i: (0, 0))],
                 (_rows(tm, d), _rows(tm, c), _rows(tm, c), _vec(d), pl.BlockSpec((3, c), lambda i: (0, 0)), _vec(c)),
                 ("arbitrary",), after=after)(dx2, dx2, dn, g2, w_down, a, ac, ac, up, up, cw)


def _ffn_up_bwd(name, da, dup, w_gate, w_up, x1, dx2, w2, sc2, o, g1, tm=512, after=None):
    s, d = x1.shape
    tm = min(tm, s)

    def body(da_ref, dup_ref, wg_ref, wu_ref, x1_ref, dx2_ref, w2_ref, sc2_ref, o_ref, g1_ref,
             dx1_ref, do_ref, dnw_ref, dsh_ref, dg1_ref):
        @pl.when(pl.program_id(0) == 0)
        def _():
            dnw_ref[...] = jnp.zeros_like(dnw_ref)
            dsh_ref[...] = jnp.zeros_like(dsh_ref)
            dg1_ref[...] = jnp.zeros_like(dg1_ref)

        dh = _dot(da_ref[...], wg_ref[...].reshape(FFN_DIM, d)) + _dot(dup_ref[...], wu_ref[...].reshape(FFN_DIM, d))
        xv = x1_ref[...]
        r = _rms(xv)
        xn = xv * r
        dxn = dh * (w2_ref[...] * (1.0 + sc2_ref[...]))
        dx1 = dx2_ref[...] + r * (dxn - xn * jnp.mean(dxn * xn, axis=-1, keepdims=True))
        dx1_ref[...] = dx1
        dnw_ref[...] += jnp.sum(dh * xn, axis=0, keepdims=True)
        dsh_ref[...] += jnp.sum(dh, axis=0, keepdims=True)
        do_ref[...] = (dx1 * g1_ref[...]).astype(BF)
        dg1_ref[...] += jnp.sum(dx1 * o_ref[...].astype(F32), axis=0, keepdims=True)

    v = _sds((1, d), F32)
    r = _rows(tm, d)
    wspec = _resident(w_gate.shape)
    return _call(body, name, (_sds((s, d), F32), _sds((s, d), BF), v, v, v), (s // tm,),
                 [_rows(tm, FFN_DIM), _rows(tm, FFN_DIM), wspec, wspec, r, r, _vec(d), _vec(d), r, _vec(d)],
                 (r, r, _vec(d), _vec(d), _vec(d)), ("arbitrary",), after=after)(da, dup, w_gate, w_up, x1, dx2, w2, sc2, o, g1)


def _loss_head(name, x, w, target, tm=512):
    s, d = x.shape
    tm = min(tm, s)

    def body(x_ref, w_ref, t_ref, dx_ref, loss_ref, dw_ref):
        @pl.when(pl.program_id(0) == 0)
        def _():
            loss_ref[...] = jnp.zeros_like(loss_ref)
            dw_ref[...] = jnp.zeros_like(dw_ref)

        xv = x_ref[...]
        r = _rms(xv)
        xn = xv * r
        err = xn * w_ref[...] - t_ref[...]
        loss_ref[...] += 0.5 * jnp.sum(jnp.mean(err * err, axis=-1, keepdims=True))
        dy = err * (1.0 / d)
        dw_ref[...] += jnp.sum(dy * xn, axis=0, keepdims=True)
        dxn = dy * w_ref[...]
        dx_ref[...] = r * (dxn - xn * jnp.mean(dxn * xn, axis=-1, keepdims=True))

    return _call(body, name, (_sds((s, d), F32), _sds((1, LANES), F32), _sds((1, d), F32)), (s // tm,),
                 [_rows(tm, d), _vec(d), _rows(tm, d)], (_rows(tm, d), _vec(LANES), _vec(d)), ("arbitrary",))(x, w, target)


def _layer_fwd(l, x, mod, p, cosf, sinf, after=None, late=None, later=None):
    sh1, sc1, g1, sh2, sc2, g2 = mod
    tag = f"l{l}_"
    h, qr, kk0, kk1, vv0, vv1, u_pre, v_pre, ga_pre, gb_pre = _in_proj(
        tag + "in_proj", x, p["norm1_w"], sc1, sh1, p["w_in"], cosf, sinf, after=after)
    y_attn = _attention(tag + "attn", qr, kk0, kk1, vv0, vv1, p["sinks"])
    y_sgu = _sgu(tag + "sgu", u_pre, v_pre, p["sgu_ln_w"], p["sgu_ln_b"], p["sgu_w"], p["sgu_bfull"])
    if late is not None:
        p = dict(p, **late(y_sgu))
    merged, pa, pb, o, x1, h2 = _mix_out(tag + "mix_out", y_sgu, y_attn, ga_pre, gb_pre, x, g1, p["proj_a"], p["proj_b"],
                                         p["w_out"], p["norm2_w"], sc2, sh2)
    if later is not None:
        p = dict(p, **later(h2))
    a, ac, up, hf = _ffn_up_act(tag + "ffn_up", h2, p["w_gate"], p["w_up"], p["conv_w"], p["conv_b"])
    dn, x2 = _ffn_down(tag + "ffn_down", hf, p["w_down"], x1, g2)
    saved = dict(x=x, h=h, qr=qr, kk0=kk0, kk1=kk1, vv0=vv0, vv1=vv1, u_pre=u_pre, v_pre=v_pre, ga_pre=ga_pre,
                 gb_pre=gb_pre, y_attn=y_attn, y_sgu=y_sgu, merged=merged, pa=pa, pb=pb, o=o, x1=x1, h2=h2, a=a, ac=ac, up=up,
                 hf=hf, dn=dn)
    return x2, saved, p


def _layer_bwd(l, dx2, mod, p, sv, cosf, sinf, after=None, emit=None, tick=None):
    sh1, sc1, g1, sh2, sc2, g2 = mod
    tag = f"l{l}_b_"
    d = D_MODEL
    g = {}
    ready = (lambda names: emit({k: g.pop(k) for k in names})) if emit else (lambda names: None)
    tick = tick or (lambda y: None)
    ddn, da, dup, dg2, g["conv_w"], g["conv_b"] = _ffn_down_bwd_act(
        tag + "ffn_down", dx2, sv["dn"], g2, p["w_down"], sv["a"], sv["ac"], sv["up"], p["conv_w"], after=after)
    g["w_down"] = _matmul_tn(tag + "dw_down", sv["hf"], ddn, tk=FFN_CHUNK, after=tick(ddn)).reshape(N_CHIPS, FFN_SHARD, d)
    g["w_gate"] = _matmul_tn(tag + "dw_gate", da, sv["h2"], tk=FFN_CHUNK).reshape(N_CHIPS, FFN_SHARD, d)
    g["w_up"] = _matmul_tn(tag + "dw_up", dup, sv["h2"], tk=FFN_CHUNK).reshape(N_CHIPS, FFN_SHARD, d)
    dx1, do, da2, dsh2, dg1 = _ffn_up_bwd(tag + "ffn_up", da, dup, p["w_gate"], p["w_up"], sv["x1"], dx2, p["norm2_w"],
                                          sc2, sv["o"], g1, after=ready(("w_down", "w_gate", "w_up")))
    g["norm2_w"] = da2 * (1.0 + sc2)
    dsc2 = da2 * p["norm2_w"]
    g["w_out"] = _matmul_tn(tag + "dw_out", sv["merged"], do, after=tick(do)).reshape(N_CHIPS, d // N_CHIPS, d)
    dpa, dpb, dga, dgb, dy_sgu, dy_attn = _mix_bwd(tag + "mix", do, p["w_out"], p["proj_a"], p["proj_b"], sv["ga_pre"],
                                                  sv["gb_pre"], sv["pa"], sv["pb"])
    g["proj_a"] = _matmul_tn(tag + "dproj_a", sv["y_sgu"], dpa).reshape(N_CHIPS, d // N_CHIPS, d)
    g["proj_b"] = _matmul_tn(tag + "dproj_b", sv["y_attn"], dpb).reshape(N_CHIPS, d // N_CHIPS, d)
    du, dv, g["sgu_w"], g["sgu_b"], g["sgu_ln_w"], g["sgu_ln_b"] = _sgu_bwd(
        tag + "sgu", sv["u_pre"], sv["v_pre"], p["sgu_ln_w"], p["sgu_ln_b"], p["sgu_w"], p["sgu_bfull"], dy_sgu,
        after=ready(("w_out", "proj_a", "proj_b")))
    dqr, dkv_cur, dkv_prev, dsink = _attention_bwd(tag + "attn", sv["qr"], sv["kk0"], sv["kk1"], sv["vv0"], sv["vv1"],
                                                   p["sinks"], dy_attn, after=tick(du))
    g["sinks"] = dsink[0, :N_Q_HEADS]
    dq, dkv = _rope_bwd(tag + "rope", dqr, dkv_cur, dkv_prev, cosf, sinf)
    dw_in = _matmul_tn_segments(tag + "dw_in", [dq, dkv, du, dv, dga, dgb], sv["h"])
    g["w_in"] = dw_in.reshape(N_CHIPS, IN_COLS // N_CHIPS, d)
    dx, da1, dsh1 = _in_proj_bwd(tag + "in_proj", dq, dkv, du, dv, dga, dgb, p["w_in"], sv["x"], p["norm1_w"], sc1, dx1)
    g["norm1_w"] = da1 * (1.0 + sc1)
    dsc1 = da1 * p["norm1_w"]
    return dx, (dsh1, dsc1, dg1, dsh2, dsc2, dg2), g


def _pad_to(a, axis, size):
    pad = [(0, 0)] * a.ndim
    pad[axis] = (0, size - a.shape[axis])
    return jnp.pad(a, pad)


def _early_params(w_in, small):
    d = D_MODEL
    return dict(
        w_in=w_in.reshape(IN_COLS, d), norm1_w=small["norm1_w"].reshape(1, d), sinks=small["sinks"],
        sgu_ln_w=small["sgu_ln_w"].reshape(1, d), sgu_ln_b=small["sgu_ln_b"].reshape(1, d), sgu_w=small["sgu_w"],
        sgu_bfull=jnp.broadcast_to(small["sgu_b"][:, :, None], (SGU_GROUPS, SGU_CHUNK, SGU_CHUNK)))


def _mix_params(proj_a, proj_b, w_out, small):
    return dict(proj_a=proj_a, proj_b=proj_b, w_out=w_out, norm2_w=small["norm2_w"].reshape(1, D_MODEL))


def _ffn_params(w_gate, w_up, w_down, conv_w, small):
    return dict(
        w_gate=w_gate, w_up=w_up, w_down=w_down, conv_w=conv_w.transpose(1, 0, 2).reshape(3, FFN_DIM),
        conv_b=small["conv_b"].reshape(1, FFN_DIM))


def _layer_params(w_in, proj_a, proj_b, w_out, w_gate, w_up, w_down, conv_w, small):
    return dict(_early_params(w_in, small), **_mix_params(proj_a, proj_b, w_out, small),
                **_ffn_params(w_gate, w_up, w_down, conv_w, small))


def _conv_grads_natural(g):
    cw = g["conv_w"]
    cb = g["conv_b"].reshape(FFN_DIM)
    return cw, cb


def _rope_tables(positions):
    inv_freq = ROPE_THETA ** (-jnp.arange(0, ROT_DIM, 2, dtype=F32) / ROT_DIM)
    ang = positions.astype(F32)[:, None] * inv_freq
    cos, sin = jnp.cos(ang), jnp.sin(ang)
    s = positions.shape[0]
    rest = HEAD_DIM - ROT_DIM
    cos_head = jnp.concatenate([cos, cos, jnp.ones((s, rest), F32)], axis=1)
    sin_head = jnp.concatenate([-sin, sin, jnp.zeros((s, rest), F32)], axis=1)
    return jnp.tile(cos_head, (1, LANES // HEAD_DIM)), jnp.tile(sin_head, (1, LANES // HEAD_DIM))


ADA_ROWS = 16


def _ada_fwd(name, c_rows, ada_w, ada_b_cols, tn=512):
    depth, d, n = ada_w.shape

    def body(c_ref, w_ref, b_ref, o_ref):
        cv = c_ref[...]
        act = (cv * _sigmoid(cv)).astype(BF)
        o_ref[0] = _dot(act, w_ref[0].astype(BF)) + b_ref[0]

    return _call(body, name, _sds((depth, ADA_ROWS, n), F32), (depth, n // tn),
                 [pl.BlockSpec((ADA_ROWS, d), lambda l, j: (0, 0)), pl.BlockSpec((1, d, tn), lambda l, j: (l, 0, j)),
                  pl.BlockSpec((1, 1, tn), lambda l, j: (l, 0, j))],
                 pl.BlockSpec((1, ADA_ROWS, tn), lambda l, j: (l, 0, j)), ("parallel", "parallel"))(c_rows, ada_w, ada_b_cols)


def _ada_bwd(name, c_rows, dmod_cols, tn=512):
    depth, _, n = dmod_cols.shape
    d = c_rows.shape[1]

    def body(c_ref, dm_ref, o_ref):
        cv = c_ref[...]
        act = (cv * _sigmoid(cv)).astype(BF)
        o_ref[0] = _dot_tn(act, dm_ref[0].astype(BF))

    return _call(body, name, _sds((depth, d, n), F32), (depth, n // tn),
                 [pl.BlockSpec((ADA_ROWS, d), lambda l, j: (0, 0)), pl.BlockSpec((1, ADA_ROWS, tn), lambda l, j: (l, 0, j))],
                 pl.BlockSpec((1, d, tn), lambda l, j: (l, 0, j)), ("parallel", "parallel"))(c_rows, dmod_cols)


def _colsum(name, a):
    r, n = a.shape

    def body(a_ref, o_ref):
        o_ref[...] = jnp.sum(a_ref[...], axis=0, keepdims=True)

    return _call(body, name, _sds((1, n), F32), (1,), [pl.BlockSpec((r, n), lambda i: (0, 0))],
                 pl.BlockSpec((1, n), lambda i: (0, 0)), ("arbitrary",))(a)


REL_SIBLING = (0, 0, 1)
REL_CHIPS = ((1, 0, 0), (0, 1, 0), (1, 1, 0))
REL_ALL = tuple((fx, fy, fc) for fx in (0, 1) for fy in (0, 1) for fc in (0, 1) if fx or fy or fc)


def _chip_of(dev):
    return 2 * dev[0] + dev[1]


def _dev_of(dev):
    return 4 * dev[0] + 2 * dev[1] + dev[2]


def _flip(dev, rel):
    return tuple(1 - m if f else m for m, f in zip(dev, rel))


def _exchange(name, arrays, n_out, stages, aliases=None):
    out_shapes, stages = stages[0], stages[1:]
    n_in = len(arrays)
    aliases = aliases or {}
    n_remote = sum(len(plan) for plan, _ in stages)
    n_local = sum(len(local) for _, local in stages)

    def at(ref, idx):
        return ref.at[idx] if len(idx) else ref

    def body(*refs):
        bufs = list(refs[:n_in + n_out])
        for i_in, i_out in aliases.items():
            bufs[i_in] = bufs[n_in + i_out]
        send_sems, recv_sems, local_sems = refs[n_in + n_out:]
        me = (lax.axis_index("x"), lax.axis_index("y"), lax.axis_index("c"))
        base_r = base_l = 0
        pending = []
        for plan, local in stages:
            def remote(k, entry, sender, receiver):
                rel, si, ssel, di, dsel = entry
                return pltpu.make_async_remote_copy(
                    src_ref=at(bufs[si], ssel(sender, receiver)), dst_ref=at(bufs[di], dsel(sender, receiver)),
                    send_sem=send_sems.at[k], recv_sem=recv_sems.at[k], device_id=_flip(me, rel), device_id_type=MESH)

            sends = [remote(base_r + k, e, me, _flip(me, e[0])) for k, e in enumerate(plan)]
            for cp in sends:
                cp.start()
            for k, (si, ssel, di, dsel) in enumerate(local):
                cp = pltpu.make_async_copy(at(bufs[si], ssel(me)), at(bufs[di], dsel(me)), local_sems.at[base_l + k])
                cp.start()
                pending.append(cp.wait)
            for k, e in enumerate(plan):
                remote(base_r + k, e, _flip(me, e[0]), me).wait_recv()
            pending += [cp.wait_send for cp in sends]
            base_r += len(plan)
            base_l += len(local)
        for wait in pending:
            wait()

    any_spec = pl.BlockSpec(memory_space=pl.ANY)
    return pl.pallas_call(
        body, name=name, out_shape=tuple(out_shapes), in_specs=[any_spec] * n_in, out_specs=tuple([any_spec] * n_out),
        input_output_aliases=dict(aliases),
        scratch_shapes=[pltpu.SemaphoreType.DMA((max(n_remote, 1),)), pltpu.SemaphoreType.DMA((max(n_remote, 1),)),
                        pltpu.SemaphoreType.DMA((max(n_local, 1),))])(*arrays)


HBM_SPEC = pl.BlockSpec(memory_space=pltpu.HBM)
SEM_SPEC = pl.BlockSpec(memory_space=pltpu.SEMAPHORE)


def _split_copies(bufs, plan, local, send_sems, recv_sems, local_sems):
    me = (lax.axis_index("x"), lax.axis_index("y"), lax.axis_index("c"))

    def at(ref, idx):
        return ref.at[idx] if len(idx) else ref

    def remote(k, sender, receiver):
        rel, si, ssel, di, dsel = plan[k]
        return pltpu.make_async_remote_copy(
            src_ref=at(bufs[si], ssel(sender, receiver)), dst_ref=at(bufs[di], dsel(sender, receiver)),
            send_sem=send_sems.at[k], recv_sem=recv_sems.at[k], device_id=_flip(me, rel), device_id_type=MESH)

    sends = [remote(k, me, _flip(me, plan[k][0])) for k in range(len(plan))]
    arrivals = [remote(k, _flip(me, plan[k][0]), me) for k in range(len(plan))]
    locs = [pltpu.make_async_copy(at(bufs[si], ssel(me)), at(bufs[di], dsel(me)), local_sems.at[k])
            for k, (si, ssel, di, dsel) in enumerate(local)]
    return sends, arrivals, locs


def _exchange_start(name, arrays, out_shapes, plan, local, inplace=False):
    n_in, n_out = len(arrays), len(out_shapes)
    n_buf = n_in + n_out

    def body(*refs):
        sems = refs[n_buf:n_buf + 3]
        bufs = refs[n_buf + 3:2 * n_buf + 3]
        sends, _, locs = _split_copies(bufs * 2 if inplace else bufs, plan, local, *sems)
        for cp in sends + locs:
            cp.start()
        refs[-1][...] = jnp.zeros_like(refs[-1])

    zones = [lax.empty(o.shape, o.dtype) for o in out_shapes]
    operands = [pltpu.with_memory_space_constraint(a, pltpu.HBM) for a in list(arrays) + zones]
    sem = lambda n: pltpu.SemaphoreType.DMA((max(n, 1),))
    out = pl.pallas_call(
        body, name=name,
        out_shape=(sem(len(plan)), sem(len(plan)), sem(len(local)), *[pltpu.HBM(a.shape, a.dtype) for a in operands],
                   _sds((SUBLANES, LANES), F32)),
        in_specs=[HBM_SPEC] * n_buf,
        out_specs=(SEM_SPEC, SEM_SPEC, SEM_SPEC, *[HBM_SPEC] * n_buf, pl.BlockSpec(memory_space=pltpu.VMEM)),
        input_output_aliases={i: 3 + i for i in range(n_buf)},
        compiler_params=pltpu.CompilerParams(has_side_effects=pltpu.SideEffectType.DATAFLOW_SIDE_EFFECTING))(*operands)
    pending = dict(name=name, sems=out[:3], thru=out[3:3 + n_in], zones=out[3 + n_in:3 + n_buf], plan=plan, local=local,
                   inplace=inplace)
    return pending, out[-1]


def _exchange_wait(pending, after, both=False):
    thru, zones, plan, local, inplace = (pending[k] for k in ("thru", "zones", "plan", "local", "inplace"))
    n_in, n_buf = len(thru), len(thru) + len(zones)

    def body(*refs):
        bufs = refs[:n_buf]
        sends, arrivals, locs = _split_copies(bufs * 2 if inplace else bufs, plan, local, *refs[n_buf:n_buf + 3])
        for cp in arrivals:
            cp.wait_recv()
        for cp in sends:
            cp.wait_send()
        for cp in locs:
            cp.wait()

    out = pl.pallas_call(
        body, name=pending["name"] + "_wait", out_shape=tuple(pltpu.HBM(a.shape, a.dtype) for a in list(thru) + list(zones)),
        in_specs=[HBM_SPEC] * n_buf + [SEM_SPEC] * 3 + [pl.BlockSpec(memory_space=pl.ANY)],
        out_specs=tuple([HBM_SPEC] * n_buf), input_output_aliases={i: i for i in range(n_buf)},
        compiler_params=pltpu.CompilerParams(has_side_effects=pltpu.SideEffectType.DATAFLOW_SIDE_EFFECTING))(
            *thru, *zones, *pending["sems"], after)
    if both:
        return out[:n_in], out[n_in:]
    return out[:n_in] if inplace else out[n_in:]


def _whole(*_):
    return ()


def _half_rows(rows, core):
    return pl.ds(core * (rows // 2), rows // 2)


def _gather_weights_plan(shards):
    n = len(shards)
    dsts = [_sds((N_CHIPS,) + a.shape, a.dtype) for a in shards]
    fetch, forward = [], []
    for t, a in enumerate(shards):
        rows = a.shape[0]
        if rows % (2 * 16) == 0:
            fetch += [(rel, t, (lambda s_, r_, rows=rows: (_half_rows(rows, s_[2]),)), n + t,
                       (lambda s_, r_, rows=rows: (_chip_of(s_), _half_rows(rows, s_[2])))) for rel in REL_CHIPS]
            forward += [(REL_SIBLING, n + t, (lambda s_, r_, rows=rows, rel=rel: (_chip_of(_flip(s_, rel)), _half_rows(rows, s_[2]))),
                         n + t, (lambda s_, r_, rows=rows, rel=rel: (_chip_of(_flip(s_, rel)), _half_rows(rows, s_[2]))))
                        for rel in REL_CHIPS]
        else:
            fetch += [(rel, t, _whole, n + t, lambda s_, r_: (_chip_of(s_),)) for rel in REL_CHIPS]
    local = [(t, _whole, n + t, lambda me: (_chip_of(me),)) for t in range(n)]
    return dsts, fetch, local, forward


def _gather_weights_start(name, shards):
    dsts, fetch, local, forward = _gather_weights_plan(shards)
    pending, token = _exchange_start(name, shards, dsts, fetch, local)
    return dict(pending, forward=forward), token


def _gather_weights_finish(pending, after):
    landed = _exchange_wait(pending, after)
    n = len(landed)
    return _exchange(pending["name"] + "_forward", landed, n, [[_sds(a.shape, a.dtype) for a in landed], (pending["forward"], [])],
                     aliases={t: t for t in range(n)})


def _gather_chips_plan(arrays):
    n = len(arrays)
    dsts = [_sds((N_CHIPS,) + a.shape, a.dtype) for a in arrays]
    plan = [(rel, t, _whole, n + t, lambda s_, r_: (_chip_of(s_),)) for t in range(n) for rel in REL_CHIPS]
    local = [(t, _whole, n + t, lambda me: (_chip_of(me),)) for t in range(n)]
    return dsts, plan, local


def _gather_chips(name, arrays):
    dsts, plan, local = _gather_chips_plan(arrays)
    return _exchange(name, arrays, len(arrays), [dsts, (plan, local)])


def _gather_chips_start(name, arrays):
    dsts, plan, local = _gather_chips_plan(arrays)
    return _exchange_start(name, arrays, dsts, plan, local)


def _gather_all(name, a):
    plan = [(rel, 0, _whole, 1, lambda s_, r_: (_dev_of(s_),)) for rel in REL_ALL]
    local = [(0, _whole, 1, lambda me: (_dev_of(me),))]
    return _exchange(name, [a], 1, [[_sds((2 * N_CHIPS,) + a.shape, a.dtype)], (plan, local)])[0]


def _swap_halves_start(name, grads):
    n = len(grads)
    dsts = [_sds((g.shape[0], g.shape[1] // 2, g.shape[2]), g.dtype) for g in grads]
    plan = [(REL_SIBLING, t, (lambda s_, r_, rows=g.shape[1]: (pl.ds(0, N_CHIPS), _half_rows(rows, r_[2]))), n + t, _whole)
            for t, g in enumerate(grads)]
    return _exchange_start(name, grads, dsts, plan, [])


def _scatter_chips_plan(sums):
    n = len(sums)
    dsts = [_sds(a.shape, a.dtype) for a in sums]
    plan = [(rel, t, lambda s_, r_: (_chip_of(r_),), n + t, lambda s_, r_: (_chip_of(s_),))
            for t in range(n) for rel in REL_CHIPS]
    local = [(t, lambda me: (_chip_of(me),), n + t, lambda me: (_chip_of(me),)) for t in range(n)]
    return dsts, plan, local


def _scatter_chips_start(name, sums):
    dsts, plan, local = _scatter_chips_plan(sums)
    return _exchange_start(name, sums, dsts, plan, local)


def _swap_back_start(name, totals, layer):
    n = len(totals)
    plan = [(REL_SIBLING, n + t, (lambda s_, r_, rows=a.shape[1]: (layer, _half_rows(rows, s_[2]))),
             n + t, (lambda s_, r_, rows=a.shape[1]: (layer, _half_rows(rows, s_[2])))) for t, a in enumerate(totals)]
    return _exchange_start(name, totals, [], plan, [], inplace=True)


def _add_halves(name, g, recv, core):
    nch, half, c = recv.shape

    def body(core_ref, g_ref, r_ref, o_ref):
        o_ref[0] = (g_ref[0, 0].astype(F32) + r_ref[0].astype(F32)).astype(o_ref.dtype)

    spec = pltpu.PrefetchScalarGridSpec(
        num_scalar_prefetch=1, grid=(nch,),
        in_specs=[pl.BlockSpec((1, 1, half, c), lambda k, core_ref: (k, core_ref[0], 0, 0)),
                  pl.BlockSpec((1, half, c), lambda k, core_ref: (k, 0, 0))],
        out_specs=pl.BlockSpec((1, half, c), lambda k, core_ref: (k, 0, 0)))
    return pl.pallas_call(body, name=name, out_shape=_sds(recv.shape, recv.dtype), grid_spec=spec,
                          compiler_params=pltpu.CompilerParams(dimension_semantics=("parallel",),
                                                               vmem_limit_bytes=VMEM_LIMIT))(
                                                                   core, g.reshape(nch, 2, half, c), recv)


def _sum_chips(name, a, core, layer, total):
    nch, half, c = a.shape

    def body(core_ref, a_ref, *rest):
        o_ref = rest[-1]
        acc = a_ref[0].astype(F32)
        for k in range(1, nch):
            acc = acc + a_ref[k].astype(F32)
        o_ref[0, 0] = acc

    in_specs = [pl.BlockSpec((nch, half, c), lambda i, core_ref: (0, 0, 0))]
    args = [core, a]
    if total is not None:
        in_specs.append(pl.BlockSpec(memory_space=pl.ANY))
        args.append(total.reshape(DEPTH, 2, half, c))
    spec = pltpu.PrefetchScalarGridSpec(
        num_scalar_prefetch=1, grid=(1,), in_specs=in_specs,
        out_specs=pl.BlockSpec((1, 1, half, c), lambda i, core_ref: (layer, core_ref[0], 0, 0)))
    out = pl.pallas_call(body, name=name, out_shape=_sds((DEPTH, 2, half, c), F32), grid_spec=spec,
                         input_output_aliases={2: 0} if total is not None else {},
                         compiler_params=pltpu.CompilerParams(dimension_semantics=("arbitrary",),
                                                              vmem_limit_bytes=VMEM_LIMIT))(*args)
    return out.reshape(DEPTH, 2 * half, c)


def _adamw_update(w, g, m, v):
    mn = ADAM_B1 * m + (1.0 - ADAM_B1) * g
    vn = ADAM_B2 * v + (1.0 - ADAM_B2) * (g * g)
    m_hat = mn / (1.0 - ADAM_B1 ** ADAM_STEP)
    v_hat = vn / (1.0 - ADAM_B2 ** ADAM_STEP)
    return -ADAM_LR * (m_hat / (jnp.sqrt(v_hat) + ADAM_EPS) + ADAM_WD * w), mn, vn


def _adamw(name, w, g, m, v, after=None):
    depth, r, c = w.shape
    tr = next(t for t in (512, 448, 384, 352, 336, 256, 192, 128, 64, 32, 16, 8) if r % t == 0 and t * c <= ADAM_TILE_ELEMS)

    def body(w_ref, g_ref, m_ref, v_ref, go_ref, d_ref, mo_ref, vo_ref):
        gv = g_ref[...]
        go_ref[...] = gv
        d_ref[...], mo_ref[...], vo_ref[...] = _adamw_update(w_ref[...], gv, m_ref[...], v_ref[...])

    spec = pl.BlockSpec((1, tr, c), lambda l, i: (l, i, 0))
    o = _sds(w.shape, F32)
    return _call(body, name, (o, o, o, o), (depth, r // tr), [spec] * 4, (spec,) * 4, ("parallel", "parallel"),
                 after=after)(w, g, m, v)


def _adamw_small(name, ws, gs, ms, vs):
    n = len(ws)

    def body(*refs):
        for t in range(n):
            w_ref, g_ref, m_ref, v_ref = (refs[k * n + t] for k in range(4))
            d_ref, mo_ref, vo_ref = (refs[(4 + k) * n + t] for k in range(3))
            d_ref[...], mo_ref[...], vo_ref[...] = _adamw_update(w_ref[...], g_ref[...], m_ref[...], v_ref[...])

    outs = [_sds(w.shape, F32) for w in ws]
    res = pl.pallas_call(body, name=name, out_shape=tuple(outs * 3))(*ws, *gs, *ms, *vs)
    return res[:n], res[n:2 * n], res[2 * n:]


def _pack(arrays, rows):
    flat = jnp.concatenate([a.reshape(-1).astype(F32) for a in arrays])
    return _pad_to(flat, 0, rows * LANES).reshape(rows, LANES)


def _unpack(packed, shapes):
    flat = packed.reshape(-1)
    out, off = [], 0
    for shp in shapes:
        n = 1
        for s_ in shp:
            n *= s_
        out.append(flat[off:off + n].reshape(shp))
        off += n
    return out


_MATRICES = ("w_in", "proj_a", "proj_b", "w_out", "w_gate", "w_up", "w_down")
_SMALL = (("norm1_w", (D_MODEL,)), ("sinks", (N_Q_HEADS,)), ("sgu_ln_w", (SGU_WIDTH,)), ("sgu_ln_b", (SGU_WIDTH,)),
          ("sgu_w", (SGU_GROUPS, SGU_CHUNK, SGU_CHUNK)), ("sgu_b", (SGU_GROUPS, SGU_CHUNK)), ("norm2_w", (D_MODEL,)),
          ("conv_w", (3, FFN_DIM)), ("conv_b", (FFN_DIM,)), ("final_norm_w", (D_MODEL,)))
SMALL_ROWS = 320
ADAM_TILE_ELEMS = 384 * 1024


def _reduce_cores_start(tag, partial):
    names = list(partial)
    pending, token = _swap_halves_start(tag + "_cores", [partial[k] for k in names])
    return dict(pending, tag=tag, names=names), token


def _reduce_chips_start(pending, core, after):
    tag, names = pending["tag"], pending["names"]
    mine, theirs = _exchange_wait(pending, after, both=True)
    sums = [_add_halves(f"{tag}_cores_add_{k}", g, r, core) for k, g, r in zip(names, mine, theirs)]
    scatter, token = _scatter_chips_start(tag + "_chips", sums)
    return dict(scatter, tag=tag, names=names), token


def _reduce_back_start(pending, l, core, totals, after):
    tag, names = pending["tag"], pending["names"]
    sums = [_sum_chips(f"{tag}_chips_add_{k}", a, core, l, totals.get(k)) for k, a in zip(names, _exchange_wait(pending, after))]
    back, _ = _swap_back_start(tag + "_back", sums, l)
    return dict(back, names=names)


def kernel(x, c, positions, ada_w, ada_b, norm1_w, w_in, attn_sinks, sgu_ln_w, sgu_ln_b, sgu_w, sgu_b, proj_a, proj_b, w_out, norm2_w, ffn_w_gate, ffn_w_up, ffn_conv_w, ffn_conv_b, ffn_w_down, final_norm_w, loss_target, m_ada_w, m_ada_b, m_norm1_w, m_w_in, m_attn_sinks, m_sgu_ln_w, m_sgu_ln_b, m_sgu_w, m_sgu_b, m_proj_a, m_proj_b, m_w_out, m_norm2_w, m_ffn_w_gate, m_ffn_w_up, m_ffn_conv_w, m_ffn_conv_b, m_ffn_w_down, m_final_norm_w, v_ada_w, v_ada_b, v_norm1_w, v_w_in, v_attn_sinks, v_sgu_ln_w, v_sgu_ln_b, v_sgu_w, v_sgu_b, v_proj_a, v_proj_b, v_w_out, v_norm2_w, v_ffn_w_gate, v_ffn_w_up, v_ffn_conv_w, v_ffn_conv_b, v_ffn_w_down, v_final_norm_w):
    d = D_MODEL
    ax, ay, ac = lax.axis_index("x"), lax.axis_index("y"), lax.axis_index("c")
    chip = 2 * ax + ay
    dev = 4 * ax + 2 * ay + ac
    core = ac.astype(jnp.int32).reshape(1)

    c_all = _gather_all("gather_cond", c.reshape(SUBLANES, d // SUBLANES)).reshape(2 * N_CHIPS, d)
    c_rows = _pad_to(c_all, 0, ADA_ROWS)
    ada_cols = ada_w.shape[2]
    ada_b_cols = lax.dynamic_slice_in_dim(ada_b, chip * ada_cols, ada_cols, axis=1).reshape(DEPTH, 1, ada_cols)
    mod_cols = _ada_fwd("ada_fwd", c_rows, ada_w, ada_b_cols)
    mod_all = _gather_chips("gather_mod", [mod_cols])[0]
    mod_mine = lax.dynamic_index_in_dim(mod_all, dev, axis=2, keepdims=False)
    mod_mine = mod_mine.transpose(1, 0, 2).reshape(DEPTH, 1, 6 * d)
    mods = [tuple(jnp.split(mod_mine[l], 6, axis=-1)) for l in range(DEPTH)]

    tr = lambda a: jnp.swapaxes(a, 1, 2)
    shards = [tr(w_in).astype(BF), proj_a.astype(BF), proj_b.astype(BF), w_out.astype(BF),
              tr(ffn_w_gate).astype(BF), tr(ffn_w_up).astype(BF), ffn_w_down.astype(BF), ffn_conv_w]
    token = mod_all[0, 0, :SUBLANES, :LANES]
    fetches = []
    for l in range(DEPTH):
        groups = []
        for tag, members in (("in", shards[:1]), ("mix", shards[1:4]), ("ffn", shards[4:])):
            behind = (token[0, 0] * 0.0).astype(members[0].dtype)
            pending, token = _gather_weights_start(f"l{l}_gather_{tag}", [members[0][l] + behind] + [a[l] for a in members[1:]])
            groups.append(pending)
        fetches.append(groups)

    small_in = dict(norm1_w=norm1_w, sinks=attn_sinks, sgu_ln_w=sgu_ln_w, sgu_ln_b=sgu_ln_b, sgu_w=sgu_w, sgu_b=sgu_b,
                    norm2_w=norm2_w, conv_b=ffn_conv_b)
    cosf, sinf = _rope_tables(positions[0])
    small_of = lambda l: {k: v[l] for k, v in small_in.items()}

    h = x[0]
    saved, params = [], []
    for l in range(DEPTH):
        first, mix, ffn = fetches[l]
        w_in_l = _gather_weights_finish(first, token if l == 0 else h)
        late = lambda y, l=l, mix=mix: _mix_params(*_gather_weights_finish(mix, y), small_of(l))
        later = lambda y, l=l, ffn=ffn: _ffn_params(*_gather_weights_finish(ffn, y), small_of(l))
        h, sv, p = _layer_fwd(l, h, mods[l], _early_params(w_in_l[0], small_of(l)), cosf, sinf, late=late, later=later)
        saved.append(sv)
        params.append(p)
    dx, loss_part, d_final = _loss_head("loss_head", h, final_norm_w.reshape(1, d), loss_target[0])
    loss = lax.psum(loss_part[0, 0], ("x", "y", "c"))

    def small_pack(l, grads):
        cw, cb = _conv_grads_natural(grads)
        nat = dict(grads, conv_w=cw, conv_b=cb, final_norm_w=d_final if l == DEPTH - 1 else jnp.zeros((d,), F32))
        return _pack([nat[k] for k, _ in _SMALL], N_CHIPS * SMALL_ROWS).reshape(N_CHIPS, SMALL_ROWS, LANES)

    waiting, inflight = [], []

    def send(tag, partial):
        pending, token = _reduce_cores_start(tag, partial)
        waiting.append(pending)
        return token

    def tick(y):
        token = None
        while waiting:
            pending, token = _reduce_chips_start(waiting.pop(0), core, y)
            inflight.append(pending)
        return token

    dmods = [None] * DEPTH
    dx, dmods[1], grads = _layer_bwd(1, dx, mods[1], params[1], saved[1], cosf, sinf)
    token = send("l1_reduce", dict({k: grads[k] for k in _MATRICES}, small=small_pack(1, grads)))
    dx, dmods[0], grads = _layer_bwd(0, dx, mods[0], params[0], saved[0], cosf, sinf, after=token,
                                     emit=lambda part: send("l0_reduce_" + "_".join(part), part), tick=tick)
    dmod_mine = jnp.concatenate([jnp.concatenate(dmods[l], axis=1) for l in range(DEPTH)], axis=1)
    dmod_all = _gather_all("gather_dmod", dmod_mine.reshape(SUBLANES, -1)).reshape(2 * N_CHIPS, DEPTH * 6 * d)
    send("l0_reduce_in", dict(w_in=grads["w_in"], small=small_pack(0, grads) + dmod_all[0, 0] * 0.0))
    tick(dmod_all)

    totals, flying = {}, None

    def land(after):
        if flying is not None:
            totals.update(zip(flying["names"], _exchange_wait(flying, after)))

    for pending in inflight[:-1]:
        land(dx)
        flying = _reduce_back_start(pending, 1 if pending["tag"].startswith("l1") else 0, core, totals, dx)
    land(dx)
    flying = None

    g_ada_b = _colsum("ada_b_grad", dmod_all).reshape(DEPTH, 6 * d)
    dmod_cols = jnp.stack([lax.dynamic_slice_in_dim(dmod_all, l * 6 * d + chip * ada_cols, ada_cols, axis=1)
                           for l in range(DEPTH)])
    g_ada_w = _ada_bwd("ada_w_grad", c_rows, _pad_to(dmod_cols, 1, ADA_ROWS))
    big = dict(w_in=(tr(w_in), tr(m_w_in), tr(v_w_in)), proj_a=(proj_a, m_proj_a, v_proj_a), proj_b=(proj_b, m_proj_b, v_proj_b),
               w_out=(w_out, m_w_out, v_w_out), w_gate=(tr(ffn_w_gate), tr(m_ffn_w_gate), tr(v_ffn_w_gate)),
               w_up=(tr(ffn_w_up), tr(m_ffn_w_up), tr(v_ffn_w_up)), w_down=(ffn_w_down, m_ffn_w_down, v_ffn_w_down))
    upd, g_big = {}, {}

    def update(k, after=None):
        res = _adamw("adamw_" + k, big[k][0], totals[k], *big[k][1:], after=after)
        res = [tr(a) for a in res] if k in ("w_in", "w_gate", "w_up") else res
        g_big[k], upd[k] = res[0], res[1:]

    for k in ("w_down", "w_gate", "w_up", "w_out", "proj_a", "proj_b"):
        update(k)
    g_big["ada_w"], *upd["ada_w"] = _adamw("adamw_ada_w", ada_w, g_ada_w, m_ada_w, v_ada_w)
    flying = _reduce_back_start(inflight[-1], 0, core, totals, upd["ada_w"][0])
    land(upd["ada_w"][1])
    gathering, token = _gather_chips_start("gather_small", [totals["small"]])
    update("w_in", after=token)
    small_all = _exchange_wait(gathering, upd["w_in"][0])[0]
    small_g = small_all.transpose(1, 0, 2, 3).reshape(DEPTH, -1)
    per_layer = [_unpack(small_g[l], [shp for _, shp in _SMALL]) for l in range(DEPTH)]
    sg = {k: jnp.stack([per_layer[l][i] for l in range(DEPTH)]) for i, (k, _) in enumerate(_SMALL)}
    g_final = sg["final_norm_w"][DEPTH - 1]
    g_conv_w = lax.dynamic_slice_in_dim(sg["conv_w"], chip * FFN_SHARD, FFN_SHARD, axis=2)

    rest = [("ada_b", ada_b, g_ada_b, m_ada_b, v_ada_b), ("norm1_w", norm1_w, sg["norm1_w"], m_norm1_w, v_norm1_w),
            ("attn_sinks", attn_sinks, sg["sinks"], m_attn_sinks, v_attn_sinks),
            ("sgu_ln_w", sgu_ln_w, sg["sgu_ln_w"], m_sgu_ln_w, v_sgu_ln_w),
            ("sgu_ln_b", sgu_ln_b, sg["sgu_ln_b"], m_sgu_ln_b, v_sgu_ln_b), ("sgu_w", sgu_w, sg["sgu_w"], m_sgu_w, v_sgu_w),
            ("sgu_b", sgu_b, sg["sgu_b"], m_sgu_b, v_sgu_b), ("norm2_w", norm2_w, sg["norm2_w"], m_norm2_w, v_norm2_w),
            ("ffn_conv_w", ffn_conv_w, g_conv_w, m_ffn_conv_w, v_ffn_conv_w),
            ("ffn_conv_b", ffn_conv_b, sg["conv_b"], m_ffn_conv_b, v_ffn_conv_b),
            ("final_norm_w", final_norm_w.reshape(1, d), g_final.reshape(1, d), m_final_norm_w.reshape(1, d),
             v_final_norm_w.reshape(1, d))]
    rest_out = _adamw_small("adamw_rest", *[[r[i] for r in rest] for i in (1, 2, 3, 4)])
    g_rest = {r[0]: r[2] for r in rest}
    u_rest = {r[0]: tuple(o[i] for o in rest_out) for i, r in enumerate(rest)}
    g_rest["final_norm_w"] = g_final
    u_rest["final_norm_w"] = tuple(a.reshape(d) for a in u_rest["final_norm_w"])

    names = ("ada_w", "ada_b", "norm1_w", "w_in", "attn_sinks", "sgu_ln_w", "sgu_ln_b", "sgu_w", "sgu_b", "proj_a", "proj_b",
             "w_out", "norm2_w", "ffn_w_gate", "ffn_w_up", "ffn_conv_w", "ffn_conv_b", "ffn_w_down", "final_norm_w")
    alias = {"ffn_w_gate": "w_gate", "ffn_w_up": "w_up", "ffn_w_down": "w_down"}
    grad_of = lambda n: g_rest[n] if n in g_rest else g_big[alias.get(n, n)]
    upd_of = lambda n: u_rest[n] if n in u_rest else upd[alias.get(n, n)]
    return (loss, dx[None], *[grad_of(n) for n in names], *[upd_of(n)[0] for n in names],
            *[upd_of(n)[1] for n in names], *[upd_of(n)[2] for n in names])
```

```python
import jax
import jax.numpy as jnp
from jax import lax
from jax.experimental import pallas as pl
from jax.experimental.pallas import tpu as pltpu

F32 = jnp.float32
BF = jnp.bfloat16

D_MODEL = 1024
N_Q_HEADS = 16
N_KV_HEADS = 2
HEAD_DIM = 64
ATTN_BLOCK = 128
ROPE_THETA = 500000.0
ROT_DIM = HEAD_DIM // 4
SGU_WIDTH = 1024
SGU_GROUPS = 8
SGU_CHUNK = 128
FFN_DIM = 2816
NORM_EPS = 1e-6
DEPTH = 2
IN_COLS = 5376
N_CHIPS = 4
FFN_SHARD = FFN_DIM // N_CHIPS
CHIPS_PER_CHUNK = 2
FFN_CHUNK = CHIPS_PER_CHUNK * FFN_SHARD
FFN_CHUNKS = FFN_DIM // FFN_CHUNK
LANES = 128
SUBLANES = 8
HALO = 16
VMEM_LIMIT = 56 * 1024 * 1024
NEG_BIG = -1e30

ADAM_LR = 0.001
ADAM_B1 = 0.9
ADAM_B2 = 0.999
ADAM_EPS = 1e-08
ADAM_WD = 0.01
ADAM_STEP = 10

MESH = pl.DeviceIdType.MESH

Q_END = 1024
KV_END = 1280
U_END = 2304
Z_END = 3328
GA_END = 4352


def _sds(shape, dtype):
    return jax.ShapeDtypeStruct(tuple(shape), dtype)


def _call(body, name, out_shape, grid, in_specs, out_specs, semantics, scratch=(), after=None):
    n_in = len(in_specs)
    fn = body
    if after is not None:
        def fn(*refs):
            return body(*refs[:n_in], *refs[n_in + 1:])

        in_specs = list(in_specs) + [pl.BlockSpec(memory_space=pl.ANY)]
    call = pl.pallas_call(
        fn, name=name, out_shape=out_shape, grid=grid, in_specs=in_specs, out_specs=out_specs,
        scratch_shapes=scratch,
        compiler_params=pltpu.CompilerParams(dimension_semantics=semantics, vmem_limit_bytes=VMEM_LIMIT))
    if after is None:
        return call
    return lambda *args: call(*args, after)


def _rows(tm, width, col=0):
    return pl.BlockSpec((tm, width), lambda i: (i, col))


def _vec(width):
    return pl.BlockSpec((1, width), lambda i: (0, 0))


def _resident(shape):
    zeros = (0,) * len(shape)
    return pl.BlockSpec(tuple(shape), lambda *_: zeros, pipeline_mode=pl.Buffered(1))


def _sigmoid(x):
    return 0.5 + 0.5 * jnp.tanh(0.5 * x)


def _gelu(x):
    return 0.5 * x * (1.0 + lax.erf(x * 0.7071067811865476))


def _gelu_and_grad(x):
    cdf = 0.5 * (1.0 + lax.erf(x * 0.7071067811865476))
    return x * cdf, cdf + x * jnp.exp(-0.5 * x * x) * 0.3989422804014327


def _dot(a, b):
    return jnp.dot(a, b, preferred_element_type=F32)


def _dot_nt(a, b):
    return lax.dot_general(a, b, (((1,), (1,)), ((), ())), preferred_element_type=F32)


def _dot_tn(a, b):
    return lax.dot_general(a, b, (((0,), (0,)), ((), ())), preferred_element_type=F32)


def _rms(xv):
    return lax.rsqrt(jnp.mean(xv * xv, axis=-1, keepdims=True) + NORM_EPS)


def _matmul_tn(name, a, b, tk=512, tn=1024, after=None):
    s, k = a.shape
    n = b.shape[1]
    tk, tn = min(tk, k), min(tn, n)

    def body(a_ref, b_ref, o_ref):
        o_ref[...] = _dot_tn(a_ref[...], b_ref[...]).astype(o_ref.dtype)

    return _call(body, name, _sds((k, n), BF), (k // tk, n // tn),
                 [pl.BlockSpec((s, tk), lambda i, j: (0, i)), pl.BlockSpec((s, tn), lambda i, j: (0, j))],
                 pl.BlockSpec((tk, tn), lambda i, j: (i, j)), ("parallel", "parallel"), after=after)(a, b)


def _matmul_tn_segments(name, segments, b, tk=256):
    s, n = b.shape
    counts = [a.shape[1] // tk for a in segments]
    firsts = [sum(counts[:t]) for t in range(len(segments))]

    def body(*refs):
        a_refs, b_ref, o_ref = refs[:len(segments)], refs[len(segments)], refs[-1]
        i = pl.program_id(0)
        for a_ref, first, count in zip(a_refs, firsts, counts):
            @pl.when((i >= first) & (i < first + count))
            def _(a_ref=a_ref):
                o_ref[...] = _dot_tn(a_ref[...], b_ref[...]).astype(BF)

    specs = [pl.BlockSpec((s, tk), lambda i, first=first, count=count: (0, jnp.clip(i - first, 0, count - 1)))
             for first, count in zip(firsts, counts)]
    return _call(body, name, _sds((sum(counts) * tk, n), BF), (sum(counts),), specs + [_resident(b.shape)],
                 pl.BlockSpec((tk, n), lambda i: (i, 0)), ("arbitrary",))(*segments, b)


def _rope_partner(v):
    lane = lax.broadcasted_iota(jnp.int32, (1, LANES), 1) % HEAD_DIM
    return jnp.where(lane < ROT_DIM // 2, pltpu.roll(v, LANES - ROT_DIM // 2, axis=1), pltpu.roll(v, ROT_DIM // 2, axis=1))


def _dup_half(v, half):
    lane = lax.broadcasted_iota(jnp.int32, (1, LANES), 1)
    keep = jnp.where((lane >= HEAD_DIM) == (half == 1), v, 0.0)
    return keep + pltpu.roll(keep, HEAD_DIM, axis=1)


def _in_proj(name, x, w, sc, sh, w_in, cosf, sinf, tm=512, after=None):
    s, d = x.shape
    tm = min(tm, s)

    def body(x_ref, w_ref, sc_ref, sh_ref, win_ref, cos_ref, sin_ref,
             h_ref, qr_ref, kk0_ref, kk1_ref, vv0_ref, vv1_ref, u_ref, v_ref, ga_ref, gb_ref):
        xv = x_ref[...]
        h = ((xv * _rms(xv)) * w_ref[...] * (1.0 + sc_ref[...]) + sh_ref[...]).astype(BF)
        h_ref[...] = h
        cosv, sinv = cos_ref[...], sin_ref[...]
        q = _dot_nt(h, win_ref[:Q_END, :])
        for j in range(D_MODEL // LANES):
            qv = q[:, j * LANES:(j + 1) * LANES]
            qr_ref[:, j * LANES:(j + 1) * LANES] = ((qv * cosv + _rope_partner(qv) * sinv) * ATTN_SCALE).astype(BF)
        kv = _dot_nt(h, win_ref[Q_END:KV_END, :])
        kr = kv[:, :LANES] * cosv + _rope_partner(kv[:, :LANES]) * sinv
        vv = kv[:, LANES:]
        kk0_ref[...] = _dup_half(kr, 0).astype(BF)
        kk1_ref[...] = _dup_half(kr, 1).astype(BF)
        vv0_ref[...] = _dup_half(vv, 0).astype(BF)
        vv1_ref[...] = _dup_half(vv, 1).astype(BF)
        u_ref[...] = _dot_nt(h, win_ref[KV_END:U_END, :])
        v_ref[...] = _dot_nt(h, win_ref[U_END:Z_END, :])
        ga_ref[...] = _dot_nt(h, win_ref[Z_END:GA_END, :]).astype(BF)
        gb_ref[...] = _dot_nt(h, win_ref[GA_END:, :]).astype(BF)

    wide, kvs, pre = _sds((s, d), BF), _sds((s, LANES), BF), _sds((s, d), F32)
    return _call(body, name, (wide, wide, kvs, kvs, kvs, kvs, pre, pre, wide, wide), (s // tm,),
                 [_rows(tm, d), _vec(d), _vec(d), _vec(d), _resident(w_in.shape), _rows(tm, LANES), _rows(tm, LANES)],
                 (_rows(tm, d), _rows(tm, d)) + (_rows(tm, LANES),) * 4 + (_rows(tm, d),) * 4, ("parallel",), after=after)(
                     x, w, sc, sh, w_in, cosf, sinf)


def _in_proj_bwd(name, dq, dkv, du, dv, dga, dgb, w_in, x, w, sc, dx_in, tm=512):
    s, d = x.shape
    tm = min(tm, s)

    def body(dq_ref, dkv_ref, du_ref, dv_ref, dga_ref, dgb_ref, win_ref, x_ref, w_ref, sc_ref, dxin_ref,
             dx_ref, da_ref, dsh_ref):
        @pl.when(pl.program_id(0) == 0)
        def _():
            da_ref[...] = jnp.zeros_like(da_ref)
            dsh_ref[...] = jnp.zeros_like(dsh_ref)

        dh = (_dot(dq_ref[...], win_ref[:Q_END, :]) + _dot(dkv_ref[...], win_ref[Q_END:KV_END, :])
              + _dot(du_ref[...], win_ref[KV_END:U_END, :]) + _dot(dv_ref[...], win_ref[U_END:Z_END, :])
              + _dot(dga_ref[...], win_ref[Z_END:GA_END, :]) + _dot(dgb_ref[...], win_ref[GA_END:, :]))
        xv = x_ref[...]
        r = _rms(xv)
        xn = xv * r
        dxn = dh * (w_ref[...] * (1.0 + sc_ref[...]))
        dx_ref[...] = dxin_ref[...] + r * (dxn - xn * jnp.mean(dxn * xn, axis=-1, keepdims=True))
        da_ref[...] += jnp.sum(dh * xn, axis=0, keepdims=True)
        dsh_ref[...] += jnp.sum(dh, axis=0, keepdims=True)

    return _call(body, name, (_sds((s, d), F32), _sds((1, d), F32), _sds((1, d), F32)), (s // tm,),
                 [_rows(tm, d), _rows(tm, 2 * LANES), _rows(tm, d), _rows(tm, d), _rows(tm, d), _rows(tm, d),
                  _resident(w_in.shape), _rows(tm, d), _vec(d), _vec(d), _rows(tm, d)],
                 (_rows(tm, d), _vec(d), _vec(d)), ("arbitrary",))(dq, dkv, du, dv, dga, dgb, w_in, x, w, sc, dx_in)


def _rope_bwd(name, dqr, dkv_cur, dkv_prev, cosf, sinf, tm=512):
    s = dqr.shape[0]
    tm = min(tm, s)
    steps = s // tm
    per = tm // ATTN_BLOCK
    nb = s // ATTN_BLOCK

    def unrope(v, cosv, sinv):
        return v * cosv - _rope_partner(v) * sinv

    def body(dq_ref, cur_ref, prev_ref, next_ref, cos_ref, sin_ref, dqo_ref, dkvo_ref):
        i = pl.program_id(0)
        cosv, sinv = cos_ref[...], sin_ref[...]
        for j in range(D_MODEL // LANES):
            dqo_ref[:, j * LANES:(j + 1) * LANES] = unrope(dq_ref[:, j * LANES:(j + 1) * LANES], cosv, sinv).astype(BF)
        nxt = jnp.where(i < steps - 1, next_ref[...], 0.0)
        if per > 1:
            shifted = jnp.concatenate([prev_ref[ATTN_BLOCK:, :], nxt], axis=0)
        else:
            shifted = nxt
        tot = cur_ref[...] + shifted
        dkvo_ref[:, :LANES] = unrope(tot[:, :LANES], cosv, sinv).astype(BF)
        dkvo_ref[:, LANES:] = tot[:, LANES:].astype(BF)

    nxt_spec = pl.BlockSpec((ATTN_BLOCK, 2 * LANES), lambda i: (jnp.minimum((i + 1) * per, nb - 1), 0))
    return _call(body, name, (_sds((s, D_MODEL), BF), _sds((s, 2 * LANES), BF)), (steps,),
                 [_rows(tm, D_MODEL), _rows(tm, 2 * LANES), _rows(tm, 2 * LANES), nxt_spec, _rows(tm, LANES),
                  _rows(tm, LANES)],
                 (_rows(tm, D_MODEL), _rows(tm, 2 * LANES)), ("parallel",))(dqr, dkv_cur, dkv_prev, dkv_prev, cosf, sinf)


Q_PER_KV = N_Q_HEADS // N_KV_HEADS
ATTN_SCALE = HEAD_DIM ** -0.5
HEADS_AHEAD_FWD = 2
HEADS_AHEAD_BWD = 3


def _band_mask_t(n):
    kj = lax.broadcasted_iota(jnp.int32, (2 * ATTN_BLOCK, ATTN_BLOCK), 0)
    qi = lax.broadcasted_iota(jnp.int32, (2 * ATTN_BLOCK, ATTN_BLOCK), 1)
    return (kj > qi) & (kj <= qi + ATTN_BLOCK) & ((n > 0) | (kj >= ATTN_BLOCK))


def _softmax_t(raw, allowed, sink):
    sc = jnp.where(allowed, raw, NEG_BIG)
    m = jnp.maximum(jnp.max(sc, axis=0, keepdims=True), sink)
    p = jnp.exp(sc - m)
    esink = jnp.exp(sink - m)
    inv = 1.0 / (jnp.sum(p, axis=0, keepdims=True) + esink)
    return p * inv, esink * inv


def _kv_specs():
    cur = pl.BlockSpec((ATTN_BLOCK, LANES), lambda n: (n, 0))
    prev = pl.BlockSpec((ATTN_BLOCK, LANES), lambda n: (jnp.maximum(n - 1, 0), 0))
    return [prev, cur] * 4


def _attention(name, qr, kk0, kk1, vv0, vv1, sinks):
    s = qr.shape[0]
    nb = s // ATTN_BLOCK

    def body(sink_ref, q_ref, k0p, k0c, k1p, k1c, v0p, v0c, v1p, v1c, y_ref):
        allowed = _band_mask_t(pl.program_id(0))
        upper = lax.broadcasted_iota(jnp.int32, (1, LANES), 1) >= HEAD_DIM
        upper_rows = lax.broadcasted_iota(jnp.int32, (LANES, 1), 0) >= HEAD_DIM
        bands = ((jnp.concatenate([k0p[...], k0c[...]], axis=0), jnp.concatenate([v0p[...], v0c[...]], axis=0)),
                 (jnp.concatenate([k1p[...], k1c[...]], axis=0), jnp.concatenate([v1p[...], v1c[...]], axis=0)))
        vbts = (bands[0][1].T, bands[1][1].T)

        def scores(h):
            hk, j, half = h // Q_PER_KV, (h % Q_PER_KV) // 2, h % 2
            col = (hk * 4 + j) * LANES
            qp = q_ref[:, col:col + LANES]
            return _dot_nt(bands[hk][0], jnp.where(upper if half else jnp.logical_not(upper), qp, jnp.zeros_like(qp)))

        out_t = None
        ahead = [scores(h) for h in range(HEADS_AHEAD_FWD)]
        for h in range(N_Q_HEADS):
            hk, j, half = h // Q_PER_KV, (h % Q_PER_KV) // 2, h % 2
            raw = ahead.pop(0)
            if h + HEADS_AHEAD_FWD < N_Q_HEADS:
                ahead.append(scores(h + HEADS_AHEAD_FWD))
            pn, _ = _softmax_t(raw, allowed, sink_ref[h])
            o_h = _dot(vbts[hk], pn.astype(BF))
            out_t = jnp.where(upper_rows, o_h, out_t) if half else o_h
            if half:
                col = (hk * 4 + j) * LANES
                y_ref[:, col:col + LANES] = out_t.T.astype(BF)

    return _call(body, name, _sds((s, D_MODEL), BF), (nb,),
                 [pl.BlockSpec(memory_space=pltpu.SMEM), pl.BlockSpec((ATTN_BLOCK, D_MODEL), lambda n: (n, 0))] + _kv_specs(),
                 pl.BlockSpec((ATTN_BLOCK, D_MODEL), lambda n: (n, 0)), ("parallel",))(
                     sinks, qr, kk0, kk0, kk1, kk1, vv0, vv0, vv1, vv1)


def _attention_bwd(name, qr, kk0, kk1, vv0, vv1, sinks, dy, after=None):
    s = qr.shape[0]
    nb = s // ATTN_BLOCK

    def body(sink_ref, q_ref, dy_ref, k0p, k0c, k1p, k1c, v0p, v0c, v1p, v1c, dq_ref, cur_ref, prev_ref, dsink_ref):
        @pl.when(pl.program_id(0) == 0)
        def _():
            dsink_ref[...] = jnp.zeros_like(dsink_ref)

        allowed = _band_mask_t(pl.program_id(0))
        lane = lax.broadcasted_iota(jnp.int32, (1, LANES), 1)
        upper = lane >= HEAD_DIM
        upper_rows = lax.broadcasted_iota(jnp.int32, (LANES, 1), 0) >= HEAD_DIM
        bands = ((jnp.concatenate([k0p[...], k0c[...]], axis=0), jnp.concatenate([v0p[...], v0c[...]], axis=0)),
                 (jnp.concatenate([k1p[...], k1c[...]], axis=0), jnp.concatenate([v1p[...], v1c[...]], axis=0)))
        kbts = (bands[0][0].T, bands[1][0].T)

        def scores(h):
            hk, j, half = h // Q_PER_KV, (h % Q_PER_KV) // 2, h % 2
            kb, vb = bands[hk]
            col = (hk * 4 + j) * LANES
            sel = upper if half else jnp.logical_not(upper)
            qp = q_ref[:, col:col + LANES]
            qa = jnp.where(sel, qp, jnp.zeros_like(qp))
            dya = jnp.where(sel, dy_ref[:, col:col + LANES], 0.0).astype(BF)
            return qa, dya, _dot_nt(kb, qa), _dot_nt(vb, dya)

        dsink = jnp.zeros((1, LANES), F32)
        dk_slab = jnp.zeros((2 * ATTN_BLOCK, LANES), F32)
        dv_slab = jnp.zeros((2 * ATTN_BLOCK, LANES), F32)
        dkk = dvv = dq_t = None
        ahead = [scores(h) for h in range(HEADS_AHEAD_BWD)]
        for h in range(N_Q_HEADS):
            hk, j, half = h // Q_PER_KV, (h % Q_PER_KV) // 2, h % 2
            qa, dya, raw, dp = ahead.pop(0)
            if h + HEADS_AHEAD_BWD < N_Q_HEADS:
                ahead.append(scores(h + HEADS_AHEAD_BWD))
            pn, psink = _softmax_t(raw, allowed, sink_ref[h])
            delta = jnp.sum(pn * dp, axis=0, keepdims=True)
            ds = (pn * (dp - delta)).astype(BF)
            dsink = dsink + jnp.where(lane == h, -jnp.sum(psink * delta), 0.0)
            dq_h = _dot(kbts[hk], ds) * ATTN_SCALE
            dq_t = jnp.where(upper_rows, dq_h, dq_t) if half else dq_h
            dk_h, dv_h = _dot(ds, qa), _dot(pn.astype(BF), dya)
            dkk, dvv = (dk_h, dv_h) if h % Q_PER_KV == 0 else (dkk + dk_h, dvv + dv_h)
            if half:
                col = (hk * 4 + j) * LANES
                dq_ref[:, col:col + LANES] = dq_t.T
            if h % Q_PER_KV == Q_PER_KV - 1:
                mine = upper if hk else jnp.logical_not(upper)
                dk_slab = jnp.where(mine, dkk + pltpu.roll(dkk, HEAD_DIM, axis=1), dk_slab)
                dv_slab = jnp.where(mine, dvv + pltpu.roll(dvv, HEAD_DIM, axis=1), dv_slab)
        prev_ref[:, :LANES] = dk_slab[:ATTN_BLOCK]
        prev_ref[:, LANES:] = dv_slab[:ATTN_BLOCK]
        cur_ref[:, :LANES] = dk_slab[ATTN_BLOCK:]
        cur_ref[:, LANES:] = dv_slab[ATTN_BLOCK:]
        dsink_ref[...] += dsink

    blk = pl.BlockSpec((ATTN_BLOCK, D_MODEL), lambda n: (n, 0))
    kvo = pl.BlockSpec((ATTN_BLOCK, 2 * LANES), lambda n: (n, 0))
    return _call(body, name,
                 (_sds((s, D_MODEL), F32), _sds((s, 2 * LANES), F32), _sds((s, 2 * LANES), F32), _sds((1, LANES), F32)),
                 (nb,), [pl.BlockSpec(memory_space=pltpu.SMEM), blk, blk] + _kv_specs(),
                 (blk, kvo, kvo, pl.BlockSpec((1, LANES), lambda n: (0, 0))), ("arbitrary",), after=after)(
                     sinks, qr, dy, kk0, kk0, kk1, kk1, vv0, vv0, vv1, vv1)


def _sgu_weights(wm_ref, g):
    t = lax.broadcasted_iota(jnp.int32, (SGU_CHUNK, SGU_CHUNK), 0)
    sidx = lax.broadcasted_iota(jnp.int32, (SGU_CHUNK, SGU_CHUNK), 1)
    return jnp.where(sidx <= t, wm_ref[g], 0.0).astype(BF)


def _layer_norm_stats(v):
    mu = jnp.mean(v, axis=-1, keepdims=True)
    cen = v - mu
    rstd = lax.rsqrt(jnp.mean(cen * cen, axis=-1, keepdims=True) + NORM_EPS)
    return cen * rstd, rstd


def _sgu(name, u_pre, v_pre, ln_w, ln_b, wm, bfull, tm=512):
    s, w = u_pre.shape
    tm = min(tm, s)

    def body(u_ref, v_ref, lw_ref, lb_ref, wm_ref, b_ref, y_ref):
        vhat, _ = _layer_norm_stats(_gelu(v_ref[...]))
        vn = (vhat * lw_ref[...] + lb_ref[...]).astype(BF)
        for g in range(SGU_GROUPS):
            wg = _sgu_weights(wm_ref, g)
            cols = slice(g * SGU_CHUNK, (g + 1) * SGU_CHUNK)
            for ch in range(tm // SGU_CHUNK):
                rows = slice(ch * SGU_CHUNK, (ch + 1) * SGU_CHUNK)
                f = _dot(wg, vn[rows, cols]) + b_ref[g]
                y_ref[rows, cols] = (_gelu(u_ref[rows, cols]) * f).astype(BF)

    full3 = pl.BlockSpec((SGU_GROUPS, SGU_CHUNK, SGU_CHUNK), lambda i: (0, 0, 0))
    return _call(body, name, _sds((s, w), BF), (s // tm,),
                 [_rows(tm, w), _rows(tm, w), _vec(w), _vec(w), full3, full3],
                 _rows(tm, w), ("parallel",))(u_pre, v_pre, ln_w, ln_b, wm, bfull)


def _sgu_bwd(name, u_pre, v_pre, ln_w, ln_b, wm, bfull, dy, tm=512, after=None):
    s, w = u_pre.shape
    tm = min(tm, s)
    steps = s // tm

    def body(u_ref, v_ref, lw_ref, lb_ref, wm_ref, b_ref, dy_ref, du_ref, dv_ref, dwm_ref, db_ref, dlw_ref, dlb_ref,
             dfsum_ref):
        i = pl.program_id(0)

        @pl.when(i == 0)
        def _():
            dwm_ref[...] = jnp.zeros_like(dwm_ref)
            dlw_ref[...] = jnp.zeros_like(dlw_ref)
            dlb_ref[...] = jnp.zeros_like(dlb_ref)
            dfsum_ref[...] = jnp.zeros_like(dfsum_ref)

        vpre = v_ref[...]
        vg, dvg_dv = _gelu_and_grad(vpre)
        vhat, rstd = _layer_norm_stats(vg)
        vn = (vhat * lw_ref[...] + lb_ref[...]).astype(BF)
        t = lax.broadcasted_iota(jnp.int32, (SGU_CHUNK, SGU_CHUNK), 0)
        sidx = lax.broadcasted_iota(jnp.int32, (SGU_CHUNK, SGU_CHUNK), 1)
        dvn_cols = []
        for g in range(SGU_GROUPS):
            wg = _sgu_weights(wm_ref, g)
            cols = slice(g * SGU_CHUNK, (g + 1) * SGU_CHUNK)
            dvn_rows = []
            dwg = jnp.zeros((SGU_CHUNK, SGU_CHUNK), F32)
            dfs = jnp.zeros((SGU_CHUNK, SGU_CHUNK), F32)
            for ch in range(tm // SGU_CHUNK):
                rows = slice(ch * SGU_CHUNK, (ch + 1) * SGU_CHUNK)
                upre = u_ref[rows, cols]
                dyv = dy_ref[rows, cols].astype(F32)
                f = _dot(wg, vn[rows, cols]) + b_ref[g]
                ug, dug_du = _gelu_and_grad(upre)
                du_ref[rows, cols] = (dyv * f * dug_du).astype(BF)
                df = dyv * ug
                dfb = df.astype(BF)
                dvn_rows.append(_dot_tn(wg, dfb))
                dwg = dwg + _dot_nt(dfb, vn[rows, cols])
                dfs = dfs + df
            dwm_ref[g] += jnp.where(sidx <= t, dwg, 0.0)
            dfsum_ref[g] += dfs
            dvn_cols.append(jnp.concatenate(dvn_rows, axis=0) if len(dvn_rows) > 1 else dvn_rows[0])
        dvn = jnp.concatenate(dvn_cols, axis=1)
        dlw_ref[...] += jnp.sum(dvn * vhat, axis=0, keepdims=True)
        dlb_ref[...] += jnp.sum(dvn, axis=0, keepdims=True)
        dvh = dvn * lw_ref[...]
        dvg = rstd * (dvh - jnp.mean(dvh, axis=-1, keepdims=True) - vhat * jnp.mean(dvh * vhat, axis=-1, keepdims=True))
        dv_ref[...] = (dvg * dvg_dv).astype(BF)

        @pl.when(i == steps - 1)
        def _():
            for g in range(SGU_GROUPS):
                db_ref[g:g + 1, :] = jnp.sum(dfsum_ref[g].T, axis=0, keepdims=True)

    full3 = pl.BlockSpec((SGU_GROUPS, SGU_CHUNK, SGU_CHUNK), lambda i: (0, 0, 0))
    return _call(body, name,
                 (_sds((s, w), BF), _sds((s, w), BF), _sds((SGU_GROUPS, SGU_CHUNK, SGU_CHUNK), F32),
                  _sds((SGU_GROUPS, SGU_CHUNK), F32), _sds((1, w), F32), _sds((1, w), F32)),
                 (steps,),
                 [_rows(tm, w), _rows(tm, w), _vec(w), _vec(w), full3, full3, _rows(tm, w)],
                 (_rows(tm, w), _rows(tm, w), full3, pl.BlockSpec((SGU_GROUPS, SGU_CHUNK), lambda i: (0, 0)), _vec(w), _vec(w)),
                 ("arbitrary",), scratch=[pltpu.VMEM((SGU_GROUPS, SGU_CHUNK, SGU_CHUNK), F32)], after=after)(
                     u_pre, v_pre, ln_w, ln_b, wm, bfull, dy)


def _mix_out(name, y_sgu, y_attn, ga_pre, gb_pre, x, g1, proj_a, proj_b, w_out, w2, sc2, sh2, tm=512):
    s, d = x.shape
    tm = min(tm, s)

    def body(ys_ref, ya_ref, ga_ref, gb_ref, x_ref, g1_ref, wa_ref, wb_ref, wo_ref, w2_ref, sc2_ref, sh2_ref,
             m_ref, pa_ref, pb_ref, o_ref, x1_ref, h2_ref):
        pa = _dot(ys_ref[...], wa_ref[...].reshape(d, d))
        pb = _dot(ya_ref[...], wb_ref[...].reshape(d, d))
        pa_ref[...] = pa.astype(BF)
        pb_ref[...] = pb.astype(BF)
        merged = (_sigmoid(ga_ref[...].astype(F32)) * pa + _sigmoid(gb_ref[...].astype(F32)) * pb).astype(BF)
        m_ref[...] = merged
        o = _dot(merged, wo_ref[...].reshape(d, d))
        o_ref[...] = o.astype(BF)
        x1 = x_ref[...] + g1_ref[...] * o
        x1_ref[...] = x1
        h2_ref[...] = ((x1 * _rms(x1)) * w2_ref[...] * (1.0 + sc2_ref[...]) + sh2_ref[...]).astype(BF)

    f, b = _sds((s, d), F32), _sds((s, d), BF)
    r = _rows(tm, d)
    wspec = _resident(proj_a.shape)
    return _call(body, name, (b, b, b, b, f, b), (s // tm,),
                 [r, r, r, r, r, _vec(d), wspec, wspec, wspec, _vec(d), _vec(d), _vec(d)], (r,) * 6, ("parallel",))(
                     y_sgu, y_attn, ga_pre, gb_pre, x, g1, proj_a, proj_b, w_out, w2, sc2, sh2)


def _mix_bwd(name, do, w_out, proj_a, proj_b, ga_pre, gb_pre, pa, pb, tm=512):
    s, d = do.shape
    tm = min(tm, s)

    def body(do_ref, wo_ref, wa_ref, wb_ref, ga_ref, gb_ref, pa_ref, pb_ref,
             dpa_ref, dpb_ref, dga_ref, dgb_ref, dys_ref, dya_ref):
        dm = _dot_nt(do_ref[...], wo_ref[...].reshape(d, d))
        ga = _sigmoid(ga_ref[...].astype(F32))
        gb = _sigmoid(gb_ref[...].astype(F32))
        dpa = (dm * ga).astype(BF)
        dpb = (dm * gb).astype(BF)
        dpa_ref[...] = dpa
        dpb_ref[...] = dpb
        dga_ref[...] = (dm * pa_ref[...].astype(F32) * ga * (1.0 - ga)).astype(BF)
        dgb_ref[...] = (dm * pb_ref[...].astype(F32) * gb * (1.0 - gb)).astype(BF)
        dys_ref[...] = _dot_nt(dpa, wa_ref[...].reshape(d, d)).astype(BF)
        dya_ref[...] = _dot_nt(dpb, wb_ref[...].reshape(d, d)).astype(BF)

    f, b = _sds((s, d), F32), _sds((s, d), BF)
    r = _rows(tm, d)
    wspec = _resident(w_out.shape)
    return _call(body, name, (b, b, b, b, b, b), (s // tm,), [r, wspec, wspec, wspec, r, r, r, r], (r,) * 6,
                 ("parallel",))(do, w_out, proj_a, proj_b, ga_pre, gb_pre, pa, pb)


def _ffn_up_act(name, h2, w_gate, w_up, cw, cb, tm=1024):
    s, d = h2.shape
    tm = min(tm, s)
    tc = FFN_CHUNK
    per = tm // HALO

    def body(h_ref, hprev_ref, wg_ref, wu_ref, cw_ref, cb_ref, a_ref, ac_ref, up_ref, hf_ref):
        hv = h_ref[...]
        wg = wg_ref[...].reshape(tc, d)
        a = _dot_nt(hv, wg).astype(BF)
        up = _dot_nt(hv, wu_ref[...].reshape(tc, d)).astype(BF)
        a_ref[...] = a
        up_ref[...] = up
        prev = jnp.where(pl.program_id(1) > 0, _dot_nt(hprev_ref[...], wg).astype(BF).astype(F32), 0.0)
        ext = jnp.concatenate([prev, a.astype(F32)], axis=0)
        ac = (cb_ref[...] + cw_ref[0:1, :] * pltpu.roll(ext, 2, axis=0) + cw_ref[1:2, :] * pltpu.roll(ext, 1, axis=0)
              + cw_ref[2:3, :] * ext)[HALO:]
        ac_ref[...] = ac.astype(BF)
        hf_ref[...] = (ac * _sigmoid(ac) * up.astype(F32)).astype(BF)

    wspec = pl.BlockSpec((CHIPS_PER_CHUNK, FFN_SHARD, d), lambda j, i: (j, 0, 0))
    ospec = pl.BlockSpec((tm, tc), lambda j, i: (i, j))
    o = _sds((s, FFN_DIM), BF)
    return _call(body, name, (o, o, o, o), (FFN_CHUNKS, s // tm),
                 [pl.BlockSpec((tm, d), lambda j, i: (i, 0)), pl.BlockSpec((HALO, d), lambda j, i: (jnp.maximum(i * per - 1, 0), 0)),
                  wspec, wspec, pl.BlockSpec((3, tc), lambda j, i: (0, j)), pl.BlockSpec((1, tc), lambda j, i: (0, j))],
                 (ospec, ospec, ospec, ospec), ("parallel", "parallel"))(h2, h2, w_gate, w_up, cw, cb)


def _ffn_down(name, hf, w_down, x1, g2, tm=512):
    s, d = x1.shape
    tm = min(tm, s)

    def body(hf_ref, wd_ref, x1_ref, g2_ref, dn_ref, x2_ref):
        dn = _dot(hf_ref[...], wd_ref[...].reshape(FFN_DIM, d))
        dn_ref[...] = dn.astype(BF)
        x2_ref[...] = x1_ref[...] + g2_ref[...] * dn

    return _call(body, name, (_sds((s, d), BF), _sds((s, d), F32)), (s // tm,),
                 [_rows(tm, FFN_DIM), _resident(w_down.shape), _rows(tm, d), _vec(d)],
                 (_rows(tm, d), _rows(tm, d)), ("parallel",))(hf, w_down, x1, g2)


def _ffn_down_bwd_act(name, dx2, dn, g2, w_down, a, ac, up, cw, tm=256, after=None):
    s, d = dx2.shape
    c = a.shape[1]
    tm = min(tm, s)
    tc = FFN_CHUNK
    per = tm // HALO
    steps = s // tm
    last = s // HALO - 1
    n = tm + HALO

    def body(dx_ref, dxnext_ref, dn_ref, g2_ref, wd_ref, a_ref, ac_ref, acnext_ref, up_ref, upnext_ref, cw_ref,
             ddn_ref, da_ref, dup_ref, dg_ref, dcw_ref, dcb_ref):
        i = pl.program_id(0)

        @pl.when(i == 0)
        def _():
            dg_ref[...] = jnp.zeros_like(dg_ref)
            dcw_ref[...] = jnp.zeros_like(dcw_ref)
            dcb_ref[...] = jnp.zeros_like(dcb_ref)

        dxv = dx_ref[...]
        ddn = (dxv * g2_ref[...]).astype(BF)
        ddn_ref[...] = ddn
        dg_ref[...] += jnp.sum(dxv * dn_ref[...].astype(F32), axis=0, keepdims=True)
        ddn_next = jnp.where(i < steps - 1, dxnext_ref[...] * g2_ref[...], 0.0).astype(BF)
        ddn_ext = jnp.concatenate([ddn, ddn_next], axis=0)
        for k in range(FFN_CHUNKS):
            cols = slice(k * tc, (k + 1) * tc)
            dh = _dot_nt(ddn_ext, wd_ref[k * CHIPS_PER_CHUNK:(k + 1) * CHIPS_PER_CHUNK].reshape(tc, d))
            ace = jnp.concatenate([ac_ref[:, cols].astype(F32), acnext_ref[:, cols].astype(F32)], axis=0)
            upe = jnp.concatenate([up_ref[:, cols].astype(F32), upnext_ref[:, cols].astype(F32)], axis=0)
            sig = _sigmoid(ace)
            silu = ace * sig
            dac = dh * upe * (sig + silu * (1.0 - sig))
            dup_ref[:, cols] = (dh[:tm] * silu[:tm]).astype(BF)
            d1 = pltpu.roll(dac, n - 1, axis=0)[:tm]
            d2 = pltpu.roll(dac, n - 2, axis=0)[:tm]
            d0 = dac[:tm]
            da_ref[:, cols] = (cw_ref[2:3, cols] * d0 + cw_ref[1:2, cols] * d1 + cw_ref[0:1, cols] * d2).astype(BF)
            a0 = a_ref[:, cols].astype(F32)
            dcb_ref[:, cols] += jnp.sum(d0, axis=0, keepdims=True)
            dcw_ref[0:1, cols] += jnp.sum(d2 * a0, axis=0, keepdims=True)
            dcw_ref[1:2, cols] += jnp.sum(d1 * a0, axis=0, keepdims=True)
            dcw_ref[2:3, cols] += jnp.sum(d0 * a0, axis=0, keepdims=True)

    nxt = lambda width: pl.BlockSpec((HALO, width), lambda i: (jnp.minimum((i + 1) * per, last), 0))
    wide = _sds((s, c), BF)
    return _call(body, name, (_sds((s, d), BF), wide, wide, _sds((1, d), F32), _sds((3, c), F32), _sds((1, c), F32)), (steps,),
                 [_rows(tm, d), nxt(d), _rows(tm, d), _vec(d), _resident(w_down.shape), _rows(tm, c), _rows(tm, c), nxt(c),
                  _rows(tm, c), nxt(c), pl.BlockSpec((3, c), lambda i: (0, 0))],
                 (_rows(tm, d), _rows(tm, c), _rows(tm, c), _vec(d), pl.BlockSpec((3, c), lambda i: (0, 0)), _vec(c)),
                 ("arbitrary",), after=after)(dx2, dx2, dn, g2, w_down, a, ac, ac, up, up, cw)


def _ffn_up_bwd(name, da, dup, w_gate, w_up, x1, dx2, w2, sc2, o, g1, tm=512, after=None):
    s, d = x1.shape
    tm = min(tm, s)

    def body(da_ref, dup_ref, wg_ref, wu_ref, x1_ref, dx2_ref, w2_ref, sc2_ref, o_ref, g1_ref,
             dx1_ref, do_ref, dnw_ref, dsh_ref, dg1_ref):
        @pl.when(pl.program_id(0) == 0)
        def _():
            dnw_ref[...] = jnp.zeros_like(dnw_ref)
            dsh_ref[...] = jnp.zeros_like(dsh_ref)
            dg1_ref[...] = jnp.zeros_like(dg1_ref)

        dh = _dot(da_ref[...], wg_ref[...].reshape(FFN_DIM, d)) + _dot(dup_ref[...], wu_ref[...].reshape(FFN_DIM, d))
        xv = x1_ref[...]
        r = _rms(xv)
        xn = xv * r
        dxn = dh * (w2_ref[...] * (1.0 + sc2_ref[...]))
        dx1 = dx2_ref[...] + r * (dxn - xn * jnp.mean(dxn * xn, axis=-1, keepdims=True))
        dx1_ref[...] = dx1
        dnw_ref[...] += jnp.sum(dh * xn, axis=0, keepdims=True)
        dsh_ref[...] += jnp.sum(dh, axis=0, keepdims=True)
        do_ref[...] = (dx1 * g1_ref[...]).astype(BF)
        dg1_ref[...] += jnp.sum(dx1 * o_ref[...].astype(F32), axis=0, keepdims=True)

    v = _sds((1, d), F32)
    r = _rows(tm, d)
    wspec = _resident(w_gate.shape)
    return _call(body, name, (_sds((s, d), F32), _sds((s, d), BF), v, v, v), (s // tm,),
                 [_rows(tm, FFN_DIM), _rows(tm, FFN_DIM), wspec, wspec, r, r, _vec(d), _vec(d), r, _vec(d)],
                 (r, r, _vec(d), _vec(d), _vec(d)), ("arbitrary",), after=after)(da, dup, w_gate, w_up, x1, dx2, w2, sc2, o, g1)


def _loss_head(name, x, w, target, tm=512):
    s, d = x.shape
    tm = min(tm, s)

    def body(x_ref, w_ref, t_ref, dx_ref, loss_ref, dw_ref):
        @pl.when(pl.program_id(0) == 0)
        def _():
            loss_ref[...] = jnp.zeros_like(loss_ref)
            dw_ref[...] = jnp.zeros_like(dw_ref)

        xv = x_ref[...]
        r = _rms(xv)
        xn = xv * r
        err = xn * w_ref[...] - t_ref[...]
        loss_ref[...] += 0.5 * jnp.sum(jnp.mean(err * err, axis=-1, keepdims=True))
        dy = err * (1.0 / d)
        dw_ref[...] += jnp.sum(dy * xn, axis=0, keepdims=True)
        dxn = dy * w_ref[...]
        dx_ref[...] = r * (dxn - xn * jnp.mean(dxn * xn, axis=-1, keepdims=True))

    return _call(body, name, (_sds((s, d), F32), _sds((1, LANES), F32), _sds((1, d), F32)), (s // tm,),
                 [_rows(tm, d), _vec(d), _rows(tm, d)], (_rows(tm, d), _vec(LANES), _vec(d)), ("arbitrary",))(x, w, target)


def _layer_fwd(l, x, mod, p, cosf, sinf, after=None, late=None, later=None):
    sh1, sc1, g1, sh2, sc2, g2 = mod
    tag = f"l{l}_"
    h, qr, kk0, kk1, vv0, vv1, u_pre, v_pre, ga_pre, gb_pre = _in_proj(
        tag + "in_proj", x, p["norm1_w"], sc1, sh1, p["w_in"], cosf, sinf, after=after)
    y_attn = _attention(tag + "attn", qr, kk0, kk1, vv0, vv1, p["sinks"])
    y_sgu = _sgu(tag + "sgu", u_pre, v_pre, p["sgu_ln_w"], p["sgu_ln_b"], p["sgu_w"], p["sgu_bfull"])
    if late is not None:
        p = dict(p, **late(y_sgu))
    merged, pa, pb, o, x1, h2 = _mix_out(tag + "mix_out", y_sgu, y_attn, ga_pre, gb_pre, x, g1, p["proj_a"], p["proj_b"],
                                         p["w_out"], p["norm2_w"], sc2, sh2)
    if later is not None:
        p = dict(p, **later(h2))
    a, ac, up, hf = _ffn_up_act(tag + "ffn_up", h2, p["w_gate"], p["w_up"], p["conv_w"], p["conv_b"])
    dn, x2 = _ffn_down(tag + "ffn_down", hf, p["w_down"], x1, g2)
    saved = dict(x=x, h=h, qr=qr, kk0=kk0, kk1=kk1, vv0=vv0, vv1=vv1, u_pre=u_pre, v_pre=v_pre, ga_pre=ga_pre,
                 gb_pre=gb_pre, y_attn=y_attn, y_sgu=y_sgu, merged=merged, pa=pa, pb=pb, o=o, x1=x1, h2=h2, a=a, ac=ac, up=up,
                 hf=hf, dn=dn)
    return x2, saved, p


def _layer_bwd(l, dx2, mod, p, sv, cosf, sinf, after=None, emit=None, tick=None):
    sh1, sc1, g1, sh2, sc2, g2 = mod
    tag = f"l{l}_b_"
    d = D_MODEL
    g = {}
    ready = (lambda names: emit({k: g.pop(k) for k in names})) if emit else (lambda names: None)
    tick = tick or (lambda y: None)
    ddn, da, dup, dg2, g["conv_w"], g["conv_b"] = _ffn_down_bwd_act(
        tag + "ffn_down", dx2, sv["dn"], g2, p["w_down"], sv["a"], sv["ac"], sv["up"], p["conv_w"], after=after)
    g["w_down"] = _matmul_tn(tag + "dw_down", sv["hf"], ddn, tk=FFN_CHUNK, after=tick(ddn)).reshape(N_CHIPS, FFN_SHARD, d)
    g["w_gate"] = _matmul_tn(tag + "dw_gate", da, sv["h2"], tk=FFN_CHUNK).reshape(N_CHIPS, FFN_SHARD, d)
    g["w_up"] = _matmul_tn(tag + "dw_up", dup, sv["h2"], tk=FFN_CHUNK).reshape(N_CHIPS, FFN_SHARD, d)
    dx1, do, da2, dsh2, dg1 = _ffn_up_bwd(tag + "ffn_up", da, dup, p["w_gate"], p["w_up"], sv["x1"], dx2, p["norm2_w"],
                                          sc2, sv["o"], g1, after=ready(("w_down", "w_gate", "w_up")))
    g["norm2_w"] = da2 * (1.0 + sc2)
    dsc2 = da2 * p["norm2_w"]
    g["w_out"] = _matmul_tn(tag + "dw_out", sv["merged"], do, after=tick(do)).reshape(N_CHIPS, d // N_CHIPS, d)
    dpa, dpb, dga, dgb, dy_sgu, dy_attn = _mix_bwd(tag + "mix", do, p["w_out"], p["proj_a"], p["proj_b"], sv["ga_pre"],
                                                  sv["gb_pre"], sv["pa"], sv["pb"])
    g["proj_a"] = _matmul_tn(tag + "dproj_a", sv["y_sgu"], dpa).reshape(N_CHIPS, d // N_CHIPS, d)
    g["proj_b"] = _matmul_tn(tag + "dproj_b", sv["y_attn"], dpb).reshape(N_CHIPS, d // N_CHIPS, d)
    du, dv, g["sgu_w"], g["sgu_b"], g["sgu_ln_w"], g["sgu_ln_b"] = _sgu_bwd(
        tag + "sgu", sv["u_pre"], sv["v_pre"], p["sgu_ln_w"], p["sgu_ln_b"], p["sgu_w"], p["sgu_bfull"], dy_sgu,
        after=ready(("w_out", "proj_a", "proj_b")))
    dqr, dkv_cur, dkv_prev, dsink = _attention_bwd(tag + "attn", sv["qr"], sv["kk0"], sv["kk1"], sv["vv0"], sv["vv1"],
                                                   p["sinks"], dy_attn, after=tick(du))
    g["sinks"] = dsink[0, :N_Q_HEADS]
    dq, dkv = _rope_bwd(tag + "rope", dqr, dkv_cur, dkv_prev, cosf, sinf)
    dw_in = _matmul_tn_segments(tag + "dw_in", [dq, dkv, du, dv, dga, dgb], sv["h"])
    g["w_in"] = dw_in.reshape(N_CHIPS, IN_COLS // N_CHIPS, d)
    dx, da1, dsh1 = _in_proj_bwd(tag + "in_proj", dq, dkv, du, dv, dga, dgb, p["w_in"], sv["x"], p["norm1_w"], sc1, dx1)
    g["norm1_w"] = da1 * (1.0 + sc1)
    dsc1 = da1 * p["norm1_w"]
    return dx, (dsh1, dsc1, dg1, dsh2, dsc2, dg2), g


def _pad_to(a, axis, size):
    pad = [(0, 0)] * a.ndim
    pad[axis] = (0, size - a.shape[axis])
    return jnp.pad(a, pad)


def _early_params(w_in, small):
    d = D_MODEL
    return dict(
        w_in=w_in.reshape(IN_COLS, d), norm1_w=small["norm1_w"].reshape(1, d), sinks=small["sinks"],
        sgu_ln_w=small["sgu_ln_w"].reshape(1, d), sgu_ln_b=small["sgu_ln_b"].reshape(1, d), sgu_w=small["sgu_w"],
        sgu_bfull=jnp.broadcast_to(small["sgu_b"][:, :, None], (SGU_GROUPS, SGU_CHUNK, SGU_CHUNK)))


def _mix_params(proj_a, proj_b, w_out, small):
    return dict(proj_a=proj_a, proj_b=proj_b, w_out=w_out, norm2_w=small["norm2_w"].reshape(1, D_MODEL))


def _ffn_params(w_gate, w_up, w_down, conv_w, small):
    return dict(
        w_gate=w_gate, w_up=w_up, w_down=w_down, conv_w=conv_w.transpose(1, 0, 2).reshape(3, FFN_DIM),
        conv_b=small["conv_b"].reshape(1, FFN_DIM))


def _layer_params(w_in, proj_a, proj_b, w_out, w_gate, w_up, w_down, conv_w, small):
    return dict(_early_params(w_in, small), **_mix_params(proj_a, proj_b, w_out, small),
                **_ffn_params(w_gate, w_up, w_down, conv_w, small))


def _conv_grads_natural(g):
    cw = g["conv_w"]
    cb = g["conv_b"].reshape(FFN_DIM)
    return cw, cb


def _rope_tables(positions):
    inv_freq = ROPE_THETA ** (-jnp.arange(0, ROT_DIM, 2, dtype=F32) / ROT_DIM)
    ang = positions.astype(F32)[:, None] * inv_freq
    cos, sin = jnp.cos(ang), jnp.sin(ang)
    s = positions.shape[0]
    rest = HEAD_DIM - ROT_DIM
    cos_head = jnp.concatenate([cos, cos, jnp.ones((s, rest), F32)], axis=1)
    sin_head = jnp.concatenate([-sin, sin, jnp.zeros((s, rest), F32)], axis=1)
    return jnp.tile(cos_head, (1, LANES // HEAD_DIM)), jnp.tile(sin_head, (1, LANES // HEAD_DIM))


ADA_ROWS = 16


def _ada_fwd(name, c_rows, ada_w, ada_b_cols, tn=512):
    depth, d, n = ada_w.shape

    def body(c_ref, w_ref, b_ref, o_ref):
        cv = c_ref[...]
        act = (cv * _sigmoid(cv)).astype(BF)
        o_ref[0] = _dot(act, w_ref[0].astype(BF)) + b_ref[0]

    return _call(body, name, _sds((depth, ADA_ROWS, n), F32), (depth, n // tn),
                 [pl.BlockSpec((ADA_ROWS, d), lambda l, j: (0, 0)), pl.BlockSpec((1, d, tn), lambda l, j: (l, 0, j)),
                  pl.BlockSpec((1, 1, tn), lambda l, j: (l, 0, j))],
                 pl.BlockSpec((1, ADA_ROWS, tn), lambda l, j: (l, 0, j)), ("parallel", "parallel"))(c_rows, ada_w, ada_b_cols)


def _ada_bwd(name, c_rows, dmod_cols, tn=512):
    depth, _, n = dmod_cols.shape
    d = c_rows.shape[1]

    def body(c_ref, dm_ref, o_ref):
        cv = c_ref[...]
        act = (cv * _sigmoid(cv)).astype(BF)
        o_ref[0] = _dot_tn(act, dm_ref[0].astype(BF))

    return _call(body, name, _sds((depth, d, n), F32), (depth, n // tn),
                 [pl.BlockSpec((ADA_ROWS, d), lambda l, j: (0, 0)), pl.BlockSpec((1, ADA_ROWS, tn), lambda l, j: (l, 0, j))],
                 pl.BlockSpec((1, d, tn), lambda l, j: (l, 0, j)), ("parallel", "parallel"))(c_rows, dmod_cols)


def _colsum(name, a):
    r, n = a.shape

    def body(a_ref, o_ref):
        o_ref[...] = jnp.sum(a_ref[...], axis=0, keepdims=True)

    return _call(body, name, _sds((1, n), F32), (1,), [pl.BlockSpec((r, n), lambda i: (0, 0))],
                 pl.BlockSpec((1, n), lambda i: (0, 0)), ("arbitrary",))(a)


REL_SIBLING = (0, 0, 1)
REL_CHIPS = ((1, 0, 0), (0, 1, 0), (1, 1, 0))
REL_ALL = tuple((fx, fy, fc) for fx in (0, 1) for fy in (0, 1) for fc in (0, 1) if fx or fy or fc)


def _chip_of(dev):
    return 2 * dev[0] + dev[1]


def _dev_of(dev):
    return 4 * dev[0] + 2 * dev[1] + dev[2]


def _flip(dev, rel):
    return tuple(1 - m if f else m for m, f in zip(dev, rel))


def _exchange(name, arrays, n_out, stages, aliases=None):
    out_shapes, stages = stages[0], stages[1:]
    n_in = len(arrays)
    aliases = aliases or {}
    n_remote = sum(len(plan) for plan, _ in stages)
    n_local = sum(len(local) for _, local in stages)

    def at(ref, idx):
        return ref.at[idx] if len(idx) else ref

    def body(*refs):
        bufs = list(refs[:n_in + n_out])
        for i_in, i_out in aliases.items():
            bufs[i_in] = bufs[n_in + i_out]
        send_sems, recv_sems, local_sems = refs[n_in + n_out:]
        me = (lax.axis_index("x"), lax.axis_index("y"), lax.axis_index("c"))
        base_r = base_l = 0
        pending = []
        for plan, local in stages:
            def remote(k, entry, sender, receiver):
                rel, si, ssel, di, dsel = entry
                return pltpu.make_async_remote_copy(
                    src_ref=at(bufs[si], ssel(sender, receiver)), dst_ref=at(bufs[di], dsel(sender, receiver)),
                    send_sem=send_sems.at[k], recv_sem=recv_sems.at[k], device_id=_flip(me, rel), device_id_type=MESH)

            sends = [remote(base_r + k, e, me, _flip(me, e[0])) for k, e in enumerate(plan)]
            for cp in sends:
                cp.start()
            for k, (si, ssel, di, dsel) in enumerate(local):
                cp = pltpu.make_async_copy(at(bufs[si], ssel(me)), at(bufs[di], dsel(me)), local_sems.at[base_l + k])
                cp.start()
                pending.append(cp.wait)
            for k, e in enumerate(plan):
                remote(base_r + k, e, _flip(me, e[0]), me).wait_recv()
            pending += [cp.wait_send for cp in sends]
            base_r += len(plan)
            base_l += len(local)
        for wait in pending:
            wait()

    any_spec = pl.BlockSpec(memory_space=pl.ANY)
    return pl.pallas_call(
        body, name=name, out_shape=tuple(out_shapes), in_specs=[any_spec] * n_in, out_specs=tuple([any_spec] * n_out),
        input_output_aliases=dict(aliases),
        scratch_shapes=[pltpu.SemaphoreType.DMA((max(n_remote, 1),)), pltpu.SemaphoreType.DMA((max(n_remote, 1),)),
                        pltpu.SemaphoreType.DMA((max(n_local, 1),))])(*arrays)


HBM_SPEC = pl.BlockSpec(memory_space=pltpu.HBM)
SEM_SPEC = pl.BlockSpec(memory_space=pltpu.SEMAPHORE)


def _split_copies(bufs, plan, local, send_sems, recv_sems, local_sems):
    me = (lax.axis_index("x"), lax.axis_index("y"), lax.axis_index("c"))

    def at(ref, idx):
        return ref.at[idx] if len(idx) else ref

    def remote(k, sender, receiver):
        rel, si, ssel, di, dsel = plan[k]
        return pltpu.make_async_remote_copy(
            src_ref=at(bufs[si], ssel(sender, receiver)), dst_ref=at(bufs[di], dsel(sender, receiver)),
            send_sem=send_sems.at[k], recv_sem=recv_sems.at[k], device_id=_flip(me, rel), device_id_type=MESH)

    sends = [remote(k, me, _flip(me, plan[k][0])) for k in range(len(plan))]
    arrivals = [remote(k, _flip(me, plan[k][0]), me) for k in range(len(plan))]
    locs = [pltpu.make_async_copy(at(bufs[si], ssel(me)), at(bufs[di], dsel(me)), local_sems.at[k])
            for k, (si, ssel, di, dsel) in enumerate(local)]
    return sends, arrivals, locs


def _exchange_start(name, arrays, out_shapes, plan, local, inplace=False):
    n_in, n_out = len(arrays), len(out_shapes)
    n_buf = n_in + n_out

    def body(*refs):
        sems = refs[n_buf:n_buf + 3]
        bufs = refs[n_buf + 3:2 * n_buf + 3]
        sends, _, locs = _split_copies(bufs * 2 if inplace else bufs, plan, local, *sems)
        for cp in sends + locs:
            cp.start()
        refs[-1][...] = jnp.zeros_like(refs[-1])

    zones = [lax.empty(o.shape, o.dtype) for o in out_shapes]
    operands = [pltpu.with_memory_space_constraint(a, pltpu.HBM) for a in list(arrays) + zones]
    sem = lambda n: pltpu.SemaphoreType.DMA((max(n, 1),))
    out = pl.pallas_call(
        body, name=name,
        out_shape=(sem(len(plan)), sem(len(plan)), sem(len(local)), *[pltpu.HBM(a.shape, a.dtype) for a in operands],
                   _sds((SUBLANES, LANES), F32)),
        in_specs=[HBM_SPEC] * n_buf,
        out_specs=(SEM_SPEC, SEM_SPEC, SEM_SPEC, *[HBM_SPEC] * n_buf, pl.BlockSpec(memory_space=pltpu.VMEM)),
        input_output_aliases={i: 3 + i for i in range(n_buf)},
        compiler_params=pltpu.CompilerParams(has_side_effects=pltpu.SideEffectType.DATAFLOW_SIDE_EFFECTING))(*operands)
    pending = dict(name=name, sems=out[:3], thru=out[3:3 + n_in], zones=out[3 + n_in:3 + n_buf], plan=plan, local=local,
                   inplace=inplace)
    return pending, out[-1]


def _exchange_wait(pending, after, both=False):
    thru, zones, plan, local, inplace = (pending[k] for k in ("thru", "zones", "plan", "local", "inplace"))
    n_in, n_buf = len(thru), len(thru) + len(zones)

    def body(*refs):
        bufs = refs[:n_buf]
        sends, arrivals, locs = _split_copies(bufs * 2 if inplace else bufs, plan, local, *refs[n_buf:n_buf + 3])
        for cp in arrivals:
            cp.wait_recv()
        for cp in sends:
            cp.wait_send()
        for cp in locs:
            cp.wait()

    out = pl.pallas_call(
        body, name=pending["name"] + "_wait", out_shape=tuple(pltpu.HBM(a.shape, a.dtype) for a in list(thru) + list(zones)),
        in_specs=[HBM_SPEC] * n_buf + [SEM_SPEC] * 3 + [pl.BlockSpec(memory_space=pl.ANY)],
        out_specs=tuple([HBM_SPEC] * n_buf), input_output_aliases={i: i for i in range(n_buf)},
        compiler_params=pltpu.CompilerParams(has_side_effects=pltpu.SideEffectType.DATAFLOW_SIDE_EFFECTING))(
            *thru, *zones, *pending["sems"], after)
    if both:
        return out[:n_in], out[n_in:]
    return out[:n_in] if inplace else out[n_in:]


def _whole(*_):
    return ()


def _half_rows(rows, core):
    return pl.ds(core * (rows // 2), rows // 2)


def _gather_weights_plan(shards):
    n = len(shards)
    dsts = [_sds((N_CHIPS,) + a.shape, a.dtype) for a in shards]
    fetch, forward = [], []
    for t, a in enumerate(shards):
        rows = a.shape[0]
        if rows % (2 * 16) == 0:
            fetch += [(rel, t, (lambda s_, r_, rows=rows: (_half_rows(rows, s_[2]),)), n + t,
                       (lambda s_, r_, rows=rows: (_chip_of(s_), _half_rows(rows, s_[2])))) for rel in REL_CHIPS]
            forward += [(REL_SIBLING, n + t, (lambda s_, r_, rows=rows, rel=rel: (_chip_of(_flip(s_, rel)), _half_rows(rows, s_[2]))),
                         n + t, (lambda s_, r_, rows=rows, rel=rel: (_chip_of(_flip(s_, rel)), _half_rows(rows, s_[2]))))
                        for rel in REL_CHIPS]
        else:
            fetch += [(rel, t, _whole, n + t, lambda s_, r_: (_chip_of(s_),)) for rel in REL_CHIPS]
    local = [(t, _whole, n + t, lambda me: (_chip_of(me),)) for t in range(n)]
    return dsts, fetch, local, forward


def _gather_weights_start(name, shards):
    dsts, fetch, local, forward = _gather_weights_plan(shards)
    pending, token = _exchange_start(name, shards, dsts, fetch, local)
    return dict(pending, forward=forward), token


def _gather_weights_finish(pending, after):
    landed = _exchange_wait(pending, after)
    n = len(landed)
    return _exchange(pending["name"] + "_forward", landed, n, [[_sds(a.shape, a.dtype) for a in landed], (pending["forward"], [])],
                     aliases={t: t for t in range(n)})


def _gather_chips_plan(arrays):
    n = len(arrays)
    dsts = [_sds((N_CHIPS,) + a.shape, a.dtype) for a in arrays]
    plan = [(rel, t, _whole, n + t, lambda s_, r_: (_chip_of(s_),)) for t in range(n) for rel in REL_CHIPS]
    local = [(t, _whole, n + t, lambda me: (_chip_of(me),)) for t in range(n)]
    return dsts, plan, local


def _gather_chips(name, arrays):
    dsts, plan, local = _gather_chips_plan(arrays)
    return _exchange(name, arrays, len(arrays), [dsts, (plan, local)])


def _gather_chips_start(name, arrays):
    dsts, plan, local = _gather_chips_plan(arrays)
    return _exchange_start(name, arrays, dsts, plan, local)


def _gather_all(name, a):
    plan = [(rel, 0, _whole, 1, lambda s_, r_: (_dev_of(s_),)) for rel in REL_ALL]
    local = [(0, _whole, 1, lambda me: (_dev_of(me),))]
    return _exchange(name, [a], 1, [[_sds((2 * N_CHIPS,) + a.shape, a.dtype)], (plan, local)])[0]


def _swap_halves_start(name, grads):
    n = len(grads)
    dsts = [_sds((g.shape[0], g.shape[1] // 2, g.shape[2]), g.dtype) for g in grads]
    plan = [(REL_SIBLING, t, (lambda s_, r_, rows=g.shape[1]: (pl.ds(0, N_CHIPS), _half_rows(rows, r_[2]))), n + t, _whole)
            for t, g in enumerate(grads)]
    return _exchange_start(name, grads, dsts, plan, [])


def _scatter_chips_plan(sums):
    n = len(sums)
    dsts = [_sds(a.shape, a.dtype) for a in sums]
    plan = [(rel, t, lambda s_, r_: (_chip_of(r_),), n + t, lambda s_, r_: (_chip_of(s_),))
            for t in range(n) for rel in REL_CHIPS]
    local = [(t, lambda me: (_chip_of(me),), n + t, lambda me: (_chip_of(me),)) for t in range(n)]
    return dsts, plan, local


def _scatter_chips_start(name, sums):
    dsts, plan, local = _scatter_chips_plan(sums)
    return _exchange_start(name, sums, dsts, plan, local)


def _swap_back_start(name, totals, layer):
    n = len(totals)
    plan = [(REL_SIBLING, n + t, (lambda s_, r_, rows=a.shape[1]: (layer, _half_rows(rows, s_[2]))),
             n + t, (lambda s_, r_, rows=a.shape[1]: (layer, _half_rows(rows, s_[2])))) for t, a in enumerate(totals)]
    return _exchange_start(name, totals, [], plan, [], inplace=True)


def _add_halves(name, gs, recvs, core):
    n = len(gs)
    nch = recvs[0].shape[0]

    def body(core_ref, *refs):
        for g_ref, r_ref, o_ref in zip(refs[:n], refs[n:2 * n], refs[2 * n:]):
            o_ref[0] = (g_ref[0, 0].astype(F32) + r_ref[0].astype(F32)).astype(o_ref.dtype)

    halves = [pl.BlockSpec((1,) + r.shape[1:], lambda k, core_ref: (k, 0, 0)) for r in recvs]
    spec = pltpu.PrefetchScalarGridSpec(
        num_scalar_prefetch=1, grid=(nch,),
        in_specs=[pl.BlockSpec((1, 1) + r.shape[1:], lambda k, core_ref: (k, core_ref[0], 0, 0)) for r in recvs] + halves,
        out_specs=tuple(halves))
    return pl.pallas_call(body, name=name, out_shape=tuple(_sds(r.shape, r.dtype) for r in recvs), grid_spec=spec,
                          compiler_params=pltpu.CompilerParams(dimension_semantics=("parallel",),
                                                               vmem_limit_bytes=VMEM_LIMIT))(
                                                                   core, *[g.reshape(nch, 2, *r.shape[1:]) for g, r in zip(gs, recvs)],
                                                                   *recvs)


def _sum_chips(name, arrays, core, layer, totals):
    n = len(arrays)
    nch = arrays[0].shape[0]
    parts = 2
    shapes = [(a.shape[1] // parts, a.shape[2]) for a in arrays]

    def body(core_ref, *refs):
        for a_ref, o_ref in zip(refs[:n], refs[-n:]):
            acc = a_ref[0].astype(F32)
            for k in range(1, nch):
                acc = acc + a_ref[k].astype(F32)
            o_ref[0, 0, 0] = acc

    in_specs = [pl.BlockSpec((nch,) + shp, lambda i, core_ref: (0, i, 0)) for shp in shapes]
    args = [core, *arrays]
    if totals is not None:
        in_specs += [pl.BlockSpec(memory_space=pl.ANY)] * n
        args += [t.reshape(DEPTH, 2, parts, *shp) for t, shp in zip(totals, shapes)]
    spec = pltpu.PrefetchScalarGridSpec(
        num_scalar_prefetch=1, grid=(parts,), in_specs=in_specs,
        out_specs=tuple(pl.BlockSpec((1, 1, 1) + shp, lambda i, core_ref: (layer, core_ref[0], i, 0, 0)) for shp in shapes))
    outs = pl.pallas_call(body, name=name, out_shape=tuple(_sds((DEPTH, 2, parts) + shp, F32) for shp in shapes), grid_spec=spec,
                          input_output_aliases={1 + n + t: t for t in range(n)} if totals is not None else {},
                          compiler_params=pltpu.CompilerParams(dimension_semantics=("arbitrary",),
                                                               vmem_limit_bytes=VMEM_LIMIT))(*args)
    return [o.reshape(DEPTH, 2 * parts * shp[0], shp[1]) for o, shp in zip(outs, shapes)]


def _adamw_update(w, g, m, v):
    mn = ADAM_B1 * m + (1.0 - ADAM_B1) * g
    vn = ADAM_B2 * v + (1.0 - ADAM_B2) * (g * g)
    m_hat = mn / (1.0 - ADAM_B1 ** ADAM_STEP)
    v_hat = vn / (1.0 - ADAM_B2 ** ADAM_STEP)
    return -ADAM_LR * (m_hat / (jnp.sqrt(v_hat) + ADAM_EPS) + ADAM_WD * w), mn, vn


def _adamw(name, w, g, m, v, after=None):
    depth, r, c = w.shape
    tr = next(t for t in (512, 448, 384, 352, 336, 256, 192, 128, 64, 32, 16, 8) if r % t == 0 and t * c <= ADAM_TILE_ELEMS)

    def body(w_ref, g_ref, m_ref, v_ref, go_ref, d_ref, mo_ref, vo_ref):
        gv = g_ref[...]
        go_ref[...] = gv
        d_ref[...], mo_ref[...], vo_ref[...] = _adamw_update(w_ref[...], gv, m_ref[...], v_ref[...])

    spec = pl.BlockSpec((1, tr, c), lambda l, i: (l, i, 0))
    o = _sds(w.shape, F32)
    return _call(body, name, (o, o, o, o), (depth, r // tr), [spec] * 4, (spec,) * 4, ("parallel", "parallel"),
                 after=after)(w, g, m, v)


def _adamw_small(name, ws, gs, ms, vs):
    n = len(ws)

    def body(*refs):
        for t in range(n):
            w_ref, g_ref, m_ref, v_ref = (refs[k * n + t] for k in range(4))
            d_ref, mo_ref, vo_ref = (refs[(4 + k) * n + t] for k in range(3))
            d_ref[...], mo_ref[...], vo_ref[...] = _adamw_update(w_ref[...], g_ref[...], m_ref[...], v_ref[...])

    outs = [_sds(w.shape, F32) for w in ws]
    res = pl.pallas_call(body, name=name, out_shape=tuple(outs * 3))(*ws, *gs, *ms, *vs)
    return res[:n], res[n:2 * n], res[2 * n:]


def _pack(arrays, rows):
    flat = jnp.concatenate([a.reshape(-1).astype(F32) for a in arrays])
    return _pad_to(flat, 0, rows * LANES).reshape(rows, LANES)


def _unpack(packed, shapes):
    flat = packed.reshape(-1)
    out, off = [], 0
    for shp in shapes:
        n = 1
        for s_ in shp:
            n *= s_
        out.append(flat[off:off + n].reshape(shp))
        off += n
    return out


_MATRICES = ("w_in", "proj_a", "proj_b", "w_out", "w_gate", "w_up", "w_down")
_SMALL = (("norm1_w", (D_MODEL,)), ("sinks", (N_Q_HEADS,)), ("sgu_ln_w", (SGU_WIDTH,)), ("sgu_ln_b", (SGU_WIDTH,)),
          ("sgu_w", (SGU_GROUPS, SGU_CHUNK, SGU_CHUNK)), ("sgu_b", (SGU_GROUPS, SGU_CHUNK)), ("norm2_w", (D_MODEL,)),
          ("conv_w", (3, FFN_DIM)), ("conv_b", (FFN_DIM,)), ("final_norm_w", (D_MODEL,)))
SMALL_ROWS = 320
ADAM_TILE_ELEMS = 384 * 1024


def _reduce_cores_start(tag, partial):
    names = list(partial)
    pending, token = _swap_halves_start(tag + "_cores", [partial[k] for k in names])
    return dict(pending, tag=tag, names=names), token


def _reduce_chips_start(pending, core, after):
    tag, names = pending["tag"], pending["names"]
    mine, theirs = _exchange_wait(pending, after, both=True)
    sums = _add_halves(tag + "_cores_add", mine, theirs, core)
    scatter, token = _scatter_chips_start(tag + "_chips", sums)
    return dict(scatter, tag=tag, names=names), token


def _reduce_back_start(pending, l, core, totals, after):
    tag, names = pending["tag"], pending["names"]
    before = [totals[k] for k in names] if names[0] in totals else None
    sums = _sum_chips(tag + "_chips_add", _exchange_wait(pending, after), core, l, before)
    back, _ = _swap_back_start(tag + "_back", sums, l)
    return dict(back, names=names)


def kernel(x, c, positions, ada_w, ada_b, norm1_w, w_in, attn_sinks, sgu_ln_w, sgu_ln_b, sgu_w, sgu_b, proj_a, proj_b, w_out, norm2_w, ffn_w_gate, ffn_w_up, ffn_conv_w, ffn_conv_b, ffn_w_down, final_norm_w, loss_target, m_ada_w, m_ada_b, m_norm1_w, m_w_in, m_attn_sinks, m_sgu_ln_w, m_sgu_ln_b, m_sgu_w, m_sgu_b, m_proj_a, m_proj_b, m_w_out, m_norm2_w, m_ffn_w_gate, m_ffn_w_up, m_ffn_conv_w, m_ffn_conv_b, m_ffn_w_down, m_final_norm_w, v_ada_w, v_ada_b, v_norm1_w, v_w_in, v_attn_sinks, v_sgu_ln_w, v_sgu_ln_b, v_sgu_w, v_sgu_b, v_proj_a, v_proj_b, v_w_out, v_norm2_w, v_ffn_w_gate, v_ffn_w_up, v_ffn_conv_w, v_ffn_conv_b, v_ffn_w_down, v_final_norm_w):
    d = D_MODEL
    ax, ay, ac = lax.axis_index("x"), lax.axis_index("y"), lax.axis_index("c")
    chip = 2 * ax + ay
    dev = 4 * ax + 2 * ay + ac
    core = ac.astype(jnp.int32).reshape(1)

    c_all = _gather_all("gather_cond", c.reshape(SUBLANES, d // SUBLANES)).reshape(2 * N_CHIPS, d)
    c_rows = _pad_to(c_all, 0, ADA_ROWS)
    ada_cols = ada_w.shape[2]
    ada_b_cols = lax.dynamic_slice_in_dim(ada_b, chip * ada_cols, ada_cols, axis=1).reshape(DEPTH, 1, ada_cols)
    mod_cols = _ada_fwd("ada_fwd", c_rows, ada_w, ada_b_cols)
    mod_all = _gather_chips("gather_mod", [mod_cols])[0]
    mod_mine = lax.dynamic_index_in_dim(mod_all, dev, axis=2, keepdims=False)
    mod_mine = mod_mine.transpose(1, 0, 2).reshape(DEPTH, 1, 6 * d)
    mods = [tuple(jnp.split(mod_mine[l], 6, axis=-1)) for l in range(DEPTH)]

    tr = lambda a: jnp.swapaxes(a, 1, 2)
    shards = [tr(w_in).astype(BF), proj_a.astype(BF), proj_b.astype(BF), w_out.astype(BF),
              tr(ffn_w_gate).astype(BF), tr(ffn_w_up).astype(BF), ffn_w_down.astype(BF), ffn_conv_w]
    token = mod_all[0, 0, :SUBLANES, :LANES]
    fetches = []
    for l in range(DEPTH):
        groups = []
        for tag, members in (("in", shards[:1]), ("mix", shards[1:4]), ("ffn", shards[4:])):
            behind = (token[0, 0] * 0.0).astype(members[0].dtype)
            pending, token = _gather_weights_start(f"l{l}_gather_{tag}", [members[0][l] + behind] + [a[l] for a in members[1:]])
            groups.append(pending)
        fetches.append(groups)

    small_in = dict(norm1_w=norm1_w, sinks=attn_sinks, sgu_ln_w=sgu_ln_w, sgu_ln_b=sgu_ln_b, sgu_w=sgu_w, sgu_b=sgu_b,
                    norm2_w=norm2_w, conv_b=ffn_conv_b)
    cosf, sinf = _rope_tables(positions[0])
    small_of = lambda l: {k: v[l] for k, v in small_in.items()}

    h = x[0]
    saved, params = [], []
    for l in range(DEPTH):
        first, mix, ffn = fetches[l]
        w_in_l = _gather_weights_finish(first, token if l == 0 else h)
        late = lambda y, l=l, mix=mix: _mix_params(*_gather_weights_finish(mix, y), small_of(l))
        later = lambda y, l=l, ffn=ffn: _ffn_params(*_gather_weights_finish(ffn, y), small_of(l))
        h, sv, p = _layer_fwd(l, h, mods[l], _early_params(w_in_l[0], small_of(l)), cosf, sinf, late=late, later=later)
        saved.append(sv)
        params.append(p)
    dx, loss_part, d_final = _loss_head("loss_head", h, final_norm_w.reshape(1, d), loss_target[0])
    loss = lax.psum(loss_part[0, 0], ("x", "y", "c"))

    def small_pack(l, grads):
        cw, cb = _conv_grads_natural(grads)
        nat = dict(grads, conv_w=cw, conv_b=cb, final_norm_w=d_final if l == DEPTH - 1 else jnp.zeros((d,), F32))
        return _pack([nat[k] for k, _ in _SMALL], N_CHIPS * SMALL_ROWS).reshape(N_CHIPS, SMALL_ROWS, LANES)

    waiting, inflight = [], []

    def send(tag, partial):
        pending, token = _reduce_cores_start(tag, partial)
        waiting.append(pending)
        return token

    def tick(y):
        token = None
        while waiting:
            pending, token = _reduce_chips_start(waiting.pop(0), core, y)
            inflight.append(pending)
        return token

    dmods = [None] * DEPTH
    dx, dmods[1], grads = _layer_bwd(1, dx, mods[1], params[1], saved[1], cosf, sinf)
    token = send("l1_reduce", dict({k: grads[k] for k in _MATRICES}, small=small_pack(1, grads)))
    dx, dmods[0], grads = _layer_bwd(0, dx, mods[0], params[0], saved[0], cosf, sinf, after=token,
                                     emit=lambda part: send("l0_reduce_" + "_".join(part), part), tick=tick)
    dmod_mine = jnp.concatenate([jnp.concatenate(dmods[l], axis=1) for l in range(DEPTH)], axis=1)
    dmod_all = _gather_all("gather_dmod", dmod_mine.reshape(SUBLANES, -1)).reshape(2 * N_CHIPS, DEPTH * 6 * d)
    send("l0_reduce_in", dict(w_in=grads["w_in"], small=small_pack(0, grads) + dmod_all[0, 0] * 0.0))
    tick(dmod_all)

    totals, flying = {}, None

    def land(after):
        if flying is not None:
            totals.update(zip(flying["names"], _exchange_wait(flying, after)))

    for pending in inflight[:-1]:
        land(dx)
        flying = _reduce_back_start(pending, 1 if pending["tag"].startswith("l1") else 0, core, totals, dx)
    land(dx)
    flying = None

    g_ada_b = _colsum("ada_b_grad", dmod_all).reshape(DEPTH, 6 * d)
    dmod_cols = jnp.stack([lax.dynamic_slice_in_dim(dmod_all, l * 6 * d + chip * ada_cols, ada_cols, axis=1)
                           for l in range(DEPTH)])
    g_ada_w = _ada_bwd("ada_w_grad", c_rows, _pad_to(dmod_cols, 1, ADA_ROWS))
    big = dict(w_in=(tr(w_in), tr(m_w_in), tr(v_w_in)), proj_a=(proj_a, m_proj_a, v_proj_a), proj_b=(proj_b, m_proj_b, v_proj_b),
               w_out=(w_out, m_w_out, v_w_out), w_gate=(tr(ffn_w_gate), tr(m_ffn_w_gate), tr(v_ffn_w_gate)),
               w_up=(tr(ffn_w_up), tr(m_ffn_w_up), tr(v_ffn_w_up)), w_down=(ffn_w_down, m_ffn_w_down, v_ffn_w_down))
    upd, g_big = {}, {}

    def update(k, after=None):
        res = _adamw("adamw_" + k, big[k][0], totals[k], *big[k][1:], after=after)
        res = [tr(a) for a in res] if k in ("w_in", "w_gate", "w_up") else res
        g_big[k], upd[k] = res[0], res[1:]

    for k in ("w_down", "w_gate", "w_up", "w_out", "proj_a", "proj_b"):
        update(k)
    g_big["ada_w"], *upd["ada_w"] = _adamw("adamw_ada_w", ada_w, g_ada_w, m_ada_w, v_ada_w)
    flying = _reduce_back_start(inflight[-1], 0, core, totals, upd["ada_w"][0])
    land(upd["ada_w"][1])
    gathering, token = _gather_chips_start("gather_small", [totals["small"]])
    update("w_in", after=token)
    small_all = _exchange_wait(gathering, upd["w_in"][0])[0]
    small_g = small_all.transpose(1, 0, 2, 3).reshape(DEPTH, -1)
    per_layer = [_unpack(small_g[l], [shp for _, shp in _SMALL]) for l in range(DEPTH)]
    sg = {k: jnp.stack([per_layer[l][i] for l in range(DEPTH)]) for i, (k, _) in enumerate(_SMALL)}
    g_final = sg["final_norm_w"][DEPTH - 1]
    g_conv_w = lax.dynamic_slice_in_dim(sg["conv_w"], chip * FFN_SHARD, FFN_SHARD, axis=2)

    rest = [("ada_b", ada_b, g_ada_b, m_ada_b, v_ada_b), ("norm1_w", norm1_w, sg["norm1_w"], m_norm1_w, v_norm1_w),
            ("attn_sinks", attn_sinks, sg["sinks"], m_attn_sinks, v_attn_sinks),
            ("sgu_ln_w", sgu_ln_w, sg["sgu_ln_w"], m_sgu_ln_w, v_sgu_ln_w),
            ("sgu_ln_b", sgu_ln_b, sg["sgu_ln_b"], m_sgu_ln_b, v_sgu_ln_b), ("sgu_w", sgu_w, sg["sgu_w"], m_sgu_w, v_sgu_w),
            ("sgu_b", sgu_b, sg["sgu_b"], m_sgu_b, v_sgu_b), ("norm2_w", norm2_w, sg["norm2_w"], m_norm2_w, v_norm2_w),
            ("ffn_conv_w", ffn_conv_w, g_conv_w, m_ffn_conv_w, v_ffn_conv_w),
            ("ffn_conv_b", ffn_conv_b, sg["conv_b"], m_ffn_conv_b, v_ffn_conv_b),
            ("final_norm_w", final_norm_w.reshape(1, d), g_final.reshape(1, d), m_final_norm_w.reshape(1, d),
             v_final_norm_w.reshape(1, d))]
    rest_out = _adamw_small("adamw_rest", *[[r[i] for r in rest] for i in (1, 2, 3, 4)])
    g_rest = {r[0]: r[2] for r in rest}
    u_rest = {r[0]: tuple(o[i] for o in rest_out) for i, r in enumerate(rest)}
    g_rest["final_norm_w"] = g_final
    u_rest["final_norm_w"] = tuple(a.reshape(d) for a in u_rest["final_norm_w"])

    names = ("ada_w", "ada_b", "norm1_w", "w_in", "attn_sinks", "sgu_ln_w", "sgu_ln_b", "sgu_w", "sgu_b", "proj_a", "proj_b",
             "w_out", "norm2_w", "ffn_w_gate", "ffn_w_up", "ffn_conv_w", "ffn_conv_b", "ffn_w_down", "final_norm_w")
    alias = {"ffn_w_gate": "w_gate", "ffn_w_up": "w_up", "ffn_w_down": "w_down"}
    grad_of = lambda n: g_rest[n] if n in g_rest else g_big[alias.get(n, n)]
    upd_of = lambda n: u_rest[n] if n in u_rest else upd[alias.get(n, n)]
    return (loss, dx[None], *[grad_of(n) for n in names], *[upd_of(n)[0] for n in names],
            *[upd_of(n)[1] for n in names], *[upd_of(n)[2] for n in names])
```

```python
import jax
import jax.numpy as jnp
from jax import lax
from jax.experimental import pallas as pl
from jax.experimental.pallas import tpu as pltpu

F32 = jnp.float32
BF = jnp.bfloat16

D_MODEL = 1024
N_Q_HEADS = 16
N_KV_HEADS = 2
HEAD_DIM = 64
ATTN_BLOCK = 128
ROPE_THETA = 500000.0
ROT_DIM = HEAD_DIM // 4
SGU_WIDTH = 1024
SGU_GROUPS = 8
SGU_CHUNK = 128
FFN_DIM = 2816
NORM_EPS = 1e-6
DEPTH = 2
IN_COLS = 5376
N_CHIPS = 4
FFN_SHARD = FFN_DIM // N_CHIPS
CHIPS_PER_CHUNK = 2
FFN_CHUNK = CHIPS_PER_CHUNK * FFN_SHARD
FFN_CHUNKS = FFN_DIM // FFN_CHUNK
LANES = 128
SUBLANES = 8
HALO = 16
VMEM_LIMIT = 56 * 1024 * 1024
NEG_BIG = -1e30

ADAM_LR = 0.001
ADAM_B1 = 0.9
ADAM_B2 = 0.999
ADAM_EPS = 1e-08
ADAM_WD = 0.01
ADAM_STEP = 10

MESH = pl.DeviceIdType.MESH

Q_END = 1024
KV_END = 1280
U_END = 2304
Z_END = 3328
GA_END = 4352


def _sds(shape, dtype):
    return jax.ShapeDtypeStruct(tuple(shape), dtype)


def _call(body, name, out_shape, grid, in_specs, out_specs, semantics, scratch=(), after=None):
    n_in = len(in_specs)
    fn = body
    if after is not None:
        def fn(*refs):
            return body(*refs[:n_in], *refs[n_in + 1:])

        in_specs = list(in_specs) + [pl.BlockSpec(memory_space=pl.ANY)]
    call = pl.pallas_call(
        fn, name=name, out_shape=out_shape, grid=grid, in_specs=in_specs, out_specs=out_specs,
        scratch_shapes=scratch,
        compiler_params=pltpu.CompilerParams(dimension_semantics=semantics, vmem_limit_bytes=VMEM_LIMIT))
    if after is None:
        return call
    return lambda *args: call(*args, after)


def _rows(tm, width, col=0):
    return pl.BlockSpec((tm, width), lambda i: (i, col))


def _vec(width):
    return pl.BlockSpec((1, width), lambda i: (0, 0))


def _resident(shape):
    zeros = (0,) * len(shape)
    return pl.BlockSpec(tuple(shape), lambda *_: zeros, pipeline_mode=pl.Buffered(1))


def _sigmoid(x):
    return 0.5 + 0.5 * jnp.tanh(0.5 * x)


def _gelu(x):
    return 0.5 * x * (1.0 + lax.erf(x * 0.7071067811865476))


def _gelu_and_grad(x):
    cdf = 0.5 * (1.0 + lax.erf(x * 0.7071067811865476))
    return x * cdf, cdf + x * jnp.exp(-0.5 * x * x) * 0.3989422804014327


def _dot(a, b):
    return jnp.dot(a, b, preferred_element_type=F32)


def _dot_nt(a, b):
    return lax.dot_general(a, b, (((1,), (1,)), ((), ())), preferred_element_type=F32)


def _dot_tn(a, b):
    return lax.dot_general(a, b, (((0,), (0,)), ((), ())), preferred_element_type=F32)


def _rms(xv):
    return lax.rsqrt(jnp.mean(xv * xv, axis=-1, keepdims=True) + NORM_EPS)


def _matmul_tn(name, a, b, tk=512, tn=1024, after=None):
    s, k = a.shape
    n = b.shape[1]
    tk, tn = min(tk, k), min(tn, n)

    def body(a_ref, b_ref, o_ref):
        o_ref[...] = _dot_tn(a_ref[...], b_ref[...]).astype(o_ref.dtype)

    return _call(body, name, _sds((k, n), BF), (k // tk, n // tn),
                 [pl.BlockSpec((s, tk), lambda i, j: (0, i)), pl.BlockSpec((s, tn), lambda i, j: (0, j))],
                 pl.BlockSpec((tk, tn), lambda i, j: (i, j)), ("parallel", "parallel"), after=after)(a, b)


def _matmul_tn_segments(name, segments, b, tk=256):
    s, n = b.shape
    counts = [a.shape[1] // tk for a in segments]
    firsts = [sum(counts[:t]) for t in range(len(segments))]

    def body(*refs):
        a_refs, b_ref, o_ref = refs[:len(segments)], refs[len(segments)], refs[-1]
        i = pl.program_id(0)
        for a_ref, first, count in zip(a_refs, firsts, counts):
            @pl.when((i >= first) & (i < first + count))
            def _(a_ref=a_ref):
                o_ref[...] = _dot_tn(a_ref[...], b_ref[...]).astype(BF)

    specs = [pl.BlockSpec((s, tk), lambda i, first=first, count=count: (0, jnp.clip(i - first, 0, count - 1)))
             for first, count in zip(firsts, counts)]
    return _call(body, name, _sds((sum(counts) * tk, n), BF), (sum(counts),), specs + [_resident(b.shape)],
                 pl.BlockSpec((tk, n), lambda i: (i, 0)), ("arbitrary",))(*segments, b)


def _rope_partner(v):
    lane = lax.broadcasted_iota(jnp.int32, (1, LANES), 1) % HEAD_DIM
    return jnp.where(lane < ROT_DIM // 2, pltpu.roll(v, LANES - ROT_DIM // 2, axis=1), pltpu.roll(v, ROT_DIM // 2, axis=1))


def _dup_half(v, half):
    lane = lax.broadcasted_iota(jnp.int32, (1, LANES), 1)
    keep = jnp.where((lane >= HEAD_DIM) == (half == 1), v, 0.0)
    return keep + pltpu.roll(keep, HEAD_DIM, axis=1)


def _in_proj(name, x, w, sc, sh, w_in, cosf, sinf, tm=512, after=None):
    s, d = x.shape
    tm = min(tm, s)

    def body(x_ref, w_ref, sc_ref, sh_ref, win_ref, cos_ref, sin_ref,
             h_ref, qr_ref, kk0_ref, kk1_ref, vv0_ref, vv1_ref, u_ref, v_ref, ga_ref, gb_ref):
        xv = x_ref[...]
        h = ((xv * _rms(xv)) * w_ref[...] * (1.0 + sc_ref[...]) + sh_ref[...]).astype(BF)
        h_ref[...] = h
        cosv, sinv = cos_ref[...], sin_ref[...]
        q = _dot_nt(h, win_ref[:Q_END, :])
        for j in range(D_MODEL // LANES):
            qv = q[:, j * LANES:(j + 1) * LANES]
            qr_ref[:, j * LANES:(j + 1) * LANES] = ((qv * cosv + _rope_partner(qv) * sinv) * ATTN_SCALE).astype(BF)
        kv = _dot_nt(h, win_ref[Q_END:KV_END, :])
        kr = kv[:, :LANES] * cosv + _rope_partner(kv[:, :LANES]) * sinv
        vv = kv[:, LANES:]
        kk0_ref[...] = _dup_half(kr, 0).astype(BF)
        kk1_ref[...] = _dup_half(kr, 1).astype(BF)
        vv0_ref[...] = _dup_half(vv, 0).astype(BF)
        vv1_ref[...] = _dup_half(vv, 1).astype(BF)
        u_ref[...] = _dot_nt(h, win_ref[KV_END:U_END, :])
        v_ref[...] = _dot_nt(h, win_ref[U_END:Z_END, :])
        ga_ref[...] = _dot_nt(h, win_ref[Z_END:GA_END, :]).astype(BF)
        gb_ref[...] = _dot_nt(h, win_ref[GA_END:, :]).astype(BF)

    wide, kvs, pre = _sds((s, d), BF), _sds((s, LANES), BF), _sds((s, d), F32)
    return _call(body, name, (wide, wide, kvs, kvs, kvs, kvs, pre, pre, wide, wide), (s // tm,),
                 [_rows(tm, d), _vec(d), _vec(d), _vec(d), _resident(w_in.shape), _rows(tm, LANES), _rows(tm, LANES)],
                 (_rows(tm, d), _rows(tm, d)) + (_rows(tm, LANES),) * 4 + (_rows(tm, d),) * 4, ("parallel",), after=after)(
                     x, w, sc, sh, w_in, cosf, sinf)


def _in_proj_bwd(name, dq, dkv, du, dv, dga, dgb, w_in, x, w, sc, dx_in, tm=512):
    s, d = x.shape
    tm = min(tm, s)

    def body(dq_ref, dkv_ref, du_ref, dv_ref, dga_ref, dgb_ref, win_ref, x_ref, w_ref, sc_ref, dxin_ref,
             dx_ref, da_ref, dsh_ref):
        @pl.when(pl.program_id(0) == 0)
        def _():
            da_ref[...] = jnp.zeros_like(da_ref)
            dsh_ref[...] = jnp.zeros_like(dsh_ref)

        dh = (_dot(dq_ref[...], win_ref[:Q_END, :]) + _dot(dkv_ref[...], win_ref[Q_END:KV_END, :])
              + _dot(du_ref[...], win_ref[KV_END:U_END, :]) + _dot(dv_ref[...], win_ref[U_END:Z_END, :])
              + _dot(dga_ref[...], win_ref[Z_END:GA_END, :]) + _dot(dgb_ref[...], win_ref[GA_END:, :]))
        xv = x_ref[...]
        r = _rms(xv)
        xn = xv * r
        dxn = dh * (w_ref[...] * (1.0 + sc_ref[...]))
        dx_ref[...] = dxin_ref[...] + r * (dxn - xn * jnp.mean(dxn * xn, axis=-1, keepdims=True))
        da_ref[...] += jnp.sum(dh * xn, axis=0, keepdims=True)
        dsh_ref[...] += jnp.sum(dh, axis=0, keepdims=True)

    return _call(body, name, (_sds((s, d), F32), _sds((1, d), F32), _sds((1, d), F32)), (s // tm,),
                 [_rows(tm, d), _rows(tm, 2 * LANES), _rows(tm, d), _rows(tm, d), _rows(tm, d), _rows(tm, d),
                  _resident(w_in.shape), _rows(tm, d), _vec(d), _vec(d), _rows(tm, d)],
                 (_rows(tm, d), _vec(d), _vec(d)), ("arbitrary",))(dq, dkv, du, dv, dga, dgb, w_in, x, w, sc, dx_in)


def _rope_bwd(name, dqr, dkv_cur, dkv_prev, cosf, sinf, tm=512):
    s = dqr.shape[0]
    tm = min(tm, s)
    steps = s // tm
    per = tm // ATTN_BLOCK
    nb = s // ATTN_BLOCK

    def unrope(v, cosv, sinv):
        return v * cosv - _rope_partner(v) * sinv

    def body(dq_ref, cur_ref, prev_ref, next_ref, cos_ref, sin_ref, dqo_ref, dkvo_ref):
        i = pl.program_id(0)
        cosv, sinv = cos_ref[...], sin_ref[...]
        for j in range(D_MODEL // LANES):
            dqo_ref[:, j * LANES:(j + 1) * LANES] = unrope(dq_ref[:, j * LANES:(j + 1) * LANES], cosv, sinv).astype(BF)
        nxt = jnp.where(i < steps - 1, next_ref[...], 0.0)
        if per > 1:
            shifted = jnp.concatenate([prev_ref[ATTN_BLOCK:, :], nxt], axis=0)
        else:
            shifted = nxt
        tot = cur_ref[...] + shifted
        dkvo_ref[:, :LANES] = unrope(tot[:, :LANES], cosv, sinv).astype(BF)
        dkvo_ref[:, LANES:] = tot[:, LANES:].astype(BF)

    nxt_spec = pl.BlockSpec((ATTN_BLOCK, 2 * LANES), lambda i: (jnp.minimum((i + 1) * per, nb - 1), 0))
    return _call(body, name, (_sds((s, D_MODEL), BF), _sds((s, 2 * LANES), BF)), (steps,),
                 [_rows(tm, D_MODEL), _rows(tm, 2 * LANES), _rows(tm, 2 * LANES), nxt_spec, _rows(tm, LANES),
                  _rows(tm, LANES)],
                 (_rows(tm, D_MODEL), _rows(tm, 2 * LANES)), ("parallel",))(dqr, dkv_cur, dkv_prev, dkv_prev, cosf, sinf)


Q_PER_KV = N_Q_HEADS // N_KV_HEADS
ATTN_SCALE = HEAD_DIM ** -0.5
HEADS_AHEAD_FWD = 2
HEADS_AHEAD_BWD = 3


def _band_mask_t(n):
    kj = lax.broadcasted_iota(jnp.int32, (2 * ATTN_BLOCK, ATTN_BLOCK), 0)
    qi = lax.broadcasted_iota(jnp.int32, (2 * ATTN_BLOCK, ATTN_BLOCK), 1)
    return (kj > qi) & (kj <= qi + ATTN_BLOCK) & ((n > 0) | (kj >= ATTN_BLOCK))


def _softmax_t(raw, allowed, sink):
    sc = jnp.where(allowed, raw, NEG_BIG)
    m = jnp.maximum(jnp.max(sc, axis=0, keepdims=True), sink)
    p = jnp.exp(sc - m)
    esink = jnp.exp(sink - m)
    inv = 1.0 / (jnp.sum(p, axis=0, keepdims=True) + esink)
    return p * inv, esink * inv


def _kv_specs():
    cur = pl.BlockSpec((ATTN_BLOCK, LANES), lambda n: (n, 0))
    prev = pl.BlockSpec((ATTN_BLOCK, LANES), lambda n: (jnp.maximum(n - 1, 0), 0))
    return [prev, cur] * 4


def _attention(name, qr, kk0, kk1, vv0, vv1, sinks):
    s = qr.shape[0]
    nb = s // ATTN_BLOCK

    def body(sink_ref, q_ref, k0p, k0c, k1p, k1c, v0p, v0c, v1p, v1c, y_ref):
        allowed = _band_mask_t(pl.program_id(0))
        upper = lax.broadcasted_iota(jnp.int32, (1, LANES), 1) >= HEAD_DIM
        upper_rows = lax.broadcasted_iota(jnp.int32, (LANES, 1), 0) >= HEAD_DIM
        bands = ((jnp.concatenate([k0p[...], k0c[...]], axis=0), jnp.concatenate([v0p[...], v0c[...]], axis=0)),
                 (jnp.concatenate([k1p[...], k1c[...]], axis=0), jnp.concatenate([v1p[...], v1c[...]], axis=0)))
        vbts = (bands[0][1].T, bands[1][1].T)

        def scores(h):
            hk, j, half = h // Q_PER_KV, (h % Q_PER_KV) // 2, h % 2
            col = (hk * 4 + j) * LANES
            qp = q_ref[:, col:col + LANES]
            return _dot_nt(bands[hk][0], jnp.where(upper if half else jnp.logical_not(upper), qp, jnp.zeros_like(qp)))

        out_t = None
        ahead = [scores(h) for h in range(HEADS_AHEAD_FWD)]
        for h in range(N_Q_HEADS):
            hk, j, half = h // Q_PER_KV, (h % Q_PER_KV) // 2, h % 2
            raw = ahead.pop(0)
            if h + HEADS_AHEAD_FWD < N_Q_HEADS:
                ahead.append(scores(h + HEADS_AHEAD_FWD))
            pn, _ = _softmax_t(raw, allowed, sink_ref[h])
            o_h = _dot(vbts[hk], pn.astype(BF))
            out_t = jnp.where(upper_rows, o_h, out_t) if half else o_h
            if half:
                col = (hk * 4 + j) * LANES
                y_ref[:, col:col + LANES] = out_t.T.astype(BF)

    return _call(body, name, _sds((s, D_MODEL), BF), (nb,),
                 [pl.BlockSpec(memory_space=pltpu.SMEM), pl.BlockSpec((ATTN_BLOCK, D_MODEL), lambda n: (n, 0))] + _kv_specs(),
                 pl.BlockSpec((ATTN_BLOCK, D_MODEL), lambda n: (n, 0)), ("parallel",))(
                     sinks, qr, kk0, kk0, kk1, kk1, vv0, vv0, vv1, vv1)


def _attention_bwd(name, qr, kk0, kk1, vv0, vv1, sinks, dy, after=None):
    s = qr.shape[0]
    nb = s // ATTN_BLOCK

    def body(sink_ref, q_ref, dy_ref, k0p, k0c, k1p, k1c, v0p, v0c, v1p, v1c, dq_ref, cur_ref, prev_ref, dsink_ref):
        @pl.when(pl.program_id(0) == 0)
        def _():
            dsink_ref[...] = jnp.zeros_like(dsink_ref)

        allowed = _band_mask_t(pl.program_id(0))
        lane = lax.broadcasted_iota(jnp.int32, (1, LANES), 1)
        upper = lane >= HEAD_DIM
        upper_rows = lax.broadcasted_iota(jnp.int32, (LANES, 1), 0) >= HEAD_DIM
        bands = ((jnp.concatenate([k0p[...], k0c[...]], axis=0), jnp.concatenate([v0p[...], v0c[...]], axis=0)),
                 (jnp.concatenate([k1p[...], k1c[...]], axis=0), jnp.concatenate([v1p[...], v1c[...]], axis=0)))
        kbts = (bands[0][0].T, bands[1][0].T)

        def scores(h):
            hk, j, half = h // Q_PER_KV, (h % Q_PER_KV) // 2, h % 2
            kb, vb = bands[hk]
            col = (hk * 4 + j) * LANES
            sel = upper if half else jnp.logical_not(upper)
            qp = q_ref[:, col:col + LANES]
            qa = jnp.where(sel, qp, jnp.zeros_like(qp))
            dya = jnp.where(sel, dy_ref[:, col:col + LANES], 0.0).astype(BF)
            return qa, dya, _dot_nt(kb, qa), _dot_nt(vb, dya)

        dsink = jnp.zeros((1, LANES), F32)
        dk_slab = jnp.zeros((2 * ATTN_BLOCK, LANES), F32)
        dv_slab = jnp.zeros((2 * ATTN_BLOCK, LANES), F32)
        dkk = dvv = dq_t = None
        ahead = [scores(h) for h in range(HEADS_AHEAD_BWD)]
        for h in range(N_Q_HEADS):
            hk, j, half = h // Q_PER_KV, (h % Q_PER_KV) // 2, h % 2
            qa, dya, raw, dp = ahead.pop(0)
            if h + HEADS_AHEAD_BWD < N_Q_HEADS:
                ahead.append(scores(h + HEADS_AHEAD_BWD))
            pn, psink = _softmax_t(raw, allowed, sink_ref[h])
            delta = jnp.sum(pn * dp, axis=0, keepdims=True)
            ds = (pn * (dp - delta)).astype(BF)
            dsink = dsink + jnp.where(lane == h, -jnp.sum(psink * delta), 0.0)
            dq_h = _dot(kbts[hk], ds) * ATTN_SCALE
            dq_t = jnp.where(upper_rows, dq_h, dq_t) if half else dq_h
            dk_h, dv_h = _dot(ds, qa), _dot(pn.astype(BF), dya)
            dkk, dvv = (dk_h, dv_h) if h % Q_PER_KV == 0 else (dkk + dk_h, dvv + dv_h)
            if half:
                col = (hk * 4 + j) * LANES
                dq_ref[:, col:col + LANES] = dq_t.T
            if h % Q_PER_KV == Q_PER_KV - 1:
                mine = upper if hk else jnp.logical_not(upper)
                dk_slab = jnp.where(mine, dkk + pltpu.roll(dkk, HEAD_DIM, axis=1), dk_slab)
                dv_slab = jnp.where(mine, dvv + pltpu.roll(dvv, HEAD_DIM, axis=1), dv_slab)
        prev_ref[:, :LANES] = dk_slab[:ATTN_BLOCK]
        prev_ref[:, LANES:] = dv_slab[:ATTN_BLOCK]
        cur_ref[:, :LANES] = dk_slab[ATTN_BLOCK:]
        cur_ref[:, LANES:] = dv_slab[ATTN_BLOCK:]
        dsink_ref[...] += dsink

    blk = pl.BlockSpec((ATTN_BLOCK, D_MODEL), lambda n: (n, 0))
    kvo = pl.BlockSpec((ATTN_BLOCK, 2 * LANES), lambda n: (n, 0))
    return _call(body, name,
                 (_sds((s, D_MODEL), F32), _sds((s, 2 * LANES), F32), _sds((s, 2 * LANES), F32), _sds((1, LANES), F32)),
                 (nb,), [pl.BlockSpec(memory_space=pltpu.SMEM), blk, blk] + _kv_specs(),
                 (blk, kvo, kvo, pl.BlockSpec((1, LANES), lambda n: (0, 0))), ("arbitrary",), after=after)(
                     sinks, qr, dy, kk0, kk0, kk1, kk1, vv0, vv0, vv1, vv1)


def _sgu_weights(wm_ref, g):
    t = lax.broadcasted_iota(jnp.int32, (SGU_CHUNK, SGU_CHUNK), 0)
    sidx = lax.broadcasted_iota(jnp.int32, (SGU_CHUNK, SGU_CHUNK), 1)
    return jnp.where(sidx <= t, wm_ref[g], 0.0).astype(BF)


def _layer_norm_stats(v):
    mu = jnp.mean(v, axis=-1, keepdims=True)
    cen = v - mu
    rstd = lax.rsqrt(jnp.mean(cen * cen, axis=-1, keepdims=True) + NORM_EPS)
    return cen * rstd, rstd


def _sgu(name, u_pre, v_pre, ln_w, ln_b, wm, bfull, tm=512):
    s, w = u_pre.shape
    tm = min(tm, s)

    def body(u_ref, v_ref, lw_ref, lb_ref, wm_ref, b_ref, y_ref):
        vhat, _ = _layer_norm_stats(_gelu(v_ref[...]))
        vn = (vhat * lw_ref[...] + lb_ref[...]).astype(BF)
        for g in range(SGU_GROUPS):
            wg = _sgu_weights(wm_ref, g)
            cols = slice(g * SGU_CHUNK, (g + 1) * SGU_CHUNK)
            for ch in range(tm // SGU_CHUNK):
                rows = slice(ch * SGU_CHUNK, (ch + 1) * SGU_CHUNK)
                f = _dot(wg, vn[rows, cols]) + b_ref[g]
                y_ref[rows, cols] = (_gelu(u_ref[rows, cols]) * f).astype(BF)

    full3 = pl.BlockSpec((SGU_GROUPS, SGU_CHUNK, SGU_CHUNK), lambda i: (0, 0, 0))
    return _call(body, name, _sds((s, w), BF), (s // tm,),
                 [_rows(tm, w), _rows(tm, w), _vec(w), _vec(w), full3, full3],
                 _rows(tm, w), ("parallel",))(u_pre, v_pre, ln_w, ln_b, wm, bfull)


def _sgu_bwd(name, u_pre, v_pre, ln_w, ln_b, wm, bfull, dy, tm=512, after=None):
    s, w = u_pre.shape
    tm = min(tm, s)
    steps = s // tm

    def body(u_ref, v_ref, lw_ref, lb_ref, wm_ref, b_ref, dy_ref, du_ref, dv_ref, dwm_ref, db_ref, dlw_ref, dlb_ref,
             dfsum_ref):
        i = pl.program_id(0)

        @pl.when(i == 0)
        def _():
            dwm_ref[...] = jnp.zeros_like(dwm_ref)
            dlw_ref[...] = jnp.zeros_like(dlw_ref)
            dlb_ref[...] = jnp.zeros_like(dlb_ref)
            dfsum_ref[...] = jnp.zeros_like(dfsum_ref)

        vpre = v_ref[...]
        vg, dvg_dv = _gelu_and_grad(vpre)
        vhat, rstd = _layer_norm_stats(vg)
        vn = (vhat * lw_ref[...] + lb_ref[...]).astype(BF)
        t = lax.broadcasted_iota(jnp.int32, (SGU_CHUNK, SGU_CHUNK), 0)
        sidx = lax.broadcasted_iota(jnp.int32, (SGU_CHUNK, SGU_CHUNK), 1)
        dvn_cols = []
        for g in range(SGU_GROUPS):
            wg = _sgu_weights(wm_ref, g)
            cols = slice(g * SGU_CHUNK, (g + 1) * SGU_CHUNK)
            dvn_rows = []
            dwg = jnp.zeros((SGU_CHUNK, SGU_CHUNK), F32)
            dfs = jnp.zeros((SGU_CHUNK, SGU_CHUNK), F32)
            for ch in range(tm // SGU_CHUNK):
                rows = slice(ch * SGU_CHUNK, (ch + 1) * SGU_CHUNK)
                upre = u_ref[rows, cols]
                dyv = dy_ref[rows, cols].astype(F32)
                f = _dot(wg, vn[rows, cols]) + b_ref[g]
                ug, dug_du = _gelu_and_grad(upre)
                du_ref[rows, cols] = (dyv * f * dug_du).astype(BF)
                df = dyv * ug
                dfb = df.astype(BF)
                dvn_rows.append(_dot_tn(wg, dfb))
                dwg = dwg + _dot_nt(dfb, vn[rows, cols])
                dfs = dfs + df
            dwm_ref[g] += jnp.where(sidx <= t, dwg, 0.0)
            dfsum_ref[g] += dfs
            dvn_cols.append(jnp.concatenate(dvn_rows, axis=0) if len(dvn_rows) > 1 else dvn_rows[0])
        dvn = jnp.concatenate(dvn_cols, axis=1)
        dlw_ref[...] += jnp.sum(dvn * vhat, axis=0, keepdims=True)
        dlb_ref[...] += jnp.sum(dvn, axis=0, keepdims=True)
        dvh = dvn * lw_ref[...]
        dvg = rstd * (dvh - jnp.mean(dvh, axis=-1, keepdims=True) - vhat * jnp.mean(dvh * vhat, axis=-1, keepdims=True))
        dv_ref[...] = (dvg * dvg_dv).astype(BF)

        @pl.when(i == steps - 1)
        def _():
            for g in range(SGU_GROUPS):
                db_ref[g:g + 1, :] = jnp.sum(dfsum_ref[g].T, axis=0, keepdims=True)

    full3 = pl.BlockSpec((SGU_GROUPS, SGU_CHUNK, SGU_CHUNK), lambda i: (0, 0, 0))
    return _call(body, name,
                 (_sds((s, w), BF), _sds((s, w), BF), _sds((SGU_GROUPS, SGU_CHUNK, SGU_CHUNK), F32),
                  _sds((SGU_GROUPS, SGU_CHUNK), F32), _sds((1, w), F32), _sds((1, w), F32)),
                 (steps,),
                 [_rows(tm, w), _rows(tm, w), _vec(w), _vec(w), full3, full3, _rows(tm, w)],
                 (_rows(tm, w), _rows(tm, w), full3, pl.BlockSpec((SGU_GROUPS, SGU_CHUNK), lambda i: (0, 0)), _vec(w), _vec(w)),
                 ("arbitrary",), scratch=[pltpu.VMEM((SGU_GROUPS, SGU_CHUNK, SGU_CHUNK), F32)], after=after)(
                     u_pre, v_pre, ln_w, ln_b, wm, bfull, dy)


def _mix_out(name, y_sgu, y_attn, ga_pre, gb_pre, x, g1, proj_a, proj_b, w_out, w2, sc2, sh2, tm=512):
    s, d = x.shape
    tm = min(tm, s)

    def body(ys_ref, ya_ref, ga_ref, gb_ref, x_ref, g1_ref, wa_ref, wb_ref, wo_ref, w2_ref, sc2_ref, sh2_ref,
             m_ref, pa_ref, pb_ref, o_ref, x1_ref, h2_ref):
        pa = _dot(ys_ref[...], wa_ref[...].reshape(d, d))
        pb = _dot(ya_ref[...], wb_ref[...].reshape(d, d))
        pa_ref[...] = pa.astype(BF)
        pb_ref[...] = pb.astype(BF)
        merged = (_sigmoid(ga_ref[...].astype(F32)) * pa + _sigmoid(gb_ref[...].astype(F32)) * pb).astype(BF)
        m_ref[...] = merged
        o = _dot(merged, wo_ref[...].reshape(d, d))
        o_ref[...] = o.astype(BF)
        x1 = x_ref[...] + g1_ref[...] * o
        x1_ref[...] = x1
        h2_ref[...] = ((x1 * _rms(x1)) * w2_ref[...] * (1.0 + sc2_ref[...]) + sh2_ref[...]).astype(BF)

    f, b = _sds((s, d), F32), _sds((s, d), BF)
    r = _rows(tm, d)
    wspec = _resident(proj_a.shape)
    return _call(body, name, (b, b, b, b, f, b), (s // tm,),
                 [r, r, r, r, r, _vec(d), wspec, wspec, wspec, _vec(d), _vec(d), _vec(d)], (r,) * 6, ("parallel",))(
                     y_sgu, y_attn, ga_pre, gb_pre, x, g1, proj_a, proj_b, w_out, w2, sc2, sh2)


def _mix_bwd(name, do, w_out, proj_a, proj_b, ga_pre, gb_pre, pa, pb, tm=512):
    s, d = do.shape
    tm = min(tm, s)

    def body(do_ref, wo_ref, wa_ref, wb_ref, ga_ref, gb_ref, pa_ref, pb_ref,
             dpa_ref, dpb_ref, dga_ref, dgb_ref, dys_ref, dya_ref):
        dm = _dot_nt(do_ref[...], wo_ref[...].reshape(d, d))
        ga = _sigmoid(ga_ref[...].astype(F32))
        gb = _sigmoid(gb_ref[...].astype(F32))
        dpa = (dm * ga).astype(BF)
        dpb = (dm * gb).astype(BF)
        dpa_ref[...] = dpa
        dpb_ref[...] = dpb
        dga_ref[...] = (dm * pa_ref[...].astype(F32) * ga * (1.0 - ga)).astype(BF)
        dgb_ref[...] = (dm * pb_ref[...].astype(F32) * gb * (1.0 - gb)).astype(BF)
        dys_ref[...] = _dot_nt(dpa, wa_ref[...].reshape(d, d)).astype(BF)
        dya_ref[...] = _dot_nt(dpb, wb_ref[...].reshape(d, d)).astype(BF)

    f, b = _sds((s, d), F32), _sds((s, d), BF)
    r = _rows(tm, d)
    wspec = _resident(w_out.shape)
    return _call(body, name, (b, b, b, b, b, b), (s // tm,), [r, wspec, wspec, wspec, r, r, r, r], (r,) * 6,
                 ("parallel",))(do, w_out, proj_a, proj_b, ga_pre, gb_pre, pa, pb)


def _ffn_up_act(name, h2, w_gate, w_up, cw, cb, tm=1024):
    s, d = h2.shape
    tm = min(tm, s)
    tc = FFN_CHUNK
    per = tm // HALO

    def body(h_ref, hprev_ref, wg_ref, wu_ref, cw_ref, cb_ref, a_ref, ac_ref, up_ref, hf_ref):
        hv = h_ref[...]
        wg = wg_ref[...].reshape(tc, d)
        a = _dot_nt(hv, wg).astype(BF)
        up = _dot_nt(hv, wu_ref[...].reshape(tc, d)).astype(BF)
        a_ref[...] = a
        up_ref[...] = up
        prev = jnp.where(pl.program_id(1) > 0, _dot_nt(hprev_ref[...], wg).astype(BF).astype(F32), 0.0)
        ext = jnp.concatenate([prev, a.astype(F32)], axis=0)
        ac = (cb_ref[...] + cw_ref[0:1, :] * pltpu.roll(ext, 2, axis=0) + cw_ref[1:2, :] * pltpu.roll(ext, 1, axis=0)
              + cw_ref[2:3, :] * ext)[HALO:]
        ac_ref[...] = ac.astype(BF)
        hf_ref[...] = (ac * _sigmoid(ac) * up.astype(F32)).astype(BF)

    wspec = pl.BlockSpec((CHIPS_PER_CHUNK, FFN_SHARD, d), lambda j, i: (j, 0, 0))
    ospec = pl.BlockSpec((tm, tc), lambda j, i: (i, j))
    o = _sds((s, FFN_DIM), BF)
    return _call(body, name, (o, o, o, o), (FFN_CHUNKS, s // tm),
                 [pl.BlockSpec((tm, d), lambda j, i: (i, 0)), pl.BlockSpec((HALO, d), lambda j, i: (jnp.maximum(i * per - 1, 0), 0)),
                  wspec, wspec, pl.BlockSpec((3, tc), lambda j, i: (0, j)), pl.BlockSpec((1, tc), lambda j, i: (0, j))],
                 (ospec, ospec, ospec, ospec), ("parallel", "parallel"))(h2, h2, w_gate, w_up, cw, cb)


def _ffn_down(name, hf, w_down, x1, g2, tm=512):
    s, d = x1.shape
    tm = min(tm, s)

    def body(hf_ref, wd_ref, x1_ref, g2_ref, dn_ref, x2_ref):
        dn = _dot(hf_ref[...], wd_ref[...].reshape(FFN_DIM, d))
        dn_ref[...] = dn.astype(BF)
        x2_ref[...] = x1_ref[...] + g2_ref[...] * dn

    return _call(body, name, (_sds((s, d), BF), _sds((s, d), F32)), (s // tm,),
                 [_rows(tm, FFN_DIM), _resident(w_down.shape), _rows(tm, d), _vec(d)],
                 (_rows(tm, d), _rows(tm, d)), ("parallel",))(hf, w_down, x1, g2)


def _ffn_down_bwd_act(name, dx2, dn, g2, w_down, a, ac, up, cw, tm=256, after=None):
    s, d = dx2.shape
    c = a.shape[1]
    tm = min(tm, s)
    tc = FFN_CHUNK
    per = tm // HALO
    steps = s // tm
    last = s // HALO - 1
    n = tm + HALO

    def body(dx_ref, dxnext_ref, dn_ref, g2_ref, wd_ref, a_ref, ac_ref, acnext_ref, up_ref, upnext_ref, cw_ref,
             ddn_ref, da_ref, dup_ref, dg_ref, dcw_ref, dcb_ref):
        i = pl.program_id(0)

        @pl.when(i == 0)
        def _():
            dg_ref[...] = jnp.zeros_like(dg_ref)
            dcw_ref[...] = jnp.zeros_like(dcw_ref)
            dcb_ref[...] = jnp.zeros_like(dcb_ref)

        dxv = dx_ref[...]
        ddn = (dxv * g2_ref[...]).astype(BF)
        ddn_ref[...] = ddn
        dg_ref[...] += jnp.sum(dxv * dn_ref[...].astype(F32), axis=0, keepdims=True)
        ddn_next = jnp.where(i < steps - 1, dxnext_ref[...] * g2_ref[...], 0.0).astype(BF)
        ddn_ext = jnp.concatenate([ddn, ddn_next], axis=0)
        for k in range(FFN_CHUNKS):
            cols = slice(k * tc, (k + 1) * tc)
            dh = _dot_nt(ddn_ext, wd_ref[k * CHIPS_PER_CHUNK:(k + 1) * CHIPS_PER_CHUNK].reshape(tc, d))
            ace = jnp.concatenate([ac_ref[:, cols].astype(F32), acnext_ref[:, cols].astype(F32)], axis=0)
            upe = jnp.concatenate([up_ref[:, cols].astype(F32), upnext_ref[:, cols].astype(F32)], axis=0)
            sig = _sigmoid(ace)
            silu = ace * sig
            dac = dh * upe * (sig + silu * (1.0 - sig))
            dup_ref[:, cols] = (dh[:tm] * silu[:tm]).astype(BF)
            d1 = pltpu.roll(dac, n - 1, axis=0)[:tm]
            d2 = pltpu.roll(dac, n - 2, axis=0)[:tm]
            d0 = dac[:tm]
            da_ref[:, cols] = (cw_ref[2:3, cols] * d0 + cw_ref[1:2, cols] * d1 + cw_ref[0:1, cols] * d2).astype(BF)
            a0 = a_ref[:, cols].astype(F32)
            dcb_ref[:, cols] += jnp.sum(d0, axis=0, keepdims=True)
            dcw_ref[0:1, cols] += jnp.sum(d2 * a0, axis=0, keepdims=True)
            dcw_ref[1:2, cols] += jnp.sum(d1 * a0, axis=0, keepdims=True)
            dcw_ref[2:3, cols] += jnp.sum(d0 * a0, axis=0, keepdims=True)

    nxt = lambda width: pl.BlockSpec((HALO, width), lambda i: (jnp.minimum((i + 1) * per, last), 0))
    wide = _sds((s, c), BF)
    return _call(body, name, (_sds((s, d), BF), wide, wide, _sds((1, d), F32), _sds((3, c), F32), _sds((1, c), F32)), (steps,),
                 [_rows(tm, d), nxt(d), _rows(tm, d), _vec(d), _resident(w_down.shape), _rows(tm, c), _rows(tm, c), nxt(c),
                  _rows(tm, c), nxt(c), pl.BlockSpec((3, c), lambda i: (0, 0))],
                 (_rows(tm, d), _rows(tm, c), _rows(tm, c), _vec(d), pl.BlockSpec((3, c), lambda i: (0, 0)), _vec(c)),
                 ("arbitrary",), after=after)(dx2, dx2, dn, g2, w_down, a, ac, ac, up, up, cw)


def _ffn_up_bwd(name, da, dup, w_gate, w_up, x1, dx2, w2, sc2, o, g1, tm=512, after=None):
    s, d = x1.shape
    tm = min(tm, s)

    def body(da_ref, dup_ref, wg_ref, wu_ref, x1_ref, dx2_ref, w2_ref, sc2_ref, o_ref, g1_ref,
             dx1_ref, do_ref, dnw_ref, dsh_ref, dg1_ref):
        @pl.when(pl.program_id(0) == 0)
        def _():
            dnw_ref[...] = jnp.zeros_like(dnw_ref)
            dsh_ref[...] = jnp.zeros_like(dsh_ref)
            dg1_ref[...] = jnp.zeros_like(dg1_ref)

        dh = _dot(da_ref[...], wg_ref[...].reshape(FFN_DIM, d)) + _dot(dup_ref[...], wu_ref[...].reshape(FFN_DIM, d))
        xv = x1_ref[...]
        r = _rms(xv)
        xn = xv * r
        dxn = dh * (w2_ref[...] * (1.0 + sc2_ref[...]))
        dx1 = dx2_ref[...] + r * (dxn - xn * jnp.mean(dxn * xn, axis=-1, keepdims=True))
        dx1_ref[...] = dx1
        dnw_ref[...] += jnp.sum(dh * xn, axis=0, keepdims=True)
        dsh_ref[...] += jnp.sum(dh, axis=0, keepdims=True)
        do_ref[...] = (dx1 * g1_ref[...]).astype(BF)
        dg1_ref[...] += jnp.sum(dx1 * o_ref[...].astype(F32), axis=0, keepdims=True)

    v = _sds((1, d), F32)
    r = _rows(tm, d)
    wspec = _resident(w_gate.shape)
    return _call(body, name, (_sds((s, d), F32), _sds((s, d), BF), v, v, v), (s // tm,),
                 [_rows(tm, FFN_DIM), _rows(tm, FFN_DIM), wspec, wspec, r, r, _vec(d), _vec(d), r, _vec(d)],
                 (r, r, _vec(d), _vec(d), _vec(d)), ("arbitrary",), after=after)(da, dup, w_gate, w_up, x1, dx2, w2, sc2, o, g1)


def _loss_head(name, x, w, target, tm=512):
    s, d = x.shape
    tm = min(tm, s)

    def body(x_ref, w_ref, t_ref, dx_ref, loss_ref, dw_ref):
        @pl.when(pl.program_id(0) == 0)
        def _():
            loss_ref[...] = jnp.zeros_like(loss_ref)
            dw_ref[...] = jnp.zeros_like(dw_ref)

        xv = x_ref[...]
        r = _rms(xv)
        xn = xv * r
        err = xn * w_ref[...] - t_ref[...]
        loss_ref[...] += 0.5 * jnp.sum(jnp.mean(err * err, axis=-1, keepdims=True))
        dy = err * (1.0 / d)
        dw_ref[...] += jnp.sum(dy * xn, axis=0, keepdims=True)
        dxn = dy * w_ref[...]
        dx_ref[...] = r * (dxn - xn * jnp.mean(dxn * xn, axis=-1, keepdims=True))

    return _call(body, name, (_sds((s, d), F32), _sds((1, LANES), F32), _sds((1, d), F32)), (s // tm,),
                 [_rows(tm, d), _vec(d), _rows(tm, d)], (_rows(tm, d), _vec(LANES), _vec(d)), ("arbitrary",))(x, w, target)


def _layer_fwd(l, x, mod, p, cosf, sinf, after=None, late=None, later=None):
    sh1, sc1, g1, sh2, sc2, g2 = mod
    tag = f"l{l}_"
    h, qr, kk0, kk1, vv0, vv1, u_pre, v_pre, ga_pre, gb_pre = _in_proj(
        tag + "in_proj", x, p["norm1_w"], sc1, sh1, p["w_in"], cosf, sinf, after=after)
    y_attn = _attention(tag + "attn", qr, kk0, kk1, vv0, vv1, p["sinks"])
    y_sgu = _sgu(tag + "sgu", u_pre, v_pre, p["sgu_ln_w"], p["sgu_ln_b"], p["sgu_w"], p["sgu_bfull"])
    if late is not None:
        p = dict(p, **late(y_sgu))
    merged, pa, pb, o, x1, h2 = _mix_out(tag + "mix_out", y_sgu, y_attn, ga_pre, gb_pre, x, g1, p["proj_a"], p["proj_b"],
                                         p["w_out"], p["norm2_w"], sc2, sh2)
    if later is not None:
        p = dict(p, **later(h2))
    a, ac, up, hf = _ffn_up_act(tag + "ffn_up", h2, p["w_gate"], p["w_up"], p["conv_w"], p["conv_b"])
    dn, x2 = _ffn_down(tag + "ffn_down", hf, p["w_down"], x1, g2)
    saved = dict(x=x, h=h, qr=qr, kk0=kk0, kk1=kk1, vv0=vv0, vv1=vv1, u_pre=u_pre, v_pre=v_pre, ga_pre=ga_pre,
                 gb_pre=gb_pre, y_attn=y_attn, y_sgu=y_sgu, merged=merged, pa=pa, pb=pb, o=o, x1=x1, h2=h2, a=a, ac=ac, up=up,
                 hf=hf, dn=dn)
    return x2, saved, p


def _layer_bwd(l, dx2, mod, p, sv, cosf, sinf, after=None, emit=None, tick=None):
    sh1, sc1, g1, sh2, sc2, g2 = mod
    tag = f"l{l}_b_"
    d = D_MODEL
    g = {}
    ready = (lambda names: emit({k: g.pop(k) for k in names})) if emit else (lambda names: None)
    tick = tick or (lambda y: None)
    ddn, da, dup, dg2, g["conv_w"], g["conv_b"] = _ffn_down_bwd_act(
        tag + "ffn_down", dx2, sv["dn"], g2, p["w_down"], sv["a"], sv["ac"], sv["up"], p["conv_w"], after=after)
    g["w_down"] = _matmul_tn(tag + "dw_down", sv["hf"], ddn, tk=FFN_CHUNK, after=tick(ddn)).reshape(N_CHIPS, FFN_SHARD, d)
    g["w_gate"] = _matmul_tn(tag + "dw_gate", da, sv["h2"], tk=FFN_CHUNK).reshape(N_CHIPS, FFN_SHARD, d)
    g["w_up"] = _matmul_tn(tag + "dw_up", dup, sv["h2"], tk=FFN_CHUNK).reshape(N_CHIPS, FFN_SHARD, d)
    dx1, do, da2, dsh2, dg1 = _ffn_up_bwd(tag + "ffn_up", da, dup, p["w_gate"], p["w_up"], sv["x1"], dx2, p["norm2_w"],
                                          sc2, sv["o"], g1, after=ready(("w_down", "w_gate", "w_up")))
    g["norm2_w"] = da2 * (1.0 + sc2)
    dsc2 = da2 * p["norm2_w"]
    g["w_out"] = _matmul_tn(tag + "dw_out", sv["merged"], do, after=tick(do)).reshape(N_CHIPS, d // N_CHIPS, d)
    dpa, dpb, dga, dgb, dy_sgu, dy_attn = _mix_bwd(tag + "mix", do, p["w_out"], p["proj_a"], p["proj_b"], sv["ga_pre"],
                                                  sv["gb_pre"], sv["pa"], sv["pb"])
    g["proj_a"] = _matmul_tn(tag + "dproj_a", sv["y_sgu"], dpa).reshape(N_CHIPS, d // N_CHIPS, d)
    g["proj_b"] = _matmul_tn(tag + "dproj_b", sv["y_attn"], dpb).reshape(N_CHIPS, d // N_CHIPS, d)
    du, dv, g["sgu_w"], g["sgu_b"], g["sgu_ln_w"], g["sgu_ln_b"] = _sgu_bwd(
        tag + "sgu", sv["u_pre"], sv["v_pre"], p["sgu_ln_w"], p["sgu_ln_b"], p["sgu_w"], p["sgu_bfull"], dy_sgu,
        after=ready(("w_out", "proj_a", "proj_b")))
    dqr, dkv_cur, dkv_prev, dsink = _attention_bwd(tag + "attn", sv["qr"], sv["kk0"], sv["kk1"], sv["vv0"], sv["vv1"],
                                                   p["sinks"], dy_attn, after=tick(du))
    g["sinks"] = dsink[0, :N_Q_HEADS]
    dq, dkv = _rope_bwd(tag + "rope", dqr, dkv_cur, dkv_prev, cosf, sinf)
    dw_in = _matmul_tn_segments(tag + "dw_in", [dq, dkv, du, dv, dga, dgb], sv["h"])
    g["w_in"] = dw_in.reshape(N_CHIPS, IN_COLS // N_CHIPS, d)
    dx, da1, dsh1 = _in_proj_bwd(tag + "in_proj", dq, dkv, du, dv, dga, dgb, p["w_in"], sv["x"], p["norm1_w"], sc1, dx1)
    g["norm1_w"] = da1 * (1.0 + sc1)
    dsc1 = da1 * p["norm1_w"]
    return dx, (dsh1, dsc1, dg1, dsh2, dsc2, dg2), g


def _pad_to(a, axis, size):
    pad = [(0, 0)] * a.ndim
    pad[axis] = (0, size - a.shape[axis])
    return jnp.pad(a, pad)


def _early_params(w_in, small):
    d = D_MODEL
    return dict(
        w_in=w_in.reshape(IN_COLS, d), norm1_w=small["norm1_w"].reshape(1, d), sinks=small["sinks"],
        sgu_ln_w=small["sgu_ln_w"].reshape(1, d), sgu_ln_b=small["sgu_ln_b"].reshape(1, d), sgu_w=small["sgu_w"],
        sgu_bfull=jnp.broadcast_to(small["sgu_b"][:, :, None], (SGU_GROUPS, SGU_CHUNK, SGU_CHUNK)))


def _mix_params(proj_a, proj_b, w_out, small):
    return dict(proj_a=proj_a, proj_b=proj_b, w_out=w_out, norm2_w=small["norm2_w"].reshape(1, D_MODEL))


def _ffn_params(w_gate, w_up, w_down, conv_w, small):
    return dict(
        w_gate=w_gate, w_up=w_up, w_down=w_down, conv_w=conv_w.transpose(1, 0, 2).reshape(3, FFN_DIM),
        conv_b=small["conv_b"].reshape(1, FFN_DIM))


def _layer_params(w_in, proj_a, proj_b, w_out, w_gate, w_up, w_down, conv_w, small):
    return dict(_early_params(w_in, small), **_mix_params(proj_a, proj_b, w_out, small),
                **_ffn_params(w_gate, w_up, w_down, conv_w, small))


def _conv_grads_natural(g):
    cw = g["conv_w"]
    cb = g["conv_b"].reshape(FFN_DIM)
    return cw, cb


def _rope_tables(positions):
    inv_freq = ROPE_THETA ** (-jnp.arange(0, ROT_DIM, 2, dtype=F32) / ROT_DIM)
    ang = positions.astype(F32)[:, None] * inv_freq
    cos, sin = jnp.cos(ang), jnp.sin(ang)
    s = positions.shape[0]
    rest = HEAD_DIM - ROT_DIM
    cos_head = jnp.concatenate([cos, cos, jnp.ones((s, rest), F32)], axis=1)
    sin_head = jnp.concatenate([-sin, sin, jnp.zeros((s, rest), F32)], axis=1)
    return jnp.tile(cos_head, (1, LANES // HEAD_DIM)), jnp.tile(sin_head, (1, LANES // HEAD_DIM))


ADA_ROWS = 16


def _ada_fwd(name, c_rows, ada_w, ada_b_cols, tn=512):
    depth, d, n = ada_w.shape

    def body(c_ref, w_ref, b_ref, o_ref):
        cv = c_ref[...]
        act = (cv * _sigmoid(cv)).astype(BF)
        o_ref[0] = _dot(act, w_ref[0].astype(BF)) + b_ref[0]

    return _call(body, name, _sds((depth, ADA_ROWS, n), F32), (depth, n // tn),
                 [pl.BlockSpec((ADA_ROWS, d), lambda l, j: (0, 0)), pl.BlockSpec((1, d, tn), lambda l, j: (l, 0, j)),
                  pl.BlockSpec((1, 1, tn), lambda l, j: (l, 0, j))],
                 pl.BlockSpec((1, ADA_ROWS, tn), lambda l, j: (l, 0, j)), ("parallel", "parallel"))(c_rows, ada_w, ada_b_cols)


def _ada_bwd(name, c_rows, dmod_cols, tn=512):
    depth, _, n = dmod_cols.shape
    d = c_rows.shape[1]

    def body(c_ref, dm_ref, o_ref):
        cv = c_ref[...]
        act = (cv * _sigmoid(cv)).astype(BF)
        o_ref[0] = _dot_tn(act, dm_ref[0].astype(BF))

    return _call(body, name, _sds((depth, d, n), F32), (depth, n // tn),
                 [pl.BlockSpec((ADA_ROWS, d), lambda l, j: (0, 0)), pl.BlockSpec((1, ADA_ROWS, tn), lambda l, j: (l, 0, j))],
                 pl.BlockSpec((1, d, tn), lambda l, j: (l, 0, j)), ("parallel", "parallel"))(c_rows, dmod_cols)


def _colsum(name, a):
    r, n = a.shape

    def body(a_ref, o_ref):
        o_ref[...] = jnp.sum(a_ref[...], axis=0, keepdims=True)

    return _call(body, name, _sds((1, n), F32), (1,), [pl.BlockSpec((r, n), lambda i: (0, 0))],
                 pl.BlockSpec((1, n), lambda i: (0, 0)), ("arbitrary",))(a)


REL_SIBLING = (0, 0, 1)
REL_CHIPS = ((1, 0, 0), (0, 1, 0), (1, 1, 0))
REL_ALL = tuple((fx, fy, fc) for fx in (0, 1) for fy in (0, 1) for fc in (0, 1) if fx or fy or fc)


def _chip_of(dev):
    return 2 * dev[0] + dev[1]


def _dev_of(dev):
    return 4 * dev[0] + 2 * dev[1] + dev[2]


def _flip(dev, rel):
    return tuple(1 - m if f else m for m, f in zip(dev, rel))


def _exchange(name, arrays, n_out, stages, aliases=None):
    out_shapes, stages = stages[0], stages[1:]
    n_in = len(arrays)
    aliases = aliases or {}
    n_remote = sum(len(plan) for plan, _ in stages)
    n_local = sum(len(local) for _, local in stages)

    def at(ref, idx):
        return ref.at[idx] if len(idx) else ref

    def body(*refs):
        bufs = list(refs[:n_in + n_out])
        for i_in, i_out in aliases.items():
            bufs[i_in] = bufs[n_in + i_out]
        send_sems, recv_sems, local_sems = refs[n_in + n_out:]
        me = (lax.axis_index("x"), lax.axis_index("y"), lax.axis_index("c"))
        base_r = base_l = 0
        pending = []
        for plan, local in stages:
            def remote(k, entry, sender, receiver):
                rel, si, ssel, di, dsel = entry
                return pltpu.make_async_remote_copy(
                    src_ref=at(bufs[si], ssel(sender, receiver)), dst_ref=at(bufs[di], dsel(sender, receiver)),
                    send_sem=send_sems.at[k], recv_sem=recv_sems.at[k], device_id=_flip(me, rel), device_id_type=MESH)

            sends = [remote(base_r + k, e, me, _flip(me, e[0])) for k, e in enumerate(plan)]
            for cp in sends:
                cp.start()
            for k, (si, ssel, di, dsel) in enumerate(local):
                cp = pltpu.make_async_copy(at(bufs[si], ssel(me)), at(bufs[di], dsel(me)), local_sems.at[base_l + k])
                cp.start()
                pending.append(cp.wait)
            for k, e in enumerate(plan):
                remote(base_r + k, e, _flip(me, e[0]), me).wait_recv()
            pending += [cp.wait_send for cp in sends]
            base_r += len(plan)
            base_l += len(local)
        for wait in pending:
            wait()

    any_spec = pl.BlockSpec(memory_space=pl.ANY)
    return pl.pallas_call(
        body, name=name, out_shape=tuple(out_shapes), in_specs=[any_spec] * n_in, out_specs=tuple([any_spec] * n_out),
        input_output_aliases=dict(aliases),
        scratch_shapes=[pltpu.SemaphoreType.DMA((max(n_remote, 1),)), pltpu.SemaphoreType.DMA((max(n_remote, 1),)),
                        pltpu.SemaphoreType.DMA((max(n_local, 1),))])(*arrays)


HBM_SPEC = pl.BlockSpec(memory_space=pltpu.HBM)
SEM_SPEC = pl.BlockSpec(memory_space=pltpu.SEMAPHORE)


def _split_copies(bufs, plan, local, send_sems, recv_sems, local_sems):
    me = (lax.axis_index("x"), lax.axis_index("y"), lax.axis_index("c"))

    def at(ref, idx):
        return ref.at[idx] if len(idx) else ref

    def remote(k, sender, receiver):
        rel, si, ssel, di, dsel = plan[k]
        return pltpu.make_async_remote_copy(
            src_ref=at(bufs[si], ssel(sender, receiver)), dst_ref=at(bufs[di], dsel(sender, receiver)),
            send_sem=send_sems.at[k], recv_sem=recv_sems.at[k], device_id=_flip(me, rel), device_id_type=MESH)

    sends = [remote(k, me, _flip(me, plan[k][0])) for k in range(len(plan))]
    arrivals = [remote(k, _flip(me, plan[k][0]), me) for k in range(len(plan))]
    locs = [pltpu.make_async_copy(at(bufs[si], ssel(me)), at(bufs[di], dsel(me)), local_sems.at[k])
            for k, (si, ssel, di, dsel) in enumerate(local)]
    return sends, arrivals, locs


def _exchange_start_many(name, stages):
    sizes = [(len(arrays), len(out_shapes)) for _, arrays, out_shapes, _, _, _ in stages]
    n_buf = sum(n_in + n_out for n_in, n_out in sizes)
    n_sem = 3 * len(stages)

    def body(*refs):
        sems, bufs = refs[n_buf:n_buf + n_sem], refs[n_buf + n_sem:2 * n_buf + n_sem]
        at = 0
        for g, ((n_in, n_out), (_, _, _, plan, local, inplace)) in enumerate(zip(sizes, stages)):
            mine = bufs[at:at + n_in + n_out]
            sends, _, locs = _split_copies(mine * 2 if inplace else mine, plan, local, *sems[3 * g:3 * g + 3])
            for cp in sends + locs:
                cp.start()
            at += n_in + n_out
        refs[-1][...] = jnp.zeros_like(refs[-1])

    operands, sem_shapes = [], []
    for _, arrays, out_shapes, plan, local, _ in stages:
        operands += list(arrays) + [lax.empty(o.shape, o.dtype) for o in out_shapes]
        sem_shapes += [pltpu.SemaphoreType.DMA((max(k, 1),)) for k in (len(plan), len(plan), len(local))]
    operands = [pltpu.with_memory_space_constraint(a, pltpu.HBM) for a in operands]
    out = pl.pallas_call(
        body, name=name,
        out_shape=(*sem_shapes, *[pltpu.HBM(a.shape, a.dtype) for a in operands], _sds((SUBLANES, LANES), F32)),
        in_specs=[HBM_SPEC] * n_buf,
        out_specs=(*[SEM_SPEC] * n_sem, *[HBM_SPEC] * n_buf, pl.BlockSpec(memory_space=pltpu.VMEM)),
        input_output_aliases={i: n_sem + i for i in range(n_buf)},
        compiler_params=pltpu.CompilerParams(has_side_effects=pltpu.SideEffectType.DATAFLOW_SIDE_EFFECTING))(*operands)
    pendings, at = [], n_sem
    for g, ((n_in, n_out), (stage, _, _, plan, local, inplace)) in enumerate(zip(sizes, stages)):
        pendings.append(dict(name=stage, sems=out[3 * g:3 * g + 3], thru=out[at:at + n_in], zones=out[at + n_in:at + n_in + n_out],
                             plan=plan, local=local, inplace=inplace))
        at += n_in + n_out
    return pendings, out[-1]


def _exchange_start(name, arrays, out_shapes, plan, local, inplace=False):
    pendings, token = _exchange_start_many(name, [(name, arrays, out_shapes, plan, local, inplace)])
    return pendings[0], token


def _exchange_wait(pending, after, both=False):
    thru, zones, plan, local, inplace = (pending[k] for k in ("thru", "zones", "plan", "local", "inplace"))
    n_in, n_buf = len(thru), len(thru) + len(zones)

    def body(*refs):
        bufs = refs[:n_buf]
        sends, arrivals, locs = _split_copies(bufs * 2 if inplace else bufs, plan, local, *refs[n_buf:n_buf + 3])
        for cp in arrivals:
            cp.wait_recv()
        for cp in sends:
            cp.wait_send()
        for cp in locs:
            cp.wait()

    out = pl.pallas_call(
        body, name=pending["name"] + "_wait", out_shape=tuple(pltpu.HBM(a.shape, a.dtype) for a in list(thru) + list(zones)),
        in_specs=[HBM_SPEC] * n_buf + [SEM_SPEC] * 3 + [pl.BlockSpec(memory_space=pl.ANY)],
        out_specs=tuple([HBM_SPEC] * n_buf), input_output_aliases={i: i for i in range(n_buf)},
        compiler_params=pltpu.CompilerParams(has_side_effects=pltpu.SideEffectType.DATAFLOW_SIDE_EFFECTING))(
            *thru, *zones, *pending["sems"], after)
    if both:
        return out[:n_in], out[n_in:]
    return out[:n_in] if inplace else out[n_in:]


def _whole(*_):
    return ()


def _half_rows(rows, core):
    return pl.ds(core * (rows // 2), rows // 2)


def _gather_weights_plan(shards):
    n = len(shards)
    dsts = [_sds((N_CHIPS,) + a.shape, a.dtype) for a in shards]
    fetch, forward = [], []
    for t, a in enumerate(shards):
        rows = a.shape[0]
        if rows % (2 * 16) == 0:
            fetch += [(rel, t, (lambda s_, r_, rows=rows: (_half_rows(rows, s_[2]),)), n + t,
                       (lambda s_, r_, rows=rows: (_chip_of(s_), _half_rows(rows, s_[2])))) for rel in REL_CHIPS]
            forward += [(REL_SIBLING, n + t, (lambda s_, r_, rows=rows, rel=rel: (_chip_of(_flip(s_, rel)), _half_rows(rows, s_[2]))),
                         n + t, (lambda s_, r_, rows=rows, rel=rel: (_chip_of(_flip(s_, rel)), _half_rows(rows, s_[2]))))
                        for rel in REL_CHIPS]
        else:
            fetch += [(rel, t, _whole, n + t, lambda s_, r_: (_chip_of(s_),)) for rel in REL_CHIPS]
    local = [(t, _whole, n + t, lambda me: (_chip_of(me),)) for t in range(n)]
    return dsts, fetch, local, forward


def _gather_weights_finish(pending, after):
    landed = _exchange_wait(pending, after)
    n = len(landed)
    return _exchange(pending["name"] + "_forward", landed, n, [[_sds(a.shape, a.dtype) for a in landed], (pending["forward"], [])],
                     aliases={t: t for t in range(n)})


def _gather_chips_plan(arrays):
    n = len(arrays)
    dsts = [_sds((N_CHIPS,) + a.shape, a.dtype) for a in arrays]
    plan = [(rel, t, _whole, n + t, lambda s_, r_: (_chip_of(s_),)) for t in range(n) for rel in REL_CHIPS]
    local = [(t, _whole, n + t, lambda me: (_chip_of(me),)) for t in range(n)]
    return dsts, plan, local


def _gather_chips(name, arrays):
    dsts, plan, local = _gather_chips_plan(arrays)
    return _exchange(name, arrays, len(arrays), [dsts, (plan, local)])


def _gather_chips_start(name, arrays):
    dsts, plan, local = _gather_chips_plan(arrays)
    return _exchange_start(name, arrays, dsts, plan, local)


def _gather_all(name, a):
    plan = [(rel, 0, _whole, 1, lambda s_, r_: (_dev_of(s_),)) for rel in REL_ALL]
    local = [(0, _whole, 1, lambda me: (_dev_of(me),))]
    return _exchange(name, [a], 1, [[_sds((2 * N_CHIPS,) + a.shape, a.dtype)], (plan, local)])[0]


def _swap_halves_start(name, grads):
    n = len(grads)
    dsts = [_sds((g.shape[0], g.shape[1] // 2, g.shape[2]), g.dtype) for g in grads]
    plan = [(REL_SIBLING, t, (lambda s_, r_, rows=g.shape[1]: (pl.ds(0, N_CHIPS), _half_rows(rows, r_[2]))), n + t, _whole)
            for t, g in enumerate(grads)]
    return _exchange_start(name, grads, dsts, plan, [])


def _scatter_chips_plan(sums):
    n = len(sums)
    dsts = [_sds(a.shape, a.dtype) for a in sums]
    plan = [(rel, t, lambda s_, r_: (_chip_of(r_),), n + t, lambda s_, r_: (_chip_of(s_),))
            for t in range(n) for rel in REL_CHIPS]
    local = [(t, lambda me: (_chip_of(me),), n + t, lambda me: (_chip_of(me),)) for t in range(n)]
    return dsts, plan, local


def _scatter_chips_start(name, sums):
    dsts, plan, local = _scatter_chips_plan(sums)
    return _exchange_start(name, sums, dsts, plan, local)


def _swap_back_start(name, totals, layer):
    n = len(totals)
    plan = [(REL_SIBLING, n + t, (lambda s_, r_, rows=a.shape[1]: (layer, _half_rows(rows, s_[2]))),
             n + t, (lambda s_, r_, rows=a.shape[1]: (layer, _half_rows(rows, s_[2])))) for t, a in enumerate(totals)]
    return _exchange_start(name, totals, [], plan, [], inplace=True)


def _add_halves(name, gs, recvs, core):
    n = len(gs)
    nch = recvs[0].shape[0]

    def body(core_ref, *refs):
        for g_ref, r_ref, o_ref in zip(refs[:n], refs[n:2 * n], refs[2 * n:]):
            o_ref[0] = (g_ref[0, 0].astype(F32) + r_ref[0].astype(F32)).astype(o_ref.dtype)

    halves = [pl.BlockSpec((1,) + r.shape[1:], lambda k, core_ref: (k, 0, 0)) for r in recvs]
    spec = pltpu.PrefetchScalarGridSpec(
        num_scalar_prefetch=1, grid=(nch,),
        in_specs=[pl.BlockSpec((1, 1) + r.shape[1:], lambda k, core_ref: (k, core_ref[0], 0, 0)) for r in recvs] + halves,
        out_specs=tuple(halves))
    return pl.pallas_call(body, name=name, out_shape=tuple(_sds(r.shape, r.dtype) for r in recvs), grid_spec=spec,
                          compiler_params=pltpu.CompilerParams(dimension_semantics=("parallel",),
                                                               vmem_limit_bytes=VMEM_LIMIT))(
                                                                   core, *[g.reshape(nch, 2, *r.shape[1:]) for g, r in zip(gs, recvs)],
                                                                   *recvs)


def _sum_chips(name, arrays, core, layer, totals):
    n = len(arrays)
    nch = arrays[0].shape[0]
    parts = 2
    shapes = [(a.shape[1] // parts, a.shape[2]) for a in arrays]

    def body(core_ref, *refs):
        for a_ref, o_ref in zip(refs[:n], refs[-n:]):
            acc = a_ref[0].astype(F32)
            for k in range(1, nch):
                acc = acc + a_ref[k].astype(F32)
            o_ref[0, 0, 0] = acc

    in_specs = [pl.BlockSpec((nch,) + shp, lambda i, core_ref: (0, i, 0)) for shp in shapes]
    args = [core, *arrays]
    if totals is not None:
        in_specs += [pl.BlockSpec(memory_space=pl.ANY)] * n
        args += [t.reshape(DEPTH, 2, parts, *shp) for t, shp in zip(totals, shapes)]
    spec = pltpu.PrefetchScalarGridSpec(
        num_scalar_prefetch=1, grid=(parts,), in_specs=in_specs,
        out_specs=tuple(pl.BlockSpec((1, 1, 1) + shp, lambda i, core_ref: (layer, core_ref[0], i, 0, 0)) for shp in shapes))
    outs = pl.pallas_call(body, name=name, out_shape=tuple(_sds((DEPTH, 2, parts) + shp, F32) for shp in shapes), grid_spec=spec,
                          input_output_aliases={1 + n + t: t for t in range(n)} if totals is not None else {},
                          compiler_params=pltpu.CompilerParams(dimension_semantics=("arbitrary",),
                                                               vmem_limit_bytes=VMEM_LIMIT))(*args)
    return [o.reshape(DEPTH, 2 * parts * shp[0], shp[1]) for o, shp in zip(outs, shapes)]


def _adamw_update(w, g, m, v):
    mn = ADAM_B1 * m + (1.0 - ADAM_B1) * g
    vn = ADAM_B2 * v + (1.0 - ADAM_B2) * (g * g)
    m_hat = mn / (1.0 - ADAM_B1 ** ADAM_STEP)
    v_hat = vn / (1.0 - ADAM_B2 ** ADAM_STEP)
    return -ADAM_LR * (m_hat / (jnp.sqrt(v_hat) + ADAM_EPS) + ADAM_WD * w), mn, vn


def _adamw(name, w, g, m, v, after=None):
    depth, r, c = w.shape
    tr = next(t for t in (512, 448, 384, 352, 336, 256, 192, 128, 64, 32, 16, 8) if r % t == 0 and t * c <= ADAM_TILE_ELEMS)

    def body(w_ref, g_ref, m_ref, v_ref, go_ref, d_ref, mo_ref, vo_ref):
        gv = g_ref[...]
        go_ref[...] = gv
        d_ref[...], mo_ref[...], vo_ref[...] = _adamw_update(w_ref[...], gv, m_ref[...], v_ref[...])

    spec = pl.BlockSpec((1, tr, c), lambda l, i: (l, i, 0))
    o = _sds(w.shape, F32)
    return _call(body, name, (o, o, o, o), (depth, r // tr), [spec] * 4, (spec,) * 4, ("parallel", "parallel"),
                 after=after)(w, g, m, v)


def _adamw_small(name, ws, gs, ms, vs):
    n = len(ws)

    def body(*refs):
        for t in range(n):
            w_ref, g_ref, m_ref, v_ref = (refs[k * n + t] for k in range(4))
            d_ref, mo_ref, vo_ref = (refs[(4 + k) * n + t] for k in range(3))
            d_ref[...], mo_ref[...], vo_ref[...] = _adamw_update(w_ref[...], g_ref[...], m_ref[...], v_ref[...])

    outs = [_sds(w.shape, F32) for w in ws]
    res = pl.pallas_call(body, name=name, out_shape=tuple(outs * 3))(*ws, *gs, *ms, *vs)
    return res[:n], res[n:2 * n], res[2 * n:]


def _pack(arrays, rows):
    flat = jnp.concatenate([a.reshape(-1).astype(F32) for a in arrays])
    return _pad_to(flat, 0, rows * LANES).reshape(rows, LANES)


def _unpack(packed, shapes):
    flat = packed.reshape(-1)
    out, off = [], 0
    for shp in shapes:
        n = 1
        for s_ in shp:
            n *= s_
        out.append(flat[off:off + n].reshape(shp))
        off += n
    return out


_MATRICES = ("w_in", "proj_a", "proj_b", "w_out", "w_gate", "w_up", "w_down")
_SMALL = (("norm1_w", (D_MODEL,)), ("sinks", (N_Q_HEADS,)), ("sgu_ln_w", (SGU_WIDTH,)), ("sgu_ln_b", (SGU_WIDTH,)),
          ("sgu_w", (SGU_GROUPS, SGU_CHUNK, SGU_CHUNK)), ("sgu_b", (SGU_GROUPS, SGU_CHUNK)), ("norm2_w", (D_MODEL,)),
          ("conv_w", (3, FFN_DIM)), ("conv_b", (FFN_DIM,)), ("final_norm_w", (D_MODEL,)))
SMALL_ROWS = 320
ADAM_TILE_ELEMS = 384 * 1024


def _reduce_cores_start(tag, partial):
    names = list(partial)
    pending, token = _swap_halves_start(tag + "_cores", [partial[k] for k in names])
    return dict(pending, tag=tag, names=names), token


def _reduce_chips_start(pending, core, after):
    tag, names = pending["tag"], pending["names"]
    mine, theirs = _exchange_wait(pending, after, both=True)
    sums = _add_halves(tag + "_cores_add", mine, theirs, core)
    scatter, token = _scatter_chips_start(tag + "_chips", sums)
    return dict(scatter, tag=tag, names=names), token


def _reduce_back_start(pending, l, core, totals, after):
    tag, names = pending["tag"], pending["names"]
    before = [totals[k] for k in names] if names[0] in totals else None
    sums = _sum_chips(tag + "_chips_add", _exchange_wait(pending, after), core, l, before)
    back, _ = _swap_back_start(tag + "_back", sums, l)
    return dict(back, names=names)


def kernel(x, c, positions, ada_w, ada_b, norm1_w, w_in, attn_sinks, sgu_ln_w, sgu_ln_b, sgu_w, sgu_b, proj_a, proj_b, w_out, norm2_w, ffn_w_gate, ffn_w_up, ffn_conv_w, ffn_conv_b, ffn_w_down, final_norm_w, loss_target, m_ada_w, m_ada_b, m_norm1_w, m_w_in, m_attn_sinks, m_sgu_ln_w, m_sgu_ln_b, m_sgu_w, m_sgu_b, m_proj_a, m_proj_b, m_w_out, m_norm2_w, m_ffn_w_gate, m_ffn_w_up, m_ffn_conv_w, m_ffn_conv_b, m_ffn_w_down, m_final_norm_w, v_ada_w, v_ada_b, v_norm1_w, v_w_in, v_attn_sinks, v_sgu_ln_w, v_sgu_ln_b, v_sgu_w, v_sgu_b, v_proj_a, v_proj_b, v_w_out, v_norm2_w, v_ffn_w_gate, v_ffn_w_up, v_ffn_conv_w, v_ffn_conv_b, v_ffn_w_down, v_final_norm_w):
    d = D_MODEL
    ax, ay, ac = lax.axis_index("x"), lax.axis_index("y"), lax.axis_index("c")
    chip = 2 * ax + ay
    dev = 4 * ax + 2 * ay + ac
    core = ac.astype(jnp.int32).reshape(1)

    c_all = _gather_all("gather_cond", c.reshape(SUBLANES, d // SUBLANES)).reshape(2 * N_CHIPS, d)
    c_rows = _pad_to(c_all, 0, ADA_ROWS)
    ada_cols = ada_w.shape[2]
    ada_b_cols = lax.dynamic_slice_in_dim(ada_b, chip * ada_cols, ada_cols, axis=1).reshape(DEPTH, 1, ada_cols)
    mod_cols = _ada_fwd("ada_fwd", c_rows, ada_w, ada_b_cols)
    mod_all = _gather_chips("gather_mod", [mod_cols])[0]
    mod_mine = lax.dynamic_index_in_dim(mod_all, dev, axis=2, keepdims=False)
    mod_mine = mod_mine.transpose(1, 0, 2).reshape(DEPTH, 1, 6 * d)
    mods = [tuple(jnp.split(mod_mine[l], 6, axis=-1)) for l in range(DEPTH)]

    tr = lambda a: jnp.swapaxes(a, 1, 2)
    shards = [tr(w_in).astype(BF), proj_a.astype(BF), proj_b.astype(BF), w_out.astype(BF),
              tr(ffn_w_gate).astype(BF), tr(ffn_w_up).astype(BF), ffn_w_down.astype(BF), ffn_conv_w]
    stages, forwards = [], []
    for l in range(DEPTH):
        for tag, members in (("in", shards[:1]), ("mix", shards[1:4]), ("ffn", shards[4:])):
            group = [a[l] for a in members]
            if not stages:
                group[0] = group[0] + (mod_all[0, 0, 0, 0] * 0.0).astype(BF)
            dsts, fetch, local, forward = _gather_weights_plan(group)
            stages.append((f"l{l}_gather_{tag}", group, dsts, fetch, local, False))
            forwards.append(forward)
    pendings, token = _exchange_start_many("gather_weights", stages)
    pendings = [dict(p, forward=f) for p, f in zip(pendings, forwards)]
    fetches = [pendings[3 * l:3 * l + 3] for l in range(DEPTH)]

    small_in = dict(norm1_w=norm1_w, sinks=attn_sinks, sgu_ln_w=sgu_ln_w, sgu_ln_b=sgu_ln_b, sgu_w=sgu_w, sgu_b=sgu_b,
                    norm2_w=norm2_w, conv_b=ffn_conv_b)
    cosf, sinf = _rope_tables(positions[0])
    small_of = lambda l: {k: v[l] for k, v in small_in.items()}

    h = x[0]
    saved, params = [], []
    for l in range(DEPTH):
        first, mix, ffn = fetches[l]
        w_in_l = _gather_weights_finish(first, token if l == 0 else h)
        late = lambda y, l=l, mix=mix: _mix_params(*_gather_weights_finish(mix, y), small_of(l))
        later = lambda y, l=l, ffn=ffn: _ffn_params(*_gather_weights_finish(ffn, y), small_of(l))
        h, sv, p = _layer_fwd(l, h, mods[l], _early_params(w_in_l[0], small_of(l)), cosf, sinf, late=late, later=later)
        saved.append(sv)
        params.append(p)
    dx, loss_part, d_final = _loss_head("loss_head", h, final_norm_w.reshape(1, d), loss_target[0])
    loss = lax.psum(loss_part[0, 0], ("x", "y", "c"))

    def small_pack(l, grads):
        cw, cb = _conv_grads_natural(grads)
        nat = dict(grads, conv_w=cw, conv_b=cb, final_norm_w=d_final if l == DEPTH - 1 else jnp.zeros((d,), F32))
        return _pack([nat[k] for k, _ in _SMALL], N_CHIPS * SMALL_ROWS).reshape(N_CHIPS, SMALL_ROWS, LANES)

    waiting, inflight = [], []

    def send(tag, partial):
        pending, token = _reduce_cores_start(tag, partial)
        waiting.append(pending)
        return token

    def tick(y):
        token = None
        while waiting:
            pending, token = _reduce_chips_start(waiting.pop(0), core, y)
            inflight.append(pending)
        return token

    dmods = [None] * DEPTH
    dx, dmods[1], grads = _layer_bwd(1, dx, mods[1], params[1], saved[1], cosf, sinf)
    token = send("l1_reduce", dict({k: grads[k] for k in _MATRICES}, small=small_pack(1, grads)))
    dx, dmods[0], grads = _layer_bwd(0, dx, mods[0], params[0], saved[0], cosf, sinf, after=token,
                                     emit=lambda part: send("l0_reduce_" + "_".join(part), part), tick=tick)
    dmod_mine = jnp.concatenate([jnp.concatenate(dmods[l], axis=1) for l in range(DEPTH)], axis=1)
    dmod_all = _gather_all("gather_dmod", dmod_mine.reshape(SUBLANES, -1)).reshape(2 * N_CHIPS, DEPTH * 6 * d)
    send("l0_reduce_in", dict(w_in=grads["w_in"], small=small_pack(0, grads) + dmod_all[0, 0] * 0.0))
    tick(dmod_all)

    totals, flying = {}, None

    def land(after):
        if flying is not None:
            totals.update(zip(flying["names"], _exchange_wait(flying, after)))

    for pending in inflight[:-1]:
        land(dx)
        flying = _reduce_back_start(pending, 1 if pending["tag"].startswith("l1") else 0, core, totals, dx)
    land(dx)
    flying = None

    g_ada_b = _colsum("ada_b_grad", dmod_all).reshape(DEPTH, 6 * d)
    dmod_cols = jnp.stack([lax.dynamic_slice_in_dim(dmod_all, l * 6 * d + chip * ada_cols, ada_cols, axis=1)
                           for l in range(DEPTH)])
    g_ada_w = _ada_bwd("ada_w_grad", c_rows, _pad_to(dmod_cols, 1, ADA_ROWS))
    big = dict(w_in=(tr(w_in), tr(m_w_in), tr(v_w_in)), proj_a=(proj_a, m_proj_a, v_proj_a), proj_b=(proj_b, m_proj_b, v_proj_b),
               w_out=(w_out, m_w_out, v_w_out), w_gate=(tr(ffn_w_gate), tr(m_ffn_w_gate), tr(v_ffn_w_gate)),
               w_up=(tr(ffn_w_up), tr(m_ffn_w_up), tr(v_ffn_w_up)), w_down=(ffn_w_down, m_ffn_w_down, v_ffn_w_down))
    upd, g_big = {}, {}

    def update(k, after=None):
        res = _adamw("adamw_" + k, big[k][0], totals[k], *big[k][1:], after=after)
        res = [tr(a) for a in res] if k in ("w_in", "w_gate", "w_up") else res
        g_big[k], upd[k] = res[0], res[1:]

    for k in ("w_down", "w_gate", "w_up", "w_out", "proj_a", "proj_b"):
        update(k)
    g_big["ada_w"], *upd["ada_w"] = _adamw("adamw_ada_w", ada_w, g_ada_w, m_ada_w, v_ada_w)
    flying = _reduce_back_start(inflight[-1], 0, core, totals, upd["ada_w"][0])
    land(upd["ada_w"][1])
    gathering, token = _gather_chips_start("gather_small", [totals["small"]])
    update("w_in", after=token)
    small_all = _exchange_wait(gathering, upd["w_in"][0])[0]
    small_g = small_all.transpose(1, 0, 2, 3).reshape(DEPTH, -1)
    per_layer = [_unpack(small_g[l], [shp for _, shp in _SMALL]) for l in range(DEPTH)]
    sg = {k: jnp.stack([per_layer[l][i] for l in range(DEPTH)]) for i, (k, _) in enumerate(_SMALL)}
    g_final = sg["final_norm_w"][DEPTH - 1]
    g_conv_w = lax.dynamic_slice_in_dim(sg["conv_w"], chip * FFN_SHARD, FFN_SHARD, axis=2)

    rest = [("ada_b", ada_b, g_ada_b, m_ada_b, v_ada_b), ("norm1_w", norm1_w, sg["norm1_w"], m_norm1_w, v_norm1_w),
            ("attn_sinks", attn_sinks, sg["sinks"], m_attn_sinks, v_attn_sinks),
            ("sgu_ln_w", sgu_ln_w, sg["sgu_ln_w"], m_sgu_ln_w, v_sgu_ln_w),
            ("sgu_ln_b", sgu_ln_b, sg["sgu_ln_b"], m_sgu_ln_b, v_sgu_ln_b), ("sgu_w", sgu_w, sg["sgu_w"], m_sgu_w, v_sgu_w),
            ("sgu_b", sgu_b, sg["sgu_b"], m_sgu_b, v_sgu_b), ("norm2_w", norm2_w, sg["norm2_w"], m_norm2_w, v_norm2_w),
            ("ffn_conv_w", ffn_conv_w, g_conv_w, m_ffn_conv_w, v_ffn_conv_w),
            ("ffn_conv_b", ffn_conv_b, sg["conv_b"], m_ffn_conv_b, v_ffn_conv_b),
            ("final_norm_w", final_norm_w.reshape(1, d), g_final.reshape(1, d), m_final_norm_w.reshape(1, d),
             v_final_norm_w.reshape(1, d))]
    rest_out = _adamw_small("adamw_rest", *[[r[i] for r in rest] for i in (1, 2, 3, 4)])
    g_rest = {r[0]: r[2] for r in rest}
    u_rest = {r[0]: tuple(o[i] for o in rest_out) for i, r in enumerate(rest)}
    g_rest["final_norm_w"] = g_final
    u_rest["final_norm_w"] = tuple(a.reshape(d) for a in u_rest["final_norm_w"])

    names = ("ada_w", "ada_b", "norm1_w", "w_in", "attn_sinks", "sgu_ln_w", "sgu_ln_b", "sgu_w", "sgu_b", "proj_a", "proj_b",
             "w_out", "norm2_w", "ffn_w_gate", "ffn_w_up", "ffn_conv_w", "ffn_conv_b", "ffn_w_down", "final_norm_w")
    alias = {"ffn_w_gate": "w_gate", "ffn_w_up": "w_up", "ffn_w_down": "w_down"}
    grad_of = lambda n: g_rest[n] if n in g_rest else g_big[alias.get(n, n)]
    upd_of = lambda n: u_rest[n] if n in u_rest else upd[alias.get(n, n)]
    return (loss, dx[None], *[grad_of(n) for n in names], *[upd_of(n)[0] for n in names],
            *[upd_of(n)[1] for n in names], *[upd_of(n)[2] for n in names])
```

```python
import jax
import jax.numpy as jnp
from jax import lax
from jax.experimental import pallas as pl
from jax.experimental.pallas import tpu as pltpu

F32 = jnp.float32
BF = jnp.bfloat16

D_MODEL = 1024
N_Q_HEADS = 16
N_KV_HEADS = 2
HEAD_DIM = 64
ATTN_BLOCK = 128
ROPE_THETA = 500000.0
ROT_DIM = HEAD_DIM // 4
SGU_WIDTH = 1024
SGU_GROUPS = 8
SGU_CHUNK = 128
FFN_DIM = 2816
NORM_EPS = 1e-6
DEPTH = 2
IN_COLS = 5376
N_CHIPS = 4
FFN_SHARD = FFN_DIM // N_CHIPS
CHIPS_PER_CHUNK = 2
FFN_CHUNK = CHIPS_PER_CHUNK * FFN_SHARD
FFN_CHUNKS = FFN_DIM // FFN_CHUNK
LANES = 128
SUBLANES = 8
HALO = 16
VMEM_LIMIT = 56 * 1024 * 1024
NEG_BIG = -1e30

ADAM_LR = 0.001
ADAM_B1 = 0.9
ADAM_B2 = 0.999
ADAM_EPS = 1e-08
ADAM_WD = 0.01
ADAM_STEP = 10

MESH = pl.DeviceIdType.MESH

Q_END = 1024
KV_END = 1280
U_END = 2304
Z_END = 3328
GA_END = 4352


def _sds(shape, dtype):
    return jax.ShapeDtypeStruct(tuple(shape), dtype)


def _call(body, name, out_shape, grid, in_specs, out_specs, semantics, scratch=(), after=None):
    n_in = len(in_specs)
    fn = body
    if after is not None:
        def fn(*refs):
            return body(*refs[:n_in], *refs[n_in + 1:])

        in_specs = list(in_specs) + [pl.BlockSpec(memory_space=pl.ANY)]
    call = pl.pallas_call(
        fn, name=name, out_shape=out_shape, grid=grid, in_specs=in_specs, out_specs=out_specs,
        scratch_shapes=scratch,
        compiler_params=pltpu.CompilerParams(dimension_semantics=semantics, vmem_limit_bytes=VMEM_LIMIT))
    if after is None:
        return call
    return lambda *args: call(*args, after)


def _rows(tm, width, col=0):
    return pl.BlockSpec((tm, width), lambda i: (i, col))


def _vec(width):
    return pl.BlockSpec((1, width), lambda i: (0, 0))


def _resident(shape):
    zeros = (0,) * len(shape)
    return pl.BlockSpec(tuple(shape), lambda *_: zeros, pipeline_mode=pl.Buffered(1))


def _sigmoid(x):
    return 0.5 + 0.5 * jnp.tanh(0.5 * x)


def _gelu(x):
    return 0.5 * x * (1.0 + lax.erf(x * 0.7071067811865476))


def _gelu_and_grad(x):
    cdf = 0.5 * (1.0 + lax.erf(x * 0.7071067811865476))
    return x * cdf, cdf + x * jnp.exp(-0.5 * x * x) * 0.3989422804014327


def _dot(a, b):
    return jnp.dot(a, b, preferred_element_type=F32)


def _dot_nt(a, b):
    return lax.dot_general(a, b, (((1,), (1,)), ((), ())), preferred_element_type=F32)


def _dot_tn(a, b):
    return lax.dot_general(a, b, (((0,), (0,)), ((), ())), preferred_element_type=F32)


def _rms(xv):
    return lax.rsqrt(jnp.mean(xv * xv, axis=-1, keepdims=True) + NORM_EPS)


def _matmul_tn(name, a, b, tk=512, tn=1024, after=None):
    s, k = a.shape
    n = b.shape[1]
    tk, tn = min(tk, k), min(tn, n)

    def body(a_ref, b_ref, o_ref):
        o_ref[...] = _dot_tn(a_ref[...], b_ref[...]).astype(o_ref.dtype)

    return _call(body, name, _sds((k, n), BF), (k // tk, n // tn),
                 [pl.BlockSpec((s, tk), lambda i, j: (0, i)), pl.BlockSpec((s, tn), lambda i, j: (0, j))],
                 pl.BlockSpec((tk, tn), lambda i, j: (i, j)), ("parallel", "parallel"), after=after)(a, b)


def _matmul_tn_segments(name, segments, b, tk=256):
    s, n = b.shape
    counts = [a.shape[1] // tk for a in segments]
    firsts = [sum(counts[:t]) for t in range(len(segments))]

    def body(*refs):
        a_refs, b_ref, o_ref = refs[:len(segments)], refs[len(segments)], refs[-1]
        i = pl.program_id(0)
        for a_ref, first, count in zip(a_refs, firsts, counts):
            @pl.when((i >= first) & (i < first + count))
            def _(a_ref=a_ref):
                o_ref[...] = _dot_tn(a_ref[...], b_ref[...]).astype(BF)

    specs = [pl.BlockSpec((s, tk), lambda i, first=first, count=count: (0, jnp.clip(i - first, 0, count - 1)))
             for first, count in zip(firsts, counts)]
    return _call(body, name, _sds((sum(counts) * tk, n), BF), (sum(counts),), specs + [_resident(b.shape)],
                 pl.BlockSpec((tk, n), lambda i: (i, 0)), ("arbitrary",))(*segments, b)


def _rope_partner(v):
    lane = lax.broadcasted_iota(jnp.int32, (1, LANES), 1) % HEAD_DIM
    return jnp.where(lane < ROT_DIM // 2, pltpu.roll(v, LANES - ROT_DIM // 2, axis=1), pltpu.roll(v, ROT_DIM // 2, axis=1))


def _dup_half(v, half):
    lane = lax.broadcasted_iota(jnp.int32, (1, LANES), 1)
    keep = jnp.where((lane >= HEAD_DIM) == (half == 1), v, 0.0)
    return keep + pltpu.roll(keep, HEAD_DIM, axis=1)


def _in_proj(name, x, w, sc, sh, w_in, cosf, sinf, tm=512, after=None):
    s, d = x.shape
    tm = min(tm, s)

    def body(x_ref, w_ref, sc_ref, sh_ref, win_ref, cos_ref, sin_ref,
             h_ref, qr_ref, kk0_ref, kk1_ref, vv0_ref, vv1_ref, u_ref, v_ref, ga_ref, gb_ref):
        xv = x_ref[...]
        h = ((xv * _rms(xv)) * w_ref[...] * (1.0 + sc_ref[...]) + sh_ref[...]).astype(BF)
        h_ref[...] = h
        cosv, sinv = cos_ref[...], sin_ref[...]
        q = _dot_nt(h, win_ref[:Q_END, :])
        for j in range(D_MODEL // LANES):
            qv = q[:, j * LANES:(j + 1) * LANES]
            qr_ref[:, j * LANES:(j + 1) * LANES] = ((qv * cosv + _rope_partner(qv) * sinv) * ATTN_SCALE).astype(BF)
        kv = _dot_nt(h, win_ref[Q_END:KV_END, :])
        kr = kv[:, :LANES] * cosv + _rope_partner(kv[:, :LANES]) * sinv
        vv = kv[:, LANES:]
        kk0_ref[...] = _dup_half(kr, 0).astype(BF)
        kk1_ref[...] = _dup_half(kr, 1).astype(BF)
        vv0_ref[...] = _dup_half(vv, 0).astype(BF)
        vv1_ref[...] = _dup_half(vv, 1).astype(BF)
        u_ref[...] = _dot_nt(h, win_ref[KV_END:U_END, :])
        v_ref[...] = _dot_nt(h, win_ref[U_END:Z_END, :])
        ga_ref[...] = _dot_nt(h, win_ref[Z_END:GA_END, :]).astype(BF)
        gb_ref[...] = _dot_nt(h, win_ref[GA_END:, :]).astype(BF)

    wide, kvs, pre = _sds((s, d), BF), _sds((s, LANES), BF), _sds((s, d), F32)
    return _call(body, name, (wide, wide, kvs, kvs, kvs, kvs, pre, pre, wide, wide), (s // tm,),
                 [_rows(tm, d), _vec(d), _vec(d), _vec(d), _resident(w_in.shape), _rows(tm, LANES), _rows(tm, LANES)],
                 (_rows(tm, d), _rows(tm, d)) + (_rows(tm, LANES),) * 4 + (_rows(tm, d),) * 4, ("parallel",), after=after)(
                     x, w, sc, sh, w_in, cosf, sinf)


def _in_proj_bwd(name, dq, dkv, du, dv, dga, dgb, w_in, x, w, sc, dx_in, tm=512):
    s, d = x.shape
    tm = min(tm, s)

    def body(dq_ref, dkv_ref, du_ref, dv_ref, dga_ref, dgb_ref, win_ref, x_ref, w_ref, sc_ref, dxin_ref,
             dx_ref, da_ref, dsh_ref):
        @pl.when(pl.program_id(0) == 0)
        def _():
            da_ref[...] = jnp.zeros_like(da_ref)
            dsh_ref[...] = jnp.zeros_like(dsh_ref)

        dh = (_dot(dq_ref[...], win_ref[:Q_END, :]) + _dot(dkv_ref[...], win_ref[Q_END:KV_END, :])
              + _dot(du_ref[...], win_ref[KV_END:U_END, :]) + _dot(dv_ref[...], win_ref[U_END:Z_END, :])
              + _dot(dga_ref[...], win_ref[Z_END:GA_END, :]) + _dot(dgb_ref[...], win_ref[GA_END:, :]))
        xv = x_ref[...]
        r = _rms(xv)
        xn = xv * r
        dxn = dh * (w_ref[...] * (1.0 + sc_ref[...]))
        dx_ref[...] = dxin_ref[...] + r * (dxn - xn * jnp.mean(dxn * xn, axis=-1, keepdims=True))
        da_ref[...] += jnp.sum(dh * xn, axis=0, keepdims=True)
        dsh_ref[...] += jnp.sum(dh, axis=0, keepdims=True)

    return _call(body, name, (_sds((s, d), F32), _sds((1, d), F32), _sds((1, d), F32)), (s // tm,),
                 [_rows(tm, d), _rows(tm, 2 * LANES), _rows(tm, d), _rows(tm, d), _rows(tm, d), _rows(tm, d),
                  _resident(w_in.shape), _rows(tm, d), _vec(d), _vec(d), _rows(tm, d)],
                 (_rows(tm, d), _vec(d), _vec(d)), ("arbitrary",))(dq, dkv, du, dv, dga, dgb, w_in, x, w, sc, dx_in)


def _rope_bwd(name, dqr, dkv_cur, dkv_prev, cosf, sinf, tm=512):
    s = dqr.shape[0]
    tm = min(tm, s)
    steps = s // tm
    per = tm // ATTN_BLOCK
    nb = s // ATTN_BLOCK

    def unrope(v, cosv, sinv):
        return v * cosv - _rope_partner(v) * sinv

    def body(dq_ref, cur_ref, prev_ref, next_ref, cos_ref, sin_ref, dqo_ref, dkvo_ref):
        i = pl.program_id(0)
        cosv, sinv = cos_ref[...], sin_ref[...]
        for j in range(D_MODEL // LANES):
            dqo_ref[:, j * LANES:(j + 1) * LANES] = unrope(dq_ref[:, j * LANES:(j + 1) * LANES], cosv, sinv).astype(BF)
        nxt = jnp.where(i < steps - 1, next_ref[...], 0.0)
        if per > 1:
            shifted = jnp.concatenate([prev_ref[ATTN_BLOCK:, :], nxt], axis=0)
        else:
            shifted = nxt
        tot = cur_ref[...] + shifted
        dkvo_ref[:, :LANES] = unrope(tot[:, :LANES], cosv, sinv).astype(BF)
        dkvo_ref[:, LANES:] = tot[:, LANES:].astype(BF)

    nxt_spec = pl.BlockSpec((ATTN_BLOCK, 2 * LANES), lambda i: (jnp.minimum((i + 1) * per, nb - 1), 0))
    return _call(body, name, (_sds((s, D_MODEL), BF), _sds((s, 2 * LANES), BF)), (steps,),
                 [_rows(tm, D_MODEL), _rows(tm, 2 * LANES), _rows(tm, 2 * LANES), nxt_spec, _rows(tm, LANES),
                  _rows(tm, LANES)],
                 (_rows(tm, D_MODEL), _rows(tm, 2 * LANES)), ("parallel",))(dqr, dkv_cur, dkv_prev, dkv_prev, cosf, sinf)


Q_PER_KV = N_Q_HEADS // N_KV_HEADS
ATTN_SCALE = HEAD_DIM ** -0.5
HEADS_AHEAD_FWD = 2
HEADS_AHEAD_BWD = 3


def _band_mask_t(n):
    kj = lax.broadcasted_iota(jnp.int32, (2 * ATTN_BLOCK, ATTN_BLOCK), 0)
    qi = lax.broadcasted_iota(jnp.int32, (2 * ATTN_BLOCK, ATTN_BLOCK), 1)
    return (kj > qi) & (kj <= qi + ATTN_BLOCK) & ((n > 0) | (kj >= ATTN_BLOCK))


def _softmax_t(raw, allowed, sink):
    sc = jnp.where(allowed, raw, NEG_BIG)
    m = jnp.maximum(jnp.max(sc, axis=0, keepdims=True), sink)
    p = jnp.exp(sc - m)
    esink = jnp.exp(sink - m)
    inv = 1.0 / (jnp.sum(p, axis=0, keepdims=True) + esink)
    return p * inv, esink * inv


def _kv_specs():
    cur = pl.BlockSpec((ATTN_BLOCK, LANES), lambda n: (n, 0))
    prev = pl.BlockSpec((ATTN_BLOCK, LANES), lambda n: (jnp.maximum(n - 1, 0), 0))
    return [prev, cur] * 4


def _attention(name, qr, kk0, kk1, vv0, vv1, sinks):
    s = qr.shape[0]
    nb = s // ATTN_BLOCK

    def body(sink_ref, q_ref, k0p, k0c, k1p, k1c, v0p, v0c, v1p, v1c, y_ref):
        allowed = _band_mask_t(pl.program_id(0))
        upper = lax.broadcasted_iota(jnp.int32, (1, LANES), 1) >= HEAD_DIM
        upper_rows = lax.broadcasted_iota(jnp.int32, (LANES, 1), 0) >= HEAD_DIM
        bands = ((jnp.concatenate([k0p[...], k0c[...]], axis=0), jnp.concatenate([v0p[...], v0c[...]], axis=0)),
                 (jnp.concatenate([k1p[...], k1c[...]], axis=0), jnp.concatenate([v1p[...], v1c[...]], axis=0)))
        vbts = (bands[0][1].T, bands[1][1].T)

        def scores(h):
            hk, j, half = h // Q_PER_KV, (h % Q_PER_KV) // 2, h % 2
            col = (hk * 4 + j) * LANES
            qp = q_ref[:, col:col + LANES]
            return _dot_nt(bands[hk][0], jnp.where(upper if half else jnp.logical_not(upper), qp, jnp.zeros_like(qp)))

        out_t = None
        ahead = [scores(h) for h in range(HEADS_AHEAD_FWD)]
        for h in range(N_Q_HEADS):
            hk, j, half = h // Q_PER_KV, (h % Q_PER_KV) // 2, h % 2
            raw = ahead.pop(0)
            if h + HEADS_AHEAD_FWD < N_Q_HEADS:
                ahead.append(scores(h + HEADS_AHEAD_FWD))
            pn, _ = _softmax_t(raw, allowed, sink_ref[h])
            o_h = _dot(vbts[hk], pn.astype(BF))
            out_t = jnp.where(upper_rows, o_h, out_t) if half else o_h
            if half:
                col = (hk * 4 + j) * LANES
                y_ref[:, col:col + LANES] = out_t.T.astype(BF)

    return _call(body, name, _sds((s, D_MODEL), BF), (nb,),
                 [pl.BlockSpec(memory_space=pltpu.SMEM), pl.BlockSpec((ATTN_BLOCK, D_MODEL), lambda n: (n, 0))] + _kv_specs(),
                 pl.BlockSpec((ATTN_BLOCK, D_MODEL), lambda n: (n, 0)), ("parallel",))(
                     sinks, qr, kk0, kk0, kk1, kk1, vv0, vv0, vv1, vv1)


def _attention_bwd(name, qr, kk0, kk1, vv0, vv1, sinks, dy, after=None):
    s = qr.shape[0]
    nb = s // ATTN_BLOCK

    def body(sink_ref, q_ref, dy_ref, k0p, k0c, k1p, k1c, v0p, v0c, v1p, v1c, dq_ref, cur_ref, prev_ref, dsink_ref):
        @pl.when(pl.program_id(0) == 0)
        def _():
            dsink_ref[...] = jnp.zeros_like(dsink_ref)

        allowed = _band_mask_t(pl.program_id(0))
        lane = lax.broadcasted_iota(jnp.int32, (1, LANES), 1)
        upper = lane >= HEAD_DIM
        upper_rows = lax.broadcasted_iota(jnp.int32, (LANES, 1), 0) >= HEAD_DIM
        bands = ((jnp.concatenate([k0p[...], k0c[...]], axis=0), jnp.concatenate([v0p[...], v0c[...]], axis=0)),
                 (jnp.concatenate([k1p[...], k1c[...]], axis=0), jnp.concatenate([v1p[...], v1c[...]], axis=0)))
        kbts = (bands[0][0].T, bands[1][0].T)

        def scores(h):
            hk, j, half = h // Q_PER_KV, (h % Q_PER_KV) // 2, h % 2
            kb, vb = bands[hk]
            col = (hk * 4 + j) * LANES
            sel = upper if half else jnp.logical_not(upper)
            qp = q_ref[:, col:col + LANES]
            qa = jnp.where(sel, qp, jnp.zeros_like(qp))
            dya = jnp.where(sel, dy_ref[:, col:col + LANES], 0.0).astype(BF)
            return qa, dya, _dot_nt(kb, qa), _dot_nt(vb, dya)

        dsink = jnp.zeros((1, LANES), F32)
        dk_slab = jnp.zeros((2 * ATTN_BLOCK, LANES), F32)
        dv_slab = jnp.zeros((2 * ATTN_BLOCK, LANES), F32)
        dkk = dvv = dq_t = None
        ahead = [scores(h) for h in range(HEADS_AHEAD_BWD)]
        for h in range(N_Q_HEADS):
            hk, j, half = h // Q_PER_KV, (h % Q_PER_KV) // 2, h % 2
            qa, dya, raw, dp = ahead.pop(0)
            if h + HEADS_AHEAD_BWD < N_Q_HEADS:
                ahead.append(scores(h + HEADS_AHEAD_BWD))
            pn, psink = _softmax_t(raw, allowed, sink_ref[h])
            delta = jnp.sum(pn * dp, axis=0, keepdims=True)
            ds = (pn * (dp - delta)).astype(BF)
            dsink = dsink + jnp.where(lane == h, -jnp.sum(psink * delta), 0.0)
            dq_h = _dot(kbts[hk], ds) * ATTN_SCALE
            dq_t = jnp.where(upper_rows, dq_h, dq_t) if half else dq_h
            dk_h, dv_h = _dot(ds, qa), _dot(pn.astype(BF), dya)
            dkk, dvv = (dk_h, dv_h) if h % Q_PER_KV == 0 else (dkk + dk_h, dvv + dv_h)
            if half:
                col = (hk * 4 + j) * LANES
                dq_ref[:, col:col + LANES] = dq_t.T
            if h % Q_PER_KV == Q_PER_KV - 1:
                mine = upper if hk else jnp.logical_not(upper)
                dk_slab = jnp.where(mine, dkk + pltpu.roll(dkk, HEAD_DIM, axis=1), dk_slab)
                dv_slab = jnp.where(mine, dvv + pltpu.roll(dvv, HEAD_DIM, axis=1), dv_slab)
        prev_ref[:, :LANES] = dk_slab[:ATTN_BLOCK]
        prev_ref[:, LANES:] = dv_slab[:ATTN_BLOCK]
        cur_ref[:, :LANES] = dk_slab[ATTN_BLOCK:]
        cur_ref[:, LANES:] = dv_slab[ATTN_BLOCK:]
        dsink_ref[...] += dsink

    blk = pl.BlockSpec((ATTN_BLOCK, D_MODEL), lambda n: (n, 0))
    kvo = pl.BlockSpec((ATTN_BLOCK, 2 * LANES), lambda n: (n, 0))
    return _call(body, name,
                 (_sds((s, D_MODEL), F32), _sds((s, 2 * LANES), F32), _sds((s, 2 * LANES), F32), _sds((1, LANES), F32)),
                 (nb,), [pl.BlockSpec(memory_space=pltpu.SMEM), blk, blk] + _kv_specs(),
                 (blk, kvo, kvo, pl.BlockSpec((1, LANES), lambda n: (0, 0))), ("arbitrary",), after=after)(
                     sinks, qr, dy, kk0, kk0, kk1, kk1, vv0, vv0, vv1, vv1)


def _sgu_weights(wm_ref, g):
    t = lax.broadcasted_iota(jnp.int32, (SGU_CHUNK, SGU_CHUNK), 0)
    sidx = lax.broadcasted_iota(jnp.int32, (SGU_CHUNK, SGU_CHUNK), 1)
    return jnp.where(sidx <= t, wm_ref[g], 0.0).astype(BF)


def _layer_norm_stats(v):
    mu = jnp.mean(v, axis=-1, keepdims=True)
    cen = v - mu
    rstd = lax.rsqrt(jnp.mean(cen * cen, axis=-1, keepdims=True) + NORM_EPS)
    return cen * rstd, rstd


def _sgu(name, u_pre, v_pre, ln_w, ln_b, wm, bfull, tm=512):
    s, w = u_pre.shape
    tm = min(tm, s)

    def body(u_ref, v_ref, lw_ref, lb_ref, wm_ref, b_ref, y_ref):
        vhat, _ = _layer_norm_stats(_gelu(v_ref[...]))
        vn = (vhat * lw_ref[...] + lb_ref[...]).astype(BF)
        for g in range(SGU_GROUPS):
            wg = _sgu_weights(wm_ref, g)
            cols = slice(g * SGU_CHUNK, (g + 1) * SGU_CHUNK)
            for ch in range(tm // SGU_CHUNK):
                rows = slice(ch * SGU_CHUNK, (ch + 1) * SGU_CHUNK)
                f = _dot(wg, vn[rows, cols]) + b_ref[g]
                y_ref[rows, cols] = (_gelu(u_ref[rows, cols]) * f).astype(BF)

    full3 = pl.BlockSpec((SGU_GROUPS, SGU_CHUNK, SGU_CHUNK), lambda i: (0, 0, 0))
    return _call(body, name, _sds((s, w), BF), (s // tm,),
                 [_rows(tm, w), _rows(tm, w), _vec(w), _vec(w), full3, full3],
                 _rows(tm, w), ("parallel",))(u_pre, v_pre, ln_w, ln_b, wm, bfull)


def _sgu_bwd(name, u_pre, v_pre, ln_w, ln_b, wm, bfull, dy, tm=512, after=None):
    s, w = u_pre.shape
    tm = min(tm, s)
    steps = s // tm

    def body(u_ref, v_ref, lw_ref, lb_ref, wm_ref, b_ref, dy_ref, du_ref, dv_ref, dwm_ref, db_ref, dlw_ref, dlb_ref,
             dfsum_ref):
        i = pl.program_id(0)

        @pl.when(i == 0)
        def _():
            dwm_ref[...] = jnp.zeros_like(dwm_ref)
            dlw_ref[...] = jnp.zeros_like(dlw_ref)
            dlb_ref[...] = jnp.zeros_like(dlb_ref)
            dfsum_ref[...] = jnp.zeros_like(dfsum_ref)

        vpre = v_ref[...]
        vg, dvg_dv = _gelu_and_grad(vpre)
        vhat, rstd = _layer_norm_stats(vg)
        vn = (vhat * lw_ref[...] + lb_ref[...]).astype(BF)
        t = lax.broadcasted_iota(jnp.int32, (SGU_CHUNK, SGU_CHUNK), 0)
        sidx = lax.broadcasted_iota(jnp.int32, (SGU_CHUNK, SGU_CHUNK), 1)
        dvn_cols = []
        for g in range(SGU_GROUPS):
            wg = _sgu_weights(wm_ref, g)
            cols = slice(g * SGU_CHUNK, (g + 1) * SGU_CHUNK)
            dvn_rows = []
            dwg = jnp.zeros((SGU_CHUNK, SGU_CHUNK), F32)
            dfs = jnp.zeros((SGU_CHUNK, SGU_CHUNK), F32)
            for ch in range(tm // SGU_CHUNK):
                rows = slice(ch * SGU_CHUNK, (ch + 1) * SGU_CHUNK)
                upre = u_ref[rows, cols]
                dyv = dy_ref[rows, cols].astype(F32)
                f = _dot(wg, vn[rows, cols]) + b_ref[g]
                ug, dug_du = _gelu_and_grad(upre)
                du_ref[rows, cols] = (dyv * f * dug_du).astype(BF)
                df = dyv * ug
                dfb = df.astype(BF)
                dvn_rows.append(_dot_tn(wg, dfb))
                dwg = dwg + _dot_nt(dfb, vn[rows, cols])
                dfs = dfs + df
            dwm_ref[g] += jnp.where(sidx <= t, dwg, 0.0)
            dfsum_ref[g] += dfs
            dvn_cols.append(jnp.concatenate(dvn_rows, axis=0) if len(dvn_rows) > 1 else dvn_rows[0])
        dvn = jnp.concatenate(dvn_cols, axis=1)
        dlw_ref[...] += jnp.sum(dvn * vhat, axis=0, keepdims=True)
        dlb_ref[...] += jnp.sum(dvn, axis=0, keepdims=True)
        dvh = dvn * lw_ref[...]
        dvg = rstd * (dvh - jnp.mean(dvh, axis=-1, keepdims=True) - vhat * jnp.mean(dvh * vhat, axis=-1, keepdims=True))
        dv_ref[...] = (dvg * dvg_dv).astype(BF)

        @pl.when(i == steps - 1)
        def _():
            for g in range(SGU_GROUPS):
                db_ref[g:g + 1, :] = jnp.sum(dfsum_ref[g].T, axis=0, keepdims=True)

    full3 = pl.BlockSpec((SGU_GROUPS, SGU_CHUNK, SGU_CHUNK), lambda i: (0, 0, 0))
    return _call(body, name,
                 (_sds((s, w), BF), _sds((s, w), BF), _sds((SGU_GROUPS, SGU_CHUNK, SGU_CHUNK), F32),
                  _sds((SGU_GROUPS, SGU_CHUNK), F32), _sds((1, w), F32), _sds((1, w), F32)),
                 (steps,),
                 [_rows(tm, w), _rows(tm, w), _vec(w), _vec(w), full3, full3, _rows(tm, w)],
                 (_rows(tm, w), _rows(tm, w), full3, pl.BlockSpec((SGU_GROUPS, SGU_CHUNK), lambda i: (0, 0)), _vec(w), _vec(w)),
                 ("arbitrary",), scratch=[pltpu.VMEM((SGU_GROUPS, SGU_CHUNK, SGU_CHUNK), F32)], after=after)(
                     u_pre, v_pre, ln_w, ln_b, wm, bfull, dy)


def _mix_out(name, y_sgu, y_attn, ga_pre, gb_pre, x, g1, proj_a, proj_b, w_out, w2, sc2, sh2, tm=512):
    s, d = x.shape
    tm = min(tm, s)

    def body(ys_ref, ya_ref, ga_ref, gb_ref, x_ref, g1_ref, wa_ref, wb_ref, wo_ref, w2_ref, sc2_ref, sh2_ref,
             m_ref, pa_ref, pb_ref, o_ref, x1_ref, h2_ref):
        pa = _dot(ys_ref[...], wa_ref[...].reshape(d, d))
        pb = _dot(ya_ref[...], wb_ref[...].reshape(d, d))
        pa_ref[...] = pa.astype(BF)
        pb_ref[...] = pb.astype(BF)
        merged = (_sigmoid(ga_ref[...].astype(F32)) * pa + _sigmoid(gb_ref[...].astype(F32)) * pb).astype(BF)
        m_ref[...] = merged
        o = _dot(merged, wo_ref[...].reshape(d, d))
        o_ref[...] = o.astype(BF)
        x1 = x_ref[...] + g1_ref[...] * o
        x1_ref[...] = x1
        h2_ref[...] = ((x1 * _rms(x1)) * w2_ref[...] * (1.0 + sc2_ref[...]) + sh2_ref[...]).astype(BF)

    f, b = _sds((s, d), F32), _sds((s, d), BF)
    r = _rows(tm, d)
    wspec = _resident(proj_a.shape)
    return _call(body, name, (b, b, b, b, f, b), (s // tm,),
                 [r, r, r, r, r, _vec(d), wspec, wspec, wspec, _vec(d), _vec(d), _vec(d)], (r,) * 6, ("parallel",))(
                     y_sgu, y_attn, ga_pre, gb_pre, x, g1, proj_a, proj_b, w_out, w2, sc2, sh2)


def _mix_bwd(name, do, w_out, proj_a, proj_b, ga_pre, gb_pre, pa, pb, tm=512):
    s, d = do.shape
    tm = min(tm, s)

    def body(do_ref, wo_ref, wa_ref, wb_ref, ga_ref, gb_ref, pa_ref, pb_ref,
             dpa_ref, dpb_ref, dga_ref, dgb_ref, dys_ref, dya_ref):
        dm = _dot_nt(do_ref[...], wo_ref[...].reshape(d, d))
        ga = _sigmoid(ga_ref[...].astype(F32))
        gb = _sigmoid(gb_ref[...].astype(F32))
        dpa = (dm * ga).astype(BF)
        dpb = (dm * gb).astype(BF)
        dpa_ref[...] = dpa
        dpb_ref[...] = dpb
        dga_ref[...] = (dm * pa_ref[...].astype(F32) * ga * (1.0 - ga)).astype(BF)
        dgb_ref[...] = (dm * pb_ref[...].astype(F32) * gb * (1.0 - gb)).astype(BF)
        dys_ref[...] = _dot_nt(dpa, wa_ref[...].reshape(d, d)).astype(BF)
        dya_ref[...] = _dot_nt(dpb, wb_ref[...].reshape(d, d)).astype(BF)

    f, b = _sds((s, d), F32), _sds((s, d), BF)
    r = _rows(tm, d)
    wspec = _resident(w_out.shape)
    return _call(body, name, (b, b, b, b, b, b), (s // tm,), [r, wspec, wspec, wspec, r, r, r, r], (r,) * 6,
                 ("parallel",))(do, w_out, proj_a, proj_b, ga_pre, gb_pre, pa, pb)


def _ffn_up_act(name, h2, w_gate, w_up, cw, cb, tm=1024):
    s, d = h2.shape
    tm = min(tm, s)
    tc = FFN_CHUNK
    per = tm // HALO

    def body(h_ref, hprev_ref, wg_ref, wu_ref, cw_ref, cb_ref, a_ref, ac_ref, up_ref, hf_ref):
        hv = h_ref[...]
        wg = wg_ref[...].reshape(tc, d)
        a = _dot_nt(hv, wg).astype(BF)
        up = _dot_nt(hv, wu_ref[...].reshape(tc, d)).astype(BF)
        a_ref[...] = a
        up_ref[...] = up
        prev = jnp.where(pl.program_id(1) > 0, _dot_nt(hprev_ref[...], wg).astype(BF).astype(F32), 0.0)
        ext = jnp.concatenate([prev, a.astype(F32)], axis=0)
        ac = (cb_ref[...] + cw_ref[0:1, :] * pltpu.roll(ext, 2, axis=0) + cw_ref[1:2, :] * pltpu.roll(ext, 1, axis=0)
              + cw_ref[2:3, :] * ext)[HALO:]
        ac_ref[...] = ac.astype(BF)
        hf_ref[...] = (ac * _sigmoid(ac) * up.astype(F32)).astype(BF)

    wspec = pl.BlockSpec((CHIPS_PER_CHUNK, FFN_SHARD, d), lambda j, i: (j, 0, 0))
    ospec = pl.BlockSpec((tm, tc), lambda j, i: (i, j))
    o = _sds((s, FFN_DIM), BF)
    return _call(body, name, (o, o, o, o), (FFN_CHUNKS, s // tm),
                 [pl.BlockSpec((tm, d), lambda j, i: (i, 0)), pl.BlockSpec((HALO, d), lambda j, i: (jnp.maximum(i * per - 1, 0), 0)),
                  wspec, wspec, pl.BlockSpec((3, tc), lambda j, i: (0, j)), pl.BlockSpec((1, tc), lambda j, i: (0, j))],
                 (ospec, ospec, ospec, ospec), ("parallel", "parallel"))(h2, h2, w_gate, w_up, cw, cb)


def _ffn_down(name, hf, w_down, x1, g2, tm=512):
    s, d = x1.shape
    tm = min(tm, s)

    def body(hf_ref, wd_ref, x1_ref, g2_ref, dn_ref, x2_ref):
        dn = _dot(hf_ref[...], wd_ref[...].reshape(FFN_DIM, d))
        dn_ref[...] = dn.astype(BF)
        x2_ref[...] = x1_ref[...] + g2_ref[...] * dn

    return _call(body, name, (_sds((s, d), BF), _sds((s, d), F32)), (s // tm,),
                 [_rows(tm, FFN_DIM), _resident(w_down.shape), _rows(tm, d), _vec(d)],
                 (_rows(tm, d), _rows(tm, d)), ("parallel",))(hf, w_down, x1, g2)


def _ffn_down_bwd_act(name, dx2, dn, g2, w_down, a, ac, up, cw, tm=256, after=None):
    s, d = dx2.shape
    c = a.shape[1]
    tm = min(tm, s)
    tc = FFN_CHUNK
    per = tm // HALO
    steps = s // tm
    last = s // HALO - 1
    n = tm + HALO

    def body(dx_ref, dxnext_ref, dn_ref, g2_ref, wd_ref, a_ref, ac_ref, acnext_ref, up_ref, upnext_ref, cw_ref,
             ddn_ref, da_ref, dup_ref, dg_ref, dcw_ref, dcb_ref):
        i = pl.program_id(0)

        @pl.when(i == 0)
        def _():
            dg_ref[...] = jnp.zeros_like(dg_ref)
            dcw_ref[...] = jnp.zeros_like(dcw_ref)
            dcb_ref[...] = jnp.zeros_like(dcb_ref)

        dxv = dx_ref[...]
        ddn = (dxv * g2_ref[...]).astype(BF)
        ddn_ref[...] = ddn
        dg_ref[...] += jnp.sum(dxv * dn_ref[...].astype(F32), axis=0, keepdims=True)
        ddn_next = jnp.where(i < steps - 1, dxnext_ref[...] * g2_ref[...], 0.0).astype(BF)
        ddn_ext = jnp.concatenate([ddn, ddn_next], axis=0)
        for k in range(FFN_CHUNKS):
            cols = slice(k * tc, (k + 1) * tc)
            dh = _dot_nt(ddn_ext, wd_ref[k * CHIPS_PER_CHUNK:(k + 1) * CHIPS_PER_CHUNK].reshape(tc, d))
            ace = jnp.concatenate([ac_ref[:, cols].astype(F32), acnext_ref[:, cols].astype(F32)], axis=0)
            upe = jnp.concatenate([up_ref[:, cols].astype(F32), upnext_ref[:, cols].astype(F32)], axis=0)
            sig = _sigmoid(ace)
            silu = ace * sig
            dac = dh * upe * (sig + silu * (1.0 - sig))
            dup_ref[:, cols] = (dh[:tm] * silu[:tm]).astype(BF)
            d1 = pltpu.roll(dac, n - 1, axis=0)[:tm]
            d2 = pltpu.roll(dac, n - 2, axis=0)[:tm]
            d0 = dac[:tm]
            da_ref[:, cols] = (cw_ref[2:3, cols] * d0 + cw_ref[1:2, cols] * d1 + cw_ref[0:1, cols] * d2).astype(BF)
            a0 = a_ref[:, cols].astype(F32)
            dcb_ref[:, cols] += jnp.sum(d0, axis=0, keepdims=True)
            dcw_ref[0:1, cols] += jnp.sum(d2 * a0, axis=0, keepdims=True)
            dcw_ref[1:2, cols] += jnp.sum(d1 * a0, axis=0, keepdims=True)
            dcw_ref[2:3, cols] += jnp.sum(d0 * a0, axis=0, keepdims=True)

    nxt = lambda width: pl.BlockSpec((HALO, width), lambda i: (jnp.minimum((i + 1) * per, last), 0))
    wide = _sds((s, c), BF)
    return _call(body, name, (_sds((s, d), BF), wide, wide, _sds((1, d), F32), _sds((3, c), F32), _sds((1, c), F32)), (steps,),
                 [_rows(tm, d), nxt(d), _rows(tm, d), _vec(d), _resident(w_down.shape), _rows(tm, c), _rows(tm, c), nxt(c),
                  _rows(tm, c), nxt(c), pl.BlockSpec((3, c), lambda i: (0, 0))],
                 (_rows(tm, d), _rows(tm, c), _rows(tm, c), _vec(d), pl.BlockSpec((3, c), lambda i: (0, 0)), _vec(c)),
                 ("arbitrary",), after=after)(dx2, dx2, dn, g2, w_down, a, ac, ac, up, up, cw)


def _ffn_up_bwd(name, da, dup, w_gate, w_up, x1, dx2, w2, sc2, o, g1, tm=512, after=None):
    s, d = x1.shape
    tm = min(tm, s)

    def body(da_ref, dup_ref, wg_ref, wu_ref, x1_ref, dx2_ref, w2_ref, sc2_ref, o_ref, g1_ref,
             dx1_ref, do_ref, dnw_ref, dsh_ref, dg1_ref):
        @pl.when(pl.program_id(0) == 0)
        def _():
            dnw_ref[...] = jnp.zeros_like(dnw_ref)
            dsh_ref[...] = jnp.zeros_like(dsh_ref)
            dg1_ref[...] = jnp.zeros_like(dg1_ref)

        dh = _dot(da_ref[...], wg_ref[...].reshape(FFN_DIM, d)) + _dot(dup_ref[...], wu_ref[...].reshape(FFN_DIM, d))
        xv = x1_ref[...]
        r = _rms(xv)
        xn = xv * r
        dxn = dh * (w2_ref[...] * (1.0 + sc2_ref[...]))
        dx1 = dx2_ref[...] + r * (dxn - xn * jnp.mean(dxn * xn, axis=-1, keepdims=True))
        dx1_ref[...] = dx1
        dnw_ref[...] += jnp.sum(dh * xn, axis=0, keepdims=True)
        dsh_ref[...] += jnp.sum(dh, axis=0, keepdims=True)
        do_ref[...] = (dx1 * g1_ref[...]).astype(BF)
        dg1_ref[...] += jnp.sum(dx1 * o_ref[...].astype(F32), axis=0, keepdims=True)

    v = _sds((1, d), F32)
    r = _rows(tm, d)
    wspec = _resident(w_gate.shape)
    return _call(body, name, (_sds((s, d), F32), _sds((s, d), BF), v, v, v), (s // tm,),
                 [_rows(tm, FFN_DIM), _rows(tm, FFN_DIM), wspec, wspec, r, r, _vec(d), _vec(d), r, _vec(d)],
                 (r, r, _vec(d), _vec(d), _vec(d)), ("arbitrary",), after=after)(da, dup, w_gate, w_up, x1, dx2, w2, sc2, o, g1)


def _loss_head(name, x, w, target, tm=512):
    s, d = x.shape
    tm = min(tm, s)

    def body(x_ref, w_ref, t_ref, dx_ref, loss_ref, dw_ref):
        @pl.when(pl.program_id(0) == 0)
        def _():
            loss_ref[...] = jnp.zeros_like(loss_ref)
            dw_ref[...] = jnp.zeros_like(dw_ref)

        xv = x_ref[...]
        r = _rms(xv)
        xn = xv * r
        err = xn * w_ref[...] - t_ref[...]
        loss_ref[...] += 0.5 * jnp.sum(jnp.mean(err * err, axis=-1, keepdims=True))
        dy = err * (1.0 / d)
        dw_ref[...] += jnp.sum(dy * xn, axis=0, keepdims=True)
        dxn = dy * w_ref[...]
        dx_ref[...] = r * (dxn - xn * jnp.mean(dxn * xn, axis=-1, keepdims=True))

    return _call(body, name, (_sds((s, d), F32), _sds((1, LANES), F32), _sds((1, d), F32)), (s // tm,),
                 [_rows(tm, d), _vec(d), _rows(tm, d)], (_rows(tm, d), _vec(LANES), _vec(d)), ("arbitrary",))(x, w, target)


def _layer_fwd(l, x, mod, p, cosf, sinf, after=None, late=None, later=None):
    sh1, sc1, g1, sh2, sc2, g2 = mod
    tag = f"l{l}_"
    h, qr, kk0, kk1, vv0, vv1, u_pre, v_pre, ga_pre, gb_pre = _in_proj(
        tag + "in_proj", x, p["norm1_w"], sc1, sh1, p["w_in"], cosf, sinf, after=after)
    y_attn = _attention(tag + "attn", qr, kk0, kk1, vv0, vv1, p["sinks"])
    y_sgu = _sgu(tag + "sgu", u_pre, v_pre, p["sgu_ln_w"], p["sgu_ln_b"], p["sgu_w"], p["sgu_bfull"])
    if late is not None:
        p = dict(p, **late(y_sgu))
    merged, pa, pb, o, x1, h2 = _mix_out(tag + "mix_out", y_sgu, y_attn, ga_pre, gb_pre, x, g1, p["proj_a"], p["proj_b"],
                                         p["w_out"], p["norm2_w"], sc2, sh2)
    if later is not None:
        p = dict(p, **later(h2))
    a, ac, up, hf = _ffn_up_act(tag + "ffn_up", h2, p["w_gate"], p["w_up"], p["conv_w"], p["conv_b"])
    dn, x2 = _ffn_down(tag + "ffn_down", hf, p["w_down"], x1, g2)
    saved = dict(x=x, h=h, qr=qr, kk0=kk0, kk1=kk1, vv0=vv0, vv1=vv1, u_pre=u_pre, v_pre=v_pre, ga_pre=ga_pre,
                 gb_pre=gb_pre, y_attn=y_attn, y_sgu=y_sgu, merged=merged, pa=pa, pb=pb, o=o, x1=x1, h2=h2, a=a, ac=ac, up=up,
                 hf=hf, dn=dn)
    return x2, saved, p


def _layer_bwd(l, dx2, mod, p, sv, cosf, sinf, after=None, emit=None, tick=None):
    sh1, sc1, g1, sh2, sc2, g2 = mod
    tag = f"l{l}_b_"
    d = D_MODEL
    g = {}
    ready = (lambda names: emit({k: g.pop(k) for k in names})) if emit else (lambda names: None)
    tick = tick or (lambda y: None)
    ddn, da, dup, dg2, g["conv_w"], g["conv_b"] = _ffn_down_bwd_act(
        tag + "ffn_down", dx2, sv["dn"], g2, p["w_down"], sv["a"], sv["ac"], sv["up"], p["conv_w"], after=after)
    g["w_down"] = _matmul_tn(tag + "dw_down", sv["hf"], ddn, tk=FFN_CHUNK, after=tick(ddn)).reshape(N_CHIPS, FFN_SHARD, d)
    g["w_gate"] = _matmul_tn(tag + "dw_gate", da, sv["h2"], tk=FFN_CHUNK).reshape(N_CHIPS, FFN_SHARD, d)
    g["w_up"] = _matmul_tn(tag + "dw_up", dup, sv["h2"], tk=FFN_CHUNK).reshape(N_CHIPS, FFN_SHARD, d)
    dx1, do, da2, dsh2, dg1 = _ffn_up_bwd(tag + "ffn_up", da, dup, p["w_gate"], p["w_up"], sv["x1"], dx2, p["norm2_w"],
                                          sc2, sv["o"], g1, after=ready(("w_down", "w_gate", "w_up")))
    g["norm2_w"] = da2 * (1.0 + sc2)
    dsc2 = da2 * p["norm2_w"]
    g["w_out"] = _matmul_tn(tag + "dw_out", sv["merged"], do, after=tick(do)).reshape(N_CHIPS, d // N_CHIPS, d)
    dpa, dpb, dga, dgb, dy_sgu, dy_attn = _mix_bwd(tag + "mix", do, p["w_out"], p["proj_a"], p["proj_b"], sv["ga_pre"],
                                                  sv["gb_pre"], sv["pa"], sv["pb"])
    g["proj_a"] = _matmul_tn(tag + "dproj_a", sv["y_sgu"], dpa).reshape(N_CHIPS, d // N_CHIPS, d)
    g["proj_b"] = _matmul_tn(tag + "dproj_b", sv["y_attn"], dpb).reshape(N_CHIPS, d // N_CHIPS, d)
    du, dv, g["sgu_w"], g["sgu_b"], g["sgu_ln_w"], g["sgu_ln_b"] = _sgu_bwd(
        tag + "sgu", sv["u_pre"], sv["v_pre"], p["sgu_ln_w"], p["sgu_ln_b"], p["sgu_w"], p["sgu_bfull"], dy_sgu,
        after=ready(("w_out", "proj_a", "proj_b")))
    dqr, dkv_cur, dkv_prev, dsink = _attention_bwd(tag + "attn", sv["qr"], sv["kk0"], sv["kk1"], sv["vv0"], sv["vv1"],
                                                   p["sinks"], dy_attn, after=tick(du))
    g["sinks"] = dsink[0, :N_Q_HEADS]
    dq, dkv = _rope_bwd(tag + "rope", dqr, dkv_cur, dkv_prev, cosf, sinf)
    dw_in = _matmul_tn_segments(tag + "dw_in", [dq, dkv, du, dv, dga, dgb], sv["h"])
    g["w_in"] = dw_in.reshape(N_CHIPS, IN_COLS // N_CHIPS, d)
    dx, da1, dsh1 = _in_proj_bwd(tag + "in_proj", dq, dkv, du, dv, dga, dgb, p["w_in"], sv["x"], p["norm1_w"], sc1, dx1)
    g["norm1_w"] = da1 * (1.0 + sc1)
    dsc1 = da1 * p["norm1_w"]
    return dx, (dsh1, dsc1, dg1, dsh2, dsc2, dg2), g


def _pad_to(a, axis, size):
    pad = [(0, 0)] * a.ndim
    pad[axis] = (0, size - a.shape[axis])
    return jnp.pad(a, pad)


def _early_params(w_in, small):
    d = D_MODEL
    return dict(
        w_in=w_in.reshape(IN_COLS, d), norm1_w=small["norm1_w"].reshape(1, d), sinks=small["sinks"],
        sgu_ln_w=small["sgu_ln_w"].reshape(1, d), sgu_ln_b=small["sgu_ln_b"].reshape(1, d), sgu_w=small["sgu_w"],
        sgu_bfull=jnp.broadcast_to(small["sgu_b"][:, :, None], (SGU_GROUPS, SGU_CHUNK, SGU_CHUNK)))


def _mix_params(proj_a, proj_b, w_out, small):
    return dict(proj_a=proj_a, proj_b=proj_b, w_out=w_out, norm2_w=small["norm2_w"].reshape(1, D_MODEL))


def _ffn_params(w_gate, w_up, w_down, conv_w, small):
    return dict(
        w_gate=w_gate, w_up=w_up, w_down=w_down, conv_w=conv_w.transpose(1, 0, 2).reshape(3, FFN_DIM),
        conv_b=small["conv_b"].reshape(1, FFN_DIM))


def _layer_params(w_in, proj_a, proj_b, w_out, w_gate, w_up, w_down, conv_w, small):
    return dict(_early_params(w_in, small), **_mix_params(proj_a, proj_b, w_out, small),
                **_ffn_params(w_gate, w_up, w_down, conv_w, small))


def _conv_grads_natural(g):
    cw = g["conv_w"]
    cb = g["conv_b"].reshape(FFN_DIM)
    return cw, cb


def _rope_tables(positions):
    inv_freq = ROPE_THETA ** (-jnp.arange(0, ROT_DIM, 2, dtype=F32) / ROT_DIM)
    ang = positions.astype(F32)[:, None] * inv_freq
    cos, sin = jnp.cos(ang), jnp.sin(ang)
    s = positions.shape[0]
    rest = HEAD_DIM - ROT_DIM
    cos_head = jnp.concatenate([cos, cos, jnp.ones((s, rest), F32)], axis=1)
    sin_head = jnp.concatenate([-sin, sin, jnp.zeros((s, rest), F32)], axis=1)
    return jnp.tile(cos_head, (1, LANES // HEAD_DIM)), jnp.tile(sin_head, (1, LANES // HEAD_DIM))


ADA_ROWS = 16


def _ada_fwd(name, c_rows, ada_w, ada_b_cols, tn=512):
    depth, d, n = ada_w.shape

    def body(c_ref, w_ref, b_ref, o_ref):
        cv = c_ref[...]
        act = (cv * _sigmoid(cv)).astype(BF)
        o_ref[0] = _dot(act, w_ref[0].astype(BF)) + b_ref[0]

    return _call(body, name, _sds((depth, ADA_ROWS, n), F32), (depth, n // tn),
                 [pl.BlockSpec((ADA_ROWS, d), lambda l, j: (0, 0)), pl.BlockSpec((1, d, tn), lambda l, j: (l, 0, j)),
                  pl.BlockSpec((1, 1, tn), lambda l, j: (l, 0, j))],
                 pl.BlockSpec((1, ADA_ROWS, tn), lambda l, j: (l, 0, j)), ("parallel", "parallel"))(c_rows, ada_w, ada_b_cols)


def _ada_bwd(name, c_rows, dmod_cols, tn=512):
    depth, _, n = dmod_cols.shape
    d = c_rows.shape[1]

    def body(c_ref, dm_ref, o_ref):
        cv = c_ref[...]
        act = (cv * _sigmoid(cv)).astype(BF)
        o_ref[0] = _dot_tn(act, dm_ref[0].astype(BF))

    return _call(body, name, _sds((depth, d, n), F32), (depth, n // tn),
                 [pl.BlockSpec((ADA_ROWS, d), lambda l, j: (0, 0)), pl.BlockSpec((1, ADA_ROWS, tn), lambda l, j: (l, 0, j))],
                 pl.BlockSpec((1, d, tn), lambda l, j: (l, 0, j)), ("parallel", "parallel"))(c_rows, dmod_cols)


def _colsum(name, a):
    r, n = a.shape

    def body(a_ref, o_ref):
        o_ref[...] = jnp.sum(a_ref[...], axis=0, keepdims=True)

    return _call(body, name, _sds((1, n), F32), (1,), [pl.BlockSpec((r, n), lambda i: (0, 0))],
                 pl.BlockSpec((1, n), lambda i: (0, 0)), ("arbitrary",))(a)


REL_SIBLING = (0, 0, 1)
REL_CHIPS = ((1, 0, 0), (0, 1, 0), (1, 1, 0))
REL_ALL = tuple((fx, fy, fc) for fx in (0, 1) for fy in (0, 1) for fc in (0, 1) if fx or fy or fc)


def _chip_of(dev):
    return 2 * dev[0] + dev[1]


def _dev_of(dev):
    return 4 * dev[0] + 2 * dev[1] + dev[2]


def _flip(dev, rel):
    return tuple(1 - m if f else m for m, f in zip(dev, rel))


def _exchange(name, arrays, n_out, stages, aliases=None):
    out_shapes, stages = stages[0], stages[1:]
    n_in = len(arrays)
    aliases = aliases or {}
    n_remote = sum(len(plan) for plan, _ in stages)
    n_local = sum(len(local) for _, local in stages)

    def at(ref, idx):
        return ref.at[idx] if len(idx) else ref

    def body(*refs):
        bufs = list(refs[:n_in + n_out])
        for i_in, i_out in aliases.items():
            bufs[i_in] = bufs[n_in + i_out]
        send_sems, recv_sems, local_sems = refs[n_in + n_out:]
        me = (lax.axis_index("x"), lax.axis_index("y"), lax.axis_index("c"))
        base_r = base_l = 0
        pending = []
        for plan, local in stages:
            def remote(k, entry, sender, receiver):
                rel, si, ssel, di, dsel = entry
                return pltpu.make_async_remote_copy(
                    src_ref=at(bufs[si], ssel(sender, receiver)), dst_ref=at(bufs[di], dsel(sender, receiver)),
                    send_sem=send_sems.at[k], recv_sem=recv_sems.at[k], device_id=_flip(me, rel), device_id_type=MESH)

            sends = [remote(base_r + k, e, me, _flip(me, e[0])) for k, e in enumerate(plan)]
            for cp in sends:
                cp.start()
            for k, (si, ssel, di, dsel) in enumerate(local):
                cp = pltpu.make_async_copy(at(bufs[si], ssel(me)), at(bufs[di], dsel(me)), local_sems.at[base_l + k])
                cp.start()
                pending.append(cp.wait)
            for k, e in enumerate(plan):
                remote(base_r + k, e, _flip(me, e[0]), me).wait_recv()
            pending += [cp.wait_send for cp in sends]
            base_r += len(plan)
            base_l += len(local)
        for wait in pending:
            wait()

    any_spec = pl.BlockSpec(memory_space=pl.ANY)
    return pl.pallas_call(
        body, name=name, out_shape=tuple(out_shapes), in_specs=[any_spec] * n_in, out_specs=tuple([any_spec] * n_out),
        input_output_aliases=dict(aliases),
        scratch_shapes=[pltpu.SemaphoreType.DMA((max(n_remote, 1),)), pltpu.SemaphoreType.DMA((max(n_remote, 1),)),
                        pltpu.SemaphoreType.DMA((max(n_local, 1),))])(*arrays)


HBM_SPEC = pl.BlockSpec(memory_space=pltpu.HBM)
SEM_SPEC = pl.BlockSpec(memory_space=pltpu.SEMAPHORE)


def _split_copies(bufs, plan, local, send_sems, recv_sems, local_sems):
    me = (lax.axis_index("x"), lax.axis_index("y"), lax.axis_index("c"))

    def at(ref, idx):
        return ref.at[idx] if len(idx) else ref

    def remote(k, sender, receiver):
        rel, si, ssel, di, dsel = plan[k]
        return pltpu.make_async_remote_copy(
            src_ref=at(bufs[si], ssel(sender, receiver)), dst_ref=at(bufs[di], dsel(sender, receiver)),
            send_sem=send_sems.at[k], recv_sem=recv_sems.at[k], device_id=_flip(me, rel), device_id_type=MESH)

    sends = [remote(k, me, _flip(me, plan[k][0])) for k in range(len(plan))]
    arrivals = [remote(k, _flip(me, plan[k][0]), me) for k in range(len(plan))]
    locs = [pltpu.make_async_copy(at(bufs[si], ssel(me)), at(bufs[di], dsel(me)), local_sems.at[k])
            for k, (si, ssel, di, dsel) in enumerate(local)]
    return sends, arrivals, locs


def _exchange_start_many(name, stages):
    sizes = [(len(arrays), len(out_shapes)) for _, arrays, out_shapes, _, _, _ in stages]
    n_buf = sum(n_in + n_out for n_in, n_out in sizes)
    n_sem = 3 * len(stages)

    def body(*refs):
        sems, bufs = refs[n_buf:n_buf + n_sem], refs[n_buf + n_sem:2 * n_buf + n_sem]
        at = 0
        for g, ((n_in, n_out), (_, _, _, plan, local, inplace)) in enumerate(zip(sizes, stages)):
            mine = bufs[at:at + n_in + n_out]
            sends, _, locs = _split_copies(mine * 2 if inplace else mine, plan, local, *sems[3 * g:3 * g + 3])
            for cp in sends + locs:
                cp.start()
            at += n_in + n_out
        refs[-1][...] = jnp.zeros_like(refs[-1])

    operands, sem_shapes = [], []
    for _, arrays, out_shapes, plan, local, _ in stages:
        operands += list(arrays) + [lax.empty(o.shape, o.dtype) for o in out_shapes]
        sem_shapes += [pltpu.SemaphoreType.DMA((max(k, 1),)) for k in (len(plan), len(plan), len(local))]
    operands = [pltpu.with_memory_space_constraint(a, pltpu.HBM) for a in operands]
    out = pl.pallas_call(
        body, name=name,
        out_shape=(*sem_shapes, *[pltpu.HBM(a.shape, a.dtype) for a in operands], _sds((SUBLANES, LANES), F32)),
        in_specs=[HBM_SPEC] * n_buf,
        out_specs=(*[SEM_SPEC] * n_sem, *[HBM_SPEC] * n_buf, pl.BlockSpec(memory_space=pltpu.VMEM)),
        input_output_aliases={i: n_sem + i for i in range(n_buf)},
        compiler_params=pltpu.CompilerParams(has_side_effects=pltpu.SideEffectType.DATAFLOW_SIDE_EFFECTING))(*operands)
    pendings, at = [], n_sem
    for g, ((n_in, n_out), (stage, _, _, plan, local, inplace)) in enumerate(zip(sizes, stages)):
        pendings.append(dict(name=stage, sems=out[3 * g:3 * g + 3], thru=out[at:at + n_in], zones=out[at + n_in:at + n_in + n_out],
                             plan=plan, local=local, inplace=inplace))
        at += n_in + n_out
    return pendings, out[-1]


def _exchange_start(name, arrays, out_shapes, plan, local, inplace=False):
    pendings, token = _exchange_start_many(name, [(name, arrays, out_shapes, plan, local, inplace)])
    return pendings[0], token


def _exchange_wait(pending, after, both=False):
    thru, zones, plan, local, inplace = (pending[k] for k in ("thru", "zones", "plan", "local", "inplace"))
    n_in, n_buf = len(thru), len(thru) + len(zones)

    def body(*refs):
        bufs = refs[:n_buf]
        sends, arrivals, locs = _split_copies(bufs * 2 if inplace else bufs, plan, local, *refs[n_buf:n_buf + 3])
        for cp in arrivals:
            cp.wait_recv()
        for cp in sends:
            cp.wait_send()
        for cp in locs:
            cp.wait()

    out = pl.pallas_call(
        body, name=pending["name"] + "_wait", out_shape=tuple(pltpu.HBM(a.shape, a.dtype) for a in list(thru) + list(zones)),
        in_specs=[HBM_SPEC] * n_buf + [SEM_SPEC] * 3 + [pl.BlockSpec(memory_space=pl.ANY)],
        out_specs=tuple([HBM_SPEC] * n_buf), input_output_aliases={i: i for i in range(n_buf)},
        compiler_params=pltpu.CompilerParams(has_side_effects=pltpu.SideEffectType.DATAFLOW_SIDE_EFFECTING))(
            *thru, *zones, *pending["sems"], after)
    if both:
        return out[:n_in], out[n_in:]
    return out[:n_in] if inplace else out[n_in:]


def _whole(*_):
    return ()


def _half_rows(rows, core):
    return pl.ds(core * (rows // 2), rows // 2)


def _gather_weights_plan(shards):
    n = len(shards)
    dsts = [_sds((N_CHIPS,) + a.shape, a.dtype) for a in shards]
    fetch, forward = [], []
    for t, a in enumerate(shards):
        rows = a.shape[0]
        if rows % (2 * 16) == 0:
            fetch += [(rel, t, (lambda s_, r_, rows=rows: (_half_rows(rows, s_[2]),)), n + t,
                       (lambda s_, r_, rows=rows: (_chip_of(s_), _half_rows(rows, s_[2])))) for rel in REL_CHIPS]
            forward += [(REL_SIBLING, n + t, (lambda s_, r_, rows=rows, rel=rel: (_chip_of(_flip(s_, rel)), _half_rows(rows, s_[2]))),
                         n + t, (lambda s_, r_, rows=rows, rel=rel: (_chip_of(_flip(s_, rel)), _half_rows(rows, s_[2]))))
                        for rel in REL_CHIPS]
        else:
            fetch += [(rel, t, _whole, n + t, lambda s_, r_: (_chip_of(s_),)) for rel in REL_CHIPS]
    local = [(t, _whole, n + t, lambda me: (_chip_of(me),)) for t in range(n)]
    return dsts, fetch, local, forward


def _gather_weights_finish(pending, after):
    landed = _exchange_wait(pending, after)
    n = len(landed)
    return _exchange(pending["name"] + "_forward", landed, n, [[_sds(a.shape, a.dtype) for a in landed], (pending["forward"], [])],
                     aliases={t: t for t in range(n)})


def _gather_chips_plan(arrays):
    n = len(arrays)
    dsts = [_sds((N_CHIPS,) + a.shape, a.dtype) for a in arrays]
    plan = [(rel, t, _whole, n + t, lambda s_, r_: (_chip_of(s_),)) for t in range(n) for rel in REL_CHIPS]
    local = [(t, _whole, n + t, lambda me: (_chip_of(me),)) for t in range(n)]
    return dsts, plan, local


def _gather_chips(name, arrays):
    dsts, plan, local = _gather_chips_plan(arrays)
    return _exchange(name, arrays, len(arrays), [dsts, (plan, local)])


def _gather_chips_start(name, arrays):
    dsts, plan, local = _gather_chips_plan(arrays)
    return _exchange_start(name, arrays, dsts, plan, local)


def _gather_all(name, a):
    plan = [(rel, 0, _whole, 1, lambda s_, r_: (_dev_of(s_),)) for rel in REL_ALL]
    local = [(0, _whole, 1, lambda me: (_dev_of(me),))]
    return _exchange(name, [a], 1, [[_sds((2 * N_CHIPS,) + a.shape, a.dtype)], (plan, local)])[0]


def _swap_halves_start(name, grads):
    n = len(grads)
    dsts = [_sds((g.shape[0], g.shape[1] // 2, g.shape[2]), g.dtype) for g in grads]
    plan = [(REL_SIBLING, t, (lambda s_, r_, rows=g.shape[1]: (pl.ds(0, N_CHIPS), _half_rows(rows, r_[2]))), n + t, _whole)
            for t, g in enumerate(grads)]
    return _exchange_start(name, grads, dsts, plan, [])


def _scatter_chips_plan(sums):
    n = len(sums)
    dsts = [_sds(a.shape, a.dtype) for a in sums]
    plan = [(rel, t, lambda s_, r_: (_chip_of(r_),), n + t, lambda s_, r_: (_chip_of(s_),))
            for t in range(n) for rel in REL_CHIPS]
    local = [(t, lambda me: (_chip_of(me),), n + t, lambda me: (_chip_of(me),)) for t in range(n)]
    return dsts, plan, local


def _scatter_chips_start(name, sums):
    dsts, plan, local = _scatter_chips_plan(sums)
    return _exchange_start(name, sums, dsts, plan, local)


def _swap_back_start(name, totals, layer):
    n = len(totals)
    plan = [(REL_SIBLING, n + t, (lambda s_, r_, rows=a.shape[1]: (layer, _half_rows(rows, s_[2]))),
             n + t, (lambda s_, r_, rows=a.shape[1]: (layer, _half_rows(rows, s_[2])))) for t, a in enumerate(totals)]
    return _exchange_start(name, totals, [], plan, [], inplace=True)


def _add_halves(name, gs, recvs, core):
    n = len(gs)
    nch = recvs[0].shape[0]

    def body(core_ref, *refs):
        for g_ref, r_ref, o_ref in zip(refs[:n], refs[n:2 * n], refs[2 * n:]):
            o_ref[0] = (g_ref[0, 0].astype(F32) + r_ref[0].astype(F32)).astype(o_ref.dtype)

    halves = [pl.BlockSpec((1,) + r.shape[1:], lambda k, core_ref: (k, 0, 0)) for r in recvs]
    spec = pltpu.PrefetchScalarGridSpec(
        num_scalar_prefetch=1, grid=(nch,),
        in_specs=[pl.BlockSpec((1, 1) + r.shape[1:], lambda k, core_ref: (k, core_ref[0], 0, 0)) for r in recvs] + halves,
        out_specs=tuple(halves))
    return pl.pallas_call(body, name=name, out_shape=tuple(_sds(r.shape, r.dtype) for r in recvs), grid_spec=spec,
                          compiler_params=pltpu.CompilerParams(dimension_semantics=("parallel",),
                                                               vmem_limit_bytes=VMEM_LIMIT))(
                                                                   core, *[g.reshape(nch, 2, *r.shape[1:]) for g, r in zip(gs, recvs)],
                                                                   *recvs)


def _sum_chips(name, arrays, core, layer, totals):
    n = len(arrays)
    nch = arrays[0].shape[0]
    parts = 2
    shapes = [(a.shape[1] // parts, a.shape[2]) for a in arrays]

    def body(core_ref, *refs):
        for a_ref, o_ref in zip(refs[:n], refs[-n:]):
            acc = a_ref[0].astype(F32)
            for k in range(1, nch):
                acc = acc + a_ref[k].astype(F32)
            o_ref[0, 0, 0] = acc

    in_specs = [pl.BlockSpec((nch,) + shp, lambda i, core_ref: (0, i, 0)) for shp in shapes]
    args = [core, *arrays]
    if totals is not None:
        in_specs += [pl.BlockSpec(memory_space=pl.ANY)] * n
        args += [t.reshape(DEPTH, 2, parts, *shp) for t, shp in zip(totals, shapes)]
    spec = pltpu.PrefetchScalarGridSpec(
        num_scalar_prefetch=1, grid=(parts,), in_specs=in_specs,
        out_specs=tuple(pl.BlockSpec((1, 1, 1) + shp, lambda i, core_ref: (layer, core_ref[0], i, 0, 0)) for shp in shapes))
    outs = pl.pallas_call(body, name=name, out_shape=tuple(_sds((DEPTH, 2, parts) + shp, F32) for shp in shapes), grid_spec=spec,
                          input_output_aliases={1 + n + t: t for t in range(n)} if totals is not None else {},
                          compiler_params=pltpu.CompilerParams(dimension_semantics=("arbitrary",),
                                                               vmem_limit_bytes=VMEM_LIMIT))(*args)
    return [o.reshape(DEPTH, 2 * parts * shp[0], shp[1]) for o, shp in zip(outs, shapes)]


def _adamw_update(w, g, m, v):
    mn = ADAM_B1 * m + (1.0 - ADAM_B1) * g
    vn = ADAM_B2 * v + (1.0 - ADAM_B2) * (g * g)
    m_hat = mn / (1.0 - ADAM_B1 ** ADAM_STEP)
    v_hat = vn / (1.0 - ADAM_B2 ** ADAM_STEP)
    return -ADAM_LR * (m_hat / (jnp.sqrt(v_hat) + ADAM_EPS) + ADAM_WD * w), mn, vn


def _adamw(name, w, g, m, v, after=None):
    depth, r, c = w.shape
    tr = next(t for t in (512, 448, 384, 352, 336, 256, 192, 128, 64, 32, 16, 8) if r % t == 0 and t * c <= ADAM_TILE_ELEMS)

    def body(w_ref, g_ref, m_ref, v_ref, go_ref, d_ref, mo_ref, vo_ref):
        gv = g_ref[...]
        go_ref[...] = gv
        d_ref[...], mo_ref[...], vo_ref[...] = _adamw_update(w_ref[...], gv, m_ref[...], v_ref[...])

    spec = pl.BlockSpec((1, tr, c), lambda l, i: (l, i, 0))
    o = _sds(w.shape, F32)
    return _call(body, name, (o, o, o, o), (depth, r // tr), [spec] * 4, (spec,) * 4, ("parallel", "parallel"),
                 after=after)(w, g, m, v)


def _adamw_small(name, ws, gs, ms, vs):
    n = len(ws)

    def body(*refs):
        for t in range(n):
            w_ref, g_ref, m_ref, v_ref = (refs[k * n + t] for k in range(4))
            d_ref, mo_ref, vo_ref = (refs[(4 + k) * n + t] for k in range(3))
            d_ref[...], mo_ref[...], vo_ref[...] = _adamw_update(w_ref[...], g_ref[...], m_ref[...], v_ref[...])

    outs = [_sds(w.shape, F32) for w in ws]
    res = pl.pallas_call(body, name=name, out_shape=tuple(outs * 3))(*ws, *gs, *ms, *vs)
    return res[:n], res[n:2 * n], res[2 * n:]


def _pack(arrays, rows):
    flat = jnp.concatenate([a.reshape(-1).astype(F32) for a in arrays])
    return _pad_to(flat, 0, rows * LANES).reshape(rows, LANES)


def _unpack(packed, shapes):
    flat = packed.reshape(-1)
    out, off = [], 0
    for shp in shapes:
        n = 1
        for s_ in shp:
            n *= s_
        out.append(flat[off:off + n].reshape(shp))
        off += n
    return out


_MATRICES = ("w_in", "proj_a", "proj_b", "w_out", "w_gate", "w_up", "w_down")
_SMALL = (("norm1_w", (D_MODEL,)), ("sinks", (N_Q_HEADS,)), ("sgu_ln_w", (SGU_WIDTH,)), ("sgu_ln_b", (SGU_WIDTH,)),
          ("sgu_w", (SGU_GROUPS, SGU_CHUNK, SGU_CHUNK)), ("sgu_b", (SGU_GROUPS, SGU_CHUNK)), ("norm2_w", (D_MODEL,)),
          ("conv_w", (3, FFN_DIM)), ("conv_b", (FFN_DIM,)), ("final_norm_w", (D_MODEL,)))
SMALL_ROWS = 320
ADAM_TILE_ELEMS = 384 * 1024


def _reduce_cores_start(tag, partial):
    names = list(partial)
    pending, token = _swap_halves_start(tag + "_cores", [partial[k] for k in names])
    return dict(pending, tag=tag, names=names), token


def _reduce_chips_start(pending, core, after):
    tag, names = pending["tag"], pending["names"]
    mine, theirs = _exchange_wait(pending, after, both=True)
    sums = _add_halves(tag + "_cores_add", mine, theirs, core)
    scatter, token = _scatter_chips_start(tag + "_chips", sums)
    return dict(scatter, tag=tag, names=names), token


def _reduce_back_start(pending, l, core, totals, after):
    tag, names = pending["tag"], pending["names"]
    before = [totals[k] for k in names] if names[0] in totals else None
    sums = _sum_chips(tag + "_chips_add", _exchange_wait(pending, after), core, l, before)
    back, _ = _swap_back_start(tag + "_back", sums, l)
    return dict(back, names=names)


def kernel(x, c, positions, ada_w, ada_b, norm1_w, w_in, attn_sinks, sgu_ln_w, sgu_ln_b, sgu_w, sgu_b, proj_a, proj_b, w_out, norm2_w, ffn_w_gate, ffn_w_up, ffn_conv_w, ffn_conv_b, ffn_w_down, final_norm_w, loss_target, m_ada_w, m_ada_b, m_norm1_w, m_w_in, m_attn_sinks, m_sgu_ln_w, m_sgu_ln_b, m_sgu_w, m_sgu_b, m_proj_a, m_proj_b, m_w_out, m_norm2_w, m_ffn_w_gate, m_ffn_w_up, m_ffn_conv_w, m_ffn_conv_b, m_ffn_w_down, m_final_norm_w, v_ada_w, v_ada_b, v_norm1_w, v_w_in, v_attn_sinks, v_sgu_ln_w, v_sgu_ln_b, v_sgu_w, v_sgu_b, v_proj_a, v_proj_b, v_w_out, v_norm2_w, v_ffn_w_gate, v_ffn_w_up, v_ffn_conv_w, v_ffn_conv_b, v_ffn_w_down, v_final_norm_w):
    d = D_MODEL
    ax, ay, ac = lax.axis_index("x"), lax.axis_index("y"), lax.axis_index("c")
    chip = 2 * ax + ay
    dev = 4 * ax + 2 * ay + ac
    core = ac.astype(jnp.int32).reshape(1)

    c_all = _gather_all("gather_cond", c.reshape(SUBLANES, d // SUBLANES)).reshape(2 * N_CHIPS, d)
    c_rows = _pad_to(c_all, 0, ADA_ROWS)
    ada_cols = ada_w.shape[2]
    ada_b_cols = lax.dynamic_slice_in_dim(ada_b, chip * ada_cols, ada_cols, axis=1).reshape(DEPTH, 1, ada_cols)
    mod_cols = _ada_fwd("ada_fwd", c_rows, ada_w, ada_b_cols)
    mod_all = _gather_chips("gather_mod", [mod_cols])[0]
    mod_mine = lax.dynamic_index_in_dim(mod_all, dev, axis=2, keepdims=False)
    mod_mine = mod_mine.transpose(1, 0, 2).reshape(DEPTH, 1, 6 * d)
    mods = [tuple(jnp.split(mod_mine[l], 6, axis=-1)) for l in range(DEPTH)]

    tr = lambda a: jnp.swapaxes(a, 1, 2)
    shards = [tr(w_in).astype(BF), proj_a.astype(BF), proj_b.astype(BF), w_out.astype(BF),
              tr(ffn_w_gate).astype(BF), tr(ffn_w_up).astype(BF), ffn_w_down.astype(BF), ffn_conv_w]
    stages, forwards = [], []
    for l in range(DEPTH):
        for tag, members in (("in", shards[:1]), ("mix", shards[1:4]), ("ffn", shards[4:])):
            group = [a[l] for a in members]
            if not stages:
                group[0] = group[0] + (mod_all[0, 0, 0, 0] * 0.0).astype(BF)
            dsts, fetch, local, forward = _gather_weights_plan(group)
            stages.append((f"l{l}_gather_{tag}", group, dsts, fetch, local, False))
            forwards.append(forward)
    first, token = _exchange_start_many("gather_weights_first", stages[:1])
    name, group, *rest = stages[1]
    stages[1] = (name, [group[0] + token[0, 0].astype(BF)] + group[1:], *rest)
    later, token = _exchange_start_many("gather_weights", stages[1:])
    pendings = [dict(p, forward=f) for p, f in zip(first + later, forwards)]
    fetches = [pendings[3 * l:3 * l + 3] for l in range(DEPTH)]

    small_in = dict(norm1_w=norm1_w, sinks=attn_sinks, sgu_ln_w=sgu_ln_w, sgu_ln_b=sgu_ln_b, sgu_w=sgu_w, sgu_b=sgu_b,
                    norm2_w=norm2_w, conv_b=ffn_conv_b)
    cosf, sinf = _rope_tables(positions[0])
    small_of = lambda l: {k: v[l] for k, v in small_in.items()}

    h = x[0]
    saved, params = [], []
    for l in range(DEPTH):
        first, mix, ffn = fetches[l]
        w_in_l = _gather_weights_finish(first, token if l == 0 else h)
        late = lambda y, l=l, mix=mix: _mix_params(*_gather_weights_finish(mix, y), small_of(l))
        later = lambda y, l=l, ffn=ffn: _ffn_params(*_gather_weights_finish(ffn, y), small_of(l))
        h, sv, p = _layer_fwd(l, h, mods[l], _early_params(w_in_l[0], small_of(l)), cosf, sinf, late=late, later=later)
        saved.append(sv)
        params.append(p)
    dx, loss_part, d_final = _loss_head("loss_head", h, final_norm_w.reshape(1, d), loss_target[0])
    loss = lax.psum(loss_part[0, 0], ("x", "y", "c"))

    def small_pack(l, grads):
        cw, cb = _conv_grads_natural(grads)
        nat = dict(grads, conv_w=cw, conv_b=cb, final_norm_w=d_final if l == DEPTH - 1 else jnp.zeros((d,), F32))
        return _pack([nat[k] for k, _ in _SMALL], N_CHIPS * SMALL_ROWS).reshape(N_CHIPS, SMALL_ROWS, LANES)

    waiting, inflight = [], []

    def send(tag, partial):
        pending, token = _reduce_cores_start(tag, partial)
        waiting.append(pending)
        return token

    def tick(y):
        token = None
        while waiting:
            pending, token = _reduce_chips_start(waiting.pop(0), core, y)
            inflight.append(pending)
        return token

    dmods = [None] * DEPTH
    dx, dmods[1], grads = _layer_bwd(1, dx, mods[1], params[1], saved[1], cosf, sinf)
    token = send("l1_reduce", dict({k: grads[k] for k in _MATRICES}, small=small_pack(1, grads)))
    dx, dmods[0], grads = _layer_bwd(0, dx, mods[0], params[0], saved[0], cosf, sinf, after=token,
                                     emit=lambda part: send("l0_reduce_" + "_".join(part), part), tick=tick)
    dmod_mine = jnp.concatenate([jnp.concatenate(dmods[l], axis=1) for l in range(DEPTH)], axis=1)
    dmod_all = _gather_all("gather_dmod", dmod_mine.reshape(SUBLANES, -1)).reshape(2 * N_CHIPS, DEPTH * 6 * d)
    send("l0_reduce_in", dict(w_in=grads["w_in"], small=small_pack(0, grads) + dmod_all[0, 0] * 0.0))
    tick(dmod_all)

    totals, flying = {}, None

    def land(after):
        if flying is not None:
            totals.update(zip(flying["names"], _exchange_wait(flying, after)))

    for pending in inflight[:-1]:
        land(dx)
        flying = _reduce_back_start(pending, 1 if pending["tag"].startswith("l1") else 0, core, totals, dx)
    land(dx)
    flying = None

    g_ada_b = _colsum("ada_b_grad", dmod_all).reshape(DEPTH, 6 * d)
    dmod_cols = jnp.stack([lax.dynamic_slice_in_dim(dmod_all, l * 6 * d + chip * ada_cols, ada_cols, axis=1)
                           for l in range(DEPTH)])
    g_ada_w = _ada_bwd("ada_w_grad", c_rows, _pad_to(dmod_cols, 1, ADA_ROWS))
    big = dict(w_in=(tr(w_in), tr(m_w_in), tr(v_w_in)), proj_a=(proj_a, m_proj_a, v_proj_a), proj_b=(proj_b, m_proj_b, v_proj_b),
               w_out=(w_out, m_w_out, v_w_out), w_gate=(tr(ffn_w_gate), tr(m_ffn_w_gate), tr(v_ffn_w_gate)),
               w_up=(tr(ffn_w_up), tr(m_ffn_w_up), tr(v_ffn_w_up)), w_down=(ffn_w_down, m_ffn_w_down, v_ffn_w_down))
    upd, g_big = {}, {}

    def update(k, after=None):
        res = _adamw("adamw_" + k, big[k][0], totals[k], *big[k][1:], after=after)
        res = [tr(a) for a in res] if k in ("w_in", "w_gate", "w_up") else res
        g_big[k], upd[k] = res[0], res[1:]

    for k in ("w_down", "w_gate", "w_up", "w_out", "proj_a", "proj_b"):
        update(k)
    g_big["ada_w"], *upd["ada_w"] = _adamw("adamw_ada_w", ada_w, g_ada_w, m_ada_w, v_ada_w)
    flying = _reduce_back_start(inflight[-1], 0, core, totals, upd["ada_w"][0])
    land(upd["ada_w"][1])
    gathering, token = _gather_chips_start("gather_small", [totals["small"]])
    update("w_in", after=token)
    small_all = _exchange_wait(gathering, upd["w_in"][0])[0]
    small_g = small_all.transpose(1, 0, 2, 3).reshape(DEPTH, -1)
    per_layer = [_unpack(small_g[l], [shp for _, shp in _SMALL]) for l in range(DEPTH)]
    sg = {k: jnp.stack([per_layer[l][i] for l in range(DEPTH)]) for i, (k, _) in enumerate(_SMALL)}
    g_final = sg["final_norm_w"][DEPTH - 1]
    g_conv_w = lax.dynamic_slice_in_dim(sg["conv_w"], chip * FFN_SHARD, FFN_SHARD, axis=2)

    rest = [("ada_b", ada_b, g_ada_b, m_ada_b, v_ada_b), ("norm1_w", norm1_w, sg["norm1_w"], m_norm1_w, v_norm1_w),
            ("attn_sinks", attn_sinks, sg["sinks"], m_attn_sinks, v_attn_sinks),
            ("sgu_ln_w", sgu_ln_w, sg["sgu_ln_w"], m_sgu_ln_w, v_sgu_ln_w),
            ("sgu_ln_b", sgu_ln_b, sg["sgu_ln_b"], m_sgu_ln_b, v_sgu_ln_b), ("sgu_w", sgu_w, sg["sgu_w"], m_sgu_w, v_sgu_w),
            ("sgu_b", sgu_b, sg["sgu_b"], m_sgu_b, v_sgu_b), ("norm2_w", norm2_w, sg["norm2_w"], m_norm2_w, v_norm2_w),
            ("ffn_conv_w", ffn_conv_w, g_conv_w, m_ffn_conv_w, v_ffn_conv_w),
            ("ffn_conv_b", ffn_conv_b, sg["conv_b"], m_ffn_conv_b, v_ffn_conv_b),
            ("final_norm_w", final_norm_w.reshape(1, d), g_final.reshape(1, d), m_final_norm_w.reshape(1, d),
             v_final_norm_w.reshape(1, d))]
    rest_out = _adamw_small("adamw_rest", *[[r[i] for r in rest] for i in (1, 2, 3, 4)])
    g_rest = {r[0]: r[2] for r in rest}
    u_rest = {r[0]: tuple(o[i] for o in rest_out) for i, r in enumerate(rest)}
    g_rest["final_norm_w"] = g_final
    u_rest["final_norm_w"] = tuple(a.reshape(d) for a in u_rest["final_norm_w"])

    names = ("ada_w", "ada_b", "norm1_w", "w_in", "attn_sinks", "sgu_ln_w", "sgu_ln_b", "sgu_w", "sgu_b", "proj_a", "proj_b",
             "w_out", "norm2_w", "ffn_w_gate", "ffn_w_up", "ffn_conv_w", "ffn_conv_b", "ffn_w_down", "final_norm_w")
    alias = {"ffn_w_gate": "w_gate", "ffn_w_up": "w_up", "ffn_w_down": "w_down"}
    grad_of = lambda n: g_rest[n] if n in g_rest else g_big[alias.get(n, n)]
    upd_of = lambda n: u_rest[n] if n in u_rest else upd[alias.get(n, n)]
    return (loss, dx[None], *[grad_of(n) for n in names], *[upd_of(n)[0] for n in names],
            *[upd_of(n)[1] for n in names], *[upd_of(n)[2] for n in names])
```

```python
import jax
import jax.numpy as jnp
from jax import lax
from jax.experimental import pallas as pl
from jax.experimental.pallas import tpu as pltpu

F32 = jnp.float32
BF = jnp.bfloat16

D_MODEL = 1024
N_Q_HEADS = 16
N_KV_HEADS = 2
HEAD_DIM = 64
ATTN_BLOCK = 128
ROPE_THETA = 500000.0
ROT_DIM = HEAD_DIM // 4
SGU_WIDTH = 1024
SGU_GROUPS = 8
SGU_CHUNK = 128
FFN_DIM = 2816
NORM_EPS = 1e-6
DEPTH = 2
IN_COLS = 5376
N_CHIPS = 4
FFN_SHARD = FFN_DIM // N_CHIPS
CHIPS_PER_CHUNK = 2
FFN_CHUNK = CHIPS_PER_CHUNK * FFN_SHARD
FFN_CHUNKS = FFN_DIM // FFN_CHUNK
LANES = 128
SUBLANES = 8
HALO = 16
VMEM_LIMIT = 56 * 1024 * 1024
NEG_BIG = -1e30

ADAM_LR = 0.001
ADAM_B1 = 0.9
ADAM_B2 = 0.999
ADAM_EPS = 1e-08
ADAM_WD = 0.01
ADAM_STEP = 10

MESH = pl.DeviceIdType.MESH

Q_END = 1024
KV_END = 1280
U_END = 2304
Z_END = 3328
GA_END = 4352


def _sds(shape, dtype):
    return jax.ShapeDtypeStruct(tuple(shape), dtype)


def _call(body, name, out_shape, grid, in_specs, out_specs, semantics, scratch=(), after=None):
    n_in = len(in_specs)
    fn = body
    if after is not None:
        def fn(*refs):
            return body(*refs[:n_in], *refs[n_in + 1:])

        in_specs = list(in_specs) + [pl.BlockSpec(memory_space=pl.ANY)]
    call = pl.pallas_call(
        fn, name=name, out_shape=out_shape, grid=grid, in_specs=in_specs, out_specs=out_specs,
        scratch_shapes=scratch,
        compiler_params=pltpu.CompilerParams(dimension_semantics=semantics, vmem_limit_bytes=VMEM_LIMIT))
    if after is None:
        return call
    return lambda *args: call(*args, after)


def _rows(tm, width, col=0):
    return pl.BlockSpec((tm, width), lambda i: (i, col))


def _vec(width):
    return pl.BlockSpec((1, width), lambda i: (0, 0))


def _resident(shape):
    zeros = (0,) * len(shape)
    return pl.BlockSpec(tuple(shape), lambda *_: zeros, pipeline_mode=pl.Buffered(1))


def _sigmoid(x):
    return 0.5 + 0.5 * jnp.tanh(0.5 * x)


def _gelu(x):
    return 0.5 * x * (1.0 + lax.erf(x * 0.7071067811865476))


def _gelu_and_grad(x):
    cdf = 0.5 * (1.0 + lax.erf(x * 0.7071067811865476))
    return x * cdf, cdf + x * jnp.exp(-0.5 * x * x) * 0.3989422804014327


def _dot(a, b):
    return jnp.dot(a, b, preferred_element_type=F32)


def _dot_nt(a, b):
    return lax.dot_general(a, b, (((1,), (1,)), ((), ())), preferred_element_type=F32)


def _dot_tn(a, b):
    return lax.dot_general(a, b, (((0,), (0,)), ((), ())), preferred_element_type=F32)


def _rms(xv):
    return lax.rsqrt(jnp.mean(xv * xv, axis=-1, keepdims=True) + NORM_EPS)


def _matmul_tn(name, a, b, tk=512, tn=1024, after=None):
    s, k = a.shape
    n = b.shape[1]
    tk, tn = min(tk, k), min(tn, n)

    def body(a_ref, b_ref, o_ref):
        o_ref[...] = _dot_tn(a_ref[...], b_ref[...]).astype(o_ref.dtype)

    return _call(body, name, _sds((k, n), BF), (k // tk, n // tn),
                 [pl.BlockSpec((s, tk), lambda i, j: (0, i)), pl.BlockSpec((s, tn), lambda i, j: (0, j))],
                 pl.BlockSpec((tk, tn), lambda i, j: (i, j)), ("parallel", "parallel"), after=after)(a, b)


def _matmul_tn_segments(name, segments, b, tk=256):
    s, n = b.shape
    counts = [a.shape[1] // tk for a in segments]
    firsts = [sum(counts[:t]) for t in range(len(segments))]

    def body(*refs):
        a_refs, b_ref, o_ref = refs[:len(segments)], refs[len(segments)], refs[-1]
        i = pl.program_id(0)
        for a_ref, first, count in zip(a_refs, firsts, counts):
            @pl.when((i >= first) & (i < first + count))
            def _(a_ref=a_ref):
                o_ref[...] = _dot_tn(a_ref[...], b_ref[...]).astype(BF)

    specs = [pl.BlockSpec((s, tk), lambda i, first=first, count=count: (0, jnp.clip(i - first, 0, count - 1)))
             for first, count in zip(firsts, counts)]
    return _call(body, name, _sds((sum(counts) * tk, n), BF), (sum(counts),), specs + [_resident(b.shape)],
                 pl.BlockSpec((tk, n), lambda i: (i, 0)), ("arbitrary",))(*segments, b)


def _rope_partner(v):
    lane = lax.broadcasted_iota(jnp.int32, (1, LANES), 1) % HEAD_DIM
    return jnp.where(lane < ROT_DIM // 2, pltpu.roll(v, LANES - ROT_DIM // 2, axis=1), pltpu.roll(v, ROT_DIM // 2, axis=1))


def _dup_half(v, half):
    lane = lax.broadcasted_iota(jnp.int32, (1, LANES), 1)
    keep = jnp.where((lane >= HEAD_DIM) == (half == 1), v, 0.0)
    return keep + pltpu.roll(keep, HEAD_DIM, axis=1)


def _in_proj(name, x, w, sc, sh, w_in, cosf, sinf, tm=512, after=None):
    s, d = x.shape
    tm = min(tm, s)

    def body(x_ref, w_ref, sc_ref, sh_ref, win_ref, cos_ref, sin_ref,
             h_ref, qr_ref, kk0_ref, kk1_ref, vv0_ref, vv1_ref, u_ref, v_ref, ga_ref, gb_ref):
        xv = x_ref[...]
        h = ((xv * _rms(xv)) * w_ref[...] * (1.0 + sc_ref[...]) + sh_ref[...]).astype(BF)
        h_ref[...] = h
        cosv, sinv = cos_ref[...], sin_ref[...]
        q = _dot_nt(h, win_ref[:Q_END, :])
        for j in range(D_MODEL // LANES):
            qv = q[:, j * LANES:(j + 1) * LANES]
            qr_ref[:, j * LANES:(j + 1) * LANES] = ((qv * cosv + _rope_partner(qv) * sinv) * ATTN_SCALE).astype(BF)
        kv = _dot_nt(h, win_ref[Q_END:KV_END, :])
        kr = kv[:, :LANES] * cosv + _rope_partner(kv[:, :LANES]) * sinv
        vv = kv[:, LANES:]
        kk0_ref[...] = _dup_half(kr, 0).astype(BF)
        kk1_ref[...] = _dup_half(kr, 1).astype(BF)
        vv0_ref[...] = _dup_half(vv, 0).astype(BF)
        vv1_ref[...] = _dup_half(vv, 1).astype(BF)
        u_ref[...] = _dot_nt(h, win_ref[KV_END:U_END, :])
        v_ref[...] = _dot_nt(h, win_ref[U_END:Z_END, :])
        ga_ref[...] = _dot_nt(h, win_ref[Z_END:GA_END, :]).astype(BF)
        gb_ref[...] = _dot_nt(h, win_ref[GA_END:, :]).astype(BF)

    wide, kvs, pre = _sds((s, d), BF), _sds((s, LANES), BF), _sds((s, d), F32)
    return _call(body, name, (wide, wide, kvs, kvs, kvs, kvs, pre, pre, wide, wide), (s // tm,),
                 [_rows(tm, d), _vec(d), _vec(d), _vec(d), _resident(w_in.shape), _rows(tm, LANES), _rows(tm, LANES)],
                 (_rows(tm, d), _rows(tm, d)) + (_rows(tm, LANES),) * 4 + (_rows(tm, d),) * 4, ("parallel",), after=after)(
                     x, w, sc, sh, w_in, cosf, sinf)


def _in_proj_bwd(name, dq, dkv, du, dv, dga, dgb, w_in, x, w, sc, dx_in, tm=512):
    s, d = x.shape
    tm = min(tm, s)

    def body(dq_ref, dkv_ref, du_ref, dv_ref, dga_ref, dgb_ref, win_ref, x_ref, w_ref, sc_ref, dxin_ref,
             dx_ref, da_ref, dsh_ref):
        @pl.when(pl.program_id(0) == 0)
        def _():
            da_ref[...] = jnp.zeros_like(da_ref)
            dsh_ref[...] = jnp.zeros_like(dsh_ref)

        dh = (_dot(dq_ref[...], win_ref[:Q_END, :]) + _dot(dkv_ref[...], win_ref[Q_END:KV_END, :])
              + _dot(du_ref[...], win_ref[KV_END:U_END, :]) + _dot(dv_ref[...], win_ref[U_END:Z_END, :])
              + _dot(dga_ref[...], win_ref[Z_END:GA_END, :]) + _dot(dgb_ref[...], win_ref[GA_END:, :]))
        xv = x_ref[...]
        r = _rms(xv)
        xn = xv * r
        dxn = dh * (w_ref[...] * (1.0 + sc_ref[...]))
        dx_ref[...] = dxin_ref[...] + r * (dxn - xn * jnp.mean(dxn * xn, axis=-1, keepdims=True))
        da_ref[...] += jnp.sum(dh * xn, axis=0, keepdims=True)
        dsh_ref[...] += jnp.sum(dh, axis=0, keepdims=True)

    return _call(body, name, (_sds((s, d), F32), _sds((1, d), F32), _sds((1, d), F32)), (s // tm,),
                 [_rows(tm, d), _rows(tm, 2 * LANES), _rows(tm, d), _rows(tm, d), _rows(tm, d), _rows(tm, d),
                  _resident(w_in.shape), _rows(tm, d), _vec(d), _vec(d), _rows(tm, d)],
                 (_rows(tm, d), _vec(d), _vec(d)), ("arbitrary",))(dq, dkv, du, dv, dga, dgb, w_in, x, w, sc, dx_in)


def _rope_bwd(name, dqr, dkv_cur, dkv_prev, cosf, sinf, tm=512):
    s = dqr.shape[0]
    tm = min(tm, s)
    steps = s // tm
    per = tm // ATTN_BLOCK
    nb = s // ATTN_BLOCK

    def unrope(v, cosv, sinv):
        return v * cosv - _rope_partner(v) * sinv

    def body(dq_ref, cur_ref, prev_ref, next_ref, cos_ref, sin_ref, dqo_ref, dkvo_ref):
        i = pl.program_id(0)
        cosv, sinv = cos_ref[...], sin_ref[...]
        for j in range(D_MODEL // LANES):
            dqo_ref[:, j * LANES:(j + 1) * LANES] = unrope(dq_ref[:, j * LANES:(j + 1) * LANES], cosv, sinv).astype(BF)
        nxt = jnp.where(i < steps - 1, next_ref[...], 0.0)
        if per > 1:
            shifted = jnp.concatenate([prev_ref[ATTN_BLOCK:, :], nxt], axis=0)
        else:
            shifted = nxt
        tot = cur_ref[...] + shifted
        dkvo_ref[:, :LANES] = unrope(tot[:, :LANES], cosv, sinv).astype(BF)
        dkvo_ref[:, LANES:] = tot[:, LANES:].astype(BF)

    nxt_spec = pl.BlockSpec((ATTN_BLOCK, 2 * LANES), lambda i: (jnp.minimum((i + 1) * per, nb - 1), 0))
    return _call(body, name, (_sds((s, D_MODEL), BF), _sds((s, 2 * LANES), BF)), (steps,),
                 [_rows(tm, D_MODEL), _rows(tm, 2 * LANES), _rows(tm, 2 * LANES), nxt_spec, _rows(tm, LANES),
                  _rows(tm, LANES)],
                 (_rows(tm, D_MODEL), _rows(tm, 2 * LANES)), ("parallel",))(dqr, dkv_cur, dkv_prev, dkv_prev, cosf, sinf)


Q_PER_KV = N_Q_HEADS // N_KV_HEADS
ATTN_SCALE = HEAD_DIM ** -0.5
HEADS_AHEAD_FWD = 2
HEADS_AHEAD_BWD = 3


def _band_mask_t(n):
    kj = lax.broadcasted_iota(jnp.int32, (2 * ATTN_BLOCK, ATTN_BLOCK), 0)
    qi = lax.broadcasted_iota(jnp.int32, (2 * ATTN_BLOCK, ATTN_BLOCK), 1)
    return (kj > qi) & (kj <= qi + ATTN_BLOCK) & ((n > 0) | (kj >= ATTN_BLOCK))


def _softmax_t(raw, allowed, sink):
    sc = jnp.where(allowed, raw, NEG_BIG)
    m = jnp.maximum(jnp.max(sc, axis=0, keepdims=True), sink)
    p = jnp.exp(sc - m)
    esink = jnp.exp(sink - m)
    inv = 1.0 / (jnp.sum(p, axis=0, keepdims=True) + esink)
    return p * inv, esink * inv


def _kv_specs():
    cur = pl.BlockSpec((ATTN_BLOCK, LANES), lambda n: (n, 0))
    prev = pl.BlockSpec((ATTN_BLOCK, LANES), lambda n: (jnp.maximum(n - 1, 0), 0))
    return [prev, cur] * 4


def _attention(name, qr, kk0, kk1, vv0, vv1, sinks):
    s = qr.shape[0]
    nb = s // ATTN_BLOCK

    def body(sink_ref, q_ref, k0p, k0c, k1p, k1c, v0p, v0c, v1p, v1c, y_ref):
        allowed = _band_mask_t(pl.program_id(0))
        upper = lax.broadcasted_iota(jnp.int32, (1, LANES), 1) >= HEAD_DIM
        upper_rows = lax.broadcasted_iota(jnp.int32, (LANES, 1), 0) >= HEAD_DIM
        bands = ((jnp.concatenate([k0p[...], k0c[...]], axis=0), jnp.concatenate([v0p[...], v0c[...]], axis=0)),
                 (jnp.concatenate([k1p[...], k1c[...]], axis=0), jnp.concatenate([v1p[...], v1c[...]], axis=0)))
        vbts = (bands[0][1].T, bands[1][1].T)

        def scores(h):
            hk, j, half = h // Q_PER_KV, (h % Q_PER_KV) // 2, h % 2
            col = (hk * 4 + j) * LANES
            qp = q_ref[:, col:col + LANES]
            return _dot_nt(bands[hk][0], jnp.where(upper if half else jnp.logical_not(upper), qp, jnp.zeros_like(qp)))

        out_t = None
        ahead = [scores(h) for h in range(HEADS_AHEAD_FWD)]
        for h in range(N_Q_HEADS):
            hk, j, half = h // Q_PER_KV, (h % Q_PER_KV) // 2, h % 2
            raw = ahead.pop(0)
            if h + HEADS_AHEAD_FWD < N_Q_HEADS:
                ahead.append(scores(h + HEADS_AHEAD_FWD))
            pn, _ = _softmax_t(raw, allowed, sink_ref[h])
            o_h = _dot(vbts[hk], pn.astype(BF))
            out_t = jnp.where(upper_rows, o_h, out_t) if half else o_h
            if half:
                col = (hk * 4 + j) * LANES
                y_ref[:, col:col + LANES] = out_t.T.astype(BF)

    return _call(body, name, _sds((s, D_MODEL), BF), (nb,),
                 [pl.BlockSpec(memory_space=pltpu.SMEM), pl.BlockSpec((ATTN_BLOCK, D_MODEL), lambda n: (n, 0))] + _kv_specs(),
                 pl.BlockSpec((ATTN_BLOCK, D_MODEL), lambda n: (n, 0)), ("parallel",))(
                     sinks, qr, kk0, kk0, kk1, kk1, vv0, vv0, vv1, vv1)


def _attention_bwd(name, qr, kk0, kk1, vv0, vv1, sinks, dy, after=None):
    s = qr.shape[0]
    nb = s // ATTN_BLOCK

    def body(sink_ref, q_ref, dy_ref, k0p, k0c, k1p, k1c, v0p, v0c, v1p, v1c, dq_ref, cur_ref, prev_ref, dsink_ref):
        @pl.when(pl.program_id(0) == 0)
        def _():
            dsink_ref[...] = jnp.zeros_like(dsink_ref)

        allowed = _band_mask_t(pl.program_id(0))
        lane = lax.broadcasted_iota(jnp.int32, (1, LANES), 1)
        upper = lane >= HEAD_DIM
        upper_rows = lax.broadcasted_iota(jnp.int32, (LANES, 1), 0) >= HEAD_DIM
        bands = ((jnp.concatenate([k0p[...], k0c[...]], axis=0), jnp.concatenate([v0p[...], v0c[...]], axis=0)),
                 (jnp.concatenate([k1p[...], k1c[...]], axis=0), jnp.concatenate([v1p[...], v1c[...]], axis=0)))
        kbts = (bands[0][0].T, bands[1][0].T)

        def scores(h):
            hk, j, half = h // Q_PER_KV, (h % Q_PER_KV) // 2, h % 2
            kb, vb = bands[hk]
            col = (hk * 4 + j) * LANES
            sel = upper if half else jnp.logical_not(upper)
            qp = q_ref[:, col:col + LANES]
            qa = jnp.where(sel, qp, jnp.zeros_like(qp))
            dya = jnp.where(sel, dy_ref[:, col:col + LANES], 0.0).astype(BF)
            return qa, dya, _dot_nt(kb, qa), _dot_nt(vb, dya)

        dsink = jnp.zeros((1, LANES), F32)
        dk_slab = jnp.zeros((2 * ATTN_BLOCK, LANES), F32)
        dv_slab = jnp.zeros((2 * ATTN_BLOCK, LANES), F32)
        dkk = dvv = dq_t = None
        ahead = [scores(h) for h in range(HEADS_AHEAD_BWD)]
        for h in range(N_Q_HEADS):
            hk, j, half = h // Q_PER_KV, (h % Q_PER_KV) // 2, h % 2
            qa, dya, raw, dp = ahead.pop(0)
            if h + HEADS_AHEAD_BWD < N_Q_HEADS:
                ahead.append(scores(h + HEADS_AHEAD_BWD))
            pn, psink = _softmax_t(raw, allowed, sink_ref[h])
            delta = jnp.sum(pn * dp, axis=0, keepdims=True)
            ds = (pn * (dp - delta)).astype(BF)
            dsink = dsink + jnp.where(lane == h, -jnp.sum(psink * delta), 0.0)
            dq_h = _dot(kbts[hk], ds) * ATTN_SCALE
            dq_t = jnp.where(upper_rows, dq_h, dq_t) if half else dq_h
            dk_h, dv_h = _dot(ds, qa), _dot(pn.astype(BF), dya)
            dkk, dvv = (dk_h, dv_h) if h % Q_PER_KV == 0 else (dkk + dk_h, dvv + dv_h)
            if half:
                col = (hk * 4 + j) * LANES
                dq_ref[:, col:col + LANES] = dq_t.T
            if h % Q_PER_KV == Q_PER_KV - 1:
                mine = upper if hk else jnp.logical_not(upper)
                dk_slab = jnp.where(mine, dkk + pltpu.roll(dkk, HEAD_DIM, axis=1), dk_slab)
                dv_slab = jnp.where(mine, dvv + pltpu.roll(dvv, HEAD_DIM, axis=1), dv_slab)
        prev_ref[:, :LANES] = dk_slab[:ATTN_BLOCK]
        prev_ref[:, LANES:] = dv_slab[:ATTN_BLOCK]
        cur_ref[:, :LANES] = dk_slab[ATTN_BLOCK:]
        cur_ref[:, LANES:] = dv_slab[ATTN_BLOCK:]
        dsink_ref[...] += dsink

    blk = pl.BlockSpec((ATTN_BLOCK, D_MODEL), lambda n: (n, 0))
    kvo = pl.BlockSpec((ATTN_BLOCK, 2 * LANES), lambda n: (n, 0))
    return _call(body, name,
                 (_sds((s, D_MODEL), F32), _sds((s, 2 * LANES), F32), _sds((s, 2 * LANES), F32), _sds((1, LANES), F32)),
                 (nb,), [pl.BlockSpec(memory_space=pltpu.SMEM), blk, blk] + _kv_specs(),
                 (blk, kvo, kvo, pl.BlockSpec((1, LANES), lambda n: (0, 0))), ("arbitrary",), after=after)(
                     sinks, qr, dy, kk0, kk0, kk1, kk1, vv0, vv0, vv1, vv1)


def _sgu_weights(wm_ref, g):
    t = lax.broadcasted_iota(jnp.int32, (SGU_CHUNK, SGU_CHUNK), 0)
    sidx = lax.broadcasted_iota(jnp.int32, (SGU_CHUNK, SGU_CHUNK), 1)
    return jnp.where(sidx <= t, wm_ref[g], 0.0).astype(BF)


def _layer_norm_stats(v):
    mu = jnp.mean(v, axis=-1, keepdims=True)
    cen = v - mu
    rstd = lax.rsqrt(jnp.mean(cen * cen, axis=-1, keepdims=True) + NORM_EPS)
    return cen * rstd, rstd


def _sgu(name, u_pre, v_pre, ln_w, ln_b, wm, bfull, tm=512):
    s, w = u_pre.shape
    tm = min(tm, s)

    def body(u_ref, v_ref, lw_ref, lb_ref, wm_ref, b_ref, y_ref):
        vhat, _ = _layer_norm_stats(_gelu(v_ref[...]))
        vn = (vhat * lw_ref[...] + lb_ref[...]).astype(BF)
        for g in range(SGU_GROUPS):
            wg = _sgu_weights(wm_ref, g)
            cols = slice(g * SGU_CHUNK, (g + 1) * SGU_CHUNK)
            for ch in range(tm // SGU_CHUNK):
                rows = slice(ch * SGU_CHUNK, (ch + 1) * SGU_CHUNK)
                f = _dot(wg, vn[rows, cols]) + b_ref[g]
                y_ref[rows, cols] = (_gelu(u_ref[rows, cols]) * f).astype(BF)

    full3 = pl.BlockSpec((SGU_GROUPS, SGU_CHUNK, SGU_CHUNK), lambda i: (0, 0, 0))
    return _call(body, name, _sds((s, w), BF), (s // tm,),
                 [_rows(tm, w), _rows(tm, w), _vec(w), _vec(w), full3, full3],
                 _rows(tm, w), ("parallel",))(u_pre, v_pre, ln_w, ln_b, wm, bfull)


def _sgu_bwd(name, u_pre, v_pre, ln_w, ln_b, wm, bfull, dy, tm=512, after=None):
    s, w = u_pre.shape
    tm = min(tm, s)
    steps = s // tm

    def body(u_ref, v_ref, lw_ref, lb_ref, wm_ref, b_ref, dy_ref, du_ref, dv_ref, dwm_ref, db_ref, dlw_ref, dlb_ref,
             dfsum_ref):
        i = pl.program_id(0)

        @pl.when(i == 0)
        def _():
            dwm_ref[...] = jnp.zeros_like(dwm_ref)
            dlw_ref[...] = jnp.zeros_like(dlw_ref)
            dlb_ref[...] = jnp.zeros_like(dlb_ref)
            dfsum_ref[...] = jnp.zeros_like(dfsum_ref)

        vpre = v_ref[...]
        vg, dvg_dv = _gelu_and_grad(vpre)
        vhat, rstd = _layer_norm_stats(vg)
        vn = (vhat * lw_ref[...] + lb_ref[...]).astype(BF)
        t = lax.broadcasted_iota(jnp.int32, (SGU_CHUNK, SGU_CHUNK), 0)
        sidx = lax.broadcasted_iota(jnp.int32, (SGU_CHUNK, SGU_CHUNK), 1)
        dvn_cols = []
        for g in range(SGU_GROUPS):
            wg = _sgu_weights(wm_ref, g)
            cols = slice(g * SGU_CHUNK, (g + 1) * SGU_CHUNK)
            dvn_rows = []
            dwg = jnp.zeros((SGU_CHUNK, SGU_CHUNK), F32)
            dfs = jnp.zeros((SGU_CHUNK, SGU_CHUNK), F32)
            for ch in range(tm // SGU_CHUNK):
                rows = slice(ch * SGU_CHUNK, (ch + 1) * SGU_CHUNK)
                upre = u_ref[rows, cols]
                dyv = dy_ref[rows, cols].astype(F32)
                f = _dot(wg, vn[rows, cols]) + b_ref[g]
                ug, dug_du = _gelu_and_grad(upre)
                du_ref[rows, cols] = (dyv * f * dug_du).astype(BF)
                df = dyv * ug
                dfb = df.astype(BF)
                dvn_rows.append(_dot_tn(wg, dfb))
                dwg = dwg + _dot_nt(dfb, vn[rows, cols])
                dfs = dfs + df
            dwm_ref[g] += jnp.where(sidx <= t, dwg, 0.0)
            dfsum_ref[g] += dfs
            dvn_cols.append(jnp.concatenate(dvn_rows, axis=0) if len(dvn_rows) > 1 else dvn_rows[0])
        dvn = jnp.concatenate(dvn_cols, axis=1)
        dlw_ref[...] += jnp.sum(dvn * vhat, axis=0, keepdims=True)
        dlb_ref[...] += jnp.sum(dvn, axis=0, keepdims=True)
        dvh = dvn * lw_ref[...]
        dvg = rstd * (dvh - jnp.mean(dvh, axis=-1, keepdims=True) - vhat * jnp.mean(dvh * vhat, axis=-1, keepdims=True))
        dv_ref[...] = (dvg * dvg_dv).astype(BF)

        @pl.when(i == steps - 1)
        def _():
            for g in range(SGU_GROUPS):
                db_ref[g:g + 1, :] = jnp.sum(dfsum_ref[g].T, axis=0, keepdims=True)

    full3 = pl.BlockSpec((SGU_GROUPS, SGU_CHUNK, SGU_CHUNK), lambda i: (0, 0, 0))
    return _call(body, name,
                 (_sds((s, w), BF), _sds((s, w), BF), _sds((SGU_GROUPS, SGU_CHUNK, SGU_CHUNK), F32),
                  _sds((SGU_GROUPS, SGU_CHUNK), F32), _sds((1, w), F32), _sds((1, w), F32)),
                 (steps,),
                 [_rows(tm, w), _rows(tm, w), _vec(w), _vec(w), full3, full3, _rows(tm, w)],
                 (_rows(tm, w), _rows(tm, w), full3, pl.BlockSpec((SGU_GROUPS, SGU_CHUNK), lambda i: (0, 0)), _vec(w), _vec(w)),
                 ("arbitrary",), scratch=[pltpu.VMEM((SGU_GROUPS, SGU_CHUNK, SGU_CHUNK), F32)], after=after)(
                     u_pre, v_pre, ln_w, ln_b, wm, bfull, dy)


def _mix_out(name, y_sgu, y_attn, ga_pre, gb_pre, x, g1, proj_a, proj_b, w_out, w2, sc2, sh2, tm=512):
    s, d = x.shape
    tm = min(tm, s)

    def body(ys_ref, ya_ref, ga_ref, gb_ref, x_ref, g1_ref, wa_ref, wb_ref, wo_ref, w2_ref, sc2_ref, sh2_ref,
             m_ref, pa_ref, pb_ref, o_ref, x1_ref, h2_ref):
        pa = _dot(ys_ref[...], wa_ref[...].reshape(d, d))
        pb = _dot(ya_ref[...], wb_ref[...].reshape(d, d))
        pa_ref[...] = pa.astype(BF)
        pb_ref[...] = pb.astype(BF)
        merged = (_sigmoid(ga_ref[...].astype(F32)) * pa + _sigmoid(gb_ref[...].astype(F32)) * pb).astype(BF)
        m_ref[...] = merged
        o = _dot(merged, wo_ref[...].reshape(d, d))
        o_ref[...] = o.astype(BF)
        x1 = x_ref[...] + g1_ref[...] * o
        x1_ref[...] = x1
        h2_ref[...] = ((x1 * _rms(x1)) * w2_ref[...] * (1.0 + sc2_ref[...]) + sh2_ref[...]).astype(BF)

    f, b = _sds((s, d), F32), _sds((s, d), BF)
    r = _rows(tm, d)
    wspec = _resident(proj_a.shape)
    return _call(body, name, (b, b, b, b, f, b), (s // tm,),
                 [r, r, r, r, r, _vec(d), wspec, wspec, wspec, _vec(d), _vec(d), _vec(d)], (r,) * 6, ("parallel",))(
                     y_sgu, y_attn, ga_pre, gb_pre, x, g1, proj_a, proj_b, w_out, w2, sc2, sh2)


def _mix_bwd(name, do, w_out, proj_a, proj_b, ga_pre, gb_pre, pa, pb, tm=512):
    s, d = do.shape
    tm = min(tm, s)

    def body(do_ref, wo_ref, wa_ref, wb_ref, ga_ref, gb_ref, pa_ref, pb_ref,
             dpa_ref, dpb_ref, dga_ref, dgb_ref, dys_ref, dya_ref):
        dm = _dot_nt(do_ref[...], wo_ref[...].reshape(d, d))
        ga = _sigmoid(ga_ref[...].astype(F32))
        gb = _sigmoid(gb_ref[...].astype(F32))
        dpa = (dm * ga).astype(BF)
        dpb = (dm * gb).astype(BF)
        dpa_ref[...] = dpa
        dpb_ref[...] = dpb
        dga_ref[...] = (dm * pa_ref[...].astype(F32) * ga * (1.0 - ga)).astype(BF)
        dgb_ref[...] = (dm * pb_ref[...].astype(F32) * gb * (1.0 - gb)).astype(BF)
        dys_ref[...] = _dot_nt(dpa, wa_ref[...].reshape(d, d)).astype(BF)
        dya_ref[...] = _dot_nt(dpb, wb_ref[...].reshape(d, d)).astype(BF)

    f, b = _sds((s, d), F32), _sds((s, d), BF)
    r = _rows(tm, d)
    wspec = _resident(w_out.shape)
    return _call(body, name, (b, b, b, b, b, b), (s // tm,), [r, wspec, wspec, wspec, r, r, r, r], (r,) * 6,
                 ("parallel",))(do, w_out, proj_a, proj_b, ga_pre, gb_pre, pa, pb)


def _ffn_up_act(name, h2, w_gate, w_up, cw, cb, tm=1024):
    s, d = h2.shape
    tm = min(tm, s)
    tc = FFN_CHUNK
    per = tm // HALO

    def body(h_ref, hprev_ref, wg_ref, wu_ref, cw_ref, cb_ref, a_ref, ac_ref, up_ref, hf_ref):
        hv = h_ref[...]
        wg = wg_ref[...].reshape(tc, d)
        a = _dot_nt(hv, wg).astype(BF)
        up = _dot_nt(hv, wu_ref[...].reshape(tc, d)).astype(BF)
        a_ref[...] = a
        up_ref[...] = up
        prev = jnp.where(pl.program_id(1) > 0, _dot_nt(hprev_ref[...], wg).astype(BF).astype(F32), 0.0)
        ext = jnp.concatenate([prev, a.astype(F32)], axis=0)
        ac = (cb_ref[...] + cw_ref[0:1, :] * pltpu.roll(ext, 2, axis=0) + cw_ref[1:2, :] * pltpu.roll(ext, 1, axis=0)
              + cw_ref[2:3, :] * ext)[HALO:]
        ac_ref[...] = ac.astype(BF)
        hf_ref[...] = (ac * _sigmoid(ac) * up.astype(F32)).astype(BF)

    wspec = pl.BlockSpec((CHIPS_PER_CHUNK, FFN_SHARD, d), lambda j, i: (j, 0, 0))
    ospec = pl.BlockSpec((tm, tc), lambda j, i: (i, j))
    o = _sds((s, FFN_DIM), BF)
    return _call(body, name, (o, o, o, o), (FFN_CHUNKS, s // tm),
                 [pl.BlockSpec((tm, d), lambda j, i: (i, 0)), pl.BlockSpec((HALO, d), lambda j, i: (jnp.maximum(i * per - 1, 0), 0)),
                  wspec, wspec, pl.BlockSpec((3, tc), lambda j, i: (0, j)), pl.BlockSpec((1, tc), lambda j, i: (0, j))],
                 (ospec, ospec, ospec, ospec), ("parallel", "parallel"))(h2, h2, w_gate, w_up, cw, cb)


def _ffn_down(name, hf, w_down, x1, g2, tm=512):
    s, d = x1.shape
    tm = min(tm, s)

    def body(hf_ref, wd_ref, x1_ref, g2_ref, dn_ref, x2_ref):
        dn = _dot(hf_ref[...], wd_ref[...].reshape(FFN_DIM, d))
        dn_ref[...] = dn.astype(BF)
        x2_ref[...] = x1_ref[...] + g2_ref[...] * dn

    return _call(body, name, (_sds((s, d), BF), _sds((s, d), F32)), (s // tm,),
                 [_rows(tm, FFN_DIM), _resident(w_down.shape), _rows(tm, d), _vec(d)],
                 (_rows(tm, d), _rows(tm, d)), ("parallel",))(hf, w_down, x1, g2)


def _ffn_down_bwd_act(name, dx2, dn, g2, w_down, a, ac, up, cw, tm=256, after=None):
    s, d = dx2.shape
    c = a.shape[1]
    tm = min(tm, s)
    tc = FFN_CHUNK
    per = tm // HALO
    steps = s // tm
    last = s // HALO - 1
    n = tm + HALO

    def body(dx_ref, dxnext_ref, dn_ref, g2_ref, wd_ref, a_ref, ac_ref, acnext_ref, up_ref, upnext_ref, cw_ref,
             ddn_ref, da_ref, dup_ref, dg_ref, dcw_ref, dcb_ref):
        i = pl.program_id(0)

        @pl.when(i == 0)
        def _():
            dg_ref[...] = jnp.zeros_like(dg_ref)
            dcw_ref[...] = jnp.zeros_like(dcw_ref)
            dcb_ref[...] = jnp.zeros_like(dcb_ref)

        dxv = dx_ref[...]
        ddn = (dxv * g2_ref[...]).astype(BF)
        ddn_ref[...] = ddn
        dg_ref[...] += jnp.sum(dxv * dn_ref[...].astype(F32), axis=0, keepdims=True)
        ddn_next = jnp.where(i < steps - 1, dxnext_ref[...] * g2_ref[...], 0.0).astype(BF)
        ddn_ext = jnp.concatenate([ddn, ddn_next], axis=0)
        for k in range(FFN_CHUNKS):
            cols = slice(k * tc, (k + 1) * tc)
            dh = _dot_nt(ddn_ext, wd_ref[k * CHIPS_PER_CHUNK:(k + 1) * CHIPS_PER_CHUNK].reshape(tc, d))
            ace = jnp.concatenate([ac_ref[:, cols].astype(F32), acnext_ref[:, cols].astype(F32)], axis=0)
            upe = jnp.concatenate([up_ref[:, cols].astype(F32), upnext_ref[:, cols].astype(F32)], axis=0)
            sig = _sigmoid(ace)
            silu = ace * sig
            dac = dh * upe * (sig + silu * (1.0 - sig))
            dup_ref[:, cols] = (dh[:tm] * silu[:tm]).astype(BF)
            d1 = pltpu.roll(dac, n - 1, axis=0)[:tm]
            d2 = pltpu.roll(dac, n - 2, axis=0)[:tm]
            d0 = dac[:tm]
            da_ref[:, cols] = (cw_ref[2:3, cols] * d0 + cw_ref[1:2, cols] * d1 + cw_ref[0:1, cols] * d2).astype(BF)
            a0 = a_ref[:, cols].astype(F32)
            dcb_ref[:, cols] += jnp.sum(d0, axis=0, keepdims=True)
            dcw_ref[0:1, cols] += jnp.sum(d2 * a0, axis=0, keepdims=True)
            dcw_ref[1:2, cols] += jnp.sum(d1 * a0, axis=0, keepdims=True)
            dcw_ref[2:3, cols] += jnp.sum(d0 * a0, axis=0, keepdims=True)

    nxt = lambda width: pl.BlockSpec((HALO, width), lambda i: (jnp.minimum((i + 1) * per, last), 0))
    wide = _sds((s, c), BF)
    return _call(body, name, (_sds((s, d), BF), wide, wide, _sds((1, d), F32), _sds((3, c), F32), _sds((1, c), F32)), (steps,),
                 [_rows(tm, d), nxt(d), _rows(tm, d), _vec(d), _resident(w_down.shape), _rows(tm, c), _rows(tm, c), nxt(c),
                  _rows(tm, c), nxt(c), pl.BlockSpec((3, c), lambda i: (0, 0))],
                 (_rows(tm, d), _rows(tm, c), _rows(tm, c), _vec(d), pl.BlockSpec((3, c), lambda i: (0, 0)), _vec(c)),
                 ("arbitrary",), after=after)(dx2, dx2, dn, g2, w_down, a, ac, ac, up, up, cw)


def _ffn_up_bwd(name, da, dup, w_gate, w_up, x1, dx2, w2, sc2, o, g1, tm=512, after=None):
    s, d = x1.shape
    tm = min(tm, s)

    def body(da_ref, dup_ref, wg_ref, wu_ref, x1_ref, dx2_ref, w2_ref, sc2_ref, o_ref, g1_ref,
             dx1_ref, do_ref, dnw_ref, dsh_ref, dg1_ref):
        @pl.when(pl.program_id(0) == 0)
        def _():
            dnw_ref[...] = jnp.zeros_like(dnw_ref)
            dsh_ref[...] = jnp.zeros_like(dsh_ref)
            dg1_ref[...] = jnp.zeros_like(dg1_ref)

        dh = _dot(da_ref[...], wg_ref[...].reshape(FFN_DIM, d)) + _dot(dup_ref[...], wu_ref[...].reshape(FFN_DIM, d))
        xv = x1_ref[...]
        r = _rms(xv)
        xn = xv * r
        dxn = dh * (w2_ref[...] * (1.0 + sc2_ref[...]))
        dx1 = dx2_ref[...] + r * (dxn - xn * jnp.mean(dxn * xn, axis=-1, keepdims=True))
        dx1_ref[...] = dx1
        dnw_ref[...] += jnp.sum(dh * xn, axis=0, keepdims=True)
        dsh_ref[...] += jnp.sum(dh, axis=0, keepdims=True)
        do_ref[...] = (dx1 * g1_ref[...]).astype(BF)
        dg1_ref[...] += jnp.sum(dx1 * o_ref[...].astype(F32), axis=0, keepdims=True)

    v = _sds((1, d), F32)
    r = _rows(tm, d)
    wspec = _resident(w_gate.shape)
    return _call(body, name, (_sds((s, d), F32), _sds((s, d), BF), v, v, v), (s // tm,),
                 [_rows(tm, FFN_DIM), _rows(tm, FFN_DIM), wspec, wspec, r, r, _vec(d), _vec(d), r, _vec(d)],
                 (r, r, _vec(d), _vec(d), _vec(d)), ("arbitrary",), after=after)(da, dup, w_gate, w_up, x1, dx2, w2, sc2, o, g1)


def _loss_head(name, x, w, target, tm=512):
    s, d = x.shape
    tm = min(tm, s)

    def body(x_ref, w_ref, t_ref, dx_ref, loss_ref, dw_ref):
        @pl.when(pl.program_id(0) == 0)
        def _():
            loss_ref[...] = jnp.zeros_like(loss_ref)
            dw_ref[...] = jnp.zeros_like(dw_ref)

        xv = x_ref[...]
        r = _rms(xv)
        xn = xv * r
        err = xn * w_ref[...] - t_ref[...]
        loss_ref[...] += 0.5 * jnp.sum(jnp.mean(err * err, axis=-1, keepdims=True))
        dy = err * (1.0 / d)
        dw_ref[...] += jnp.sum(dy * xn, axis=0, keepdims=True)
        dxn = dy * w_ref[...]
        dx_ref[...] = r * (dxn - xn * jnp.mean(dxn * xn, axis=-1, keepdims=True))

    return _call(body, name, (_sds((s, d), F32), _sds((1, LANES), F32), _sds((1, d), F32)), (s // tm,),
                 [_rows(tm, d), _vec(d), _rows(tm, d)], (_rows(tm, d), _vec(LANES), _vec(d)), ("arbitrary",))(x, w, target)


def _layer_fwd(l, x, mod, p, cosf, sinf, after=None, late=None, later=None):
    sh1, sc1, g1, sh2, sc2, g2 = mod
    tag = f"l{l}_"
    h, qr, kk0, kk1, vv0, vv1, u_pre, v_pre, ga_pre, gb_pre = _in_proj(
        tag + "in_proj", x, p["norm1_w"], sc1, sh1, p["w_in"], cosf, sinf, after=after)
    y_attn = _attention(tag + "attn", qr, kk0, kk1, vv0, vv1, p["sinks"])
    y_sgu = _sgu(tag + "sgu", u_pre, v_pre, p["sgu_ln_w"], p["sgu_ln_b"], p["sgu_w"], p["sgu_bfull"])
    if late is not None:
        p = dict(p, **late(y_sgu))
    merged, pa, pb, o, x1, h2 = _mix_out(tag + "mix_out", y_sgu, y_attn, ga_pre, gb_pre, x, g1, p["proj_a"], p["proj_b"],
                                         p["w_out"], p["norm2_w"], sc2, sh2)
    if later is not None:
        p = dict(p, **later(h2))
    a, ac, up, hf = _ffn_up_act(tag + "ffn_up", h2, p["w_gate"], p["w_up"], p["conv_w"], p["conv_b"])
    dn, x2 = _ffn_down(tag + "ffn_down", hf, p["w_down"], x1, g2)
    saved = dict(x=x, h=h, qr=qr, kk0=kk0, kk1=kk1, vv0=vv0, vv1=vv1, u_pre=u_pre, v_pre=v_pre, ga_pre=ga_pre,
                 gb_pre=gb_pre, y_attn=y_attn, y_sgu=y_sgu, merged=merged, pa=pa, pb=pb, o=o, x1=x1, h2=h2, a=a, ac=ac, up=up,
                 hf=hf, dn=dn)
    return x2, saved, p


def _layer_bwd(l, dx2, mod, p, sv, cosf, sinf, after=None, emit=None, tick=None):
    sh1, sc1, g1, sh2, sc2, g2 = mod
    tag = f"l{l}_b_"
    d = D_MODEL
    g = {}
    ready = (lambda names: emit({k: g.pop(k) for k in names})) if emit else (lambda names: None)
    tick = tick or (lambda y: None)
    ddn, da, dup, dg2, g["conv_w"], g["conv_b"] = _ffn_down_bwd_act(
        tag + "ffn_down", dx2, sv["dn"], g2, p["w_down"], sv["a"], sv["ac"], sv["up"], p["conv_w"], after=after)
    g["w_down"] = _matmul_tn(tag + "dw_down", sv["hf"], ddn, tk=FFN_CHUNK, after=tick(ddn)).reshape(N_CHIPS, FFN_SHARD, d)
    g["w_gate"] = _matmul_tn(tag + "dw_gate", da, sv["h2"], tk=FFN_CHUNK).reshape(N_CHIPS, FFN_SHARD, d)
    g["w_up"] = _matmul_tn(tag + "dw_up", dup, sv["h2"], tk=FFN_CHUNK).reshape(N_CHIPS, FFN_SHARD, d)
    dx1, do, da2, dsh2, dg1 = _ffn_up_bwd(tag + "ffn_up", da, dup, p["w_gate"], p["w_up"], sv["x1"], dx2, p["norm2_w"],
                                          sc2, sv["o"], g1, after=ready(("w_down", "w_gate", "w_up")))
    g["norm2_w"] = da2 * (1.0 + sc2)
    dsc2 = da2 * p["norm2_w"]
    g["w_out"] = _matmul_tn(tag + "dw_out", sv["merged"], do, after=tick(do)).reshape(N_CHIPS, d // N_CHIPS, d)
    dpa, dpb, dga, dgb, dy_sgu, dy_attn = _mix_bwd(tag + "mix", do, p["w_out"], p["proj_a"], p["proj_b"], sv["ga_pre"],
                                                  sv["gb_pre"], sv["pa"], sv["pb"])
    g["proj_a"] = _matmul_tn(tag + "dproj_a", sv["y_sgu"], dpa).reshape(N_CHIPS, d // N_CHIPS, d)
    g["proj_b"] = _matmul_tn(tag + "dproj_b", sv["y_attn"], dpb).reshape(N_CHIPS, d // N_CHIPS, d)
    du, dv, g["sgu_w"], g["sgu_b"], g["sgu_ln_w"], g["sgu_ln_b"] = _sgu_bwd(
        tag + "sgu", sv["u_pre"], sv["v_pre"], p["sgu_ln_w"], p["sgu_ln_b"], p["sgu_w"], p["sgu_bfull"], dy_sgu,
        after=ready(("w_out", "proj_a", "proj_b")))
    dqr, dkv_cur, dkv_prev, dsink = _attention_bwd(tag + "attn", sv["qr"], sv["kk0"], sv["kk1"], sv["vv0"], sv["vv1"],
                                                   p["sinks"], dy_attn, after=tick(du))
    g["sinks"] = dsink[0, :N_Q_HEADS]
    dq, dkv = _rope_bwd(tag + "rope", dqr, dkv_cur, dkv_prev, cosf, sinf)
    dw_in = _matmul_tn_segments(tag + "dw_in", [dq, dkv, du, dv, dga, dgb], sv["h"])
    g["w_in"] = dw_in.reshape(N_CHIPS, IN_COLS // N_CHIPS, d)
    dx, da1, dsh1 = _in_proj_bwd(tag + "in_proj", dq, dkv, du, dv, dga, dgb, p["w_in"], sv["x"], p["norm1_w"], sc1, dx1)
    g["norm1_w"] = da1 * (1.0 + sc1)
    dsc1 = da1 * p["norm1_w"]
    return dx, (dsh1, dsc1, dg1, dsh2, dsc2, dg2), g


def _pad_to(a, axis, size):
    pad = [(0, 0)] * a.ndim
    pad[axis] = (0, size - a.shape[axis])
    return jnp.pad(a, pad)


def _early_params(w_in, small):
    d = D_MODEL
    return dict(
        w_in=w_in.reshape(IN_COLS, d), norm1_w=small["norm1_w"].reshape(1, d), sinks=small["sinks"],
        sgu_ln_w=small["sgu_ln_w"].reshape(1, d), sgu_ln_b=small["sgu_ln_b"].reshape(1, d), sgu_w=small["sgu_w"],
        sgu_bfull=jnp.broadcast_to(small["sgu_b"][:, :, None], (SGU_GROUPS, SGU_CHUNK, SGU_CHUNK)))


def _mix_params(proj_a, proj_b, w_out, small):
    return dict(proj_a=proj_a, proj_b=proj_b, w_out=w_out, norm2_w=small["norm2_w"].reshape(1, D_MODEL))


def _ffn_params(w_gate, w_up, w_down, conv_w, small):
    return dict(
        w_gate=w_gate, w_up=w_up, w_down=w_down, conv_w=conv_w.transpose(1, 0, 2).reshape(3, FFN_DIM),
        conv_b=small["conv_b"].reshape(1, FFN_DIM))


def _layer_params(w_in, proj_a, proj_b, w_out, w_gate, w_up, w_down, conv_w, small):
    return dict(_early_params(w_in, small), **_mix_params(proj_a, proj_b, w_out, small),
                **_ffn_params(w_gate, w_up, w_down, conv_w, small))


def _conv_grads_natural(g):
    cw = g["conv_w"]
    cb = g["conv_b"].reshape(FFN_DIM)
    return cw, cb


def _rope_tables(positions):
    inv_freq = ROPE_THETA ** (-jnp.arange(0, ROT_DIM, 2, dtype=F32) / ROT_DIM)
    ang = positions.astype(F32)[:, None] * inv_freq
    cos, sin = jnp.cos(ang), jnp.sin(ang)
    s = positions.shape[0]
    rest = HEAD_DIM - ROT_DIM
    cos_head = jnp.concatenate([cos, cos, jnp.ones((s, rest), F32)], axis=1)
    sin_head = jnp.concatenate([-sin, sin, jnp.zeros((s, rest), F32)], axis=1)
    return jnp.tile(cos_head, (1, LANES // HEAD_DIM)), jnp.tile(sin_head, (1, LANES // HEAD_DIM))


ADA_ROWS = 16


def _ada_fwd(name, c_rows, ada_w, ada_b_cols, tn=512):
    depth, d, n = ada_w.shape

    def body(c_ref, w_ref, b_ref, o_ref):
        cv = c_ref[...]
        act = (cv * _sigmoid(cv)).astype(BF)
        o_ref[0] = _dot(act, w_ref[0].astype(BF)) + b_ref[0]

    return _call(body, name, _sds((depth, ADA_ROWS, n), F32), (depth, n // tn),
                 [pl.BlockSpec((ADA_ROWS, d), lambda l, j: (0, 0)), pl.BlockSpec((1, d, tn), lambda l, j: (l, 0, j)),
                  pl.BlockSpec((1, 1, tn), lambda l, j: (l, 0, j))],
                 pl.BlockSpec((1, ADA_ROWS, tn), lambda l, j: (l, 0, j)), ("parallel", "parallel"))(c_rows, ada_w, ada_b_cols)


def _ada_bwd(name, c_rows, dmod_cols, tn=512):
    depth, _, n = dmod_cols.shape
    d = c_rows.shape[1]

    def body(c_ref, dm_ref, o_ref):
        cv = c_ref[...]
        act = (cv * _sigmoid(cv)).astype(BF)
        o_ref[0] = _dot_tn(act, dm_ref[0].astype(BF))

    return _call(body, name, _sds((depth, d, n), F32), (depth, n // tn),
                 [pl.BlockSpec((ADA_ROWS, d), lambda l, j: (0, 0)), pl.BlockSpec((1, ADA_ROWS, tn), lambda l, j: (l, 0, j))],
                 pl.BlockSpec((1, d, tn), lambda l, j: (l, 0, j)), ("parallel", "parallel"))(c_rows, dmod_cols)


def _colsum(name, a):
    r, n = a.shape

    def body(a_ref, o_ref):
        o_ref[...] = jnp.sum(a_ref[...], axis=0, keepdims=True)

    return _call(body, name, _sds((1, n), F32), (1,), [pl.BlockSpec((r, n), lambda i: (0, 0))],
                 pl.BlockSpec((1, n), lambda i: (0, 0)), ("arbitrary",))(a)


REL_SIBLING = (0, 0, 1)
REL_CHIPS = ((1, 0, 0), (0, 1, 0), (1, 1, 0))
REL_ALL = tuple((fx, fy, fc) for fx in (0, 1) for fy in (0, 1) for fc in (0, 1) if fx or fy or fc)


def _chip_of(dev):
    return 2 * dev[0] + dev[1]


def _dev_of(dev):
    return 4 * dev[0] + 2 * dev[1] + dev[2]


def _flip(dev, rel):
    return tuple(1 - m if f else m for m, f in zip(dev, rel))


def _exchange(name, arrays, n_out, stages, aliases=None):
    out_shapes, stages = stages[0], stages[1:]
    n_in = len(arrays)
    aliases = aliases or {}
    n_remote = sum(len(plan) for plan, _ in stages)
    n_local = sum(len(local) for _, local in stages)

    def at(ref, idx):
        return ref.at[idx] if len(idx) else ref

    def body(*refs):
        bufs = list(refs[:n_in + n_out])
        for i_in, i_out in aliases.items():
            bufs[i_in] = bufs[n_in + i_out]
        send_sems, recv_sems, local_sems = refs[n_in + n_out:]
        me = (lax.axis_index("x"), lax.axis_index("y"), lax.axis_index("c"))
        base_r = base_l = 0
        pending = []
        for plan, local in stages:
            def remote(k, entry, sender, receiver):
                rel, si, ssel, di, dsel = entry
                return pltpu.make_async_remote_copy(
                    src_ref=at(bufs[si], ssel(sender, receiver)), dst_ref=at(bufs[di], dsel(sender, receiver)),
                    send_sem=send_sems.at[k], recv_sem=recv_sems.at[k], device_id=_flip(me, rel), device_id_type=MESH)

            sends = [remote(base_r + k, e, me, _flip(me, e[0])) for k, e in enumerate(plan)]
            for cp in sends:
                cp.start()
            for k, (si, ssel, di, dsel) in enumerate(local):
                cp = pltpu.make_async_copy(at(bufs[si], ssel(me)), at(bufs[di], dsel(me)), local_sems.at[base_l + k])
                cp.start()
                pending.append(cp.wait)
            for k, e in enumerate(plan):
                remote(base_r + k, e, _flip(me, e[0]), me).wait_recv()
            pending += [cp.wait_send for cp in sends]
            base_r += len(plan)
            base_l += len(local)
        for wait in pending:
            wait()

    any_spec = pl.BlockSpec(memory_space=pl.ANY)
    return pl.pallas_call(
        body, name=name, out_shape=tuple(out_shapes), in_specs=[any_spec] * n_in, out_specs=tuple([any_spec] * n_out),
        input_output_aliases=dict(aliases),
        scratch_shapes=[pltpu.SemaphoreType.DMA((max(n_remote, 1),)), pltpu.SemaphoreType.DMA((max(n_remote, 1),)),
                        pltpu.SemaphoreType.DMA((max(n_local, 1),))])(*arrays)


HBM_SPEC = pl.BlockSpec(memory_space=pltpu.HBM)
SEM_SPEC = pl.BlockSpec(memory_space=pltpu.SEMAPHORE)


def _split_copies(bufs, plan, local, send_sems, recv_sems, local_sems):
    me = (lax.axis_index("x"), lax.axis_index("y"), lax.axis_index("c"))

    def at(ref, idx):
        return ref.at[idx] if len(idx) else ref

    def remote(k, sender, receiver):
        rel, si, ssel, di, dsel = plan[k]
        return pltpu.make_async_remote_copy(
            src_ref=at(bufs[si], ssel(sender, receiver)), dst_ref=at(bufs[di], dsel(sender, receiver)),
            send_sem=send_sems.at[k], recv_sem=recv_sems.at[k], device_id=_flip(me, rel), device_id_type=MESH)

    sends = [remote(k, me, _flip(me, plan[k][0])) for k in range(len(plan))]
    arrivals = [remote(k, _flip(me, plan[k][0]), me) for k in range(len(plan))]
    locs = [pltpu.make_async_copy(at(bufs[si], ssel(me)), at(bufs[di], dsel(me)), local_sems.at[k])
            for k, (si, ssel, di, dsel) in enumerate(local)]
    return sends, arrivals, locs


def _exchange_start(name, arrays, out_shapes, plan, local, inplace=False):
    n_in, n_out = len(arrays), len(out_shapes)
    n_buf = n_in + n_out

    def body(*refs):
        sems = refs[n_buf:n_buf + 3]
        bufs = refs[n_buf + 3:2 * n_buf + 3]
        sends, _, locs = _split_copies(bufs * 2 if inplace else bufs, plan, local, *sems)
        for cp in sends + locs:
            cp.start()
        refs[-1][...] = jnp.zeros_like(refs[-1])

    zones = [lax.empty(o.shape, o.dtype) for o in out_shapes]
    operands = [pltpu.with_memory_space_constraint(a, pltpu.HBM) for a in list(arrays) + zones]
    sem = lambda n: pltpu.SemaphoreType.DMA((max(n, 1),))
    out = pl.pallas_call(
        body, name=name,
        out_shape=(sem(len(plan)), sem(len(plan)), sem(len(local)), *[pltpu.HBM(a.shape, a.dtype) for a in operands],
                   _sds((SUBLANES, LANES), F32)),
        in_specs=[HBM_SPEC] * n_buf,
        out_specs=(SEM_SPEC, SEM_SPEC, SEM_SPEC, *[HBM_SPEC] * n_buf, pl.BlockSpec(memory_space=pltpu.VMEM)),
        input_output_aliases={i: 3 + i for i in range(n_buf)},
        compiler_params=pltpu.CompilerParams(has_side_effects=pltpu.SideEffectType.DATAFLOW_SIDE_EFFECTING))(*operands)
    pending = dict(name=name, sems=out[:3], thru=out[3:3 + n_in], zones=out[3 + n_in:3 + n_buf], plan=plan, local=local,
                   inplace=inplace)
    return pending, out[-1]


def _exchange_wait(pending, after, both=False):
    thru, zones, plan, local, inplace = (pending[k] for k in ("thru", "zones", "plan", "local", "inplace"))
    n_in, n_buf = len(thru), len(thru) + len(zones)

    def body(*refs):
        bufs = refs[:n_buf]
        sends, arrivals, locs = _split_copies(bufs * 2 if inplace else bufs, plan, local, *refs[n_buf:n_buf + 3])
        for cp in arrivals:
            cp.wait_recv()
        for cp in sends:
            cp.wait_send()
        for cp in locs:
            cp.wait()

    out = pl.pallas_call(
        body, name=pending["name"] + "_wait", out_shape=tuple(pltpu.HBM(a.shape, a.dtype) for a in list(thru) + list(zones)),
        in_specs=[HBM_SPEC] * n_buf + [SEM_SPEC] * 3 + [pl.BlockSpec(memory_space=pl.ANY)],
        out_specs=tuple([HBM_SPEC] * n_buf), input_output_aliases={i: i for i in range(n_buf)},
        compiler_params=pltpu.CompilerParams(has_side_effects=pltpu.SideEffectType.DATAFLOW_SIDE_EFFECTING))(
            *thru, *zones, *pending["sems"], after)
    if both:
        return out[:n_in], out[n_in:]
    return out[:n_in] if inplace else out[n_in:]


def _whole(*_):
    return ()


def _half_rows(rows, core):
    return pl.ds(core * (rows // 2), rows // 2)


def _gather_weights_plan(shards):
    n = len(shards)
    dsts = [_sds((N_CHIPS,) + a.shape, a.dtype) for a in shards]
    fetch, forward = [], []
    for t, a in enumerate(shards):
        rows = a.shape[0]
        if rows % (2 * 16) == 0:
            fetch += [(rel, t, (lambda s_, r_, rows=rows: (_half_rows(rows, s_[2]),)), n + t,
                       (lambda s_, r_, rows=rows: (_chip_of(s_), _half_rows(rows, s_[2])))) for rel in REL_CHIPS]
            forward += [(REL_SIBLING, n + t, (lambda s_, r_, rows=rows, rel=rel: (_chip_of(_flip(s_, rel)), _half_rows(rows, s_[2]))),
                         n + t, (lambda s_, r_, rows=rows, rel=rel: (_chip_of(_flip(s_, rel)), _half_rows(rows, s_[2]))))
                        for rel in REL_CHIPS]
        else:
            fetch += [(rel, t, _whole, n + t, lambda s_, r_: (_chip_of(s_),)) for rel in REL_CHIPS]
    local = [(t, _whole, n + t, lambda me: (_chip_of(me),)) for t in range(n)]
    return dsts, fetch, local, forward


def _gather_weights_start(name, shards):
    dsts, fetch, local, forward = _gather_weights_plan(shards)
    pending, token = _exchange_start(name, shards, dsts, fetch, local)
    return dict(pending, forward=forward), token


def _gather_weights_finish(pending, after):
    landed = _exchange_wait(pending, after)
    n = len(landed)
    return _exchange(pending["name"] + "_forward", landed, n, [[_sds(a.shape, a.dtype) for a in landed], (pending["forward"], [])],
                     aliases={t: t for t in range(n)})


def _gather_chips_plan(arrays):
    n = len(arrays)
    dsts = [_sds((N_CHIPS,) + a.shape, a.dtype) for a in arrays]
    plan = [(rel, t, _whole, n + t, lambda s_, r_: (_chip_of(s_),)) for t in range(n) for rel in REL_CHIPS]
    local = [(t, _whole, n + t, lambda me: (_chip_of(me),)) for t in range(n)]
    return dsts, plan, local


def _gather_chips(name, arrays):
    dsts, plan, local = _gather_chips_plan(arrays)
    return _exchange(name, arrays, len(arrays), [dsts, (plan, local)])


def _gather_chips_start(name, arrays):
    dsts, plan, local = _gather_chips_plan(arrays)
    return _exchange_start(name, arrays, dsts, plan, local)


def _gather_all(name, a):
    plan = [(rel, 0, _whole, 1, lambda s_, r_: (_dev_of(s_),)) for rel in REL_ALL]
    local = [(0, _whole, 1, lambda me: (_dev_of(me),))]
    return _exchange(name, [a], 1, [[_sds((2 * N_CHIPS,) + a.shape, a.dtype)], (plan, local)])[0]


def _swap_halves_start(name, grads):
    n = len(grads)
    dsts = [_sds((g.shape[0], g.shape[1] // 2, g.shape[2]), g.dtype) for g in grads]
    plan = [(REL_SIBLING, t, (lambda s_, r_, rows=g.shape[1]: (pl.ds(0, N_CHIPS), _half_rows(rows, r_[2]))), n + t, _whole)
            for t, g in enumerate(grads)]
    return _exchange_start(name, grads, dsts, plan, [])


def _scatter_chips_plan(sums):
    n = len(sums)
    dsts = [_sds(a.shape, a.dtype) for a in sums]
    plan = [(rel, t, lambda s_, r_: (_chip_of(r_),), n + t, lambda s_, r_: (_chip_of(s_),))
            for t in range(n) for rel in REL_CHIPS]
    local = [(t, lambda me: (_chip_of(me),), n + t, lambda me: (_chip_of(me),)) for t in range(n)]
    return dsts, plan, local


def _scatter_chips_start(name, sums):
    dsts, plan, local = _scatter_chips_plan(sums)
    return _exchange_start(name, sums, dsts, plan, local)


def _swap_back_start(name, totals, layer):
    n = len(totals)
    plan = [(REL_SIBLING, n + t, (lambda s_, r_, rows=a.shape[1]: (layer, _half_rows(rows, s_[2]))),
             n + t, (lambda s_, r_, rows=a.shape[1]: (layer, _half_rows(rows, s_[2])))) for t, a in enumerate(totals)]
    return _exchange_start(name, totals, [], plan, [], inplace=True)


def _add_halves(name, gs, recvs, core):
    n = len(gs)
    nch = recvs[0].shape[0]

    def body(core_ref, *refs):
        for g_ref, r_ref, o_ref in zip(refs[:n], refs[n:2 * n], refs[2 * n:]):
            o_ref[0] = (g_ref[0, 0].astype(F32) + r_ref[0].astype(F32)).astype(o_ref.dtype)

    halves = [pl.BlockSpec((1,) + r.shape[1:], lambda k, core_ref: (k, 0, 0)) for r in recvs]
    spec = pltpu.PrefetchScalarGridSpec(
        num_scalar_prefetch=1, grid=(nch,),
        in_specs=[pl.BlockSpec((1, 1) + r.shape[1:], lambda k, core_ref: (k, core_ref[0], 0, 0)) for r in recvs] + halves,
        out_specs=tuple(halves))
    return pl.pallas_call(body, name=name, out_shape=tuple(_sds(r.shape, r.dtype) for r in recvs), grid_spec=spec,
                          compiler_params=pltpu.CompilerParams(dimension_semantics=("parallel",),
                                                               vmem_limit_bytes=VMEM_LIMIT))(
                                                                   core, *[g.reshape(nch, 2, *r.shape[1:]) for g, r in zip(gs, recvs)],
                                                                   *recvs)


def _sum_chips(name, arrays, core, layer, totals):
    n = len(arrays)
    nch = arrays[0].shape[0]
    parts = 2
    shapes = [(a.shape[1] // parts, a.shape[2]) for a in arrays]

    def body(core_ref, *refs):
        for a_ref, o_ref in zip(refs[:n], refs[-n:]):
            acc = a_ref[0].astype(F32)
            for k in range(1, nch):
                acc = acc + a_ref[k].astype(F32)
            o_ref[0, 0, 0] = acc

    in_specs = [pl.BlockSpec((nch,) + shp, lambda i, core_ref: (0, i, 0)) for shp in shapes]
    args = [core, *arrays]
    if totals is not None:
        in_specs += [pl.BlockSpec(memory_space=pl.ANY)] * n
        args += [t.reshape(DEPTH, 2, parts, *shp) for t, shp in zip(totals, shapes)]
    spec = pltpu.PrefetchScalarGridSpec(
        num_scalar_prefetch=1, grid=(parts,), in_specs=in_specs,
        out_specs=tuple(pl.BlockSpec((1, 1, 1) + shp, lambda i, core_ref: (layer, core_ref[0], i, 0, 0)) for shp in shapes))
    outs = pl.pallas_call(body, name=name, out_shape=tuple(_sds((DEPTH, 2, parts) + shp, F32) for shp in shapes), grid_spec=spec,
                          input_output_aliases={1 + n + t: t for t in range(n)} if totals is not None else {},
                          compiler_params=pltpu.CompilerParams(dimension_semantics=("arbitrary",),
                                                               vmem_limit_bytes=VMEM_LIMIT))(*args)
    return [o.reshape(DEPTH, 2 * parts * shp[0], shp[1]) for o, shp in zip(outs, shapes)]


def _adamw_update(w, g, m, v):
    mn = ADAM_B1 * m + (1.0 - ADAM_B1) * g
    vn = ADAM_B2 * v + (1.0 - ADAM_B2) * (g * g)
    m_hat = mn / (1.0 - ADAM_B1 ** ADAM_STEP)
    v_hat = vn / (1.0 - ADAM_B2 ** ADAM_STEP)
    return -ADAM_LR * (m_hat / (jnp.sqrt(v_hat) + ADAM_EPS) + ADAM_WD * w), mn, vn


def _adamw(name, w, g, m, v, after=None):
    depth, r, c = w.shape
    tr = next(t for t in (512, 448, 384, 352, 336, 256, 192, 128, 64, 32, 16, 8) if r % t == 0 and t * c <= ADAM_TILE_ELEMS)

    def body(w_ref, g_ref, m_ref, v_ref, go_ref, d_ref, mo_ref, vo_ref):
        gv = g_ref[...]
        go_ref[...] = gv
        d_ref[...], mo_ref[...], vo_ref[...] = _adamw_update(w_ref[...], gv, m_ref[...], v_ref[...])

    spec = pl.BlockSpec((1, tr, c), lambda l, i: (l, i, 0))
    o = _sds(w.shape, F32)
    return _call(body, name, (o, o, o, o), (depth, r // tr), [spec] * 4, (spec,) * 4, ("parallel", "parallel"),
                 after=after)(w, g, m, v)


def _adamw_small(name, ws, gs, ms, vs):
    n = len(ws)

    def body(*refs):
        for t in range(n):
            w_ref, g_ref, m_ref, v_ref = (refs[k * n + t] for k in range(4))
            d_ref, mo_ref, vo_ref = (refs[(4 + k) * n + t] for k in range(3))
            d_ref[...], mo_ref[...], vo_ref[...] = _adamw_update(w_ref[...], g_ref[...], m_ref[...], v_ref[...])

    outs = [_sds(w.shape, F32) for w in ws]
    res = pl.pallas_call(body, name=name, out_shape=tuple(outs * 3))(*ws, *gs, *ms, *vs)
    return res[:n], res[n:2 * n], res[2 * n:]


def _pack(arrays, rows):
    flat = jnp.concatenate([a.reshape(-1).astype(F32) for a in arrays])
    return _pad_to(flat, 0, rows * LANES).reshape(rows, LANES)


def _unpack(packed, shapes):
    flat = packed.reshape(-1)
    out, off = [], 0
    for shp in shapes:
        n = 1
        for s_ in shp:
            n *= s_
        out.append(flat[off:off + n].reshape(shp))
        off += n
    return out


_MATRICES = ("w_in", "proj_a", "proj_b", "w_out", "w_gate", "w_up", "w_down")
_SMALL = (("norm1_w", (D_MODEL,)), ("sinks", (N_Q_HEADS,)), ("sgu_ln_w", (SGU_WIDTH,)), ("sgu_ln_b", (SGU_WIDTH,)),
          ("sgu_w", (SGU_GROUPS, SGU_CHUNK, SGU_CHUNK)), ("sgu_b", (SGU_GROUPS, SGU_CHUNK)), ("norm2_w", (D_MODEL,)),
          ("conv_w", (3, FFN_DIM)), ("conv_b", (FFN_DIM,)), ("final_norm_w", (D_MODEL,)))
SMALL_ROWS = 320
ADAM_TILE_ELEMS = 384 * 1024


def _reduce_cores_start(tag, partial):
    names = list(partial)
    pending, token = _swap_halves_start(tag + "_cores", [partial[k] for k in names])
    return dict(pending, tag=tag, names=names), token


def _reduce_chips_start(pending, core, after):
    tag, names = pending["tag"], pending["names"]
    mine, theirs = _exchange_wait(pending, after, both=True)
    sums = _add_halves(tag + "_cores_add", mine, theirs, core)
    scatter, token = _scatter_chips_start(tag + "_chips", sums)
    return dict(scatter, tag=tag, names=names), token


def _reduce_back_start(pending, l, core, totals, after):
    tag, names = pending["tag"], pending["names"]
    before = [totals[k] for k in names] if names[0] in totals else None
    sums = _sum_chips(tag + "_chips_add", _exchange_wait(pending, after), core, l, before)
    back, _ = _swap_back_start(tag + "_back", sums, l)
    return dict(back, names=names)


def kernel(x, c, positions, ada_w, ada_b, norm1_w, w_in, attn_sinks, sgu_ln_w, sgu_ln_b, sgu_w, sgu_b, proj_a, proj_b, w_out, norm2_w, ffn_w_gate, ffn_w_up, ffn_conv_w, ffn_conv_b, ffn_w_down, final_norm_w, loss_target, m_ada_w, m_ada_b, m_norm1_w, m_w_in, m_attn_sinks, m_sgu_ln_w, m_sgu_ln_b, m_sgu_w, m_sgu_b, m_proj_a, m_proj_b, m_w_out, m_norm2_w, m_ffn_w_gate, m_ffn_w_up, m_ffn_conv_w, m_ffn_conv_b, m_ffn_w_down, m_final_norm_w, v_ada_w, v_ada_b, v_norm1_w, v_w_in, v_attn_sinks, v_sgu_ln_w, v_sgu_ln_b, v_sgu_w, v_sgu_b, v_proj_a, v_proj_b, v_w_out, v_norm2_w, v_ffn_w_gate, v_ffn_w_up, v_ffn_conv_w, v_ffn_conv_b, v_ffn_w_down, v_final_norm_w):
    d = D_MODEL
    ax, ay, ac = lax.axis_index("x"), lax.axis_index("y"), lax.axis_index("c")
    chip = 2 * ax + ay
    dev = 4 * ax + 2 * ay + ac
    core = ac.astype(jnp.int32).reshape(1)

    c_all = _gather_all("gather_cond", c.reshape(SUBLANES, d // SUBLANES)).reshape(2 * N_CHIPS, d)
    c_rows = _pad_to(c_all, 0, ADA_ROWS)
    ada_cols = ada_w.shape[2]
    ada_b_cols = lax.dynamic_slice_in_dim(ada_b, chip * ada_cols, ada_cols, axis=1).reshape(DEPTH, 1, ada_cols)
    mod_cols = _ada_fwd("ada_fwd", c_rows, ada_w, ada_b_cols)
    mod_all = _gather_chips("gather_mod", [mod_cols])[0]
    mod_mine = lax.dynamic_index_in_dim(mod_all, dev, axis=2, keepdims=False)
    mod_mine = mod_mine.transpose(1, 0, 2).reshape(DEPTH, 1, 6 * d)
    mods = [tuple(jnp.split(mod_mine[l], 6, axis=-1)) for l in range(DEPTH)]

    tr = lambda a: jnp.swapaxes(a, 1, 2)
    shards = [tr(w_in).astype(BF), proj_a.astype(BF), proj_b.astype(BF), w_out.astype(BF),
              tr(ffn_w_gate).astype(BF), tr(ffn_w_up).astype(BF), ffn_w_down.astype(BF), ffn_conv_w]
    token = mod_all[0, 0, :SUBLANES, :LANES]
    fetches = []
    for l in range(DEPTH):
        groups = []
        for tag, members in (("in", shards[:1]), ("mix", shards[1:4]), ("ffn", shards[4:])):
            behind = (token[0, 0] * 0.0).astype(members[0].dtype)
            pending, token = _gather_weights_start(f"l{l}_gather_{tag}", [members[0][l] + behind] + [a[l] for a in members[1:]])
            groups.append(pending)
        fetches.append(groups)

    small_in = dict(norm1_w=norm1_w, sinks=attn_sinks, sgu_ln_w=sgu_ln_w, sgu_ln_b=sgu_ln_b, sgu_w=sgu_w, sgu_b=sgu_b,
                    norm2_w=norm2_w, conv_b=ffn_conv_b)
    cosf, sinf = _rope_tables(positions[0])
    small_of = lambda l: {k: v[l] for k, v in small_in.items()}

    h = x[0]
    saved, params = [], []
    for l in range(DEPTH):
        first, mix, ffn = fetches[l]
        w_in_l = _gather_weights_finish(first, token if l == 0 else h)
        late = lambda y, l=l, mix=mix: _mix_params(*_gather_weights_finish(mix, y), small_of(l))
        later = lambda y, l=l, ffn=ffn: _ffn_params(*_gather_weights_finish(ffn, y), small_of(l))
        h, sv, p = _layer_fwd(l, h, mods[l], _early_params(w_in_l[0], small_of(l)), cosf, sinf, late=late, later=later)
        saved.append(sv)
        params.append(p)
    dx, loss_part, d_final = _loss_head("loss_head", h, final_norm_w.reshape(1, d), loss_target[0])
    loss = lax.psum(loss_part[0, 0], ("x", "y", "c"))

    def small_pack(l, grads):
        cw, cb = _conv_grads_natural(grads)
        nat = dict(grads, conv_w=cw, conv_b=cb, final_norm_w=d_final if l == DEPTH - 1 else jnp.zeros((d,), F32))
        return _pack([nat[k] for k, _ in _SMALL], N_CHIPS * SMALL_ROWS).reshape(N_CHIPS, SMALL_ROWS, LANES)

    waiting, inflight = [], []

    def send(tag, partial):
        pending, token = _reduce_cores_start(tag, partial)
        waiting.append(pending)
        return token

    def tick(y):
        token = None
        while waiting:
            pending, token = _reduce_chips_start(waiting.pop(0), core, y)
            inflight.append(pending)
        return token

    dmods = [None] * DEPTH
    dx, dmods[1], grads = _layer_bwd(1, dx, mods[1], params[1], saved[1], cosf, sinf)
    token = send("l1_reduce", dict({k: grads[k] for k in _MATRICES}, small=small_pack(1, grads)))
    dx, dmods[0], grads = _layer_bwd(0, dx, mods[0], params[0], saved[0], cosf, sinf, after=token,
                                     emit=lambda part: send("l0_reduce_" + "_".join(part), part), tick=tick)
    dmod_mine = jnp.concatenate([jnp.concatenate(dmods[l], axis=1) for l in range(DEPTH)], axis=1)
    dmod_all = _gather_all("gather_dmod", dmod_mine.reshape(SUBLANES, -1)).reshape(2 * N_CHIPS, DEPTH * 6 * d)
    send("l0_reduce_in", dict(w_in=grads["w_in"], small=small_pack(0, grads) + dmod_all[0, 0] * 0.0))
    tick(dmod_all)

    totals, flying = {}, None

    def land(after):
        if flying is not None:
            totals.update(zip(flying["names"], _exchange_wait(flying, after)))

    for pending in inflight[:-1]:
        land(dx)
        flying = _reduce_back_start(pending, 1 if pending["tag"].startswith("l1") else 0, core, totals, dx)
    land(dx)
    flying = None

    g_ada_b = _colsum("ada_b_grad", dmod_all).reshape(DEPTH, 6 * d)
    dmod_cols = jnp.stack([lax.dynamic_slice_in_dim(dmod_all, l * 6 * d + chip * ada_cols, ada_cols, axis=1)
                           for l in range(DEPTH)])
    g_ada_w = _ada_bwd("ada_w_grad", c_rows, _pad_to(dmod_cols, 1, ADA_ROWS))
    big = dict(w_in=(tr(w_in), tr(m_w_in), tr(v_w_in)), proj_a=(proj_a, m_proj_a, v_proj_a), proj_b=(proj_b, m_proj_b, v_proj_b),
               w_out=(w_out, m_w_out, v_w_out), w_gate=(tr(ffn_w_gate), tr(m_ffn_w_gate), tr(v_ffn_w_gate)),
               w_up=(tr(ffn_w_up), tr(m_ffn_w_up), tr(v_ffn_w_up)), w_down=(ffn_w_down, m_ffn_w_down, v_ffn_w_down))
    upd, g_big, done = {}, {}, {}

    def update(k, after=None):
        res = _adamw("adamw_" + k, big[k][0], totals[k], *big[k][1:], after=after)
        done[k] = res[1]
        res = [tr(a) for a in res] if k in ("w_in", "w_gate", "w_up") else res
        g_big[k], upd[k] = res[0], res[1:]

    for k in ("w_down", "w_gate", "w_up", "w_out", "proj_a", "proj_b"):
        update(k)
    g_big["ada_w"], *upd["ada_w"] = _adamw("adamw_ada_w", ada_w, g_ada_w, m_ada_w, v_ada_w)
    flying = _reduce_back_start(inflight[-1], 0, core, totals, upd["ada_w"][0])
    land(upd["ada_w"][1])
    gathering, token = _gather_chips_start("gather_small", [totals["small"]])
    update("w_in", after=token)
    small_all = _exchange_wait(gathering, done["w_in"])[0]
    small_g = small_all.transpose(1, 0, 2, 3).reshape(DEPTH, -1)
    per_layer = [_unpack(small_g[l], [shp for _, shp in _SMALL]) for l in range(DEPTH)]
    sg = {k: jnp.stack([per_layer[l][i] for l in range(DEPTH)]) for i, (k, _) in enumerate(_SMALL)}
    g_final = sg["final_norm_w"][DEPTH - 1]
    g_conv_w = lax.dynamic_slice_in_dim(sg["conv_w"], chip * FFN_SHARD, FFN_SHARD, axis=2)

    rest = [("ada_b", ada_b, g_ada_b, m_ada_b, v_ada_b), ("norm1_w", norm1_w, sg["norm1_w"], m_norm1_w, v_norm1_w),
            ("attn_sinks", attn_sinks, sg["sinks"], m_attn_sinks, v_attn_sinks),
            ("sgu_ln_w", sgu_ln_w, sg["sgu_ln_w"], m_sgu_ln_w, v_sgu_ln_w),
            ("sgu_ln_b", sgu_ln_b, sg["sgu_ln_b"], m_sgu_ln_b, v_sgu_ln_b), ("sgu_w", sgu_w, sg["sgu_w"], m_sgu_w, v_sgu_w),
            ("sgu_b", sgu_b, sg["sgu_b"], m_sgu_b, v_sgu_b), ("norm2_w", norm2_w, sg["norm2_w"], m_norm2_w, v_norm2_w),
            ("ffn_conv_w", ffn_conv_w, g_conv_w, m_ffn_conv_w, v_ffn_conv_w),
            ("ffn_conv_b", ffn_conv_b, sg["conv_b"], m_ffn_conv_b, v_ffn_conv_b),
            ("final_norm_w", final_norm_w.reshape(1, d), g_final.reshape(1, d), m_final_norm_w.reshape(1, d),
             v_final_norm_w.reshape(1, d))]
    rest_out = _adamw_small("adamw_rest", *[[r[i] for r in rest] for i in (1, 2, 3, 4)])
    g_rest = {r[0]: r[2] for r in rest}
    u_rest = {r[0]: tuple(o[i] for o in rest_out) for i, r in enumerate(rest)}
    g_rest["final_norm_w"] = g_final
    u_rest["final_norm_w"] = tuple(a.reshape(d) for a in u_rest["final_norm_w"])

    names = ("ada_w", "ada_b", "norm1_w", "w_in", "attn_sinks", "sgu_ln_w", "sgu_ln_b", "sgu_w", "sgu_b", "proj_a", "proj_b",
             "w_out", "norm2_w", "ffn_w_gate", "ffn_w_up", "ffn_conv_w", "ffn_conv_b", "ffn_w_down", "final_norm_w")
    alias = {"ffn_w_gate": "w_gate", "ffn_w_up": "w_up", "ffn_w_down": "w_down"}
    grad_of = lambda n: g_rest[n] if n in g_rest else g_big[alias.get(n, n)]
    upd_of = lambda n: u_rest[n] if n in u_rest else upd[alias.get(n, n)]
    return (loss, dx[None], *[grad_of(n) for n in names], *[upd_of(n)[0] for n in names],
            *[upd_of(n)[1] for n in names], *[upd_of(n)[2] for n in names])
```

```python
import jax
import jax.numpy as jnp
from jax import lax
from jax.experimental import pallas as pl
from jax.experimental.pallas import tpu as pltpu

F32 = jnp.float32
BF = jnp.bfloat16

D_MODEL = 1024
N_Q_HEADS = 16
N_KV_HEADS = 2
HEAD_DIM = 64
ATTN_BLOCK = 128
ROPE_THETA = 500000.0
ROT_DIM = HEAD_DIM // 4
SGU_WIDTH = 1024
SGU_GROUPS = 8
SGU_CHUNK = 128
FFN_DIM = 2816
NORM_EPS = 1e-6
DEPTH = 2
IN_COLS = 5376
N_CHIPS = 4
FFN_SHARD = FFN_DIM // N_CHIPS
CHIPS_PER_CHUNK = 2
FFN_CHUNK = CHIPS_PER_CHUNK * FFN_SHARD
FFN_CHUNKS = FFN_DIM // FFN_CHUNK
LANES = 128
SUBLANES = 8
HALO = 16
VMEM_LIMIT = 56 * 1024 * 1024
NEG_BIG = -1e30

ADAM_LR = 0.001
ADAM_B1 = 0.9
ADAM_B2 = 0.999
ADAM_EPS = 1e-08
ADAM_WD = 0.01
ADAM_STEP = 10

MESH = pl.DeviceIdType.MESH

Q_END = 1024
KV_END = 1280
U_END = 2304
Z_END = 3328
GA_END = 4352


def _sds(shape, dtype):
    return jax.ShapeDtypeStruct(tuple(shape), dtype)


def _call(body, name, out_shape, grid, in_specs, out_specs, semantics, scratch=(), after=None):
    n_in = len(in_specs)
    fn = body
    if after is not None:
        def fn(*refs):
            return body(*refs[:n_in], *refs[n_in + 1:])

        in_specs = list(in_specs) + [pl.BlockSpec(memory_space=pl.ANY)]
    call = pl.pallas_call(
        fn, name=name, out_shape=out_shape, grid=grid, in_specs=in_specs, out_specs=out_specs,
        scratch_shapes=scratch,
        compiler_params=pltpu.CompilerParams(dimension_semantics=semantics, vmem_limit_bytes=VMEM_LIMIT))
    if after is None:
        return call
    return lambda *args: call(*args, after)


def _rows(tm, width, col=0):
    return pl.BlockSpec((tm, width), lambda i: (i, col))


def _vec(width):
    return pl.BlockSpec((1, width), lambda i: (0, 0))


def _resident(shape):
    zeros = (0,) * len(shape)
    return pl.BlockSpec(tuple(shape), lambda *_: zeros, pipeline_mode=pl.Buffered(1))


def _sigmoid(x):
    return 0.5 + 0.5 * jnp.tanh(0.5 * x)


def _gelu(x):
    return 0.5 * x * (1.0 + lax.erf(x * 0.7071067811865476))


def _gelu_and_grad(x):
    cdf = 0.5 * (1.0 + lax.erf(x * 0.7071067811865476))
    return x * cdf, cdf + x * jnp.exp(-0.5 * x * x) * 0.3989422804014327


def _dot(a, b):
    return jnp.dot(a, b, preferred_element_type=F32)


def _dot_nt(a, b):
    return lax.dot_general(a, b, (((1,), (1,)), ((), ())), preferred_element_type=F32)


def _dot_tn(a, b):
    return lax.dot_general(a, b, (((0,), (0,)), ((), ())), preferred_element_type=F32)


def _rms(xv):
    return lax.rsqrt(jnp.mean(xv * xv, axis=-1, keepdims=True) + NORM_EPS)


def _matmul_tn(name, a, b, tk=512, tn=1024, after=None):
    s, k = a.shape
    n = b.shape[1]
    tk, tn = min(tk, k), min(tn, n)

    def body(a_ref, b_ref, o_ref):
        o_ref[...] = _dot_tn(a_ref[...], b_ref[...]).astype(o_ref.dtype)

    return _call(body, name, _sds((k, n), BF), (k // tk, n // tn),
                 [pl.BlockSpec((s, tk), lambda i, j: (0, i)), pl.BlockSpec((s, tn), lambda i, j: (0, j))],
                 pl.BlockSpec((tk, tn), lambda i, j: (i, j)), ("parallel", "parallel"), after=after)(a, b)


def _matmul_tn_segments(name, segments, b, tk=256):
    s, n = b.shape
    counts = [a.shape[1] // tk for a in segments]
    firsts = [sum(counts[:t]) for t in range(len(segments))]

    def body(*refs):
        a_refs, b_ref, o_ref = refs[:len(segments)], refs[len(segments)], refs[-1]
        i = pl.program_id(0)
        for a_ref, first, count in zip(a_refs, firsts, counts):
            @pl.when((i >= first) & (i < first + count))
            def _(a_ref=a_ref):
                o_ref[...] = _dot_tn(a_ref[...], b_ref[...]).astype(BF)

    specs = [pl.BlockSpec((s, tk), lambda i, first=first, count=count: (0, jnp.clip(i - first, 0, count - 1)))
             for first, count in zip(firsts, counts)]
    return _call(body, name, _sds((sum(counts) * tk, n), BF), (sum(counts),), specs + [_resident(b.shape)],
                 pl.BlockSpec((tk, n), lambda i: (i, 0)), ("arbitrary",))(*segments, b)


def _rope_partner(v):
    lane = lax.broadcasted_iota(jnp.int32, (1, LANES), 1) % HEAD_DIM
    return jnp.where(lane < ROT_DIM // 2, pltpu.roll(v, LANES - ROT_DIM // 2, axis=1), pltpu.roll(v, ROT_DIM // 2, axis=1))


def _dup_half(v, half):
    lane = lax.broadcasted_iota(jnp.int32, (1, LANES), 1)
    keep = jnp.where((lane >= HEAD_DIM) == (half == 1), v, 0.0)
    return keep + pltpu.roll(keep, HEAD_DIM, axis=1)


def _in_proj(name, x, w, sc, sh, w_in, cosf, sinf, tm=512, after=None):
    s, d = x.shape
    tm = min(tm, s)

    def body(x_ref, w_ref, sc_ref, sh_ref, win_ref, cos_ref, sin_ref,
             h_ref, qr_ref, kk0_ref, kk1_ref, vv0_ref, vv1_ref, u_ref, v_ref, ga_ref, gb_ref):
        xv = x_ref[...]
        h = ((xv * _rms(xv)) * w_ref[...] * (1.0 + sc_ref[...]) + sh_ref[...]).astype(BF)
        h_ref[...] = h
        cosv, sinv = cos_ref[...], sin_ref[...]
        q = _dot_nt(h, win_ref[:Q_END, :])
        for j in range(D_MODEL // LANES):
            qv = q[:, j * LANES:(j + 1) * LANES]
            qr_ref[:, j * LANES:(j + 1) * LANES] = ((qv * cosv + _rope_partner(qv) * sinv) * ATTN_SCALE).astype(BF)
        kv = _dot_nt(h, win_ref[Q_END:KV_END, :])
        kr = kv[:, :LANES] * cosv + _rope_partner(kv[:, :LANES]) * sinv
        vv = kv[:, LANES:]
        kk0_ref[...] = _dup_half(kr, 0).astype(BF)
        kk1_ref[...] = _dup_half(kr, 1).astype(BF)
        vv0_ref[...] = _dup_half(vv, 0).astype(BF)
        vv1_ref[...] = _dup_half(vv, 1).astype(BF)
        u_ref[...] = _dot_nt(h, win_ref[KV_END:U_END, :])
        v_ref[...] = _dot_nt(h, win_ref[U_END:Z_END, :])
        ga_ref[...] = _dot_nt(h, win_ref[Z_END:GA_END, :]).astype(BF)
        gb_ref[...] = _dot_nt(h, win_ref[GA_END:, :]).astype(BF)

    wide, kvs, pre = _sds((s, d), BF), _sds((s, LANES), BF), _sds((s, d), F32)
    return _call(body, name, (wide, wide, kvs, kvs, kvs, kvs, pre, pre, wide, wide), (s // tm,),
                 [_rows(tm, d), _vec(d), _vec(d), _vec(d), _resident(w_in.shape), _rows(tm, LANES), _rows(tm, LANES)],
                 (_rows(tm, d), _rows(tm, d)) + (_rows(tm, LANES),) * 4 + (_rows(tm, d),) * 4, ("parallel",), after=after)(
                     x, w, sc, sh, w_in, cosf, sinf)


def _in_proj_bwd(name, dq, dkv, du, dv, dga, dgb, w_in, x, w, sc, dx_in, tm=512):
    s, d = x.shape
    tm = min(tm, s)

    def body(dq_ref, dkv_ref, du_ref, dv_ref, dga_ref, dgb_ref, win_ref, x_ref, w_ref, sc_ref, dxin_ref,
             dx_ref, da_ref, dsh_ref):
        @pl.when(pl.program_id(0) == 0)
        def _():
            da_ref[...] = jnp.zeros_like(da_ref)
            dsh_ref[...] = jnp.zeros_like(dsh_ref)

        dh = (_dot(dq_ref[...], win_ref[:Q_END, :]) + _dot(dkv_ref[...], win_ref[Q_END:KV_END, :])
              + _dot(du_ref[...], win_ref[KV_END:U_END, :]) + _dot(dv_ref[...], win_ref[U_END:Z_END, :])
              + _dot(dga_ref[...], win_ref[Z_END:GA_END, :]) + _dot(dgb_ref[...], win_ref[GA_END:, :]))
        xv = x_ref[...]
        r = _rms(xv)
        xn = xv * r
        dxn = dh * (w_ref[...] * (1.0 + sc_ref[...]))
        dx_ref[...] = dxin_ref[...] + r * (dxn - xn * jnp.mean(dxn * xn, axis=-1, keepdims=True))
        da_ref[...] += jnp.sum(dh * xn, axis=0, keepdims=True)
        dsh_ref[...] += jnp.sum(dh, axis=0, keepdims=True)

    return _call(body, name, (_sds((s, d), F32), _sds((1, d), F32), _sds((1, d), F32)), (s // tm,),
                 [_rows(tm, d), _rows(tm, 2 * LANES), _rows(tm, d), _rows(tm, d), _rows(tm, d), _rows(tm, d),
                  _resident(w_in.shape), _rows(tm, d), _vec(d), _vec(d), _rows(tm, d)],
                 (_rows(tm, d), _vec(d), _vec(d)), ("arbitrary",))(dq, dkv, du, dv, dga, dgb, w_in, x, w, sc, dx_in)


def _rope_bwd(name, dqr, dkv_cur, dkv_prev, cosf, sinf, tm=512):
    s = dqr.shape[0]
    tm = min(tm, s)
    steps = s // tm
    per = tm // ATTN_BLOCK
    nb = s // ATTN_BLOCK

    def unrope(v, cosv, sinv):
        return v * cosv - _rope_partner(v) * sinv

    def body(dq_ref, cur_ref, prev_ref, next_ref, cos_ref, sin_ref, dqo_ref, dkvo_ref):
        i = pl.program_id(0)
        cosv, sinv = cos_ref[...], sin_ref[...]
        for j in range(D_MODEL // LANES):
            dqo_ref[:, j * LANES:(j + 1) * LANES] = unrope(dq_ref[:, j * LANES:(j + 1) * LANES], cosv, sinv).astype(BF)
        nxt = jnp.where(i < steps - 1, next_ref[...], 0.0)
        if per > 1:
            shifted = jnp.concatenate([prev_ref[ATTN_BLOCK:, :], nxt], axis=0)
        else:
            shifted = nxt
        tot = cur_ref[...] + shifted
        dkvo_ref[:, :LANES] = unrope(tot[:, :LANES], cosv, sinv).astype(BF)
        dkvo_ref[:, LANES:] = tot[:, LANES:].astype(BF)

    nxt_spec = pl.BlockSpec((ATTN_BLOCK, 2 * LANES), lambda i: (jnp.minimum((i + 1) * per, nb - 1), 0))
    return _call(body, name, (_sds((s, D_MODEL), BF), _sds((s, 2 * LANES), BF)), (steps,),
                 [_rows(tm, D_MODEL), _rows(tm, 2 * LANES), _rows(tm, 2 * LANES), nxt_spec, _rows(tm, LANES),
                  _rows(tm, LANES)],
                 (_rows(tm, D_MODEL), _rows(tm, 2 * LANES)), ("parallel",))(dqr, dkv_cur, dkv_prev, dkv_prev, cosf, sinf)


Q_PER_KV = N_Q_HEADS // N_KV_HEADS
ATTN_SCALE = HEAD_DIM ** -0.5
HEADS_AHEAD_FWD = 2
HEADS_AHEAD_BWD = 3


def _band_mask_t(n):
    kj = lax.broadcasted_iota(jnp.int32, (2 * ATTN_BLOCK, ATTN_BLOCK), 0)
    qi = lax.broadcasted_iota(jnp.int32, (2 * ATTN_BLOCK, ATTN_BLOCK), 1)
    return (kj > qi) & (kj <= qi + ATTN_BLOCK) & ((n > 0) | (kj >= ATTN_BLOCK))


def _softmax_t(raw, allowed, sink):
    sc = jnp.where(allowed, raw, NEG_BIG)
    m = jnp.maximum(jnp.max(sc, axis=0, keepdims=True), sink)
    p = jnp.exp(sc - m)
    esink = jnp.exp(sink - m)
    inv = 1.0 / (jnp.sum(p, axis=0, keepdims=True) + esink)
    return p * inv, esink * inv


def _kv_specs():
    cur = pl.BlockSpec((ATTN_BLOCK, LANES), lambda n: (n, 0))
    prev = pl.BlockSpec((ATTN_BLOCK, LANES), lambda n: (jnp.maximum(n - 1, 0), 0))
    return [prev, cur] * 4


def _attention(name, qr, kk0, kk1, vv0, vv1, sinks):
    s = qr.shape[0]
    nb = s // ATTN_BLOCK

    def body(sink_ref, q_ref, k0p, k0c, k1p, k1c, v0p, v0c, v1p, v1c, y_ref):
        allowed = _band_mask_t(pl.program_id(0))
        upper = lax.broadcasted_iota(jnp.int32, (1, LANES), 1) >= HEAD_DIM
        upper_rows = lax.broadcasted_iota(jnp.int32, (LANES, 1), 0) >= HEAD_DIM
        bands = ((jnp.concatenate([k0p[...], k0c[...]], axis=0), jnp.concatenate([v0p[...], v0c[...]], axis=0)),
                 (jnp.concatenate([k1p[...], k1c[...]], axis=0), jnp.concatenate([v1p[...], v1c[...]], axis=0)))
        vbts = (bands[0][1].T, bands[1][1].T)

        def scores(h):
            hk, j, half = h // Q_PER_KV, (h % Q_PER_KV) // 2, h % 2
            col = (hk * 4 + j) * LANES
            qp = q_ref[:, col:col + LANES]
            return _dot_nt(bands[hk][0], jnp.where(upper if half else jnp.logical_not(upper), qp, jnp.zeros_like(qp)))

        out_t = None
        ahead = [scores(h) for h in range(HEADS_AHEAD_FWD)]
        for h in range(N_Q_HEADS):
            hk, j, half = h // Q_PER_KV, (h % Q_PER_KV) // 2, h % 2
            raw = ahead.pop(0)
            if h + HEADS_AHEAD_FWD < N_Q_HEADS:
                ahead.append(scores(h + HEADS_AHEAD_FWD))
            pn, _ = _softmax_t(raw, allowed, sink_ref[h])
            o_h = _dot(vbts[hk], pn.astype(BF))
            out_t = jnp.where(upper_rows, o_h, out_t) if half else o_h
            if half:
                col = (hk * 4 + j) * LANES
                y_ref[:, col:col + LANES] = out_t.T.astype(BF)

    return _call(body, name, _sds((s, D_MODEL), BF), (nb,),
                 [pl.BlockSpec(memory_space=pltpu.SMEM), pl.BlockSpec((ATTN_BLOCK, D_MODEL), lambda n: (n, 0))] + _kv_specs(),
                 pl.BlockSpec((ATTN_BLOCK, D_MODEL), lambda n: (n, 0)), ("parallel",))(
                     sinks, qr, kk0, kk0, kk1, kk1, vv0, vv0, vv1, vv1)


def _attention_bwd(name, qr, kk0, kk1, vv0, vv1, sinks, dy, after=None):
    s = qr.shape[0]
    nb = s // ATTN_BLOCK

    def body(sink_ref, q_ref, dy_ref, k0p, k0c, k1p, k1c, v0p, v0c, v1p, v1c, dq_ref, cur_ref, prev_ref, dsink_ref):
        @pl.when(pl.program_id(0) == 0)
        def _():
            dsink_ref[...] = jnp.zeros_like(dsink_ref)

        allowed = _band_mask_t(pl.program_id(0))
        lane = lax.broadcasted_iota(jnp.int32, (1, LANES), 1)
        upper = lane >= HEAD_DIM
        upper_rows = lax.broadcasted_iota(jnp.int32, (LANES, 1), 0) >= HEAD_DIM
        bands = ((jnp.concatenate([k0p[...], k0c[...]], axis=0), jnp.concatenate([v0p[...], v0c[...]], axis=0)),
                 (jnp.concatenate([k1p[...], k1c[...]], axis=0), jnp.concatenate([v1p[...], v1c[...]], axis=0)))
        kbts = (bands[0][0].T, bands[1][0].T)

        def scores(h):
            hk, j, half = h // Q_PER_KV, (h % Q_PER_KV) // 2, h % 2
            kb, vb = bands[hk]
            col = (hk * 4 + j) * LANES
            sel = upper if half else jnp.logical_not(upper)
            qp = q_ref[:, col:col + LANES]
            qa = jnp.where(sel, qp, jnp.zeros_like(qp))
            dya = jnp.where(sel, dy_ref[:, col:col + LANES], 0.0).astype(BF)
            return qa, dya, _dot_nt(kb, qa), _dot_nt(vb, dya)

        dsink = jnp.zeros((1, LANES), F32)
        dk_slab = jnp.zeros((2 * ATTN_BLOCK, LANES), F32)
        dv_slab = jnp.zeros((2 * ATTN_BLOCK, LANES), F32)
        dkk = dvv = dq_t = None
        ahead = [scores(h) for h in range(HEADS_AHEAD_BWD)]
        for h in range(N_Q_HEADS):
            hk, j, half = h // Q_PER_KV, (h % Q_PER_KV) // 2, h % 2
            qa, dya, raw, dp = ahead.pop(0)
            if h + HEADS_AHEAD_BWD < N_Q_HEADS:
                ahead.append(scores(h + HEADS_AHEAD_BWD))
            pn, psink = _softmax_t(raw, allowed, sink_ref[h])
            delta = jnp.sum(pn * dp, axis=0, keepdims=True)
            ds = (pn * (dp - delta)).astype(BF)
            dsink = dsink + jnp.where(lane == h, -jnp.sum(psink * delta), 0.0)
            dq_h = _dot(kbts[hk], ds) * ATTN_SCALE
            dq_t = jnp.where(upper_rows, dq_h, dq_t) if half else dq_h
            dk_h, dv_h = _dot(ds, qa), _dot(pn.astype(BF), dya)
            dkk, dvv = (dk_h, dv_h) if h % Q_PER_KV == 0 else (dkk + dk_h, dvv + dv_h)
            if half:
                col = (hk * 4 + j) * LANES
                dq_ref[:, col:col + LANES] = dq_t.T
            if h % Q_PER_KV == Q_PER_KV - 1:
                mine = upper if hk else jnp.logical_not(upper)
                dk_slab = jnp.where(mine, dkk + pltpu.roll(dkk, HEAD_DIM, axis=1), dk_slab)
                dv_slab = jnp.where(mine, dvv + pltpu.roll(dvv, HEAD_DIM, axis=1), dv_slab)
        prev_ref[:, :LANES] = dk_slab[:ATTN_BLOCK]
        prev_ref[:, LANES:] = dv_slab[:ATTN_BLOCK]
        cur_ref[:, :LANES] = dk_slab[ATTN_BLOCK:]
        cur_ref[:, LANES:] = dv_slab[ATTN_BLOCK:]
        dsink_ref[...] += dsink

    blk = pl.BlockSpec((ATTN_BLOCK, D_MODEL), lambda n: (n, 0))
    kvo = pl.BlockSpec((ATTN_BLOCK, 2 * LANES), lambda n: (n, 0))
    return _call(body, name,
                 (_sds((s, D_MODEL), F32), _sds((s, 2 * LANES), F32), _sds((s, 2 * LANES), F32), _sds((1, LANES), F32)),
                 (nb,), [pl.BlockSpec(memory_space=pltpu.SMEM), blk, blk] + _kv_specs(),
                 (blk, kvo, kvo, pl.BlockSpec((1, LANES), lambda n: (0, 0))), ("arbitrary",), after=after)(
                     sinks, qr, dy, kk0, kk0, kk1, kk1, vv0, vv0, vv1, vv1)


def _sgu_weights(wm_ref, g):
    t = lax.broadcasted_iota(jnp.int32, (SGU_CHUNK, SGU_CHUNK), 0)
    sidx = lax.broadcasted_iota(jnp.int32, (SGU_CHUNK, SGU_CHUNK), 1)
    return jnp.where(sidx <= t, wm_ref[g], 0.0).astype(BF)


def _layer_norm_stats(v):
    mu = jnp.mean(v, axis=-1, keepdims=True)
    cen = v - mu
    rstd = lax.rsqrt(jnp.mean(cen * cen, axis=-1, keepdims=True) + NORM_EPS)
    return cen * rstd, rstd


def _sgu(name, u_pre, v_pre, ln_w, ln_b, wm, bfull, tm=512):
    s, w = u_pre.shape
    tm = min(tm, s)

    def body(u_ref, v_ref, lw_ref, lb_ref, wm_ref, b_ref, y_ref):
        vhat, _ = _layer_norm_stats(_gelu(v_ref[...]))
        vn = (vhat * lw_ref[...] + lb_ref[...]).astype(BF)
        for g in range(SGU_GROUPS):
            wg = _sgu_weights(wm_ref, g)
            cols = slice(g * SGU_CHUNK, (g + 1) * SGU_CHUNK)
            for ch in range(tm // SGU_CHUNK):
                rows = slice(ch * SGU_CHUNK, (ch + 1) * SGU_CHUNK)
                f = _dot(wg, vn[rows, cols]) + b_ref[g]
                y_ref[rows, cols] = (_gelu(u_ref[rows, cols]) * f).astype(BF)

    full3 = pl.BlockSpec((SGU_GROUPS, SGU_CHUNK, SGU_CHUNK), lambda i: (0, 0, 0))
    return _call(body, name, _sds((s, w), BF), (s // tm,),
                 [_rows(tm, w), _rows(tm, w), _vec(w), _vec(w), full3, full3],
                 _rows(tm, w), ("parallel",))(u_pre, v_pre, ln_w, ln_b, wm, bfull)


def _sgu_bwd(name, u_pre, v_pre, ln_w, ln_b, wm, bfull, dy, tm=512, after=None):
    s, w = u_pre.shape
    tm = min(tm, s)
    steps = s // tm

    def body(u_ref, v_ref, lw_ref, lb_ref, wm_ref, b_ref, dy_ref, du_ref, dv_ref, dwm_ref, db_ref, dlw_ref, dlb_ref,
             dfsum_ref):
        i = pl.program_id(0)

        @pl.when(i == 0)
        def _():
            dwm_ref[...] = jnp.zeros_like(dwm_ref)
            dlw_ref[...] = jnp.zeros_like(dlw_ref)
            dlb_ref[...] = jnp.zeros_like(dlb_ref)
            dfsum_ref[...] = jnp.zeros_like(dfsum_ref)

        vpre = v_ref[...]
        vg, dvg_dv = _gelu_and_grad(vpre)
        vhat, rstd = _layer_norm_stats(vg)
        vn = (vhat * lw_ref[...] + lb_ref[...]).astype(BF)
        t = lax.broadcasted_iota(jnp.int32, (SGU_CHUNK, SGU_CHUNK), 0)
        sidx = lax.broadcasted_iota(jnp.int32, (SGU_CHUNK, SGU_CHUNK), 1)
        dvn_cols = []
        for g in range(SGU_GROUPS):
            wg = _sgu_weights(wm_ref, g)
            cols = slice(g * SGU_CHUNK, (g + 1) * SGU_CHUNK)
            dvn_rows = []
            dwg = jnp.zeros((SGU_CHUNK, SGU_CHUNK), F32)
            dfs = jnp.zeros((SGU_CHUNK, SGU_CHUNK), F32)
            for ch in range(tm // SGU_CHUNK):
                rows = slice(ch * SGU_CHUNK, (ch + 1) * SGU_CHUNK)
                upre = u_ref[rows, cols]
                dyv = dy_ref[rows, cols].astype(F32)
                f = _dot(wg, vn[rows, cols]) + b_ref[g]
                ug, dug_du = _gelu_and_grad(upre)
                du_ref[rows, cols] = (dyv * f * dug_du).astype(BF)
                df = dyv * ug
                dfb = df.astype(BF)
                dvn_rows.append(_dot_tn(wg, dfb))
                dwg = dwg + _dot_nt(dfb, vn[rows, cols])
                dfs = dfs + df
            dwm_ref[g] += jnp.where(sidx <= t, dwg, 0.0)
            dfsum_ref[g] += dfs
            dvn_cols.append(jnp.concatenate(dvn_rows, axis=0) if len(dvn_rows) > 1 else dvn_rows[0])
        dvn = jnp.concatenate(dvn_cols, axis=1)
        dlw_ref[...] += jnp.sum(dvn * vhat, axis=0, keepdims=True)
        dlb_ref[...] += jnp.sum(dvn, axis=0, keepdims=True)
        dvh = dvn * lw_ref[...]
        dvg = rstd * (dvh - jnp.mean(dvh, axis=-1, keepdims=True) - vhat * jnp.mean(dvh * vhat, axis=-1, keepdims=True))
        dv_ref[...] = (dvg * dvg_dv).astype(BF)

        @pl.when(i == steps - 1)
        def _():
            for g in range(SGU_GROUPS):
                db_ref[g:g + 1, :] = jnp.sum(dfsum_ref[g].T, axis=0, keepdims=True)

    full3 = pl.BlockSpec((SGU_GROUPS, SGU_CHUNK, SGU_CHUNK), lambda i: (0, 0, 0))
    return _call(body, name,
                 (_sds((s, w), BF), _sds((s, w), BF), _sds((SGU_GROUPS, SGU_CHUNK, SGU_CHUNK), F32),
                  _sds((SGU_GROUPS, SGU_CHUNK), F32), _sds((1, w), F32), _sds((1, w), F32)),
                 (steps,),
                 [_rows(tm, w), _rows(tm, w), _vec(w), _vec(w), full3, full3, _rows(tm, w)],
                 (_rows(tm, w), _rows(tm, w), full3, pl.BlockSpec((SGU_GROUPS, SGU_CHUNK), lambda i: (0, 0)), _vec(w), _vec(w)),
                 ("arbitrary",), scratch=[pltpu.VMEM((SGU_GROUPS, SGU_CHUNK, SGU_CHUNK), F32)], after=after)(
                     u_pre, v_pre, ln_w, ln_b, wm, bfull, dy)


def _mix_out(name, y_sgu, y_attn, ga_pre, gb_pre, x, g1, proj_a, proj_b, w_out, w2, sc2, sh2, tm=512):
    s, d = x.shape
    tm = min(tm, s)

    def body(ys_ref, ya_ref, ga_ref, gb_ref, x_ref, g1_ref, wa_ref, wb_ref, wo_ref, w2_ref, sc2_ref, sh2_ref,
             m_ref, pa_ref, pb_ref, o_ref, x1_ref, h2_ref):
        pa = _dot(ys_ref[...], wa_ref[...].reshape(d, d))
        pb = _dot(ya_ref[...], wb_ref[...].reshape(d, d))
        pa_ref[...] = pa.astype(BF)
        pb_ref[...] = pb.astype(BF)
        merged = (_sigmoid(ga_ref[...].astype(F32)) * pa + _sigmoid(gb_ref[...].astype(F32)) * pb).astype(BF)
        m_ref[...] = merged
        o = _dot(merged, wo_ref[...].reshape(d, d))
        o_ref[...] = o.astype(BF)
        x1 = x_ref[...] + g1_ref[...] * o
        x1_ref[...] = x1
        h2_ref[...] = ((x1 * _rms(x1)) * w2_ref[...] * (1.0 + sc2_ref[...]) + sh2_ref[...]).astype(BF)

    f, b = _sds((s, d), F32), _sds((s, d), BF)
    r = _rows(tm, d)
    wspec = _resident(proj_a.shape)
    return _call(body, name, (b, b, b, b, f, b), (s // tm,),
                 [r, r, r, r, r, _vec(d), wspec, wspec, wspec, _vec(d), _vec(d), _vec(d)], (r,) * 6, ("parallel",))(
                     y_sgu, y_attn, ga_pre, gb_pre, x, g1, proj_a, proj_b, w_out, w2, sc2, sh2)


def _mix_bwd(name, do, w_out, proj_a, proj_b, ga_pre, gb_pre, pa, pb, tm=512):
    s, d = do.shape
    tm = min(tm, s)

    def body(do_ref, wo_ref, wa_ref, wb_ref, ga_ref, gb_ref, pa_ref, pb_ref,
             dpa_ref, dpb_ref, dga_ref, dgb_ref, dys_ref, dya_ref):
        dm = _dot_nt(do_ref[...], wo_ref[...].reshape(d, d))
        ga = _sigmoid(ga_ref[...].astype(F32))
        gb = _sigmoid(gb_ref[...].astype(F32))
        dpa = (dm * ga).astype(BF)
        dpb = (dm * gb).astype(BF)
        dpa_ref[...] = dpa
        dpb_ref[...] = dpb
        dga_ref[...] = (dm * pa_ref[...].astype(F32) * ga * (1.0 - ga)).astype(BF)
        dgb_ref[...] = (dm * pb_ref[...].astype(F32) * gb * (1.0 - gb)).astype(BF)
        dys_ref[...] = _dot_nt(dpa, wa_ref[...].reshape(d, d)).astype(BF)
        dya_ref[...] = _dot_nt(dpb, wb_ref[...].reshape(d, d)).astype(BF)

    f, b = _sds((s, d), F32), _sds((s, d), BF)
    r = _rows(tm, d)
    wspec = _resident(w_out.shape)
    return _call(body, name, (b, b, b, b, b, b), (s // tm,), [r, wspec, wspec, wspec, r, r, r, r], (r,) * 6,
                 ("parallel",))(do, w_out, proj_a, proj_b, ga_pre, gb_pre, pa, pb)


def _ffn_up_act(name, h2, w_gate, w_up, cw, cb, tm=1024):
    s, d = h2.shape
    tm = min(tm, s)
    tc = FFN_CHUNK
    per = tm // HALO

    def body(h_ref, hprev_ref, wg_ref, wu_ref, cw_ref, cb_ref, a_ref, ac_ref, up_ref, hf_ref):
        hv = h_ref[...]
        wg = wg_ref[...].reshape(tc, d)
        a = _dot_nt(hv, wg).astype(BF)
        up = _dot_nt(hv, wu_ref[...].reshape(tc, d)).astype(BF)
        a_ref[...] = a
        up_ref[...] = up
        prev = jnp.where(pl.program_id(1) > 0, _dot_nt(hprev_ref[...], wg).astype(BF).astype(F32), 0.0)
        ext = jnp.concatenate([prev, a.astype(F32)], axis=0)
        ac = (cb_ref[...] + cw_ref[0:1, :] * pltpu.roll(ext, 2, axis=0) + cw_ref[1:2, :] * pltpu.roll(ext, 1, axis=0)
              + cw_ref[2:3, :] * ext)[HALO:]
        ac_ref[...] = ac.astype(BF)
        hf_ref[...] = (ac * _sigmoid(ac) * up.astype(F32)).astype(BF)

    wspec = pl.BlockSpec((CHIPS_PER_CHUNK, FFN_SHARD, d), lambda j, i: (j, 0, 0))
    ospec = pl.BlockSpec((tm, tc), lambda j, i: (i, j))
    o = _sds((s, FFN_DIM), BF)
    return _call(body, name, (o, o, o, o), (FFN_CHUNKS, s // tm),
                 [pl.BlockSpec((tm, d), lambda j, i: (i, 0)), pl.BlockSpec((HALO, d), lambda j, i: (jnp.maximum(i * per - 1, 0), 0)),
                  wspec, wspec, pl.BlockSpec((3, tc), lambda j, i: (0, j)), pl.BlockSpec((1, tc), lambda j, i: (0, j))],
                 (ospec, ospec, ospec, ospec), ("parallel", "parallel"))(h2, h2, w_gate, w_up, cw, cb)


def _ffn_down(name, hf, w_down, x1, g2, tm=512):
    s, d = x1.shape
    tm = min(tm, s)

    def body(hf_ref, wd_ref, x1_ref, g2_ref, dn_ref, x2_ref):
        dn = _dot(hf_ref[...], wd_ref[...].reshape(FFN_DIM, d))
        dn_ref[...] = dn.astype(BF)
        x2_ref[...] = x1_ref[...] + g2_ref[...] * dn

    return _call(body, name, (_sds((s, d), BF), _sds((s, d), F32)), (s // tm,),
                 [_rows(tm, FFN_DIM), _resident(w_down.shape), _rows(tm, d), _vec(d)],
                 (_rows(tm, d), _rows(tm, d)), ("parallel",))(hf, w_down, x1, g2)


def _ffn_down_bwd_act(name, dx2, dn, g2, w_down, a, ac, up, cw, tm=256, after=None):
    s, d = dx2.shape
    c = a.shape[1]
    tm = min(tm, s)
    tc = FFN_CHUNK
    per = tm // HALO
    steps = s // tm
    last = s // HALO - 1
    n = tm + HALO

    def body(dx_ref, dxnext_ref, dn_ref, g2_ref, wd_ref, a_ref, ac_ref, acnext_ref, up_ref, upnext_ref, cw_ref,
             ddn_ref, da_ref, dup_ref, dg_ref, dcw_ref, dcb_ref):
        i = pl.program_id(0)

        @pl.when(i == 0)
        def _():
            dg_ref[...] = jnp.zeros_like(dg_ref)
            dcw_ref[...] = jnp.zeros_like(dcw_ref)
            dcb_ref[...] = jnp.zeros_like(dcb_ref)

        dxv = dx_ref[...]
        ddn = (dxv * g2_ref[...]).astype(BF)
        ddn_ref[...] = ddn
        dg_ref[...] += jnp.sum(dxv * dn_ref[...].astype(F32), axis=0, keepdims=True)
        ddn_next = jnp.where(i < steps - 1, dxnext_ref[...] * g2_ref[...], 0.0).astype(BF)
        ddn_ext = jnp.concatenate([ddn, ddn_next], axis=0)
        for k in range(FFN_CHUNKS):
            cols = slice(k * tc, (k + 1) * tc)
            dh = _dot_nt(ddn_ext, wd_ref[k * CHIPS_PER_CHUNK:(k + 1) * CHIPS_PER_CHUNK].reshape(tc, d))
            ace = jnp.concatenate([ac_ref[:, cols].astype(F32), acnext_ref[:, cols].astype(F32)], axis=0)
            upe = jnp.concatenate([up_ref[:, cols].astype(F32), upnext_ref[:, cols].astype(F32)], axis=0)
            sig = _sigmoid(ace)
            silu = ace * sig
            dac = dh * upe * (sig + silu * (1.0 - sig))
            dup_ref[:, cols] = (dh[:tm] * silu[:tm]).astype(BF)
            d1 = pltpu.roll(dac, n - 1, axis=0)[:tm]
            d2 = pltpu.roll(dac, n - 2, axis=0)[:tm]
            d0 = dac[:tm]
            da_ref[:, cols] = (cw_ref[2:3, cols] * d0 + cw_ref[1:2, cols] * d1 + cw_ref[0:1, cols] * d2).astype(BF)
            a0 = a_ref[:, cols].astype(F32)
            dcb_ref[:, cols] += jnp.sum(d0, axis=0, keepdims=True)
            dcw_ref[0:1, cols] += jnp.sum(d2 * a0, axis=0, keepdims=True)
            dcw_ref[1:2, cols] += jnp.sum(d1 * a0, axis=0, keepdims=True)
            dcw_ref[2:3, cols] += jnp.sum(d0 * a0, axis=0, keepdims=True)

    nxt = lambda width: pl.BlockSpec((HALO, width), lambda i: (jnp.minimum((i + 1) * per, last), 0))
    wide = _sds((s, c), BF)
    return _call(body, name, (_sds((s, d), BF), wide, wide, _sds((1, d), F32), _sds((3, c), F32), _sds((1, c), F32)), (steps,),
                 [_rows(tm, d), nxt(d), _rows(tm, d), _vec(d), _resident(w_down.shape), _rows(tm, c), _rows(tm, c), nxt(c),
                  _rows(tm, c), nxt(c), pl.BlockSpec((3, c), lambda i: (0, 0))],
                 (_rows(tm, d), _rows(tm, c), _rows(tm, c), _vec(d), pl.BlockSpec((3, c), lambda i: (0, 0)), _vec(c)),
                 ("arbitrary",), after=after)(dx2, dx2, dn, g2, w_down, a, ac, ac, up, up, cw)


def _ffn_up_bwd(name, da, dup, w_gate, w_up, x1, dx2, w2, sc2, o, g1, tm=512, after=None):
    s, d = x1.shape
    tm = min(tm, s)

    def body(da_ref, dup_ref, wg_ref, wu_ref, x1_ref, dx2_ref, w2_ref, sc2_ref, o_ref, g1_ref,
             dx1_ref, do_ref, dnw_ref, dsh_ref, dg1_ref):
        @pl.when(pl.program_id(0) == 0)
        def _():
            dnw_ref[...] = jnp.zeros_like(dnw_ref)
            dsh_ref[...] = jnp.zeros_like(dsh_ref)
            dg1_ref[...] = jnp.zeros_like(dg1_ref)

        dh = _dot(da_ref[...], wg_ref[...].reshape(FFN_DIM, d)) + _dot(dup_ref[...], wu_ref[...].reshape(FFN_DIM, d))
        xv = x1_ref[...]
        r = _rms(xv)
        xn = xv * r
        dxn = dh * (w2_ref[...] * (1.0 + sc2_ref[...]))
        dx1 = dx2_ref[...] + r * (dxn - xn * jnp.mean(dxn * xn, axis=-1, keepdims=True))
        dx1_ref[...] = dx1
        dnw_ref[...] += jnp.sum(dh * xn, axis=0, keepdims=True)
        dsh_ref[...] += jnp.sum(dh, axis=0, keepdims=True)
        do_ref[...] = (dx1 * g1_ref[...]).astype(BF)
        dg1_ref[...] += jnp.sum(dx1 * o_ref[...].astype(F32), axis=0, keepdims=True)

    v = _sds((1, d), F32)
    r = _rows(tm, d)
    wspec = _resident(w_gate.shape)
    return _call(body, name, (_sds((s, d), F32), _sds((s, d), BF), v, v, v), (s // tm,),
                 [_rows(tm, FFN_DIM), _rows(tm, FFN_DIM), wspec, wspec, r, r, _vec(d), _vec(d), r, _vec(d)],
                 (r, r, _vec(d), _vec(d), _vec(d)), ("arbitrary",), after=after)(da, dup, w_gate, w_up, x1, dx2, w2, sc2, o, g1)


def _loss_head(name, x, w, target, tm=512):
    s, d = x.shape
    tm = min(tm, s)

    def body(x_ref, w_ref, t_ref, dx_ref, loss_ref, dw_ref):
        @pl.when(pl.program_id(0) == 0)
        def _():
            loss_ref[...] = jnp.zeros_like(loss_ref)
            dw_ref[...] = jnp.zeros_like(dw_ref)

        xv = x_ref[...]
        r = _rms(xv)
        xn = xv * r
        err = xn * w_ref[...] - t_ref[...]
        loss_ref[...] += 0.5 * jnp.sum(jnp.mean(err * err, axis=-1, keepdims=True))
        dy = err * (1.0 / d)
        dw_ref[...] += jnp.sum(dy * xn, axis=0, keepdims=True)
        dxn = dy * w_ref[...]
        dx_ref[...] = r * (dxn - xn * jnp.mean(dxn * xn, axis=-1, keepdims=True))

    return _call(body, name, (_sds((s, d), F32), _sds((1, LANES), F32), _sds((1, d), F32)), (s // tm,),
                 [_rows(tm, d), _vec(d), _rows(tm, d)], (_rows(tm, d), _vec(LANES), _vec(d)), ("arbitrary",))(x, w, target)


def _layer_fwd(l, x, mod, p, cosf, sinf, after=None, late=None, later=None):
    sh1, sc1, g1, sh2, sc2, g2 = mod
    tag = f"l{l}_"
    h, qr, kk0, kk1, vv0, vv1, u_pre, v_pre, ga_pre, gb_pre = _in_proj(
        tag + "in_proj", x, p["norm1_w"], sc1, sh1, p["w_in"], cosf, sinf, after=after)
    y_attn = _attention(tag + "attn", qr, kk0, kk1, vv0, vv1, p["sinks"])
    y_sgu = _sgu(tag + "sgu", u_pre, v_pre, p["sgu_ln_w"], p["sgu_ln_b"], p["sgu_w"], p["sgu_bfull"])
    if late is not None:
        p = dict(p, **late(y_sgu))
    merged, pa, pb, o, x1, h2 = _mix_out(tag + "mix_out", y_sgu, y_attn, ga_pre, gb_pre, x, g1, p["proj_a"], p["proj_b"],
                                         p["w_out"], p["norm2_w"], sc2, sh2)
    if later is not None:
        p = dict(p, **later(h2))
    a, ac, up, hf = _ffn_up_act(tag + "ffn_up", h2, p["w_gate"], p["w_up"], p["conv_w"], p["conv_b"])
    dn, x2 = _ffn_down(tag + "ffn_down", hf, p["w_down"], x1, g2)
    saved = dict(x=x, h=h, qr=qr, kk0=kk0, kk1=kk1, vv0=vv0, vv1=vv1, u_pre=u_pre, v_pre=v_pre, ga_pre=ga_pre,
                 gb_pre=gb_pre, y_attn=y_attn, y_sgu=y_sgu, merged=merged, pa=pa, pb=pb, o=o, x1=x1, h2=h2, a=a, ac=ac, up=up,
                 hf=hf, dn=dn)
    return x2, saved, p


def _layer_bwd(l, dx2, mod, p, sv, cosf, sinf, after=None, emit=None, tick=None):
    sh1, sc1, g1, sh2, sc2, g2 = mod
    tag = f"l{l}_b_"
    d = D_MODEL
    g = {}
    ready = (lambda names: emit({k: g.pop(k) for k in names})) if emit else (lambda names: None)
    tick = tick or (lambda y: None)
    ddn, da, dup, dg2, g["conv_w"], g["conv_b"] = _ffn_down_bwd_act(
        tag + "ffn_down", dx2, sv["dn"], g2, p["w_down"], sv["a"], sv["ac"], sv["up"], p["conv_w"], after=after)
    g["w_down"] = _matmul_tn(tag + "dw_down", sv["hf"], ddn, tk=FFN_CHUNK, after=tick(ddn)).reshape(N_CHIPS, FFN_SHARD, d)
    g["w_gate"] = _matmul_tn(tag + "dw_gate", da, sv["h2"], tk=FFN_CHUNK).reshape(N_CHIPS, FFN_SHARD, d)
    g["w_up"] = _matmul_tn(tag + "dw_up", dup, sv["h2"], tk=FFN_CHUNK).reshape(N_CHIPS, FFN_SHARD, d)
    dx1, do, da2, dsh2, dg1 = _ffn_up_bwd(tag + "ffn_up", da, dup, p["w_gate"], p["w_up"], sv["x1"], dx2, p["norm2_w"],
                                          sc2, sv["o"], g1, after=ready(("w_down", "w_gate", "w_up")))
    g["norm2_w"] = da2 * (1.0 + sc2)
    dsc2 = da2 * p["norm2_w"]
    g["w_out"] = _matmul_tn(tag + "dw_out", sv["merged"], do, after=tick(do)).reshape(N_CHIPS, d // N_CHIPS, d)
    dpa, dpb, dga, dgb, dy_sgu, dy_attn = _mix_bwd(tag + "mix", do, p["w_out"], p["proj_a"], p["proj_b"], sv["ga_pre"],
                                                  sv["gb_pre"], sv["pa"], sv["pb"])
    g["proj_a"] = _matmul_tn(tag + "dproj_a", sv["y_sgu"], dpa).reshape(N_CHIPS, d // N_CHIPS, d)
    g["proj_b"] = _matmul_tn(tag + "dproj_b", sv["y_attn"], dpb).reshape(N_CHIPS, d // N_CHIPS, d)
    du, dv, g["sgu_w"], g["sgu_b"], g["sgu_ln_w"], g["sgu_ln_b"] = _sgu_bwd(
        tag + "sgu", sv["u_pre"], sv["v_pre"], p["sgu_ln_w"], p["sgu_ln_b"], p["sgu_w"], p["sgu_bfull"], dy_sgu,
        after=ready(("w_out", "proj_a", "proj_b")))
    dqr, dkv_cur, dkv_prev, dsink = _attention_bwd(tag + "attn", sv["qr"], sv["kk0"], sv["kk1"], sv["vv0"], sv["vv1"],
                                                   p["sinks"], dy_attn, after=tick(du))
    g["sinks"] = dsink[0, :N_Q_HEADS]
    dq, dkv = _rope_bwd(tag + "rope", dqr, dkv_cur, dkv_prev, cosf, sinf)
    dw_in = _matmul_tn_segments(tag + "dw_in", [dq, dkv, du, dv, dga, dgb], sv["h"])
    g["w_in"] = dw_in.reshape(N_CHIPS, IN_COLS // N_CHIPS, d)
    dx, da1, dsh1 = _in_proj_bwd(tag + "in_proj", dq, dkv, du, dv, dga, dgb, p["w_in"], sv["x"], p["norm1_w"], sc1, dx1)
    g["norm1_w"] = da1 * (1.0 + sc1)
    dsc1 = da1 * p["norm1_w"]
    return dx, (dsh1, dsc1, dg1, dsh2, dsc2, dg2), g


def _pad_to(a, axis, size):
    pad = [(0, 0)] * a.ndim
    pad[axis] = (0, size - a.shape[axis])
    return jnp.pad(a, pad)


def _early_params(w_in, small):
    d = D_MODEL
    return dict(
        w_in=w_in.reshape(IN_COLS, d), norm1_w=small["norm1_w"].reshape(1, d), sinks=small["sinks"],
        sgu_ln_w=small["sgu_ln_w"].reshape(1, d), sgu_ln_b=small["sgu_ln_b"].reshape(1, d), sgu_w=small["sgu_w"],
        sgu_bfull=jnp.broadcast_to(small["sgu_b"][:, :, None], (SGU_GROUPS, SGU_CHUNK, SGU_CHUNK)))


def _mix_params(proj_a, proj_b, w_out, small):
    return dict(proj_a=proj_a, proj_b=proj_b, w_out=w_out, norm2_w=small["norm2_w"].reshape(1, D_MODEL))


def _ffn_params(w_gate, w_up, w_down, conv_w, small):
    return dict(
        w_gate=w_gate, w_up=w_up, w_down=w_down, conv_w=conv_w.transpose(1, 0, 2).reshape(3, FFN_DIM),
        conv_b=small["conv_b"].reshape(1, FFN_DIM))


def _layer_params(w_in, proj_a, proj_b, w_out, w_gate, w_up, w_down, conv_w, small):
    return dict(_early_params(w_in, small), **_mix_params(proj_a, proj_b, w_out, small),
                **_ffn_params(w_gate, w_up, w_down, conv_w, small))


def _conv_grads_natural(g):
    cw = g["conv_w"]
    cb = g["conv_b"].reshape(FFN_DIM)
    return cw, cb


def _rope_tables(positions):
    inv_freq = ROPE_THETA ** (-jnp.arange(0, ROT_DIM, 2, dtype=F32) / ROT_DIM)
    ang = positions.astype(F32)[:, None] * inv_freq
    cos, sin = jnp.cos(ang), jnp.sin(ang)
    s = positions.shape[0]
    rest = HEAD_DIM - ROT_DIM
    cos_head = jnp.concatenate([cos, cos, jnp.ones((s, rest), F32)], axis=1)
    sin_head = jnp.concatenate([-sin, sin, jnp.zeros((s, rest), F32)], axis=1)
    return jnp.tile(cos_head, (1, LANES // HEAD_DIM)), jnp.tile(sin_head, (1, LANES // HEAD_DIM))


ADA_ROWS = 16


def _ada_fwd(name, c_rows, ada_w, ada_b_cols, tn=512):
    depth, d, n = ada_w.shape

    def body(c_ref, w_ref, b_ref, o_ref):
        cv = c_ref[...]
        act = (cv * _sigmoid(cv)).astype(BF)
        o_ref[0] = _dot(act, w_ref[0].astype(BF)) + b_ref[0]

    return _call(body, name, _sds((depth, ADA_ROWS, n), F32), (depth, n // tn),
                 [pl.BlockSpec((ADA_ROWS, d), lambda l, j: (0, 0)), pl.BlockSpec((1, d, tn), lambda l, j: (l, 0, j)),
                  pl.BlockSpec((1, 1, tn), lambda l, j: (l, 0, j))],
                 pl.BlockSpec((1, ADA_ROWS, tn), lambda l, j: (l, 0, j)), ("parallel", "parallel"))(c_rows, ada_w, ada_b_cols)


def _ada_bwd(name, c_rows, dmod_cols, tn=512):
    depth, _, n = dmod_cols.shape
    d = c_rows.shape[1]

    def body(c_ref, dm_ref, o_ref):
        cv = c_ref[...]
        act = (cv * _sigmoid(cv)).astype(BF)
        o_ref[0] = _dot_tn(act, dm_ref[0].astype(BF))

    return _call(body, name, _sds((depth, d, n), F32), (depth, n // tn),
                 [pl.BlockSpec((ADA_ROWS, d), lambda l, j: (0, 0)), pl.BlockSpec((1, ADA_ROWS, tn), lambda l, j: (l, 0, j))],
                 pl.BlockSpec((1, d, tn), lambda l, j: (l, 0, j)), ("parallel", "parallel"))(c_rows, dmod_cols)


def _colsum(name, a):
    r, n = a.shape

    def body(a_ref, o_ref):
        o_ref[...] = jnp.sum(a_ref[...], axis=0, keepdims=True)

    return _call(body, name, _sds((1, n), F32), (1,), [pl.BlockSpec((r, n), lambda i: (0, 0))],
                 pl.BlockSpec((1, n), lambda i: (0, 0)), ("arbitrary",))(a)


REL_SIBLING = (0, 0, 1)
REL_CHIPS = ((1, 0, 0), (0, 1, 0), (1, 1, 0))
REL_ALL = tuple((fx, fy, fc) for fx in (0, 1) for fy in (0, 1) for fc in (0, 1) if fx or fy or fc)


def _chip_of(dev):
    return 2 * dev[0] + dev[1]


def _dev_of(dev):
    return 4 * dev[0] + 2 * dev[1] + dev[2]


def _flip(dev, rel):
    return tuple(1 - m if f else m for m, f in zip(dev, rel))


def _exchange(name, arrays, n_out, stages, aliases=None):
    out_shapes, stages = stages[0], stages[1:]
    n_in = len(arrays)
    aliases = aliases or {}
    n_remote = sum(len(plan) for plan, _ in stages)
    n_local = sum(len(local) for _, local in stages)

    def at(ref, idx):
        return ref.at[idx] if len(idx) else ref

    def body(*refs):
        bufs = list(refs[:n_in + n_out])
        for i_in, i_out in aliases.items():
            bufs[i_in] = bufs[n_in + i_out]
        send_sems, recv_sems, local_sems = refs[n_in + n_out:]
        me = (lax.axis_index("x"), lax.axis_index("y"), lax.axis_index("c"))
        base_r = base_l = 0
        pending = []
        for plan, local in stages:
            def remote(k, entry, sender, receiver):
                rel, si, ssel, di, dsel = entry
                return pltpu.make_async_remote_copy(
                    src_ref=at(bufs[si], ssel(sender, receiver)), dst_ref=at(bufs[di], dsel(sender, receiver)),
                    send_sem=send_sems.at[k], recv_sem=recv_sems.at[k], device_id=_flip(me, rel), device_id_type=MESH)

            sends = [remote(base_r + k, e, me, _flip(me, e[0])) for k, e in enumerate(plan)]
            for cp in sends:
                cp.start()
            for k, (si, ssel, di, dsel) in enumerate(local):
                cp = pltpu.make_async_copy(at(bufs[si], ssel(me)), at(bufs[di], dsel(me)), local_sems.at[base_l + k])
                cp.start()
                pending.append(cp.wait)
            for k, e in enumerate(plan):
                remote(base_r + k, e, _flip(me, e[0]), me).wait_recv()
            pending += [cp.wait_send for cp in sends]
            base_r += len(plan)
            base_l += len(local)
        for wait in pending:
            wait()

    any_spec = pl.BlockSpec(memory_space=pl.ANY)
    return pl.pallas_call(
        body, name=name, out_shape=tuple(out_shapes), in_specs=[any_spec] * n_in, out_specs=tuple([any_spec] * n_out),
        input_output_aliases=dict(aliases),
        scratch_shapes=[pltpu.SemaphoreType.DMA((max(n_remote, 1),)), pltpu.SemaphoreType.DMA((max(n_remote, 1),)),
                        pltpu.SemaphoreType.DMA((max(n_local, 1),))])(*arrays)


HBM_SPEC = pl.BlockSpec(memory_space=pltpu.HBM)
SEM_SPEC = pl.BlockSpec(memory_space=pltpu.SEMAPHORE)


def _split_copies(bufs, plan, local, send_sems, recv_sems, local_sems):
    me = (lax.axis_index("x"), lax.axis_index("y"), lax.axis_index("c"))

    def at(ref, idx):
        return ref.at[idx] if len(idx) else ref

    def remote(k, sender, receiver):
        rel, si, ssel, di, dsel = plan[k]
        return pltpu.make_async_remote_copy(
            src_ref=at(bufs[si], ssel(sender, receiver)), dst_ref=at(bufs[di], dsel(sender, receiver)),
            send_sem=send_sems.at[k], recv_sem=recv_sems.at[k], device_id=_flip(me, rel), device_id_type=MESH)

    sends = [remote(k, me, _flip(me, plan[k][0])) for k in range(len(plan))]
    arrivals = [remote(k, _flip(me, plan[k][0]), me) for k in range(len(plan))]
    locs = [pltpu.make_async_copy(at(bufs[si], ssel(me)), at(bufs[di], dsel(me)), local_sems.at[k])
            for k, (si, ssel, di, dsel) in enumerate(local)]
    return sends, arrivals, locs


def _exchange_start(name, arrays, out_shapes, plan, local, inplace=False):
    n_in, n_out = len(arrays), len(out_shapes)
    n_buf = n_in + n_out

    def body(*refs):
        sems = refs[n_buf:n_buf + 3]
        bufs = refs[n_buf + 3:2 * n_buf + 3]
        sends, _, locs = _split_copies(bufs * 2 if inplace else bufs, plan, local, *sems)
        for cp in sends + locs:
            cp.start()
        refs[-1][...] = jnp.zeros_like(refs[-1])

    zones = [lax.empty(o.shape, o.dtype) for o in out_shapes]
    operands = [pltpu.with_memory_space_constraint(a, pltpu.HBM) for a in list(arrays) + zones]
    sem = lambda n: pltpu.SemaphoreType.DMA((max(n, 1),))
    out = pl.pallas_call(
        body, name=name,
        out_shape=(sem(len(plan)), sem(len(plan)), sem(len(local)), *[pltpu.HBM(a.shape, a.dtype) for a in operands],
                   _sds((SUBLANES, LANES), F32)),
        in_specs=[HBM_SPEC] * n_buf,
        out_specs=(SEM_SPEC, SEM_SPEC, SEM_SPEC, *[HBM_SPEC] * n_buf, pl.BlockSpec(memory_space=pltpu.VMEM)),
        input_output_aliases={i: 3 + i for i in range(n_buf)},
        compiler_params=pltpu.CompilerParams(has_side_effects=pltpu.SideEffectType.DATAFLOW_SIDE_EFFECTING))(*operands)
    pending = dict(name=name, sems=out[:3], thru=out[3:3 + n_in], zones=out[3 + n_in:3 + n_buf], plan=plan, local=local,
                   inplace=inplace)
    return pending, out[-1]


def _exchange_wait(pending, after, both=False):
    thru, zones, plan, local, inplace = (pending[k] for k in ("thru", "zones", "plan", "local", "inplace"))
    n_in, n_buf = len(thru), len(thru) + len(zones)

    def body(*refs):
        bufs = refs[:n_buf]
        sends, arrivals, locs = _split_copies(bufs * 2 if inplace else bufs, plan, local, *refs[n_buf:n_buf + 3])
        for cp in arrivals:
            cp.wait_recv()
        for cp in sends:
            cp.wait_send()
        for cp in locs:
            cp.wait()

    out = pl.pallas_call(
        body, name=pending["name"] + "_wait", out_shape=tuple(pltpu.HBM(a.shape, a.dtype) for a in list(thru) + list(zones)),
        in_specs=[HBM_SPEC] * n_buf + [SEM_SPEC] * 3 + [pl.BlockSpec(memory_space=pl.ANY)],
        out_specs=tuple([HBM_SPEC] * n_buf), input_output_aliases={i: i for i in range(n_buf)},
        compiler_params=pltpu.CompilerParams(has_side_effects=pltpu.SideEffectType.DATAFLOW_SIDE_EFFECTING))(
            *thru, *zones, *pending["sems"], after)
    if both:
        return out[:n_in], out[n_in:]
    return out[:n_in] if inplace else out[n_in:]


def _whole(*_):
    return ()


def _half_rows(rows, core):
    return pl.ds(core * (rows // 2), rows // 2)


def _gather_weights_plan(shards):
    n = len(shards)
    dsts = [_sds((N_CHIPS,) + a.shape, a.dtype) for a in shards]
    fetch, forward = [], []
    for t, a in enumerate(shards):
        rows = a.shape[0]
        if rows % (2 * 16) == 0:
            fetch += [(rel, t, (lambda s_, r_, rows=rows: (_half_rows(rows, s_[2]),)), n + t,
                       (lambda s_, r_, rows=rows: (_chip_of(s_), _half_rows(rows, s_[2])))) for rel in REL_CHIPS]
            forward += [(REL_SIBLING, n + t, (lambda s_, r_, rows=rows, rel=rel: (_chip_of(_flip(s_, rel)), _half_rows(rows, s_[2]))),
                         n + t, (lambda s_, r_, rows=rows, rel=rel: (_chip_of(_flip(s_, rel)), _half_rows(rows, s_[2]))))
                        for rel in REL_CHIPS]
        else:
            fetch += [(rel, t, _whole, n + t, lambda s_, r_: (_chip_of(s_),)) for rel in REL_CHIPS]
    local = [(t, _whole, n + t, lambda me: (_chip_of(me),)) for t in range(n)]
    return dsts, fetch, local, forward


def _gather_weights_start(name, shards):
    dsts, fetch, local, forward = _gather_weights_plan(shards)
    pending, token = _exchange_start(name, shards, dsts, fetch, local)
    return dict(pending, forward=forward), token


def _gather_weights_finish(pending, after):
    landed = _exchange_wait(pending, after)
    n = len(landed)
    return _exchange(pending["name"] + "_forward", landed, n, [[_sds(a.shape, a.dtype) for a in landed], (pending["forward"], [])],
                     aliases={t: t for t in range(n)})


def _gather_chips_plan(arrays):
    n = len(arrays)
    dsts = [_sds((N_CHIPS,) + a.shape, a.dtype) for a in arrays]
    plan = [(rel, t, _whole, n + t, lambda s_, r_: (_chip_of(s_),)) for t in range(n) for rel in REL_CHIPS]
    local = [(t, _whole, n + t, lambda me: (_chip_of(me),)) for t in range(n)]
    return dsts, plan, local


def _gather_chips(name, arrays):
    dsts, plan, local = _gather_chips_plan(arrays)
    return _exchange(name, arrays, len(arrays), [dsts, (plan, local)])


def _gather_chips_start(name, arrays):
    dsts, plan, local = _gather_chips_plan(arrays)
    return _exchange_start(name, arrays, dsts, plan, local)


def _gather_all(name, a):
    plan = [(rel, 0, _whole, 1, lambda s_, r_: (_dev_of(s_),)) for rel in REL_ALL]
    local = [(0, _whole, 1, lambda me: (_dev_of(me),))]
    return _exchange(name, [a], 1, [[_sds((2 * N_CHIPS,) + a.shape, a.dtype)], (plan, local)])[0]


def _swap_halves_start(name, grads):
    n = len(grads)
    dsts = [_sds((g.shape[0], g.shape[1] // 2, g.shape[2]), g.dtype) for g in grads]
    plan = [(REL_SIBLING, t, (lambda s_, r_, rows=g.shape[1]: (pl.ds(0, N_CHIPS), _half_rows(rows, r_[2]))), n + t, _whole)
            for t, g in enumerate(grads)]
    return _exchange_start(name, grads, dsts, plan, [])


def _scatter_chips_plan(sums):
    n = len(sums)
    dsts = [_sds(a.shape, a.dtype) for a in sums]
    plan = [(rel, t, lambda s_, r_: (_chip_of(r_),), n + t, lambda s_, r_: (_chip_of(s_),))
            for t in range(n) for rel in REL_CHIPS]
    local = [(t, lambda me: (_chip_of(me),), n + t, lambda me: (_chip_of(me),)) for t in range(n)]
    return dsts, plan, local


def _scatter_chips_start(name, sums):
    dsts, plan, local = _scatter_chips_plan(sums)
    return _exchange_start(name, sums, dsts, plan, local)


def _swap_back_start(name, totals, layer):
    n = len(totals)
    plan = [(REL_SIBLING, n + t, (lambda s_, r_, rows=a.shape[1]: (layer, _half_rows(rows, s_[2]))),
             n + t, (lambda s_, r_, rows=a.shape[1]: (layer, _half_rows(rows, s_[2])))) for t, a in enumerate(totals)]
    return _exchange_start(name, totals, [], plan, [], inplace=True)


def _add_halves(name, gs, recvs, core):
    n = len(gs)
    nch = recvs[0].shape[0]

    def body(core_ref, *refs):
        for g_ref, r_ref, o_ref in zip(refs[:n], refs[n:2 * n], refs[2 * n:]):
            o_ref[0] = (g_ref[0, 0].astype(F32) + r_ref[0].astype(F32)).astype(o_ref.dtype)

    halves = [pl.BlockSpec((1,) + r.shape[1:], lambda k, core_ref: (k, 0, 0)) for r in recvs]
    spec = pltpu.PrefetchScalarGridSpec(
        num_scalar_prefetch=1, grid=(nch,),
        in_specs=[pl.BlockSpec((1, 1) + r.shape[1:], lambda k, core_ref: (k, core_ref[0], 0, 0)) for r in recvs] + halves,
        out_specs=tuple(halves))
    return pl.pallas_call(body, name=name, out_shape=tuple(_sds(r.shape, r.dtype) for r in recvs), grid_spec=spec,
                          compiler_params=pltpu.CompilerParams(dimension_semantics=("parallel",),
                                                               vmem_limit_bytes=VMEM_LIMIT))(
                                                                   core, *[g.reshape(nch, 2, *r.shape[1:]) for g, r in zip(gs, recvs)],
                                                                   *recvs)


def _sum_chips(name, arrays, core, layer, totals):
    n = len(arrays)
    nch = arrays[0].shape[0]
    parts = 2
    shapes = [(a.shape[1] // parts, a.shape[2]) for a in arrays]

    def body(core_ref, *refs):
        for a_ref, o_ref in zip(refs[:n], refs[-n:]):
            acc = a_ref[0].astype(F32)
            for k in range(1, nch):
                acc = acc + a_ref[k].astype(F32)
            o_ref[0, 0, 0] = acc

    in_specs = [pl.BlockSpec((nch,) + shp, lambda i, core_ref: (0, i, 0)) for shp in shapes]
    args = [core, *arrays]
    if totals is not None:
        in_specs += [pl.BlockSpec(memory_space=pl.ANY)] * n
        args += [t.reshape(DEPTH, 2, parts, *shp) for t, shp in zip(totals, shapes)]
    spec = pltpu.PrefetchScalarGridSpec(
        num_scalar_prefetch=1, grid=(parts,), in_specs=in_specs,
        out_specs=tuple(pl.BlockSpec((1, 1, 1) + shp, lambda i, core_ref: (layer, core_ref[0], i, 0, 0)) for shp in shapes))
    outs = pl.pallas_call(body, name=name, out_shape=tuple(_sds((DEPTH, 2, parts) + shp, F32) for shp in shapes), grid_spec=spec,
                          input_output_aliases={1 + n + t: t for t in range(n)} if totals is not None else {},
                          compiler_params=pltpu.CompilerParams(dimension_semantics=("arbitrary",),
                                                               vmem_limit_bytes=VMEM_LIMIT))(*args)
    return [o.reshape(DEPTH, 2 * parts * shp[0], shp[1]) for o, shp in zip(outs, shapes)]


def _adamw_update(w, g, m, v):
    mn = ADAM_B1 * m + (1.0 - ADAM_B1) * g
    vn = ADAM_B2 * v + (1.0 - ADAM_B2) * (g * g)
    m_hat = mn / (1.0 - ADAM_B1 ** ADAM_STEP)
    v_hat = vn / (1.0 - ADAM_B2 ** ADAM_STEP)
    return -ADAM_LR * (m_hat / (jnp.sqrt(v_hat) + ADAM_EPS) + ADAM_WD * w), mn, vn


def _adamw(name, w, g, m, v, after=None):
    depth, r, c = w.shape
    tr = next(t for t in (512, 448, 384, 352, 336, 256, 192, 128, 64, 32, 16, 8) if r % t == 0 and t * c <= ADAM_TILE_ELEMS)

    def body(w_ref, g_ref, m_ref, v_ref, go_ref, d_ref, mo_ref, vo_ref):
        gv = g_ref[...]
        go_ref[...] = gv
        d_ref[...], mo_ref[...], vo_ref[...] = _adamw_update(w_ref[...], gv, m_ref[...], v_ref[...])

    spec = pl.BlockSpec((1, tr, c), lambda l, i: (l, i, 0))
    o = _sds(w.shape, F32)
    return _call(body, name, (o, o, o, o), (depth, r // tr), [spec] * 4, (spec,) * 4, ("parallel", "parallel"),
                 after=after)(w, g, m, v)


def _adamw_small(name, ws, gs, ms, vs):
    n = len(ws)

    def body(*refs):
        for t in range(n):
            w_ref, g_ref, m_ref, v_ref = (refs[k * n + t] for k in range(4))
            d_ref, mo_ref, vo_ref = (refs[(4 + k) * n + t] for k in range(3))
            d_ref[...], mo_ref[...], vo_ref[...] = _adamw_update(w_ref[...], g_ref[...], m_ref[...], v_ref[...])

    outs = [_sds(w.shape, F32) for w in ws]
    res = pl.pallas_call(body, name=name, out_shape=tuple(outs * 3))(*ws, *gs, *ms, *vs)
    return res[:n], res[n:2 * n], res[2 * n:]


def _pack(arrays, rows):
    flat = jnp.concatenate([a.reshape(-1).astype(F32) for a in arrays])
    return _pad_to(flat, 0, rows * LANES).reshape(rows, LANES)


def _unpack(packed, shapes):
    flat = packed.reshape(-1)
    out, off = [], 0
    for shp in shapes:
        n = 1
        for s_ in shp:
            n *= s_
        out.append(flat[off:off + n].reshape(shp))
        off += n
    return out


_MATRICES = ("w_in", "proj_a", "proj_b", "w_out", "w_gate", "w_up", "w_down")
_SMALL = (("norm1_w", (D_MODEL,)), ("sinks", (N_Q_HEADS,)), ("sgu_ln_w", (SGU_WIDTH,)), ("sgu_ln_b", (SGU_WIDTH,)),
          ("sgu_w", (SGU_GROUPS, SGU_CHUNK, SGU_CHUNK)), ("sgu_b", (SGU_GROUPS, SGU_CHUNK)), ("norm2_w", (D_MODEL,)),
          ("conv_w", (3, FFN_DIM)), ("conv_b", (FFN_DIM,)), ("final_norm_w", (D_MODEL,)))
SMALL_ROWS = 320
ADAM_TILE_ELEMS = 384 * 1024


def _reduce_cores_start(tag, partial):
    names = list(partial)
    pending, token = _swap_halves_start(tag + "_cores", [partial[k] for k in names])
    return dict(pending, tag=tag, names=names), token


def _reduce_chips_start(pending, core, after):
    tag, names = pending["tag"], pending["names"]
    mine, theirs = _exchange_wait(pending, after, both=True)
    sums = _add_halves(tag + "_cores_add", mine, theirs, core)
    scatter, token = _scatter_chips_start(tag + "_chips", sums)
    return dict(scatter, tag=tag, names=names), token


def _reduce_back_start(pending, l, core, totals, after):
    tag, names = pending["tag"], pending["names"]
    before = [totals[k] for k in names] if names[0] in totals else None
    sums = _sum_chips(tag + "_chips_add", _exchange_wait(pending, after), core, l, before)
    back, _ = _swap_back_start(tag + "_back", sums, l)
    return dict(back, names=names)


def kernel(x, c, positions, ada_w, ada_b, norm1_w, w_in, attn_sinks, sgu_ln_w, sgu_ln_b, sgu_w, sgu_b, proj_a, proj_b, w_out, norm2_w, ffn_w_gate, ffn_w_up, ffn_conv_w, ffn_conv_b, ffn_w_down, final_norm_w, loss_target, m_ada_w, m_ada_b, m_norm1_w, m_w_in, m_attn_sinks, m_sgu_ln_w, m_sgu_ln_b, m_sgu_w, m_sgu_b, m_proj_a, m_proj_b, m_w_out, m_norm2_w, m_ffn_w_gate, m_ffn_w_up, m_ffn_conv_w, m_ffn_conv_b, m_ffn_w_down, m_final_norm_w, v_ada_w, v_ada_b, v_norm1_w, v_w_in, v_attn_sinks, v_sgu_ln_w, v_sgu_ln_b, v_sgu_w, v_sgu_b, v_proj_a, v_proj_b, v_w_out, v_norm2_w, v_ffn_w_gate, v_ffn_w_up, v_ffn_conv_w, v_ffn_conv_b, v_ffn_w_down, v_final_norm_w):
    d = D_MODEL
    ax, ay, ac = lax.axis_index("x"), lax.axis_index("y"), lax.axis_index("c")
    chip = 2 * ax + ay
    dev = 4 * ax + 2 * ay + ac
    core = ac.astype(jnp.int32).reshape(1)

    tr = lambda a: jnp.swapaxes(a, 1, 2)
    shards = [tr(w_in).astype(BF), proj_a.astype(BF), proj_b.astype(BF), w_out.astype(BF),
              tr(ffn_w_gate).astype(BF), tr(ffn_w_up).astype(BF), ffn_w_down.astype(BF), ffn_conv_w]
    first_fetch, token = _gather_weights_start("l0_gather_in", [shards[0][0]])
    c = c + (token[0, 0] * 0.0).astype(c.dtype)

    c_all = _gather_all("gather_cond", c.reshape(SUBLANES, d // SUBLANES)).reshape(2 * N_CHIPS, d)
    c_rows = _pad_to(c_all, 0, ADA_ROWS)
    ada_cols = ada_w.shape[2]
    ada_b_cols = lax.dynamic_slice_in_dim(ada_b, chip * ada_cols, ada_cols, axis=1).reshape(DEPTH, 1, ada_cols)
    mod_cols = _ada_fwd("ada_fwd", c_rows, ada_w, ada_b_cols)
    mod_all = _gather_chips("gather_mod", [mod_cols])[0]
    mod_mine = lax.dynamic_index_in_dim(mod_all, dev, axis=2, keepdims=False)
    mod_mine = mod_mine.transpose(1, 0, 2).reshape(DEPTH, 1, 6 * d)
    mods = [tuple(jnp.split(mod_mine[l], 6, axis=-1)) for l in range(DEPTH)]

    token = mod_all[0, 0, :SUBLANES, :LANES]
    fetches = []
    for l in range(DEPTH):
        groups = []
        for tag, members in (("in", shards[:1]), ("mix", shards[1:4]), ("ffn", shards[4:])):
            if (l, tag) == (0, "in"):
                groups.append(first_fetch)
                continue
            behind = (token[0, 0] * 0.0).astype(members[0].dtype)
            pending, token = _gather_weights_start(f"l{l}_gather_{tag}", [members[0][l] + behind] + [a[l] for a in members[1:]])
            groups.append(pending)
        fetches.append(groups)

    small_in = dict(norm1_w=norm1_w, sinks=attn_sinks, sgu_ln_w=sgu_ln_w, sgu_ln_b=sgu_ln_b, sgu_w=sgu_w, sgu_b=sgu_b,
                    norm2_w=norm2_w, conv_b=ffn_conv_b)
    cosf, sinf = _rope_tables(positions[0])
    small_of = lambda l: {k: v[l] for k, v in small_in.items()}

    h = x[0]
    saved, params = [], []
    for l in range(DEPTH):
        first, mix, ffn = fetches[l]
        w_in_l = _gather_weights_finish(first, token if l == 0 else h)
        late = lambda y, l=l, mix=mix: _mix_params(*_gather_weights_finish(mix, y), small_of(l))
        later = lambda y, l=l, ffn=ffn: _ffn_params(*_gather_weights_finish(ffn, y), small_of(l))
        h, sv, p = _layer_fwd(l, h, mods[l], _early_params(w_in_l[0], small_of(l)), cosf, sinf, late=late, later=later)
        saved.append(sv)
        params.append(p)
    dx, loss_part, d_final = _loss_head("loss_head", h, final_norm_w.reshape(1, d), loss_target[0])
    loss = lax.psum(loss_part[0, 0], ("x", "y", "c"))

    def small_pack(l, grads):
        cw, cb = _conv_grads_natural(grads)
        nat = dict(grads, conv_w=cw, conv_b=cb, final_norm_w=d_final if l == DEPTH - 1 else jnp.zeros((d,), F32))
        return _pack([nat[k] for k, _ in _SMALL], N_CHIPS * SMALL_ROWS).reshape(N_CHIPS, SMALL_ROWS, LANES)

    waiting, inflight = [], []

    def send(tag, partial):
        pending, token = _reduce_cores_start(tag, partial)
        waiting.append(pending)
        return token

    def tick(y):
        token = None
        while waiting:
            pending, token = _reduce_chips_start(waiting.pop(0), core, y)
            inflight.append(pending)
        return token

    dmods = [None] * DEPTH
    dx, dmods[1], grads = _layer_bwd(1, dx, mods[1], params[1], saved[1], cosf, sinf)
    token = send("l1_reduce", dict({k: grads[k] for k in _MATRICES}, small=small_pack(1, grads)))
    dx, dmods[0], grads = _layer_bwd(0, dx, mods[0], params[0], saved[0], cosf, sinf, after=token,
                                     emit=lambda part: send("l0_reduce_" + "_".join(part), part), tick=tick)
    dmod_mine = jnp.concatenate([jnp.concatenate(dmods[l], axis=1) for l in range(DEPTH)], axis=1)
    dmod_all = _gather_all("gather_dmod", dmod_mine.reshape(SUBLANES, -1)).reshape(2 * N_CHIPS, DEPTH * 6 * d)
    send("l0_reduce_in", dict(w_in=grads["w_in"], small=small_pack(0, grads) + dmod_all[0, 0] * 0.0))
    tick(dmod_all)

    totals, flying = {}, None

    def land(after):
        if flying is not None:
            totals.update(zip(flying["names"], _exchange_wait(flying, after)))

    for pending in inflight[:-1]:
        land(dx)
        flying = _reduce_back_start(pending, 1 if pending["tag"].startswith("l1") else 0, core, totals, dx)
    land(dx)
    flying = None

    g_ada_b = _colsum("ada_b_grad", dmod_all).reshape(DEPTH, 6 * d)
    dmod_cols = jnp.stack([lax.dynamic_slice_in_dim(dmod_all, l * 6 * d + chip * ada_cols, ada_cols, axis=1)
                           for l in range(DEPTH)])
    g_ada_w = _ada_bwd("ada_w_grad", c_rows, _pad_to(dmod_cols, 1, ADA_ROWS))
    big = dict(w_in=(tr(w_in), tr(m_w_in), tr(v_w_in)), proj_a=(proj_a, m_proj_a, v_proj_a), proj_b=(proj_b, m_proj_b, v_proj_b),
               w_out=(w_out, m_w_out, v_w_out), w_gate=(tr(ffn_w_gate), tr(m_ffn_w_gate), tr(v_ffn_w_gate)),
               w_up=(tr(ffn_w_up), tr(m_ffn_w_up), tr(v_ffn_w_up)), w_down=(ffn_w_down, m_ffn_w_down, v_ffn_w_down))
    upd, g_big, done = {}, {}, {}

    def update(k, after=None):
        res = _adamw("adamw_" + k, big[k][0], totals[k], *big[k][1:], after=after)
        done[k] = res[1]
        res = [tr(a) for a in res] if k in ("w_in", "w_gate", "w_up") else res
        g_big[k], upd[k] = res[0], res[1:]

    for k in ("w_down", "w_gate", "w_up", "w_out", "proj_a", "proj_b"):
        update(k)
    g_big["ada_w"], *upd["ada_w"] = _adamw("adamw_ada_w", ada_w, g_ada_w, m_ada_w, v_ada_w)
    flying = _reduce_back_start(inflight[-1], 0, core, totals, upd["ada_w"][0])
    land(upd["ada_w"][1])
    gathering, token = _gather_chips_start("gather_small", [totals["small"]])
    update("w_in", after=token)
    small_all = _exchange_wait(gathering, done["w_in"])[0]
    small_g = small_all.transpose(1, 0, 2, 3).reshape(DEPTH, -1)
    per_layer = [_unpack(small_g[l], [shp for _, shp in _SMALL]) for l in range(DEPTH)]
    sg = {k: jnp.stack([per_layer[l][i] for l in range(DEPTH)]) for i, (k, _) in enumerate(_SMALL)}
    g_final = sg["final_norm_w"][DEPTH - 1]
    g_conv_w = lax.dynamic_slice_in_dim(sg["conv_w"], chip * FFN_SHARD, FFN_SHARD, axis=2)

    rest = [("ada_b", ada_b, g_ada_b, m_ada_b, v_ada_b), ("norm1_w", norm1_w, sg["norm1_w"], m_norm1_w, v_norm1_w),
            ("attn_sinks", attn_sinks, sg["sinks"], m_attn_sinks, v_attn_sinks),
            ("sgu_ln_w", sgu_ln_w, sg["sgu_ln_w"], m_sgu_ln_w, v_sgu_ln_w),
            ("sgu_ln_b", sgu_ln_b, sg["sgu_ln_b"], m_sgu_ln_b, v_sgu_ln_b), ("sgu_w", sgu_w, sg["sgu_w"], m_sgu_w, v_sgu_w),
            ("sgu_b", sgu_b, sg["sgu_b"], m_sgu_b, v_sgu_b), ("norm2_w", norm2_w, sg["norm2_w"], m_norm2_w, v_norm2_w),
            ("ffn_conv_w", ffn_conv_w, g_conv_w, m_ffn_conv_w, v_ffn_conv_w),
            ("ffn_conv_b", ffn_conv_b, sg["conv_b"], m_ffn_conv_b, v_ffn_conv_b),
            ("final_norm_w", final_norm_w.reshape(1, d), g_final.reshape(1, d), m_final_norm_w.reshape(1, d),
             v_final_norm_w.reshape(1, d))]
    rest_out = _adamw_small("adamw_rest", *[[r[i] for r in rest] for i in (1, 2, 3, 4)])
    g_rest = {r[0]: r[2] for r in rest}
    u_rest = {r[0]: tuple(o[i] for o in rest_out) for i, r in enumerate(rest)}
    g_rest["final_norm_w"] = g_final
    u_rest["final_norm_w"] = tuple(a.reshape(d) for a in u_rest["final_norm_w"])

    names = ("ada_w", "ada_b", "norm1_w", "w_in", "attn_sinks", "sgu_ln_w", "sgu_ln_b", "sgu_w", "sgu_b", "proj_a", "proj_b",
             "w_out", "norm2_w", "ffn_w_gate", "ffn_w_up", "ffn_conv_w", "ffn_conv_b", "ffn_w_down", "final_norm_w")
    alias = {"ffn_w_gate": "w_gate", "ffn_w_up": "w_up", "ffn_w_down": "w_down"}
    grad_of = lambda n: g_rest[n] if n in g_rest else g_big[alias.get(n, n)]
    upd_of = lambda n: u_rest[n] if n in u_rest else upd[alias.get(n, n)]
    return (loss, dx[None], *[grad_of(n) for n in names], *[upd_of(n)[0] for n in names],
            *[upd_of(n)[1] for n in names], *[upd_of(n)[2] for n in names])
```

```python
import jax
import jax.numpy as jnp
from jax import lax
from jax.experimental import pallas as pl
from jax.experimental.pallas import tpu as pltpu

F32 = jnp.float32
BF = jnp.bfloat16

D_MODEL = 1024
N_Q_HEADS = 16
N_KV_HEADS = 2
HEAD_DIM = 64
ATTN_BLOCK = 128
ROPE_THETA = 500000.0
ROT_DIM = HEAD_DIM // 4
SGU_WIDTH = 1024
SGU_GROUPS = 8
SGU_CHUNK = 128
FFN_DIM = 2816
NORM_EPS = 1e-6
DEPTH = 2
IN_COLS = 5376
N_CHIPS = 4
FFN_SHARD = FFN_DIM // N_CHIPS
CHIPS_PER_CHUNK = 2
FFN_CHUNK = CHIPS_PER_CHUNK * FFN_SHARD
FFN_CHUNKS = FFN_DIM // FFN_CHUNK
LANES = 128
SUBLANES = 8
HALO = 16
VMEM_LIMIT = 56 * 1024 * 1024
NEG_BIG = -1e30

ADAM_LR = 0.001
ADAM_B1 = 0.9
ADAM_B2 = 0.999
ADAM_EPS = 1e-08
ADAM_WD = 0.01
ADAM_STEP = 10

MESH = pl.DeviceIdType.MESH

Q_END = 1024
KV_END = 1280
U_END = 2304
Z_END = 3328
GA_END = 4352


def _sds(shape, dtype):
    return jax.ShapeDtypeStruct(tuple(shape), dtype)


def _call(body, name, out_shape, grid, in_specs, out_specs, semantics, scratch=(), after=None):
    n_in = len(in_specs)
    fn = body
    if after is not None:
        def fn(*refs):
            return body(*refs[:n_in], *refs[n_in + 1:])

        in_specs = list(in_specs) + [pl.BlockSpec(memory_space=pl.ANY)]
    call = pl.pallas_call(
        fn, name=name, out_shape=out_shape, grid=grid, in_specs=in_specs, out_specs=out_specs,
        scratch_shapes=scratch,
        compiler_params=pltpu.CompilerParams(dimension_semantics=semantics, vmem_limit_bytes=VMEM_LIMIT))
    if after is None:
        return call
    return lambda *args: call(*args, after)


def _rows(tm, width, col=0):
    return pl.BlockSpec((tm, width), lambda i: (i, col))


def _vec(width):
    return pl.BlockSpec((1, width), lambda i: (0, 0))


def _resident(shape):
    zeros = (0,) * len(shape)
    return pl.BlockSpec(tuple(shape), lambda *_: zeros, pipeline_mode=pl.Buffered(1))


def _sigmoid(x):
    return 0.5 + 0.5 * jnp.tanh(0.5 * x)


def _gelu(x):
    return 0.5 * x * (1.0 + lax.erf(x * 0.7071067811865476))


def _gelu_and_grad(x):
    cdf = 0.5 * (1.0 + lax.erf(x * 0.7071067811865476))
    return x * cdf, cdf + x * jnp.exp(-0.5 * x * x) * 0.3989422804014327


def _dot(a, b):
    return jnp.dot(a, b, preferred_element_type=F32)


def _dot_nt(a, b):
    return lax.dot_general(a, b, (((1,), (1,)), ((), ())), preferred_element_type=F32)


def _dot_tn(a, b):
    return lax.dot_general(a, b, (((0,), (0,)), ((), ())), preferred_element_type=F32)


def _rms(xv):
    return lax.rsqrt(jnp.mean(xv * xv, axis=-1, keepdims=True) + NORM_EPS)


def _matmul_tn(name, a, b, tk=512, tn=1024, after=None):
    s, k = a.shape
    n = b.shape[1]
    tk, tn = min(tk, k), min(tn, n)

    def body(a_ref, b_ref, o_ref):
        o_ref[...] = _dot_tn(a_ref[...], b_ref[...]).astype(o_ref.dtype)

    return _call(body, name, _sds((k, n), BF), (k // tk, n // tn),
                 [pl.BlockSpec((s, tk), lambda i, j: (0, i)), pl.BlockSpec((s, tn), lambda i, j: (0, j))],
                 pl.BlockSpec((tk, tn), lambda i, j: (i, j)), ("parallel", "parallel"), after=after)(a, b)


def _matmul_tn_segments(name, segments, b, tk=256):
    s, n = b.shape
    counts = [a.shape[1] // tk for a in segments]
    firsts = [sum(counts[:t]) for t in range(len(segments))]

    def body(*refs):
        a_refs, b_ref, o_ref = refs[:len(segments)], refs[len(segments)], refs[-1]
        i = pl.program_id(0)
        for a_ref, first, count in zip(a_refs, firsts, counts):
            @pl.when((i >= first) & (i < first + count))
            def _(a_ref=a_ref):
                o_ref[...] = _dot_tn(a_ref[...], b_ref[...]).astype(BF)

    specs = [pl.BlockSpec((s, tk), lambda i, first=first, count=count: (0, jnp.clip(i - first, 0, count - 1)))
             for first, count in zip(firsts, counts)]
    return _call(body, name, _sds((sum(counts) * tk, n), BF), (sum(counts),), specs + [_resident(b.shape)],
                 pl.BlockSpec((tk, n), lambda i: (i, 0)), ("arbitrary",))(*segments, b)


def _rope_partner(v):
    lane = lax.broadcasted_iota(jnp.int32, (1, LANES), 1) % HEAD_DIM
    return jnp.where(lane < ROT_DIM // 2, pltpu.roll(v, LANES - ROT_DIM // 2, axis=1), pltpu.roll(v, ROT_DIM // 2, axis=1))


def _dup_half(v, half):
    lane = lax.broadcasted_iota(jnp.int32, (1, LANES), 1)
    keep = jnp.where((lane >= HEAD_DIM) == (half == 1), v, 0.0)
    return keep + pltpu.roll(keep, HEAD_DIM, axis=1)


def _in_proj(name, x, w, sc, sh, w_in, cosf, sinf, tm=512, after=None):
    s, d = x.shape
    tm = min(tm, s)

    def body(x_ref, w_ref, sc_ref, sh_ref, win_ref, cos_ref, sin_ref,
             h_ref, qr_ref, kk0_ref, kk1_ref, vv0_ref, vv1_ref, u_ref, v_ref, ga_ref, gb_ref):
        xv = x_ref[...]
        h = ((xv * _rms(xv)) * w_ref[...] * (1.0 + sc_ref[...]) + sh_ref[...]).astype(BF)
        h_ref[...] = h
        cosv, sinv = cos_ref[...], sin_ref[...]
        q = _dot_nt(h, win_ref[:Q_END, :])
        for j in range(D_MODEL // LANES):
            qv = q[:, j * LANES:(j + 1) * LANES]
            qr_ref[:, j * LANES:(j + 1) * LANES] = ((qv * cosv + _rope_partner(qv) * sinv) * ATTN_SCALE).astype(BF)
        kv = _dot_nt(h, win_ref[Q_END:KV_END, :])
        kr = kv[:, :LANES] * cosv + _rope_partner(kv[:, :LANES]) * sinv
        vv = kv[:, LANES:]
        kk0_ref[...] = _dup_half(kr, 0).astype(BF)
        kk1_ref[...] = _dup_half(kr, 1).astype(BF)
        vv0_ref[...] = _dup_half(vv, 0).astype(BF)
        vv1_ref[...] = _dup_half(vv, 1).astype(BF)
        u_ref[...] = _dot_nt(h, win_ref[KV_END:U_END, :])
        v_ref[...] = _dot_nt(h, win_ref[U_END:Z_END, :])
        ga_ref[...] = _dot_nt(h, win_ref[Z_END:GA_END, :]).astype(BF)
        gb_ref[...] = _dot_nt(h, win_ref[GA_END:, :]).astype(BF)

    wide, kvs, pre = _sds((s, d), BF), _sds((s, LANES), BF), _sds((s, d), F32)
    return _call(body, name, (wide, wide, kvs, kvs, kvs, kvs, pre, pre, wide, wide), (s // tm,),
                 [_rows(tm, d), _vec(d), _vec(d), _vec(d), _resident(w_in.shape), _rows(tm, LANES), _rows(tm, LANES)],
                 (_rows(tm, d), _rows(tm, d)) + (_rows(tm, LANES),) * 4 + (_rows(tm, d),) * 4, ("parallel",), after=after)(
                     x, w, sc, sh, w_in, cosf, sinf)


def _in_proj_bwd(name, dq, dkv, du, dv, dga, dgb, w_in, x, w, sc, dx_in, tm=512):
    s, d = x.shape
    tm = min(tm, s)

    def body(dq_ref, dkv_ref, du_ref, dv_ref, dga_ref, dgb_ref, win_ref, x_ref, w_ref, sc_ref, dxin_ref,
             dx_ref, da_ref, dsh_ref):
        @pl.when(pl.program_id(0) == 0)
        def _():
            da_ref[...] = jnp.zeros_like(da_ref)
            dsh_ref[...] = jnp.zeros_like(dsh_ref)

        dh = (_dot(dq_ref[...], win_ref[:Q_END, :]) + _dot(dkv_ref[...], win_ref[Q_END:KV_END, :])
              + _dot(du_ref[...], win_ref[KV_END:U_END, :]) + _dot(dv_ref[...], win_ref[U_END:Z_END, :])
              + _dot(dga_ref[...], win_ref[Z_END:GA_END, :]) + _dot(dgb_ref[...], win_ref[GA_END:, :]))
        xv = x_ref[...]
        r = _rms(xv)
        xn = xv * r
        dxn = dh * (w_ref[...] * (1.0 + sc_ref[...]))
        dx_ref[...] = dxin_ref[...] + r * (dxn - xn * jnp.mean(dxn * xn, axis=-1, keepdims=True))
        da_ref[...] += jnp.sum(dh * xn, axis=0, keepdims=True)
        dsh_ref[...] += jnp.sum(dh, axis=0, keepdims=True)

    return _call(body, name, (_sds((s, d), F32), _sds((1, d), F32), _sds((1, d), F32)), (s // tm,),
                 [_rows(tm, d), _rows(tm, 2 * LANES), _rows(tm, d), _rows(tm, d), _rows(tm, d), _rows(tm, d),
                  _resident(w_in.shape), _rows(tm, d), _vec(d), _vec(d), _rows(tm, d)],
                 (_rows(tm, d), _vec(d), _vec(d)), ("arbitrary",))(dq, dkv, du, dv, dga, dgb, w_in, x, w, sc, dx_in)


def _rope_bwd(name, dqr, dkv_cur, dkv_prev, cosf, sinf, tm=512):
    s = dqr.shape[0]
    tm = min(tm, s)
    steps = s // tm
    per = tm // ATTN_BLOCK
    nb = s // ATTN_BLOCK

    def unrope(v, cosv, sinv):
        return v * cosv - _rope_partner(v) * sinv

    def body(dq_ref, cur_ref, prev_ref, next_ref, cos_ref, sin_ref, dqo_ref, dkvo_ref):
        i = pl.program_id(0)
        cosv, sinv = cos_ref[...], sin_ref[...]
        for j in range(D_MODEL // LANES):
            dqo_ref[:, j * LANES:(j + 1) * LANES] = unrope(dq_ref[:, j * LANES:(j + 1) * LANES], cosv, sinv).astype(BF)
        nxt = jnp.where(i < steps - 1, next_ref[...], 0.0)
        if per > 1:
            shifted = jnp.concatenate([prev_ref[ATTN_BLOCK:, :], nxt], axis=0)
        else:
            shifted = nxt
        tot = cur_ref[...] + shifted
        dkvo_ref[:, :LANES] = unrope(tot[:, :LANES], cosv, sinv).astype(BF)
        dkvo_ref[:, LANES:] = tot[:, LANES:].astype(BF)

    nxt_spec = pl.BlockSpec((ATTN_BLOCK, 2 * LANES), lambda i: (jnp.minimum((i + 1) * per, nb - 1), 0))
    return _call(body, name, (_sds((s, D_MODEL), BF), _sds((s, 2 * LANES), BF)), (steps,),
                 [_rows(tm, D_MODEL), _rows(tm, 2 * LANES), _rows(tm, 2 * LANES), nxt_spec, _rows(tm, LANES),
                  _rows(tm, LANES)],
                 (_rows(tm, D_MODEL), _rows(tm, 2 * LANES)), ("parallel",))(dqr, dkv_cur, dkv_prev, dkv_prev, cosf, sinf)


Q_PER_KV = N_Q_HEADS // N_KV_HEADS
ATTN_SCALE = HEAD_DIM ** -0.5
HEADS_AHEAD_FWD = 2
HEADS_AHEAD_BWD = 3


def _band_mask_t(n):
    kj = lax.broadcasted_iota(jnp.int32, (2 * ATTN_BLOCK, ATTN_BLOCK), 0)
    qi = lax.broadcasted_iota(jnp.int32, (2 * ATTN_BLOCK, ATTN_BLOCK), 1)
    return (kj > qi) & (kj <= qi + ATTN_BLOCK) & ((n > 0) | (kj >= ATTN_BLOCK))


def _softmax_t(raw, allowed, sink):
    sc = jnp.where(allowed, raw, NEG_BIG)
    m = jnp.maximum(jnp.max(sc, axis=0, keepdims=True), sink)
    p = jnp.exp(sc - m)
    esink = jnp.exp(sink - m)
    inv = 1.0 / (jnp.sum(p, axis=0, keepdims=True) + esink)
    return p * inv, esink * inv


def _kv_specs():
    cur = pl.BlockSpec((ATTN_BLOCK, LANES), lambda n: (n, 0))
    prev = pl.BlockSpec((ATTN_BLOCK, LANES), lambda n: (jnp.maximum(n - 1, 0), 0))
    return [prev, cur] * 4


def _attention(name, qr, kk0, kk1, vv0, vv1, sinks):
    s = qr.shape[0]
    nb = s // ATTN_BLOCK

    def body(sink_ref, q_ref, k0p, k0c, k1p, k1c, v0p, v0c, v1p, v1c, y_ref):
        allowed = _band_mask_t(pl.program_id(0))
        upper = lax.broadcasted_iota(jnp.int32, (1, LANES), 1) >= HEAD_DIM
        upper_rows = lax.broadcasted_iota(jnp.int32, (LANES, 1), 0) >= HEAD_DIM
        bands = ((jnp.concatenate([k0p[...], k0c[...]], axis=0), jnp.concatenate([v0p[...], v0c[...]], axis=0)),
                 (jnp.concatenate([k1p[...], k1c[...]], axis=0), jnp.concatenate([v1p[...], v1c[...]], axis=0)))
        vbts = (bands[0][1].T, bands[1][1].T)

        def scores(h):
            hk, j, half = h // Q_PER_KV, (h % Q_PER_KV) // 2, h % 2
            col = (hk * 4 + j) * LANES
            qp = q_ref[:, col:col + LANES]
            return _dot_nt(bands[hk][0], jnp.where(upper if half else jnp.logical_not(upper), qp, jnp.zeros_like(qp)))

        out_t = None
        ahead = [scores(h) for h in range(HEADS_AHEAD_FWD)]
        for h in range(N_Q_HEADS):
            hk, j, half = h // Q_PER_KV, (h % Q_PER_KV) // 2, h % 2
            raw = ahead.pop(0)
            if h + HEADS_AHEAD_FWD < N_Q_HEADS:
                ahead.append(scores(h + HEADS_AHEAD_FWD))
            pn, _ = _softmax_t(raw, allowed, sink_ref[h])
            o_h = _dot(vbts[hk], pn.astype(BF))
            out_t = jnp.where(upper_rows, o_h, out_t) if half else o_h
            if half:
                col = (hk * 4 + j) * LANES
                y_ref[:, col:col + LANES] = out_t.T.astype(BF)

    return _call(body, name, _sds((s, D_MODEL), BF), (nb,),
                 [pl.BlockSpec(memory_space=pltpu.SMEM), pl.BlockSpec((ATTN_BLOCK, D_MODEL), lambda n: (n, 0))] + _kv_specs(),
                 pl.BlockSpec((ATTN_BLOCK, D_MODEL), lambda n: (n, 0)), ("parallel",))(
                     sinks, qr, kk0, kk0, kk1, kk1, vv0, vv0, vv1, vv1)


def _attention_bwd(name, qr, kk0, kk1, vv0, vv1, sinks, dy, after=None):
    s = qr.shape[0]
    nb = s // ATTN_BLOCK

    def body(sink_ref, q_ref, dy_ref, k0p, k0c, k1p, k1c, v0p, v0c, v1p, v1c, dq_ref, cur_ref, prev_ref, dsink_ref):
        @pl.when(pl.program_id(0) == 0)
        def _():
            dsink_ref[...] = jnp.zeros_like(dsink_ref)

        allowed = _band_mask_t(pl.program_id(0))
        lane = lax.broadcasted_iota(jnp.int32, (1, LANES), 1)
        upper = lane >= HEAD_DIM
        upper_rows = lax.broadcasted_iota(jnp.int32, (LANES, 1), 0) >= HEAD_DIM
        bands = ((jnp.concatenate([k0p[...], k0c[...]], axis=0), jnp.concatenate([v0p[...], v0c[...]], axis=0)),
                 (jnp.concatenate([k1p[...], k1c[...]], axis=0), jnp.concatenate([v1p[...], v1c[...]], axis=0)))
        kbts = (bands[0][0].T, bands[1][0].T)

        def scores(h):
            hk, j, half = h // Q_PER_KV, (h % Q_PER_KV) // 2, h % 2
            kb, vb = bands[hk]
            col = (hk * 4 + j) * LANES
            sel = upper if half else jnp.logical_not(upper)
            qp = q_ref[:, col:col + LANES]
            qa = jnp.where(sel, qp, jnp.zeros_like(qp))
            dya = jnp.where(sel, dy_ref[:, col:col + LANES], 0.0).astype(BF)
            return qa, dya, _dot_nt(kb, qa), _dot_nt(vb, dya)

        dsink = jnp.zeros((1, LANES), F32)
        dk_slab = jnp.zeros((2 * ATTN_BLOCK, LANES), F32)
        dv_slab = jnp.zeros((2 * ATTN_BLOCK, LANES), F32)
        dkk = dvv = dq_t = None
        ahead = [scores(h) for h in range(HEADS_AHEAD_BWD)]
        for h in range(N_Q_HEADS):
            hk, j, half = h // Q_PER_KV, (h % Q_PER_KV) // 2, h % 2
            qa, dya, raw, dp = ahead.pop(0)
            if h + HEADS_AHEAD_BWD < N_Q_HEADS:
                ahead.append(scores(h + HEADS_AHEAD_BWD))
            pn, psink = _softmax_t(raw, allowed, sink_ref[h])
            delta = jnp.sum(pn * dp, axis=0, keepdims=True)
            ds = (pn * (dp - delta)).astype(BF)
            dsink = dsink + jnp.where(lane == h, -jnp.sum(psink * delta), 0.0)
            dq_h = _dot(kbts[hk], ds) * ATTN_SCALE
            dq_t = jnp.where(upper_rows, dq_h, dq_t) if half else dq_h
            dk_h, dv_h = _dot(ds, qa), _dot(pn.astype(BF), dya)
            dkk, dvv = (dk_h, dv_h) if h % Q_PER_KV == 0 else (dkk + dk_h, dvv + dv_h)
            if half:
                col = (hk * 4 + j) * LANES
                dq_ref[:, col:col + LANES] = dq_t.T
            if h % Q_PER_KV == Q_PER_KV - 1:
                mine = upper if hk else jnp.logical_not(upper)
                dk_slab = jnp.where(mine, dkk + pltpu.roll(dkk, HEAD_DIM, axis=1), dk_slab)
                dv_slab = jnp.where(mine, dvv + pltpu.roll(dvv, HEAD_DIM, axis=1), dv_slab)
        prev_ref[:, :LANES] = dk_slab[:ATTN_BLOCK]
        prev_ref[:, LANES:] = dv_slab[:ATTN_BLOCK]
        cur_ref[:, :LANES] = dk_slab[ATTN_BLOCK:]
        cur_ref[:, LANES:] = dv_slab[ATTN_BLOCK:]
        dsink_ref[...] += dsink

    blk = pl.BlockSpec((ATTN_BLOCK, D_MODEL), lambda n: (n, 0))
    kvo = pl.BlockSpec((ATTN_BLOCK, 2 * LANES), lambda n: (n, 0))
    return _call(body, name,
                 (_sds((s, D_MODEL), F32), _sds((s, 2 * LANES), F32), _sds((s, 2 * LANES), F32), _sds((1, LANES), F32)),
                 (nb,), [pl.BlockSpec(memory_space=pltpu.SMEM), blk, blk] + _kv_specs(),
                 (blk, kvo, kvo, pl.BlockSpec((1, LANES), lambda n: (0, 0))), ("arbitrary",), after=after)(
                     sinks, qr, dy, kk0, kk0, kk1, kk1, vv0, vv0, vv1, vv1)


def _sgu_weights(wm_ref, g):
    t = lax.broadcasted_iota(jnp.int32, (SGU_CHUNK, SGU_CHUNK), 0)
    sidx = lax.broadcasted_iota(jnp.int32, (SGU_CHUNK, SGU_CHUNK), 1)
    return jnp.where(sidx <= t, wm_ref[g], 0.0).astype(BF)


def _layer_norm_stats(v):
    mu = jnp.mean(v, axis=-1, keepdims=True)
    cen = v - mu
    rstd = lax.rsqrt(jnp.mean(cen * cen, axis=-1, keepdims=True) + NORM_EPS)
    return cen * rstd, rstd


def _sgu(name, u_pre, v_pre, ln_w, ln_b, wm, bfull, tm=512):
    s, w = u_pre.shape
    tm = min(tm, s)

    def body(u_ref, v_ref, lw_ref, lb_ref, wm_ref, b_ref, y_ref):
        vhat, _ = _layer_norm_stats(_gelu(v_ref[...]))
        vn = (vhat * lw_ref[...] + lb_ref[...]).astype(BF)
        for g in range(SGU_GROUPS):
            wg = _sgu_weights(wm_ref, g)
            cols = slice(g * SGU_CHUNK, (g + 1) * SGU_CHUNK)
            for ch in range(tm // SGU_CHUNK):
                rows = slice(ch * SGU_CHUNK, (ch + 1) * SGU_CHUNK)
                f = _dot(wg, vn[rows, cols]) + b_ref[g]
                y_ref[rows, cols] = (_gelu(u_ref[rows, cols]) * f).astype(BF)

    full3 = pl.BlockSpec((SGU_GROUPS, SGU_CHUNK, SGU_CHUNK), lambda i: (0, 0, 0))
    return _call(body, name, _sds((s, w), BF), (s // tm,),
                 [_rows(tm, w), _rows(tm, w), _vec(w), _vec(w), full3, full3],
                 _rows(tm, w), ("parallel",))(u_pre, v_pre, ln_w, ln_b, wm, bfull)


def _sgu_bwd(name, u_pre, v_pre, ln_w, ln_b, wm, bfull, dy, tm=512, after=None):
    s, w = u_pre.shape
    tm = min(tm, s)
    steps = s // tm

    def body(u_ref, v_ref, lw_ref, lb_ref, wm_ref, b_ref, dy_ref, du_ref, dv_ref, dwm_ref, db_ref, dlw_ref, dlb_ref,
             dfsum_ref):
        i = pl.program_id(0)

        @pl.when(i == 0)
        def _():
            dwm_ref[...] = jnp.zeros_like(dwm_ref)
            dlw_ref[...] = jnp.zeros_like(dlw_ref)
            dlb_ref[...] = jnp.zeros_like(dlb_ref)
            dfsum_ref[...] = jnp.zeros_like(dfsum_ref)

        vpre = v_ref[...]
        vg, dvg_dv = _gelu_and_grad(vpre)
        vhat, rstd = _layer_norm_stats(vg)
        vn = (vhat * lw_ref[...] + lb_ref[...]).astype(BF)
        t = lax.broadcasted_iota(jnp.int32, (SGU_CHUNK, SGU_CHUNK), 0)
        sidx = lax.broadcasted_iota(jnp.int32, (SGU_CHUNK, SGU_CHUNK), 1)
        dvn_cols = []
        for g in range(SGU_GROUPS):
            wg = _sgu_weights(wm_ref, g)
            cols = slice(g * SGU_CHUNK, (g + 1) * SGU_CHUNK)
            dvn_rows = []
            dwg = jnp.zeros((SGU_CHUNK, SGU_CHUNK), F32)
            dfs = jnp.zeros((SGU_CHUNK, SGU_CHUNK), F32)
            for ch in range(tm // SGU_CHUNK):
                rows = slice(ch * SGU_CHUNK, (ch + 1) * SGU_CHUNK)
                upre = u_ref[rows, cols]
                dyv = dy_ref[rows, cols].astype(F32)
                f = _dot(wg, vn[rows, cols]) + b_ref[g]
                ug, dug_du = _gelu_and_grad(upre)
                du_ref[rows, cols] = (dyv * f * dug_du).astype(BF)
                df = dyv * ug
                dfb = df.astype(BF)
                dvn_rows.append(_dot_tn(wg, dfb))
                dwg = dwg + _dot_nt(dfb, vn[rows, cols])
                dfs = dfs + df
            dwm_ref[g] += jnp.where(sidx <= t, dwg, 0.0)
            dfsum_ref[g] += dfs
            dvn_cols.append(jnp.concatenate(dvn_rows, axis=0) if len(dvn_rows) > 1 else dvn_rows[0])
        dvn = jnp.concatenate(dvn_cols, axis=1)
        dlw_ref[...] += jnp.sum(dvn * vhat, axis=0, keepdims=True)
        dlb_ref[...] += jnp.sum(dvn, axis=0, keepdims=True)
        dvh = dvn * lw_ref[...]
        dvg = rstd * (dvh - jnp.mean(dvh, axis=-1, keepdims=True) - vhat * jnp.mean(dvh * vhat, axis=-1, keepdims=True))
        dv_ref[...] = (dvg * dvg_dv).astype(BF)

        @pl.when(i == steps - 1)
        def _():
            for g in range(SGU_GROUPS):
                db_ref[g:g + 1, :] = jnp.sum(dfsum_ref[g].T, axis=0, keepdims=True)

    full3 = pl.BlockSpec((SGU_GROUPS, SGU_CHUNK, SGU_CHUNK), lambda i: (0, 0, 0))
    return _call(body, name,
                 (_sds((s, w), BF), _sds((s, w), BF), _sds((SGU_GROUPS, SGU_CHUNK, SGU_CHUNK), F32),
                  _sds((SGU_GROUPS, SGU_CHUNK), F32), _sds((1, w), F32), _sds((1, w), F32)),
                 (steps,),
                 [_rows(tm, w), _rows(tm, w), _vec(w), _vec(w), full3, full3, _rows(tm, w)],
                 (_rows(tm, w), _rows(tm, w), full3, pl.BlockSpec((SGU_GROUPS, SGU_CHUNK), lambda i: (0, 0)), _vec(w), _vec(w)),
                 ("arbitrary",), scratch=[pltpu.VMEM((SGU_GROUPS, SGU_CHUNK, SGU_CHUNK), F32)], after=after)(
                     u_pre, v_pre, ln_w, ln_b, wm, bfull, dy)


def _mix_out(name, y_sgu, y_attn, ga_pre, gb_pre, x, g1, proj_a, proj_b, w_out, w2, sc2, sh2, tm=512):
    s, d = x.shape
    tm = min(tm, s)

    def body(ys_ref, ya_ref, ga_ref, gb_ref, x_ref, g1_ref, wa_ref, wb_ref, wo_ref, w2_ref, sc2_ref, sh2_ref,
             m_ref, pa_ref, pb_ref, o_ref, x1_ref, h2_ref):
        pa = _dot(ys_ref[...], wa_ref[...].reshape(d, d))
        pb = _dot(ya_ref[...], wb_ref[...].reshape(d, d))
        pa_ref[...] = pa.astype(BF)
        pb_ref[...] = pb.astype(BF)
        merged = (_sigmoid(ga_ref[...].astype(F32)) * pa + _sigmoid(gb_ref[...].astype(F32)) * pb).astype(BF)
        m_ref[...] = merged
        o = _dot(merged, wo_ref[...].reshape(d, d))
        o_ref[...] = o.astype(BF)
        x1 = x_ref[...] + g1_ref[...] * o
        x1_ref[...] = x1
        h2_ref[...] = ((x1 * _rms(x1)) * w2_ref[...] * (1.0 + sc2_ref[...]) + sh2_ref[...]).astype(BF)

    f, b = _sds((s, d), F32), _sds((s, d), BF)
    r = _rows(tm, d)
    wspec = _resident(proj_a.shape)
    return _call(body, name, (b, b, b, b, f, b), (s // tm,),
                 [r, r, r, r, r, _vec(d), wspec, wspec, wspec, _vec(d), _vec(d), _vec(d)], (r,) * 6, ("parallel",))(
                     y_sgu, y_attn, ga_pre, gb_pre, x, g1, proj_a, proj_b, w_out, w2, sc2, sh2)


def _mix_bwd(name, do, w_out, proj_a, proj_b, ga_pre, gb_pre, pa, pb, tm=512):
    s, d = do.shape
    tm = min(tm, s)

    def body(do_ref, wo_ref, wa_ref, wb_ref, ga_ref, gb_ref, pa_ref, pb_ref,
             dpa_ref, dpb_ref, dga_ref, dgb_ref, dys_ref, dya_ref):
        dm = _dot_nt(do_ref[...], wo_ref[...].reshape(d, d))
        ga = _sigmoid(ga_ref[...].astype(F32))
        gb = _sigmoid(gb_ref[...].astype(F32))
        dpa = (dm * ga).astype(BF)
        dpb = (dm * gb).astype(BF)
        dpa_ref[...] = dpa
        dpb_ref[...] = dpb
        dga_ref[...] = (dm * pa_ref[...].astype(F32) * ga * (1.0 - ga)).astype(BF)
        dgb_ref[...] = (dm * pb_ref[...].astype(F32) * gb * (1.0 - gb)).astype(BF)
        dys_ref[...] = _dot_nt(dpa, wa_ref[...].reshape(d, d)).astype(BF)
        dya_ref[...] = _dot_nt(dpb, wb_ref[...].reshape(d, d)).astype(BF)

    f, b = _sds((s, d), F32), _sds((s, d), BF)
    r = _rows(tm, d)
    wspec = _resident(w_out.shape)
    return _call(body, name, (b, b, b, b, b, b), (s // tm,), [r, wspec, wspec, wspec, r, r, r, r], (r,) * 6,
                 ("parallel",))(do, w_out, proj_a, proj_b, ga_pre, gb_pre, pa, pb)


def _ffn_up_act(name, h2, w_gate, w_up, cw, cb, tm=1024):
    s, d = h2.shape
    tm = min(tm, s)
    tc = FFN_CHUNK
    per = tm // HALO

    def body(h_ref, hprev_ref, wg_ref, wu_ref, cw_ref, cb_ref, a_ref, ac_ref, up_ref, hf_ref):
        hv = h_ref[...]
        wg = wg_ref[...].reshape(tc, d)
        a = _dot_nt(hv, wg).astype(BF)
        up = _dot_nt(hv, wu_ref[...].reshape(tc, d)).astype(BF)
        a_ref[...] = a
        up_ref[...] = up
        prev = jnp.where(pl.program_id(1) > 0, _dot_nt(hprev_ref[...], wg).astype(BF).astype(F32), 0.0)
        ext = jnp.concatenate([prev, a.astype(F32)], axis=0)
        ac = (cb_ref[...] + cw_ref[0:1, :] * pltpu.roll(ext, 2, axis=0) + cw_ref[1:2, :] * pltpu.roll(ext, 1, axis=0)
              + cw_ref[2:3, :] * ext)[HALO:]
        ac_ref[...] = ac.astype(BF)
        hf_ref[...] = (ac * _sigmoid(ac) * up.astype(F32)).astype(BF)

    wspec = pl.BlockSpec((CHIPS_PER_CHUNK, FFN_SHARD, d), lambda j, i: (j, 0, 0))
    ospec = pl.BlockSpec((tm, tc), lambda j, i: (i, j))
    o = _sds((s, FFN_DIM), BF)
    return _call(body, name, (o, o, o, o), (FFN_CHUNKS, s // tm),
                 [pl.BlockSpec((tm, d), lambda j, i: (i, 0)), pl.BlockSpec((HALO, d), lambda j, i: (jnp.maximum(i * per - 1, 0), 0)),
                  wspec, wspec, pl.BlockSpec((3, tc), lambda j, i: (0, j)), pl.BlockSpec((1, tc), lambda j, i: (0, j))],
                 (ospec, ospec, ospec, ospec), ("parallel", "parallel"))(h2, h2, w_gate, w_up, cw, cb)


def _ffn_down(name, hf, w_down, x1, g2, tm=512):
    s, d = x1.shape
    tm = min(tm, s)

    def body(hf_ref, wd_ref, x1_ref, g2_ref, dn_ref, x2_ref):
        dn = _dot(hf_ref[...], wd_ref[...].reshape(FFN_DIM, d))
        dn_ref[...] = dn.astype(BF)
        x2_ref[...] = x1_ref[...] + g2_ref[...] * dn

    return _call(body, name, (_sds((s, d), BF), _sds((s, d), F32)), (s // tm,),
                 [_rows(tm, FFN_DIM), _resident(w_down.shape), _rows(tm, d), _vec(d)],
                 (_rows(tm, d), _rows(tm, d)), ("parallel",))(hf, w_down, x1, g2)


def _ffn_down_bwd_act(name, dx2, dn, g2, w_down, a, ac, up, cw, tm=256, after=None):
    s, d = dx2.shape
    c = a.shape[1]
    tm = min(tm, s)
    tc = FFN_CHUNK
    per = tm // HALO
    steps = s // tm
    last = s // HALO - 1
    n = tm + HALO

    def body(dx_ref, dxnext_ref, dn_ref, g2_ref, wd_ref, a_ref, ac_ref, acnext_ref, up_ref, upnext_ref, cw_ref,
             ddn_ref, da_ref, dup_ref, dg_ref, dcw_ref, dcb_ref):
        i = pl.program_id(0)

        @pl.when(i == 0)
        def _():
            dg_ref[...] = jnp.zeros_like(dg_ref)
            dcw_ref[...] = jnp.zeros_like(dcw_ref)
            dcb_ref[...] = jnp.zeros_like(dcb_ref)

        dxv = dx_ref[...]
        ddn = (dxv * g2_ref[...]).astype(BF)
        ddn_ref[...] = ddn
        dg_ref[...] += jnp.sum(dxv * dn_ref[...].astype(F32), axis=0, keepdims=True)
        ddn_next = jnp.where(i < steps - 1, dxnext_ref[...] * g2_ref[...], 0.0).astype(BF)
        ddn_ext = jnp.concatenate([ddn, ddn_next], axis=0)
        for k in range(FFN_CHUNKS):
            cols = slice(k * tc, (k + 1) * tc)
            dh = _dot_nt(ddn_ext, wd_ref[k * CHIPS_PER_CHUNK:(k + 1) * CHIPS_PER_CHUNK].reshape(tc, d))
            ace = jnp.concatenate([ac_ref[:, cols].astype(F32), acnext_ref[:, cols].astype(F32)], axis=0)
            upe = jnp.concatenate([up_ref[:, cols].astype(F32), upnext_ref[:, cols].astype(F32)], axis=0)
            sig = _sigmoid(ace)
            silu = ace * sig
            dac = dh * upe * (sig + silu * (1.0 - sig))
            dup_ref[:, cols] = (dh[:tm] * silu[:tm]).astype(BF)
            d1 = pltpu.roll(dac, n - 1, axis=0)[:tm]
            d2 = pltpu.roll(dac, n - 2, axis=0)[:tm]
            d0 = dac[:tm]
            da_ref[:, cols] = (cw_ref[2:3, cols] * d0 + cw_ref[1:2, cols] * d1 + cw_ref[0:1, cols] * d2).astype(BF)
            a0 = a_ref[:, cols].astype(F32)
            dcb_ref[:, cols] += jnp.sum(d0, axis=0, keepdims=True)
            dcw_ref[0:1, cols] += jnp.sum(d2 * a0, axis=0, keepdims=True)
            dcw_ref[1:2, cols] += jnp.sum(d1 * a0, axis=0, keepdims=True)
            dcw_ref[2:3, cols] += jnp.sum(d0 * a0, axis=0, keepdims=True)

    nxt = lambda width: pl.BlockSpec((HALO, width), lambda i: (jnp.minimum((i + 1) * per, last), 0))
    wide = _sds((s, c), BF)
    return _call(body, name, (_sds((s, d), BF), wide, wide, _sds((1, d), F32), _sds((3, c), F32), _sds((1, c), F32)), (steps,),
                 [_rows(tm, d), nxt(d), _rows(tm, d), _vec(d), _resident(w_down.shape), _rows(tm, c), _rows(tm, c), nxt(c),
                  _rows(tm, c), nxt(c), pl.BlockSpec((3, c), lambda i: (0, 0))],
                 (_rows(tm, d), _rows(tm, c), _rows(tm, c), _vec(d), pl.BlockSpec((3, c), lambda i: (0, 0)), _vec(c)),
                 ("arbitrary",), after=after)(dx2, dx2, dn, g2, w_down, a, ac, ac, up, up, cw)


def _ffn_up_bwd(name, da, dup, w_gate, w_up, x1, dx2, w2, sc2, o, g1, tm=512, after=None):
    s, d = x1.shape
    tm = min(tm, s)

    def body(da_ref, dup_ref, wg_ref, wu_ref, x1_ref, dx2_ref, w2_ref, sc2_ref, o_ref, g1_ref,
             dx1_ref, do_ref, dnw_ref, dsh_ref, dg1_ref):
        @pl.when(pl.program_id(0) == 0)
        def _():
            dnw_ref[...] = jnp.zeros_like(dnw_ref)
            dsh_ref[...] = jnp.zeros_like(dsh_ref)
            dg1_ref[...] = jnp.zeros_like(dg1_ref)

        dh = _dot(da_ref[...], wg_ref[...].reshape(FFN_DIM, d)) + _dot(dup_ref[...], wu_ref[...].reshape(FFN_DIM, d))
        xv = x1_ref[...]
        r = _rms(xv)
        xn = xv * r
        dxn = dh * (w2_ref[...] * (1.0 + sc2_ref[...]))
        dx1 = dx2_ref[...] + r * (dxn - xn * jnp.mean(dxn * xn, axis=-1, keepdims=True))
        dx1_ref[...] = dx1
        dnw_ref[...] += jnp.sum(dh * xn, axis=0, keepdims=True)
        dsh_ref[...] += jnp.sum(dh, axis=0, keepdims=True)
        do_ref[...] = (dx1 * g1_ref[...]).astype(BF)
        dg1_ref[...] += jnp.sum(dx1 * o_ref[...].astype(F32), axis=0, keepdims=True)

    v = _sds((1, d), F32)
    r = _rows(tm, d)
    wspec = _resident(w_gate.shape)
    return _call(body, name, (_sds((s, d), F32), _sds((s, d), BF), v, v, v), (s // tm,),
                 [_rows(tm, FFN_DIM), _rows(tm, FFN_DIM), wspec, wspec, r, r, _vec(d), _vec(d), r, _vec(d)],
                 (r, r, _vec(d), _vec(d), _vec(d)), ("arbitrary",), after=after)(da, dup, w_gate, w_up, x1, dx2, w2, sc2, o, g1)


def _loss_head(name, x, w, target, tm=512):
    s, d = x.shape
    tm = min(tm, s)

    def body(x_ref, w_ref, t_ref, dx_ref, loss_ref, dw_ref):
        @pl.when(pl.program_id(0) == 0)
        def _():
            loss_ref[...] = jnp.zeros_like(loss_ref)
            dw_ref[...] = jnp.zeros_like(dw_ref)

        xv = x_ref[...]
        r = _rms(xv)
        xn = xv * r
        err = xn * w_ref[...] - t_ref[...]
        loss_ref[...] += 0.5 * jnp.sum(jnp.mean(err * err, axis=-1, keepdims=True))
        dy = err * (1.0 / d)
        dw_ref[...] += jnp.sum(dy * xn, axis=0, keepdims=True)
        dxn = dy * w_ref[...]
        dx_ref[...] = r * (dxn - xn * jnp.mean(dxn * xn, axis=-1, keepdims=True))

    return _call(body, name, (_sds((s, d), F32), _sds((1, LANES), F32), _sds((1, d), F32)), (s // tm,),
                 [_rows(tm, d), _vec(d), _rows(tm, d)], (_rows(tm, d), _vec(LANES), _vec(d)), ("arbitrary",))(x, w, target)


def _layer_fwd(l, x, mod, p, cosf, sinf, after=None, late=None, later=None):
    sh1, sc1, g1, sh2, sc2, g2 = mod
    tag = f"l{l}_"
    h, qr, kk0, kk1, vv0, vv1, u_pre, v_pre, ga_pre, gb_pre = _in_proj(
        tag + "in_proj", x, p["norm1_w"], sc1, sh1, p["w_in"], cosf, sinf, after=after)
    y_attn = _attention(tag + "attn", qr, kk0, kk1, vv0, vv1, p["sinks"])
    y_sgu = _sgu(tag + "sgu", u_pre, v_pre, p["sgu_ln_w"], p["sgu_ln_b"], p["sgu_w"], p["sgu_bfull"])
    if late is not None:
        p = dict(p, **late(y_sgu))
    merged, pa, pb, o, x1, h2 = _mix_out(tag + "mix_out", y_sgu, y_attn, ga_pre, gb_pre, x, g1, p["proj_a"], p["proj_b"],
                                         p["w_out"], p["norm2_w"], sc2, sh2)
    if later is not None:
        p = dict(p, **later(h2))
    a, ac, up, hf = _ffn_up_act(tag + "ffn_up", h2, p["w_gate"], p["w_up"], p["conv_w"], p["conv_b"])
    dn, x2 = _ffn_down(tag + "ffn_down", hf, p["w_down"], x1, g2)
    saved = dict(x=x, h=h, qr=qr, kk0=kk0, kk1=kk1, vv0=vv0, vv1=vv1, u_pre=u_pre, v_pre=v_pre, ga_pre=ga_pre,
                 gb_pre=gb_pre, y_attn=y_attn, y_sgu=y_sgu, merged=merged, pa=pa, pb=pb, o=o, x1=x1, h2=h2, a=a, ac=ac, up=up,
                 hf=hf, dn=dn)
    return x2, saved, p


def _layer_bwd(l, dx2, mod, p, sv, cosf, sinf, after=None, emit=None, tick=None):
    sh1, sc1, g1, sh2, sc2, g2 = mod
    tag = f"l{l}_b_"
    d = D_MODEL
    g = {}
    ready = (lambda names: emit({k: g.pop(k) for k in names})) if emit else (lambda names: None)
    tick = tick or (lambda y: None)
    ddn, da, dup, dg2, g["conv_w"], g["conv_b"] = _ffn_down_bwd_act(
        tag + "ffn_down", dx2, sv["dn"], g2, p["w_down"], sv["a"], sv["ac"], sv["up"], p["conv_w"], after=after)
    g["w_down"] = _matmul_tn(tag + "dw_down", sv["hf"], ddn, tk=FFN_CHUNK, after=tick(ddn)).reshape(N_CHIPS, FFN_SHARD, d)
    g["w_gate"] = _matmul_tn(tag + "dw_gate", da, sv["h2"], tk=FFN_CHUNK).reshape(N_CHIPS, FFN_SHARD, d)
    g["w_up"] = _matmul_tn(tag + "dw_up", dup, sv["h2"], tk=FFN_CHUNK).reshape(N_CHIPS, FFN_SHARD, d)
    dx1, do, da2, dsh2, dg1 = _ffn_up_bwd(tag + "ffn_up", da, dup, p["w_gate"], p["w_up"], sv["x1"], dx2, p["norm2_w"],
                                          sc2, sv["o"], g1, after=ready(("w_down", "w_gate", "w_up")))
    g["norm2_w"] = da2 * (1.0 + sc2)
    dsc2 = da2 * p["norm2_w"]
    g["w_out"] = _matmul_tn(tag + "dw_out", sv["merged"], do, after=tick(do)).reshape(N_CHIPS, d // N_CHIPS, d)
    dpa, dpb, dga, dgb, dy_sgu, dy_attn = _mix_bwd(tag + "mix", do, p["w_out"], p["proj_a"], p["proj_b"], sv["ga_pre"],
                                                  sv["gb_pre"], sv["pa"], sv["pb"])
    g["proj_a"] = _matmul_tn(tag + "dproj_a", sv["y_sgu"], dpa).reshape(N_CHIPS, d // N_CHIPS, d)
    g["proj_b"] = _matmul_tn(tag + "dproj_b", sv["y_attn"], dpb).reshape(N_CHIPS, d // N_CHIPS, d)
    du, dv, g["sgu_w"], g["sgu_b"], g["sgu_ln_w"], g["sgu_ln_b"] = _sgu_bwd(
        tag + "sgu", sv["u_pre"], sv["v_pre"], p["sgu_ln_w"], p["sgu_ln_b"], p["sgu_w"], p["sgu_bfull"], dy_sgu,
        after=ready(("w_out", "proj_a", "proj_b")))
    dqr, dkv_cur, dkv_prev, dsink = _attention_bwd(tag + "attn", sv["qr"], sv["kk0"], sv["kk1"], sv["vv0"], sv["vv1"],
                                                   p["sinks"], dy_attn, after=tick(du))
    g["sinks"] = dsink[0, :N_Q_HEADS]
    dq, dkv = _rope_bwd(tag + "rope", dqr, dkv_cur, dkv_prev, cosf, sinf)
    dw_in = _matmul_tn_segments(tag + "dw_in", [dq, dkv, du, dv, dga, dgb], sv["h"])
    g["w_in"] = dw_in.reshape(N_CHIPS, IN_COLS // N_CHIPS, d)
    dx, da1, dsh1 = _in_proj_bwd(tag + "in_proj", dq, dkv, du, dv, dga, dgb, p["w_in"], sv["x"], p["norm1_w"], sc1, dx1)
    g["norm1_w"] = da1 * (1.0 + sc1)
    dsc1 = da1 * p["norm1_w"]
    return dx, (dsh1, dsc1, dg1, dsh2, dsc2, dg2), g


def _pad_to(a, axis, size):
    pad = [(0, 0)] * a.ndim
    pad[axis] = (0, size - a.shape[axis])
    return jnp.pad(a, pad)


def _early_params(w_in, small):
    d = D_MODEL
    return dict(
        w_in=w_in.reshape(IN_COLS, d), norm1_w=small["norm1_w"].reshape(1, d), sinks=small["sinks"],
        sgu_ln_w=small["sgu_ln_w"].reshape(1, d), sgu_ln_b=small["sgu_ln_b"].reshape(1, d), sgu_w=small["sgu_w"],
        sgu_bfull=jnp.broadcast_to(small["sgu_b"][:, :, None], (SGU_GROUPS, SGU_CHUNK, SGU_CHUNK)))


def _mix_params(proj_a, proj_b, w_out, small):
    return dict(proj_a=proj_a, proj_b=proj_b, w_out=w_out, norm2_w=small["norm2_w"].reshape(1, D_MODEL))


def _ffn_params(w_gate, w_up, w_down, conv_w, small):
    return dict(
        w_gate=w_gate, w_up=w_up, w_down=w_down, conv_w=conv_w.transpose(1, 0, 2).reshape(3, FFN_DIM),
        conv_b=small["conv_b"].reshape(1, FFN_DIM))


def _layer_params(w_in, proj_a, proj_b, w_out, w_gate, w_up, w_down, conv_w, small):
    return dict(_early_params(w_in, small), **_mix_params(proj_a, proj_b, w_out, small),
                **_ffn_params(w_gate, w_up, w_down, conv_w, small))


def _conv_grads_natural(g):
    cw = g["conv_w"]
    cb = g["conv_b"].reshape(FFN_DIM)
    return cw, cb


def _rope_tables(positions):
    inv_freq = ROPE_THETA ** (-jnp.arange(0, ROT_DIM, 2, dtype=F32) / ROT_DIM)
    ang = positions.astype(F32)[:, None] * inv_freq
    cos, sin = jnp.cos(ang), jnp.sin(ang)
    s = positions.shape[0]
    rest = HEAD_DIM - ROT_DIM
    cos_head = jnp.concatenate([cos, cos, jnp.ones((s, rest), F32)], axis=1)
    sin_head = jnp.concatenate([-sin, sin, jnp.zeros((s, rest), F32)], axis=1)
    return jnp.tile(cos_head, (1, LANES // HEAD_DIM)), jnp.tile(sin_head, (1, LANES // HEAD_DIM))


ADA_ROWS = 16


def _ada_fwd(name, c_rows, ada_w, ada_b_cols, tn=512):
    depth, d, n = ada_w.shape

    def body(c_ref, w_ref, b_ref, o_ref):
        cv = c_ref[...]
        act = (cv * _sigmoid(cv)).astype(BF)
        o_ref[0] = _dot(act, w_ref[0].astype(BF)) + b_ref[0]

    return _call(body, name, _sds((depth, ADA_ROWS, n), F32), (depth, n // tn),
                 [pl.BlockSpec((ADA_ROWS, d), lambda l, j: (0, 0)), pl.BlockSpec((1, d, tn), lambda l, j: (l, 0, j)),
                  pl.BlockSpec((1, 1, tn), lambda l, j: (l, 0, j))],
                 pl.BlockSpec((1, ADA_ROWS, tn), lambda l, j: (l, 0, j)), ("parallel", "parallel"))(c_rows, ada_w, ada_b_cols)


def _ada_bwd(name, c_rows, dmod_cols, tn=512):
    depth, _, n = dmod_cols.shape
    d = c_rows.shape[1]

    def body(c_ref, dm_ref, o_ref):
        cv = c_ref[...]
        act = (cv * _sigmoid(cv)).astype(BF)
        o_ref[0] = _dot_tn(act, dm_ref[0].astype(BF))

    return _call(body, name, _sds((depth, d, n), F32), (depth, n // tn),
                 [pl.BlockSpec((ADA_ROWS, d), lambda l, j: (0, 0)), pl.BlockSpec((1, ADA_ROWS, tn), lambda l, j: (l, 0, j))],
                 pl.BlockSpec((1, d, tn), lambda l, j: (l, 0, j)), ("parallel", "parallel"))(c_rows, dmod_cols)


def _colsum(name, a):
    r, n = a.shape

    def body(a_ref, o_ref):
        o_ref[...] = jnp.sum(a_ref[...], axis=0, keepdims=True)

    return _call(body, name, _sds((1, n), F32), (1,), [pl.BlockSpec((r, n), lambda i: (0, 0))],
                 pl.BlockSpec((1, n), lambda i: (0, 0)), ("arbitrary",))(a)


REL_SIBLING = (0, 0, 1)
REL_CHIPS = ((1, 0, 0), (0, 1, 0), (1, 1, 0))
REL_ALL = tuple((fx, fy, fc) for fx in (0, 1) for fy in (0, 1) for fc in (0, 1) if fx or fy or fc)


def _chip_of(dev):
    return 2 * dev[0] + dev[1]


def _dev_of(dev):
    return 4 * dev[0] + 2 * dev[1] + dev[2]


def _flip(dev, rel):
    return tuple(1 - m if f else m for m, f in zip(dev, rel))


def _exchange(name, arrays, n_out, stages, aliases=None):
    out_shapes, stages = stages[0], stages[1:]
    n_in = len(arrays)
    aliases = aliases or {}
    n_remote = sum(len(plan) for plan, _ in stages)
    n_local = sum(len(local) for _, local in stages)

    def at(ref, idx):
        return ref.at[idx] if len(idx) else ref

    def body(*refs):
        bufs = list(refs[:n_in + n_out])
        for i_in, i_out in aliases.items():
            bufs[i_in] = bufs[n_in + i_out]
        send_sems, recv_sems, local_sems = refs[n_in + n_out:]
        me = (lax.axis_index("x"), lax.axis_index("y"), lax.axis_index("c"))
        base_r = base_l = 0
        pending = []
        for plan, local in stages:
            def remote(k, entry, sender, receiver):
                rel, si, ssel, di, dsel = entry
                return pltpu.make_async_remote_copy(
                    src_ref=at(bufs[si], ssel(sender, receiver)), dst_ref=at(bufs[di], dsel(sender, receiver)),
                    send_sem=send_sems.at[k], recv_sem=recv_sems.at[k], device_id=_flip(me, rel), device_id_type=MESH)

            sends = [remote(base_r + k, e, me, _flip(me, e[0])) for k, e in enumerate(plan)]
            for cp in sends:
                cp.start()
            for k, (si, ssel, di, dsel) in enumerate(local):
                cp = pltpu.make_async_copy(at(bufs[si], ssel(me)), at(bufs[di], dsel(me)), local_sems.at[base_l + k])
                cp.start()
                pending.append(cp.wait)
            for k, e in enumerate(plan):
                remote(base_r + k, e, _flip(me, e[0]), me).wait_recv()
            pending += [cp.wait_send for cp in sends]
            base_r += len(plan)
            base_l += len(local)
        for wait in pending:
            wait()

    any_spec = pl.BlockSpec(memory_space=pl.ANY)
    return pl.pallas_call(
        body, name=name, out_shape=tuple(out_shapes), in_specs=[any_spec] * n_in, out_specs=tuple([any_spec] * n_out),
        input_output_aliases=dict(aliases),
        scratch_shapes=[pltpu.SemaphoreType.DMA((max(n_remote, 1),)), pltpu.SemaphoreType.DMA((max(n_remote, 1),)),
                        pltpu.SemaphoreType.DMA((max(n_local, 1),))])(*arrays)


HBM_SPEC = pl.BlockSpec(memory_space=pltpu.HBM)
SEM_SPEC = pl.BlockSpec(memory_space=pltpu.SEMAPHORE)


def _split_copies(bufs, plan, local, send_sems, recv_sems, local_sems):
    me = (lax.axis_index("x"), lax.axis_index("y"), lax.axis_index("c"))

    def at(ref, idx):
        return ref.at[idx] if len(idx) else ref

    def remote(k, sender, receiver):
        rel, si, ssel, di, dsel = plan[k]
        return pltpu.make_async_remote_copy(
            src_ref=at(bufs[si], ssel(sender, receiver)), dst_ref=at(bufs[di], dsel(sender, receiver)),
            send_sem=send_sems.at[k], recv_sem=recv_sems.at[k], device_id=_flip(me, rel), device_id_type=MESH)

    sends = [remote(k, me, _flip(me, plan[k][0])) for k in range(len(plan))]
    arrivals = [remote(k, _flip(me, plan[k][0]), me) for k in range(len(plan))]
    locs = [pltpu.make_async_copy(at(bufs[si], ssel(me)), at(bufs[di], dsel(me)), local_sems.at[k])
            for k, (si, ssel, di, dsel) in enumerate(local)]
    return sends, arrivals, locs


def _exchange_start_many(name, stages):
    sizes = [(len(arrays), len(out_shapes)) for _, arrays, out_shapes, _, _, _ in stages]
    n_buf = sum(n_in + n_out for n_in, n_out in sizes)
    n_sem = 3 * len(stages)

    def body(*refs):
        sems, bufs = refs[n_buf:n_buf + n_sem], refs[n_buf + n_sem:2 * n_buf + n_sem]
        at = 0
        for g, ((n_in, n_out), (_, _, _, plan, local, inplace)) in enumerate(zip(sizes, stages)):
            mine = bufs[at:at + n_in + n_out]
            sends, _, locs = _split_copies(mine * 2 if inplace else mine, plan, local, *sems[3 * g:3 * g + 3])
            for cp in sends + locs:
                cp.start()
            at += n_in + n_out
        refs[-1][...] = jnp.zeros_like(refs[-1])

    operands, sem_shapes = [], []
    for _, arrays, out_shapes, plan, local, _ in stages:
        operands += list(arrays) + [lax.empty(o.shape, o.dtype) for o in out_shapes]
        sem_shapes += [pltpu.SemaphoreType.DMA((max(k, 1),)) for k in (len(plan), len(plan), len(local))]
    operands = [pltpu.with_memory_space_constraint(a, pltpu.HBM) for a in operands]
    out = pl.pallas_call(
        body, name=name,
        out_shape=(*sem_shapes, *[pltpu.HBM(a.shape, a.dtype) for a in operands], _sds((SUBLANES, LANES), F32)),
        in_specs=[HBM_SPEC] * n_buf,
        out_specs=(*[SEM_SPEC] * n_sem, *[HBM_SPEC] * n_buf, pl.BlockSpec(memory_space=pltpu.VMEM)),
        input_output_aliases={i: n_sem + i for i in range(n_buf)},
        compiler_params=pltpu.CompilerParams(has_side_effects=pltpu.SideEffectType.DATAFLOW_SIDE_EFFECTING))(*operands)
    pendings, at = [], n_sem
    for g, ((n_in, n_out), (stage, _, _, plan, local, inplace)) in enumerate(zip(sizes, stages)):
        pendings.append(dict(name=stage, sems=out[3 * g:3 * g + 3], thru=out[at:at + n_in], zones=out[at + n_in:at + n_in + n_out],
                             plan=plan, local=local, inplace=inplace))
        at += n_in + n_out
    return pendings, out[-1]


def _exchange_start(name, arrays, out_shapes, plan, local, inplace=False):
    pendings, token = _exchange_start_many(name, [(name, arrays, out_shapes, plan, local, inplace)])
    return pendings[0], token


def _exchange_wait(pending, after, both=False):
    thru, zones, plan, local, inplace = (pending[k] for k in ("thru", "zones", "plan", "local", "inplace"))
    n_in, n_buf = len(thru), len(thru) + len(zones)

    def body(*refs):
        bufs = refs[:n_buf]
        sends, arrivals, locs = _split_copies(bufs * 2 if inplace else bufs, plan, local, *refs[n_buf:n_buf + 3])
        for cp in arrivals:
            cp.wait_recv()
        for cp in sends:
            cp.wait_send()
        for cp in locs:
            cp.wait()

    out = pl.pallas_call(
        body, name=pending["name"] + "_wait", out_shape=tuple(pltpu.HBM(a.shape, a.dtype) for a in list(thru) + list(zones)),
        in_specs=[HBM_SPEC] * n_buf + [SEM_SPEC] * 3 + [pl.BlockSpec(memory_space=pl.ANY)],
        out_specs=tuple([HBM_SPEC] * n_buf), input_output_aliases={i: i for i in range(n_buf)},
        compiler_params=pltpu.CompilerParams(has_side_effects=pltpu.SideEffectType.DATAFLOW_SIDE_EFFECTING))(
            *thru, *zones, *pending["sems"], after)
    if both:
        return out[:n_in], out[n_in:]
    return out[:n_in] if inplace else out[n_in:]


def _whole(*_):
    return ()


def _half_rows(rows, core):
    return pl.ds(core * (rows // 2), rows // 2)


def _gather_weights_plan(shards):
    n = len(shards)
    dsts = [_sds((N_CHIPS,) + a.shape, a.dtype) for a in shards]
    fetch, forward = [], []
    for t, a in enumerate(shards):
        rows = a.shape[0]
        if rows % (2 * 16) == 0:
            fetch += [(rel, t, (lambda s_, r_, rows=rows: (_half_rows(rows, s_[2]),)), n + t,
                       (lambda s_, r_, rows=rows: (_chip_of(s_), _half_rows(rows, s_[2])))) for rel in REL_CHIPS]
            forward += [(REL_SIBLING, n + t, (lambda s_, r_, rows=rows, rel=rel: (_chip_of(_flip(s_, rel)), _half_rows(rows, s_[2]))),
                         n + t, (lambda s_, r_, rows=rows, rel=rel: (_chip_of(_flip(s_, rel)), _half_rows(rows, s_[2]))))
                        for rel in REL_CHIPS]
        else:
            fetch += [(rel, t, _whole, n + t, lambda s_, r_: (_chip_of(s_),)) for rel in REL_CHIPS]
    local = [(t, _whole, n + t, lambda me: (_chip_of(me),)) for t in range(n)]
    return dsts, fetch, local, forward


def _gather_weights_finish(pending, after):
    landed = _exchange_wait(pending, after)
    n = len(landed)
    return _exchange(pending["name"] + "_forward", landed, n, [[_sds(a.shape, a.dtype) for a in landed], (pending["forward"], [])],
                     aliases={t: t for t in range(n)})


def _gather_chips_plan(arrays):
    n = len(arrays)
    dsts = [_sds((N_CHIPS,) + a.shape, a.dtype) for a in arrays]
    plan = [(rel, t, _whole, n + t, lambda s_, r_: (_chip_of(s_),)) for t in range(n) for rel in REL_CHIPS]
    local = [(t, _whole, n + t, lambda me: (_chip_of(me),)) for t in range(n)]
    return dsts, plan, local


def _gather_chips(name, arrays):
    dsts, plan, local = _gather_chips_plan(arrays)
    return _exchange(name, arrays, len(arrays), [dsts, (plan, local)])


def _gather_chips_start(name, arrays):
    dsts, plan, local = _gather_chips_plan(arrays)
    return _exchange_start(name, arrays, dsts, plan, local)


def _gather_all(name, a):
    plan = [(rel, 0, _whole, 1, lambda s_, r_: (_dev_of(s_),)) for rel in REL_ALL]
    local = [(0, _whole, 1, lambda me: (_dev_of(me),))]
    return _exchange(name, [a], 1, [[_sds((2 * N_CHIPS,) + a.shape, a.dtype)], (plan, local)])[0]


def _swap_halves_start(name, grads):
    n = len(grads)
    dsts = [_sds((g.shape[0], g.shape[1] // 2, g.shape[2]), g.dtype) for g in grads]
    plan = [(REL_SIBLING, t, (lambda s_, r_, rows=g.shape[1]: (pl.ds(0, N_CHIPS), _half_rows(rows, r_[2]))), n + t, _whole)
            for t, g in enumerate(grads)]
    return _exchange_start(name, grads, dsts, plan, [])


def _scatter_chips_plan(sums):
    n = len(sums)
    dsts = [_sds(a.shape, a.dtype) for a in sums]
    plan = [(rel, t, lambda s_, r_: (_chip_of(r_),), n + t, lambda s_, r_: (_chip_of(s_),))
            for t in range(n) for rel in REL_CHIPS]
    local = [(t, lambda me: (_chip_of(me),), n + t, lambda me: (_chip_of(me),)) for t in range(n)]
    return dsts, plan, local


def _scatter_chips_start(name, sums):
    dsts, plan, local = _scatter_chips_plan(sums)
    return _exchange_start(name, sums, dsts, plan, local)


def _swap_back_start(name, totals, layer):
    n = len(totals)
    plan = [(REL_SIBLING, n + t, (lambda s_, r_, rows=a.shape[1]: (layer, _half_rows(rows, s_[2]))),
             n + t, (lambda s_, r_, rows=a.shape[1]: (layer, _half_rows(rows, s_[2])))) for t, a in enumerate(totals)]
    return _exchange_start(name, totals, [], plan, [], inplace=True)


def _add_halves(name, gs, recvs, core):
    n = len(gs)
    nch = recvs[0].shape[0]

    def body(core_ref, *refs):
        for g_ref, r_ref, o_ref in zip(refs[:n], refs[n:2 * n], refs[2 * n:]):
            o_ref[0] = (g_ref[0, 0].astype(F32) + r_ref[0].astype(F32)).astype(o_ref.dtype)

    halves = [pl.BlockSpec((1,) + r.shape[1:], lambda k, core_ref: (k, 0, 0)) for r in recvs]
    spec = pltpu.PrefetchScalarGridSpec(
        num_scalar_prefetch=1, grid=(nch,),
        in_specs=[pl.BlockSpec((1, 1) + r.shape[1:], lambda k, core_ref: (k, core_ref[0], 0, 0)) for r in recvs] + halves,
        out_specs=tuple(halves))
    return pl.pallas_call(body, name=name, out_shape=tuple(_sds(r.shape, r.dtype) for r in recvs), grid_spec=spec,
                          compiler_params=pltpu.CompilerParams(dimension_semantics=("parallel",),
                                                               vmem_limit_bytes=VMEM_LIMIT))(
                                                                   core, *[g.reshape(nch, 2, *r.shape[1:]) for g, r in zip(gs, recvs)],
                                                                   *recvs)


def _sum_chips(name, arrays, core, layer, totals):
    n = len(arrays)
    nch = arrays[0].shape[0]
    parts = 2
    shapes = [(a.shape[1] // parts, a.shape[2]) for a in arrays]

    def body(core_ref, *refs):
        for a_ref, o_ref in zip(refs[:n], refs[-n:]):
            acc = a_ref[0].astype(F32)
            for k in range(1, nch):
                acc = acc + a_ref[k].astype(F32)
            o_ref[0, 0, 0] = acc

    in_specs = [pl.BlockSpec((nch,) + shp, lambda i, core_ref: (0, i, 0)) for shp in shapes]
    args = [core, *arrays]
    if totals is not None:
        in_specs += [pl.BlockSpec(memory_space=pl.ANY)] * n
        args += [t.reshape(DEPTH, 2, parts, *shp) for t, shp in zip(totals, shapes)]
    spec = pltpu.PrefetchScalarGridSpec(
        num_scalar_prefetch=1, grid=(parts,), in_specs=in_specs,
        out_specs=tuple(pl.BlockSpec((1, 1, 1) + shp, lambda i, core_ref: (layer, core_ref[0], i, 0, 0)) for shp in shapes))
    outs = pl.pallas_call(body, name=name, out_shape=tuple(_sds((DEPTH, 2, parts) + shp, F32) for shp in shapes), grid_spec=spec,
                          input_output_aliases={1 + n + t: t for t in range(n)} if totals is not None else {},
                          compiler_params=pltpu.CompilerParams(dimension_semantics=("arbitrary",),
                                                               vmem_limit_bytes=VMEM_LIMIT))(*args)
    return [o.reshape(DEPTH, 2 * parts * shp[0], shp[1]) for o, shp in zip(outs, shapes)]


def _adamw_update(w, g, m, v):
    mn = ADAM_B1 * m + (1.0 - ADAM_B1) * g
    vn = ADAM_B2 * v + (1.0 - ADAM_B2) * (g * g)
    m_hat = mn / (1.0 - ADAM_B1 ** ADAM_STEP)
    v_hat = vn / (1.0 - ADAM_B2 ** ADAM_STEP)
    return -ADAM_LR * (m_hat / (jnp.sqrt(v_hat) + ADAM_EPS) + ADAM_WD * w), mn, vn


def _adamw(name, w, g, m, v, after=None):
    depth, r, c = w.shape
    tr = next(t for t in (512, 448, 384, 352, 336, 256, 192, 128, 64, 32, 16, 8) if r % t == 0 and t * c <= ADAM_TILE_ELEMS)

    def body(w_ref, g_ref, m_ref, v_ref, go_ref, d_ref, mo_ref, vo_ref):
        gv = g_ref[...]
        go_ref[...] = gv
        d_ref[...], mo_ref[...], vo_ref[...] = _adamw_update(w_ref[...], gv, m_ref[...], v_ref[...])

    spec = pl.BlockSpec((1, tr, c), lambda l, i: (l, i, 0))
    o = _sds(w.shape, F32)
    return _call(body, name, (o, o, o, o), (depth, r // tr), [spec] * 4, (spec,) * 4, ("parallel", "parallel"),
                 after=after)(w, g, m, v)


def _adamw_small(name, ws, gs, ms, vs):
    n = len(ws)

    def body(*refs):
        for t in range(n):
            w_ref, g_ref, m_ref, v_ref = (refs[k * n + t] for k in range(4))
            d_ref, mo_ref, vo_ref = (refs[(4 + k) * n + t] for k in range(3))
            d_ref[...], mo_ref[...], vo_ref[...] = _adamw_update(w_ref[...], g_ref[...], m_ref[...], v_ref[...])

    outs = [_sds(w.shape, F32) for w in ws]
    res = pl.pallas_call(body, name=name, out_shape=tuple(outs * 3))(*ws, *gs, *ms, *vs)
    return res[:n], res[n:2 * n], res[2 * n:]


def _pack(arrays, rows):
    flat = jnp.concatenate([a.reshape(-1).astype(F32) for a in arrays])
    return _pad_to(flat, 0, rows * LANES).reshape(rows, LANES)


def _unpack(packed, shapes):
    flat = packed.reshape(-1)
    out, off = [], 0
    for shp in shapes:
        n = 1
        for s_ in shp:
            n *= s_
        out.append(flat[off:off + n].reshape(shp))
        off += n
    return out


_MATRICES = ("w_in", "proj_a", "proj_b", "w_out", "w_gate", "w_up", "w_down")
_SMALL = (("norm1_w", (D_MODEL,)), ("sinks", (N_Q_HEADS,)), ("sgu_ln_w", (SGU_WIDTH,)), ("sgu_ln_b", (SGU_WIDTH,)),
          ("sgu_w", (SGU_GROUPS, SGU_CHUNK, SGU_CHUNK)), ("sgu_b", (SGU_GROUPS, SGU_CHUNK)), ("norm2_w", (D_MODEL,)),
          ("conv_w", (3, FFN_DIM)), ("conv_b", (FFN_DIM,)), ("final_norm_w", (D_MODEL,)))
SMALL_ROWS = 320
ADAM_TILE_ELEMS = 384 * 1024


def _reduce_cores_start(tag, partial):
    names = list(partial)
    pending, token = _swap_halves_start(tag + "_cores", [partial[k] for k in names])
    return dict(pending, tag=tag, names=names), token


def _reduce_chips_start(pending, core, after):
    tag, names = pending["tag"], pending["names"]
    mine, theirs = _exchange_wait(pending, after, both=True)
    sums = _add_halves(tag + "_cores_add", mine, theirs, core)
    scatter, token = _scatter_chips_start(tag + "_chips", sums)
    return dict(scatter, tag=tag, names=names), token


def _reduce_back_start(pending, l, core, totals, after):
    tag, names = pending["tag"], pending["names"]
    before = [totals[k] for k in names] if names[0] in totals else None
    sums = _sum_chips(tag + "_chips_add", _exchange_wait(pending, after), core, l, before)
    back, _ = _swap_back_start(tag + "_back", sums, l)
    return dict(back, names=names)


def kernel(x, c, positions, ada_w, ada_b, norm1_w, w_in, attn_sinks, sgu_ln_w, sgu_ln_b, sgu_w, sgu_b, proj_a, proj_b, w_out, norm2_w, ffn_w_gate, ffn_w_up, ffn_conv_w, ffn_conv_b, ffn_w_down, final_norm_w, loss_target, m_ada_w, m_ada_b, m_norm1_w, m_w_in, m_attn_sinks, m_sgu_ln_w, m_sgu_ln_b, m_sgu_w, m_sgu_b, m_proj_a, m_proj_b, m_w_out, m_norm2_w, m_ffn_w_gate, m_ffn_w_up, m_ffn_conv_w, m_ffn_conv_b, m_ffn_w_down, m_final_norm_w, v_ada_w, v_ada_b, v_norm1_w, v_w_in, v_attn_sinks, v_sgu_ln_w, v_sgu_ln_b, v_sgu_w, v_sgu_b, v_proj_a, v_proj_b, v_w_out, v_norm2_w, v_ffn_w_gate, v_ffn_w_up, v_ffn_conv_w, v_ffn_conv_b, v_ffn_w_down, v_final_norm_w):
    d = D_MODEL
    ax, ay, ac = lax.axis_index("x"), lax.axis_index("y"), lax.axis_index("c")
    chip = 2 * ax + ay
    dev = 4 * ax + 2 * ay + ac
    core = ac.astype(jnp.int32).reshape(1)

    c_all = _gather_all("gather_cond", c.reshape(SUBLANES, d // SUBLANES)).reshape(2 * N_CHIPS, d)
    c_rows = _pad_to(c_all, 0, ADA_ROWS)
    ada_cols = ada_w.shape[2]
    ada_b_cols = lax.dynamic_slice_in_dim(ada_b, chip * ada_cols, ada_cols, axis=1).reshape(DEPTH, 1, ada_cols)
    mod_cols = _ada_fwd("ada_fwd", c_rows, ada_w, ada_b_cols)
    mod_all = _gather_chips("gather_mod", [mod_cols])[0]
    mod_mine = lax.dynamic_index_in_dim(mod_all, dev, axis=2, keepdims=False)
    mod_mine = mod_mine.transpose(1, 0, 2).reshape(DEPTH, 1, 6 * d)
    mods = [tuple(jnp.split(mod_mine[l], 6, axis=-1)) for l in range(DEPTH)]

    tr = lambda a: jnp.swapaxes(a, 1, 2)
    shards = [tr(w_in).astype(BF), proj_a.astype(BF), proj_b.astype(BF), w_out.astype(BF),
              tr(ffn_w_gate).astype(BF), tr(ffn_w_up).astype(BF), ffn_w_down.astype(BF), ffn_conv_w]
    stages, forwards = [], []
    for l in range(DEPTH):
        for tag, members in (("in", shards[:1]), ("mix", shards[1:4]), ("ffn", shards[4:])):
            group = [a[l] for a in members]
            if not stages:
                group[0] = group[0] + (mod_all[0, 0, 0, 0] * 0.0).astype(BF)
            dsts, fetch, local, forward = _gather_weights_plan(group)
            stages.append((f"l{l}_gather_{tag}", group, dsts, fetch, local, False))
            forwards.append(forward)
    first, token = _exchange_start_many("gather_weights_first", stages[:1])
    w_in_first = _gather_weights_finish(dict(first[0], forward=forwards[0]), token)
    name, group, *rest = stages[1]
    stages[1] = (name, [group[0] + (w_in_first[0][0, 0] * 0.0).astype(BF)] + group[1:], *rest)
    later, token = _exchange_start_many("gather_weights", stages[1:])
    pendings = [None] + [dict(p, forward=f) for p, f in zip(later, forwards[1:])]
    fetches = [pendings[3 * l:3 * l + 3] for l in range(DEPTH)]

    small_in = dict(norm1_w=norm1_w, sinks=attn_sinks, sgu_ln_w=sgu_ln_w, sgu_ln_b=sgu_ln_b, sgu_w=sgu_w, sgu_b=sgu_b,
                    norm2_w=norm2_w, conv_b=ffn_conv_b)
    cosf, sinf = _rope_tables(positions[0])
    small_of = lambda l: {k: v[l] for k, v in small_in.items()}

    h = x[0]
    saved, params = [], []
    for l in range(DEPTH):
        first, mix, ffn = fetches[l]
        w_in_l = w_in_first if l == 0 else _gather_weights_finish(first, h)
        late = lambda y, l=l, mix=mix: _mix_params(*_gather_weights_finish(mix, y), small_of(l))
        later = lambda y, l=l, ffn=ffn: _ffn_params(*_gather_weights_finish(ffn, y), small_of(l))
        h, sv, p = _layer_fwd(l, h, mods[l], _early_params(w_in_l[0], small_of(l)), cosf, sinf, after=token if l == 0 else None,
                              late=late, later=later)
        saved.append(sv)
        params.append(p)
    dx, loss_part, d_final = _loss_head("loss_head", h, final_norm_w.reshape(1, d), loss_target[0])
    loss = lax.psum(loss_part[0, 0], ("x", "y", "c"))

    def small_pack(l, grads):
        cw, cb = _conv_grads_natural(grads)
        nat = dict(grads, conv_w=cw, conv_b=cb, final_norm_w=d_final if l == DEPTH - 1 else jnp.zeros((d,), F32))
        return _pack([nat[k] for k, _ in _SMALL], N_CHIPS * SMALL_ROWS).reshape(N_CHIPS, SMALL_ROWS, LANES)

    waiting, inflight = [], []

    def send(tag, partial):
        pending, token = _reduce_cores_start(tag, partial)
        waiting.append(pending)
        return token

    def tick(y):
        token = None
        while waiting:
            pending, token = _reduce_chips_start(waiting.pop(0), core, y)
            inflight.append(pending)
        return token

    dmods = [None] * DEPTH
    dx, dmods[1], grads = _layer_bwd(1, dx, mods[1], params[1], saved[1], cosf, sinf)
    token = send("l1_reduce", dict({k: grads[k] for k in _MATRICES}, small=small_pack(1, grads)))
    dx, dmods[0], grads = _layer_bwd(0, dx, mods[0], params[0], saved[0], cosf, sinf, after=token,
                                     emit=lambda part: send("l0_reduce_" + "_".join(part), part), tick=tick)
    dmod_mine = jnp.concatenate([jnp.concatenate(dmods[l], axis=1) for l in range(DEPTH)], axis=1)
    dmod_all = _gather_all("gather_dmod", dmod_mine.reshape(SUBLANES, -1)).reshape(2 * N_CHIPS, DEPTH * 6 * d)
    send("l0_reduce_in", dict(w_in=grads["w_in"], small=small_pack(0, grads) + dmod_all[0, 0] * 0.0))
    tick(dmod_all)

    totals, flying = {}, None

    def land(after):
        if flying is not None:
            totals.update(zip(flying["names"], _exchange_wait(flying, after)))

    for pending in inflight[:-1]:
        land(dx)
        flying = _reduce_back_start(pending, 1 if pending["tag"].startswith("l1") else 0, core, totals, dx)
    land(dx)
    flying = None

    g_ada_b = _colsum("ada_b_grad", dmod_all).reshape(DEPTH, 6 * d)
    dmod_cols = jnp.stack([lax.dynamic_slice_in_dim(dmod_all, l * 6 * d + chip * ada_cols, ada_cols, axis=1)
                           for l in range(DEPTH)])
    g_ada_w = _ada_bwd("ada_w_grad", c_rows, _pad_to(dmod_cols, 1, ADA_ROWS))
    big = dict(w_in=(tr(w_in), tr(m_w_in), tr(v_w_in)), proj_a=(proj_a, m_proj_a, v_proj_a), proj_b=(proj_b, m_proj_b, v_proj_b),
               w_out=(w_out, m_w_out, v_w_out), w_gate=(tr(ffn_w_gate), tr(m_ffn_w_gate), tr(v_ffn_w_gate)),
               w_up=(tr(ffn_w_up), tr(m_ffn_w_up), tr(v_ffn_w_up)), w_down=(ffn_w_down, m_ffn_w_down, v_ffn_w_down))
    upd, g_big, done = {}, {}, {}

    def update(k, after=None):
        res = _adamw("adamw_" + k, big[k][0], totals[k], *big[k][1:], after=after)
        done[k] = res[1]
        res = [tr(a) for a in res] if k in ("w_in", "w_gate", "w_up") else res
        g_big[k], upd[k] = res[0], res[1:]

    for k in ("w_down", "w_gate", "w_up", "w_out", "proj_a", "proj_b"):
        update(k)
    g_big["ada_w"], *upd["ada_w"] = _adamw("adamw_ada_w", ada_w, g_ada_w, m_ada_w, v_ada_w)
    flying = _reduce_back_start(inflight[-1], 0, core, totals, upd["ada_w"][0])
    land(upd["ada_w"][1])
    gathering, token = _gather_chips_start("gather_small", [totals["small"]])
    update("w_in", after=token)
    small_all = _exchange_wait(gathering, done["w_in"])[0]
    small_g = small_all.transpose(1, 0, 2, 3).reshape(DEPTH, -1)
    per_layer = [_unpack(small_g[l], [shp for _, shp in _SMALL]) for l in range(DEPTH)]
    sg = {k: jnp.stack([per_layer[l][i] for l in range(DEPTH)]) for i, (k, _) in enumerate(_SMALL)}
    g_final = sg["final_norm_w"][DEPTH - 1]
    g_conv_w = lax.dynamic_slice_in_dim(sg["conv_w"], chip * FFN_SHARD, FFN_SHARD, axis=2)

    rest = [("ada_b", ada_b, g_ada_b, m_ada_b, v_ada_b), ("norm1_w", norm1_w, sg["norm1_w"], m_norm1_w, v_norm1_w),
            ("attn_sinks", attn_sinks, sg["sinks"], m_attn_sinks, v_attn_sinks),
            ("sgu_ln_w", sgu_ln_w, sg["sgu_ln_w"], m_sgu_ln_w, v_sgu_ln_w),
            ("sgu_ln_b", sgu_ln_b, sg["sgu_ln_b"], m_sgu_ln_b, v_sgu_ln_b), ("sgu_w", sgu_w, sg["sgu_w"], m_sgu_w, v_sgu_w),
            ("sgu_b", sgu_b, sg["sgu_b"], m_sgu_b, v_sgu_b), ("norm2_w", norm2_w, sg["norm2_w"], m_norm2_w, v_norm2_w),
            ("ffn_conv_w", ffn_conv_w, g_conv_w, m_ffn_conv_w, v_ffn_conv_w),
            ("ffn_conv_b", ffn_conv_b, sg["conv_b"], m_ffn_conv_b, v_ffn_conv_b),
            ("final_norm_w", final_norm_w.reshape(1, d), g_final.reshape(1, d), m_final_norm_w.reshape(1, d),
             v_final_norm_w.reshape(1, d))]
    rest_out = _adamw_small("adamw_rest", *[[r[i] for r in rest] for i in (1, 2, 3, 4)])
    g_rest = {r[0]: r[2] for r in rest}
    u_rest = {r[0]: tuple(o[i] for o in rest_out) for i, r in enumerate(rest)}
    g_rest["final_norm_w"] = g_final
    u_rest["final_norm_w"] = tuple(a.reshape(d) for a in u_rest["final_norm_w"])

    names = ("ada_w", "ada_b", "norm1_w", "w_in", "attn_sinks", "sgu_ln_w", "sgu_ln_b", "sgu_w", "sgu_b", "proj_a", "proj_b",
             "w_out", "norm2_w", "ffn_w_gate", "ffn_w_up", "ffn_conv_w", "ffn_conv_b", "ffn_w_down", "final_norm_w")
    alias = {"ffn_w_gate": "w_gate", "ffn_w_up": "w_up", "ffn_w_down": "w_down"}
    grad_of = lambda n: g_rest[n] if n in g_rest else g_big[alias.get(n, n)]
    upd_of = lambda n: u_rest[n] if n in u_rest else upd[alias.get(n, n)]
    return (loss, dx[None], *[grad_of(n) for n in names], *[upd_of(n)[0] for n in names],
            *[upd_of(n)[1] for n in names], *[upd_of(n)[2] for n in names])
```
